```python
import math
import jax
import jax.numpy as jnp
from jax import lax
import numpy as np

D_MODEL = 1024
BATCH = 8
SEQ = 4096
DEPTH = 2

ATTN_HEAD_DIM = 64
DILATION_PATTERNS = ((128, 1), (512, 4), (2048, 16))
HEADS_PER_PATTERN = 8
N_ATTN_HEADS = HEADS_PER_PATTERN * len(DILATION_PATTERNS)
ATTN_WIDTH = N_ATTN_HEADS * ATTN_HEAD_DIM
ROT_DIM = ATTN_HEAD_DIM // 4
ROPE_THETA = 500000.0
NEG_BIG = -1e30

SSD_INNER = 2 * D_MODEL
SSD_HEAD_DIM = 64
SSD_HEADS = SSD_INNER // SSD_HEAD_DIM
SSD_GROUPS = 8
SSD_STATE = 128
CONV_WIDTH = 5
CHUNK = 128
SSD_XBC = SSD_INNER + 2 * SSD_GROUPS * SSD_STATE
SSD_IN = SSD_INNER + SSD_XBC + 2 * SSD_HEADS

NORM_EPS = 1e-6

kernel_name = "hybrid_dilated_attn_bissd_encoder"


def rms_norm(x, w):
    xf = x.astype(jnp.float32)
    y = xf * lax.rsqrt(jnp.mean(xf * xf, axis=-1, keepdims=True) + NORM_EPS)
    return (y * w.astype(jnp.float32)).astype(x.dtype)


def partial_rotary(t, cos, sin):
    half = ROT_DIM // 2
    t1, t2, rest = t[..., :half], t[..., half:ROT_DIM], t[..., ROT_DIM:]
    return jnp.concatenate([t1 * cos - t2 * sin, t2 * cos + t1 * sin, rest], axis=-1)


def banded_attention(q, k, v, half):
    n, l, h, dh = q.shape
    blk = half
    nb = -(-l // blk)
    pad = nb * blk - l
    qb = jnp.pad(q, ((0, 0), (0, pad), (0, 0), (0, 0))).reshape(n, nb, blk, h, dh)

    def windows(t):
        tp = jnp.pad(t, ((0, 0), (blk, pad + blk), (0, 0), (0, 0))).reshape(n, nb + 2, blk, h, dh)
        return jnp.concatenate([tp[:, :-2], tp[:, 1:-1], tp[:, 2:]], axis=2)

    kw, vw = windows(k), windows(v)
    s = jnp.einsum('njqhd,njkhd->njhqk', qb, kw).astype(jnp.float32) / math.sqrt(dh)
    qpos = jnp.arange(nb)[:, None] * blk + jnp.arange(blk)[None, :]
    kpos = jnp.arange(nb)[:, None] * blk - blk + jnp.arange(3 * blk)[None, :]
    rel = kpos[:, None, :] - qpos[:, :, None]
    valid = (jnp.abs(rel) <= half) & (kpos[:, None, :] >= 0) & (kpos[:, None, :] < l)
    s = jnp.where(valid[None, :, None], s, NEG_BIG)
    m = jnp.max(s, axis=-1, keepdims=True)
    p = jnp.exp(s - m)
    den = jnp.sum(p, axis=-1)
    o = jnp.einsum('njhqk,njkhd->njqhd', p, vw.astype(jnp.float32))
    o = o / den.transpose(0, 1, 3, 2)[..., None]
    lse = (m[..., 0] + jnp.log(den)).transpose(0, 1, 3, 2)
    o = o.reshape(n, nb * blk, h, dh)[:, :l]
    lse = lse.reshape(n, nb * blk, h)[:, :l]
    return o, lse


def dilated_group(q, k, v, window, dilation):
    b, s, h, dh = q.shape
    l = s // dilation
    half = (window // 2) // dilation

    def to_sub(t):
        return t.reshape(b, l, dilation, h, dh).transpose(0, 2, 1, 3, 4).reshape(b * dilation, l, h, dh)

    o, lse = banded_attention(to_sub(q), to_sub(k), to_sub(v), half)
    o = o.reshape(b, dilation, l, h, dh).transpose(0, 2, 1, 3, 4).reshape(b, s, h, dh)
    lse = lse.reshape(b, dilation, l, h).transpose(0, 2, 1, 3).reshape(b, s, h)
    return o, lse


def dilated_attention_mixer(h, positions, w_in, w_out):
    b, s, _ = h.shape
    proj = h @ w_in
    q, k, v, z = jnp.split(proj, 4, axis=-1)
    q = q.reshape(b, s, N_ATTN_HEADS, ATTN_HEAD_DIM)
    k = k.reshape(b, s, N_ATTN_HEADS, ATTN_HEAD_DIM)
    v = v.reshape(b, s, N_ATTN_HEADS, ATTN_HEAD_DIM)
    inv_freq = ROPE_THETA ** (-jnp.arange(0, ROT_DIM, 2, dtype=jnp.float32) / ROT_DIM)
    ang = positions.astype(jnp.float32)[..., None] * inv_freq
    cos = jnp.cos(ang)[:, :, None, :].astype(q.dtype)
    sin = jnp.sin(ang)[:, :, None, :].astype(q.dtype)
    q = partial_rotary(q, cos, sin)
    k = partial_rotary(k, cos, sin)
    outs, lses = [], []
    for g, (window, dilation) in enumerate(DILATION_PATTERNS):
        sl = slice(g * HEADS_PER_PATTERN, (g + 1) * HEADS_PER_PATTERN)
        o, l = dilated_group(q[:, :, sl], k[:, :, sl], v[:, :, sl], window, dilation)
        outs.append(o)
        lses.append(l)
    alpha = jax.nn.softmax(jnp.stack(lses, axis=0), axis=0)
    o = jnp.concatenate([outs[g] * alpha[g][..., None] for g in range(len(DILATION_PATTERNS))], axis=2)
    o = o.reshape(b, s, ATTN_WIDTH).astype(h.dtype)
    return (o * jax.nn.silu(z)) @ w_out


def ssd_scan(x, dt, a, b_mat, c_mat):
    bsz, s, h, p = x.shape
    g, n = b_mat.shape[2], b_mat.shape[3]
    r = h // g
    c = s // CHUNK
    l = CHUNK
    xdt = (x.astype(jnp.float32) * dt[..., None]).reshape(bsz, c, l, g, r, p)
    da = (dt * a.astype(jnp.float32)).reshape(bsz, c, l, g, r)
    acs = jnp.cumsum(da, axis=2)
    bc = b_mat.astype(jnp.float32).reshape(bsz, c, l, g, n)
    cc = c_mat.astype(jnp.float32).reshape(bsz, c, l, g, n)
    acs_t = acs.transpose(0, 1, 3, 4, 2)
    seg = acs_t[..., :, None] - acs_t[..., None, :]
    lower = jnp.tril(jnp.ones((l, l), dtype=bool))
    decay = jnp.exp(jnp.where(lower, seg, -jnp.inf))
    cb = jnp.einsum('bclgn,bcsgn->bcgls', cc, bc)
    y_diag = jnp.einsum('bcgrls,bcsgrp->bclgrp', cb[:, :, :, None] * decay, xdt)
    decay_states = jnp.exp(acs[:, :, -1:] - acs)
    states = jnp.einsum('bclgn,bclgrp->bcgrpn', bc, xdt * decay_states[..., None])
    chunk_decay = jnp.exp(acs[:, :, -1])

    def step(carry, inp):
        st, dec = inp
        return carry * dec[..., None, None] + st, carry

    init = jnp.zeros((bsz, g, r, p, n), jnp.float32)
    _, prev = lax.scan(step, init, (states.transpose(1, 0, 2, 3, 4, 5), chunk_decay.transpose(1, 0, 2, 3)))
    y_off = jnp.einsum('bclgn,cbgrpn->bclgrp', cc, prev) * jnp.exp(acs)[..., None]
    return (y_diag + y_off).reshape(bsz, s, h, p)


def bi_ssd_mixer(h, w_in, conv_w, conv_b, dt_bias, a_log, d_skip, norm_w, w_out):
    b, s, _ = h.shape
    proj = h @ w_in
    z = proj[..., :SSD_INNER]
    xbc = proj[..., SSD_INNER:SSD_INNER + SSD_XBC]
    dt_raw = proj[..., SSD_INNER + SSD_XBC:].reshape(b, s, 2, SSD_HEADS)
    pad = CONV_WIDTH // 2
    xbc = lax.conv_general_dilated(
        xbc, conv_w.reshape(CONV_WIDTH, 1, SSD_XBC), window_strides=(1,),
        padding=[(pad, pad)], dimension_numbers=('NWC', 'WIO', 'NWC'),
        feature_group_count=SSD_XBC)
    xbc = jax.nn.silu(xbc + conv_b)
    xs = xbc[..., :SSD_INNER].reshape(b, s, SSD_HEADS, SSD_HEAD_DIM)
    bm = xbc[..., SSD_INNER:SSD_INNER + SSD_GROUPS * SSD_STATE].reshape(b, s, SSD_GROUPS, SSD_STATE)
    cm = xbc[..., SSD_INNER + SSD_GROUPS * SSD_STATE:].reshape(b, s, SSD_GROUPS, SSD_STATE)
    dt = jax.nn.softplus(dt_raw.astype(jnp.float32) + dt_bias.astype(jnp.float32))
    a = -jnp.exp(a_log.astype(jnp.float32))
    y_f = ssd_scan(xs, dt[:, :, 0], a[0], bm, cm)
    y_b = jnp.flip(ssd_scan(jnp.flip(xs, 1), jnp.flip(dt[:, :, 1], 1), a[1],
                            jnp.flip(bm, 1), jnp.flip(cm, 1)), 1)
    y = y_f + y_b + xs.astype(jnp.float32) * d_skip.astype(jnp.float32)[:, None]
    y = y.reshape(b, s, SSD_INNER) * jax.nn.silu(z.astype(jnp.float32))
    y = rms_norm(y, norm_w).astype(h.dtype)
    return y @ w_out


def _fwd_setup_inputs(seed: int = 0) -> dict:
    key = jax.random.key(seed)
    ks = jax.random.split(key, 20)
    n_attn = (DEPTH + 1) // 2
    n_ssd = DEPTH // 2
    f32 = jnp.float32
    x = jax.random.normal(ks[0], (BATCH, SEQ, D_MODEL), f32)
    c = jax.random.normal(ks[1], (BATCH, D_MODEL), f32)
    positions = (jnp.arange(SEQ, dtype=jnp.int32)[None, :]
                 + jax.random.randint(ks[2], (BATCH, 1), 0, 1024, dtype=jnp.int32))
    norm_w = 1.0 + 0.02 * jax.random.normal(ks[3], (DEPTH, D_MODEL), f32)
    mod_w = jax.random.normal(ks[4], (DEPTH, D_MODEL, 3 * D_MODEL), f32) * D_MODEL ** -0.5
    mod_b = 0.02 * jax.random.normal(ks[5], (DEPTH, 3 * D_MODEL), f32)
    attn_w_in = jax.random.normal(ks[6], (n_attn, D_MODEL, 4 * ATTN_WIDTH), f32) * D_MODEL ** -0.5
    attn_w_out = jax.random.normal(ks[7], (n_attn, ATTN_WIDTH, D_MODEL), f32) * ATTN_WIDTH ** -0.5
    ssd_w_in = jax.random.normal(ks[8], (n_ssd, D_MODEL, SSD_IN), f32) * D_MODEL ** -0.5
    ssd_conv_w = jax.random.normal(ks[9], (n_ssd, CONV_WIDTH, SSD_XBC), f32) * CONV_WIDTH ** -0.5
    ssd_conv_b = 0.02 * jax.random.normal(ks[10], (n_ssd, SSD_XBC), f32)
    dt0 = jnp.exp(jax.random.uniform(ks[11], (n_ssd, 2, SSD_HEADS), f32)
                  * (math.log(0.1) - math.log(0.001)) + math.log(0.001))
    ssd_dt_bias = dt0 + jnp.log(-jnp.expm1(-dt0))
    ssd_a_log = jnp.log(jax.random.uniform(ks[12], (n_ssd, 2, SSD_HEADS), f32, 1.0, 16.0))
    ssd_d = 1.0 + 0.1 * jax.random.normal(ks[13], (n_ssd, SSD_HEADS), f32)
    ssd_norm_w = 1.0 + 0.02 * jax.random.normal(ks[14], (n_ssd, SSD_INNER), f32)
    ssd_w_out = jax.random.normal(ks[15], (n_ssd, SSD_INNER, D_MODEL), f32) * SSD_INNER ** -0.5
    final_norm_w = 1.0 + 0.02 * jax.random.normal(ks[16], (D_MODEL,), f32)
    return {"x": x, "c": c, "positions": positions, "norm_w": norm_w, "mod_w": mod_w,
            "mod_b": mod_b, "attn_w_in": attn_w_in, "attn_w_out": attn_w_out,
            "ssd_w_in": ssd_w_in, "ssd_conv_w": ssd_conv_w, "ssd_conv_b": ssd_conv_b,
            "ssd_dt_bias": ssd_dt_bias, "ssd_a_log": ssd_a_log, "ssd_d": ssd_d,
            "ssd_norm_w": ssd_norm_w, "ssd_w_out": ssd_w_out, "final_norm_w": final_norm_w}


def _fwd_reference(x, c, positions, norm_w, mod_w, mod_b, attn_w_in, attn_w_out, ssd_w_in,
              ssd_conv_w, ssd_conv_b, ssd_dt_bias, ssd_a_log, ssd_d, ssd_norm_w, ssd_w_out,
              final_norm_w):
    cond = jax.nn.silu(c)
    for i in range(DEPTH):
        mod = cond @ mod_w[i] + mod_b[i]
        shift, scale, gate = jnp.split(mod, 3, axis=-1)
        hn = rms_norm(x, norm_w[i]) * (1.0 + scale[:, None, :]) + shift[:, None, :]
        j = i // 2
        if i % 2 == 0:
            y = dilated_attention_mixer(hn, positions, attn_w_in[j], attn_w_out[j])
        else:
            y = bi_ssd_mixer(hn, ssd_w_in[j], ssd_conv_w[j], ssd_conv_b[j], ssd_dt_bias[j],
                             ssd_a_log[j], ssd_d[j], ssd_norm_w[j], ssd_w_out[j])
        x = x + gate[:, None, :] * y.astype(x.dtype)
    return rms_norm(x, final_norm_w)


import jax as _jax
import jax.numpy as _jnp

TWIN_FORMAT = 'train_step'
FWD_PARAMS = ['x', 'c', 'positions', 'norm_w', 'mod_w', 'mod_b', 'attn_w_in', 'attn_w_out', 'ssd_w_in', 'ssd_conv_w', 'ssd_conv_b', 'ssd_dt_bias', 'ssd_a_log', 'ssd_d', 'ssd_norm_w', 'ssd_w_out', 'final_norm_w']
TWIN_WEIGHTS = ['norm_w', 'mod_w', 'mod_b', 'attn_w_in', 'attn_w_out', 'ssd_w_in', 'ssd_conv_w', 'ssd_conv_b', 'ssd_dt_bias', 'ssd_a_log', 'ssd_d', 'ssd_norm_w', 'ssd_w_out', 'final_norm_w']
TWIN_DIFF_INPUT = 'x'
TWIN_INPUTS = ['x', 'c', 'positions', 'norm_w', 'mod_w', 'mod_b', 'attn_w_in', 'attn_w_out', 'ssd_w_in', 'ssd_conv_w', 'ssd_conv_b', 'ssd_dt_bias', 'ssd_a_log', 'ssd_d', 'ssd_norm_w', 'ssd_w_out', 'final_norm_w', 'loss_target', 'm_norm_w', 'm_mod_w', 'm_mod_b', 'm_attn_w_in', 'm_attn_w_out', 'm_ssd_w_in', 'm_ssd_conv_w', 'm_ssd_conv_b', 'm_ssd_dt_bias', 'm_ssd_a_log', 'm_ssd_d', 'm_ssd_norm_w', 'm_ssd_w_out', 'm_final_norm_w', 'v_norm_w', 'v_mod_w', 'v_mod_b', 'v_attn_w_in', 'v_attn_w_out', 'v_ssd_w_in', 'v_ssd_conv_w', 'v_ssd_conv_b', 'v_ssd_dt_bias', 'v_ssd_a_log', 'v_ssd_d', 'v_ssd_norm_w', 'v_ssd_w_out', 'v_final_norm_w']
TWIN_OUTPUTS = ['loss', 'grad_x', 'grad_norm_w', 'grad_mod_w', 'grad_mod_b', 'grad_attn_w_in', 'grad_attn_w_out', 'grad_ssd_w_in', 'grad_ssd_conv_w', 'grad_ssd_conv_b', 'grad_ssd_dt_bias', 'grad_ssd_a_log', 'grad_ssd_d', 'grad_ssd_norm_w', 'grad_ssd_w_out', 'grad_final_norm_w', 'delta_norm_w', 'delta_mod_w', 'delta_mod_b', 'delta_attn_w_in', 'delta_attn_w_out', 'delta_ssd_w_in', 'delta_ssd_conv_w', 'delta_ssd_conv_b', 'delta_ssd_dt_bias', 'delta_ssd_a_log', 'delta_ssd_d', 'delta_ssd_norm_w', 'delta_ssd_w_out', 'delta_final_norm_w', 'new_m_norm_w', 'new_m_mod_w', 'new_m_mod_b', 'new_m_attn_w_in', 'new_m_attn_w_out', 'new_m_ssd_w_in', 'new_m_ssd_conv_w', 'new_m_ssd_conv_b', 'new_m_ssd_dt_bias', 'new_m_ssd_a_log', 'new_m_ssd_d', 'new_m_ssd_norm_w', 'new_m_ssd_w_out', 'new_m_final_norm_w', 'new_v_norm_w', 'new_v_mod_w', 'new_v_mod_b', 'new_v_attn_w_in', 'new_v_attn_w_out', 'new_v_ssd_w_in', 'new_v_ssd_conv_w', 'new_v_ssd_conv_b', 'new_v_ssd_dt_bias', 'new_v_ssd_a_log', 'new_v_ssd_d', 'new_v_ssd_norm_w', 'new_v_ssd_w_out', 'new_v_final_norm_w']
TWIN_LEAF_KINDS = {'loss': 'loss', 'grad_x': 'grad_x', 'grad_norm_w': 'grad_w', 'grad_mod_w': 'grad_w', 'grad_mod_b': 'grad_w', 'grad_attn_w_in': 'grad_w', 'grad_attn_w_out': 'grad_w', 'grad_ssd_w_in': 'grad_w', 'grad_ssd_conv_w': 'grad_w', 'grad_ssd_conv_b': 'grad_w', 'grad_ssd_dt_bias': 'grad_w', 'grad_ssd_a_log': 'grad_w', 'grad_ssd_d': 'grad_w', 'grad_ssd_norm_w': 'grad_w', 'grad_ssd_w_out': 'grad_w', 'grad_final_norm_w': 'grad_w', 'delta_norm_w': 'delta_w', 'delta_mod_w': 'delta_w', 'delta_mod_b': 'delta_w', 'delta_attn_w_in': 'delta_w', 'delta_attn_w_out': 'delta_w', 'delta_ssd_w_in': 'delta_w', 'delta_ssd_conv_w': 'delta_w', 'delta_ssd_conv_b': 'delta_w', 'delta_ssd_dt_bias': 'delta_w', 'delta_ssd_a_log': 'delta_w', 'delta_ssd_d': 'delta_w', 'delta_ssd_norm_w': 'delta_w', 'delta_ssd_w_out': 'delta_w', 'delta_final_norm_w': 'delta_w', 'new_m_norm_w': 'new_m', 'new_m_mod_w': 'new_m', 'new_m_mod_b': 'new_m', 'new_m_attn_w_in': 'new_m', 'new_m_attn_w_out': 'new_m', 'new_m_ssd_w_in': 'new_m', 'new_m_ssd_conv_w': 'new_m', 'new_m_ssd_conv_b': 'new_m', 'new_m_ssd_dt_bias': 'new_m', 'new_m_ssd_a_log': 'new_m', 'new_m_ssd_d': 'new_m', 'new_m_ssd_norm_w': 'new_m', 'new_m_ssd_w_out': 'new_m', 'new_m_final_norm_w': 'new_m', 'new_v_norm_w': 'new_v', 'new_v_mod_w': 'new_v', 'new_v_mod_b': 'new_v', 'new_v_attn_w_in': 'new_v', 'new_v_attn_w_out': 'new_v', 'new_v_ssd_w_in': 'new_v', 'new_v_ssd_conv_w': 'new_v', 'new_v_ssd_conv_b': 'new_v', 'new_v_ssd_dt_bias': 'new_v', 'new_v_ssd_a_log': 'new_v', 'new_v_ssd_d': 'new_v', 'new_v_ssd_norm_w': 'new_v', 'new_v_ssd_w_out': 'new_v', 'new_v_final_norm_w': 'new_v'}


def _forward(args):
    return _fwd_reference(*[args[k] for k in FWD_PARAMS])


def _output_shape():
    out = _jax.eval_shape(lambda: _forward(_fwd_setup_inputs(0)))
    return out.shape, out.dtype

N_MICROBATCH = 1
ADAM_LR = 0.001
ADAM_B1 = 0.9
ADAM_B2 = 0.999
ADAM_EPS = 1e-08
ADAM_WD = 0.01
ADAM_STEP = 10
PER_EXAMPLE_BATCH_AXIS = {'x': 0, 'c': 0, 'positions': 0, 'loss_target': 0}
SHARED_INPUTS = []
_WEIGHT_DTYPES = {'norm_w': _jnp.float32, 'mod_w': _jnp.float32, 'mod_b': _jnp.float32, 'attn_w_in': _jnp.float32, 'attn_w_out': _jnp.float32, 'ssd_w_in': _jnp.float32, 'ssd_conv_w': _jnp.float32, 'ssd_conv_b': _jnp.float32, 'ssd_dt_bias': _jnp.float32, 'ssd_a_log': _jnp.float32, 'ssd_d': _jnp.float32, 'ssd_norm_w': _jnp.float32, 'ssd_w_out': _jnp.float32, 'final_norm_w': _jnp.float32}
MOMENT_SCALE = {'norm_w': 9.333303e-02, 'mod_w': 7.390095e-02, 'mod_b': 1.223225e-01, 'attn_w_in': 2.390209e-02, 'attn_w_out': 3.097063e-02, 'ssd_w_in': 6.003475e-02, 'ssd_conv_w': 5.275444e-02, 'ssd_conv_b': 5.945716e-02, 'ssd_dt_bias': 1.311259e-01, 'ssd_a_log': 1.995976e-01, 'ssd_d': 1.356432e-01, 'ssd_norm_w': 6.858394e-02, 'ssd_w_out': 9.979371e-02, 'final_norm_w': 3.267502e+01}


def _to_microbatches(a, axis):
    t = _jnp.moveaxis(a, axis, 0)
    t = t.reshape((N_MICROBATCH, t.shape[0] // N_MICROBATCH) + t.shape[1:])
    return _jnp.moveaxis(t, 1, axis + 1)


def setup_inputs(seed: int = 0) -> dict:
    inp = _fwd_setup_inputs(seed)
    key = _jax.random.fold_in(_jax.random.key(seed), 7919)
    shape, _ = _output_shape()
    out = dict(inp)
    out["loss_target"] = _jax.random.normal(_jax.random.fold_in(key, 0), shape, _jnp.float32)
    for i, name in enumerate(TWIN_WEIGHTS):
        w = inp[name].astype(_jnp.float32)
        if MOMENT_SCALE is None:
            s = _jnp.sqrt(_jnp.mean(_jnp.square(w)) + 1e-30)
        else:
            s = MOMENT_SCALE[name]
        km, kv = _jax.random.split(_jax.random.fold_in(key, i + 1))
        out[name] = w
        out["m_" + name] = s * _jax.random.normal(km, w.shape, _jnp.float32)
        out["v_" + name] = (s * s) * _jax.random.uniform(kv, w.shape, _jnp.float32, 0.5, 1.5)
    if N_MICROBATCH > 1:
        for name, axis in PER_EXAMPLE_BATCH_AXIS.items():
            out[name] = _to_microbatches(out[name], axis)
    return {'x': out['x'], 'c': out['c'], 'positions': out['positions'], 'norm_w': out['norm_w'], 'mod_w': out['mod_w'], 'mod_b': out['mod_b'], 'attn_w_in': out['attn_w_in'], 'attn_w_out': out['attn_w_out'], 'ssd_w_in': out['ssd_w_in'], 'ssd_conv_w': out['ssd_conv_w'], 'ssd_conv_b': out['ssd_conv_b'], 'ssd_dt_bias': out['ssd_dt_bias'], 'ssd_a_log': out['ssd_a_log'], 'ssd_d': out['ssd_d'], 'ssd_norm_w': out['ssd_norm_w'], 'ssd_w_out': out['ssd_w_out'], 'final_norm_w': out['final_norm_w'], 'loss_target': out['loss_target'], 'm_norm_w': out['m_norm_w'], 'm_mod_w': out['m_mod_w'], 'm_mod_b': out['m_mod_b'], 'm_attn_w_in': out['m_attn_w_in'], 'm_attn_w_out': out['m_attn_w_out'], 'm_ssd_w_in': out['m_ssd_w_in'], 'm_ssd_conv_w': out['m_ssd_conv_w'], 'm_ssd_conv_b': out['m_ssd_conv_b'], 'm_ssd_dt_bias': out['m_ssd_dt_bias'], 'm_ssd_a_log': out['m_ssd_a_log'], 'm_ssd_d': out['m_ssd_d'], 'm_ssd_norm_w': out['m_ssd_norm_w'], 'm_ssd_w_out': out['m_ssd_w_out'], 'm_final_norm_w': out['m_final_norm_w'], 'v_norm_w': out['v_norm_w'], 'v_mod_w': out['v_mod_w'], 'v_mod_b': out['v_mod_b'], 'v_attn_w_in': out['v_attn_w_in'], 'v_attn_w_out': out['v_attn_w_out'], 'v_ssd_w_in': out['v_ssd_w_in'], 'v_ssd_conv_w': out['v_ssd_conv_w'], 'v_ssd_conv_b': out['v_ssd_conv_b'], 'v_ssd_dt_bias': out['v_ssd_dt_bias'], 'v_ssd_a_log': out['v_ssd_a_log'], 'v_ssd_d': out['v_ssd_d'], 'v_ssd_norm_w': out['v_ssd_norm_w'], 'v_ssd_w_out': out['v_ssd_w_out'], 'v_final_norm_w': out['v_final_norm_w']}


def _loss(weights, diff, rest, loss_target):
    with _jax.named_scope("forward"):
        args = {**rest, TWIN_DIFF_INPUT: diff, **{k: w.astype(_WEIGHT_DTYPES[k]) for k, w in weights.items()}}
        y = _forward(args)
    with _jax.named_scope("loss_head"):
        err = _jnp.square(y.astype(_jnp.float32) - loss_target)
        return 0.5 * _jnp.sum(_jnp.mean(err, axis=-1)) if err.ndim else 0.5 * err


def _adamw(w, g, m, v):
    m = ADAM_B1 * m + (1.0 - ADAM_B1) * g
    v = ADAM_B2 * v + (1.0 - ADAM_B2) * _jnp.square(g)
    m_hat = m / (1.0 - ADAM_B1 ** ADAM_STEP)
    v_hat = v / (1.0 - ADAM_B2 ** ADAM_STEP)
    delta = -ADAM_LR * (m_hat / (_jnp.sqrt(v_hat) + ADAM_EPS) + ADAM_WD * w)
    return delta, m, v


def reference(x, c, positions, norm_w, mod_w, mod_b, attn_w_in, attn_w_out, ssd_w_in, ssd_conv_w, ssd_conv_b, ssd_dt_bias, ssd_a_log, ssd_d, ssd_norm_w, ssd_w_out, final_norm_w, loss_target, m_norm_w, m_mod_w, m_mod_b, m_attn_w_in, m_attn_w_out, m_ssd_w_in, m_ssd_conv_w, m_ssd_conv_b, m_ssd_dt_bias, m_ssd_a_log, m_ssd_d, m_ssd_norm_w, m_ssd_w_out, m_final_norm_w, v_norm_w, v_mod_w, v_mod_b, v_attn_w_in, v_attn_w_out, v_ssd_w_in, v_ssd_conv_w, v_ssd_conv_b, v_ssd_dt_bias, v_ssd_a_log, v_ssd_d, v_ssd_norm_w, v_ssd_w_out, v_final_norm_w):
    given = dict(x=x, c=c, positions=positions, norm_w=norm_w, mod_w=mod_w, mod_b=mod_b, attn_w_in=attn_w_in, attn_w_out=attn_w_out, ssd_w_in=ssd_w_in, ssd_conv_w=ssd_conv_w, ssd_conv_b=ssd_conv_b, ssd_dt_bias=ssd_dt_bias, ssd_a_log=ssd_a_log, ssd_d=ssd_d, ssd_norm_w=ssd_norm_w, ssd_w_out=ssd_w_out, final_norm_w=final_norm_w, loss_target=loss_target, m_norm_w=m_norm_w, m_mod_w=m_mod_w, m_mod_b=m_mod_b, m_attn_w_in=m_attn_w_in, m_attn_w_out=m_attn_w_out, m_ssd_w_in=m_ssd_w_in, m_ssd_conv_w=m_ssd_conv_w, m_ssd_conv_b=m_ssd_conv_b, m_ssd_dt_bias=m_ssd_dt_bias, m_ssd_a_log=m_ssd_a_log, m_ssd_d=m_ssd_d, m_ssd_norm_w=m_ssd_norm_w, m_ssd_w_out=m_ssd_w_out, m_final_norm_w=m_final_norm_w, v_norm_w=v_norm_w, v_mod_w=v_mod_w, v_mod_b=v_mod_b, v_attn_w_in=v_attn_w_in, v_attn_w_out=v_attn_w_out, v_ssd_w_in=v_ssd_w_in, v_ssd_conv_w=v_ssd_conv_w, v_ssd_conv_b=v_ssd_conv_b, v_ssd_dt_bias=v_ssd_dt_bias, v_ssd_a_log=v_ssd_a_log, v_ssd_d=v_ssd_d, v_ssd_norm_w=v_ssd_norm_w, v_ssd_w_out=v_ssd_w_out, v_final_norm_w=v_final_norm_w)
    weights = {n: given[n] for n in TWIN_WEIGHTS}
    shared = {n: given[n] for n in SHARED_INPUTS}
    per_example = {n: given[n] for n in ['x', 'c', 'positions']}
    grad_fn = _jax.value_and_grad(_loss, argnums=(0, 1))

    def one_microbatch(ex, loss_target):
        ex = dict(ex)
        diff = ex.pop(TWIN_DIFF_INPUT)
        return grad_fn(weights, diff, {**shared, **ex}, loss_target)

    if N_MICROBATCH == 1:
        loss, (grad_w, grad_x) = one_microbatch(per_example, given["loss_target"])
    else:
        def body(carry, xs):
            loss_sum, grad_sum = carry
            l_k, (gw_k, gx_k) = one_microbatch(xs[0], xs[1])
            with _jax.named_scope("update"):
                return (loss_sum + l_k, _jax.tree.map(_jnp.add, grad_sum, gw_k)), gx_k

        init = (_jnp.zeros((), _jnp.float32), _jax.tree.map(_jnp.zeros_like, weights))
        (loss, grad_w), grad_x = _jax.lax.scan(body, init, (per_example, given["loss_target"]))
    with _jax.named_scope("update"):
        delta_w, new_m, new_v = {}, {}, {}
        for n in TWIN_WEIGHTS:
            delta_w[n], new_m[n], new_v[n] = _adamw(weights[n], grad_w[n], given["m_" + n], given["v_" + n])
    return (loss, grad_x, *[grad_w[n] for n in TWIN_WEIGHTS], *[delta_w[n] for n in TWIN_WEIGHTS],
            *[new_m[n] for n in TWIN_WEIGHTS], *[new_v[n] for n in TWIN_WEIGHTS])
```

```python
import functools
import math

import jax
import jax.numpy as jnp
from jax import lax
from jax.experimental import pallas as pl
from jax.experimental.pallas import tpu as pltpu

F32 = jnp.float32
BF16 = jnp.bfloat16
HI = lax.Precision.HIGHEST
MESH = pl.DeviceIdType.MESH
NDEV = 8

NORM_EPS = 1e-6
ROPE_THETA = 500000.0
ROT_DIM = 16
HEAD_DIM = 64
DILATIONS = (1, 4, 16)
BAND = 64
NEG_BIG = -1e30
CHUNK = 128
SSD_HEADS = 32
SSD_GROUPS = 8
CONV_WIDTH = 5

ADAM_LR = 0.001
ADAM_B1 = 0.9
ADAM_B2 = 0.999
ADAM_EPS = 1e-08
ADAM_WD = 0.01
ADAM_STEP = 10

VMEM_BIG = 56 * 1024 * 1024


def _params(sem=None, vmem=None):
    kw = {}
    if sem is not None:
        kw["dimension_semantics"] = sem
    if vmem is not None:
        kw["vmem_limit_bytes"] = vmem
    return pltpu.CompilerParams(**kw)


def _dg(a, b, ca, cb, prec=None):
    return lax.dot_general(a, b, (((ca,), (cb,)), ((), ())), preferred_element_type=F32, precision=prec)


def _nn(a, b):
    return _dg(a.astype(BF16), b.astype(BF16), 1, 0)


def _nt(a, b):
    return _dg(a.astype(BF16), b.astype(BF16), 1, 1)


def _tn(a, b):
    return _dg(a.astype(BF16), b.astype(BF16), 0, 0)


def _hnn(a, b):
    return _dg(a, b, 1, 0, HI)


@jax.custom_vjp
def _bnn(a, b):
    return _nn(a, b)


_bnn.defvjp(lambda a, b: (_nn(a, b), (a, b)), lambda r, g: (_nt(g, r[1]), _tn(r[0], g)))


@jax.custom_vjp
def _bnt(a, b):
    return _nt(a, b)


_bnt.defvjp(lambda a, b: (_nt(a, b), (a, b)), lambda r, g: (_nn(g, r[1]), _tn(g, r[0])))


@jax.custom_vjp
def _btn(a, b):
    return _tn(a, b)


_btn.defvjp(lambda a, b: (_tn(a, b), (a, b)), lambda r, g: (_nt(r[1], g), _nn(r[0], g)))


def _silu(x):
    return x * jax.nn.sigmoid(x)


def _matmul(name, a, b, mode, out_dtype, tm, tn, tk, *, epilogue=None, tiled=(), mrows=(), ncols=(),
            b_noff=0, b_koff=0, n_out=None, out_blocks=None):
    if mode == "tn":
        K, M = a.shape
    else:
        M, K = a.shape
    N = n_out if n_out is not None else (b.shape[0] if mode == "nt" else b.shape[1])
    tm, tn, tk = min(tm, M), min(tn, N), min(tk, K)
    assert M % tm == 0 and N % tn == 0 and K % tk == 0, (name, M, N, K, tm, tn, tk)
    assert b_noff % tn == 0 and b_koff % tk == 0
    no, ko = b_noff // tn, b_koff // tk
    nk = K // tk
    if mode == "tn":
        a_spec = pl.BlockSpec((tk, tm), lambda i, j, k: (k, i))
    else:
        a_spec = pl.BlockSpec((tm, tk), lambda i, j, k: (i, k))
    if mode == "nt":
        b_spec = pl.BlockSpec((tn, tk), lambda i, j, k: (j + no, k + ko))
    else:
        b_spec = pl.BlockSpec((tk, tn), lambda i, j, k: (k + ko, j + no))
    specs = [a_spec, b_spec]
    specs += [pl.BlockSpec((tm, tn), lambda i, j, k: (i, j)) for _ in tiled]
    specs += [pl.BlockSpec((tm, r.shape[1]), lambda i, j, k: (i, 0)) for r in mrows]
    specs += [pl.BlockSpec((1, tn), lambda i, j, k: (0, j)) for _ in ncols]
    if out_blocks is None:
        out_shape = jax.ShapeDtypeStruct((M, N), out_dtype)
        out_spec = pl.BlockSpec((tm, tn), lambda i, j, k: (i, j))
    else:
        nper = N // out_blocks
        assert nper % tn == 0
        jb = nper // tn
        out_shape = jax.ShapeDtypeStruct((out_blocks, M, nper), out_dtype)
        out_spec = pl.BlockSpec((None, tm, tn), lambda i, j, k: (j // jb, i, j % jb))
    ne = len(tiled) + len(mrows) + len(ncols)
    dot = {"nn": _nn, "nt": _nt, "tn": _tn}[mode]

    def body(a_ref, b_ref, *rest):
        extras, o_ref = rest[:ne], rest[ne]

        def finish(acc):
            if epilogue is not None:
                acc = epilogue(acc, *[e[...] for e in extras])
            o_ref[...] = acc.astype(o_ref.dtype)

        if nk == 1:
            finish(dot(a_ref[...], b_ref[...]))
        else:
            acc_ref = rest[ne + 1]
            k = pl.program_id(2)

            @pl.when(k == 0)
            def _():
                acc_ref[...] = jnp.zeros_like(acc_ref)

            acc_ref[...] += dot(a_ref[...], b_ref[...])

            @pl.when(k == nk - 1)
            def _():
                finish(acc_ref[...])

    return pl.pallas_call(
        body, name=name, out_shape=out_shape, grid=(M // tm, N // tn, nk),
        in_specs=specs, out_specs=out_spec,
        scratch_shapes=[] if nk == 1 else [pltpu.VMEM((tm, tn), F32)],
        compiler_params=_params(("parallel", "parallel", "arbitrary"), VMEM_BIG),
    )(a, b, *tiled, *mrows, *ncols)


def _rowwise(name, fn, tiled, consts, outs, accs, ts):
    tl = [(t, t.shape[1], 0) if not isinstance(t, tuple) else t for t in tiled]
    s_len = tl[0][0].shape[0]
    assert s_len % ts == 0
    nt_, nc_, no_ = len(tl), len(consts), len(outs)

    def body(*refs):
        t_refs, c_refs = refs[:nt_], refs[nt_:nt_ + nc_]
        o_refs, a_refs = refs[nt_ + nc_:nt_ + nc_ + no_], refs[nt_ + nc_ + no_:]
        res_o, res_a = fn(*[r[...] for r in t_refs], *[r[...] for r in c_refs])
        for r, v in zip(o_refs, res_o, strict=True):
            r[...] = v.astype(r.dtype)
        if a_refs:
            @pl.when(pl.program_id(0) == 0)
            def _():
                for r in a_refs:
                    r[...] = jnp.zeros_like(r)

            for r, v in zip(a_refs, res_a, strict=True):
                r[...] += v

    in_specs = [pl.BlockSpec((ts, w), functools.partial(lambda i, cb: (i, cb), cb=cb)) for (_, w, cb) in tl]
    in_specs += [pl.BlockSpec(c.shape, lambda i: (0, 0)) for c in consts]
    out_specs = [pl.BlockSpec((ts, c), lambda i: (i, 0)) for (c, _) in outs]
    out_specs += [pl.BlockSpec(shp, lambda i: (0, 0)) for shp in accs]
    out_shape = [jax.ShapeDtypeStruct((s_len, c), dt) for (c, dt) in outs]
    out_shape += [jax.ShapeDtypeStruct(shp, F32) for shp in accs]
    res = pl.pallas_call(
        body, name=name, out_shape=out_shape, grid=(s_len // ts,), in_specs=in_specs, out_specs=out_specs,
        compiler_params=_params(("arbitrary",) if accs else ("parallel",), VMEM_BIG),
    )(*[t[0] for t in tl], *consts)
    return res[:no_], res[no_:]


def _norm_mod_fn(x, nw, sc, sh):
    r = lax.rsqrt(jnp.mean(x * x, axis=-1, keepdims=True) + NORM_EPS)
    return (x * r * nw) * (1.0 + sc) + sh


def _norm_mod_fwd(name, x, nw, sc, sh):
    (hn,), _ = _rowwise(name, lambda x, nw, sc, sh: ([_norm_mod_fn(x, nw, sc, sh)], []),
                        [x], [nw, sc, sh], [(x.shape[1], BF16)], [], 512)
    return hn


def _norm_mod_bwd(name, x, dhn_parts, dres, nw, sc, sh):
    n = len(dhn_parts)
    d = x.shape[1]

    def fn(x, *rest):
        dhn = rest[0]
        for p in rest[1:n]:
            dhn = dhn + p
        dres, nw, sc, sh = rest[n:]
        _, vjp = jax.vjp(_norm_mod_fn, x, nw, sc, sh)
        dx, dnw, dsc, dsh = vjp(dhn)
        return [dx + dres], [dnw, dsc, dsh]

    (dx,), (g_nw, dsc, dsh) = _rowwise(name, fn, [x, *dhn_parts, dres], [nw, sc, sh], [(d, F32)],
                                       [(1, d), (1, d), (1, d)], 256)
    return dx, g_nw, dsc, dsh


def _rope_tables(pos_col, inv_row):
    def fn(pos, inv):
        ang = pos.astype(F32) * inv
        e = lax.broadcasted_iota(jnp.int32, (1, 128), 1) % HEAD_DIM
        cos, sin = jnp.cos(ang), jnp.sin(ang)
        half = ROT_DIM // 2
        return [jnp.where(e < ROT_DIM, cos, 1.0), jnp.where(e < half, -sin, 0.0),
                jnp.where((e >= half) & (e < ROT_DIM), sin, 0.0)], []

    (c, sa, sb), _ = _rowwise("rope_tables", fn, [pos_col], [inv_row], [(128, F32)] * 3, [], 512)
    return c, sa, sb


def _rot_fwd(t, c, sa, sb):
    n = t.shape[1]
    rep = n // 128
    c, sa, sb = (jnp.tile(u, (1, rep)) for u in (c, sa, sb))
    return t * c + pltpu.roll(t, n - ROT_DIM // 2, 1) * sa + pltpu.roll(t, ROT_DIM // 2, 1) * sb


def _rot_bwd(g, c, sa, sb):
    n = g.shape[1]
    rep = n // 128
    c, sa, sb = (jnp.tile(u, (1, rep)) for u in (c, sa, sb))
    return g * c + pltpu.roll(g * sa, ROT_DIM // 2, 1) + pltpu.roll(g * sb, n - ROT_DIM // 2, 1)


ATT_TQ = 128
ATT_TK = ATT_TQ + 2 * BAND


def _attn_specs(g, d, l):
    q_spec = pl.BlockSpec((l, 128), lambda r, hp: (0, r * 24 + 4 * g + hp))
    k_spec = pl.BlockSpec((l, 128), lambda r, hp: (0, r * 24 + 12 + 4 * g + hp))
    v_spec = pl.BlockSpec((l, 128), lambda r, hp: (0, r * 12 + 4 * g + hp))
    o_spec = pl.BlockSpec((l, 128), lambda r, hp: (0, r * 4 + hp))
    return q_spec, k_spec, v_spec, o_spec


def _attn_tile_geometry(t, l):
    q0 = pl.multiple_of(t * ATT_TQ, ATT_TQ)
    ws = pl.multiple_of(jnp.clip(t * ATT_TQ - BAND, 0, l - ATT_TK), BAND)
    qpos = q0 + lax.broadcasted_iota(jnp.int32, (ATT_TQ, 1), 0)
    kpos = ws + lax.broadcasted_iota(jnp.int32, (1, ATT_TK), 1)
    valid = jnp.abs(kpos - qpos) <= BAND
    return q0, ws, valid


def _attn_fwd(g, qk, v):
    s_len = qk.shape[0]
    d = DILATIONS[g]
    l = s_len // d
    assert l % ATT_TQ == 0 and l >= ATT_TK
    q_spec, k_spec, v_spec, o_spec = _attn_specs(g, d, l)
    scale = 1.0 / math.sqrt(HEAD_DIM)

    def body(q_ref, k_ref, v_ref, o_ref, lse_ref):
        lane = lax.broadcasted_iota(jnp.int32, (1, 128), 1)
        in_h = [lane < HEAD_DIM, lane >= HEAD_DIM]

        def tile(t, carry):
            q0, ws, valid = _attn_tile_geometry(t, l)
            q = q_ref[pl.ds(q0, ATT_TQ), :]
            k = k_ref[pl.ds(ws, ATT_TK), :]
            vv = v_ref[pl.ds(ws, ATT_TK), :]
            outs, lses = [], []
            for h in range(2):
                qm = jnp.where(in_h[h], q, jnp.zeros_like(q))
                s = jnp.where(valid, _nt(qm, k) * scale, NEG_BIG)
                m = jnp.max(s, axis=1, keepdims=True)
                p = jnp.exp(s - m)
                den = jnp.sum(p, axis=1, keepdims=True)
                outs.append(_nn(p, vv) / den)
                lses.append(m + jnp.log(den))
            o_ref[pl.ds(q0, ATT_TQ), :] = jnp.where(in_h[0], outs[0], outs[1])
            lse_ref[pl.ds(q0, ATT_TQ), :] = jnp.where(in_h[0], lses[0], lses[1])
            return carry

        lax.fori_loop(0, l // ATT_TQ, tile, 0)

    o, lse = pl.pallas_call(
        body, name=f"attn_fwd_g{g}", grid=(d, 4),
        out_shape=[jax.ShapeDtypeStruct((l, d * 512), F32)] * 2,
        in_specs=[q_spec, k_spec, v_spec], out_specs=[o_spec, o_spec],
        compiler_params=_params(("parallel", "parallel"), VMEM_BIG),
    )(qk.reshape(l, d * 3072), qk.reshape(l, d * 3072), v.reshape(l, d * 1536))
    return o.reshape(s_len, 512), lse.reshape(s_len, 512)


def _attn_bwd(g, qk, v, o, lse, do, dlse):
    s_len = qk.shape[0]
    d = DILATIONS[g]
    l = s_len // d
    q_spec, k_spec, v_spec, o_spec = _attn_specs(g, d, l)
    scale = 1.0 / math.sqrt(HEAD_DIM)

    def body(q_ref, k_ref, v_ref, o_ref, lse_ref, do_ref, dlse_ref, dq_ref, dk_ref, dv_ref):
        lane = lax.broadcasted_iota(jnp.int32, (1, 128), 1)
        in_h = [lane < HEAD_DIM, lane >= HEAD_DIM]
        dk_ref[...] = jnp.zeros_like(dk_ref)
        dv_ref[...] = jnp.zeros_like(dv_ref)

        def tile(t, carry):
            q0, ws, valid = _attn_tile_geometry(t, l)
            rows = pl.ds(q0, ATT_TQ)
            win = pl.ds(ws, ATT_TK)
            q, k, vv = q_ref[rows, :], k_ref[win, :], v_ref[win, :]
            dout, lse_t, dlse_t = do_ref[rows, :], lse_ref[rows, :], dlse_ref[rows, :]
            od = dout * o_ref[rows, :]
            dqs, dks, dvs = [], [], []
            for h in range(2):
                c0 = h * HEAD_DIM
                qm = jnp.where(in_h[h], q, jnp.zeros_like(q))
                s = jnp.where(valid, _nt(qm, k) * scale, NEG_BIG)
                p = jnp.exp(s - lse_t[:, c0:c0 + 1])
                dom = jnp.where(in_h[h], dout, 0.0)
                dp = _nt(dom, vv)
                delta = jnp.sum(jnp.where(in_h[h], od, 0.0), axis=1, keepdims=True)
                ds = (p * (dp - delta + dlse_t[:, c0:c0 + 1]) * scale).astype(BF16)
                dqs.append(_nn(ds, k))
                dks.append(_tn(ds, q))
                dvs.append(_tn(p, dout))
            dq_ref[rows, :] = jnp.where(in_h[0], dqs[0], dqs[1])
            dk_ref[win, :] += jnp.where(in_h[0], dks[0], dks[1])
            dv_ref[win, :] += jnp.where(in_h[0], dvs[0], dvs[1])
            return carry

        lax.fori_loop(0, l // ATT_TQ, tile, 0)

    view = lambda a: a.reshape(l, d * 512)
    dq, dk, dv = pl.pallas_call(
        body, name=f"attn_bwd_g{g}", grid=(d, 4),
        out_shape=[jax.ShapeDtypeStruct((l, d * 512), F32)] * 3,
        in_specs=[q_spec, k_spec, v_spec, o_spec, o_spec, o_spec, o_spec], out_specs=[o_spec] * 3,
        compiler_params=_params(("parallel", "parallel"), VMEM_BIG),
    )(qk.reshape(l, d * 3072), qk.reshape(l, d * 3072), v.reshape(l, d * 1536), view(o), view(lse), view(do), view(dlse))
    return dq.reshape(s_len, 512), dk.reshape(s_len, 512), dv.reshape(s_len, 512)


def _mix_weights(ls):
    mx = jnp.maximum(jnp.maximum(ls[0], ls[1]), ls[2])
    es = [jnp.exp(x - mx) for x in ls]
    tot = es[0] + es[1] + es[2]
    return [e / tot for e in es]


def _attn_out(os_, lses, z, x, gate, w_out):
    s_len, dm = x.shape
    tm = 256
    wdt = 512

    def body(o0, o1, o2, l0, l1, l2, z_ref, x_ref, g_ref, w_ref, a_ref, y_ref, x1_ref):
        alphas = _mix_weights([l0[...], l1[...], l2[...]])
        y = jnp.zeros((tm, dm), F32)
        for g, o_ref in enumerate((o0, o1, o2)):
            a_g = (o_ref[...] * alphas[g] * _silu(z_ref[:, g * wdt:(g + 1) * wdt])).astype(BF16)
            a_ref[:, g * wdt:(g + 1) * wdt] = a_g
            y = y + _nn(a_g, w_ref[g * wdt:(g + 1) * wdt, :])
        y_ref[...] = y
        x1_ref[...] = x_ref[...] + g_ref[...] * y

    row = lambda c: pl.BlockSpec((tm, c), lambda i: (i, 0))
    return pl.pallas_call(
        body, name="attn_out", grid=(s_len // tm,),
        out_shape=[jax.ShapeDtypeStruct((s_len, 3 * wdt), BF16), jax.ShapeDtypeStruct((s_len, dm), F32),
                   jax.ShapeDtypeStruct((s_len, dm), F32)],
        in_specs=[row(wdt)] * 6 + [row(3 * wdt), row(dm), pl.BlockSpec((1, dm), lambda i: (0, 0)),
                                   pl.BlockSpec(w_out.shape, lambda i: (0, 0))],
        out_specs=[row(3 * wdt), row(dm), row(dm)],
        compiler_params=_params(("parallel",), VMEM_BIG),
    )(*os_, *lses, z, x, gate, w_out)


def _mix_bwd(da, os_, lses, z):
    wdt = 512

    def fn(da, o0, o1, o2, l0, l1, l2, z):
        os_t, ls = [o0, o1, o2], [l0, l1, l2]
        alphas = _mix_weights(ls)
        hi = lax.broadcasted_iota(jnp.int32, (wdt, wdt), 0) // HEAD_DIM
        hj = lax.broadcasted_iota(jnp.int32, (wdt, wdt), 1) // HEAD_DIM
        seg = (hi == hj).astype(F32)
        dos, dal, dzs = [], [], []
        for g in range(3):
            zg = z[:, g * wdt:(g + 1) * wdt]
            sig = jax.nn.sigmoid(zg)
            dag = da[:, g * wdt:(g + 1) * wdt]
            dmix = dag * zg * sig
            dzs.append(dag * os_t[g] * alphas[g] * (sig * (1.0 + zg * (1.0 - sig))))
            dos.append(dmix * alphas[g])
            dal.append(_hnn(dmix * os_t[g], seg))
        mean = alphas[0] * dal[0] + alphas[1] * dal[1] + alphas[2] * dal[2]
        dls = [alphas[g] * (dal[g] - mean) for g in range(3)]
        return dos + dls + [jnp.concatenate(dzs, axis=1)], []

    outs, _ = _rowwise("mix_bwd", fn, [da, *os_, *lses, z], [], [(wdt, F32)] * 6 + [(3 * wdt, BF16)], [], 256)
    return outs[:3], outs[3:6], outs[6]


def _rot_pack_bwd(dqs, dks, dvs, tabs):
    wdt = 512

    def fn(*args):
        grads, (c, sa, sb) = args[:9], args[9:]
        cols = [_rot_bwd(gq, c, sa, sb) for gq in grads[:6]] + list(grads[6:])
        return [jnp.concatenate(cols, axis=1)], []

    (out,), _ = _rowwise("rot_pack_bwd", fn, [*dqs, *dks, *dvs, *tabs], [], [(9 * wdt, BF16)], [], 256)
    return out


CONV_CB = 256
CONV_R = 128
CONV_PAD = 8


def _conv_window_sum(win, w, off, sign):
    acc = None
    for j in range(CONV_WIDTH):
        o = off + sign * j
        term = win[o:o + CONV_R, :] * w[j:j + 1, :]
        acc = term if acc is None else acc + term
    return acc


def _conv_fwd(xpre, cw, cb):
    s_len, ch = xpre.shape
    nchunk = s_len // CONV_R

    def body(x_ref, w_ref, b_ref, o_ref, xp):
        zero = jnp.zeros((CONV_PAD, CONV_CB), F32)
        xp[0:CONV_PAD, :] = zero
        xp[s_len + CONV_PAD:s_len + 2 * CONV_PAD, :] = zero

        def fill(ci, carry):
            base = pl.multiple_of(ci * CONV_R, CONV_R)
            xp[pl.ds(base + CONV_PAD, CONV_R), :] = x_ref[pl.ds(base, CONV_R), :]
            return carry

        lax.fori_loop(0, nchunk, fill, 0)
        w = w_ref[...]
        b = b_ref[...]

        def chunk(ci, carry):
            base = pl.multiple_of(ci * CONV_R, CONV_R)
            win = xp[pl.ds(base, CONV_R + 2 * CONV_PAD), :]
            u = _conv_window_sum(win, w, CONV_PAD - CONV_WIDTH // 2, 1) + b
            o_ref[pl.ds(base, CONV_R), :] = _silu(u)
            return carry

        lax.fori_loop(0, nchunk, chunk, 0)

    col = lambda r: pl.BlockSpec((r, CONV_CB), lambda j: (0, j))
    return pl.pallas_call(
        body, name="conv_fwd", grid=(ch // CONV_CB,), out_shape=jax.ShapeDtypeStruct((s_len, ch), F32),
        in_specs=[col(s_len), col(CONV_WIDTH), col(1)], out_specs=col(s_len),
        scratch_shapes=[pltpu.VMEM((s_len + 2 * CONV_PAD, CONV_CB), F32)],
        compiler_params=_params(("parallel",), VMEM_BIG),
    )(xpre, cw, cb)


def _conv_bwd(xpre, da, db, cw, cb):
    s_len, ch = xpre.shape
    nchunk = s_len // CONV_R
    half = CONV_WIDTH // 2

    def body(x_ref, da_ref, db_ref, w_ref, b_ref, dx_ref, gw_ref, gb_ref, xp, dcp):
        zero = jnp.zeros((CONV_PAD, CONV_CB), F32)
        for buf in (xp, dcp):
            buf[0:CONV_PAD, :] = zero
            buf[s_len + CONV_PAD:s_len + 2 * CONV_PAD, :] = zero

        def fill(ci, carry):
            base = pl.multiple_of(ci * CONV_R, CONV_R)
            xp[pl.ds(base + CONV_PAD, CONV_R), :] = x_ref[pl.ds(base, CONV_R), :]
            return carry

        lax.fori_loop(0, nchunk, fill, 0)
        w = w_ref[...]
        b = b_ref[...]

        def first(ci, carry):
            base = pl.multiple_of(ci * CONV_R, CONV_R)
            win = xp[pl.ds(base, CONV_R + 2 * CONV_PAD), :]
            u = _conv_window_sum(win, w, CONV_PAD - half, 1) + b
            sig = jax.nn.sigmoid(u)
            dc = (da_ref[pl.ds(base, CONV_R), :] + db_ref[pl.ds(base, CONV_R), :]) * (sig * (1.0 + u * (1.0 - sig)))
            dcp[pl.ds(base + CONV_PAD, CONV_R), :] = dc
            gb = carry[0] + jnp.sum(dc, axis=0, keepdims=True)
            gws = [carry[1 + j] + jnp.sum(dc * win[CONV_PAD - half + j:CONV_PAD - half + j + CONV_R, :],
                                          axis=0, keepdims=True) for j in range(CONV_WIDTH)]
            return (gb, *gws)

        z1 = jnp.zeros((1, CONV_CB), F32)
        sums = lax.fori_loop(0, nchunk, first, (z1,) * (1 + CONV_WIDTH))
        gb_ref[...] = sums[0]
        for j in range(CONV_WIDTH):
            gw_ref[j:j + 1, :] = sums[1 + j]

        def second(ci, carry):
            base = pl.multiple_of(ci * CONV_R, CONV_R)
            win = dcp[pl.ds(base, CONV_R + 2 * CONV_PAD), :]
            dx_ref[pl.ds(base, CONV_R), :] = _conv_window_sum(win, w, CONV_PAD + half, -1).astype(dx_ref.dtype)
            return carry

        lax.fori_loop(0, nchunk, second, 0)

    col = lambda r: pl.BlockSpec((r, CONV_CB), lambda j: (0, j))
    return pl.pallas_call(
        body, name="conv_bwd", grid=(ch // CONV_CB,),
        out_shape=[jax.ShapeDtypeStruct((s_len, ch), BF16), jax.ShapeDtypeStruct((CONV_WIDTH, ch), F32),
                   jax.ShapeDtypeStruct((1, ch), F32)],
        in_specs=[col(s_len), col(s_len), col(s_len), col(CONV_WIDTH), col(1)],
        out_specs=[col(s_len), col(CONV_WIDTH), col(1)],
        scratch_shapes=[pltpu.VMEM((s_len + 2 * CONV_PAD, CONV_CB), F32)] * 2,
        compiler_params=_params(("parallel",), VMEM_BIG),
    )(xpre, da, db, cw, cb)


SSD_GW = 256
SSD_N = 128


def _ssd_group_fn(g, dirn):
    ln = CHUNK
    nh = 2 * SSD_HEADS

    def f(xs, bm, cm, dt, alog, st):
        da = dt * (-jnp.exp(alog))
        ri = lax.broadcasted_iota(jnp.int32, (ln, ln), 0)
        cj = lax.broadcasted_iota(jnp.int32, (ln, ln), 1)
        mask = (cj <= ri) if dirn == 0 else (cj >= ri)
        cum = _hnn(mask.astype(F32), da)
        cum_t = cum.T
        last = ln - 1 if dirn == 0 else 0
        rowsel = (lax.broadcasted_iota(jnp.int32, (ln, 1), 0) == last).astype(F32)
        tot = jnp.sum(cum * rowsel, axis=0, keepdims=True)
        hh = lax.broadcasted_iota(jnp.int32, (nh, SSD_GW), 0)
        jj = lax.broadcasted_iota(jnp.int32, (nh, SSD_GW), 1)
        expand = (hh == dirn * SSD_HEADS + 4 * g + jj // HEAD_DIM).astype(F32)
        xdt = xs * _hnn(dt, expand)
        w_e = _hnn(jnp.exp(tot - cum), expand)
        ce_e = _hnn(jnp.exp(cum), expand)
        cd_e = jnp.sum(ce_e * rowsel, axis=0, keepdims=True)
        cb = _bnt(cm, bm)
        y = _bnn(cm, st) * ce_e
        lane_head = lax.broadcasted_iota(jnp.int32, (1, SSD_GW), 1) // HEAD_DIM
        for j in range(4):
            hidx = dirn * SSD_HEADS + 4 * g + j
            col = jnp.sum(cum * (lax.broadcasted_iota(jnp.int32, (1, nh), 1) == hidx).astype(F32), axis=1, keepdims=True)
            row = jnp.sum(cum_t * (lax.broadcasted_iota(jnp.int32, (nh, 1), 0) == hidx).astype(F32), axis=0, keepdims=True)
            dec = jnp.exp(jnp.where(mask, col - row, NEG_BIG))
            y = y + _bnn(cb * dec, xdt) * (lane_head == j).astype(F32)
        st_out = st * cd_e + _btn(bm, xdt * w_e)
        return y, st_out

    return f


def _ssd_in_specs(kk):
    ln = CHUNK
    return [pl.BlockSpec((ln, 2048), lambda i: (kk(i), 0)),
            pl.BlockSpec((ln, 1024), lambda i: (kk(i), 2)),
            pl.BlockSpec((ln, 1024), lambda i: (kk(i), 3)),
            pl.BlockSpec((ln, 2 * SSD_HEADS), lambda i: (kk(i), 0)),
            pl.BlockSpec((1, 2 * SSD_HEADS), lambda i: (0, 0))]


def _ssd_fwd(xbc, dt, alog, dirn):
    s_len = xbc.shape[0]
    nc = s_len // CHUNK
    kk = (lambda i: i) if dirn == 0 else (lambda i: nc - 1 - i)

    def body(x_ref, b_ref, c_ref, dt_ref, al_ref, y_ref, sts_ref, st):
        @pl.when(pl.program_id(0) == 0)
        def _():
            st[...] = jnp.zeros_like(st)

        sts_ref[0] = st[...]
        dtv, al = dt_ref[...], al_ref[...]
        for g in range(SSD_GROUPS):
            xc = slice(g * SSD_GW, (g + 1) * SSD_GW)
            gc = slice(g * SSD_N, (g + 1) * SSD_N)
            y, st_new = _ssd_group_fn(g, dirn)(x_ref[:, xc], b_ref[:, gc], c_ref[:, gc], dtv, al, st[:, xc])
            y_ref[:, xc] = y
            st[:, xc] = st_new

    return pl.pallas_call(
        body, name=f"ssd_fwd_d{dirn}", grid=(nc,),
        out_shape=[jax.ShapeDtypeStruct((s_len, 2048), F32), jax.ShapeDtypeStruct((nc, SSD_N, 2048), F32)],
        in_specs=_ssd_in_specs(kk),
        out_specs=[pl.BlockSpec((CHUNK, 2048), lambda i: (kk(i), 0)),
                   pl.BlockSpec((1, SSD_N, 2048), lambda i: (kk(i), 0, 0))],
        scratch_shapes=[pltpu.VMEM((SSD_N, 2048), F32)],
        compiler_params=_params(("arbitrary",), VMEM_BIG),
    )(xbc, xbc, xbc, dt, alog)


def _ssd_bwd(xbc, dt, alog, states, dy, d_e, dirn):
    s_len = xbc.shape[0]
    nc = s_len // CHUNK
    kk = (lambda i: nc - 1 - i) if dirn == 0 else (lambda i: i)

    def body(x_ref, b_ref, c_ref, dt_ref, al_ref, sts_ref, dy_ref, de_ref, dx_ref, ddt_ref, dal_ref, dst):
        @pl.when(pl.program_id(0) == 0)
        def _():
            dst[...] = jnp.zeros_like(dst)
            dal_ref[...] = jnp.zeros_like(dal_ref)

        dtv, al = dt_ref[...], al_ref[...]
        ddt = jnp.zeros_like(dtv)
        dal = jnp.zeros_like(al)
        for g in range(SSD_GROUPS):
            xc = slice(g * SSD_GW, (g + 1) * SSD_GW)
            gc = slice(g * SSD_N, (g + 1) * SSD_N)
            _, vjp = jax.vjp(_ssd_group_fn(g, dirn), x_ref[:, xc], b_ref[:, gc], c_ref[:, gc], dtv, al, sts_ref[0, :, xc])
            dyg = dy_ref[:, xc]
            dxs, dbm, dcm, ddt_g, dal_g, dst_g = vjp((dyg, dst[:, xc]))
            if dirn == 0:
                dxs = dxs + dyg * de_ref[:, xc]
            dx_ref[:, xc] = dxs
            dx_ref[:, 2048 + g * SSD_N:2048 + (g + 1) * SSD_N] = dbm
            dx_ref[:, 3072 + g * SSD_N:3072 + (g + 1) * SSD_N] = dcm
            dst[:, xc] = dst_g
            ddt = ddt + ddt_g
            dal = dal + dal_g
        ddt_ref[...] = ddt
        dal_ref[...] += dal

    return pl.pallas_call(
        body, name=f"ssd_bwd_d{dirn}", grid=(nc,),
        out_shape=[jax.ShapeDtypeStruct((s_len, 4096), F32), jax.ShapeDtypeStruct((s_len, 2 * SSD_HEADS), F32),
                   jax.ShapeDtypeStruct((1, 2 * SSD_HEADS), F32)],
        in_specs=_ssd_in_specs(kk) + [pl.BlockSpec((1, SSD_N, 2048), lambda i: (kk(i), 0, 0)),
                                      pl.BlockSpec((CHUNK, 2048), lambda i: (kk(i), 0)),
                                      pl.BlockSpec((1, 2048), lambda i: (0, 0))],
        out_specs=[pl.BlockSpec((CHUNK, 4096), lambda i: (kk(i), 0)),
                   pl.BlockSpec((CHUNK, 2 * SSD_HEADS), lambda i: (kk(i), 0)),
                   pl.BlockSpec((1, 2 * SSD_HEADS), lambda i: (0, 0))],
        scratch_shapes=[pltpu.VMEM((SSD_N, 2048), F32)],
        compiler_params=_params(("arbitrary",), VMEM_BIG),
    )(xbc, xbc, xbc, dt, alog, states, dy, d_e)


def _gate_norm_fn(yf, yb, xs, z, d_e, nw):
    yg = (yf + yb + xs * d_e) * _silu(z)
    return yg * lax.rsqrt(jnp.mean(yg * yg, axis=-1, keepdims=True) + NORM_EPS) * nw


def _gate_norm_fwd(yf, yb, xbc, z, d_e, nw):
    (u,), _ = _rowwise("ssd_gate_norm", lambda *a: ([_gate_norm_fn(*a)], []),
                       [yf, yb, (xbc, 2048, 0), z], [d_e, nw], [(2048, BF16)], [], 256)
    return u


def _gate_norm_bwd(du, yf, yb, xbc, z, d_e, nw):
    def fn(du, yf, yb, xs, z, d_e, nw):
        _, vjp = jax.vjp(_gate_norm_fn, yf, yb, xs, z, d_e, nw)
        dyf, _, _, dz, dde, dnw = vjp(du)
        hh = lax.broadcasted_iota(jnp.int32, (2048, SSD_HEADS), 0) // HEAD_DIM
        jj = lax.broadcasted_iota(jnp.int32, (2048, SSD_HEADS), 1)
        return [dyf, dz], [dnw, _hnn(jnp.broadcast_to(dde, (8, 2048)), (hh == jj).astype(F32))[0:1]]

    (dys, dz), (g_nw, g_d) = _rowwise("ssd_gate_norm_bwd", fn, [du, yf, yb, (xbc, 2048, 0), z], [d_e, nw],
                                      [(2048, F32), (2048, BF16)], [(1, 2048), (1, SSD_HEADS)], 128)
    return dys, dz, g_nw, g_d


def _loss_bwd(x1, y1, tgt, gate, fnw):
    dm = x1.shape[1]

    def fn(x1, y1, tgt, gate, fnw):
        def head(x2, fnw):
            yf = (x2 * lax.rsqrt(jnp.mean(x2 * x2, axis=-1, keepdims=True) + NORM_EPS)) * fnw
            err = yf - tgt
            return 0.5 * jnp.sum(jnp.mean(err * err, axis=-1, keepdims=True), axis=0, keepdims=True)

        x2 = x1 + gate * y1
        loss, vjp = jax.vjp(head, x2, fnw)
        dx2, dfnw = vjp(jnp.ones((1, 1), F32))
        return [dx2, gate * dx2], [dfnw, jnp.sum(dx2 * y1, axis=0, keepdims=True), jnp.broadcast_to(loss, (1, 128))]

    (dx2, dy1), (g_fnw, dgate, loss) = _rowwise("loss_bwd", fn, [x1, y1, tgt], [gate, fnw], [(dm, F32), (dm, BF16)],
                                                [(1, dm), (1, dm), (1, 128)], 256)
    return dx2, dy1, g_fnw, dgate, loss


def _gate_bwd(dx, y, gate):
    dm = dx.shape[1]
    (dy,), (dgate,) = _rowwise("gate_bwd", lambda dx, y, gate: ([gate * dx], [jnp.sum(dx * y, axis=0, keepdims=True)]),
                               [dx, y], [gate], [(dm, BF16)], [(1, dm)], 512)
    return dy, dgate


def _softplus_fwd(dt_raw, bias):
    (dt,), _ = _rowwise("dt_softplus", lambda r, b: ([jax.nn.softplus(r + b)], []), [dt_raw], [bias],
                        [(dt_raw.shape[1], F32)], [], 512)
    return dt


def _softplus_bwd(ddt_f, ddt_b, dt_raw, bias):
    def fn(df, db, r, b):
        g = (df + db) * jax.nn.sigmoid(r + b)
        return [g], [jnp.sum(g, axis=0, keepdims=True)]

    w = dt_raw.shape[1]
    (g,), (gb,) = _rowwise("dt_softplus_bwd", fn, [ddt_f, ddt_b, dt_raw], [bias], [(w, BF16)], [(1, w)], 512)
    return g, gb


def _whole(a):
    nd = len(a.shape)
    return pl.BlockSpec(a.shape, lambda *_: (0,) * nd)


def _mod_part(c_all, mod_w):
    nl, _, ncol = mod_w.shape
    nb = c_all.shape[0]

    def body(c_ref, w_ref, o_ref):
        cond = _silu(c_ref[...])
        for i in range(nl):
            o_ref[i * nb:(i + 1) * nb, :] = _nn(cond, w_ref[i])

    return pl.pallas_call(body, name="mod_part", out_shape=jax.ShapeDtypeStruct((nl * nb, ncol), F32),
                          compiler_params=_params(None, VMEM_BIG))(c_all, mod_w)


def _mod_finish(mod_nb, mod_b):
    def body(a_ref, b_ref, o_ref):
        o_ref[...] = a_ref[...] + b_ref[...]

    return pl.pallas_call(body, name="mod_finish", out_shape=jax.ShapeDtypeStruct(mod_b.shape, F32))(mod_nb, mod_b)


def _mod_grad(c_all, dmod_sh):
    nl, nb, ncol = dmod_sh.shape
    dm = c_all.shape[1]

    def body(c_ref, d_ref, o_ref):
        cond = _silu(c_ref[...])
        for i in range(nl):
            o_ref[i] = _tn(cond, d_ref[i])

    return pl.pallas_call(body, name="mod_grad", out_shape=jax.ShapeDtypeStruct((nl, dm, ncol), F32),
                          compiler_params=_params(None, VMEM_BIG))(c_all, dmod_sh)


PACK_ROWS = 16
PACK_COLS = 1024


def _pack_small(rows6, nw2, fnw, b64, a64, d32, extra=None):
    args = [rows6, nw2, fnw, b64, a64, d32] + ([extra] if extra is not None else [])

    def body(*refs):
        o_ref = refs[-1]
        o_ref[...] = jnp.zeros_like(o_ref)
        o_ref[0:6, :] = refs[0][...]
        o_ref[6:8, :] = refs[1][...]
        o_ref[8:9, :] = refs[2][...]
        o_ref[9:10, 0:64] = refs[3][...]
        o_ref[9:10, 64:128] = refs[4][...]
        o_ref[9:10, 128:160] = refs[5][...]
        if extra is not None:
            o_ref[9:10, 256:384] = refs[6][...]

    return pl.pallas_call(body, name="pack_small", out_shape=jax.ShapeDtypeStruct((PACK_ROWS, PACK_COLS), F32))(*args)


def _unpack_small(p):
    return (p[0:6].reshape(2, 3 * PACK_COLS), p[6:8], p[8], p[9, 0:64].reshape(1, 2, 32), p[9, 64:128].reshape(1, 2, 32),
            p[9, 128:160].reshape(1, 32))


def _pack_ssd_small(cw, cb, nw):
    def body(cw_ref, cb_ref, nw_ref, o_ref):
        o_ref[...] = jnp.zeros_like(o_ref)
        o_ref[0:5, :] = cw_ref[...]
        o_ref[5:6, :] = cb_ref[...]
        o_ref[6:7, 0:256] = nw_ref[...]

    return pl.pallas_call(body, name="pack_ssd_small", out_shape=jax.ShapeDtypeStruct((8, 512), F32))(cw, cb, nw)


def _adamw(name, w, parts, m, v, tr):
    r_, c_ = w.shape
    p_ = parts.shape[0]
    tr = min(tr, r_)
    assert r_ % tr == 0

    def body(w_ref, p_ref, m_ref, v_ref, g_ref, d_ref, m2_ref, v2_ref):
        g = p_ref[0].astype(F32)
        for s in range(1, p_):
            g = g + p_ref[s].astype(F32)
        m2 = ADAM_B1 * m_ref[...] + (1.0 - ADAM_B1) * g
        v2 = ADAM_B2 * v_ref[...] + (1.0 - ADAM_B2) * (g * g)
        m_hat = m2 / (1.0 - ADAM_B1 ** ADAM_STEP)
        v_hat = v2 / (1.0 - ADAM_B2 ** ADAM_STEP)
        g_ref[...] = g
        d_ref[...] = -ADAM_LR * (m_hat / (jnp.sqrt(v_hat) + ADAM_EPS) + ADAM_WD * w_ref[...])
        m2_ref[...] = m2
        v2_ref[...] = v2

    blk = pl.BlockSpec((tr, c_), lambda i: (i, 0))
    return pl.pallas_call(
        body, name=name, grid=(r_ // tr,), out_shape=[jax.ShapeDtypeStruct((r_, c_), F32)] * 4,
        in_specs=[blk, pl.BlockSpec((p_, tr, c_), lambda i: (0, i, 0)), blk, blk], out_specs=[blk] * 4,
        compiler_params=_params(("parallel",), VMEM_BIG),
    )(w, parts, m, v)


def _dev_index(p):
    return 4 * p[0] + 2 * p[1] + p[2]


def _all_gather(name, xs):
    n = len(xs)
    hbm = pl.BlockSpec(memory_space=pl.ANY)

    def body(*refs):
        x_refs, o_refs = refs[:n], refs[n:2 * n]
        send_sems, recv_sems, local_sems = refs[2 * n:]
        x, y, c = lax.axis_index("x"), lax.axis_index("y"), lax.axis_index("c")
        me, sibling = (x, y, c), (x, y, 1 - c)
        chips = [(1 - x, y), (x, 1 - y), (1 - x, 1 - y)]

        def copy(a, k, block, to, src=None):
            dst = o_refs[a].at[_dev_index(block)]
            return pltpu.make_async_remote_copy(
                src_ref=dst if src is None else src, dst_ref=dst, send_sem=send_sems.at[a, k],
                recv_sem=recv_sems.at[a, k], device_id=to, device_id_type=MESH)

        mine = [pltpu.make_async_copy(x_refs[a], o_refs[a].at[_dev_index(me)], local_sems.at[a]) for a in range(n)]
        for cp in mine:
            cp.start()
        first = []
        for a in range(n):
            first.append(copy(a, 0, me, sibling, src=x_refs[a]))
            first += [copy(a, 1 + j, me, (*chip, c), src=x_refs[a]) for j, chip in enumerate(chips)]
        for cp in first:
            cp.start()
        passed = []
        for j, chip in enumerate(chips):
            for a in range(n):
                copy(a, 1 + j, (*chip, c), me).wait_recv()
                cp = copy(a, 4 + j, (*chip, c), sibling)
                cp.start()
                passed.append(cp)
        for a in range(n):
            copy(a, 0, sibling, me).wait_recv()
            for j, chip in enumerate(chips):
                copy(a, 4 + j, (*chip, 1 - c), me).wait_recv()
        for cp in first + passed:
            cp.wait_send()
        for cp in mine:
            cp.wait()

    return pl.pallas_call(
        body, name=name, out_shape=[jax.ShapeDtypeStruct((NDEV, *x.shape), x.dtype) for x in xs],
        in_specs=[hbm] * n, out_specs=[hbm] * n,
        scratch_shapes=[pltpu.SemaphoreType.DMA((n, 7)), pltpu.SemaphoreType.DMA((n, 7)), pltpu.SemaphoreType.DMA((n,))],
    )(*xs)


def _all_to_all(name, xs):
    n = len(xs)
    hbm = pl.BlockSpec(memory_space=pl.ANY)

    def body(*refs):
        x_refs, o_refs = refs[:n], refs[n:2 * n]
        send_sems, recv_sems, local_sems = refs[2 * n:]
        x, y, c = lax.axis_index("x"), lax.axis_index("y"), lax.axis_index("c")
        me = (x, y, c)
        mine = [pltpu.make_async_copy(x_refs[a].at[_dev_index(me)], o_refs[a].at[_dev_index(me)], local_sems.at[a])
                for a in range(n)]
        for cp in mine:
            cp.start()
        copies = []
        for k in range(1, NDEV):
            peer = tuple(1 - v if (k >> b) & 1 else v for v, b in zip(me, (2, 1, 0)))
            for a in range(n):
                copies.append(pltpu.make_async_remote_copy(
                    src_ref=x_refs[a].at[_dev_index(peer)], dst_ref=o_refs[a].at[_dev_index(me)],
                    send_sem=send_sems.at[a, k - 1], recv_sem=recv_sems.at[a, k - 1], device_id=peer, device_id_type=MESH))
        for cp in copies:
            cp.start()
        for cp in copies:
            cp.wait()
        for cp in mine:
            cp.wait()

    return pl.pallas_call(
        body, name=name, out_shape=[jax.ShapeDtypeStruct(x.shape, x.dtype) for x in xs],
        in_specs=[hbm] * n, out_specs=[hbm] * n,
        scratch_shapes=[pltpu.SemaphoreType.DMA((n, 7)), pltpu.SemaphoreType.DMA((n, 7)), pltpu.SemaphoreType.DMA((n,))],
    )(*xs)


def kernel(x, c, positions, norm_w, mod_w, mod_b, attn_w_in, attn_w_out, ssd_w_in, ssd_conv_w, ssd_conv_b, ssd_dt_bias, ssd_a_log, ssd_d, ssd_norm_w, ssd_w_out, final_norm_w, loss_target, m_norm_w, m_mod_w, m_mod_b, m_attn_w_in, m_attn_w_out, m_ssd_w_in, m_ssd_conv_w, m_ssd_conv_b, m_ssd_dt_bias, m_ssd_a_log, m_ssd_d, m_ssd_norm_w, m_ssd_w_out, m_final_norm_w, v_norm_w, v_mod_w, v_mod_b, v_attn_w_in, v_attn_w_out, v_ssd_w_in, v_ssd_conv_w, v_ssd_conv_b, v_ssd_dt_bias, v_ssd_a_log, v_ssd_d, v_ssd_norm_w, v_ssd_w_out, v_final_norm_w):
    s_len, dm = x.shape[1], x.shape[2]
    me = 4 * lax.axis_index("x") + 2 * lax.axis_index("y") + lax.axis_index("c")
    x0 = x.reshape(s_len, dm)
    tgt = loss_target.reshape(s_len, dm)
    aw = 3 * 512
    si = 2 * dm
    sxbc = 2 * si
    n_ssd_in = ssd_w_in.shape[2] * NDEV

    ssd_small = _pack_ssd_small(ssd_conv_w[0], ssd_conv_b, ssd_norm_w)
    g_ai, g_ao, g_si, g_so, g_small, c_all = _all_gather("gather_weights", [
        attn_w_in[0].astype(BF16), attn_w_out[0].astype(BF16), ssd_w_in[0].astype(BF16), ssd_w_out[0].astype(BF16),
        ssd_small, c])
    w_ai = g_ai.transpose(1, 0, 2).reshape(dm, 4 * aw)
    w_ao = g_ao.reshape(aw, dm)
    w_si = g_si.transpose(1, 0, 2).reshape(dm, n_ssd_in)
    w_z, w_xbc, w_dt = w_si[:, :si], w_si[:, si:si + sxbc], w_si[:, si + sxbc:]
    w_so = g_so.reshape(si, dm)
    conv_w = g_small[:, 0:CONV_WIDTH, :].transpose(1, 0, 2).reshape(CONV_WIDTH, sxbc)
    conv_b = g_small[:, 5, :].reshape(1, sxbc)
    snw = g_small[:, 6, 0:si // NDEV].reshape(1, si)
    c_all = c_all.reshape(NDEV, dm)

    part = _mod_part(c_all, mod_w)
    (part_all,) = _all_gather("gather_mod", [part])
    mod_nb = jnp.stack([lax.dynamic_index_in_dim(part_all, i * NDEV + me, axis=1, keepdims=False).reshape(3 * dm)
                        for i in range(2)])
    mod = _mod_finish(mod_nb, mod_b)
    shift = [mod[i:i + 1, 0:dm] for i in range(2)]
    scale = [mod[i:i + 1, dm:2 * dm] for i in range(2)]
    gate = [mod[i:i + 1, 2 * dm:3 * dm] for i in range(2)]
    nw = [norm_w[i:i + 1] for i in range(2)]

    hn0 = _norm_mod_fwd("norm0", x0, nw[0], scale[0], shift[0])
    inv_freq = ROPE_THETA ** (-jnp.arange(0, ROT_DIM, 2, dtype=F32) / ROT_DIM)
    lane = jnp.arange(128) % HEAD_DIM
    inv_row = jnp.where(lane < ROT_DIM, inv_freq[lane % (ROT_DIM // 2)], 0.0).reshape(1, 128).astype(F32)
    tabs = _rope_tables(positions.reshape(s_len, 1), inv_row)
    qk = _matmul("proj_qk", hn0, w_ai, "nn", BF16, 512, 512, dm, epilogue=_rot_fwd, mrows=tabs, n_out=2 * aw)
    v = _matmul("proj_v", hn0, w_ai, "nn", BF16, 512, 512, dm, b_noff=2 * aw, n_out=aw)
    z0 = _matmul("proj_z", hn0, w_ai, "nn", F32, 512, 512, dm, b_noff=3 * aw, n_out=aw)
    att = [_attn_fwd(g, qk, v) for g in range(3)]
    os_, lses = [a[0] for a in att], [a[1] for a in att]
    a0, y0, x1 = _attn_out(os_, lses, z0, x0, gate[0], w_ao)

    hn1 = _norm_mod_fwd("norm1", x1, nw[1], scale[1], shift[1])
    z1 = _matmul("ssd_proj_z", hn1, w_z, "nn", F32, 512, 512, dm)
    xpre = _matmul("ssd_proj_xbc", hn1, w_xbc, "nn", F32, 512, 512, dm)
    dt_raw = _matmul("ssd_proj_dt", hn1, w_dt, "nn", F32, 512, 64, dm)
    xbc = _conv_fwd(xpre, conv_w, conv_b)
    dt_bias = ssd_dt_bias.reshape(1, 2 * SSD_HEADS)
    alog = ssd_a_log.reshape(1, 2 * SSD_HEADS)
    dt = _softplus_fwd(dt_raw, dt_bias)
    y_f, st_f = _ssd_fwd(xbc, dt, alog, 0)
    y_b, st_b = _ssd_fwd(xbc, dt, alog, 1)
    d_e = jnp.repeat(ssd_d.reshape(SSD_HEADS), HEAD_DIM).reshape(1, si)
    u = _gate_norm_fwd(y_f, y_b, xbc, z1, d_e, snw)
    y1 = _matmul("ssd_out", u, w_so, "nn", F32, 512, 512, si)

    fnw = final_norm_w.reshape(1, dm)
    dx2, dy1, g_fnw, dgate1, loss_part = _loss_bwd(x1, y1, tgt, gate[1], fnw)
    du = _matmul("ssd_out_dx", dy1, w_so, "nt", F32, 512, 512, dm)
    gw_so = _matmul("ssd_out_dw", u, dy1, "tn", BF16, 512, 512, 1024)
    dys, dz1, g_snw, g_d = _gate_norm_bwd(du, y_f, y_b, xbc, z1, d_e, snw)
    dxbc_f, ddt_f, dalog_f = _ssd_bwd(xbc, dt, alog, st_f, dys, d_e, 0)
    dxbc_b, ddt_b, dalog_b = _ssd_bwd(xbc, dt, alog, st_b, dys, d_e, 1)
    dpre, g_cw, g_cb = _conv_bwd(xpre, dxbc_f, dxbc_b, conv_w, conv_b)
    ddt_raw, g_dtb = _softplus_bwd(ddt_f, ddt_b, dt_raw, dt_bias)
    dhn1 = [_matmul("ssd_proj_z_dx", dz1, w_z, "nt", F32, 512, 512, 1024),
            _matmul("ssd_proj_xbc_dx", dpre, w_xbc, "nt", F32, 512, 512, 1024),
            _matmul("ssd_proj_dt_dx", ddt_raw, w_dt, "nt", F32, 512, 512, 64)]
    gw_si = jnp.concatenate([_matmul("ssd_proj_z_dw", hn1, dz1, "tn", BF16, 512, 512, 1024),
                             _matmul("ssd_proj_xbc_dw", hn1, dpre, "tn", BF16, 512, 512, 1024),
                             _matmul("ssd_proj_dt_dw", hn1, ddt_raw, "tn", BF16, 512, 64, 1024)], axis=1)
    dx1, g_nw1, dsc1, dsh1 = _norm_mod_bwd("norm1_bwd", x1, dhn1, dx2, nw[1], scale[1], shift[1])

    dy0, dgate0 = _gate_bwd(dx1, y0, gate[0])
    da0 = _matmul("attn_out_dx", dy0, w_ao, "nt", F32, 512, 512, dm)
    gw_ao = _matmul("attn_out_dw", a0, dy0, "tn", BF16, 512, 512, 1024)
    dos, dls, dz0 = _mix_bwd(da0, os_, lses, z0)
    datt = [_attn_bwd(g, qk, v, os_[g], lses[g], dos[g], dls[g]) for g in range(3)]
    dqkv = _rot_pack_bwd([t[0] for t in datt], [t[1] for t in datt], [t[2] for t in datt], tabs)
    dhn0 = [_matmul("proj_qkv_dx", dqkv, w_ai, "nt", F32, 512, 512, 1536, n_out=dm),
            _matmul("proj_z_dx", dz0, w_ai, "nt", F32, 512, 512, 1536, b_koff=3 * aw, n_out=dm)]
    nblk = ssd_w_in.shape[2]
    wcol = attn_w_in.shape[2]
    gw_ai = jnp.concatenate([
        _matmul("proj_qkv_dw", hn0, dqkv, "tn", BF16, 512, wcol // 2, 1024, out_blocks=3 * aw // wcol),
        _matmul("proj_z_dw", hn0, dz0, "tn", BF16, 512, wcol // 2, 1024, out_blocks=aw // wcol)], axis=0)
    dx0, g_nw0, dsc0, dsh0 = _norm_mod_bwd("norm0_bwd", x0, dhn0, dx1, nw[0], scale[0], shift[0])

    rows6 = jnp.concatenate([dsh0, dsc0, dgate0, dsh1, dsc1, dgate1], axis=0)
    small_g = _pack_small(rows6, jnp.concatenate([g_nw0, g_nw1], axis=0), g_fnw, g_dtb, dalog_f + dalog_b, g_d, loss_part)
    (small_all,) = _all_gather("gather_small_grads", [small_g])
    small_w = _pack_small(mod_b.reshape(6, dm), norm_w, fnw, dt_bias, alog, ssd_d)
    small_m = _pack_small(m_mod_b.reshape(6, dm), m_norm_w, m_final_norm_w.reshape(1, dm), m_ssd_dt_bias.reshape(1, 64),
                          m_ssd_a_log.reshape(1, 64), m_ssd_d)
    small_v = _pack_small(v_mod_b.reshape(6, dm), v_norm_w, v_final_norm_w.reshape(1, dm), v_ssd_dt_bias.reshape(1, 64),
                          v_ssd_a_log.reshape(1, 64), v_ssd_d)
    small_out = _adamw("adamw_small", small_w, small_all, small_m, small_v, PACK_ROWS)
    loss = small_out[0][9, 256]
    sg, sd, sm, sv = (_unpack_small(p) for p in small_out)

    ncol = mod_w.shape[2]
    dmod_all = small_all[:, 0:6, :].reshape(NDEV, 2, 3 * dm)
    dmod_sh = lax.dynamic_slice_in_dim(dmod_all, me * ncol, ncol, axis=2).transpose(1, 0, 2)
    g_modw = _mod_grad(c_all, dmod_sh).reshape(1, 2 * dm, ncol)
    modw_out = _adamw("adamw_mod_w", mod_w.reshape(2 * dm, ncol), g_modw, m_mod_w.reshape(2 * dm, ncol),
                      v_mod_w.reshape(2 * dm, ncol), 256)

    ssd_small_g = _pack_ssd_small_blocks(g_cw, g_cb, g_snw)
    r_ai, r_ao, r_si, r_so, r_small = _all_to_all("scatter_grads", [
        gw_ai, gw_ao.reshape(NDEV, aw // NDEV, dm), gw_si.reshape(dm, NDEV, n_ssd_in // NDEV).transpose(1, 0, 2),
        gw_so.reshape(NDEV, si // NDEV, dm), ssd_small_g])
    ai_out = _adamw("adamw_attn_w_in", attn_w_in[0], r_ai, m_attn_w_in[0], v_attn_w_in[0], 256)
    ao_out = _adamw("adamw_attn_w_out", attn_w_out[0], r_ao, m_attn_w_out[0], v_attn_w_out[0], 192)
    si_out = _adamw("adamw_ssd_w_in", ssd_w_in[0], r_si, m_ssd_w_in[0], v_ssd_w_in[0], 256)
    so_out = _adamw("adamw_ssd_w_out", ssd_w_out[0], r_so, m_ssd_w_out[0], v_ssd_w_out[0], 256)
    ssd_small_m = _pack_ssd_small(m_ssd_conv_w[0], m_ssd_conv_b, m_ssd_norm_w)
    ssd_small_v = _pack_ssd_small(v_ssd_conv_w[0], v_ssd_conv_b, v_ssd_norm_w)
    ss_out = _adamw("adamw_ssd_small", ssd_small, r_small, ssd_small_m, ssd_small_v, 8)

    def ssd_small_unpack(p):
        return p[0:5][None], p[5:6], p[6:7, 0:si // NDEV]

    cwo, cbo, nwo = zip(*(ssd_small_unpack(p) for p in ss_out))
    per_kind = []
    for k in range(4):
        s = (sg, sd, sm, sv)[k]
        per_kind.append([
            s[1], modw_out[k].reshape(mod_w.shape), s[0], ai_out[k][None], ao_out[k][None], si_out[k][None],
            cwo[k], cbo[k], s[3], s[4], s[5], nwo[k], so_out[k][None], s[2]])
    return (loss, dx0.reshape(x.shape), *per_kind[0], *per_kind[1], *per_kind[2], *per_kind[3])


def _pack_ssd_small_blocks(g_cw, g_cb, g_nw):
    nper = g_cw.shape[1] // NDEV
    nwper = g_nw.shape[1] // NDEV

    def body(cw_ref, cb_ref, nw_ref, o_ref):
        o_ref[...] = jnp.zeros_like(o_ref)
        for d in range(NDEV):
            o_ref[d, 0:5, :] = cw_ref[:, d * nper:(d + 1) * nper]
            o_ref[d, 5:6, :] = cb_ref[:, d * nper:(d + 1) * nper]
            o_ref[d, 6:7, 0:nwper] = nw_ref[:, d * nwper:(d + 1) * nwper]

    return pl.pallas_call(body, name="pack_ssd_small_grads", out_shape=jax.ShapeDtypeStruct((NDEV, 8, nper), F32))(g_cw, g_cb, g_nw)
```

```python
import functools
import math

import jax
import jax.numpy as jnp
from jax import lax
from jax.experimental import pallas as pl
from jax.experimental.pallas import tpu as pltpu

F32 = jnp.float32
BF16 = jnp.bfloat16
HI = lax.Precision.HIGHEST
MESH = pl.DeviceIdType.MESH
NDEV = 8

NORM_EPS = 1e-6
ROPE_THETA = 500000.0
ROT_DIM = 16
HEAD_DIM = 64
DILATIONS = (1, 4, 16)
BAND = 64
NEG_BIG = -1e30
CHUNK = 128
SSD_HEADS = 32
SSD_GROUPS = 8
CONV_WIDTH = 5

ADAM_LR = 0.001
ADAM_B1 = 0.9
ADAM_B2 = 0.999
ADAM_EPS = 1e-08
ADAM_WD = 0.01
ADAM_STEP = 10

VMEM_BIG = 56 * 1024 * 1024
MM_T = 1024


def _params(sem=None, vmem=None):
    kw = {}
    if sem is not None:
        kw["dimension_semantics"] = sem
    if vmem is not None:
        kw["vmem_limit_bytes"] = vmem
    return pltpu.CompilerParams(**kw)


def _dg(a, b, ca, cb, prec=None):
    return lax.dot_general(a, b, (((ca,), (cb,)), ((), ())), preferred_element_type=F32, precision=prec)


def _nn(a, b):
    return _dg(a.astype(BF16), b.astype(BF16), 1, 0)


def _nt(a, b):
    return _dg(a.astype(BF16), b.astype(BF16), 1, 1)


def _tn(a, b):
    return _dg(a.astype(BF16), b.astype(BF16), 0, 0)


def _hnn(a, b):
    return _dg(a, b, 1, 0, HI)


@jax.custom_vjp
def _bnn(a, b):
    return _nn(a, b)


_bnn.defvjp(lambda a, b: (_nn(a, b), (a, b)), lambda r, g: (_nt(g, r[1]), _tn(r[0], g)))


@jax.custom_vjp
def _bnt(a, b):
    return _nt(a, b)


_bnt.defvjp(lambda a, b: (_nt(a, b), (a, b)), lambda r, g: (_nn(g, r[1]), _tn(g, r[0])))


@jax.custom_vjp
def _btn(a, b):
    return _tn(a, b)


_btn.defvjp(lambda a, b: (_tn(a, b), (a, b)), lambda r, g: (_nt(r[1], g), _nn(r[0], g)))


def _silu(x):
    return x * jax.nn.sigmoid(x)


def _matmul(name, a, b, mode, out_dtype, tm, tn, tk, *, epilogue=None, tiled=(), mrows=(), ncols=(),
            b_noff=0, b_koff=0, n_out=None, out_blocks=None):
    if mode == "tn":
        K, M = a.shape
    else:
        M, K = a.shape
    N = n_out if n_out is not None else (b.shape[0] if mode == "nt" else b.shape[1])
    tm, tn, tk = min(tm, M), min(tn, N), min(tk, K)
    assert M % tm == 0 and N % tn == 0 and K % tk == 0, (name, M, N, K, tm, tn, tk)
    assert b_noff % tn == 0 and b_koff % tk == 0
    no, ko = b_noff // tn, b_koff // tk
    nk = K // tk
    if mode == "tn":
        a_spec = pl.BlockSpec((tk, tm), lambda i, j, k: (k, i))
    else:
        a_spec = pl.BlockSpec((tm, tk), lambda i, j, k: (i, k))
    if mode == "nt":
        b_spec = pl.BlockSpec((tn, tk), lambda i, j, k: (j + no, k + ko))
    else:
        b_spec = pl.BlockSpec((tk, tn), lambda i, j, k: (k + ko, j + no))
    specs = [a_spec, b_spec]
    specs += [pl.BlockSpec((tm, tn), lambda i, j, k: (i, j)) for _ in tiled]
    specs += [pl.BlockSpec((tm, r.shape[1]), lambda i, j, k: (i, 0)) for r in mrows]
    specs += [pl.BlockSpec((1, tn), lambda i, j, k: (0, j)) for _ in ncols]
    if out_blocks is None:
        out_shape = jax.ShapeDtypeStruct((M, N), out_dtype)
        out_spec = pl.BlockSpec((tm, tn), lambda i, j, k: (i, j))
    else:
        nper = N // out_blocks
        assert nper % tn == 0
        jb = nper // tn
        out_shape = jax.ShapeDtypeStruct((out_blocks, M, nper), out_dtype)
        out_spec = pl.BlockSpec((None, tm, tn), lambda i, j, k: (j // jb, i, j % jb))
    ne = len(tiled) + len(mrows) + len(ncols)
    dot = {"nn": _nn, "nt": _nt, "tn": _tn}[mode]

    def body(a_ref, b_ref, *rest):
        extras, o_ref = rest[:ne], rest[ne]

        def finish(acc):
            if epilogue is not None:
                acc = epilogue(acc, *[e[...] for e in extras])
            o_ref[...] = acc.astype(o_ref.dtype)

        if nk == 1:
            finish(dot(a_ref[...], b_ref[...]))
        else:
            acc_ref = rest[ne + 1]
            k = pl.program_id(2)

            @pl.when(k == 0)
            def _():
                acc_ref[...] = jnp.zeros_like(acc_ref)

            acc_ref[...] += dot(a_ref[...], b_ref[...])

            @pl.when(k == nk - 1)
            def _():
                finish(acc_ref[...])

    return pl.pallas_call(
        body, name=name, out_shape=out_shape, grid=(M // tm, N // tn, nk),
        in_specs=specs, out_specs=out_spec,
        scratch_shapes=[] if nk == 1 else [pltpu.VMEM((tm, tn), F32)],
        compiler_params=_params(("parallel", "parallel", "arbitrary"), VMEM_BIG),
    )(a, b, *tiled, *mrows, *ncols)


def _rowwise(name, fn, tiled, consts, outs, accs, ts):
    tl = [(t, t.shape[1], 0) if not isinstance(t, tuple) else t for t in tiled]
    s_len = tl[0][0].shape[0]
    assert s_len % ts == 0
    nt_, nc_, no_ = len(tl), len(consts), len(outs)

    def body(*refs):
        t_refs, c_refs = refs[:nt_], refs[nt_:nt_ + nc_]
        o_refs, a_refs = refs[nt_ + nc_:nt_ + nc_ + no_], refs[nt_ + nc_ + no_:]
        res_o, res_a = fn(*[r[...] for r in t_refs], *[r[...] for r in c_refs])
        for r, v in zip(o_refs, res_o, strict=True):
            r[...] = v.astype(r.dtype)
        if a_refs:
            @pl.when(pl.program_id(0) == 0)
            def _():
                for r in a_refs:
                    r[...] = jnp.zeros_like(r)

            for r, v in zip(a_refs, res_a, strict=True):
                r[...] += v

    in_specs = [pl.BlockSpec((ts, w), functools.partial(lambda i, cb: (i, cb), cb=cb)) for (_, w, cb) in tl]
    in_specs += [pl.BlockSpec(c.shape, lambda i: (0, 0)) for c in consts]
    out_specs = [pl.BlockSpec((ts, c), lambda i: (i, 0)) for (c, _) in outs]
    out_specs += [pl.BlockSpec(shp, lambda i: (0, 0)) for shp in accs]
    out_shape = [jax.ShapeDtypeStruct((s_len, c), dt) for (c, dt) in outs]
    out_shape += [jax.ShapeDtypeStruct(shp, F32) for shp in accs]
    res = pl.pallas_call(
        body, name=name, out_shape=out_shape, grid=(s_len // ts,), in_specs=in_specs, out_specs=out_specs,
        compiler_params=_params(("arbitrary",) if accs else ("parallel",), VMEM_BIG),
    )(*[t[0] for t in tl], *consts)
    return res[:no_], res[no_:]


def _norm_mod_fn(x, nw, sc, sh):
    r = lax.rsqrt(jnp.mean(x * x, axis=-1, keepdims=True) + NORM_EPS)
    return (x * r * nw) * (1.0 + sc) + sh


def _norm_mod_fwd(name, x, nw, sc, sh):
    (hn,), _ = _rowwise(name, lambda x, nw, sc, sh: ([_norm_mod_fn(x, nw, sc, sh)], []),
                        [x], [nw, sc, sh], [(x.shape[1], BF16)], [], 512)
    return hn


def _norm_mod_bwd(name, x, dhn_parts, dres, nw, sc, sh):
    n = len(dhn_parts)
    d = x.shape[1]

    def fn(x, *rest):
        dhn = rest[0]
        for p in rest[1:n]:
            dhn = dhn + p
        dres, nw, sc, sh = rest[n:]
        _, vjp = jax.vjp(_norm_mod_fn, x, nw, sc, sh)
        dx, dnw, dsc, dsh = vjp(dhn)
        return [dx + dres], [dnw, dsc, dsh]

    (dx,), (g_nw, dsc, dsh) = _rowwise(name, fn, [x, *dhn_parts, dres], [nw, sc, sh], [(d, F32)],
                                       [(1, d), (1, d), (1, d)], 256)
    return dx, g_nw, dsc, dsh


def _rope_tables(pos_col, inv_row):
    def fn(pos, inv):
        ang = pos.astype(F32) * inv
        e = lax.broadcasted_iota(jnp.int32, (1, 128), 1) % HEAD_DIM
        cos, sin = jnp.cos(ang), jnp.sin(ang)
        half = ROT_DIM // 2
        return [jnp.where(e < ROT_DIM, cos, 1.0), jnp.where(e < half, -sin, 0.0),
                jnp.where((e >= half) & (e < ROT_DIM), sin, 0.0)], []

    (c, sa, sb), _ = _rowwise("rope_tables", fn, [pos_col], [inv_row], [(128, F32)] * 3, [], 512)
    return c, sa, sb


def _rot_fwd(t, c, sa, sb):
    n = t.shape[1]
    rep = n // 128
    c, sa, sb = (jnp.tile(u, (1, rep)) for u in (c, sa, sb))
    return t * c + pltpu.roll(t, n - ROT_DIM // 2, 1) * sa + pltpu.roll(t, ROT_DIM // 2, 1) * sb


def _rot_bwd(g, c, sa, sb):
    n = g.shape[1]
    rep = n // 128
    c, sa, sb = (jnp.tile(u, (1, rep)) for u in (c, sa, sb))
    return g * c + pltpu.roll(g * sa, ROT_DIM // 2, 1) + pltpu.roll(g * sb, n - ROT_DIM // 2, 1)


ATT_TQ = 128
ATT_TK = ATT_TQ + 2 * BAND


def _attn_specs(g, d, l):
    q_spec = pl.BlockSpec((l, 128), lambda r, hp: (0, r * 24 + 4 * g + hp))
    k_spec = pl.BlockSpec((l, 128), lambda r, hp: (0, r * 24 + 12 + 4 * g + hp))
    v_spec = pl.BlockSpec((l, 128), lambda r, hp: (0, r * 12 + 4 * g + hp))
    o_spec = pl.BlockSpec((l, 128), lambda r, hp: (0, r * 4 + hp))
    return q_spec, k_spec, v_spec, o_spec


def _attn_tile_geometry(t, l):
    q0 = pl.multiple_of(t * ATT_TQ, ATT_TQ)
    ws = pl.multiple_of(jnp.clip(t * ATT_TQ - BAND, 0, l - ATT_TK), BAND)
    qpos = q0 + lax.broadcasted_iota(jnp.int32, (ATT_TQ, 1), 0)
    kpos = ws + lax.broadcasted_iota(jnp.int32, (1, ATT_TK), 1)
    valid = jnp.abs(kpos - qpos) <= BAND
    return q0, ws, valid


def _attn_fwd(g, qk, v):
    s_len = qk.shape[0]
    d = DILATIONS[g]
    l = s_len // d
    assert l % ATT_TQ == 0 and l >= ATT_TK
    q_spec, k_spec, v_spec, o_spec = _attn_specs(g, d, l)
    scale = 1.0 / math.sqrt(HEAD_DIM)

    def body(q_ref, k_ref, v_ref, o_ref, lse_ref):
        lane = lax.broadcasted_iota(jnp.int32, (1, 128), 1)
        in_h = [lane < HEAD_DIM, lane >= HEAD_DIM]

        def tile(t, carry):
            q0, ws, valid = _attn_tile_geometry(t, l)
            q = q_ref[pl.ds(q0, ATT_TQ), :]
            k = k_ref[pl.ds(ws, ATT_TK), :]
            vv = v_ref[pl.ds(ws, ATT_TK), :]
            outs, lses = [], []
            for h in range(2):
                qm = jnp.where(in_h[h], q, jnp.zeros_like(q))
                s = jnp.where(valid, _nt(qm, k) * scale, NEG_BIG)
                m = jnp.max(s, axis=1, keepdims=True)
                p = jnp.exp(s - m)
                den = jnp.sum(p, axis=1, keepdims=True)
                outs.append(_nn(p, vv) / den)
                lses.append(m + jnp.log(den))
            o_ref[pl.ds(q0, ATT_TQ), :] = jnp.where(in_h[0], outs[0], outs[1])
            lse_ref[pl.ds(q0, ATT_TQ), :] = jnp.where(in_h[0], lses[0], lses[1])
            return carry

        lax.fori_loop(0, l // ATT_TQ, tile, 0, unroll=2)

    o, lse = pl.pallas_call(
        body, name=f"attn_fwd_g{g}", grid=(d, 4),
        out_shape=[jax.ShapeDtypeStruct((l, d * 512), F32)] * 2,
        in_specs=[q_spec, k_spec, v_spec], out_specs=[o_spec, o_spec],
        compiler_params=_params(("parallel", "parallel"), VMEM_BIG),
    )(qk.reshape(l, d * 3072), qk.reshape(l, d * 3072), v.reshape(l, d * 1536))
    return o.reshape(s_len, 512), lse.reshape(s_len, 512)


def _attn_bwd(g, qk, v, o, lse, do, dlse):
    s_len = qk.shape[0]
    d = DILATIONS[g]
    l = s_len // d
    q_spec, k_spec, v_spec, o_spec = _attn_specs(g, d, l)
    scale = 1.0 / math.sqrt(HEAD_DIM)

    def body(q_ref, k_ref, v_ref, o_ref, lse_ref, do_ref, dlse_ref, dq_ref, dk_ref, dv_ref):
        lane = lax.broadcasted_iota(jnp.int32, (1, 128), 1)
        in_h = [lane < HEAD_DIM, lane >= HEAD_DIM]
        dk_ref[...] = jnp.zeros_like(dk_ref)
        dv_ref[...] = jnp.zeros_like(dv_ref)

        def tile(t, carry):
            q0, ws, valid = _attn_tile_geometry(t, l)
            rows = pl.ds(q0, ATT_TQ)
            win = pl.ds(ws, ATT_TK)
            q, k, vv = q_ref[rows, :], k_ref[win, :], v_ref[win, :]
            dout, lse_t, dlse_t = do_ref[rows, :], lse_ref[rows, :], dlse_ref[rows, :]
            od = dout * o_ref[rows, :]
            dqs, dks, dvs = [], [], []
            for h in range(2):
                c0 = h * HEAD_DIM
                qm = jnp.where(in_h[h], q, jnp.zeros_like(q))
                s = jnp.where(valid, _nt(qm, k) * scale, NEG_BIG)
                p = jnp.exp(s - lse_t[:, c0:c0 + 1])
                dom = jnp.where(in_h[h], dout, 0.0)
                dp = _nt(dom, vv)
                delta = jnp.sum(jnp.where(in_h[h], od, 0.0), axis=1, keepdims=True)
                ds = (p * (dp - delta + dlse_t[:, c0:c0 + 1]) * scale).astype(BF16)
                dqs.append(_nn(ds, k))
                dks.append(_tn(ds, q))
                dvs.append(_tn(p, dout))
            dq_ref[rows, :] = jnp.where(in_h[0], dqs[0], dqs[1])
            dk_ref[win, :] += jnp.where(in_h[0], dks[0], dks[1])
            dv_ref[win, :] += jnp.where(in_h[0], dvs[0], dvs[1])
            return carry

        lax.fori_loop(0, l // ATT_TQ, tile, 0, unroll=2)

    view = lambda a: a.reshape(l, d * 512)
    dq, dk, dv = pl.pallas_call(
        body, name=f"attn_bwd_g{g}", grid=(d, 4),
        out_shape=[jax.ShapeDtypeStruct((l, d * 512), F32)] * 3,
        in_specs=[q_spec, k_spec, v_spec, o_spec, o_spec, o_spec, o_spec], out_specs=[o_spec] * 3,
        compiler_params=_params(("parallel", "parallel"), VMEM_BIG),
    )(qk.reshape(l, d * 3072), qk.reshape(l, d * 3072), v.reshape(l, d * 1536), view(o), view(lse), view(do), view(dlse))
    return dq.reshape(s_len, 512), dk.reshape(s_len, 512), dv.reshape(s_len, 512)


def _mix_weights(ls):
    mx = jnp.maximum(jnp.maximum(ls[0], ls[1]), ls[2])
    es = [jnp.exp(x - mx) for x in ls]
    tot = es[0] + es[1] + es[2]
    return [e / tot for e in es]


def _attn_out(os_, lses, z, x, gate, w_out):
    s_len, dm = x.shape
    tm = 256
    wdt = 512

    def body(o0, o1, o2, l0, l1, l2, z_ref, x_ref, g_ref, w_ref, a_ref, y_ref, x1_ref):
        alphas = _mix_weights([l0[...], l1[...], l2[...]])
        y = jnp.zeros((tm, dm), F32)
        for g, o_ref in enumerate((o0, o1, o2)):
            a_g = (o_ref[...] * alphas[g] * _silu(z_ref[:, g * wdt:(g + 1) * wdt])).astype(BF16)
            a_ref[:, g * wdt:(g + 1) * wdt] = a_g
            y = y + _nn(a_g, w_ref[g * wdt:(g + 1) * wdt, :])
        y_ref[...] = y
        x1_ref[...] = x_ref[...] + g_ref[...] * y

    row = lambda c: pl.BlockSpec((tm, c), lambda i: (i, 0))
    return pl.pallas_call(
        body, name="attn_out", grid=(s_len // tm,),
        out_shape=[jax.ShapeDtypeStruct((s_len, 3 * wdt), BF16), jax.ShapeDtypeStruct((s_len, dm), F32),
                   jax.ShapeDtypeStruct((s_len, dm), F32)],
        in_specs=[row(wdt)] * 6 + [row(3 * wdt), row(dm), pl.BlockSpec((1, dm), lambda i: (0, 0)),
                                   pl.BlockSpec(w_out.shape, lambda i: (0, 0))],
        out_specs=[row(3 * wdt), row(dm), row(dm)],
        compiler_params=_params(("parallel",), VMEM_BIG),
    )(*os_, *lses, z, x, gate, w_out)


def _mix_bwd(da, os_, lses, z):
    wdt = 512

    def fn(da, o0, o1, o2, l0, l1, l2, z):
        os_t, ls = [o0, o1, o2], [l0, l1, l2]
        alphas = _mix_weights(ls)
        hi = lax.broadcasted_iota(jnp.int32, (wdt, wdt), 0) // HEAD_DIM
        hj = lax.broadcasted_iota(jnp.int32, (wdt, wdt), 1) // HEAD_DIM
        seg = (hi == hj).astype(F32)
        dos, dal, dzs = [], [], []
        for g in range(3):
            zg = z[:, g * wdt:(g + 1) * wdt]
            sig = jax.nn.sigmoid(zg)
            dag = da[:, g * wdt:(g + 1) * wdt]
            dmix = dag * zg * sig
            dzs.append(dag * os_t[g] * alphas[g] * (sig * (1.0 + zg * (1.0 - sig))))
            dos.append(dmix * alphas[g])
            dal.append(_hnn(dmix * os_t[g], seg))
        mean = alphas[0] * dal[0] + alphas[1] * dal[1] + alphas[2] * dal[2]
        dls = [alphas[g] * (dal[g] - mean) for g in range(3)]
        return dos + dls + [jnp.concatenate(dzs, axis=1)], []

    outs, _ = _rowwise("mix_bwd", fn, [da, *os_, *lses, z], [], [(wdt, F32)] * 6 + [(3 * wdt, BF16)], [], 256)
    return outs[:3], outs[3:6], outs[6]


def _rot_pack_bwd(dqs, dks, dvs, tabs):
    wdt = 512

    def fn(*args):
        grads, (c, sa, sb) = args[:9], args[9:]
        cols = [_rot_bwd(gq, c, sa, sb) for gq in grads[:6]] + list(grads[6:])
        return [jnp.concatenate(cols, axis=1)], []

    (out,), _ = _rowwise("rot_pack_bwd", fn, [*dqs, *dks, *dvs, *tabs], [], [(9 * wdt, BF16)], [], 256)
    return out


CONV_CB = 256
CONV_R = 128
CONV_PAD = 8


def _conv_window_sum(win, w, off, sign):
    acc = None
    for j in range(CONV_WIDTH):
        o = off + sign * j
        term = win[o:o + CONV_R, :] * w[j:j + 1, :]
        acc = term if acc is None else acc + term
    return acc


def _conv_fwd(xpre, cw, cb):
    s_len, ch = xpre.shape
    nchunk = s_len // CONV_R

    def body(x_ref, w_ref, b_ref, o_ref, xp):
        zero = jnp.zeros((CONV_PAD, CONV_CB), F32)
        xp[0:CONV_PAD, :] = zero
        xp[s_len + CONV_PAD:s_len + 2 * CONV_PAD, :] = zero

        def fill(ci, carry):
            base = pl.multiple_of(ci * CONV_R, CONV_R)
            xp[pl.ds(base + CONV_PAD, CONV_R), :] = x_ref[pl.ds(base, CONV_R), :]
            return carry

        lax.fori_loop(0, nchunk, fill, 0)
        w = w_ref[...]
        b = b_ref[...]

        def chunk(ci, carry):
            base = pl.multiple_of(ci * CONV_R, CONV_R)
            win = xp[pl.ds(base, CONV_R + 2 * CONV_PAD), :]
            u = _conv_window_sum(win, w, CONV_PAD - CONV_WIDTH // 2, 1) + b
            o_ref[pl.ds(base, CONV_R), :] = _silu(u)
            return carry

        lax.fori_loop(0, nchunk, chunk, 0)

    col = lambda r: pl.BlockSpec((r, CONV_CB), lambda j: (0, j))
    return pl.pallas_call(
        body, name="conv_fwd", grid=(ch // CONV_CB,), out_shape=jax.ShapeDtypeStruct((s_len, ch), F32),
        in_specs=[col(s_len), col(CONV_WIDTH), col(1)], out_specs=col(s_len),
        scratch_shapes=[pltpu.VMEM((s_len + 2 * CONV_PAD, CONV_CB), F32)],
        compiler_params=_params(("parallel",), VMEM_BIG),
    )(xpre, cw, cb)


def _conv_bwd(xpre, da, db, cw, cb):
    s_len, ch = xpre.shape
    nchunk = s_len // CONV_R
    half = CONV_WIDTH // 2

    def body(x_ref, da_ref, db_ref, w_ref, b_ref, dx_ref, gw_ref, gb_ref, xp, dcp):
        zero = jnp.zeros((CONV_PAD, CONV_CB), F32)
        for buf in (xp, dcp):
            buf[0:CONV_PAD, :] = zero
            buf[s_len + CONV_PAD:s_len + 2 * CONV_PAD, :] = zero

        def fill(ci, carry):
            base = pl.multiple_of(ci * CONV_R, CONV_R)
            xp[pl.ds(base + CONV_PAD, CONV_R), :] = x_ref[pl.ds(base, CONV_R), :]
            return carry

        lax.fori_loop(0, nchunk, fill, 0)
        w = w_ref[...]
        b = b_ref[...]

        def first(ci, carry):
            base = pl.multiple_of(ci * CONV_R, CONV_R)
            win = xp[pl.ds(base, CONV_R + 2 * CONV_PAD), :]
            u = _conv_window_sum(win, w, CONV_PAD - half, 1) + b
            sig = jax.nn.sigmoid(u)
            dc = (da_ref[pl.ds(base, CONV_R), :] + db_ref[pl.ds(base, CONV_R), :]) * (sig * (1.0 + u * (1.0 - sig)))
            dcp[pl.ds(base + CONV_PAD, CONV_R), :] = dc
            gb = carry[0] + jnp.sum(dc, axis=0, keepdims=True)
            gws = [carry[1 + j] + jnp.sum(dc * win[CONV_PAD - half + j:CONV_PAD - half + j + CONV_R, :],
                                          axis=0, keepdims=True) for j in range(CONV_WIDTH)]
            return (gb, *gws)

        z1 = jnp.zeros((1, CONV_CB), F32)
        sums = lax.fori_loop(0, nchunk, first, (z1,) * (1 + CONV_WIDTH))
        gb_ref[...] = sums[0]
        for j in range(CONV_WIDTH):
            gw_ref[j:j + 1, :] = sums[1 + j]

        def second(ci, carry):
            base = pl.multiple_of(ci * CONV_R, CONV_R)
            win = dcp[pl.ds(base, CONV_R + 2 * CONV_PAD), :]
            dx_ref[pl.ds(base, CONV_R), :] = _conv_window_sum(win, w, CONV_PAD + half, -1).astype(dx_ref.dtype)
            return carry

        lax.fori_loop(0, nchunk, second, 0)

    col = lambda r: pl.BlockSpec((r, CONV_CB), lambda j: (0, j))
    return pl.pallas_call(
        body, name="conv_bwd", grid=(ch // CONV_CB,),
        out_shape=[jax.ShapeDtypeStruct((s_len, ch), BF16), jax.ShapeDtypeStruct((CONV_WIDTH, ch), F32),
                   jax.ShapeDtypeStruct((1, ch), F32)],
        in_specs=[col(s_len), col(s_len), col(s_len), col(CONV_WIDTH), col(1)],
        out_specs=[col(s_len), col(CONV_WIDTH), col(1)],
        scratch_shapes=[pltpu.VMEM((s_len + 2 * CONV_PAD, CONV_CB), F32)] * 2,
        compiler_params=_params(("parallel",), VMEM_BIG),
    )(xpre, da, db, cw, cb)


SSD_GW = 256
SSD_N = 128


@jax.custom_vjp
def _expand(x, e):
    x1 = x.astype(BF16)
    r1 = x - x1.astype(F32)
    x2 = r1.astype(BF16)
    x3 = (r1 - x2.astype(F32)).astype(BF16)
    eb = e.astype(BF16)
    return _dg(x1, eb, 1, 0) + _dg(x2, eb, 1, 0) + _dg(x3, eb, 1, 0)


def _expand_fwd(x, e):
    return _expand(x, e), e


def _expand_bwd(e, g):
    g1 = g.astype(BF16)
    g2 = (g - g1.astype(F32)).astype(BF16)
    eb = e.astype(BF16)
    return _dg(g1, eb, 1, 1) + _dg(g2, eb, 1, 1), jnp.zeros_like(e)


_expand.defvjp(_expand_fwd, _expand_bwd)


def _ssd_mask(dirn):
    ri = lax.broadcasted_iota(jnp.int32, (CHUNK, CHUNK), 0)
    cj = lax.broadcasted_iota(jnp.int32, (CHUNK, CHUNK), 1)
    return (cj <= ri) if dirn == 0 else (cj >= ri)


def _ssd_rowsel(dirn):
    last = CHUNK - 1 if dirn == 0 else 0
    return (lax.broadcasted_iota(jnp.int32, (CHUNK, 1), 0) == last).astype(F32)


def _ssd_chunk_pre(dirn):
    nh = 2 * SSD_HEADS

    def f(dt, alog):
        da = dt * (-jnp.exp(alog))
        cum = _hnn(_ssd_mask(dirn).astype(F32), da)
        tot = jnp.sum(cum * _ssd_rowsel(dirn), axis=0, keepdims=True)
        hh = lax.broadcasted_iota(jnp.int32, (nh, SSD_HEADS * HEAD_DIM), 0)
        jj = lax.broadcasted_iota(jnp.int32, (nh, SSD_HEADS * HEAD_DIM), 1)
        expand = (hh == dirn * SSD_HEADS + jj // HEAD_DIM).astype(F32)
        return cum, cum.T, _expand(dt, expand), _expand(jnp.exp(tot - cum), expand), _expand(jnp.exp(cum), expand)

    return f


def _ssd_group_fn(g, dirn):
    nh = 2 * SSD_HEADS

    def f(xs, bm, cm, st, cum, cum_t, dt_e, w_e, ce_e):
        mask = _ssd_mask(dirn)
        xdt = xs * dt_e
        cd_e = jnp.sum(ce_e * _ssd_rowsel(dirn), axis=0, keepdims=True)
        cb = _bnt(cm, bm)
        y = _bnn(cm, st) * ce_e
        lane_head = lax.broadcasted_iota(jnp.int32, (1, SSD_GW), 1) // HEAD_DIM
        for j in range(4):
            hidx = dirn * SSD_HEADS + 4 * g + j
            col = jnp.sum(cum * (lax.broadcasted_iota(jnp.int32, (1, nh), 1) == hidx).astype(F32), axis=1, keepdims=True)
            row = jnp.sum(cum_t * (lax.broadcasted_iota(jnp.int32, (nh, 1), 0) == hidx).astype(F32), axis=0, keepdims=True)
            dec = jnp.exp(jnp.where(mask, col - row, NEG_BIG))
            y = y + _bnn(cb * dec, xdt) * (lane_head == j).astype(F32)
        st_out = st * cd_e + _btn(bm, xdt * w_e)
        return y, st_out

    return f


def _ssd_in_specs(kk):
    ln = CHUNK
    return [pl.BlockSpec((ln, 2048), lambda i: (kk(i), 0)),
            pl.BlockSpec((ln, 1024), lambda i: (kk(i), 2)),
            pl.BlockSpec((ln, 1024), lambda i: (kk(i), 3)),
            pl.BlockSpec((ln, 2 * SSD_HEADS), lambda i: (kk(i), 0)),
            pl.BlockSpec((1, 2 * SSD_HEADS), lambda i: (0, 0))]


def _ssd_fwd(xbc, dt, alog, dirn):
    s_len = xbc.shape[0]
    nc = s_len // CHUNK
    kk = (lambda i: i) if dirn == 0 else (lambda i: nc - 1 - i)

    def body(x_ref, b_ref, c_ref, dt_ref, al_ref, y_ref, sts_ref, st):
        @pl.when(pl.program_id(0) == 0)
        def _():
            st[...] = jnp.zeros_like(st)

        sts_ref[0] = st[...]
        cum, cum_t, dt_e, w_e, ce_e = _ssd_chunk_pre(dirn)(dt_ref[...], al_ref[...])
        for g in range(SSD_GROUPS):
            xc = slice(g * SSD_GW, (g + 1) * SSD_GW)
            gc = slice(g * SSD_N, (g + 1) * SSD_N)
            y, st_new = _ssd_group_fn(g, dirn)(x_ref[:, xc], b_ref[:, gc], c_ref[:, gc], st[:, xc], cum, cum_t,
                                               dt_e[:, xc], w_e[:, xc], ce_e[:, xc])
            y_ref[:, xc] = y
            st[:, xc] = st_new

    return pl.pallas_call(
        body, name=f"ssd_fwd_d{dirn}", grid=(nc,),
        out_shape=[jax.ShapeDtypeStruct((s_len, 2048), F32), jax.ShapeDtypeStruct((nc, SSD_N, 2048), F32)],
        in_specs=_ssd_in_specs(kk),
        out_specs=[pl.BlockSpec((CHUNK, 2048), lambda i: (kk(i), 0)),
                   pl.BlockSpec((1, SSD_N, 2048), lambda i: (kk(i), 0, 0))],
        scratch_shapes=[pltpu.VMEM((SSD_N, 2048), F32)],
        compiler_params=_params(("arbitrary",), VMEM_BIG),
    )(xbc, xbc, xbc, dt, alog)


def _ssd_bwd(xbc, dt, alog, states, dy, d_e, dirn):
    s_len = xbc.shape[0]
    nc = s_len // CHUNK
    kk = (lambda i: nc - 1 - i) if dirn == 0 else (lambda i: i)

    def body(x_ref, b_ref, c_ref, dt_ref, al_ref, sts_ref, dy_ref, de_ref, dx_ref, ddt_ref, dal_ref, dst):
        @pl.when(pl.program_id(0) == 0)
        def _():
            dst[...] = jnp.zeros_like(dst)
            dal_ref[...] = jnp.zeros_like(dal_ref)

        (cum, cum_t, dt_e, w_e, ce_e), pre_vjp = jax.vjp(_ssd_chunk_pre(dirn), dt_ref[...], al_ref[...])
        dcum = jnp.zeros_like(cum)
        dcum_t = jnp.zeros_like(cum_t)
        d_dt_e, d_w_e, d_ce_e = [], [], []
        for g in range(SSD_GROUPS):
            xc = slice(g * SSD_GW, (g + 1) * SSD_GW)
            gc = slice(g * SSD_N, (g + 1) * SSD_N)
            _, vjp = jax.vjp(_ssd_group_fn(g, dirn), x_ref[:, xc], b_ref[:, gc], c_ref[:, gc], sts_ref[0, :, xc], cum, cum_t,
                             dt_e[:, xc], w_e[:, xc], ce_e[:, xc])
            dyg = dy_ref[:, xc]
            dxs, dbm, dcm, dst_g, dcum_g, dcum_t_g, ddte_g, dwe_g, dcee_g = vjp((dyg, dst[:, xc]))
            if dirn == 0:
                dxs = dxs + dyg * de_ref[:, xc]
            dx_ref[:, xc] = dxs
            dx_ref[:, 2048 + g * SSD_N:2048 + (g + 1) * SSD_N] = dbm
            dx_ref[:, 3072 + g * SSD_N:3072 + (g + 1) * SSD_N] = dcm
            dst[:, xc] = dst_g
            dcum = dcum + dcum_g
            dcum_t = dcum_t + dcum_t_g
            d_dt_e.append(ddte_g)
            d_w_e.append(dwe_g)
            d_ce_e.append(dcee_g)
        ddt, dal = pre_vjp((dcum, dcum_t, jnp.concatenate(d_dt_e, axis=1), jnp.concatenate(d_w_e, axis=1),
                            jnp.concatenate(d_ce_e, axis=1)))
        ddt_ref[...] = ddt
        dal_ref[...] += dal

    return pl.pallas_call(
        body, name=f"ssd_bwd_d{dirn}", grid=(nc,),
        out_shape=[jax.ShapeDtypeStruct((s_len, 4096), F32), jax.ShapeDtypeStruct((s_len, 2 * SSD_HEADS), F32),
                   jax.ShapeDtypeStruct((1, 2 * SSD_HEADS), F32)],
        in_specs=_ssd_in_specs(kk) + [pl.BlockSpec((1, SSD_N, 2048), lambda i: (kk(i), 0, 0)),
                                      pl.BlockSpec((CHUNK, 2048), lambda i: (kk(i), 0)),
                                      pl.BlockSpec((1, 2048), lambda i: (0, 0))],
        out_specs=[pl.BlockSpec((CHUNK, 4096), lambda i: (kk(i), 0)),
                   pl.BlockSpec((CHUNK, 2 * SSD_HEADS), lambda i: (kk(i), 0)),
                   pl.BlockSpec((1, 2 * SSD_HEADS), lambda i: (0, 0))],
        scratch_shapes=[pltpu.VMEM((SSD_N, 2048), F32)],
        compiler_params=_params(("arbitrary",), VMEM_BIG),
    )(xbc, xbc, xbc, dt, alog, states, dy, d_e)


def _gate_norm_fn(yf, yb, xs, z, d_e, nw):
    yg = (yf + yb + xs * d_e) * _silu(z)
    return yg * lax.rsqrt(jnp.mean(yg * yg, axis=-1, keepdims=True) + NORM_EPS) * nw


def _gate_norm_fwd(yf, yb, xbc, z, d_e, nw):
    (u,), _ = _rowwise("ssd_gate_norm", lambda *a: ([_gate_norm_fn(*a)], []),
                       [yf, yb, (xbc, 2048, 0), z], [d_e, nw], [(2048, BF16)], [], 256)
    return u


def _gate_norm_bwd(du, yf, yb, xbc, z, d_e, nw):
    def fn(du, yf, yb, xs, z, d_e, nw):
        _, vjp = jax.vjp(_gate_norm_fn, yf, yb, xs, z, d_e, nw)
        dyf, _, _, dz, dde, dnw = vjp(du)
        hh = lax.broadcasted_iota(jnp.int32, (2048, SSD_HEADS), 0) // HEAD_DIM
        jj = lax.broadcasted_iota(jnp.int32, (2048, SSD_HEADS), 1)
        return [dyf, dz], [dnw, _hnn(jnp.broadcast_to(dde, (8, 2048)), (hh == jj).astype(F32))[0:1]]

    (dys, dz), (g_nw, g_d) = _rowwise("ssd_gate_norm_bwd", fn, [du, yf, yb, (xbc, 2048, 0), z], [d_e, nw],
                                      [(2048, F32), (2048, BF16)], [(1, 2048), (1, SSD_HEADS)], 128)
    return dys, dz, g_nw, g_d


def _loss_bwd(x1, y1, tgt, gate, fnw):
    dm = x1.shape[1]

    def fn(x1, y1, tgt, gate, fnw):
        def head(x2, fnw):
            yf = (x2 * lax.rsqrt(jnp.mean(x2 * x2, axis=-1, keepdims=True) + NORM_EPS)) * fnw
            err = yf - tgt
            return 0.5 * jnp.sum(jnp.mean(err * err, axis=-1, keepdims=True), axis=0, keepdims=True)

        x2 = x1 + gate * y1
        loss, vjp = jax.vjp(head, x2, fnw)
        dx2, dfnw = vjp(jnp.ones((1, 1), F32))
        return [dx2, gate * dx2], [dfnw, jnp.sum(dx2 * y1, axis=0, keepdims=True), jnp.broadcast_to(loss, (1, 128))]

    (dx2, dy1), (g_fnw, dgate, loss) = _rowwise("loss_bwd", fn, [x1, y1, tgt], [gate, fnw], [(dm, F32), (dm, BF16)],
                                                [(1, dm), (1, dm), (1, 128)], 256)
    return dx2, dy1, g_fnw, dgate, loss


def _gate_bwd(dx, y, gate):
    dm = dx.shape[1]
    (dy,), (dgate,) = _rowwise("gate_bwd", lambda dx, y, gate: ([gate * dx], [jnp.sum(dx * y, axis=0, keepdims=True)]),
                               [dx, y], [gate], [(dm, BF16)], [(1, dm)], 512)
    return dy, dgate


def _softplus_fwd(dt_raw, bias):
    (dt,), _ = _rowwise("dt_softplus", lambda r, b: ([jax.nn.softplus(r + b)], []), [dt_raw], [bias],
                        [(dt_raw.shape[1], F32)], [], 512)
    return dt


def _softplus_bwd(ddt_f, ddt_b, dt_raw, bias):
    def fn(df, db, r, b):
        g = (df + db) * jax.nn.sigmoid(r + b)
        return [g], [jnp.sum(g, axis=0, keepdims=True)]

    w = dt_raw.shape[1]
    (g,), (gb,) = _rowwise("dt_softplus_bwd", fn, [ddt_f, ddt_b, dt_raw], [bias], [(w, BF16)], [(1, w)], 512)
    return g, gb


def _whole(a):
    nd = len(a.shape)
    return pl.BlockSpec(a.shape, lambda *_: (0,) * nd)


def _mod_part(c_all, mod_w):
    nl, _, ncol = mod_w.shape
    nb = c_all.shape[0]

    def body(c_ref, w_ref, o_ref):
        cond = _silu(c_ref[...])
        for i in range(nl):
            o_ref[i * nb:(i + 1) * nb, :] = _nn(cond, w_ref[i])

    return pl.pallas_call(body, name="mod_part", out_shape=jax.ShapeDtypeStruct((nl * nb, ncol), F32),
                          compiler_params=_params(None, VMEM_BIG))(c_all, mod_w)


def _mod_finish(mod_nb, mod_b):
    def body(a_ref, b_ref, o_ref):
        o_ref[...] = a_ref[...] + b_ref[...]

    return pl.pallas_call(body, name="mod_finish", out_shape=jax.ShapeDtypeStruct(mod_b.shape, F32))(mod_nb, mod_b)


def _mod_grad(c_all, dmod_sh):
    nl, nb, ncol = dmod_sh.shape
    dm = c_all.shape[1]

    def body(c_ref, d_ref, o_ref):
        cond = _silu(c_ref[...])
        for i in range(nl):
            o_ref[i] = _tn(cond, d_ref[i])

    return pl.pallas_call(body, name="mod_grad", out_shape=jax.ShapeDtypeStruct((nl, dm, ncol), F32),
                          compiler_params=_params(None, VMEM_BIG))(c_all, dmod_sh)


PACK_ROWS = 16
PACK_COLS = 1024


def _pack_small(rows6, nw2, fnw, b64, a64, d32, extra=None):
    args = [rows6, nw2, fnw, b64, a64, d32] + ([extra] if extra is not None else [])

    def body(*refs):
        o_ref = refs[-1]
        o_ref[...] = jnp.zeros_like(o_ref)
        o_ref[0:6, :] = refs[0][...]
        o_ref[6:8, :] = refs[1][...]
        o_ref[8:9, :] = refs[2][...]
        o_ref[9:10, 0:64] = refs[3][...]
        o_ref[9:10, 64:128] = refs[4][...]
        o_ref[9:10, 128:160] = refs[5][...]
        if extra is not None:
            o_ref[9:10, 256:384] = refs[6][...]

    return pl.pallas_call(body, name="pack_small", out_shape=jax.ShapeDtypeStruct((PACK_ROWS, PACK_COLS), F32))(*args)


def _unpack_small(p):
    return (p[0:6].reshape(2, 3 * PACK_COLS), p[6:8], p[8], p[9, 0:64].reshape(1, 2, 32), p[9, 64:128].reshape(1, 2, 32),
            p[9, 128:160].reshape(1, 32))


def _pack_ssd_small(cw, cb, nw):
    def body(cw_ref, cb_ref, nw_ref, o_ref):
        o_ref[...] = jnp.zeros_like(o_ref)
        o_ref[0:5, :] = cw_ref[...]
        o_ref[5:6, :] = cb_ref[...]
        o_ref[6:7, 0:256] = nw_ref[...]

    return pl.pallas_call(body, name="pack_ssd_small", out_shape=jax.ShapeDtypeStruct((8, 512), F32))(cw, cb, nw)


def _adamw(name, w, parts, m, v, tr):
    r_, c_ = w.shape
    p_ = parts.shape[0]
    tr = min(tr, r_)
    assert r_ % tr == 0

    def body(w_ref, p_ref, m_ref, v_ref, g_ref, d_ref, m2_ref, v2_ref):
        g = p_ref[0].astype(F32)
        for s in range(1, p_):
            g = g + p_ref[s].astype(F32)
        m2 = ADAM_B1 * m_ref[...] + (1.0 - ADAM_B1) * g
        v2 = ADAM_B2 * v_ref[...] + (1.0 - ADAM_B2) * (g * g)
        m_hat = m2 / (1.0 - ADAM_B1 ** ADAM_STEP)
        v_hat = v2 / (1.0 - ADAM_B2 ** ADAM_STEP)
        g_ref[...] = g
        d_ref[...] = -ADAM_LR * (m_hat / (jnp.sqrt(v_hat) + ADAM_EPS) + ADAM_WD * w_ref[...])
        m2_ref[...] = m2
        v2_ref[...] = v2

    blk = pl.BlockSpec((tr, c_), lambda i: (i, 0))
    return pl.pallas_call(
        body, name=name, grid=(r_ // tr,), out_shape=[jax.ShapeDtypeStruct((r_, c_), F32)] * 4,
        in_specs=[blk, pl.BlockSpec((p_, tr, c_), lambda i: (0, i, 0)), blk, blk], out_specs=[blk] * 4,
        compiler_params=_params(("parallel",), VMEM_BIG),
    )(w, parts, m, v)


def _dev_index(p):
    return 4 * p[0] + 2 * p[1] + p[2]


def _all_gather(name, xs):
    n = len(xs)
    hbm = pl.BlockSpec(memory_space=pl.ANY)

    def body(*refs):
        x_refs, o_refs = refs[:n], refs[n:2 * n]
        send_sems, recv_sems, local_sems = refs[2 * n:]
        x, y, c = lax.axis_index("x"), lax.axis_index("y"), lax.axis_index("c")
        me, sibling = (x, y, c), (x, y, 1 - c)
        chips = [(1 - x, y), (x, 1 - y), (1 - x, 1 - y)]

        def copy(a, k, block, to, src=None):
            dst = o_refs[a].at[_dev_index(block)]
            return pltpu.make_async_remote_copy(
                src_ref=dst if src is None else src, dst_ref=dst, send_sem=send_sems.at[a, k],
                recv_sem=recv_sems.at[a, k], device_id=to, device_id_type=MESH)

        mine = [pltpu.make_async_copy(x_refs[a], o_refs[a].at[_dev_index(me)], local_sems.at[a]) for a in range(n)]
        for cp in mine:
            cp.start()
        first = []
        for a in range(n):
            first.append(copy(a, 0, me, sibling, src=x_refs[a]))
            first += [copy(a, 1 + j, me, (*chip, c), src=x_refs[a]) for j, chip in enumerate(chips)]
        for cp in first:
            cp.start()
        passed = []
        for j, chip in enumerate(chips):
            for a in range(n):
                copy(a, 1 + j, (*chip, c), me).wait_recv()
                cp = copy(a, 4 + j, (*chip, c), sibling)
                cp.start()
                passed.append(cp)
        for a in range(n):
            copy(a, 0, sibling, me).wait_recv()
            for j, chip in enumerate(chips):
                copy(a, 4 + j, (*chip, 1 - c), me).wait_recv()
        for cp in first + passed:
            cp.wait_send()
        for cp in mine:
            cp.wait()

    return pl.pallas_call(
        body, name=name, out_shape=[jax.ShapeDtypeStruct((NDEV, *x.shape), x.dtype) for x in xs],
        in_specs=[hbm] * n, out_specs=[hbm] * n,
        scratch_shapes=[pltpu.SemaphoreType.DMA((n, 7)), pltpu.SemaphoreType.DMA((n, 7)), pltpu.SemaphoreType.DMA((n,))],
    )(*xs)


def _all_to_all(name, xs):
    n = len(xs)
    hbm = pl.BlockSpec(memory_space=pl.ANY)

    def body(*refs):
        x_refs, o_refs = refs[:n], refs[n:2 * n]
        send_sems, recv_sems, local_sems = refs[2 * n:]
        x, y, c = lax.axis_index("x"), lax.axis_index("y"), lax.axis_index("c")
        me = (x, y, c)
        mine = [pltpu.make_async_copy(x_refs[a].at[_dev_index(me)], o_refs[a].at[_dev_index(me)], local_sems.at[a])
                for a in range(n)]
        for cp in mine:
            cp.start()
        copies = []
        for k in range(1, NDEV):
            peer = tuple(1 - v if (k >> b) & 1 else v for v, b in zip(me, (2, 1, 0)))
            for a in range(n):
                copies.append(pltpu.make_async_remote_copy(
                    src_ref=x_refs[a].at[_dev_index(peer)], dst_ref=o_refs[a].at[_dev_index(me)],
                    send_sem=send_sems.at[a, k - 1], recv_sem=recv_sems.at[a, k - 1], device_id=peer, device_id_type=MESH))
        for cp in copies:
            cp.start()
        for cp in copies:
            cp.wait()
        for cp in mine:
            cp.wait()

    return pl.pallas_call(
        body, name=name, out_shape=[jax.ShapeDtypeStruct(x.shape, x.dtype) for x in xs],
        in_specs=[hbm] * n, out_specs=[hbm] * n,
        scratch_shapes=[pltpu.SemaphoreType.DMA((n, 7)), pltpu.SemaphoreType.DMA((n, 7)), pltpu.SemaphoreType.DMA((n,))],
    )(*xs)


def kernel(x, c, positions, norm_w, mod_w, mod_b, attn_w_in, attn_w_out, ssd_w_in, ssd_conv_w, ssd_conv_b, ssd_dt_bias, ssd_a_log, ssd_d, ssd_norm_w, ssd_w_out, final_norm_w, loss_target, m_norm_w, m_mod_w, m_mod_b, m_attn_w_in, m_attn_w_out, m_ssd_w_in, m_ssd_conv_w, m_ssd_conv_b, m_ssd_dt_bias, m_ssd_a_log, m_ssd_d, m_ssd_norm_w, m_ssd_w_out, m_final_norm_w, v_norm_w, v_mod_w, v_mod_b, v_attn_w_in, v_attn_w_out, v_ssd_w_in, v_ssd_conv_w, v_ssd_conv_b, v_ssd_dt_bias, v_ssd_a_log, v_ssd_d, v_ssd_norm_w, v_ssd_w_out, v_final_norm_w):
    s_len, dm = x.shape[1], x.shape[2]
    me = 4 * lax.axis_index("x") + 2 * lax.axis_index("y") + lax.axis_index("c")
    x0 = x.reshape(s_len, dm)
    tgt = loss_target.reshape(s_len, dm)
    aw = 3 * 512
    si = 2 * dm
    sxbc = 2 * si
    n_ssd_in = ssd_w_in.shape[2] * NDEV

    ssd_small = _pack_ssd_small(ssd_conv_w[0], ssd_conv_b, ssd_norm_w)
    g_ai, g_ao, g_si, g_so, g_small, c_all = _all_gather("gather_weights", [
        attn_w_in[0].astype(BF16), attn_w_out[0].astype(BF16), ssd_w_in[0].astype(BF16), ssd_w_out[0].astype(BF16),
        ssd_small, c])
    w_ai = g_ai.transpose(1, 0, 2).reshape(dm, 4 * aw)
    w_ao = g_ao.reshape(aw, dm)
    w_si = g_si.transpose(1, 0, 2).reshape(dm, n_ssd_in)
    w_z, w_xbc, w_dt = w_si[:, :si], w_si[:, si:si + sxbc], w_si[:, si + sxbc:]
    w_so = g_so.reshape(si, dm)
    conv_w = g_small[:, 0:CONV_WIDTH, :].transpose(1, 0, 2).reshape(CONV_WIDTH, sxbc)
    conv_b = g_small[:, 5, :].reshape(1, sxbc)
    snw = g_small[:, 6, 0:si // NDEV].reshape(1, si)
    c_all = c_all.reshape(NDEV, dm)

    part = _mod_part(c_all, mod_w)
    (part_all,) = _all_gather("gather_mod", [part])
    mod_nb = jnp.stack([lax.dynamic_index_in_dim(part_all, i * NDEV + me, axis=1, keepdims=False).reshape(3 * dm)
                        for i in range(2)])
    mod = _mod_finish(mod_nb, mod_b)
    shift = [mod[i:i + 1, 0:dm] for i in range(2)]
    scale = [mod[i:i + 1, dm:2 * dm] for i in range(2)]
    gate = [mod[i:i + 1, 2 * dm:3 * dm] for i in range(2)]
    nw = [norm_w[i:i + 1] for i in range(2)]

    hn0 = _norm_mod_fwd("norm0", x0, nw[0], scale[0], shift[0])
    inv_freq = ROPE_THETA ** (-jnp.arange(0, ROT_DIM, 2, dtype=F32) / ROT_DIM)
    lane = jnp.arange(128) % HEAD_DIM
    inv_row = jnp.where(lane < ROT_DIM, inv_freq[lane % (ROT_DIM // 2)], 0.0).reshape(1, 128).astype(F32)
    tabs = _rope_tables(positions.reshape(s_len, 1), inv_row)
    qk = _matmul("proj_qk", hn0, w_ai, "nn", BF16, MM_T, MM_T, dm, epilogue=_rot_fwd, mrows=tabs, n_out=2 * aw)
    v = _matmul("proj_v", hn0, w_ai, "nn", BF16, MM_T, aw // 2, dm, b_noff=2 * aw, n_out=aw)
    z0 = _matmul("proj_z", hn0, w_ai, "nn", F32, MM_T, aw // 2, dm, b_noff=3 * aw, n_out=aw)
    att = [_attn_fwd(g, qk, v) for g in range(3)]
    os_, lses = [a[0] for a in att], [a[1] for a in att]
    a0, y0, x1 = _attn_out(os_, lses, z0, x0, gate[0], w_ao)

    hn1 = _norm_mod_fwd("norm1", x1, nw[1], scale[1], shift[1])
    z1 = _matmul("ssd_proj_z", hn1, w_z, "nn", F32, MM_T, MM_T, dm)
    xpre = _matmul("ssd_proj_xbc", hn1, w_xbc, "nn", F32, MM_T, MM_T, dm)
    dt_raw = _matmul("ssd_proj_dt", hn1, w_dt, "nn", F32, MM_T, 64, dm)
    xbc = _conv_fwd(xpre, conv_w, conv_b)
    dt_bias = ssd_dt_bias.reshape(1, 2 * SSD_HEADS)
    alog = ssd_a_log.reshape(1, 2 * SSD_HEADS)
    dt = _softplus_fwd(dt_raw, dt_bias)
    y_f, st_f = _ssd_fwd(xbc, dt, alog, 0)
    y_b, st_b = _ssd_fwd(xbc, dt, alog, 1)
    d_e = jnp.repeat(ssd_d.reshape(SSD_HEADS), HEAD_DIM).reshape(1, si)
    u = _gate_norm_fwd(y_f, y_b, xbc, z1, d_e, snw)
    y1 = _matmul("ssd_out", u, w_so, "nn", F32, MM_T, MM_T, si)

    fnw = final_norm_w.reshape(1, dm)
    dx2, dy1, g_fnw, dgate1, loss_part = _loss_bwd(x1, y1, tgt, gate[1], fnw)
    du = _matmul("ssd_out_dx", dy1, w_so, "nt", F32, MM_T, MM_T, dm)
    gw_so = _matmul("ssd_out_dw", u, dy1, "tn", BF16, MM_T, MM_T, MM_T)
    dys, dz1, g_snw, g_d = _gate_norm_bwd(du, y_f, y_b, xbc, z1, d_e, snw)
    dxbc_f, ddt_f, dalog_f = _ssd_bwd(xbc, dt, alog, st_f, dys, d_e, 0)
    dxbc_b, ddt_b, dalog_b = _ssd_bwd(xbc, dt, alog, st_b, dys, d_e, 1)
    dpre, g_cw, g_cb = _conv_bwd(xpre, dxbc_f, dxbc_b, conv_w, conv_b)
    ddt_raw, g_dtb = _softplus_bwd(ddt_f, ddt_b, dt_raw, dt_bias)
    dhn1 = [_matmul("ssd_proj_z_dx", dz1, w_z, "nt", F32, MM_T, MM_T, MM_T),
            _matmul("ssd_proj_xbc_dx", dpre, w_xbc, "nt", F32, MM_T, MM_T, MM_T),
            _matmul("ssd_proj_dt_dx", ddt_raw, w_dt, "nt", F32, MM_T, MM_T, 64)]
    gw_si = jnp.concatenate([_matmul("ssd_proj_z_dw", hn1, dz1, "tn", BF16, MM_T, MM_T, MM_T),
                             _matmul("ssd_proj_xbc_dw", hn1, dpre, "tn", BF16, MM_T, MM_T, MM_T),
                             _matmul("ssd_proj_dt_dw", hn1, ddt_raw, "tn", BF16, MM_T, 64, MM_T)], axis=1)
    dx1, g_nw1, dsc1, dsh1 = _norm_mod_bwd("norm1_bwd", x1, dhn1, dx2, nw[1], scale[1], shift[1])

    dy0, dgate0 = _gate_bwd(dx1, y0, gate[0])
    da0 = _matmul("attn_out_dx", dy0, w_ao, "nt", F32, MM_T, aw // 2, dm)
    gw_ao = _matmul("attn_out_dw", a0, dy0, "tn", BF16, aw // 2, MM_T, MM_T)
    dos, dls, dz0 = _mix_bwd(da0, os_, lses, z0)
    datt = [_attn_bwd(g, qk, v, os_[g], lses[g], dos[g], dls[g]) for g in range(3)]
    dqkv = _rot_pack_bwd([t[0] for t in datt], [t[1] for t in datt], [t[2] for t in datt], tabs)
    dhn0 = [_matmul("proj_qkv_dx", dqkv, w_ai, "nt", F32, MM_T, MM_T, aw, n_out=dm),
            _matmul("proj_z_dx", dz0, w_ai, "nt", F32, MM_T, MM_T, aw, b_koff=3 * aw, n_out=dm)]
    wcol = attn_w_in.shape[2]
    gw_ai = jnp.concatenate([
        _matmul("proj_qkv_dw", hn0, dqkv, "tn", BF16, MM_T, wcol, MM_T, out_blocks=3 * aw // wcol),
        _matmul("proj_z_dw", hn0, dz0, "tn", BF16, MM_T, wcol, MM_T, out_blocks=aw // wcol)], axis=0)
    dx0, g_nw0, dsc0, dsh0 = _norm_mod_bwd("norm0_bwd", x0, dhn0, dx1, nw[0], scale[0], shift[0])

    rows6 = jnp.concatenate([dsh0, dsc0, dgate0, dsh1, dsc1, dgate1], axis=0)
    small_g = _pack_small(rows6, jnp.concatenate([g_nw0, g_nw1], axis=0), g_fnw, g_dtb, dalog_f + dalog_b, g_d, loss_part)
    (small_all,) = _all_gather("gather_small_grads", [small_g])
    small_w = _pack_small(mod_b.reshape(6, dm), norm_w, fnw, dt_bias, alog, ssd_d)
    small_m = _pack_small(m_mod_b.reshape(6, dm), m_norm_w, m_final_norm_w.reshape(1, dm), m_ssd_dt_bias.reshape(1, 64),
                          m_ssd_a_log.reshape(1, 64), m_ssd_d)
    small_v = _pack_small(v_mod_b.reshape(6, dm), v_norm_w, v_final_norm_w.reshape(1, dm), v_ssd_dt_bias.reshape(1, 64),
                          v_ssd_a_log.reshape(1, 64), v_ssd_d)
    small_out = _adamw("adamw_small", small_w, small_all, small_m, small_v, PACK_ROWS)
    loss = small_out[0][9, 256]
    sg, sd, sm, sv = (_unpack_small(p) for p in small_out)

    ncol = mod_w.shape[2]
    dmod_all = small_all[:, 0:6, :].reshape(NDEV, 2, 3 * dm)
    dmod_sh = lax.dynamic_slice_in_dim(dmod_all, me * ncol, ncol, axis=2).transpose(1, 0, 2)
    g_modw = _mod_grad(c_all, dmod_sh).reshape(1, 2 * dm, ncol)
    modw_out = _adamw("adamw_mod_w", mod_w.reshape(2 * dm, ncol), g_modw, m_mod_w.reshape(2 * dm, ncol),
                      v_mod_w.reshape(2 * dm, ncol), 256)

    ssd_small_g = _pack_ssd_small_blocks(g_cw, g_cb, g_snw)
    r_ai, r_ao, r_si, r_so, r_small = _all_to_all("scatter_grads", [
        gw_ai, gw_ao.reshape(NDEV, aw // NDEV, dm), gw_si.reshape(dm, NDEV, n_ssd_in // NDEV).transpose(1, 0, 2),
        gw_so.reshape(NDEV, si // NDEV, dm), ssd_small_g])
    ai_out = _adamw("adamw_attn_w_in", attn_w_in[0], r_ai, m_attn_w_in[0], v_attn_w_in[0], 256)
    ao_out = _adamw("adamw_attn_w_out", attn_w_out[0], r_ao, m_attn_w_out[0], v_attn_w_out[0], 192)
    si_out = _adamw("adamw_ssd_w_in", ssd_w_in[0], r_si, m_ssd_w_in[0], v_ssd_w_in[0], 256)
    so_out = _adamw("adamw_ssd_w_out", ssd_w_out[0], r_so, m_ssd_w_out[0], v_ssd_w_out[0], 256)
    ssd_small_m = _pack_ssd_small(m_ssd_conv_w[0], m_ssd_conv_b, m_ssd_norm_w)
    ssd_small_v = _pack_ssd_small(v_ssd_conv_w[0], v_ssd_conv_b, v_ssd_norm_w)
    ss_out = _adamw("adamw_ssd_small", ssd_small, r_small, ssd_small_m, ssd_small_v, 8)

    def ssd_small_unpack(p):
        return p[0:5][None], p[5:6], p[6:7, 0:si // NDEV]

    cwo, cbo, nwo = zip(*(ssd_small_unpack(p) for p in ss_out))
    per_kind = []
    for k in range(4):
        s = (sg, sd, sm, sv)[k]
        per_kind.append([
            s[1], modw_out[k].reshape(mod_w.shape), s[0], ai_out[k][None], ao_out[k][None], si_out[k][None],
            cwo[k], cbo[k], s[3], s[4], s[5], nwo[k], so_out[k][None], s[2]])
    return (loss, dx0.reshape(x.shape), *per_kind[0], *per_kind[1], *per_kind[2], *per_kind[3])


def _pack_ssd_small_blocks(g_cw, g_cb, g_nw):
    nper = g_cw.shape[1] // NDEV
    nwper = g_nw.shape[1] // NDEV

    def body(cw_ref, cb_ref, nw_ref, o_ref):
        o_ref[...] = jnp.zeros_like(o_ref)
        for d in range(NDEV):
            o_ref[d, 0:5, :] = cw_ref[:, d * nper:(d + 1) * nper]
            o_ref[d, 5:6, :] = cb_ref[:, d * nper:(d + 1) * nper]
            o_ref[d, 6:7, 0:nwper] = nw_ref[:, d * nwper:(d + 1) * nwper]

    return pl.pallas_call(body, name="pack_ssd_small_grads", out_shape=jax.ShapeDtypeStruct((NDEV, 8, nper), F32))(g_cw, g_cb, g_nw)
```

```python
import functools
import math

import jax
import jax.numpy as jnp
from jax import lax
from jax.experimental import pallas as pl
from jax.experimental.pallas import tpu as pltpu

F32 = jnp.float32
BF16 = jnp.bfloat16
HI = lax.Precision.HIGHEST
MESH = pl.DeviceIdType.MESH
NDEV = 8

NORM_EPS = 1e-6
ROPE_THETA = 500000.0
ROT_DIM = 16
HEAD_DIM = 64
DILATIONS = (1, 4, 16)
BAND = 64
NEG_BIG = -1e30
CHUNK = 128
SSD_HEADS = 32
SSD_GROUPS = 8
CONV_WIDTH = 5

ADAM_LR = 0.001
ADAM_B1 = 0.9
ADAM_B2 = 0.999
ADAM_EPS = 1e-08
ADAM_WD = 0.01
ADAM_STEP = 10

VMEM_BIG = 56 * 1024 * 1024
MM_T = 1024


def _params(sem=None, vmem=None):
    kw = {}
    if sem is not None:
        kw["dimension_semantics"] = sem
    if vmem is not None:
        kw["vmem_limit_bytes"] = vmem
    return pltpu.CompilerParams(**kw)


def _dg(a, b, ca, cb, prec=None):
    return lax.dot_general(a, b, (((ca,), (cb,)), ((), ())), preferred_element_type=F32, precision=prec)


def _nn(a, b):
    return _dg(a.astype(BF16), b.astype(BF16), 1, 0)


def _nt(a, b):
    return _dg(a.astype(BF16), b.astype(BF16), 1, 1)


def _tn(a, b):
    return _dg(a.astype(BF16), b.astype(BF16), 0, 0)


def _hnn(a, b):
    return _dg(a, b, 1, 0, HI)


@jax.custom_vjp
def _bnn(a, b):
    return _nn(a, b)


_bnn.defvjp(lambda a, b: (_nn(a, b), (a, b)), lambda r, g: (_nt(g, r[1]), _tn(r[0], g)))


@jax.custom_vjp
def _bnt(a, b):
    return _nt(a, b)


_bnt.defvjp(lambda a, b: (_nt(a, b), (a, b)), lambda r, g: (_nn(g, r[1]), _tn(g, r[0])))


@jax.custom_vjp
def _btn(a, b):
    return _tn(a, b)


_btn.defvjp(lambda a, b: (_tn(a, b), (a, b)), lambda r, g: (_nt(r[1], g), _nn(r[0], g)))


def _silu(x):
    return x * jax.nn.sigmoid(x)


def _matmul(name, a, b, mode, out_dtype, tm, tn, tk, *, epilogue=None, tiled=(), mrows=(), ncols=(),
            b_noff=0, b_koff=0, n_out=None, out_blocks=None):
    if mode == "tn":
        K, M = a.shape
    else:
        M, K = a.shape
    N = n_out if n_out is not None else (b.shape[0] if mode == "nt" else b.shape[1])
    tm, tn, tk = min(tm, M), min(tn, N), min(tk, K)
    assert M % tm == 0 and N % tn == 0 and K % tk == 0, (name, M, N, K, tm, tn, tk)
    assert b_noff % tn == 0 and b_koff % tk == 0
    no, ko = b_noff // tn, b_koff // tk
    nk = K // tk
    if mode == "tn":
        a_spec = pl.BlockSpec((tk, tm), lambda i, j, k: (k, i))
    else:
        a_spec = pl.BlockSpec((tm, tk), lambda i, j, k: (i, k))
    if mode == "nt":
        b_spec = pl.BlockSpec((tn, tk), lambda i, j, k: (j + no, k + ko))
    else:
        b_spec = pl.BlockSpec((tk, tn), lambda i, j, k: (k + ko, j + no))
    specs = [a_spec, b_spec]
    specs += [pl.BlockSpec((tm, tn), lambda i, j, k: (i, j)) for _ in tiled]
    specs += [pl.BlockSpec((tm, r.shape[1]), lambda i, j, k: (i, 0)) for r in mrows]
    specs += [pl.BlockSpec((1, tn), lambda i, j, k: (0, j)) for _ in ncols]
    if out_blocks is None:
        out_shape = jax.ShapeDtypeStruct((M, N), out_dtype)
        out_spec = pl.BlockSpec((tm, tn), lambda i, j, k: (i, j))
    else:
        nper = N // out_blocks
        assert nper % tn == 0
        jb = nper // tn
        out_shape = jax.ShapeDtypeStruct((out_blocks, M, nper), out_dtype)
        out_spec = pl.BlockSpec((None, tm, tn), lambda i, j, k: (j // jb, i, j % jb))
    ne = len(tiled) + len(mrows) + len(ncols)
    dot = {"nn": _nn, "nt": _nt, "tn": _tn}[mode]

    def body(a_ref, b_ref, *rest):
        extras, o_ref = rest[:ne], rest[ne]

        def finish(acc):
            if epilogue is not None:
                acc = epilogue(acc, *[e[...] for e in extras])
            o_ref[...] = acc.astype(o_ref.dtype)

        if nk == 1:
            finish(dot(a_ref[...], b_ref[...]))
        else:
            acc_ref = rest[ne + 1]
            k = pl.program_id(2)

            @pl.when(k == 0)
            def _():
                acc_ref[...] = jnp.zeros_like(acc_ref)

            acc_ref[...] += dot(a_ref[...], b_ref[...])

            @pl.when(k == nk - 1)
            def _():
                finish(acc_ref[...])

    return pl.pallas_call(
        body, name=name, out_shape=out_shape, grid=(M // tm, N // tn, nk),
        in_specs=specs, out_specs=out_spec,
        scratch_shapes=[] if nk == 1 else [pltpu.VMEM((tm, tn), F32)],
        compiler_params=_params(("parallel", "parallel", "arbitrary"), VMEM_BIG),
    )(a, b, *tiled, *mrows, *ncols)


def _rowwise(name, fn, tiled, consts, outs, accs, ts):
    tl = [(t, t.shape[1], 0) if not isinstance(t, tuple) else t for t in tiled]
    s_len = tl[0][0].shape[0]
    assert s_len % ts == 0
    nt_, nc_, no_ = len(tl), len(consts), len(outs)

    def body(*refs):
        t_refs, c_refs = refs[:nt_], refs[nt_:nt_ + nc_]
        o_refs, a_refs = refs[nt_ + nc_:nt_ + nc_ + no_], refs[nt_ + nc_ + no_:]
        res_o, res_a = fn(*[r[...] for r in t_refs], *[r[...] for r in c_refs])
        for r, v in zip(o_refs, res_o, strict=True):
            r[...] = v.astype(r.dtype)
        if a_refs:
            @pl.when(pl.program_id(0) == 0)
            def _():
                for r in a_refs:
                    r[...] = jnp.zeros_like(r)

            for r, v in zip(a_refs, res_a, strict=True):
                r[...] += v

    in_specs = [pl.BlockSpec((ts, w), functools.partial(lambda i, cb: (i, cb), cb=cb)) for (_, w, cb) in tl]
    in_specs += [pl.BlockSpec(c.shape, lambda i: (0, 0)) for c in consts]
    out_specs = [pl.BlockSpec((ts, c), lambda i: (i, 0)) for (c, _) in outs]
    out_specs += [pl.BlockSpec(shp, lambda i: (0, 0)) for shp in accs]
    out_shape = [jax.ShapeDtypeStruct((s_len, c), dt) for (c, dt) in outs]
    out_shape += [jax.ShapeDtypeStruct(shp, F32) for shp in accs]
    res = pl.pallas_call(
        body, name=name, out_shape=out_shape, grid=(s_len // ts,), in_specs=in_specs, out_specs=out_specs,
        compiler_params=_params(("arbitrary",) if accs else ("parallel",), VMEM_BIG),
    )(*[t[0] for t in tl], *consts)
    return res[:no_], res[no_:]


def _norm_mod_fn(x, nw, sc, sh):
    r = lax.rsqrt(jnp.mean(x * x, axis=-1, keepdims=True) + NORM_EPS)
    return (x * r * nw) * (1.0 + sc) + sh


def _norm_mod_fwd(name, x, nw, sc, sh):
    (hn,), _ = _rowwise(name, lambda x, nw, sc, sh: ([_norm_mod_fn(x, nw, sc, sh)], []),
                        [x], [nw, sc, sh], [(x.shape[1], BF16)], [], 512)
    return hn


def _norm_mod_bwd(name, x, dhn_parts, dres, nw, sc, sh):
    n = len(dhn_parts)
    d = x.shape[1]

    def fn(x, *rest):
        dhn = rest[0]
        for p in rest[1:n]:
            dhn = dhn + p
        dres, nw, sc, sh = rest[n:]
        _, vjp = jax.vjp(_norm_mod_fn, x, nw, sc, sh)
        dx, dnw, dsc, dsh = vjp(dhn)
        return [dx + dres], [dnw, dsc, dsh]

    (dx,), (g_nw, dsc, dsh) = _rowwise(name, fn, [x, *dhn_parts, dres], [nw, sc, sh], [(d, F32)],
                                       [(1, d), (1, d), (1, d)], 256)
    return dx, g_nw, dsc, dsh


def _rope_tables(pos_col, inv_row):
    def fn(pos, inv):
        ang = pos.astype(F32) * inv
        e = lax.broadcasted_iota(jnp.int32, (1, 128), 1) % HEAD_DIM
        cos, sin = jnp.cos(ang), jnp.sin(ang)
        half = ROT_DIM // 2
        return [jnp.where(e < ROT_DIM, cos, 1.0), jnp.where(e < half, -sin, 0.0),
                jnp.where((e >= half) & (e < ROT_DIM), sin, 0.0)], []

    (c, sa, sb), _ = _rowwise("rope_tables", fn, [pos_col], [inv_row], [(128, F32)] * 3, [], 512)
    return c, sa, sb


def _rot_fwd(t, c, sa, sb):
    n = t.shape[1]
    rep = n // 128
    c, sa, sb = (jnp.tile(u, (1, rep)) for u in (c, sa, sb))
    return t * c + pltpu.roll(t, n - ROT_DIM // 2, 1) * sa + pltpu.roll(t, ROT_DIM // 2, 1) * sb


def _rot_bwd(g, c, sa, sb):
    n = g.shape[1]
    rep = n // 128
    c, sa, sb = (jnp.tile(u, (1, rep)) for u in (c, sa, sb))
    return g * c + pltpu.roll(g * sa, ROT_DIM // 2, 1) + pltpu.roll(g * sb, n - ROT_DIM // 2, 1)


ATT_TQ = 128
ATT_TK = ATT_TQ + 2 * BAND


def _attn_specs(g, s_len):
    def blk(off):
        return pl.BlockSpec((s_len, 128), functools.partial(lambda hp, off: (0, off + hp), off=off))

    return blk(4 * g), blk(12 + 4 * g), blk(4 * g), blk(0)


def _attn_tile_geometry(t, d, l):
    nts = l // ATT_TQ
    r = t // nts
    q0 = (t % nts) * ATT_TQ
    ws = jnp.clip(q0 - BAND, 0, l - ATT_TK)
    qpos = q0 + lax.broadcasted_iota(jnp.int32, (ATT_TQ, 1), 0)
    kpos = ws + lax.broadcasted_iota(jnp.int32, (1, ATT_TK), 1)
    valid = jnp.abs(kpos - qpos) <= BAND
    if d == 1:
        return pl.ds(pl.multiple_of(q0, ATT_TQ), ATT_TQ), pl.ds(pl.multiple_of(ws, BAND), ATT_TK), valid
    return pl.ds(r + d * q0, ATT_TQ, stride=d), pl.ds(r + d * ws, ATT_TK, stride=d), valid


def _attn_fwd(g, qk, v):
    s_len = qk.shape[0]
    d = DILATIONS[g]
    l = s_len // d
    assert l % ATT_TQ == 0 and l >= ATT_TK
    q_spec, k_spec, v_spec, o_spec = _attn_specs(g, s_len)
    scale = 1.0 / math.sqrt(HEAD_DIM)

    def body(q_ref, k_ref, v_ref, o_ref, lse_ref):
        lane = lax.broadcasted_iota(jnp.int32, (1, 128), 1)
        in_h = [lane < HEAD_DIM, lane >= HEAD_DIM]

        def tile(t, carry):
            rows, win, valid = _attn_tile_geometry(t, d, l)
            q = q_ref[rows, :].astype(BF16)
            k = k_ref[win, :].astype(BF16)
            vv = v_ref[win, :].astype(BF16)
            outs, lses = [], []
            for h in range(2):
                qm = jnp.where(in_h[h], q, jnp.zeros_like(q))
                s = jnp.where(valid, _nt(qm, k) * scale, NEG_BIG)
                m = jnp.max(s, axis=1, keepdims=True)
                p = jnp.exp(s - m)
                den = jnp.sum(p, axis=1, keepdims=True)
                outs.append(_nn(p, vv) / den)
                lses.append(m + jnp.log(den))
            o_ref[rows, :] = jnp.where(in_h[0], outs[0], outs[1])
            lse_ref[rows, :] = jnp.where(in_h[0], lses[0], lses[1])
            return carry

        lax.fori_loop(0, s_len // ATT_TQ, tile, 0, unroll=2)

    return pl.pallas_call(
        body, name=f"attn_fwd_g{g}", grid=(4,),
        out_shape=[jax.ShapeDtypeStruct((s_len, 512), F32)] * 2,
        in_specs=[q_spec, k_spec, v_spec], out_specs=[o_spec, o_spec],
        compiler_params=_params(("parallel",), VMEM_BIG),
    )(qk, qk, v)


def _attn_bwd(g, qk, v, o, lse, do, dlse):
    s_len = qk.shape[0]
    d = DILATIONS[g]
    l = s_len // d
    q_spec, k_spec, v_spec, o_spec = _attn_specs(g, s_len)
    scale = 1.0 / math.sqrt(HEAD_DIM)

    def body(q_ref, k_ref, v_ref, o_ref, lse_ref, do_ref, dlse_ref, dq_ref, dk_ref, dv_ref):
        lane = lax.broadcasted_iota(jnp.int32, (1, 128), 1)
        in_h = [lane < HEAD_DIM, lane >= HEAD_DIM]
        dk_ref[...] = jnp.zeros_like(dk_ref)
        dv_ref[...] = jnp.zeros_like(dv_ref)

        def tile(t, carry):
            rows, win, valid = _attn_tile_geometry(t, d, l)
            q, k, vv = q_ref[rows, :].astype(BF16), k_ref[win, :].astype(BF16), v_ref[win, :].astype(BF16)
            dout, lse_t, dlse_t = do_ref[rows, :], lse_ref[rows, :], dlse_ref[rows, :]
            od = dout * o_ref[rows, :]
            dqs, dks, dvs = [], [], []
            for h in range(2):
                c0 = h * HEAD_DIM
                qm = jnp.where(in_h[h], q, jnp.zeros_like(q))
                s = jnp.where(valid, _nt(qm, k) * scale, NEG_BIG)
                p = jnp.exp(s - lse_t[:, c0:c0 + 1])
                dom = jnp.where(in_h[h], dout, 0.0)
                dp = _nt(dom, vv)
                delta = jnp.sum(jnp.where(in_h[h], od, 0.0), axis=1, keepdims=True)
                ds = (p * (dp - delta + dlse_t[:, c0:c0 + 1]) * scale).astype(BF16)
                dqs.append(_nn(ds, k))
                dks.append(_tn(ds, q))
                dvs.append(_tn(p, dout))
            dq_ref[rows, :] = jnp.where(in_h[0], dqs[0], dqs[1])
            dk_ref[win, :] += jnp.where(in_h[0], dks[0], dks[1])
            dv_ref[win, :] += jnp.where(in_h[0], dvs[0], dvs[1])
            return carry

        lax.fori_loop(0, s_len // ATT_TQ, tile, 0, unroll=2)

    return pl.pallas_call(
        body, name=f"attn_bwd_g{g}", grid=(4,),
        out_shape=[jax.ShapeDtypeStruct((s_len, 512), F32)] * 3,
        in_specs=[q_spec, k_spec, v_spec, o_spec, o_spec, o_spec, o_spec], out_specs=[o_spec] * 3,
        compiler_params=_params(("parallel",), VMEM_BIG),
    )(qk, qk, v, o, lse, do, dlse)


def _mix_weights(ls):
    mx = jnp.maximum(jnp.maximum(ls[0], ls[1]), ls[2])
    es = [jnp.exp(x - mx) for x in ls]
    tot = es[0] + es[1] + es[2]
    return [e / tot for e in es]


def _attn_out(os_, lses, z, x, gate, w_out):
    s_len, dm = x.shape
    tm = 256
    wdt = 512

    def body(o0, o1, o2, l0, l1, l2, z_ref, x_ref, g_ref, w_ref, a_ref, y_ref, x1_ref):
        alphas = _mix_weights([l0[...], l1[...], l2[...]])
        y = jnp.zeros((tm, dm), F32)
        for g, o_ref in enumerate((o0, o1, o2)):
            a_g = (o_ref[...] * alphas[g] * _silu(z_ref[:, g * wdt:(g + 1) * wdt])).astype(BF16)
            a_ref[:, g * wdt:(g + 1) * wdt] = a_g
            y = y + _nn(a_g, w_ref[g * wdt:(g + 1) * wdt, :])
        y_ref[...] = y
        x1_ref[...] = x_ref[...] + g_ref[...] * y

    row = lambda c: pl.BlockSpec((tm, c), lambda i: (i, 0))
    return pl.pallas_call(
        body, name="attn_out", grid=(s_len // tm,),
        out_shape=[jax.ShapeDtypeStruct((s_len, 3 * wdt), BF16), jax.ShapeDtypeStruct((s_len, dm), F32),
                   jax.ShapeDtypeStruct((s_len, dm), F32)],
        in_specs=[row(wdt)] * 6 + [row(3 * wdt), row(dm), pl.BlockSpec((1, dm), lambda i: (0, 0)),
                                   pl.BlockSpec(w_out.shape, lambda i: (0, 0))],
        out_specs=[row(3 * wdt), row(dm), row(dm)],
        compiler_params=_params(("parallel",), VMEM_BIG),
    )(*os_, *lses, z, x, gate, w_out)


def _mix_bwd(da, os_, lses, z):
    wdt = 512

    def fn(da, o0, o1, o2, l0, l1, l2, z):
        os_t, ls = [o0, o1, o2], [l0, l1, l2]
        alphas = _mix_weights(ls)
        hi = lax.broadcasted_iota(jnp.int32, (wdt, wdt), 0) // HEAD_DIM
        hj = lax.broadcasted_iota(jnp.int32, (wdt, wdt), 1) // HEAD_DIM
        seg = (hi == hj).astype(F32)
        dos, dal, dzs = [], [], []
        for g in range(3):
            zg = z[:, g * wdt:(g + 1) * wdt]
            sig = jax.nn.sigmoid(zg)
            dag = da[:, g * wdt:(g + 1) * wdt]
            dmix = dag * zg * sig
            dzs.append(dag * os_t[g] * alphas[g] * (sig * (1.0 + zg * (1.0 - sig))))
            dos.append(dmix * alphas[g])
            dal.append(_hnn(dmix * os_t[g], seg))
        mean = alphas[0] * dal[0] + alphas[1] * dal[1] + alphas[2] * dal[2]
        dls = [alphas[g] * (dal[g] - mean) for g in range(3)]
        return dos + dls + [jnp.concatenate(dzs, axis=1)], []

    outs, _ = _rowwise("mix_bwd", fn, [da, *os_, *lses, z], [], [(wdt, F32)] * 6 + [(3 * wdt, BF16)], [], 256)
    return outs[:3], outs[3:6], outs[6]


def _rot_pack_bwd(dqs, dks, dvs, tabs):
    wdt = 512

    def fn(*args):
        grads, (c, sa, sb) = args[:9], args[9:]
        cols = [_rot_bwd(gq, c, sa, sb) for gq in grads[:6]] + list(grads[6:])
        return [jnp.concatenate(cols, axis=1)], []

    (out,), _ = _rowwise("rot_pack_bwd", fn, [*dqs, *dks, *dvs, *tabs], [], [(9 * wdt, BF16)], [], 256)
    return out


CONV_CB = 256
CONV_R = 128
CONV_PAD = 8


def _conv_window_sum(win, w, off, sign):
    acc = None
    for j in range(CONV_WIDTH):
        o = off + sign * j
        term = win[o:o + CONV_R, :] * w[j:j + 1, :]
        acc = term if acc is None else acc + term
    return acc


def _conv_fwd(xpre, cw, cb):
    s_len, ch = xpre.shape
    nchunk = s_len // CONV_R

    def body(x_ref, w_ref, b_ref, o_ref, xp):
        zero = jnp.zeros((CONV_PAD, CONV_CB), F32)
        xp[0:CONV_PAD, :] = zero
        xp[s_len + CONV_PAD:s_len + 2 * CONV_PAD, :] = zero

        def fill(ci, carry):
            base = pl.multiple_of(ci * CONV_R, CONV_R)
            xp[pl.ds(base + CONV_PAD, CONV_R), :] = x_ref[pl.ds(base, CONV_R), :]
            return carry

        lax.fori_loop(0, nchunk, fill, 0)
        w = w_ref[...]
        b = b_ref[...]

        def chunk(ci, carry):
            base = pl.multiple_of(ci * CONV_R, CONV_R)
            win = xp[pl.ds(base, CONV_R + 2 * CONV_PAD), :]
            u = _conv_window_sum(win, w, CONV_PAD - CONV_WIDTH // 2, 1) + b
            o_ref[pl.ds(base, CONV_R), :] = _silu(u)
            return carry

        lax.fori_loop(0, nchunk, chunk, 0)

    col = lambda r: pl.BlockSpec((r, CONV_CB), lambda j: (0, j))
    return pl.pallas_call(
        body, name="conv_fwd", grid=(ch // CONV_CB,), out_shape=jax.ShapeDtypeStruct((s_len, ch), F32),
        in_specs=[col(s_len), col(CONV_WIDTH), col(1)], out_specs=col(s_len),
        scratch_shapes=[pltpu.VMEM((s_len + 2 * CONV_PAD, CONV_CB), F32)],
        compiler_params=_params(("parallel",), VMEM_BIG),
    )(xpre, cw, cb)


def _conv_bwd(xpre, da, db, cw, cb):
    s_len, ch = xpre.shape
    nchunk = s_len // CONV_R
    half = CONV_WIDTH // 2

    def body(x_ref, da_ref, db_ref, w_ref, b_ref, dx_ref, gw_ref, gb_ref, xp, dcp):
        zero = jnp.zeros((CONV_PAD, CONV_CB), F32)
        for buf in (xp, dcp):
            buf[0:CONV_PAD, :] = zero
            buf[s_len + CONV_PAD:s_len + 2 * CONV_PAD, :] = zero

        def fill(ci, carry):
            base = pl.multiple_of(ci * CONV_R, CONV_R)
            xp[pl.ds(base + CONV_PAD, CONV_R), :] = x_ref[pl.ds(base, CONV_R), :]
            return carry

        lax.fori_loop(0, nchunk, fill, 0)
        w = w_ref[...]
        b = b_ref[...]

        def first(ci, carry):
            base = pl.multiple_of(ci * CONV_R, CONV_R)
            win = xp[pl.ds(base, CONV_R + 2 * CONV_PAD), :]
            u = _conv_window_sum(win, w, CONV_PAD - half, 1) + b
            sig = jax.nn.sigmoid(u)
            dc = (da_ref[pl.ds(base, CONV_R), :] + db_ref[pl.ds(base, CONV_R), :]) * (sig * (1.0 + u * (1.0 - sig)))
            dcp[pl.ds(base + CONV_PAD, CONV_R), :] = dc
            gb = carry[0] + jnp.sum(dc, axis=0, keepdims=True)
            gws = [carry[1 + j] + jnp.sum(dc * win[CONV_PAD - half + j:CONV_PAD - half + j + CONV_R, :],
                                          axis=0, keepdims=True) for j in range(CONV_WIDTH)]
            return (gb, *gws)

        z1 = jnp.zeros((1, CONV_CB), F32)
        sums = lax.fori_loop(0, nchunk, first, (z1,) * (1 + CONV_WIDTH))
        gb_ref[...] = sums[0]
        for j in range(CONV_WIDTH):
            gw_ref[j:j + 1, :] = sums[1 + j]

        def second(ci, carry):
            base = pl.multiple_of(ci * CONV_R, CONV_R)
            win = dcp[pl.ds(base, CONV_R + 2 * CONV_PAD), :]
            dx_ref[pl.ds(base, CONV_R), :] = _conv_window_sum(win, w, CONV_PAD + half, -1).astype(dx_ref.dtype)
            return carry

        lax.fori_loop(0, nchunk, second, 0)

    col = lambda r: pl.BlockSpec((r, CONV_CB), lambda j: (0, j))
    return pl.pallas_call(
        body, name="conv_bwd", grid=(ch // CONV_CB,),
        out_shape=[jax.ShapeDtypeStruct((s_len, ch), BF16), jax.ShapeDtypeStruct((CONV_WIDTH, ch), F32),
                   jax.ShapeDtypeStruct((1, ch), F32)],
        in_specs=[col(s_len), col(s_len), col(s_len), col(CONV_WIDTH), col(1)],
        out_specs=[col(s_len), col(CONV_WIDTH), col(1)],
        scratch_shapes=[pltpu.VMEM((s_len + 2 * CONV_PAD, CONV_CB), F32)] * 2,
        compiler_params=_params(("parallel",), VMEM_BIG),
    )(xpre, da, db, cw, cb)


SSD_GW = 256
SSD_N = 128


@jax.custom_vjp
def _expand(x, e):
    x1 = x.astype(BF16)
    r1 = x - x1.astype(F32)
    x2 = r1.astype(BF16)
    x3 = (r1 - x2.astype(F32)).astype(BF16)
    eb = e.astype(BF16)
    return _dg(x1, eb, 1, 0) + _dg(x2, eb, 1, 0) + _dg(x3, eb, 1, 0)


def _expand_fwd(x, e):
    return _expand(x, e), e


def _expand_bwd(e, g):
    g1 = g.astype(BF16)
    g2 = (g - g1.astype(F32)).astype(BF16)
    eb = e.astype(BF16)
    return _dg(g1, eb, 1, 1) + _dg(g2, eb, 1, 1), jnp.zeros_like(e)


_expand.defvjp(_expand_fwd, _expand_bwd)


def _ssd_mask(dirn):
    ri = lax.broadcasted_iota(jnp.int32, (CHUNK, CHUNK), 0)
    cj = lax.broadcasted_iota(jnp.int32, (CHUNK, CHUNK), 1)
    return (cj <= ri) if dirn == 0 else (cj >= ri)


def _ssd_rowsel(dirn):
    last = CHUNK - 1 if dirn == 0 else 0
    return (lax.broadcasted_iota(jnp.int32, (CHUNK, 1), 0) == last).astype(F32)


def _ssd_chunk_pre(dirn):
    nh = 2 * SSD_HEADS

    def f(dt, alog):
        da = dt * (-jnp.exp(alog))
        cum = _hnn(_ssd_mask(dirn).astype(F32), da)
        tot = jnp.sum(cum * _ssd_rowsel(dirn), axis=0, keepdims=True)
        hh = lax.broadcasted_iota(jnp.int32, (nh, SSD_HEADS * HEAD_DIM), 0)
        jj = lax.broadcasted_iota(jnp.int32, (nh, SSD_HEADS * HEAD_DIM), 1)
        expand = (hh == dirn * SSD_HEADS + jj // HEAD_DIM).astype(F32)
        return cum, cum.T, _expand(dt, expand), _expand(jnp.exp(tot - cum), expand), _expand(jnp.exp(cum), expand)

    return f


def _ssd_group_fn(g, dirn):
    nh = 2 * SSD_HEADS

    def f(xs, bm, cm, st, cum, cum_t, dt_e, w_e, ce_e):
        mask = _ssd_mask(dirn)
        xdt = xs * dt_e
        cd_e = jnp.sum(ce_e * _ssd_rowsel(dirn), axis=0, keepdims=True)
        cb = _bnt(cm, bm)
        y = _bnn(cm, st) * ce_e
        lane_head = lax.broadcasted_iota(jnp.int32, (1, SSD_GW), 1) // HEAD_DIM
        for j in range(4):
            hidx = dirn * SSD_HEADS + 4 * g + j
            col = jnp.sum(cum * (lax.broadcasted_iota(jnp.int32, (1, nh), 1) == hidx).astype(F32), axis=1, keepdims=True)
            row = jnp.sum(cum_t * (lax.broadcasted_iota(jnp.int32, (nh, 1), 0) == hidx).astype(F32), axis=0, keepdims=True)
            dec = jnp.exp(jnp.where(mask, col - row, NEG_BIG))
            y = y + _bnn(cb * dec, xdt) * (lane_head == j).astype(F32)
        st_out = st * cd_e + _btn(bm, xdt * w_e)
        return y, st_out

    return f


def _ssd_in_specs(kk):
    ln = CHUNK
    return [pl.BlockSpec((ln, 2048), lambda i: (kk(i), 0)),
            pl.BlockSpec((ln, 1024), lambda i: (kk(i), 2)),
            pl.BlockSpec((ln, 1024), lambda i: (kk(i), 3)),
            pl.BlockSpec((ln, 2 * SSD_HEADS), lambda i: (kk(i), 0)),
            pl.BlockSpec((1, 2 * SSD_HEADS), lambda i: (0, 0))]


def _ssd_fwd(xbc, dt, alog, dirn):
    s_len = xbc.shape[0]
    nc = s_len // CHUNK
    kk = (lambda i: i) if dirn == 0 else (lambda i: nc - 1 - i)

    def body(x_ref, b_ref, c_ref, dt_ref, al_ref, y_ref, sts_ref, st):
        @pl.when(pl.program_id(0) == 0)
        def _():
            st[...] = jnp.zeros_like(st)

        sts_ref[0] = st[...]
        cum, cum_t, dt_e, w_e, ce_e = _ssd_chunk_pre(dirn)(dt_ref[...], al_ref[...])
        for g in range(SSD_GROUPS):
            xc = slice(g * SSD_GW, (g + 1) * SSD_GW)
            gc = slice(g * SSD_N, (g + 1) * SSD_N)
            y, st_new = _ssd_group_fn(g, dirn)(x_ref[:, xc], b_ref[:, gc], c_ref[:, gc], st[:, xc], cum, cum_t,
                                               dt_e[:, xc], w_e[:, xc], ce_e[:, xc])
            y_ref[:, xc] = y
            st[:, xc] = st_new

    return pl.pallas_call(
        body, name=f"ssd_fwd_d{dirn}", grid=(nc,),
        out_shape=[jax.ShapeDtypeStruct((s_len, 2048), F32), jax.ShapeDtypeStruct((nc, SSD_N, 2048), F32)],
        in_specs=_ssd_in_specs(kk),
        out_specs=[pl.BlockSpec((CHUNK, 2048), lambda i: (kk(i), 0)),
                   pl.BlockSpec((1, SSD_N, 2048), lambda i: (kk(i), 0, 0))],
        scratch_shapes=[pltpu.VMEM((SSD_N, 2048), F32)],
        compiler_params=_params(("arbitrary",), VMEM_BIG),
    )(xbc, xbc, xbc, dt, alog)


def _ssd_bwd(xbc, dt, alog, states, dy, d_e, dirn):
    s_len = xbc.shape[0]
    nc = s_len // CHUNK
    kk = (lambda i: nc - 1 - i) if dirn == 0 else (lambda i: i)

    def body(x_ref, b_ref, c_ref, dt_ref, al_ref, sts_ref, dy_ref, de_ref, dx_ref, ddt_ref, dal_ref, dst):
        @pl.when(pl.program_id(0) == 0)
        def _():
            dst[...] = jnp.zeros_like(dst)
            dal_ref[...] = jnp.zeros_like(dal_ref)

        (cum, cum_t, dt_e, w_e, ce_e), pre_vjp = jax.vjp(_ssd_chunk_pre(dirn), dt_ref[...], al_ref[...])
        dcum = jnp.zeros_like(cum)
        dcum_t = jnp.zeros_like(cum_t)
        d_dt_e, d_w_e, d_ce_e = [], [], []
        for g in range(SSD_GROUPS):
            xc = slice(g * SSD_GW, (g + 1) * SSD_GW)
            gc = slice(g * SSD_N, (g + 1) * SSD_N)
            _, vjp = jax.vjp(_ssd_group_fn(g, dirn), x_ref[:, xc], b_ref[:, gc], c_ref[:, gc], sts_ref[0, :, xc], cum, cum_t,
                             dt_e[:, xc], w_e[:, xc], ce_e[:, xc])
            dyg = dy_ref[:, xc]
            dxs, dbm, dcm, dst_g, dcum_g, dcum_t_g, ddte_g, dwe_g, dcee_g = vjp((dyg, dst[:, xc]))
            if dirn == 0:
                dxs = dxs + dyg * de_ref[:, xc]
            dx_ref[:, xc] = dxs
            dx_ref[:, 2048 + g * SSD_N:2048 + (g + 1) * SSD_N] = dbm
            dx_ref[:, 3072 + g * SSD_N:3072 + (g + 1) * SSD_N] = dcm
            dst[:, xc] = dst_g
            dcum = dcum + dcum_g
            dcum_t = dcum_t + dcum_t_g
            d_dt_e.append(ddte_g)
            d_w_e.append(dwe_g)
            d_ce_e.append(dcee_g)
        ddt, dal = pre_vjp((dcum, dcum_t, jnp.concatenate(d_dt_e, axis=1), jnp.concatenate(d_w_e, axis=1),
                            jnp.concatenate(d_ce_e, axis=1)))
        ddt_ref[...] = ddt
        dal_ref[...] += dal

    return pl.pallas_call(
        body, name=f"ssd_bwd_d{dirn}", grid=(nc,),
        out_shape=[jax.ShapeDtypeStruct((s_len, 4096), F32), jax.ShapeDtypeStruct((s_len, 2 * SSD_HEADS), F32),
                   jax.ShapeDtypeStruct((1, 2 * SSD_HEADS), F32)],
        in_specs=_ssd_in_specs(kk) + [pl.BlockSpec((1, SSD_N, 2048), lambda i: (kk(i), 0, 0)),
                                      pl.BlockSpec((CHUNK, 2048), lambda i: (kk(i), 0)),
                                      pl.BlockSpec((1, 2048), lambda i: (0, 0))],
        out_specs=[pl.BlockSpec((CHUNK, 4096), lambda i: (kk(i), 0)),
                   pl.BlockSpec((CHUNK, 2 * SSD_HEADS), lambda i: (kk(i), 0)),
                   pl.BlockSpec((1, 2 * SSD_HEADS), lambda i: (0, 0))],
        scratch_shapes=[pltpu.VMEM((SSD_N, 2048), F32)],
        compiler_params=_params(("arbitrary",), VMEM_BIG),
    )(xbc, xbc, xbc, dt, alog, states, dy, d_e)


def _gate_norm_fn(yf, yb, xs, z, d_e, nw):
    yg = (yf + yb + xs * d_e) * _silu(z)
    return yg * lax.rsqrt(jnp.mean(yg * yg, axis=-1, keepdims=True) + NORM_EPS) * nw


def _gate_norm_fwd(yf, yb, xbc, z, d_e, nw):
    (u,), _ = _rowwise("ssd_gate_norm", lambda *a: ([_gate_norm_fn(*a)], []),
                       [yf, yb, (xbc, 2048, 0), z], [d_e, nw], [(2048, BF16)], [], 256)
    return u


def _gate_norm_bwd(du, yf, yb, xbc, z, d_e, nw):
    def fn(du, yf, yb, xs, z, d_e, nw):
        _, vjp = jax.vjp(_gate_norm_fn, yf, yb, xs, z, d_e, nw)
        dyf, _, _, dz, dde, dnw = vjp(du)
        hh = lax.broadcasted_iota(jnp.int32, (2048, SSD_HEADS), 0) // HEAD_DIM
        jj = lax.broadcasted_iota(jnp.int32, (2048, SSD_HEADS), 1)
        return [dyf, dz], [dnw, _hnn(jnp.broadcast_to(dde, (8, 2048)), (hh == jj).astype(F32))[0:1]]

    (dys, dz), (g_nw, g_d) = _rowwise("ssd_gate_norm_bwd", fn, [du, yf, yb, (xbc, 2048, 0), z], [d_e, nw],
                                      [(2048, F32), (2048, BF16)], [(1, 2048), (1, SSD_HEADS)], 128)
    return dys, dz, g_nw, g_d


def _loss_bwd(x1, y1, tgt, gate, fnw):
    dm = x1.shape[1]

    def fn(x1, y1, tgt, gate, fnw):
        def head(x2, fnw):
            yf = (x2 * lax.rsqrt(jnp.mean(x2 * x2, axis=-1, keepdims=True) + NORM_EPS)) * fnw
            err = yf - tgt
            return 0.5 * jnp.sum(jnp.mean(err * err, axis=-1, keepdims=True), axis=0, keepdims=True)

        x2 = x1 + gate * y1
        loss, vjp = jax.vjp(head, x2, fnw)
        dx2, dfnw = vjp(jnp.ones((1, 1), F32))
        return [dx2, gate * dx2], [dfnw, jnp.sum(dx2 * y1, axis=0, keepdims=True), jnp.broadcast_to(loss, (1, 128))]

    (dx2, dy1), (g_fnw, dgate, loss) = _rowwise("loss_bwd", fn, [x1, y1, tgt], [gate, fnw], [(dm, F32), (dm, BF16)],
                                                [(1, dm), (1, dm), (1, 128)], 256)
    return dx2, dy1, g_fnw, dgate, loss


def _gate_bwd(dx, y, gate):
    dm = dx.shape[1]
    (dy,), (dgate,) = _rowwise("gate_bwd", lambda dx, y, gate: ([gate * dx], [jnp.sum(dx * y, axis=0, keepdims=True)]),
                               [dx, y], [gate], [(dm, BF16)], [(1, dm)], 512)
    return dy, dgate


def _softplus_fwd(dt_raw, bias):
    (dt,), _ = _rowwise("dt_softplus", lambda r, b: ([jax.nn.softplus(r + b)], []), [dt_raw], [bias],
                        [(dt_raw.shape[1], F32)], [], 512)
    return dt


def _softplus_bwd(ddt_f, ddt_b, dt_raw, bias):
    def fn(df, db, r, b):
        g = (df + db) * jax.nn.sigmoid(r + b)
        return [g], [jnp.sum(g, axis=0, keepdims=True)]

    w = dt_raw.shape[1]
    (g,), (gb,) = _rowwise("dt_softplus_bwd", fn, [ddt_f, ddt_b, dt_raw], [bias], [(w, BF16)], [(1, w)], 512)
    return g, gb


def _whole(a):
    nd = len(a.shape)
    return pl.BlockSpec(a.shape, lambda *_: (0,) * nd)


def _mod_part(c_all, mod_w):
    nl, _, ncol = mod_w.shape
    nb = c_all.shape[0]

    def body(c_ref, w_ref, o_ref):
        cond = _silu(c_ref[...])
        for i in range(nl):
            o_ref[i * nb:(i + 1) * nb, :] = _nn(cond, w_ref[i])

    return pl.pallas_call(body, name="mod_part", out_shape=jax.ShapeDtypeStruct((nl * nb, ncol), F32),
                          compiler_params=_params(None, VMEM_BIG))(c_all, mod_w)


def _mod_finish(mod_nb, mod_b):
    def body(a_ref, b_ref, o_ref):
        o_ref[...] = a_ref[...] + b_ref[...]

    return pl.pallas_call(body, name="mod_finish", out_shape=jax.ShapeDtypeStruct(mod_b.shape, F32))(mod_nb, mod_b)


def _mod_grad(c_all, dmod_sh):
    nl, nb, ncol = dmod_sh.shape
    dm = c_all.shape[1]

    def body(c_ref, d_ref, o_ref):
        cond = _silu(c_ref[...])
        for i in range(nl):
            o_ref[i] = _tn(cond, d_ref[i])

    return pl.pallas_call(body, name="mod_grad", out_shape=jax.ShapeDtypeStruct((nl, dm, ncol), F32),
                          compiler_params=_params(None, VMEM_BIG))(c_all, dmod_sh)


PACK_ROWS = 16
PACK_COLS = 1024


def _pack_small(rows6, nw2, fnw, b64, a64, d32, extra=None):
    args = [rows6, nw2, fnw, b64, a64, d32] + ([extra] if extra is not None else [])

    def body(*refs):
        o_ref = refs[-1]
        o_ref[...] = jnp.zeros_like(o_ref)
        o_ref[0:6, :] = refs[0][...]
        o_ref[6:8, :] = refs[1][...]
        o_ref[8:9, :] = refs[2][...]
        o_ref[9:10, 0:64] = refs[3][...]
        o_ref[9:10, 64:128] = refs[4][...]
        o_ref[9:10, 128:160] = refs[5][...]
        if extra is not None:
            o_ref[9:10, 256:384] = refs[6][...]

    return pl.pallas_call(body, name="pack_small", out_shape=jax.ShapeDtypeStruct((PACK_ROWS, PACK_COLS), F32))(*args)


def _unpack_small(p):
    return (p[0:6].reshape(2, 3 * PACK_COLS), p[6:8], p[8], p[9, 0:64].reshape(1, 2, 32), p[9, 64:128].reshape(1, 2, 32),
            p[9, 128:160].reshape(1, 32))


def _pack_ssd_small(cw, cb, nw):
    def body(cw_ref, cb_ref, nw_ref, o_ref):
        o_ref[...] = jnp.zeros_like(o_ref)
        o_ref[0:5, :] = cw_ref[...]
        o_ref[5:6, :] = cb_ref[...]
        o_ref[6:7, 0:256] = nw_ref[...]

    return pl.pallas_call(body, name="pack_ssd_small", out_shape=jax.ShapeDtypeStruct((8, 512), F32))(cw, cb, nw)


def _adamw(name, w, parts, m, v, tr):
    r_, c_ = w.shape
    p_ = parts.shape[0]
    tr = min(tr, r_)
    assert r_ % tr == 0

    def body(w_ref, p_ref, m_ref, v_ref, g_ref, d_ref, m2_ref, v2_ref):
        g = p_ref[0].astype(F32)
        for s in range(1, p_):
            g = g + p_ref[s].astype(F32)
        m2 = ADAM_B1 * m_ref[...] + (1.0 - ADAM_B1) * g
        v2 = ADAM_B2 * v_ref[...] + (1.0 - ADAM_B2) * (g * g)
        m_hat = m2 / (1.0 - ADAM_B1 ** ADAM_STEP)
        v_hat = v2 / (1.0 - ADAM_B2 ** ADAM_STEP)
        g_ref[...] = g
        d_ref[...] = -ADAM_LR * (m_hat / (jnp.sqrt(v_hat) + ADAM_EPS) + ADAM_WD * w_ref[...])
        m2_ref[...] = m2
        v2_ref[...] = v2

    blk = pl.BlockSpec((tr, c_), lambda i: (i, 0))
    return pl.pallas_call(
        body, name=name, grid=(r_ // tr,), out_shape=[jax.ShapeDtypeStruct((r_, c_), F32)] * 4,
        in_specs=[blk, pl.BlockSpec((p_, tr, c_), lambda i: (0, i, 0)), blk, blk], out_specs=[blk] * 4,
        compiler_params=_params(("parallel",), VMEM_BIG),
    )(w, parts, m, v)


def _dev_index(p):
    return 4 * p[0] + 2 * p[1] + p[2]


def _all_gather(name, xs):
    n = len(xs)
    hbm = pl.BlockSpec(memory_space=pl.ANY)

    def body(*refs):
        x_refs, o_refs = refs[:n], refs[n:2 * n]
        send_sems, recv_sems, local_sems = refs[2 * n:]
        x, y, c = lax.axis_index("x"), lax.axis_index("y"), lax.axis_index("c")
        me, sibling = (x, y, c), (x, y, 1 - c)
        chips = [(1 - x, y), (x, 1 - y), (1 - x, 1 - y)]

        def copy(a, k, block, to, src=None):
            dst = o_refs[a].at[_dev_index(block)]
            return pltpu.make_async_remote_copy(
                src_ref=dst if src is None else src, dst_ref=dst, send_sem=send_sems.at[a, k],
                recv_sem=recv_sems.at[a, k], device_id=to, device_id_type=MESH)

        mine = [pltpu.make_async_copy(x_refs[a], o_refs[a].at[_dev_index(me)], local_sems.at[a]) for a in range(n)]
        for cp in mine:
            cp.start()
        first = []
        for a in range(n):
            first.append(copy(a, 0, me, sibling, src=x_refs[a]))
            first += [copy(a, 1 + j, me, (*chip, c), src=x_refs[a]) for j, chip in enumerate(chips)]
        for cp in first:
            cp.start()
        passed = []
        for j, chip in enumerate(chips):
            for a in range(n):
                copy(a, 1 + j, (*chip, c), me).wait_recv()
                cp = copy(a, 4 + j, (*chip, c), sibling)
                cp.start()
                passed.append(cp)
        for a in range(n):
            copy(a, 0, sibling, me).wait_recv()
            for j, chip in enumerate(chips):
                copy(a, 4 + j, (*chip, 1 - c), me).wait_recv()
        for cp in first + passed:
            cp.wait_send()
        for cp in mine:
            cp.wait()

    return pl.pallas_call(
        body, name=name, out_shape=[jax.ShapeDtypeStruct((NDEV, *x.shape), x.dtype) for x in xs],
        in_specs=[hbm] * n, out_specs=[hbm] * n,
        scratch_shapes=[pltpu.SemaphoreType.DMA((n, 7)), pltpu.SemaphoreType.DMA((n, 7)), pltpu.SemaphoreType.DMA((n,))],
    )(*xs)


def _all_to_all(name, xs):
    n = len(xs)
    hbm = pl.BlockSpec(memory_space=pl.ANY)

    def body(*refs):
        x_refs, o_refs = refs[:n], refs[n:2 * n]
        send_sems, recv_sems, local_sems = refs[2 * n:]
        x, y, c = lax.axis_index("x"), lax.axis_index("y"), lax.axis_index("c")
        me = (x, y, c)
        mine = [pltpu.make_async_copy(x_refs[a].at[_dev_index(me)], o_refs[a].at[_dev_index(me)], local_sems.at[a])
                for a in range(n)]
        for cp in mine:
            cp.start()
        copies = []
        for k in range(1, NDEV):
            peer = tuple(1 - v if (k >> b) & 1 else v for v, b in zip(me, (2, 1, 0)))
            for a in range(n):
                copies.append(pltpu.make_async_remote_copy(
                    src_ref=x_refs[a].at[_dev_index(peer)], dst_ref=o_refs[a].at[_dev_index(me)],
                    send_sem=send_sems.at[a, k - 1], recv_sem=recv_sems.at[a, k - 1], device_id=peer, device_id_type=MESH))
        for cp in copies:
            cp.start()
        for cp in copies:
            cp.wait()
        for cp in mine:
            cp.wait()

    return pl.pallas_call(
        body, name=name, out_shape=[jax.ShapeDtypeStruct(x.shape, x.dtype) for x in xs],
        in_specs=[hbm] * n, out_specs=[hbm] * n,
        scratch_shapes=[pltpu.SemaphoreType.DMA((n, 7)), pltpu.SemaphoreType.DMA((n, 7)), pltpu.SemaphoreType.DMA((n,))],
    )(*xs)


def kernel(x, c, positions, norm_w, mod_w, mod_b, attn_w_in, attn_w_out, ssd_w_in, ssd_conv_w, ssd_conv_b, ssd_dt_bias, ssd_a_log, ssd_d, ssd_norm_w, ssd_w_out, final_norm_w, loss_target, m_norm_w, m_mod_w, m_mod_b, m_attn_w_in, m_attn_w_out, m_ssd_w_in, m_ssd_conv_w, m_ssd_conv_b, m_ssd_dt_bias, m_ssd_a_log, m_ssd_d, m_ssd_norm_w, m_ssd_w_out, m_final_norm_w, v_norm_w, v_mod_w, v_mod_b, v_attn_w_in, v_attn_w_out, v_ssd_w_in, v_ssd_conv_w, v_ssd_conv_b, v_ssd_dt_bias, v_ssd_a_log, v_ssd_d, v_ssd_norm_w, v_ssd_w_out, v_final_norm_w):
    s_len, dm = x.shape[1], x.shape[2]
    me = 4 * lax.axis_index("x") + 2 * lax.axis_index("y") + lax.axis_index("c")
    x0 = x.reshape(s_len, dm)
    tgt = loss_target.reshape(s_len, dm)
    aw = 3 * 512
    si = 2 * dm
    sxbc = 2 * si
    n_ssd_in = ssd_w_in.shape[2] * NDEV

    ssd_small = _pack_ssd_small(ssd_conv_w[0], ssd_conv_b, ssd_norm_w)
    g_ai, g_ao, g_si, g_so, g_small, c_all = _all_gather("gather_weights", [
        attn_w_in[0].astype(BF16), attn_w_out[0].astype(BF16), ssd_w_in[0].astype(BF16), ssd_w_out[0].astype(BF16),
        ssd_small, c])
    w_ai = g_ai.transpose(1, 0, 2).reshape(dm, 4 * aw)
    w_ao = g_ao.reshape(aw, dm)
    w_si = g_si.transpose(1, 0, 2).reshape(dm, n_ssd_in)
    w_z, w_xbc, w_dt = w_si[:, :si], w_si[:, si:si + sxbc], w_si[:, si + sxbc:]
    w_so = g_so.reshape(si, dm)
    conv_w = g_small[:, 0:CONV_WIDTH, :].transpose(1, 0, 2).reshape(CONV_WIDTH, sxbc)
    conv_b = g_small[:, 5, :].reshape(1, sxbc)
    snw = g_small[:, 6, 0:si // NDEV].reshape(1, si)
    c_all = c_all.reshape(NDEV, dm)

    part = _mod_part(c_all, mod_w)
    (part_all,) = _all_gather("gather_mod", [part])
    mod_nb = jnp.stack([lax.dynamic_index_in_dim(part_all, i * NDEV + me, axis=1, keepdims=False).reshape(3 * dm)
                        for i in range(2)])
    mod = _mod_finish(mod_nb, mod_b)
    shift = [mod[i:i + 1, 0:dm] for i in range(2)]
    scale = [mod[i:i + 1, dm:2 * dm] for i in range(2)]
    gate = [mod[i:i + 1, 2 * dm:3 * dm] for i in range(2)]
    nw = [norm_w[i:i + 1] for i in range(2)]

    hn0 = _norm_mod_fwd("norm0", x0, nw[0], scale[0], shift[0])
    inv_freq = ROPE_THETA ** (-jnp.arange(0, ROT_DIM, 2, dtype=F32) / ROT_DIM)
    lane = jnp.arange(128) % HEAD_DIM
    inv_row = jnp.where(lane < ROT_DIM, inv_freq[lane % (ROT_DIM // 2)], 0.0).reshape(1, 128).astype(F32)
    tabs = _rope_tables(positions.reshape(s_len, 1), inv_row)
    qk = _matmul("proj_qk", hn0, w_ai, "nn", F32, MM_T, MM_T, dm, epilogue=_rot_fwd, mrows=tabs, n_out=2 * aw)
    v = _matmul("proj_v", hn0, w_ai, "nn", F32, MM_T, aw // 2, dm, b_noff=2 * aw, n_out=aw)
    z0 = _matmul("proj_z", hn0, w_ai, "nn", F32, MM_T, aw // 2, dm, b_noff=3 * aw, n_out=aw)
    att = [_attn_fwd(g, qk, v) for g in range(3)]
    os_, lses = [a[0] for a in att], [a[1] for a in att]
    a0, y0, x1 = _attn_out(os_, lses, z0, x0, gate[0], w_ao)

    hn1 = _norm_mod_fwd("norm1", x1, nw[1], scale[1], shift[1])
    z1 = _matmul("ssd_proj_z", hn1, w_z, "nn", F32, MM_T, MM_T, dm)
    xpre = _matmul("ssd_proj_xbc", hn1, w_xbc, "nn", F32, MM_T, MM_T, dm)
    dt_raw = _matmul("ssd_proj_dt", hn1, w_dt, "nn", F32, MM_T, 64, dm)
    xbc = _conv_fwd(xpre, conv_w, conv_b)
    dt_bias = ssd_dt_bias.reshape(1, 2 * SSD_HEADS)
    alog = ssd_a_log.reshape(1, 2 * SSD_HEADS)
    dt = _softplus_fwd(dt_raw, dt_bias)
    y_f, st_f = _ssd_fwd(xbc, dt, alog, 0)
    y_b, st_b = _ssd_fwd(xbc, dt, alog, 1)
    d_e = jnp.repeat(ssd_d.reshape(SSD_HEADS), HEAD_DIM).reshape(1, si)
    u = _gate_norm_fwd(y_f, y_b, xbc, z1, d_e, snw)
    y1 = _matmul("ssd_out", u, w_so, "nn", F32, MM_T, MM_T, si)

    fnw = final_norm_w.reshape(1, dm)
    dx2, dy1, g_fnw, dgate1, loss_part = _loss_bwd(x1, y1, tgt, gate[1], fnw)
    du = _matmul("ssd_out_dx", dy1, w_so, "nt", F32, MM_T, MM_T, dm)
    gw_so = _matmul("ssd_out_dw", u, dy1, "tn", BF16, MM_T, MM_T, MM_T)
    dys, dz1, g_snw, g_d = _gate_norm_bwd(du, y_f, y_b, xbc, z1, d_e, snw)
    dxbc_f, ddt_f, dalog_f = _ssd_bwd(xbc, dt, alog, st_f, dys, d_e, 0)
    dxbc_b, ddt_b, dalog_b = _ssd_bwd(xbc, dt, alog, st_b, dys, d_e, 1)
    dpre, g_cw, g_cb = _conv_bwd(xpre, dxbc_f, dxbc_b, conv_w, conv_b)
    ddt_raw, g_dtb = _softplus_bwd(ddt_f, ddt_b, dt_raw, dt_bias)
    dhn1 = [_matmul("ssd_proj_z_dx", dz1, w_z, "nt", F32, MM_T, MM_T, MM_T),
            _matmul("ssd_proj_xbc_dx", dpre, w_xbc, "nt", F32, MM_T, MM_T, MM_T),
            _matmul("ssd_proj_dt_dx", ddt_raw, w_dt, "nt", F32, MM_T, MM_T, 64)]
    gw_si = jnp.concatenate([_matmul("ssd_proj_z_dw", hn1, dz1, "tn", BF16, MM_T, MM_T, MM_T),
                             _matmul("ssd_proj_xbc_dw", hn1, dpre, "tn", BF16, MM_T, MM_T, MM_T),
                             _matmul("ssd_proj_dt_dw", hn1, ddt_raw, "tn", BF16, MM_T, 64, MM_T)], axis=1)
    dx1, g_nw1, dsc1, dsh1 = _norm_mod_bwd("norm1_bwd", x1, dhn1, dx2, nw[1], scale[1], shift[1])

    dy0, dgate0 = _gate_bwd(dx1, y0, gate[0])
    da0 = _matmul("attn_out_dx", dy0, w_ao, "nt", F32, MM_T, aw // 2, dm)
    gw_ao = _matmul("attn_out_dw", a0, dy0, "tn", BF16, aw // 2, MM_T, MM_T)
    dos, dls, dz0 = _mix_bwd(da0, os_, lses, z0)
    datt = [_attn_bwd(g, qk, v, os_[g], lses[g], dos[g], dls[g]) for g in range(3)]
    dqkv = _rot_pack_bwd([t[0] for t in datt], [t[1] for t in datt], [t[2] for t in datt], tabs)
    dhn0 = [_matmul("proj_qkv_dx", dqkv, w_ai, "nt", F32, MM_T, MM_T, aw, n_out=dm),
            _matmul("proj_z_dx", dz0, w_ai, "nt", F32, MM_T, MM_T, aw, b_koff=3 * aw, n_out=dm)]
    wcol = attn_w_in.shape[2]
    gw_ai = jnp.concatenate([
        _matmul("proj_qkv_dw", hn0, dqkv, "tn", BF16, MM_T, wcol, MM_T, out_blocks=3 * aw // wcol),
        _matmul("proj_z_dw", hn0, dz0, "tn", BF16, MM_T, wcol, MM_T, out_blocks=aw // wcol)], axis=0)
    dx0, g_nw0, dsc0, dsh0 = _norm_mod_bwd("norm0_bwd", x0, dhn0, dx1, nw[0], scale[0], shift[0])

    rows6 = jnp.concatenate([dsh0, dsc0, dgate0, dsh1, dsc1, dgate1], axis=0)
    small_g = _pack_small(rows6, jnp.concatenate([g_nw0, g_nw1], axis=0), g_fnw, g_dtb, dalog_f + dalog_b, g_d, loss_part)
    (small_all,) = _all_gather("gather_small_grads", [small_g])
    small_w = _pack_small(mod_b.reshape(6, dm), norm_w, fnw, dt_bias, alog, ssd_d)
    small_m = _pack_small(m_mod_b.reshape(6, dm), m_norm_w, m_final_norm_w.reshape(1, dm), m_ssd_dt_bias.reshape(1, 64),
                          m_ssd_a_log.reshape(1, 64), m_ssd_d)
    small_v = _pack_small(v_mod_b.reshape(6, dm), v_norm_w, v_final_norm_w.reshape(1, dm), v_ssd_dt_bias.reshape(1, 64),
                          v_ssd_a_log.reshape(1, 64), v_ssd_d)
    small_out = _adamw("adamw_small", small_w, small_all, small_m, small_v, PACK_ROWS)
    loss = small_out[0][9, 256]
    sg, sd, sm, sv = (_unpack_small(p) for p in small_out)

    ncol = mod_w.shape[2]
    dmod_all = small_all[:, 0:6, :].reshape(NDEV, 2, 3 * dm)
    dmod_sh = lax.dynamic_slice_in_dim(dmod_all, me * ncol, ncol, axis=2).transpose(1, 0, 2)
    g_modw = _mod_grad(c_all, dmod_sh).reshape(1, 2 * dm, ncol)
    modw_out = _adamw("adamw_mod_w", mod_w.reshape(2 * dm, ncol), g_modw, m_mod_w.reshape(2 * dm, ncol),
                      v_mod_w.reshape(2 * dm, ncol), 256)

    ssd_small_g = _pack_ssd_small_blocks(g_cw, g_cb, g_snw)
    r_ai, r_ao, r_si, r_so, r_small = _all_to_all("scatter_grads", [
        gw_ai, gw_ao.reshape(NDEV, aw // NDEV, dm), gw_si.reshape(dm, NDEV, n_ssd_in // NDEV).transpose(1, 0, 2),
        gw_so.reshape(NDEV, si // NDEV, dm), ssd_small_g])
    ai_out = _adamw("adamw_attn_w_in", attn_w_in[0], r_ai, m_attn_w_in[0], v_attn_w_in[0], 256)
    ao_out = _adamw("adamw_attn_w_out", attn_w_out[0], r_ao, m_attn_w_out[0], v_attn_w_out[0], 192)
    si_out = _adamw("adamw_ssd_w_in", ssd_w_in[0], r_si, m_ssd_w_in[0], v_ssd_w_in[0], 256)
    so_out = _adamw("adamw_ssd_w_out", ssd_w_out[0], r_so, m_ssd_w_out[0], v_ssd_w_out[0], 256)
    ssd_small_m = _pack_ssd_small(m_ssd_conv_w[0], m_ssd_conv_b, m_ssd_norm_w)
    ssd_small_v = _pack_ssd_small(v_ssd_conv_w[0], v_ssd_conv_b, v_ssd_norm_w)
    ss_out = _adamw("adamw_ssd_small", ssd_small, r_small, ssd_small_m, ssd_small_v, 8)

    def ssd_small_unpack(p):
        return p[0:5][None], p[5:6], p[6:7, 0:si // NDEV]

    cwo, cbo, nwo = zip(*(ssd_small_unpack(p) for p in ss_out))
    per_kind = []
    for k in range(4):
        s = (sg, sd, sm, sv)[k]
        per_kind.append([
            s[1], modw_out[k].reshape(mod_w.shape), s[0], ai_out[k][None], ao_out[k][None], si_out[k][None],
            cwo[k], cbo[k], s[3], s[4], s[5], nwo[k], so_out[k][None], s[2]])
    return (loss, dx0.reshape(x.shape), *per_kind[0], *per_kind[1], *per_kind[2], *per_kind[3])


def _pack_ssd_small_blocks(g_cw, g_cb, g_nw):
    nper = g_cw.shape[1] // NDEV
    nwper = g_nw.shape[1] // NDEV

    def body(cw_ref, cb_ref, nw_ref, o_ref):
        o_ref[...] = jnp.zeros_like(o_ref)
        for d in range(NDEV):
            o_ref[d, 0:5, :] = cw_ref[:, d * nper:(d + 1) * nper]
            o_ref[d, 5:6, :] = cb_ref[:, d * nper:(d + 1) * nper]
            o_ref[d, 6:7, 0:nwper] = nw_ref[:, d * nwper:(d + 1) * nwper]

    return pl.pallas_call(body, name="pack_ssd_small_grads", out_shape=jax.ShapeDtypeStruct((NDEV, 8, nper), F32))(g_cw, g_cb, g_nw)
```

```python
import functools
import math

import jax
import jax.numpy as jnp
from jax import lax
from jax.experimental import pallas as pl
from jax.experimental.pallas import tpu as pltpu

F32 = jnp.float32
BF16 = jnp.bfloat16
HI = lax.Precision.HIGHEST
MESH = pl.DeviceIdType.MESH
NDEV = 8

NORM_EPS = 1e-6
ROPE_THETA = 500000.0
ROT_DIM = 16
HEAD_DIM = 64
DILATIONS = (1, 4, 16)
BAND = 64
NEG_BIG = -1e30
CHUNK = 128
SSD_HEADS = 32
SSD_GROUPS = 8
CONV_WIDTH = 5

ADAM_LR = 0.001
ADAM_B1 = 0.9
ADAM_B2 = 0.999
ADAM_EPS = 1e-08
ADAM_WD = 0.01
ADAM_STEP = 10

VMEM_BIG = 56 * 1024 * 1024
MM_T = 1024


def _params(sem=None, vmem=None):
    kw = {}
    if sem is not None:
        kw["dimension_semantics"] = sem
    if vmem is not None:
        kw["vmem_limit_bytes"] = vmem
    return pltpu.CompilerParams(**kw)


def _dg(a, b, ca, cb, prec=None):
    return lax.dot_general(a, b, (((ca,), (cb,)), ((), ())), preferred_element_type=F32, precision=prec)


def _nn(a, b):
    return _dg(a.astype(BF16), b.astype(BF16), 1, 0)


def _nt(a, b):
    return _dg(a.astype(BF16), b.astype(BF16), 1, 1)


def _tn(a, b):
    return _dg(a.astype(BF16), b.astype(BF16), 0, 0)


def _hnn(a, b):
    return _dg(a, b, 1, 0, HI)


@jax.custom_vjp
def _bnn(a, b):
    return _nn(a, b)


_bnn.defvjp(lambda a, b: (_nn(a, b), (a, b)), lambda r, g: (_nt(g, r[1]), _tn(r[0], g)))


@jax.custom_vjp
def _bnt(a, b):
    return _nt(a, b)


_bnt.defvjp(lambda a, b: (_nt(a, b), (a, b)), lambda r, g: (_nn(g, r[1]), _tn(g, r[0])))


@jax.custom_vjp
def _btn(a, b):
    return _tn(a, b)


_btn.defvjp(lambda a, b: (_tn(a, b), (a, b)), lambda r, g: (_nt(r[1], g), _nn(r[0], g)))


def _silu(x):
    return x * jax.nn.sigmoid(x)


def _matmul(name, a, b, mode, out_dtype, tm, tn, tk, *, epilogue=None, tiled=(), mrows=(), ncols=(),
            b_noff=0, b_koff=0, n_out=None, out_blocks=None):
    if mode == "tn":
        K, M = a.shape
    else:
        M, K = a.shape
    N = n_out if n_out is not None else (b.shape[0] if mode == "nt" else b.shape[1])
    tm, tn, tk = min(tm, M), min(tn, N), min(tk, K)
    assert M % tm == 0 and N % tn == 0 and K % tk == 0, (name, M, N, K, tm, tn, tk)
    assert b_noff % tn == 0 and b_koff % tk == 0
    no, ko = b_noff // tn, b_koff // tk
    nk = K // tk
    if mode == "tn":
        a_spec = pl.BlockSpec((tk, tm), lambda i, j, k: (k, i))
    else:
        a_spec = pl.BlockSpec((tm, tk), lambda i, j, k: (i, k))
    if mode == "nt":
        b_spec = pl.BlockSpec((tn, tk), lambda i, j, k: (j + no, k + ko))
    else:
        b_spec = pl.BlockSpec((tk, tn), lambda i, j, k: (k + ko, j + no))
    specs = [a_spec, b_spec]
    specs += [pl.BlockSpec((tm, tn), lambda i, j, k: (i, j)) for _ in tiled]
    specs += [pl.BlockSpec((tm, r.shape[1]), lambda i, j, k: (i, 0)) for r in mrows]
    specs += [pl.BlockSpec((1, tn), lambda i, j, k: (0, j)) for _ in ncols]
    if out_blocks is None:
        out_shape = jax.ShapeDtypeStruct((M, N), out_dtype)
        out_spec = pl.BlockSpec((tm, tn), lambda i, j, k: (i, j))
    else:
        nper = N // out_blocks
        assert nper % tn == 0
        jb = nper // tn
        out_shape = jax.ShapeDtypeStruct((out_blocks, M, nper), out_dtype)
        out_spec = pl.BlockSpec((None, tm, tn), lambda i, j, k: (j // jb, i, j % jb))
    ne = len(tiled) + len(mrows) + len(ncols)
    dot = {"nn": _nn, "nt": _nt, "tn": _tn}[mode]

    def body(a_ref, b_ref, *rest):
        extras, o_ref = rest[:ne], rest[ne]

        def finish(acc):
            if epilogue is not None:
                acc = epilogue(acc, *[e[...] for e in extras])
            o_ref[...] = acc.astype(o_ref.dtype)

        if nk == 1:
            finish(dot(a_ref[...], b_ref[...]))
        else:
            acc_ref = rest[ne + 1]
            k = pl.program_id(2)

            @pl.when(k == 0)
            def _():
                acc_ref[...] = jnp.zeros_like(acc_ref)

            acc_ref[...] += dot(a_ref[...], b_ref[...])

            @pl.when(k == nk - 1)
            def _():
                finish(acc_ref[...])

    return pl.pallas_call(
        body, name=name, out_shape=out_shape, grid=(M // tm, N // tn, nk),
        in_specs=specs, out_specs=out_spec,
        scratch_shapes=[] if nk == 1 else [pltpu.VMEM((tm, tn), F32)],
        compiler_params=_params(("parallel", "parallel", "arbitrary"), VMEM_BIG),
    )(a, b, *tiled, *mrows, *ncols)


def _rowwise(name, fn, tiled, consts, outs, accs, ts):
    tl = [(t, t.shape[1], 0) if not isinstance(t, tuple) else t for t in tiled]
    s_len = tl[0][0].shape[0]
    assert s_len % ts == 0
    nt_, nc_, no_ = len(tl), len(consts), len(outs)

    def body(*refs):
        t_refs, c_refs = refs[:nt_], refs[nt_:nt_ + nc_]
        o_refs, a_refs = refs[nt_ + nc_:nt_ + nc_ + no_], refs[nt_ + nc_ + no_:]
        res_o, res_a = fn(*[r[...] for r in t_refs], *[r[...] for r in c_refs])
        for r, v in zip(o_refs, res_o, strict=True):
            r[...] = v.astype(r.dtype)
        if a_refs:
            @pl.when(pl.program_id(0) == 0)
            def _():
                for r in a_refs:
                    r[...] = jnp.zeros_like(r)

            for r, v in zip(a_refs, res_a, strict=True):
                r[...] += v

    in_specs = [pl.BlockSpec((ts, w), functools.partial(lambda i, cb: (i, cb), cb=cb)) for (_, w, cb) in tl]
    in_specs += [pl.BlockSpec(c.shape, lambda i: (0, 0)) for c in consts]
    out_specs = [pl.BlockSpec((ts, c), lambda i: (i, 0)) for (c, _) in outs]
    out_specs += [pl.BlockSpec(shp, lambda i: (0, 0)) for shp in accs]
    out_shape = [jax.ShapeDtypeStruct((s_len, c), dt) for (c, dt) in outs]
    out_shape += [jax.ShapeDtypeStruct(shp, F32) for shp in accs]
    res = pl.pallas_call(
        body, name=name, out_shape=out_shape, grid=(s_len // ts,), in_specs=in_specs, out_specs=out_specs,
        compiler_params=_params(("arbitrary",) if accs else ("parallel",), VMEM_BIG),
    )(*[t[0] for t in tl], *consts)
    return res[:no_], res[no_:]


def _norm_mod_fn(x, nw, sc, sh):
    r = lax.rsqrt(jnp.mean(x * x, axis=-1, keepdims=True) + NORM_EPS)
    return (x * r * nw) * (1.0 + sc) + sh


def _norm_mod_fwd(name, x, nw, sc, sh):
    (hn,), _ = _rowwise(name, lambda x, nw, sc, sh: ([_norm_mod_fn(x, nw, sc, sh)], []),
                        [x], [nw, sc, sh], [(x.shape[1], BF16)], [], 512)
    return hn


def _norm_mod_bwd(name, x, dhn_parts, dres, nw, sc, sh):
    n = len(dhn_parts)
    d = x.shape[1]

    def fn(x, *rest):
        dhn = rest[0]
        for p in rest[1:n]:
            dhn = dhn + p
        dres, nw, sc, sh = rest[n:]
        _, vjp = jax.vjp(_norm_mod_fn, x, nw, sc, sh)
        dx, dnw, dsc, dsh = vjp(dhn)
        return [dx + dres], [dnw, dsc, dsh]

    (dx,), (g_nw, dsc, dsh) = _rowwise(name, fn, [x, *dhn_parts, dres], [nw, sc, sh], [(d, F32)],
                                       [(1, d), (1, d), (1, d)], 256)
    return dx, g_nw, dsc, dsh


def _rope_tables(pos_col, inv_row):
    def fn(pos, inv):
        ang = pos.astype(F32) * inv
        e = lax.broadcasted_iota(jnp.int32, (1, 128), 1) % HEAD_DIM
        cos, sin = jnp.cos(ang), jnp.sin(ang)
        half = ROT_DIM // 2
        return [jnp.where(e < ROT_DIM, cos, 1.0), jnp.where(e < half, -sin, 0.0),
                jnp.where((e >= half) & (e < ROT_DIM), sin, 0.0)], []

    (c, sa, sb), _ = _rowwise("rope_tables", fn, [pos_col], [inv_row], [(128, F32)] * 3, [], 512)
    return c, sa, sb


def _rot_fwd(t, c, sa, sb):
    n = t.shape[1]
    rep = n // 128
    c, sa, sb = (jnp.tile(u, (1, rep)) for u in (c, sa, sb))
    return t * c + pltpu.roll(t, n - ROT_DIM // 2, 1) * sa + pltpu.roll(t, ROT_DIM // 2, 1) * sb


def _rot_bwd(g, c, sa, sb):
    n = g.shape[1]
    rep = n // 128
    c, sa, sb = (jnp.tile(u, (1, rep)) for u in (c, sa, sb))
    return g * c + pltpu.roll(g * sa, ROT_DIM // 2, 1) + pltpu.roll(g * sb, n - ROT_DIM // 2, 1)


ATT_TQ = 128
ATT_TK = ATT_TQ + 2 * BAND


def _attn_specs(g, s_len):
    def blk(off):
        return pl.BlockSpec((s_len, 128), functools.partial(lambda hp, off: (0, off + hp), off=off))

    return blk(4 * g), blk(12 + 4 * g), blk(4 * g), blk(0)


def _attn_tile_geometry(t, d, l):
    nts = l // ATT_TQ
    r = t // nts
    q0 = (t % nts) * ATT_TQ
    ws = jnp.clip(q0 - BAND, 0, l - ATT_TK)
    qpos = q0 + lax.broadcasted_iota(jnp.int32, (ATT_TQ, 1), 0)
    kpos = ws + lax.broadcasted_iota(jnp.int32, (1, ATT_TK), 1)
    valid = jnp.abs(kpos - qpos) <= BAND
    if d == 1:
        return pl.ds(pl.multiple_of(q0, ATT_TQ), ATT_TQ), pl.ds(pl.multiple_of(ws, BAND), ATT_TK), valid
    return pl.ds(r + d * q0, ATT_TQ, stride=d), pl.ds(r + d * ws, ATT_TK, stride=d), valid


def _attn_fwd(g, qk, v):
    s_len = qk.shape[0]
    d = DILATIONS[g]
    l = s_len // d
    assert l % ATT_TQ == 0 and l >= ATT_TK
    q_spec, k_spec, v_spec, o_spec = _attn_specs(g, s_len)
    scale = 1.0 / math.sqrt(HEAD_DIM)

    def body(q_ref, k_ref, v_ref, o_ref, lse_ref):
        lane = lax.broadcasted_iota(jnp.int32, (1, 128), 1)
        in_h = [lane < HEAD_DIM, lane >= HEAD_DIM]

        def tile(t, carry):
            rows, win, valid = _attn_tile_geometry(t, d, l)
            q = q_ref[rows, :].astype(BF16)
            k = k_ref[win, :].astype(BF16)
            vv = v_ref[win, :].astype(BF16)
            outs, lses = [], []
            for h in range(2):
                qm = jnp.where(in_h[h], q, jnp.zeros_like(q))
                s = jnp.where(valid, _nt(qm, k) * scale, NEG_BIG)
                m = jnp.max(s, axis=1, keepdims=True)
                p = jnp.exp(s - m)
                den = jnp.sum(p, axis=1, keepdims=True)
                outs.append(_nn(p, vv) / den)
                lses.append(m + jnp.log(den))
            o_ref[rows, :] = jnp.where(in_h[0], outs[0], outs[1])
            lse_ref[rows, :] = jnp.where(in_h[0], lses[0], lses[1])
            return carry

        lax.fori_loop(0, s_len // ATT_TQ, tile, 0, unroll=2)

    return pl.pallas_call(
        body, name=f"attn_fwd_g{g}", grid=(4,),
        out_shape=[jax.ShapeDtypeStruct((s_len, 512), F32)] * 2,
        in_specs=[q_spec, k_spec, v_spec], out_specs=[o_spec, o_spec],
        compiler_params=_params(("parallel",), VMEM_BIG),
    )(qk, qk, v)


def _attn_bwd(g, qk, v, o, lse, do, dlse):
    s_len = qk.shape[0]
    d = DILATIONS[g]
    l = s_len // d
    q_spec, k_spec, v_spec, o_spec = _attn_specs(g, s_len)
    scale = 1.0 / math.sqrt(HEAD_DIM)

    def body(q_ref, k_ref, v_ref, o_ref, lse_ref, do_ref, dlse_ref, dq_ref, dk_ref, dv_ref):
        lane = lax.broadcasted_iota(jnp.int32, (1, 128), 1)
        in_h = [lane < HEAD_DIM, lane >= HEAD_DIM]
        dk_ref[...] = jnp.zeros_like(dk_ref)
        dv_ref[...] = jnp.zeros_like(dv_ref)

        def tile(t, carry):
            rows, win, valid = _attn_tile_geometry(t, d, l)
            q, k, vv = q_ref[rows, :].astype(BF16), k_ref[win, :].astype(BF16), v_ref[win, :].astype(BF16)
            dout, lse_t, dlse_t = do_ref[rows, :], lse_ref[rows, :], dlse_ref[rows, :]
            od = dout * o_ref[rows, :]
            dqs, dks, dvs = [], [], []
            for h in range(2):
                c0 = h * HEAD_DIM
                qm = jnp.where(in_h[h], q, jnp.zeros_like(q))
                s = jnp.where(valid, _nt(qm, k) * scale, NEG_BIG)
                p = jnp.exp(s - lse_t[:, c0:c0 + 1])
                dom = jnp.where(in_h[h], dout, 0.0)
                dp = _nt(dom, vv)
                delta = jnp.sum(jnp.where(in_h[h], od, 0.0), axis=1, keepdims=True)
                ds = (p * (dp - delta + dlse_t[:, c0:c0 + 1]) * scale).astype(BF16)
                dqs.append(_nn(ds, k))
                dks.append(_tn(ds, q))
                dvs.append(_tn(p, dout))
            dq_ref[rows, :] = jnp.where(in_h[0], dqs[0], dqs[1])
            dk_ref[win, :] += jnp.where(in_h[0], dks[0], dks[1])
            dv_ref[win, :] += jnp.where(in_h[0], dvs[0], dvs[1])
            return carry

        lax.fori_loop(0, s_len // ATT_TQ, tile, 0, unroll=2)

    return pl.pallas_call(
        body, name=f"attn_bwd_g{g}", grid=(4,),
        out_shape=[jax.ShapeDtypeStruct((s_len, 512), F32)] * 3,
        in_specs=[q_spec, k_spec, v_spec, o_spec, o_spec, o_spec, o_spec], out_specs=[o_spec] * 3,
        compiler_params=_params(("parallel",), VMEM_BIG),
    )(qk, qk, v, o, lse, do, dlse)


def _mix_weights(ls):
    mx = jnp.maximum(jnp.maximum(ls[0], ls[1]), ls[2])
    es = [jnp.exp(x - mx) for x in ls]
    tot = es[0] + es[1] + es[2]
    return [e / tot for e in es]


def _attn_out(os_, lses, z, x, gate, w_out):
    s_len, dm = x.shape
    tm = 256
    wdt = 512

    def body(o0, o1, o2, l0, l1, l2, z_ref, x_ref, g_ref, w_ref, a_ref, y_ref, x1_ref):
        alphas = _mix_weights([l0[...], l1[...], l2[...]])
        y = jnp.zeros((tm, dm), F32)
        for g, o_ref in enumerate((o0, o1, o2)):
            a_g = (o_ref[...] * alphas[g] * _silu(z_ref[:, g * wdt:(g + 1) * wdt])).astype(BF16)
            a_ref[:, g * wdt:(g + 1) * wdt] = a_g
            y = y + _nn(a_g, w_ref[g * wdt:(g + 1) * wdt, :])
        y_ref[...] = y
        x1_ref[...] = x_ref[...] + g_ref[...] * y

    row = lambda c: pl.BlockSpec((tm, c), lambda i: (i, 0))
    return pl.pallas_call(
        body, name="attn_out", grid=(s_len // tm,),
        out_shape=[jax.ShapeDtypeStruct((s_len, 3 * wdt), BF16), jax.ShapeDtypeStruct((s_len, dm), F32),
                   jax.ShapeDtypeStruct((s_len, dm), F32)],
        in_specs=[row(wdt)] * 6 + [row(3 * wdt), row(dm), pl.BlockSpec((1, dm), lambda i: (0, 0)),
                                   pl.BlockSpec(w_out.shape, lambda i: (0, 0))],
        out_specs=[row(3 * wdt), row(dm), row(dm)],
        compiler_params=_params(("parallel",), VMEM_BIG),
    )(*os_, *lses, z, x, gate, w_out)


def _mix_bwd(da, os_, lses, z):
    wdt = 512

    def fn(da, o0, o1, o2, l0, l1, l2, z):
        os_t, ls = [o0, o1, o2], [l0, l1, l2]
        alphas = _mix_weights(ls)
        hi = lax.broadcasted_iota(jnp.int32, (wdt, wdt), 0) // HEAD_DIM
        hj = lax.broadcasted_iota(jnp.int32, (wdt, wdt), 1) // HEAD_DIM
        seg = (hi == hj).astype(F32)
        dos, dal, dzs = [], [], []
        for g in range(3):
            zg = z[:, g * wdt:(g + 1) * wdt]
            sig = jax.nn.sigmoid(zg)
            dag = da[:, g * wdt:(g + 1) * wdt]
            dmix = dag * zg * sig
            dzs.append(dag * os_t[g] * alphas[g] * (sig * (1.0 + zg * (1.0 - sig))))
            dos.append(dmix * alphas[g])
            dal.append(_hnn(dmix * os_t[g], seg))
        mean = alphas[0] * dal[0] + alphas[1] * dal[1] + alphas[2] * dal[2]
        dls = [alphas[g] * (dal[g] - mean) for g in range(3)]
        return dos + dls + [jnp.concatenate(dzs, axis=1)], []

    outs, _ = _rowwise("mix_bwd", fn, [da, *os_, *lses, z], [], [(wdt, F32)] * 6 + [(3 * wdt, BF16)], [], 256)
    return outs[:3], outs[3:6], outs[6]


def _rot_pack_bwd(dqs, dks, dvs, tabs):
    wdt = 512

    def fn(*args):
        grads, (c, sa, sb) = args[:9], args[9:]
        cols = [_rot_bwd(gq, c, sa, sb) for gq in grads[:6]] + list(grads[6:])
        return [jnp.concatenate(cols, axis=1)], []

    (out,), _ = _rowwise("rot_pack_bwd", fn, [*dqs, *dks, *dvs, *tabs], [], [(9 * wdt, BF16)], [], 256)
    return out


CONV_CB = 256
CONV_R = 128
CONV_PAD = 8


def _conv_window_sum(win, w, off, sign):
    acc = None
    for j in range(CONV_WIDTH):
        o = off + sign * j
        term = win[o:o + CONV_R, :] * w[j:j + 1, :]
        acc = term if acc is None else acc + term
    return acc


def _conv_fwd(xpre, cw, cb):
    s_len, ch = xpre.shape
    nchunk = s_len // CONV_R

    def body(x_ref, w_ref, b_ref, o_ref, xp):
        zero = jnp.zeros((CONV_PAD, CONV_CB), F32)
        xp[0:CONV_PAD, :] = zero
        xp[s_len + CONV_PAD:s_len + 2 * CONV_PAD, :] = zero

        def fill(ci, carry):
            base = pl.multiple_of(ci * CONV_R, CONV_R)
            xp[pl.ds(base + CONV_PAD, CONV_R), :] = x_ref[pl.ds(base, CONV_R), :]
            return carry

        lax.fori_loop(0, nchunk, fill, 0)
        w = w_ref[...]
        b = b_ref[...]

        def chunk(ci, carry):
            base = pl.multiple_of(ci * CONV_R, CONV_R)
            win = xp[pl.ds(base, CONV_R + 2 * CONV_PAD), :]
            u = _conv_window_sum(win, w, CONV_PAD - CONV_WIDTH // 2, 1) + b
            o_ref[pl.ds(base, CONV_R), :] = _silu(u)
            return carry

        lax.fori_loop(0, nchunk, chunk, 0)

    col = lambda r: pl.BlockSpec((r, CONV_CB), lambda j: (0, j))
    return pl.pallas_call(
        body, name="conv_fwd", grid=(ch // CONV_CB,), out_shape=jax.ShapeDtypeStruct((s_len, ch), F32),
        in_specs=[col(s_len), col(CONV_WIDTH), col(1)], out_specs=col(s_len),
        scratch_shapes=[pltpu.VMEM((s_len + 2 * CONV_PAD, CONV_CB), F32)],
        compiler_params=_params(("parallel",), VMEM_BIG),
    )(xpre, cw, cb)


def _conv_bwd(xpre, da, db, cw, cb):
    s_len, ch = xpre.shape
    nchunk = s_len // CONV_R
    half = CONV_WIDTH // 2

    def body(x_ref, da_ref, db_ref, w_ref, b_ref, dx_ref, gw_ref, gb_ref, xp, dcp):
        zero = jnp.zeros((CONV_PAD, CONV_CB), F32)
        for buf in (xp, dcp):
            buf[0:CONV_PAD, :] = zero
            buf[s_len + CONV_PAD:s_len + 2 * CONV_PAD, :] = zero

        def fill(ci, carry):
            base = pl.multiple_of(ci * CONV_R, CONV_R)
            xp[pl.ds(base + CONV_PAD, CONV_R), :] = x_ref[pl.ds(base, CONV_R), :]
            return carry

        lax.fori_loop(0, nchunk, fill, 0)
        w = w_ref[...]
        b = b_ref[...]

        def first(ci, carry):
            base = pl.multiple_of(ci * CONV_R, CONV_R)
            win = xp[pl.ds(base, CONV_R + 2 * CONV_PAD), :]
            u = _conv_window_sum(win, w, CONV_PAD - half, 1) + b
            sig = jax.nn.sigmoid(u)
            dc = (da_ref[pl.ds(base, CONV_R), :] + db_ref[pl.ds(base, CONV_R), :]) * (sig * (1.0 + u * (1.0 - sig)))
            dcp[pl.ds(base + CONV_PAD, CONV_R), :] = dc
            gb = carry[0] + jnp.sum(dc, axis=0, keepdims=True)
            gws = [carry[1 + j] + jnp.sum(dc * win[CONV_PAD - half + j:CONV_PAD - half + j + CONV_R, :],
                                          axis=0, keepdims=True) for j in range(CONV_WIDTH)]
            return (gb, *gws)

        z1 = jnp.zeros((1, CONV_CB), F32)
        sums = lax.fori_loop(0, nchunk, first, (z1,) * (1 + CONV_WIDTH))
        gb_ref[...] = sums[0]
        for j in range(CONV_WIDTH):
            gw_ref[j:j + 1, :] = sums[1 + j]

        def second(ci, carry):
            base = pl.multiple_of(ci * CONV_R, CONV_R)
            win = dcp[pl.ds(base, CONV_R + 2 * CONV_PAD), :]
            dx_ref[pl.ds(base, CONV_R), :] = _conv_window_sum(win, w, CONV_PAD + half, -1).astype(dx_ref.dtype)
            return carry

        lax.fori_loop(0, nchunk, second, 0)

    col = lambda r: pl.BlockSpec((r, CONV_CB), lambda j: (0, j))
    return pl.pallas_call(
        body, name="conv_bwd", grid=(ch // CONV_CB,),
        out_shape=[jax.ShapeDtypeStruct((s_len, ch), BF16), jax.ShapeDtypeStruct((CONV_WIDTH, ch), F32),
                   jax.ShapeDtypeStruct((1, ch), F32)],
        in_specs=[col(s_len), col(s_len), col(s_len), col(CONV_WIDTH), col(1)],
        out_specs=[col(s_len), col(CONV_WIDTH), col(1)],
        scratch_shapes=[pltpu.VMEM((s_len + 2 * CONV_PAD, CONV_CB), F32)] * 2,
        compiler_params=_params(("parallel",), VMEM_BIG),
    )(xpre, da, db, cw, cb)


SSD_GW = 256
SSD_N = 128


@jax.custom_vjp
def _expand(x, e):
    x1 = x.astype(BF16)
    r1 = x - x1.astype(F32)
    x2 = r1.astype(BF16)
    x3 = (r1 - x2.astype(F32)).astype(BF16)
    eb = e.astype(BF16)
    return _dg(x1, eb, 1, 0) + _dg(x2, eb, 1, 0) + _dg(x3, eb, 1, 0)


def _expand_fwd(x, e):
    return _expand(x, e), e


def _expand_bwd(e, g):
    g1 = g.astype(BF16)
    g2 = (g - g1.astype(F32)).astype(BF16)
    eb = e.astype(BF16)
    return _dg(g1, eb, 1, 1) + _dg(g2, eb, 1, 1), jnp.zeros_like(e)


_expand.defvjp(_expand_fwd, _expand_bwd)


def _ssd_mask(dirn):
    ri = lax.broadcasted_iota(jnp.int32, (CHUNK, CHUNK), 0)
    cj = lax.broadcasted_iota(jnp.int32, (CHUNK, CHUNK), 1)
    return (cj <= ri) if dirn == 0 else (cj >= ri)


def _ssd_rowsel(dirn):
    last = CHUNK - 1 if dirn == 0 else 0
    return (lax.broadcasted_iota(jnp.int32, (CHUNK, 1), 0) == last).astype(F32)


def _ssd_chunk_pre(dirn):
    nh = 2 * SSD_HEADS

    def f(dt, alog):
        da = dt * (-jnp.exp(alog))
        cum = _hnn(_ssd_mask(dirn).astype(F32), da)
        tot = jnp.sum(cum * _ssd_rowsel(dirn), axis=0, keepdims=True)
        hh = lax.broadcasted_iota(jnp.int32, (nh, SSD_HEADS * HEAD_DIM), 0)
        jj = lax.broadcasted_iota(jnp.int32, (nh, SSD_HEADS * HEAD_DIM), 1)
        expand = (hh == dirn * SSD_HEADS + jj // HEAD_DIM).astype(F32)
        return cum, cum.T, _expand(dt, expand), _expand(jnp.exp(tot - cum), expand), _expand(jnp.exp(cum), expand)

    return f


def _ssd_group_fn(g, dirn):
    nh = 2 * SSD_HEADS

    def f(xs, bm, cm, st, cum, cum_t, dt_e, w_e, ce_e):
        mask = _ssd_mask(dirn)
        xdt = xs * dt_e
        cd_e = jnp.sum(ce_e * _ssd_rowsel(dirn), axis=0, keepdims=True)
        cb = _bnt(cm, bm)
        y = _bnn(cm, st) * ce_e
        lane_head = lax.broadcasted_iota(jnp.int32, (1, SSD_GW), 1) // HEAD_DIM
        for j in range(4):
            hidx = dirn * SSD_HEADS + 4 * g + j
            col = jnp.sum(cum * (lax.broadcasted_iota(jnp.int32, (1, nh), 1) == hidx).astype(F32), axis=1, keepdims=True)
            row = jnp.sum(cum_t * (lax.broadcasted_iota(jnp.int32, (nh, 1), 0) == hidx).astype(F32), axis=0, keepdims=True)
            dec = jnp.exp(jnp.where(mask, col - row, NEG_BIG))
            y = y + _bnn(cb * dec, xdt) * (lane_head == j).astype(F32)
        st_out = st * cd_e + _btn(bm, xdt * w_e)
        return y, st_out

    return f


def _ssd_in_specs(kk):
    ln = CHUNK
    return [pl.BlockSpec((ln, 2048), lambda i: (kk(i), 0)),
            pl.BlockSpec((ln, 1024), lambda i: (kk(i), 2)),
            pl.BlockSpec((ln, 1024), lambda i: (kk(i), 3)),
            pl.BlockSpec((ln, 2 * SSD_HEADS), lambda i: (kk(i), 0)),
            pl.BlockSpec((1, 2 * SSD_HEADS), lambda i: (0, 0))]


def _ssd_fwd(xbc, dt, alog, dirn):
    s_len = xbc.shape[0]
    nc = s_len // CHUNK
    kk = (lambda i: i) if dirn == 0 else (lambda i: nc - 1 - i)

    def body(x_ref, b_ref, c_ref, dt_ref, al_ref, y_ref, sts_ref, st):
        @pl.when(pl.program_id(0) == 0)
        def _():
            st[...] = jnp.zeros_like(st)

        sts_ref[0] = st[...]
        cum, cum_t, dt_e, w_e, ce_e = _ssd_chunk_pre(dirn)(dt_ref[...], al_ref[...])
        for g in range(SSD_GROUPS):
            xc = slice(g * SSD_GW, (g + 1) * SSD_GW)
            gc = slice(g * SSD_N, (g + 1) * SSD_N)
            y, st_new = _ssd_group_fn(g, dirn)(x_ref[:, xc], b_ref[:, gc], c_ref[:, gc], st[:, xc], cum, cum_t,
                                               dt_e[:, xc], w_e[:, xc], ce_e[:, xc])
            y_ref[:, xc] = y
            st[:, xc] = st_new

    return pl.pallas_call(
        body, name=f"ssd_fwd_d{dirn}", grid=(nc,),
        out_shape=[jax.ShapeDtypeStruct((s_len, 2048), F32), jax.ShapeDtypeStruct((nc, SSD_N, 2048), F32)],
        in_specs=_ssd_in_specs(kk),
        out_specs=[pl.BlockSpec((CHUNK, 2048), lambda i: (kk(i), 0)),
                   pl.BlockSpec((1, SSD_N, 2048), lambda i: (kk(i), 0, 0))],
        scratch_shapes=[pltpu.VMEM((SSD_N, 2048), F32)],
        compiler_params=_params(("arbitrary",), VMEM_BIG),
    )(xbc, xbc, xbc, dt, alog)


def _ssd_bwd(xbc, dt, alog, states, dy, d_e, dirn):
    s_len = xbc.shape[0]
    nc = s_len // CHUNK
    kk = (lambda i: nc - 1 - i) if dirn == 0 else (lambda i: i)

    def body(x_ref, b_ref, c_ref, dt_ref, al_ref, sts_ref, dy_ref, de_ref, dx_ref, ddt_ref, dal_ref, dst):
        @pl.when(pl.program_id(0) == 0)
        def _():
            dst[...] = jnp.zeros_like(dst)
            dal_ref[...] = jnp.zeros_like(dal_ref)

        (cum, cum_t, dt_e, w_e, ce_e), pre_vjp = jax.vjp(_ssd_chunk_pre(dirn), dt_ref[...], al_ref[...])
        dcum = jnp.zeros_like(cum)
        dcum_t = jnp.zeros_like(cum_t)
        d_dt_e, d_w_e, d_ce_e = [], [], []
        for g in range(SSD_GROUPS):
            xc = slice(g * SSD_GW, (g + 1) * SSD_GW)
            gc = slice(g * SSD_N, (g + 1) * SSD_N)
            _, vjp = jax.vjp(_ssd_group_fn(g, dirn), x_ref[:, xc], b_ref[:, gc], c_ref[:, gc], sts_ref[0, :, xc], cum, cum_t,
                             dt_e[:, xc], w_e[:, xc], ce_e[:, xc])
            dyg = dy_ref[:, xc]
            dxs, dbm, dcm, dst_g, dcum_g, dcum_t_g, ddte_g, dwe_g, dcee_g = vjp((dyg, dst[:, xc]))
            if dirn == 0:
                dxs = dxs + dyg * de_ref[:, xc]
            dx_ref[:, xc] = dxs
            dx_ref[:, 2048 + g * SSD_N:2048 + (g + 1) * SSD_N] = dbm
            dx_ref[:, 3072 + g * SSD_N:3072 + (g + 1) * SSD_N] = dcm
            dst[:, xc] = dst_g
            dcum = dcum + dcum_g
            dcum_t = dcum_t + dcum_t_g
            d_dt_e.append(ddte_g)
            d_w_e.append(dwe_g)
            d_ce_e.append(dcee_g)
        ddt, dal = pre_vjp((dcum, dcum_t, jnp.concatenate(d_dt_e, axis=1), jnp.concatenate(d_w_e, axis=1),
                            jnp.concatenate(d_ce_e, axis=1)))
        ddt_ref[...] = ddt
        dal_ref[...] += dal

    return pl.pallas_call(
        body, name=f"ssd_bwd_d{dirn}", grid=(nc,),
        out_shape=[jax.ShapeDtypeStruct((s_len, 4096), F32), jax.ShapeDtypeStruct((s_len, 2 * SSD_HEADS), F32),
                   jax.ShapeDtypeStruct((1, 2 * SSD_HEADS), F32)],
        in_specs=_ssd_in_specs(kk) + [pl.BlockSpec((1, SSD_N, 2048), lambda i: (kk(i), 0, 0)),
                                      pl.BlockSpec((CHUNK, 2048), lambda i: (kk(i), 0)),
                                      pl.BlockSpec((1, 2048), lambda i: (0, 0))],
        out_specs=[pl.BlockSpec((CHUNK, 4096), lambda i: (kk(i), 0)),
                   pl.BlockSpec((CHUNK, 2 * SSD_HEADS), lambda i: (kk(i), 0)),
                   pl.BlockSpec((1, 2 * SSD_HEADS), lambda i: (0, 0))],
        scratch_shapes=[pltpu.VMEM((SSD_N, 2048), F32)],
        compiler_params=_params(("arbitrary",), VMEM_BIG),
    )(xbc, xbc, xbc, dt, alog, states, dy, d_e)


def _gate_norm_fn(yf, yb, xs, z, d_e, nw):
    yg = (yf + yb + xs * d_e) * _silu(z)
    return yg * lax.rsqrt(jnp.mean(yg * yg, axis=-1, keepdims=True) + NORM_EPS) * nw


def _gate_norm_fwd(yf, yb, xbc, z, d_e, nw):
    (u,), _ = _rowwise("ssd_gate_norm", lambda *a: ([_gate_norm_fn(*a)], []),
                       [yf, yb, (xbc, 2048, 0), z], [d_e, nw], [(2048, BF16)], [], 256)
    return u


def _gate_norm_bwd(du, yf, yb, xbc, z, d_e, nw):
    def fn(du, yf, yb, xs, z, d_e, nw):
        _, vjp = jax.vjp(_gate_norm_fn, yf, yb, xs, z, d_e, nw)
        dyf, _, _, dz, dde, dnw = vjp(du)
        hh = lax.broadcasted_iota(jnp.int32, (2048, SSD_HEADS), 0) // HEAD_DIM
        jj = lax.broadcasted_iota(jnp.int32, (2048, SSD_HEADS), 1)
        return [dyf, dz], [dnw, _hnn(jnp.broadcast_to(dde, (8, 2048)), (hh == jj).astype(F32))[0:1]]

    (dys, dz), (g_nw, g_d) = _rowwise("ssd_gate_norm_bwd", fn, [du, yf, yb, (xbc, 2048, 0), z], [d_e, nw],
                                      [(2048, F32), (2048, BF16)], [(1, 2048), (1, SSD_HEADS)], 128)
    return dys, dz, g_nw, g_d


def _loss_bwd(x1, y1, tgt, gate, fnw):
    dm = x1.shape[1]

    def fn(x1, y1, tgt, gate, fnw):
        def head(x2, fnw):
            yf = (x2 * lax.rsqrt(jnp.mean(x2 * x2, axis=-1, keepdims=True) + NORM_EPS)) * fnw
            err = yf - tgt
            return 0.5 * jnp.sum(jnp.mean(err * err, axis=-1, keepdims=True), axis=0, keepdims=True)

        x2 = x1 + gate * y1
        loss, vjp = jax.vjp(head, x2, fnw)
        dx2, dfnw = vjp(jnp.ones((1, 1), F32))
        return [dx2, gate * dx2], [dfnw, jnp.sum(dx2 * y1, axis=0, keepdims=True), jnp.broadcast_to(loss, (1, 128))]

    (dx2, dy1), (g_fnw, dgate, loss) = _rowwise("loss_bwd", fn, [x1, y1, tgt], [gate, fnw], [(dm, F32), (dm, BF16)],
                                                [(1, dm), (1, dm), (1, 128)], 256)
    return dx2, dy1, g_fnw, dgate, loss


def _gate_bwd(dx, y, gate):
    dm = dx.shape[1]
    (dy,), (dgate,) = _rowwise("gate_bwd", lambda dx, y, gate: ([gate * dx], [jnp.sum(dx * y, axis=0, keepdims=True)]),
                               [dx, y], [gate], [(dm, BF16)], [(1, dm)], 512)
    return dy, dgate


def _softplus_fwd(dt_raw, bias):
    (dt,), _ = _rowwise("dt_softplus", lambda r, b: ([jax.nn.softplus(r + b)], []), [dt_raw], [bias],
                        [(dt_raw.shape[1], F32)], [], 512)
    return dt


def _softplus_bwd(ddt_f, ddt_b, dt_raw, bias):
    def fn(df, db, r, b):
        g = (df + db) * jax.nn.sigmoid(r + b)
        return [g], [jnp.sum(g, axis=0, keepdims=True)]

    w = dt_raw.shape[1]
    (g,), (gb,) = _rowwise("dt_softplus_bwd", fn, [ddt_f, ddt_b, dt_raw], [bias], [(w, BF16)], [(1, w)], 512)
    return g, gb


def _whole(a):
    nd = len(a.shape)
    return pl.BlockSpec(a.shape, lambda *_: (0,) * nd)


def _mod_part(c_all, mod_w):
    nl, _, ncol = mod_w.shape
    nb = c_all.shape[0]

    def body(c_ref, w_ref, o_ref):
        cond = _silu(c_ref[...])
        for i in range(nl):
            o_ref[i * nb:(i + 1) * nb, :] = _nn(cond, w_ref[i])

    return pl.pallas_call(body, name="mod_part", out_shape=jax.ShapeDtypeStruct((nl * nb, ncol), F32),
                          compiler_params=_params(None, VMEM_BIG))(c_all, mod_w)


def _mod_finish(mod_nb, mod_b):
    def body(a_ref, b_ref, o_ref):
        o_ref[...] = a_ref[...] + b_ref[...]

    return pl.pallas_call(body, name="mod_finish", out_shape=jax.ShapeDtypeStruct(mod_b.shape, F32))(mod_nb, mod_b)


def _mod_grad(c_all, dmod_sh):
    nl, nb, ncol = dmod_sh.shape
    dm = c_all.shape[1]

    def body(c_ref, d_ref, o_ref):
        cond = _silu(c_ref[...])
        for i in range(nl):
            o_ref[i] = _tn(cond, d_ref[i])

    return pl.pallas_call(body, name="mod_grad", out_shape=jax.ShapeDtypeStruct((nl, dm, ncol), F32),
                          compiler_params=_params(None, VMEM_BIG))(c_all, dmod_sh)


PACK_ROWS = 16
PACK_COLS = 1024


def _pack_small(rows6, nw2, fnw, b64, a64, d32, extra=None):
    args = [rows6, nw2, fnw, b64, a64, d32] + ([extra] if extra is not None else [])

    def body(*refs):
        o_ref = refs[-1]
        o_ref[...] = jnp.zeros_like(o_ref)
        o_ref[0:6, :] = refs[0][...]
        o_ref[6:8, :] = refs[1][...]
        o_ref[8:9, :] = refs[2][...]
        o_ref[9:10, 0:64] = refs[3][...]
        o_ref[9:10, 64:128] = refs[4][...]
        o_ref[9:10, 128:160] = refs[5][...]
        if extra is not None:
            o_ref[9:10, 256:384] = refs[6][...]

    return pl.pallas_call(body, name="pack_small", out_shape=jax.ShapeDtypeStruct((PACK_ROWS, PACK_COLS), F32))(*args)


def _unpack_small(p):
    return (p[0:6].reshape(2, 3 * PACK_COLS), p[6:8], p[8], p[9, 0:64].reshape(1, 2, 32), p[9, 64:128].reshape(1, 2, 32),
            p[9, 128:160].reshape(1, 32))


def _pack_ssd_small(cw, cb, nw):
    def body(cw_ref, cb_ref, nw_ref, o_ref):
        o_ref[...] = jnp.zeros_like(o_ref)
        o_ref[0:5, :] = cw_ref[...]
        o_ref[5:6, :] = cb_ref[...]
        o_ref[6:7, 0:256] = nw_ref[...]

    return pl.pallas_call(body, name="pack_ssd_small", out_shape=jax.ShapeDtypeStruct((8, 512), F32))(cw, cb, nw)


def _adamw(name, w, parts, m, v, tr):
    r_, c_ = w.shape
    p_ = parts.shape[0]
    tr = min(tr, r_)
    assert r_ % tr == 0

    def body(w_ref, p_ref, m_ref, v_ref, g_ref, d_ref, m2_ref, v2_ref):
        g = p_ref[0].astype(F32)
        for s in range(1, p_):
            g = g + p_ref[s].astype(F32)
        m2 = ADAM_B1 * m_ref[...] + (1.0 - ADAM_B1) * g
        v2 = ADAM_B2 * v_ref[...] + (1.0 - ADAM_B2) * (g * g)
        m_hat = m2 / (1.0 - ADAM_B1 ** ADAM_STEP)
        v_hat = v2 / (1.0 - ADAM_B2 ** ADAM_STEP)
        g_ref[...] = g
        d_ref[...] = -ADAM_LR * (m_hat / (jnp.sqrt(v_hat) + ADAM_EPS) + ADAM_WD * w_ref[...])
        m2_ref[...] = m2
        v2_ref[...] = v2

    blk = pl.BlockSpec((tr, c_), lambda i: (i, 0))
    return pl.pallas_call(
        body, name=name, grid=(r_ // tr,), out_shape=[jax.ShapeDtypeStruct((r_, c_), F32)] * 4,
        in_specs=[blk, pl.BlockSpec((p_, tr, c_), lambda i: (0, i, 0)), blk, blk], out_specs=[blk] * 4,
        compiler_params=_params(("parallel",), VMEM_BIG),
    )(w, parts, m, v)


def _dev_index(p):
    return 4 * p[0] + 2 * p[1] + p[2]


def _all_gather(name, xs):
    n = len(xs)
    hbm = pl.BlockSpec(memory_space=pl.ANY)

    def body(*refs):
        x_refs, o_refs = refs[:n], refs[n:2 * n]
        send_sems, recv_sems, local_sems = refs[2 * n:]
        x, y, c = lax.axis_index("x"), lax.axis_index("y"), lax.axis_index("c")
        me, sibling = (x, y, c), (x, y, 1 - c)
        chips = [(1 - x, y), (x, 1 - y), (1 - x, 1 - y)]

        def copy(a, k, block, to, src=None):
            dst = o_refs[a].at[_dev_index(block)]
            return pltpu.make_async_remote_copy(
                src_ref=dst if src is None else src, dst_ref=dst, send_sem=send_sems.at[a, k],
                recv_sem=recv_sems.at[a, k], device_id=to, device_id_type=MESH)

        mine = [pltpu.make_async_copy(x_refs[a], o_refs[a].at[_dev_index(me)], local_sems.at[a]) for a in range(n)]
        for cp in mine:
            cp.start()
        first = []
        for a in range(n):
            first.append(copy(a, 0, me, sibling, src=x_refs[a]))
            first += [copy(a, 1 + j, me, (*chip, c), src=x_refs[a]) for j, chip in enumerate(chips)]
        for cp in first:
            cp.start()
        passed = []
        for j, chip in enumerate(chips):
            for a in range(n):
                copy(a, 1 + j, (*chip, c), me).wait_recv()
                cp = copy(a, 4 + j, (*chip, c), sibling)
                cp.start()
                passed.append(cp)
        for a in range(n):
            copy(a, 0, sibling, me).wait_recv()
            for j, chip in enumerate(chips):
                copy(a, 4 + j, (*chip, 1 - c), me).wait_recv()
        for cp in first + passed:
            cp.wait_send()
        for cp in mine:
            cp.wait()

    return pl.pallas_call(
        body, name=name, out_shape=[jax.ShapeDtypeStruct((NDEV, *x.shape), x.dtype) for x in xs],
        in_specs=[hbm] * n, out_specs=[hbm] * n,
        scratch_shapes=[pltpu.SemaphoreType.DMA((n, 7)), pltpu.SemaphoreType.DMA((n, 7)), pltpu.SemaphoreType.DMA((n,))],
    )(*xs)


_HBM = pl.BlockSpec(memory_space=pltpu.HBM)
_SEM = pl.BlockSpec(memory_space=pltpu.SEMAPHORE)
_EFFECT = pltpu.SideEffectType.DATAFLOW_SIDE_EFFECTING


def _mesh_position():
    return lax.axis_index("x"), lax.axis_index("y"), lax.axis_index("c")


def _peers(me):
    return [(k, tuple(1 - v if (k >> b) & 1 else v for v, b in zip(me, (2, 1, 0)))) for k in range(1, NDEV)]


def _landing_zones(name, xs, scatter):
    n = len(xs)
    any_ = pl.BlockSpec(memory_space=pl.ANY)

    def body(*refs):
        x_refs, o_refs, sems = refs[:n], refs[n:2 * n], refs[2 * n]
        me = _dev_index(_mesh_position())
        cps = [pltpu.make_async_copy(x_refs[a].at[me] if scatter else x_refs[a], o_refs[a].at[me], sems.at[a])
               for a in range(n)]
        for cp in cps:
            cp.start()
        for cp in cps:
            cp.wait()

    shapes = [x.shape if scatter else (NDEV, *x.shape) for x in xs]
    return pl.pallas_call(
        body, name=name, out_shape=[jax.ShapeDtypeStruct(s, x.dtype) for s, x in zip(shapes, xs)],
        in_specs=[any_] * n, out_specs=[any_] * n, scratch_shapes=[pltpu.SemaphoreType.DMA((n,))],
    )(*xs)


def _exchange_copies(x_refs, land_refs, send_sems, recv_sems, scatter):
    me = _mesh_position()
    out = []
    for k, peer in _peers(me):
        for a, (x_ref, land_ref) in enumerate(zip(x_refs, land_refs)):
            sem = a * (NDEV - 1) + k - 1
            out.append(pltpu.make_async_remote_copy(
                src_ref=x_ref.at[_dev_index(peer)] if scatter else x_ref, dst_ref=land_ref.at[_dev_index(me)],
                send_sem=send_sems.at[sem], recv_sem=recv_sems.at[sem], device_id=peer, device_id_type=MESH))
    return out


def _exchange_start(name, xs, lands, scatter, dep):
    n = len(xs)

    def body(*refs):
        x_refs, land_refs = refs[:n], refs[n:2 * n]
        send_sems, recv_sems = refs[2 * n + 1], refs[2 * n + 2]
        token = refs[-1]
        for cp in _exchange_copies(x_refs, land_refs, send_sems, recv_sems, scatter):
            cp.start()
        token[...] = jnp.zeros_like(token)

    sems = pltpu.SemaphoreType.DMA((n * (NDEV - 1),))
    res = pl.pallas_call(
        body, name=name,
        out_shape=(sems, sems, *[pltpu.HBM(a.shape, a.dtype) for a in (*xs, *lands)], jax.ShapeDtypeStruct((8, 128), F32)),
        in_specs=[_HBM] * (2 * n) + [pl.BlockSpec(memory_space=pl.ANY)],
        out_specs=(_SEM, _SEM, *[_HBM] * (2 * n), pl.BlockSpec(memory_space=pltpu.VMEM)),
        input_output_aliases={i: 2 + i for i in range(2 * n)},
        compiler_params=pltpu.CompilerParams(has_side_effects=_EFFECT),
    )(*[pltpu.with_memory_space_constraint(a, pltpu.HBM) for a in (*xs, *lands)], dep)
    return res[:-1], res[-1]


def _exchange_wait(name, handles, scatter, after):
    send_sems, recv_sems = handles[0], handles[1]
    bufs = handles[2:]
    n = len(bufs) // 2

    def body(*refs):
        x_refs, land_refs = refs[:n], refs[n:2 * n]
        s_sems, r_sems = refs[2 * n], refs[2 * n + 1]
        for cp in _exchange_copies(x_refs, land_refs, s_sems, r_sems, scatter):
            cp.wait_send()
            cp.wait_recv()

    res = pl.pallas_call(
        body, name=name, out_shape=tuple(pltpu.HBM(a.shape, a.dtype) for a in bufs),
        in_specs=[_HBM] * (2 * n) + [_SEM, _SEM, pl.BlockSpec(memory_space=pl.ANY)], out_specs=tuple([_HBM] * (2 * n)),
        input_output_aliases={i: i for i in range(2 * n)},
        compiler_params=pltpu.CompilerParams(has_side_effects=_EFFECT),
    )(*bufs, send_sems, recv_sems, after)
    return res[n:]


def kernel(x, c, positions, norm_w, mod_w, mod_b, attn_w_in, attn_w_out, ssd_w_in, ssd_conv_w, ssd_conv_b, ssd_dt_bias, ssd_a_log, ssd_d, ssd_norm_w, ssd_w_out, final_norm_w, loss_target, m_norm_w, m_mod_w, m_mod_b, m_attn_w_in, m_attn_w_out, m_ssd_w_in, m_ssd_conv_w, m_ssd_conv_b, m_ssd_dt_bias, m_ssd_a_log, m_ssd_d, m_ssd_norm_w, m_ssd_w_out, m_final_norm_w, v_norm_w, v_mod_w, v_mod_b, v_attn_w_in, v_attn_w_out, v_ssd_w_in, v_ssd_conv_w, v_ssd_conv_b, v_ssd_dt_bias, v_ssd_a_log, v_ssd_d, v_ssd_norm_w, v_ssd_w_out, v_final_norm_w):
    s_len, dm = x.shape[1], x.shape[2]
    me = 4 * lax.axis_index("x") + 2 * lax.axis_index("y") + lax.axis_index("c")
    x0 = x.reshape(s_len, dm)
    tgt = loss_target.reshape(s_len, dm)
    aw = 3 * 512
    si = 2 * dm
    sxbc = 2 * si
    n_ssd_in = ssd_w_in.shape[2] * NDEV

    g_ai, c_all = _all_gather("gather_attn_w_in", [attn_w_in[0].astype(BF16), c])
    w_ai = g_ai.transpose(1, 0, 2).reshape(dm, 4 * aw)
    c_all = c_all.reshape(NDEV, dm)
    ssd_small = _pack_ssd_small(ssd_conv_w[0], ssd_conv_b, ssd_norm_w)
    late_shards = [attn_w_out[0].astype(BF16), ssd_w_in[0].astype(BF16), ssd_w_out[0].astype(BF16), ssd_small]
    w_handles, w_token = _exchange_start("weights_start", late_shards, _landing_zones("weights_place", late_shards, False),
                                         False, g_ai)

    part = _mod_part(c_all, mod_w)
    (part_all,) = _all_gather("gather_mod", [part])
    mod_nb = jnp.stack([lax.dynamic_index_in_dim(part_all, i * NDEV + me, axis=1, keepdims=False).reshape(3 * dm)
                        for i in range(2)])
    mod = _mod_finish(mod_nb, mod_b)
    shift = [mod[i:i + 1, 0:dm] for i in range(2)]
    scale = [mod[i:i + 1, dm:2 * dm] for i in range(2)]
    gate = [mod[i:i + 1, 2 * dm:3 * dm] for i in range(2)]
    nw = [norm_w[i:i + 1] + w_token[0:1, 0:1] for i in range(2)]

    hn0 = _norm_mod_fwd("norm0", x0, nw[0], scale[0], shift[0])
    inv_freq = ROPE_THETA ** (-jnp.arange(0, ROT_DIM, 2, dtype=F32) / ROT_DIM)
    lane = jnp.arange(128) % HEAD_DIM
    inv_row = jnp.where(lane < ROT_DIM, inv_freq[lane % (ROT_DIM // 2)], 0.0).reshape(1, 128).astype(F32)
    tabs = _rope_tables(positions.reshape(s_len, 1), inv_row)
    qk = _matmul("proj_qk", hn0, w_ai, "nn", F32, MM_T, MM_T, dm, epilogue=_rot_fwd, mrows=tabs, n_out=2 * aw)
    v = _matmul("proj_v", hn0, w_ai, "nn", F32, MM_T, aw // 2, dm, b_noff=2 * aw, n_out=aw)
    z0 = _matmul("proj_z", hn0, w_ai, "nn", F32, MM_T, aw // 2, dm, b_noff=3 * aw, n_out=aw)
    att = [_attn_fwd(g, qk, v) for g in range(3)]
    os_, lses = [a[0] for a in att], [a[1] for a in att]
    g_ao, g_si, g_so, g_small = _exchange_wait("weights_wait", w_handles, False, lses[2])
    w_ao = g_ao.reshape(aw, dm)
    w_si = g_si.transpose(1, 0, 2).reshape(dm, n_ssd_in)
    w_z, w_xbc, w_dt = w_si[:, :si], w_si[:, si:si + sxbc], w_si[:, si + sxbc:]
    w_so = g_so.reshape(si, dm)
    conv_w = g_small[:, 0:CONV_WIDTH, :].transpose(1, 0, 2).reshape(CONV_WIDTH, sxbc)
    conv_b = g_small[:, 5, :].reshape(1, sxbc)
    snw = g_small[:, 6, 0:si // NDEV].reshape(1, si)
    a0, y0, x1 = _attn_out(os_, lses, z0, x0, gate[0], w_ao)

    hn1 = _norm_mod_fwd("norm1", x1, nw[1], scale[1], shift[1])
    z1 = _matmul("ssd_proj_z", hn1, w_z, "nn", F32, MM_T, MM_T, dm)
    xpre = _matmul("ssd_proj_xbc", hn1, w_xbc, "nn", F32, MM_T, MM_T, dm)
    dt_raw = _matmul("ssd_proj_dt", hn1, w_dt, "nn", F32, MM_T, 64, dm)
    xbc = _conv_fwd(xpre, conv_w, conv_b)
    dt_bias = ssd_dt_bias.reshape(1, 2 * SSD_HEADS)
    alog = ssd_a_log.reshape(1, 2 * SSD_HEADS)
    dt = _softplus_fwd(dt_raw, dt_bias)
    y_f, st_f = _ssd_fwd(xbc, dt, alog, 0)
    y_b, st_b = _ssd_fwd(xbc, dt, alog, 1)
    d_e = jnp.repeat(ssd_d.reshape(SSD_HEADS), HEAD_DIM).reshape(1, si)
    u = _gate_norm_fwd(y_f, y_b, xbc, z1, d_e, snw)
    y1 = _matmul("ssd_out", u, w_so, "nn", F32, MM_T, MM_T, si)

    fnw = final_norm_w.reshape(1, dm)
    dx2, dy1, g_fnw, dgate1, loss_part = _loss_bwd(x1, y1, tgt, gate[1], fnw)
    du = _matmul("ssd_out_dx", dy1, w_so, "nt", F32, MM_T, MM_T, dm)
    gw_so = _matmul("ssd_out_dw", u, dy1, "tn", BF16, MM_T, MM_T, MM_T)
    dys, dz1, g_snw, g_d = _gate_norm_bwd(du, y_f, y_b, xbc, z1, d_e, snw)
    dxbc_f, ddt_f, dalog_f = _ssd_bwd(xbc, dt, alog, st_f, dys, d_e, 0)
    dxbc_b, ddt_b, dalog_b = _ssd_bwd(xbc, dt, alog, st_b, dys, d_e, 1)
    dpre, g_cw, g_cb = _conv_bwd(xpre, dxbc_f, dxbc_b, conv_w, conv_b)
    ddt_raw, g_dtb = _softplus_bwd(ddt_f, ddt_b, dt_raw, dt_bias)
    dhn1 = [_matmul("ssd_proj_z_dx", dz1, w_z, "nt", F32, MM_T, MM_T, MM_T),
            _matmul("ssd_proj_xbc_dx", dpre, w_xbc, "nt", F32, MM_T, MM_T, MM_T),
            _matmul("ssd_proj_dt_dx", ddt_raw, w_dt, "nt", F32, MM_T, MM_T, 64)]
    gw_si = jnp.concatenate([_matmul("ssd_proj_z_dw", hn1, dz1, "tn", BF16, MM_T, MM_T, MM_T),
                             _matmul("ssd_proj_xbc_dw", hn1, dpre, "tn", BF16, MM_T, MM_T, MM_T),
                             _matmul("ssd_proj_dt_dw", hn1, ddt_raw, "tn", BF16, MM_T, 64, MM_T)], axis=1)
    dx1, g_nw1, dsc1, dsh1 = _norm_mod_bwd("norm1_bwd", x1, dhn1, dx2, nw[1], scale[1], shift[1])

    l1_grads = [gw_so.reshape(NDEV, si // NDEV, dm), gw_si.reshape(dm, NDEV, n_ssd_in // NDEV).transpose(1, 0, 2),
                _pack_ssd_small_blocks(g_cw, g_cb, g_snw)]
    l1_handles, l1_token = _exchange_start("l1_grads_start", l1_grads, _landing_zones("l1_grads_place", l1_grads, True),
                                           True, dx1)

    dy0, dgate0 = _gate_bwd(dx1, y0, gate[0] + l1_token[0:1, 0:1])
    da0 = _matmul("attn_out_dx", dy0, w_ao, "nt", F32, MM_T, aw // 2, dm)
    gw_ao = _matmul("attn_out_dw", a0, dy0, "tn", BF16, aw // 2, MM_T, MM_T)
    dos, dls, dz0 = _mix_bwd(da0, os_, lses, z0)
    datt = [_attn_bwd(g, qk, v, os_[g], lses[g], dos[g], dls[g]) for g in range(3)]
    dqkv = _rot_pack_bwd([t[0] for t in datt], [t[1] for t in datt], [t[2] for t in datt], tabs)
    wcol = attn_w_in.shape[2]
    gw_ai = jnp.concatenate([
        _matmul("proj_qkv_dw", hn0, dqkv, "tn", BF16, MM_T, wcol, MM_T, out_blocks=3 * aw // wcol),
        _matmul("proj_z_dw", hn0, dz0, "tn", BF16, MM_T, wcol, MM_T, out_blocks=aw // wcol)], axis=0)
    l0_grads = [gw_ai, gw_ao.reshape(NDEV, aw // NDEV, dm)]
    l0_handles, l0_token = _exchange_start("l0_grads_start", l0_grads, _landing_zones("l0_grads_place", l0_grads, True),
                                           True, dqkv)
    zero_row = jnp.tile(l0_token[0:1], (1, dm // 128))
    after_start = lambda acc, t: acc + t
    dhn0 = [_matmul("proj_qkv_dx", dqkv, w_ai, "nt", F32, MM_T, MM_T, aw, n_out=dm, epilogue=after_start, ncols=(zero_row,)),
            _matmul("proj_z_dx", dz0, w_ai, "nt", F32, MM_T, MM_T, aw, b_koff=3 * aw, n_out=dm)]
    dx0, g_nw0, dsc0, dsh0 = _norm_mod_bwd("norm0_bwd", x0, dhn0, dx1, nw[0], scale[0], shift[0])

    rows6 = jnp.concatenate([dsh0, dsc0, dgate0, dsh1, dsc1, dgate1], axis=0)
    small_g = _pack_small(rows6, jnp.concatenate([g_nw0, g_nw1], axis=0), g_fnw, g_dtb, dalog_f + dalog_b, g_d, loss_part)
    (small_all,) = _all_gather("gather_small_grads", [small_g])
    small_w = _pack_small(mod_b.reshape(6, dm), norm_w, fnw, dt_bias, alog, ssd_d)
    small_m = _pack_small(m_mod_b.reshape(6, dm), m_norm_w, m_final_norm_w.reshape(1, dm), m_ssd_dt_bias.reshape(1, 64),
                          m_ssd_a_log.reshape(1, 64), m_ssd_d)
    small_v = _pack_small(v_mod_b.reshape(6, dm), v_norm_w, v_final_norm_w.reshape(1, dm), v_ssd_dt_bias.reshape(1, 64),
                          v_ssd_a_log.reshape(1, 64), v_ssd_d)
    small_out = _adamw("adamw_small", small_w, small_all, small_m, small_v, PACK_ROWS)
    loss = small_out[0][9, 256]
    sg, sd, sm, sv = (_unpack_small(p) for p in small_out)

    ncol = mod_w.shape[2]
    dmod_all = small_all[:, 0:6, :].reshape(NDEV, 2, 3 * dm)
    dmod_sh = lax.dynamic_slice_in_dim(dmod_all, me * ncol, ncol, axis=2).transpose(1, 0, 2)
    g_modw = _mod_grad(c_all, dmod_sh).reshape(1, 2 * dm, ncol)
    modw_out = _adamw("adamw_mod_w", mod_w.reshape(2 * dm, ncol), g_modw, m_mod_w.reshape(2 * dm, ncol),
                      v_mod_w.reshape(2 * dm, ncol), 256)

    r_so, r_si, r_small = _exchange_wait("l1_grads_wait", l1_handles, True, modw_out[0])
    si_out = _adamw("adamw_ssd_w_in", ssd_w_in[0], r_si, m_ssd_w_in[0], v_ssd_w_in[0], 256)
    so_out = _adamw("adamw_ssd_w_out", ssd_w_out[0], r_so, m_ssd_w_out[0], v_ssd_w_out[0], 256)
    ssd_small_m = _pack_ssd_small(m_ssd_conv_w[0], m_ssd_conv_b, m_ssd_norm_w)
    ssd_small_v = _pack_ssd_small(v_ssd_conv_w[0], v_ssd_conv_b, v_ssd_norm_w)
    ss_out = _adamw("adamw_ssd_small", ssd_small, r_small, ssd_small_m, ssd_small_v, 8)
    r_ai, r_ao = _exchange_wait("l0_grads_wait", l0_handles, True, so_out[0])
    ai_out = _adamw("adamw_attn_w_in", attn_w_in[0], r_ai, m_attn_w_in[0], v_attn_w_in[0], 256)
    ao_out = _adamw("adamw_attn_w_out", attn_w_out[0], r_ao, m_attn_w_out[0], v_attn_w_out[0], 192)

    def ssd_small_unpack(p):
        return p[0:5][None], p[5:6], p[6:7, 0:si // NDEV]

    cwo, cbo, nwo = zip(*(ssd_small_unpack(p) for p in ss_out))
    per_kind = []
    for k in range(4):
        s = (sg, sd, sm, sv)[k]
        per_kind.append([
            s[1], modw_out[k].reshape(mod_w.shape), s[0], ai_out[k][None], ao_out[k][None], si_out[k][None],
            cwo[k], cbo[k], s[3], s[4], s[5], nwo[k], so_out[k][None], s[2]])
    return (loss, dx0.reshape(x.shape), *per_kind[0], *per_kind[1], *per_kind[2], *per_kind[3])


def _pack_ssd_small_blocks(g_cw, g_cb, g_nw):
    nper = g_cw.shape[1] // NDEV
    nwper = g_nw.shape[1] // NDEV

    def body(cw_ref, cb_ref, nw_ref, o_ref):
        o_ref[...] = jnp.zeros_like(o_ref)
        for d in range(NDEV):
            o_ref[d, 0:5, :] = cw_ref[:, d * nper:(d + 1) * nper]
            o_ref[d, 5:6, :] = cb_ref[:, d * nper:(d + 1) * nper]
            o_ref[d, 6:7, 0:nwper] = nw_ref[:, d * nwper:(d + 1) * nwper]

    return pl.pallas_call(body, name="pack_ssd_small_grads", out_shape=jax.ShapeDtypeStruct((NDEV, 8, nper), F32))(g_cw, g_cb, g_nw)
```

```python
import functools
import math

import jax
import jax.numpy as jnp
from jax import lax
from jax.experimental import pallas as pl
from jax.experimental.pallas import tpu as pltpu

F32 = jnp.float32
BF16 = jnp.bfloat16
HI = lax.Precision.HIGHEST
MESH = pl.DeviceIdType.MESH
NDEV = 8

NORM_EPS = 1e-6
ROPE_THETA = 500000.0
ROT_DIM = 16
HEAD_DIM = 64
DILATIONS = (1, 4, 16)
BAND = 64
NEG_BIG = -1e30
CHUNK = 128
SSD_HEADS = 32
SSD_GROUPS = 8
CONV_WIDTH = 5

ADAM_LR = 0.001
ADAM_B1 = 0.9
ADAM_B2 = 0.999
ADAM_EPS = 1e-08
ADAM_WD = 0.01
ADAM_STEP = 10

VMEM_BIG = 56 * 1024 * 1024
MM_T = 1024


def _params(sem=None, vmem=None):
    kw = {}
    if sem is not None:
        kw["dimension_semantics"] = sem
    if vmem is not None:
        kw["vmem_limit_bytes"] = vmem
    return pltpu.CompilerParams(**kw)


def _dg(a, b, ca, cb, prec=None):
    return lax.dot_general(a, b, (((ca,), (cb,)), ((), ())), preferred_element_type=F32, precision=prec)


def _nn(a, b):
    return _dg(a.astype(BF16), b.astype(BF16), 1, 0)


def _nt(a, b):
    return _dg(a.astype(BF16), b.astype(BF16), 1, 1)


def _tn(a, b):
    return _dg(a.astype(BF16), b.astype(BF16), 0, 0)


def _hnn(a, b):
    return _dg(a, b, 1, 0, HI)


@jax.custom_vjp
def _bnn(a, b):
    return _nn(a, b)


_bnn.defvjp(lambda a, b: (_nn(a, b), (a, b)), lambda r, g: (_nt(g, r[1]), _tn(r[0], g)))


@jax.custom_vjp
def _bnt(a, b):
    return _nt(a, b)


_bnt.defvjp(lambda a, b: (_nt(a, b), (a, b)), lambda r, g: (_nn(g, r[1]), _tn(g, r[0])))


@jax.custom_vjp
def _btn(a, b):
    return _tn(a, b)


_btn.defvjp(lambda a, b: (_tn(a, b), (a, b)), lambda r, g: (_nt(r[1], g), _nn(r[0], g)))


def _silu(x):
    return x * jax.nn.sigmoid(x)


def _matmul(name, a, b, mode, out_dtype, tm, tn, tk, *, epilogue=None, tiled=(), mrows=(), ncols=(),
            b_noff=0, b_koff=0, n_out=None, out_blocks=None):
    if mode == "tn":
        K, M = a.shape
    else:
        M, K = a.shape
    N = n_out if n_out is not None else (b.shape[0] if mode == "nt" else b.shape[1])
    tm, tn, tk = min(tm, M), min(tn, N), min(tk, K)
    assert M % tm == 0 and N % tn == 0 and K % tk == 0, (name, M, N, K, tm, tn, tk)
    assert b_noff % tn == 0 and b_koff % tk == 0
    no, ko = b_noff // tn, b_koff // tk
    nk = K // tk
    if mode == "tn":
        a_spec = pl.BlockSpec((tk, tm), lambda i, j, k: (k, i))
    else:
        a_spec = pl.BlockSpec((tm, tk), lambda i, j, k: (i, k))
    if mode == "nt":
        b_spec = pl.BlockSpec((tn, tk), lambda i, j, k: (j + no, k + ko))
    else:
        b_spec = pl.BlockSpec((tk, tn), lambda i, j, k: (k + ko, j + no))
    specs = [a_spec, b_spec]
    specs += [pl.BlockSpec((tm, tn), lambda i, j, k: (i, j)) for _ in tiled]
    specs += [pl.BlockSpec((tm, r.shape[1]), lambda i, j, k: (i, 0)) for r in mrows]
    specs += [pl.BlockSpec((1, tn), lambda i, j, k: (0, j)) for _ in ncols]
    if out_blocks is None:
        out_shape = jax.ShapeDtypeStruct((M, N), out_dtype)
        out_spec = pl.BlockSpec((tm, tn), lambda i, j, k: (i, j))
    else:
        nper = N // out_blocks
        assert nper % tn == 0
        jb = nper // tn
        out_shape = jax.ShapeDtypeStruct((out_blocks, M, nper), out_dtype)
        out_spec = pl.BlockSpec((None, tm, tn), lambda i, j, k: (j // jb, i, j % jb))
    ne = len(tiled) + len(mrows) + len(ncols)
    dot = {"nn": _nn, "nt": _nt, "tn": _tn}[mode]

    def body(a_ref, b_ref, *rest):
        extras, o_ref = rest[:ne], rest[ne]

        def finish(acc):
            if epilogue is not None:
                acc = epilogue(acc, *[e[...] for e in extras])
            o_ref[...] = acc.astype(o_ref.dtype)

        if nk == 1:
            finish(dot(a_ref[...], b_ref[...]))
        else:
            acc_ref = rest[ne + 1]
            k = pl.program_id(2)

            @pl.when(k == 0)
            def _():
                acc_ref[...] = jnp.zeros_like(acc_ref)

            acc_ref[...] += dot(a_ref[...], b_ref[...])

            @pl.when(k == nk - 1)
            def _():
                finish(acc_ref[...])

    return pl.pallas_call(
        body, name=name, out_shape=out_shape, grid=(M // tm, N // tn, nk),
        in_specs=specs, out_specs=out_spec,
        scratch_shapes=[] if nk == 1 else [pltpu.VMEM((tm, tn), F32)],
        compiler_params=_params(("parallel", "parallel", "arbitrary"), VMEM_BIG),
    )(a, b, *tiled, *mrows, *ncols)


def _rowwise(name, fn, tiled, consts, outs, accs, ts):
    tl = [(t, t.shape[1], 0) if not isinstance(t, tuple) else t for t in tiled]
    s_len = tl[0][0].shape[0]
    assert s_len % ts == 0
    nt_, nc_, no_ = len(tl), len(consts), len(outs)

    def body(*refs):
        t_refs, c_refs = refs[:nt_], refs[nt_:nt_ + nc_]
        o_refs, a_refs = refs[nt_ + nc_:nt_ + nc_ + no_], refs[nt_ + nc_ + no_:]
        res_o, res_a = fn(*[r[...] for r in t_refs], *[r[...] for r in c_refs])
        for r, v in zip(o_refs, res_o, strict=True):
            r[...] = v.astype(r.dtype)
        if a_refs:
            @pl.when(pl.program_id(0) == 0)
            def _():
                for r in a_refs:
                    r[...] = jnp.zeros_like(r)

            for r, v in zip(a_refs, res_a, strict=True):
                r[...] += v

    in_specs = [pl.BlockSpec((ts, w), functools.partial(lambda i, cb: (i, cb), cb=cb)) for (_, w, cb) in tl]
    in_specs += [pl.BlockSpec(c.shape, lambda i: (0, 0)) for c in consts]
    out_specs = [pl.BlockSpec((ts, c), lambda i: (i, 0)) for (c, _) in outs]
    out_specs += [pl.BlockSpec(shp, lambda i: (0, 0)) for shp in accs]
    out_shape = [jax.ShapeDtypeStruct((s_len, c), dt) for (c, dt) in outs]
    out_shape += [jax.ShapeDtypeStruct(shp, F32) for shp in accs]
    res = pl.pallas_call(
        body, name=name, out_shape=out_shape, grid=(s_len // ts,), in_specs=in_specs, out_specs=out_specs,
        compiler_params=_params(("arbitrary",) if accs else ("parallel",), VMEM_BIG),
    )(*[t[0] for t in tl], *consts)
    return res[:no_], res[no_:]


def _norm_mod_fn(x, nw, sc, sh):
    r = lax.rsqrt(jnp.mean(x * x, axis=-1, keepdims=True) + NORM_EPS)
    return (x * r * nw) * (1.0 + sc) + sh


def _norm_mod_fwd(name, x, nw, sc, sh):
    (hn,), _ = _rowwise(name, lambda x, nw, sc, sh: ([_norm_mod_fn(x, nw, sc, sh)], []),
                        [x], [nw, sc, sh], [(x.shape[1], BF16)], [], 512)
    return hn


def _norm_mod_bwd(name, x, dhn_parts, dres, nw, sc, sh):
    n = len(dhn_parts)
    d = x.shape[1]

    def fn(x, *rest):
        dhn = rest[0]
        for p in rest[1:n]:
            dhn = dhn + p
        dres, nw, sc, sh = rest[n:]
        _, vjp = jax.vjp(_norm_mod_fn, x, nw, sc, sh)
        dx, dnw, dsc, dsh = vjp(dhn)
        return [dx + dres], [dnw, dsc, dsh]

    (dx,), (g_nw, dsc, dsh) = _rowwise(name, fn, [x, *dhn_parts, dres], [nw, sc, sh], [(d, F32)],
                                       [(1, d), (1, d), (1, d)], 256)
    return dx, g_nw, dsc, dsh


def _rope_tables(pos_col, inv_row):
    def fn(pos, inv):
        ang = pos.astype(F32) * inv
        e = lax.broadcasted_iota(jnp.int32, (1, 128), 1) % HEAD_DIM
        cos, sin = jnp.cos(ang), jnp.sin(ang)
        half = ROT_DIM // 2
        return [jnp.where(e < ROT_DIM, cos, 1.0), jnp.where(e < half, -sin, 0.0),
                jnp.where((e >= half) & (e < ROT_DIM), sin, 0.0)], []

    (c, sa, sb), _ = _rowwise("rope_tables", fn, [pos_col], [inv_row], [(128, F32)] * 3, [], 512)
    return c, sa, sb


def _rot_fwd(t, c, sa, sb):
    n = t.shape[1]
    rep = n // 128
    c, sa, sb = (jnp.tile(u, (1, rep)) for u in (c, sa, sb))
    return t * c + pltpu.roll(t, n - ROT_DIM // 2, 1) * sa + pltpu.roll(t, ROT_DIM // 2, 1) * sb


def _rot_bwd(g, c, sa, sb):
    n = g.shape[1]
    rep = n // 128
    c, sa, sb = (jnp.tile(u, (1, rep)) for u in (c, sa, sb))
    return g * c + pltpu.roll(g * sa, ROT_DIM // 2, 1) + pltpu.roll(g * sb, n - ROT_DIM // 2, 1)


ATT_TQ = 128
ATT_TK = ATT_TQ + 2 * BAND


def _attn_specs(g, s_len):
    def blk(off):
        return pl.BlockSpec((s_len, 128), functools.partial(lambda hp, off: (0, off + hp), off=off))

    return blk(4 * g), blk(12 + 4 * g), blk(4 * g), blk(0)


def _attn_tile_geometry(t, d, l):
    nts = l // ATT_TQ
    r = t // nts
    q0 = (t % nts) * ATT_TQ
    ws = jnp.clip(q0 - BAND, 0, l - ATT_TK)
    qpos = q0 + lax.broadcasted_iota(jnp.int32, (ATT_TQ, 1), 0)
    kpos = ws + lax.broadcasted_iota(jnp.int32, (1, ATT_TK), 1)
    valid = jnp.abs(kpos - qpos) <= BAND
    if d == 1:
        return pl.ds(pl.multiple_of(q0, ATT_TQ), ATT_TQ), pl.ds(pl.multiple_of(ws, BAND), ATT_TK), valid
    return pl.ds(r + d * q0, ATT_TQ, stride=d), pl.ds(r + d * ws, ATT_TK, stride=d), valid


def _attn_fwd(g, qk, v):
    s_len = qk.shape[0]
    d = DILATIONS[g]
    l = s_len // d
    assert l % ATT_TQ == 0 and l >= ATT_TK
    q_spec, k_spec, v_spec, o_spec = _attn_specs(g, s_len)
    scale = 1.0 / math.sqrt(HEAD_DIM)

    def body(q_ref, k_ref, v_ref, o_ref, lse_ref):
        lane = lax.broadcasted_iota(jnp.int32, (1, 128), 1)
        in_h = [lane < HEAD_DIM, lane >= HEAD_DIM]

        def tile(t, carry):
            rows, win, valid = _attn_tile_geometry(t, d, l)
            q = q_ref[rows, :].astype(BF16)
            k = k_ref[win, :].astype(BF16)
            vv = v_ref[win, :].astype(BF16)
            outs, lses = [], []
            for h in range(2):
                qm = jnp.where(in_h[h], q, jnp.zeros_like(q))
                s = jnp.where(valid, _nt(qm, k) * scale, NEG_BIG)
                m = jnp.max(s, axis=1, keepdims=True)
                p = jnp.exp(s - m)
                den = jnp.sum(p, axis=1, keepdims=True)
                outs.append(_nn(p, vv) / den)
                lses.append(m + jnp.log(den))
            o_ref[rows, :] = jnp.where(in_h[0], outs[0], outs[1])
            lse_ref[rows, :] = jnp.where(in_h[0], lses[0], lses[1])
            return carry

        lax.fori_loop(0, s_len // ATT_TQ, tile, 0, unroll=4)

    return pl.pallas_call(
        body, name=f"attn_fwd_g{g}", grid=(4,),
        out_shape=[jax.ShapeDtypeStruct((s_len, 512), F32)] * 2,
        in_specs=[q_spec, k_spec, v_spec], out_specs=[o_spec, o_spec],
        compiler_params=_params(("parallel",), VMEM_BIG),
    )(qk, qk, v)


def _attn_bwd(g, qk, v, o, lse, do, dlse):
    s_len = qk.shape[0]
    d = DILATIONS[g]
    l = s_len // d
    q_spec, k_spec, v_spec, o_spec = _attn_specs(g, s_len)
    scale = 1.0 / math.sqrt(HEAD_DIM)

    def body(q_ref, k_ref, v_ref, o_ref, lse_ref, do_ref, dlse_ref, dq_ref, dk_ref, dv_ref):
        lane = lax.broadcasted_iota(jnp.int32, (1, 128), 1)
        in_h = [lane < HEAD_DIM, lane >= HEAD_DIM]
        dk_ref[...] = jnp.zeros_like(dk_ref)
        dv_ref[...] = jnp.zeros_like(dv_ref)

        def tile(t, carry):
            rows, win, valid = _attn_tile_geometry(t, d, l)
            q, k, vv = q_ref[rows, :].astype(BF16), k_ref[win, :].astype(BF16), v_ref[win, :].astype(BF16)
            dout, lse_t, dlse_t = do_ref[rows, :], lse_ref[rows, :], dlse_ref[rows, :]
            od = dout * o_ref[rows, :]
            dqs, dks, dvs = [], [], []
            for h in range(2):
                c0 = h * HEAD_DIM
                qm = jnp.where(in_h[h], q, jnp.zeros_like(q))
                s = jnp.where(valid, _nt(qm, k) * scale, NEG_BIG)
                p = jnp.exp(s - lse_t[:, c0:c0 + 1])
                dom = jnp.where(in_h[h], dout, 0.0)
                dp = _nt(dom, vv)
                delta = jnp.sum(jnp.where(in_h[h], od, 0.0), axis=1, keepdims=True)
                ds = (p * (dp - delta + dlse_t[:, c0:c0 + 1]) * scale).astype(BF16)
                dqs.append(_nn(ds, k))
                dks.append(_tn(ds, q))
                dvs.append(_tn(p, dout))
            dq_ref[rows, :] = jnp.where(in_h[0], dqs[0], dqs[1])
            dk_ref[win, :] += jnp.where(in_h[0], dks[0], dks[1])
            dv_ref[win, :] += jnp.where(in_h[0], dvs[0], dvs[1])
            return carry

        lax.fori_loop(0, s_len // ATT_TQ, tile, 0, unroll=4)

    return pl.pallas_call(
        body, name=f"attn_bwd_g{g}", grid=(4,),
        out_shape=[jax.ShapeDtypeStruct((s_len, 512), F32)] * 3,
        in_specs=[q_spec, k_spec, v_spec, o_spec, o_spec, o_spec, o_spec], out_specs=[o_spec] * 3,
        compiler_params=_params(("parallel",), VMEM_BIG),
    )(qk, qk, v, o, lse, do, dlse)


def _mix_weights(ls):
    mx = jnp.maximum(jnp.maximum(ls[0], ls[1]), ls[2])
    es = [jnp.exp(x - mx) for x in ls]
    tot = es[0] + es[1] + es[2]
    return [e / tot for e in es]


def _attn_out(os_, lses, z, x, gate, w_out):
    s_len, dm = x.shape
    tm = 256
    wdt = 512

    def body(o0, o1, o2, l0, l1, l2, z_ref, x_ref, g_ref, w_ref, a_ref, y_ref, x1_ref):
        alphas = _mix_weights([l0[...], l1[...], l2[...]])
        y = jnp.zeros((tm, dm), F32)
        for g, o_ref in enumerate((o0, o1, o2)):
            a_g = (o_ref[...] * alphas[g] * _silu(z_ref[:, g * wdt:(g + 1) * wdt])).astype(BF16)
            a_ref[:, g * wdt:(g + 1) * wdt] = a_g
            y = y + _nn(a_g, w_ref[g * wdt:(g + 1) * wdt, :])
        y_ref[...] = y
        x1_ref[...] = x_ref[...] + g_ref[...] * y

    row = lambda c: pl.BlockSpec((tm, c), lambda i: (i, 0))
    return pl.pallas_call(
        body, name="attn_out", grid=(s_len // tm,),
        out_shape=[jax.ShapeDtypeStruct((s_len, 3 * wdt), BF16), jax.ShapeDtypeStruct((s_len, dm), F32),
                   jax.ShapeDtypeStruct((s_len, dm), F32)],
        in_specs=[row(wdt)] * 6 + [row(3 * wdt), row(dm), pl.BlockSpec((1, dm), lambda i: (0, 0)),
                                   pl.BlockSpec(w_out.shape, lambda i: (0, 0))],
        out_specs=[row(3 * wdt), row(dm), row(dm)],
        compiler_params=_params(("parallel",), VMEM_BIG),
    )(*os_, *lses, z, x, gate, w_out)


def _mix_bwd(da, os_, lses, z):
    wdt = 512

    def fn(da, o0, o1, o2, l0, l1, l2, z):
        os_t, ls = [o0, o1, o2], [l0, l1, l2]
        alphas = _mix_weights(ls)
        hi = lax.broadcasted_iota(jnp.int32, (wdt, wdt), 0) // HEAD_DIM
        hj = lax.broadcasted_iota(jnp.int32, (wdt, wdt), 1) // HEAD_DIM
        seg = (hi == hj).astype(F32)
        dos, dal, dzs = [], [], []
        for g in range(3):
            zg = z[:, g * wdt:(g + 1) * wdt]
            sig = jax.nn.sigmoid(zg)
            dag = da[:, g * wdt:(g + 1) * wdt]
            dmix = dag * zg * sig
            dzs.append(dag * os_t[g] * alphas[g] * (sig * (1.0 + zg * (1.0 - sig))))
            dos.append(dmix * alphas[g])
            dal.append(_hnn(dmix * os_t[g], seg))
        mean = alphas[0] * dal[0] + alphas[1] * dal[1] + alphas[2] * dal[2]
        dls = [alphas[g] * (dal[g] - mean) for g in range(3)]
        return dos + dls + [jnp.concatenate(dzs, axis=1)], []

    outs, _ = _rowwise("mix_bwd", fn, [da, *os_, *lses, z], [], [(wdt, F32)] * 6 + [(3 * wdt, BF16)], [], 256)
    return outs[:3], outs[3:6], outs[6]


def _rot_pack_bwd(dqs, dks, dvs, tabs):
    wdt = 512

    def fn(*args):
        grads, (c, sa, sb) = args[:9], args[9:]
        cols = [_rot_bwd(gq, c, sa, sb) for gq in grads[:6]] + list(grads[6:])
        return [jnp.concatenate(cols, axis=1)], []

    (out,), _ = _rowwise("rot_pack_bwd", fn, [*dqs, *dks, *dvs, *tabs], [], [(9 * wdt, BF16)], [], 256)
    return out


CONV_CB = 256
CONV_R = 128
CONV_PAD = 8


def _conv_window_sum(win, w, off, sign):
    acc = None
    for j in range(CONV_WIDTH):
        o = off + sign * j
        term = win[o:o + CONV_R, :] * w[j:j + 1, :]
        acc = term if acc is None else acc + term
    return acc


def _conv_fwd(xpre, cw, cb):
    s_len, ch = xpre.shape
    nchunk = s_len // CONV_R

    def body(x_ref, w_ref, b_ref, o_ref, xp):
        zero = jnp.zeros((CONV_PAD, CONV_CB), F32)
        xp[0:CONV_PAD, :] = zero
        xp[s_len + CONV_PAD:s_len + 2 * CONV_PAD, :] = zero

        def fill(ci, carry):
            base = pl.multiple_of(ci * CONV_R, CONV_R)
            xp[pl.ds(base + CONV_PAD, CONV_R), :] = x_ref[pl.ds(base, CONV_R), :]
            return carry

        lax.fori_loop(0, nchunk, fill, 0)
        w = w_ref[...]
        b = b_ref[...]

        def chunk(ci, carry):
            base = pl.multiple_of(ci * CONV_R, CONV_R)
            win = xp[pl.ds(base, CONV_R + 2 * CONV_PAD), :]
            u = _conv_window_sum(win, w, CONV_PAD - CONV_WIDTH // 2, 1) + b
            o_ref[pl.ds(base, CONV_R), :] = _silu(u)
            return carry

        lax.fori_loop(0, nchunk, chunk, 0)

    col = lambda r: pl.BlockSpec((r, CONV_CB), lambda j: (0, j))
    return pl.pallas_call(
        body, name="conv_fwd", grid=(ch // CONV_CB,), out_shape=jax.ShapeDtypeStruct((s_len, ch), F32),
        in_specs=[col(s_len), col(CONV_WIDTH), col(1)], out_specs=col(s_len),
        scratch_shapes=[pltpu.VMEM((s_len + 2 * CONV_PAD, CONV_CB), F32)],
        compiler_params=_params(("parallel",), VMEM_BIG),
    )(xpre, cw, cb)


def _conv_bwd(xpre, da, db, cw, cb):
    s_len, ch = xpre.shape
    nchunk = s_len // CONV_R
    half = CONV_WIDTH // 2

    def body(x_ref, da_ref, db_ref, w_ref, b_ref, dx_ref, gw_ref, gb_ref, xp, dcp):
        zero = jnp.zeros((CONV_PAD, CONV_CB), F32)
        for buf in (xp, dcp):
            buf[0:CONV_PAD, :] = zero
            buf[s_len + CONV_PAD:s_len + 2 * CONV_PAD, :] = zero

        def fill(ci, carry):
            base = pl.multiple_of(ci * CONV_R, CONV_R)
            xp[pl.ds(base + CONV_PAD, CONV_R), :] = x_ref[pl.ds(base, CONV_R), :]
            return carry

        lax.fori_loop(0, nchunk, fill, 0)
        w = w_ref[...]
        b = b_ref[...]

        def first(ci, carry):
            base = pl.multiple_of(ci * CONV_R, CONV_R)
            win = xp[pl.ds(base, CONV_R + 2 * CONV_PAD), :]
            u = _conv_window_sum(win, w, CONV_PAD - half, 1) + b
            sig = jax.nn.sigmoid(u)
            dc = (da_ref[pl.ds(base, CONV_R), :] + db_ref[pl.ds(base, CONV_R), :]) * (sig * (1.0 + u * (1.0 - sig)))
            dcp[pl.ds(base + CONV_PAD, CONV_R), :] = dc
            gb = carry[0] + jnp.sum(dc, axis=0, keepdims=True)
            gws = [carry[1 + j] + jnp.sum(dc * win[CONV_PAD - half + j:CONV_PAD - half + j + CONV_R, :],
                                          axis=0, keepdims=True) for j in range(CONV_WIDTH)]
            return (gb, *gws)

        z1 = jnp.zeros((1, CONV_CB), F32)
        sums = lax.fori_loop(0, nchunk, first, (z1,) * (1 + CONV_WIDTH))
        gb_ref[...] = sums[0]
        for j in range(CONV_WIDTH):
            gw_ref[j:j + 1, :] = sums[1 + j]

        def second(ci, carry):
            base = pl.multiple_of(ci * CONV_R, CONV_R)
            win = dcp[pl.ds(base, CONV_R + 2 * CONV_PAD), :]
            dx_ref[pl.ds(base, CONV_R), :] = _conv_window_sum(win, w, CONV_PAD + half, -1).astype(dx_ref.dtype)
            return carry

        lax.fori_loop(0, nchunk, second, 0)

    col = lambda r: pl.BlockSpec((r, CONV_CB), lambda j: (0, j))
    return pl.pallas_call(
        body, name="conv_bwd", grid=(ch // CONV_CB,),
        out_shape=[jax.ShapeDtypeStruct((s_len, ch), BF16), jax.ShapeDtypeStruct((CONV_WIDTH, ch), F32),
                   jax.ShapeDtypeStruct((1, ch), F32)],
        in_specs=[col(s_len), col(s_len), col(s_len), col(CONV_WIDTH), col(1)],
        out_specs=[col(s_len), col(CONV_WIDTH), col(1)],
        scratch_shapes=[pltpu.VMEM((s_len + 2 * CONV_PAD, CONV_CB), F32)] * 2,
        compiler_params=_params(("parallel",), VMEM_BIG),
    )(xpre, da, db, cw, cb)


SSD_GW = 256
SSD_N = 128


@jax.custom_vjp
def _expand(x, e):
    x1 = x.astype(BF16)
    r1 = x - x1.astype(F32)
    x2 = r1.astype(BF16)
    x3 = (r1 - x2.astype(F32)).astype(BF16)
    eb = e.astype(BF16)
    return _dg(x1, eb, 1, 0) + _dg(x2, eb, 1, 0) + _dg(x3, eb, 1, 0)


def _expand_fwd(x, e):
    return _expand(x, e), e


def _expand_bwd(e, g):
    g1 = g.astype(BF16)
    g2 = (g - g1.astype(F32)).astype(BF16)
    eb = e.astype(BF16)
    return _dg(g1, eb, 1, 1) + _dg(g2, eb, 1, 1), jnp.zeros_like(e)


_expand.defvjp(_expand_fwd, _expand_bwd)


def _ssd_mask(dirn):
    ri = lax.broadcasted_iota(jnp.int32, (CHUNK, CHUNK), 0)
    cj = lax.broadcasted_iota(jnp.int32, (CHUNK, CHUNK), 1)
    return (cj <= ri) if dirn == 0 else (cj >= ri)


def _ssd_rowsel(dirn):
    last = CHUNK - 1 if dirn == 0 else 0
    return (lax.broadcasted_iota(jnp.int32, (CHUNK, 1), 0) == last).astype(F32)


def _ssd_chunk_pre(dirn):
    nh = 2 * SSD_HEADS

    def f(dt, alog):
        da = dt * (-jnp.exp(alog))
        cum = _hnn(_ssd_mask(dirn).astype(F32), da)
        tot = jnp.sum(cum * _ssd_rowsel(dirn), axis=0, keepdims=True)
        hh = lax.broadcasted_iota(jnp.int32, (nh, SSD_HEADS * HEAD_DIM), 0)
        jj = lax.broadcasted_iota(jnp.int32, (nh, SSD_HEADS * HEAD_DIM), 1)
        expand = (hh == dirn * SSD_HEADS + jj // HEAD_DIM).astype(F32)
        return cum, cum.T, _expand(dt, expand), _expand(jnp.exp(tot - cum), expand), _expand(jnp.exp(cum), expand)

    return f


def _ssd_group_fn(g, dirn):
    nh = 2 * SSD_HEADS

    def f(xs, bm, cm, st, cum, cum_t, dt_e, w_e, ce_e):
        mask = _ssd_mask(dirn)
        xdt = xs * dt_e
        cd_e = jnp.sum(ce_e * _ssd_rowsel(dirn), axis=0, keepdims=True)
        cb = _bnt(cm, bm)
        lane_head = lax.broadcasted_iota(jnp.int32, (1, SSD_GW), 1) // HEAD_DIM
        decayed, inputs = [], []
        for j in range(4):
            hidx = dirn * SSD_HEADS + 4 * g + j
            col = jnp.sum(cum * (lax.broadcasted_iota(jnp.int32, (1, nh), 1) == hidx).astype(F32), axis=1, keepdims=True)
            row = jnp.sum(cum_t * (lax.broadcasted_iota(jnp.int32, (nh, 1), 0) == hidx).astype(F32), axis=0, keepdims=True)
            decayed.append(cb * jnp.exp(jnp.where(mask, col - row, NEG_BIG)))
            inputs.append(xdt * (lane_head == j).astype(F32))
        y = _bnn(cm, st) * ce_e + _bnn(jnp.concatenate(decayed, axis=1), jnp.concatenate(inputs, axis=0))
        st_out = st * cd_e + _btn(bm, xdt * w_e)
        return y, st_out

    return f


def _ssd_in_specs(kk):
    ln = CHUNK
    return [pl.BlockSpec((ln, 2048), lambda i: (kk(i), 0)),
            pl.BlockSpec((ln, 1024), lambda i: (kk(i), 2)),
            pl.BlockSpec((ln, 1024), lambda i: (kk(i), 3)),
            pl.BlockSpec((ln, 2 * SSD_HEADS), lambda i: (kk(i), 0)),
            pl.BlockSpec((1, 2 * SSD_HEADS), lambda i: (0, 0))]


def _ssd_fwd(xbc, dt, alog, dirn):
    s_len = xbc.shape[0]
    nc = s_len // CHUNK
    kk = (lambda i: i) if dirn == 0 else (lambda i: nc - 1 - i)

    def body(x_ref, b_ref, c_ref, dt_ref, al_ref, y_ref, sts_ref, st):
        @pl.when(pl.program_id(0) == 0)
        def _():
            st[...] = jnp.zeros_like(st)

        sts_ref[0] = st[...]
        cum, cum_t, dt_e, w_e, ce_e = _ssd_chunk_pre(dirn)(dt_ref[...], al_ref[...])
        for g in range(SSD_GROUPS):
            xc = slice(g * SSD_GW, (g + 1) * SSD_GW)
            gc = slice(g * SSD_N, (g + 1) * SSD_N)
            y, st_new = _ssd_group_fn(g, dirn)(x_ref[:, xc], b_ref[:, gc], c_ref[:, gc], st[:, xc], cum, cum_t,
                                               dt_e[:, xc], w_e[:, xc], ce_e[:, xc])
            y_ref[:, xc] = y
            st[:, xc] = st_new

    return pl.pallas_call(
        body, name=f"ssd_fwd_d{dirn}", grid=(nc,),
        out_shape=[jax.ShapeDtypeStruct((s_len, 2048), F32), jax.ShapeDtypeStruct((nc, SSD_N, 2048), F32)],
        in_specs=_ssd_in_specs(kk),
        out_specs=[pl.BlockSpec((CHUNK, 2048), lambda i: (kk(i), 0)),
                   pl.BlockSpec((1, SSD_N, 2048), lambda i: (kk(i), 0, 0))],
        scratch_shapes=[pltpu.VMEM((SSD_N, 2048), F32)],
        compiler_params=_params(("arbitrary",), VMEM_BIG),
    )(xbc, xbc, xbc, dt, alog)


def _ssd_bwd(xbc, dt, alog, states, dy, d_e, dirn):
    s_len = xbc.shape[0]
    nc = s_len // CHUNK
    kk = (lambda i: nc - 1 - i) if dirn == 0 else (lambda i: i)

    def body(x_ref, b_ref, c_ref, dt_ref, al_ref, sts_ref, dy_ref, de_ref, dx_ref, ddt_ref, dal_ref, dst):
        @pl.when(pl.program_id(0) == 0)
        def _():
            dst[...] = jnp.zeros_like(dst)
            dal_ref[...] = jnp.zeros_like(dal_ref)

        (cum, cum_t, dt_e, w_e, ce_e), pre_vjp = jax.vjp(_ssd_chunk_pre(dirn), dt_ref[...], al_ref[...])
        dcum = jnp.zeros_like(cum)
        dcum_t = jnp.zeros_like(cum_t)
        d_dt_e, d_w_e, d_ce_e = [], [], []
        for g in range(SSD_GROUPS):
            xc = slice(g * SSD_GW, (g + 1) * SSD_GW)
            gc = slice(g * SSD_N, (g + 1) * SSD_N)
            _, vjp = jax.vjp(_ssd_group_fn(g, dirn), x_ref[:, xc], b_ref[:, gc], c_ref[:, gc], sts_ref[0, :, xc], cum, cum_t,
                             dt_e[:, xc], w_e[:, xc], ce_e[:, xc])
            dyg = dy_ref[:, xc]
            dxs, dbm, dcm, dst_g, dcum_g, dcum_t_g, ddte_g, dwe_g, dcee_g = vjp((dyg, dst[:, xc]))
            if dirn == 0:
                dxs = dxs + dyg * de_ref[:, xc]
            dx_ref[:, xc] = dxs
            dx_ref[:, 2048 + g * SSD_N:2048 + (g + 1) * SSD_N] = dbm
            dx_ref[:, 3072 + g * SSD_N:3072 + (g + 1) * SSD_N] = dcm
            dst[:, xc] = dst_g
            dcum = dcum + dcum_g
            dcum_t = dcum_t + dcum_t_g
            d_dt_e.append(ddte_g)
            d_w_e.append(dwe_g)
            d_ce_e.append(dcee_g)
        ddt, dal = pre_vjp((dcum, dcum_t, jnp.concatenate(d_dt_e, axis=1), jnp.concatenate(d_w_e, axis=1),
                            jnp.concatenate(d_ce_e, axis=1)))
        ddt_ref[...] = ddt
        dal_ref[...] += dal

    return pl.pallas_call(
        body, name=f"ssd_bwd_d{dirn}", grid=(nc,),
        out_shape=[jax.ShapeDtypeStruct((s_len, 4096), F32), jax.ShapeDtypeStruct((s_len, 2 * SSD_HEADS), F32),
                   jax.ShapeDtypeStruct((1, 2 * SSD_HEADS), F32)],
        in_specs=_ssd_in_specs(kk) + [pl.BlockSpec((1, SSD_N, 2048), lambda i: (kk(i), 0, 0)),
                                      pl.BlockSpec((CHUNK, 2048), lambda i: (kk(i), 0)),
                                      pl.BlockSpec((1, 2048), lambda i: (0, 0))],
        out_specs=[pl.BlockSpec((CHUNK, 4096), lambda i: (kk(i), 0)),
                   pl.BlockSpec((CHUNK, 2 * SSD_HEADS), lambda i: (kk(i), 0)),
                   pl.BlockSpec((1, 2 * SSD_HEADS), lambda i: (0, 0))],
        scratch_shapes=[pltpu.VMEM((SSD_N, 2048), F32)],
        compiler_params=_params(("arbitrary",), VMEM_BIG),
    )(xbc, xbc, xbc, dt, alog, states, dy, d_e)


def _gate_norm_fn(yf, yb, xs, z, d_e, nw):
    yg = (yf + yb + xs * d_e) * _silu(z)
    return yg * lax.rsqrt(jnp.mean(yg * yg, axis=-1, keepdims=True) + NORM_EPS) * nw


def _gate_norm_fwd(yf, yb, xbc, z, d_e, nw):
    (u,), _ = _rowwise("ssd_gate_norm", lambda *a: ([_gate_norm_fn(*a)], []),
                       [yf, yb, (xbc, 2048, 0), z], [d_e, nw], [(2048, BF16)], [], 256)
    return u


def _gate_norm_bwd(du, yf, yb, xbc, z, d_e, nw):
    def fn(du, yf, yb, xs, z, d_e, nw):
        _, vjp = jax.vjp(_gate_norm_fn, yf, yb, xs, z, d_e, nw)
        dyf, _, _, dz, dde, dnw = vjp(du)
        hh = lax.broadcasted_iota(jnp.int32, (2048, SSD_HEADS), 0) // HEAD_DIM
        jj = lax.broadcasted_iota(jnp.int32, (2048, SSD_HEADS), 1)
        return [dyf, dz], [dnw, _hnn(jnp.broadcast_to(dde, (8, 2048)), (hh == jj).astype(F32))[0:1]]

    (dys, dz), (g_nw, g_d) = _rowwise("ssd_gate_norm_bwd", fn, [du, yf, yb, (xbc, 2048, 0), z], [d_e, nw],
                                      [(2048, F32), (2048, BF16)], [(1, 2048), (1, SSD_HEADS)], 128)
    return dys, dz, g_nw, g_d


def _loss_bwd(x1, y1, tgt, gate, fnw):
    dm = x1.shape[1]

    def fn(x1, y1, tgt, gate, fnw):
        def head(x2, fnw):
            yf = (x2 * lax.rsqrt(jnp.mean(x2 * x2, axis=-1, keepdims=True) + NORM_EPS)) * fnw
            err = yf - tgt
            return 0.5 * jnp.sum(jnp.mean(err * err, axis=-1, keepdims=True), axis=0, keepdims=True)

        x2 = x1 + gate * y1
        loss, vjp = jax.vjp(head, x2, fnw)
        dx2, dfnw = vjp(jnp.ones((1, 1), F32))
        return [dx2, gate * dx2], [dfnw, jnp.sum(dx2 * y1, axis=0, keepdims=True), jnp.broadcast_to(loss, (1, 128))]

    (dx2, dy1), (g_fnw, dgate, loss) = _rowwise("loss_bwd", fn, [x1, y1, tgt], [gate, fnw], [(dm, F32), (dm, BF16)],
                                                [(1, dm), (1, dm), (1, 128)], 256)
    return dx2, dy1, g_fnw, dgate, loss


def _gate_bwd(dx, y, gate):
    dm = dx.shape[1]
    (dy,), (dgate,) = _rowwise("gate_bwd", lambda dx, y, gate: ([gate * dx], [jnp.sum(dx * y, axis=0, keepdims=True)]),
                               [dx, y], [gate], [(dm, BF16)], [(1, dm)], 512)
    return dy, dgate


def _softplus_fwd(dt_raw, bias):
    (dt,), _ = _rowwise("dt_softplus", lambda r, b: ([jax.nn.softplus(r + b)], []), [dt_raw], [bias],
                        [(dt_raw.shape[1], F32)], [], 512)
    return dt


def _softplus_bwd(ddt_f, ddt_b, dt_raw, bias):
    def fn(df, db, r, b):
        g = (df + db) * jax.nn.sigmoid(r + b)
        return [g], [jnp.sum(g, axis=0, keepdims=True)]

    w = dt_raw.shape[1]
    (g,), (gb,) = _rowwise("dt_softplus_bwd", fn, [ddt_f, ddt_b, dt_raw], [bias], [(w, BF16)], [(1, w)], 512)
    return g, gb


def _whole(a):
    nd = len(a.shape)
    return pl.BlockSpec(a.shape, lambda *_: (0,) * nd)


def _mod_part(c_all, mod_w):
    nl, _, ncol = mod_w.shape
    nb = c_all.shape[0]

    def body(c_ref, w_ref, o_ref):
        cond = _silu(c_ref[...])
        for i in range(nl):
            o_ref[i * nb:(i + 1) * nb, :] = _nn(cond, w_ref[i])

    return pl.pallas_call(body, name="mod_part", out_shape=jax.ShapeDtypeStruct((nl * nb, ncol), F32),
                          compiler_params=_params(None, VMEM_BIG))(c_all, mod_w)


def _mod_finish(mod_nb, mod_b):
    def body(a_ref, b_ref, o_ref):
        o_ref[...] = a_ref[...] + b_ref[...]

    return pl.pallas_call(body, name="mod_finish", out_shape=jax.ShapeDtypeStruct(mod_b.shape, F32))(mod_nb, mod_b)


def _mod_grad(c_all, dmod_sh):
    nl, nb, ncol = dmod_sh.shape
    dm = c_all.shape[1]

    def body(c_ref, d_ref, o_ref):
        cond = _silu(c_ref[...])
        for i in range(nl):
            o_ref[i] = _tn(cond, d_ref[i])

    return pl.pallas_call(body, name="mod_grad", out_shape=jax.ShapeDtypeStruct((nl, dm, ncol), F32),
                          compiler_params=_params(None, VMEM_BIG))(c_all, dmod_sh)


PACK_ROWS = 16
PACK_COLS = 1024


def _pack_small(rows6, nw2, fnw, b64, a64, d32, extra=None):
    args = [rows6, nw2, fnw, b64, a64, d32] + ([extra] if extra is not None else [])

    def body(*refs):
        o_ref = refs[-1]
        o_ref[...] = jnp.zeros_like(o_ref)
        o_ref[0:6, :] = refs[0][...]
        o_ref[6:8, :] = refs[1][...]
        o_ref[8:9, :] = refs[2][...]
        o_ref[9:10, 0:64] = refs[3][...]
        o_ref[9:10, 64:128] = refs[4][...]
        o_ref[9:10, 128:160] = refs[5][...]
        if extra is not None:
            o_ref[9:10, 256:384] = refs[6][...]

    return pl.pallas_call(body, name="pack_small", out_shape=jax.ShapeDtypeStruct((PACK_ROWS, PACK_COLS), F32))(*args)


def _unpack_small(p):
    return (p[0:6].reshape(2, 3 * PACK_COLS), p[6:8], p[8], p[9, 0:64].reshape(1, 2, 32), p[9, 64:128].reshape(1, 2, 32),
            p[9, 128:160].reshape(1, 32))


def _pack_ssd_small(cw, cb, nw):
    def body(cw_ref, cb_ref, nw_ref, o_ref):
        o_ref[...] = jnp.zeros_like(o_ref)
        o_ref[0:5, :] = cw_ref[...]
        o_ref[5:6, :] = cb_ref[...]
        o_ref[6:7, 0:256] = nw_ref[...]

    return pl.pallas_call(body, name="pack_ssd_small", out_shape=jax.ShapeDtypeStruct((8, 512), F32))(cw, cb, nw)


def _adamw(name, w, parts, m, v, tr, tc=None):
    r_, c_ = w.shape
    p_ = parts.shape[0]
    tr = min(tr, r_)
    tc = c_ if tc is None else tc
    assert r_ % tr == 0 and c_ % tc == 0

    def body(w_ref, p_ref, m_ref, v_ref, g_ref, d_ref, m2_ref, v2_ref):
        g = p_ref[0].astype(F32)
        for s in range(1, p_):
            g = g + p_ref[s].astype(F32)
        m2 = ADAM_B1 * m_ref[...] + (1.0 - ADAM_B1) * g
        v2 = ADAM_B2 * v_ref[...] + (1.0 - ADAM_B2) * (g * g)
        m_hat = m2 / (1.0 - ADAM_B1 ** ADAM_STEP)
        v_hat = v2 / (1.0 - ADAM_B2 ** ADAM_STEP)
        g_ref[...] = g
        d_ref[...] = -ADAM_LR * (m_hat / (jnp.sqrt(v_hat) + ADAM_EPS) + ADAM_WD * w_ref[...])
        m2_ref[...] = m2
        v2_ref[...] = v2

    blk = pl.BlockSpec((tr, tc), lambda i, j: (i, j))
    return pl.pallas_call(
        body, name=name, grid=(r_ // tr, c_ // tc), out_shape=[jax.ShapeDtypeStruct((r_, c_), F32)] * 4,
        in_specs=[blk, pl.BlockSpec((p_, tr, tc), lambda i, j: (0, i, j)), blk, blk], out_specs=[blk] * 4,
        compiler_params=_params(("parallel", "parallel"), VMEM_BIG),
    )(w, parts, m, v)


def _dev_index(p):
    return 4 * p[0] + 2 * p[1] + p[2]


def _all_gather(name, xs):
    n = len(xs)
    hbm = pl.BlockSpec(memory_space=pl.ANY)

    def body(*refs):
        x_refs, o_refs = refs[:n], refs[n:2 * n]
        send_sems, recv_sems, local_sems = refs[2 * n:]
        x, y, c = lax.axis_index("x"), lax.axis_index("y"), lax.axis_index("c")
        me, sibling = (x, y, c), (x, y, 1 - c)
        chips = [(1 - x, y), (x, 1 - y), (1 - x, 1 - y)]

        def copy(a, k, block, to, src=None):
            dst = o_refs[a].at[_dev_index(block)]
            return pltpu.make_async_remote_copy(
                src_ref=dst if src is None else src, dst_ref=dst, send_sem=send_sems.at[a, k],
                recv_sem=recv_sems.at[a, k], device_id=to, device_id_type=MESH)

        mine = [pltpu.make_async_copy(x_refs[a], o_refs[a].at[_dev_index(me)], local_sems.at[a]) for a in range(n)]
        for cp in mine:
            cp.start()
        first = []
        for a in range(n):
            first.append(copy(a, 0, me, sibling, src=x_refs[a]))
            first += [copy(a, 1 + j, me, (*chip, c), src=x_refs[a]) for j, chip in enumerate(chips)]
        for cp in first:
            cp.start()
        passed = []
        for j, chip in enumerate(chips):
            for a in range(n):
                copy(a, 1 + j, (*chip, c), me).wait_recv()
                cp = copy(a, 4 + j, (*chip, c), sibling)
                cp.start()
                passed.append(cp)
        for a in range(n):
            copy(a, 0, sibling, me).wait_recv()
            for j, chip in enumerate(chips):
                copy(a, 4 + j, (*chip, 1 - c), me).wait_recv()
        for cp in first + passed:
            cp.wait_send()
        for cp in mine:
            cp.wait()

    return pl.pallas_call(
        body, name=name, out_shape=[jax.ShapeDtypeStruct((NDEV, *x.shape), x.dtype) for x in xs],
        in_specs=[hbm] * n, out_specs=[hbm] * n,
        scratch_shapes=[pltpu.SemaphoreType.DMA((n, 7)), pltpu.SemaphoreType.DMA((n, 7)), pltpu.SemaphoreType.DMA((n,))],
    )(*xs)


_HBM = pl.BlockSpec(memory_space=pltpu.HBM)
_SEM = pl.BlockSpec(memory_space=pltpu.SEMAPHORE)
_EFFECT = pltpu.SideEffectType.DATAFLOW_SIDE_EFFECTING


def _mesh_position():
    return lax.axis_index("x"), lax.axis_index("y"), lax.axis_index("c")


def _peers(me):
    return [(k, tuple(1 - v if (k >> b) & 1 else v for v, b in zip(me, (2, 1, 0)))) for k in range(1, NDEV)]


def _landing_zones(name, xs, scatter):
    me = _dev_index(_mesh_position()).astype(jnp.int32).reshape(1)
    lands = []
    for a, x in enumerate(xs):
        rows, cols = x.shape[-2:]
        tr = 256 if rows % 256 == 0 else rows

        def body(me_ref, x_ref, o_ref):
            o_ref[...] = x_ref[...]

        if scatter:
            in_spec = pl.BlockSpec((None, tr, cols), lambda i, me_ref: (me_ref[0], i, 0))
        else:
            in_spec = pl.BlockSpec((tr, cols), lambda i, me_ref: (i, 0))
        lands.append(pl.pallas_call(
            body, name=f"{name}_{a}", out_shape=jax.ShapeDtypeStruct((NDEV, rows, cols), x.dtype),
            grid_spec=pltpu.PrefetchScalarGridSpec(
                num_scalar_prefetch=1, grid=(rows // tr,), in_specs=[in_spec],
                out_specs=pl.BlockSpec((None, tr, cols), lambda i, me_ref: (me_ref[0], i, 0))),
            compiler_params=_params(("arbitrary",)),
        )(me, x))
    return lands


def _exchange_copies(x_refs, land_refs, send_sems, recv_sems, scatter):
    me = _mesh_position()
    out = []
    for k, peer in _peers(me):
        for a, (x_ref, land_ref) in enumerate(zip(x_refs, land_refs)):
            sem = a * (NDEV - 1) + k - 1
            out.append(pltpu.make_async_remote_copy(
                src_ref=x_ref.at[_dev_index(peer)] if scatter else x_ref, dst_ref=land_ref.at[_dev_index(me)],
                send_sem=send_sems.at[sem], recv_sem=recv_sems.at[sem], device_id=peer, device_id_type=MESH))
    return out


def _exchange_start(name, xs, lands, scatter, dep):
    n = len(xs)

    def body(*refs):
        x_refs, land_refs = refs[:n], refs[n:2 * n]
        send_sems, recv_sems = refs[2 * n + 1], refs[2 * n + 2]
        token = refs[-1]
        for cp in _exchange_copies(x_refs, land_refs, send_sems, recv_sems, scatter):
            cp.start()
        token[...] = jnp.zeros_like(token)

    sems = pltpu.SemaphoreType.DMA((n * (NDEV - 1),))
    res = pl.pallas_call(
        body, name=name,
        out_shape=(sems, sems, *[pltpu.HBM(a.shape, a.dtype) for a in (*xs, *lands)], jax.ShapeDtypeStruct((8, 128), F32)),
        in_specs=[_HBM] * (2 * n) + [pl.BlockSpec(memory_space=pl.ANY)],
        out_specs=(_SEM, _SEM, *[_HBM] * (2 * n), pl.BlockSpec(memory_space=pltpu.VMEM)),
        input_output_aliases={i: 2 + i for i in range(2 * n)},
        compiler_params=pltpu.CompilerParams(has_side_effects=_EFFECT),
    )(*[pltpu.with_memory_space_constraint(a, pltpu.HBM) for a in (*xs, *lands)], dep)
    return res[:-1], res[-1]


def _exchange_wait(name, handles, scatter, after):
    send_sems, recv_sems = handles[0], handles[1]
    bufs = handles[2:]
    n = len(bufs) // 2

    def body(*refs):
        x_refs, land_refs = refs[:n], refs[n:2 * n]
        s_sems, r_sems = refs[2 * n], refs[2 * n + 1]
        for cp in _exchange_copies(x_refs, land_refs, s_sems, r_sems, scatter):
            cp.wait_send()
            cp.wait_recv()

    res = pl.pallas_call(
        body, name=name, out_shape=tuple(pltpu.HBM(a.shape, a.dtype) for a in bufs),
        in_specs=[_HBM] * (2 * n) + [_SEM, _SEM, pl.BlockSpec(memory_space=pl.ANY)], out_specs=tuple([_HBM] * (2 * n)),
        input_output_aliases={i: i for i in range(2 * n)},
        compiler_params=pltpu.CompilerParams(has_side_effects=_EFFECT),
    )(*bufs, send_sems, recv_sems, after)
    return res[n:]


def kernel(x, c, positions, norm_w, mod_w, mod_b, attn_w_in, attn_w_out, ssd_w_in, ssd_conv_w, ssd_conv_b, ssd_dt_bias, ssd_a_log, ssd_d, ssd_norm_w, ssd_w_out, final_norm_w, loss_target, m_norm_w, m_mod_w, m_mod_b, m_attn_w_in, m_attn_w_out, m_ssd_w_in, m_ssd_conv_w, m_ssd_conv_b, m_ssd_dt_bias, m_ssd_a_log, m_ssd_d, m_ssd_norm_w, m_ssd_w_out, m_final_norm_w, v_norm_w, v_mod_w, v_mod_b, v_attn_w_in, v_attn_w_out, v_ssd_w_in, v_ssd_conv_w, v_ssd_conv_b, v_ssd_dt_bias, v_ssd_a_log, v_ssd_d, v_ssd_norm_w, v_ssd_w_out, v_final_norm_w):
    s_len, dm = x.shape[1], x.shape[2]
    me = 4 * lax.axis_index("x") + 2 * lax.axis_index("y") + lax.axis_index("c")
    x0 = x.reshape(s_len, dm)
    tgt = loss_target.reshape(s_len, dm)
    aw = 3 * 512
    si = 2 * dm
    sxbc = 2 * si
    n_ssd_in = ssd_w_in.shape[2] * NDEV

    g_ai, c_all = _all_gather("gather_attn_w_in", [attn_w_in[0].astype(BF16), c])
    w_ai = g_ai.transpose(1, 0, 2).reshape(dm, 4 * aw)
    c_all = c_all.reshape(NDEV, dm)
    ssd_small = _pack_ssd_small(ssd_conv_w[0], ssd_conv_b, ssd_norm_w)
    late_shards = [attn_w_out[0].astype(BF16), ssd_w_in[0].T.astype(BF16), ssd_w_out[0].astype(BF16), ssd_small]
    w_handles, w_token = _exchange_start("weights_start", late_shards, _landing_zones("weights_place", late_shards, False),
                                         False, g_ai)

    part = _mod_part(c_all, mod_w)
    (part_all,) = _all_gather("gather_mod", [part])
    mod_nb = jnp.stack([lax.dynamic_index_in_dim(part_all, i * NDEV + me, axis=1, keepdims=False).reshape(3 * dm)
                        for i in range(2)])
    mod = _mod_finish(mod_nb, mod_b)
    shift = [mod[i:i + 1, 0:dm] for i in range(2)]
    scale = [mod[i:i + 1, dm:2 * dm] for i in range(2)]
    gate = [mod[i:i + 1, 2 * dm:3 * dm] for i in range(2)]
    nw = [norm_w[i:i + 1] + w_token[0:1, 0:1] for i in range(2)]

    hn0 = _norm_mod_fwd("norm0", x0, nw[0], scale[0], shift[0])
    inv_freq = ROPE_THETA ** (-jnp.arange(0, ROT_DIM, 2, dtype=F32) / ROT_DIM)
    lane = jnp.arange(128) % HEAD_DIM
    inv_row = jnp.where(lane < ROT_DIM, inv_freq[lane % (ROT_DIM // 2)], 0.0).reshape(1, 128).astype(F32)
    tabs = _rope_tables(positions.reshape(s_len, 1), inv_row)
    qk = _matmul("proj_qk", hn0, w_ai, "nn", F32, MM_T, MM_T, dm, epilogue=_rot_fwd, mrows=tabs, n_out=2 * aw)
    v = _matmul("proj_v", hn0, w_ai, "nn", F32, MM_T, aw // 2, dm, b_noff=2 * aw, n_out=aw)
    z0 = _matmul("proj_z", hn0, w_ai, "nn", F32, MM_T, aw // 2, dm, b_noff=3 * aw, n_out=aw)
    att = [_attn_fwd(g, qk, v) for g in range(3)]
    os_, lses = [a[0] for a in att], [a[1] for a in att]
    g_ao, g_si, g_so, g_small = _exchange_wait("weights_wait", w_handles, False, lses[2])
    w_ao = g_ao.reshape(aw, dm)
    w_si_t = g_si.reshape(n_ssd_in, dm)
    w_so = g_so.reshape(si, dm)
    conv_w = g_small[:, 0:CONV_WIDTH, :].transpose(1, 0, 2).reshape(CONV_WIDTH, sxbc)
    conv_b = g_small[:, 5, :].reshape(1, sxbc)
    snw = g_small[:, 6, 0:si // NDEV].reshape(1, si)
    a0, y0, x1 = _attn_out(os_, lses, z0, x0, gate[0], w_ao)

    hn1 = _norm_mod_fwd("norm1", x1, nw[1], scale[1], shift[1])
    ndt = 2 * SSD_HEADS
    z1 = _matmul("ssd_proj_z", hn1, w_si_t, "nt", F32, MM_T, MM_T, dm, n_out=si)
    xpre = _matmul("ssd_proj_xbc", hn1, w_si_t, "nt", F32, MM_T, MM_T, dm, b_noff=si, n_out=sxbc)
    dt_raw = _matmul("ssd_proj_dt", hn1, w_si_t, "nt", F32, MM_T, ndt, dm, b_noff=si + sxbc, n_out=ndt)
    xbc = _conv_fwd(xpre, conv_w, conv_b)
    dt_bias = ssd_dt_bias.reshape(1, 2 * SSD_HEADS)
    alog = ssd_a_log.reshape(1, 2 * SSD_HEADS)
    dt = _softplus_fwd(dt_raw, dt_bias)
    y_f, st_f = _ssd_fwd(xbc, dt, alog, 0)
    y_b, st_b = _ssd_fwd(xbc, dt, alog, 1)
    d_e = jnp.repeat(ssd_d.reshape(SSD_HEADS), HEAD_DIM).reshape(1, si)
    u = _gate_norm_fwd(y_f, y_b, xbc, z1, d_e, snw)
    y1 = _matmul("ssd_out", u, w_so, "nn", F32, MM_T, MM_T, si)

    fnw = final_norm_w.reshape(1, dm)
    dx2, dy1, g_fnw, dgate1, loss_part = _loss_bwd(x1, y1, tgt, gate[1], fnw)
    du = _matmul("ssd_out_dx", dy1, w_so, "nt", F32, MM_T, MM_T, dm)
    gw_so = _matmul("ssd_out_dw", u, dy1, "tn", BF16, MM_T, MM_T, MM_T)
    dys, dz1, g_snw, g_d = _gate_norm_bwd(du, y_f, y_b, xbc, z1, d_e, snw)
    dxbc_f, ddt_f, dalog_f = _ssd_bwd(xbc, dt, alog, st_f, dys, d_e, 0)
    dxbc_b, ddt_b, dalog_b = _ssd_bwd(xbc, dt, alog, st_b, dys, d_e, 1)
    dpre, g_cw, g_cb = _conv_bwd(xpre, dxbc_f, dxbc_b, conv_w, conv_b)
    ddt_raw, g_dtb = _softplus_bwd(ddt_f, ddt_b, dt_raw, dt_bias)
    dhn1 = [_matmul("ssd_proj_z_dx", dz1, w_si_t, "nn", F32, MM_T, MM_T, MM_T),
            _matmul("ssd_proj_xbc_dx", dpre, w_si_t, "nn", F32, MM_T, MM_T, MM_T, b_koff=si),
            _matmul("ssd_proj_dt_dx", ddt_raw, w_si_t, "nn", F32, MM_T, MM_T, ndt, b_koff=si + sxbc)]
    gw_si_t = jnp.concatenate([_matmul("ssd_proj_z_dw", dz1, hn1, "tn", BF16, MM_T, MM_T, MM_T),
                               _matmul("ssd_proj_xbc_dw", dpre, hn1, "tn", BF16, MM_T, MM_T, MM_T),
                               _matmul("ssd_proj_dt_dw", ddt_raw, hn1, "tn", BF16, ndt, MM_T, MM_T)], axis=0)
    dx1, g_nw1, dsc1, dsh1 = _norm_mod_bwd("norm1_bwd", x1, dhn1, dx2, nw[1], scale[1], shift[1])

    l1_grads = [gw_so.reshape(NDEV, si // NDEV, dm), gw_si_t.reshape(NDEV, n_ssd_in // NDEV, dm),
                _pack_ssd_small_blocks(g_cw, g_cb, g_snw)]
    l1_handles, l1_token = _exchange_start("l1_grads_start", l1_grads, _landing_zones("l1_grads_place", l1_grads, True),
                                           True, dx1)

    dy0, dgate0 = _gate_bwd(dx1, y0, gate[0] + l1_token[0:1, 0:1])
    da0 = _matmul("attn_out_dx", dy0, w_ao, "nt", F32, MM_T, aw // 2, dm)
    gw_ao = _matmul("attn_out_dw", a0, dy0, "tn", BF16, aw // 2, MM_T, MM_T)
    dos, dls, dz0 = _mix_bwd(da0, os_, lses, z0)
    datt = [_attn_bwd(g, qk, v, os_[g], lses[g], dos[g], dls[g]) for g in range(3)]
    dqkv = _rot_pack_bwd([t[0] for t in datt], [t[1] for t in datt], [t[2] for t in datt], tabs)
    wcol = attn_w_in.shape[2]
    gw_ai = jnp.concatenate([
        _matmul("proj_qkv_dw", hn0, dqkv, "tn", BF16, MM_T, wcol, MM_T, out_blocks=3 * aw // wcol),
        _matmul("proj_z_dw", hn0, dz0, "tn", BF16, MM_T, wcol, MM_T, out_blocks=aw // wcol)], axis=0)
    l0_grads = [gw_ai, gw_ao.reshape(NDEV, aw // NDEV, dm)]
    l0_handles, l0_token = _exchange_start("l0_grads_start", l0_grads, _landing_zones("l0_grads_place", l0_grads, True),
                                           True, dqkv)
    zero_row = jnp.tile(l0_token[0:1], (1, dm // 128))
    after_start = lambda acc, t: acc + t
    dhn0 = [_matmul("proj_qkv_dx", dqkv, w_ai, "nt", F32, MM_T, MM_T, aw, n_out=dm, epilogue=after_start, ncols=(zero_row,)),
            _matmul("proj_z_dx", dz0, w_ai, "nt", F32, MM_T, MM_T, aw, b_koff=3 * aw, n_out=dm)]
    dx0, g_nw0, dsc0, dsh0 = _norm_mod_bwd("norm0_bwd", x0, dhn0, dx1, nw[0], scale[0], shift[0])

    rows6 = jnp.concatenate([dsh0, dsc0, dgate0, dsh1, dsc1, dgate1], axis=0)
    small_g = _pack_small(rows6, jnp.concatenate([g_nw0, g_nw1], axis=0), g_fnw, g_dtb, dalog_f + dalog_b, g_d, loss_part)
    (small_all,) = _all_gather("gather_small_grads", [small_g])
    small_w = _pack_small(mod_b.reshape(6, dm), norm_w, fnw, dt_bias, alog, ssd_d)
    small_m = _pack_small(m_mod_b.reshape(6, dm), m_norm_w, m_final_norm_w.reshape(1, dm), m_ssd_dt_bias.reshape(1, 64),
                          m_ssd_a_log.reshape(1, 64), m_ssd_d)
    small_v = _pack_small(v_mod_b.reshape(6, dm), v_norm_w, v_final_norm_w.reshape(1, dm), v_ssd_dt_bias.reshape(1, 64),
                          v_ssd_a_log.reshape(1, 64), v_ssd_d)
    small_out = _adamw("adamw_small", small_w, small_all, small_m, small_v, PACK_ROWS)
    loss = small_out[0][9, 256]
    sg, sd, sm, sv = (_unpack_small(p) for p in small_out)

    ncol = mod_w.shape[2]
    dmod_all = small_all[:, 0:6, :].reshape(NDEV, 2, 3 * dm)
    dmod_sh = lax.dynamic_slice_in_dim(dmod_all, me * ncol, ncol, axis=2).transpose(1, 0, 2)
    g_modw = _mod_grad(c_all, dmod_sh).reshape(1, 2 * dm, ncol)
    modw_out = _adamw("adamw_mod_w", mod_w.reshape(2 * dm, ncol), g_modw, m_mod_w.reshape(2 * dm, ncol),
                      v_mod_w.reshape(2 * dm, ncol), 256)

    r_so, r_si, r_small = _exchange_wait("l1_grads_wait", l1_handles, True, modw_out[0])
    si_out = [o.T for o in _adamw("adamw_ssd_w_in", ssd_w_in[0].T, r_si, m_ssd_w_in[0].T, v_ssd_w_in[0].T, n_ssd_in // NDEV, 256)]
    so_out = _adamw("adamw_ssd_w_out", ssd_w_out[0], r_so, m_ssd_w_out[0], v_ssd_w_out[0], 256)
    ssd_small_m = _pack_ssd_small(m_ssd_conv_w[0], m_ssd_conv_b, m_ssd_norm_w)
    ssd_small_v = _pack_ssd_small(v_ssd_conv_w[0], v_ssd_conv_b, v_ssd_norm_w)
    ss_out = _adamw("adamw_ssd_small", ssd_small, r_small, ssd_small_m, ssd_small_v, 8)
    r_ai, r_ao = _exchange_wait("l0_grads_wait", l0_handles, True, so_out[0])
    ai_out = _adamw("adamw_attn_w_in", attn_w_in[0], r_ai, m_attn_w_in[0], v_attn_w_in[0], 256)
    ao_out = _adamw("adamw_attn_w_out", attn_w_out[0], r_ao, m_attn_w_out[0], v_attn_w_out[0], 192)

    def ssd_small_unpack(p):
        return p[0:5][None], p[5:6], p[6:7, 0:si // NDEV]

    cwo, cbo, nwo = zip(*(ssd_small_unpack(p) for p in ss_out))
    per_kind = []
    for k in range(4):
        s = (sg, sd, sm, sv)[k]
        per_kind.append([
            s[1], modw_out[k].reshape(mod_w.shape), s[0], ai_out[k][None], ao_out[k][None], si_out[k][None],
            cwo[k], cbo[k], s[3], s[4], s[5], nwo[k], so_out[k][None], s[2]])
    return (loss, dx0.reshape(x.shape), *per_kind[0], *per_kind[1], *per_kind[2], *per_kind[3])


def _pack_ssd_small_blocks(g_cw, g_cb, g_nw):
    nper = g_cw.shape[1] // NDEV
    nwper = g_nw.shape[1] // NDEV

    def body(cw_ref, cb_ref, nw_ref, o_ref):
        o_ref[...] = jnp.zeros_like(o_ref)
        for d in range(NDEV):
            o_ref[d, 0:5, :] = cw_ref[:, d * nper:(d + 1) * nper]
            o_ref[d, 5:6, :] = cb_ref[:, d * nper:(d + 1) * nper]
            o_ref[d, 6:7, 0:nwper] = nw_ref[:, d * nwper:(d + 1) * nwper]

    return pl.pallas_call(body, name="pack_ssd_small_grads", out_shape=jax.ShapeDtypeStruct((NDEV, 8, nper), F32))(g_cw, g_cb, g_nw)
```

```python
import functools
import math

import jax
import jax.numpy as jnp
from jax import lax
from jax.experimental import pallas as pl
from jax.experimental.pallas import tpu as pltpu

F32 = jnp.float32
BF16 = jnp.bfloat16
HI = lax.Precision.HIGHEST
MESH = pl.DeviceIdType.MESH
NDEV = 8

NORM_EPS = 1e-6
ROPE_THETA = 500000.0
ROT_DIM = 16
HEAD_DIM = 64
DILATIONS = (1, 4, 16)
BAND = 64
NEG_BIG = -1e30
CHUNK = 128
SSD_HEADS = 32
SSD_GROUPS = 8
CONV_WIDTH = 5

ADAM_LR = 0.001
ADAM_B1 = 0.9
ADAM_B2 = 0.999
ADAM_EPS = 1e-08
ADAM_WD = 0.01
ADAM_STEP = 10

VMEM_BIG = 56 * 1024 * 1024
MM_T = 1024


def _params(sem=None, vmem=None):
    kw = {}
    if sem is not None:
        kw["dimension_semantics"] = sem
    if vmem is not None:
        kw["vmem_limit_bytes"] = vmem
    return pltpu.CompilerParams(**kw)


def _dg(a, b, ca, cb, prec=None):
    return lax.dot_general(a, b, (((ca,), (cb,)), ((), ())), preferred_element_type=F32, precision=prec)


def _nn(a, b):
    return _dg(a.astype(BF16), b.astype(BF16), 1, 0)


def _nt(a, b):
    return _dg(a.astype(BF16), b.astype(BF16), 1, 1)


def _tn(a, b):
    return _dg(a.astype(BF16), b.astype(BF16), 0, 0)


def _hnn(a, b):
    return _dg(a, b, 1, 0, HI)


@jax.custom_vjp
def _bnn(a, b):
    return _nn(a, b)


_bnn.defvjp(lambda a, b: (_nn(a, b), (a, b)), lambda r, g: (_nt(g, r[1]), _tn(r[0], g)))


@jax.custom_vjp
def _bnt(a, b):
    return _nt(a, b)


_bnt.defvjp(lambda a, b: (_nt(a, b), (a, b)), lambda r, g: (_nn(g, r[1]), _tn(g, r[0])))


@jax.custom_vjp
def _btn(a, b):
    return _tn(a, b)


_btn.defvjp(lambda a, b: (_tn(a, b), (a, b)), lambda r, g: (_nt(r[1], g), _nn(r[0], g)))


def _silu(x):
    return x * jax.nn.sigmoid(x)


def _matmul(name, a, b, mode, out_dtype, tm, tn, tk, *, epilogue=None, tiled=(), mrows=(), ncols=(),
            b_noff=0, b_koff=0, n_out=None, out_blocks=None, dest=None):
    if mode == "tn":
        K, M = a.shape
    else:
        M, K = a.shape
    N = n_out if n_out is not None else (b.shape[0] if mode == "nt" else b.shape[1])
    tm, tn, tk = min(tm, M), min(tn, N), min(tk, K)
    assert M % tm == 0 and N % tn == 0 and K % tk == 0, (name, M, N, K, tm, tn, tk)
    assert b_noff % tn == 0 and b_koff % tk == 0
    no, ko = b_noff // tn, b_koff // tk
    nk = K // tk
    if mode == "tn":
        a_spec = pl.BlockSpec((tk, tm), lambda i, j, k: (k, i))
    else:
        a_spec = pl.BlockSpec((tm, tk), lambda i, j, k: (i, k))
    if mode == "nt":
        b_spec = pl.BlockSpec((tn, tk), lambda i, j, k: (j + no, k + ko))
    else:
        b_spec = pl.BlockSpec((tk, tn), lambda i, j, k: (k + ko, j + no))
    specs = [a_spec, b_spec]
    specs += [pl.BlockSpec((tm, tn), lambda i, j, k: (i, j)) for _ in tiled]
    specs += [pl.BlockSpec((tm, r.shape[1]), lambda i, j, k: (i, 0)) for r in mrows]
    specs += [pl.BlockSpec((1, tn), lambda i, j, k: (0, j)) for _ in ncols]
    total, off, earlier = dest if dest is not None else (None, 0, None)
    if out_blocks is None:
        assert off % tm == 0
        mo = off // tm
        out_shape = jax.ShapeDtypeStruct((M if total is None else total, N), out_dtype)
        out_spec = pl.BlockSpec((tm, tn), lambda i, j, k: (i + mo, j))
    else:
        nper = N // out_blocks
        assert nper % tn == 0
        jb = nper // tn
        out_shape = jax.ShapeDtypeStruct((out_blocks if total is None else total, M, nper), out_dtype)
        out_spec = pl.BlockSpec((None, tm, tn), lambda i, j, k: (j // jb + off, i, j % jb))
    if earlier is not None:
        assert earlier.shape == out_shape.shape and earlier.dtype == out_shape.dtype
    ne = len(tiled) + len(mrows) + len(ncols)
    dot = {"nn": _nn, "nt": _nt, "tn": _tn}[mode]

    def body(a_ref, b_ref, *rest):
        extras, o_ref = rest[:ne], rest[ne]

        def finish(acc):
            if epilogue is not None:
                acc = epilogue(acc, *[e[...] for e in extras])
            o_ref[...] = acc.astype(o_ref.dtype)

        if nk == 1:
            finish(dot(a_ref[...], b_ref[...]))
        else:
            acc_ref = rest[ne + 1]
            k = pl.program_id(2)

            @pl.when(k == 0)
            def _():
                acc_ref[...] = jnp.zeros_like(acc_ref)

            acc_ref[...] += dot(a_ref[...], b_ref[...])

            @pl.when(k == nk - 1)
            def _():
                finish(acc_ref[...])

    args = [a, b, *tiled, *mrows, *ncols]
    aliases = {}
    if earlier is not None:
        specs.append(pl.BlockSpec(memory_space=pl.ANY))
        aliases = {len(args): 0}
        args.append(earlier)

    def body_with_dest(*refs):
        body(*refs[:2 + ne], *refs[2 + ne + (earlier is not None):])

    return pl.pallas_call(
        body_with_dest, name=name, out_shape=out_shape, grid=(M // tm, N // tn, nk),
        in_specs=specs, out_specs=out_spec, input_output_aliases=aliases,
        scratch_shapes=[] if nk == 1 else [pltpu.VMEM((tm, tn), F32)],
        compiler_params=_params(("parallel", "parallel", "arbitrary"), VMEM_BIG),
    )(*args)


def _rowwise(name, fn, tiled, consts, outs, accs, ts):
    tl = [(t, t.shape[1], 0) if not isinstance(t, tuple) else t for t in tiled]
    s_len = tl[0][0].shape[0]
    assert s_len % ts == 0
    nt_, nc_, no_ = len(tl), len(consts), len(outs)

    def body(*refs):
        t_refs, c_refs = refs[:nt_], refs[nt_:nt_ + nc_]
        o_refs, a_refs = refs[nt_ + nc_:nt_ + nc_ + no_], refs[nt_ + nc_ + no_:]
        res_o, res_a = fn(*[r[...] for r in t_refs], *[r[...] for r in c_refs])
        for r, v in zip(o_refs, res_o, strict=True):
            r[...] = v.astype(r.dtype)
        if a_refs:
            @pl.when(pl.program_id(0) == 0)
            def _():
                for r in a_refs:
                    r[...] = jnp.zeros_like(r)

            for r, v in zip(a_refs, res_a, strict=True):
                r[...] += v

    in_specs = [pl.BlockSpec((ts, w), functools.partial(lambda i, cb: (i, cb), cb=cb)) for (_, w, cb) in tl]
    in_specs += [pl.BlockSpec(c.shape, lambda i: (0, 0)) for c in consts]
    out_specs = [pl.BlockSpec((ts, c), lambda i: (i, 0)) for (c, _) in outs]
    out_specs += [pl.BlockSpec(shp, lambda i: (0, 0)) for shp in accs]
    out_shape = [jax.ShapeDtypeStruct((s_len, c), dt) for (c, dt) in outs]
    out_shape += [jax.ShapeDtypeStruct(shp, F32) for shp in accs]
    res = pl.pallas_call(
        body, name=name, out_shape=out_shape, grid=(s_len // ts,), in_specs=in_specs, out_specs=out_specs,
        compiler_params=_params(("arbitrary",) if accs else ("parallel",), VMEM_BIG),
    )(*[t[0] for t in tl], *consts)
    return res[:no_], res[no_:]


def _norm_mod_fn(x, nw, sc, sh):
    r = lax.rsqrt(jnp.mean(x * x, axis=-1, keepdims=True) + NORM_EPS)
    return (x * r * nw) * (1.0 + sc) + sh


def _norm_mod_fwd(name, x, nw, sc, sh):
    (hn,), _ = _rowwise(name, lambda x, nw, sc, sh: ([_norm_mod_fn(x, nw, sc, sh)], []),
                        [x], [nw, sc, sh], [(x.shape[1], BF16)], [], 512)
    return hn


def _norm_mod_bwd(name, x, dhn_parts, dres, nw, sc, sh):
    n = len(dhn_parts)
    d = x.shape[1]

    def fn(x, *rest):
        dhn = rest[0]
        for p in rest[1:n]:
            dhn = dhn + p
        dres, nw, sc, sh = rest[n:]
        _, vjp = jax.vjp(_norm_mod_fn, x, nw, sc, sh)
        dx, dnw, dsc, dsh = vjp(dhn)
        return [dx + dres], [dnw, dsc, dsh]

    (dx,), (g_nw, dsc, dsh) = _rowwise(name, fn, [x, *dhn_parts, dres], [nw, sc, sh], [(d, F32)],
                                       [(1, d), (1, d), (1, d)], 256)
    return dx, g_nw, dsc, dsh


def _rope_tables(pos_col, inv_row):
    def fn(pos, inv):
        ang = pos.astype(F32) * inv
        e = lax.broadcasted_iota(jnp.int32, (1, 128), 1) % HEAD_DIM
        cos, sin = jnp.cos(ang), jnp.sin(ang)
        half = ROT_DIM // 2
        return [jnp.where(e < ROT_DIM, cos, 1.0), jnp.where(e < half, -sin, 0.0),
                jnp.where((e >= half) & (e < ROT_DIM), sin, 0.0)], []

    (c, sa, sb), _ = _rowwise("rope_tables", fn, [pos_col], [inv_row], [(128, F32)] * 3, [], 512)
    return c, sa, sb


def _rot_fwd(t, c, sa, sb):
    n = t.shape[1]
    rep = n // 128
    c, sa, sb = (jnp.tile(u, (1, rep)) for u in (c, sa, sb))
    return t * c + pltpu.roll(t, n - ROT_DIM // 2, 1) * sa + pltpu.roll(t, ROT_DIM // 2, 1) * sb


def _rot_bwd(g, c, sa, sb):
    n = g.shape[1]
    rep = n // 128
    c, sa, sb = (jnp.tile(u, (1, rep)) for u in (c, sa, sb))
    return g * c + pltpu.roll(g * sa, ROT_DIM // 2, 1) + pltpu.roll(g * sb, n - ROT_DIM // 2, 1)


ATT_TQ = 128
ATT_TK = ATT_TQ + 2 * BAND


def _attn_specs(g, s_len):
    def blk(off):
        return pl.BlockSpec((s_len, 128), functools.partial(lambda hp, off: (0, off + hp), off=off))

    return blk(4 * g), blk(12 + 4 * g), blk(4 * g), blk(0)


def _attn_tile_geometry(t, d, l):
    nts = l // ATT_TQ
    r = t // nts
    q0 = (t % nts) * ATT_TQ
    ws = jnp.clip(q0 - BAND, 0, l - ATT_TK)
    qpos = q0 + lax.broadcasted_iota(jnp.int32, (ATT_TQ, 1), 0)
    kpos = ws + lax.broadcasted_iota(jnp.int32, (1, ATT_TK), 1)
    valid = jnp.abs(kpos - qpos) <= BAND
    if d == 1:
        return pl.ds(pl.multiple_of(q0, ATT_TQ), ATT_TQ), pl.ds(pl.multiple_of(ws, BAND), ATT_TK), valid
    return pl.ds(r + d * q0, ATT_TQ, stride=d), pl.ds(r + d * ws, ATT_TK, stride=d), valid


def _attn_fwd(g, qk, v):
    s_len = qk.shape[0]
    d = DILATIONS[g]
    l = s_len // d
    assert l % ATT_TQ == 0 and l >= ATT_TK
    q_spec, k_spec, v_spec, o_spec = _attn_specs(g, s_len)
    scale = 1.0 / math.sqrt(HEAD_DIM)

    def body(q_ref, k_ref, v_ref, o_ref, lse_ref):
        lane = lax.broadcasted_iota(jnp.int32, (1, 128), 1)
        in_h = [lane < HEAD_DIM, lane >= HEAD_DIM]

        def tile(t, carry):
            rows, win, valid = _attn_tile_geometry(t, d, l)
            q = q_ref[rows, :].astype(BF16)
            k = k_ref[win, :].astype(BF16)
            vv = v_ref[win, :].astype(BF16)
            outs, lses = [], []
            for h in range(2):
                qm = jnp.where(in_h[h], q, jnp.zeros_like(q))
                s = jnp.where(valid, _nt(qm, k) * scale, NEG_BIG)
                m = jnp.max(s, axis=1, keepdims=True)
                p = jnp.exp(s - m)
                den = jnp.sum(p, axis=1, keepdims=True)
                outs.append(_nn(p, vv) / den)
                lses.append(m + jnp.log(den))
            o_ref[rows, :] = jnp.where(in_h[0], outs[0], outs[1])
            lse_ref[rows, :] = jnp.where(in_h[0], lses[0], lses[1])
            return carry

        lax.fori_loop(0, s_len // ATT_TQ, tile, 0, unroll=4)

    return pl.pallas_call(
        body, name=f"attn_fwd_g{g}", grid=(4,),
        out_shape=[jax.ShapeDtypeStruct((s_len, 512), F32)] * 2,
        in_specs=[q_spec, k_spec, v_spec], out_specs=[o_spec, o_spec],
        compiler_params=_params(("parallel",), VMEM_BIG),
    )(qk, qk, v)


def _attn_bwd(g, qk, v, o, lse, do, dlse):
    s_len = qk.shape[0]
    d = DILATIONS[g]
    l = s_len // d
    q_spec, k_spec, v_spec, o_spec = _attn_specs(g, s_len)
    scale = 1.0 / math.sqrt(HEAD_DIM)

    def body(q_ref, k_ref, v_ref, o_ref, lse_ref, do_ref, dlse_ref, dq_ref, dk_ref, dv_ref):
        lane = lax.broadcasted_iota(jnp.int32, (1, 128), 1)
        in_h = [lane < HEAD_DIM, lane >= HEAD_DIM]
        dk_ref[...] = jnp.zeros_like(dk_ref)
        dv_ref[...] = jnp.zeros_like(dv_ref)

        def tile(t, carry):
            rows, win, valid = _attn_tile_geometry(t, d, l)
            q, k, vv = q_ref[rows, :].astype(BF16), k_ref[win, :].astype(BF16), v_ref[win, :].astype(BF16)
            dout, lse_t, dlse_t = do_ref[rows, :], lse_ref[rows, :], dlse_ref[rows, :]
            od = dout * o_ref[rows, :]
            dqs, dks, dvs = [], [], []
            for h in range(2):
                c0 = h * HEAD_DIM
                qm = jnp.where(in_h[h], q, jnp.zeros_like(q))
                s = jnp.where(valid, _nt(qm, k) * scale, NEG_BIG)
                p = jnp.exp(s - lse_t[:, c0:c0 + 1])
                dom = jnp.where(in_h[h], dout, 0.0)
                dp = _nt(dom, vv)
                delta = jnp.sum(jnp.where(in_h[h], od, 0.0), axis=1, keepdims=True)
                ds = (p * (dp - delta + dlse_t[:, c0:c0 + 1]) * scale).astype(BF16)
                dqs.append(_nn(ds, k))
                dks.append(_tn(ds, q))
                dvs.append(_tn(p, dout))
            dq_ref[rows, :] = jnp.where(in_h[0], dqs[0], dqs[1])
            dk_ref[win, :] += jnp.where(in_h[0], dks[0], dks[1])
            dv_ref[win, :] += jnp.where(in_h[0], dvs[0], dvs[1])
            return carry

        lax.fori_loop(0, s_len // ATT_TQ, tile, 0, unroll=4)

    return pl.pallas_call(
        body, name=f"attn_bwd_g{g}", grid=(4,),
        out_shape=[jax.ShapeDtypeStruct((s_len, 512), F32)] * 3,
        in_specs=[q_spec, k_spec, v_spec, o_spec, o_spec, o_spec, o_spec], out_specs=[o_spec] * 3,
        compiler_params=_params(("parallel",), VMEM_BIG),
    )(qk, qk, v, o, lse, do, dlse)


def _mix_weights(ls):
    mx = jnp.maximum(jnp.maximum(ls[0], ls[1]), ls[2])
    es = [jnp.exp(x - mx) for x in ls]
    tot = es[0] + es[1] + es[2]
    return [e / tot for e in es]


def _attn_out(os_, lses, z, x, gate, w_out):
    s_len, dm = x.shape
    tm = 256
    wdt = 512

    def body(o0, o1, o2, l0, l1, l2, z_ref, x_ref, g_ref, w_ref, a_ref, y_ref, x1_ref):
        alphas = _mix_weights([l0[...], l1[...], l2[...]])
        y = jnp.zeros((tm, dm), F32)
        for g, o_ref in enumerate((o0, o1, o2)):
            a_g = (o_ref[...] * alphas[g] * _silu(z_ref[:, g * wdt:(g + 1) * wdt])).astype(BF16)
            a_ref[:, g * wdt:(g + 1) * wdt] = a_g
            y = y + _nn(a_g, w_ref[g * wdt:(g + 1) * wdt, :])
        y_ref[...] = y
        x1_ref[...] = x_ref[...] + g_ref[...] * y

    row = lambda c: pl.BlockSpec((tm, c), lambda i: (i, 0))
    return pl.pallas_call(
        body, name="attn_out", grid=(s_len // tm,),
        out_shape=[jax.ShapeDtypeStruct((s_len, 3 * wdt), BF16), jax.ShapeDtypeStruct((s_len, dm), F32),
                   jax.ShapeDtypeStruct((s_len, dm), F32)],
        in_specs=[row(wdt)] * 6 + [row(3 * wdt), row(dm), pl.BlockSpec((1, dm), lambda i: (0, 0)),
                                   pl.BlockSpec(w_out.shape, lambda i: (0, 0))],
        out_specs=[row(3 * wdt), row(dm), row(dm)],
        compiler_params=_params(("parallel",), VMEM_BIG),
    )(*os_, *lses, z, x, gate, w_out)


def _mix_bwd(da, os_, lses, z):
    wdt = 512

    def fn(da, o0, o1, o2, l0, l1, l2, z):
        os_t, ls = [o0, o1, o2], [l0, l1, l2]
        alphas = _mix_weights(ls)
        hi = lax.broadcasted_iota(jnp.int32, (wdt, wdt), 0) // HEAD_DIM
        hj = lax.broadcasted_iota(jnp.int32, (wdt, wdt), 1) // HEAD_DIM
        seg = (hi == hj).astype(F32)
        dos, dal, dzs = [], [], []
        for g in range(3):
            zg = z[:, g * wdt:(g + 1) * wdt]
            sig = jax.nn.sigmoid(zg)
            dag = da[:, g * wdt:(g + 1) * wdt]
            dmix = dag * zg * sig
            dzs.append(dag * os_t[g] * alphas[g] * (sig * (1.0 + zg * (1.0 - sig))))
            dos.append(dmix * alphas[g])
            dal.append(_hnn(dmix * os_t[g], seg))
        mean = alphas[0] * dal[0] + alphas[1] * dal[1] + alphas[2] * dal[2]
        dls = [alphas[g] * (dal[g] - mean) for g in range(3)]
        return dos + dls + [jnp.concatenate(dzs, axis=1)], []

    outs, _ = _rowwise("mix_bwd", fn, [da, *os_, *lses, z], [], [(wdt, F32)] * 6 + [(3 * wdt, BF16)], [], 256)
    return outs[:3], outs[3:6], outs[6]


def _rot_pack_bwd(dqs, dks, dvs, tabs):
    wdt = 512

    def fn(*args):
        grads, (c, sa, sb) = args[:9], args[9:]
        cols = [_rot_bwd(gq, c, sa, sb) for gq in grads[:6]] + list(grads[6:])
        return [jnp.concatenate(cols, axis=1)], []

    (out,), _ = _rowwise("rot_pack_bwd", fn, [*dqs, *dks, *dvs, *tabs], [], [(9 * wdt, BF16)], [], 256)
    return out


CONV_CB = 128
CONV_R = 256
CONV_PAD = 8


def _conv_taps(buf, base, off, sign):
    return [buf[pl.ds(base + off + sign * j, CONV_R), :] for j in range(CONV_WIDTH)]


def _conv_tap_sum(taps, w):
    acc = None
    for j, t in enumerate(taps):
        term = t * w[j:j + 1, :]
        acc = term if acc is None else acc + term
    return acc


def _conv_fwd(xpre, cw, cb):
    s_len, ch = xpre.shape
    nchunk = s_len // CONV_R

    def body(x_ref, w_ref, b_ref, o_ref, xp):
        zero = jnp.zeros((CONV_PAD, CONV_CB), F32)
        xp[0:CONV_PAD, :] = zero
        xp[s_len + CONV_PAD:s_len + 2 * CONV_PAD, :] = zero

        def fill(ci, carry):
            base = pl.multiple_of(ci * CONV_R, CONV_R)
            xp[pl.ds(base + CONV_PAD, CONV_R), :] = x_ref[pl.ds(base, CONV_R), :]
            return carry

        lax.fori_loop(0, nchunk, fill, 0)
        w = w_ref[...]
        b = b_ref[...]

        def chunk(ci, carry):
            base = pl.multiple_of(ci * CONV_R, CONV_R)
            u = _conv_tap_sum(_conv_taps(xp, base, CONV_PAD - CONV_WIDTH // 2, 1), w) + b
            o_ref[pl.ds(base, CONV_R), :] = _silu(u)
            return carry

        lax.fori_loop(0, nchunk, chunk, 0)

    col = lambda r: pl.BlockSpec((r, CONV_CB), lambda j: (0, j))
    return pl.pallas_call(
        body, name="conv_fwd", grid=(ch // CONV_CB,), out_shape=jax.ShapeDtypeStruct((s_len, ch), F32),
        in_specs=[col(s_len), col(CONV_WIDTH), col(1)], out_specs=col(s_len),
        scratch_shapes=[pltpu.VMEM((s_len + 2 * CONV_PAD, CONV_CB), F32)],
        compiler_params=_params(("parallel",), VMEM_BIG),
    )(xpre, cw, cb)


def _conv_bwd(xpre, da, db, cw, cb):
    s_len, ch = xpre.shape
    nchunk = s_len // CONV_R
    half = CONV_WIDTH // 2

    def body(x_ref, da_ref, db_ref, w_ref, b_ref, dx_ref, gw_ref, gb_ref, xp, dcp):
        zero = jnp.zeros((CONV_PAD, CONV_CB), F32)
        for buf in (xp, dcp):
            buf[0:CONV_PAD, :] = zero
            buf[s_len + CONV_PAD:s_len + 2 * CONV_PAD, :] = zero

        def fill(ci, carry):
            base = pl.multiple_of(ci * CONV_R, CONV_R)
            xp[pl.ds(base + CONV_PAD, CONV_R), :] = x_ref[pl.ds(base, CONV_R), :]
            return carry

        lax.fori_loop(0, nchunk, fill, 0)
        w = w_ref[...]
        b = b_ref[...]

        def first(ci, carry):
            base = pl.multiple_of(ci * CONV_R, CONV_R)
            taps = _conv_taps(xp, base, CONV_PAD - half, 1)
            u = _conv_tap_sum(taps, w) + b
            sig = jax.nn.sigmoid(u)
            dc = (da_ref[pl.ds(base, CONV_R), :] + db_ref[pl.ds(base, CONV_R), :]) * (sig * (1.0 + u * (1.0 - sig)))
            dcp[pl.ds(base + CONV_PAD, CONV_R), :] = dc
            gb = carry[0] + jnp.sum(dc, axis=0, keepdims=True)
            gws = [carry[1 + j] + jnp.sum(dc * taps[j], axis=0, keepdims=True) for j in range(CONV_WIDTH)]
            return (gb, *gws)

        z1 = jnp.zeros((1, CONV_CB), F32)
        sums = lax.fori_loop(0, nchunk, first, (z1,) * (1 + CONV_WIDTH))
        gb_ref[...] = sums[0]
        for j in range(CONV_WIDTH):
            gw_ref[j:j + 1, :] = sums[1 + j]

        def second(ci, carry):
            base = pl.multiple_of(ci * CONV_R, CONV_R)
            dx_ref[pl.ds(base, CONV_R), :] = _conv_tap_sum(_conv_taps(dcp, base, CONV_PAD + half, -1), w).astype(dx_ref.dtype)
            return carry

        lax.fori_loop(0, nchunk, second, 0)

    col = lambda r: pl.BlockSpec((r, CONV_CB), lambda j: (0, j))
    return pl.pallas_call(
        body, name="conv_bwd", grid=(ch // CONV_CB,),
        out_shape=[jax.ShapeDtypeStruct((s_len, ch), BF16), jax.ShapeDtypeStruct((CONV_WIDTH, ch), F32),
                   jax.ShapeDtypeStruct((1, ch), F32)],
        in_specs=[col(s_len), col(s_len), col(s_len), col(CONV_WIDTH), col(1)],
        out_specs=[col(s_len), col(CONV_WIDTH), col(1)],
        scratch_shapes=[pltpu.VMEM((s_len + 2 * CONV_PAD, CONV_CB), F32)] * 2,
        compiler_params=_params(("parallel",), VMEM_BIG),
    )(xpre, da, db, cw, cb)


SSD_GW = 256
SSD_N = 128


def _bf16_parts(x, n):
    parts, rest = [], x
    for _ in range(n):
        p = rest.astype(BF16)
        parts.append(p)
        rest = rest - p.astype(F32)
    return parts


@jax.custom_vjp
def _expand(x, e):
    eb = e.astype(BF16)
    return sum(_dg(p, eb, 1, 0) for p in _bf16_parts(x, 2))


def _expand_fwd(x, e):
    return _expand(x, e), e


def _expand_bwd(e, g):
    eb = e.astype(BF16)
    return sum(_dg(p, eb, 1, 1) for p in _bf16_parts(g, 2)), jnp.zeros_like(e)


_expand.defvjp(_expand_fwd, _expand_bwd)


@jax.custom_vjp
def _running_sum(tri, x):
    tb = tri.astype(BF16)
    return sum(_dg(tb, p, 1, 0) for p in _bf16_parts(x, 3))


def _running_sum_fwd(tri, x):
    return _running_sum(tri, x), tri


def _running_sum_bwd(tri, g):
    tb = tri.astype(BF16)
    return jnp.zeros_like(tri), sum(_dg(tb, p, 0, 0) for p in _bf16_parts(g, 3))


_running_sum.defvjp(_running_sum_fwd, _running_sum_bwd)


def _ssd_mask(dirn):
    ri = lax.broadcasted_iota(jnp.int32, (CHUNK, CHUNK), 0)
    cj = lax.broadcasted_iota(jnp.int32, (CHUNK, CHUNK), 1)
    return (cj <= ri) if dirn == 0 else (cj >= ri)


def _ssd_rowsel(dirn):
    last = CHUNK - 1 if dirn == 0 else 0
    return (lax.broadcasted_iota(jnp.int32, (CHUNK, 1), 0) == last).astype(F32)


def _ssd_chunk_pre(dirn):
    nh = 2 * SSD_HEADS

    def f(dt, alog):
        da = dt * (-jnp.exp(alog))
        cum = _running_sum(_ssd_mask(dirn).astype(F32), da)
        tot = jnp.sum(cum * _ssd_rowsel(dirn), axis=0, keepdims=True)
        hh = lax.broadcasted_iota(jnp.int32, (nh, SSD_HEADS * HEAD_DIM), 0)
        jj = lax.broadcasted_iota(jnp.int32, (nh, SSD_HEADS * HEAD_DIM), 1)
        expand = (hh == dirn * SSD_HEADS + jj // HEAD_DIM).astype(F32)
        return cum, cum.T, _expand(dt, expand), _expand(jnp.exp(tot - cum), expand), _expand(jnp.exp(cum), expand)

    return f


def _ssd_group_fn(g, dirn, stacked):
    nh = 2 * SSD_HEADS

    def f(xs, bm, cm, st, cum, cum_t, dt_e, w_e, ce_e):
        mask = _ssd_mask(dirn)
        xdt = xs * dt_e
        cd_e = jnp.sum(ce_e * _ssd_rowsel(dirn), axis=0, keepdims=True)
        cb = _bnt(cm, bm)
        lane_head = lax.broadcasted_iota(jnp.int32, (1, SSD_GW), 1) // HEAD_DIM
        y = _bnn(cm, st) * ce_e
        decayed, inputs = [], []
        for j in range(4):
            hidx = dirn * SSD_HEADS + 4 * g + j
            col = jnp.sum(cum * (lax.broadcasted_iota(jnp.int32, (1, nh), 1) == hidx).astype(F32), axis=1, keepdims=True)
            row = jnp.sum(cum_t * (lax.broadcasted_iota(jnp.int32, (nh, 1), 0) == hidx).astype(F32), axis=0, keepdims=True)
            dec = cb * jnp.exp(jnp.where(mask, col - row, NEG_BIG))
            head = (lane_head == j).astype(F32)
            if stacked:
                decayed.append(dec)
                inputs.append(xdt * head)
            else:
                y = y + _bnn(dec, xdt) * head
        if stacked:
            y = y + _bnn(jnp.concatenate(decayed, axis=1), jnp.concatenate(inputs, axis=0))
        st_out = st * cd_e + _btn(bm, xdt * w_e)
        return y, st_out

    return f


def _ssd_in_specs(kk):
    ln = CHUNK
    return [pl.BlockSpec((ln, 2048), lambda i: (kk(i), 0)),
            pl.BlockSpec((ln, 1024), lambda i: (kk(i), 2)),
            pl.BlockSpec((ln, 1024), lambda i: (kk(i), 3)),
            pl.BlockSpec((ln, 2 * SSD_HEADS), lambda i: (kk(i), 0)),
            pl.BlockSpec((1, 2 * SSD_HEADS), lambda i: (0, 0))]


def _ssd_fwd(xbc, dt, alog, dirn):
    s_len = xbc.shape[0]
    nc = s_len // CHUNK
    kk = (lambda i: i) if dirn == 0 else (lambda i: nc - 1 - i)

    def body(x_ref, b_ref, c_ref, dt_ref, al_ref, y_ref, sts_ref, st):
        @pl.when(pl.program_id(0) == 0)
        def _():
            st[...] = jnp.zeros_like(st)

        sts_ref[0] = st[...]
        cum, cum_t, dt_e, w_e, ce_e = _ssd_chunk_pre(dirn)(dt_ref[...], al_ref[...])
        for g in range(SSD_GROUPS):
            xc = slice(g * SSD_GW, (g + 1) * SSD_GW)
            gc = slice(g * SSD_N, (g + 1) * SSD_N)
            y, st_new = _ssd_group_fn(g, dirn, True)(x_ref[:, xc], b_ref[:, gc], c_ref[:, gc], st[:, xc], cum, cum_t,
                                               dt_e[:, xc], w_e[:, xc], ce_e[:, xc])
            y_ref[:, xc] = y
            st[:, xc] = st_new

    return pl.pallas_call(
        body, name=f"ssd_fwd_d{dirn}", grid=(nc,),
        out_shape=[jax.ShapeDtypeStruct((s_len, 2048), F32), jax.ShapeDtypeStruct((nc, SSD_N, 2048), F32)],
        in_specs=_ssd_in_specs(kk),
        out_specs=[pl.BlockSpec((CHUNK, 2048), lambda i: (kk(i), 0)),
                   pl.BlockSpec((1, SSD_N, 2048), lambda i: (kk(i), 0, 0))],
        scratch_shapes=[pltpu.VMEM((SSD_N, 2048), F32)],
        compiler_params=_params(("arbitrary",), VMEM_BIG),
    )(xbc, xbc, xbc, dt, alog)


def _ssd_bwd(xbc, dt, alog, states, dy, d_e, dirn):
    s_len = xbc.shape[0]
    nc = s_len // CHUNK
    kk = (lambda i: nc - 1 - i) if dirn == 0 else (lambda i: i)

    def body(x_ref, b_ref, c_ref, dt_ref, al_ref, sts_ref, dy_ref, de_ref, dx_ref, ddt_ref, dal_ref, dst):
        @pl.when(pl.program_id(0) == 0)
        def _():
            dst[...] = jnp.zeros_like(dst)
            dal_ref[...] = jnp.zeros_like(dal_ref)

        (cum, cum_t, dt_e, w_e, ce_e), pre_vjp = jax.vjp(_ssd_chunk_pre(dirn), dt_ref[...], al_ref[...])
        dcum = jnp.zeros_like(cum)
        dcum_t = jnp.zeros_like(cum_t)
        d_dt_e, d_w_e, d_ce_e = [], [], []
        for g in range(SSD_GROUPS):
            xc = slice(g * SSD_GW, (g + 1) * SSD_GW)
            gc = slice(g * SSD_N, (g + 1) * SSD_N)
            _, vjp = jax.vjp(_ssd_group_fn(g, dirn, False), x_ref[:, xc], b_ref[:, gc], c_ref[:, gc], sts_ref[0, :, xc], cum, cum_t,
                             dt_e[:, xc], w_e[:, xc], ce_e[:, xc])
            dyg = dy_ref[:, xc]
            dxs, dbm, dcm, dst_g, dcum_g, dcum_t_g, ddte_g, dwe_g, dcee_g = vjp((dyg, dst[:, xc]))
            if dirn == 0:
                dxs = dxs + dyg * de_ref[:, xc]
            dx_ref[:, xc] = dxs
            dx_ref[:, 2048 + g * SSD_N:2048 + (g + 1) * SSD_N] = dbm
            dx_ref[:, 3072 + g * SSD_N:3072 + (g + 1) * SSD_N] = dcm
            dst[:, xc] = dst_g
            dcum = dcum + dcum_g
            dcum_t = dcum_t + dcum_t_g
            d_dt_e.append(ddte_g)
            d_w_e.append(dwe_g)
            d_ce_e.append(dcee_g)
        ddt, dal = pre_vjp((dcum, dcum_t, jnp.concatenate(d_dt_e, axis=1), jnp.concatenate(d_w_e, axis=1),
                            jnp.concatenate(d_ce_e, axis=1)))
        ddt_ref[...] = ddt
        dal_ref[...] += dal

    return pl.pallas_call(
        body, name=f"ssd_bwd_d{dirn}", grid=(nc,),
        out_shape=[jax.ShapeDtypeStruct((s_len, 4096), F32), jax.ShapeDtypeStruct((s_len, 2 * SSD_HEADS), F32),
                   jax.ShapeDtypeStruct((1, 2 * SSD_HEADS), F32)],
        in_specs=_ssd_in_specs(kk) + [pl.BlockSpec((1, SSD_N, 2048), lambda i: (kk(i), 0, 0)),
                                      pl.BlockSpec((CHUNK, 2048), lambda i: (kk(i), 0)),
                                      pl.BlockSpec((1, 2048), lambda i: (0, 0))],
        out_specs=[pl.BlockSpec((CHUNK, 4096), lambda i: (kk(i), 0)),
                   pl.BlockSpec((CHUNK, 2 * SSD_HEADS), lambda i: (kk(i), 0)),
                   pl.BlockSpec((1, 2 * SSD_HEADS), lambda i: (0, 0))],
        scratch_shapes=[pltpu.VMEM((SSD_N, 2048), F32)],
        compiler_params=_params(("arbitrary",), VMEM_BIG),
    )(xbc, xbc, xbc, dt, alog, states, dy, d_e)


def _gate_norm_fn(yf, yb, xs, z, d_e, nw):
    yg = (yf + yb + xs * d_e) * _silu(z)
    return yg * lax.rsqrt(jnp.mean(yg * yg, axis=-1, keepdims=True) + NORM_EPS) * nw


def _gate_norm_fwd(yf, yb, xbc, z, d_e, nw):
    (u,), _ = _rowwise("ssd_gate_norm", lambda *a: ([_gate_norm_fn(*a)], []),
                       [yf, yb, (xbc, 2048, 0), z], [d_e, nw], [(2048, BF16)], [], 256)
    return u


def _gate_norm_bwd(du, yf, yb, xbc, z, d_e, nw):
    def fn(du, yf, yb, xs, z, d_e, nw):
        _, vjp = jax.vjp(_gate_norm_fn, yf, yb, xs, z, d_e, nw)
        dyf, _, _, dz, dde, dnw = vjp(du)
        hh = lax.broadcasted_iota(jnp.int32, (2048, SSD_HEADS), 0) // HEAD_DIM
        jj = lax.broadcasted_iota(jnp.int32, (2048, SSD_HEADS), 1)
        return [dyf, dz], [dnw, _hnn(jnp.broadcast_to(dde, (8, 2048)), (hh == jj).astype(F32))[0:1]]

    (dys, dz), (g_nw, g_d) = _rowwise("ssd_gate_norm_bwd", fn, [du, yf, yb, (xbc, 2048, 0), z], [d_e, nw],
                                      [(2048, F32), (2048, BF16)], [(1, 2048), (1, SSD_HEADS)], 256)
    return dys, dz, g_nw, g_d


def _loss_bwd(x1, y1, tgt, gate, fnw):
    dm = x1.shape[1]

    def fn(x1, y1, tgt, gate, fnw):
        def head(x2, fnw):
            yf = (x2 * lax.rsqrt(jnp.mean(x2 * x2, axis=-1, keepdims=True) + NORM_EPS)) * fnw
            err = yf - tgt
            return 0.5 * jnp.sum(jnp.mean(err * err, axis=-1, keepdims=True), axis=0, keepdims=True)

        x2 = x1 + gate * y1
        loss, vjp = jax.vjp(head, x2, fnw)
        dx2, dfnw = vjp(jnp.ones((1, 1), F32))
        return [dx2, gate * dx2], [dfnw, jnp.sum(dx2 * y1, axis=0, keepdims=True), jnp.broadcast_to(loss, (1, 128))]

    (dx2, dy1), (g_fnw, dgate, loss) = _rowwise("loss_bwd", fn, [x1, y1, tgt], [gate, fnw], [(dm, F32), (dm, BF16)],
                                                [(1, dm), (1, dm), (1, 128)], 256)
    return dx2, dy1, g_fnw, dgate, loss


def _gate_bwd(dx, y, gate):
    dm = dx.shape[1]
    (dy,), (dgate,) = _rowwise("gate_bwd", lambda dx, y, gate: ([gate * dx], [jnp.sum(dx * y, axis=0, keepdims=True)]),
                               [dx, y], [gate], [(dm, BF16)], [(1, dm)], 512)
    return dy, dgate


def _softplus_fwd(dt_raw, bias):
    (dt,), _ = _rowwise("dt_softplus", lambda r, b: ([jax.nn.softplus(r + b)], []), [dt_raw], [bias],
                        [(dt_raw.shape[1], F32)], [], 512)
    return dt


def _softplus_bwd(ddt_f, ddt_b, dt_raw, bias):
    def fn(df, db, r, b):
        g = (df + db) * jax.nn.sigmoid(r + b)
        return [g], [jnp.sum(g, axis=0, keepdims=True)]

    w = dt_raw.shape[1]
    (g,), (gb,) = _rowwise("dt_softplus_bwd", fn, [ddt_f, ddt_b, dt_raw], [bias], [(w, BF16)], [(1, w)], 512)
    return g, gb


def _whole(a):
    nd = len(a.shape)
    return pl.BlockSpec(a.shape, lambda *_: (0,) * nd)


def _mod_part(c_all, mod_w):
    nl, _, ncol = mod_w.shape
    nb = c_all.shape[0]

    def body(c_ref, w_ref, o_ref):
        cond = _silu(c_ref[...])
        for i in range(nl):
            o_ref[i * nb:(i + 1) * nb, :] = _nn(cond, w_ref[i])

    return pl.pallas_call(body, name="mod_part", out_shape=jax.ShapeDtypeStruct((nl * nb, ncol), F32),
                          compiler_params=_params(None, VMEM_BIG))(c_all, mod_w)


def _mod_finish(mod_nb, mod_b):
    def body(a_ref, b_ref, o_ref):
        o_ref[...] = a_ref[...] + b_ref[...]

    return pl.pallas_call(body, name="mod_finish", out_shape=jax.ShapeDtypeStruct(mod_b.shape, F32))(mod_nb, mod_b)


def _mod_grad(c_all, dmod_sh):
    nl, nb, ncol = dmod_sh.shape
    dm = c_all.shape[1]

    def body(c_ref, d_ref, o_ref):
        cond = _silu(c_ref[...])
        for i in range(nl):
            o_ref[i] = _tn(cond, d_ref[i])

    return pl.pallas_call(body, name="mod_grad", out_shape=jax.ShapeDtypeStruct((nl, dm, ncol), F32),
                          compiler_params=_params(None, VMEM_BIG))(c_all, dmod_sh)


PACK_ROWS = 16
PACK_COLS = 1024


def _pack_small(rows6, nw2, fnw, b64, a64, d32, extra=None):
    args = [rows6, nw2, fnw, b64, a64, d32] + ([extra] if extra is not None else [])

    def body(*refs):
        o_ref = refs[-1]
        o_ref[...] = jnp.zeros_like(o_ref)
        o_ref[0:6, :] = refs[0][...]
        o_ref[6:8, :] = refs[1][...]
        o_ref[8:9, :] = refs[2][...]
        o_ref[9:10, 0:64] = refs[3][...]
        o_ref[9:10, 64:128] = refs[4][...]
        o_ref[9:10, 128:160] = refs[5][...]
        if extra is not None:
            o_ref[9:10, 256:384] = refs[6][...]

    return pl.pallas_call(body, name="pack_small", out_shape=jax.ShapeDtypeStruct((PACK_ROWS, PACK_COLS), F32))(*args)


def _unpack_small(p):
    return (p[0:6].reshape(2, 3 * PACK_COLS), p[6:8], p[8], p[9, 0:64].reshape(1, 2, 32), p[9, 64:128].reshape(1, 2, 32),
            p[9, 128:160].reshape(1, 32))


def _pack_ssd_small(cw, cb, nw):
    def body(cw_ref, cb_ref, nw_ref, o_ref):
        o_ref[...] = jnp.zeros_like(o_ref)
        o_ref[0:5, :] = cw_ref[...]
        o_ref[5:6, :] = cb_ref[...]
        o_ref[6:7, 0:256] = nw_ref[...]

    return pl.pallas_call(body, name="pack_ssd_small", out_shape=jax.ShapeDtypeStruct((8, 512), F32))(cw, cb, nw)


def _adamw(name, w, parts, m, v, tr, tc=None):
    r_, c_ = w.shape
    p_ = parts.shape[0]
    tr = min(tr, r_)
    tc = c_ if tc is None else tc
    assert r_ % tr == 0 and c_ % tc == 0

    def body(w_ref, p_ref, m_ref, v_ref, g_ref, d_ref, m2_ref, v2_ref):
        g = p_ref[0].astype(F32)
        for s in range(1, p_):
            g = g + p_ref[s].astype(F32)
        m2 = ADAM_B1 * m_ref[...] + (1.0 - ADAM_B1) * g
        v2 = ADAM_B2 * v_ref[...] + (1.0 - ADAM_B2) * (g * g)
        m_hat = m2 / (1.0 - ADAM_B1 ** ADAM_STEP)
        v_hat = v2 / (1.0 - ADAM_B2 ** ADAM_STEP)
        g_ref[...] = g
        d_ref[...] = -ADAM_LR * (m_hat / (jnp.sqrt(v_hat) + ADAM_EPS) + ADAM_WD * w_ref[...])
        m2_ref[...] = m2
        v2_ref[...] = v2

    blk = pl.BlockSpec((tr, tc), lambda i, j: (i, j))
    return pl.pallas_call(
        body, name=name, grid=(r_ // tr, c_ // tc), out_shape=[jax.ShapeDtypeStruct((r_, c_), F32)] * 4,
        in_specs=[blk, pl.BlockSpec((p_, tr, tc), lambda i, j: (0, i, j)), blk, blk], out_specs=[blk] * 4,
        compiler_params=_params(("parallel", "parallel"), VMEM_BIG),
    )(w, parts, m, v)


def _dev_index(p):
    return 4 * p[0] + 2 * p[1] + p[2]


def _all_gather(name, xs):
    n = len(xs)
    hbm = pl.BlockSpec(memory_space=pl.ANY)

    def body(*refs):
        x_refs, o_refs = refs[:n], refs[n:2 * n]
        send_sems, recv_sems, local_sems = refs[2 * n:]
        x, y, c = lax.axis_index("x"), lax.axis_index("y"), lax.axis_index("c")
        me, sibling = (x, y, c), (x, y, 1 - c)
        chips = [(1 - x, y), (x, 1 - y), (1 - x, 1 - y)]

        def copy(a, k, block, to, src=None):
            dst = o_refs[a].at[_dev_index(block)]
            return pltpu.make_async_remote_copy(
                src_ref=dst if src is None else src, dst_ref=dst, send_sem=send_sems.at[a, k],
                recv_sem=recv_sems.at[a, k], device_id=to, device_id_type=MESH)

        mine = [pltpu.make_async_copy(x_refs[a], o_refs[a].at[_dev_index(me)], local_sems.at[a]) for a in range(n)]
        for cp in mine:
            cp.start()
        first = []
        for a in range(n):
            first.append(copy(a, 0, me, sibling, src=x_refs[a]))
            first += [copy(a, 1 + j, me, (*chip, c), src=x_refs[a]) for j, chip in enumerate(chips)]
        for cp in first:
            cp.start()
        passed = []
        for j, chip in enumerate(chips):
            for a in range(n):
                copy(a, 1 + j, (*chip, c), me).wait_recv()
                cp = copy(a, 4 + j, (*chip, c), sibling)
                cp.start()
                passed.append(cp)
        for a in range(n):
            copy(a, 0, sibling, me).wait_recv()
            for j, chip in enumerate(chips):
                copy(a, 4 + j, (*chip, 1 - c), me).wait_recv()
        for cp in first + passed:
            cp.wait_send()
        for cp in mine:
            cp.wait()

    return pl.pallas_call(
        body, name=name, out_shape=[jax.ShapeDtypeStruct((NDEV, *x.shape), x.dtype) for x in xs],
        in_specs=[hbm] * n, out_specs=[hbm] * n,
        scratch_shapes=[pltpu.SemaphoreType.DMA((n, 7)), pltpu.SemaphoreType.DMA((n, 7)), pltpu.SemaphoreType.DMA((n,))],
    )(*xs)


_HBM = pl.BlockSpec(memory_space=pltpu.HBM)
_SEM = pl.BlockSpec(memory_space=pltpu.SEMAPHORE)
_EFFECT = pltpu.SideEffectType.DATAFLOW_SIDE_EFFECTING


def _mesh_position():
    return lax.axis_index("x"), lax.axis_index("y"), lax.axis_index("c")


def _peers(me):
    return [(k, tuple(1 - v if (k >> b) & 1 else v for v, b in zip(me, (2, 1, 0)))) for k in range(1, NDEV)]


def _landing_zones(name, xs, scatter):
    me = _dev_index(_mesh_position()).astype(jnp.int32).reshape(1)
    lands = []
    for a, x in enumerate(xs):
        rows, cols = x.shape[-2:]
        tr = 256 if rows % 256 == 0 else rows

        def body(me_ref, x_ref, o_ref):
            o_ref[...] = x_ref[...]

        if scatter:
            in_spec = pl.BlockSpec((None, tr, cols), lambda i, me_ref: (me_ref[0], i, 0))
        else:
            in_spec = pl.BlockSpec((tr, cols), lambda i, me_ref: (i, 0))
        lands.append(pl.pallas_call(
            body, name=f"{name}_{a}", out_shape=jax.ShapeDtypeStruct((NDEV, rows, cols), x.dtype),
            grid_spec=pltpu.PrefetchScalarGridSpec(
                num_scalar_prefetch=1, grid=(rows // tr,), in_specs=[in_spec],
                out_specs=pl.BlockSpec((None, tr, cols), lambda i, me_ref: (me_ref[0], i, 0))),
            compiler_params=_params(("arbitrary",)),
        )(me, x))
    return lands


def _exchange_copies(x_refs, land_refs, send_sems, recv_sems, scatter):
    me = _mesh_position()
    out = []
    for k, peer in _peers(me):
        for a, (x_ref, land_ref) in enumerate(zip(x_refs, land_refs)):
            sem = a * (NDEV - 1) + k - 1
            out.append(pltpu.make_async_remote_copy(
                src_ref=x_ref.at[_dev_index(peer)] if scatter else x_ref, dst_ref=land_ref.at[_dev_index(me)],
                send_sem=send_sems.at[sem], recv_sem=recv_sems.at[sem], device_id=peer, device_id_type=MESH))
    return out


def _exchange_start(name, xs, lands, scatter, dep):
    n = len(xs)

    def body(*refs):
        x_refs, land_refs = refs[:n], refs[n:2 * n]
        send_sems, recv_sems = refs[2 * n + 1], refs[2 * n + 2]
        token = refs[-1]
        for cp in _exchange_copies(x_refs, land_refs, send_sems, recv_sems, scatter):
            cp.start()
        token[...] = jnp.zeros_like(token)

    sems = pltpu.SemaphoreType.DMA((n * (NDEV - 1),))
    res = pl.pallas_call(
        body, name=name,
        out_shape=(sems, sems, *[pltpu.HBM(a.shape, a.dtype) for a in (*xs, *lands)], jax.ShapeDtypeStruct((8, 128), F32)),
        in_specs=[_HBM] * (2 * n) + [pl.BlockSpec(memory_space=pl.ANY)],
        out_specs=(_SEM, _SEM, *[_HBM] * (2 * n), pl.BlockSpec(memory_space=pltpu.VMEM)),
        input_output_aliases={i: 2 + i for i in range(2 * n)},
        compiler_params=pltpu.CompilerParams(has_side_effects=_EFFECT),
    )(*[pltpu.with_memory_space_constraint(a, pltpu.HBM) for a in (*xs, *lands)], dep)
    return res[:-1], res[-1]


def _exchange_wait(name, handles, scatter, after):
    send_sems, recv_sems = handles[0], handles[1]
    bufs = handles[2:]
    n = len(bufs) // 2

    def body(*refs):
        x_refs, land_refs = refs[:n], refs[n:2 * n]
        s_sems, r_sems = refs[2 * n], refs[2 * n + 1]
        for cp in _exchange_copies(x_refs, land_refs, s_sems, r_sems, scatter):
            cp.wait_send()
            cp.wait_recv()

    res = pl.pallas_call(
        body, name=name, out_shape=tuple(pltpu.HBM(a.shape, a.dtype) for a in bufs),
        in_specs=[_HBM] * (2 * n) + [_SEM, _SEM, pl.BlockSpec(memory_space=pl.ANY)], out_specs=tuple([_HBM] * (2 * n)),
        input_output_aliases={i: i for i in range(2 * n)},
        compiler_params=pltpu.CompilerParams(has_side_effects=_EFFECT),
    )(*bufs, send_sems, recv_sems, after)
    return res[n:]


def kernel(x, c, positions, norm_w, mod_w, mod_b, attn_w_in, attn_w_out, ssd_w_in, ssd_conv_w, ssd_conv_b, ssd_dt_bias, ssd_a_log, ssd_d, ssd_norm_w, ssd_w_out, final_norm_w, loss_target, m_norm_w, m_mod_w, m_mod_b, m_attn_w_in, m_attn_w_out, m_ssd_w_in, m_ssd_conv_w, m_ssd_conv_b, m_ssd_dt_bias, m_ssd_a_log, m_ssd_d, m_ssd_norm_w, m_ssd_w_out, m_final_norm_w, v_norm_w, v_mod_w, v_mod_b, v_attn_w_in, v_attn_w_out, v_ssd_w_in, v_ssd_conv_w, v_ssd_conv_b, v_ssd_dt_bias, v_ssd_a_log, v_ssd_d, v_ssd_norm_w, v_ssd_w_out, v_final_norm_w):
    s_len, dm = x.shape[1], x.shape[2]
    me = 4 * lax.axis_index("x") + 2 * lax.axis_index("y") + lax.axis_index("c")
    x0 = x.reshape(s_len, dm)
    tgt = loss_target.reshape(s_len, dm)
    aw = 3 * 512
    si = 2 * dm
    sxbc = 2 * si
    n_ssd_in = ssd_w_in.shape[2] * NDEV

    g_ai, c_all = _all_gather("gather_attn_w_in", [attn_w_in[0].astype(BF16), c])
    w_ai = g_ai.transpose(1, 0, 2).reshape(dm, 4 * aw)
    c_all = c_all.reshape(NDEV, dm)
    ssd_small = _pack_ssd_small(ssd_conv_w[0], ssd_conv_b, ssd_norm_w)
    late_shards = [attn_w_out[0].astype(BF16), ssd_w_in[0].T.astype(BF16), ssd_w_out[0].astype(BF16), ssd_small]
    w_handles, w_token = _exchange_start("weights_start", late_shards, _landing_zones("weights_place", late_shards, False),
                                         False, g_ai)

    part = _mod_part(c_all, mod_w)
    (part_all,) = _all_gather("gather_mod", [part])
    mod_nb = jnp.stack([lax.dynamic_index_in_dim(part_all, i * NDEV + me, axis=1, keepdims=False).reshape(3 * dm)
                        for i in range(2)])
    mod = _mod_finish(mod_nb, mod_b)
    shift = [mod[i:i + 1, 0:dm] for i in range(2)]
    scale = [mod[i:i + 1, dm:2 * dm] for i in range(2)]
    gate = [mod[i:i + 1, 2 * dm:3 * dm] for i in range(2)]
    nw = [norm_w[i:i + 1] + w_token[0:1, 0:1] for i in range(2)]

    hn0 = _norm_mod_fwd("norm0", x0, nw[0], scale[0], shift[0])
    inv_freq = ROPE_THETA ** (-jnp.arange(0, ROT_DIM, 2, dtype=F32) / ROT_DIM)
    lane = jnp.arange(128) % HEAD_DIM
    inv_row = jnp.where(lane < ROT_DIM, inv_freq[lane % (ROT_DIM // 2)], 0.0).reshape(1, 128).astype(F32)
    tabs = _rope_tables(positions.reshape(s_len, 1), inv_row)
    qk = _matmul("proj_qk", hn0, w_ai, "nn", F32, MM_T, MM_T, dm, epilogue=_rot_fwd, mrows=tabs, n_out=2 * aw)
    v = _matmul("proj_v", hn0, w_ai, "nn", F32, MM_T, aw // 2, dm, b_noff=2 * aw, n_out=aw)
    z0 = _matmul("proj_z", hn0, w_ai, "nn", F32, MM_T, aw // 2, dm, b_noff=3 * aw, n_out=aw)
    att = [_attn_fwd(g, qk, v) for g in range(3)]
    os_, lses = [a[0] for a in att], [a[1] for a in att]
    g_ao, g_si, g_so, g_small = _exchange_wait("weights_wait", w_handles, False, lses[2])
    w_ao = g_ao.reshape(aw, dm)
    w_si_t = g_si.reshape(n_ssd_in, dm)
    w_so = g_so.reshape(si, dm)
    conv_w = g_small[:, 0:CONV_WIDTH, :].transpose(1, 0, 2).reshape(CONV_WIDTH, sxbc)
    conv_b = g_small[:, 5, :].reshape(1, sxbc)
    snw = g_small[:, 6, 0:si // NDEV].reshape(1, si)
    a0, y0, x1 = _attn_out(os_, lses, z0, x0, gate[0], w_ao)

    hn1 = _norm_mod_fwd("norm1", x1, nw[1], scale[1], shift[1])
    ndt = 2 * SSD_HEADS
    z1 = _matmul("ssd_proj_z", hn1, w_si_t, "nt", F32, MM_T, MM_T, dm, n_out=si)
    xpre = _matmul("ssd_proj_xbc", hn1, w_si_t, "nt", F32, MM_T, MM_T, dm, b_noff=si, n_out=sxbc)
    dt_raw = _matmul("ssd_proj_dt", hn1, w_si_t, "nt", F32, MM_T, ndt, dm, b_noff=si + sxbc, n_out=ndt)
    xbc = _conv_fwd(xpre, conv_w, conv_b)
    dt_bias = ssd_dt_bias.reshape(1, 2 * SSD_HEADS)
    alog = ssd_a_log.reshape(1, 2 * SSD_HEADS)
    dt = _softplus_fwd(dt_raw, dt_bias)
    y_f, st_f = _ssd_fwd(xbc, dt, alog, 0)
    y_b, st_b = _ssd_fwd(xbc, dt, alog, 1)
    d_e = jnp.repeat(ssd_d.reshape(SSD_HEADS), HEAD_DIM).reshape(1, si)
    u = _gate_norm_fwd(y_f, y_b, xbc, z1, d_e, snw)
    y1 = _matmul("ssd_out", u, w_so, "nn", F32, MM_T, MM_T, si)

    fnw = final_norm_w.reshape(1, dm)
    dx2, dy1, g_fnw, dgate1, loss_part = _loss_bwd(x1, y1, tgt, gate[1], fnw)
    du = _matmul("ssd_out_dx", dy1, w_so, "nt", F32, MM_T, MM_T, dm)
    gw_so = _matmul("ssd_out_dw", u, dy1, "tn", BF16, MM_T, MM_T, MM_T)
    dys, dz1, g_snw, g_d = _gate_norm_bwd(du, y_f, y_b, xbc, z1, d_e, snw)
    dxbc_f, ddt_f, dalog_f = _ssd_bwd(xbc, dt, alog, st_f, dys, d_e, 0)
    dxbc_b, ddt_b, dalog_b = _ssd_bwd(xbc, dt, alog, st_b, dys, d_e, 1)
    dpre, g_cw, g_cb = _conv_bwd(xpre, dxbc_f, dxbc_b, conv_w, conv_b)
    ddt_raw, g_dtb = _softplus_bwd(ddt_f, ddt_b, dt_raw, dt_bias)
    dhn1 = [_matmul("ssd_proj_z_dx", dz1, w_si_t, "nn", F32, MM_T, MM_T, MM_T),
            _matmul("ssd_proj_xbc_dx", dpre, w_si_t, "nn", F32, MM_T, MM_T, MM_T, b_koff=si),
            _matmul("ssd_proj_dt_dx", ddt_raw, w_si_t, "nn", F32, MM_T, MM_T, ndt, b_koff=si + sxbc)]
    gw_si_t = _matmul("ssd_proj_z_dw", dz1, hn1, "tn", BF16, MM_T, MM_T, MM_T, dest=(n_ssd_in, 0, None))
    gw_si_t = _matmul("ssd_proj_xbc_dw", dpre, hn1, "tn", BF16, MM_T, MM_T, MM_T, dest=(n_ssd_in, si, gw_si_t))
    gw_si_t = _matmul("ssd_proj_dt_dw", ddt_raw, hn1, "tn", BF16, ndt, MM_T, MM_T, dest=(n_ssd_in, si + sxbc, gw_si_t))
    dx1, g_nw1, dsc1, dsh1 = _norm_mod_bwd("norm1_bwd", x1, dhn1, dx2, nw[1], scale[1], shift[1])

    l1_grads = [gw_so.reshape(NDEV, si // NDEV, dm), gw_si_t.reshape(NDEV, n_ssd_in // NDEV, dm),
                _pack_ssd_small_blocks(g_cw, g_cb, g_snw)]
    l1_handles, l1_token = _exchange_start("l1_grads_start", l1_grads, _landing_zones("l1_grads_place", l1_grads, True),
                                           True, dx1)

    dy0, dgate0 = _gate_bwd(dx1, y0, gate[0] + l1_token[0:1, 0:1])
    da0 = _matmul("attn_out_dx", dy0, w_ao, "nt", F32, MM_T, aw // 2, dm)
    gw_ao = _matmul("attn_out_dw", a0, dy0, "tn", BF16, aw // 2, MM_T, MM_T)
    dos, dls, dz0 = _mix_bwd(da0, os_, lses, z0)
    datt = [_attn_bwd(g, qk, v, os_[g], lses[g], dos[g], dls[g]) for g in range(3)]
    dqkv = _rot_pack_bwd([t[0] for t in datt], [t[1] for t in datt], [t[2] for t in datt], tabs)
    wcol = attn_w_in.shape[2]
    gw_ai = _matmul("proj_qkv_dw", hn0, dqkv, "tn", BF16, MM_T, wcol, MM_T, out_blocks=3 * aw // wcol, dest=(NDEV, 0, None))
    gw_ai = _matmul("proj_z_dw", hn0, dz0, "tn", BF16, MM_T, wcol, MM_T, out_blocks=aw // wcol,
                    dest=(NDEV, 3 * aw // wcol, gw_ai))
    l0_grads = [gw_ai, gw_ao.reshape(NDEV, aw // NDEV, dm)]
    l0_handles, l0_token = _exchange_start("l0_grads_start", l0_grads, _landing_zones("l0_grads_place", l0_grads, True),
                                           True, dqkv)
    zero_row = jnp.tile(l0_token[0:1], (1, dm // 128))
    after_start = lambda acc, t: acc + t
    dhn0 = [_matmul("proj_qkv_dx", dqkv, w_ai, "nt", F32, MM_T, MM_T, aw, n_out=dm, epilogue=after_start, ncols=(zero_row,)),
            _matmul("proj_z_dx", dz0, w_ai, "nt", F32, MM_T, MM_T, aw, b_koff=3 * aw, n_out=dm)]
    dx0, g_nw0, dsc0, dsh0 = _norm_mod_bwd("norm0_bwd", x0, dhn0, dx1, nw[0], scale[0], shift[0])

    rows6 = jnp.concatenate([dsh0, dsc0, dgate0, dsh1, dsc1, dgate1], axis=0)
    small_g = _pack_small(rows6, jnp.concatenate([g_nw0, g_nw1], axis=0), g_fnw, g_dtb, dalog_f + dalog_b, g_d, loss_part)
    (small_all,) = _all_gather("gather_small_grads", [small_g])
    small_w = _pack_small(mod_b.reshape(6, dm), norm_w, fnw, dt_bias, alog, ssd_d)
    small_m = _pack_small(m_mod_b.reshape(6, dm), m_norm_w, m_final_norm_w.reshape(1, dm), m_ssd_dt_bias.reshape(1, 64),
                          m_ssd_a_log.reshape(1, 64), m_ssd_d)
    small_v = _pack_small(v_mod_b.reshape(6, dm), v_norm_w, v_final_norm_w.reshape(1, dm), v_ssd_dt_bias.reshape(1, 64),
                          v_ssd_a_log.reshape(1, 64), v_ssd_d)
    small_out = _adamw("adamw_small", small_w, small_all, small_m, small_v, PACK_ROWS)
    loss = small_out[0][9, 256]
    sg, sd, sm, sv = (_unpack_small(p) for p in small_out)

    ncol = mod_w.shape[2]
    dmod_all = small_all[:, 0:6, :].reshape(NDEV, 2, 3 * dm)
    dmod_sh = lax.dynamic_slice_in_dim(dmod_all, me * ncol, ncol, axis=2).transpose(1, 0, 2)
    g_modw = _mod_grad(c_all, dmod_sh).reshape(1, 2 * dm, ncol)
    modw_out = _adamw("adamw_mod_w", mod_w.reshape(2 * dm, ncol), g_modw, m_mod_w.reshape(2 * dm, ncol),
                      v_mod_w.reshape(2 * dm, ncol), 256)

    r_so, r_si, r_small = _exchange_wait("l1_grads_wait", l1_handles, True, modw_out[0])
    si_out = [o.T for o in _adamw("adamw_ssd_w_in", ssd_w_in[0].T, r_si, m_ssd_w_in[0].T, v_ssd_w_in[0].T, n_ssd_in // NDEV, 256)]
    so_out = _adamw("adamw_ssd_w_out", ssd_w_out[0], r_so, m_ssd_w_out[0], v_ssd_w_out[0], 256)
    ssd_small_m = _pack_ssd_small(m_ssd_conv_w[0], m_ssd_conv_b, m_ssd_norm_w)
    ssd_small_v = _pack_ssd_small(v_ssd_conv_w[0], v_ssd_conv_b, v_ssd_norm_w)
    ss_out = _adamw("adamw_ssd_small", ssd_small, r_small, ssd_small_m, ssd_small_v, 8)
    r_ai, r_ao = _exchange_wait("l0_grads_wait", l0_handles, True, so_out[0])
    ai_out = _adamw("adamw_attn_w_in", attn_w_in[0], r_ai, m_attn_w_in[0], v_attn_w_in[0], 256)
    ao_out = _adamw("adamw_attn_w_out", attn_w_out[0], r_ao, m_attn_w_out[0], v_attn_w_out[0], 192)

    def ssd_small_unpack(p):
        return p[0:5][None], p[5:6], p[6:7, 0:si // NDEV]

    cwo, cbo, nwo = zip(*(ssd_small_unpack(p) for p in ss_out))
    per_kind = []
    for k in range(4):
        s = (sg, sd, sm, sv)[k]
        per_kind.append([
            s[1], modw_out[k].reshape(mod_w.shape), s[0], ai_out[k][None], ao_out[k][None], si_out[k][None],
            cwo[k], cbo[k], s[3], s[4], s[5], nwo[k], so_out[k][None], s[2]])
    return (loss, dx0.reshape(x.shape), *per_kind[0], *per_kind[1], *per_kind[2], *per_kind[3])


def _pack_ssd_small_blocks(g_cw, g_cb, g_nw):
    nper = g_cw.shape[1] // NDEV
    nwper = g_nw.shape[1] // NDEV

    def body(cw_ref, cb_ref, nw_ref, o_ref):
        o_ref[...] = jnp.zeros_like(o_ref)
        for d in range(NDEV):
            o_ref[d, 0:5, :] = cw_ref[:, d * nper:(d + 1) * nper]
            o_ref[d, 5:6, :] = cb_ref[:, d * nper:(d + 1) * nper]
            o_ref[d, 6:7, 0:nwper] = nw_ref[:, d * nwper:(d + 1) * nwper]

    return pl.pallas_call(body, name="pack_ssd_small_grads", out_shape=jax.ShapeDtypeStruct((NDEV, 8, nper), F32))(g_cw, g_cb, g_nw)
```

```python
import functools
import math

import jax
import jax.numpy as jnp
from jax import lax
from jax.experimental import pallas as pl
from jax.experimental.pallas import tpu as pltpu

F32 = jnp.float32
BF16 = jnp.bfloat16
HI = lax.Precision.HIGHEST
MESH = pl.DeviceIdType.MESH
NDEV = 8

NORM_EPS = 1e-6
ROPE_THETA = 500000.0
ROT_DIM = 16
HEAD_DIM = 64
DILATIONS = (1, 4, 16)
BAND = 64
NEG_BIG = -1e30
CHUNK = 128
SSD_HEADS = 32
SSD_GROUPS = 8
CONV_WIDTH = 5

ADAM_LR = 0.001
ADAM_B1 = 0.9
ADAM_B2 = 0.999
ADAM_EPS = 1e-08
ADAM_WD = 0.01
ADAM_STEP = 10

VMEM_BIG = 56 * 1024 * 1024
MM_T = 1024


def _params(sem=None, vmem=None):
    kw = {}
    if sem is not None:
        kw["dimension_semantics"] = sem
    if vmem is not None:
        kw["vmem_limit_bytes"] = vmem
    return pltpu.CompilerParams(**kw)


def _dg(a, b, ca, cb, prec=None):
    return lax.dot_general(a, b, (((ca,), (cb,)), ((), ())), preferred_element_type=F32, precision=prec)


def _nn(a, b):
    return _dg(a.astype(BF16), b.astype(BF16), 1, 0)


def _nt(a, b):
    return _dg(a.astype(BF16), b.astype(BF16), 1, 1)


def _tn(a, b):
    return _dg(a.astype(BF16), b.astype(BF16), 0, 0)


def _hnn(a, b):
    return _dg(a, b, 1, 0, HI)


@jax.custom_vjp
def _bnn(a, b):
    return _nn(a, b)


_bnn.defvjp(lambda a, b: (_nn(a, b), (a, b)), lambda r, g: (_nt(g, r[1]), _tn(r[0], g)))


@jax.custom_vjp
def _bnt(a, b):
    return _nt(a, b)


_bnt.defvjp(lambda a, b: (_nt(a, b), (a, b)), lambda r, g: (_nn(g, r[1]), _tn(g, r[0])))


@jax.custom_vjp
def _btn(a, b):
    return _tn(a, b)


_btn.defvjp(lambda a, b: (_tn(a, b), (a, b)), lambda r, g: (_nt(r[1], g), _nn(r[0], g)))


def _silu(x):
    return x * jax.nn.sigmoid(x)


def _matmul(name, a, b, mode, out_dtype, tm, tn, tk, *, epilogue=None, tiled=(), mrows=(), ncols=(),
            b_noff=0, b_koff=0, n_out=None, out_blocks=None, dest=None):
    if mode == "tn":
        K, M = a.shape
    else:
        M, K = a.shape
    N = n_out if n_out is not None else (b.shape[0] if mode == "nt" else b.shape[1])
    tm, tn, tk = min(tm, M), min(tn, N), min(tk, K)
    assert M % tm == 0 and N % tn == 0 and K % tk == 0, (name, M, N, K, tm, tn, tk)
    assert b_noff % tn == 0 and b_koff % tk == 0
    no, ko = b_noff // tn, b_koff // tk
    nk = K // tk
    if mode == "tn":
        a_spec = pl.BlockSpec((tk, tm), lambda i, j, k: (k, i))
    else:
        a_spec = pl.BlockSpec((tm, tk), lambda i, j, k: (i, k))
    if mode == "nt":
        b_spec = pl.BlockSpec((tn, tk), lambda i, j, k: (j + no, k + ko))
    else:
        b_spec = pl.BlockSpec((tk, tn), lambda i, j, k: (k + ko, j + no))
    specs = [a_spec, b_spec]
    specs += [pl.BlockSpec((tm, tn), lambda i, j, k: (i, j)) for _ in tiled]
    specs += [pl.BlockSpec((tm, r.shape[1]), lambda i, j, k: (i, 0)) for r in mrows]
    specs += [pl.BlockSpec((1, tn), lambda i, j, k: (0, j)) for _ in ncols]
    total, off, earlier = dest if dest is not None else (None, 0, None)
    if out_blocks is None:
        assert off % tm == 0
        mo = off // tm
        out_shape = jax.ShapeDtypeStruct((M if total is None else total, N), out_dtype)
        out_spec = pl.BlockSpec((tm, tn), lambda i, j, k: (i + mo, j))
    else:
        nper = N // out_blocks
        assert nper % tn == 0
        jb = nper // tn
        out_shape = jax.ShapeDtypeStruct((out_blocks if total is None else total, M, nper), out_dtype)
        out_spec = pl.BlockSpec((None, tm, tn), lambda i, j, k: (j // jb + off, i, j % jb))
    if earlier is not None:
        assert earlier.shape == out_shape.shape and earlier.dtype == out_shape.dtype
    ne = len(tiled) + len(mrows) + len(ncols)
    dot = {"nn": _nn, "nt": _nt, "tn": _tn}[mode]

    def body(a_ref, b_ref, *rest):
        extras, o_ref = rest[:ne], rest[ne]

        def finish(acc):
            if epilogue is not None:
                acc = epilogue(acc, *[e[...] for e in extras])
            o_ref[...] = acc.astype(o_ref.dtype)

        if nk == 1:
            finish(dot(a_ref[...], b_ref[...]))
        else:
            acc_ref = rest[ne + 1]
            k = pl.program_id(2)

            @pl.when(k == 0)
            def _():
                acc_ref[...] = jnp.zeros_like(acc_ref)

            acc_ref[...] += dot(a_ref[...], b_ref[...])

            @pl.when(k == nk - 1)
            def _():
                finish(acc_ref[...])

    args = [a, b, *tiled, *mrows, *ncols]
    aliases = {}
    if earlier is not None:
        specs.append(pl.BlockSpec(memory_space=pl.ANY))
        aliases = {len(args): 0}
        args.append(earlier)

    def body_with_dest(*refs):
        body(*refs[:2 + ne], *refs[2 + ne + (earlier is not None):])

    return pl.pallas_call(
        body_with_dest, name=name, out_shape=out_shape, grid=(M // tm, N // tn, nk),
        in_specs=specs, out_specs=out_spec, input_output_aliases=aliases,
        scratch_shapes=[] if nk == 1 else [pltpu.VMEM((tm, tn), F32)],
        compiler_params=_params(("parallel", "parallel", "arbitrary"), VMEM_BIG),
    )(*args)


def _rowwise(name, fn, tiled, consts, outs, accs, ts):
    tl = [(t, t.shape[1], 0) if not isinstance(t, tuple) else t for t in tiled]
    s_len = tl[0][0].shape[0]
    assert s_len % ts == 0
    nt_, nc_, no_ = len(tl), len(consts), len(outs)

    def body(*refs):
        t_refs, c_refs = refs[:nt_], refs[nt_:nt_ + nc_]
        o_refs, a_refs = refs[nt_ + nc_:nt_ + nc_ + no_], refs[nt_ + nc_ + no_:]
        res_o, res_a = fn(*[r[...] for r in t_refs], *[r[...] for r in c_refs])
        for r, v in zip(o_refs, res_o, strict=True):
            r[...] = v.astype(r.dtype)
        if a_refs:
            @pl.when(pl.program_id(0) == 0)
            def _():
                for r in a_refs:
                    r[...] = jnp.zeros_like(r)

            for r, v in zip(a_refs, res_a, strict=True):
                r[...] += v

    in_specs = [pl.BlockSpec((ts, w), functools.partial(lambda i, cb: (i, cb), cb=cb)) for (_, w, cb) in tl]
    in_specs += [pl.BlockSpec(c.shape, lambda i: (0, 0)) for c in consts]
    out_specs = [pl.BlockSpec((ts, c), lambda i: (i, 0)) for (c, _) in outs]
    out_specs += [pl.BlockSpec(shp, lambda i: (0, 0)) for shp in accs]
    out_shape = [jax.ShapeDtypeStruct((s_len, c), dt) for (c, dt) in outs]
    out_shape += [jax.ShapeDtypeStruct(shp, F32) for shp in accs]
    res = pl.pallas_call(
        body, name=name, out_shape=out_shape, grid=(s_len // ts,), in_specs=in_specs, out_specs=out_specs,
        compiler_params=_params(("arbitrary",) if accs else ("parallel",), VMEM_BIG),
    )(*[t[0] for t in tl], *consts)
    return res[:no_], res[no_:]


def _norm_mod_fn(x, nw, sc, sh):
    r = lax.rsqrt(jnp.mean(x * x, axis=-1, keepdims=True) + NORM_EPS)
    return (x * r * nw) * (1.0 + sc) + sh


def _norm_mod_fwd(name, x, nw, sc, sh):
    (hn,), _ = _rowwise(name, lambda x, nw, sc, sh: ([_norm_mod_fn(x, nw, sc, sh)], []),
                        [x], [nw, sc, sh], [(x.shape[1], BF16)], [], 512)
    return hn


def _norm_mod_bwd(name, x, dhn_parts, dres, nw, sc, sh):
    n = len(dhn_parts)
    d = x.shape[1]

    def fn(x, *rest):
        dhn = rest[0]
        for p in rest[1:n]:
            dhn = dhn + p
        dres, nw, sc, sh = rest[n:]
        _, vjp = jax.vjp(_norm_mod_fn, x, nw, sc, sh)
        dx, dnw, dsc, dsh = vjp(dhn)
        return [dx + dres], [dnw, dsc, dsh]

    (dx,), (g_nw, dsc, dsh) = _rowwise(name, fn, [x, *dhn_parts, dres], [nw, sc, sh], [(d, F32)],
                                       [(1, d), (1, d), (1, d)], 256)
    return dx, g_nw, dsc, dsh


def _rope_tables(pos_col, inv_row):
    def fn(pos, inv):
        ang = pos.astype(F32) * inv
        e = lax.broadcasted_iota(jnp.int32, (1, 128), 1) % HEAD_DIM
        cos, sin = jnp.cos(ang), jnp.sin(ang)
        half = ROT_DIM // 2
        return [jnp.where(e < ROT_DIM, cos, 1.0), jnp.where(e < half, -sin, 0.0),
                jnp.where((e >= half) & (e < ROT_DIM), sin, 0.0)], []

    (c, sa, sb), _ = _rowwise("rope_tables", fn, [pos_col], [inv_row], [(128, F32)] * 3, [], 512)
    return c, sa, sb


def _rot_fwd(t, c, sa, sb):
    n = t.shape[1]
    rep = n // 128
    c, sa, sb = (jnp.tile(u, (1, rep)) for u in (c, sa, sb))
    return t * c + pltpu.roll(t, n - ROT_DIM // 2, 1) * sa + pltpu.roll(t, ROT_DIM // 2, 1) * sb


def _rot_bwd(g, c, sa, sb):
    n = g.shape[1]
    rep = n // 128
    c, sa, sb = (jnp.tile(u, (1, rep)) for u in (c, sa, sb))
    return g * c + pltpu.roll(g * sa, ROT_DIM // 2, 1) + pltpu.roll(g * sb, n - ROT_DIM // 2, 1)


ATT_TQ = 128
ATT_TK = ATT_TQ + 2 * BAND


def _attn_specs(g, s_len):
    def blk(off):
        return pl.BlockSpec((s_len, 128), functools.partial(lambda hp, off: (0, off + hp), off=off))

    return blk(4 * g), blk(12 + 4 * g), blk(4 * g), blk(0)


def _attn_tile_geometry(t, d, l):
    nts = l // ATT_TQ
    r = t // nts
    ts = t % nts
    q0 = ts * ATT_TQ
    ws = jnp.clip(q0 - BAND, 0, l - ATT_TK)
    kind = jnp.where(ts == 0, 0, jnp.where(ts == nts - 1, 2, 1))
    if d == 1:
        return pl.ds(pl.multiple_of(q0, ATT_TQ), ATT_TQ), pl.ds(pl.multiple_of(ws, BAND), ATT_TK), kind
    return pl.ds(r + d * q0, ATT_TQ, stride=d), pl.ds(r + d * ws, ATT_TK, stride=d), kind


def _attn_fill_bias(bias_ref):
    iq = lax.broadcasted_iota(jnp.int32, (2 * ATT_TQ, 1), 0) % ATT_TQ
    ik = lax.broadcasted_iota(jnp.int32, (1, ATT_TK), 1)
    for i, off in enumerate((0, -BAND, -2 * BAND)):
        bias_ref[i] = jnp.where(jnp.abs(ik + off - iq) <= BAND, 0.0, NEG_BIG)


def _split_heads(t, in_h):
    zero = jnp.zeros_like(t)
    return jnp.concatenate([jnp.where(in_h[0], t, zero), jnp.where(in_h[1], t, zero)], axis=0)


def _attn_fwd(g, qk, v):
    s_len = qk.shape[0]
    d = DILATIONS[g]
    l = s_len // d
    assert l % ATT_TQ == 0 and l >= ATT_TK
    q_spec, k_spec, v_spec, o_spec = _attn_specs(g, s_len)
    scale = 1.0 / math.sqrt(HEAD_DIM)

    def body(q_ref, k_ref, v_ref, o_ref, lse_ref, bias_ref):
        lane = lax.broadcasted_iota(jnp.int32, (1, 128), 1)
        in_h = [lane < HEAD_DIM, lane >= HEAD_DIM]
        _attn_fill_bias(bias_ref)

        def tile(t, carry):
            rows, win, kind = _attn_tile_geometry(t, d, l)
            q = (q_ref[rows, :] * scale).astype(BF16)
            k = k_ref[win, :].astype(BF16)
            vv = v_ref[win, :].astype(BF16)
            s = _nt(_split_heads(q, in_h), k) + bias_ref[kind]
            m = jnp.max(s, axis=1, keepdims=True)
            p = jnp.exp(s - m)
            den = jnp.sum(p, axis=1, keepdims=True)
            out = _nn(p, vv) / den
            lse = m + jnp.log(den)
            o_ref[rows, :] = jnp.where(in_h[0], out[:ATT_TQ], out[ATT_TQ:])
            lse_ref[rows, :] = jnp.where(in_h[0], lse[:ATT_TQ], lse[ATT_TQ:])
            return carry

        lax.fori_loop(0, s_len // ATT_TQ, tile, 0, unroll=4)

    return pl.pallas_call(
        body, name=f"attn_fwd_g{g}", grid=(4,),
        out_shape=[jax.ShapeDtypeStruct((s_len, 512), F32)] * 2,
        in_specs=[q_spec, k_spec, v_spec], out_specs=[o_spec, o_spec],
        scratch_shapes=[pltpu.VMEM((3, 2 * ATT_TQ, ATT_TK), F32)],
        compiler_params=_params(("parallel",), VMEM_BIG),
    )(qk, qk, v)


def _attn_bwd(g, qk, v, o, lse, do, dlse):
    s_len = qk.shape[0]
    d = DILATIONS[g]
    l = s_len // d
    q_spec, k_spec, v_spec, o_spec = _attn_specs(g, s_len)
    scale = 1.0 / math.sqrt(HEAD_DIM)

    def body(q_ref, k_ref, v_ref, o_ref, lse_ref, do_ref, dlse_ref, dq_ref, dk_ref, dv_ref, bias_ref):
        lane = lax.broadcasted_iota(jnp.int32, (1, 128), 1)
        in_h = [lane < HEAD_DIM, lane >= HEAD_DIM]
        dk_ref[...] = jnp.zeros_like(dk_ref)
        dv_ref[...] = jnp.zeros_like(dv_ref)
        _attn_fill_bias(bias_ref)

        def tile(t, carry):
            rows, win, kind = _attn_tile_geometry(t, d, l)
            k, vv = k_ref[win, :].astype(BF16), v_ref[win, :].astype(BF16)
            dout, lse_t, dlse_t = do_ref[rows, :], lse_ref[rows, :], dlse_ref[rows, :]
            od = dout * o_ref[rows, :]
            q2 = _split_heads((q_ref[rows, :] * scale).astype(BF16), in_h)
            do2 = _split_heads(dout.astype(BF16), in_h)
            head_col = lambda a: jnp.concatenate([a[:, 0:1], a[:, HEAD_DIM:HEAD_DIM + 1]], axis=0)
            delta = jnp.concatenate([jnp.sum(jnp.where(m, od, 0.0), axis=1, keepdims=True) for m in in_h], axis=0)
            p = jnp.exp(_nt(q2, k) + bias_ref[kind] - head_col(lse_t))
            ds = (p * (_nt(do2, vv) - delta + head_col(dlse_t))).astype(BF16)
            dq2 = _nn(ds, k) * scale
            dq_ref[rows, :] = jnp.where(in_h[0], dq2[:ATT_TQ], dq2[ATT_TQ:])
            dk_ref[win, :] += _tn(ds, q2)
            dv_ref[win, :] += _tn(p, do2)
            return carry

        lax.fori_loop(0, s_len // ATT_TQ, tile, 0, unroll=4)

    return pl.pallas_call(
        body, name=f"attn_bwd_g{g}", grid=(4,),
        out_shape=[jax.ShapeDtypeStruct((s_len, 512), F32)] * 3,
        in_specs=[q_spec, k_spec, v_spec, o_spec, o_spec, o_spec, o_spec], out_specs=[o_spec] * 3,
        scratch_shapes=[pltpu.VMEM((3, 2 * ATT_TQ, ATT_TK), F32)],
        compiler_params=_params(("parallel",), VMEM_BIG),
    )(qk, qk, v, o, lse, do, dlse)


def _mix_weights(ls):
    mx = jnp.maximum(jnp.maximum(ls[0], ls[1]), ls[2])
    es = [jnp.exp(x - mx) for x in ls]
    tot = es[0] + es[1] + es[2]
    return [e / tot for e in es]


def _attn_out(os_, lses, z, x, gate, w_out):
    s_len, dm = x.shape
    tm = 256
    wdt = 512

    def body(o0, o1, o2, l0, l1, l2, z_ref, x_ref, g_ref, w_ref, a_ref, y_ref, x1_ref):
        alphas = _mix_weights([l0[...], l1[...], l2[...]])
        y = jnp.zeros((tm, dm), F32)
        for g, o_ref in enumerate((o0, o1, o2)):
            a_g = (o_ref[...] * alphas[g] * _silu(z_ref[:, g * wdt:(g + 1) * wdt])).astype(BF16)
            a_ref[:, g * wdt:(g + 1) * wdt] = a_g
            y = y + _nn(a_g, w_ref[g * wdt:(g + 1) * wdt, :])
        y_ref[...] = y
        x1_ref[...] = x_ref[...] + g_ref[...] * y

    row = lambda c: pl.BlockSpec((tm, c), lambda i: (i, 0))
    return pl.pallas_call(
        body, name="attn_out", grid=(s_len // tm,),
        out_shape=[jax.ShapeDtypeStruct((s_len, 3 * wdt), BF16), jax.ShapeDtypeStruct((s_len, dm), F32),
                   jax.ShapeDtypeStruct((s_len, dm), F32)],
        in_specs=[row(wdt)] * 6 + [row(3 * wdt), row(dm), pl.BlockSpec((1, dm), lambda i: (0, 0)),
                                   pl.BlockSpec(w_out.shape, lambda i: (0, 0))],
        out_specs=[row(3 * wdt), row(dm), row(dm)],
        compiler_params=_params(("parallel",), VMEM_BIG),
    )(*os_, *lses, z, x, gate, w_out)


def _mix_bwd(da, os_, lses, z):
    wdt = 512

    def fn(da, o0, o1, o2, l0, l1, l2, z):
        os_t, ls = [o0, o1, o2], [l0, l1, l2]
        alphas = _mix_weights(ls)
        hi = lax.broadcasted_iota(jnp.int32, (wdt, wdt), 0) // HEAD_DIM
        hj = lax.broadcasted_iota(jnp.int32, (wdt, wdt), 1) // HEAD_DIM
        seg = (hi == hj).astype(F32)
        dos, dal, dzs = [], [], []
        for g in range(3):
            zg = z[:, g * wdt:(g + 1) * wdt]
            sig = jax.nn.sigmoid(zg)
            dag = da[:, g * wdt:(g + 1) * wdt]
            dmix = dag * zg * sig
            dzs.append(dag * os_t[g] * alphas[g] * (sig * (1.0 + zg * (1.0 - sig))))
            dos.append(dmix * alphas[g])
            dal.append(_hnn(dmix * os_t[g], seg))
        mean = alphas[0] * dal[0] + alphas[1] * dal[1] + alphas[2] * dal[2]
        dls = [alphas[g] * (dal[g] - mean) for g in range(3)]
        return dos + dls + [jnp.concatenate(dzs, axis=1)], []

    outs, _ = _rowwise("mix_bwd", fn, [da, *os_, *lses, z], [], [(wdt, F32)] * 6 + [(3 * wdt, BF16)], [], 256)
    return outs[:3], outs[3:6], outs[6]


def _rot_pack_bwd(dqs, dks, dvs, tabs):
    wdt = 512

    def fn(*args):
        grads, (c, sa, sb) = args[:9], args[9:]
        cols = [_rot_bwd(gq, c, sa, sb) for gq in grads[:6]] + list(grads[6:])
        return [jnp.concatenate(cols, axis=1)], []

    (out,), _ = _rowwise("rot_pack_bwd", fn, [*dqs, *dks, *dvs, *tabs], [], [(9 * wdt, BF16)], [], 256)
    return out


CONV_CB = 128
CONV_R = 256
CONV_PAD = 8


def _conv_taps(buf, base, off, sign):
    return [buf[pl.ds(base + off + sign * j, CONV_R), :] for j in range(CONV_WIDTH)]


def _conv_tap_sum(taps, w):
    acc = None
    for j, t in enumerate(taps):
        term = t * w[j:j + 1, :]
        acc = term if acc is None else acc + term
    return acc


def _conv_fwd(xpre, cw, cb):
    s_len, ch = xpre.shape
    nchunk = s_len // CONV_R

    def body(x_ref, w_ref, b_ref, o_ref, xp):
        zero = jnp.zeros((CONV_PAD, CONV_CB), F32)
        xp[0:CONV_PAD, :] = zero
        xp[s_len + CONV_PAD:s_len + 2 * CONV_PAD, :] = zero

        def fill(ci, carry):
            base = pl.multiple_of(ci * CONV_R, CONV_R)
            xp[pl.ds(base + CONV_PAD, CONV_R), :] = x_ref[pl.ds(base, CONV_R), :]
            return carry

        lax.fori_loop(0, nchunk, fill, 0)
        w = w_ref[...]
        b = b_ref[...]

        def chunk(ci, carry):
            base = pl.multiple_of(ci * CONV_R, CONV_R)
            u = _conv_tap_sum(_conv_taps(xp, base, CONV_PAD - CONV_WIDTH // 2, 1), w) + b
            o_ref[pl.ds(base, CONV_R), :] = _silu(u)
            return carry

        lax.fori_loop(0, nchunk, chunk, 0)

    col = lambda r: pl.BlockSpec((r, CONV_CB), lambda j: (0, j))
    return pl.pallas_call(
        body, name="conv_fwd", grid=(ch // CONV_CB,), out_shape=jax.ShapeDtypeStruct((s_len, ch), F32),
        in_specs=[col(s_len), col(CONV_WIDTH), col(1)], out_specs=col(s_len),
        scratch_shapes=[pltpu.VMEM((s_len + 2 * CONV_PAD, CONV_CB), F32)],
        compiler_params=_params(("parallel",), VMEM_BIG),
    )(xpre, cw, cb)


def _conv_bwd(xpre, da, db, cw, cb):
    s_len, ch = xpre.shape
    nchunk = s_len // CONV_R
    half = CONV_WIDTH // 2

    def body(x_ref, da_ref, db_ref, w_ref, b_ref, dx_ref, gw_ref, gb_ref, xp, dcp):
        zero = jnp.zeros((CONV_PAD, CONV_CB), F32)
        for buf in (xp, dcp):
            buf[0:CONV_PAD, :] = zero
            buf[s_len + CONV_PAD:s_len + 2 * CONV_PAD, :] = zero

        def fill(ci, carry):
            base = pl.multiple_of(ci * CONV_R, CONV_R)
            xp[pl.ds(base + CONV_PAD, CONV_R), :] = x_ref[pl.ds(base, CONV_R), :]
            return carry

        lax.fori_loop(0, nchunk, fill, 0)
        w = w_ref[...]
        b = b_ref[...]

        def first(ci, carry):
            base = pl.multiple_of(ci * CONV_R, CONV_R)
            taps = _conv_taps(xp, base, CONV_PAD - half, 1)
            u = _conv_tap_sum(taps, w) + b
            sig = jax.nn.sigmoid(u)
            dc = (da_ref[pl.ds(base, CONV_R), :] + db_ref[pl.ds(base, CONV_R), :]) * (sig * (1.0 + u * (1.0 - sig)))
            dcp[pl.ds(base + CONV_PAD, CONV_R), :] = dc
            gb = carry[0] + jnp.sum(dc, axis=0, keepdims=True)
            gws = [carry[1 + j] + jnp.sum(dc * taps[j], axis=0, keepdims=True) for j in range(CONV_WIDTH)]
            return (gb, *gws)

        z1 = jnp.zeros((1, CONV_CB), F32)
        sums = lax.fori_loop(0, nchunk, first, (z1,) * (1 + CONV_WIDTH))
        gb_ref[...] = sums[0]
        for j in range(CONV_WIDTH):
            gw_ref[j:j + 1, :] = sums[1 + j]

        def second(ci, carry):
            base = pl.multiple_of(ci * CONV_R, CONV_R)
            dx_ref[pl.ds(base, CONV_R), :] = _conv_tap_sum(_conv_taps(dcp, base, CONV_PAD + half, -1), w).astype(dx_ref.dtype)
            return carry

        lax.fori_loop(0, nchunk, second, 0)

    col = lambda r: pl.BlockSpec((r, CONV_CB), lambda j: (0, j))
    return pl.pallas_call(
        body, name="conv_bwd", grid=(ch // CONV_CB,),
        out_shape=[jax.ShapeDtypeStruct((s_len, ch), BF16), jax.ShapeDtypeStruct((CONV_WIDTH, ch), F32),
                   jax.ShapeDtypeStruct((1, ch), F32)],
        in_specs=[col(s_len), col(s_len), col(s_len), col(CONV_WIDTH), col(1)],
        out_specs=[col(s_len), col(CONV_WIDTH), col(1)],
        scratch_shapes=[pltpu.VMEM((s_len + 2 * CONV_PAD, CONV_CB), F32)] * 2,
        compiler_params=_params(("parallel",), VMEM_BIG),
    )(xpre, da, db, cw, cb)


SSD_GW = 256
SSD_N = 128


def _bf16_parts(x, n):
    parts, rest = [], x
    for _ in range(n):
        p = rest.astype(BF16)
        parts.append(p)
        rest = rest - p.astype(F32)
    return parts


@jax.custom_vjp
def _expand(x, e):
    eb = e.astype(BF16)
    return sum(_dg(p, eb, 1, 0) for p in _bf16_parts(x, 2))


def _expand_fwd(x, e):
    return _expand(x, e), e


def _expand_bwd(e, g):
    eb = e.astype(BF16)
    return sum(_dg(p, eb, 1, 1) for p in _bf16_parts(g, 2)), jnp.zeros_like(e)


_expand.defvjp(_expand_fwd, _expand_bwd)


@jax.custom_vjp
def _running_sum(tri, x):
    tb = tri.astype(BF16)
    return sum(_dg(tb, p, 1, 0) for p in _bf16_parts(x, 3))


def _running_sum_fwd(tri, x):
    return _running_sum(tri, x), tri


def _running_sum_bwd(tri, g):
    tb = tri.astype(BF16)
    return jnp.zeros_like(tri), sum(_dg(tb, p, 0, 0) for p in _bf16_parts(g, 3))


_running_sum.defvjp(_running_sum_fwd, _running_sum_bwd)


def _ssd_mask(dirn):
    ri = lax.broadcasted_iota(jnp.int32, (CHUNK, CHUNK), 0)
    cj = lax.broadcasted_iota(jnp.int32, (CHUNK, CHUNK), 1)
    return (cj <= ri) if dirn == 0 else (cj >= ri)


def _ssd_rowsel(dirn):
    last = CHUNK - 1 if dirn == 0 else 0
    return (lax.broadcasted_iota(jnp.int32, (CHUNK, 1), 0) == last).astype(F32)


def _ssd_chunk_pre(dirn):
    nh = 2 * SSD_HEADS

    def f(dt, alog):
        da = dt * (-jnp.exp(alog))
        cum = _running_sum(_ssd_mask(dirn).astype(F32), da)
        tot = jnp.sum(cum * _ssd_rowsel(dirn), axis=0, keepdims=True)
        hh = lax.broadcasted_iota(jnp.int32, (nh, SSD_HEADS * HEAD_DIM), 0)
        jj = lax.broadcasted_iota(jnp.int32, (nh, SSD_HEADS * HEAD_DIM), 1)
        expand = (hh == dirn * SSD_HEADS + jj // HEAD_DIM).astype(F32)
        return cum, cum.T, _expand(dt, expand), _expand(jnp.exp(tot - cum), expand), _expand(jnp.exp(cum), expand)

    return f


def _ssd_group_fn(g, dirn, stacked):
    nh = 2 * SSD_HEADS

    def f(xs, bm, cm, st, cum, cum_t, dt_e, w_e, ce_e):
        mask = _ssd_mask(dirn)
        xdt = xs * dt_e
        cd_e = jnp.sum(ce_e * _ssd_rowsel(dirn), axis=0, keepdims=True)
        cb = _bnt(cm, bm)
        lane_head = lax.broadcasted_iota(jnp.int32, (1, SSD_GW), 1) // HEAD_DIM
        y = _bnn(cm, st) * ce_e
        decayed, inputs = [], []
        for j in range(4):
            hidx = dirn * SSD_HEADS + 4 * g + j
            col = jnp.sum(cum * (lax.broadcasted_iota(jnp.int32, (1, nh), 1) == hidx).astype(F32), axis=1, keepdims=True)
            row = jnp.sum(cum_t * (lax.broadcasted_iota(jnp.int32, (nh, 1), 0) == hidx).astype(F32), axis=0, keepdims=True)
            dec = cb * jnp.exp(jnp.where(mask, col - row, NEG_BIG))
            head = (lane_head == j).astype(F32)
            if stacked:
                decayed.append(dec)
                inputs.append(xdt * head)
            else:
                y = y + _bnn(dec, xdt) * head
        if stacked:
            y = y + _bnn(jnp.concatenate(decayed, axis=1), jnp.concatenate(inputs, axis=0))
        st_out = st * cd_e + _btn(bm, xdt * w_e)
        return y, st_out

    return f


def _ssd_in_specs(kk):
    ln = CHUNK
    return [pl.BlockSpec((ln, 2048), lambda i: (kk(i), 0)),
            pl.BlockSpec((ln, 1024), lambda i: (kk(i), 2)),
            pl.BlockSpec((ln, 1024), lambda i: (kk(i), 3)),
            pl.BlockSpec((ln, 2 * SSD_HEADS), lambda i: (kk(i), 0)),
            pl.BlockSpec((1, 2 * SSD_HEADS), lambda i: (0, 0))]


def _ssd_fwd(xbc, dt, alog, dirn):
    s_len = xbc.shape[0]
    nc = s_len // CHUNK
    kk = (lambda i: i) if dirn == 0 else (lambda i: nc - 1 - i)

    def body(x_ref, b_ref, c_ref, dt_ref, al_ref, y_ref, sts_ref, st):
        @pl.when(pl.program_id(0) == 0)
        def _():
            st[...] = jnp.zeros_like(st)

        sts_ref[0] = st[...]
        cum, cum_t, dt_e, w_e, ce_e = _ssd_chunk_pre(dirn)(dt_ref[...], al_ref[...])
        for g in range(SSD_GROUPS):
            xc = slice(g * SSD_GW, (g + 1) * SSD_GW)
            gc = slice(g * SSD_N, (g + 1) * SSD_N)
            y, st_new = _ssd_group_fn(g, dirn, True)(x_ref[:, xc], b_ref[:, gc], c_ref[:, gc], st[:, xc], cum, cum_t,
                                               dt_e[:, xc], w_e[:, xc], ce_e[:, xc])
            y_ref[:, xc] = y
            st[:, xc] = st_new

    return pl.pallas_call(
        body, name=f"ssd_fwd_d{dirn}", grid=(nc,),
        out_shape=[jax.ShapeDtypeStruct((s_len, 2048), F32), jax.ShapeDtypeStruct((nc, SSD_N, 2048), F32)],
        in_specs=_ssd_in_specs(kk),
        out_specs=[pl.BlockSpec((CHUNK, 2048), lambda i: (kk(i), 0)),
                   pl.BlockSpec((1, SSD_N, 2048), lambda i: (kk(i), 0, 0))],
        scratch_shapes=[pltpu.VMEM((SSD_N, 2048), F32)],
        compiler_params=_params(("arbitrary",), VMEM_BIG),
    )(xbc, xbc, xbc, dt, alog)


def _ssd_bwd(xbc, dt, alog, states, dy, d_e, dirn):
    s_len = xbc.shape[0]
    nc = s_len // CHUNK
    kk = (lambda i: nc - 1 - i) if dirn == 0 else (lambda i: i)

    def body(x_ref, b_ref, c_ref, dt_ref, al_ref, sts_ref, dy_ref, de_ref, dx_ref, ddt_ref, dal_ref, dst):
        @pl.when(pl.program_id(0) == 0)
        def _():
            dst[...] = jnp.zeros_like(dst)
            dal_ref[...] = jnp.zeros_like(dal_ref)

        (cum, cum_t, dt_e, w_e, ce_e), pre_vjp = jax.vjp(_ssd_chunk_pre(dirn), dt_ref[...], al_ref[...])
        dcum = jnp.zeros_like(cum)
        dcum_t = jnp.zeros_like(cum_t)
        d_dt_e, d_w_e, d_ce_e = [], [], []
        for g in range(SSD_GROUPS):
            xc = slice(g * SSD_GW, (g + 1) * SSD_GW)
            gc = slice(g * SSD_N, (g + 1) * SSD_N)
            _, vjp = jax.vjp(_ssd_group_fn(g, dirn, False), x_ref[:, xc], b_ref[:, gc], c_ref[:, gc], sts_ref[0, :, xc], cum, cum_t,
                             dt_e[:, xc], w_e[:, xc], ce_e[:, xc])
            dyg = dy_ref[:, xc]
            dxs, dbm, dcm, dst_g, dcum_g, dcum_t_g, ddte_g, dwe_g, dcee_g = vjp((dyg, dst[:, xc]))
            if dirn == 0:
                dxs = dxs + dyg * de_ref[:, xc]
            dx_ref[:, xc] = dxs
            dx_ref[:, 2048 + g * SSD_N:2048 + (g + 1) * SSD_N] = dbm
            dx_ref[:, 3072 + g * SSD_N:3072 + (g + 1) * SSD_N] = dcm
            dst[:, xc] = dst_g
            dcum = dcum + dcum_g
            dcum_t = dcum_t + dcum_t_g
            d_dt_e.append(ddte_g)
            d_w_e.append(dwe_g)
            d_ce_e.append(dcee_g)
        ddt, dal = pre_vjp((dcum, dcum_t, jnp.concatenate(d_dt_e, axis=1), jnp.concatenate(d_w_e, axis=1),
                            jnp.concatenate(d_ce_e, axis=1)))
        ddt_ref[...] = ddt
        dal_ref[...] += dal

    return pl.pallas_call(
        body, name=f"ssd_bwd_d{dirn}", grid=(nc,),
        out_shape=[jax.ShapeDtypeStruct((s_len, 4096), F32), jax.ShapeDtypeStruct((s_len, 2 * SSD_HEADS), F32),
                   jax.ShapeDtypeStruct((1, 2 * SSD_HEADS), F32)],
        in_specs=_ssd_in_specs(kk) + [pl.BlockSpec((1, SSD_N, 2048), lambda i: (kk(i), 0, 0)),
                                      pl.BlockSpec((CHUNK, 2048), lambda i: (kk(i), 0)),
                                      pl.BlockSpec((1, 2048), lambda i: (0, 0))],
        out_specs=[pl.BlockSpec((CHUNK, 4096), lambda i: (kk(i), 0)),
                   pl.BlockSpec((CHUNK, 2 * SSD_HEADS), lambda i: (kk(i), 0)),
                   pl.BlockSpec((1, 2 * SSD_HEADS), lambda i: (0, 0))],
        scratch_shapes=[pltpu.VMEM((SSD_N, 2048), F32)],
        compiler_params=_params(("arbitrary",), VMEM_BIG),
    )(xbc, xbc, xbc, dt, alog, states, dy, d_e)


def _gate_norm_fn(yf, yb, xs, z, d_e, nw):
    yg = (yf + yb + xs * d_e) * _silu(z)
    return yg * lax.rsqrt(jnp.mean(yg * yg, axis=-1, keepdims=True) + NORM_EPS) * nw


def _gate_norm_fwd(yf, yb, xbc, z, d_e, nw):
    (u,), _ = _rowwise("ssd_gate_norm", lambda *a: ([_gate_norm_fn(*a)], []),
                       [yf, yb, (xbc, 2048, 0), z], [d_e, nw], [(2048, BF16)], [], 256)
    return u


def _gate_norm_bwd(du, yf, yb, xbc, z, d_e, nw):
    def fn(du, yf, yb, xs, z, d_e, nw):
        sig = jax.nn.sigmoid(z)
        gate = z * sig
        ysum = yf + yb + xs * d_e
        yg = ysum * gate
        r = lax.rsqrt(jnp.mean(yg * yg, axis=-1, keepdims=True) + NORM_EPS)
        t = du * nw
        dyg = t * r - yg * (jnp.mean(t * yg, axis=-1, keepdims=True) * (r * r * r))
        dys = dyg * gate
        dz = dyg * ysum * (sig * (1.0 + z * (1.0 - sig)))
        dnw = jnp.sum(du * yg * r, axis=0, keepdims=True)
        dde = jnp.sum(dys * xs, axis=0, keepdims=True)
        hh = lax.broadcasted_iota(jnp.int32, (2048, SSD_HEADS), 0) // HEAD_DIM
        jj = lax.broadcasted_iota(jnp.int32, (2048, SSD_HEADS), 1)
        return [dys, dz], [dnw, _hnn(jnp.broadcast_to(dde, (8, 2048)), (hh == jj).astype(F32))[0:1]]

    (dys, dz), (g_nw, g_d) = _rowwise("ssd_gate_norm_bwd", fn, [du, yf, yb, (xbc, 2048, 0), z], [d_e, nw],
                                      [(2048, F32), (2048, BF16)], [(1, 2048), (1, SSD_HEADS)], 256)
    return dys, dz, g_nw, g_d


def _loss_bwd(x1, y1, tgt, gate, fnw):
    dm = x1.shape[1]

    def fn(x1, y1, tgt, gate, fnw):
        def head(x2, fnw):
            yf = (x2 * lax.rsqrt(jnp.mean(x2 * x2, axis=-1, keepdims=True) + NORM_EPS)) * fnw
            err = yf - tgt
            return 0.5 * jnp.sum(jnp.mean(err * err, axis=-1, keepdims=True), axis=0, keepdims=True)

        x2 = x1 + gate * y1
        loss, vjp = jax.vjp(head, x2, fnw)
        dx2, dfnw = vjp(jnp.ones((1, 1), F32))
        return [dx2, gate * dx2], [dfnw, jnp.sum(dx2 * y1, axis=0, keepdims=True), jnp.broadcast_to(loss, (1, 128))]

    (dx2, dy1), (g_fnw, dgate, loss) = _rowwise("loss_bwd", fn, [x1, y1, tgt], [gate, fnw], [(dm, F32), (dm, BF16)],
                                                [(1, dm), (1, dm), (1, 128)], 256)
    return dx2, dy1, g_fnw, dgate, loss


def _gate_bwd(dx, y, gate):
    dm = dx.shape[1]
    (dy,), (dgate,) = _rowwise("gate_bwd", lambda dx, y, gate: ([gate * dx], [jnp.sum(dx * y, axis=0, keepdims=True)]),
                               [dx, y], [gate], [(dm, BF16)], [(1, dm)], 512)
    return dy, dgate


def _softplus_fwd(dt_raw, bias):
    (dt,), _ = _rowwise("dt_softplus", lambda r, b: ([jax.nn.softplus(r + b)], []), [dt_raw], [bias],
                        [(dt_raw.shape[1], F32)], [], 512)
    return dt


def _softplus_bwd(ddt_f, ddt_b, dt_raw, bias):
    def fn(df, db, r, b):
        g = (df + db) * jax.nn.sigmoid(r + b)
        return [g], [jnp.sum(g, axis=0, keepdims=True)]

    w = dt_raw.shape[1]
    (g,), (gb,) = _rowwise("dt_softplus_bwd", fn, [ddt_f, ddt_b, dt_raw], [bias], [(w, BF16)], [(1, w)], 512)
    return g, gb


def _whole(a):
    nd = len(a.shape)
    return pl.BlockSpec(a.shape, lambda *_: (0,) * nd)


def _mod_part(c_all, mod_w):
    nl, _, ncol = mod_w.shape
    nb = c_all.shape[0]

    def body(c_ref, w_ref, o_ref):
        cond = _silu(c_ref[...])
        for i in range(nl):
            o_ref[i * nb:(i + 1) * nb, :] = _nn(cond, w_ref[i])

    return pl.pallas_call(body, name="mod_part", out_shape=jax.ShapeDtypeStruct((nl * nb, ncol), F32),
                          compiler_params=_params(None, VMEM_BIG))(c_all, mod_w)


def _mod_finish(mod_nb, mod_b):
    def body(a_ref, b_ref, o_ref):
        o_ref[...] = a_ref[...] + b_ref[...]

    return pl.pallas_call(body, name="mod_finish", out_shape=jax.ShapeDtypeStruct(mod_b.shape, F32))(mod_nb, mod_b)


def _mod_grad(c_all, dmod_sh):
    nl, nb, ncol = dmod_sh.shape
    dm = c_all.shape[1]

    def body(c_ref, d_ref, o_ref):
        cond = _silu(c_ref[...])
        for i in range(nl):
            o_ref[i] = _tn(cond, d_ref[i])

    return pl.pallas_call(body, name="mod_grad", out_shape=jax.ShapeDtypeStruct((nl, dm, ncol), F32),
                          compiler_params=_params(None, VMEM_BIG))(c_all, dmod_sh)


PACK_ROWS = 16
PACK_COLS = 1024


def _pack_small(rows6, nw2, fnw, b64, a64, d32, extra=None):
    args = [rows6, nw2, fnw, b64, a64, d32] + ([extra] if extra is not None else [])

    def body(*refs):
        o_ref = refs[-1]
        o_ref[...] = jnp.zeros_like(o_ref)
        o_ref[0:6, :] = refs[0][...]
        o_ref[6:8, :] = refs[1][...]
        o_ref[8:9, :] = refs[2][...]
        o_ref[9:10, 0:64] = refs[3][...]
        o_ref[9:10, 64:128] = refs[4][...]
        o_ref[9:10, 128:160] = refs[5][...]
        if extra is not None:
            o_ref[9:10, 256:384] = refs[6][...]

    return pl.pallas_call(body, name="pack_small", out_shape=jax.ShapeDtypeStruct((PACK_ROWS, PACK_COLS), F32))(*args)


def _unpack_small(p):
    return (p[0:6].reshape(2, 3 * PACK_COLS), p[6:8], p[8], p[9, 0:64].reshape(1, 2, 32), p[9, 64:128].reshape(1, 2, 32),
            p[9, 128:160].reshape(1, 32))


def _pack_ssd_small(cw, cb, nw):
    def body(cw_ref, cb_ref, nw_ref, o_ref):
        o_ref[...] = jnp.zeros_like(o_ref)
        o_ref[0:5, :] = cw_ref[...]
        o_ref[5:6, :] = cb_ref[...]
        o_ref[6:7, 0:256] = nw_ref[...]

    return pl.pallas_call(body, name="pack_ssd_small", out_shape=jax.ShapeDtypeStruct((8, 512), F32))(cw, cb, nw)


def _adamw(name, w, parts, m, v, tr, tc=None):
    r_, c_ = w.shape
    p_ = parts.shape[0]
    tr = min(tr, r_)
    tc = c_ if tc is None else tc
    assert r_ % tr == 0 and c_ % tc == 0

    def body(w_ref, p_ref, m_ref, v_ref, g_ref, d_ref, m2_ref, v2_ref):
        g = p_ref[0].astype(F32)
        for s in range(1, p_):
            g = g + p_ref[s].astype(F32)
        m2 = ADAM_B1 * m_ref[...] + (1.0 - ADAM_B1) * g
        v2 = ADAM_B2 * v_ref[...] + (1.0 - ADAM_B2) * (g * g)
        m_hat = m2 / (1.0 - ADAM_B1 ** ADAM_STEP)
        v_hat = v2 / (1.0 - ADAM_B2 ** ADAM_STEP)
        g_ref[...] = g
        d_ref[...] = -ADAM_LR * (m_hat / (jnp.sqrt(v_hat) + ADAM_EPS) + ADAM_WD * w_ref[...])
        m2_ref[...] = m2
        v2_ref[...] = v2

    blk = pl.BlockSpec((tr, tc), lambda i, j: (i, j))
    return pl.pallas_call(
        body, name=name, grid=(r_ // tr, c_ // tc), out_shape=[jax.ShapeDtypeStruct((r_, c_), F32)] * 4,
        in_specs=[blk, pl.BlockSpec((p_, tr, tc), lambda i, j: (0, i, j)), blk, blk], out_specs=[blk] * 4,
        compiler_params=_params(("parallel", "parallel"), VMEM_BIG),
    )(w, parts, m, v)


def _dev_index(p):
    return 4 * p[0] + 2 * p[1] + p[2]


def _all_gather(name, xs):
    n = len(xs)
    hbm = pl.BlockSpec(memory_space=pl.ANY)

    def body(*refs):
        x_refs, o_refs = refs[:n], refs[n:2 * n]
        send_sems, recv_sems, local_sems = refs[2 * n:]
        x, y, c = lax.axis_index("x"), lax.axis_index("y"), lax.axis_index("c")
        me, sibling = (x, y, c), (x, y, 1 - c)
        chips = [(1 - x, y), (x, 1 - y), (1 - x, 1 - y)]

        def copy(a, k, block, to, src=None):
            dst = o_refs[a].at[_dev_index(block)]
            return pltpu.make_async_remote_copy(
                src_ref=dst if src is None else src, dst_ref=dst, send_sem=send_sems.at[a, k],
                recv_sem=recv_sems.at[a, k], device_id=to, device_id_type=MESH)

        mine = [pltpu.make_async_copy(x_refs[a], o_refs[a].at[_dev_index(me)], local_sems.at[a]) for a in range(n)]
        for cp in mine:
            cp.start()
        first = []
        for a in range(n):
            first.append(copy(a, 0, me, sibling, src=x_refs[a]))
            first += [copy(a, 1 + j, me, (*chip, c), src=x_refs[a]) for j, chip in enumerate(chips)]
        for cp in first:
            cp.start()
        passed = []
        for j, chip in enumerate(chips):
            for a in range(n):
                copy(a, 1 + j, (*chip, c), me).wait_recv()
                cp = copy(a, 4 + j, (*chip, c), sibling)
                cp.start()
                passed.append(cp)
        for a in range(n):
            copy(a, 0, sibling, me).wait_recv()
            for j, chip in enumerate(chips):
                copy(a, 4 + j, (*chip, 1 - c), me).wait_recv()
        for cp in first + passed:
            cp.wait_send()
        for cp in mine:
            cp.wait()

    return pl.pallas_call(
        body, name=name, out_shape=[jax.ShapeDtypeStruct((NDEV, *x.shape), x.dtype) for x in xs],
        in_specs=[hbm] * n, out_specs=[hbm] * n,
        scratch_shapes=[pltpu.SemaphoreType.DMA((n, 7)), pltpu.SemaphoreType.DMA((n, 7)), pltpu.SemaphoreType.DMA((n,))],
    )(*xs)


_HBM = pl.BlockSpec(memory_space=pltpu.HBM)
_SEM = pl.BlockSpec(memory_space=pltpu.SEMAPHORE)
_EFFECT = pltpu.SideEffectType.DATAFLOW_SIDE_EFFECTING


def _mesh_position():
    return lax.axis_index("x"), lax.axis_index("y"), lax.axis_index("c")


def _peers(me):
    return [(k, tuple(1 - v if (k >> b) & 1 else v for v, b in zip(me, (2, 1, 0)))) for k in range(1, NDEV)]


def _landing_zones(name, xs, scatter):
    me = _dev_index(_mesh_position()).astype(jnp.int32).reshape(1)
    lands = []
    for a, x in enumerate(xs):
        rows, cols = x.shape[-2:]
        tr = 256 if rows % 256 == 0 else rows

        def body(me_ref, x_ref, o_ref):
            o_ref[...] = x_ref[...]

        if scatter:
            in_spec = pl.BlockSpec((None, tr, cols), lambda i, me_ref: (me_ref[0], i, 0))
        else:
            in_spec = pl.BlockSpec((tr, cols), lambda i, me_ref: (i, 0))
        lands.append(pl.pallas_call(
            body, name=f"{name}_{a}", out_shape=jax.ShapeDtypeStruct((NDEV, rows, cols), x.dtype),
            grid_spec=pltpu.PrefetchScalarGridSpec(
                num_scalar_prefetch=1, grid=(rows // tr,), in_specs=[in_spec],
                out_specs=pl.BlockSpec((None, tr, cols), lambda i, me_ref: (me_ref[0], i, 0))),
            compiler_params=_params(("arbitrary",)),
        )(me, x))
    return lands


def _exchange_copies(x_refs, land_refs, send_sems, recv_sems, scatter):
    me = _mesh_position()
    out = []
    for k, peer in _peers(me):
        for a, (x_ref, land_ref) in enumerate(zip(x_refs, land_refs)):
            sem = a * (NDEV - 1) + k - 1
            out.append(pltpu.make_async_remote_copy(
                src_ref=x_ref.at[_dev_index(peer)] if scatter else x_ref, dst_ref=land_ref.at[_dev_index(me)],
                send_sem=send_sems.at[sem], recv_sem=recv_sems.at[sem], device_id=peer, device_id_type=MESH))
    return out


def _exchange_start(name, xs, lands, scatter, dep):
    n = len(xs)

    def body(*refs):
        x_refs, land_refs = refs[:n], refs[n:2 * n]
        send_sems, recv_sems = refs[2 * n + 1], refs[2 * n + 2]
        token = refs[-1]
        for cp in _exchange_copies(x_refs, land_refs, send_sems, recv_sems, scatter):
            cp.start()
        token[...] = jnp.zeros_like(token)

    sems = pltpu.SemaphoreType.DMA((n * (NDEV - 1),))
    res = pl.pallas_call(
        body, name=name,
        out_shape=(sems, sems, *[pltpu.HBM(a.shape, a.dtype) for a in (*xs, *lands)], jax.ShapeDtypeStruct((8, 128), F32)),
        in_specs=[_HBM] * (2 * n) + [pl.BlockSpec(memory_space=pl.ANY)],
        out_specs=(_SEM, _SEM, *[_HBM] * (2 * n), pl.BlockSpec(memory_space=pltpu.VMEM)),
        input_output_aliases={i: 2 + i for i in range(2 * n)},
        compiler_params=pltpu.CompilerParams(has_side_effects=_EFFECT),
    )(*[pltpu.with_memory_space_constraint(a, pltpu.HBM) for a in (*xs, *lands)], dep)
    return res[:-1], res[-1]


def _exchange_wait(name, handles, scatter, after):
    send_sems, recv_sems = handles[0], handles[1]
    bufs = handles[2:]
    n = len(bufs) // 2

    def body(*refs):
        x_refs, land_refs = refs[:n], refs[n:2 * n]
        s_sems, r_sems = refs[2 * n], refs[2 * n + 1]
        for cp in _exchange_copies(x_refs, land_refs, s_sems, r_sems, scatter):
            cp.wait_send()
            cp.wait_recv()

    res = pl.pallas_call(
        body, name=name, out_shape=tuple(pltpu.HBM(a.shape, a.dtype) for a in bufs),
        in_specs=[_HBM] * (2 * n) + [_SEM, _SEM, pl.BlockSpec(memory_space=pl.ANY)], out_specs=tuple([_HBM] * (2 * n)),
        input_output_aliases={i: i for i in range(2 * n)},
        compiler_params=pltpu.CompilerParams(has_side_effects=_EFFECT),
    )(*bufs, send_sems, recv_sems, after)
    return res[n:]


def kernel(x, c, positions, norm_w, mod_w, mod_b, attn_w_in, attn_w_out, ssd_w_in, ssd_conv_w, ssd_conv_b, ssd_dt_bias, ssd_a_log, ssd_d, ssd_norm_w, ssd_w_out, final_norm_w, loss_target, m_norm_w, m_mod_w, m_mod_b, m_attn_w_in, m_attn_w_out, m_ssd_w_in, m_ssd_conv_w, m_ssd_conv_b, m_ssd_dt_bias, m_ssd_a_log, m_ssd_d, m_ssd_norm_w, m_ssd_w_out, m_final_norm_w, v_norm_w, v_mod_w, v_mod_b, v_attn_w_in, v_attn_w_out, v_ssd_w_in, v_ssd_conv_w, v_ssd_conv_b, v_ssd_dt_bias, v_ssd_a_log, v_ssd_d, v_ssd_norm_w, v_ssd_w_out, v_final_norm_w):
    s_len, dm = x.shape[1], x.shape[2]
    me = 4 * lax.axis_index("x") + 2 * lax.axis_index("y") + lax.axis_index("c")
    x0 = x.reshape(s_len, dm)
    tgt = loss_target.reshape(s_len, dm)
    aw = 3 * 512
    si = 2 * dm
    sxbc = 2 * si
    n_ssd_in = ssd_w_in.shape[2] * NDEV

    g_ai, c_all = _all_gather("gather_attn_w_in", [attn_w_in[0].astype(BF16), c])
    w_ai = g_ai.transpose(1, 0, 2).reshape(dm, 4 * aw)
    c_all = c_all.reshape(NDEV, dm)
    ssd_small = _pack_ssd_small(ssd_conv_w[0], ssd_conv_b, ssd_norm_w)
    ao_shard = [attn_w_out[0].astype(BF16)]
    ao_handles, ao_token = _exchange_start("w_out_start", ao_shard, _landing_zones("w_out_place", ao_shard, False), False, g_ai)
    late_shards = [ssd_w_in[0].T.astype(BF16), ssd_w_out[0].astype(BF16), ssd_small]
    w_handles, w_token = _exchange_start("weights_start", late_shards, _landing_zones("weights_place", late_shards, False),
                                         False, ao_token)

    part = _mod_part(c_all, mod_w)
    (part_all,) = _all_gather("gather_mod", [part])
    mod_nb = jnp.stack([lax.dynamic_index_in_dim(part_all, i * NDEV + me, axis=1, keepdims=False).reshape(3 * dm)
                        for i in range(2)])
    mod = _mod_finish(mod_nb, mod_b)
    shift = [mod[i:i + 1, 0:dm] for i in range(2)]
    scale = [mod[i:i + 1, dm:2 * dm] for i in range(2)]
    gate = [mod[i:i + 1, 2 * dm:3 * dm] for i in range(2)]
    nw = [norm_w[i:i + 1] + w_token[0:1, 0:1] for i in range(2)]

    hn0 = _norm_mod_fwd("norm0", x0, nw[0], scale[0], shift[0])
    inv_freq = ROPE_THETA ** (-jnp.arange(0, ROT_DIM, 2, dtype=F32) / ROT_DIM)
    lane = jnp.arange(128) % HEAD_DIM
    inv_row = jnp.where(lane < ROT_DIM, inv_freq[lane % (ROT_DIM // 2)], 0.0).reshape(1, 128).astype(F32)
    tabs = _rope_tables(positions.reshape(s_len, 1), inv_row)
    qk = _matmul("proj_qk", hn0, w_ai, "nn", F32, MM_T, MM_T, dm, epilogue=_rot_fwd, mrows=tabs, n_out=2 * aw)
    v = _matmul("proj_v", hn0, w_ai, "nn", F32, MM_T, aw // 2, dm, b_noff=2 * aw, n_out=aw)
    z0 = _matmul("proj_z", hn0, w_ai, "nn", F32, MM_T, aw // 2, dm, b_noff=3 * aw, n_out=aw)
    att = [_attn_fwd(g, qk, v) for g in range(3)]
    os_, lses = [a[0] for a in att], [a[1] for a in att]
    (g_ao,) = _exchange_wait("w_out_wait", ao_handles, False, lses[2])
    a0, y0, x1 = _attn_out(os_, lses, z0, x0, gate[0], g_ao.reshape(aw, dm))

    hn1 = _norm_mod_fwd("norm1", x1, nw[1], scale[1], shift[1])
    g_si, g_so, g_small = _exchange_wait("weights_wait", w_handles, False, hn1)
    w_ao = g_ao.reshape(aw, dm)
    w_si_t = g_si.reshape(n_ssd_in, dm)
    w_so = g_so.reshape(si, dm)
    conv_w = g_small[:, 0:CONV_WIDTH, :].transpose(1, 0, 2).reshape(CONV_WIDTH, sxbc)
    conv_b = g_small[:, 5, :].reshape(1, sxbc)
    snw = g_small[:, 6, 0:si // NDEV].reshape(1, si)
    ndt = 2 * SSD_HEADS
    z1 = _matmul("ssd_proj_z", hn1, w_si_t, "nt", F32, MM_T, MM_T, dm, n_out=si)
    xpre = _matmul("ssd_proj_xbc", hn1, w_si_t, "nt", F32, MM_T, MM_T, dm, b_noff=si, n_out=sxbc)
    dt_raw = _matmul("ssd_proj_dt", hn1, w_si_t, "nt", F32, MM_T, ndt, dm, b_noff=si + sxbc, n_out=ndt)
    xbc = _conv_fwd(xpre, conv_w, conv_b)
    dt_bias = ssd_dt_bias.reshape(1, 2 * SSD_HEADS)
    alog = ssd_a_log.reshape(1, 2 * SSD_HEADS)
    dt = _softplus_fwd(dt_raw, dt_bias)
    y_f, st_f = _ssd_fwd(xbc, dt, alog, 0)
    y_b, st_b = _ssd_fwd(xbc, dt, alog, 1)
    d_e = jnp.repeat(ssd_d.reshape(SSD_HEADS), HEAD_DIM).reshape(1, si)
    u = _gate_norm_fwd(y_f, y_b, xbc, z1, d_e, snw)
    y1 = _matmul("ssd_out", u, w_so, "nn", F32, MM_T, MM_T, si)

    fnw = final_norm_w.reshape(1, dm)
    dx2, dy1, g_fnw, dgate1, loss_part = _loss_bwd(x1, y1, tgt, gate[1], fnw)
    du = _matmul("ssd_out_dx", dy1, w_so, "nt", F32, MM_T, MM_T, dm)
    gw_so = _matmul("ssd_out_dw", u, dy1, "tn", BF16, MM_T, MM_T, MM_T)
    dys, dz1, g_snw, g_d = _gate_norm_bwd(du, y_f, y_b, xbc, z1, d_e, snw)
    dxbc_f, ddt_f, dalog_f = _ssd_bwd(xbc, dt, alog, st_f, dys, d_e, 0)
    dxbc_b, ddt_b, dalog_b = _ssd_bwd(xbc, dt, alog, st_b, dys, d_e, 1)
    dpre, g_cw, g_cb = _conv_bwd(xpre, dxbc_f, dxbc_b, conv_w, conv_b)
    ddt_raw, g_dtb = _softplus_bwd(ddt_f, ddt_b, dt_raw, dt_bias)
    dhn1 = [_matmul("ssd_proj_z_dx", dz1, w_si_t, "nn", F32, MM_T, MM_T, MM_T),
            _matmul("ssd_proj_xbc_dx", dpre, w_si_t, "nn", F32, MM_T, MM_T, MM_T, b_koff=si),
            _matmul("ssd_proj_dt_dx", ddt_raw, w_si_t, "nn", F32, MM_T, MM_T, ndt, b_koff=si + sxbc)]
    gw_si_t = _matmul("ssd_proj_z_dw", dz1, hn1, "tn", BF16, MM_T, MM_T, MM_T, dest=(n_ssd_in, 0, None))
    gw_si_t = _matmul("ssd_proj_xbc_dw", dpre, hn1, "tn", BF16, MM_T, MM_T, MM_T, dest=(n_ssd_in, si, gw_si_t))
    gw_si_t = _matmul("ssd_proj_dt_dw", ddt_raw, hn1, "tn", BF16, ndt, MM_T, MM_T, dest=(n_ssd_in, si + sxbc, gw_si_t))
    dx1, g_nw1, dsc1, dsh1 = _norm_mod_bwd("norm1_bwd", x1, dhn1, dx2, nw[1], scale[1], shift[1])

    l1_grads = [gw_so.reshape(NDEV, si // NDEV, dm), gw_si_t.reshape(NDEV, n_ssd_in // NDEV, dm),
                _pack_ssd_small_blocks(g_cw, g_cb, g_snw)]
    l1_handles, l1_token = _exchange_start("l1_grads_start", l1_grads, _landing_zones("l1_grads_place", l1_grads, True),
                                           True, dx1)

    dy0, dgate0 = _gate_bwd(dx1, y0, gate[0] + l1_token[0:1, 0:1])
    da0 = _matmul("attn_out_dx", dy0, w_ao, "nt", F32, MM_T, aw // 2, dm)
    gw_ao = _matmul("attn_out_dw", a0, dy0, "tn", BF16, aw // 2, MM_T, MM_T)
    dos, dls, dz0 = _mix_bwd(da0, os_, lses, z0)
    datt = [_attn_bwd(g, qk, v, os_[g], lses[g], dos[g], dls[g]) for g in range(3)]
    dqkv = _rot_pack_bwd([t[0] for t in datt], [t[1] for t in datt], [t[2] for t in datt], tabs)
    wcol = attn_w_in.shape[2]
    gw_ai = _matmul("proj_qkv_dw", hn0, dqkv, "tn", BF16, MM_T, wcol, MM_T, out_blocks=3 * aw // wcol, dest=(NDEV, 0, None))
    gw_ai = _matmul("proj_z_dw", hn0, dz0, "tn", BF16, MM_T, wcol, MM_T, out_blocks=aw // wcol,
                    dest=(NDEV, 3 * aw // wcol, gw_ai))
    l0_grads = [gw_ai, gw_ao.reshape(NDEV, aw // NDEV, dm)]
    l0_handles, l0_token = _exchange_start("l0_grads_start", l0_grads, _landing_zones("l0_grads_place", l0_grads, True),
                                           True, dqkv)
    zero_row = jnp.tile(l0_token[0:1], (1, dm // 128))
    after_start = lambda acc, t: acc + t
    dhn0 = [_matmul("proj_qkv_dx", dqkv, w_ai, "nt", F32, MM_T, MM_T, aw, n_out=dm, epilogue=after_start, ncols=(zero_row,)),
            _matmul("proj_z_dx", dz0, w_ai, "nt", F32, MM_T, MM_T, aw, b_koff=3 * aw, n_out=dm)]
    dx0, g_nw0, dsc0, dsh0 = _norm_mod_bwd("norm0_bwd", x0, dhn0, dx1, nw[0], scale[0], shift[0])

    rows6 = jnp.concatenate([dsh0, dsc0, dgate0, dsh1, dsc1, dgate1], axis=0)
    small_g = _pack_small(rows6, jnp.concatenate([g_nw0, g_nw1], axis=0), g_fnw, g_dtb, dalog_f + dalog_b, g_d, loss_part)
    (small_all,) = _all_gather("gather_small_grads", [small_g])
    small_w = _pack_small(mod_b.reshape(6, dm), norm_w, fnw, dt_bias, alog, ssd_d)
    small_m = _pack_small(m_mod_b.reshape(6, dm), m_norm_w, m_final_norm_w.reshape(1, dm), m_ssd_dt_bias.reshape(1, 64),
                          m_ssd_a_log.reshape(1, 64), m_ssd_d)
    small_v = _pack_small(v_mod_b.reshape(6, dm), v_norm_w, v_final_norm_w.reshape(1, dm), v_ssd_dt_bias.reshape(1, 64),
                          v_ssd_a_log.reshape(1, 64), v_ssd_d)
    small_out = _adamw("adamw_small", small_w, small_all, small_m, small_v, PACK_ROWS)
    loss = small_out[0][9, 256]
    sg, sd, sm, sv = (_unpack_small(p) for p in small_out)

    ncol = mod_w.shape[2]
    dmod_all = small_all[:, 0:6, :].reshape(NDEV, 2, 3 * dm)
    dmod_sh = lax.dynamic_slice_in_dim(dmod_all, me * ncol, ncol, axis=2).transpose(1, 0, 2)
    g_modw = _mod_grad(c_all, dmod_sh).reshape(1, 2 * dm, ncol)
    modw_out = _adamw("adamw_mod_w", mod_w.reshape(2 * dm, ncol), g_modw, m_mod_w.reshape(2 * dm, ncol),
                      v_mod_w.reshape(2 * dm, ncol), 256)

    r_so, r_si, r_small = _exchange_wait("l1_grads_wait", l1_handles, True, modw_out[0])
    si_out = [o.T for o in _adamw("adamw_ssd_w_in", ssd_w_in[0].T, r_si, m_ssd_w_in[0].T, v_ssd_w_in[0].T, n_ssd_in // NDEV, 256)]
    so_out = _adamw("adamw_ssd_w_out", ssd_w_out[0], r_so, m_ssd_w_out[0], v_ssd_w_out[0], 256)
    ssd_small_m = _pack_ssd_small(m_ssd_conv_w[0], m_ssd_conv_b, m_ssd_norm_w)
    ssd_small_v = _pack_ssd_small(v_ssd_conv_w[0], v_ssd_conv_b, v_ssd_norm_w)
    ss_out = _adamw("adamw_ssd_small", ssd_small, r_small, ssd_small_m, ssd_small_v, 8)
    r_ai, r_ao = _exchange_wait("l0_grads_wait", l0_handles, True, so_out[0])
    ai_out = _adamw("adamw_attn_w_in", attn_w_in[0], r_ai, m_attn_w_in[0], v_attn_w_in[0], 256)
    ao_out = _adamw("adamw_attn_w_out", attn_w_out[0], r_ao, m_attn_w_out[0], v_attn_w_out[0], 192)

    def ssd_small_unpack(p):
        return p[0:5][None], p[5:6], p[6:7, 0:si // NDEV]

    cwo, cbo, nwo = zip(*(ssd_small_unpack(p) for p in ss_out))
    per_kind = []
    for k in range(4):
        s = (sg, sd, sm, sv)[k]
        per_kind.append([
            s[1], modw_out[k].reshape(mod_w.shape), s[0], ai_out[k][None], ao_out[k][None], si_out[k][None],
            cwo[k], cbo[k], s[3], s[4], s[5], nwo[k], so_out[k][None], s[2]])
    return (loss, dx0.reshape(x.shape), *per_kind[0], *per_kind[1], *per_kind[2], *per_kind[3])


def _pack_ssd_small_blocks(g_cw, g_cb, g_nw):
    nper = g_cw.shape[1] // NDEV
    nwper = g_nw.shape[1] // NDEV

    def body(cw_ref, cb_ref, nw_ref, o_ref):
        o_ref[...] = jnp.zeros_like(o_ref)
        for d in range(NDEV):
            o_ref[d, 0:5, :] = cw_ref[:, d * nper:(d + 1) * nper]
            o_ref[d, 5:6, :] = cb_ref[:, d * nper:(d + 1) * nper]
            o_ref[d, 6:7, 0:nwper] = nw_ref[:, d * nwper:(d + 1) * nwper]

    return pl.pallas_call(body, name="pack_ssd_small_grads", out_shape=jax.ShapeDtypeStruct((NDEV, 8, nper), F32))(g_cw, g_cb, g_nw)
```

```python
import functools
import math

import jax
import jax.numpy as jnp
from jax import lax
from jax.experimental import pallas as pl
from jax.experimental.pallas import tpu as pltpu

F32 = jnp.float32
BF16 = jnp.bfloat16
HI = lax.Precision.HIGHEST
MESH = pl.DeviceIdType.MESH
NDEV = 8

NORM_EPS = 1e-6
ROPE_THETA = 500000.0
ROT_DIM = 16
HEAD_DIM = 64
DILATIONS = (1, 4, 16)
BAND = 64
NEG_BIG = -1e30
CHUNK = 128
SSD_HEADS = 32
SSD_GROUPS = 8
CONV_WIDTH = 5

ADAM_LR = 0.001
ADAM_B1 = 0.9
ADAM_B2 = 0.999
ADAM_EPS = 1e-08
ADAM_WD = 0.01
ADAM_STEP = 10

VMEM_BIG = 56 * 1024 * 1024
MM_T = 1024


def _params(sem=None, vmem=None):
    kw = {}
    if sem is not None:
        kw["dimension_semantics"] = sem
    if vmem is not None:
        kw["vmem_limit_bytes"] = vmem
    return pltpu.CompilerParams(**kw)


def _dg(a, b, ca, cb, prec=None):
    return lax.dot_general(a, b, (((ca,), (cb,)), ((), ())), preferred_element_type=F32, precision=prec)


def _nn(a, b):
    return _dg(a.astype(BF16), b.astype(BF16), 1, 0)


def _nt(a, b):
    return _dg(a.astype(BF16), b.astype(BF16), 1, 1)


def _tn(a, b):
    return _dg(a.astype(BF16), b.astype(BF16), 0, 0)


def _hnn(a, b):
    return _dg(a, b, 1, 0, HI)


@jax.custom_vjp
def _bnn(a, b):
    return _nn(a, b)


_bnn.defvjp(lambda a, b: (_nn(a, b), (a, b)), lambda r, g: (_nt(g, r[1]), _tn(r[0], g)))


@jax.custom_vjp
def _bnt(a, b):
    return _nt(a, b)


_bnt.defvjp(lambda a, b: (_nt(a, b), (a, b)), lambda r, g: (_nn(g, r[1]), _tn(g, r[0])))


@jax.custom_vjp
def _btn(a, b):
    return _tn(a, b)


_btn.defvjp(lambda a, b: (_tn(a, b), (a, b)), lambda r, g: (_nt(r[1], g), _nn(r[0], g)))


def _silu(x):
    return x * jax.nn.sigmoid(x)


def _matmul(name, a, b, mode, out_dtype, tm, tn, tk, *, epilogue=None, tiled=(), mrows=(), ncols=(),
            b_noff=0, b_koff=0, n_out=None, out_blocks=None, dest=None):
    if mode == "tn":
        K, M = a.shape
    else:
        M, K = a.shape
    N = n_out if n_out is not None else (b.shape[0] if mode == "nt" else b.shape[1])
    tm, tn, tk = min(tm, M), min(tn, N), min(tk, K)
    assert M % tm == 0 and N % tn == 0 and K % tk == 0, (name, M, N, K, tm, tn, tk)
    assert b_noff % tn == 0 and b_koff % tk == 0
    no, ko = b_noff // tn, b_koff // tk
    nk = K // tk
    if mode == "tn":
        a_spec = pl.BlockSpec((tk, tm), lambda i, j, k: (k, i))
    else:
        a_spec = pl.BlockSpec((tm, tk), lambda i, j, k: (i, k))
    if mode == "nt":
        b_spec = pl.BlockSpec((tn, tk), lambda i, j, k: (j + no, k + ko))
    else:
        b_spec = pl.BlockSpec((tk, tn), lambda i, j, k: (k + ko, j + no))
    specs = [a_spec, b_spec]
    specs += [pl.BlockSpec((tm, tn), lambda i, j, k: (i, j)) for _ in tiled]
    specs += [pl.BlockSpec((tm, r.shape[1]), lambda i, j, k: (i, 0)) for r in mrows]
    specs += [pl.BlockSpec((1, tn), lambda i, j, k: (0, j)) for _ in ncols]
    total, off, earlier = dest if dest is not None else (None, 0, None)
    if out_blocks is None:
        assert off % tm == 0
        mo = off // tm
        out_shape = jax.ShapeDtypeStruct((M if total is None else total, N), out_dtype)
        out_spec = pl.BlockSpec((tm, tn), lambda i, j, k: (i + mo, j))
    else:
        nper = N // out_blocks
        assert nper % tn == 0
        jb = nper // tn
        out_shape = jax.ShapeDtypeStruct((out_blocks if total is None else total, M, nper), out_dtype)
        out_spec = pl.BlockSpec((None, tm, tn), lambda i, j, k: (j // jb + off, i, j % jb))
    if earlier is not None:
        assert earlier.shape == out_shape.shape and earlier.dtype == out_shape.dtype
    ne = len(tiled) + len(mrows) + len(ncols)
    dot = {"nn": _nn, "nt": _nt, "tn": _tn}[mode]

    def body(a_ref, b_ref, *rest):
        extras, o_ref = rest[:ne], rest[ne]

        def finish(acc):
            if epilogue is not None:
                acc = epilogue(acc, *[e[...] for e in extras])
            o_ref[...] = acc.astype(o_ref.dtype)

        if nk == 1:
            finish(dot(a_ref[...], b_ref[...]))
        else:
            acc_ref = rest[ne + 1]
            k = pl.program_id(2)

            @pl.when(k == 0)
            def _():
                acc_ref[...] = jnp.zeros_like(acc_ref)

            acc_ref[...] += dot(a_ref[...], b_ref[...])

            @pl.when(k == nk - 1)
            def _():
                finish(acc_ref[...])

    args = [a, b, *tiled, *mrows, *ncols]
    aliases = {}
    if earlier is not None:
        specs.append(pl.BlockSpec(memory_space=pl.ANY))
        aliases = {len(args): 0}
        args.append(earlier)

    def body_with_dest(*refs):
        body(*refs[:2 + ne], *refs[2 + ne + (earlier is not None):])

    return pl.pallas_call(
        body_with_dest, name=name, out_shape=out_shape, grid=(M // tm, N // tn, nk),
        in_specs=specs, out_specs=out_spec, input_output_aliases=aliases,
        scratch_shapes=[] if nk == 1 else [pltpu.VMEM((tm, tn), F32)],
        compiler_params=_params(("parallel", "parallel", "arbitrary"), VMEM_BIG),
    )(*args)


def _rowwise(name, fn, tiled, consts, outs, accs, ts):
    tl = [(t, t.shape[1], 0) if not isinstance(t, tuple) else t for t in tiled]
    s_len = tl[0][0].shape[0]
    assert s_len % ts == 0
    nt_, nc_, no_ = len(tl), len(consts), len(outs)

    def body(*refs):
        t_refs, c_refs = refs[:nt_], refs[nt_:nt_ + nc_]
        o_refs, a_refs = refs[nt_ + nc_:nt_ + nc_ + no_], refs[nt_ + nc_ + no_:]
        res_o, res_a = fn(*[r[...] for r in t_refs], *[r[...] for r in c_refs])
        for r, v in zip(o_refs, res_o, strict=True):
            r[...] = v.astype(r.dtype)
        if a_refs:
            @pl.when(pl.program_id(0) == 0)
            def _():
                for r in a_refs:
                    r[...] = jnp.zeros_like(r)

            for r, v in zip(a_refs, res_a, strict=True):
                r[...] += v

    in_specs = [pl.BlockSpec((ts, w), functools.partial(lambda i, cb: (i, cb), cb=cb)) for (_, w, cb) in tl]
    in_specs += [pl.BlockSpec(c.shape, lambda i: (0, 0)) for c in consts]
    out_specs = [pl.BlockSpec((ts, c), lambda i: (i, 0)) for (c, _) in outs]
    out_specs += [pl.BlockSpec(shp, lambda i: (0, 0)) for shp in accs]
    out_shape = [jax.ShapeDtypeStruct((s_len, c), dt) for (c, dt) in outs]
    out_shape += [jax.ShapeDtypeStruct(shp, F32) for shp in accs]
    res = pl.pallas_call(
        body, name=name, out_shape=out_shape, grid=(s_len // ts,), in_specs=in_specs, out_specs=out_specs,
        compiler_params=_params(("arbitrary",) if accs else ("parallel",), VMEM_BIG),
    )(*[t[0] for t in tl], *consts)
    return res[:no_], res[no_:]


def _norm_mod_fn(x, nw, sc, sh):
    r = lax.rsqrt(jnp.mean(x * x, axis=-1, keepdims=True) + NORM_EPS)
    return (x * r * nw) * (1.0 + sc) + sh


def _norm_mod_fwd(name, x, nw, sc, sh):
    (hn,), _ = _rowwise(name, lambda x, nw, sc, sh: ([_norm_mod_fn(x, nw, sc, sh)], []),
                        [x], [nw, sc, sh], [(x.shape[1], BF16)], [], 512)
    return hn


def _norm_mod_bwd(name, x, dhn_parts, dres, nw, sc, sh):
    n = len(dhn_parts)
    d = x.shape[1]

    def fn(x, *rest):
        dhn = rest[0]
        for p in rest[1:n]:
            dhn = dhn + p
        dres, nw, sc, sh = rest[n:]
        _, vjp = jax.vjp(_norm_mod_fn, x, nw, sc, sh)
        dx, dnw, dsc, dsh = vjp(dhn)
        return [dx + dres], [dnw, dsc, dsh]

    (dx,), (g_nw, dsc, dsh) = _rowwise(name, fn, [x, *dhn_parts, dres], [nw, sc, sh], [(d, F32)],
                                       [(1, d), (1, d), (1, d)], 256)
    return dx, g_nw, dsc, dsh


def _rope_tables(pos_col, inv_row):
    def fn(pos, inv):
        ang = pos.astype(F32) * inv
        e = lax.broadcasted_iota(jnp.int32, (1, 128), 1) % HEAD_DIM
        cos, sin = jnp.cos(ang), jnp.sin(ang)
        half = ROT_DIM // 2
        return [jnp.where(e < ROT_DIM, cos, 1.0), jnp.where(e < half, -sin, 0.0),
                jnp.where((e >= half) & (e < ROT_DIM), sin, 0.0)], []

    (c, sa, sb), _ = _rowwise("rope_tables", fn, [pos_col], [inv_row], [(128, F32)] * 3, [], 512)
    return c, sa, sb


def _rot_fwd(t, c, sa, sb):
    n = t.shape[1]
    rep = n // 128
    c, sa, sb = (jnp.tile(u, (1, rep)) for u in (c, sa, sb))
    return t * c + pltpu.roll(t, n - ROT_DIM // 2, 1) * sa + pltpu.roll(t, ROT_DIM // 2, 1) * sb


def _rot_bwd(g, c, sa, sb):
    n = g.shape[1]
    rep = n // 128
    c, sa, sb = (jnp.tile(u, (1, rep)) for u in (c, sa, sb))
    return g * c + pltpu.roll(g * sa, ROT_DIM // 2, 1) + pltpu.roll(g * sb, n - ROT_DIM // 2, 1)


ATT_TQ = 128
ATT_TK = ATT_TQ + 2 * BAND


def _attn_specs(g, s_len):
    def blk(off):
        return pl.BlockSpec((s_len, 128), functools.partial(lambda hp, off: (0, off + hp), off=off))

    return blk(4 * g), blk(12 + 4 * g), blk(4 * g), blk(0)


def _attn_tile_geometry(t, d, l):
    nts = l // ATT_TQ
    r = t // nts
    ts = t % nts
    q0 = ts * ATT_TQ
    ws = jnp.clip(q0 - BAND, 0, l - ATT_TK)
    kind = jnp.where(ts == 0, 0, jnp.where(ts == nts - 1, 2, 1))
    if d == 1:
        return pl.ds(pl.multiple_of(q0, ATT_TQ), ATT_TQ), pl.ds(pl.multiple_of(ws, BAND), ATT_TK), kind
    return pl.ds(r + d * q0, ATT_TQ, stride=d), pl.ds(r + d * ws, ATT_TK, stride=d), kind


def _attn_fill_bias(bias_ref):
    iq = lax.broadcasted_iota(jnp.int32, (2 * ATT_TQ, 1), 0) % ATT_TQ
    ik = lax.broadcasted_iota(jnp.int32, (1, ATT_TK), 1)
    for i, off in enumerate((0, -BAND, -2 * BAND)):
        bias_ref[i] = jnp.where(jnp.abs(ik + off - iq) <= BAND, 0.0, NEG_BIG)


def _split_heads(t, in_h):
    zero = jnp.zeros_like(t)
    return jnp.concatenate([jnp.where(in_h[0], t, zero), jnp.where(in_h[1], t, zero)], axis=0)


def _attn_fwd(g, qk, v):
    s_len = qk.shape[0]
    d = DILATIONS[g]
    l = s_len // d
    assert l % ATT_TQ == 0 and l >= ATT_TK
    q_spec, k_spec, v_spec, o_spec = _attn_specs(g, s_len)
    scale = 1.0 / math.sqrt(HEAD_DIM)

    def body(q_ref, k_ref, v_ref, o_ref, lse_ref, bias_ref):
        lane = lax.broadcasted_iota(jnp.int32, (1, 128), 1)
        in_h = [lane < HEAD_DIM, lane >= HEAD_DIM]
        _attn_fill_bias(bias_ref)

        def tile(t, carry):
            rows, win, kind = _attn_tile_geometry(t, d, l)
            q = (q_ref[rows, :] * scale).astype(BF16)
            k = k_ref[win, :].astype(BF16)
            vv = v_ref[win, :].astype(BF16)
            s = _nt(_split_heads(q, in_h), k) + bias_ref[kind]
            m = jnp.max(s, axis=1, keepdims=True)
            p = jnp.exp(s - m)
            den = jnp.sum(p, axis=1, keepdims=True)
            out = _nn(p, vv) / den
            lse = m + jnp.log(den)
            o_ref[rows, :] = jnp.where(in_h[0], out[:ATT_TQ], out[ATT_TQ:])
            lse_ref[rows, :] = jnp.where(in_h[0], lse[:ATT_TQ], lse[ATT_TQ:])
            return carry

        lax.fori_loop(0, s_len // ATT_TQ, tile, 0, unroll=4)

    return pl.pallas_call(
        body, name=f"attn_fwd_g{g}", grid=(4,),
        out_shape=[jax.ShapeDtypeStruct((s_len, 512), F32)] * 2,
        in_specs=[q_spec, k_spec, v_spec], out_specs=[o_spec, o_spec],
        scratch_shapes=[pltpu.VMEM((3, 2 * ATT_TQ, ATT_TK), F32)],
        compiler_params=_params(("parallel",), VMEM_BIG),
    )(qk, qk, v)


def _attn_bwd(g, qk, v, o, lse, do, dlse):
    s_len = qk.shape[0]
    d = DILATIONS[g]
    l = s_len // d
    q_spec, k_spec, v_spec, o_spec = _attn_specs(g, s_len)
    scale = 1.0 / math.sqrt(HEAD_DIM)

    def body(q_ref, k_ref, v_ref, o_ref, lse_ref, do_ref, dlse_ref, dq_ref, dk_ref, dv_ref, bias_ref):
        lane = lax.broadcasted_iota(jnp.int32, (1, 128), 1)
        in_h = [lane < HEAD_DIM, lane >= HEAD_DIM]
        dk_ref[...] = jnp.zeros_like(dk_ref)
        dv_ref[...] = jnp.zeros_like(dv_ref)
        _attn_fill_bias(bias_ref)

        def tile(t, carry):
            rows, win, kind = _attn_tile_geometry(t, d, l)
            k, vv = k_ref[win, :].astype(BF16), v_ref[win, :].astype(BF16)
            dout, lse_t, dlse_t = do_ref[rows, :], lse_ref[rows, :], dlse_ref[rows, :]
            od = dout * o_ref[rows, :]
            q2 = _split_heads((q_ref[rows, :] * scale).astype(BF16), in_h)
            do2 = _split_heads(dout.astype(BF16), in_h)
            head_col = lambda a: jnp.concatenate([a[:, 0:1], a[:, HEAD_DIM:HEAD_DIM + 1]], axis=0)
            delta = jnp.concatenate([jnp.sum(jnp.where(m, od, 0.0), axis=1, keepdims=True) for m in in_h], axis=0)
            p = jnp.exp(_nt(q2, k) + bias_ref[kind] - head_col(lse_t))
            ds = (p * (_nt(do2, vv) - delta + head_col(dlse_t))).astype(BF16)
            dq2 = _nn(ds, k) * scale
            dq_ref[rows, :] = jnp.where(in_h[0], dq2[:ATT_TQ], dq2[ATT_TQ:])
            dk_ref[win, :] += _tn(ds, q2)
            dv_ref[win, :] += _tn(p, do2)
            return carry

        lax.fori_loop(0, s_len // ATT_TQ, tile, 0, unroll=4)

    return pl.pallas_call(
        body, name=f"attn_bwd_g{g}", grid=(4,),
        out_shape=[jax.ShapeDtypeStruct((s_len, 512), F32)] * 3,
        in_specs=[q_spec, k_spec, v_spec, o_spec, o_spec, o_spec, o_spec], out_specs=[o_spec] * 3,
        scratch_shapes=[pltpu.VMEM((3, 2 * ATT_TQ, ATT_TK), F32)],
        compiler_params=_params(("parallel",), VMEM_BIG),
    )(qk, qk, v, o, lse, do, dlse)


def _mix_weights(ls):
    mx = jnp.maximum(jnp.maximum(ls[0], ls[1]), ls[2])
    es = [jnp.exp(x - mx) for x in ls]
    tot = es[0] + es[1] + es[2]
    return [e / tot for e in es]


def _attn_out(os_, lses, z, x, gate, w_out):
    s_len, dm = x.shape
    tm = 256
    wdt = 512

    def body(o0, o1, o2, l0, l1, l2, z_ref, x_ref, g_ref, w_ref, a_ref, y_ref, x1_ref):
        alphas = _mix_weights([l0[...], l1[...], l2[...]])
        y = jnp.zeros((tm, dm), F32)
        for g, o_ref in enumerate((o0, o1, o2)):
            a_g = (o_ref[...] * alphas[g] * _silu(z_ref[:, g * wdt:(g + 1) * wdt])).astype(BF16)
            a_ref[:, g * wdt:(g + 1) * wdt] = a_g
            y = y + _nn(a_g, w_ref[g * wdt:(g + 1) * wdt, :])
        y_ref[...] = y
        x1_ref[...] = x_ref[...] + g_ref[...] * y

    row = lambda c: pl.BlockSpec((tm, c), lambda i: (i, 0))
    return pl.pallas_call(
        body, name="attn_out", grid=(s_len // tm,),
        out_shape=[jax.ShapeDtypeStruct((s_len, 3 * wdt), BF16), jax.ShapeDtypeStruct((s_len, dm), F32),
                   jax.ShapeDtypeStruct((s_len, dm), F32)],
        in_specs=[row(wdt)] * 6 + [row(3 * wdt), row(dm), pl.BlockSpec((1, dm), lambda i: (0, 0)),
                                   pl.BlockSpec(w_out.shape, lambda i: (0, 0))],
        out_specs=[row(3 * wdt), row(dm), row(dm)],
        compiler_params=_params(("parallel",), VMEM_BIG),
    )(*os_, *lses, z, x, gate, w_out)


def _mix_bwd(da, os_, lses, z):
    wdt = 512

    def fn(da, o0, o1, o2, l0, l1, l2, z):
        os_t, ls = [o0, o1, o2], [l0, l1, l2]
        alphas = _mix_weights(ls)
        hi = lax.broadcasted_iota(jnp.int32, (wdt, wdt), 0) // HEAD_DIM
        hj = lax.broadcasted_iota(jnp.int32, (wdt, wdt), 1) // HEAD_DIM
        seg = (hi == hj).astype(F32)
        dos, dal, dzs = [], [], []
        for g in range(3):
            zg = z[:, g * wdt:(g + 1) * wdt]
            sig = jax.nn.sigmoid(zg)
            dag = da[:, g * wdt:(g + 1) * wdt]
            dmix = dag * zg * sig
            dzs.append(dag * os_t[g] * alphas[g] * (sig * (1.0 + zg * (1.0 - sig))))
            dos.append(dmix * alphas[g])
            dal.append(_hnn(dmix * os_t[g], seg))
        mean = alphas[0] * dal[0] + alphas[1] * dal[1] + alphas[2] * dal[2]
        dls = [alphas[g] * (dal[g] - mean) for g in range(3)]
        return dos + dls + [jnp.concatenate(dzs, axis=1)], []

    outs, _ = _rowwise("mix_bwd", fn, [da, *os_, *lses, z], [], [(wdt, F32)] * 6 + [(3 * wdt, BF16)], [], 256)
    return outs[:3], outs[3:6], outs[6]


def _rot_pack_bwd(dqs, dks, dvs, tabs):
    wdt = 512

    def fn(*args):
        grads, (c, sa, sb) = args[:9], args[9:]
        cols = [_rot_bwd(gq, c, sa, sb) for gq in grads[:6]] + list(grads[6:])
        return [jnp.concatenate(cols, axis=1)], []

    (out,), _ = _rowwise("rot_pack_bwd", fn, [*dqs, *dks, *dvs, *tabs], [], [(9 * wdt, BF16)], [], 256)
    return out


CONV_CB = 128
CONV_R = 256
CONV_PAD = 8


def _conv_taps(buf, base, off, sign):
    return [buf[pl.ds(base + off + sign * j, CONV_R), :] for j in range(CONV_WIDTH)]


def _conv_tap_sum(taps, w):
    acc = None
    for j, t in enumerate(taps):
        term = t * w[j:j + 1, :]
        acc = term if acc is None else acc + term
    return acc


def _conv_fwd(xpre, cw, cb):
    s_len, ch = xpre.shape
    nchunk = s_len // CONV_R

    def body(x_ref, w_ref, b_ref, o_ref, xp):
        zero = jnp.zeros((CONV_PAD, CONV_CB), F32)
        xp[0:CONV_PAD, :] = zero
        xp[s_len + CONV_PAD:s_len + 2 * CONV_PAD, :] = zero

        def fill(ci, carry):
            base = pl.multiple_of(ci * CONV_R, CONV_R)
            xp[pl.ds(base + CONV_PAD, CONV_R), :] = x_ref[pl.ds(base, CONV_R), :]
            return carry

        lax.fori_loop(0, nchunk, fill, 0)
        w = w_ref[...]
        b = b_ref[...]

        def chunk(ci, carry):
            base = pl.multiple_of(ci * CONV_R, CONV_R)
            u = _conv_tap_sum(_conv_taps(xp, base, CONV_PAD - CONV_WIDTH // 2, 1), w) + b
            o_ref[pl.ds(base, CONV_R), :] = _silu(u)
            return carry

        lax.fori_loop(0, nchunk, chunk, 0)

    col = lambda r: pl.BlockSpec((r, CONV_CB), lambda j: (0, j))
    return pl.pallas_call(
        body, name="conv_fwd", grid=(ch // CONV_CB,), out_shape=jax.ShapeDtypeStruct((s_len, ch), F32),
        in_specs=[col(s_len), col(CONV_WIDTH), col(1)], out_specs=col(s_len),
        scratch_shapes=[pltpu.VMEM((s_len + 2 * CONV_PAD, CONV_CB), F32)],
        compiler_params=_params(("parallel",), VMEM_BIG),
    )(xpre, cw, cb)


def _conv_bwd(xpre, da, db, cw, cb):
    s_len, ch = xpre.shape
    nchunk = s_len // CONV_R
    half = CONV_WIDTH // 2

    def body(x_ref, da_ref, db_ref, w_ref, b_ref, dx_ref, gw_ref, gb_ref, xp, dcp):
        zero = jnp.zeros((CONV_PAD, CONV_CB), F32)
        for buf in (xp, dcp):
            buf[0:CONV_PAD, :] = zero
            buf[s_len + CONV_PAD:s_len + 2 * CONV_PAD, :] = zero

        def fill(ci, carry):
            base = pl.multiple_of(ci * CONV_R, CONV_R)
            xp[pl.ds(base + CONV_PAD, CONV_R), :] = x_ref[pl.ds(base, CONV_R), :]
            return carry

        lax.fori_loop(0, nchunk, fill, 0)
        w = w_ref[...]
        b = b_ref[...]

        def first(ci, carry):
            base = pl.multiple_of(ci * CONV_R, CONV_R)
            taps = _conv_taps(xp, base, CONV_PAD - half, 1)
            u = _conv_tap_sum(taps, w) + b
            sig = jax.nn.sigmoid(u)
            dc = (da_ref[pl.ds(base, CONV_R), :] + db_ref[pl.ds(base, CONV_R), :]) * (sig * (1.0 + u * (1.0 - sig)))
            dcp[pl.ds(base + CONV_PAD, CONV_R), :] = dc
            gb = carry[0] + jnp.sum(dc, axis=0, keepdims=True)
            gws = [carry[1 + j] + jnp.sum(dc * taps[j], axis=0, keepdims=True) for j in range(CONV_WIDTH)]
            return (gb, *gws)

        z1 = jnp.zeros((1, CONV_CB), F32)
        sums = lax.fori_loop(0, nchunk, first, (z1,) * (1 + CONV_WIDTH))
        gb_ref[...] = sums[0]
        for j in range(CONV_WIDTH):
            gw_ref[j:j + 1, :] = sums[1 + j]

        def second(ci, carry):
            base = pl.multiple_of(ci * CONV_R, CONV_R)
            dx_ref[pl.ds(base, CONV_R), :] = _conv_tap_sum(_conv_taps(dcp, base, CONV_PAD + half, -1), w).astype(dx_ref.dtype)
            return carry

        lax.fori_loop(0, nchunk, second, 0)

    col = lambda r: pl.BlockSpec((r, CONV_CB), lambda j: (0, j))
    return pl.pallas_call(
        body, name="conv_bwd", grid=(ch // CONV_CB,),
        out_shape=[jax.ShapeDtypeStruct((s_len, ch), BF16), jax.ShapeDtypeStruct((CONV_WIDTH, ch), F32),
                   jax.ShapeDtypeStruct((1, ch), F32)],
        in_specs=[col(s_len), col(s_len), col(s_len), col(CONV_WIDTH), col(1)],
        out_specs=[col(s_len), col(CONV_WIDTH), col(1)],
        scratch_shapes=[pltpu.VMEM((s_len + 2 * CONV_PAD, CONV_CB), F32)] * 2,
        compiler_params=_params(("parallel",), VMEM_BIG),
    )(xpre, da, db, cw, cb)


SSD_GW = 256
SSD_N = 128


def _bf16_parts(x, n):
    parts, rest = [], x
    for _ in range(n):
        p = rest.astype(BF16)
        parts.append(p)
        rest = rest - p.astype(F32)
    return parts


@jax.custom_vjp
def _expand(x, e):
    eb = e.astype(BF16)
    return sum(_dg(p, eb, 1, 0) for p in _bf16_parts(x, 2))


def _expand_fwd(x, e):
    return _expand(x, e), e


def _expand_bwd(e, g):
    eb = e.astype(BF16)
    return sum(_dg(p, eb, 1, 1) for p in _bf16_parts(g, 2)), jnp.zeros_like(e)


_expand.defvjp(_expand_fwd, _expand_bwd)


@jax.custom_vjp
def _running_sum(tri, x):
    tb = tri.astype(BF16)
    return sum(_dg(tb, p, 1, 0) for p in _bf16_parts(x, 3))


def _running_sum_fwd(tri, x):
    return _running_sum(tri, x), tri


def _running_sum_bwd(tri, g):
    tb = tri.astype(BF16)
    return jnp.zeros_like(tri), sum(_dg(tb, p, 0, 0) for p in _bf16_parts(g, 3))


_running_sum.defvjp(_running_sum_fwd, _running_sum_bwd)


def _ssd_mask(dirn):
    ri = lax.broadcasted_iota(jnp.int32, (CHUNK, CHUNK), 0)
    cj = lax.broadcasted_iota(jnp.int32, (CHUNK, CHUNK), 1)
    return (cj <= ri) if dirn == 0 else (cj >= ri)


def _ssd_rowsel(dirn):
    last = CHUNK - 1 if dirn == 0 else 0
    return (lax.broadcasted_iota(jnp.int32, (CHUNK, 1), 0) == last).astype(F32)


def _ssd_chunk_pre(dirn):
    nh = 2 * SSD_HEADS

    def f(dt, alog):
        da = dt * (-jnp.exp(alog))
        cum = _running_sum(_ssd_mask(dirn).astype(F32), da)
        tot = jnp.sum(cum * _ssd_rowsel(dirn), axis=0, keepdims=True)
        hh = lax.broadcasted_iota(jnp.int32, (nh, SSD_HEADS * HEAD_DIM), 0)
        jj = lax.broadcasted_iota(jnp.int32, (nh, SSD_HEADS * HEAD_DIM), 1)
        expand = (hh == dirn * SSD_HEADS + jj // HEAD_DIM).astype(F32)
        return cum, cum.T, _expand(dt, expand), _expand(jnp.exp(tot - cum), expand), _expand(jnp.exp(cum), expand)

    return f


def _ssd_group_fn(g, dirn, stacked):
    nh = 2 * SSD_HEADS

    def f(xs, bm, cm, st, cum, cum_t, dt_e, w_e, ce_e):
        mask = _ssd_mask(dirn)
        xdt = xs * dt_e
        cd_e = jnp.sum(ce_e * _ssd_rowsel(dirn), axis=0, keepdims=True)
        cb = _bnt(cm, bm)
        lane_head = lax.broadcasted_iota(jnp.int32, (1, SSD_GW), 1) // HEAD_DIM
        y = _bnn(cm, st) * ce_e
        decayed, inputs = [], []
        for j in range(4):
            hidx = dirn * SSD_HEADS + 4 * g + j
            col = jnp.sum(cum * (lax.broadcasted_iota(jnp.int32, (1, nh), 1) == hidx).astype(F32), axis=1, keepdims=True)
            row = jnp.sum(cum_t * (lax.broadcasted_iota(jnp.int32, (nh, 1), 0) == hidx).astype(F32), axis=0, keepdims=True)
            dec = cb * jnp.exp(jnp.where(mask, col - row, NEG_BIG))
            head = (lane_head == j).astype(F32)
            if stacked:
                decayed.append(dec)
                inputs.append(xdt * head)
            else:
                y = y + _bnn(dec, xdt) * head
        if stacked:
            y = y + _bnn(jnp.concatenate(decayed, axis=1), jnp.concatenate(inputs, axis=0))
        st_out = st * cd_e + _btn(bm, xdt * w_e)
        return y, st_out

    return f


def _ssd_in_specs(kk):
    ln = CHUNK
    return [pl.BlockSpec((ln, 2048), lambda i: (kk(i), 0)),
            pl.BlockSpec((ln, 1024), lambda i: (kk(i), 2)),
            pl.BlockSpec((ln, 1024), lambda i: (kk(i), 3)),
            pl.BlockSpec((ln, 2 * SSD_HEADS), lambda i: (kk(i), 0)),
            pl.BlockSpec((1, 2 * SSD_HEADS), lambda i: (0, 0))]


def _ssd_fwd(xbc, dt, alog, dirn):
    s_len = xbc.shape[0]
    nc = s_len // CHUNK
    kk = (lambda i: i) if dirn == 0 else (lambda i: nc - 1 - i)

    def body(x_ref, b_ref, c_ref, dt_ref, al_ref, y_ref, sts_ref, st):
        @pl.when(pl.program_id(0) == 0)
        def _():
            st[...] = jnp.zeros_like(st)

        sts_ref[0] = st[...]
        cum, cum_t, dt_e, w_e, ce_e = _ssd_chunk_pre(dirn)(dt_ref[...], al_ref[...])
        for g in range(SSD_GROUPS):
            xc = slice(g * SSD_GW, (g + 1) * SSD_GW)
            gc = slice(g * SSD_N, (g + 1) * SSD_N)
            y, st_new = _ssd_group_fn(g, dirn, True)(x_ref[:, xc], b_ref[:, gc], c_ref[:, gc], st[:, xc], cum, cum_t,
                                               dt_e[:, xc], w_e[:, xc], ce_e[:, xc])
            y_ref[:, xc] = y
            st[:, xc] = st_new

    return pl.pallas_call(
        body, name=f"ssd_fwd_d{dirn}", grid=(nc,),
        out_shape=[jax.ShapeDtypeStruct((s_len, 2048), F32), jax.ShapeDtypeStruct((nc, SSD_N, 2048), F32)],
        in_specs=_ssd_in_specs(kk),
        out_specs=[pl.BlockSpec((CHUNK, 2048), lambda i: (kk(i), 0)),
                   pl.BlockSpec((1, SSD_N, 2048), lambda i: (kk(i), 0, 0))],
        scratch_shapes=[pltpu.VMEM((SSD_N, 2048), F32)],
        compiler_params=_params(("arbitrary",), VMEM_BIG),
    )(xbc, xbc, xbc, dt, alog)


def _ssd_bwd(xbc, dt, alog, states, dy, d_e, dirn):
    s_len = xbc.shape[0]
    nc = s_len // CHUNK
    kk = (lambda i: nc - 1 - i) if dirn == 0 else (lambda i: i)

    def body(x_ref, b_ref, c_ref, dt_ref, al_ref, sts_ref, dy_ref, de_ref, dx_ref, ddt_ref, dal_ref, dst):
        @pl.when(pl.program_id(0) == 0)
        def _():
            dst[...] = jnp.zeros_like(dst)
            dal_ref[...] = jnp.zeros_like(dal_ref)

        (cum, cum_t, dt_e, w_e, ce_e), pre_vjp = jax.vjp(_ssd_chunk_pre(dirn), dt_ref[...], al_ref[...])
        dcum = jnp.zeros_like(cum)
        dcum_t = jnp.zeros_like(cum_t)
        d_dt_e, d_w_e, d_ce_e = [], [], []
        for g in range(SSD_GROUPS):
            xc = slice(g * SSD_GW, (g + 1) * SSD_GW)
            gc = slice(g * SSD_N, (g + 1) * SSD_N)
            _, vjp = jax.vjp(_ssd_group_fn(g, dirn, False), x_ref[:, xc], b_ref[:, gc], c_ref[:, gc], sts_ref[0, :, xc], cum, cum_t,
                             dt_e[:, xc], w_e[:, xc], ce_e[:, xc])
            dyg = dy_ref[:, xc]
            dxs, dbm, dcm, dst_g, dcum_g, dcum_t_g, ddte_g, dwe_g, dcee_g = vjp((dyg, dst[:, xc]))
            if dirn == 0:
                dxs = dxs + dyg * de_ref[:, xc]
            dx_ref[:, xc] = dxs
            dx_ref[:, 2048 + g * SSD_N:2048 + (g + 1) * SSD_N] = dbm
            dx_ref[:, 3072 + g * SSD_N:3072 + (g + 1) * SSD_N] = dcm
            dst[:, xc] = dst_g
            dcum = dcum + dcum_g
            dcum_t = dcum_t + dcum_t_g
            d_dt_e.append(ddte_g)
            d_w_e.append(dwe_g)
            d_ce_e.append(dcee_g)
        ddt, dal = pre_vjp((dcum, dcum_t, jnp.concatenate(d_dt_e, axis=1), jnp.concatenate(d_w_e, axis=1),
                            jnp.concatenate(d_ce_e, axis=1)))
        ddt_ref[...] = ddt
        dal_ref[...] += dal

    return pl.pallas_call(
        body, name=f"ssd_bwd_d{dirn}", grid=(nc,),
        out_shape=[jax.ShapeDtypeStruct((s_len, 4096), F32), jax.ShapeDtypeStruct((s_len, 2 * SSD_HEADS), F32),
                   jax.ShapeDtypeStruct((1, 2 * SSD_HEADS), F32)],
        in_specs=_ssd_in_specs(kk) + [pl.BlockSpec((1, SSD_N, 2048), lambda i: (kk(i), 0, 0)),
                                      pl.BlockSpec((CHUNK, 2048), lambda i: (kk(i), 0)),
                                      pl.BlockSpec((1, 2048), lambda i: (0, 0))],
        out_specs=[pl.BlockSpec((CHUNK, 4096), lambda i: (kk(i), 0)),
                   pl.BlockSpec((CHUNK, 2 * SSD_HEADS), lambda i: (kk(i), 0)),
                   pl.BlockSpec((1, 2 * SSD_HEADS), lambda i: (0, 0))],
        scratch_shapes=[pltpu.VMEM((SSD_N, 2048), F32)],
        compiler_params=_params(("arbitrary",), VMEM_BIG),
    )(xbc, xbc, xbc, dt, alog, states, dy, d_e)


def _gate_norm_fn(yf, yb, xs, z, d_e, nw):
    yg = (yf + yb + xs * d_e) * _silu(z)
    return yg * lax.rsqrt(jnp.mean(yg * yg, axis=-1, keepdims=True) + NORM_EPS) * nw


def _gate_norm_fwd(yf, yb, xbc, z, d_e, nw):
    (u,), _ = _rowwise("ssd_gate_norm", lambda *a: ([_gate_norm_fn(*a)], []),
                       [yf, yb, (xbc, 2048, 0), z], [d_e, nw], [(2048, BF16)], [], 256)
    return u


def _gate_norm_bwd(du, yf, yb, xbc, z, d_e, nw):
    def fn(du, yf, yb, xs, z, d_e, nw):
        sig = jax.nn.sigmoid(z)
        gate = z * sig
        ysum = yf + yb + xs * d_e
        yg = ysum * gate
        r = lax.rsqrt(jnp.mean(yg * yg, axis=-1, keepdims=True) + NORM_EPS)
        t = du * nw
        dyg = t * r - yg * (jnp.mean(t * yg, axis=-1, keepdims=True) * (r * r * r))
        dys = dyg * gate
        dz = dyg * ysum * (sig * (1.0 + z * (1.0 - sig)))
        dnw = jnp.sum(du * yg * r, axis=0, keepdims=True)
        dde = jnp.sum(dys * xs, axis=0, keepdims=True)
        hh = lax.broadcasted_iota(jnp.int32, (2048, SSD_HEADS), 0) // HEAD_DIM
        jj = lax.broadcasted_iota(jnp.int32, (2048, SSD_HEADS), 1)
        return [dys, dz], [dnw, _hnn(jnp.broadcast_to(dde, (8, 2048)), (hh == jj).astype(F32))[0:1]]

    (dys, dz), (g_nw, g_d) = _rowwise("ssd_gate_norm_bwd", fn, [du, yf, yb, (xbc, 2048, 0), z], [d_e, nw],
                                      [(2048, F32), (2048, BF16)], [(1, 2048), (1, SSD_HEADS)], 256)
    return dys, dz, g_nw, g_d


def _loss_bwd(x1, y1, tgt, gate, fnw):
    dm = x1.shape[1]

    def fn(x1, y1, tgt, gate, fnw):
        def head(x2, fnw):
            yf = (x2 * lax.rsqrt(jnp.mean(x2 * x2, axis=-1, keepdims=True) + NORM_EPS)) * fnw
            err = yf - tgt
            return 0.5 * jnp.sum(jnp.mean(err * err, axis=-1, keepdims=True), axis=0, keepdims=True)

        x2 = x1 + gate * y1
        loss, vjp = jax.vjp(head, x2, fnw)
        dx2, dfnw = vjp(jnp.ones((1, 1), F32))
        return [dx2, gate * dx2], [dfnw, jnp.sum(dx2 * y1, axis=0, keepdims=True), jnp.broadcast_to(loss, (1, 128))]

    (dx2, dy1), (g_fnw, dgate, loss) = _rowwise("loss_bwd", fn, [x1, y1, tgt], [gate, fnw], [(dm, F32), (dm, BF16)],
                                                [(1, dm), (1, dm), (1, 128)], 256)
    return dx2, dy1, g_fnw, dgate, loss


def _gate_bwd(dx, y, gate):
    dm = dx.shape[1]
    (dy,), (dgate,) = _rowwise("gate_bwd", lambda dx, y, gate: ([gate * dx], [jnp.sum(dx * y, axis=0, keepdims=True)]),
                               [dx, y], [gate], [(dm, BF16)], [(1, dm)], 512)
    return dy, dgate


def _softplus_fwd(dt_raw, bias):
    (dt,), _ = _rowwise("dt_softplus", lambda r, b: ([jax.nn.softplus(r + b)], []), [dt_raw], [bias],
                        [(dt_raw.shape[1], F32)], [], 512)
    return dt


def _softplus_bwd(ddt_f, ddt_b, dt_raw, bias):
    def fn(df, db, r, b):
        g = (df + db) * jax.nn.sigmoid(r + b)
        return [g], [jnp.sum(g, axis=0, keepdims=True)]

    w = dt_raw.shape[1]
    (g,), (gb,) = _rowwise("dt_softplus_bwd", fn, [ddt_f, ddt_b, dt_raw], [bias], [(w, BF16)], [(1, w)], 512)
    return g, gb


def _whole(a):
    nd = len(a.shape)
    return pl.BlockSpec(a.shape, lambda *_: (0,) * nd)


def _mod_part(c_all, mod_w):
    nl, _, ncol = mod_w.shape
    nb = c_all.shape[0]

    def body(c_ref, w_ref, o_ref):
        cond = _silu(c_ref[...])
        for i in range(nl):
            o_ref[i * nb:(i + 1) * nb, :] = _nn(cond, w_ref[i])

    return pl.pallas_call(body, name="mod_part", out_shape=jax.ShapeDtypeStruct((nl * nb, ncol), F32),
                          compiler_params=_params(None, VMEM_BIG))(c_all, mod_w)


def _mod_finish(mod_nb, mod_b, norm_w, tokens):
    nl, dm = norm_w.shape

    def body(a_ref, b_ref, nw_ref, *rest):
        tok_refs, o_refs = rest[:len(tokens)], rest[len(tokens):]
        tok = sum(t[0:1, 0:1] for t in tok_refs)
        for i in range(nl):
            for k in range(3):
                cols = slice(k * dm, (k + 1) * dm)
                o_refs[4 * i + k][...] = a_ref[i:i + 1, cols] + b_ref[i:i + 1, cols]
            o_refs[4 * i + 3][...] = nw_ref[i:i + 1, :] + tok

    rows = pl.pallas_call(body, name="mod_finish", out_shape=[jax.ShapeDtypeStruct((1, dm), F32)] * (4 * nl))(
        mod_nb, mod_b, norm_w, *tokens)
    return [rows[4 * i:4 * i + 4] for i in range(nl)]


def _mod_grad(c_all, dmod_sh):
    nl, nb, ncol = dmod_sh.shape
    dm = c_all.shape[1]

    def body(c_ref, d_ref, o_ref):
        cond = _silu(c_ref[...])
        for i in range(nl):
            o_ref[i] = _tn(cond, d_ref[i])

    return pl.pallas_call(body, name="mod_grad", out_shape=jax.ShapeDtypeStruct((nl, dm, ncol), F32),
                          compiler_params=_params(None, VMEM_BIG))(c_all, dmod_sh)


PACK_ROWS = 16
PACK_COLS = 1024


def _pack_small(rows, b64, a64s, d32, extra):
    nr, na = len(rows), len(a64s)

    def body(*refs):
        o_ref = refs[-1]
        o_ref[...] = jnp.zeros_like(o_ref)
        for i in range(nr):
            o_ref[i:i + 1, :] = refs[i][...]
        b_ref, a_refs, d_ref, e_ref = refs[nr], refs[nr + 1:nr + 1 + na], refs[nr + 1 + na], refs[nr + 2 + na]
        o_ref[nr:nr + 1, 0:64] = b_ref[...]
        o_ref[nr:nr + 1, 64:128] = sum(a[...] for a in a_refs)
        o_ref[nr:nr + 1, 128:160] = d_ref[...]
        o_ref[nr:nr + 1, 256:384] = e_ref[...]

    return pl.pallas_call(body, name="pack_small", out_shape=jax.ShapeDtypeStruct((PACK_ROWS, PACK_COLS), F32))(
        *rows, b64, *a64s, d32, extra)


def _pack_ssd_small(cw, cb, nw):
    def body(cw_ref, cb_ref, nw_ref, o_ref):
        o_ref[...] = jnp.zeros_like(o_ref)
        o_ref[0:5, :] = cw_ref[...]
        o_ref[5:6, :] = cb_ref[...]
        o_ref[6:7, 0:256] = nw_ref[...]

    return pl.pallas_call(body, name="pack_ssd_small", out_shape=jax.ShapeDtypeStruct((8, 512), F32))(cw, cb, nw)


def _sum_parts(p_ref):
    g = p_ref[0].astype(F32)
    for s in range(1, p_ref.shape[0]):
        g = g + p_ref[s].astype(F32)
    return g


def _adam_update(w, g, m, v):
    m2 = ADAM_B1 * m + (1.0 - ADAM_B1) * g
    v2 = ADAM_B2 * v + (1.0 - ADAM_B2) * (g * g)
    m_hat = m2 / (1.0 - ADAM_B1 ** ADAM_STEP)
    v_hat = v2 / (1.0 - ADAM_B2 ** ADAM_STEP)
    return -ADAM_LR * (m_hat / (jnp.sqrt(v_hat) + ADAM_EPS) + ADAM_WD * w), m2, v2


def _adamw_windows(name, parts, params, windows, extra=None):
    n = len(params)

    def body(p_ref, *rest):
        ins, outs = rest[:3 * n], rest[3 * n:]
        g = _sum_parts(p_ref)
        for pi, rows, cols, idx in windows:
            w_ref, m_ref, v_ref = ins[3 * pi:3 * pi + 3]
            gw = g[rows, cols]
            dw, m2, v2 = _adam_update(w_ref[idx], gw, m_ref[idx], v_ref[idx])
            for o_ref, val in zip(outs[4 * pi:4 * pi + 4], (gw, dw, m2, v2), strict=True):
                o_ref[idx] = val
        if extra is not None:
            outs[4 * n][...] = g[extra[0], extra[1]]

    out_shape = [jax.ShapeDtypeStruct(w.shape, F32) for (w, _, _) in params for _ in range(4)]
    if extra is not None:
        out_shape.append(jax.ShapeDtypeStruct((extra[0].stop - extra[0].start, extra[1].stop - extra[1].start), F32))
    res = pl.pallas_call(body, name=name, out_shape=out_shape)(parts, *[a for p in params for a in p])
    return [res[4 * i:4 * i + 4] for i in range(n)] + ([res[4 * n]] if extra is not None else [])


def _adamw(name, w, parts, m, v, tr, tc=None):
    r_, c_ = w.shape
    p_ = parts.shape[0]
    tr = min(tr, r_)
    tc = c_ if tc is None else tc
    assert r_ % tr == 0 and c_ % tc == 0

    def body(w_ref, p_ref, m_ref, v_ref, g_ref, d_ref, m2_ref, v2_ref):
        g = _sum_parts(p_ref)
        g_ref[...] = g
        d_ref[...], m2_ref[...], v2_ref[...] = _adam_update(w_ref[...], g, m_ref[...], v_ref[...])

    blk = pl.BlockSpec((tr, tc), lambda i, j: (i, j))
    return pl.pallas_call(
        body, name=name, grid=(r_ // tr, c_ // tc), out_shape=[jax.ShapeDtypeStruct((r_, c_), F32)] * 4,
        in_specs=[blk, pl.BlockSpec((p_, tr, tc), lambda i, j: (0, i, j)), blk, blk], out_specs=[blk] * 4,
        compiler_params=_params(("parallel", "parallel"), VMEM_BIG),
    )(w, parts, m, v)


def _dev_index(p):
    return 4 * p[0] + 2 * p[1] + p[2]


def _all_gather(name, xs):
    n = len(xs)
    hbm = pl.BlockSpec(memory_space=pl.ANY)

    def body(*refs):
        x_refs, o_refs = refs[:n], refs[n:2 * n]
        send_sems, recv_sems, local_sems = refs[2 * n:]
        x, y, c = lax.axis_index("x"), lax.axis_index("y"), lax.axis_index("c")
        me, sibling = (x, y, c), (x, y, 1 - c)
        chips = [(1 - x, y), (x, 1 - y), (1 - x, 1 - y)]

        def copy(a, k, block, to, src=None):
            dst = o_refs[a].at[_dev_index(block)]
            return pltpu.make_async_remote_copy(
                src_ref=dst if src is None else src, dst_ref=dst, send_sem=send_sems.at[a, k],
                recv_sem=recv_sems.at[a, k], device_id=to, device_id_type=MESH)

        mine = [pltpu.make_async_copy(x_refs[a], o_refs[a].at[_dev_index(me)], local_sems.at[a]) for a in range(n)]
        for cp in mine:
            cp.start()
        first = []
        for a in range(n):
            first.append(copy(a, 0, me, sibling, src=x_refs[a]))
            first += [copy(a, 1 + j, me, (*chip, c), src=x_refs[a]) for j, chip in enumerate(chips)]
        for cp in first:
            cp.start()
        passed = []
        for j, chip in enumerate(chips):
            for a in range(n):
                copy(a, 1 + j, (*chip, c), me).wait_recv()
                cp = copy(a, 4 + j, (*chip, c), sibling)
                cp.start()
                passed.append(cp)
        for a in range(n):
            copy(a, 0, sibling, me).wait_recv()
            for j, chip in enumerate(chips):
                copy(a, 4 + j, (*chip, 1 - c), me).wait_recv()
        for cp in first + passed:
            cp.wait_send()
        for cp in mine:
            cp.wait()

    return pl.pallas_call(
        body, name=name, out_shape=[jax.ShapeDtypeStruct((NDEV, *x.shape), x.dtype) for x in xs],
        in_specs=[hbm] * n, out_specs=[hbm] * n,
        scratch_shapes=[pltpu.SemaphoreType.DMA((n, 7)), pltpu.SemaphoreType.DMA((n, 7)), pltpu.SemaphoreType.DMA((n,))],
    )(*xs)


_HBM = pl.BlockSpec(memory_space=pltpu.HBM)
_SEM = pl.BlockSpec(memory_space=pltpu.SEMAPHORE)
_EFFECT = pltpu.SideEffectType.DATAFLOW_SIDE_EFFECTING


def _mesh_position():
    return lax.axis_index("x"), lax.axis_index("y"), lax.axis_index("c")


def _peers(me):
    return [(k, tuple(1 - v if (k >> b) & 1 else v for v, b in zip(me, (2, 1, 0)))) for k in range(1, NDEV)]


def _landing_zones(name, xs, scatter):
    me = _dev_index(_mesh_position()).astype(jnp.int32).reshape(1)
    lands = []
    for a, x in enumerate(xs):
        rows, cols = x.shape[-2:]
        tr = 256 if rows % 256 == 0 else rows

        def body(me_ref, x_ref, o_ref):
            o_ref[...] = x_ref[...]

        if scatter:
            in_spec = pl.BlockSpec((None, tr, cols), lambda i, me_ref: (me_ref[0], i, 0))
        else:
            in_spec = pl.BlockSpec((tr, cols), lambda i, me_ref: (i, 0))
        lands.append(pl.pallas_call(
            body, name=f"{name}_{a}", out_shape=jax.ShapeDtypeStruct((NDEV, rows, cols), x.dtype),
            grid_spec=pltpu.PrefetchScalarGridSpec(
                num_scalar_prefetch=1, grid=(rows // tr,), in_specs=[in_spec],
                out_specs=pl.BlockSpec((None, tr, cols), lambda i, me_ref: (me_ref[0], i, 0))),
            compiler_params=_params(("arbitrary",)),
        )(me, x))
    return lands


def _exchange_copies(x_refs, land_refs, send_sems, recv_sems, scatter):
    me = _mesh_position()
    out = []
    for k, peer in _peers(me):
        for a, (x_ref, land_ref) in enumerate(zip(x_refs, land_refs)):
            sem = a * (NDEV - 1) + k - 1
            out.append(pltpu.make_async_remote_copy(
                src_ref=x_ref.at[_dev_index(peer)] if scatter else x_ref, dst_ref=land_ref.at[_dev_index(me)],
                send_sem=send_sems.at[sem], recv_sem=recv_sems.at[sem], device_id=peer, device_id_type=MESH))
    return out


def _exchange_start(name, xs, lands, scatter, dep):
    n = len(xs)

    def body(*refs):
        x_refs, land_refs = refs[:n], refs[n:2 * n]
        send_sems, recv_sems = refs[2 * n + 1], refs[2 * n + 2]
        token = refs[-1]
        for cp in _exchange_copies(x_refs, land_refs, send_sems, recv_sems, scatter):
            cp.start()
        token[...] = jnp.zeros_like(token)

    sems = pltpu.SemaphoreType.DMA((n * (NDEV - 1),))
    res = pl.pallas_call(
        body, name=name,
        out_shape=(sems, sems, *[pltpu.HBM(a.shape, a.dtype) for a in (*xs, *lands)], jax.ShapeDtypeStruct((8, 128), F32)),
        in_specs=[_HBM] * (2 * n) + [pl.BlockSpec(memory_space=pl.ANY)],
        out_specs=(_SEM, _SEM, *[_HBM] * (2 * n), pl.BlockSpec(memory_space=pltpu.VMEM)),
        input_output_aliases={i: 2 + i for i in range(2 * n)},
        compiler_params=pltpu.CompilerParams(has_side_effects=_EFFECT),
    )(*[pltpu.with_memory_space_constraint(a, pltpu.HBM) for a in (*xs, *lands)], dep)
    return res[:-1], res[-1]


def _exchange_wait(name, handles, scatter, after):
    send_sems, recv_sems = handles[0], handles[1]
    bufs = handles[2:]
    n = len(bufs) // 2

    def body(*refs):
        x_refs, land_refs = refs[:n], refs[n:2 * n]
        s_sems, r_sems = refs[2 * n], refs[2 * n + 1]
        for cp in _exchange_copies(x_refs, land_refs, s_sems, r_sems, scatter):
            cp.wait_send()
            cp.wait_recv()

    res = pl.pallas_call(
        body, name=name, out_shape=tuple(pltpu.HBM(a.shape, a.dtype) for a in bufs),
        in_specs=[_HBM] * (2 * n) + [_SEM, _SEM, pl.BlockSpec(memory_space=pl.ANY)], out_specs=tuple([_HBM] * (2 * n)),
        input_output_aliases={i: i for i in range(2 * n)},
        compiler_params=pltpu.CompilerParams(has_side_effects=_EFFECT),
    )(*bufs, send_sems, recv_sems, after)
    return res[n:]


def kernel(x, c, positions, norm_w, mod_w, mod_b, attn_w_in, attn_w_out, ssd_w_in, ssd_conv_w, ssd_conv_b, ssd_dt_bias, ssd_a_log, ssd_d, ssd_norm_w, ssd_w_out, final_norm_w, loss_target, m_norm_w, m_mod_w, m_mod_b, m_attn_w_in, m_attn_w_out, m_ssd_w_in, m_ssd_conv_w, m_ssd_conv_b, m_ssd_dt_bias, m_ssd_a_log, m_ssd_d, m_ssd_norm_w, m_ssd_w_out, m_final_norm_w, v_norm_w, v_mod_w, v_mod_b, v_attn_w_in, v_attn_w_out, v_ssd_w_in, v_ssd_conv_w, v_ssd_conv_b, v_ssd_dt_bias, v_ssd_a_log, v_ssd_d, v_ssd_norm_w, v_ssd_w_out, v_final_norm_w):
    s_len, dm = x.shape[1], x.shape[2]
    me = 4 * lax.axis_index("x") + 2 * lax.axis_index("y") + lax.axis_index("c")
    x0 = x.reshape(s_len, dm)
    tgt = loss_target.reshape(s_len, dm)
    aw = 3 * 512
    si = 2 * dm
    sxbc = 2 * si
    n_ssd_in = ssd_w_in.shape[2] * NDEV

    g_ai, c_all = _all_gather("gather_attn_w_in", [attn_w_in[0].astype(BF16), c])
    w_ai = g_ai.transpose(1, 0, 2).reshape(dm, 4 * aw)
    c_all = c_all.reshape(NDEV, dm)
    ssd_small = _pack_ssd_small(ssd_conv_w[0], ssd_conv_b, ssd_norm_w)
    ao_shard = [attn_w_out[0].astype(BF16)]
    ao_handles, ao_token = _exchange_start("w_out_start", ao_shard, _landing_zones("w_out_place", ao_shard, False), False, g_ai)
    late_shards = [ssd_w_in[0].T.astype(BF16), ssd_w_out[0].astype(BF16), ssd_small]
    w_handles, w_token = _exchange_start("weights_start", late_shards, _landing_zones("weights_place", late_shards, False),
                                         False, ao_token)

    part = _mod_part(c_all, mod_w)
    (part_all,) = _all_gather("gather_mod", [part])
    mod_nb = jnp.stack([lax.dynamic_index_in_dim(part_all, i * NDEV + me, axis=1, keepdims=False).reshape(3 * dm)
                        for i in range(2)])
    (shift0, scale0, gate0, nw0), (shift1, scale1, gate1, nw1) = _mod_finish(mod_nb, mod_b, norm_w, [ao_token, w_token])
    shift, scale, gate, nw = [shift0, shift1], [scale0, scale1], [gate0, gate1], [nw0, nw1]

    hn0 = _norm_mod_fwd("norm0", x0, nw[0], scale[0], shift[0])
    inv_freq = ROPE_THETA ** (-jnp.arange(0, ROT_DIM, 2, dtype=F32) / ROT_DIM)
    lane = jnp.arange(128) % HEAD_DIM
    inv_row = jnp.where(lane < ROT_DIM, inv_freq[lane % (ROT_DIM // 2)], 0.0).reshape(1, 128).astype(F32)
    tabs = _rope_tables(positions.reshape(s_len, 1), inv_row)
    qk = _matmul("proj_qk", hn0, w_ai, "nn", F32, MM_T, MM_T, dm, epilogue=_rot_fwd, mrows=tabs, n_out=2 * aw)
    v = _matmul("proj_v", hn0, w_ai, "nn", F32, MM_T, aw // 2, dm, b_noff=2 * aw, n_out=aw)
    z0 = _matmul("proj_z", hn0, w_ai, "nn", F32, MM_T, aw // 2, dm, b_noff=3 * aw, n_out=aw)
    att = [_attn_fwd(g, qk, v) for g in range(3)]
    os_, lses = [a[0] for a in att], [a[1] for a in att]
    (g_ao,) = _exchange_wait("w_out_wait", ao_handles, False, lses[2])
    a0, y0, x1 = _attn_out(os_, lses, z0, x0, gate[0], g_ao.reshape(aw, dm))

    hn1 = _norm_mod_fwd("norm1", x1, nw[1], scale[1], shift[1])
    g_si, g_so, g_small = _exchange_wait("weights_wait", w_handles, False, hn1)
    w_ao = g_ao.reshape(aw, dm)
    w_si_t = g_si.reshape(n_ssd_in, dm)
    w_so = g_so.reshape(si, dm)
    conv_w = g_small[:, 0:CONV_WIDTH, :].transpose(1, 0, 2).reshape(CONV_WIDTH, sxbc)
    conv_b = g_small[:, 5, :].reshape(1, sxbc)
    snw = g_small[:, 6, 0:si // NDEV].reshape(1, si)
    ndt = 2 * SSD_HEADS
    z1 = _matmul("ssd_proj_z", hn1, w_si_t, "nt", F32, MM_T, MM_T, dm, n_out=si)
    xpre = _matmul("ssd_proj_xbc", hn1, w_si_t, "nt", F32, MM_T, MM_T, dm, b_noff=si, n_out=sxbc)
    dt_raw = _matmul("ssd_proj_dt", hn1, w_si_t, "nt", F32, MM_T, ndt, dm, b_noff=si + sxbc, n_out=ndt)
    xbc = _conv_fwd(xpre, conv_w, conv_b)
    dt_bias = ssd_dt_bias.reshape(1, 2 * SSD_HEADS)
    alog = ssd_a_log.reshape(1, 2 * SSD_HEADS)
    dt = _softplus_fwd(dt_raw, dt_bias)
    y_f, st_f = _ssd_fwd(xbc, dt, alog, 0)
    y_b, st_b = _ssd_fwd(xbc, dt, alog, 1)
    d_e = jnp.repeat(ssd_d.reshape(SSD_HEADS), HEAD_DIM).reshape(1, si)
    u = _gate_norm_fwd(y_f, y_b, xbc, z1, d_e, snw)
    y1 = _matmul("ssd_out", u, w_so, "nn", F32, MM_T, MM_T, si)

    fnw = final_norm_w.reshape(1, dm)
    dx2, dy1, g_fnw, dgate1, loss_part = _loss_bwd(x1, y1, tgt, gate[1], fnw)
    du = _matmul("ssd_out_dx", dy1, w_so, "nt", F32, MM_T, MM_T, dm)
    gw_so = _matmul("ssd_out_dw", u, dy1, "tn", BF16, MM_T, MM_T, MM_T)
    dys, dz1, g_snw, g_d = _gate_norm_bwd(du, y_f, y_b, xbc, z1, d_e, snw)
    dxbc_f, ddt_f, dalog_f = _ssd_bwd(xbc, dt, alog, st_f, dys, d_e, 0)
    dxbc_b, ddt_b, dalog_b = _ssd_bwd(xbc, dt, alog, st_b, dys, d_e, 1)
    dpre, g_cw, g_cb = _conv_bwd(xpre, dxbc_f, dxbc_b, conv_w, conv_b)
    ddt_raw, g_dtb = _softplus_bwd(ddt_f, ddt_b, dt_raw, dt_bias)
    dhn1 = [_matmul("ssd_proj_z_dx", dz1, w_si_t, "nn", F32, MM_T, MM_T, MM_T),
            _matmul("ssd_proj_xbc_dx", dpre, w_si_t, "nn", F32, MM_T, MM_T, MM_T, b_koff=si),
            _matmul("ssd_proj_dt_dx", ddt_raw, w_si_t, "nn", F32, MM_T, MM_T, ndt, b_koff=si + sxbc)]
    gw_si_t = _matmul("ssd_proj_z_dw", dz1, hn1, "tn", BF16, MM_T, MM_T, MM_T, dest=(n_ssd_in, 0, None))
    gw_si_t = _matmul("ssd_proj_xbc_dw", dpre, hn1, "tn", BF16, MM_T, MM_T, MM_T, dest=(n_ssd_in, si, gw_si_t))
    gw_si_t = _matmul("ssd_proj_dt_dw", ddt_raw, hn1, "tn", BF16, ndt, MM_T, MM_T, dest=(n_ssd_in, si + sxbc, gw_si_t))
    dx1, g_nw1, dsc1, dsh1 = _norm_mod_bwd("norm1_bwd", x1, dhn1, dx2, nw[1], scale[1], shift[1])

    l1_grads = [gw_so.reshape(NDEV, si // NDEV, dm), gw_si_t.reshape(NDEV, n_ssd_in // NDEV, dm),
                _pack_ssd_small_blocks(g_cw, g_cb, g_snw)]
    l1_handles, l1_token = _exchange_start("l1_grads_start", l1_grads, _landing_zones("l1_grads_place", l1_grads, True),
                                           True, dx1)

    dy0, dgate0 = _gate_bwd(dx1, y0, gate[0] + l1_token[0:1, 0:1])
    da0 = _matmul("attn_out_dx", dy0, w_ao, "nt", F32, MM_T, aw // 2, dm)
    gw_ao = _matmul("attn_out_dw", a0, dy0, "tn", BF16, aw // 2, MM_T, MM_T)
    dos, dls, dz0 = _mix_bwd(da0, os_, lses, z0)
    datt = [_attn_bwd(g, qk, v, os_[g], lses[g], dos[g], dls[g]) for g in range(3)]
    dqkv = _rot_pack_bwd([t[0] for t in datt], [t[1] for t in datt], [t[2] for t in datt], tabs)
    wcol = attn_w_in.shape[2]
    gw_ai = _matmul("proj_qkv_dw", hn0, dqkv, "tn", BF16, MM_T, wcol, MM_T, out_blocks=3 * aw // wcol, dest=(NDEV, 0, None))
    gw_ai = _matmul("proj_z_dw", hn0, dz0, "tn", BF16, MM_T, wcol, MM_T, out_blocks=aw // wcol,
                    dest=(NDEV, 3 * aw // wcol, gw_ai))
    l0_grads = [gw_ai, gw_ao.reshape(NDEV, aw // NDEV, dm)]
    l0_handles, l0_token = _exchange_start("l0_grads_start", l0_grads, _landing_zones("l0_grads_place", l0_grads, True),
                                           True, dqkv)
    zero_row = jnp.tile(l0_token[0:1], (1, dm // 128))
    after_start = lambda acc, t: acc + t
    dhn0 = [_matmul("proj_qkv_dx", dqkv, w_ai, "nt", F32, MM_T, MM_T, aw, n_out=dm, epilogue=after_start, ncols=(zero_row,)),
            _matmul("proj_z_dx", dz0, w_ai, "nt", F32, MM_T, MM_T, aw, b_koff=3 * aw, n_out=dm)]
    dx0, g_nw0, dsc0, dsh0 = _norm_mod_bwd("norm0_bwd", x0, dhn0, dx1, nw[0], scale[0], shift[0])

    small_g = [_pack_small([dsh0, dsc0, dgate0, dsh1, dsc1, dgate1, g_nw0, g_nw1, g_fnw], g_dtb, [dalog_f, dalog_b], g_d, loss_part)]
    sm_handles, sm_token = _exchange_start("small_grads_start", small_g, _landing_zones("small_grads_place", small_g, False),
                                           False, dx0)

    whole = (slice(None), slice(None))
    r_so, r_si, r_small = _exchange_wait("l1_grads_wait", l1_handles, True, sm_token)
    si_out = [o.T for o in _adamw("adamw_ssd_w_in", ssd_w_in[0].T, r_si, m_ssd_w_in[0].T, v_ssd_w_in[0].T, n_ssd_in // NDEV, 256)]
    so_out = _adamw("adamw_ssd_w_out", ssd_w_out[0], r_so, m_ssd_w_out[0], v_ssd_w_out[0], 256)
    cw_cols = ssd_conv_w.shape[2]
    cw_out, cb_out, snw_out = _adamw_windows(
        "adamw_ssd_small", r_small,
        [(ssd_conv_w, m_ssd_conv_w, v_ssd_conv_w), (ssd_conv_b, m_ssd_conv_b, v_ssd_conv_b),
         (ssd_norm_w, m_ssd_norm_w, v_ssd_norm_w)],
        [(0, slice(0, CONV_WIDTH), slice(0, cw_cols), (0, slice(None), slice(None))),
         (1, slice(5, 6), slice(0, cw_cols), whole), (2, slice(6, 7), slice(0, si // NDEV), whole)])
    r_ai, r_ao = _exchange_wait("l0_grads_wait", l0_handles, True, so_out[0])
    ai_out = _adamw("adamw_attn_w_in", attn_w_in[0], r_ai, m_attn_w_in[0], v_attn_w_in[0], 256)
    ao_out = _adamw("adamw_attn_w_out", attn_w_out[0], r_ao, m_attn_w_out[0], v_attn_w_out[0], 192)

    (small_all,) = _exchange_wait("small_grads_wait", sm_handles, False, ai_out[0])
    full = slice(0, PACK_COLS)
    nhd = SSD_HEADS
    windows = [(0, slice(3 * i + k, 3 * i + k + 1), full, (slice(i, i + 1), slice(k * dm, (k + 1) * dm)))
               for i in range(2) for k in range(3)]
    windows += [(1, slice(6 + i, 7 + i), full, (slice(i, i + 1), slice(None))) for i in range(2)]
    windows += [(2, slice(8, 9), full, whole)]
    windows += [(3 + q, slice(9, 10), slice(2 * nhd * q + nhd * j, 2 * nhd * q + nhd * (j + 1)), (0, slice(j, j + 1), slice(None)))
                for q in range(2) for j in range(2)]
    windows += [(5, slice(9, 10), slice(4 * nhd, 5 * nhd), whole)]
    as_row = lambda a: a.reshape(1, dm)
    mb_out, nw_out, fnw_out, dtb_out, alog_out, d_out, loss = _adamw_windows(
        "adamw_small", small_all,
        [(mod_b, m_mod_b, v_mod_b), (norm_w, m_norm_w, v_norm_w), (fnw, as_row(m_final_norm_w), as_row(v_final_norm_w)),
         (ssd_dt_bias, m_ssd_dt_bias, v_ssd_dt_bias), (ssd_a_log, m_ssd_a_log, v_ssd_a_log), (ssd_d, m_ssd_d, v_ssd_d)],
        windows, extra=(slice(9, 10), slice(256, 257)))
    loss = loss.reshape(())

    ncol = mod_w.shape[2]
    dmod_all = small_all[:, 0:6, :].reshape(NDEV, 2, 3 * dm)
    dmod_sh = lax.dynamic_slice_in_dim(dmod_all, me * ncol, ncol, axis=2).transpose(1, 0, 2)
    g_modw = _mod_grad(c_all, dmod_sh).reshape(1, 2 * dm, ncol)
    modw_out = _adamw("adamw_mod_w", mod_w.reshape(2 * dm, ncol), g_modw, m_mod_w.reshape(2 * dm, ncol),
                      v_mod_w.reshape(2 * dm, ncol), 256)

    per_kind = []
    for k in range(4):
        per_kind.append([
            nw_out[k], modw_out[k].reshape(mod_w.shape), mb_out[k], ai_out[k][None], ao_out[k][None], si_out[k][None],
            cw_out[k], cb_out[k], dtb_out[k], alog_out[k], d_out[k], snw_out[k], so_out[k][None], fnw_out[k].reshape(dm)])
    return (loss, dx0.reshape(x.shape), *per_kind[0], *per_kind[1], *per_kind[2], *per_kind[3])


def _pack_ssd_small_blocks(g_cw, g_cb, g_nw):
    nper = g_cw.shape[1] // NDEV
    nwper = g_nw.shape[1] // NDEV

    def body(cw_ref, cb_ref, nw_ref, o_ref):
        o_ref[...] = jnp.zeros_like(o_ref)
        for d in range(NDEV):
            o_ref[d, 0:5, :] = cw_ref[:, d * nper:(d + 1) * nper]
            o_ref[d, 5:6, :] = cb_ref[:, d * nper:(d + 1) * nper]
            o_ref[d, 6:7, 0:nwper] = nw_ref[:, d * nwper:(d + 1) * nwper]

    return pl.pallas_call(body, name="pack_ssd_small_grads", out_shape=jax.ShapeDtypeStruct((NDEV, 8, nper), F32))(g_cw, g_cb, g_nw)
```

```python
import functools
import math

import jax
import jax.numpy as jnp
from jax import lax
from jax.experimental import pallas as pl
from jax.experimental.pallas import tpu as pltpu

F32 = jnp.float32
BF16 = jnp.bfloat16
HI = lax.Precision.HIGHEST
MESH = pl.DeviceIdType.MESH
NDEV = 8

NORM_EPS = 1e-6
ROPE_THETA = 500000.0
ROT_DIM = 16
HEAD_DIM = 64
DILATIONS = (1, 4, 16)
BAND = 64
NEG_BIG = -1e30
CHUNK = 128
SSD_HEADS = 32
SSD_GROUPS = 8
CONV_WIDTH = 5

ADAM_LR = 0.001
ADAM_B1 = 0.9
ADAM_B2 = 0.999
ADAM_EPS = 1e-08
ADAM_WD = 0.01
ADAM_STEP = 10

VMEM_BIG = 56 * 1024 * 1024
MM_T = 1024


def _params(sem=None, vmem=None):
    kw = {}
    if sem is not None:
        kw["dimension_semantics"] = sem
    if vmem is not None:
        kw["vmem_limit_bytes"] = vmem
    return pltpu.CompilerParams(**kw)


def _dg(a, b, ca, cb, prec=None):
    return lax.dot_general(a, b, (((ca,), (cb,)), ((), ())), preferred_element_type=F32, precision=prec)


def _nn(a, b):
    return _dg(a.astype(BF16), b.astype(BF16), 1, 0)


def _nt(a, b):
    return _dg(a.astype(BF16), b.astype(BF16), 1, 1)


def _tn(a, b):
    return _dg(a.astype(BF16), b.astype(BF16), 0, 0)


def _hnn(a, b):
    return _dg(a, b, 1, 0, HI)


@jax.custom_vjp
def _bnn(a, b):
    return _nn(a, b)


_bnn.defvjp(lambda a, b: (_nn(a, b), (a, b)), lambda r, g: (_nt(g, r[1]), _tn(r[0], g)))


@jax.custom_vjp
def _bnt(a, b):
    return _nt(a, b)


_bnt.defvjp(lambda a, b: (_nt(a, b), (a, b)), lambda r, g: (_nn(g, r[1]), _tn(g, r[0])))


@jax.custom_vjp
def _btn(a, b):
    return _tn(a, b)


_btn.defvjp(lambda a, b: (_tn(a, b), (a, b)), lambda r, g: (_nt(r[1], g), _nn(r[0], g)))


def _silu(x):
    return x * jax.nn.sigmoid(x)


def _matmul(name, a, b, mode, out_dtype, tm, tn, tk, *, epilogue=None, tiled=(), mrows=(), ncols=(),
            b_noff=0, b_koff=0, n_out=None, out_blocks=None, dest=None):
    if mode == "tn":
        K, M = a.shape
    else:
        M, K = a.shape
    N = n_out if n_out is not None else (b.shape[0] if mode == "nt" else b.shape[1])
    tm, tn, tk = min(tm, M), min(tn, N), min(tk, K)
    assert M % tm == 0 and N % tn == 0 and K % tk == 0, (name, M, N, K, tm, tn, tk)
    assert b_noff % tn == 0 and b_koff % tk == 0
    no, ko = b_noff // tn, b_koff // tk
    nk = K // tk
    if mode == "tn":
        a_spec = pl.BlockSpec((tk, tm), lambda i, j, k: (k, i))
    else:
        a_spec = pl.BlockSpec((tm, tk), lambda i, j, k: (i, k))
    if mode == "nt":
        b_spec = pl.BlockSpec((tn, tk), lambda i, j, k: (j + no, k + ko))
    else:
        b_spec = pl.BlockSpec((tk, tn), lambda i, j, k: (k + ko, j + no))
    specs = [a_spec, b_spec]
    specs += [pl.BlockSpec((tm, tn), lambda i, j, k: (i, j)) for _ in tiled]
    specs += [pl.BlockSpec((tm, r.shape[1]), lambda i, j, k: (i, 0)) for r in mrows]
    specs += [pl.BlockSpec((1, tn), lambda i, j, k: (0, j)) for _ in ncols]
    total, off, earlier = dest if dest is not None else (None, 0, None)
    if out_blocks is None:
        assert off % tm == 0
        mo = off // tm
        out_shape = jax.ShapeDtypeStruct((M if total is None else total, N), out_dtype)
        out_spec = pl.BlockSpec((tm, tn), lambda i, j, k: (i + mo, j))
    else:
        nper = N // out_blocks
        assert nper % tn == 0
        jb = nper // tn
        out_shape = jax.ShapeDtypeStruct((out_blocks if total is None else total, M, nper), out_dtype)
        out_spec = pl.BlockSpec((None, tm, tn), lambda i, j, k: (j // jb + off, i, j % jb))
    if earlier is not None:
        assert earlier.shape == out_shape.shape and earlier.dtype == out_shape.dtype
    ne = len(tiled) + len(mrows) + len(ncols)
    dot = {"nn": _nn, "nt": _nt, "tn": _tn}[mode]

    def body(a_ref, b_ref, *rest):
        extras, o_ref = rest[:ne], rest[ne]

        def finish(acc):
            if epilogue is not None:
                acc = epilogue(acc, *[e[...] for e in extras])
            o_ref[...] = acc.astype(o_ref.dtype)

        if nk == 1:
            finish(dot(a_ref[...], b_ref[...]))
        else:
            acc_ref = rest[ne + 1]
            k = pl.program_id(2)

            @pl.when(k == 0)
            def _():
                acc_ref[...] = jnp.zeros_like(acc_ref)

            acc_ref[...] += dot(a_ref[...], b_ref[...])

            @pl.when(k == nk - 1)
            def _():
                finish(acc_ref[...])

    args = [a, b, *tiled, *mrows, *ncols]
    aliases = {}
    if earlier is not None:
        specs.append(pl.BlockSpec(memory_space=pl.ANY))
        aliases = {len(args): 0}
        args.append(earlier)

    def body_with_dest(*refs):
        body(*refs[:2 + ne], *refs[2 + ne + (earlier is not None):])

    return pl.pallas_call(
        body_with_dest, name=name, out_shape=out_shape, grid=(M // tm, N // tn, nk),
        in_specs=specs, out_specs=out_spec, input_output_aliases=aliases,
        scratch_shapes=[] if nk == 1 else [pltpu.VMEM((tm, tn), F32)],
        compiler_params=_params(("parallel", "parallel", "arbitrary"), VMEM_BIG),
    )(*args)


def _rowwise(name, fn, tiled, consts, outs, accs, ts):
    tl = [(t, t.shape[1], 0) if not isinstance(t, tuple) else t for t in tiled]
    s_len = tl[0][0].shape[0]
    assert s_len % ts == 0
    nt_, nc_, no_ = len(tl), len(consts), len(outs)

    def body(*refs):
        t_refs, c_refs = refs[:nt_], refs[nt_:nt_ + nc_]
        o_refs, a_refs = refs[nt_ + nc_:nt_ + nc_ + no_], refs[nt_ + nc_ + no_:]
        res_o, res_a = fn(*[r[...] for r in t_refs], *[r[...] for r in c_refs])
        for r, v in zip(o_refs, res_o, strict=True):
            r[...] = v.astype(r.dtype)
        if a_refs:
            @pl.when(pl.program_id(0) == 0)
            def _():
                for r in a_refs:
                    r[...] = jnp.zeros_like(r)

            for r, v in zip(a_refs, res_a, strict=True):
                r[...] += v

    in_specs = [pl.BlockSpec((ts, w), functools.partial(lambda i, cb: (i, cb), cb=cb)) for (_, w, cb) in tl]
    in_specs += [pl.BlockSpec(c.shape, lambda i: (0, 0)) for c in consts]
    out_specs = [pl.BlockSpec((ts, c), lambda i: (i, 0)) for (c, _) in outs]
    out_specs += [pl.BlockSpec(shp, lambda i: (0, 0)) for shp in accs]
    out_shape = [jax.ShapeDtypeStruct((s_len, c), dt) for (c, dt) in outs]
    out_shape += [jax.ShapeDtypeStruct(shp, F32) for shp in accs]
    res = pl.pallas_call(
        body, name=name, out_shape=out_shape, grid=(s_len // ts,), in_specs=in_specs, out_specs=out_specs,
        compiler_params=_params(("arbitrary",) if accs else ("parallel",), VMEM_BIG),
    )(*[t[0] for t in tl], *consts)
    return res[:no_], res[no_:]


def _norm_mod_fn(x, nw, sc, sh):
    r = lax.rsqrt(jnp.mean(x * x, axis=-1, keepdims=True) + NORM_EPS)
    return (x * r * nw) * (1.0 + sc) + sh


def _norm_mod_fwd(name, x, nw, sc, sh):
    (hn,), _ = _rowwise(name, lambda x, nw, sc, sh: ([_norm_mod_fn(x, nw, sc, sh)], []),
                        [x], [nw, sc, sh], [(x.shape[1], BF16)], [], 512)
    return hn


def _norm_mod_bwd(name, x, dhn_parts, dres, nw, sc, sh):
    n = len(dhn_parts)
    d = x.shape[1]

    def fn(x, *rest):
        dhn = rest[0]
        for p in rest[1:n]:
            dhn = dhn + p
        dres, nw, sc, sh = rest[n:]
        _, vjp = jax.vjp(_norm_mod_fn, x, nw, sc, sh)
        dx, dnw, dsc, dsh = vjp(dhn)
        return [dx + dres], [dnw, dsc, dsh]

    (dx,), (g_nw, dsc, dsh) = _rowwise(name, fn, [x, *dhn_parts, dres], [nw, sc, sh], [(d, F32)],
                                       [(1, d), (1, d), (1, d)], 256)
    return dx, g_nw, dsc, dsh


def _rope_tables(pos_col, inv_row):
    def fn(pos, inv):
        ang = pos.astype(F32) * inv
        e = lax.broadcasted_iota(jnp.int32, (1, 128), 1) % HEAD_DIM
        cos, sin = jnp.cos(ang), jnp.sin(ang)
        half = ROT_DIM // 2
        return [jnp.where(e < ROT_DIM, cos, 1.0), jnp.where(e < half, -sin, 0.0),
                jnp.where((e >= half) & (e < ROT_DIM), sin, 0.0)], []

    (c, sa, sb), _ = _rowwise("rope_tables", fn, [pos_col], [inv_row], [(128, F32)] * 3, [], 512)
    return c, sa, sb


def _rot_fwd(t, c, sa, sb):
    n = t.shape[1]
    rep = n // 128
    c, sa, sb = (jnp.tile(u, (1, rep)) for u in (c, sa, sb))
    return t * c + pltpu.roll(t, n - ROT_DIM // 2, 1) * sa + pltpu.roll(t, ROT_DIM // 2, 1) * sb


def _rot_bwd(g, c, sa, sb):
    n = g.shape[1]
    rep = n // 128
    c, sa, sb = (jnp.tile(u, (1, rep)) for u in (c, sa, sb))
    return g * c + pltpu.roll(g * sa, ROT_DIM // 2, 1) + pltpu.roll(g * sb, n - ROT_DIM // 2, 1)


ATT_TQ = 128
ATT_TK = ATT_TQ + 2 * BAND


def _attn_specs(g, s_len):
    def blk(off):
        return pl.BlockSpec((s_len, 128), functools.partial(lambda hp, off: (0, off + hp), off=off))

    return blk(4 * g), blk(12 + 4 * g), blk(4 * g), blk(0)


def _attn_tile_geometry(t, d, l):
    nts = l // ATT_TQ
    r = t // nts
    ts = t % nts
    q0 = ts * ATT_TQ
    ws = jnp.clip(q0 - BAND, 0, l - ATT_TK)
    kind = jnp.where(ts == 0, 0, jnp.where(ts == nts - 1, 2, 1))
    if d == 1:
        return pl.ds(pl.multiple_of(q0, ATT_TQ), ATT_TQ), pl.ds(pl.multiple_of(ws, BAND), ATT_TK), kind
    return pl.ds(r + d * q0, ATT_TQ, stride=d), pl.ds(r + d * ws, ATT_TK, stride=d), kind


def _attn_fill_bias(bias_ref):
    iq = lax.broadcasted_iota(jnp.int32, (2 * ATT_TQ, 1), 0) % ATT_TQ
    ik = lax.broadcasted_iota(jnp.int32, (1, ATT_TK), 1)
    for i, off in enumerate((0, -BAND, -2 * BAND)):
        bias_ref[i] = jnp.where(jnp.abs(ik + off - iq) <= BAND, 0.0, NEG_BIG)


def _split_heads(t, in_h):
    zero = jnp.zeros_like(t)
    return jnp.concatenate([jnp.where(in_h[0], t, zero), jnp.where(in_h[1], t, zero)], axis=0)


def _attn_fwd(g, qk, v):
    s_len = qk.shape[0]
    d = DILATIONS[g]
    l = s_len // d
    assert l % ATT_TQ == 0 and l >= ATT_TK
    q_spec, k_spec, v_spec, o_spec = _attn_specs(g, s_len)
    scale = 1.0 / math.sqrt(HEAD_DIM)

    def body(q_ref, k_ref, v_ref, o_ref, lse_ref, bias_ref):
        lane = lax.broadcasted_iota(jnp.int32, (1, 128), 1)
        in_h = [lane < HEAD_DIM, lane >= HEAD_DIM]
        _attn_fill_bias(bias_ref)

        def tile(t, carry):
            rows, win, kind = _attn_tile_geometry(t, d, l)
            q = (q_ref[rows, :] * scale).astype(BF16)
            k = k_ref[win, :].astype(BF16)
            vv = v_ref[win, :].astype(BF16)
            s = _nt(_split_heads(q, in_h), k) + bias_ref[kind]
            m = jnp.max(s, axis=1, keepdims=True)
            p = jnp.exp(s - m)
            den = jnp.sum(p, axis=1, keepdims=True)
            out = _nn(p, vv) / den
            lse = m + jnp.log(den)
            o_ref[rows, :] = jnp.where(in_h[0], out[:ATT_TQ], out[ATT_TQ:])
            lse_ref[rows, :] = jnp.where(in_h[0], lse[:ATT_TQ], lse[ATT_TQ:])
            return carry

        lax.fori_loop(0, s_len // ATT_TQ, tile, 0, unroll=4)

    return pl.pallas_call(
        body, name=f"attn_fwd_g{g}", grid=(4,),
        out_shape=[jax.ShapeDtypeStruct((s_len, 512), F32)] * 2,
        in_specs=[q_spec, k_spec, v_spec], out_specs=[o_spec, o_spec],
        scratch_shapes=[pltpu.VMEM((3, 2 * ATT_TQ, ATT_TK), F32)],
        compiler_params=_params(("parallel",), VMEM_BIG),
    )(qk, qk, v)


def _attn_bwd(g, qk, v, o, lse, do, dlse):
    s_len = qk.shape[0]
    d = DILATIONS[g]
    l = s_len // d
    q_spec, k_spec, v_spec, o_spec = _attn_specs(g, s_len)
    scale = 1.0 / math.sqrt(HEAD_DIM)

    def body(q_ref, k_ref, v_ref, o_ref, lse_ref, do_ref, dlse_ref, dq_ref, dk_ref, dv_ref, bias_ref):
        lane = lax.broadcasted_iota(jnp.int32, (1, 128), 1)
        in_h = [lane < HEAD_DIM, lane >= HEAD_DIM]
        dk_ref[...] = jnp.zeros_like(dk_ref)
        dv_ref[...] = jnp.zeros_like(dv_ref)
        _attn_fill_bias(bias_ref)

        def tile(t, carry):
            rows, win, kind = _attn_tile_geometry(t, d, l)
            k, vv = k_ref[win, :].astype(BF16), v_ref[win, :].astype(BF16)
            dout, lse_t, dlse_t = do_ref[rows, :], lse_ref[rows, :], dlse_ref[rows, :]
            od = dout * o_ref[rows, :]
            q2 = _split_heads((q_ref[rows, :] * scale).astype(BF16), in_h)
            do2 = _split_heads(dout.astype(BF16), in_h)
            head_col = lambda a: jnp.concatenate([a[:, 0:1], a[:, HEAD_DIM:HEAD_DIM + 1]], axis=0)
            delta = jnp.concatenate([jnp.sum(jnp.where(m, od, 0.0), axis=1, keepdims=True) for m in in_h], axis=0)
            p = jnp.exp(_nt(q2, k) + bias_ref[kind] - head_col(lse_t))
            ds = (p * (_nt(do2, vv) - delta + head_col(dlse_t))).astype(BF16)
            dq2 = _nn(ds, k) * scale
            dq_ref[rows, :] = jnp.where(in_h[0], dq2[:ATT_TQ], dq2[ATT_TQ:])
            dk_ref[win, :] += _tn(ds, q2)
            dv_ref[win, :] += _tn(p, do2)
            return carry

        lax.fori_loop(0, s_len // ATT_TQ, tile, 0, unroll=4)

    return pl.pallas_call(
        body, name=f"attn_bwd_g{g}", grid=(4,),
        out_shape=[jax.ShapeDtypeStruct((s_len, 512), F32)] * 3,
        in_specs=[q_spec, k_spec, v_spec, o_spec, o_spec, o_spec, o_spec], out_specs=[o_spec] * 3,
        scratch_shapes=[pltpu.VMEM((3, 2 * ATT_TQ, ATT_TK), F32)],
        compiler_params=_params(("parallel",), VMEM_BIG),
    )(qk, qk, v, o, lse, do, dlse)


def _mix_weights(ls):
    mx = jnp.maximum(jnp.maximum(ls[0], ls[1]), ls[2])
    es = [jnp.exp(x - mx) for x in ls]
    tot = es[0] + es[1] + es[2]
    return [e / tot for e in es]


def _attn_out(os_, lses, z, x, gate, w_out):
    s_len, dm = x.shape
    tm = 256
    wdt = 512

    def body(o0, o1, o2, l0, l1, l2, z_ref, x_ref, g_ref, w_ref, a_ref, y_ref, x1_ref):
        alphas = _mix_weights([l0[...], l1[...], l2[...]])
        y = jnp.zeros((tm, dm), F32)
        for g, o_ref in enumerate((o0, o1, o2)):
            a_g = (o_ref[...] * alphas[g] * _silu(z_ref[:, g * wdt:(g + 1) * wdt])).astype(BF16)
            a_ref[:, g * wdt:(g + 1) * wdt] = a_g
            y = y + _nn(a_g, w_ref[g * wdt:(g + 1) * wdt, :])
        y_ref[...] = y
        x1_ref[...] = x_ref[...] + g_ref[...] * y

    row = lambda c: pl.BlockSpec((tm, c), lambda i: (i, 0))
    return pl.pallas_call(
        body, name="attn_out", grid=(s_len // tm,),
        out_shape=[jax.ShapeDtypeStruct((s_len, 3 * wdt), BF16), jax.ShapeDtypeStruct((s_len, dm), F32),
                   jax.ShapeDtypeStruct((s_len, dm), F32)],
        in_specs=[row(wdt)] * 6 + [row(3 * wdt), row(dm), pl.BlockSpec((1, dm), lambda i: (0, 0)),
                                   pl.BlockSpec(w_out.shape, lambda i: (0, 0))],
        out_specs=[row(3 * wdt), row(dm), row(dm)],
        compiler_params=_params(("parallel",), VMEM_BIG),
    )(*os_, *lses, z, x, gate, w_out)


def _mix_bwd(da, os_, lses, z):
    wdt = 512

    def fn(da, o0, o1, o2, l0, l1, l2, z):
        os_t, ls = [o0, o1, o2], [l0, l1, l2]
        alphas = _mix_weights(ls)
        hi = lax.broadcasted_iota(jnp.int32, (wdt, wdt), 0) // HEAD_DIM
        hj = lax.broadcasted_iota(jnp.int32, (wdt, wdt), 1) // HEAD_DIM
        seg = (hi == hj).astype(F32)
        dos, dal, dzs = [], [], []
        for g in range(3):
            zg = z[:, g * wdt:(g + 1) * wdt]
            sig = jax.nn.sigmoid(zg)
            dag = da[:, g * wdt:(g + 1) * wdt]
            dmix = dag * zg * sig
            dzs.append(dag * os_t[g] * alphas[g] * (sig * (1.0 + zg * (1.0 - sig))))
            dos.append(dmix * alphas[g])
            dal.append(_hnn(dmix * os_t[g], seg))
        mean = alphas[0] * dal[0] + alphas[1] * dal[1] + alphas[2] * dal[2]
        dls = [alphas[g] * (dal[g] - mean) for g in range(3)]
        return dos + dls + [jnp.concatenate(dzs, axis=1)], []

    outs, _ = _rowwise("mix_bwd", fn, [da, *os_, *lses, z], [], [(wdt, F32)] * 6 + [(3 * wdt, BF16)], [], 256)
    return outs[:3], outs[3:6], outs[6]


def _rot_pack_bwd(dqs, dks, dvs, tabs):
    wdt = 512

    def fn(*args):
        grads, (c, sa, sb) = args[:9], args[9:]
        cols = [_rot_bwd(gq, c, sa, sb) for gq in grads[:6]] + list(grads[6:])
        return [jnp.concatenate(cols, axis=1)], []

    (out,), _ = _rowwise("rot_pack_bwd", fn, [*dqs, *dks, *dvs, *tabs], [], [(9 * wdt, BF16)], [], 256)
    return out


CONV_CB = 128
CONV_R = 256
CONV_PAD = 8


def _conv_taps(buf, base, off, sign):
    return [buf[pl.ds(base + off + sign * j, CONV_R), :] for j in range(CONV_WIDTH)]


def _conv_tap_sum(taps, w):
    acc = None
    for j, t in enumerate(taps):
        term = t * w[j:j + 1, :]
        acc = term if acc is None else acc + term
    return acc


def _conv_fwd(xpre, cw, cb):
    s_len, ch = xpre.shape
    nchunk = s_len // CONV_R

    def body(x_ref, w_ref, b_ref, o_ref, xp):
        zero = jnp.zeros((CONV_PAD, CONV_CB), F32)
        xp[0:CONV_PAD, :] = zero
        xp[s_len + CONV_PAD:s_len + 2 * CONV_PAD, :] = zero

        def fill(ci, carry):
            base = pl.multiple_of(ci * CONV_R, CONV_R)
            xp[pl.ds(base + CONV_PAD, CONV_R), :] = x_ref[pl.ds(base, CONV_R), :]
            return carry

        lax.fori_loop(0, nchunk, fill, 0)
        w = w_ref[...]
        b = b_ref[...]

        def chunk(ci, carry):
            base = pl.multiple_of(ci * CONV_R, CONV_R)
            u = _conv_tap_sum(_conv_taps(xp, base, CONV_PAD - CONV_WIDTH // 2, 1), w) + b
            o_ref[pl.ds(base, CONV_R), :] = _silu(u)
            return carry

        lax.fori_loop(0, nchunk, chunk, 0)

    col = lambda r: pl.BlockSpec((r, CONV_CB), lambda j: (0, j))
    return pl.pallas_call(
        body, name="conv_fwd", grid=(ch // CONV_CB,), out_shape=jax.ShapeDtypeStruct((s_len, ch), F32),
        in_specs=[col(s_len), col(CONV_WIDTH), col(1)], out_specs=col(s_len),
        scratch_shapes=[pltpu.VMEM((s_len + 2 * CONV_PAD, CONV_CB), F32)],
        compiler_params=_params(("parallel",), VMEM_BIG),
    )(xpre, cw, cb)


def _conv_bwd(xpre, da, db, cw, cb):
    s_len, ch = xpre.shape
    nchunk = s_len // CONV_R
    half = CONV_WIDTH // 2

    def body(x_ref, da_ref, db_ref, w_ref, b_ref, dx_ref, gw_ref, gb_ref, xp, dcp):
        zero = jnp.zeros((CONV_PAD, CONV_CB), F32)
        for buf in (xp, dcp):
            buf[0:CONV_PAD, :] = zero
            buf[s_len + CONV_PAD:s_len + 2 * CONV_PAD, :] = zero

        def fill(ci, carry):
            base = pl.multiple_of(ci * CONV_R, CONV_R)
            xp[pl.ds(base + CONV_PAD, CONV_R), :] = x_ref[pl.ds(base, CONV_R), :]
            return carry

        lax.fori_loop(0, nchunk, fill, 0)
        w = w_ref[...]
        b = b_ref[...]

        def first(ci, carry):
            base = pl.multiple_of(ci * CONV_R, CONV_R)
            taps = _conv_taps(xp, base, CONV_PAD - half, 1)
            u = _conv_tap_sum(taps, w) + b
            sig = jax.nn.sigmoid(u)
            dc = (da_ref[pl.ds(base, CONV_R), :] + db_ref[pl.ds(base, CONV_R), :]) * (sig * (1.0 + u * (1.0 - sig)))
            dcp[pl.ds(base + CONV_PAD, CONV_R), :] = dc
            gb = carry[0] + jnp.sum(dc, axis=0, keepdims=True)
            gws = [carry[1 + j] + jnp.sum(dc * taps[j], axis=0, keepdims=True) for j in range(CONV_WIDTH)]
            return (gb, *gws)

        z1 = jnp.zeros((1, CONV_CB), F32)
        sums = lax.fori_loop(0, nchunk, first, (z1,) * (1 + CONV_WIDTH))
        gb_ref[...] = sums[0]
        for j in range(CONV_WIDTH):
            gw_ref[j:j + 1, :] = sums[1 + j]

        def second(ci, carry):
            base = pl.multiple_of(ci * CONV_R, CONV_R)
            dx_ref[pl.ds(base, CONV_R), :] = _conv_tap_sum(_conv_taps(dcp, base, CONV_PAD + half, -1), w).astype(dx_ref.dtype)
            return carry

        lax.fori_loop(0, nchunk, second, 0)

    col = lambda r: pl.BlockSpec((r, CONV_CB), lambda j: (0, j))
    return pl.pallas_call(
        body, name="conv_bwd", grid=(ch // CONV_CB,),
        out_shape=[jax.ShapeDtypeStruct((s_len, ch), BF16), jax.ShapeDtypeStruct((CONV_WIDTH, ch), F32),
                   jax.ShapeDtypeStruct((1, ch), F32)],
        in_specs=[col(s_len), col(s_len), col(s_len), col(CONV_WIDTH), col(1)],
        out_specs=[col(s_len), col(CONV_WIDTH), col(1)],
        scratch_shapes=[pltpu.VMEM((s_len + 2 * CONV_PAD, CONV_CB), F32)] * 2,
        compiler_params=_params(("parallel",), VMEM_BIG),
    )(xpre, da, db, cw, cb)


SSD_GW = 256
SSD_N = 128


def _bf16_parts(x, n):
    parts, rest = [], x
    for _ in range(n):
        p = rest.astype(BF16)
        parts.append(p)
        rest = rest - p.astype(F32)
    return parts


@jax.custom_vjp
def _expand(x, e):
    eb = e.astype(BF16)
    return sum(_dg(p, eb, 1, 0) for p in _bf16_parts(x, 2))


def _expand_fwd(x, e):
    return _expand(x, e), e


def _expand_bwd(e, g):
    eb = e.astype(BF16)
    return sum(_dg(p, eb, 1, 1) for p in _bf16_parts(g, 2)), jnp.zeros_like(e)


_expand.defvjp(_expand_fwd, _expand_bwd)


@jax.custom_vjp
def _running_sum(tri, x):
    tb = tri.astype(BF16)
    return sum(_dg(tb, p, 1, 0) for p in _bf16_parts(x, 3))


def _running_sum_fwd(tri, x):
    return _running_sum(tri, x), tri


def _running_sum_bwd(tri, g):
    tb = tri.astype(BF16)
    return jnp.zeros_like(tri), sum(_dg(tb, p, 0, 0) for p in _bf16_parts(g, 3))


_running_sum.defvjp(_running_sum_fwd, _running_sum_bwd)


def _ssd_mask(dirn):
    ri = lax.broadcasted_iota(jnp.int32, (CHUNK, CHUNK), 0)
    cj = lax.broadcasted_iota(jnp.int32, (CHUNK, CHUNK), 1)
    return (cj <= ri) if dirn == 0 else (cj >= ri)


def _ssd_rowsel(dirn):
    last = CHUNK - 1 if dirn == 0 else 0
    return (lax.broadcasted_iota(jnp.int32, (CHUNK, 1), 0) == last).astype(F32)


def _ssd_chunk_pre(dirn):
    nh = 2 * SSD_HEADS

    def f(dt, alog):
        da = dt * (-jnp.exp(alog))
        cum = _running_sum(_ssd_mask(dirn).astype(F32), da)
        tot = jnp.sum(cum * _ssd_rowsel(dirn), axis=0, keepdims=True)
        hh = lax.broadcasted_iota(jnp.int32, (nh, SSD_HEADS * HEAD_DIM), 0)
        jj = lax.broadcasted_iota(jnp.int32, (nh, SSD_HEADS * HEAD_DIM), 1)
        expand = (hh == dirn * SSD_HEADS + jj // HEAD_DIM).astype(F32)
        return cum, cum.T, _expand(dt, expand), _expand(jnp.exp(tot - cum), expand), _expand(jnp.exp(cum), expand)

    return f


def _ssd_group_fn(g, dirn, stacked):
    nh = 2 * SSD_HEADS

    def f(xs, bm, cm, st, cum, cum_t, dt_e, w_e, ce_e):
        mask = _ssd_mask(dirn)
        xdt = xs * dt_e
        cd_e = jnp.sum(ce_e * _ssd_rowsel(dirn), axis=0, keepdims=True)
        cb = _bnt(cm, bm)
        lane_head = lax.broadcasted_iota(jnp.int32, (1, SSD_GW), 1) // HEAD_DIM
        y = _bnn(cm, st) * ce_e
        decayed, inputs = [], []
        for j in range(4):
            hidx = dirn * SSD_HEADS + 4 * g + j
            col = jnp.sum(cum * (lax.broadcasted_iota(jnp.int32, (1, nh), 1) == hidx).astype(F32), axis=1, keepdims=True)
            row = jnp.sum(cum_t * (lax.broadcasted_iota(jnp.int32, (nh, 1), 0) == hidx).astype(F32), axis=0, keepdims=True)
            dec = cb * jnp.exp(jnp.where(mask, col - row, NEG_BIG))
            head = (lane_head == j).astype(F32)
            if stacked:
                decayed.append(dec)
                inputs.append(xdt * head)
            else:
                y = y + _bnn(dec, xdt) * head
        if stacked:
            y = y + _bnn(jnp.concatenate(decayed, axis=1), jnp.concatenate(inputs, axis=0))
        st_out = st * cd_e + _btn(bm, xdt * w_e)
        return y, st_out

    return f


def _ssd_in_specs(kk):
    ln = CHUNK
    return [pl.BlockSpec((ln, 2048), lambda i: (kk(i), 0)),
            pl.BlockSpec((ln, 1024), lambda i: (kk(i), 2)),
            pl.BlockSpec((ln, 1024), lambda i: (kk(i), 3)),
            pl.BlockSpec((ln, 2 * SSD_HEADS), lambda i: (kk(i), 0)),
            pl.BlockSpec((1, 2 * SSD_HEADS), lambda i: (0, 0))]


def _ssd_fwd(xbc, dt, alog, dirn):
    s_len = xbc.shape[0]
    nc = s_len // CHUNK
    kk = (lambda i: i) if dirn == 0 else (lambda i: nc - 1 - i)

    def body(x_ref, b_ref, c_ref, dt_ref, al_ref, y_ref, sts_ref, st):
        @pl.when(pl.program_id(0) == 0)
        def _():
            st[...] = jnp.zeros_like(st)

        sts_ref[0] = st[...]
        cum, cum_t, dt_e, w_e, ce_e = _ssd_chunk_pre(dirn)(dt_ref[...], al_ref[...])
        for g in range(SSD_GROUPS):
            xc = slice(g * SSD_GW, (g + 1) * SSD_GW)
            gc = slice(g * SSD_N, (g + 1) * SSD_N)
            y, st_new = _ssd_group_fn(g, dirn, True)(x_ref[:, xc], b_ref[:, gc], c_ref[:, gc], st[:, xc], cum, cum_t,
                                               dt_e[:, xc], w_e[:, xc], ce_e[:, xc])
            y_ref[:, xc] = y
            st[:, xc] = st_new

    return pl.pallas_call(
        body, name=f"ssd_fwd_d{dirn}", grid=(nc,),
        out_shape=[jax.ShapeDtypeStruct((s_len, 2048), F32), jax.ShapeDtypeStruct((nc, SSD_N, 2048), F32)],
        in_specs=_ssd_in_specs(kk),
        out_specs=[pl.BlockSpec((CHUNK, 2048), lambda i: (kk(i), 0)),
                   pl.BlockSpec((1, SSD_N, 2048), lambda i: (kk(i), 0, 0))],
        scratch_shapes=[pltpu.VMEM((SSD_N, 2048), F32)],
        compiler_params=_params(("arbitrary",), VMEM_BIG),
    )(xbc, xbc, xbc, dt, alog)


def _ssd_bwd(xbc, dt, alog, states, dy, d_e, dirn):
    s_len = xbc.shape[0]
    nc = s_len // CHUNK
    kk = (lambda i: nc - 1 - i) if dirn == 0 else (lambda i: i)

    def body(x_ref, b_ref, c_ref, dt_ref, al_ref, sts_ref, dy_ref, de_ref, dx_ref, ddt_ref, dal_ref, dst):
        @pl.when(pl.program_id(0) == 0)
        def _():
            dst[...] = jnp.zeros_like(dst)
            dal_ref[...] = jnp.zeros_like(dal_ref)

        (cum, cum_t, dt_e, w_e, ce_e), pre_vjp = jax.vjp(_ssd_chunk_pre(dirn), dt_ref[...], al_ref[...])
        dcum = jnp.zeros_like(cum)
        dcum_t = jnp.zeros_like(cum_t)
        d_dt_e, d_w_e, d_ce_e = [], [], []
        for g in range(SSD_GROUPS):
            xc = slice(g * SSD_GW, (g + 1) * SSD_GW)
            gc = slice(g * SSD_N, (g + 1) * SSD_N)
            _, vjp = jax.vjp(_ssd_group_fn(g, dirn, False), x_ref[:, xc], b_ref[:, gc], c_ref[:, gc], sts_ref[0, :, xc], cum, cum_t,
                             dt_e[:, xc], w_e[:, xc], ce_e[:, xc])
            dyg = dy_ref[:, xc]
            dxs, dbm, dcm, dst_g, dcum_g, dcum_t_g, ddte_g, dwe_g, dcee_g = vjp((dyg, dst[:, xc]))
            if dirn == 0:
                dxs = dxs + dyg * de_ref[:, xc]
            dx_ref[:, xc] = dxs
            dx_ref[:, 2048 + g * SSD_N:2048 + (g + 1) * SSD_N] = dbm
            dx_ref[:, 3072 + g * SSD_N:3072 + (g + 1) * SSD_N] = dcm
            dst[:, xc] = dst_g
            dcum = dcum + dcum_g
            dcum_t = dcum_t + dcum_t_g
            d_dt_e.append(ddte_g)
            d_w_e.append(dwe_g)
            d_ce_e.append(dcee_g)
        ddt, dal = pre_vjp((dcum, dcum_t, jnp.concatenate(d_dt_e, axis=1), jnp.concatenate(d_w_e, axis=1),
                            jnp.concatenate(d_ce_e, axis=1)))
        ddt_ref[...] = ddt
        dal_ref[...] += dal

    return pl.pallas_call(
        body, name=f"ssd_bwd_d{dirn}", grid=(nc,),
        out_shape=[jax.ShapeDtypeStruct((s_len, 4096), F32), jax.ShapeDtypeStruct((s_len, 2 * SSD_HEADS), F32),
                   jax.ShapeDtypeStruct((1, 2 * SSD_HEADS), F32)],
        in_specs=_ssd_in_specs(kk) + [pl.BlockSpec((1, SSD_N, 2048), lambda i: (kk(i), 0, 0)),
                                      pl.BlockSpec((CHUNK, 2048), lambda i: (kk(i), 0)),
                                      pl.BlockSpec((1, 2048), lambda i: (0, 0))],
        out_specs=[pl.BlockSpec((CHUNK, 4096), lambda i: (kk(i), 0)),
                   pl.BlockSpec((CHUNK, 2 * SSD_HEADS), lambda i: (kk(i), 0)),
                   pl.BlockSpec((1, 2 * SSD_HEADS), lambda i: (0, 0))],
        scratch_shapes=[pltpu.VMEM((SSD_N, 2048), F32)],
        compiler_params=_params(("arbitrary",), VMEM_BIG),
    )(xbc, xbc, xbc, dt, alog, states, dy, d_e)


def _gate_norm_fn(yf, yb, xs, z, d_e, nw):
    yg = (yf + yb + xs * d_e) * _silu(z)
    return yg * lax.rsqrt(jnp.mean(yg * yg, axis=-1, keepdims=True) + NORM_EPS) * nw


def _gate_norm_fwd(yf, yb, xbc, z, d_e, nw):
    (u,), _ = _rowwise("ssd_gate_norm", lambda *a: ([_gate_norm_fn(*a)], []),
                       [yf, yb, (xbc, 2048, 0), z], [d_e, nw], [(2048, BF16)], [], 256)
    return u


def _gate_norm_bwd(du, yf, yb, xbc, z, d_e, nw):
    def fn(du, yf, yb, xs, z, d_e, nw):
        sig = jax.nn.sigmoid(z)
        gate = z * sig
        ysum = yf + yb + xs * d_e
        yg = ysum * gate
        r = lax.rsqrt(jnp.mean(yg * yg, axis=-1, keepdims=True) + NORM_EPS)
        t = du * nw
        dyg = t * r - yg * (jnp.mean(t * yg, axis=-1, keepdims=True) * (r * r * r))
        dys = dyg * gate
        dz = dyg * ysum * (sig * (1.0 + z * (1.0 - sig)))
        dnw = jnp.sum(du * yg * r, axis=0, keepdims=True)
        dde = jnp.sum(dys * xs, axis=0, keepdims=True)
        hh = lax.broadcasted_iota(jnp.int32, (2048, SSD_HEADS), 0) // HEAD_DIM
        jj = lax.broadcasted_iota(jnp.int32, (2048, SSD_HEADS), 1)
        return [dys, dz], [dnw, _hnn(jnp.broadcast_to(dde, (8, 2048)), (hh == jj).astype(F32))[0:1]]

    (dys, dz), (g_nw, g_d) = _rowwise("ssd_gate_norm_bwd", fn, [du, yf, yb, (xbc, 2048, 0), z], [d_e, nw],
                                      [(2048, F32), (2048, BF16)], [(1, 2048), (1, SSD_HEADS)], 256)
    return dys, dz, g_nw, g_d


def _loss_bwd(x1, y1, tgt, gate, fnw):
    dm = x1.shape[1]

    def fn(x1, y1, tgt, gate, fnw):
        def head(x2, fnw):
            yf = (x2 * lax.rsqrt(jnp.mean(x2 * x2, axis=-1, keepdims=True) + NORM_EPS)) * fnw
            err = yf - tgt
            return 0.5 * jnp.sum(jnp.mean(err * err, axis=-1, keepdims=True), axis=0, keepdims=True)

        x2 = x1 + gate * y1
        loss, vjp = jax.vjp(head, x2, fnw)
        dx2, dfnw = vjp(jnp.ones((1, 1), F32))
        return [dx2, gate * dx2], [dfnw, jnp.sum(dx2 * y1, axis=0, keepdims=True), jnp.broadcast_to(loss, (1, 128))]

    (dx2, dy1), (g_fnw, dgate, loss) = _rowwise("loss_bwd", fn, [x1, y1, tgt], [gate, fnw], [(dm, F32), (dm, BF16)],
                                                [(1, dm), (1, dm), (1, 128)], 256)
    return dx2, dy1, g_fnw, dgate, loss


def _gate_bwd(dx, y, gate):
    dm = dx.shape[1]
    (dy,), (dgate,) = _rowwise("gate_bwd", lambda dx, y, gate: ([gate * dx], [jnp.sum(dx * y, axis=0, keepdims=True)]),
                               [dx, y], [gate], [(dm, BF16)], [(1, dm)], 512)
    return dy, dgate


def _softplus_fwd(dt_raw, bias):
    (dt,), _ = _rowwise("dt_softplus", lambda r, b: ([jax.nn.softplus(r + b)], []), [dt_raw], [bias],
                        [(dt_raw.shape[1], F32)], [], 512)
    return dt


def _softplus_bwd(ddt_f, ddt_b, dt_raw, bias):
    def fn(df, db, r, b):
        g = (df + db) * jax.nn.sigmoid(r + b)
        return [g], [jnp.sum(g, axis=0, keepdims=True)]

    w = dt_raw.shape[1]
    (g,), (gb,) = _rowwise("dt_softplus_bwd", fn, [ddt_f, ddt_b, dt_raw], [bias], [(w, BF16)], [(1, w)], 512)
    return g, gb


def _whole(a):
    nd = len(a.shape)
    return pl.BlockSpec(a.shape, lambda *_: (0,) * nd)


def _mod_part(c_all, mod_w):
    nl, _, ncol = mod_w.shape
    nb = c_all.shape[0]

    def body(c_ref, w_ref, o_ref):
        cond = _silu(c_ref[...])
        for i in range(nl):
            o_ref[i * nb:(i + 1) * nb, :] = _nn(cond, w_ref[i])

    return pl.pallas_call(body, name="mod_part", out_shape=jax.ShapeDtypeStruct((nl * nb, ncol), F32),
                          compiler_params=_params(None, VMEM_BIG))(c_all, mod_w)


def _mod_finish(mod_nb, mod_b, norm_w, tokens):
    nl, dm = norm_w.shape

    def body(a_ref, b_ref, nw_ref, *rest):
        tok_refs, o_refs = rest[:len(tokens)], rest[len(tokens):]
        tok = sum(t[0:1, 0:1] for t in tok_refs)
        for i in range(nl):
            for k in range(3):
                cols = slice(k * dm, (k + 1) * dm)
                o_refs[4 * i + k][...] = a_ref[i:i + 1, cols] + b_ref[i:i + 1, cols]
            o_refs[4 * i + 3][...] = nw_ref[i:i + 1, :] + tok

    rows = pl.pallas_call(body, name="mod_finish", out_shape=[jax.ShapeDtypeStruct((1, dm), F32)] * (4 * nl))(
        mod_nb, mod_b, norm_w, *tokens)
    return [rows[4 * i:4 * i + 4] for i in range(nl)]


def _mod_grad(c_all, dmod_sh):
    nl, nb, ncol = dmod_sh.shape
    dm = c_all.shape[1]

    def body(c_ref, d_ref, o_ref):
        cond = _silu(c_ref[...])
        for i in range(nl):
            o_ref[i] = _tn(cond, d_ref[i])

    return pl.pallas_call(body, name="mod_grad", out_shape=jax.ShapeDtypeStruct((nl, dm, ncol), F32),
                          compiler_params=_params(None, VMEM_BIG))(c_all, dmod_sh)


PACK_ROWS = 16
PACK_COLS = 1024


def _pack_small(rows, b64, a64s, d32, extra):
    nr, na = len(rows), len(a64s)

    def body(*refs):
        o_ref = refs[-1]
        o_ref[...] = jnp.zeros_like(o_ref)
        for i in range(nr):
            o_ref[i:i + 1, :] = refs[i][...]
        b_ref, a_refs, d_ref, e_ref = refs[nr], refs[nr + 1:nr + 1 + na], refs[nr + 1 + na], refs[nr + 2 + na]
        o_ref[nr:nr + 1, 0:64] = b_ref[...]
        o_ref[nr:nr + 1, 64:128] = sum(a[...] for a in a_refs)
        o_ref[nr:nr + 1, 128:160] = d_ref[...]
        o_ref[nr:nr + 1, 256:384] = e_ref[...]

    return pl.pallas_call(body, name="pack_small", out_shape=jax.ShapeDtypeStruct((PACK_ROWS, PACK_COLS), F32))(
        *rows, b64, *a64s, d32, extra)


def _pack_ssd_small(cw, cb, nw):
    def body(cw_ref, cb_ref, nw_ref, o_ref):
        o_ref[...] = jnp.zeros_like(o_ref)
        o_ref[0:5, :] = cw_ref[...]
        o_ref[5:6, :] = cb_ref[...]
        o_ref[6:7, 0:256] = nw_ref[...]

    return pl.pallas_call(body, name="pack_ssd_small", out_shape=jax.ShapeDtypeStruct((8, 512), F32))(cw, cb, nw)


def _sum_parts(p_ref):
    g = p_ref[0].astype(F32)
    for s in range(1, p_ref.shape[0]):
        g = g + p_ref[s].astype(F32)
    return g


def _adam_update(w, g, m, v):
    m2 = ADAM_B1 * m + (1.0 - ADAM_B1) * g
    v2 = ADAM_B2 * v + (1.0 - ADAM_B2) * (g * g)
    m_hat = m2 / (1.0 - ADAM_B1 ** ADAM_STEP)
    v_hat = v2 / (1.0 - ADAM_B2 ** ADAM_STEP)
    return -ADAM_LR * (m_hat / (jnp.sqrt(v_hat) + ADAM_EPS) + ADAM_WD * w), m2, v2


def _adamw_windows(name, parts, params, windows, extra=None):
    n = len(params)

    def body(p_ref, *rest):
        ins, outs = rest[:3 * n], rest[3 * n:]
        g = _sum_parts(p_ref)
        for pi, rows, cols, idx in windows:
            w_ref, m_ref, v_ref = ins[3 * pi:3 * pi + 3]
            gw = g[rows, cols]
            dw, m2, v2 = _adam_update(w_ref[idx], gw, m_ref[idx], v_ref[idx])
            for o_ref, val in zip(outs[4 * pi:4 * pi + 4], (gw, dw, m2, v2), strict=True):
                o_ref[idx] = val
        if extra is not None:
            outs[4 * n][...] = g[extra[0], extra[1]]

    out_shape = [jax.ShapeDtypeStruct(w.shape, F32) for (w, _, _) in params for _ in range(4)]
    if extra is not None:
        out_shape.append(jax.ShapeDtypeStruct((extra[0].stop - extra[0].start, extra[1].stop - extra[1].start), F32))
    res = pl.pallas_call(body, name=name, out_shape=out_shape)(parts, *[a for p in params for a in p])
    return [res[4 * i:4 * i + 4] for i in range(n)] + ([res[4 * n]] if extra is not None else [])


def _adamw(name, w, parts, m, v, tr, tc=None):
    r_, c_ = w.shape
    p_ = parts.shape[0]
    tr = min(tr, r_)
    tc = c_ if tc is None else tc
    assert r_ % tr == 0 and c_ % tc == 0

    def body(w_ref, p_ref, m_ref, v_ref, g_ref, d_ref, m2_ref, v2_ref):
        g = _sum_parts(p_ref)
        g_ref[...] = g
        d_ref[...], m2_ref[...], v2_ref[...] = _adam_update(w_ref[...], g, m_ref[...], v_ref[...])

    blk = pl.BlockSpec((tr, tc), lambda i, j: (i, j))
    return pl.pallas_call(
        body, name=name, grid=(r_ // tr, c_ // tc), out_shape=[jax.ShapeDtypeStruct((r_, c_), F32)] * 4,
        in_specs=[blk, pl.BlockSpec((p_, tr, tc), lambda i, j: (0, i, j)), blk, blk], out_specs=[blk] * 4,
        compiler_params=_params(("parallel", "parallel"), VMEM_BIG),
    )(w, parts, m, v)


def _dev_index(p):
    return 4 * p[0] + 2 * p[1] + p[2]


def _all_gather(name, xs):
    n = len(xs)
    hbm = pl.BlockSpec(memory_space=pl.ANY)

    def body(*refs):
        x_refs, o_refs = refs[:n], refs[n:2 * n]
        send_sems, recv_sems, local_sems = refs[2 * n:]
        x, y, c = lax.axis_index("x"), lax.axis_index("y"), lax.axis_index("c")
        me, sibling = (x, y, c), (x, y, 1 - c)
        chips = [(1 - x, y), (x, 1 - y), (1 - x, 1 - y)]

        def copy(a, k, block, to, src=None):
            dst = o_refs[a].at[_dev_index(block)]
            return pltpu.make_async_remote_copy(
                src_ref=dst if src is None else src, dst_ref=dst, send_sem=send_sems.at[a, k],
                recv_sem=recv_sems.at[a, k], device_id=to, device_id_type=MESH)

        mine = [pltpu.make_async_copy(x_refs[a], o_refs[a].at[_dev_index(me)], local_sems.at[a]) for a in range(n)]
        for cp in mine:
            cp.start()
        first = []
        for a in range(n):
            first.append(copy(a, 0, me, sibling, src=x_refs[a]))
            first += [copy(a, 1 + j, me, (*chip, c), src=x_refs[a]) for j, chip in enumerate(chips)]
        for cp in first:
            cp.start()
        passed = []
        for j, chip in enumerate(chips):
            for a in range(n):
                copy(a, 1 + j, (*chip, c), me).wait_recv()
                cp = copy(a, 4 + j, (*chip, c), sibling)
                cp.start()
                passed.append(cp)
        for a in range(n):
            copy(a, 0, sibling, me).wait_recv()
            for j, chip in enumerate(chips):
                copy(a, 4 + j, (*chip, 1 - c), me).wait_recv()
        for cp in first + passed:
            cp.wait_send()
        for cp in mine:
            cp.wait()

    return pl.pallas_call(
        body, name=name, out_shape=[jax.ShapeDtypeStruct((NDEV, *x.shape), x.dtype) for x in xs],
        in_specs=[hbm] * n, out_specs=[hbm] * n,
        scratch_shapes=[pltpu.SemaphoreType.DMA((n, 7)), pltpu.SemaphoreType.DMA((n, 7)), pltpu.SemaphoreType.DMA((n,))],
    )(*xs)


_HBM = pl.BlockSpec(memory_space=pltpu.HBM)
_SEM = pl.BlockSpec(memory_space=pltpu.SEMAPHORE)
_EFFECT = pltpu.SideEffectType.DATAFLOW_SIDE_EFFECTING


def _mesh_position():
    return lax.axis_index("x"), lax.axis_index("y"), lax.axis_index("c")


def _peers(me):
    return [(k, tuple(1 - v if (k >> b) & 1 else v for v, b in zip(me, (2, 1, 0)))) for k in range(1, NDEV)]


EXCHANGE_COPIES = {"gather": NDEV - 1, "scatter": NDEV - 1, "pair": 4, "chips": 3}
NCHIP = NDEV // 2


def _landing_zones(name, xs, mode):
    x_, y_, c_ = _mesh_position()
    mine = (2 * x_ + y_ if mode == "chips" else _dev_index((x_, y_, c_))).astype(jnp.int32).reshape(1)
    lands = []
    for a, x in enumerate(xs):
        rows, cols = x.shape[-2:]
        if mode == "pair":
            lands.append(lax.empty((NCHIP, rows, cols), x.dtype))
            continue
        tr = 256 if rows % 256 == 0 else rows

        def body(me_ref, x_ref, o_ref):
            o_ref[...] = x_ref[...]

        if mode == "gather":
            in_spec = pl.BlockSpec((tr, cols), lambda i, me_ref: (i, 0))
        else:
            in_spec = pl.BlockSpec((None, tr, cols), lambda i, me_ref: (me_ref[0], i, 0))
        lands.append(pl.pallas_call(
            body, name=f"{name}_{a}",
            out_shape=jax.ShapeDtypeStruct((NCHIP if mode == "chips" else NDEV, rows, cols), x.dtype),
            grid_spec=pltpu.PrefetchScalarGridSpec(
                num_scalar_prefetch=1, grid=(rows // tr,), in_specs=[in_spec],
                out_specs=pl.BlockSpec((None, tr, cols), lambda i, me_ref: (me_ref[0], i, 0))),
            compiler_params=_params(("arbitrary",)),
        )(mine, x))
    return lands


def _exchange_copies(x_refs, land_refs, send_sems, recv_sems, mode):
    x_, y_, c_ = me = _mesh_position()
    per_array = EXCHANGE_COPIES[mode]
    out = []

    def add(a, k, src, dst, peer):
        sem = a * per_array + k
        out.append(pltpu.make_async_remote_copy(src_ref=src, dst_ref=dst, send_sem=send_sems.at[sem], recv_sem=recv_sems.at[sem],
                                                device_id=peer, device_id_type=MESH))

    for a, (x_ref, land_ref) in enumerate(zip(x_refs, land_refs)):
        if mode in ("gather", "scatter"):
            for k, peer in _peers(me):
                add(a, k - 1, x_ref.at[_dev_index(peer)] if mode == "scatter" else x_ref, land_ref.at[_dev_index(me)], peer)
        elif mode == "pair":
            for chip in range(NCHIP):
                add(a, chip, x_ref.at[2 * chip + 1 - c_], land_ref.at[chip], (x_, y_, 1 - c_))
        else:
            for k in range(1, NCHIP):
                px, py = (1 - x_ if k & 2 else x_), (1 - y_ if k & 1 else y_)
                add(a, k - 1, x_ref.at[2 * px + py], land_ref.at[2 * x_ + y_], (px, py, c_))
    return out


def _exchange_start(name, xs, lands, mode, dep):
    n = len(xs)

    def body(*refs):
        x_refs, land_refs = refs[:n], refs[n:2 * n]
        send_sems, recv_sems = refs[2 * n + 1], refs[2 * n + 2]
        token = refs[-1]
        for cp in _exchange_copies(x_refs, land_refs, send_sems, recv_sems, mode):
            cp.start()
        token[...] = jnp.zeros_like(token)

    sems = pltpu.SemaphoreType.DMA((n * EXCHANGE_COPIES[mode],))
    res = pl.pallas_call(
        body, name=name,
        out_shape=(sems, sems, *[pltpu.HBM(a.shape, a.dtype) for a in (*xs, *lands)], jax.ShapeDtypeStruct((8, 128), F32)),
        in_specs=[_HBM] * (2 * n) + [pl.BlockSpec(memory_space=pl.ANY)],
        out_specs=(_SEM, _SEM, *[_HBM] * (2 * n), pl.BlockSpec(memory_space=pltpu.VMEM)),
        input_output_aliases={i: 2 + i for i in range(2 * n)},
        compiler_params=pltpu.CompilerParams(has_side_effects=_EFFECT),
    )(*[pltpu.with_memory_space_constraint(a, pltpu.HBM) for a in (*xs, *lands)], dep)
    return res[:-1], res[-1]


def _exchange_wait(name, handles, mode, after):
    send_sems, recv_sems = handles[0], handles[1]
    bufs = handles[2:]
    n = len(bufs) // 2

    def body(*refs):
        x_refs, land_refs = refs[:n], refs[n:2 * n]
        s_sems, r_sems = refs[2 * n], refs[2 * n + 1]
        for cp in _exchange_copies(x_refs, land_refs, s_sems, r_sems, mode):
            cp.wait_send()
            cp.wait_recv()

    res = pl.pallas_call(
        body, name=name, out_shape=tuple(pltpu.HBM(a.shape, a.dtype) for a in bufs),
        in_specs=[_HBM] * (2 * n) + [_SEM, _SEM, pl.BlockSpec(memory_space=pl.ANY)], out_specs=tuple([_HBM] * (2 * n)),
        input_output_aliases={i: i for i in range(2 * n)},
        compiler_params=pltpu.CompilerParams(has_side_effects=_EFFECT),
    )(*bufs, send_sems, recv_sems, after)
    return res[n:]


def _pair_sum(name, x, from_sibling):
    _, rows, cols = x.shape
    tr = 256 if rows % 256 == 0 else rows
    core = lax.axis_index("c").astype(jnp.int32).reshape(1)

    def body(c_ref, x_ref, s_ref, o_ref):
        o_ref[...] = (x_ref[...].astype(F32) + s_ref[...].astype(F32)).astype(o_ref.dtype)

    return pl.pallas_call(
        body, name=name, out_shape=jax.ShapeDtypeStruct((NCHIP, rows, cols), x.dtype),
        grid_spec=pltpu.PrefetchScalarGridSpec(
            num_scalar_prefetch=1, grid=(NCHIP, rows // tr),
            in_specs=[pl.BlockSpec((None, tr, cols), lambda j, i, c_ref: (2 * j + c_ref[0], i, 0)),
                      pl.BlockSpec((None, tr, cols), lambda j, i, c_ref: (j, i, 0))],
            out_specs=pl.BlockSpec((None, tr, cols), lambda j, i, c_ref: (j, i, 0))),
        compiler_params=_params(("parallel", "parallel")),
    )(core, x, from_sibling)


def kernel(x, c, positions, norm_w, mod_w, mod_b, attn_w_in, attn_w_out, ssd_w_in, ssd_conv_w, ssd_conv_b, ssd_dt_bias, ssd_a_log, ssd_d, ssd_norm_w, ssd_w_out, final_norm_w, loss_target, m_norm_w, m_mod_w, m_mod_b, m_attn_w_in, m_attn_w_out, m_ssd_w_in, m_ssd_conv_w, m_ssd_conv_b, m_ssd_dt_bias, m_ssd_a_log, m_ssd_d, m_ssd_norm_w, m_ssd_w_out, m_final_norm_w, v_norm_w, v_mod_w, v_mod_b, v_attn_w_in, v_attn_w_out, v_ssd_w_in, v_ssd_conv_w, v_ssd_conv_b, v_ssd_dt_bias, v_ssd_a_log, v_ssd_d, v_ssd_norm_w, v_ssd_w_out, v_final_norm_w):
    s_len, dm = x.shape[1], x.shape[2]
    me = 4 * lax.axis_index("x") + 2 * lax.axis_index("y") + lax.axis_index("c")
    x0 = x.reshape(s_len, dm)
    tgt = loss_target.reshape(s_len, dm)
    aw = 3 * 512
    si = 2 * dm
    sxbc = 2 * si
    n_ssd_in = ssd_w_in.shape[2] * NDEV

    g_ai, c_all = _all_gather("gather_attn_w_in", [attn_w_in[0].astype(BF16), c])
    w_ai = g_ai.transpose(1, 0, 2).reshape(dm, 4 * aw)
    c_all = c_all.reshape(NDEV, dm)

    part = _mod_part(c_all, mod_w)
    (part_all,) = _all_gather("gather_mod", [part])
    mod_nb = jnp.stack([lax.dynamic_index_in_dim(part_all, i * NDEV + me, axis=1, keepdims=False).reshape(3 * dm)
                        for i in range(2)])

    ssd_small = _pack_ssd_small(ssd_conv_w[0], ssd_conv_b, ssd_norm_w)
    ao_shard = [attn_w_out[0].astype(BF16)]
    ao_handles, ao_token = _exchange_start("w_out_start", ao_shard, _landing_zones("w_out_place", ao_shard, "gather"), "gather",
                                           part_all)
    late_shards = [ssd_w_in[0].T.astype(BF16), ssd_w_out[0].astype(BF16), ssd_small]
    w_handles, w_token = _exchange_start("weights_start", late_shards, _landing_zones("weights_place", late_shards, "gather"),
                                         "gather", ao_token)
    (shift0, scale0, gate0, nw0), (shift1, scale1, gate1, nw1) = _mod_finish(mod_nb, mod_b, norm_w, [ao_token, w_token])
    shift, scale, gate, nw = [shift0, shift1], [scale0, scale1], [gate0, gate1], [nw0, nw1]

    hn0 = _norm_mod_fwd("norm0", x0, nw[0], scale[0], shift[0])
    inv_freq = ROPE_THETA ** (-jnp.arange(0, ROT_DIM, 2, dtype=F32) / ROT_DIM)
    lane = jnp.arange(128) % HEAD_DIM
    inv_row = jnp.where(lane < ROT_DIM, inv_freq[lane % (ROT_DIM // 2)], 0.0).reshape(1, 128).astype(F32)
    tabs = _rope_tables(positions.reshape(s_len, 1), inv_row)
    qk = _matmul("proj_qk", hn0, w_ai, "nn", F32, MM_T, MM_T, dm, epilogue=_rot_fwd, mrows=tabs, n_out=2 * aw)
    v = _matmul("proj_v", hn0, w_ai, "nn", F32, MM_T, aw // 2, dm, b_noff=2 * aw, n_out=aw)
    z0 = _matmul("proj_z", hn0, w_ai, "nn", F32, MM_T, aw // 2, dm, b_noff=3 * aw, n_out=aw)
    att = [_attn_fwd(g, qk, v) for g in range(3)]
    os_, lses = [a[0] for a in att], [a[1] for a in att]
    (g_ao,) = _exchange_wait("w_out_wait", ao_handles, "gather", lses[2])
    a0, y0, x1 = _attn_out(os_, lses, z0, x0, gate[0], g_ao.reshape(aw, dm))

    hn1 = _norm_mod_fwd("norm1", x1, nw[1], scale[1], shift[1])
    g_si, g_so, g_small = _exchange_wait("weights_wait", w_handles, "gather", hn1)
    w_ao = g_ao.reshape(aw, dm)
    w_si_t = g_si.reshape(n_ssd_in, dm)
    w_so = g_so.reshape(si, dm)
    conv_w = g_small[:, 0:CONV_WIDTH, :].transpose(1, 0, 2).reshape(CONV_WIDTH, sxbc)
    conv_b = g_small[:, 5, :].reshape(1, sxbc)
    snw = g_small[:, 6, 0:si // NDEV].reshape(1, si)
    ndt = 2 * SSD_HEADS
    z1 = _matmul("ssd_proj_z", hn1, w_si_t, "nt", F32, MM_T, MM_T, dm, n_out=si)
    xpre = _matmul("ssd_proj_xbc", hn1, w_si_t, "nt", F32, MM_T, MM_T, dm, b_noff=si, n_out=sxbc)
    dt_raw = _matmul("ssd_proj_dt", hn1, w_si_t, "nt", F32, MM_T, ndt, dm, b_noff=si + sxbc, n_out=ndt)
    xbc = _conv_fwd(xpre, conv_w, conv_b)
    dt_bias = ssd_dt_bias.reshape(1, 2 * SSD_HEADS)
    alog = ssd_a_log.reshape(1, 2 * SSD_HEADS)
    dt = _softplus_fwd(dt_raw, dt_bias)
    y_f, st_f = _ssd_fwd(xbc, dt, alog, 0)
    y_b, st_b = _ssd_fwd(xbc, dt, alog, 1)
    d_e = jnp.repeat(ssd_d.reshape(SSD_HEADS), HEAD_DIM).reshape(1, si)
    u = _gate_norm_fwd(y_f, y_b, xbc, z1, d_e, snw)
    y1 = _matmul("ssd_out", u, w_so, "nn", F32, MM_T, MM_T, si)

    fnw = final_norm_w.reshape(1, dm)
    dx2, dy1, g_fnw, dgate1, loss_part = _loss_bwd(x1, y1, tgt, gate[1], fnw)
    du = _matmul("ssd_out_dx", dy1, w_so, "nt", F32, MM_T, MM_T, dm)
    gw_so = _matmul("ssd_out_dw", u, dy1, "tn", BF16, MM_T, MM_T, MM_T)
    dys, dz1, g_snw, g_d = _gate_norm_bwd(du, y_f, y_b, xbc, z1, d_e, snw)
    dxbc_f, ddt_f, dalog_f = _ssd_bwd(xbc, dt, alog, st_f, dys, d_e, 0)
    dxbc_b, ddt_b, dalog_b = _ssd_bwd(xbc, dt, alog, st_b, dys, d_e, 1)
    dpre, g_cw, g_cb = _conv_bwd(xpre, dxbc_f, dxbc_b, conv_w, conv_b)
    ddt_raw, g_dtb = _softplus_bwd(ddt_f, ddt_b, dt_raw, dt_bias)
    dhn1 = [_matmul("ssd_proj_z_dx", dz1, w_si_t, "nn", F32, MM_T, MM_T, MM_T),
            _matmul("ssd_proj_xbc_dx", dpre, w_si_t, "nn", F32, MM_T, MM_T, MM_T, b_koff=si),
            _matmul("ssd_proj_dt_dx", ddt_raw, w_si_t, "nn", F32, MM_T, MM_T, ndt, b_koff=si + sxbc)]
    gw_si_t = _matmul("ssd_proj_z_dw", dz1, hn1, "tn", BF16, MM_T, MM_T, MM_T, dest=(n_ssd_in, 0, None))
    gw_si_t = _matmul("ssd_proj_xbc_dw", dpre, hn1, "tn", BF16, MM_T, MM_T, MM_T, dest=(n_ssd_in, si, gw_si_t))
    gw_si_t = _matmul("ssd_proj_dt_dw", ddt_raw, hn1, "tn", BF16, ndt, MM_T, MM_T, dest=(n_ssd_in, si + sxbc, gw_si_t))
    dx1, g_nw1, dsc1, dsh1 = _norm_mod_bwd("norm1_bwd", x1, dhn1, dx2, nw[1], scale[1], shift[1])

    l1_grads = [gw_so.reshape(NDEV, si // NDEV, dm), gw_si_t.reshape(NDEV, n_ssd_in // NDEV, dm),
                _pack_ssd_small_blocks(g_cw, g_cb, g_snw)]
    l1_handles, l1_token = _exchange_start("l1_grads_start", l1_grads, _landing_zones("l1_grads_place", l1_grads, "scatter"),
                                           "scatter", dx1)

    dy0, dgate0 = _gate_bwd(dx1, y0, gate[0] + l1_token[0:1, 0:1])
    da0 = _matmul("attn_out_dx", dy0, w_ao, "nt", F32, MM_T, aw // 2, dm)
    gw_ao = _matmul("attn_out_dw", a0, dy0, "tn", BF16, aw // 2, MM_T, MM_T)
    dos, dls, dz0 = _mix_bwd(da0, os_, lses, z0)
    datt = [_attn_bwd(g, qk, v, os_[g], lses[g], dos[g], dls[g]) for g in range(3)]
    dqkv = _rot_pack_bwd([t[0] for t in datt], [t[1] for t in datt], [t[2] for t in datt], tabs)
    wcol = attn_w_in.shape[2]
    gw_ai = _matmul("proj_qkv_dw", hn0, dqkv, "tn", BF16, MM_T, wcol, MM_T, out_blocks=3 * aw // wcol, dest=(NDEV, 0, None))
    gw_ai = _matmul("proj_z_dw", hn0, dz0, "tn", BF16, MM_T, wcol, MM_T, out_blocks=aw // wcol,
                    dest=(NDEV, 3 * aw // wcol, gw_ai))
    after_start = lambda acc, t: acc + t
    zero_row = lambda token: jnp.tile(token[0:1], (1, dm // 128))
    l0_grads = [gw_ai, gw_ao.reshape(NDEV, aw // NDEV, dm)]
    pair_handles, pair_token = _exchange_start("l0_pair_start", l0_grads, _landing_zones("l0_pair_place", l0_grads, "pair"),
                                               "pair", dqkv)
    dhn0_z = _matmul("proj_z_dx", dz0, w_ai, "nt", F32, MM_T, MM_T, aw, b_koff=3 * aw, n_out=dm, epilogue=after_start,
                     ncols=(zero_row(pair_token),))
    from_sibling = _exchange_wait("l0_pair_wait", pair_handles, "pair", dhn0_z)
    chip_sums = [_pair_sum(f"l0_pair_sum_{a}", g, s) for a, (g, s) in enumerate(zip(l0_grads, from_sibling))]
    l0_handles, l0_token = _exchange_start("l0_grads_start", chip_sums, _landing_zones("l0_grads_place", chip_sums, "chips"),
                                           "chips", dhn0_z)
    dhn0 = [_matmul("proj_qkv_dx", dqkv, w_ai, "nt", F32, MM_T, MM_T, aw, n_out=dm, epilogue=after_start,
                    ncols=(zero_row(l0_token),)), dhn0_z]
    dx0, g_nw0, dsc0, dsh0 = _norm_mod_bwd("norm0_bwd", x0, dhn0, dx1, nw[0], scale[0], shift[0])

    small_g = [_pack_small([dsh0, dsc0, dgate0, dsh1, dsc1, dgate1, g_nw0, g_nw1, g_fnw], g_dtb, [dalog_f, dalog_b], g_d, loss_part)]
    sm_handles, sm_token = _exchange_start("small_grads_start", small_g, _landing_zones("small_grads_place", small_g, "gather"),
                                           "gather", dx0)

    whole = (slice(None), slice(None))
    r_so, r_si, r_small = _exchange_wait("l1_grads_wait", l1_handles, "scatter", sm_token)
    si_out = [o.T for o in _adamw("adamw_ssd_w_in", ssd_w_in[0].T, r_si, m_ssd_w_in[0].T, v_ssd_w_in[0].T, n_ssd_in // NDEV, 256)]
    so_out = _adamw("adamw_ssd_w_out", ssd_w_out[0], r_so, m_ssd_w_out[0], v_ssd_w_out[0], 256)
    cw_cols = ssd_conv_w.shape[2]
    cw_out, cb_out, snw_out = _adamw_windows(
        "adamw_ssd_small", r_small,
        [(ssd_conv_w, m_ssd_conv_w, v_ssd_conv_w), (ssd_conv_b, m_ssd_conv_b, v_ssd_conv_b),
         (ssd_norm_w, m_ssd_norm_w, v_ssd_norm_w)],
        [(0, slice(0, CONV_WIDTH), slice(0, cw_cols), (0, slice(None), slice(None))),
         (1, slice(5, 6), slice(0, cw_cols), whole), (2, slice(6, 7), slice(0, si // NDEV), whole)])
    r_ai, r_ao = _exchange_wait("l0_grads_wait", l0_handles, "chips", so_out[0])
    ai_out = _adamw("adamw_attn_w_in", attn_w_in[0], r_ai, m_attn_w_in[0], v_attn_w_in[0], 256)
    ao_out = _adamw("adamw_attn_w_out", attn_w_out[0], r_ao, m_attn_w_out[0], v_attn_w_out[0], 192)

    (small_all,) = _exchange_wait("small_grads_wait", sm_handles, "gather", ai_out[0])
    full = slice(0, PACK_COLS)
    nhd = SSD_HEADS
    windows = [(0, slice(3 * i + k, 3 * i + k + 1), full, (slice(i, i + 1), slice(k * dm, (k + 1) * dm)))
               for i in range(2) for k in range(3)]
    windows += [(1, slice(6 + i, 7 + i), full, (slice(i, i + 1), slice(None))) for i in range(2)]
    windows += [(2, slice(8, 9), full, whole)]
    windows += [(3 + q, slice(9, 10), slice(2 * nhd * q + nhd * j, 2 * nhd * q + nhd * (j + 1)), (0, slice(j, j + 1), slice(None)))
                for q in range(2) for j in range(2)]
    windows += [(5, slice(9, 10), slice(4 * nhd, 5 * nhd), whole)]
    as_row = lambda a: a.reshape(1, dm)
    mb_out, nw_out, fnw_out, dtb_out, alog_out, d_out, loss = _adamw_windows(
        "adamw_small", small_all,
        [(mod_b, m_mod_b, v_mod_b), (norm_w, m_norm_w, v_norm_w), (fnw, as_row(m_final_norm_w), as_row(v_final_norm_w)),
         (ssd_dt_bias, m_ssd_dt_bias, v_ssd_dt_bias), (ssd_a_log, m_ssd_a_log, v_ssd_a_log), (ssd_d, m_ssd_d, v_ssd_d)],
        windows, extra=(slice(9, 10), slice(256, 257)))
    loss = loss.reshape(())

    ncol = mod_w.shape[2]
    dmod_all = small_all[:, 0:6, :].reshape(NDEV, 2, 3 * dm)
    dmod_sh = lax.dynamic_slice_in_dim(dmod_all, me * ncol, ncol, axis=2).transpose(1, 0, 2)
    g_modw = _mod_grad(c_all, dmod_sh).reshape(1, 2 * dm, ncol)
    modw_out = _adamw("adamw_mod_w", mod_w.reshape(2 * dm, ncol), g_modw, m_mod_w.reshape(2 * dm, ncol),
                      v_mod_w.reshape(2 * dm, ncol), 256)

    per_kind = []
    for k in range(4):
        per_kind.append([
            nw_out[k], modw_out[k].reshape(mod_w.shape), mb_out[k], ai_out[k][None], ao_out[k][None], si_out[k][None],
            cw_out[k], cb_out[k], dtb_out[k], alog_out[k], d_out[k], snw_out[k], so_out[k][None], fnw_out[k].reshape(dm)])
    return (loss, dx0.reshape(x.shape), *per_kind[0], *per_kind[1], *per_kind[2], *per_kind[3])


def _pack_ssd_small_blocks(g_cw, g_cb, g_nw):
    nper = g_cw.shape[1] // NDEV
    nwper = g_nw.shape[1] // NDEV

    def body(cw_ref, cb_ref, nw_ref, o_ref):
        o_ref[...] = jnp.zeros_like(o_ref)
        for d in range(NDEV):
            o_ref[d, 0:5, :] = cw_ref[:, d * nper:(d + 1) * nper]
            o_ref[d, 5:6, :] = cb_ref[:, d * nper:(d + 1) * nper]
            o_ref[d, 6:7, 0:nwper] = nw_ref[:, d * nwper:(d + 1) * nwper]

    return pl.pallas_call(body, name="pack_ssd_small_grads", out_shape=jax.ShapeDtypeStruct((NDEV, 8, nper), F32))(g_cw, g_cb, g_nw)
```

```python
import functools
import math

import jax
import jax.numpy as jnp
from jax import lax
from jax.experimental import pallas as pl
from jax.experimental.pallas import tpu as pltpu

F32 = jnp.float32
BF16 = jnp.bfloat16
HI = lax.Precision.HIGHEST
MESH = pl.DeviceIdType.MESH
NDEV = 8

NORM_EPS = 1e-6
ROPE_THETA = 500000.0
ROT_DIM = 16
HEAD_DIM = 64
DILATIONS = (1, 4, 16)
BAND = 64
NEG_BIG = -1e30
CHUNK = 128
SSD_HEADS = 32
SSD_GROUPS = 8
CONV_WIDTH = 5

ADAM_LR = 0.001
ADAM_B1 = 0.9
ADAM_B2 = 0.999
ADAM_EPS = 1e-08
ADAM_WD = 0.01
ADAM_STEP = 10

VMEM_BIG = 56 * 1024 * 1024
MM_T = 1024


def _params(sem=None, vmem=None):
    kw = {}
    if sem is not None:
        kw["dimension_semantics"] = sem
    if vmem is not None:
        kw["vmem_limit_bytes"] = vmem
    return pltpu.CompilerParams(**kw)


def _dg(a, b, ca, cb, prec=None):
    return lax.dot_general(a, b, (((ca,), (cb,)), ((), ())), preferred_element_type=F32, precision=prec)


def _nn(a, b):
    return _dg(a.astype(BF16), b.astype(BF16), 1, 0)


def _nt(a, b):
    return _dg(a.astype(BF16), b.astype(BF16), 1, 1)


def _tn(a, b):
    return _dg(a.astype(BF16), b.astype(BF16), 0, 0)


def _hnn(a, b):
    return _dg(a, b, 1, 0, HI)


@jax.custom_vjp
def _bnn(a, b):
    return _nn(a, b)


_bnn.defvjp(lambda a, b: (_nn(a, b), (a, b)), lambda r, g: (_nt(g, r[1]), _tn(r[0], g)))


@jax.custom_vjp
def _bnt(a, b):
    return _nt(a, b)


_bnt.defvjp(lambda a, b: (_nt(a, b), (a, b)), lambda r, g: (_nn(g, r[1]), _tn(g, r[0])))


@jax.custom_vjp
def _btn(a, b):
    return _tn(a, b)


_btn.defvjp(lambda a, b: (_tn(a, b), (a, b)), lambda r, g: (_nt(r[1], g), _nn(r[0], g)))


def _silu(x):
    return x * jax.nn.sigmoid(x)


def _matmul(name, a, b, mode, out_dtype, tm, tn, tk, *, epilogue=None, tiled=(), mrows=(), ncols=(),
            b_noff=0, b_koff=0, n_out=None, out_blocks=None, dest=None):
    if mode == "tn":
        K, M = a.shape
    else:
        M, K = a.shape
    N = n_out if n_out is not None else (b.shape[0] if mode == "nt" else b.shape[1])
    tm, tn, tk = min(tm, M), min(tn, N), min(tk, K)
    assert M % tm == 0 and N % tn == 0 and K % tk == 0, (name, M, N, K, tm, tn, tk)
    assert b_noff % tn == 0 and b_koff % tk == 0
    no, ko = b_noff // tn, b_koff // tk
    nk = K // tk
    if mode == "tn":
        a_spec = pl.BlockSpec((tk, tm), lambda i, j, k: (k, i))
    else:
        a_spec = pl.BlockSpec((tm, tk), lambda i, j, k: (i, k))
    if mode == "nt":
        b_spec = pl.BlockSpec((tn, tk), lambda i, j, k: (j + no, k + ko))
    else:
        b_spec = pl.BlockSpec((tk, tn), lambda i, j, k: (k + ko, j + no))
    specs = [a_spec, b_spec]
    specs += [pl.BlockSpec((tm, tn), lambda i, j, k: (i, j)) for _ in tiled]
    specs += [pl.BlockSpec((tm, r.shape[1]), lambda i, j, k: (i, 0)) for r in mrows]
    specs += [pl.BlockSpec((1, tn), lambda i, j, k: (0, j)) for _ in ncols]
    total, off, earlier = dest if dest is not None else (None, 0, None)
    if out_blocks is None:
        assert off % tm == 0
        mo = off // tm
        out_shape = jax.ShapeDtypeStruct((M if total is None else total, N), out_dtype)
        out_spec = pl.BlockSpec((tm, tn), lambda i, j, k: (i + mo, j))
    else:
        nper = N // out_blocks
        assert nper % tn == 0
        jb = nper // tn
        out_shape = jax.ShapeDtypeStruct((out_blocks if total is None else total, M, nper), out_dtype)
        out_spec = pl.BlockSpec((None, tm, tn), lambda i, j, k: (j // jb + off, i, j % jb))
    if earlier is not None:
        assert earlier.shape == out_shape.shape and earlier.dtype == out_shape.dtype
    ne = len(tiled) + len(mrows) + len(ncols)
    dot = {"nn": _nn, "nt": _nt, "tn": _tn}[mode]

    def body(a_ref, b_ref, *rest):
        extras, o_ref = rest[:ne], rest[ne]

        def finish(acc):
            if epilogue is not None:
                acc = epilogue(acc, *[e[...] for e in extras])
            o_ref[...] = acc.astype(o_ref.dtype)

        if nk == 1:
            finish(dot(a_ref[...], b_ref[...]))
        else:
            acc_ref = rest[ne + 1]
            k = pl.program_id(2)

            @pl.when(k == 0)
            def _():
                acc_ref[...] = jnp.zeros_like(acc_ref)

            acc_ref[...] += dot(a_ref[...], b_ref[...])

            @pl.when(k == nk - 1)
            def _():
                finish(acc_ref[...])

    args = [a, b, *tiled, *mrows, *ncols]
    aliases = {}
    if earlier is not None:
        specs.append(pl.BlockSpec(memory_space=pl.ANY))
        aliases = {len(args): 0}
        args.append(earlier)

    def body_with_dest(*refs):
        body(*refs[:2 + ne], *refs[2 + ne + (earlier is not None):])

    return pl.pallas_call(
        body_with_dest, name=name, out_shape=out_shape, grid=(M // tm, N // tn, nk),
        in_specs=specs, out_specs=out_spec, input_output_aliases=aliases,
        scratch_shapes=[] if nk == 1 else [pltpu.VMEM((tm, tn), F32)],
        compiler_params=_params(("parallel", "parallel", "arbitrary"), VMEM_BIG),
    )(*args)


def _matmul_rows(name, a, b, mode, tm, tk, fn, rows, consts, outs, accs, *, n_out=None, b_noff=0, b_koff=0):
    M, K = a.shape
    N = n_out if n_out is not None else (b.shape[0] if mode == "nt" else b.shape[1])
    tm, tk = min(tm, M), min(tk, K)
    assert M % tm == 0 and K % tk == 0 and b_koff % tk == 0 and b_noff % N == 0, (name, M, N, K)
    no, ko, nk = b_noff // N, b_koff // tk, K // tk
    rl = [(t, t.shape[1], 0) if not isinstance(t, tuple) else t for t in rows]
    nr, nc, no_, na = len(rl), len(consts), len(outs), len(accs)
    dot = _nt if mode == "nt" else _nn

    def body(a_ref, b_ref, *rest):
        r_refs, c_refs = rest[:nr], rest[nr:nr + nc]
        o_refs, acc_refs = rest[nr + nc:nr + nc + no_], rest[nr + nc + no_:nr + nc + no_ + na]
        i, k = pl.program_id(0), pl.program_id(1)

        def finish(prod):
            res_o, res_a = fn(prod, *[r[...] for r in r_refs], *[c[...] for c in c_refs])
            for r, v in zip(o_refs, res_o, strict=True):
                r[...] = v.astype(r.dtype)
            if acc_refs:
                @pl.when(i == 0)
                def _():
                    for r in acc_refs:
                        r[...] = jnp.zeros_like(r)

                for r, v in zip(acc_refs, res_a, strict=True):
                    r[...] += v

        if nk == 1:
            finish(dot(a_ref[...], b_ref[...]))
        else:
            prod_ref = rest[-1]

            @pl.when(k == 0)
            def _():
                prod_ref[...] = jnp.zeros_like(prod_ref)

            prod_ref[...] += dot(a_ref[...], b_ref[...])

            @pl.when(k == nk - 1)
            def _():
                finish(prod_ref[...])

    if mode == "nt":
        b_spec = pl.BlockSpec((N, tk), lambda i, k: (no, k + ko))
    else:
        b_spec = pl.BlockSpec((tk, N), lambda i, k: (k + ko, no))
    in_specs = [pl.BlockSpec((tm, tk), lambda i, k: (i, k)), b_spec]
    in_specs += [pl.BlockSpec((tm, w), functools.partial(lambda i, k, cb: (i, cb), cb=cb)) for (_, w, cb) in rl]
    in_specs += [pl.BlockSpec(c.shape, lambda i, k: (0, 0)) for c in consts]
    out_specs = [pl.BlockSpec((tm, c), lambda i, k: (i, 0)) for (c, _) in outs]
    out_specs += [pl.BlockSpec(shp, lambda i, k: (0, 0)) for shp in accs]
    out_shape = [jax.ShapeDtypeStruct((M, c), dt) for (c, dt) in outs] + [jax.ShapeDtypeStruct(shp, F32) for shp in accs]
    res = pl.pallas_call(
        body, name=name, out_shape=out_shape, grid=(M // tm, nk), in_specs=in_specs, out_specs=out_specs,
        scratch_shapes=[] if nk == 1 else [pltpu.VMEM((tm, N), F32)],
        compiler_params=_params(("arbitrary" if accs else "parallel", "arbitrary"), VMEM_BIG),
    )(a, b, *[t[0] for t in rl], *consts)
    return res[:no_], res[no_:]


def _rowwise(name, fn, tiled, consts, outs, accs, ts):
    tl = [(t, t.shape[1], 0) if not isinstance(t, tuple) else t for t in tiled]
    s_len = tl[0][0].shape[0]
    assert s_len % ts == 0
    nt_, nc_, no_ = len(tl), len(consts), len(outs)

    def body(*refs):
        t_refs, c_refs = refs[:nt_], refs[nt_:nt_ + nc_]
        o_refs, a_refs = refs[nt_ + nc_:nt_ + nc_ + no_], refs[nt_ + nc_ + no_:]
        res_o, res_a = fn(*[r[...] for r in t_refs], *[r[...] for r in c_refs])
        for r, v in zip(o_refs, res_o, strict=True):
            r[...] = v.astype(r.dtype)
        if a_refs:
            @pl.when(pl.program_id(0) == 0)
            def _():
                for r in a_refs:
                    r[...] = jnp.zeros_like(r)

            for r, v in zip(a_refs, res_a, strict=True):
                r[...] += v

    in_specs = [pl.BlockSpec((ts, w), functools.partial(lambda i, cb: (i, cb), cb=cb)) for (_, w, cb) in tl]
    in_specs += [pl.BlockSpec(c.shape, lambda i: (0, 0)) for c in consts]
    out_specs = [pl.BlockSpec((ts, c), lambda i: (i, 0)) for (c, _) in outs]
    out_specs += [pl.BlockSpec(shp, lambda i: (0, 0)) for shp in accs]
    out_shape = [jax.ShapeDtypeStruct((s_len, c), dt) for (c, dt) in outs]
    out_shape += [jax.ShapeDtypeStruct(shp, F32) for shp in accs]
    res = pl.pallas_call(
        body, name=name, out_shape=out_shape, grid=(s_len // ts,), in_specs=in_specs, out_specs=out_specs,
        compiler_params=_params(("arbitrary",) if accs else ("parallel",), VMEM_BIG),
    )(*[t[0] for t in tl], *consts)
    return res[:no_], res[no_:]


def _norm_mod_fn(x, nw, sc, sh):
    r = lax.rsqrt(jnp.mean(x * x, axis=-1, keepdims=True) + NORM_EPS)
    return (x * r * nw) * (1.0 + sc) + sh


def _norm_mod_fwd(name, x, nw, sc, sh):
    (hn,), _ = _rowwise(name, lambda x, nw, sc, sh: ([_norm_mod_fn(x, nw, sc, sh)], []),
                        [x], [nw, sc, sh], [(x.shape[1], BF16)], [], 512)
    return hn


def _norm_mod_bwd(name, last, x, dhn_parts, dres, nw, sc, sh, prev=None):
    n = len(dhn_parts)
    d = x.shape[1]
    a, b, mode, tk, kw = last

    def fn(dhn, x, *rest):
        for p in rest[:n]:
            dhn = dhn + p
        dres, rest = rest[n], rest[n + 1:]
        y_prev, (nw, sc, sh), gate = (rest[0], rest[1:4], rest[4]) if prev is not None else (None, rest[0:3], None)
        _, vjp = jax.vjp(_norm_mod_fn, x, nw, sc, sh)
        dx, dnw, dsc, dsh = vjp(dhn)
        dx = dx + dres
        if prev is None:
            return [dx], [dnw, dsc, dsh]
        return [dx, gate * dx], [dnw, dsc, dsh, jnp.sum(dx * y_prev, axis=0, keepdims=True)]

    rows = [x, *dhn_parts, dres] + ([prev[0]] if prev is not None else [])
    consts = [nw, sc, sh] + ([prev[1]] if prev is not None else [])
    outs = [(d, F32)] + ([(d, BF16)] if prev is not None else [])
    res_o, res_a = _matmul_rows(name, a, b, mode, 512, tk, fn, rows, consts, outs, [(1, d)] * (3 + (prev is not None)), **kw)
    return (*res_o, *res_a)


def _rope_tables(pos_col, inv_row):
    def fn(pos, inv):
        ang = pos.astype(F32) * inv
        e = lax.broadcasted_iota(jnp.int32, (1, 128), 1) % HEAD_DIM
        cos, sin = jnp.cos(ang), jnp.sin(ang)
        half = ROT_DIM // 2
        return [jnp.where(e < ROT_DIM, cos, 1.0), jnp.where(e < half, -sin, 0.0),
                jnp.where((e >= half) & (e < ROT_DIM), sin, 0.0)], []

    (c, sa, sb), _ = _rowwise("rope_tables", fn, [pos_col], [inv_row], [(128, F32)] * 3, [], 512)
    return c, sa, sb


def _rot_fwd(t, c, sa, sb):
    n = t.shape[1]
    rep = n // 128
    c, sa, sb = (jnp.tile(u, (1, rep)) for u in (c, sa, sb))
    return t * c + pltpu.roll(t, n - ROT_DIM // 2, 1) * sa + pltpu.roll(t, ROT_DIM // 2, 1) * sb


def _rot_bwd(g, c, sa, sb):
    n = g.shape[1]
    rep = n // 128
    c, sa, sb = (jnp.tile(u, (1, rep)) for u in (c, sa, sb))
    return g * c + pltpu.roll(g * sa, ROT_DIM // 2, 1) + pltpu.roll(g * sb, n - ROT_DIM // 2, 1)


ATT_TQ = 128
ATT_TK = ATT_TQ + 2 * BAND


def _attn_specs(g, s_len):
    def blk(off):
        return pl.BlockSpec((s_len, 128), functools.partial(lambda hp, off: (0, off + hp), off=off))

    return blk(4 * g), blk(12 + 4 * g), blk(4 * g), blk(0)


def _attn_tile_geometry(t, d, l):
    nts = l // ATT_TQ
    r = t // nts
    ts = t % nts
    q0 = ts * ATT_TQ
    ws = jnp.clip(q0 - BAND, 0, l - ATT_TK)
    kind = jnp.where(ts == 0, 0, jnp.where(ts == nts - 1, 2, 1))
    if d == 1:
        return pl.ds(pl.multiple_of(q0, ATT_TQ), ATT_TQ), pl.ds(pl.multiple_of(ws, BAND), ATT_TK), kind
    return pl.ds(r + d * q0, ATT_TQ, stride=d), pl.ds(r + d * ws, ATT_TK, stride=d), kind


def _attn_fill_bias(bias_ref):
    iq = lax.broadcasted_iota(jnp.int32, (2 * ATT_TQ, 1), 0) % ATT_TQ
    ik = lax.broadcasted_iota(jnp.int32, (1, ATT_TK), 1)
    for i, off in enumerate((0, -BAND, -2 * BAND)):
        bias_ref[i] = jnp.where(jnp.abs(ik + off - iq) <= BAND, 0.0, NEG_BIG)


def _split_heads(t, in_h):
    zero = jnp.zeros_like(t)
    return jnp.concatenate([jnp.where(in_h[0], t, zero), jnp.where(in_h[1], t, zero)], axis=0)


def _attn_fwd(g, qk, v):
    s_len = qk.shape[0]
    d = DILATIONS[g]
    l = s_len // d
    assert l % ATT_TQ == 0 and l >= ATT_TK
    q_spec, k_spec, v_spec, o_spec = _attn_specs(g, s_len)
    scale = 1.0 / math.sqrt(HEAD_DIM)

    def body(q_ref, k_ref, v_ref, o_ref, lse_ref, bias_ref):
        lane = lax.broadcasted_iota(jnp.int32, (1, 128), 1)
        in_h = [lane < HEAD_DIM, lane >= HEAD_DIM]
        _attn_fill_bias(bias_ref)

        def tile(t, carry):
            rows, win, kind = _attn_tile_geometry(t, d, l)
            q = (q_ref[rows, :] * scale).astype(BF16)
            k = k_ref[win, :].astype(BF16)
            vv = v_ref[win, :].astype(BF16)
            s = _nt(_split_heads(q, in_h), k) + bias_ref[kind]
            m = jnp.max(s, axis=1, keepdims=True)
            p = jnp.exp(s - m)
            den = jnp.sum(p, axis=1, keepdims=True)
            out = _nn(p, vv) / den
            lse = m + jnp.log(den)
            o_ref[rows, :] = jnp.where(in_h[0], out[:ATT_TQ], out[ATT_TQ:])
            lse_ref[rows, :] = jnp.where(in_h[0], lse[:ATT_TQ], lse[ATT_TQ:])
            return carry

        lax.fori_loop(0, s_len // ATT_TQ, tile, 0, unroll=4)

    return pl.pallas_call(
        body, name=f"attn_fwd_g{g}", grid=(4,),
        out_shape=[jax.ShapeDtypeStruct((s_len, 512), F32)] * 2,
        in_specs=[q_spec, k_spec, v_spec], out_specs=[o_spec, o_spec],
        scratch_shapes=[pltpu.VMEM((3, 2 * ATT_TQ, ATT_TK), F32)],
        compiler_params=_params(("parallel",), VMEM_BIG),
    )(qk, qk, v)


def _attn_bwd(g, qk, v, o, lse, do, dlse):
    s_len = qk.shape[0]
    d = DILATIONS[g]
    l = s_len // d
    q_spec, k_spec, v_spec, o_spec = _attn_specs(g, s_len)
    scale = 1.0 / math.sqrt(HEAD_DIM)

    def body(q_ref, k_ref, v_ref, o_ref, lse_ref, do_ref, dlse_ref, dq_ref, dk_ref, dv_ref, bias_ref):
        lane = lax.broadcasted_iota(jnp.int32, (1, 128), 1)
        in_h = [lane < HEAD_DIM, lane >= HEAD_DIM]
        dk_ref[...] = jnp.zeros_like(dk_ref)
        dv_ref[...] = jnp.zeros_like(dv_ref)
        _attn_fill_bias(bias_ref)

        def tile(t, carry):
            rows, win, kind = _attn_tile_geometry(t, d, l)
            k, vv = k_ref[win, :].astype(BF16), v_ref[win, :].astype(BF16)
            dout, lse_t, dlse_t = do_ref[rows, :], lse_ref[rows, :], dlse_ref[rows, :]
            od = dout * o_ref[rows, :]
            q2 = _split_heads((q_ref[rows, :] * scale).astype(BF16), in_h)
            do2 = _split_heads(dout.astype(BF16), in_h)
            head_col = lambda a: jnp.concatenate([a[:, 0:1], a[:, HEAD_DIM:HEAD_DIM + 1]], axis=0)
            delta = jnp.concatenate([jnp.sum(jnp.where(m, od, 0.0), axis=1, keepdims=True) for m in in_h], axis=0)
            p = jnp.exp(_nt(q2, k) + bias_ref[kind] - head_col(lse_t))
            ds = (p * (_nt(do2, vv) - delta + head_col(dlse_t))).astype(BF16)
            dq2 = _nn(ds, k) * scale
            dq_ref[rows, :] = jnp.where(in_h[0], dq2[:ATT_TQ], dq2[ATT_TQ:])
            dk_ref[win, :] += _tn(ds, q2)
            dv_ref[win, :] += _tn(p, do2)
            return carry

        lax.fori_loop(0, s_len // ATT_TQ, tile, 0, unroll=4)

    return pl.pallas_call(
        body, name=f"attn_bwd_g{g}", grid=(4,),
        out_shape=[jax.ShapeDtypeStruct((s_len, 512), F32)] * 3,
        in_specs=[q_spec, k_spec, v_spec, o_spec, o_spec, o_spec, o_spec], out_specs=[o_spec] * 3,
        scratch_shapes=[pltpu.VMEM((3, 2 * ATT_TQ, ATT_TK), F32)],
        compiler_params=_params(("parallel",), VMEM_BIG),
    )(qk, qk, v, o, lse, do, dlse)


def _mix_weights(ls):
    mx = jnp.maximum(jnp.maximum(ls[0], ls[1]), ls[2])
    es = [jnp.exp(x - mx) for x in ls]
    tot = es[0] + es[1] + es[2]
    return [e / tot for e in es]


def _attn_out(os_, lses, z, x, gate, w_out):
    s_len, dm = x.shape
    tm = 256
    wdt = 512

    def body(o0, o1, o2, l0, l1, l2, z_ref, x_ref, g_ref, w_ref, a_ref, y_ref, x1_ref):
        alphas = _mix_weights([l0[...], l1[...], l2[...]])
        y = jnp.zeros((tm, dm), F32)
        for g, o_ref in enumerate((o0, o1, o2)):
            a_g = (o_ref[...] * alphas[g] * _silu(z_ref[:, g * wdt:(g + 1) * wdt])).astype(BF16)
            a_ref[:, g * wdt:(g + 1) * wdt] = a_g
            y = y + _nn(a_g, w_ref[g * wdt:(g + 1) * wdt, :])
        y_ref[...] = y
        x1_ref[...] = x_ref[...] + g_ref[...] * y

    row = lambda c: pl.BlockSpec((tm, c), lambda i: (i, 0))
    return pl.pallas_call(
        body, name="attn_out", grid=(s_len // tm,),
        out_shape=[jax.ShapeDtypeStruct((s_len, 3 * wdt), BF16), jax.ShapeDtypeStruct((s_len, dm), F32),
                   jax.ShapeDtypeStruct((s_len, dm), F32)],
        in_specs=[row(wdt)] * 6 + [row(3 * wdt), row(dm), pl.BlockSpec((1, dm), lambda i: (0, 0)),
                                   pl.BlockSpec(w_out.shape, lambda i: (0, 0))],
        out_specs=[row(3 * wdt), row(dm), row(dm)],
        compiler_params=_params(("parallel",), VMEM_BIG),
    )(*os_, *lses, z, x, gate, w_out)


def _mix_bwd(dy, w_out, os_, lses, z):
    wdt = 512

    def fn(da, o0, o1, o2, l0, l1, l2, z):
        os_t, ls = [o0, o1, o2], [l0, l1, l2]
        alphas = _mix_weights(ls)
        hi = lax.broadcasted_iota(jnp.int32, (wdt, wdt), 0) // HEAD_DIM
        hj = lax.broadcasted_iota(jnp.int32, (wdt, wdt), 1) // HEAD_DIM
        seg = (hi == hj).astype(F32)
        dos, dal, dzs = [], [], []
        for g in range(3):
            zg = z[:, g * wdt:(g + 1) * wdt]
            sig = jax.nn.sigmoid(zg)
            dag = da[:, g * wdt:(g + 1) * wdt]
            dmix = dag * zg * sig
            dzs.append(dag * os_t[g] * alphas[g] * (sig * (1.0 + zg * (1.0 - sig))))
            dos.append(dmix * alphas[g])
            dal.append(_hnn(dmix * os_t[g], seg))
        mean = alphas[0] * dal[0] + alphas[1] * dal[1] + alphas[2] * dal[2]
        dls = [alphas[g] * (dal[g] - mean) for g in range(3)]
        return dos + dls + [jnp.concatenate(dzs, axis=1)], []

    outs, _ = _matmul_rows("attn_out_dx_mix_bwd", dy, w_out, "nt", 256, dy.shape[1], fn, [*os_, *lses, z], [],
                           [(wdt, F32)] * 6 + [(3 * wdt, BF16)], [])
    return outs[:3], outs[3:6], outs[6]


def _rot_pack_bwd(dqs, dks, dvs, tabs):
    wdt = 512

    def fn(*args):
        grads, (c, sa, sb) = args[:9], args[9:]
        cols = [_rot_bwd(gq, c, sa, sb) for gq in grads[:6]] + list(grads[6:])
        return [jnp.concatenate(cols, axis=1)], []

    (out,), _ = _rowwise("rot_pack_bwd", fn, [*dqs, *dks, *dvs, *tabs], [], [(9 * wdt, BF16)], [], 256)
    return out


CONV_CB = 128
CONV_R = 256
CONV_PAD = 8


def _conv_taps(buf, base, off, sign):
    return [buf[pl.ds(base + off + sign * j, CONV_R), :] for j in range(CONV_WIDTH)]


def _conv_tap_sum(taps, w):
    acc = None
    for j, t in enumerate(taps):
        term = t * w[j:j + 1, :]
        acc = term if acc is None else acc + term
    return acc


def _conv_fwd(xpre, cw, cb):
    s_len, ch = xpre.shape
    nchunk = s_len // CONV_R

    def body(x_ref, w_ref, b_ref, o_ref, xp):
        zero = jnp.zeros((CONV_PAD, CONV_CB), F32)
        xp[0:CONV_PAD, :] = zero
        xp[s_len + CONV_PAD:s_len + 2 * CONV_PAD, :] = zero

        def fill(ci, carry):
            base = pl.multiple_of(ci * CONV_R, CONV_R)
            xp[pl.ds(base + CONV_PAD, CONV_R), :] = x_ref[pl.ds(base, CONV_R), :]
            return carry

        lax.fori_loop(0, nchunk, fill, 0)
        w = w_ref[...]
        b = b_ref[...]

        def chunk(ci, carry):
            base = pl.multiple_of(ci * CONV_R, CONV_R)
            u = _conv_tap_sum(_conv_taps(xp, base, CONV_PAD - CONV_WIDTH // 2, 1), w) + b
            o_ref[pl.ds(base, CONV_R), :] = _silu(u)
            return carry

        lax.fori_loop(0, nchunk, chunk, 0)

    col = lambda r: pl.BlockSpec((r, CONV_CB), lambda j: (0, j))
    return pl.pallas_call(
        body, name="conv_fwd", grid=(ch // CONV_CB,), out_shape=jax.ShapeDtypeStruct((s_len, ch), F32),
        in_specs=[col(s_len), col(CONV_WIDTH), col(1)], out_specs=col(s_len),
        scratch_shapes=[pltpu.VMEM((s_len + 2 * CONV_PAD, CONV_CB), F32)],
        compiler_params=_params(("parallel",), VMEM_BIG),
    )(xpre, cw, cb)


def _conv_bwd(xpre, da, db, cw, cb):
    s_len, ch = xpre.shape
    nchunk = s_len // CONV_R
    half = CONV_WIDTH // 2

    def body(x_ref, da_ref, db_ref, w_ref, b_ref, dx_ref, gw_ref, gb_ref, xp, dcp):
        zero = jnp.zeros((CONV_PAD, CONV_CB), F32)
        for buf in (xp, dcp):
            buf[0:CONV_PAD, :] = zero
            buf[s_len + CONV_PAD:s_len + 2 * CONV_PAD, :] = zero

        def fill(ci, carry):
            base = pl.multiple_of(ci * CONV_R, CONV_R)
            xp[pl.ds(base + CONV_PAD, CONV_R), :] = x_ref[pl.ds(base, CONV_R), :]
            return carry

        lax.fori_loop(0, nchunk, fill, 0)
        w = w_ref[...]
        b = b_ref[...]

        def first(ci, carry):
            base = pl.multiple_of(ci * CONV_R, CONV_R)
            taps = _conv_taps(xp, base, CONV_PAD - half, 1)
            u = _conv_tap_sum(taps, w) + b
            sig = jax.nn.sigmoid(u)
            dc = (da_ref[pl.ds(base, CONV_R), :] + db_ref[pl.ds(base, CONV_R), :]) * (sig * (1.0 + u * (1.0 - sig)))
            dcp[pl.ds(base + CONV_PAD, CONV_R), :] = dc
            gb = carry[0] + jnp.sum(dc, axis=0, keepdims=True)
            gws = [carry[1 + j] + jnp.sum(dc * taps[j], axis=0, keepdims=True) for j in range(CONV_WIDTH)]
            return (gb, *gws)

        z1 = jnp.zeros((1, CONV_CB), F32)
        sums = lax.fori_loop(0, nchunk, first, (z1,) * (1 + CONV_WIDTH))
        gb_ref[...] = sums[0]
        for j in range(CONV_WIDTH):
            gw_ref[j:j + 1, :] = sums[1 + j]

        def second(ci, carry):
            base = pl.multiple_of(ci * CONV_R, CONV_R)
            dx_ref[pl.ds(base, CONV_R), :] = _conv_tap_sum(_conv_taps(dcp, base, CONV_PAD + half, -1), w).astype(dx_ref.dtype)
            return carry

        lax.fori_loop(0, nchunk, second, 0)

    col = lambda r: pl.BlockSpec((r, CONV_CB), lambda j: (0, j))
    return pl.pallas_call(
        body, name="conv_bwd", grid=(ch // CONV_CB,),
        out_shape=[jax.ShapeDtypeStruct((s_len, ch), BF16), jax.ShapeDtypeStruct((CONV_WIDTH, ch), F32),
                   jax.ShapeDtypeStruct((1, ch), F32)],
        in_specs=[col(s_len), col(s_len), col(s_len), col(CONV_WIDTH), col(1)],
        out_specs=[col(s_len), col(CONV_WIDTH), col(1)],
        scratch_shapes=[pltpu.VMEM((s_len + 2 * CONV_PAD, CONV_CB), F32)] * 2,
        compiler_params=_params(("parallel",), VMEM_BIG),
    )(xpre, da, db, cw, cb)


SSD_GW = 256
SSD_N = 128


def _bf16_parts(x, n):
    parts, rest = [], x
    for _ in range(n):
        p = rest.astype(BF16)
        parts.append(p)
        rest = rest - p.astype(F32)
    return parts


@jax.custom_vjp
def _expand(x, e):
    eb = e.astype(BF16)
    return sum(_dg(p, eb, 1, 0) for p in _bf16_parts(x, 2))


def _expand_fwd(x, e):
    return _expand(x, e), e


def _expand_bwd(e, g):
    eb = e.astype(BF16)
    return sum(_dg(p, eb, 1, 1) for p in _bf16_parts(g, 2)), jnp.zeros_like(e)


_expand.defvjp(_expand_fwd, _expand_bwd)


@jax.custom_vjp
def _running_sum(tri, x):
    tb = tri.astype(BF16)
    return sum(_dg(tb, p, 1, 0) for p in _bf16_parts(x, 3))


def _running_sum_fwd(tri, x):
    return _running_sum(tri, x), tri


def _running_sum_bwd(tri, g):
    tb = tri.astype(BF16)
    return jnp.zeros_like(tri), sum(_dg(tb, p, 0, 0) for p in _bf16_parts(g, 3))


_running_sum.defvjp(_running_sum_fwd, _running_sum_bwd)


def _ssd_mask(dirn):
    ri = lax.broadcasted_iota(jnp.int32, (CHUNK, CHUNK), 0)
    cj = lax.broadcasted_iota(jnp.int32, (CHUNK, CHUNK), 1)
    return (cj <= ri) if dirn == 0 else (cj >= ri)


def _ssd_rowsel(dirn):
    last = CHUNK - 1 if dirn == 0 else 0
    return (lax.broadcasted_iota(jnp.int32, (CHUNK, 1), 0) == last).astype(F32)


def _ssd_chunk_pre(dirn):
    nh = 2 * SSD_HEADS

    def f(dt, alog):
        da = dt * (-jnp.exp(alog))
        cum = _running_sum(_ssd_mask(dirn).astype(F32), da)
        tot = jnp.sum(cum * _ssd_rowsel(dirn), axis=0, keepdims=True)
        hh = lax.broadcasted_iota(jnp.int32, (nh, SSD_HEADS * HEAD_DIM), 0)
        jj = lax.broadcasted_iota(jnp.int32, (nh, SSD_HEADS * HEAD_DIM), 1)
        expand = (hh == dirn * SSD_HEADS + jj // HEAD_DIM).astype(F32)
        return cum, cum.T, _expand(dt, expand), _expand(jnp.exp(tot - cum), expand), _expand(jnp.exp(cum), expand)

    return f


def _ssd_group_fn(g, dirn, stacked):
    nh = 2 * SSD_HEADS

    def f(xs, bm, cm, st, cum, cum_t, dt_e, w_e, ce_e):
        mask = _ssd_mask(dirn)
        xdt = xs * dt_e
        cd_e = jnp.sum(ce_e * _ssd_rowsel(dirn), axis=0, keepdims=True)
        cb = _bnt(cm, bm)
        lane_head = lax.broadcasted_iota(jnp.int32, (1, SSD_GW), 1) // HEAD_DIM
        y = _bnn(cm, st) * ce_e
        decayed, inputs = [], []
        for j in range(4):
            hidx = dirn * SSD_HEADS + 4 * g + j
            col = jnp.sum(cum * (lax.broadcasted_iota(jnp.int32, (1, nh), 1) == hidx).astype(F32), axis=1, keepdims=True)
            row = jnp.sum(cum_t * (lax.broadcasted_iota(jnp.int32, (nh, 1), 0) == hidx).astype(F32), axis=0, keepdims=True)
            dec = cb * jnp.exp(jnp.where(mask, col - row, NEG_BIG))
            head = (lane_head == j).astype(F32)
            if stacked:
                decayed.append(dec)
                inputs.append(xdt * head)
            else:
                y = y + _bnn(dec, xdt) * head
        if stacked:
            y = y + _bnn(jnp.concatenate(decayed, axis=1), jnp.concatenate(inputs, axis=0))
        st_out = st * cd_e + _btn(bm, xdt * w_e)
        return y, st_out

    return f


def _ssd_in_specs(kk):
    ln = CHUNK
    return [pl.BlockSpec((ln, 2048), lambda i: (kk(i), 0)),
            pl.BlockSpec((ln, 1024), lambda i: (kk(i), 2)),
            pl.BlockSpec((ln, 1024), lambda i: (kk(i), 3)),
            pl.BlockSpec((ln, 2 * SSD_HEADS), lambda i: (kk(i), 0)),
            pl.BlockSpec((1, 2 * SSD_HEADS), lambda i: (0, 0))]


def _ssd_fwd(xbc, dt, alog, dirn):
    s_len = xbc.shape[0]
    nc = s_len // CHUNK
    kk = (lambda i: i) if dirn == 0 else (lambda i: nc - 1 - i)

    def body(x_ref, b_ref, c_ref, dt_ref, al_ref, y_ref, sts_ref, st):
        @pl.when(pl.program_id(0) == 0)
        def _():
            st[...] = jnp.zeros_like(st)

        sts_ref[0] = st[...]
        cum, cum_t, dt_e, w_e, ce_e = _ssd_chunk_pre(dirn)(dt_ref[...], al_ref[...])
        for g in range(SSD_GROUPS):
            xc = slice(g * SSD_GW, (g + 1) * SSD_GW)
            gc = slice(g * SSD_N, (g + 1) * SSD_N)
            y, st_new = _ssd_group_fn(g, dirn, True)(x_ref[:, xc], b_ref[:, gc], c_ref[:, gc], st[:, xc], cum, cum_t,
                                               dt_e[:, xc], w_e[:, xc], ce_e[:, xc])
            y_ref[:, xc] = y
            st[:, xc] = st_new

    return pl.pallas_call(
        body, name=f"ssd_fwd_d{dirn}", grid=(nc,),
        out_shape=[jax.ShapeDtypeStruct((s_len, 2048), F32), jax.ShapeDtypeStruct((nc, SSD_N, 2048), F32)],
        in_specs=_ssd_in_specs(kk),
        out_specs=[pl.BlockSpec((CHUNK, 2048), lambda i: (kk(i), 0)),
                   pl.BlockSpec((1, SSD_N, 2048), lambda i: (kk(i), 0, 0))],
        scratch_shapes=[pltpu.VMEM((SSD_N, 2048), F32)],
        compiler_params=_params(("arbitrary",), VMEM_BIG),
    )(xbc, xbc, xbc, dt, alog)


def _ssd_bwd(xbc, dt, alog, states, dy, d_e, dirn):
    s_len = xbc.shape[0]
    nc = s_len // CHUNK
    kk = (lambda i: nc - 1 - i) if dirn == 0 else (lambda i: i)

    def body(x_ref, b_ref, c_ref, dt_ref, al_ref, sts_ref, dy_ref, de_ref, dx_ref, ddt_ref, dal_ref, dst):
        @pl.when(pl.program_id(0) == 0)
        def _():
            dst[...] = jnp.zeros_like(dst)
            dal_ref[...] = jnp.zeros_like(dal_ref)

        (cum, cum_t, dt_e, w_e, ce_e), pre_vjp = jax.vjp(_ssd_chunk_pre(dirn), dt_ref[...], al_ref[...])
        dcum = jnp.zeros_like(cum)
        dcum_t = jnp.zeros_like(cum_t)
        d_dt_e, d_w_e, d_ce_e = [], [], []
        for g in range(SSD_GROUPS):
            xc = slice(g * SSD_GW, (g + 1) * SSD_GW)
            gc = slice(g * SSD_N, (g + 1) * SSD_N)
            _, vjp = jax.vjp(_ssd_group_fn(g, dirn, False), x_ref[:, xc], b_ref[:, gc], c_ref[:, gc], sts_ref[0, :, xc], cum, cum_t,
                             dt_e[:, xc], w_e[:, xc], ce_e[:, xc])
            dyg = dy_ref[:, xc]
            dxs, dbm, dcm, dst_g, dcum_g, dcum_t_g, ddte_g, dwe_g, dcee_g = vjp((dyg, dst[:, xc]))
            if dirn == 0:
                dxs = dxs + dyg * de_ref[:, xc]
            dx_ref[:, xc] = dxs
            dx_ref[:, 2048 + g * SSD_N:2048 + (g + 1) * SSD_N] = dbm
            dx_ref[:, 3072 + g * SSD_N:3072 + (g + 1) * SSD_N] = dcm
            dst[:, xc] = dst_g
            dcum = dcum + dcum_g
            dcum_t = dcum_t + dcum_t_g
            d_dt_e.append(ddte_g)
            d_w_e.append(dwe_g)
            d_ce_e.append(dcee_g)
        ddt, dal = pre_vjp((dcum, dcum_t, jnp.concatenate(d_dt_e, axis=1), jnp.concatenate(d_w_e, axis=1),
                            jnp.concatenate(d_ce_e, axis=1)))
        ddt_ref[...] = ddt
        dal_ref[...] += dal

    return pl.pallas_call(
        body, name=f"ssd_bwd_d{dirn}", grid=(nc,),
        out_shape=[jax.ShapeDtypeStruct((s_len, 4096), F32), jax.ShapeDtypeStruct((s_len, 2 * SSD_HEADS), F32),
                   jax.ShapeDtypeStruct((1, 2 * SSD_HEADS), F32)],
        in_specs=_ssd_in_specs(kk) + [pl.BlockSpec((1, SSD_N, 2048), lambda i: (kk(i), 0, 0)),
                                      pl.BlockSpec((CHUNK, 2048), lambda i: (kk(i), 0)),
                                      pl.BlockSpec((1, 2048), lambda i: (0, 0))],
        out_specs=[pl.BlockSpec((CHUNK, 4096), lambda i: (kk(i), 0)),
                   pl.BlockSpec((CHUNK, 2 * SSD_HEADS), lambda i: (kk(i), 0)),
                   pl.BlockSpec((1, 2 * SSD_HEADS), lambda i: (0, 0))],
        scratch_shapes=[pltpu.VMEM((SSD_N, 2048), F32)],
        compiler_params=_params(("arbitrary",), VMEM_BIG),
    )(xbc, xbc, xbc, dt, alog, states, dy, d_e)


def _gate_norm_fn(yf, yb, xs, z, d_e, nw):
    yg = (yf + yb + xs * d_e) * _silu(z)
    return yg * lax.rsqrt(jnp.mean(yg * yg, axis=-1, keepdims=True) + NORM_EPS) * nw


def _gate_norm_fwd(yf, yb, xbc, z, d_e, nw):
    (u,), _ = _rowwise("ssd_gate_norm", lambda *a: ([_gate_norm_fn(*a)], []),
                       [yf, yb, (xbc, 2048, 0), z], [d_e, nw], [(2048, BF16)], [], 256)
    return u


def _gate_norm_bwd(dy, w_out, yf, yb, xbc, z, d_e, nw):
    def fn(du, yf, yb, xs, z, d_e, nw):
        sig = jax.nn.sigmoid(z)
        gate = z * sig
        ysum = yf + yb + xs * d_e
        yg = ysum * gate
        r = lax.rsqrt(jnp.mean(yg * yg, axis=-1, keepdims=True) + NORM_EPS)
        t = du * nw
        dyg = t * r - yg * (jnp.mean(t * yg, axis=-1, keepdims=True) * (r * r * r))
        dys = dyg * gate
        dz = dyg * ysum * (sig * (1.0 + z * (1.0 - sig)))
        dnw = jnp.sum(du * yg * r, axis=0, keepdims=True)
        dde = jnp.sum(dys * xs, axis=0, keepdims=True)
        hh = lax.broadcasted_iota(jnp.int32, (2048, SSD_HEADS), 0) // HEAD_DIM
        jj = lax.broadcasted_iota(jnp.int32, (2048, SSD_HEADS), 1)
        return [dys, dz], [dnw, _hnn(jnp.broadcast_to(dde, (8, 2048)), (hh == jj).astype(F32))[0:1]]

    (dys, dz), (g_nw, g_d) = _matmul_rows("ssd_out_dx_gate_norm_bwd", dy, w_out, "nt", 256, dy.shape[1], fn,
                                          [yf, yb, (xbc, 2048, 0), z], [d_e, nw], [(2048, F32), (2048, BF16)],
                                          [(1, 2048), (1, SSD_HEADS)])
    return dys, dz, g_nw, g_d


def _loss_bwd(u, w_out, x1, tgt, gate, fnw):
    dm = x1.shape[1]

    def fn(y1, x1, tgt, gate, fnw):
        def head(x2, fnw):
            yf = (x2 * lax.rsqrt(jnp.mean(x2 * x2, axis=-1, keepdims=True) + NORM_EPS)) * fnw
            err = yf - tgt
            return 0.5 * jnp.sum(jnp.mean(err * err, axis=-1, keepdims=True), axis=0, keepdims=True)

        x2 = x1 + gate * y1
        loss, vjp = jax.vjp(head, x2, fnw)
        dx2, dfnw = vjp(jnp.ones((1, 1), F32))
        return [dx2, gate * dx2], [dfnw, jnp.sum(dx2 * y1, axis=0, keepdims=True), jnp.broadcast_to(loss, (1, 128))]

    (dx2, dy1), (g_fnw, dgate, loss) = _matmul_rows("ssd_out_loss", u, w_out, "nn", 512, 1024, fn, [x1, tgt], [gate, fnw],
                                                    [(dm, F32), (dm, BF16)], [(1, dm), (1, dm), (1, 128)])
    return dx2, dy1, g_fnw, dgate, loss


def _softplus_fwd(dt_raw, bias):
    (dt,), _ = _rowwise("dt_softplus", lambda r, b: ([jax.nn.softplus(r + b)], []), [dt_raw], [bias],
                        [(dt_raw.shape[1], F32)], [], 512)
    return dt


def _softplus_bwd(ddt_f, ddt_b, dt_raw, bias):
    def fn(df, db, r, b):
        g = (df + db) * jax.nn.sigmoid(r + b)
        return [g], [jnp.sum(g, axis=0, keepdims=True)]

    w = dt_raw.shape[1]
    (g,), (gb,) = _rowwise("dt_softplus_bwd", fn, [ddt_f, ddt_b, dt_raw], [bias], [(w, BF16)], [(1, w)], 512)
    return g, gb


def _whole(a):
    nd = len(a.shape)
    return pl.BlockSpec(a.shape, lambda *_: (0,) * nd)


def _mod_part(c_all, mod_w):
    nl, _, ncol = mod_w.shape
    nb = c_all.shape[0]

    def body(c_ref, w_ref, o_ref):
        cond = _silu(c_ref[...])
        for i in range(nl):
            o_ref[i * nb:(i + 1) * nb, :] = _nn(cond, w_ref[i])

    return pl.pallas_call(body, name="mod_part", out_shape=jax.ShapeDtypeStruct((nl * nb, ncol), F32),
                          compiler_params=_params(None, VMEM_BIG))(c_all, mod_w)


def _mod_finish(mod_nb, mod_b, norm_w, tokens):
    nl, dm = norm_w.shape

    def body(a_ref, b_ref, nw_ref, *rest):
        tok_refs, o_refs = rest[:len(tokens)], rest[len(tokens):]
        tok = sum(t[0:1, 0:1] for t in tok_refs)
        for i in range(nl):
            for k in range(3):
                cols = slice(k * dm, (k + 1) * dm)
                o_refs[4 * i + k][...] = a_ref[i:i + 1, cols] + b_ref[i:i + 1, cols]
            o_refs[4 * i + 3][...] = nw_ref[i:i + 1, :] + tok

    rows = pl.pallas_call(body, name="mod_finish", out_shape=[jax.ShapeDtypeStruct((1, dm), F32)] * (4 * nl))(
        mod_nb, mod_b, norm_w, *tokens)
    return [rows[4 * i:4 * i + 4] for i in range(nl)]


def _mod_grad(c_all, dmod_sh):
    nl, nb, ncol = dmod_sh.shape
    dm = c_all.shape[1]

    def body(c_ref, d_ref, o_ref):
        cond = _silu(c_ref[...])
        for i in range(nl):
            o_ref[i] = _tn(cond, d_ref[i])

    return pl.pallas_call(body, name="mod_grad", out_shape=jax.ShapeDtypeStruct((nl, dm, ncol), F32),
                          compiler_params=_params(None, VMEM_BIG))(c_all, dmod_sh)


PACK_ROWS = 16
PACK_COLS = 1024


def _pack_small(rows, b64, a64s, d32, extra):
    nr, na = len(rows), len(a64s)

    def body(*refs):
        o_ref = refs[-1]
        o_ref[...] = jnp.zeros_like(o_ref)
        for i in range(nr):
            o_ref[i:i + 1, :] = refs[i][...]
        b_ref, a_refs, d_ref, e_ref = refs[nr], refs[nr + 1:nr + 1 + na], refs[nr + 1 + na], refs[nr + 2 + na]
        o_ref[nr:nr + 1, 0:64] = b_ref[...]
        o_ref[nr:nr + 1, 64:128] = sum(a[...] for a in a_refs)
        o_ref[nr:nr + 1, 128:160] = d_ref[...]
        o_ref[nr:nr + 1, 256:384] = e_ref[...]

    return pl.pallas_call(body, name="pack_small", out_shape=jax.ShapeDtypeStruct((PACK_ROWS, PACK_COLS), F32))(
        *rows, b64, *a64s, d32, extra)


def _pack_ssd_small(cw, cb, nw):
    def body(cw_ref, cb_ref, nw_ref, o_ref):
        o_ref[...] = jnp.zeros_like(o_ref)
        o_ref[0:5, :] = cw_ref[...]
        o_ref[5:6, :] = cb_ref[...]
        o_ref[6:7, 0:256] = nw_ref[...]

    return pl.pallas_call(body, name="pack_ssd_small", out_shape=jax.ShapeDtypeStruct((8, 512), F32))(cw, cb, nw)


def _sum_parts(p_ref):
    g = p_ref[0].astype(F32)
    for s in range(1, p_ref.shape[0]):
        g = g + p_ref[s].astype(F32)
    return g


def _adam_update(w, g, m, v):
    m2 = ADAM_B1 * m + (1.0 - ADAM_B1) * g
    v2 = ADAM_B2 * v + (1.0 - ADAM_B2) * (g * g)
    m_hat = m2 / (1.0 - ADAM_B1 ** ADAM_STEP)
    v_hat = v2 / (1.0 - ADAM_B2 ** ADAM_STEP)
    return -ADAM_LR * (m_hat / (jnp.sqrt(v_hat) + ADAM_EPS) + ADAM_WD * w), m2, v2


def _adamw_windows(name, parts, params, windows, extra=None):
    n = len(params)

    def body(p_ref, *rest):
        ins, outs = rest[:3 * n], rest[3 * n:]
        g = _sum_parts(p_ref)
        for pi, rows, cols, idx in windows:
            w_ref, m_ref, v_ref = ins[3 * pi:3 * pi + 3]
            gw = g[rows, cols]
            dw, m2, v2 = _adam_update(w_ref[idx], gw, m_ref[idx], v_ref[idx])
            for o_ref, val in zip(outs[4 * pi:4 * pi + 4], (gw, dw, m2, v2), strict=True):
                o_ref[idx] = val
        if extra is not None:
            outs[4 * n][...] = g[extra[0], extra[1]]

    out_shape = [jax.ShapeDtypeStruct(w.shape, F32) for (w, _, _) in params for _ in range(4)]
    if extra is not None:
        out_shape.append(jax.ShapeDtypeStruct((extra[0].stop - extra[0].start, extra[1].stop - extra[1].start), F32))
    res = pl.pallas_call(body, name=name, out_shape=out_shape)(parts, *[a for p in params for a in p])
    return [res[4 * i:4 * i + 4] for i in range(n)] + ([res[4 * n]] if extra is not None else [])


def _adamw(name, w, parts, m, v, tr, tc=None):
    r_, c_ = w.shape
    p_ = parts.shape[0]
    tr = min(tr, r_)
    tc = c_ if tc is None else tc
    assert r_ % tr == 0 and c_ % tc == 0

    def body(w_ref, p_ref, m_ref, v_ref, g_ref, d_ref, m2_ref, v2_ref):
        g = _sum_parts(p_ref)
        g_ref[...] = g
        d_ref[...], m2_ref[...], v2_ref[...] = _adam_update(w_ref[...], g, m_ref[...], v_ref[...])

    blk = pl.BlockSpec((tr, tc), lambda i, j: (i, j))
    return pl.pallas_call(
        body, name=name, grid=(r_ // tr, c_ // tc), out_shape=[jax.ShapeDtypeStruct((r_, c_), F32)] * 4,
        in_specs=[blk, pl.BlockSpec((p_, tr, tc), lambda i, j: (0, i, j)), blk, blk], out_specs=[blk] * 4,
        compiler_params=_params(("parallel", "parallel"), VMEM_BIG),
    )(w, parts, m, v)


def _dev_index(p):
    return 4 * p[0] + 2 * p[1] + p[2]


def _all_gather(name, xs):
    n = len(xs)
    hbm = pl.BlockSpec(memory_space=pl.ANY)

    def body(*refs):
        x_refs, o_refs = refs[:n], refs[n:2 * n]
        send_sems, recv_sems, local_sems = refs[2 * n:]
        x, y, c = lax.axis_index("x"), lax.axis_index("y"), lax.axis_index("c")
        me, sibling = (x, y, c), (x, y, 1 - c)
        chips = [(1 - x, y), (x, 1 - y), (1 - x, 1 - y)]

        def copy(a, k, block, to, src=None):
            dst = o_refs[a].at[_dev_index(block)]
            return pltpu.make_async_remote_copy(
                src_ref=dst if src is None else src, dst_ref=dst, send_sem=send_sems.at[a, k],
                recv_sem=recv_sems.at[a, k], device_id=to, device_id_type=MESH)

        mine = [pltpu.make_async_copy(x_refs[a], o_refs[a].at[_dev_index(me)], local_sems.at[a]) for a in range(n)]
        for cp in mine:
            cp.start()
        first = []
        for a in range(n):
            first.append(copy(a, 0, me, sibling, src=x_refs[a]))
            first += [copy(a, 1 + j, me, (*chip, c), src=x_refs[a]) for j, chip in enumerate(chips)]
        for cp in first:
            cp.start()
        passed = []
        for j, chip in enumerate(chips):
            for a in range(n):
                copy(a, 1 + j, (*chip, c), me).wait_recv()
                cp = copy(a, 4 + j, (*chip, c), sibling)
                cp.start()
                passed.append(cp)
        for a in range(n):
            copy(a, 0, sibling, me).wait_recv()
            for j, chip in enumerate(chips):
                copy(a, 4 + j, (*chip, 1 - c), me).wait_recv()
        for cp in first + passed:
            cp.wait_send()
        for cp in mine:
            cp.wait()

    return pl.pallas_call(
        body, name=name, out_shape=[jax.ShapeDtypeStruct((NDEV, *x.shape), x.dtype) for x in xs],
        in_specs=[hbm] * n, out_specs=[hbm] * n,
        scratch_shapes=[pltpu.SemaphoreType.DMA((n, 7)), pltpu.SemaphoreType.DMA((n, 7)), pltpu.SemaphoreType.DMA((n,))],
    )(*xs)


_HBM = pl.BlockSpec(memory_space=pltpu.HBM)
_SEM = pl.BlockSpec(memory_space=pltpu.SEMAPHORE)
_EFFECT = pltpu.SideEffectType.DATAFLOW_SIDE_EFFECTING


def _mesh_position():
    return lax.axis_index("x"), lax.axis_index("y"), lax.axis_index("c")


def _peers(me):
    return [(k, tuple(1 - v if (k >> b) & 1 else v for v, b in zip(me, (2, 1, 0)))) for k in range(1, NDEV)]


EXCHANGE_COPIES = {"gather": NDEV - 1, "scatter": NDEV - 1, "pair": 4, "chips": 3}
NCHIP = NDEV // 2


def _landing_zones(name, xs, mode):
    x_, y_, c_ = _mesh_position()
    mine = (2 * x_ + y_ if mode == "chips" else _dev_index((x_, y_, c_))).astype(jnp.int32).reshape(1)
    lands = []
    for a, x in enumerate(xs):
        rows, cols = x.shape[-2:]
        if mode == "pair":
            lands.append(lax.empty((NCHIP, rows, cols), x.dtype))
            continue
        tr = 256 if rows % 256 == 0 else rows

        def body(me_ref, x_ref, o_ref):
            o_ref[...] = x_ref[...]

        if mode == "gather":
            in_spec = pl.BlockSpec((tr, cols), lambda i, me_ref: (i, 0))
        else:
            in_spec = pl.BlockSpec((None, tr, cols), lambda i, me_ref: (me_ref[0], i, 0))
        lands.append(pl.pallas_call(
            body, name=f"{name}_{a}",
            out_shape=jax.ShapeDtypeStruct((NCHIP if mode == "chips" else NDEV, rows, cols), x.dtype),
            grid_spec=pltpu.PrefetchScalarGridSpec(
                num_scalar_prefetch=1, grid=(rows // tr,), in_specs=[in_spec],
                out_specs=pl.BlockSpec((None, tr, cols), lambda i, me_ref: (me_ref[0], i, 0))),
            compiler_params=_params(("arbitrary",)),
        )(mine, x))
    return lands


def _exchange_copies(x_refs, land_refs, send_sems, recv_sems, mode):
    x_, y_, c_ = me = _mesh_position()
    per_array = EXCHANGE_COPIES[mode]
    out = []

    def add(a, k, src, dst, peer):
        sem = a * per_array + k
        out.append(pltpu.make_async_remote_copy(src_ref=src, dst_ref=dst, send_sem=send_sems.at[sem], recv_sem=recv_sems.at[sem],
                                                device_id=peer, device_id_type=MESH))

    for a, (x_ref, land_ref) in enumerate(zip(x_refs, land_refs)):
        if mode in ("gather", "scatter"):
            for k, peer in _peers(me):
                add(a, k - 1, x_ref.at[_dev_index(peer)] if mode == "scatter" else x_ref, land_ref.at[_dev_index(me)], peer)
        elif mode == "pair":
            for chip in range(NCHIP):
                add(a, chip, x_ref.at[2 * chip + 1 - c_], land_ref.at[chip], (x_, y_, 1 - c_))
        else:
            for k in range(1, NCHIP):
                px, py = (1 - x_ if k & 2 else x_), (1 - y_ if k & 1 else y_)
                add(a, k - 1, x_ref.at[2 * px + py], land_ref.at[2 * x_ + y_], (px, py, c_))
    return out


def _exchange_start(name, xs, lands, mode, dep):
    n = len(xs)

    def body(*refs):
        x_refs, land_refs = refs[:n], refs[n:2 * n]
        send_sems, recv_sems = refs[2 * n + 1], refs[2 * n + 2]
        token = refs[-1]
        for cp in _exchange_copies(x_refs, land_refs, send_sems, recv_sems, mode):
            cp.start()
        token[...] = jnp.zeros_like(token)

    sems = pltpu.SemaphoreType.DMA((n * EXCHANGE_COPIES[mode],))
    res = pl.pallas_call(
        body, name=name,
        out_shape=(sems, sems, *[pltpu.HBM(a.shape, a.dtype) for a in (*xs, *lands)], jax.ShapeDtypeStruct((8, 128), F32)),
        in_specs=[_HBM] * (2 * n) + [pl.BlockSpec(memory_space=pl.ANY)],
        out_specs=(_SEM, _SEM, *[_HBM] * (2 * n), pl.BlockSpec(memory_space=pltpu.VMEM)),
        input_output_aliases={i: 2 + i for i in range(2 * n)},
        compiler_params=pltpu.CompilerParams(has_side_effects=_EFFECT),
    )(*[pltpu.with_memory_space_constraint(a, pltpu.HBM) for a in (*xs, *lands)], dep)
    return res[:-1], res[-1]


def _exchange_wait(name, handles, mode, after):
    send_sems, recv_sems = handles[0], handles[1]
    bufs = handles[2:]
    n = len(bufs) // 2

    def body(*refs):
        x_refs, land_refs = refs[:n], refs[n:2 * n]
        s_sems, r_sems = refs[2 * n], refs[2 * n + 1]
        for cp in _exchange_copies(x_refs, land_refs, s_sems, r_sems, mode):
            cp.wait_send()
            cp.wait_recv()

    res = pl.pallas_call(
        body, name=name, out_shape=tuple(pltpu.HBM(a.shape, a.dtype) for a in bufs),
        in_specs=[_HBM] * (2 * n) + [_SEM, _SEM, pl.BlockSpec(memory_space=pl.ANY)], out_specs=tuple([_HBM] * (2 * n)),
        input_output_aliases={i: i for i in range(2 * n)},
        compiler_params=pltpu.CompilerParams(has_side_effects=_EFFECT),
    )(*bufs, send_sems, recv_sems, after)
    return res[n:]


def _pair_sum(name, x, from_sibling):
    _, rows, cols = x.shape
    tr = 256 if rows % 256 == 0 else rows
    core = lax.axis_index("c").astype(jnp.int32).reshape(1)

    def body(c_ref, x_ref, s_ref, o_ref):
        o_ref[...] = (x_ref[...].astype(F32) + s_ref[...].astype(F32)).astype(o_ref.dtype)

    return pl.pallas_call(
        body, name=name, out_shape=jax.ShapeDtypeStruct((NCHIP, rows, cols), x.dtype),
        grid_spec=pltpu.PrefetchScalarGridSpec(
            num_scalar_prefetch=1, grid=(NCHIP, rows // tr),
            in_specs=[pl.BlockSpec((None, tr, cols), lambda j, i, c_ref: (2 * j + c_ref[0], i, 0)),
                      pl.BlockSpec((None, tr, cols), lambda j, i, c_ref: (j, i, 0))],
            out_specs=pl.BlockSpec((None, tr, cols), lambda j, i, c_ref: (j, i, 0))),
        compiler_params=_params(("parallel", "parallel")),
    )(core, x, from_sibling)


def kernel(x, c, positions, norm_w, mod_w, mod_b, attn_w_in, attn_w_out, ssd_w_in, ssd_conv_w, ssd_conv_b, ssd_dt_bias, ssd_a_log, ssd_d, ssd_norm_w, ssd_w_out, final_norm_w, loss_target, m_norm_w, m_mod_w, m_mod_b, m_attn_w_in, m_attn_w_out, m_ssd_w_in, m_ssd_conv_w, m_ssd_conv_b, m_ssd_dt_bias, m_ssd_a_log, m_ssd_d, m_ssd_norm_w, m_ssd_w_out, m_final_norm_w, v_norm_w, v_mod_w, v_mod_b, v_attn_w_in, v_attn_w_out, v_ssd_w_in, v_ssd_conv_w, v_ssd_conv_b, v_ssd_dt_bias, v_ssd_a_log, v_ssd_d, v_ssd_norm_w, v_ssd_w_out, v_final_norm_w):
    s_len, dm = x.shape[1], x.shape[2]
    me = 4 * lax.axis_index("x") + 2 * lax.axis_index("y") + lax.axis_index("c")
    x0 = x.reshape(s_len, dm)
    tgt = loss_target.reshape(s_len, dm)
    aw = 3 * 512
    si = 2 * dm
    sxbc = 2 * si
    n_ssd_in = ssd_w_in.shape[2] * NDEV

    g_ai, c_all = _all_gather("gather_attn_w_in", [attn_w_in[0].astype(BF16), c])
    w_ai = g_ai.transpose(1, 0, 2).reshape(dm, 4 * aw)
    c_all = c_all.reshape(NDEV, dm)

    part = _mod_part(c_all, mod_w)
    (part_all,) = _all_gather("gather_mod", [part])
    mod_nb = jnp.stack([lax.dynamic_index_in_dim(part_all, i * NDEV + me, axis=1, keepdims=False).reshape(3 * dm)
                        for i in range(2)])

    ssd_small = _pack_ssd_small(ssd_conv_w[0], ssd_conv_b, ssd_norm_w)
    ao_shard = [attn_w_out[0].astype(BF16)]
    ao_handles, ao_token = _exchange_start("w_out_start", ao_shard, _landing_zones("w_out_place", ao_shard, "gather"), "gather",
                                           part_all)
    late_shards = [ssd_w_in[0].T.astype(BF16), ssd_w_out[0].astype(BF16), ssd_small]
    w_handles, w_token = _exchange_start("weights_start", late_shards, _landing_zones("weights_place", late_shards, "gather"),
                                         "gather", ao_token)
    (shift0, scale0, gate0, nw0), (shift1, scale1, gate1, nw1) = _mod_finish(mod_nb, mod_b, norm_w, [ao_token, w_token])
    shift, scale, gate, nw = [shift0, shift1], [scale0, scale1], [gate0, gate1], [nw0, nw1]

    hn0 = _norm_mod_fwd("norm0", x0, nw[0], scale[0], shift[0])
    inv_freq = ROPE_THETA ** (-jnp.arange(0, ROT_DIM, 2, dtype=F32) / ROT_DIM)
    lane = jnp.arange(128) % HEAD_DIM
    inv_row = jnp.where(lane < ROT_DIM, inv_freq[lane % (ROT_DIM // 2)], 0.0).reshape(1, 128).astype(F32)
    tabs = _rope_tables(positions.reshape(s_len, 1), inv_row)
    qk = _matmul("proj_qk", hn0, w_ai, "nn", F32, MM_T, MM_T, dm, epilogue=_rot_fwd, mrows=tabs, n_out=2 * aw)
    v = _matmul("proj_v", hn0, w_ai, "nn", F32, MM_T, aw // 2, dm, b_noff=2 * aw, n_out=aw)
    z0 = _matmul("proj_z", hn0, w_ai, "nn", F32, MM_T, aw // 2, dm, b_noff=3 * aw, n_out=aw)
    att = [_attn_fwd(g, qk, v) for g in range(3)]
    os_, lses = [a[0] for a in att], [a[1] for a in att]
    (g_ao,) = _exchange_wait("w_out_wait", ao_handles, "gather", lses[2])
    a0, y0, x1 = _attn_out(os_, lses, z0, x0, gate[0], g_ao.reshape(aw, dm))

    hn1 = _norm_mod_fwd("norm1", x1, nw[1], scale[1], shift[1])
    g_si, g_so, g_small = _exchange_wait("weights_wait", w_handles, "gather", hn1)
    w_ao = g_ao.reshape(aw, dm)
    w_si_t = g_si.reshape(n_ssd_in, dm)
    w_so = g_so.reshape(si, dm)
    conv_w = g_small[:, 0:CONV_WIDTH, :].transpose(1, 0, 2).reshape(CONV_WIDTH, sxbc)
    conv_b = g_small[:, 5, :].reshape(1, sxbc)
    snw = g_small[:, 6, 0:si // NDEV].reshape(1, si)
    ndt = 2 * SSD_HEADS
    z1 = _matmul("ssd_proj_z", hn1, w_si_t, "nt", F32, MM_T, MM_T, dm, n_out=si)
    xpre = _matmul("ssd_proj_xbc", hn1, w_si_t, "nt", F32, MM_T, MM_T, dm, b_noff=si, n_out=sxbc)
    dt_raw = _matmul("ssd_proj_dt", hn1, w_si_t, "nt", F32, MM_T, ndt, dm, b_noff=si + sxbc, n_out=ndt)
    xbc = _conv_fwd(xpre, conv_w, conv_b)
    dt_bias = ssd_dt_bias.reshape(1, 2 * SSD_HEADS)
    alog = ssd_a_log.reshape(1, 2 * SSD_HEADS)
    dt = _softplus_fwd(dt_raw, dt_bias)
    y_f, st_f = _ssd_fwd(xbc, dt, alog, 0)
    y_b, st_b = _ssd_fwd(xbc, dt, alog, 1)
    d_e = jnp.repeat(ssd_d.reshape(SSD_HEADS), HEAD_DIM).reshape(1, si)
    u = _gate_norm_fwd(y_f, y_b, xbc, z1, d_e, snw)

    fnw = final_norm_w.reshape(1, dm)
    dx2, dy1, g_fnw, dgate1, loss_part = _loss_bwd(u, w_so, x1, tgt, gate[1], fnw)
    gw_so = _matmul("ssd_out_dw", u, dy1, "tn", BF16, MM_T, MM_T, MM_T)
    dys, dz1, g_snw, g_d = _gate_norm_bwd(dy1, w_so, y_f, y_b, xbc, z1, d_e, snw)
    dxbc_f, ddt_f, dalog_f = _ssd_bwd(xbc, dt, alog, st_f, dys, d_e, 0)
    dxbc_b, ddt_b, dalog_b = _ssd_bwd(xbc, dt, alog, st_b, dys, d_e, 1)
    dpre, g_cw, g_cb = _conv_bwd(xpre, dxbc_f, dxbc_b, conv_w, conv_b)
    ddt_raw, g_dtb = _softplus_bwd(ddt_f, ddt_b, dt_raw, dt_bias)
    dhn1 = [_matmul("ssd_proj_z_dx", dz1, w_si_t, "nn", F32, MM_T, MM_T, MM_T),
            _matmul("ssd_proj_xbc_dx", dpre, w_si_t, "nn", F32, MM_T, MM_T, MM_T, b_koff=si)]
    gw_si_t = _matmul("ssd_proj_z_dw", dz1, hn1, "tn", BF16, MM_T, MM_T, MM_T, dest=(n_ssd_in, 0, None))
    gw_si_t = _matmul("ssd_proj_xbc_dw", dpre, hn1, "tn", BF16, MM_T, MM_T, MM_T, dest=(n_ssd_in, si, gw_si_t))
    gw_si_t = _matmul("ssd_proj_dt_dw", ddt_raw, hn1, "tn", BF16, ndt, MM_T, MM_T, dest=(n_ssd_in, si + sxbc, gw_si_t))

    l1_grads = [gw_so.reshape(NDEV, si // NDEV, dm), gw_si_t.reshape(NDEV, n_ssd_in // NDEV, dm),
                _pack_ssd_small_blocks(g_cw, g_cb, g_snw)]
    l1_handles, l1_token = _exchange_start("l1_grads_start", l1_grads, _landing_zones("l1_grads_place", l1_grads, "scatter"),
                                           "scatter", dhn1[1])
    dx1, dy0, g_nw1, dsc1, dsh1, dgate0 = _norm_mod_bwd(
        "ssd_proj_dt_dx_norm1_bwd", (ddt_raw, w_si_t, "nn", ndt, dict(b_koff=si + sxbc)), x1, dhn1, dx2,
        nw[1], scale[1], shift[1], prev=(y0, gate[0] + l1_token[0:1, 0:1]))

    gw_ao = _matmul("attn_out_dw", a0, dy0, "tn", BF16, aw // 2, MM_T, MM_T)
    dos, dls, dz0 = _mix_bwd(dy0, w_ao, os_, lses, z0)
    datt = [_attn_bwd(g, qk, v, os_[g], lses[g], dos[g], dls[g]) for g in range(3)]
    dqkv = _rot_pack_bwd([t[0] for t in datt], [t[1] for t in datt], [t[2] for t in datt], tabs)
    wcol = attn_w_in.shape[2]
    gw_ai = _matmul("proj_qkv_dw", hn0, dqkv, "tn", BF16, MM_T, wcol, MM_T, out_blocks=3 * aw // wcol, dest=(NDEV, 0, None))
    gw_ai = _matmul("proj_z_dw", hn0, dz0, "tn", BF16, MM_T, wcol, MM_T, out_blocks=aw // wcol,
                    dest=(NDEV, 3 * aw // wcol, gw_ai))
    after_start = lambda acc, t: acc + t
    zero_row = lambda token: jnp.tile(token[0:1], (1, dm // 128))
    l0_grads = [gw_ai, gw_ao.reshape(NDEV, aw // NDEV, dm)]
    pair_handles, pair_token = _exchange_start("l0_pair_start", l0_grads, _landing_zones("l0_pair_place", l0_grads, "pair"),
                                               "pair", dqkv)
    dhn0_z = _matmul("proj_z_dx", dz0, w_ai, "nt", F32, MM_T, MM_T, aw, b_koff=3 * aw, n_out=dm, epilogue=after_start,
                     ncols=(zero_row(pair_token),))
    from_sibling = _exchange_wait("l0_pair_wait", pair_handles, "pair", dhn0_z)
    chip_sums = [_pair_sum(f"l0_pair_sum_{a}", g, s) for a, (g, s) in enumerate(zip(l0_grads, from_sibling))]
    l0_handles, l0_token = _exchange_start("l0_grads_start", chip_sums, _landing_zones("l0_grads_place", chip_sums, "chips"),
                                           "chips", dhn0_z)
    dx0, g_nw0, dsc0, dsh0 = _norm_mod_bwd(
        "proj_qkv_dx_norm0_bwd", (dqkv, w_ai, "nt", aw, dict(n_out=dm)), x0, [dhn0_z], dx1,
        nw[0], scale[0], shift[0] + zero_row(l0_token))

    small_g = [_pack_small([dsh0, dsc0, dgate0, dsh1, dsc1, dgate1, g_nw0, g_nw1, g_fnw], g_dtb, [dalog_f, dalog_b], g_d, loss_part)]
    sm_handles, sm_token = _exchange_start("small_grads_start", small_g, _landing_zones("small_grads_place", small_g, "gather"),
                                           "gather", dx0)

    whole = (slice(None), slice(None))
    r_so, r_si, r_small = _exchange_wait("l1_grads_wait", l1_handles, "scatter", sm_token)
    si_out = [o.T for o in _adamw("adamw_ssd_w_in", ssd_w_in[0].T, r_si, m_ssd_w_in[0].T, v_ssd_w_in[0].T, n_ssd_in // NDEV, 256)]
    so_out = _adamw("adamw_ssd_w_out", ssd_w_out[0], r_so, m_ssd_w_out[0], v_ssd_w_out[0], 256)
    cw_cols = ssd_conv_w.shape[2]
    cw_out, cb_out, snw_out = _adamw_windows(
        "adamw_ssd_small", r_small,
        [(ssd_conv_w, m_ssd_conv_w, v_ssd_conv_w), (ssd_conv_b, m_ssd_conv_b, v_ssd_conv_b),
         (ssd_norm_w, m_ssd_norm_w, v_ssd_norm_w)],
        [(0, slice(0, CONV_WIDTH), slice(0, cw_cols), (0, slice(None), slice(None))),
         (1, slice(5, 6), slice(0, cw_cols), whole), (2, slice(6, 7), slice(0, si // NDEV), whole)])
    r_ai, r_ao = _exchange_wait("l0_grads_wait", l0_handles, "chips", so_out[0])
    ai_out = _adamw("adamw_attn_w_in", attn_w_in[0], r_ai, m_attn_w_in[0], v_attn_w_in[0], 256)
    ao_out = _adamw("adamw_attn_w_out", attn_w_out[0], r_ao, m_attn_w_out[0], v_attn_w_out[0], 192)

    (small_all,) = _exchange_wait("small_grads_wait", sm_handles, "gather", ai_out[0])
    full = slice(0, PACK_COLS)
    nhd = SSD_HEADS
    windows = [(0, slice(3 * i + k, 3 * i + k + 1), full, (slice(i, i + 1), slice(k * dm, (k + 1) * dm)))
               for i in range(2) for k in range(3)]
    windows += [(1, slice(6 + i, 7 + i), full, (slice(i, i + 1), slice(None))) for i in range(2)]
    windows += [(2, slice(8, 9), full, whole)]
    windows += [(3 + q, slice(9, 10), slice(2 * nhd * q + nhd * j, 2 * nhd * q + nhd * (j + 1)), (0, slice(j, j + 1), slice(None)))
                for q in range(2) for j in range(2)]
    windows += [(5, slice(9, 10), slice(4 * nhd, 5 * nhd), whole)]
    as_row = lambda a: a.reshape(1, dm)
    mb_out, nw_out, fnw_out, dtb_out, alog_out, d_out, loss = _adamw_windows(
        "adamw_small", small_all,
        [(mod_b, m_mod_b, v_mod_b), (norm_w, m_norm_w, v_norm_w), (fnw, as_row(m_final_norm_w), as_row(v_final_norm_w)),
         (ssd_dt_bias, m_ssd_dt_bias, v_ssd_dt_bias), (ssd_a_log, m_ssd_a_log, v_ssd_a_log), (ssd_d, m_ssd_d, v_ssd_d)],
        windows, extra=(slice(9, 10), slice(256, 257)))
    loss = loss.reshape(())

    ncol = mod_w.shape[2]
    dmod_all = small_all[:, 0:6, :].reshape(NDEV, 2, 3 * dm)
    dmod_sh = lax.dynamic_slice_in_dim(dmod_all, me * ncol, ncol, axis=2).transpose(1, 0, 2)
    g_modw = _mod_grad(c_all, dmod_sh).reshape(1, 2 * dm, ncol)
    modw_out = _adamw("adamw_mod_w", mod_w.reshape(2 * dm, ncol), g_modw, m_mod_w.reshape(2 * dm, ncol),
                      v_mod_w.reshape(2 * dm, ncol), 256)

    per_kind = []
    for k in range(4):
        per_kind.append([
            nw_out[k], modw_out[k].reshape(mod_w.shape), mb_out[k], ai_out[k][None], ao_out[k][None], si_out[k][None],
            cw_out[k], cb_out[k], dtb_out[k], alog_out[k], d_out[k], snw_out[k], so_out[k][None], fnw_out[k].reshape(dm)])
    return (loss, dx0.reshape(x.shape), *per_kind[0], *per_kind[1], *per_kind[2], *per_kind[3])


def _pack_ssd_small_blocks(g_cw, g_cb, g_nw):
    nper = g_cw.shape[1] // NDEV
    nwper = g_nw.shape[1] // NDEV

    def body(cw_ref, cb_ref, nw_ref, o_ref):
        o_ref[...] = jnp.zeros_like(o_ref)
        for d in range(NDEV):
            o_ref[d, 0:5, :] = cw_ref[:, d * nper:(d + 1) * nper]
            o_ref[d, 5:6, :] = cb_ref[:, d * nper:(d + 1) * nper]
            o_ref[d, 6:7, 0:nwper] = nw_ref[:, d * nwper:(d + 1) * nwper]

    return pl.pallas_call(body, name="pack_ssd_small_grads", out_shape=jax.ShapeDtypeStruct((NDEV, 8, nper), F32))(g_cw, g_cb, g_nw)
```

```python
import functools
import math

import jax
import jax.numpy as jnp
from jax import lax
from jax.experimental import pallas as pl
from jax.experimental.pallas import tpu as pltpu

F32 = jnp.float32
BF16 = jnp.bfloat16
HI = lax.Precision.HIGHEST
MESH = pl.DeviceIdType.MESH
NDEV = 8

NORM_EPS = 1e-6
ROPE_THETA = 500000.0
ROT_DIM = 16
HEAD_DIM = 64
DILATIONS = (1, 4, 16)
BAND = 64
NEG_BIG = -1e30
CHUNK = 128
SSD_HEADS = 32
SSD_GROUPS = 8
CONV_WIDTH = 5

ADAM_LR = 0.001
ADAM_B1 = 0.9
ADAM_B2 = 0.999
ADAM_EPS = 1e-08
ADAM_WD = 0.01
ADAM_STEP = 10

VMEM_BIG = 56 * 1024 * 1024
MM_T = 1024


def _params(sem=None, vmem=None):
    kw = {}
    if sem is not None:
        kw["dimension_semantics"] = sem
    if vmem is not None:
        kw["vmem_limit_bytes"] = vmem
    return pltpu.CompilerParams(**kw)


def _dg(a, b, ca, cb, prec=None):
    return lax.dot_general(a, b, (((ca,), (cb,)), ((), ())), preferred_element_type=F32, precision=prec)


def _nn(a, b):
    return _dg(a.astype(BF16), b.astype(BF16), 1, 0)


def _nt(a, b):
    return _dg(a.astype(BF16), b.astype(BF16), 1, 1)


def _tn(a, b):
    return _dg(a.astype(BF16), b.astype(BF16), 0, 0)


def _hnn(a, b):
    return _dg(a, b, 1, 0, HI)


@jax.custom_vjp
def _bnn(a, b):
    return _nn(a, b)


_bnn.defvjp(lambda a, b: (_nn(a, b), (a, b)), lambda r, g: (_nt(g, r[1]), _tn(r[0], g)))


@jax.custom_vjp
def _bnt(a, b):
    return _nt(a, b)


_bnt.defvjp(lambda a, b: (_nt(a, b), (a, b)), lambda r, g: (_nn(g, r[1]), _tn(g, r[0])))


@jax.custom_vjp
def _btn(a, b):
    return _tn(a, b)


_btn.defvjp(lambda a, b: (_tn(a, b), (a, b)), lambda r, g: (_nt(r[1], g), _nn(r[0], g)))


def _silu(x):
    return x * jax.nn.sigmoid(x)


def _matmul(name, a, b, mode, out_dtype, tm, tn, tk, *, epilogue=None, tiled=(), mrows=(), ncols=(),
            b_noff=0, b_koff=0, n_out=None, out_blocks=None, dest=None):
    if mode == "tn":
        K, M = a.shape
    else:
        M, K = a.shape
    N = n_out if n_out is not None else (b.shape[0] if mode == "nt" else b.shape[1])
    tm, tn, tk = min(tm, M), min(tn, N), min(tk, K)
    assert M % tm == 0 and N % tn == 0 and K % tk == 0, (name, M, N, K, tm, tn, tk)
    assert b_noff % tn == 0 and b_koff % tk == 0
    no, ko = b_noff // tn, b_koff // tk
    nk = K // tk
    if mode == "tn":
        a_spec = pl.BlockSpec((tk, tm), lambda i, j, k: (k, i))
    else:
        a_spec = pl.BlockSpec((tm, tk), lambda i, j, k: (i, k))
    if mode == "nt":
        b_spec = pl.BlockSpec((tn, tk), lambda i, j, k: (j + no, k + ko))
    else:
        b_spec = pl.BlockSpec((tk, tn), lambda i, j, k: (k + ko, j + no))
    specs = [a_spec, b_spec]
    specs += [pl.BlockSpec((tm, tn), lambda i, j, k: (i, j)) for _ in tiled]
    specs += [pl.BlockSpec((tm, r.shape[1]), lambda i, j, k: (i, 0)) for r in mrows]
    specs += [pl.BlockSpec((1, tn), lambda i, j, k: (0, j)) for _ in ncols]
    total, off, earlier = dest if dest is not None else (None, 0, None)
    if out_blocks is None:
        assert off % tm == 0
        mo = off // tm
        out_shape = jax.ShapeDtypeStruct((M if total is None else total, N), out_dtype)
        out_spec = pl.BlockSpec((tm, tn), lambda i, j, k: (i + mo, j))
    else:
        nper = N // out_blocks
        assert nper % tn == 0
        jb = nper // tn
        out_shape = jax.ShapeDtypeStruct((out_blocks if total is None else total, M, nper), out_dtype)
        out_spec = pl.BlockSpec((None, tm, tn), lambda i, j, k: (j // jb + off, i, j % jb))
    if earlier is not None:
        assert earlier.shape == out_shape.shape and earlier.dtype == out_shape.dtype
    ne = len(tiled) + len(mrows) + len(ncols)
    dot = {"nn": _nn, "nt": _nt, "tn": _tn}[mode]

    def body(a_ref, b_ref, *rest):
        extras, o_ref = rest[:ne], rest[ne]

        def finish(acc):
            if epilogue is not None:
                acc = epilogue(acc, *[e[...] for e in extras])
            o_ref[...] = acc.astype(o_ref.dtype)

        if nk == 1:
            finish(dot(a_ref[...], b_ref[...]))
        else:
            acc_ref = rest[ne + 1]
            k = pl.program_id(2)

            @pl.when(k == 0)
            def _():
                acc_ref[...] = jnp.zeros_like(acc_ref)

            acc_ref[...] += dot(a_ref[...], b_ref[...])

            @pl.when(k == nk - 1)
            def _():
                finish(acc_ref[...])

    args = [a, b, *tiled, *mrows, *ncols]
    aliases = {}
    if earlier is not None:
        specs.append(pl.BlockSpec(memory_space=pl.ANY))
        aliases = {len(args): 0}
        args.append(earlier)

    def body_with_dest(*refs):
        body(*refs[:2 + ne], *refs[2 + ne + (earlier is not None):])

    return pl.pallas_call(
        body_with_dest, name=name, out_shape=out_shape, grid=(M // tm, N // tn, nk),
        in_specs=specs, out_specs=out_spec, input_output_aliases=aliases,
        scratch_shapes=[] if nk == 1 else [pltpu.VMEM((tm, tn), F32)],
        compiler_params=_params(("parallel", "parallel", "arbitrary"), VMEM_BIG),
    )(*args)


def _matmul_rows(name, a, b, mode, tm, tk, fn, rows, consts, outs, accs, *, n_out=None, b_noff=0, b_koff=0):
    rl = [(t, t.shape[1], 0) if not isinstance(t, tuple) else t for t in rows]
    make_a = a if callable(a) else None
    M, K = (rl[0][0].shape[0], b.shape[1 if mode == "nt" else 0]) if make_a else a.shape
    N = n_out if n_out is not None else (b.shape[0] if mode == "nt" else b.shape[1])
    tm, tk = min(tm, M), min(tk, K)
    assert M % tm == 0 and K % tk == 0 and b_koff % tk == 0 and b_noff % N == 0, (name, M, N, K)
    no, ko, nk = b_noff // N, b_koff // tk, K // tk
    assert make_a is None or nk == 1
    nr, nc, no_, na = len(rl), len(consts), len(outs), len(accs)
    dot = _nt if mode == "nt" else _nn

    def body(*refs):
        a_ref, b_ref, rest = (None, refs[0], refs[1:]) if make_a else (refs[0], refs[1], refs[2:])
        r_refs, c_refs = rest[:nr], rest[nr:nr + nc]
        o_refs, acc_refs = rest[nr + nc:nr + nc + no_], rest[nr + nc + no_:nr + nc + no_ + na]
        i, k = pl.program_id(0), pl.program_id(1)

        def finish(prod, *made):
            res_o, res_a = fn(prod, *made, *[r[...] for r in r_refs], *[c[...] for c in c_refs])
            for r, v in zip(o_refs, res_o, strict=True):
                r[...] = v.astype(r.dtype)
            if acc_refs:
                @pl.when(i == 0)
                def _():
                    for r in acc_refs:
                        r[...] = jnp.zeros_like(r)

                for r, v in zip(acc_refs, res_a, strict=True):
                    r[...] += v

        if make_a:
            left = make_a(*[r[...] for r in r_refs], *[c[...] for c in c_refs])
            finish(dot(left, b_ref[...]), left)
        elif nk == 1:
            finish(dot(a_ref[...], b_ref[...]))
        else:
            prod_ref = rest[-1]

            @pl.when(k == 0)
            def _():
                prod_ref[...] = jnp.zeros_like(prod_ref)

            prod_ref[...] += dot(a_ref[...], b_ref[...])

            @pl.when(k == nk - 1)
            def _():
                finish(prod_ref[...])

    if mode == "nt":
        b_spec = pl.BlockSpec((N, tk), lambda i, k: (no, k + ko))
    else:
        b_spec = pl.BlockSpec((tk, N), lambda i, k: (k + ko, no))
    in_specs = ([] if make_a else [pl.BlockSpec((tm, tk), lambda i, k: (i, k))]) + [b_spec]
    in_specs += [pl.BlockSpec((tm, w), functools.partial(lambda i, k, cb: (i, cb), cb=cb)) for (_, w, cb) in rl]
    in_specs += [pl.BlockSpec(c.shape, lambda i, k: (0, 0)) for c in consts]
    out_specs = [pl.BlockSpec((tm, c), lambda i, k: (i, 0)) for (c, _) in outs]
    out_specs += [pl.BlockSpec(shp, lambda i, k: (0, 0)) for shp in accs]
    out_shape = [jax.ShapeDtypeStruct((M, c), dt) for (c, dt) in outs] + [jax.ShapeDtypeStruct(shp, F32) for shp in accs]
    res = pl.pallas_call(
        body, name=name, out_shape=out_shape, grid=(M // tm, nk), in_specs=in_specs, out_specs=out_specs,
        scratch_shapes=[] if nk == 1 else [pltpu.VMEM((tm, N), F32)],
        compiler_params=_params(("arbitrary" if accs else "parallel", "arbitrary"), VMEM_BIG),
    )(*([] if make_a else [a]), b, *[t[0] for t in rl], *consts)
    return res[:no_], res[no_:]


def _rowwise(name, fn, tiled, consts, outs, accs, ts):
    tl = [(t, t.shape[1], 0) if not isinstance(t, tuple) else t for t in tiled]
    s_len = tl[0][0].shape[0]
    assert s_len % ts == 0
    nt_, nc_, no_ = len(tl), len(consts), len(outs)

    def body(*refs):
        t_refs, c_refs = refs[:nt_], refs[nt_:nt_ + nc_]
        o_refs, a_refs = refs[nt_ + nc_:nt_ + nc_ + no_], refs[nt_ + nc_ + no_:]
        res_o, res_a = fn(*[r[...] for r in t_refs], *[r[...] for r in c_refs])
        for r, v in zip(o_refs, res_o, strict=True):
            r[...] = v.astype(r.dtype)
        if a_refs:
            @pl.when(pl.program_id(0) == 0)
            def _():
                for r in a_refs:
                    r[...] = jnp.zeros_like(r)

            for r, v in zip(a_refs, res_a, strict=True):
                r[...] += v

    in_specs = [pl.BlockSpec((ts, w), functools.partial(lambda i, cb: (i, cb), cb=cb)) for (_, w, cb) in tl]
    in_specs += [pl.BlockSpec(c.shape, lambda i: (0, 0)) for c in consts]
    out_specs = [pl.BlockSpec((ts, c), lambda i: (i, 0)) for (c, _) in outs]
    out_specs += [pl.BlockSpec(shp, lambda i: (0, 0)) for shp in accs]
    out_shape = [jax.ShapeDtypeStruct((s_len, c), dt) for (c, dt) in outs]
    out_shape += [jax.ShapeDtypeStruct(shp, F32) for shp in accs]
    res = pl.pallas_call(
        body, name=name, out_shape=out_shape, grid=(s_len // ts,), in_specs=in_specs, out_specs=out_specs,
        compiler_params=_params(("arbitrary",) if accs else ("parallel",), VMEM_BIG),
    )(*[t[0] for t in tl], *consts)
    return res[:no_], res[no_:]


def _norm_mod_fn(x, nw, sc, sh):
    r = lax.rsqrt(jnp.mean(x * x, axis=-1, keepdims=True) + NORM_EPS)
    return (x * r * nw) * (1.0 + sc) + sh


def _norm_mod_fwd(name, x, nw, sc, sh):
    (hn,), _ = _rowwise(name, lambda x, nw, sc, sh: ([_norm_mod_fn(x, nw, sc, sh)], []),
                        [x], [nw, sc, sh], [(x.shape[1], BF16)], [], 512)
    return hn


def _norm_mod_bwd(name, last, x, dhn_parts, dres, nw, sc, sh, prev=None):
    n = len(dhn_parts)
    d = x.shape[1]
    a, b, mode, tk, kw = last

    def fn(dhn, x, *rest):
        for p in rest[:n]:
            dhn = dhn + p
        dres, rest = rest[n], rest[n + 1:]
        y_prev, (nw, sc, sh), gate = (rest[0], rest[1:4], rest[4]) if prev is not None else (None, rest[0:3], None)
        _, vjp = jax.vjp(_norm_mod_fn, x, nw, sc, sh)
        dx, dnw, dsc, dsh = vjp(dhn)
        dx = dx + dres
        if prev is None:
            return [dx], [dnw, dsc, dsh]
        return [dx, gate * dx], [dnw, dsc, dsh, jnp.sum(dx * y_prev, axis=0, keepdims=True)]

    rows = [x, *dhn_parts, dres] + ([prev[0]] if prev is not None else [])
    consts = [nw, sc, sh] + ([prev[1]] if prev is not None else [])
    outs = [(d, F32)] + ([(d, BF16)] if prev is not None else [])
    res_o, res_a = _matmul_rows(name, a, b, mode, 512, tk, fn, rows, consts, outs, [(1, d)] * (3 + (prev is not None)), **kw)
    return (*res_o, *res_a)


def _rope_tables(pos_col, inv_row):
    def fn(pos, inv):
        ang = pos.astype(F32) * inv
        e = lax.broadcasted_iota(jnp.int32, (1, 128), 1) % HEAD_DIM
        cos, sin = jnp.cos(ang), jnp.sin(ang)
        half = ROT_DIM // 2
        return [jnp.where(e < ROT_DIM, cos, 1.0), jnp.where(e < half, -sin, 0.0),
                jnp.where((e >= half) & (e < ROT_DIM), sin, 0.0)], []

    (c, sa, sb), _ = _rowwise("rope_tables", fn, [pos_col], [inv_row], [(128, F32)] * 3, [], 512)
    return c, sa, sb


def _rot_fwd(t, c, sa, sb):
    n = t.shape[1]
    rep = n // 128
    c, sa, sb = (jnp.tile(u, (1, rep)) for u in (c, sa, sb))
    return t * c + pltpu.roll(t, n - ROT_DIM // 2, 1) * sa + pltpu.roll(t, ROT_DIM // 2, 1) * sb


def _rot_bwd(g, c, sa, sb):
    n = g.shape[1]
    rep = n // 128
    c, sa, sb = (jnp.tile(u, (1, rep)) for u in (c, sa, sb))
    return g * c + pltpu.roll(g * sa, ROT_DIM // 2, 1) + pltpu.roll(g * sb, n - ROT_DIM // 2, 1)


ATT_TQ = 128
ATT_TK = ATT_TQ + 2 * BAND


def _attn_specs(g, s_len):
    def blk(off):
        return pl.BlockSpec((s_len, 128), functools.partial(lambda hp, off: (0, off + hp), off=off))

    return blk(4 * g), blk(12 + 4 * g), blk(4 * g), blk(0)


def _attn_tile_geometry(t, d, l):
    nts = l // ATT_TQ
    r = t // nts
    ts = t % nts
    q0 = ts * ATT_TQ
    ws = jnp.clip(q0 - BAND, 0, l - ATT_TK)
    kind = jnp.where(ts == 0, 0, jnp.where(ts == nts - 1, 2, 1))
    if d == 1:
        return pl.ds(pl.multiple_of(q0, ATT_TQ), ATT_TQ), pl.ds(pl.multiple_of(ws, BAND), ATT_TK), kind
    return pl.ds(r + d * q0, ATT_TQ, stride=d), pl.ds(r + d * ws, ATT_TK, stride=d), kind


def _attn_fill_bias(bias_ref):
    iq = lax.broadcasted_iota(jnp.int32, (2 * ATT_TQ, 1), 0) % ATT_TQ
    ik = lax.broadcasted_iota(jnp.int32, (1, ATT_TK), 1)
    for i, off in enumerate((0, -BAND, -2 * BAND)):
        bias_ref[i] = jnp.where(jnp.abs(ik + off - iq) <= BAND, 0.0, NEG_BIG)


def _split_heads(t, in_h):
    zero = jnp.zeros_like(t)
    return jnp.concatenate([jnp.where(in_h[0], t, zero), jnp.where(in_h[1], t, zero)], axis=0)


def _attn_fwd(g, qk, v):
    s_len = qk.shape[0]
    d = DILATIONS[g]
    l = s_len // d
    assert l % ATT_TQ == 0 and l >= ATT_TK
    q_spec, k_spec, v_spec, o_spec = _attn_specs(g, s_len)
    scale = 1.0 / math.sqrt(HEAD_DIM)

    def body(q_ref, k_ref, v_ref, o_ref, lse_ref, bias_ref):
        lane = lax.broadcasted_iota(jnp.int32, (1, 128), 1)
        in_h = [lane < HEAD_DIM, lane >= HEAD_DIM]
        _attn_fill_bias(bias_ref)

        def tile(t, carry):
            rows, win, kind = _attn_tile_geometry(t, d, l)
            q = (q_ref[rows, :] * scale).astype(BF16)
            k = k_ref[win, :].astype(BF16)
            vv = v_ref[win, :].astype(BF16)
            s = _nt(_split_heads(q, in_h), k) + bias_ref[kind]
            m = jnp.max(s, axis=1, keepdims=True)
            p = jnp.exp(s - m)
            den = jnp.sum(p, axis=1, keepdims=True)
            out = _nn(p, vv) / den
            lse = m + jnp.log(den)
            o_ref[rows, :] = jnp.where(in_h[0], out[:ATT_TQ], out[ATT_TQ:])
            lse_ref[rows, :] = jnp.where(in_h[0], lse[:ATT_TQ], lse[ATT_TQ:])
            return carry

        lax.fori_loop(0, s_len // ATT_TQ, tile, 0, unroll=4)

    return pl.pallas_call(
        body, name=f"attn_fwd_g{g}", grid=(4,),
        out_shape=[jax.ShapeDtypeStruct((s_len, 512), F32)] * 2,
        in_specs=[q_spec, k_spec, v_spec], out_specs=[o_spec, o_spec],
        scratch_shapes=[pltpu.VMEM((3, 2 * ATT_TQ, ATT_TK), F32)],
        compiler_params=_params(("parallel",), VMEM_BIG),
    )(qk, qk, v)


def _attn_bwd(g, qk, v, o, lse, do, dlse):
    s_len = qk.shape[0]
    d = DILATIONS[g]
    l = s_len // d
    q_spec, k_spec, v_spec, o_spec = _attn_specs(g, s_len)
    scale = 1.0 / math.sqrt(HEAD_DIM)

    def body(q_ref, k_ref, v_ref, o_ref, lse_ref, do_ref, dlse_ref, dq_ref, dk_ref, dv_ref, bias_ref):
        lane = lax.broadcasted_iota(jnp.int32, (1, 128), 1)
        in_h = [lane < HEAD_DIM, lane >= HEAD_DIM]
        dk_ref[...] = jnp.zeros_like(dk_ref)
        dv_ref[...] = jnp.zeros_like(dv_ref)
        _attn_fill_bias(bias_ref)

        def tile(t, carry):
            rows, win, kind = _attn_tile_geometry(t, d, l)
            k, vv = k_ref[win, :].astype(BF16), v_ref[win, :].astype(BF16)
            dout, lse_t, dlse_t = do_ref[rows, :], lse_ref[rows, :], dlse_ref[rows, :]
            od = dout * o_ref[rows, :]
            q2 = _split_heads((q_ref[rows, :] * scale).astype(BF16), in_h)
            do2 = _split_heads(dout.astype(BF16), in_h)
            head_col = lambda a: jnp.concatenate([a[:, 0:1], a[:, HEAD_DIM:HEAD_DIM + 1]], axis=0)
            delta = jnp.concatenate([jnp.sum(jnp.where(m, od, 0.0), axis=1, keepdims=True) for m in in_h], axis=0)
            p = jnp.exp(_nt(q2, k) + bias_ref[kind] - head_col(lse_t))
            ds = (p * (_nt(do2, vv) - delta + head_col(dlse_t))).astype(BF16)
            dq2 = _nn(ds, k) * scale
            dq_ref[rows, :] = jnp.where(in_h[0], dq2[:ATT_TQ], dq2[ATT_TQ:])
            dk_ref[win, :] += _tn(ds, q2)
            dv_ref[win, :] += _tn(p, do2)
            return carry

        lax.fori_loop(0, s_len // ATT_TQ, tile, 0, unroll=4)

    return pl.pallas_call(
        body, name=f"attn_bwd_g{g}", grid=(4,),
        out_shape=[jax.ShapeDtypeStruct((s_len, 512), F32)] * 3,
        in_specs=[q_spec, k_spec, v_spec, o_spec, o_spec, o_spec, o_spec], out_specs=[o_spec] * 3,
        scratch_shapes=[pltpu.VMEM((3, 2 * ATT_TQ, ATT_TK), F32)],
        compiler_params=_params(("parallel",), VMEM_BIG),
    )(qk, qk, v, o, lse, do, dlse)


def _mix_weights(ls):
    mx = jnp.maximum(jnp.maximum(ls[0], ls[1]), ls[2])
    es = [jnp.exp(x - mx) for x in ls]
    tot = es[0] + es[1] + es[2]
    return [e / tot for e in es]


def _attn_out(os_, lses, z, x, gate, w_out):
    s_len, dm = x.shape
    tm = 256
    wdt = 512
    z, z_block = z

    def body(o0, o1, o2, l0, l1, l2, z_ref, x_ref, g_ref, w_ref, a_ref, y_ref, x1_ref):
        alphas = _mix_weights([l0[...], l1[...], l2[...]])
        y = jnp.zeros((tm, dm), F32)
        for g, o_ref in enumerate((o0, o1, o2)):
            a_g = (o_ref[...] * alphas[g] * _silu(z_ref[:, g * wdt:(g + 1) * wdt])).astype(BF16)
            a_ref[:, g * wdt:(g + 1) * wdt] = a_g
            y = y + _nn(a_g, w_ref[g * wdt:(g + 1) * wdt, :])
        y_ref[...] = y
        x1_ref[...] = x_ref[...] + g_ref[...] * y

    row = lambda c: pl.BlockSpec((tm, c), lambda i: (i, 0))
    return pl.pallas_call(
        body, name="attn_out", grid=(s_len // tm,),
        out_shape=[jax.ShapeDtypeStruct((s_len, 3 * wdt), BF16), jax.ShapeDtypeStruct((s_len, dm), F32),
                   jax.ShapeDtypeStruct((s_len, dm), F32)],
        in_specs=[row(wdt)] * 6 + [pl.BlockSpec((tm, 3 * wdt), lambda i: (i, z_block)), row(dm),
                                   pl.BlockSpec((1, dm), lambda i: (0, 0)), pl.BlockSpec(w_out.shape, lambda i: (0, 0))],
        out_specs=[row(3 * wdt), row(dm), row(dm)],
        compiler_params=_params(("parallel",), VMEM_BIG),
    )(*os_, *lses, z, x, gate, w_out)


def _mix_bwd(dy, w_out, os_, lses, z):
    wdt = 512

    def fn(da, o0, o1, o2, l0, l1, l2, z):
        os_t, ls = [o0, o1, o2], [l0, l1, l2]
        alphas = _mix_weights(ls)
        hi = lax.broadcasted_iota(jnp.int32, (wdt, wdt), 0) // HEAD_DIM
        hj = lax.broadcasted_iota(jnp.int32, (wdt, wdt), 1) // HEAD_DIM
        seg = (hi == hj).astype(F32)
        dos, dal, dzs = [], [], []
        for g in range(3):
            zg = z[:, g * wdt:(g + 1) * wdt]
            sig = jax.nn.sigmoid(zg)
            dag = da[:, g * wdt:(g + 1) * wdt]
            dmix = dag * zg * sig
            dzs.append(dag * os_t[g] * alphas[g] * (sig * (1.0 + zg * (1.0 - sig))))
            dos.append(dmix * alphas[g])
            dal.append(_hnn(dmix * os_t[g], seg))
        mean = alphas[0] * dal[0] + alphas[1] * dal[1] + alphas[2] * dal[2]
        dls = [alphas[g] * (dal[g] - mean) for g in range(3)]
        return dos + dls + [jnp.concatenate(dzs, axis=1)], []

    outs, _ = _matmul_rows("attn_out_dx_mix_bwd", dy, w_out, "nt", 256, dy.shape[1], fn, [*os_, *lses, (z[0], 3 * wdt, z[1])], [],
                           [(wdt, F32)] * 6 + [(3 * wdt, BF16)], [])
    return outs[:3], outs[3:6], outs[6]


def _rot_pack_bwd(dqs, dks, dvs, tabs):
    wdt = 512

    def fn(*args):
        grads, (c, sa, sb) = args[:9], args[9:]
        cols = [_rot_bwd(gq, c, sa, sb) for gq in grads[:6]] + list(grads[6:])
        return [jnp.concatenate(cols, axis=1)], []

    (out,), _ = _rowwise("rot_pack_bwd", fn, [*dqs, *dks, *dvs, *tabs], [], [(9 * wdt, BF16)], [], 256)
    return out


CONV_CB = 128
CONV_R = 256
CONV_PAD = 8


def _conv_taps(buf, base, off, sign):
    return [buf[pl.ds(base + off + sign * j, CONV_R), :] for j in range(CONV_WIDTH)]


def _conv_tap_sum(taps, w):
    acc = None
    for j, t in enumerate(taps):
        term = t * w[j:j + 1, :]
        acc = term if acc is None else acc + term
    return acc


def _conv_fwd(xpre, cw, cb):
    s_len, ch = xpre.shape
    nchunk = s_len // CONV_R

    def body(x_ref, w_ref, b_ref, o_ref, xp):
        zero = jnp.zeros((CONV_PAD, CONV_CB), F32)
        xp[0:CONV_PAD, :] = zero
        xp[s_len + CONV_PAD:s_len + 2 * CONV_PAD, :] = zero

        def fill(ci, carry):
            base = pl.multiple_of(ci * CONV_R, CONV_R)
            xp[pl.ds(base + CONV_PAD, CONV_R), :] = x_ref[pl.ds(base, CONV_R), :]
            return carry

        lax.fori_loop(0, nchunk, fill, 0)
        w = w_ref[...]
        b = b_ref[...]

        def chunk(ci, carry):
            base = pl.multiple_of(ci * CONV_R, CONV_R)
            u = _conv_tap_sum(_conv_taps(xp, base, CONV_PAD - CONV_WIDTH // 2, 1), w) + b
            o_ref[pl.ds(base, CONV_R), :] = _silu(u)
            return carry

        lax.fori_loop(0, nchunk, chunk, 0, unroll=2)

    col = lambda r: pl.BlockSpec((r, CONV_CB), lambda j: (0, j))
    return pl.pallas_call(
        body, name="conv_fwd", grid=(ch // CONV_CB,), out_shape=jax.ShapeDtypeStruct((s_len, ch), F32),
        in_specs=[col(s_len), col(CONV_WIDTH), col(1)], out_specs=col(s_len),
        scratch_shapes=[pltpu.VMEM((s_len + 2 * CONV_PAD, CONV_CB), F32)],
        compiler_params=_params(("parallel",), VMEM_BIG),
    )(xpre, cw, cb)


def _conv_bwd(xpre, da, db, cw, cb):
    s_len, ch = xpre.shape
    nchunk = s_len // CONV_R
    half = CONV_WIDTH // 2

    def body(x_ref, da_ref, db_ref, w_ref, b_ref, dx_ref, gw_ref, gb_ref, xp, dcp):
        zero = jnp.zeros((CONV_PAD, CONV_CB), F32)
        for buf in (xp, dcp):
            buf[0:CONV_PAD, :] = zero
            buf[s_len + CONV_PAD:s_len + 2 * CONV_PAD, :] = zero

        def fill(ci, carry):
            base = pl.multiple_of(ci * CONV_R, CONV_R)
            xp[pl.ds(base + CONV_PAD, CONV_R), :] = x_ref[pl.ds(base, CONV_R), :]
            return carry

        lax.fori_loop(0, nchunk, fill, 0)
        w = w_ref[...]
        b = b_ref[...]

        def first(ci, carry):
            base = pl.multiple_of(ci * CONV_R, CONV_R)
            taps = _conv_taps(xp, base, CONV_PAD - half, 1)
            u = _conv_tap_sum(taps, w) + b
            sig = jax.nn.sigmoid(u)
            dc = (da_ref[pl.ds(base, CONV_R), :] + db_ref[pl.ds(base, CONV_R), :]) * (sig * (1.0 + u * (1.0 - sig)))
            dcp[pl.ds(base + CONV_PAD, CONV_R), :] = dc
            gb = carry[0] + jnp.sum(dc, axis=0, keepdims=True)
            gws = [carry[1 + j] + jnp.sum(dc * taps[j], axis=0, keepdims=True) for j in range(CONV_WIDTH)]
            return (gb, *gws)

        z1 = jnp.zeros((1, CONV_CB), F32)
        sums = lax.fori_loop(0, nchunk, first, (z1,) * (1 + CONV_WIDTH), unroll=2)
        gb_ref[...] = sums[0]
        for j in range(CONV_WIDTH):
            gw_ref[j:j + 1, :] = sums[1 + j]

        def second(ci, carry):
            base = pl.multiple_of(ci * CONV_R, CONV_R)
            dx_ref[pl.ds(base, CONV_R), :] = _conv_tap_sum(_conv_taps(dcp, base, CONV_PAD + half, -1), w).astype(dx_ref.dtype)
            return carry

        lax.fori_loop(0, nchunk, second, 0, unroll=2)

    col = lambda r: pl.BlockSpec((r, CONV_CB), lambda j: (0, j))
    return pl.pallas_call(
        body, name="conv_bwd", grid=(ch // CONV_CB,),
        out_shape=[jax.ShapeDtypeStruct((s_len, ch), BF16), jax.ShapeDtypeStruct((CONV_WIDTH, ch), F32),
                   jax.ShapeDtypeStruct((1, ch), F32)],
        in_specs=[col(s_len), col(s_len), col(s_len), col(CONV_WIDTH), col(1)],
        out_specs=[col(s_len), col(CONV_WIDTH), col(1)],
        scratch_shapes=[pltpu.VMEM((s_len + 2 * CONV_PAD, CONV_CB), F32)] * 2,
        compiler_params=_params(("parallel",), VMEM_BIG),
    )(xpre, da, db, cw, cb)


SSD_GW = 256
SSD_N = 128


def _bf16_parts(x, n):
    parts, rest = [], x
    for _ in range(n):
        p = rest.astype(BF16)
        parts.append(p)
        rest = rest - p.astype(F32)
    return parts


@jax.custom_vjp
def _expand(x, e):
    eb = e.astype(BF16)
    return sum(_dg(p, eb, 1, 0) for p in _bf16_parts(x, 2))


def _expand_fwd(x, e):
    return _expand(x, e), e


def _expand_bwd(e, g):
    eb = e.astype(BF16)
    return sum(_dg(p, eb, 1, 1) for p in _bf16_parts(g, 2)), jnp.zeros_like(e)


_expand.defvjp(_expand_fwd, _expand_bwd)


@jax.custom_vjp
def _running_sum(tri, x):
    tb = tri.astype(BF16)
    return sum(_dg(tb, p, 1, 0) for p in _bf16_parts(x, 3))


def _running_sum_fwd(tri, x):
    return _running_sum(tri, x), tri


def _running_sum_bwd(tri, g):
    tb = tri.astype(BF16)
    return jnp.zeros_like(tri), sum(_dg(tb, p, 0, 0) for p in _bf16_parts(g, 3))


_running_sum.defvjp(_running_sum_fwd, _running_sum_bwd)


def _ssd_mask(dirn):
    ri = lax.broadcasted_iota(jnp.int32, (CHUNK, CHUNK), 0)
    cj = lax.broadcasted_iota(jnp.int32, (CHUNK, CHUNK), 1)
    return (cj <= ri) if dirn == 0 else (cj >= ri)


def _ssd_rowsel(dirn):
    last = CHUNK - 1 if dirn == 0 else 0
    return (lax.broadcasted_iota(jnp.int32, (CHUNK, 1), 0) == last).astype(F32)


def _ssd_chunk_pre(dirn):
    nh = 2 * SSD_HEADS

    def f(dt, alog):
        da = dt * (-jnp.exp(alog))
        cum = _running_sum(_ssd_mask(dirn).astype(F32), da)
        tot = jnp.sum(cum * _ssd_rowsel(dirn), axis=0, keepdims=True)
        hh = lax.broadcasted_iota(jnp.int32, (nh, SSD_HEADS * HEAD_DIM), 0)
        jj = lax.broadcasted_iota(jnp.int32, (nh, SSD_HEADS * HEAD_DIM), 1)
        expand = (hh == dirn * SSD_HEADS + jj // HEAD_DIM).astype(F32)
        return cum, cum.T, _expand(dt, expand), _expand(jnp.exp(tot - cum), expand), _expand(jnp.exp(cum), expand)

    return f


def _ssd_group_fn(g, dirn, stacked):
    nh = 2 * SSD_HEADS

    def f(xs, bm, cm, st, cum, cum_t, dt_e, w_e, ce_e):
        mask = _ssd_mask(dirn)
        xdt = xs * dt_e
        cd_e = jnp.sum(ce_e * _ssd_rowsel(dirn), axis=0, keepdims=True)
        cb = _bnt(cm, bm)
        lane_head = lax.broadcasted_iota(jnp.int32, (1, SSD_GW), 1) // HEAD_DIM
        y = _bnn(cm, st) * ce_e
        decayed, inputs = [], []
        for j in range(4):
            hidx = dirn * SSD_HEADS + 4 * g + j
            col = jnp.sum(cum * (lax.broadcasted_iota(jnp.int32, (1, nh), 1) == hidx).astype(F32), axis=1, keepdims=True)
            row = jnp.sum(cum_t * (lax.broadcasted_iota(jnp.int32, (nh, 1), 0) == hidx).astype(F32), axis=0, keepdims=True)
            dec = cb * jnp.exp(jnp.where(mask, col - row, NEG_BIG))
            head = (lane_head == j).astype(F32)
            if stacked:
                decayed.append(dec)
                inputs.append(xdt * head)
            else:
                y = y + _bnn(dec, xdt) * head
        if stacked:
            y = y + _bnn(jnp.concatenate(decayed, axis=1), jnp.concatenate(inputs, axis=0))
        st_out = st * cd_e + _btn(bm, xdt * w_e)
        return y, st_out

    return f


def _ssd_in_specs(kk):
    ln = CHUNK
    return [pl.BlockSpec((ln, 2048), lambda i: (kk(i), 0)),
            pl.BlockSpec((ln, 1024), lambda i: (kk(i), 2)),
            pl.BlockSpec((ln, 1024), lambda i: (kk(i), 3)),
            pl.BlockSpec((ln, 2 * SSD_HEADS), lambda i: (kk(i), 0)),
            pl.BlockSpec((1, 2 * SSD_HEADS), lambda i: (0, 0))]


def _ssd_fwd(xbc, dt, alog, dirn):
    s_len = xbc.shape[0]
    nc = s_len // CHUNK
    kk = (lambda i: i) if dirn == 0 else (lambda i: nc - 1 - i)

    def body(x_ref, b_ref, c_ref, dt_ref, al_ref, y_ref, sts_ref, st):
        @pl.when(pl.program_id(0) == 0)
        def _():
            st[...] = jnp.zeros_like(st)

        sts_ref[0] = st[...]
        cum, cum_t, dt_e, w_e, ce_e = _ssd_chunk_pre(dirn)(dt_ref[...], al_ref[...])
        for g in range(SSD_GROUPS):
            xc = slice(g * SSD_GW, (g + 1) * SSD_GW)
            gc = slice(g * SSD_N, (g + 1) * SSD_N)
            y, st_new = _ssd_group_fn(g, dirn, True)(x_ref[:, xc], b_ref[:, gc], c_ref[:, gc], st[:, xc], cum, cum_t,
                                               dt_e[:, xc], w_e[:, xc], ce_e[:, xc])
            y_ref[:, xc] = y
            st[:, xc] = st_new

    return pl.pallas_call(
        body, name=f"ssd_fwd_d{dirn}", grid=(nc,),
        out_shape=[jax.ShapeDtypeStruct((s_len, 2048), F32), jax.ShapeDtypeStruct((nc, SSD_N, 2048), F32)],
        in_specs=_ssd_in_specs(kk),
        out_specs=[pl.BlockSpec((CHUNK, 2048), lambda i: (kk(i), 0)),
                   pl.BlockSpec((1, SSD_N, 2048), lambda i: (kk(i), 0, 0))],
        scratch_shapes=[pltpu.VMEM((SSD_N, 2048), F32)],
        compiler_params=_params(("arbitrary",), VMEM_BIG),
    )(xbc, xbc, xbc, dt, alog)


def _ssd_bwd(xbc, dt, alog, states, dy, d_e, dirn):
    s_len = xbc.shape[0]
    nc = s_len // CHUNK
    kk = (lambda i: nc - 1 - i) if dirn == 0 else (lambda i: i)

    def body(x_ref, b_ref, c_ref, dt_ref, al_ref, sts_ref, dy_ref, de_ref, dx_ref, ddt_ref, dal_ref, dst):
        @pl.when(pl.program_id(0) == 0)
        def _():
            dst[...] = jnp.zeros_like(dst)
            dal_ref[...] = jnp.zeros_like(dal_ref)

        (cum, cum_t, dt_e, w_e, ce_e), pre_vjp = jax.vjp(_ssd_chunk_pre(dirn), dt_ref[...], al_ref[...])
        dcum = jnp.zeros_like(cum)
        dcum_t = jnp.zeros_like(cum_t)
        d_dt_e, d_w_e, d_ce_e = [], [], []
        for g in range(SSD_GROUPS):
            xc = slice(g * SSD_GW, (g + 1) * SSD_GW)
            gc = slice(g * SSD_N, (g + 1) * SSD_N)
            _, vjp = jax.vjp(_ssd_group_fn(g, dirn, False), x_ref[:, xc], b_ref[:, gc], c_ref[:, gc], sts_ref[0, :, xc], cum, cum_t,
                             dt_e[:, xc], w_e[:, xc], ce_e[:, xc])
            dyg = dy_ref[:, xc]
            dxs, dbm, dcm, dst_g, dcum_g, dcum_t_g, ddte_g, dwe_g, dcee_g = vjp((dyg, dst[:, xc]))
            if dirn == 0:
                dxs = dxs + dyg * de_ref[:, xc]
            dx_ref[:, xc] = dxs
            dx_ref[:, 2048 + g * SSD_N:2048 + (g + 1) * SSD_N] = dbm
            dx_ref[:, 3072 + g * SSD_N:3072 + (g + 1) * SSD_N] = dcm
            dst[:, xc] = dst_g
            dcum = dcum + dcum_g
            dcum_t = dcum_t + dcum_t_g
            d_dt_e.append(ddte_g)
            d_w_e.append(dwe_g)
            d_ce_e.append(dcee_g)
        ddt, dal = pre_vjp((dcum, dcum_t, jnp.concatenate(d_dt_e, axis=1), jnp.concatenate(d_w_e, axis=1),
                            jnp.concatenate(d_ce_e, axis=1)))
        ddt_ref[...] = ddt
        dal_ref[...] += dal

    return pl.pallas_call(
        body, name=f"ssd_bwd_d{dirn}", grid=(nc,),
        out_shape=[jax.ShapeDtypeStruct((s_len, 4096), F32), jax.ShapeDtypeStruct((s_len, 2 * SSD_HEADS), F32),
                   jax.ShapeDtypeStruct((1, 2 * SSD_HEADS), F32)],
        in_specs=_ssd_in_specs(kk) + [pl.BlockSpec((1, SSD_N, 2048), lambda i: (kk(i), 0, 0)),
                                      pl.BlockSpec((CHUNK, 2048), lambda i: (kk(i), 0)),
                                      pl.BlockSpec((1, 2048), lambda i: (0, 0))],
        out_specs=[pl.BlockSpec((CHUNK, 4096), lambda i: (kk(i), 0)),
                   pl.BlockSpec((CHUNK, 2 * SSD_HEADS), lambda i: (kk(i), 0)),
                   pl.BlockSpec((1, 2 * SSD_HEADS), lambda i: (0, 0))],
        scratch_shapes=[pltpu.VMEM((SSD_N, 2048), F32)],
        compiler_params=_params(("arbitrary",), VMEM_BIG),
    )(xbc, xbc, xbc, dt, alog, states, dy, d_e)


def _gate_norm_fn(yf, yb, xs, z, d_e, nw):
    yg = (yf + yb + xs * d_e) * _silu(z)
    return yg * lax.rsqrt(jnp.mean(yg * yg, axis=-1, keepdims=True) + NORM_EPS) * nw


def _gate_norm_bwd(dy, w_out, yf, yb, xbc, z, d_e, nw):
    def fn(du, yf, yb, xs, z, d_e, nw):
        sig = jax.nn.sigmoid(z)
        gate = z * sig
        ysum = yf + yb + xs * d_e
        yg = ysum * gate
        r = lax.rsqrt(jnp.mean(yg * yg, axis=-1, keepdims=True) + NORM_EPS)
        t = du * nw
        dyg = t * r - yg * (jnp.mean(t * yg, axis=-1, keepdims=True) * (r * r * r))
        dys = dyg * gate
        dz = dyg * ysum * (sig * (1.0 + z * (1.0 - sig)))
        dnw = jnp.sum(du * yg * r, axis=0, keepdims=True)
        dde = jnp.sum(dys * xs, axis=0, keepdims=True)
        hh = lax.broadcasted_iota(jnp.int32, (2048, SSD_HEADS), 0) // HEAD_DIM
        jj = lax.broadcasted_iota(jnp.int32, (2048, SSD_HEADS), 1)
        return [dys, dz], [dnw, _hnn(jnp.broadcast_to(dde, (8, 2048)), (hh == jj).astype(F32))[0:1]]

    (dys, dz), (g_nw, g_d) = _matmul_rows("ssd_out_dx_gate_norm_bwd", dy, w_out, "nt", 256, dy.shape[1], fn,
                                          [yf, yb, (xbc, 2048, 0), z], [d_e, nw], [(2048, F32), (2048, BF16)],
                                          [(1, 2048), (1, SSD_HEADS)])
    return dys, dz, g_nw, g_d


def _ssd_tail_loss(yf, yb, xbc, z, d_e, snw, w_out, x1, tgt, gate, fnw):
    dm = x1.shape[1]
    si = yf.shape[1]

    def make_u(yf, yb, xs, z, x1, tgt, d_e, snw, gate, fnw):
        return _gate_norm_fn(yf, yb, xs, z, d_e, snw).astype(BF16)

    def fn(y1, u, yf, yb, xs, z, x1, tgt, d_e, snw, gate, fnw):
        def head(x2, fnw):
            yf = (x2 * lax.rsqrt(jnp.mean(x2 * x2, axis=-1, keepdims=True) + NORM_EPS)) * fnw
            err = yf - tgt
            return 0.5 * jnp.sum(jnp.mean(err * err, axis=-1, keepdims=True), axis=0, keepdims=True)

        x2 = x1 + gate * y1
        loss, vjp = jax.vjp(head, x2, fnw)
        dx2, dfnw = vjp(jnp.ones((1, 1), F32))
        return [u, dx2, gate * dx2], [dfnw, jnp.sum(dx2 * y1, axis=0, keepdims=True), jnp.broadcast_to(loss, (1, 128))]

    (u, dx2, dy1), (g_fnw, dgate, loss) = _matmul_rows(
        "ssd_out_loss", make_u, w_out, "nn", 256, si, fn, [yf, yb, (xbc, si, 0), z, x1, tgt], [d_e, snw, gate, fnw],
        [(si, BF16), (dm, F32), (dm, BF16)], [(1, dm), (1, dm), (1, 128)])
    return u, dx2, dy1, g_fnw, dgate, loss


def _softplus_fwd(dt_raw, bias):
    (dt,), _ = _rowwise("dt_softplus", lambda r, b: ([jax.nn.softplus(r + b)], []), [dt_raw], [bias],
                        [(dt_raw.shape[1], F32)], [], 512)
    return dt


def _softplus_bwd(ddt_f, ddt_b, dt_raw, bias):
    def fn(df, db, r, b):
        g = (df + db) * jax.nn.sigmoid(r + b)
        return [g], [jnp.sum(g, axis=0, keepdims=True)]

    w = dt_raw.shape[1]
    (g,), (gb,) = _rowwise("dt_softplus_bwd", fn, [ddt_f, ddt_b, dt_raw], [bias], [(w, BF16)], [(1, w)], 512)
    return g, gb


def _whole(a):
    nd = len(a.shape)
    return pl.BlockSpec(a.shape, lambda *_: (0,) * nd)


def _mod_part(c_all, mod_w):
    nl, _, ncol = mod_w.shape
    nb = c_all.shape[0]

    def body(c_ref, w_ref, o_ref):
        cond = _silu(c_ref[...])
        for i in range(nl):
            o_ref[i * nb:(i + 1) * nb, :] = _nn(cond, w_ref[i])

    return pl.pallas_call(body, name="mod_part", out_shape=jax.ShapeDtypeStruct((nl * nb, ncol), F32),
                          compiler_params=_params(None, VMEM_BIG))(c_all, mod_w)


def _mod_finish(mod_nb, mod_b, norm_w, tokens):
    nl, dm = norm_w.shape

    def body(a_ref, b_ref, nw_ref, *rest):
        tok_refs, o_refs = rest[:len(tokens)], rest[len(tokens):]
        tok = sum(t[0:1, 0:1] for t in tok_refs)
        for i in range(nl):
            for k in range(3):
                cols = slice(k * dm, (k + 1) * dm)
                o_refs[4 * i + k][...] = a_ref[i:i + 1, cols] + b_ref[i:i + 1, cols]
            o_refs[4 * i + 3][...] = nw_ref[i:i + 1, :] + tok

    rows = pl.pallas_call(body, name="mod_finish", out_shape=[jax.ShapeDtypeStruct((1, dm), F32)] * (4 * nl))(
        mod_nb, mod_b, norm_w, *tokens)
    return [rows[4 * i:4 * i + 4] for i in range(nl)]


def _mod_grad(c_all, dmod_sh):
    nl, nb, ncol = dmod_sh.shape
    dm = c_all.shape[1]

    def body(c_ref, d_ref, o_ref):
        cond = _silu(c_ref[...])
        for i in range(nl):
            o_ref[i] = _tn(cond, d_ref[i])

    return pl.pallas_call(body, name="mod_grad", out_shape=jax.ShapeDtypeStruct((nl, dm, ncol), F32),
                          compiler_params=_params(None, VMEM_BIG))(c_all, dmod_sh)


PACK_ROWS = 16
PACK_COLS = 1024


def _pack_small(rows, b64, a64s, d32, extra):
    nr, na = len(rows), len(a64s)

    def body(*refs):
        o_ref = refs[-1]
        o_ref[...] = jnp.zeros_like(o_ref)
        for i in range(nr):
            o_ref[i:i + 1, :] = refs[i][...]
        b_ref, a_refs, d_ref, e_ref = refs[nr], refs[nr + 1:nr + 1 + na], refs[nr + 1 + na], refs[nr + 2 + na]
        o_ref[nr:nr + 1, 0:64] = b_ref[...]
        o_ref[nr:nr + 1, 64:128] = sum(a[...] for a in a_refs)
        o_ref[nr:nr + 1, 128:160] = d_ref[...]
        o_ref[nr:nr + 1, 256:384] = e_ref[...]

    return pl.pallas_call(body, name="pack_small", out_shape=jax.ShapeDtypeStruct((PACK_ROWS, PACK_COLS), F32))(
        *rows, b64, *a64s, d32, extra)


def _pack_ssd_small(cw, cb, nw):
    def body(cw_ref, cb_ref, nw_ref, o_ref):
        o_ref[...] = jnp.zeros_like(o_ref)
        o_ref[0:5, :] = cw_ref[...]
        o_ref[5:6, :] = cb_ref[...]
        o_ref[6:7, 0:256] = nw_ref[...]

    return pl.pallas_call(body, name="pack_ssd_small", out_shape=jax.ShapeDtypeStruct((8, 512), F32))(cw, cb, nw)


def _sum_parts(p_ref):
    g = p_ref[0].astype(F32)
    for s in range(1, p_ref.shape[0]):
        g = g + p_ref[s].astype(F32)
    return g


def _adam_update(w, g, m, v):
    m2 = ADAM_B1 * m + (1.0 - ADAM_B1) * g
    v2 = ADAM_B2 * v + (1.0 - ADAM_B2) * (g * g)
    m_hat = m2 / (1.0 - ADAM_B1 ** ADAM_STEP)
    v_hat = v2 / (1.0 - ADAM_B2 ** ADAM_STEP)
    return -ADAM_LR * (m_hat / (jnp.sqrt(v_hat) + ADAM_EPS) + ADAM_WD * w), m2, v2


def _adamw_windows(name, parts, params, windows, extra=None):
    n = len(params)

    def body(p_ref, *rest):
        ins, outs = rest[:3 * n], rest[3 * n:]
        g = _sum_parts(p_ref)
        for pi, rows, cols, idx in windows:
            w_ref, m_ref, v_ref = ins[3 * pi:3 * pi + 3]
            gw = g[rows, cols]
            dw, m2, v2 = _adam_update(w_ref[idx], gw, m_ref[idx], v_ref[idx])
            for o_ref, val in zip(outs[4 * pi:4 * pi + 4], (gw, dw, m2, v2), strict=True):
                o_ref[idx] = val
        if extra is not None:
            outs[4 * n][...] = g[extra[0], extra[1]]

    out_shape = [jax.ShapeDtypeStruct(w.shape, F32) for (w, _, _) in params for _ in range(4)]
    if extra is not None:
        out_shape.append(jax.ShapeDtypeStruct((extra[0].stop - extra[0].start, extra[1].stop - extra[1].start), F32))
    res = pl.pallas_call(body, name=name, out_shape=out_shape)(parts, *[a for p in params for a in p])
    return [res[4 * i:4 * i + 4] for i in range(n)] + ([res[4 * n]] if extra is not None else [])


def _adamw(name, w, parts, m, v, tr, tc=None):
    r_, c_ = w.shape
    p_ = parts.shape[0]
    tr = min(tr, r_)
    tc = c_ if tc is None else tc
    assert r_ % tr == 0 and c_ % tc == 0

    def body(w_ref, p_ref, m_ref, v_ref, g_ref, d_ref, m2_ref, v2_ref):
        g = _sum_parts(p_ref)
        g_ref[...] = g
        d_ref[...], m2_ref[...], v2_ref[...] = _adam_update(w_ref[...], g, m_ref[...], v_ref[...])

    blk = pl.BlockSpec((tr, tc), lambda i, j: (i, j))
    return pl.pallas_call(
        body, name=name, grid=(r_ // tr, c_ // tc), out_shape=[jax.ShapeDtypeStruct((r_, c_), F32)] * 4,
        in_specs=[blk, pl.BlockSpec((p_, tr, tc), lambda i, j: (0, i, j)), blk, blk], out_specs=[blk] * 4,
        compiler_params=_params(("parallel", "parallel"), VMEM_BIG),
    )(w, parts, m, v)


def _dev_index(p):
    return 4 * p[0] + 2 * p[1] + p[2]


def _all_gather(name, xs):
    n = len(xs)
    hbm = pl.BlockSpec(memory_space=pl.ANY)

    def body(*refs):
        x_refs, o_refs = refs[:n], refs[n:2 * n]
        send_sems, recv_sems, local_sems = refs[2 * n:]
        x, y, c = lax.axis_index("x"), lax.axis_index("y"), lax.axis_index("c")
        me, sibling = (x, y, c), (x, y, 1 - c)
        chips = [(1 - x, y), (x, 1 - y), (1 - x, 1 - y)]

        def copy(a, k, block, to, src=None):
            dst = o_refs[a].at[_dev_index(block)]
            return pltpu.make_async_remote_copy(
                src_ref=dst if src is None else src, dst_ref=dst, send_sem=send_sems.at[a, k],
                recv_sem=recv_sems.at[a, k], device_id=to, device_id_type=MESH)

        mine = [pltpu.make_async_copy(x_refs[a], o_refs[a].at[_dev_index(me)], local_sems.at[a]) for a in range(n)]
        for cp in mine:
            cp.start()
        first = []
        for a in range(n):
            first.append(copy(a, 0, me, sibling, src=x_refs[a]))
            first += [copy(a, 1 + j, me, (*chip, c), src=x_refs[a]) for j, chip in enumerate(chips)]
        for cp in first:
            cp.start()
        passed = []
        for j, chip in enumerate(chips):
            for a in range(n):
                copy(a, 1 + j, (*chip, c), me).wait_recv()
                cp = copy(a, 4 + j, (*chip, c), sibling)
                cp.start()
                passed.append(cp)
        for a in range(n):
            copy(a, 0, sibling, me).wait_recv()
            for j, chip in enumerate(chips):
                copy(a, 4 + j, (*chip, 1 - c), me).wait_recv()
        for cp in first + passed:
            cp.wait_send()
        for cp in mine:
            cp.wait()

    return pl.pallas_call(
        body, name=name, out_shape=[jax.ShapeDtypeStruct((NDEV, *x.shape), x.dtype) for x in xs],
        in_specs=[hbm] * n, out_specs=[hbm] * n,
        scratch_shapes=[pltpu.SemaphoreType.DMA((n, 7)), pltpu.SemaphoreType.DMA((n, 7)), pltpu.SemaphoreType.DMA((n,))],
    )(*xs)


_HBM = pl.BlockSpec(memory_space=pltpu.HBM)
_SEM = pl.BlockSpec(memory_space=pltpu.SEMAPHORE)
_EFFECT = pltpu.SideEffectType.DATAFLOW_SIDE_EFFECTING


def _mesh_position():
    return lax.axis_index("x"), lax.axis_index("y"), lax.axis_index("c")


def _peers(me):
    return [(k, tuple(1 - v if (k >> b) & 1 else v for v, b in zip(me, (2, 1, 0)))) for k in range(1, NDEV)]


EXCHANGE_COPIES = {"gather": NDEV - 1, "scatter": NDEV - 1, "pair": 4, "chips": 3}
NCHIP = NDEV // 2


def _landing_zones(name, xs, mode):
    x_, y_, c_ = _mesh_position()
    mine = (2 * x_ + y_ if mode == "chips" else _dev_index((x_, y_, c_))).astype(jnp.int32).reshape(1)
    lands = []
    for a, x in enumerate(xs):
        rows, cols = x.shape[-2:]
        if mode == "pair":
            lands.append(lax.empty((NCHIP, rows, cols), x.dtype))
            continue
        tr = 256 if rows % 256 == 0 else rows

        def body(me_ref, x_ref, o_ref):
            o_ref[...] = x_ref[...]

        if mode == "gather":
            in_spec = pl.BlockSpec((tr, cols), lambda i, me_ref: (i, 0))
        else:
            in_spec = pl.BlockSpec((None, tr, cols), lambda i, me_ref: (me_ref[0], i, 0))
        lands.append(pl.pallas_call(
            body, name=f"{name}_{a}",
            out_shape=jax.ShapeDtypeStruct((NCHIP if mode == "chips" else NDEV, rows, cols), x.dtype),
            grid_spec=pltpu.PrefetchScalarGridSpec(
                num_scalar_prefetch=1, grid=(rows // tr,), in_specs=[in_spec],
                out_specs=pl.BlockSpec((None, tr, cols), lambda i, me_ref: (me_ref[0], i, 0))),
            compiler_params=_params(("arbitrary",)),
        )(mine, x))
    return lands


def _exchange_copies(x_refs, land_refs, send_sems, recv_sems, mode):
    x_, y_, c_ = me = _mesh_position()
    per_array = EXCHANGE_COPIES[mode]
    out = []

    def add(a, k, src, dst, peer):
        sem = a * per_array + k
        out.append(pltpu.make_async_remote_copy(src_ref=src, dst_ref=dst, send_sem=send_sems.at[sem], recv_sem=recv_sems.at[sem],
                                                device_id=peer, device_id_type=MESH))

    for a, (x_ref, land_ref) in enumerate(zip(x_refs, land_refs)):
        if mode in ("gather", "scatter"):
            for k, peer in _peers(me):
                add(a, k - 1, x_ref.at[_dev_index(peer)] if mode == "scatter" else x_ref, land_ref.at[_dev_index(me)], peer)
        elif mode == "pair":
            for chip in range(NCHIP):
                add(a, chip, x_ref.at[2 * chip + 1 - c_], land_ref.at[chip], (x_, y_, 1 - c_))
        else:
            for k in range(1, NCHIP):
                px, py = (1 - x_ if k & 2 else x_), (1 - y_ if k & 1 else y_)
                add(a, k - 1, x_ref.at[2 * px + py], land_ref.at[2 * x_ + y_], (px, py, c_))
    return out


def _exchange_start(name, xs, lands, mode, dep):
    n = len(xs)

    def body(*refs):
        x_refs, land_refs = refs[:n], refs[n:2 * n]
        send_sems, recv_sems = refs[2 * n + 1], refs[2 * n + 2]
        token = refs[-1]
        for cp in _exchange_copies(x_refs, land_refs, send_sems, recv_sems, mode):
            cp.start()
        token[...] = jnp.zeros_like(token)

    sems = pltpu.SemaphoreType.DMA((n * EXCHANGE_COPIES[mode],))
    res = pl.pallas_call(
        body, name=name,
        out_shape=(sems, sems, *[pltpu.HBM(a.shape, a.dtype) for a in (*xs, *lands)], jax.ShapeDtypeStruct((8, 128), F32)),
        in_specs=[_HBM] * (2 * n) + [pl.BlockSpec(memory_space=pl.ANY)],
        out_specs=(_SEM, _SEM, *[_HBM] * (2 * n), pl.BlockSpec(memory_space=pltpu.VMEM)),
        input_output_aliases={i: 2 + i for i in range(2 * n)},
        compiler_params=pltpu.CompilerParams(has_side_effects=_EFFECT),
    )(*[pltpu.with_memory_space_constraint(a, pltpu.HBM) for a in (*xs, *lands)], dep)
    return res[:-1], res[-1]


def _exchange_wait(name, handles, mode, after):
    send_sems, recv_sems = handles[0], handles[1]
    bufs = handles[2:]
    n = len(bufs) // 2

    def body(*refs):
        x_refs, land_refs = refs[:n], refs[n:2 * n]
        s_sems, r_sems = refs[2 * n], refs[2 * n + 1]
        for cp in _exchange_copies(x_refs, land_refs, s_sems, r_sems, mode):
            cp.wait_send()
            cp.wait_recv()

    res = pl.pallas_call(
        body, name=name, out_shape=tuple(pltpu.HBM(a.shape, a.dtype) for a in bufs),
        in_specs=[_HBM] * (2 * n) + [_SEM, _SEM, pl.BlockSpec(memory_space=pl.ANY)], out_specs=tuple([_HBM] * (2 * n)),
        input_output_aliases={i: i for i in range(2 * n)},
        compiler_params=pltpu.CompilerParams(has_side_effects=_EFFECT),
    )(*bufs, send_sems, recv_sems, after)
    return res[n:]


def _pair_sum(name, x, from_sibling):
    _, rows, cols = x.shape
    tr = 256 if rows % 256 == 0 else rows
    core = lax.axis_index("c").astype(jnp.int32).reshape(1)

    def body(c_ref, x_ref, s_ref, o_ref):
        o_ref[...] = (x_ref[...].astype(F32) + s_ref[...].astype(F32)).astype(o_ref.dtype)

    return pl.pallas_call(
        body, name=name, out_shape=jax.ShapeDtypeStruct((NCHIP, rows, cols), x.dtype),
        grid_spec=pltpu.PrefetchScalarGridSpec(
            num_scalar_prefetch=1, grid=(NCHIP, rows // tr),
            in_specs=[pl.BlockSpec((None, tr, cols), lambda j, i, c_ref: (2 * j + c_ref[0], i, 0)),
                      pl.BlockSpec((None, tr, cols), lambda j, i, c_ref: (j, i, 0))],
            out_specs=pl.BlockSpec((None, tr, cols), lambda j, i, c_ref: (j, i, 0))),
        compiler_params=_params(("parallel", "parallel")),
    )(core, x, from_sibling)


def kernel(x, c, positions, norm_w, mod_w, mod_b, attn_w_in, attn_w_out, ssd_w_in, ssd_conv_w, ssd_conv_b, ssd_dt_bias, ssd_a_log, ssd_d, ssd_norm_w, ssd_w_out, final_norm_w, loss_target, m_norm_w, m_mod_w, m_mod_b, m_attn_w_in, m_attn_w_out, m_ssd_w_in, m_ssd_conv_w, m_ssd_conv_b, m_ssd_dt_bias, m_ssd_a_log, m_ssd_d, m_ssd_norm_w, m_ssd_w_out, m_final_norm_w, v_norm_w, v_mod_w, v_mod_b, v_attn_w_in, v_attn_w_out, v_ssd_w_in, v_ssd_conv_w, v_ssd_conv_b, v_ssd_dt_bias, v_ssd_a_log, v_ssd_d, v_ssd_norm_w, v_ssd_w_out, v_final_norm_w):
    s_len, dm = x.shape[1], x.shape[2]
    me = 4 * lax.axis_index("x") + 2 * lax.axis_index("y") + lax.axis_index("c")
    x0 = x.reshape(s_len, dm)
    tgt = loss_target.reshape(s_len, dm)
    aw = 3 * 512
    si = 2 * dm
    sxbc = 2 * si
    n_ssd_in = ssd_w_in.shape[2] * NDEV

    g_ai, c_all = _all_gather("gather_attn_w_in", [attn_w_in[0].astype(BF16), c])
    w_ai = g_ai.transpose(1, 0, 2).reshape(dm, 4 * aw)
    c_all = c_all.reshape(NDEV, dm)

    part = _mod_part(c_all, mod_w)
    (part_all,) = _all_gather("gather_mod", [part])
    mod_nb = jnp.stack([lax.dynamic_index_in_dim(part_all, i * NDEV + me, axis=1, keepdims=False).reshape(3 * dm)
                        for i in range(2)])

    ssd_small = _pack_ssd_small(ssd_conv_w[0], ssd_conv_b, ssd_norm_w)
    ao_shard = [attn_w_out[0].astype(BF16)]
    ao_handles, ao_token = _exchange_start("w_out_start", ao_shard, _landing_zones("w_out_place", ao_shard, "gather"), "gather",
                                           part_all)
    late_shards = [ssd_w_in[0].T.astype(BF16), ssd_w_out[0].astype(BF16), ssd_small]
    w_handles, w_token = _exchange_start("weights_start", late_shards, _landing_zones("weights_place", late_shards, "gather"),
                                         "gather", ao_token)
    (shift0, scale0, gate0, nw0), (shift1, scale1, gate1, nw1) = _mod_finish(mod_nb, mod_b, norm_w, [ao_token, w_token])
    shift, scale, gate, nw = [shift0, shift1], [scale0, scale1], [gate0, gate1], [nw0, nw1]

    hn0 = _norm_mod_fwd("norm0", x0, nw[0], scale[0], shift[0])
    inv_freq = ROPE_THETA ** (-jnp.arange(0, ROT_DIM, 2, dtype=F32) / ROT_DIM)
    lane = jnp.arange(128) % HEAD_DIM
    inv_row = jnp.where(lane < ROT_DIM, inv_freq[lane % (ROT_DIM // 2)], 0.0).reshape(1, 128).astype(F32)
    tabs = _rope_tables(positions.reshape(s_len, 1), inv_row)
    qk = _matmul("proj_qk", hn0, w_ai, "nn", F32, MM_T, MM_T, dm, epilogue=_rot_fwd, mrows=tabs, n_out=2 * aw)
    v = _matmul("proj_vz", hn0, w_ai, "nn", F32, MM_T, MM_T, dm, b_noff=2 * aw, n_out=2 * aw)
    z0 = (v, 1)
    att = [_attn_fwd(g, qk, v) for g in range(3)]
    os_, lses = [a[0] for a in att], [a[1] for a in att]
    (g_ao,) = _exchange_wait("w_out_wait", ao_handles, "gather", lses[2])
    a0, y0, x1 = _attn_out(os_, lses, z0, x0, gate[0], g_ao.reshape(aw, dm))

    hn1 = _norm_mod_fwd("norm1", x1, nw[1], scale[1], shift[1])
    g_si, g_so, g_small = _exchange_wait("weights_wait", w_handles, "gather", hn1)
    w_ao = g_ao.reshape(aw, dm)
    w_si_t = g_si.reshape(n_ssd_in, dm)
    w_so = g_so.reshape(si, dm)
    conv_w = g_small[:, 0:CONV_WIDTH, :].transpose(1, 0, 2).reshape(CONV_WIDTH, sxbc)
    conv_b = g_small[:, 5, :].reshape(1, sxbc)
    snw = g_small[:, 6, 0:si // NDEV].reshape(1, si)
    ndt = 2 * SSD_HEADS
    z1 = _matmul("ssd_proj_z", hn1, w_si_t, "nt", F32, MM_T, MM_T, dm, n_out=si)
    xpre = _matmul("ssd_proj_xbc", hn1, w_si_t, "nt", F32, MM_T, MM_T, dm, b_noff=si, n_out=sxbc)
    dt_raw = _matmul("ssd_proj_dt", hn1, w_si_t, "nt", F32, MM_T, ndt, dm, b_noff=si + sxbc, n_out=ndt)
    xbc = _conv_fwd(xpre, conv_w, conv_b)
    dt_bias = ssd_dt_bias.reshape(1, 2 * SSD_HEADS)
    alog = ssd_a_log.reshape(1, 2 * SSD_HEADS)
    dt = _softplus_fwd(dt_raw, dt_bias)
    y_f, st_f = _ssd_fwd(xbc, dt, alog, 0)
    y_b, st_b = _ssd_fwd(xbc, dt, alog, 1)
    d_e = jnp.repeat(ssd_d.reshape(SSD_HEADS), HEAD_DIM).reshape(1, si)

    fnw = final_norm_w.reshape(1, dm)
    u, dx2, dy1, g_fnw, dgate1, loss_part = _ssd_tail_loss(y_f, y_b, xbc, z1, d_e, snw, w_so, x1, tgt, gate[1], fnw)
    gw_so = _matmul("ssd_out_dw", u, dy1, "tn", BF16, MM_T, MM_T, MM_T)
    dys, dz1, g_snw, g_d = _gate_norm_bwd(dy1, w_so, y_f, y_b, xbc, z1, d_e, snw)
    dxbc_f, ddt_f, dalog_f = _ssd_bwd(xbc, dt, alog, st_f, dys, d_e, 0)
    dxbc_b, ddt_b, dalog_b = _ssd_bwd(xbc, dt, alog, st_b, dys, d_e, 1)
    dpre, g_cw, g_cb = _conv_bwd(xpre, dxbc_f, dxbc_b, conv_w, conv_b)
    ddt_raw, g_dtb = _softplus_bwd(ddt_f, ddt_b, dt_raw, dt_bias)
    dhn1 = [_matmul("ssd_proj_z_dx", dz1, w_si_t, "nn", F32, MM_T, MM_T, MM_T),
            _matmul("ssd_proj_xbc_dx", dpre, w_si_t, "nn", F32, MM_T, MM_T, MM_T, b_koff=si)]
    gw_si_t = _matmul("ssd_proj_z_dw", dz1, hn1, "tn", BF16, MM_T, MM_T, MM_T, dest=(n_ssd_in, 0, None))
    gw_si_t = _matmul("ssd_proj_xbc_dw", dpre, hn1, "tn", BF16, MM_T, MM_T, MM_T, dest=(n_ssd_in, si, gw_si_t))
    gw_si_t = _matmul("ssd_proj_dt_dw", ddt_raw, hn1, "tn", BF16, ndt, MM_T, MM_T, dest=(n_ssd_in, si + sxbc, gw_si_t))

    l1_grads = [gw_so.reshape(NDEV, si // NDEV, dm), gw_si_t.reshape(NDEV, n_ssd_in // NDEV, dm),
                _pack_ssd_small_blocks(g_cw, g_cb, g_snw)]
    l1_handles, l1_token = _exchange_start("l1_grads_start", l1_grads, _landing_zones("l1_grads_place", l1_grads, "scatter"),
                                           "scatter", dhn1[1])
    dx1, dy0, g_nw1, dsc1, dsh1, dgate0 = _norm_mod_bwd(
        "ssd_proj_dt_dx_norm1_bwd", (ddt_raw, w_si_t, "nn", ndt, dict(b_koff=si + sxbc)), x1, dhn1, dx2,
        nw[1], scale[1], shift[1], prev=(y0, gate[0] + l1_token[0:1, 0:1]))

    gw_ao = _matmul("attn_out_dw", a0, dy0, "tn", BF16, aw // 2, MM_T, MM_T)
    dos, dls, dz0 = _mix_bwd(dy0, w_ao, os_, lses, z0)
    datt = [_attn_bwd(g, qk, v, os_[g], lses[g], dos[g], dls[g]) for g in range(3)]
    dqkv = _rot_pack_bwd([t[0] for t in datt], [t[1] for t in datt], [t[2] for t in datt], tabs)
    wcol = attn_w_in.shape[2]
    gw_ai = _matmul("proj_qkv_dw", hn0, dqkv, "tn", BF16, MM_T, wcol, MM_T, out_blocks=3 * aw // wcol, dest=(NDEV, 0, None))
    gw_ai = _matmul("proj_z_dw", hn0, dz0, "tn", BF16, MM_T, wcol, MM_T, out_blocks=aw // wcol,
                    dest=(NDEV, 3 * aw // wcol, gw_ai))
    after_start = lambda acc, t: acc + t
    zero_row = lambda token: jnp.tile(token[0:1], (1, dm // 128))
    l0_grads = [gw_ai, gw_ao.reshape(NDEV, aw // NDEV, dm)]
    pair_handles, pair_token = _exchange_start("l0_pair_start", l0_grads, _landing_zones("l0_pair_place", l0_grads, "pair"),
                                               "pair", dqkv)
    dhn0_z = _matmul("proj_z_dx", dz0, w_ai, "nt", F32, MM_T, MM_T, aw, b_koff=3 * aw, n_out=dm, epilogue=after_start,
                     ncols=(zero_row(pair_token),))
    from_sibling = _exchange_wait("l0_pair_wait", pair_handles, "pair", dhn0_z)
    chip_sums = [_pair_sum(f"l0_pair_sum_{a}", g, s) for a, (g, s) in enumerate(zip(l0_grads, from_sibling))]
    l0_handles, l0_token = _exchange_start("l0_grads_start", chip_sums, _landing_zones("l0_grads_place", chip_sums, "chips"),
                                           "chips", dhn0_z)
    dx0, g_nw0, dsc0, dsh0 = _norm_mod_bwd(
        "proj_qkv_dx_norm0_bwd", (dqkv, w_ai, "nt", aw, dict(n_out=dm)), x0, [dhn0_z], dx1,
        nw[0], scale[0], shift[0] + zero_row(l0_token))

    small_g = [_pack_small([dsh0, dsc0, dgate0, dsh1, dsc1, dgate1, g_nw0, g_nw1, g_fnw], g_dtb, [dalog_f, dalog_b], g_d, loss_part)]
    sm_handles, sm_token = _exchange_start("small_grads_start", small_g, _landing_zones("small_grads_place", small_g, "gather"),
                                           "gather", dx0)

    whole = (slice(None), slice(None))
    r_so, r_si, r_small = _exchange_wait("l1_grads_wait", l1_handles, "scatter", sm_token)
    si_out = [o.T for o in _adamw("adamw_ssd_w_in", ssd_w_in[0].T, r_si, m_ssd_w_in[0].T, v_ssd_w_in[0].T, n_ssd_in // NDEV, 256)]
    so_out = _adamw("adamw_ssd_w_out", ssd_w_out[0], r_so, m_ssd_w_out[0], v_ssd_w_out[0], 256)
    cw_cols = ssd_conv_w.shape[2]
    cw_out, cb_out, snw_out = _adamw_windows(
        "adamw_ssd_small", r_small,
        [(ssd_conv_w, m_ssd_conv_w, v_ssd_conv_w), (ssd_conv_b, m_ssd_conv_b, v_ssd_conv_b),
         (ssd_norm_w, m_ssd_norm_w, v_ssd_norm_w)],
        [(0, slice(0, CONV_WIDTH), slice(0, cw_cols), (0, slice(None), slice(None))),
         (1, slice(5, 6), slice(0, cw_cols), whole), (2, slice(6, 7), slice(0, si // NDEV), whole)])
    r_ai, r_ao = _exchange_wait("l0_grads_wait", l0_handles, "chips", so_out[0])
    ai_out = _adamw("adamw_attn_w_in", attn_w_in[0], r_ai, m_attn_w_in[0], v_attn_w_in[0], 256)
    ao_out = _adamw("adamw_attn_w_out", attn_w_out[0], r_ao, m_attn_w_out[0], v_attn_w_out[0], 192)

    (small_all,) = _exchange_wait("small_grads_wait", sm_handles, "gather", ai_out[0])
    full = slice(0, PACK_COLS)
    nhd = SSD_HEADS
    windows = [(0, slice(3 * i + k, 3 * i + k + 1), full, (slice(i, i + 1), slice(k * dm, (k + 1) * dm)))
               for i in range(2) for k in range(3)]
    windows += [(1, slice(6 + i, 7 + i), full, (slice(i, i + 1), slice(None))) for i in range(2)]
    windows += [(2, slice(8, 9), full, whole)]
    windows += [(3 + q, slice(9, 10), slice(2 * nhd * q + nhd * j, 2 * nhd * q + nhd * (j + 1)), (0, slice(j, j + 1), slice(None)))
                for q in range(2) for j in range(2)]
    windows += [(5, slice(9, 10), slice(4 * nhd, 5 * nhd), whole)]
    as_row = lambda a: a.reshape(1, dm)
    mb_out, nw_out, fnw_out, dtb_out, alog_out, d_out, loss = _adamw_windows(
        "adamw_small", small_all,
        [(mod_b, m_mod_b, v_mod_b), (norm_w, m_norm_w, v_norm_w), (fnw, as_row(m_final_norm_w), as_row(v_final_norm_w)),
         (ssd_dt_bias, m_ssd_dt_bias, v_ssd_dt_bias), (ssd_a_log, m_ssd_a_log, v_ssd_a_log), (ssd_d, m_ssd_d, v_ssd_d)],
        windows, extra=(slice(9, 10), slice(256, 257)))
    loss = loss.reshape(())

    ncol = mod_w.shape[2]
    dmod_all = small_all[:, 0:6, :].reshape(NDEV, 2, 3 * dm)
    dmod_sh = lax.dynamic_slice_in_dim(dmod_all, me * ncol, ncol, axis=2).transpose(1, 0, 2)
    g_modw = _mod_grad(c_all, dmod_sh).reshape(1, 2 * dm, ncol)
    modw_out = _adamw("adamw_mod_w", mod_w.reshape(2 * dm, ncol), g_modw, m_mod_w.reshape(2 * dm, ncol),
                      v_mod_w.reshape(2 * dm, ncol), 256)

    per_kind = []
    for k in range(4):
        per_kind.append([
            nw_out[k], modw_out[k].reshape(mod_w.shape), mb_out[k], ai_out[k][None], ao_out[k][None], si_out[k][None],
            cw_out[k], cb_out[k], dtb_out[k], alog_out[k], d_out[k], snw_out[k], so_out[k][None], fnw_out[k].reshape(dm)])
    return (loss, dx0.reshape(x.shape), *per_kind[0], *per_kind[1], *per_kind[2], *per_kind[3])


def _pack_ssd_small_blocks(g_cw, g_cb, g_nw):
    nper = g_cw.shape[1] // NDEV
    nwper = g_nw.shape[1] // NDEV

    def body(cw_ref, cb_ref, nw_ref, o_ref):
        o_ref[...] = jnp.zeros_like(o_ref)
        for d in range(NDEV):
            o_ref[d, 0:5, :] = cw_ref[:, d * nper:(d + 1) * nper]
            o_ref[d, 5:6, :] = cb_ref[:, d * nper:(d + 1) * nper]
            o_ref[d, 6:7, 0:nwper] = nw_ref[:, d * nwper:(d + 1) * nwper]

    return pl.pallas_call(body, name="pack_ssd_small_grads", out_shape=jax.ShapeDtypeStruct((NDEV, 8, nper), F32))(g_cw, g_cb, g_nw)
```

```python
import functools
import math

import jax
import jax.numpy as jnp
from jax import lax
from jax.experimental import pallas as pl
from jax.experimental.pallas import tpu as pltpu

F32 = jnp.float32
BF16 = jnp.bfloat16
HI = lax.Precision.HIGHEST
MESH = pl.DeviceIdType.MESH
NDEV = 8

NORM_EPS = 1e-6
ROPE_THETA = 500000.0
ROT_DIM = 16
HEAD_DIM = 64
DILATIONS = (1, 4, 16)
BAND = 64
NEG_BIG = -1e30
CHUNK = 128
SSD_HEADS = 32
SSD_GROUPS = 8
CONV_WIDTH = 5

ADAM_LR = 0.001
ADAM_B1 = 0.9
ADAM_B2 = 0.999
ADAM_EPS = 1e-08
ADAM_WD = 0.01
ADAM_STEP = 10

VMEM_BIG = 56 * 1024 * 1024
MM_T = 1024


def _params(sem=None, vmem=None):
    kw = {}
    if sem is not None:
        kw["dimension_semantics"] = sem
    if vmem is not None:
        kw["vmem_limit_bytes"] = vmem
    return pltpu.CompilerParams(**kw)


def _dg(a, b, ca, cb, prec=None):
    return lax.dot_general(a, b, (((ca,), (cb,)), ((), ())), preferred_element_type=F32, precision=prec)


def _nn(a, b):
    return _dg(a.astype(BF16), b.astype(BF16), 1, 0)


def _nt(a, b):
    return _dg(a.astype(BF16), b.astype(BF16), 1, 1)


def _tn(a, b):
    return _dg(a.astype(BF16), b.astype(BF16), 0, 0)


def _hnn(a, b):
    return _dg(a, b, 1, 0, HI)


@jax.custom_vjp
def _bnn(a, b):
    return _nn(a, b)


_bnn.defvjp(lambda a, b: (_nn(a, b), (a, b)), lambda r, g: (_nt(g, r[1]), _tn(r[0], g)))


@jax.custom_vjp
def _bnt(a, b):
    return _nt(a, b)


_bnt.defvjp(lambda a, b: (_nt(a, b), (a, b)), lambda r, g: (_nn(g, r[1]), _tn(g, r[0])))


@jax.custom_vjp
def _btn(a, b):
    return _tn(a, b)


_btn.defvjp(lambda a, b: (_tn(a, b), (a, b)), lambda r, g: (_nt(r[1], g), _nn(r[0], g)))


def _silu(x):
    return x * jax.nn.sigmoid(x)


def _matmul(name, a, b, mode, out_dtype, tm, tn, tk, *, epilogue=None, tiled=(), mrows=(), ncols=(),
            b_noff=0, b_koff=0, n_out=None, out_blocks=None, dest=None):
    if mode == "tn":
        K, M = a.shape
    else:
        M, K = a.shape
    N = n_out if n_out is not None else (b.shape[0] if mode == "nt" else b.shape[1])
    tm, tn, tk = min(tm, M), min(tn, N), min(tk, K)
    assert M % tm == 0 and N % tn == 0 and K % tk == 0, (name, M, N, K, tm, tn, tk)
    assert b_noff % tn == 0 and b_koff % tk == 0
    no, ko = b_noff // tn, b_koff // tk
    nk = K // tk
    if mode == "tn":
        a_spec = pl.BlockSpec((tk, tm), lambda i, j, k: (k, i))
    else:
        a_spec = pl.BlockSpec((tm, tk), lambda i, j, k: (i, k))
    if mode == "nt":
        b_spec = pl.BlockSpec((tn, tk), lambda i, j, k: (j + no, k + ko))
    else:
        b_spec = pl.BlockSpec((tk, tn), lambda i, j, k: (k + ko, j + no))
    specs = [a_spec, b_spec]
    specs += [pl.BlockSpec((tm, tn), lambda i, j, k: (i, j)) for _ in tiled]
    specs += [pl.BlockSpec((tm, r.shape[1]), lambda i, j, k: (i, 0)) for r in mrows]
    specs += [pl.BlockSpec((1, tn), lambda i, j, k: (0, j)) for _ in ncols]
    total, off, earlier = dest if dest is not None else (None, 0, None)
    if out_blocks is None:
        assert off % tm == 0
        mo = off // tm
        out_shape = jax.ShapeDtypeStruct((M if total is None else total, N), out_dtype)
        out_spec = pl.BlockSpec((tm, tn), lambda i, j, k: (i + mo, j))
    else:
        nper = N // out_blocks
        assert nper % tn == 0
        jb = nper // tn
        out_shape = jax.ShapeDtypeStruct((out_blocks if total is None else total, M, nper), out_dtype)
        out_spec = pl.BlockSpec((None, tm, tn), lambda i, j, k: (j // jb + off, i, j % jb))
    if earlier is not None:
        assert earlier.shape == out_shape.shape and earlier.dtype == out_shape.dtype
    ne = len(tiled) + len(mrows) + len(ncols)
    dot = {"nn": _nn, "nt": _nt, "tn": _tn}[mode]

    def body(a_ref, b_ref, *rest):
        extras, o_ref = rest[:ne], rest[ne]

        def finish(acc):
            if epilogue is not None:
                acc = epilogue(acc, *[e[...] for e in extras])
            o_ref[...] = acc.astype(o_ref.dtype)

        if nk == 1:
            finish(dot(a_ref[...], b_ref[...]))
        else:
            acc_ref = rest[ne + 1]
            k = pl.program_id(2)

            @pl.when(k == 0)
            def _():
                acc_ref[...] = jnp.zeros_like(acc_ref)

            acc_ref[...] += dot(a_ref[...], b_ref[...])

            @pl.when(k == nk - 1)
            def _():
                finish(acc_ref[...])

    args = [a, b, *tiled, *mrows, *ncols]
    aliases = {}
    if earlier is not None:
        specs.append(pl.BlockSpec(memory_space=pl.ANY))
        aliases = {len(args): 0}
        args.append(earlier)

    def body_with_dest(*refs):
        body(*refs[:2 + ne], *refs[2 + ne + (earlier is not None):])

    return pl.pallas_call(
        body_with_dest, name=name, out_shape=out_shape, grid=(M // tm, N // tn, nk),
        in_specs=specs, out_specs=out_spec, input_output_aliases=aliases,
        scratch_shapes=[] if nk == 1 else [pltpu.VMEM((tm, tn), F32)],
        compiler_params=_params(("parallel", "parallel", "arbitrary"), VMEM_BIG),
    )(*args)


def _matmul_rows(name, a, b, mode, tm, tk, fn, rows, consts, outs, accs, *, n_out=None, b_noff=0, b_koff=0):
    rl = [(t, t.shape[1], 0) if not isinstance(t, tuple) else t for t in rows]
    make_a = a if callable(a) else None
    M, K = (rl[0][0].shape[0], b.shape[1 if mode == "nt" else 0]) if make_a else a.shape
    N = n_out if n_out is not None else (b.shape[0] if mode == "nt" else b.shape[1])
    tm, tk = min(tm, M), min(tk, K)
    assert M % tm == 0 and K % tk == 0 and b_koff % tk == 0 and b_noff % N == 0, (name, M, N, K)
    no, ko, nk = b_noff // N, b_koff // tk, K // tk
    assert make_a is None or nk == 1
    nr, nc, no_, na = len(rl), len(consts), len(outs), len(accs)
    dot = _nt if mode == "nt" else _nn

    def body(*refs):
        a_ref, b_ref, rest = (None, refs[0], refs[1:]) if make_a else (refs[0], refs[1], refs[2:])
        r_refs, c_refs = rest[:nr], rest[nr:nr + nc]
        o_refs, acc_refs = rest[nr + nc:nr + nc + no_], rest[nr + nc + no_:nr + nc + no_ + na]
        i, k = pl.program_id(0), pl.program_id(1)

        def finish(prod, *made):
            res_o, res_a = fn(prod, *made, *[r[...] for r in r_refs], *[c[...] for c in c_refs])
            for r, v in zip(o_refs, res_o, strict=True):
                r[...] = v.astype(r.dtype)
            if acc_refs:
                @pl.when(i == 0)
                def _():
                    for r in acc_refs:
                        r[...] = jnp.zeros_like(r)

                for r, v in zip(acc_refs, res_a, strict=True):
                    r[...] += v

        if make_a:
            left = make_a(*[r[...] for r in r_refs], *[c[...] for c in c_refs])
            finish(dot(left, b_ref[...]), left)
        elif nk == 1:
            finish(dot(a_ref[...], b_ref[...]))
        else:
            prod_ref = rest[-1]

            @pl.when(k == 0)
            def _():
                prod_ref[...] = jnp.zeros_like(prod_ref)

            prod_ref[...] += dot(a_ref[...], b_ref[...])

            @pl.when(k == nk - 1)
            def _():
                finish(prod_ref[...])

    if mode == "nt":
        b_spec = pl.BlockSpec((N, tk), lambda i, k: (no, k + ko))
    else:
        b_spec = pl.BlockSpec((tk, N), lambda i, k: (k + ko, no))
    in_specs = ([] if make_a else [pl.BlockSpec((tm, tk), lambda i, k: (i, k))]) + [b_spec]
    in_specs += [pl.BlockSpec((tm, w), functools.partial(lambda i, k, cb: (i, cb), cb=cb)) for (_, w, cb) in rl]
    in_specs += [pl.BlockSpec(c.shape, lambda i, k: (0, 0)) for c in consts]
    out_specs = [pl.BlockSpec((tm, c), lambda i, k: (i, 0)) for (c, _) in outs]
    out_specs += [pl.BlockSpec(shp, lambda i, k: (0, 0)) for shp in accs]
    out_shape = [jax.ShapeDtypeStruct((M, c), dt) for (c, dt) in outs] + [jax.ShapeDtypeStruct(shp, F32) for shp in accs]
    res = pl.pallas_call(
        body, name=name, out_shape=out_shape, grid=(M // tm, nk), in_specs=in_specs, out_specs=out_specs,
        scratch_shapes=[] if nk == 1 else [pltpu.VMEM((tm, N), F32)],
        compiler_params=_params(("arbitrary" if accs else "parallel", "arbitrary"), VMEM_BIG),
    )(*([] if make_a else [a]), b, *[t[0] for t in rl], *consts)
    return res[:no_], res[no_:]


def _rowwise(name, fn, tiled, consts, outs, accs, ts):
    tl = [(t, t.shape[1], 0) if not isinstance(t, tuple) else t for t in tiled]
    s_len = tl[0][0].shape[0]
    assert s_len % ts == 0
    nt_, nc_, no_ = len(tl), len(consts), len(outs)

    def body(*refs):
        t_refs, c_refs = refs[:nt_], refs[nt_:nt_ + nc_]
        o_refs, a_refs = refs[nt_ + nc_:nt_ + nc_ + no_], refs[nt_ + nc_ + no_:]
        res_o, res_a = fn(*[r[...] for r in t_refs], *[r[...] for r in c_refs])
        for r, v in zip(o_refs, res_o, strict=True):
            r[...] = v.astype(r.dtype)
        if a_refs:
            @pl.when(pl.program_id(0) == 0)
            def _():
                for r in a_refs:
                    r[...] = jnp.zeros_like(r)

            for r, v in zip(a_refs, res_a, strict=True):
                r[...] += v

    in_specs = [pl.BlockSpec((ts, w), functools.partial(lambda i, cb: (i, cb), cb=cb)) for (_, w, cb) in tl]
    in_specs += [pl.BlockSpec(c.shape, lambda i: (0, 0)) for c in consts]
    out_specs = [pl.BlockSpec((ts, c), lambda i: (i, 0)) for (c, _) in outs]
    out_specs += [pl.BlockSpec(shp, lambda i: (0, 0)) for shp in accs]
    out_shape = [jax.ShapeDtypeStruct((s_len, c), dt) for (c, dt) in outs]
    out_shape += [jax.ShapeDtypeStruct(shp, F32) for shp in accs]
    res = pl.pallas_call(
        body, name=name, out_shape=out_shape, grid=(s_len // ts,), in_specs=in_specs, out_specs=out_specs,
        compiler_params=_params(("arbitrary",) if accs else ("parallel",), VMEM_BIG),
    )(*[t[0] for t in tl], *consts)
    return res[:no_], res[no_:]


def _norm_mod_fn(x, nw, sc, sh):
    r = lax.rsqrt(jnp.mean(x * x, axis=-1, keepdims=True) + NORM_EPS)
    return (x * r * nw) * (1.0 + sc) + sh


def _norm_mod_fwd(name, x, nw, sc, sh):
    (hn,), _ = _rowwise(name, lambda x, nw, sc, sh: ([_norm_mod_fn(x, nw, sc, sh)], []),
                        [x], [nw, sc, sh], [(x.shape[1], BF16)], [], 512)
    return hn


def _norm_mod_bwd(name, last, x, dhn_parts, dres, nw, sc, sh, prev=None):
    n = len(dhn_parts)
    d = x.shape[1]
    a, b, mode, tk, kw = last

    def fn(dhn, x, *rest):
        for p in rest[:n]:
            dhn = dhn + p
        dres, rest = rest[n], rest[n + 1:]
        y_prev, (nw, sc, sh), gate = (rest[0], rest[1:4], rest[4]) if prev is not None else (None, rest[0:3], None)
        _, vjp = jax.vjp(_norm_mod_fn, x, nw, sc, sh)
        dx, dnw, dsc, dsh = vjp(dhn)
        dx = dx + dres
        if prev is None:
            return [dx], [dnw, dsc, dsh]
        return [dx, gate * dx], [dnw, dsc, dsh, jnp.sum(dx * y_prev, axis=0, keepdims=True)]

    rows = [x, *dhn_parts, dres] + ([prev[0]] if prev is not None else [])
    consts = [nw, sc, sh] + ([prev[1]] if prev is not None else [])
    outs = [(d, F32)] + ([(d, BF16)] if prev is not None else [])
    res_o, res_a = _matmul_rows(name, a, b, mode, 512, tk, fn, rows, consts, outs, [(1, d)] * (3 + (prev is not None)), **kw)
    return (*res_o, *res_a)


def _rope_tables(pos_col, inv_row):
    def fn(pos, inv):
        ang = pos.astype(F32) * inv
        e = lax.broadcasted_iota(jnp.int32, (1, 128), 1) % HEAD_DIM
        cos, sin = jnp.cos(ang), jnp.sin(ang)
        half = ROT_DIM // 2
        return [jnp.where(e < ROT_DIM, cos, 1.0), jnp.where(e < half, -sin, 0.0),
                jnp.where((e >= half) & (e < ROT_DIM), sin, 0.0)], []

    (c, sa, sb), _ = _rowwise("rope_tables", fn, [pos_col], [inv_row], [(128, F32)] * 3, [], 512)
    return c, sa, sb


def _rot_fwd(t, c, sa, sb):
    n = t.shape[1]
    rep = n // 128
    c, sa, sb = (jnp.tile(u, (1, rep)) for u in (c, sa, sb))
    return t * c + pltpu.roll(t, n - ROT_DIM // 2, 1) * sa + pltpu.roll(t, ROT_DIM // 2, 1) * sb


def _rot_bwd(g, c, sa, sb):
    n = g.shape[1]
    rep = n // 128
    c, sa, sb = (jnp.tile(u, (1, rep)) for u in (c, sa, sb))
    return g * c + pltpu.roll(g * sa, ROT_DIM // 2, 1) + pltpu.roll(g * sb, n - ROT_DIM // 2, 1)


ATT_TQ = 128
ATT_TK = ATT_TQ + 2 * BAND


def _attn_specs(g, s_len):
    def blk(off):
        return pl.BlockSpec((s_len, 128), functools.partial(lambda hp, off: (0, off + hp), off=off))

    return blk(4 * g), blk(12 + 4 * g), blk(4 * g), blk(0)


def _attn_tile_geometry(t, d, l):
    nts = l // ATT_TQ
    r = t // nts
    ts = t % nts
    q0 = ts * ATT_TQ
    ws = jnp.clip(q0 - BAND, 0, l - ATT_TK)
    kind = jnp.where(ts == 0, 0, jnp.where(ts == nts - 1, 2, 1))
    if d == 1:
        return pl.ds(pl.multiple_of(q0, ATT_TQ), ATT_TQ), pl.ds(pl.multiple_of(ws, BAND), ATT_TK), kind
    return pl.ds(r + d * q0, ATT_TQ, stride=d), pl.ds(r + d * ws, ATT_TK, stride=d), kind


def _attn_fill_bias(bias_ref):
    iq = lax.broadcasted_iota(jnp.int32, (2 * ATT_TQ, 1), 0) % ATT_TQ
    ik = lax.broadcasted_iota(jnp.int32, (1, ATT_TK), 1)
    for i, off in enumerate((0, -BAND, -2 * BAND)):
        bias_ref[i] = jnp.where(jnp.abs(ik + off - iq) <= BAND, 0.0, NEG_BIG)


def _split_heads(t, in_h):
    zero = jnp.zeros_like(t)
    return jnp.concatenate([jnp.where(in_h[0], t, zero), jnp.where(in_h[1], t, zero)], axis=0)


def _attn_fwd(g, qk, v):
    s_len = qk.shape[0]
    d = DILATIONS[g]
    l = s_len // d
    assert l % ATT_TQ == 0 and l >= ATT_TK
    q_spec, k_spec, v_spec, o_spec = _attn_specs(g, s_len)
    scale = 1.0 / math.sqrt(HEAD_DIM)

    def body(q_ref, k_ref, v_ref, o_ref, lse_ref, bias_ref):
        lane = lax.broadcasted_iota(jnp.int32, (1, 128), 1)
        in_h = [lane < HEAD_DIM, lane >= HEAD_DIM]
        _attn_fill_bias(bias_ref)

        def tile(t, carry):
            rows, win, kind = _attn_tile_geometry(t, d, l)
            q = (q_ref[rows, :] * scale).astype(BF16)
            k = k_ref[win, :].astype(BF16)
            vv = v_ref[win, :].astype(BF16)
            s = _nt(_split_heads(q, in_h), k) + bias_ref[kind]
            m = jnp.max(s, axis=1, keepdims=True)
            p = jnp.exp(s - m)
            den = jnp.sum(p, axis=1, keepdims=True)
            out = _nn(p, vv) / den
            lse = m + jnp.log(den)
            o_ref[rows, :] = jnp.where(in_h[0], out[:ATT_TQ], out[ATT_TQ:])
            lse_ref[rows, :] = jnp.where(in_h[0], lse[:ATT_TQ], lse[ATT_TQ:])
            return carry

        lax.fori_loop(0, s_len // ATT_TQ, tile, 0, unroll=4)

    return pl.pallas_call(
        body, name=f"attn_fwd_g{g}", grid=(4,),
        out_shape=[jax.ShapeDtypeStruct((s_len, 512), F32)] * 2,
        in_specs=[q_spec, k_spec, v_spec], out_specs=[o_spec, o_spec],
        scratch_shapes=[pltpu.VMEM((3, 2 * ATT_TQ, ATT_TK), F32)],
        compiler_params=_params(("parallel",), VMEM_BIG),
    )(qk, qk, v)


def _attn_bwd(g, qk, v, o, lse, do, dlse):
    s_len = qk.shape[0]
    d = DILATIONS[g]
    l = s_len // d
    q_spec, k_spec, v_spec, o_spec = _attn_specs(g, s_len)
    scale = 1.0 / math.sqrt(HEAD_DIM)

    def body(q_ref, k_ref, v_ref, o_ref, lse_ref, do_ref, dlse_ref, dq_ref, dk_ref, dv_ref, bias_ref):
        lane = lax.broadcasted_iota(jnp.int32, (1, 128), 1)
        in_h = [lane < HEAD_DIM, lane >= HEAD_DIM]
        dk_ref[...] = jnp.zeros_like(dk_ref)
        dv_ref[...] = jnp.zeros_like(dv_ref)
        _attn_fill_bias(bias_ref)

        def tile(t, carry):
            rows, win, kind = _attn_tile_geometry(t, d, l)
            k, vv = k_ref[win, :].astype(BF16), v_ref[win, :].astype(BF16)
            dout, lse_t, dlse_t = do_ref[rows, :], lse_ref[rows, :], dlse_ref[rows, :]
            od = dout * o_ref[rows, :]
            q2 = _split_heads((q_ref[rows, :] * scale).astype(BF16), in_h)
            do2 = _split_heads(dout.astype(BF16), in_h)
            head_col = lambda a: jnp.concatenate([a[:, 0:1], a[:, HEAD_DIM:HEAD_DIM + 1]], axis=0)
            delta = jnp.concatenate([jnp.sum(jnp.where(m, od, 0.0), axis=1, keepdims=True) for m in in_h], axis=0)
            p = jnp.exp(_nt(q2, k) + bias_ref[kind] - head_col(lse_t))
            ds = (p * (_nt(do2, vv) - delta + head_col(dlse_t))).astype(BF16)
            dq2 = _nn(ds, k) * scale
            dq_ref[rows, :] = jnp.where(in_h[0], dq2[:ATT_TQ], dq2[ATT_TQ:])
            dk_ref[win, :] += _tn(ds, q2)
            dv_ref[win, :] += _tn(p, do2)
            return carry

        lax.fori_loop(0, s_len // ATT_TQ, tile, 0, unroll=4)

    return pl.pallas_call(
        body, name=f"attn_bwd_g{g}", grid=(4,),
        out_shape=[jax.ShapeDtypeStruct((s_len, 512), F32)] * 3,
        in_specs=[q_spec, k_spec, v_spec, o_spec, o_spec, o_spec, o_spec], out_specs=[o_spec] * 3,
        scratch_shapes=[pltpu.VMEM((3, 2 * ATT_TQ, ATT_TK), F32)],
        compiler_params=_params(("parallel",), VMEM_BIG),
    )(qk, qk, v, o, lse, do, dlse)


def _mix_weights(ls):
    mx = jnp.maximum(jnp.maximum(ls[0], ls[1]), ls[2])
    es = [jnp.exp(x - mx) for x in ls]
    tot = es[0] + es[1] + es[2]
    return [e / tot for e in es]


def _attn_out(os_, lses, z, x, gate, w_out):
    s_len, dm = x.shape
    tm = 256
    wdt = 512
    z, z_block = z

    def body(o0, o1, o2, l0, l1, l2, z_ref, x_ref, g_ref, w_ref, a_ref, y_ref, x1_ref):
        alphas = _mix_weights([l0[...], l1[...], l2[...]])
        y = jnp.zeros((tm, dm), F32)
        for g, o_ref in enumerate((o0, o1, o2)):
            a_g = (o_ref[...] * alphas[g] * _silu(z_ref[:, g * wdt:(g + 1) * wdt])).astype(BF16)
            a_ref[:, g * wdt:(g + 1) * wdt] = a_g
            y = y + _nn(a_g, w_ref[g * wdt:(g + 1) * wdt, :])
        y_ref[...] = y
        x1_ref[...] = x_ref[...] + g_ref[...] * y

    row = lambda c: pl.BlockSpec((tm, c), lambda i: (i, 0))
    return pl.pallas_call(
        body, name="attn_out", grid=(s_len // tm,),
        out_shape=[jax.ShapeDtypeStruct((s_len, 3 * wdt), BF16), jax.ShapeDtypeStruct((s_len, dm), F32),
                   jax.ShapeDtypeStruct((s_len, dm), F32)],
        in_specs=[row(wdt)] * 6 + [pl.BlockSpec((tm, 3 * wdt), lambda i: (i, z_block)), row(dm),
                                   pl.BlockSpec((1, dm), lambda i: (0, 0)), pl.BlockSpec(w_out.shape, lambda i: (0, 0))],
        out_specs=[row(3 * wdt), row(dm), row(dm)],
        compiler_params=_params(("parallel",), VMEM_BIG),
    )(*os_, *lses, z, x, gate, w_out)


def _mix_bwd(dy, w_out, os_, lses, z):
    wdt = 512

    def fn(da, o0, o1, o2, l0, l1, l2, z):
        os_t, ls = [o0, o1, o2], [l0, l1, l2]
        alphas = _mix_weights(ls)
        hi = lax.broadcasted_iota(jnp.int32, (wdt, wdt), 0) // HEAD_DIM
        hj = lax.broadcasted_iota(jnp.int32, (wdt, wdt), 1) // HEAD_DIM
        seg = (hi == hj).astype(F32)
        dos, dal, dzs = [], [], []
        for g in range(3):
            zg = z[:, g * wdt:(g + 1) * wdt]
            sig = jax.nn.sigmoid(zg)
            dag = da[:, g * wdt:(g + 1) * wdt]
            dmix = dag * zg * sig
            dzs.append(dag * os_t[g] * alphas[g] * (sig * (1.0 + zg * (1.0 - sig))))
            dos.append(dmix * alphas[g])
            dal.append(_hnn(dmix * os_t[g], seg))
        mean = alphas[0] * dal[0] + alphas[1] * dal[1] + alphas[2] * dal[2]
        dls = [alphas[g] * (dal[g] - mean) for g in range(3)]
        return dos + dls + [jnp.concatenate(dzs, axis=1)], []

    outs, _ = _matmul_rows("attn_out_dx_mix_bwd", dy, w_out, "nt", 256, dy.shape[1], fn, [*os_, *lses, (z[0], 3 * wdt, z[1])], [],
                           [(wdt, F32)] * 6 + [(3 * wdt, BF16)], [])
    return outs[:3], outs[3:6], outs[6]


def _rot_pack_bwd(dqs, dks, dvs, tabs):
    wdt = 512

    def fn(*args):
        grads, (c, sa, sb) = args[:9], args[9:]
        cols = [_rot_bwd(gq, c, sa, sb) for gq in grads[:6]] + list(grads[6:])
        return [jnp.concatenate(cols, axis=1)], []

    (out,), _ = _rowwise("rot_pack_bwd", fn, [*dqs, *dks, *dvs, *tabs], [], [(9 * wdt, BF16)], [], 256)
    return out


CONV_CB = 128
CONV_R = 256
CONV_PAD = 8


def _conv_taps(buf, base, off, sign):
    return [buf[pl.ds(base + off + sign * j, CONV_R), :] for j in range(CONV_WIDTH)]


def _conv_tap_sum(taps, w):
    acc = None
    for j, t in enumerate(taps):
        term = t * w[j:j + 1, :]
        acc = term if acc is None else acc + term
    return acc


def _conv_fwd(xpre, cw, cb):
    s_len, ch = xpre.shape
    nchunk = s_len // CONV_R

    def body(x_ref, w_ref, b_ref, o_ref, xp):
        zero = jnp.zeros((CONV_PAD, CONV_CB), F32)
        xp[0:CONV_PAD, :] = zero
        xp[s_len + CONV_PAD:s_len + 2 * CONV_PAD, :] = zero

        def fill(ci, carry):
            base = pl.multiple_of(ci * CONV_R, CONV_R)
            xp[pl.ds(base + CONV_PAD, CONV_R), :] = x_ref[pl.ds(base, CONV_R), :]
            return carry

        lax.fori_loop(0, nchunk, fill, 0)
        w = w_ref[...]
        b = b_ref[...]

        def chunk(ci, carry):
            base = pl.multiple_of(ci * CONV_R, CONV_R)
            u = _conv_tap_sum(_conv_taps(xp, base, CONV_PAD - CONV_WIDTH // 2, 1), w) + b
            o_ref[pl.ds(base, CONV_R), :] = _silu(u)
            return carry

        lax.fori_loop(0, nchunk, chunk, 0, unroll=2)

    col = lambda r: pl.BlockSpec((r, CONV_CB), lambda j: (0, j))
    return pl.pallas_call(
        body, name="conv_fwd", grid=(ch // CONV_CB,), out_shape=jax.ShapeDtypeStruct((s_len, ch), F32),
        in_specs=[col(s_len), col(CONV_WIDTH), col(1)], out_specs=col(s_len),
        scratch_shapes=[pltpu.VMEM((s_len + 2 * CONV_PAD, CONV_CB), F32)],
        compiler_params=_params(("parallel",), VMEM_BIG),
    )(xpre, cw, cb)


def _conv_bwd(xpre, da, db, cw, cb):
    s_len, ch = xpre.shape
    nchunk = s_len // CONV_R
    half = CONV_WIDTH // 2

    def body(x_ref, da_ref, db_ref, w_ref, b_ref, dx_ref, gw_ref, gb_ref, xp, dcp):
        zero = jnp.zeros((CONV_PAD, CONV_CB), F32)
        for buf in (xp, dcp):
            buf[0:CONV_PAD, :] = zero
            buf[s_len + CONV_PAD:s_len + 2 * CONV_PAD, :] = zero

        def fill(ci, carry):
            base = pl.multiple_of(ci * CONV_R, CONV_R)
            xp[pl.ds(base + CONV_PAD, CONV_R), :] = x_ref[pl.ds(base, CONV_R), :]
            return carry

        lax.fori_loop(0, nchunk, fill, 0)
        w = w_ref[...]
        b = b_ref[...]

        def first(ci, carry):
            base = pl.multiple_of(ci * CONV_R, CONV_R)
            taps = _conv_taps(xp, base, CONV_PAD - half, 1)
            u = _conv_tap_sum(taps, w) + b
            sig = jax.nn.sigmoid(u)
            dc = (da_ref[pl.ds(base, CONV_R), :] + db_ref[pl.ds(base, CONV_R), :]) * (sig * (1.0 + u * (1.0 - sig)))
            dcp[pl.ds(base + CONV_PAD, CONV_R), :] = dc
            gb = carry[0] + jnp.sum(dc, axis=0, keepdims=True)
            gws = [carry[1 + j] + jnp.sum(dc * taps[j], axis=0, keepdims=True) for j in range(CONV_WIDTH)]
            return (gb, *gws)

        z1 = jnp.zeros((1, CONV_CB), F32)
        sums = lax.fori_loop(0, nchunk, first, (z1,) * (1 + CONV_WIDTH), unroll=2)
        gb_ref[...] = sums[0]
        for j in range(CONV_WIDTH):
            gw_ref[j:j + 1, :] = sums[1 + j]

        def second(ci, carry):
            base = pl.multiple_of(ci * CONV_R, CONV_R)
            dx_ref[pl.ds(base, CONV_R), :] = _conv_tap_sum(_conv_taps(dcp, base, CONV_PAD + half, -1), w).astype(dx_ref.dtype)
            return carry

        lax.fori_loop(0, nchunk, second, 0, unroll=2)

    col = lambda r: pl.BlockSpec((r, CONV_CB), lambda j: (0, j))
    return pl.pallas_call(
        body, name="conv_bwd", grid=(ch // CONV_CB,),
        out_shape=[jax.ShapeDtypeStruct((s_len, ch), BF16), jax.ShapeDtypeStruct((CONV_WIDTH, ch), F32),
                   jax.ShapeDtypeStruct((1, ch), F32)],
        in_specs=[col(s_len), col(s_len), col(s_len), col(CONV_WIDTH), col(1)],
        out_specs=[col(s_len), col(CONV_WIDTH), col(1)],
        scratch_shapes=[pltpu.VMEM((s_len + 2 * CONV_PAD, CONV_CB), F32)] * 2,
        compiler_params=_params(("parallel",), VMEM_BIG),
    )(xpre, da, db, cw, cb)


SSD_GW = 256
SSD_N = 128
SSD_DTW = 128


def _bf16_parts(x, n):
    parts, rest = [], x
    for _ in range(n):
        p = rest.astype(BF16)
        parts.append(p)
        rest = rest - p.astype(F32)
    return parts


@jax.custom_vjp
def _expand(x, e):
    eb = e.astype(BF16)
    return _dg(jnp.concatenate(_bf16_parts(x, 2), axis=1), jnp.concatenate([eb, eb], axis=0), 1, 0)


def _expand_fwd(x, e):
    return _expand(x, e), e


def _expand_bwd(e, g):
    eb = e.astype(BF16)
    return sum(_dg(p, eb, 1, 1) for p in _bf16_parts(g, 2)), jnp.zeros_like(e)


_expand.defvjp(_expand_fwd, _expand_bwd)


@jax.custom_vjp
def _running_sum(tri, x):
    tb = tri.astype(BF16)
    return sum(_dg(tb, p, 1, 0) for p in _bf16_parts(x, 3))


def _running_sum_fwd(tri, x):
    return _running_sum(tri, x), tri


def _running_sum_bwd(tri, g):
    tb = tri.astype(BF16)
    return jnp.zeros_like(tri), sum(_dg(tb, p, 0, 0) for p in _bf16_parts(g, 3))


_running_sum.defvjp(_running_sum_fwd, _running_sum_bwd)


def _ssd_mask(dirn):
    ri = lax.broadcasted_iota(jnp.int32, (CHUNK, CHUNK), 0)
    cj = lax.broadcasted_iota(jnp.int32, (CHUNK, CHUNK), 1)
    return (cj <= ri) if dirn == 0 else (cj >= ri)


def _ssd_rowsel(dirn):
    last = CHUNK - 1 if dirn == 0 else 0
    return (lax.broadcasted_iota(jnp.int32, (CHUNK, 1), 0) == last).astype(F32)


def _ssd_chunk_pre(dirn):
    nh = SSD_DTW

    def f(dt, alog):
        da = dt * (-jnp.exp(alog))
        cum = _running_sum(_ssd_mask(dirn).astype(F32), da)
        tot = jnp.sum(cum * _ssd_rowsel(dirn), axis=0, keepdims=True)
        hh = lax.broadcasted_iota(jnp.int32, (nh, SSD_HEADS * HEAD_DIM), 0)
        jj = lax.broadcasted_iota(jnp.int32, (nh, SSD_HEADS * HEAD_DIM), 1)
        expand = (hh == dirn * SSD_HEADS + jj // HEAD_DIM).astype(F32)
        return cum, cum.T, _expand(dt, expand), _expand(jnp.exp(tot - cum), expand), _expand(jnp.exp(cum), expand)

    return f


def _ssd_group_fn(g, dirn, stacked):
    nh = SSD_DTW

    def f(xs, bm, cm, st, cum, cum_t, dt_e, w_e, ce_e):
        mask = _ssd_mask(dirn)
        xdt = xs * dt_e
        cd_e = jnp.sum(ce_e * _ssd_rowsel(dirn), axis=0, keepdims=True)
        cb = _bnt(cm, bm)
        lane_head = lax.broadcasted_iota(jnp.int32, (1, SSD_GW), 1) // HEAD_DIM
        y = _bnn(cm, st) * ce_e
        decayed, inputs = [], []
        for j in range(4):
            hidx = dirn * SSD_HEADS + 4 * g + j
            col = jnp.sum(cum * (lax.broadcasted_iota(jnp.int32, (1, nh), 1) == hidx).astype(F32), axis=1, keepdims=True)
            row = jnp.sum(cum_t * (lax.broadcasted_iota(jnp.int32, (nh, 1), 0) == hidx).astype(F32), axis=0, keepdims=True)
            dec = cb * jnp.exp(jnp.where(mask, col - row, NEG_BIG))
            head = (lane_head == j).astype(F32)
            if stacked:
                decayed.append(dec)
                inputs.append(xdt * head)
            else:
                y = y + _bnn(dec, xdt) * head
        if stacked:
            y = y + _bnn(jnp.concatenate(decayed, axis=1), jnp.concatenate(inputs, axis=0))
        st_out = st * cd_e + _btn(bm, xdt * w_e)
        return y, st_out

    return f


def _ssd_in_specs(kk):
    ln = CHUNK
    return [pl.BlockSpec((ln, 2048), lambda i: (kk(i), 0)),
            pl.BlockSpec((ln, 1024), lambda i: (kk(i), 2)),
            pl.BlockSpec((ln, 1024), lambda i: (kk(i), 3)),
            pl.BlockSpec((ln, SSD_DTW), lambda i: (kk(i), 0)),
            pl.BlockSpec((1, SSD_DTW), lambda i: (0, 0))]


def _ssd_fwd(xbc, dt, alog, dirn):
    s_len = xbc.shape[0]
    nc = s_len // CHUNK
    kk = (lambda i: i) if dirn == 0 else (lambda i: nc - 1 - i)

    def body(x_ref, b_ref, c_ref, dt_ref, al_ref, y_ref, sts_ref, st):
        @pl.when(pl.program_id(0) == 0)
        def _():
            st[...] = jnp.zeros_like(st)

        sts_ref[0] = st[...]
        cum, cum_t, dt_e, w_e, ce_e = _ssd_chunk_pre(dirn)(dt_ref[...], al_ref[...])
        for g in range(SSD_GROUPS):
            xc = slice(g * SSD_GW, (g + 1) * SSD_GW)
            gc = slice(g * SSD_N, (g + 1) * SSD_N)
            y, st_new = _ssd_group_fn(g, dirn, True)(x_ref[:, xc], b_ref[:, gc], c_ref[:, gc], st[:, xc], cum, cum_t,
                                               dt_e[:, xc], w_e[:, xc], ce_e[:, xc])
            y_ref[:, xc] = y
            st[:, xc] = st_new

    return pl.pallas_call(
        body, name=f"ssd_fwd_d{dirn}", grid=(nc,),
        out_shape=[jax.ShapeDtypeStruct((s_len, 2048), F32), jax.ShapeDtypeStruct((nc, SSD_N, 2048), F32)],
        in_specs=_ssd_in_specs(kk),
        out_specs=[pl.BlockSpec((CHUNK, 2048), lambda i: (kk(i), 0)),
                   pl.BlockSpec((1, SSD_N, 2048), lambda i: (kk(i), 0, 0))],
        scratch_shapes=[pltpu.VMEM((SSD_N, 2048), F32)],
        compiler_params=_params(("arbitrary",), VMEM_BIG),
    )(xbc, xbc, xbc, dt, alog)


def _ssd_bwd(xbc, dt, alog, states, dy, d_e, dirn):
    s_len = xbc.shape[0]
    nc = s_len // CHUNK
    kk = (lambda i: nc - 1 - i) if dirn == 0 else (lambda i: i)

    def body(x_ref, b_ref, c_ref, dt_ref, al_ref, sts_ref, dy_ref, de_ref, dx_ref, ddt_ref, dal_ref, dst):
        @pl.when(pl.program_id(0) == 0)
        def _():
            dst[...] = jnp.zeros_like(dst)
            dal_ref[...] = jnp.zeros_like(dal_ref)

        (cum, cum_t, dt_e, w_e, ce_e), pre_vjp = jax.vjp(_ssd_chunk_pre(dirn), dt_ref[...], al_ref[...])
        dcum = jnp.zeros_like(cum)
        dcum_t = jnp.zeros_like(cum_t)
        d_dt_e, d_w_e, d_ce_e = [], [], []
        for g in range(SSD_GROUPS):
            xc = slice(g * SSD_GW, (g + 1) * SSD_GW)
            gc = slice(g * SSD_N, (g + 1) * SSD_N)
            _, vjp = jax.vjp(_ssd_group_fn(g, dirn, False), x_ref[:, xc], b_ref[:, gc], c_ref[:, gc], sts_ref[0, :, xc], cum, cum_t,
                             dt_e[:, xc], w_e[:, xc], ce_e[:, xc])
            dyg = dy_ref[:, xc]
            dxs, dbm, dcm, dst_g, dcum_g, dcum_t_g, ddte_g, dwe_g, dcee_g = vjp((dyg, dst[:, xc]))
            if dirn == 0:
                dxs = dxs + dyg * de_ref[:, xc]
            dx_ref[:, xc] = dxs
            dx_ref[:, 2048 + g * SSD_N:2048 + (g + 1) * SSD_N] = dbm
            dx_ref[:, 3072 + g * SSD_N:3072 + (g + 1) * SSD_N] = dcm
            dst[:, xc] = dst_g
            dcum = dcum + dcum_g
            dcum_t = dcum_t + dcum_t_g
            d_dt_e.append(ddte_g)
            d_w_e.append(dwe_g)
            d_ce_e.append(dcee_g)
        ddt, dal = pre_vjp((dcum, dcum_t, jnp.concatenate(d_dt_e, axis=1), jnp.concatenate(d_w_e, axis=1),
                            jnp.concatenate(d_ce_e, axis=1)))
        ddt_ref[...] = ddt
        dal_ref[...] += dal

    return pl.pallas_call(
        body, name=f"ssd_bwd_d{dirn}", grid=(nc,),
        out_shape=[jax.ShapeDtypeStruct((s_len, 4096), F32), jax.ShapeDtypeStruct((s_len, SSD_DTW), F32),
                   jax.ShapeDtypeStruct((1, SSD_DTW), F32)],
        in_specs=_ssd_in_specs(kk) + [pl.BlockSpec((1, SSD_N, 2048), lambda i: (kk(i), 0, 0)),
                                      pl.BlockSpec((CHUNK, 2048), lambda i: (kk(i), 0)),
                                      pl.BlockSpec((1, 2048), lambda i: (0, 0))],
        out_specs=[pl.BlockSpec((CHUNK, 4096), lambda i: (kk(i), 0)),
                   pl.BlockSpec((CHUNK, SSD_DTW), lambda i: (kk(i), 0)),
                   pl.BlockSpec((1, SSD_DTW), lambda i: (0, 0))],
        scratch_shapes=[pltpu.VMEM((SSD_N, 2048), F32)],
        compiler_params=_params(("arbitrary",), VMEM_BIG),
    )(xbc, xbc, xbc, dt, alog, states, dy, d_e)


def _gate_norm_fn(yf, yb, xs, z, d_e, nw):
    yg = (yf + yb + xs * d_e) * _silu(z)
    return yg * lax.rsqrt(jnp.mean(yg * yg, axis=-1, keepdims=True) + NORM_EPS) * nw


def _gate_norm_bwd(dy, w_out, yf, yb, xbc, z, d_e, nw):
    def fn(du, yf, yb, xs, z, d_e, nw):
        sig = jax.nn.sigmoid(z)
        gate = z * sig
        ysum = yf + yb + xs * d_e
        yg = ysum * gate
        r = lax.rsqrt(jnp.mean(yg * yg, axis=-1, keepdims=True) + NORM_EPS)
        t = du * nw
        dyg = t * r - yg * (jnp.mean(t * yg, axis=-1, keepdims=True) * (r * r * r))
        dys = dyg * gate
        dz = dyg * ysum * (sig * (1.0 + z * (1.0 - sig)))
        dnw = jnp.sum(du * yg * r, axis=0, keepdims=True)
        dde = jnp.sum(dys * xs, axis=0, keepdims=True)
        hh = lax.broadcasted_iota(jnp.int32, (2048, SSD_HEADS), 0) // HEAD_DIM
        jj = lax.broadcasted_iota(jnp.int32, (2048, SSD_HEADS), 1)
        return [dys, dz], [dnw, _hnn(jnp.broadcast_to(dde, (8, 2048)), (hh == jj).astype(F32))[0:1]]

    (dys, dz), (g_nw, g_d) = _matmul_rows("ssd_out_dx_gate_norm_bwd", dy, w_out, "nt", 256, dy.shape[1], fn,
                                          [yf, yb, (xbc, 2048, 0), z], [d_e, nw], [(2048, F32), (2048, BF16)],
                                          [(1, 2048), (1, SSD_HEADS)])
    return dys, dz, g_nw, g_d


def _ssd_tail_loss(yf, yb, xbc, z, d_e, snw, w_out, x1, tgt, gate, fnw):
    dm = x1.shape[1]
    si = yf.shape[1]

    def make_u(yf, yb, xs, z, x1, tgt, d_e, snw, gate, fnw):
        return _gate_norm_fn(yf, yb, xs, z, d_e, snw).astype(BF16)

    def fn(y1, u, yf, yb, xs, z, x1, tgt, d_e, snw, gate, fnw):
        def head(x2, fnw):
            yf = (x2 * lax.rsqrt(jnp.mean(x2 * x2, axis=-1, keepdims=True) + NORM_EPS)) * fnw
            err = yf - tgt
            return 0.5 * jnp.sum(jnp.mean(err * err, axis=-1, keepdims=True), axis=0, keepdims=True)

        x2 = x1 + gate * y1
        loss, vjp = jax.vjp(head, x2, fnw)
        dx2, dfnw = vjp(jnp.ones((1, 1), F32))
        return [u, dx2, gate * dx2], [dfnw, jnp.sum(dx2 * y1, axis=0, keepdims=True), jnp.broadcast_to(loss, (1, 128))]

    (u, dx2, dy1), (g_fnw, dgate, loss) = _matmul_rows(
        "ssd_out_loss", make_u, w_out, "nn", 256, si, fn, [yf, yb, (xbc, si, 0), z, x1, tgt], [d_e, snw, gate, fnw],
        [(si, BF16), (dm, F32), (dm, BF16)], [(1, dm), (1, dm), (1, 128)])
    return u, dx2, dy1, g_fnw, dgate, loss


def _softplus_fwd(dt_raw, bias):
    (dt,), _ = _rowwise("dt_softplus", lambda r, b: ([jax.nn.softplus(r + b)], []), [dt_raw], [bias],
                        [(dt_raw.shape[1], F32)], [], 512)
    return dt


def _softplus_bwd(ddt_f, ddt_b, dt_raw, bias):
    def fn(df, db, r, b):
        g = (df + db) * jax.nn.sigmoid(r + b)
        return [g], [jnp.sum(g, axis=0, keepdims=True)]

    w = dt_raw.shape[1]
    (g,), (gb,) = _rowwise("dt_softplus_bwd", fn, [ddt_f, ddt_b, dt_raw], [bias], [(w, BF16)], [(1, w)], 512)
    return g, gb


def _whole(a):
    nd = len(a.shape)
    return pl.BlockSpec(a.shape, lambda *_: (0,) * nd)


def _mod_part(c_all, mod_w):
    nl, _, ncol = mod_w.shape
    nb = c_all.shape[0]

    def body(c_ref, w_ref, o_ref):
        cond = _silu(c_ref[...])
        for i in range(nl):
            o_ref[i * nb:(i + 1) * nb, :] = _nn(cond, w_ref[i])

    return pl.pallas_call(body, name="mod_part", out_shape=jax.ShapeDtypeStruct((nl * nb, ncol), F32),
                          compiler_params=_params(None, VMEM_BIG))(c_all, mod_w)


def _mod_finish(mod_nb, mod_b, norm_w, tokens):
    nl, dm = norm_w.shape

    def body(a_ref, b_ref, nw_ref, *rest):
        tok_refs, o_refs = rest[:len(tokens)], rest[len(tokens):]
        tok = sum(t[0:1, 0:1] for t in tok_refs)
        for i in range(nl):
            for k in range(3):
                cols = slice(k * dm, (k + 1) * dm)
                o_refs[4 * i + k][...] = a_ref[i:i + 1, cols] + b_ref[i:i + 1, cols]
            o_refs[4 * i + 3][...] = nw_ref[i:i + 1, :] + tok

    rows = pl.pallas_call(body, name="mod_finish", out_shape=[jax.ShapeDtypeStruct((1, dm), F32)] * (4 * nl))(
        mod_nb, mod_b, norm_w, *tokens)
    return [rows[4 * i:4 * i + 4] for i in range(nl)]


def _mod_grad(c_all, dmod_sh):
    nl, nb, ncol = dmod_sh.shape
    dm = c_all.shape[1]

    def body(c_ref, d_ref, o_ref):
        cond = _silu(c_ref[...])
        for i in range(nl):
            o_ref[i] = _tn(cond, d_ref[i])

    return pl.pallas_call(body, name="mod_grad", out_shape=jax.ShapeDtypeStruct((nl, dm, ncol), F32),
                          compiler_params=_params(None, VMEM_BIG))(c_all, dmod_sh)


PACK_ROWS = 16
PACK_COLS = 1024


def _pack_small(rows, b64, a64s, d32, extra):
    nr, na = len(rows), len(a64s)

    def body(*refs):
        o_ref = refs[-1]
        o_ref[...] = jnp.zeros_like(o_ref)
        for i in range(nr):
            o_ref[i:i + 1, :] = refs[i][...]
        b_ref, a_refs, d_ref, e_ref = refs[nr], refs[nr + 1:nr + 1 + na], refs[nr + 1 + na], refs[nr + 2 + na]
        o_ref[nr:nr + 1, 0:64] = b_ref[:, 0:64]
        o_ref[nr:nr + 1, 64:128] = sum(a[:, 0:64] for a in a_refs)
        o_ref[nr:nr + 1, 128:160] = d_ref[...]
        o_ref[nr:nr + 1, 256:384] = e_ref[...]

    return pl.pallas_call(body, name="pack_small", out_shape=jax.ShapeDtypeStruct((PACK_ROWS, PACK_COLS), F32))(
        *rows, b64, *a64s, d32, extra)


def _pack_ssd_small(cw, cb, nw):
    def body(cw_ref, cb_ref, nw_ref, o_ref):
        o_ref[...] = jnp.zeros_like(o_ref)
        o_ref[0:5, :] = cw_ref[...]
        o_ref[5:6, :] = cb_ref[...]
        o_ref[6:7, 0:256] = nw_ref[...]

    return pl.pallas_call(body, name="pack_ssd_small", out_shape=jax.ShapeDtypeStruct((8, 512), F32))(cw, cb, nw)


def _sum_parts(p_ref):
    g = p_ref[0].astype(F32)
    for s in range(1, p_ref.shape[0]):
        g = g + p_ref[s].astype(F32)
    return g


def _adam_update(w, g, m, v):
    m2 = ADAM_B1 * m + (1.0 - ADAM_B1) * g
    v2 = ADAM_B2 * v + (1.0 - ADAM_B2) * (g * g)
    m_hat = m2 / (1.0 - ADAM_B1 ** ADAM_STEP)
    v_hat = v2 / (1.0 - ADAM_B2 ** ADAM_STEP)
    return -ADAM_LR * (m_hat / (jnp.sqrt(v_hat) + ADAM_EPS) + ADAM_WD * w), m2, v2


def _adamw_windows(name, parts, params, windows, extra=None):
    n = len(params)

    def body(p_ref, *rest):
        ins, outs = rest[:3 * n], rest[3 * n:]
        g = _sum_parts(p_ref)
        for pi, rows, cols, idx in windows:
            w_ref, m_ref, v_ref = ins[3 * pi:3 * pi + 3]
            gw = g[rows, cols]
            dw, m2, v2 = _adam_update(w_ref[idx], gw, m_ref[idx], v_ref[idx])
            for o_ref, val in zip(outs[4 * pi:4 * pi + 4], (gw, dw, m2, v2), strict=True):
                o_ref[idx] = val
        if extra is not None:
            outs[4 * n][...] = g[extra[0], extra[1]]

    out_shape = [jax.ShapeDtypeStruct(w.shape, F32) for (w, _, _) in params for _ in range(4)]
    if extra is not None:
        out_shape.append(jax.ShapeDtypeStruct((extra[0].stop - extra[0].start, extra[1].stop - extra[1].start), F32))
    res = pl.pallas_call(body, name=name, out_shape=out_shape)(parts, *[a for p in params for a in p])
    return [res[4 * i:4 * i + 4] for i in range(n)] + ([res[4 * n]] if extra is not None else [])


def _adamw(name, w, parts, m, v, tr, tc=None):
    r_, c_ = w.shape
    p_ = parts.shape[0]
    tr = min(tr, r_)
    tc = c_ if tc is None else tc
    assert r_ % tr == 0 and c_ % tc == 0

    def body(w_ref, p_ref, m_ref, v_ref, g_ref, d_ref, m2_ref, v2_ref):
        g = _sum_parts(p_ref)
        g_ref[...] = g
        d_ref[...], m2_ref[...], v2_ref[...] = _adam_update(w_ref[...], g, m_ref[...], v_ref[...])

    blk = pl.BlockSpec((tr, tc), lambda i, j: (i, j))
    return pl.pallas_call(
        body, name=name, grid=(r_ // tr, c_ // tc), out_shape=[jax.ShapeDtypeStruct((r_, c_), F32)] * 4,
        in_specs=[blk, pl.BlockSpec((p_, tr, tc), lambda i, j: (0, i, j)), blk, blk], out_specs=[blk] * 4,
        compiler_params=_params(("parallel", "parallel"), VMEM_BIG),
    )(w, parts, m, v)


def _dev_index(p):
    return 4 * p[0] + 2 * p[1] + p[2]


def _all_gather(name, xs):
    n = len(xs)
    hbm = pl.BlockSpec(memory_space=pl.ANY)

    def body(*refs):
        x_refs, o_refs = refs[:n], refs[n:2 * n]
        send_sems, recv_sems, local_sems = refs[2 * n:]
        x, y, c = lax.axis_index("x"), lax.axis_index("y"), lax.axis_index("c")
        me, sibling = (x, y, c), (x, y, 1 - c)
        chips = [(1 - x, y), (x, 1 - y), (1 - x, 1 - y)]

        def copy(a, k, block, to, src=None):
            dst = o_refs[a].at[_dev_index(block)]
            return pltpu.make_async_remote_copy(
                src_ref=dst if src is None else src, dst_ref=dst, send_sem=send_sems.at[a, k],
                recv_sem=recv_sems.at[a, k], device_id=to, device_id_type=MESH)

        mine = [pltpu.make_async_copy(x_refs[a], o_refs[a].at[_dev_index(me)], local_sems.at[a]) for a in range(n)]
        for cp in mine:
            cp.start()
        first = []
        for a in range(n):
            first.append(copy(a, 0, me, sibling, src=x_refs[a]))
            first += [copy(a, 1 + j, me, (*chip, c), src=x_refs[a]) for j, chip in enumerate(chips)]
        for cp in first:
            cp.start()
        passed = []
        for j, chip in enumerate(chips):
            for a in range(n):
                copy(a, 1 + j, (*chip, c), me).wait_recv()
                cp = copy(a, 4 + j, (*chip, c), sibling)
                cp.start()
                passed.append(cp)
        for a in range(n):
            copy(a, 0, sibling, me).wait_recv()
            for j, chip in enumerate(chips):
                copy(a, 4 + j, (*chip, 1 - c), me).wait_recv()
        for cp in first + passed:
            cp.wait_send()
        for cp in mine:
            cp.wait()

    return pl.pallas_call(
        body, name=name, out_shape=[jax.ShapeDtypeStruct((NDEV, *x.shape), x.dtype) for x in xs],
        in_specs=[hbm] * n, out_specs=[hbm] * n,
        scratch_shapes=[pltpu.SemaphoreType.DMA((n, 7)), pltpu.SemaphoreType.DMA((n, 7)), pltpu.SemaphoreType.DMA((n,))],
    )(*xs)


_HBM = pl.BlockSpec(memory_space=pltpu.HBM)
_SEM = pl.BlockSpec(memory_space=pltpu.SEMAPHORE)
_EFFECT = pltpu.SideEffectType.DATAFLOW_SIDE_EFFECTING


def _mesh_position():
    return lax.axis_index("x"), lax.axis_index("y"), lax.axis_index("c")


def _peers(me):
    return [(k, tuple(1 - v if (k >> b) & 1 else v for v, b in zip(me, (2, 1, 0)))) for k in range(1, NDEV)]


EXCHANGE_COPIES = {"gather": NDEV - 1, "scatter": NDEV - 1, "pair": 4, "chips": 3}
NCHIP = NDEV // 2


def _landing_zones(name, xs, mode):
    x_, y_, c_ = _mesh_position()
    mine = (2 * x_ + y_ if mode == "chips" else _dev_index((x_, y_, c_))).astype(jnp.int32).reshape(1)
    lands = []
    for a, x in enumerate(xs):
        rows, cols = x.shape[-2:]
        if mode == "pair":
            lands.append(lax.empty((NCHIP, rows, cols), x.dtype))
            continue
        tr = 256 if rows % 256 == 0 else rows

        def body(me_ref, x_ref, o_ref):
            o_ref[...] = x_ref[...]

        if mode == "gather":
            in_spec = pl.BlockSpec((tr, cols), lambda i, me_ref: (i, 0))
        else:
            in_spec = pl.BlockSpec((None, tr, cols), lambda i, me_ref: (me_ref[0], i, 0))
        lands.append(pl.pallas_call(
            body, name=f"{name}_{a}",
            out_shape=jax.ShapeDtypeStruct((NCHIP if mode == "chips" else NDEV, rows, cols), x.dtype),
            grid_spec=pltpu.PrefetchScalarGridSpec(
                num_scalar_prefetch=1, grid=(rows // tr,), in_specs=[in_spec],
                out_specs=pl.BlockSpec((None, tr, cols), lambda i, me_ref: (me_ref[0], i, 0))),
            compiler_params=_params(("arbitrary",)),
        )(mine, x))
    return lands


def _exchange_copies(x_refs, land_refs, send_sems, recv_sems, mode):
    x_, y_, c_ = me = _mesh_position()
    per_array = EXCHANGE_COPIES[mode]
    out = []

    def add(a, k, src, dst, peer):
        sem = a * per_array + k
        out.append(pltpu.make_async_remote_copy(src_ref=src, dst_ref=dst, send_sem=send_sems.at[sem], recv_sem=recv_sems.at[sem],
                                                device_id=peer, device_id_type=MESH))

    for a, (x_ref, land_ref) in enumerate(zip(x_refs, land_refs)):
        if mode in ("gather", "scatter"):
            for k, peer in _peers(me):
                add(a, k - 1, x_ref.at[_dev_index(peer)] if mode == "scatter" else x_ref, land_ref.at[_dev_index(me)], peer)
        elif mode == "pair":
            for chip in range(NCHIP):
                add(a, chip, x_ref.at[2 * chip + 1 - c_], land_ref.at[chip], (x_, y_, 1 - c_))
        else:
            for k in range(1, NCHIP):
                px, py = (1 - x_ if k & 2 else x_), (1 - y_ if k & 1 else y_)
                add(a, k - 1, x_ref.at[2 * px + py], land_ref.at[2 * x_ + y_], (px, py, c_))
    return out


def _exchange_start(name, xs, lands, mode, dep):
    n = len(xs)

    def body(*refs):
        x_refs, land_refs = refs[:n], refs[n:2 * n]
        send_sems, recv_sems = refs[2 * n + 1], refs[2 * n + 2]
        token = refs[-1]
        for cp in _exchange_copies(x_refs, land_refs, send_sems, recv_sems, mode):
            cp.start()
        token[...] = jnp.zeros_like(token)

    sems = pltpu.SemaphoreType.DMA((n * EXCHANGE_COPIES[mode],))
    res = pl.pallas_call(
        body, name=name,
        out_shape=(sems, sems, *[pltpu.HBM(a.shape, a.dtype) for a in (*xs, *lands)], jax.ShapeDtypeStruct((8, 128), F32)),
        in_specs=[_HBM] * (2 * n) + [pl.BlockSpec(memory_space=pl.ANY)],
        out_specs=(_SEM, _SEM, *[_HBM] * (2 * n), pl.BlockSpec(memory_space=pltpu.VMEM)),
        input_output_aliases={i: 2 + i for i in range(2 * n)},
        compiler_params=pltpu.CompilerParams(has_side_effects=_EFFECT),
    )(*[pltpu.with_memory_space_constraint(a, pltpu.HBM) for a in (*xs, *lands)], dep)
    return res[:-1], res[-1]


def _exchange_wait(name, handles, mode, after):
    send_sems, recv_sems = handles[0], handles[1]
    bufs = handles[2:]
    n = len(bufs) // 2

    def body(*refs):
        x_refs, land_refs = refs[:n], refs[n:2 * n]
        s_sems, r_sems = refs[2 * n], refs[2 * n + 1]
        for cp in _exchange_copies(x_refs, land_refs, s_sems, r_sems, mode):
            cp.wait_send()
            cp.wait_recv()

    res = pl.pallas_call(
        body, name=name, out_shape=tuple(pltpu.HBM(a.shape, a.dtype) for a in bufs),
        in_specs=[_HBM] * (2 * n) + [_SEM, _SEM, pl.BlockSpec(memory_space=pl.ANY)], out_specs=tuple([_HBM] * (2 * n)),
        input_output_aliases={i: i for i in range(2 * n)},
        compiler_params=pltpu.CompilerParams(has_side_effects=_EFFECT),
    )(*bufs, send_sems, recv_sems, after)
    return res[n:]


def _pair_sum(name, x, from_sibling):
    _, rows, cols = x.shape
    tr = 256 if rows % 256 == 0 else rows
    core = lax.axis_index("c").astype(jnp.int32).reshape(1)

    def body(c_ref, x_ref, s_ref, o_ref):
        o_ref[...] = (x_ref[...].astype(F32) + s_ref[...].astype(F32)).astype(o_ref.dtype)

    return pl.pallas_call(
        body, name=name, out_shape=jax.ShapeDtypeStruct((NCHIP, rows, cols), x.dtype),
        grid_spec=pltpu.PrefetchScalarGridSpec(
            num_scalar_prefetch=1, grid=(NCHIP, rows // tr),
            in_specs=[pl.BlockSpec((None, tr, cols), lambda j, i, c_ref: (2 * j + c_ref[0], i, 0)),
                      pl.BlockSpec((None, tr, cols), lambda j, i, c_ref: (j, i, 0))],
            out_specs=pl.BlockSpec((None, tr, cols), lambda j, i, c_ref: (j, i, 0))),
        compiler_params=_params(("parallel", "parallel")),
    )(core, x, from_sibling)


def kernel(x, c, positions, norm_w, mod_w, mod_b, attn_w_in, attn_w_out, ssd_w_in, ssd_conv_w, ssd_conv_b, ssd_dt_bias, ssd_a_log, ssd_d, ssd_norm_w, ssd_w_out, final_norm_w, loss_target, m_norm_w, m_mod_w, m_mod_b, m_attn_w_in, m_attn_w_out, m_ssd_w_in, m_ssd_conv_w, m_ssd_conv_b, m_ssd_dt_bias, m_ssd_a_log, m_ssd_d, m_ssd_norm_w, m_ssd_w_out, m_final_norm_w, v_norm_w, v_mod_w, v_mod_b, v_attn_w_in, v_attn_w_out, v_ssd_w_in, v_ssd_conv_w, v_ssd_conv_b, v_ssd_dt_bias, v_ssd_a_log, v_ssd_d, v_ssd_norm_w, v_ssd_w_out, v_final_norm_w):
    s_len, dm = x.shape[1], x.shape[2]
    me = 4 * lax.axis_index("x") + 2 * lax.axis_index("y") + lax.axis_index("c")
    x0 = x.reshape(s_len, dm)
    tgt = loss_target.reshape(s_len, dm)
    aw = 3 * 512
    si = 2 * dm
    sxbc = 2 * si
    n_ssd_in = ssd_w_in.shape[2] * NDEV

    g_ai, c_all = _all_gather("gather_attn_w_in", [attn_w_in[0].astype(BF16), c])
    w_ai = g_ai.transpose(1, 0, 2).reshape(dm, 4 * aw)
    c_all = c_all.reshape(NDEV, dm)

    part = _mod_part(c_all, mod_w)
    (part_all,) = _all_gather("gather_mod", [part])
    mod_nb = jnp.stack([lax.dynamic_index_in_dim(part_all, i * NDEV + me, axis=1, keepdims=False).reshape(3 * dm)
                        for i in range(2)])

    ssd_small = _pack_ssd_small(ssd_conv_w[0], ssd_conv_b, ssd_norm_w)
    ao_shard = [attn_w_out[0].astype(BF16)]
    ao_handles, ao_token = _exchange_start("w_out_start", ao_shard, _landing_zones("w_out_place", ao_shard, "gather"), "gather",
                                           part_all)
    late_shards = [ssd_w_in[0].T.astype(BF16), ssd_w_out[0].astype(BF16), ssd_small]
    w_handles, w_token = _exchange_start("weights_start", late_shards, _landing_zones("weights_place", late_shards, "gather"),
                                         "gather", ao_token)
    (shift0, scale0, gate0, nw0), (shift1, scale1, gate1, nw1) = _mod_finish(mod_nb, mod_b, norm_w, [ao_token, w_token])
    shift, scale, gate, nw = [shift0, shift1], [scale0, scale1], [gate0, gate1], [nw0, nw1]

    hn0 = _norm_mod_fwd("norm0", x0, nw[0], scale[0], shift[0])
    inv_freq = ROPE_THETA ** (-jnp.arange(0, ROT_DIM, 2, dtype=F32) / ROT_DIM)
    lane = jnp.arange(128) % HEAD_DIM
    inv_row = jnp.where(lane < ROT_DIM, inv_freq[lane % (ROT_DIM // 2)], 0.0).reshape(1, 128).astype(F32)
    tabs = _rope_tables(positions.reshape(s_len, 1), inv_row)
    qk = _matmul("proj_qk", hn0, w_ai, "nn", F32, MM_T, MM_T, dm, epilogue=_rot_fwd, mrows=tabs, n_out=2 * aw)
    v = _matmul("proj_vz", hn0, w_ai, "nn", F32, MM_T, MM_T, dm, b_noff=2 * aw, n_out=2 * aw)
    z0 = (v, 1)
    att = [_attn_fwd(g, qk, v) for g in range(3)]
    os_, lses = [a[0] for a in att], [a[1] for a in att]
    (g_ao,) = _exchange_wait("w_out_wait", ao_handles, "gather", lses[2])
    a0, y0, x1 = _attn_out(os_, lses, z0, x0, gate[0], g_ao.reshape(aw, dm))

    hn1 = _norm_mod_fwd("norm1", x1, nw[1], scale[1], shift[1])
    g_si, g_so, g_small = _exchange_wait("weights_wait", w_handles, "gather", hn1)
    w_ao = g_ao.reshape(aw, dm)
    w_si_t = g_si.reshape(n_ssd_in, dm)
    w_so = g_so.reshape(si, dm)
    conv_w = g_small[:, 0:CONV_WIDTH, :].transpose(1, 0, 2).reshape(CONV_WIDTH, sxbc)
    conv_b = g_small[:, 5, :].reshape(1, sxbc)
    snw = g_small[:, 6, 0:si // NDEV].reshape(1, si)
    ndt = 2 * SSD_HEADS
    z1 = _matmul("ssd_proj_z", hn1, w_si_t, "nt", F32, MM_T, MM_T, dm, n_out=si)
    xpre = _matmul("ssd_proj_xbc", hn1, w_si_t, "nt", F32, MM_T, MM_T, dm, b_noff=si, n_out=sxbc)
    dt_raw = _matmul("ssd_proj_dt", hn1, w_si_t, "nt", F32, MM_T, ndt, dm, b_noff=si + sxbc, n_out=ndt)
    xbc = _conv_fwd(xpre, conv_w, conv_b)
    widen = lambda a: jnp.pad(a, ((0, 0), (0, SSD_DTW - ndt)))
    dt_raw = widen(dt_raw)
    dt_bias = widen(ssd_dt_bias.reshape(1, ndt))
    alog = widen(ssd_a_log.reshape(1, ndt))
    dt = _softplus_fwd(dt_raw, dt_bias)
    y_f, st_f = _ssd_fwd(xbc, dt, alog, 0)
    y_b, st_b = _ssd_fwd(xbc, dt, alog, 1)
    d_e = jnp.repeat(ssd_d.reshape(SSD_HEADS), HEAD_DIM).reshape(1, si)

    fnw = final_norm_w.reshape(1, dm)
    u, dx2, dy1, g_fnw, dgate1, loss_part = _ssd_tail_loss(y_f, y_b, xbc, z1, d_e, snw, w_so, x1, tgt, gate[1], fnw)
    gw_so = _matmul("ssd_out_dw", u, dy1, "tn", BF16, MM_T, MM_T, MM_T)
    dys, dz1, g_snw, g_d = _gate_norm_bwd(dy1, w_so, y_f, y_b, xbc, z1, d_e, snw)
    dxbc_f, ddt_f, dalog_f = _ssd_bwd(xbc, dt, alog, st_f, dys, d_e, 0)
    dxbc_b, ddt_b, dalog_b = _ssd_bwd(xbc, dt, alog, st_b, dys, d_e, 1)
    dpre, g_cw, g_cb = _conv_bwd(xpre, dxbc_f, dxbc_b, conv_w, conv_b)
    ddt_raw, g_dtb = _softplus_bwd(ddt_f, ddt_b, dt_raw, dt_bias)
    ddt_raw = ddt_raw[:, :ndt]
    dhn1 = [_matmul("ssd_proj_z_dx", dz1, w_si_t, "nn", F32, MM_T, MM_T, MM_T),
            _matmul("ssd_proj_xbc_dx", dpre, w_si_t, "nn", F32, MM_T, MM_T, MM_T, b_koff=si)]
    gw_si_t = _matmul("ssd_proj_z_dw", dz1, hn1, "tn", BF16, MM_T, MM_T, MM_T, dest=(n_ssd_in, 0, None))
    gw_si_t = _matmul("ssd_proj_xbc_dw", dpre, hn1, "tn", BF16, MM_T, MM_T, MM_T, dest=(n_ssd_in, si, gw_si_t))
    gw_si_t = _matmul("ssd_proj_dt_dw", ddt_raw, hn1, "tn", BF16, ndt, MM_T, MM_T, dest=(n_ssd_in, si + sxbc, gw_si_t))

    l1_grads = [gw_so.reshape(NDEV, si // NDEV, dm), gw_si_t.reshape(NDEV, n_ssd_in // NDEV, dm),
                _pack_ssd_small_blocks(g_cw, g_cb, g_snw)]
    l1_handles, l1_token = _exchange_start("l1_grads_start", l1_grads, _landing_zones("l1_grads_place", l1_grads, "scatter"),
                                           "scatter", dhn1[1])
    dx1, dy0, g_nw1, dsc1, dsh1, dgate0 = _norm_mod_bwd(
        "ssd_proj_dt_dx_norm1_bwd", (ddt_raw, w_si_t, "nn", ndt, dict(b_koff=si + sxbc)), x1, dhn1, dx2,
        nw[1], scale[1], shift[1], prev=(y0, gate[0] + l1_token[0:1, 0:1]))

    gw_ao = _matmul("attn_out_dw", a0, dy0, "tn", BF16, aw // 2, MM_T, MM_T)
    dos, dls, dz0 = _mix_bwd(dy0, w_ao, os_, lses, z0)
    datt = [_attn_bwd(g, qk, v, os_[g], lses[g], dos[g], dls[g]) for g in range(3)]
    dqkv = _rot_pack_bwd([t[0] for t in datt], [t[1] for t in datt], [t[2] for t in datt], tabs)
    wcol = attn_w_in.shape[2]
    gw_ai = _matmul("proj_qkv_dw", hn0, dqkv, "tn", BF16, MM_T, wcol, MM_T, out_blocks=3 * aw // wcol, dest=(NDEV, 0, None))
    gw_ai = _matmul("proj_z_dw", hn0, dz0, "tn", BF16, MM_T, wcol, MM_T, out_blocks=aw // wcol,
                    dest=(NDEV, 3 * aw // wcol, gw_ai))
    after_start = lambda acc, t: acc + t
    zero_row = lambda token: jnp.tile(token[0:1], (1, dm // 128))
    l0_grads = [gw_ai, gw_ao.reshape(NDEV, aw // NDEV, dm)]
    pair_handles, pair_token = _exchange_start("l0_pair_start", l0_grads, _landing_zones("l0_pair_place", l0_grads, "pair"),
                                               "pair", dqkv)
    dhn0_z = _matmul("proj_z_dx", dz0, w_ai, "nt", F32, MM_T, MM_T, aw, b_koff=3 * aw, n_out=dm, epilogue=after_start,
                     ncols=(zero_row(pair_token),))
    from_sibling = _exchange_wait("l0_pair_wait", pair_handles, "pair", dhn0_z)
    chip_sums = [_pair_sum(f"l0_pair_sum_{a}", g, s) for a, (g, s) in enumerate(zip(l0_grads, from_sibling))]
    l0_handles, l0_token = _exchange_start("l0_grads_start", chip_sums, _landing_zones("l0_grads_place", chip_sums, "chips"),
                                           "chips", dhn0_z)
    dx0, g_nw0, dsc0, dsh0 = _norm_mod_bwd(
        "proj_qkv_dx_norm0_bwd", (dqkv, w_ai, "nt", aw, dict(n_out=dm)), x0, [dhn0_z], dx1,
        nw[0], scale[0], shift[0] + zero_row(l0_token))

    small_g = [_pack_small([dsh0, dsc0, dgate0, dsh1, dsc1, dgate1, g_nw0, g_nw1, g_fnw], g_dtb, [dalog_f, dalog_b], g_d, loss_part)]
    sm_handles, sm_token = _exchange_start("small_grads_start", small_g, _landing_zones("small_grads_place", small_g, "gather"),
                                           "gather", dx0)

    whole = (slice(None), slice(None))
    r_so, r_si, r_small = _exchange_wait("l1_grads_wait", l1_handles, "scatter", sm_token)
    si_out = [o.T for o in _adamw("adamw_ssd_w_in", ssd_w_in[0].T, r_si, m_ssd_w_in[0].T, v_ssd_w_in[0].T, n_ssd_in // NDEV, 256)]
    so_out = _adamw("adamw_ssd_w_out", ssd_w_out[0], r_so, m_ssd_w_out[0], v_ssd_w_out[0], 256)
    cw_cols = ssd_conv_w.shape[2]
    cw_out, cb_out, snw_out = _adamw_windows(
        "adamw_ssd_small", r_small,
        [(ssd_conv_w, m_ssd_conv_w, v_ssd_conv_w), (ssd_conv_b, m_ssd_conv_b, v_ssd_conv_b),
         (ssd_norm_w, m_ssd_norm_w, v_ssd_norm_w)],
        [(0, slice(0, CONV_WIDTH), slice(0, cw_cols), (0, slice(None), slice(None))),
         (1, slice(5, 6), slice(0, cw_cols), whole), (2, slice(6, 7), slice(0, si // NDEV), whole)])
    r_ai, r_ao = _exchange_wait("l0_grads_wait", l0_handles, "chips", so_out[0])
    ai_out = _adamw("adamw_attn_w_in", attn_w_in[0], r_ai, m_attn_w_in[0], v_attn_w_in[0], 256)
    ao_out = _adamw("adamw_attn_w_out", attn_w_out[0], r_ao, m_attn_w_out[0], v_attn_w_out[0], 192)

    (small_all,) = _exchange_wait("small_grads_wait", sm_handles, "gather", ai_out[0])
    full = slice(0, PACK_COLS)
    nhd = SSD_HEADS
    windows = [(0, slice(3 * i + k, 3 * i + k + 1), full, (slice(i, i + 1), slice(k * dm, (k + 1) * dm)))
               for i in range(2) for k in range(3)]
    windows += [(1, slice(6 + i, 7 + i), full, (slice(i, i + 1), slice(None))) for i in range(2)]
    windows += [(2, slice(8, 9), full, whole)]
    windows += [(3 + q, slice(9, 10), slice(2 * nhd * q + nhd * j, 2 * nhd * q + nhd * (j + 1)), (0, slice(j, j + 1), slice(None)))
                for q in range(2) for j in range(2)]
    windows += [(5, slice(9, 10), slice(4 * nhd, 5 * nhd), whole)]
    as_row = lambda a: a.reshape(1, dm)
    mb_out, nw_out, fnw_out, dtb_out, alog_out, d_out, loss = _adamw_windows(
        "adamw_small", small_all,
        [(mod_b, m_mod_b, v_mod_b), (norm_w, m_norm_w, v_norm_w), (fnw, as_row(m_final_norm_w), as_row(v_final_norm_w)),
         (ssd_dt_bias, m_ssd_dt_bias, v_ssd_dt_bias), (ssd_a_log, m_ssd_a_log, v_ssd_a_log), (ssd_d, m_ssd_d, v_ssd_d)],
        windows, extra=(slice(9, 10), slice(256, 257)))
    loss = loss.reshape(())

    ncol = mod_w.shape[2]
    dmod_all = small_all[:, 0:6, :].reshape(NDEV, 2, 3 * dm)
    dmod_sh = lax.dynamic_slice_in_dim(dmod_all, me * ncol, ncol, axis=2).transpose(1, 0, 2)
    g_modw = _mod_grad(c_all, dmod_sh).reshape(1, 2 * dm, ncol)
    modw_out = _adamw("adamw_mod_w", mod_w.reshape(2 * dm, ncol), g_modw, m_mod_w.reshape(2 * dm, ncol),
                      v_mod_w.reshape(2 * dm, ncol), 256)

    per_kind = []
    for k in range(4):
        per_kind.append([
            nw_out[k], modw_out[k].reshape(mod_w.shape), mb_out[k], ai_out[k][None], ao_out[k][None], si_out[k][None],
            cw_out[k], cb_out[k], dtb_out[k], alog_out[k], d_out[k], snw_out[k], so_out[k][None], fnw_out[k].reshape(dm)])
    return (loss, dx0.reshape(x.shape), *per_kind[0], *per_kind[1], *per_kind[2], *per_kind[3])


def _pack_ssd_small_blocks(g_cw, g_cb, g_nw):
    nper = g_cw.shape[1] // NDEV
    nwper = g_nw.shape[1] // NDEV

    def body(cw_ref, cb_ref, nw_ref, o_ref):
        o_ref[...] = jnp.zeros_like(o_ref)
        for d in range(NDEV):
            o_ref[d, 0:5, :] = cw_ref[:, d * nper:(d + 1) * nper]
            o_ref[d, 5:6, :] = cb_ref[:, d * nper:(d + 1) * nper]
            o_ref[d, 6:7, 0:nwper] = nw_ref[:, d * nwper:(d + 1) * nwper]

    return pl.pallas_call(body, name="pack_ssd_small_grads", out_shape=jax.ShapeDtypeStruct((NDEV, 8, nper), F32))(g_cw, g_cb, g_nw)
```

```python
import functools
import math

import jax
import jax.numpy as jnp
from jax import lax
from jax.experimental import pallas as pl
from jax.experimental.pallas import tpu as pltpu

F32 = jnp.float32
BF16 = jnp.bfloat16
HI = lax.Precision.HIGHEST
MESH = pl.DeviceIdType.MESH
NDEV = 8

NORM_EPS = 1e-6
ROPE_THETA = 500000.0
ROT_DIM = 16
HEAD_DIM = 64
DILATIONS = (1, 4, 16)
BAND = 64
NEG_BIG = -1e30
CHUNK = 128
SSD_HEADS = 32
SSD_GROUPS = 8
CONV_WIDTH = 5

ADAM_LR = 0.001
ADAM_B1 = 0.9
ADAM_B2 = 0.999
ADAM_EPS = 1e-08
ADAM_WD = 0.01
ADAM_STEP = 10

VMEM_BIG = 56 * 1024 * 1024
MM_T = 1024


def _params(sem=None, vmem=None):
    kw = {}
    if sem is not None:
        kw["dimension_semantics"] = sem
    if vmem is not None:
        kw["vmem_limit_bytes"] = vmem
    return pltpu.CompilerParams(**kw)


def _dg(a, b, ca, cb, prec=None):
    return lax.dot_general(a, b, (((ca,), (cb,)), ((), ())), preferred_element_type=F32, precision=prec)


def _nn(a, b):
    return _dg(a.astype(BF16), b.astype(BF16), 1, 0)


def _nt(a, b):
    return _dg(a.astype(BF16), b.astype(BF16), 1, 1)


def _tn(a, b):
    return _dg(a.astype(BF16), b.astype(BF16), 0, 0)


def _hnn(a, b):
    return _dg(a, b, 1, 0, HI)


@jax.custom_vjp
def _bnn(a, b):
    return _nn(a, b)


_bnn.defvjp(lambda a, b: (_nn(a, b), (a, b)), lambda r, g: (_nt(g, r[1]), _tn(r[0], g)))


@jax.custom_vjp
def _bnt(a, b):
    return _nt(a, b)


_bnt.defvjp(lambda a, b: (_nt(a, b), (a, b)), lambda r, g: (_nn(g, r[1]), _tn(g, r[0])))


@jax.custom_vjp
def _btn(a, b):
    return _tn(a, b)


_btn.defvjp(lambda a, b: (_tn(a, b), (a, b)), lambda r, g: (_nt(r[1], g), _nn(r[0], g)))


def _silu(x):
    return x * jax.nn.sigmoid(x)


def _matmul(name, a, b, mode, out_dtype, tm, tn, tk, *, epilogue=None, tiled=(), mrows=(), ncols=(),
            b_noff=0, b_koff=0, n_out=None, out_blocks=None, dest=None):
    if mode == "tn":
        K, M = a.shape
    else:
        M, K = a.shape
    N = n_out if n_out is not None else (b.shape[0] if mode == "nt" else b.shape[1])
    tm, tn, tk = min(tm, M), min(tn, N), min(tk, K)
    assert M % tm == 0 and N % tn == 0 and K % tk == 0, (name, M, N, K, tm, tn, tk)
    assert b_noff % tn == 0 and b_koff % tk == 0
    no, ko = b_noff // tn, b_koff // tk
    nk = K // tk
    if mode == "tn":
        a_spec = pl.BlockSpec((tk, tm), lambda i, j, k: (k, i))
    else:
        a_spec = pl.BlockSpec((tm, tk), lambda i, j, k: (i, k))
    if mode == "nt":
        b_spec = pl.BlockSpec((tn, tk), lambda i, j, k: (j + no, k + ko))
    else:
        b_spec = pl.BlockSpec((tk, tn), lambda i, j, k: (k + ko, j + no))
    specs = [a_spec, b_spec]
    specs += [pl.BlockSpec((tm, tn), lambda i, j, k: (i, j)) for _ in tiled]
    specs += [pl.BlockSpec((tm, r.shape[1]), lambda i, j, k: (i, 0)) for r in mrows]
    specs += [pl.BlockSpec((1, tn), lambda i, j, k: (0, j)) for _ in ncols]
    total, off, earlier = dest if dest is not None else (None, 0, None)
    if out_blocks is None:
        assert off % tm == 0
        mo = off // tm
        out_shape = jax.ShapeDtypeStruct((M if total is None else total, N), out_dtype)
        out_spec = pl.BlockSpec((tm, tn), lambda i, j, k: (i + mo, j))
    else:
        nper = N // out_blocks
        assert nper % tn == 0
        jb = nper // tn
        out_shape = jax.ShapeDtypeStruct((out_blocks if total is None else total, M, nper), out_dtype)
        out_spec = pl.BlockSpec((None, tm, tn), lambda i, j, k: (j // jb + off, i, j % jb))
    if earlier is not None:
        assert earlier.shape == out_shape.shape and earlier.dtype == out_shape.dtype
    ne = len(tiled) + len(mrows) + len(ncols)
    dot = {"nn": _nn, "nt": _nt, "tn": _tn}[mode]

    def body(a_ref, b_ref, *rest):
        extras, o_ref = rest[:ne], rest[ne]

        def finish(acc):
            if epilogue is not None:
                acc = epilogue(acc, *[e[...] for e in extras])
            o_ref[...] = acc.astype(o_ref.dtype)

        if nk == 1:
            finish(dot(a_ref[...], b_ref[...]))
        else:
            acc_ref = rest[ne + 1]
            k = pl.program_id(2)

            @pl.when(k == 0)
            def _():
                acc_ref[...] = jnp.zeros_like(acc_ref)

            acc_ref[...] += dot(a_ref[...], b_ref[...])

            @pl.when(k == nk - 1)
            def _():
                finish(acc_ref[...])

    args = [a, b, *tiled, *mrows, *ncols]
    aliases = {}
    if earlier is not None:
        specs.append(pl.BlockSpec(memory_space=pl.ANY))
        aliases = {len(args): 0}
        args.append(earlier)

    def body_with_dest(*refs):
        body(*refs[:2 + ne], *refs[2 + ne + (earlier is not None):])

    return pl.pallas_call(
        body_with_dest, name=name, out_shape=out_shape, grid=(M // tm, N // tn, nk),
        in_specs=specs, out_specs=out_spec, input_output_aliases=aliases,
        scratch_shapes=[] if nk == 1 else [pltpu.VMEM((tm, tn), F32)],
        compiler_params=_params(("parallel", "parallel", "arbitrary"), VMEM_BIG),
    )(*args)


def _matmul_rows(name, a, b, mode, tm, tk, fn, rows, consts, outs, accs, *, n_out=None, b_noff=0, b_koff=0):
    rl = [(t, t.shape[1], 0) if not isinstance(t, tuple) else t for t in rows]
    make_a = a if callable(a) else None
    M, K = (rl[0][0].shape[0], b.shape[1 if mode == "nt" else 0]) if make_a else a.shape
    N = n_out if n_out is not None else (b.shape[0] if mode == "nt" else b.shape[1])
    tm, tk = min(tm, M), min(tk, K)
    assert M % tm == 0 and K % tk == 0 and b_koff % tk == 0 and b_noff % N == 0, (name, M, N, K)
    no, ko, nk = b_noff // N, b_koff // tk, K // tk
    assert make_a is None or nk == 1
    nr, nc, no_, na = len(rl), len(consts), len(outs), len(accs)
    dot = _nt if mode == "nt" else _nn

    def body(*refs):
        a_ref, b_ref, rest = (None, refs[0], refs[1:]) if make_a else (refs[0], refs[1], refs[2:])
        r_refs, c_refs = rest[:nr], rest[nr:nr + nc]
        o_refs, acc_refs = rest[nr + nc:nr + nc + no_], rest[nr + nc + no_:nr + nc + no_ + na]
        i, k = pl.program_id(0), pl.program_id(1)

        def finish(prod, *made):
            res_o, res_a = fn(prod, *made, *[r[...] for r in r_refs], *[c[...] for c in c_refs])
            for r, v in zip(o_refs, res_o, strict=True):
                r[...] = v.astype(r.dtype)
            if acc_refs:
                @pl.when(i == 0)
                def _():
                    for r in acc_refs:
                        r[...] = jnp.zeros_like(r)

                for r, v in zip(acc_refs, res_a, strict=True):
                    r[...] += v

        if make_a:
            left = make_a(*[r[...] for r in r_refs], *[c[...] for c in c_refs])
            finish(dot(left, b_ref[...]), left)
        elif nk == 1:
            finish(dot(a_ref[...], b_ref[...]))
        else:
            prod_ref = rest[-1]

            @pl.when(k == 0)
            def _():
                prod_ref[...] = jnp.zeros_like(prod_ref)

            prod_ref[...] += dot(a_ref[...], b_ref[...])

            @pl.when(k == nk - 1)
            def _():
                finish(prod_ref[...])

    if mode == "nt":
        b_spec = pl.BlockSpec((N, tk), lambda i, k: (no, k + ko))
    else:
        b_spec = pl.BlockSpec((tk, N), lambda i, k: (k + ko, no))
    in_specs = ([] if make_a else [pl.BlockSpec((tm, tk), lambda i, k: (i, k))]) + [b_spec]
    in_specs += [pl.BlockSpec((tm, w), functools.partial(lambda i, k, cb: (i, cb), cb=cb)) for (_, w, cb) in rl]
    in_specs += [pl.BlockSpec(c.shape, lambda i, k: (0, 0)) for c in consts]
    out_specs = [pl.BlockSpec((tm, c), lambda i, k: (i, 0)) for (c, _) in outs]
    out_specs += [pl.BlockSpec(shp, lambda i, k: (0, 0)) for shp in accs]
    out_shape = [jax.ShapeDtypeStruct((M, c), dt) for (c, dt) in outs] + [jax.ShapeDtypeStruct(shp, F32) for shp in accs]
    res = pl.pallas_call(
        body, name=name, out_shape=out_shape, grid=(M // tm, nk), in_specs=in_specs, out_specs=out_specs,
        scratch_shapes=[] if nk == 1 else [pltpu.VMEM((tm, N), F32)],
        compiler_params=_params(("arbitrary" if accs else "parallel", "arbitrary"), VMEM_BIG),
    )(*([] if make_a else [a]), b, *[t[0] for t in rl], *consts)
    return res[:no_], res[no_:]


def _rowwise(name, fn, tiled, consts, outs, accs, ts):
    tl = [(t, t.shape[1], 0) if not isinstance(t, tuple) else t for t in tiled]
    s_len = tl[0][0].shape[0]
    assert s_len % ts == 0
    nt_, nc_, no_ = len(tl), len(consts), len(outs)

    def body(*refs):
        t_refs, c_refs = refs[:nt_], refs[nt_:nt_ + nc_]
        o_refs, a_refs = refs[nt_ + nc_:nt_ + nc_ + no_], refs[nt_ + nc_ + no_:]
        res_o, res_a = fn(*[r[...] for r in t_refs], *[r[...] for r in c_refs])
        for r, v in zip(o_refs, res_o, strict=True):
            r[...] = v.astype(r.dtype)
        if a_refs:
            @pl.when(pl.program_id(0) == 0)
            def _():
                for r in a_refs:
                    r[...] = jnp.zeros_like(r)

            for r, v in zip(a_refs, res_a, strict=True):
                r[...] += v

    in_specs = [pl.BlockSpec((ts, w), functools.partial(lambda i, cb: (i, cb), cb=cb)) for (_, w, cb) in tl]
    in_specs += [pl.BlockSpec(c.shape, lambda i: (0, 0)) for c in consts]
    out_specs = [pl.BlockSpec((ts, c), lambda i: (i, 0)) for (c, _) in outs]
    out_specs += [pl.BlockSpec(shp, lambda i: (0, 0)) for shp in accs]
    out_shape = [jax.ShapeDtypeStruct((s_len, c), dt) for (c, dt) in outs]
    out_shape += [jax.ShapeDtypeStruct(shp, F32) for shp in accs]
    res = pl.pallas_call(
        body, name=name, out_shape=out_shape, grid=(s_len // ts,), in_specs=in_specs, out_specs=out_specs,
        compiler_params=_params(("arbitrary",) if accs else ("parallel",), VMEM_BIG),
    )(*[t[0] for t in tl], *consts)
    return res[:no_], res[no_:]


def _norm_mod_fn(x, nw, sc, sh):
    r = lax.rsqrt(jnp.mean(x * x, axis=-1, keepdims=True) + NORM_EPS)
    return (x * r * nw) * (1.0 + sc) + sh


def _norm_mod_fwd(name, x, nw, sc, sh):
    (hn,), _ = _rowwise(name, lambda x, nw, sc, sh: ([_norm_mod_fn(x, nw, sc, sh)], []),
                        [x], [nw, sc, sh], [(x.shape[1], BF16)], [], 512)
    return hn


def _norm_mod_bwd(name, last, x, dhn_parts, dres, nw, sc, sh, prev=None):
    n = len(dhn_parts)
    d = x.shape[1]
    a, b, mode, tk, kw = last

    def fn(dhn, x, *rest):
        for p in rest[:n]:
            dhn = dhn + p
        dres, rest = rest[n], rest[n + 1:]
        y_prev, (nw, sc, sh), gate = (rest[0], rest[1:4], rest[4]) if prev is not None else (None, rest[0:3], None)
        _, vjp = jax.vjp(_norm_mod_fn, x, nw, sc, sh)
        dx, dnw, dsc, dsh = vjp(dhn)
        dx = dx + dres
        if prev is None:
            return [dx], [dnw, dsc, dsh]
        return [dx, gate * dx], [dnw, dsc, dsh, jnp.sum(dx * y_prev, axis=0, keepdims=True)]

    rows = [x, *dhn_parts, dres] + ([prev[0]] if prev is not None else [])
    consts = [nw, sc, sh] + ([prev[1]] if prev is not None else [])
    outs = [(d, F32)] + ([(d, BF16)] if prev is not None else [])
    res_o, res_a = _matmul_rows(name, a, b, mode, 512, tk, fn, rows, consts, outs, [(1, d)] * (3 + (prev is not None)), **kw)
    return (*res_o, *res_a)


def _rope_tables(pos_col, inv_row):
    def fn(pos, inv):
        ang = pos.astype(F32) * inv
        e = lax.broadcasted_iota(jnp.int32, (1, 128), 1) % HEAD_DIM
        cos, sin = jnp.cos(ang), jnp.sin(ang)
        half = ROT_DIM // 2
        return [jnp.where(e < ROT_DIM, cos, 1.0), jnp.where(e < half, -sin, 0.0),
                jnp.where((e >= half) & (e < ROT_DIM), sin, 0.0)], []

    (c, sa, sb), _ = _rowwise("rope_tables", fn, [pos_col], [inv_row], [(128, F32)] * 3, [], 512)
    return c, sa, sb


def _rot_fwd(t, c, sa, sb):
    n = t.shape[1]
    rep = n // 128
    c, sa, sb = (jnp.tile(u, (1, rep)) for u in (c, sa, sb))
    return t * c + pltpu.roll(t, n - ROT_DIM // 2, 1) * sa + pltpu.roll(t, ROT_DIM // 2, 1) * sb


def _rot_bwd(g, c, sa, sb):
    n = g.shape[1]
    rep = n // 128
    c, sa, sb = (jnp.tile(u, (1, rep)) for u in (c, sa, sb))
    return g * c + pltpu.roll(g * sa, ROT_DIM // 2, 1) + pltpu.roll(g * sb, n - ROT_DIM // 2, 1)


ATT_TQ = 128
ATT_TK = ATT_TQ + 2 * BAND


def _attn_specs(g, s_len):
    def blk(off):
        return pl.BlockSpec((s_len, 128), functools.partial(lambda hp, off: (0, off + hp), off=off))

    return blk(4 * g), blk(12 + 4 * g), blk(4 * g), blk(0)


def _attn_tile_geometry(t, d, l):
    nts = l // ATT_TQ
    r = t // nts
    ts = t % nts
    q0 = ts * ATT_TQ
    ws = jnp.clip(q0 - BAND, 0, l - ATT_TK)
    kind = jnp.where(ts == 0, 0, jnp.where(ts == nts - 1, 2, 1))
    if d == 1:
        return pl.ds(pl.multiple_of(q0, ATT_TQ), ATT_TQ), pl.ds(pl.multiple_of(ws, BAND), ATT_TK), kind
    return pl.ds(r + d * q0, ATT_TQ, stride=d), pl.ds(r + d * ws, ATT_TK, stride=d), kind


def _attn_fill_bias(bias_ref):
    iq = lax.broadcasted_iota(jnp.int32, (2 * ATT_TQ, 1), 0) % ATT_TQ
    ik = lax.broadcasted_iota(jnp.int32, (1, ATT_TK), 1)
    for i, off in enumerate((0, -BAND, -2 * BAND)):
        bias_ref[i] = jnp.where(jnp.abs(ik + off - iq) <= BAND, 0.0, NEG_BIG)


def _split_heads(t, in_h):
    zero = jnp.zeros_like(t)
    return jnp.concatenate([jnp.where(in_h[0], t, zero), jnp.where(in_h[1], t, zero)], axis=0)


def _attn_fwd(g, qk, v):
    s_len = qk.shape[0]
    d = DILATIONS[g]
    l = s_len // d
    assert l % ATT_TQ == 0 and l >= ATT_TK
    q_spec, k_spec, v_spec, o_spec = _attn_specs(g, s_len)
    scale = 1.0 / math.sqrt(HEAD_DIM)

    def body(q_ref, k_ref, v_ref, o_ref, lse_ref, bias_ref):
        lane = lax.broadcasted_iota(jnp.int32, (1, 128), 1)
        in_h = [lane < HEAD_DIM, lane >= HEAD_DIM]
        _attn_fill_bias(bias_ref)

        def tile(t, carry):
            rows, win, kind = _attn_tile_geometry(t, d, l)
            q = (q_ref[rows, :] * scale).astype(BF16)
            k = k_ref[win, :].astype(BF16)
            vv = v_ref[win, :].astype(BF16)
            s = _nt(_split_heads(q, in_h), k) + bias_ref[kind]
            m = jnp.max(s, axis=1, keepdims=True)
            p = jnp.exp(s - m)
            den = jnp.sum(p, axis=1, keepdims=True)
            out = _nn(p, vv) / den
            lse = m + jnp.log(den)
            o_ref[rows, :] = jnp.where(in_h[0], out[:ATT_TQ], out[ATT_TQ:])
            lse_ref[rows, :] = jnp.where(in_h[0], lse[:ATT_TQ], lse[ATT_TQ:])
            return carry

        lax.fori_loop(0, s_len // ATT_TQ, tile, 0, unroll=4)

    return pl.pallas_call(
        body, name=f"attn_fwd_g{g}", grid=(4,),
        out_shape=[jax.ShapeDtypeStruct((s_len, 512), F32)] * 2,
        in_specs=[q_spec, k_spec, v_spec], out_specs=[o_spec, o_spec],
        scratch_shapes=[pltpu.VMEM((3, 2 * ATT_TQ, ATT_TK), F32)],
        compiler_params=_params(("parallel",), VMEM_BIG),
    )(qk, qk, v)


def _attn_bwd(g, qk, v, o, lse, do, dlse):
    s_len = qk.shape[0]
    d = DILATIONS[g]
    l = s_len // d
    q_spec, k_spec, v_spec, o_spec = _attn_specs(g, s_len)
    scale = 1.0 / math.sqrt(HEAD_DIM)

    def body(q_ref, k_ref, v_ref, o_ref, lse_ref, do_ref, dlse_ref, dq_ref, dk_ref, dv_ref, bias_ref):
        lane = lax.broadcasted_iota(jnp.int32, (1, 128), 1)
        in_h = [lane < HEAD_DIM, lane >= HEAD_DIM]
        dk_ref[...] = jnp.zeros_like(dk_ref)
        dv_ref[...] = jnp.zeros_like(dv_ref)
        _attn_fill_bias(bias_ref)

        def tile(t, carry):
            rows, win, kind = _attn_tile_geometry(t, d, l)
            k, vv = k_ref[win, :].astype(BF16), v_ref[win, :].astype(BF16)
            dout, lse_t, dlse_t = do_ref[rows, :], lse_ref[rows, :], dlse_ref[rows, :]
            od = dout * o_ref[rows, :]
            q2 = _split_heads((q_ref[rows, :] * scale).astype(BF16), in_h)
            do2 = _split_heads(dout.astype(BF16), in_h)
            head_col = lambda a: jnp.concatenate([a[:, 0:1], a[:, HEAD_DIM:HEAD_DIM + 1]], axis=0)
            delta = jnp.concatenate([jnp.sum(jnp.where(m, od, 0.0), axis=1, keepdims=True) for m in in_h], axis=0)
            p = jnp.exp(_nt(q2, k) + bias_ref[kind] - head_col(lse_t))
            ds = (p * (_nt(do2, vv) - delta + head_col(dlse_t))).astype(BF16)
            dq2 = _nn(ds, k) * scale
            dq_ref[rows, :] = jnp.where(in_h[0], dq2[:ATT_TQ], dq2[ATT_TQ:])
            dk_ref[win, :] += _tn(ds, q2)
            dv_ref[win, :] += _tn(p, do2)
            return carry

        lax.fori_loop(0, s_len // ATT_TQ, tile, 0, unroll=4)

    return pl.pallas_call(
        body, name=f"attn_bwd_g{g}", grid=(4,),
        out_shape=[jax.ShapeDtypeStruct((s_len, 512), F32)] * 3,
        in_specs=[q_spec, k_spec, v_spec, o_spec, o_spec, o_spec, o_spec], out_specs=[o_spec] * 3,
        scratch_shapes=[pltpu.VMEM((3, 2 * ATT_TQ, ATT_TK), F32)],
        compiler_params=_params(("parallel",), VMEM_BIG),
    )(qk, qk, v, o, lse, do, dlse)


def _mix_weights(ls):
    mx = jnp.maximum(jnp.maximum(ls[0], ls[1]), ls[2])
    es = [jnp.exp(x - mx) for x in ls]
    tot = es[0] + es[1] + es[2]
    return [e / tot for e in es]


def _attn_out(os_, lses, z, x, gate, w_out):
    s_len, dm = x.shape
    tm = 256
    wdt = 512
    z, z_block = z

    def body(o0, o1, o2, l0, l1, l2, z_ref, x_ref, g_ref, w_ref, a_ref, y_ref, x1_ref):
        alphas = _mix_weights([l0[...], l1[...], l2[...]])
        y = jnp.zeros((tm, dm), F32)
        for g, o_ref in enumerate((o0, o1, o2)):
            a_g = (o_ref[...] * alphas[g] * _silu(z_ref[:, g * wdt:(g + 1) * wdt])).astype(BF16)
            a_ref[:, g * wdt:(g + 1) * wdt] = a_g
            y = y + _nn(a_g, w_ref[g * wdt:(g + 1) * wdt, :])
        y_ref[...] = y
        x1_ref[...] = x_ref[...] + g_ref[...] * y

    row = lambda c: pl.BlockSpec((tm, c), lambda i: (i, 0))
    return pl.pallas_call(
        body, name="attn_out", grid=(s_len // tm,),
        out_shape=[jax.ShapeDtypeStruct((s_len, 3 * wdt), BF16), jax.ShapeDtypeStruct((s_len, dm), F32),
                   jax.ShapeDtypeStruct((s_len, dm), F32)],
        in_specs=[row(wdt)] * 6 + [pl.BlockSpec((tm, 3 * wdt), lambda i: (i, z_block)), row(dm),
                                   pl.BlockSpec((1, dm), lambda i: (0, 0)), pl.BlockSpec(w_out.shape, lambda i: (0, 0))],
        out_specs=[row(3 * wdt), row(dm), row(dm)],
        compiler_params=_params(("parallel",), VMEM_BIG),
    )(*os_, *lses, z, x, gate, w_out)


def _mix_bwd(dy, w_out, os_, lses, z):
    wdt = 512

    def fn(da, o0, o1, o2, l0, l1, l2, z):
        os_t, ls = [o0, o1, o2], [l0, l1, l2]
        alphas = _mix_weights(ls)
        hi = lax.broadcasted_iota(jnp.int32, (wdt, wdt), 0) // HEAD_DIM
        hj = lax.broadcasted_iota(jnp.int32, (wdt, wdt), 1) // HEAD_DIM
        seg = (hi == hj).astype(F32)
        dos, dal, dzs = [], [], []
        for g in range(3):
            zg = z[:, g * wdt:(g + 1) * wdt]
            sig = jax.nn.sigmoid(zg)
            dag = da[:, g * wdt:(g + 1) * wdt]
            dmix = dag * zg * sig
            dzs.append(dag * os_t[g] * alphas[g] * (sig * (1.0 + zg * (1.0 - sig))))
            dos.append(dmix * alphas[g])
            dal.append(_hnn(dmix * os_t[g], seg))
        mean = alphas[0] * dal[0] + alphas[1] * dal[1] + alphas[2] * dal[2]
        dls = [alphas[g] * (dal[g] - mean) for g in range(3)]
        return dos + dls + [jnp.concatenate(dzs, axis=1)], []

    outs, _ = _matmul_rows("attn_out_dx_mix_bwd", dy, w_out, "nt", 256, dy.shape[1], fn, [*os_, *lses, (z[0], 3 * wdt, z[1])], [],
                           [(wdt, F32)] * 6 + [(3 * wdt, BF16)], [])
    return outs[:3], outs[3:6], outs[6]


def _rot_pack_bwd(dqs, dks, dvs, tabs):
    wdt = 512

    def fn(*args):
        grads, (c, sa, sb) = args[:9], args[9:]
        cols = [_rot_bwd(gq, c, sa, sb) for gq in grads[:6]] + list(grads[6:])
        return [jnp.concatenate(cols, axis=1)], []

    (out,), _ = _rowwise("rot_pack_bwd", fn, [*dqs, *dks, *dvs, *tabs], [], [(9 * wdt, BF16)], [], 256)
    return out


CONV_CB = 128
CONV_R = 256
CONV_PAD = 8


def _conv_taps(buf, base, off, sign):
    return [buf[pl.ds(base + off + sign * j, CONV_R), :] for j in range(CONV_WIDTH)]


def _conv_tap_sum(taps, w):
    acc = None
    for j, t in enumerate(taps):
        term = t * w[j:j + 1, :]
        acc = term if acc is None else acc + term
    return acc


def _conv_fwd(xpre, cw, cb):
    s_len, ch = xpre.shape
    nchunk = s_len // CONV_R

    def body(x_ref, w_ref, b_ref, o_ref, xp):
        zero = jnp.zeros((CONV_PAD, CONV_CB), F32)
        xp[0:CONV_PAD, :] = zero
        xp[s_len + CONV_PAD:s_len + 2 * CONV_PAD, :] = zero

        def fill(ci, carry):
            base = pl.multiple_of(ci * CONV_R, CONV_R)
            xp[pl.ds(base + CONV_PAD, CONV_R), :] = x_ref[pl.ds(base, CONV_R), :]
            return carry

        lax.fori_loop(0, nchunk, fill, 0)
        w = w_ref[...]
        b = b_ref[...]

        def chunk(ci, carry):
            base = pl.multiple_of(ci * CONV_R, CONV_R)
            u = _conv_tap_sum(_conv_taps(xp, base, CONV_PAD - CONV_WIDTH // 2, 1), w) + b
            o_ref[pl.ds(base, CONV_R), :] = _silu(u)
            return carry

        lax.fori_loop(0, nchunk, chunk, 0, unroll=2)

    col = lambda r: pl.BlockSpec((r, CONV_CB), lambda j: (0, j))
    return pl.pallas_call(
        body, name="conv_fwd", grid=(ch // CONV_CB,), out_shape=jax.ShapeDtypeStruct((s_len, ch), F32),
        in_specs=[col(s_len), col(CONV_WIDTH), col(1)], out_specs=col(s_len),
        scratch_shapes=[pltpu.VMEM((s_len + 2 * CONV_PAD, CONV_CB), F32)],
        compiler_params=_params(("parallel",), VMEM_BIG),
    )(xpre, cw, cb)


def _conv_bwd(xpre, da, cw, cb):
    s_len, ch = xpre.shape
    nchunk = s_len // CONV_R
    half = CONV_WIDTH // 2

    def body(x_ref, da_ref, w_ref, b_ref, dx_ref, gw_ref, gb_ref, xp, dcp):
        zero = jnp.zeros((CONV_PAD, CONV_CB), F32)
        for buf in (xp, dcp):
            buf[0:CONV_PAD, :] = zero
            buf[s_len + CONV_PAD:s_len + 2 * CONV_PAD, :] = zero

        def fill(ci, carry):
            base = pl.multiple_of(ci * CONV_R, CONV_R)
            xp[pl.ds(base + CONV_PAD, CONV_R), :] = x_ref[pl.ds(base, CONV_R), :]
            return carry

        lax.fori_loop(0, nchunk, fill, 0)
        w = w_ref[...]
        b = b_ref[...]

        def first(ci, carry):
            base = pl.multiple_of(ci * CONV_R, CONV_R)
            taps = _conv_taps(xp, base, CONV_PAD - half, 1)
            u = _conv_tap_sum(taps, w) + b
            sig = jax.nn.sigmoid(u)
            dc = da_ref[pl.ds(base, CONV_R), :] * (sig * (1.0 + u * (1.0 - sig)))
            dcp[pl.ds(base + CONV_PAD, CONV_R), :] = dc
            gb = carry[0] + jnp.sum(dc, axis=0, keepdims=True)
            gws = [carry[1 + j] + jnp.sum(dc * taps[j], axis=0, keepdims=True) for j in range(CONV_WIDTH)]
            return (gb, *gws)

        z1 = jnp.zeros((1, CONV_CB), F32)
        sums = lax.fori_loop(0, nchunk, first, (z1,) * (1 + CONV_WIDTH), unroll=2)
        gb_ref[...] = sums[0]
        for j in range(CONV_WIDTH):
            gw_ref[j:j + 1, :] = sums[1 + j]

        def second(ci, carry):
            base = pl.multiple_of(ci * CONV_R, CONV_R)
            dx_ref[pl.ds(base, CONV_R), :] = _conv_tap_sum(_conv_taps(dcp, base, CONV_PAD + half, -1), w).astype(dx_ref.dtype)
            return carry

        lax.fori_loop(0, nchunk, second, 0, unroll=2)

    col = lambda r: pl.BlockSpec((r, CONV_CB), lambda j: (0, j))
    return pl.pallas_call(
        body, name="conv_bwd", grid=(ch // CONV_CB,),
        out_shape=[jax.ShapeDtypeStruct((s_len, ch), BF16), jax.ShapeDtypeStruct((CONV_WIDTH, ch), F32),
                   jax.ShapeDtypeStruct((1, ch), F32)],
        in_specs=[col(s_len), col(s_len), col(CONV_WIDTH), col(1)],
        out_specs=[col(s_len), col(CONV_WIDTH), col(1)],
        scratch_shapes=[pltpu.VMEM((s_len + 2 * CONV_PAD, CONV_CB), F32)] * 2,
        compiler_params=_params(("parallel",), VMEM_BIG),
    )(xpre, da, cw, cb)


SSD_GW = 256
SSD_N = 128
SSD_DTW = 128


def _bf16_parts(x, n):
    parts, rest = [], x
    for _ in range(n):
        p = rest.astype(BF16)
        parts.append(p)
        rest = rest - p.astype(F32)
    return parts


@jax.custom_vjp
def _expand(x, e):
    eb = e.astype(BF16)
    return _dg(jnp.concatenate(_bf16_parts(x, 2), axis=1), jnp.concatenate([eb, eb], axis=0), 1, 0)


def _expand_fwd(x, e):
    return _expand(x, e), e


def _expand_bwd(e, g):
    eb = e.astype(BF16)
    return sum(_dg(p, eb, 1, 1) for p in _bf16_parts(g, 2)), jnp.zeros_like(e)


_expand.defvjp(_expand_fwd, _expand_bwd)


@jax.custom_vjp
def _running_sum(tri, x):
    tb = tri.astype(BF16)
    return sum(_dg(tb, p, 1, 0) for p in _bf16_parts(x, 3))


def _running_sum_fwd(tri, x):
    return _running_sum(tri, x), tri


def _running_sum_bwd(tri, g):
    tb = tri.astype(BF16)
    return jnp.zeros_like(tri), sum(_dg(tb, p, 0, 0) for p in _bf16_parts(g, 3))


_running_sum.defvjp(_running_sum_fwd, _running_sum_bwd)


def _pick_col(a, h):
    @jax.custom_vjp
    def pick(a):
        return a[:, h:h + 1]

    pick.defvjp(lambda a: (a[:, h:h + 1], None),
                lambda _, g: (g * (lax.broadcasted_iota(jnp.int32, (1, a.shape[1]), 1) == h).astype(F32),))
    return pick(a)


def _pick_row(a, h):
    @jax.custom_vjp
    def pick(a):
        return a[h:h + 1, :]

    pick.defvjp(lambda a: (a[h:h + 1, :], None),
                lambda _, g: (g * (lax.broadcasted_iota(jnp.int32, (a.shape[0], 1), 0) == h).astype(F32),))
    return pick(a)


def _ssd_mask(dirn):
    ri = lax.broadcasted_iota(jnp.int32, (CHUNK, CHUNK), 0)
    cj = lax.broadcasted_iota(jnp.int32, (CHUNK, CHUNK), 1)
    return (cj <= ri) if dirn == 0 else (cj >= ri)


def _ssd_rowsel(dirn):
    last = CHUNK - 1 if dirn == 0 else 0
    return (lax.broadcasted_iota(jnp.int32, (CHUNK, 1), 0) == last).astype(F32)


def _ssd_chunk_pre(dirn):
    nh = SSD_DTW

    def f(dt, alog):
        da = dt * (-jnp.exp(alog))
        cum = _running_sum(_ssd_mask(dirn).astype(F32), da)
        tot = jnp.sum(cum * _ssd_rowsel(dirn), axis=0, keepdims=True)
        hh = lax.broadcasted_iota(jnp.int32, (nh, SSD_HEADS * HEAD_DIM), 0)
        jj = lax.broadcasted_iota(jnp.int32, (nh, SSD_HEADS * HEAD_DIM), 1)
        expand = (hh == dirn * SSD_HEADS + jj // HEAD_DIM).astype(F32)
        return cum, cum.T, _expand(dt, expand), _expand(jnp.exp(tot - cum), expand), _expand(jnp.exp(cum), expand)

    return f


def _ssd_group_fn(g, dirn, stacked):
    nh = SSD_DTW

    def f(xs, bm, cm, st, cum, cum_t, dt_e, w_e, ce_e):
        mask = _ssd_mask(dirn)
        xdt = xs * dt_e
        cd_e = jnp.sum(ce_e * _ssd_rowsel(dirn), axis=0, keepdims=True)
        cb = _bnt(cm, bm)
        lane_head = lax.broadcasted_iota(jnp.int32, (1, SSD_GW), 1) // HEAD_DIM
        y = _bnn(cm, st) * ce_e
        decayed, inputs = [], []
        for j in range(4):
            hidx = dirn * SSD_HEADS + 4 * g + j
            col, row = _pick_col(cum, hidx), _pick_row(cum_t, hidx)
            dec = cb * jnp.exp(jnp.where(mask, col - row, NEG_BIG))
            head = (lane_head == j).astype(F32)
            if stacked:
                decayed.append(dec)
                inputs.append(xdt * head)
            else:
                y = y + _bnn(dec, xdt) * head
        if stacked:
            y = y + _bnn(jnp.concatenate(decayed, axis=1), jnp.concatenate(inputs, axis=0))
        st_out = st * cd_e + _btn(bm, xdt * w_e)
        return y, st_out

    return f


def _ssd_in_specs(kk):
    ln = CHUNK
    return [pl.BlockSpec((ln, 2048), lambda i: (kk(i), 0)),
            pl.BlockSpec((ln, 1024), lambda i: (kk(i), 2)),
            pl.BlockSpec((ln, 1024), lambda i: (kk(i), 3)),
            pl.BlockSpec((ln, SSD_DTW), lambda i: (kk(i), 0)),
            pl.BlockSpec((1, SSD_DTW), lambda i: (0, 0))]


def _ssd_fwd(xbc, dt, alog, dirn):
    s_len = xbc.shape[0]
    nc = s_len // CHUNK
    kk = (lambda i: i) if dirn == 0 else (lambda i: nc - 1 - i)

    def body(x_ref, b_ref, c_ref, dt_ref, al_ref, y_ref, sts_ref, st):
        @pl.when(pl.program_id(0) == 0)
        def _():
            st[...] = jnp.zeros_like(st)

        sts_ref[0] = st[...]
        cum, cum_t, dt_e, w_e, ce_e = _ssd_chunk_pre(dirn)(dt_ref[...], al_ref[...])
        for g in range(SSD_GROUPS):
            xc = slice(g * SSD_GW, (g + 1) * SSD_GW)
            gc = slice(g * SSD_N, (g + 1) * SSD_N)
            y, st_new = _ssd_group_fn(g, dirn, True)(x_ref[:, xc], b_ref[:, gc], c_ref[:, gc], st[:, xc], cum, cum_t,
                                               dt_e[:, xc], w_e[:, xc], ce_e[:, xc])
            y_ref[:, xc] = y
            st[:, xc] = st_new

    return pl.pallas_call(
        body, name=f"ssd_fwd_d{dirn}", grid=(nc,),
        out_shape=[jax.ShapeDtypeStruct((s_len, 2048), F32), jax.ShapeDtypeStruct((nc, SSD_N, 2048), F32)],
        in_specs=_ssd_in_specs(kk),
        out_specs=[pl.BlockSpec((CHUNK, 2048), lambda i: (kk(i), 0)),
                   pl.BlockSpec((1, SSD_N, 2048), lambda i: (kk(i), 0, 0))],
        scratch_shapes=[pltpu.VMEM((SSD_N, 2048), F32)],
        compiler_params=_params(("arbitrary",), VMEM_BIG),
    )(xbc, xbc, xbc, dt, alog)


def _ssd_bwd(xbc, dt, alog, states, dy, d_e, dirn, prior=None):
    s_len = xbc.shape[0]
    nc = s_len // CHUNK
    kk = (lambda i: nc - 1 - i) if dirn == 0 else (lambda i: i)

    def body(x_ref, b_ref, c_ref, dt_ref, al_ref, sts_ref, dy_ref, de_ref, *rest):
        prior_ref = rest[0] if prior is not None else None
        dx_ref, ddt_ref, dal_ref, dst = rest[prior is not None:]
        plus_prior = (lambda v, cols: v + prior_ref[:, cols]) if prior is not None else (lambda v, cols: v)

        @pl.when(pl.program_id(0) == 0)
        def _():
            dst[...] = jnp.zeros_like(dst)
            dal_ref[...] = jnp.zeros_like(dal_ref)

        (cum, cum_t, dt_e, w_e, ce_e), pre_vjp = jax.vjp(_ssd_chunk_pre(dirn), dt_ref[...], al_ref[...])
        dcum = jnp.zeros_like(cum)
        dcum_t = jnp.zeros_like(cum_t)
        d_dt_e, d_w_e, d_ce_e = [], [], []
        for g in range(SSD_GROUPS):
            xc = slice(g * SSD_GW, (g + 1) * SSD_GW)
            gc = slice(g * SSD_N, (g + 1) * SSD_N)
            _, vjp = jax.vjp(_ssd_group_fn(g, dirn, False), x_ref[:, xc], b_ref[:, gc], c_ref[:, gc], sts_ref[0, :, xc], cum, cum_t,
                             dt_e[:, xc], w_e[:, xc], ce_e[:, xc])
            dyg = dy_ref[:, xc]
            dxs, dbm, dcm, dst_g, dcum_g, dcum_t_g, ddte_g, dwe_g, dcee_g = vjp((dyg, dst[:, xc]))
            if dirn == 0:
                dxs = dxs + dyg * de_ref[:, xc]
            bc, cc = slice(2048 + g * SSD_N, 2048 + (g + 1) * SSD_N), slice(3072 + g * SSD_N, 3072 + (g + 1) * SSD_N)
            dx_ref[:, xc] = plus_prior(dxs, xc)
            dx_ref[:, bc] = plus_prior(dbm, bc)
            dx_ref[:, cc] = plus_prior(dcm, cc)
            dst[:, xc] = dst_g
            dcum = dcum + dcum_g
            dcum_t = dcum_t + dcum_t_g
            d_dt_e.append(ddte_g)
            d_w_e.append(dwe_g)
            d_ce_e.append(dcee_g)
        ddt, dal = pre_vjp((dcum, dcum_t, jnp.concatenate(d_dt_e, axis=1), jnp.concatenate(d_w_e, axis=1),
                            jnp.concatenate(d_ce_e, axis=1)))
        ddt_ref[...] = ddt
        dal_ref[...] += dal

    return pl.pallas_call(
        body, name=f"ssd_bwd_d{dirn}", grid=(nc,),
        out_shape=[jax.ShapeDtypeStruct((s_len, 4096), F32), jax.ShapeDtypeStruct((s_len, SSD_DTW), F32),
                   jax.ShapeDtypeStruct((1, SSD_DTW), F32)],
        in_specs=_ssd_in_specs(kk) + [pl.BlockSpec((1, SSD_N, 2048), lambda i: (kk(i), 0, 0)),
                                      pl.BlockSpec((CHUNK, 2048), lambda i: (kk(i), 0)),
                                      pl.BlockSpec((1, 2048), lambda i: (0, 0))]
        + ([pl.BlockSpec((CHUNK, 4096), lambda i: (kk(i), 0))] if prior is not None else []),
        out_specs=[pl.BlockSpec((CHUNK, 4096), lambda i: (kk(i), 0)),
                   pl.BlockSpec((CHUNK, SSD_DTW), lambda i: (kk(i), 0)),
                   pl.BlockSpec((1, SSD_DTW), lambda i: (0, 0))],
        scratch_shapes=[pltpu.VMEM((SSD_N, 2048), F32)],
        compiler_params=_params(("arbitrary",), VMEM_BIG),
    )(xbc, xbc, xbc, dt, alog, states, dy, d_e, *([prior] if prior is not None else []))


def _gate_norm_fn(yf, yb, xs, z, d_e, nw):
    yg = (yf + yb + xs * d_e) * _silu(z)
    return yg * lax.rsqrt(jnp.mean(yg * yg, axis=-1, keepdims=True) + NORM_EPS) * nw


def _gate_norm_bwd(dy, w_out, yf, yb, xbc, z, d_e, nw):
    def fn(du, yf, yb, xs, z, d_e, nw):
        sig = jax.nn.sigmoid(z)
        gate = z * sig
        ysum = yf + yb + xs * d_e
        yg = ysum * gate
        r = lax.rsqrt(jnp.mean(yg * yg, axis=-1, keepdims=True) + NORM_EPS)
        t = du * nw
        dyg = t * r - yg * (jnp.mean(t * yg, axis=-1, keepdims=True) * (r * r * r))
        dys = dyg * gate
        dz = dyg * ysum * (sig * (1.0 + z * (1.0 - sig)))
        dnw = jnp.sum(du * yg * r, axis=0, keepdims=True)
        dde = jnp.sum(dys * xs, axis=0, keepdims=True)
        hh = lax.broadcasted_iota(jnp.int32, (2048, SSD_HEADS), 0) // HEAD_DIM
        jj = lax.broadcasted_iota(jnp.int32, (2048, SSD_HEADS), 1)
        return [dys, dz], [dnw, _hnn(jnp.broadcast_to(dde, (8, 2048)), (hh == jj).astype(F32))[0:1]]

    (dys, dz), (g_nw, g_d) = _matmul_rows("ssd_out_dx_gate_norm_bwd", dy, w_out, "nt", 256, dy.shape[1], fn,
                                          [yf, yb, (xbc, 2048, 0), z], [d_e, nw], [(2048, F32), (2048, BF16)],
                                          [(1, 2048), (1, SSD_HEADS)])
    return dys, dz, g_nw, g_d


def _ssd_tail_loss(yf, yb, xbc, z, d_e, snw, w_out, x1, tgt, gate, fnw):
    dm = x1.shape[1]
    si = yf.shape[1]

    def make_u(yf, yb, xs, z, x1, tgt, d_e, snw, gate, fnw):
        return _gate_norm_fn(yf, yb, xs, z, d_e, snw).astype(BF16)

    def fn(y1, u, yf, yb, xs, z, x1, tgt, d_e, snw, gate, fnw):
        def head(x2, fnw):
            yf = (x2 * lax.rsqrt(jnp.mean(x2 * x2, axis=-1, keepdims=True) + NORM_EPS)) * fnw
            err = yf - tgt
            return 0.5 * jnp.sum(jnp.mean(err * err, axis=-1, keepdims=True), axis=0, keepdims=True)

        x2 = x1 + gate * y1
        loss, vjp = jax.vjp(head, x2, fnw)
        dx2, dfnw = vjp(jnp.ones((1, 1), F32))
        return [u, dx2, gate * dx2], [dfnw, jnp.sum(dx2 * y1, axis=0, keepdims=True), jnp.broadcast_to(loss, (1, 128))]

    (u, dx2, dy1), (g_fnw, dgate, loss) = _matmul_rows(
        "ssd_out_loss", make_u, w_out, "nn", 256, si, fn, [yf, yb, (xbc, si, 0), z, x1, tgt], [d_e, snw, gate, fnw],
        [(si, BF16), (dm, F32), (dm, BF16)], [(1, dm), (1, dm), (1, 128)])
    return u, dx2, dy1, g_fnw, dgate, loss


def _softplus_fwd(dt_raw, bias):
    (dt,), _ = _rowwise("dt_softplus", lambda r, b: ([jax.nn.softplus(r + b)], []), [dt_raw], [bias],
                        [(dt_raw.shape[1], F32)], [], 512)
    return dt


def _softplus_bwd(ddt_f, ddt_b, dt_raw, bias):
    def fn(df, db, r, b):
        g = (df + db) * jax.nn.sigmoid(r + b)
        return [g], [jnp.sum(g, axis=0, keepdims=True)]

    w = dt_raw.shape[1]
    (g,), (gb,) = _rowwise("dt_softplus_bwd", fn, [ddt_f, ddt_b, dt_raw], [bias], [(w, BF16)], [(1, w)], 512)
    return g, gb


def _whole(a):
    nd = len(a.shape)
    return pl.BlockSpec(a.shape, lambda *_: (0,) * nd)


def _mod_part(c_all, mod_w):
    nl, _, ncol = mod_w.shape
    nb = c_all.shape[0]

    def body(c_ref, w_ref, o_ref):
        cond = _silu(c_ref[...])
        for i in range(nl):
            o_ref[i * nb:(i + 1) * nb, :] = _nn(cond, w_ref[i])

    return pl.pallas_call(body, name="mod_part", out_shape=jax.ShapeDtypeStruct((nl * nb, ncol), F32),
                          compiler_params=_params(None, VMEM_BIG))(c_all, mod_w)


def _mod_finish(mod_nb, mod_b, norm_w, tokens):
    nl, dm = norm_w.shape

    def body(a_ref, b_ref, nw_ref, *rest):
        tok_refs, o_refs = rest[:len(tokens)], rest[len(tokens):]
        tok = sum(t[0:1, 0:1] for t in tok_refs)
        for i in range(nl):
            for k in range(3):
                cols = slice(k * dm, (k + 1) * dm)
                o_refs[4 * i + k][...] = a_ref[i:i + 1, cols] + b_ref[i:i + 1, cols]
            o_refs[4 * i + 3][...] = nw_ref[i:i + 1, :] + tok

    rows = pl.pallas_call(body, name="mod_finish", out_shape=[jax.ShapeDtypeStruct((1, dm), F32)] * (4 * nl))(
        mod_nb, mod_b, norm_w, *tokens)
    return [rows[4 * i:4 * i + 4] for i in range(nl)]


def _mod_grad(c_all, dmod_sh):
    nl, nb, ncol = dmod_sh.shape
    dm = c_all.shape[1]

    def body(c_ref, d_ref, o_ref):
        cond = _silu(c_ref[...])
        for i in range(nl):
            o_ref[i] = _tn(cond, d_ref[i])

    return pl.pallas_call(body, name="mod_grad", out_shape=jax.ShapeDtypeStruct((nl, dm, ncol), F32),
                          compiler_params=_params(None, VMEM_BIG))(c_all, dmod_sh)


PACK_ROWS = 16
PACK_COLS = 1024


def _pack_small(rows, b64, a64s, d32, extra):
    nr, na = len(rows), len(a64s)

    def body(*refs):
        o_ref = refs[-1]
        o_ref[...] = jnp.zeros_like(o_ref)
        for i in range(nr):
            o_ref[i:i + 1, :] = refs[i][...]
        b_ref, a_refs, d_ref, e_ref = refs[nr], refs[nr + 1:nr + 1 + na], refs[nr + 1 + na], refs[nr + 2 + na]
        o_ref[nr:nr + 1, 0:64] = b_ref[:, 0:64]
        o_ref[nr:nr + 1, 64:128] = sum(a[:, 0:64] for a in a_refs)
        o_ref[nr:nr + 1, 128:160] = d_ref[...]
        o_ref[nr:nr + 1, 256:384] = e_ref[...]

    return pl.pallas_call(body, name="pack_small", out_shape=jax.ShapeDtypeStruct((PACK_ROWS, PACK_COLS), F32))(
        *rows, b64, *a64s, d32, extra)


def _pack_ssd_small(cw, cb, nw):
    def body(cw_ref, cb_ref, nw_ref, o_ref):
        o_ref[...] = jnp.zeros_like(o_ref)
        o_ref[0:5, :] = cw_ref[...]
        o_ref[5:6, :] = cb_ref[...]
        o_ref[6:7, 0:256] = nw_ref[...]

    return pl.pallas_call(body, name="pack_ssd_small", out_shape=jax.ShapeDtypeStruct((8, 512), F32))(cw, cb, nw)


def _sum_parts(p_ref):
    g = p_ref[0].astype(F32)
    for s in range(1, p_ref.shape[0]):
        g = g + p_ref[s].astype(F32)
    return g


def _adam_update(w, g, m, v):
    m2 = ADAM_B1 * m + (1.0 - ADAM_B1) * g
    v2 = ADAM_B2 * v + (1.0 - ADAM_B2) * (g * g)
    m_hat = m2 / (1.0 - ADAM_B1 ** ADAM_STEP)
    v_hat = v2 / (1.0 - ADAM_B2 ** ADAM_STEP)
    return -ADAM_LR * (m_hat / (jnp.sqrt(v_hat) + ADAM_EPS) + ADAM_WD * w), m2, v2


def _adamw_windows(name, parts, params, windows, extra=None):
    n = len(params)

    def body(p_ref, *rest):
        ins, outs = rest[:3 * n], rest[3 * n:]
        g = _sum_parts(p_ref)
        for pi, rows, cols, idx in windows:
            w_ref, m_ref, v_ref = ins[3 * pi:3 * pi + 3]
            gw = g[rows, cols]
            dw, m2, v2 = _adam_update(w_ref[idx], gw, m_ref[idx], v_ref[idx])
            for o_ref, val in zip(outs[4 * pi:4 * pi + 4], (gw, dw, m2, v2), strict=True):
                o_ref[idx] = val
        if extra is not None:
            outs[4 * n][...] = g[extra[0], extra[1]]

    out_shape = [jax.ShapeDtypeStruct(w.shape, F32) for (w, _, _) in params for _ in range(4)]
    if extra is not None:
        out_shape.append(jax.ShapeDtypeStruct((extra[0].stop - extra[0].start, extra[1].stop - extra[1].start), F32))
    res = pl.pallas_call(body, name=name, out_shape=out_shape)(parts, *[a for p in params for a in p])
    return [res[4 * i:4 * i + 4] for i in range(n)] + ([res[4 * n]] if extra is not None else [])


def _adamw(name, w, parts, m, v, tr, tc=None):
    r_, c_ = w.shape
    p_ = parts.shape[0]
    tr = min(tr, r_)
    tc = c_ if tc is None else tc
    assert r_ % tr == 0 and c_ % tc == 0

    def body(w_ref, p_ref, m_ref, v_ref, g_ref, d_ref, m2_ref, v2_ref):
        g = _sum_parts(p_ref)
        g_ref[...] = g
        d_ref[...], m2_ref[...], v2_ref[...] = _adam_update(w_ref[...], g, m_ref[...], v_ref[...])

    blk = pl.BlockSpec((tr, tc), lambda i, j: (i, j))
    return pl.pallas_call(
        body, name=name, grid=(r_ // tr, c_ // tc), out_shape=[jax.ShapeDtypeStruct((r_, c_), F32)] * 4,
        in_specs=[blk, pl.BlockSpec((p_, tr, tc), lambda i, j: (0, i, j)), blk, blk], out_specs=[blk] * 4,
        compiler_params=_params(("parallel", "parallel"), VMEM_BIG),
    )(w, parts, m, v)


def _dev_index(p):
    return 4 * p[0] + 2 * p[1] + p[2]


def _all_gather(name, xs):
    n = len(xs)
    hbm = pl.BlockSpec(memory_space=pl.ANY)

    def body(*refs):
        x_refs, o_refs = refs[:n], refs[n:2 * n]
        send_sems, recv_sems, local_sems = refs[2 * n:]
        x, y, c = lax.axis_index("x"), lax.axis_index("y"), lax.axis_index("c")
        me, sibling = (x, y, c), (x, y, 1 - c)
        chips = [(1 - x, y), (x, 1 - y), (1 - x, 1 - y)]

        def copy(a, k, block, to, src=None):
            dst = o_refs[a].at[_dev_index(block)]
            return pltpu.make_async_remote_copy(
                src_ref=dst if src is None else src, dst_ref=dst, send_sem=send_sems.at[a, k],
                recv_sem=recv_sems.at[a, k], device_id=to, device_id_type=MESH)

        mine = [pltpu.make_async_copy(x_refs[a], o_refs[a].at[_dev_index(me)], local_sems.at[a]) for a in range(n)]
        for cp in mine:
            cp.start()
        first = []
        for a in range(n):
            first.append(copy(a, 0, me, sibling, src=x_refs[a]))
            first += [copy(a, 1 + j, me, (*chip, c), src=x_refs[a]) for j, chip in enumerate(chips)]
        for cp in first:
            cp.start()
        passed = []
        for j, chip in enumerate(chips):
            for a in range(n):
                copy(a, 1 + j, (*chip, c), me).wait_recv()
                cp = copy(a, 4 + j, (*chip, c), sibling)
                cp.start()
                passed.append(cp)
        for a in range(n):
            copy(a, 0, sibling, me).wait_recv()
            for j, chip in enumerate(chips):
                copy(a, 4 + j, (*chip, 1 - c), me).wait_recv()
        for cp in first + passed:
            cp.wait_send()
        for cp in mine:
            cp.wait()

    return pl.pallas_call(
        body, name=name, out_shape=[jax.ShapeDtypeStruct((NDEV, *x.shape), x.dtype) for x in xs],
        in_specs=[hbm] * n, out_specs=[hbm] * n,
        scratch_shapes=[pltpu.SemaphoreType.DMA((n, 7)), pltpu.SemaphoreType.DMA((n, 7)), pltpu.SemaphoreType.DMA((n,))],
    )(*xs)


_HBM = pl.BlockSpec(memory_space=pltpu.HBM)
_SEM = pl.BlockSpec(memory_space=pltpu.SEMAPHORE)
_EFFECT = pltpu.SideEffectType.DATAFLOW_SIDE_EFFECTING


def _mesh_position():
    return lax.axis_index("x"), lax.axis_index("y"), lax.axis_index("c")


def _peers(me):
    return [(k, tuple(1 - v if (k >> b) & 1 else v for v, b in zip(me, (2, 1, 0)))) for k in range(1, NDEV)]


EXCHANGE_COPIES = {"gather": NDEV - 1, "scatter": NDEV - 1, "pair": 4, "chips": 3}
NCHIP = NDEV // 2


def _landing_zones(name, xs, mode):
    x_, y_, c_ = _mesh_position()
    mine = (2 * x_ + y_ if mode == "chips" else _dev_index((x_, y_, c_))).astype(jnp.int32).reshape(1)
    lands = []
    for a, x in enumerate(xs):
        rows, cols = x.shape[-2:]
        if mode == "pair":
            lands.append(lax.empty((NCHIP, rows, cols), x.dtype))
            continue
        tr = 256 if rows % 256 == 0 else rows

        def body(me_ref, x_ref, o_ref):
            o_ref[...] = x_ref[...]

        if mode == "gather":
            in_spec = pl.BlockSpec((tr, cols), lambda i, me_ref: (i, 0))
        else:
            in_spec = pl.BlockSpec((None, tr, cols), lambda i, me_ref: (me_ref[0], i, 0))
        lands.append(pl.pallas_call(
            body, name=f"{name}_{a}",
            out_shape=jax.ShapeDtypeStruct((NCHIP if mode == "chips" else NDEV, rows, cols), x.dtype),
            grid_spec=pltpu.PrefetchScalarGridSpec(
                num_scalar_prefetch=1, grid=(rows // tr,), in_specs=[in_spec],
                out_specs=pl.BlockSpec((None, tr, cols), lambda i, me_ref: (me_ref[0], i, 0))),
            compiler_params=_params(("arbitrary",)),
        )(mine, x))
    return lands


def _exchange_copies(x_refs, land_refs, send_sems, recv_sems, mode):
    x_, y_, c_ = me = _mesh_position()
    per_array = EXCHANGE_COPIES[mode]
    out = []

    def add(a, k, src, dst, peer):
        sem = a * per_array + k
        out.append(pltpu.make_async_remote_copy(src_ref=src, dst_ref=dst, send_sem=send_sems.at[sem], recv_sem=recv_sems.at[sem],
                                                device_id=peer, device_id_type=MESH))

    for a, (x_ref, land_ref) in enumerate(zip(x_refs, land_refs)):
        if mode in ("gather", "scatter"):
            for k, peer in _peers(me):
                add(a, k - 1, x_ref.at[_dev_index(peer)] if mode == "scatter" else x_ref, land_ref.at[_dev_index(me)], peer)
        elif mode == "pair":
            for chip in range(NCHIP):
                add(a, chip, x_ref.at[2 * chip + 1 - c_], land_ref.at[chip], (x_, y_, 1 - c_))
        else:
            for k in range(1, NCHIP):
                px, py = (1 - x_ if k & 2 else x_), (1 - y_ if k & 1 else y_)
                add(a, k - 1, x_ref.at[2 * px + py], land_ref.at[2 * x_ + y_], (px, py, c_))
    return out


def _exchange_start(name, xs, lands, mode, dep):
    n = len(xs)

    def body(*refs):
        x_refs, land_refs = refs[:n], refs[n:2 * n]
        send_sems, recv_sems = refs[2 * n + 1], refs[2 * n + 2]
        token = refs[-1]
        for cp in _exchange_copies(x_refs, land_refs, send_sems, recv_sems, mode):
            cp.start()
        token[...] = jnp.zeros_like(token)

    sems = pltpu.SemaphoreType.DMA((n * EXCHANGE_COPIES[mode],))
    res = pl.pallas_call(
        body, name=name,
        out_shape=(sems, sems, *[pltpu.HBM(a.shape, a.dtype) for a in (*xs, *lands)], jax.ShapeDtypeStruct((8, 128), F32)),
        in_specs=[_HBM] * (2 * n) + [pl.BlockSpec(memory_space=pl.ANY)],
        out_specs=(_SEM, _SEM, *[_HBM] * (2 * n), pl.BlockSpec(memory_space=pltpu.VMEM)),
        input_output_aliases={i: 2 + i for i in range(2 * n)},
        compiler_params=pltpu.CompilerParams(has_side_effects=_EFFECT),
    )(*[pltpu.with_memory_space_constraint(a, pltpu.HBM) for a in (*xs, *lands)], dep)
    return res[:-1], res[-1]


def _exchange_wait(name, handles, mode, after):
    send_sems, recv_sems = handles[0], handles[1]
    bufs = handles[2:]
    n = len(bufs) // 2

    def body(*refs):
        x_refs, land_refs = refs[:n], refs[n:2 * n]
        s_sems, r_sems = refs[2 * n], refs[2 * n + 1]
        for cp in _exchange_copies(x_refs, land_refs, s_sems, r_sems, mode):
            cp.wait_send()
            cp.wait_recv()

    res = pl.pallas_call(
        body, name=name, out_shape=tuple(pltpu.HBM(a.shape, a.dtype) for a in bufs),
        in_specs=[_HBM] * (2 * n) + [_SEM, _SEM, pl.BlockSpec(memory_space=pl.ANY)], out_specs=tuple([_HBM] * (2 * n)),
        input_output_aliases={i: i for i in range(2 * n)},
        compiler_params=pltpu.CompilerParams(has_side_effects=_EFFECT),
    )(*bufs, send_sems, recv_sems, after)
    return res[n:]


def _pair_sum(name, x, from_sibling):
    _, rows, cols = x.shape
    tr = 256 if rows % 256 == 0 else rows
    core = lax.axis_index("c").astype(jnp.int32).reshape(1)

    def body(c_ref, x_ref, s_ref, o_ref):
        o_ref[...] = (x_ref[...].astype(F32) + s_ref[...].astype(F32)).astype(o_ref.dtype)

    return pl.pallas_call(
        body, name=name, out_shape=jax.ShapeDtypeStruct((NCHIP, rows, cols), x.dtype),
        grid_spec=pltpu.PrefetchScalarGridSpec(
            num_scalar_prefetch=1, grid=(NCHIP, rows // tr),
            in_specs=[pl.BlockSpec((None, tr, cols), lambda j, i, c_ref: (2 * j + c_ref[0], i, 0)),
                      pl.BlockSpec((None, tr, cols), lambda j, i, c_ref: (j, i, 0))],
            out_specs=pl.BlockSpec((None, tr, cols), lambda j, i, c_ref: (j, i, 0))),
        compiler_params=_params(("parallel", "parallel")),
    )(core, x, from_sibling)


def kernel(x, c, positions, norm_w, mod_w, mod_b, attn_w_in, attn_w_out, ssd_w_in, ssd_conv_w, ssd_conv_b, ssd_dt_bias, ssd_a_log, ssd_d, ssd_norm_w, ssd_w_out, final_norm_w, loss_target, m_norm_w, m_mod_w, m_mod_b, m_attn_w_in, m_attn_w_out, m_ssd_w_in, m_ssd_conv_w, m_ssd_conv_b, m_ssd_dt_bias, m_ssd_a_log, m_ssd_d, m_ssd_norm_w, m_ssd_w_out, m_final_norm_w, v_norm_w, v_mod_w, v_mod_b, v_attn_w_in, v_attn_w_out, v_ssd_w_in, v_ssd_conv_w, v_ssd_conv_b, v_ssd_dt_bias, v_ssd_a_log, v_ssd_d, v_ssd_norm_w, v_ssd_w_out, v_final_norm_w):
    s_len, dm = x.shape[1], x.shape[2]
    me = 4 * lax.axis_index("x") + 2 * lax.axis_index("y") + lax.axis_index("c")
    x0 = x.reshape(s_len, dm)
    tgt = loss_target.reshape(s_len, dm)
    aw = 3 * 512
    si = 2 * dm
    sxbc = 2 * si
    n_ssd_in = ssd_w_in.shape[2] * NDEV

    g_ai, c_all = _all_gather("gather_attn_w_in", [attn_w_in[0].astype(BF16), c])
    w_ai = g_ai.transpose(1, 0, 2).reshape(dm, 4 * aw)
    c_all = c_all.reshape(NDEV, dm)

    part = _mod_part(c_all, mod_w)
    (part_all,) = _all_gather("gather_mod", [part])
    mod_nb = jnp.stack([lax.dynamic_index_in_dim(part_all, i * NDEV + me, axis=1, keepdims=False).reshape(3 * dm)
                        for i in range(2)])

    ssd_small = _pack_ssd_small(ssd_conv_w[0], ssd_conv_b, ssd_norm_w)
    ao_shard = [attn_w_out[0].astype(BF16)]
    ao_handles, ao_token = _exchange_start("w_out_start", ao_shard, _landing_zones("w_out_place", ao_shard, "gather"), "gather",
                                           part_all)
    late_shards = [ssd_w_in[0].T.astype(BF16), ssd_w_out[0].astype(BF16), ssd_small]
    w_handles, w_token = _exchange_start("weights_start", late_shards, _landing_zones("weights_place", late_shards, "gather"),
                                         "gather", ao_token)
    (shift0, scale0, gate0, nw0), (shift1, scale1, gate1, nw1) = _mod_finish(mod_nb, mod_b, norm_w, [ao_token, w_token])
    shift, scale, gate, nw = [shift0, shift1], [scale0, scale1], [gate0, gate1], [nw0, nw1]

    hn0 = _norm_mod_fwd("norm0", x0, nw[0], scale[0], shift[0])
    inv_freq = ROPE_THETA ** (-jnp.arange(0, ROT_DIM, 2, dtype=F32) / ROT_DIM)
    lane = jnp.arange(128) % HEAD_DIM
    inv_row = jnp.where(lane < ROT_DIM, inv_freq[lane % (ROT_DIM // 2)], 0.0).reshape(1, 128).astype(F32)
    tabs = _rope_tables(positions.reshape(s_len, 1), inv_row)
    qk = _matmul("proj_qk", hn0, w_ai, "nn", F32, MM_T, MM_T, dm, epilogue=_rot_fwd, mrows=tabs, n_out=2 * aw)
    v = _matmul("proj_vz", hn0, w_ai, "nn", F32, MM_T, MM_T, dm, b_noff=2 * aw, n_out=2 * aw)
    z0 = (v, 1)
    att = [_attn_fwd(g, qk, v) for g in range(3)]
    os_, lses = [a[0] for a in att], [a[1] for a in att]
    (g_ao,) = _exchange_wait("w_out_wait", ao_handles, "gather", lses[2])
    a0, y0, x1 = _attn_out(os_, lses, z0, x0, gate[0], g_ao.reshape(aw, dm))

    hn1 = _norm_mod_fwd("norm1", x1, nw[1], scale[1], shift[1])
    g_si, g_so, g_small = _exchange_wait("weights_wait", w_handles, "gather", hn1)
    w_ao = g_ao.reshape(aw, dm)
    w_si_t = g_si.reshape(n_ssd_in, dm)
    w_so = g_so.reshape(si, dm)
    conv_w = g_small[:, 0:CONV_WIDTH, :].transpose(1, 0, 2).reshape(CONV_WIDTH, sxbc)
    conv_b = g_small[:, 5, :].reshape(1, sxbc)
    snw = g_small[:, 6, 0:si // NDEV].reshape(1, si)
    ndt = 2 * SSD_HEADS
    z1 = _matmul("ssd_proj_z", hn1, w_si_t, "nt", F32, MM_T, MM_T, dm, n_out=si)
    xpre = _matmul("ssd_proj_xbc", hn1, w_si_t, "nt", F32, MM_T, MM_T, dm, b_noff=si, n_out=sxbc)
    dt_raw = _matmul("ssd_proj_dt", hn1, w_si_t, "nt", F32, MM_T, ndt, dm, b_noff=si + sxbc, n_out=ndt)
    xbc = _conv_fwd(xpre, conv_w, conv_b)
    widen = lambda a: jnp.pad(a, ((0, 0), (0, SSD_DTW - ndt)))
    dt_raw = widen(dt_raw)
    dt_bias = widen(ssd_dt_bias.reshape(1, ndt))
    alog = widen(ssd_a_log.reshape(1, ndt))
    dt = _softplus_fwd(dt_raw, dt_bias)
    y_f, st_f = _ssd_fwd(xbc, dt, alog, 0)
    y_b, st_b = _ssd_fwd(xbc, dt, alog, 1)
    d_e = jnp.repeat(ssd_d.reshape(SSD_HEADS), HEAD_DIM).reshape(1, si)

    fnw = final_norm_w.reshape(1, dm)
    u, dx2, dy1, g_fnw, dgate1, loss_part = _ssd_tail_loss(y_f, y_b, xbc, z1, d_e, snw, w_so, x1, tgt, gate[1], fnw)
    gw_so = _matmul("ssd_out_dw", u, dy1, "tn", BF16, MM_T, MM_T, MM_T)
    dys, dz1, g_snw, g_d = _gate_norm_bwd(dy1, w_so, y_f, y_b, xbc, z1, d_e, snw)
    dxbc_f, ddt_f, dalog_f = _ssd_bwd(xbc, dt, alog, st_f, dys, d_e, 0)
    dxbc, ddt_b, dalog_b = _ssd_bwd(xbc, dt, alog, st_b, dys, d_e, 1, prior=dxbc_f)
    dpre, g_cw, g_cb = _conv_bwd(xpre, dxbc, conv_w, conv_b)
    ddt_raw, g_dtb = _softplus_bwd(ddt_f, ddt_b, dt_raw, dt_bias)
    ddt_raw = ddt_raw[:, :ndt]
    dhn1 = [_matmul("ssd_proj_z_dx", dz1, w_si_t, "nn", F32, MM_T, MM_T, MM_T),
            _matmul("ssd_proj_xbc_dx", dpre, w_si_t, "nn", F32, MM_T, MM_T, MM_T, b_koff=si)]
    gw_si_t = _matmul("ssd_proj_z_dw", dz1, hn1, "tn", BF16, MM_T, MM_T, MM_T, dest=(n_ssd_in, 0, None))
    gw_si_t = _matmul("ssd_proj_xbc_dw", dpre, hn1, "tn", BF16, MM_T, MM_T, MM_T, dest=(n_ssd_in, si, gw_si_t))
    gw_si_t = _matmul("ssd_proj_dt_dw", ddt_raw, hn1, "tn", BF16, ndt, MM_T, MM_T, dest=(n_ssd_in, si + sxbc, gw_si_t))

    l1_grads = [gw_so.reshape(NDEV, si // NDEV, dm), gw_si_t.reshape(NDEV, n_ssd_in // NDEV, dm),
                _pack_ssd_small_blocks(g_cw, g_cb, g_snw)]
    l1_handles, l1_token = _exchange_start("l1_grads_start", l1_grads, _landing_zones("l1_grads_place", l1_grads, "scatter"),
                                           "scatter", dhn1[1])
    dx1, dy0, g_nw1, dsc1, dsh1, dgate0 = _norm_mod_bwd(
        "ssd_proj_dt_dx_norm1_bwd", (ddt_raw, w_si_t, "nn", ndt, dict(b_koff=si + sxbc)), x1, dhn1, dx2,
        nw[1], scale[1], shift[1], prev=(y0, gate[0] + l1_token[0:1, 0:1]))

    gw_ao = _matmul("attn_out_dw", a0, dy0, "tn", BF16, aw // 2, MM_T, MM_T)
    dos, dls, dz0 = _mix_bwd(dy0, w_ao, os_, lses, z0)
    datt = [_attn_bwd(g, qk, v, os_[g], lses[g], dos[g], dls[g]) for g in range(3)]
    dqkv = _rot_pack_bwd([t[0] for t in datt], [t[1] for t in datt], [t[2] for t in datt], tabs)
    wcol = attn_w_in.shape[2]
    gw_ai = _matmul("proj_qkv_dw", hn0, dqkv, "tn", BF16, MM_T, wcol, MM_T, out_blocks=3 * aw // wcol, dest=(NDEV, 0, None))
    gw_ai = _matmul("proj_z_dw", hn0, dz0, "tn", BF16, MM_T, wcol, MM_T, out_blocks=aw // wcol,
                    dest=(NDEV, 3 * aw // wcol, gw_ai))
    after_start = lambda acc, t: acc + t
    zero_row = lambda token: jnp.tile(token[0:1], (1, dm // 128))
    l0_grads = [gw_ai, gw_ao.reshape(NDEV, aw // NDEV, dm)]
    pair_handles, pair_token = _exchange_start("l0_pair_start", l0_grads, _landing_zones("l0_pair_place", l0_grads, "pair"),
                                               "pair", dqkv)
    dhn0_z = _matmul("proj_z_dx", dz0, w_ai, "nt", F32, MM_T, MM_T, aw, b_koff=3 * aw, n_out=dm, epilogue=after_start,
                     ncols=(zero_row(pair_token),))
    from_sibling = _exchange_wait("l0_pair_wait", pair_handles, "pair", dhn0_z)
    chip_sums = [_pair_sum(f"l0_pair_sum_{a}", g, s) for a, (g, s) in enumerate(zip(l0_grads, from_sibling))]
    l0_handles, l0_token = _exchange_start("l0_grads_start", chip_sums, _landing_zones("l0_grads_place", chip_sums, "chips"),
                                           "chips", dhn0_z)
    dx0, g_nw0, dsc0, dsh0 = _norm_mod_bwd(
        "proj_qkv_dx_norm0_bwd", (dqkv, w_ai, "nt", aw, dict(n_out=dm)), x0, [dhn0_z], dx1,
        nw[0], scale[0], shift[0] + zero_row(l0_token))

    small_g = [_pack_small([dsh0, dsc0, dgate0, dsh1, dsc1, dgate1, g_nw0, g_nw1, g_fnw], g_dtb, [dalog_f, dalog_b], g_d, loss_part)]
    sm_handles, sm_token = _exchange_start("small_grads_start", small_g, _landing_zones("small_grads_place", small_g, "gather"),
                                           "gather", dx0)

    whole = (slice(None), slice(None))
    r_so, r_si, r_small = _exchange_wait("l1_grads_wait", l1_handles, "scatter", sm_token)
    si_out = [o.T for o in _adamw("adamw_ssd_w_in", ssd_w_in[0].T, r_si, m_ssd_w_in[0].T, v_ssd_w_in[0].T, n_ssd_in // NDEV, 256)]
    so_out = _adamw("adamw_ssd_w_out", ssd_w_out[0], r_so, m_ssd_w_out[0], v_ssd_w_out[0], 256)
    cw_cols = ssd_conv_w.shape[2]
    cw_out, cb_out, snw_out = _adamw_windows(
        "adamw_ssd_small", r_small,
        [(ssd_conv_w, m_ssd_conv_w, v_ssd_conv_w), (ssd_conv_b, m_ssd_conv_b, v_ssd_conv_b),
         (ssd_norm_w, m_ssd_norm_w, v_ssd_norm_w)],
        [(0, slice(0, CONV_WIDTH), slice(0, cw_cols), (0, slice(None), slice(None))),
         (1, slice(5, 6), slice(0, cw_cols), whole), (2, slice(6, 7), slice(0, si // NDEV), whole)])
    r_ai, r_ao = _exchange_wait("l0_grads_wait", l0_handles, "chips", so_out[0])
    ai_out = _adamw("adamw_attn_w_in", attn_w_in[0], r_ai, m_attn_w_in[0], v_attn_w_in[0], 256)
    ao_out = _adamw("adamw_attn_w_out", attn_w_out[0], r_ao, m_attn_w_out[0], v_attn_w_out[0], 192)

    (small_all,) = _exchange_wait("small_grads_wait", sm_handles, "gather", ai_out[0])
    full = slice(0, PACK_COLS)
    nhd = SSD_HEADS
    windows = [(0, slice(3 * i + k, 3 * i + k + 1), full, (slice(i, i + 1), slice(k * dm, (k + 1) * dm)))
               for i in range(2) for k in range(3)]
    windows += [(1, slice(6 + i, 7 + i), full, (slice(i, i + 1), slice(None))) for i in range(2)]
    windows += [(2, slice(8, 9), full, whole)]
    windows += [(3 + q, slice(9, 10), slice(2 * nhd * q + nhd * j, 2 * nhd * q + nhd * (j + 1)), (0, slice(j, j + 1), slice(None)))
                for q in range(2) for j in range(2)]
    windows += [(5, slice(9, 10), slice(4 * nhd, 5 * nhd), whole)]
    as_row = lambda a: a.reshape(1, dm)
    mb_out, nw_out, fnw_out, dtb_out, alog_out, d_out, loss = _adamw_windows(
        "adamw_small", small_all,
        [(mod_b, m_mod_b, v_mod_b), (norm_w, m_norm_w, v_norm_w), (fnw, as_row(m_final_norm_w), as_row(v_final_norm_w)),
         (ssd_dt_bias, m_ssd_dt_bias, v_ssd_dt_bias), (ssd_a_log, m_ssd_a_log, v_ssd_a_log), (ssd_d, m_ssd_d, v_ssd_d)],
        windows, extra=(slice(9, 10), slice(256, 257)))
    loss = loss.reshape(())

    ncol = mod_w.shape[2]
    dmod_all = small_all[:, 0:6, :].reshape(NDEV, 2, 3 * dm)
    dmod_sh = lax.dynamic_slice_in_dim(dmod_all, me * ncol, ncol, axis=2).transpose(1, 0, 2)
    g_modw = _mod_grad(c_all, dmod_sh).reshape(1, 2 * dm, ncol)
    modw_out = _adamw("adamw_mod_w", mod_w.reshape(2 * dm, ncol), g_modw, m_mod_w.reshape(2 * dm, ncol),
                      v_mod_w.reshape(2 * dm, ncol), 256)

    per_kind = []
    for k in range(4):
        per_kind.append([
            nw_out[k], modw_out[k].reshape(mod_w.shape), mb_out[k], ai_out[k][None], ao_out[k][None], si_out[k][None],
            cw_out[k], cb_out[k], dtb_out[k], alog_out[k], d_out[k], snw_out[k], so_out[k][None], fnw_out[k].reshape(dm)])
    return (loss, dx0.reshape(x.shape), *per_kind[0], *per_kind[1], *per_kind[2], *per_kind[3])


def _pack_ssd_small_blocks(g_cw, g_cb, g_nw):
    nper = g_cw.shape[1] // NDEV
    nwper = g_nw.shape[1] // NDEV

    def body(cw_ref, cb_ref, nw_ref, o_ref):
        o_ref[...] = jnp.zeros_like(o_ref)
        for d in range(NDEV):
            o_ref[d, 0:5, :] = cw_ref[:, d * nper:(d + 1) * nper]
            o_ref[d, 5:6, :] = cb_ref[:, d * nper:(d + 1) * nper]
            o_ref[d, 6:7, 0:nwper] = nw_ref[:, d * nwper:(d + 1) * nwper]

    return pl.pallas_call(body, name="pack_ssd_small_grads", out_shape=jax.ShapeDtypeStruct((NDEV, 8, nper), F32))(g_cw, g_cb, g_nw)
```

```python
import functools
import math

import jax
import jax.numpy as jnp
from jax import lax
from jax.experimental import pallas as pl
from jax.experimental.pallas import tpu as pltpu

F32 = jnp.float32
BF16 = jnp.bfloat16
HI = lax.Precision.HIGHEST
MESH = pl.DeviceIdType.MESH
NDEV = 8

NORM_EPS = 1e-6
ROPE_THETA = 500000.0
ROT_DIM = 16
HEAD_DIM = 64
DILATIONS = (1, 4, 16)
BAND = 64
NEG_BIG = -1e30
CHUNK = 128
SSD_HEADS = 32
SSD_GROUPS = 8
CONV_WIDTH = 5

ADAM_LR = 0.001
ADAM_B1 = 0.9
ADAM_B2 = 0.999
ADAM_EPS = 1e-08
ADAM_WD = 0.01
ADAM_STEP = 10

VMEM_BIG = 56 * 1024 * 1024
MM_T = 1024


def _params(sem=None, vmem=None):
    kw = {}
    if sem is not None:
        kw["dimension_semantics"] = sem
    if vmem is not None:
        kw["vmem_limit_bytes"] = vmem
    return pltpu.CompilerParams(**kw)


def _dg(a, b, ca, cb, prec=None):
    return lax.dot_general(a, b, (((ca,), (cb,)), ((), ())), preferred_element_type=F32, precision=prec)


def _nn(a, b):
    return _dg(a.astype(BF16), b.astype(BF16), 1, 0)


def _nt(a, b):
    return _dg(a.astype(BF16), b.astype(BF16), 1, 1)


def _tn(a, b):
    return _dg(a.astype(BF16), b.astype(BF16), 0, 0)


def _hnn(a, b):
    return _dg(a, b, 1, 0, HI)


@jax.custom_vjp
def _bnn(a, b):
    return _nn(a, b)


_bnn.defvjp(lambda a, b: (_nn(a, b), (a, b)), lambda r, g: (_nt(g, r[1]), _tn(r[0], g)))


@jax.custom_vjp
def _bnt(a, b):
    return _nt(a, b)


_bnt.defvjp(lambda a, b: (_nt(a, b), (a, b)), lambda r, g: (_nn(g, r[1]), _tn(g, r[0])))


@jax.custom_vjp
def _btn(a, b):
    return _tn(a, b)


_btn.defvjp(lambda a, b: (_tn(a, b), (a, b)), lambda r, g: (_nt(r[1], g), _nn(r[0], g)))


def _silu(x):
    return x * jax.nn.sigmoid(x)


def _matmul(name, a, b, mode, out_dtype, tm, tn, tk, *, epilogue=None, tiled=(), mrows=(), ncols=(),
            b_noff=0, b_koff=0, n_out=None, out_blocks=None, dest=None):
    if mode == "tn":
        K, M = a.shape
    else:
        M, K = a.shape
    N = n_out if n_out is not None else (b.shape[0] if mode == "nt" else b.shape[1])
    tm, tn, tk = min(tm, M), min(tn, N), min(tk, K)
    assert M % tm == 0 and N % tn == 0 and K % tk == 0, (name, M, N, K, tm, tn, tk)
    assert b_noff % tn == 0 and b_koff % tk == 0
    no, ko = b_noff // tn, b_koff // tk
    nk = K // tk
    if mode == "tn":
        a_spec = pl.BlockSpec((tk, tm), lambda i, j, k: (k, i))
    else:
        a_spec = pl.BlockSpec((tm, tk), lambda i, j, k: (i, k))
    if mode == "nt":
        b_spec = pl.BlockSpec((tn, tk), lambda i, j, k: (j + no, k + ko))
    else:
        b_spec = pl.BlockSpec((tk, tn), lambda i, j, k: (k + ko, j + no))
    specs = [a_spec, b_spec]
    specs += [pl.BlockSpec((tm, tn), lambda i, j, k: (i, j)) for _ in tiled]
    specs += [pl.BlockSpec((tm, r.shape[1]), lambda i, j, k: (i, 0)) for r in mrows]
    specs += [pl.BlockSpec((1, tn), lambda i, j, k: (0, j)) for _ in ncols]
    total, off, earlier = dest if dest is not None else (None, 0, None)
    if out_blocks is None:
        assert off % tm == 0
        mo = off // tm
        out_shape = jax.ShapeDtypeStruct((M if total is None else total, N), out_dtype)
        out_spec = pl.BlockSpec((tm, tn), lambda i, j, k: (i + mo, j))
    else:
        nper = N // out_blocks
        assert nper % tn == 0
        jb = nper // tn
        out_shape = jax.ShapeDtypeStruct((out_blocks if total is None else total, M, nper), out_dtype)
        out_spec = pl.BlockSpec((None, tm, tn), lambda i, j, k: (j // jb + off, i, j % jb))
    if earlier is not None:
        assert earlier.shape == out_shape.shape and earlier.dtype == out_shape.dtype
    ne = len(tiled) + len(mrows) + len(ncols)
    dot = {"nn": _nn, "nt": _nt, "tn": _tn}[mode]

    def body(a_ref, b_ref, *rest):
        extras, o_ref = rest[:ne], rest[ne]

        def finish(acc):
            if epilogue is not None:
                acc = epilogue(acc, *[e[...] for e in extras])
            o_ref[...] = acc.astype(o_ref.dtype)

        if nk == 1:
            finish(dot(a_ref[...], b_ref[...]))
        else:
            acc_ref = rest[ne + 1]
            k = pl.program_id(2)

            @pl.when(k == 0)
            def _():
                acc_ref[...] = jnp.zeros_like(acc_ref)

            acc_ref[...] += dot(a_ref[...], b_ref[...])

            @pl.when(k == nk - 1)
            def _():
                finish(acc_ref[...])

    args = [a, b, *tiled, *mrows, *ncols]
    aliases = {}
    if earlier is not None:
        specs.append(pl.BlockSpec(memory_space=pl.ANY))
        aliases = {len(args): 0}
        args.append(earlier)

    def body_with_dest(*refs):
        body(*refs[:2 + ne], *refs[2 + ne + (earlier is not None):])

    return pl.pallas_call(
        body_with_dest, name=name, out_shape=out_shape, grid=(M // tm, N // tn, nk),
        in_specs=specs, out_specs=out_spec, input_output_aliases=aliases,
        scratch_shapes=[] if nk == 1 else [pltpu.VMEM((tm, tn), F32)],
        compiler_params=_params(("parallel", "parallel", "arbitrary"), VMEM_BIG),
    )(*args)


def _matmul_rows(name, a, b, mode, tm, tk, fn, rows, consts, outs, accs, *, n_out=None, b_noff=0, b_koff=0):
    rl = [(t, t.shape[1], 0) if not isinstance(t, tuple) else t for t in rows]
    make_a = a if callable(a) else None
    M, K = (rl[0][0].shape[0], b.shape[1 if mode == "nt" else 0]) if make_a else a.shape
    N = n_out if n_out is not None else (b.shape[0] if mode == "nt" else b.shape[1])
    tm, tk = min(tm, M), min(tk, K)
    assert M % tm == 0 and K % tk == 0 and b_koff % tk == 0 and b_noff % N == 0, (name, M, N, K)
    no, ko, nk = b_noff // N, b_koff // tk, K // tk
    assert make_a is None or nk == 1
    nr, nc, no_, na = len(rl), len(consts), len(outs), len(accs)
    dot = _nt if mode == "nt" else _nn

    def body(*refs):
        a_ref, b_ref, rest = (None, refs[0], refs[1:]) if make_a else (refs[0], refs[1], refs[2:])
        r_refs, c_refs = rest[:nr], rest[nr:nr + nc]
        o_refs, acc_refs = rest[nr + nc:nr + nc + no_], rest[nr + nc + no_:nr + nc + no_ + na]
        i, k = pl.program_id(0), pl.program_id(1)

        def finish(prod, *made):
            res_o, res_a = fn(prod, *made, *[r[...] for r in r_refs], *[c[...] for c in c_refs])
            for r, v in zip(o_refs, res_o, strict=True):
                r[...] = v.astype(r.dtype)
            if acc_refs:
                @pl.when(i == 0)
                def _():
                    for r in acc_refs:
                        r[...] = jnp.zeros_like(r)

                for r, v in zip(acc_refs, res_a, strict=True):
                    r[...] += v

        if make_a:
            left = make_a(*[r[...] for r in r_refs], *[c[...] for c in c_refs])
            finish(dot(left, b_ref[...]), left)
        elif nk == 1:
            finish(dot(a_ref[...], b_ref[...]))
        else:
            prod_ref = rest[-1]

            @pl.when(k == 0)
            def _():
                prod_ref[...] = jnp.zeros_like(prod_ref)

            prod_ref[...] += dot(a_ref[...], b_ref[...])

            @pl.when(k == nk - 1)
            def _():
                finish(prod_ref[...])

    if mode == "nt":
        b_spec = pl.BlockSpec((N, tk), lambda i, k: (no, k + ko))
    else:
        b_spec = pl.BlockSpec((tk, N), lambda i, k: (k + ko, no))
    in_specs = ([] if make_a else [pl.BlockSpec((tm, tk), lambda i, k: (i, k))]) + [b_spec]
    in_specs += [pl.BlockSpec((tm, w), functools.partial(lambda i, k, cb: (i, cb), cb=cb)) for (_, w, cb) in rl]
    in_specs += [pl.BlockSpec(c.shape, lambda i, k: (0, 0)) for c in consts]
    out_specs = [pl.BlockSpec((tm, c), lambda i, k: (i, 0)) for (c, _) in outs]
    out_specs += [pl.BlockSpec(shp, lambda i, k: (0, 0)) for shp in accs]
    out_shape = [jax.ShapeDtypeStruct((M, c), dt) for (c, dt) in outs] + [jax.ShapeDtypeStruct(shp, F32) for shp in accs]
    res = pl.pallas_call(
        body, name=name, out_shape=out_shape, grid=(M // tm, nk), in_specs=in_specs, out_specs=out_specs,
        scratch_shapes=[] if nk == 1 else [pltpu.VMEM((tm, N), F32)],
        compiler_params=_params(("arbitrary" if accs else "parallel", "arbitrary"), VMEM_BIG),
    )(*([] if make_a else [a]), b, *[t[0] for t in rl], *consts)
    return res[:no_], res[no_:]


def _rowwise(name, fn, tiled, consts, outs, accs, ts):
    tl = [(t, t.shape[1], 0) if not isinstance(t, tuple) else t for t in tiled]
    s_len = tl[0][0].shape[0]
    assert s_len % ts == 0
    nt_, nc_, no_ = len(tl), len(consts), len(outs)

    def body(*refs):
        t_refs, c_refs = refs[:nt_], refs[nt_:nt_ + nc_]
        o_refs, a_refs = refs[nt_ + nc_:nt_ + nc_ + no_], refs[nt_ + nc_ + no_:]
        res_o, res_a = fn(*[r[...] for r in t_refs], *[r[...] for r in c_refs])
        for r, v in zip(o_refs, res_o, strict=True):
            r[...] = v.astype(r.dtype)
        if a_refs:
            @pl.when(pl.program_id(0) == 0)
            def _():
                for r in a_refs:
                    r[...] = jnp.zeros_like(r)

            for r, v in zip(a_refs, res_a, strict=True):
                r[...] += v

    in_specs = [pl.BlockSpec((ts, w), functools.partial(lambda i, cb: (i, cb), cb=cb)) for (_, w, cb) in tl]
    in_specs += [pl.BlockSpec(c.shape, lambda i: (0, 0)) for c in consts]
    out_specs = [pl.BlockSpec((ts, c), lambda i: (i, 0)) for (c, _) in outs]
    out_specs += [pl.BlockSpec(shp, lambda i: (0, 0)) for shp in accs]
    out_shape = [jax.ShapeDtypeStruct((s_len, c), dt) for (c, dt) in outs]
    out_shape += [jax.ShapeDtypeStruct(shp, F32) for shp in accs]
    res = pl.pallas_call(
        body, name=name, out_shape=out_shape, grid=(s_len // ts,), in_specs=in_specs, out_specs=out_specs,
        compiler_params=_params(("arbitrary",) if accs else ("parallel",), VMEM_BIG),
    )(*[t[0] for t in tl], *consts)
    return res[:no_], res[no_:]


def _norm_mod_fn(x, nw, sc, sh):
    r = lax.rsqrt(jnp.mean(x * x, axis=-1, keepdims=True) + NORM_EPS)
    return (x * r * nw) * (1.0 + sc) + sh


def _norm_mod_fwd(name, x, nw, sc, sh):
    (hn,), _ = _rowwise(name, lambda x, nw, sc, sh: ([_norm_mod_fn(x, nw, sc, sh)], []),
                        [x], [nw, sc, sh], [(x.shape[1], BF16)], [], 512)
    return hn


def _norm_mod_bwd(name, last, x, dhn_parts, dres, nw, sc, sh, prev=None):
    n = len(dhn_parts)
    d = x.shape[1]
    a, b, mode, tk, kw = last

    def fn(dhn, x, *rest):
        for p in rest[:n]:
            dhn = dhn + p
        dres, rest = rest[n], rest[n + 1:]
        y_prev, (nw, sc, sh), gate = (rest[0], rest[1:4], rest[4]) if prev is not None else (None, rest[0:3], None)
        _, vjp = jax.vjp(_norm_mod_fn, x, nw, sc, sh)
        dx, dnw, dsc, dsh = vjp(dhn)
        dx = dx + dres
        if prev is None:
            return [dx], [dnw, dsc, dsh]
        return [dx, gate * dx], [dnw, dsc, dsh, jnp.sum(dx * y_prev, axis=0, keepdims=True)]

    rows = [x, *dhn_parts, dres] + ([prev[0]] if prev is not None else [])
    consts = [nw, sc, sh] + ([prev[1]] if prev is not None else [])
    outs = [(d, F32)] + ([(d, BF16)] if prev is not None else [])
    res_o, res_a = _matmul_rows(name, a, b, mode, 512, tk, fn, rows, consts, outs, [(1, d)] * (3 + (prev is not None)), **kw)
    return (*res_o, *res_a)


def _rope_tables(pos_col, inv_row):
    def fn(pos, inv):
        ang = pos.astype(F32) * inv
        e = lax.broadcasted_iota(jnp.int32, (1, 128), 1) % HEAD_DIM
        cos, sin = jnp.cos(ang), jnp.sin(ang)
        half = ROT_DIM // 2
        return [jnp.where(e < ROT_DIM, cos, 1.0), jnp.where(e < half, -sin, 0.0),
                jnp.where((e >= half) & (e < ROT_DIM), sin, 0.0)], []

    (c, sa, sb), _ = _rowwise("rope_tables", fn, [pos_col], [inv_row], [(128, F32)] * 3, [], 512)
    return c, sa, sb


def _rot_fwd(t, c, sa, sb):
    n = t.shape[1]
    rep = n // 128
    c, sa, sb = (jnp.tile(u, (1, rep)) for u in (c, sa, sb))
    return t * c + pltpu.roll(t, n - ROT_DIM // 2, 1) * sa + pltpu.roll(t, ROT_DIM // 2, 1) * sb


def _rot_bwd(g, c, sa, sb):
    n = g.shape[1]
    rep = n // 128
    c, sa, sb = (jnp.tile(u, (1, rep)) for u in (c, sa, sb))
    return g * c + pltpu.roll(g * sa, ROT_DIM // 2, 1) + pltpu.roll(g * sb, n - ROT_DIM // 2, 1)


ATT_TQ = 128


def _attn_tiles(l):
    tk = ATT_TQ + 2 * BAND
    return (l, l) if l <= tk else (ATT_TQ, tk)


def _attn_specs(g, s_len):
    def blk(off):
        return pl.BlockSpec((s_len, 128), functools.partial(lambda hp, off: (0, off + hp), off=off))

    return blk(4 * g), blk(12 + 4 * g), blk(4 * g), blk(0)


def _attn_tile_geometry(t, d, l):
    tq, tk = _attn_tiles(l)
    nts = l // tq
    r = t // nts
    ts = t % nts
    q0 = ts * tq
    ws = jnp.clip(q0 - BAND, 0, l - tk)
    kind = jnp.where(ts == 0, 0, jnp.where(ts == nts - 1, 2, 1))
    if d == 1:
        return pl.ds(pl.multiple_of(q0, tq), tq), pl.ds(pl.multiple_of(ws, BAND), tk), kind
    return pl.ds(r + d * q0, tq, stride=d), pl.ds(r + d * ws, tk, stride=d), kind


def _attn_fill_bias(bias_ref):
    _, tq2, tk = bias_ref.shape
    iq = lax.broadcasted_iota(jnp.int32, (tq2, 1), 0) % (tq2 // 2)
    ik = lax.broadcasted_iota(jnp.int32, (1, tk), 1)
    for i, off in enumerate((0, -BAND, -2 * BAND)):
        bias_ref[i] = jnp.where(jnp.abs(ik + off - iq) <= BAND, 0.0, NEG_BIG)


def _split_heads(t, in_h):
    zero = jnp.zeros_like(t)
    return jnp.concatenate([jnp.where(in_h[0], t, zero), jnp.where(in_h[1], t, zero)], axis=0)


def _attn_fwd(g, qk, v):
    s_len = qk.shape[0]
    d = DILATIONS[g]
    l = s_len // d
    tq, tk = _attn_tiles(l)
    assert l % tq == 0 and l >= tk
    q_spec, k_spec, v_spec, o_spec = _attn_specs(g, s_len)
    scale = 1.0 / math.sqrt(HEAD_DIM)

    def body(q_ref, k_ref, v_ref, o_ref, lse_ref, bias_ref):
        lane = lax.broadcasted_iota(jnp.int32, (1, 128), 1)
        in_h = [lane < HEAD_DIM, lane >= HEAD_DIM]
        _attn_fill_bias(bias_ref)

        def tile(t, carry):
            rows, win, kind = _attn_tile_geometry(t, d, l)
            q = (q_ref[rows, :] * scale).astype(BF16)
            k = k_ref[win, :].astype(BF16)
            vv = v_ref[win, :].astype(BF16)
            s = _nt(_split_heads(q, in_h), k) + bias_ref[kind]
            m = jnp.max(s, axis=1, keepdims=True)
            p = jnp.exp(s - m)
            den = jnp.sum(p, axis=1, keepdims=True)
            out = _nn(p, vv) / den
            lse = m + jnp.log(den)
            o_ref[rows, :] = jnp.where(in_h[0], out[:tq], out[tq:])
            lse_ref[rows, :] = jnp.where(in_h[0], lse[:tq], lse[tq:])
            return carry

        lax.fori_loop(0, s_len // tq, tile, 0, unroll=4 * ATT_TQ // tq)

    return pl.pallas_call(
        body, name=f"attn_fwd_g{g}", grid=(4,),
        out_shape=[jax.ShapeDtypeStruct((s_len, 512), F32)] * 2,
        in_specs=[q_spec, k_spec, v_spec], out_specs=[o_spec, o_spec],
        scratch_shapes=[pltpu.VMEM((3, 2 * tq, tk), F32)],
        compiler_params=_params(("parallel",), VMEM_BIG),
    )(qk, qk, v)


def _attn_bwd(g, qk, v, o, lse, do, dlse):
    s_len = qk.shape[0]
    d = DILATIONS[g]
    l = s_len // d
    tq, tk = _attn_tiles(l)
    q_spec, k_spec, v_spec, o_spec = _attn_specs(g, s_len)
    scale = 1.0 / math.sqrt(HEAD_DIM)

    def body(q_ref, k_ref, v_ref, o_ref, lse_ref, do_ref, dlse_ref, dq_ref, dk_ref, dv_ref, bias_ref):
        lane = lax.broadcasted_iota(jnp.int32, (1, 128), 1)
        in_h = [lane < HEAD_DIM, lane >= HEAD_DIM]
        dk_ref[...] = jnp.zeros_like(dk_ref)
        dv_ref[...] = jnp.zeros_like(dv_ref)
        _attn_fill_bias(bias_ref)

        def tile(t, carry):
            rows, win, kind = _attn_tile_geometry(t, d, l)
            k, vv = k_ref[win, :].astype(BF16), v_ref[win, :].astype(BF16)
            dout, lse_t, dlse_t = do_ref[rows, :], lse_ref[rows, :], dlse_ref[rows, :]
            od = dout * o_ref[rows, :]
            q2 = _split_heads((q_ref[rows, :] * scale).astype(BF16), in_h)
            do2 = _split_heads(dout.astype(BF16), in_h)
            head_col = lambda a: jnp.concatenate([a[:, 0:1], a[:, HEAD_DIM:HEAD_DIM + 1]], axis=0)
            delta = jnp.concatenate([jnp.sum(jnp.where(m, od, 0.0), axis=1, keepdims=True) for m in in_h], axis=0)
            p = jnp.exp(_nt(q2, k) + bias_ref[kind] - head_col(lse_t))
            ds = (p * (_nt(do2, vv) - delta + head_col(dlse_t))).astype(BF16)
            dq2 = _nn(ds, k) * scale
            dq_ref[rows, :] = jnp.where(in_h[0], dq2[:tq], dq2[tq:])
            dk_ref[win, :] += _tn(ds, q2)
            dv_ref[win, :] += _tn(p, do2)
            return carry

        lax.fori_loop(0, s_len // tq, tile, 0, unroll=4 * ATT_TQ // tq)

    return pl.pallas_call(
        body, name=f"attn_bwd_g{g}", grid=(4,),
        out_shape=[jax.ShapeDtypeStruct((s_len, 512), F32)] * 3,
        in_specs=[q_spec, k_spec, v_spec, o_spec, o_spec, o_spec, o_spec], out_specs=[o_spec] * 3,
        scratch_shapes=[pltpu.VMEM((3, 2 * tq, tk), F32)],
        compiler_params=_params(("parallel",), VMEM_BIG),
    )(qk, qk, v, o, lse, do, dlse)


def _mix_weights(ls):
    mx = jnp.maximum(jnp.maximum(ls[0], ls[1]), ls[2])
    es = [jnp.exp(x - mx) for x in ls]
    tot = es[0] + es[1] + es[2]
    return [e / tot for e in es]


def _attn_out(os_, lses, z, x, gate, w_out):
    s_len, dm = x.shape
    tm = 256
    wdt = 512
    z, z_block = z

    def body(o0, o1, o2, l0, l1, l2, z_ref, x_ref, g_ref, w_ref, a_ref, y_ref, x1_ref):
        alphas = _mix_weights([l0[...], l1[...], l2[...]])
        y = jnp.zeros((tm, dm), F32)
        for g, o_ref in enumerate((o0, o1, o2)):
            a_g = (o_ref[...] * alphas[g] * _silu(z_ref[:, g * wdt:(g + 1) * wdt])).astype(BF16)
            a_ref[:, g * wdt:(g + 1) * wdt] = a_g
            y = y + _nn(a_g, w_ref[g * wdt:(g + 1) * wdt, :])
        y_ref[...] = y
        x1_ref[...] = x_ref[...] + g_ref[...] * y

    row = lambda c: pl.BlockSpec((tm, c), lambda i: (i, 0))
    return pl.pallas_call(
        body, name="attn_out", grid=(s_len // tm,),
        out_shape=[jax.ShapeDtypeStruct((s_len, 3 * wdt), BF16), jax.ShapeDtypeStruct((s_len, dm), F32),
                   jax.ShapeDtypeStruct((s_len, dm), F32)],
        in_specs=[row(wdt)] * 6 + [pl.BlockSpec((tm, 3 * wdt), lambda i: (i, z_block)), row(dm),
                                   pl.BlockSpec((1, dm), lambda i: (0, 0)), pl.BlockSpec(w_out.shape, lambda i: (0, 0))],
        out_specs=[row(3 * wdt), row(dm), row(dm)],
        compiler_params=_params(("parallel",), VMEM_BIG),
    )(*os_, *lses, z, x, gate, w_out)


def _mix_bwd(dy, w_out, os_, lses, z):
    wdt = 512

    def fn(da, o0, o1, o2, l0, l1, l2, z):
        os_t, ls = [o0, o1, o2], [l0, l1, l2]
        alphas = _mix_weights(ls)
        hi = lax.broadcasted_iota(jnp.int32, (wdt, wdt), 0) // HEAD_DIM
        hj = lax.broadcasted_iota(jnp.int32, (wdt, wdt), 1) // HEAD_DIM
        seg = (hi == hj).astype(F32)
        dos, dal, dzs = [], [], []
        for g in range(3):
            zg = z[:, g * wdt:(g + 1) * wdt]
            sig = jax.nn.sigmoid(zg)
            dag = da[:, g * wdt:(g + 1) * wdt]
            dmix = dag * zg * sig
            dzs.append(dag * os_t[g] * alphas[g] * (sig * (1.0 + zg * (1.0 - sig))))
            dos.append(dmix * alphas[g])
            dal.append(_hnn(dmix * os_t[g], seg))
        mean = alphas[0] * dal[0] + alphas[1] * dal[1] + alphas[2] * dal[2]
        dls = [alphas[g] * (dal[g] - mean) for g in range(3)]
        return dos + dls + [jnp.concatenate(dzs, axis=1)], []

    outs, _ = _matmul_rows("attn_out_dx_mix_bwd", dy, w_out, "nt", 256, dy.shape[1], fn, [*os_, *lses, (z[0], 3 * wdt, z[1])], [],
                           [(wdt, F32)] * 6 + [(3 * wdt, BF16)], [])
    return outs[:3], outs[3:6], outs[6]


def _rot_pack_bwd(dqs, dks, dvs, tabs):
    wdt = 512

    def fn(*args):
        grads, (c, sa, sb) = args[:9], args[9:]
        cols = [_rot_bwd(gq, c, sa, sb) for gq in grads[:6]] + list(grads[6:])
        return [jnp.concatenate(cols, axis=1)], []

    (out,), _ = _rowwise("rot_pack_bwd", fn, [*dqs, *dks, *dvs, *tabs], [], [(9 * wdt, BF16)], [], 256)
    return out


CONV_CB = 128
CONV_R = 256
CONV_PAD = 8


def _conv_taps(buf, base, off, sign):
    return [buf[pl.ds(base + off + sign * j, CONV_R), :] for j in range(CONV_WIDTH)]


def _conv_tap_sum(taps, w):
    acc = None
    for j, t in enumerate(taps):
        term = t * w[j:j + 1, :]
        acc = term if acc is None else acc + term
    return acc


def _conv_fwd(xpre, cw, cb):
    s_len, ch = xpre.shape
    nchunk = s_len // CONV_R

    def body(x_ref, w_ref, b_ref, o_ref, xp):
        zero = jnp.zeros((CONV_PAD, CONV_CB), F32)
        xp[0:CONV_PAD, :] = zero
        xp[s_len + CONV_PAD:s_len + 2 * CONV_PAD, :] = zero

        def fill(ci, carry):
            base = pl.multiple_of(ci * CONV_R, CONV_R)
            xp[pl.ds(base + CONV_PAD, CONV_R), :] = x_ref[pl.ds(base, CONV_R), :]
            return carry

        lax.fori_loop(0, nchunk, fill, 0)
        w = w_ref[...]
        b = b_ref[...]

        def chunk(ci, carry):
            base = pl.multiple_of(ci * CONV_R, CONV_R)
            u = _conv_tap_sum(_conv_taps(xp, base, CONV_PAD - CONV_WIDTH // 2, 1), w) + b
            o_ref[pl.ds(base, CONV_R), :] = _silu(u)
            return carry

        lax.fori_loop(0, nchunk, chunk, 0, unroll=2)

    col = lambda r: pl.BlockSpec((r, CONV_CB), lambda j: (0, j))
    return pl.pallas_call(
        body, name="conv_fwd", grid=(ch // CONV_CB,), out_shape=jax.ShapeDtypeStruct((s_len, ch), F32),
        in_specs=[col(s_len), col(CONV_WIDTH), col(1)], out_specs=col(s_len),
        scratch_shapes=[pltpu.VMEM((s_len + 2 * CONV_PAD, CONV_CB), F32)],
        compiler_params=_params(("parallel",), VMEM_BIG),
    )(xpre, cw, cb)


def _conv_bwd(xpre, da, cw, cb):
    s_len, ch = xpre.shape
    nchunk = s_len // CONV_R
    half = CONV_WIDTH // 2

    def body(x_ref, da_ref, w_ref, b_ref, dx_ref, gw_ref, gb_ref, xp, dcp):
        zero = jnp.zeros((CONV_PAD, CONV_CB), F32)
        for buf in (xp, dcp):
            buf[0:CONV_PAD, :] = zero
            buf[s_len + CONV_PAD:s_len + 2 * CONV_PAD, :] = zero

        def fill(ci, carry):
            base = pl.multiple_of(ci * CONV_R, CONV_R)
            xp[pl.ds(base + CONV_PAD, CONV_R), :] = x_ref[pl.ds(base, CONV_R), :]
            return carry

        lax.fori_loop(0, nchunk, fill, 0)
        w = w_ref[...]
        b = b_ref[...]

        def first(ci, carry):
            base = pl.multiple_of(ci * CONV_R, CONV_R)
            taps = _conv_taps(xp, base, CONV_PAD - half, 1)
            u = _conv_tap_sum(taps, w) + b
            sig = jax.nn.sigmoid(u)
            dc = da_ref[pl.ds(base, CONV_R), :] * (sig * (1.0 + u * (1.0 - sig)))
            dcp[pl.ds(base + CONV_PAD, CONV_R), :] = dc
            gb = carry[0] + jnp.sum(dc, axis=0, keepdims=True)
            gws = [carry[1 + j] + jnp.sum(dc * taps[j], axis=0, keepdims=True) for j in range(CONV_WIDTH)]
            return (gb, *gws)

        z1 = jnp.zeros((1, CONV_CB), F32)
        sums = lax.fori_loop(0, nchunk, first, (z1,) * (1 + CONV_WIDTH), unroll=2)
        gb_ref[...] = sums[0]
        for j in range(CONV_WIDTH):
            gw_ref[j:j + 1, :] = sums[1 + j]

        def second(ci, carry):
            base = pl.multiple_of(ci * CONV_R, CONV_R)
            dx_ref[pl.ds(base, CONV_R), :] = _conv_tap_sum(_conv_taps(dcp, base, CONV_PAD + half, -1), w).astype(dx_ref.dtype)
            return carry

        lax.fori_loop(0, nchunk, second, 0, unroll=2)

    col = lambda r: pl.BlockSpec((r, CONV_CB), lambda j: (0, j))
    return pl.pallas_call(
        body, name="conv_bwd", grid=(ch // CONV_CB,),
        out_shape=[jax.ShapeDtypeStruct((s_len, ch), BF16), jax.ShapeDtypeStruct((CONV_WIDTH, ch), F32),
                   jax.ShapeDtypeStruct((1, ch), F32)],
        in_specs=[col(s_len), col(s_len), col(CONV_WIDTH), col(1)],
        out_specs=[col(s_len), col(CONV_WIDTH), col(1)],
        scratch_shapes=[pltpu.VMEM((s_len + 2 * CONV_PAD, CONV_CB), F32)] * 2,
        compiler_params=_params(("parallel",), VMEM_BIG),
    )(xpre, da, cw, cb)


SSD_GW = 256
SSD_N = 128
SSD_DTW = 128


def _bf16_parts(x, n):
    parts, rest = [], x
    for _ in range(n):
        p = rest.astype(BF16)
        parts.append(p)
        rest = rest - p.astype(F32)
    return parts


@jax.custom_vjp
def _expand(x, e):
    eb = e.astype(BF16)
    return _dg(jnp.concatenate(_bf16_parts(x, 2), axis=1), jnp.concatenate([eb, eb], axis=0), 1, 0)


def _expand_fwd(x, e):
    return _expand(x, e), e


def _expand_bwd(e, g):
    return _dg(g.astype(BF16), e.astype(BF16), 1, 1), jnp.zeros_like(e)


_expand.defvjp(_expand_fwd, _expand_bwd)


@jax.custom_vjp
def _running_sum(tri, x):
    tb = tri.astype(BF16)
    return sum(_dg(tb, p, 1, 0) for p in _bf16_parts(x, 3))


def _running_sum_fwd(tri, x):
    return _running_sum(tri, x), tri


def _running_sum_bwd(tri, g):
    tb = tri.astype(BF16)
    return jnp.zeros_like(tri), sum(_dg(tb, p, 0, 0) for p in _bf16_parts(g, 3))


_running_sum.defvjp(_running_sum_fwd, _running_sum_bwd)


def _pick_col(a, h):
    @jax.custom_vjp
    def pick(a):
        return a[:, h:h + 1]

    pick.defvjp(lambda a: (a[:, h:h + 1], None),
                lambda _, g: (g * (lax.broadcasted_iota(jnp.int32, (1, a.shape[1]), 1) == h).astype(F32),))
    return pick(a)


def _pick_row(a, h):
    @jax.custom_vjp
    def pick(a):
        return a[h:h + 1, :]

    pick.defvjp(lambda a: (a[h:h + 1, :], None),
                lambda _, g: (g * (lax.broadcasted_iota(jnp.int32, (a.shape[0], 1), 0) == h).astype(F32),))
    return pick(a)


def _ssd_mask(dirn):
    ri = lax.broadcasted_iota(jnp.int32, (CHUNK, CHUNK), 0)
    cj = lax.broadcasted_iota(jnp.int32, (CHUNK, CHUNK), 1)
    return (cj <= ri) if dirn == 0 else (cj >= ri)


def _ssd_rowsel(dirn):
    last = CHUNK - 1 if dirn == 0 else 0
    return (lax.broadcasted_iota(jnp.int32, (CHUNK, 1), 0) == last).astype(F32)


def _ssd_chunk_pre(dirn):
    nh = SSD_DTW

    def f(dt, alog):
        da = dt * (-jnp.exp(alog))
        cum = _running_sum(_ssd_mask(dirn).astype(F32), da)
        tot = jnp.sum(cum * _ssd_rowsel(dirn), axis=0, keepdims=True)
        hh = lax.broadcasted_iota(jnp.int32, (nh, SSD_HEADS * HEAD_DIM), 0)
        jj = lax.broadcasted_iota(jnp.int32, (nh, SSD_HEADS * HEAD_DIM), 1)
        expand = (hh == dirn * SSD_HEADS + jj // HEAD_DIM).astype(F32)
        return cum, cum.T, _expand(dt, expand), _expand(jnp.exp(tot - cum), expand), _expand(jnp.exp(cum), expand)

    return f


def _ssd_group_fn(g, dirn, stacked):
    def f(xs, bm, cm, st, cum, cum_t, dt_e, w_e, ce_e):
        mask = _ssd_mask(dirn)
        xdt = xs * dt_e
        cd_e = jnp.sum(ce_e * _ssd_rowsel(dirn), axis=0, keepdims=True)
        cb = _bnt(cm, bm)
        lane_head = lax.broadcasted_iota(jnp.int32, (1, SSD_GW), 1) // HEAD_DIM
        y = _bnn(cm, st) * ce_e
        decayed, inputs = [], []
        for j in range(4):
            hidx = dirn * SSD_HEADS + 4 * g + j
            col, row = _pick_col(cum, hidx), _pick_row(cum_t, hidx)
            dec = cb * jnp.exp(jnp.where(mask, col - row, NEG_BIG))
            head = (lane_head == j).astype(F32)
            if stacked:
                decayed.append(dec)
                inputs.append(xdt * head)
            else:
                y = y + _bnn(dec, xdt) * head
        if stacked:
            y = y + _bnn(jnp.concatenate(decayed, axis=1), jnp.concatenate(inputs, axis=0))
        st_out = st * cd_e + _btn(bm, xdt * w_e)
        return y, st_out

    return f


def _ssd_in_specs(kk):
    ln = CHUNK
    return [pl.BlockSpec((ln, 2048), lambda i: (kk(i), 0)),
            pl.BlockSpec((ln, 1024), lambda i: (kk(i), 2)),
            pl.BlockSpec((ln, 1024), lambda i: (kk(i), 3)),
            pl.BlockSpec((ln, SSD_DTW), lambda i: (kk(i), 0)),
            pl.BlockSpec((1, SSD_DTW), lambda i: (0, 0))]


def _ssd_fwd(xbc, dt, alog, dirn):
    s_len = xbc.shape[0]
    nc = s_len // CHUNK
    kk = (lambda i: i) if dirn == 0 else (lambda i: nc - 1 - i)

    def body(x_ref, b_ref, c_ref, dt_ref, al_ref, y_ref, sts_ref, st):
        @pl.when(pl.program_id(0) == 0)
        def _():
            st[...] = jnp.zeros_like(st)

        sts_ref[0] = st[...]
        cum, cum_t, dt_e, w_e, ce_e = _ssd_chunk_pre(dirn)(dt_ref[...], al_ref[...])
        for g in range(SSD_GROUPS):
            xc = slice(g * SSD_GW, (g + 1) * SSD_GW)
            gc = slice(g * SSD_N, (g + 1) * SSD_N)
            y, st_new = _ssd_group_fn(g, dirn, True)(x_ref[:, xc], b_ref[:, gc], c_ref[:, gc], st[:, xc], cum, cum_t,
                                               dt_e[:, xc], w_e[:, xc], ce_e[:, xc])
            y_ref[:, xc] = y
            st[:, xc] = st_new

    return pl.pallas_call(
        body, name=f"ssd_fwd_d{dirn}", grid=(nc,),
        out_shape=[jax.ShapeDtypeStruct((s_len, 2048), F32), jax.ShapeDtypeStruct((nc, SSD_N, 2048), F32)],
        in_specs=_ssd_in_specs(kk),
        out_specs=[pl.BlockSpec((CHUNK, 2048), lambda i: (kk(i), 0)),
                   pl.BlockSpec((1, SSD_N, 2048), lambda i: (kk(i), 0, 0))],
        scratch_shapes=[pltpu.VMEM((SSD_N, 2048), F32)],
        compiler_params=_params(("arbitrary",), VMEM_BIG),
    )(xbc, xbc, xbc, dt, alog)


def _ssd_bwd(xbc, dt, alog, states, dy, d_e, dirn, prior=None):
    s_len = xbc.shape[0]
    nc = s_len // CHUNK
    kk = (lambda i: nc - 1 - i) if dirn == 0 else (lambda i: i)

    def body(x_ref, b_ref, c_ref, dt_ref, al_ref, sts_ref, dy_ref, de_ref, *rest):
        prior_ref = rest[0] if prior is not None else None
        dx_ref, ddt_ref, dal_ref, dst = rest[prior is not None:]
        plus_prior = (lambda v, cols: v + prior_ref[:, cols]) if prior is not None else (lambda v, cols: v)

        @pl.when(pl.program_id(0) == 0)
        def _():
            dst[...] = jnp.zeros_like(dst)
            dal_ref[...] = jnp.zeros_like(dal_ref)

        (cum, cum_t, dt_e, w_e, ce_e), pre_vjp = jax.vjp(_ssd_chunk_pre(dirn), dt_ref[...], al_ref[...])
        dcum = jnp.zeros_like(cum)
        dcum_t = jnp.zeros_like(cum_t)
        d_dt_e, d_w_e, d_ce_e = [], [], []
        for g in range(SSD_GROUPS):
            xc = slice(g * SSD_GW, (g + 1) * SSD_GW)
            gc = slice(g * SSD_N, (g + 1) * SSD_N)
            _, vjp = jax.vjp(_ssd_group_fn(g, dirn, False), x_ref[:, xc], b_ref[:, gc], c_ref[:, gc], sts_ref[0, :, xc], cum, cum_t,
                             dt_e[:, xc], w_e[:, xc], ce_e[:, xc])
            dyg = dy_ref[:, xc]
            dxs, dbm, dcm, dst_g, dcum_g, dcum_t_g, ddte_g, dwe_g, dcee_g = vjp((dyg, dst[:, xc]))
            if dirn == 0:
                dxs = dxs + dyg * de_ref[:, xc]
            bc, cc = slice(2048 + g * SSD_N, 2048 + (g + 1) * SSD_N), slice(3072 + g * SSD_N, 3072 + (g + 1) * SSD_N)
            dx_ref[:, xc] = plus_prior(dxs, xc)
            dx_ref[:, bc] = plus_prior(dbm, bc)
            dx_ref[:, cc] = plus_prior(dcm, cc)
            dst[:, xc] = dst_g
            dcum = dcum + dcum_g
            dcum_t = dcum_t + dcum_t_g
            d_dt_e.append(ddte_g)
            d_w_e.append(dwe_g)
            d_ce_e.append(dcee_g)
        ddt, dal = pre_vjp((dcum, dcum_t, jnp.concatenate(d_dt_e, axis=1), jnp.concatenate(d_w_e, axis=1),
                            jnp.concatenate(d_ce_e, axis=1)))
        ddt_ref[...] = ddt
        dal_ref[...] += dal

    return pl.pallas_call(
        body, name=f"ssd_bwd_d{dirn}", grid=(nc,),
        out_shape=[jax.ShapeDtypeStruct((s_len, 4096), F32), jax.ShapeDtypeStruct((s_len, SSD_DTW), F32),
                   jax.ShapeDtypeStruct((1, SSD_DTW), F32)],
        in_specs=_ssd_in_specs(kk) + [pl.BlockSpec((1, SSD_N, 2048), lambda i: (kk(i), 0, 0)),
                                      pl.BlockSpec((CHUNK, 2048), lambda i: (kk(i), 0)),
                                      pl.BlockSpec((1, 2048), lambda i: (0, 0))]
        + ([pl.BlockSpec((CHUNK, 4096), lambda i: (kk(i), 0))] if prior is not None else []),
        out_specs=[pl.BlockSpec((CHUNK, 4096), lambda i: (kk(i), 0)),
                   pl.BlockSpec((CHUNK, SSD_DTW), lambda i: (kk(i), 0)),
                   pl.BlockSpec((1, SSD_DTW), lambda i: (0, 0))],
        scratch_shapes=[pltpu.VMEM((SSD_N, 2048), F32)],
        compiler_params=_params(("arbitrary",), VMEM_BIG),
    )(xbc, xbc, xbc, dt, alog, states, dy, d_e, *([prior] if prior is not None else []))


def _gate_norm_fn(yf, yb, xs, z, d_e, nw):
    yg = (yf + yb + xs * d_e) * _silu(z)
    return yg * lax.rsqrt(jnp.mean(yg * yg, axis=-1, keepdims=True) + NORM_EPS) * nw


def _gate_norm_bwd(dy, w_out, yf, yb, xbc, z, d_e, nw):
    def fn(du, yf, yb, xs, z, d_e, nw):
        sig = jax.nn.sigmoid(z)
        gate = z * sig
        ysum = yf + yb + xs * d_e
        yg = ysum * gate
        r = lax.rsqrt(jnp.mean(yg * yg, axis=-1, keepdims=True) + NORM_EPS)
        t = du * nw
        dyg = t * r - yg * (jnp.mean(t * yg, axis=-1, keepdims=True) * (r * r * r))
        dys = dyg * gate
        dz = dyg * ysum * (sig * (1.0 + z * (1.0 - sig)))
        dnw = jnp.sum(du * yg * r, axis=0, keepdims=True)
        dde = jnp.sum(dys * xs, axis=0, keepdims=True)
        hh = lax.broadcasted_iota(jnp.int32, (2048, SSD_HEADS), 0) // HEAD_DIM
        jj = lax.broadcasted_iota(jnp.int32, (2048, SSD_HEADS), 1)
        return [dys, dz], [dnw, _hnn(jnp.broadcast_to(dde, (8, 2048)), (hh == jj).astype(F32))[0:1]]

    (dys, dz), (g_nw, g_d) = _matmul_rows("ssd_out_dx_gate_norm_bwd", dy, w_out, "nt", 256, dy.shape[1], fn,
                                          [yf, yb, (xbc, 2048, 0), z], [d_e, nw], [(2048, F32), (2048, BF16)],
                                          [(1, 2048), (1, SSD_HEADS)])
    return dys, dz, g_nw, g_d


def _ssd_tail_loss(yf, yb, xbc, z, d_e, snw, w_out, x1, tgt, gate, fnw):
    dm = x1.shape[1]
    si = yf.shape[1]

    def make_u(yf, yb, xs, z, x1, tgt, d_e, snw, gate, fnw):
        return _gate_norm_fn(yf, yb, xs, z, d_e, snw).astype(BF16)

    def fn(y1, u, yf, yb, xs, z, x1, tgt, d_e, snw, gate, fnw):
        def head(x2, fnw):
            yf = (x2 * lax.rsqrt(jnp.mean(x2 * x2, axis=-1, keepdims=True) + NORM_EPS)) * fnw
            err = yf - tgt
            return 0.5 * jnp.sum(jnp.mean(err * err, axis=-1, keepdims=True), axis=0, keepdims=True)

        x2 = x1 + gate * y1
        loss, vjp = jax.vjp(head, x2, fnw)
        dx2, dfnw = vjp(jnp.ones((1, 1), F32))
        return [u, dx2, gate * dx2], [dfnw, jnp.sum(dx2 * y1, axis=0, keepdims=True), jnp.broadcast_to(loss, (1, 128))]

    (u, dx2, dy1), (g_fnw, dgate, loss) = _matmul_rows(
        "ssd_out_loss", make_u, w_out, "nn", 256, si, fn, [yf, yb, (xbc, si, 0), z, x1, tgt], [d_e, snw, gate, fnw],
        [(si, BF16), (dm, F32), (dm, BF16)], [(1, dm), (1, dm), (1, 128)])
    return u, dx2, dy1, g_fnw, dgate, loss


def _softplus_fwd(dt_raw, bias):
    (dt,), _ = _rowwise("dt_softplus", lambda r, b: ([jax.nn.softplus(r + b)], []), [dt_raw], [bias],
                        [(dt_raw.shape[1], F32)], [], 512)
    return dt


def _softplus_bwd(ddt_f, ddt_b, dt_raw, bias):
    def fn(df, db, r, b):
        g = (df + db) * jax.nn.sigmoid(r + b)
        return [g], [jnp.sum(g, axis=0, keepdims=True)]

    w = dt_raw.shape[1]
    (g,), (gb,) = _rowwise("dt_softplus_bwd", fn, [ddt_f, ddt_b, dt_raw], [bias], [(w, BF16)], [(1, w)], 512)
    return g, gb


def _whole(a):
    nd = len(a.shape)
    return pl.BlockSpec(a.shape, lambda *_: (0,) * nd)


def _mod_part(c_all, mod_w):
    nl, _, ncol = mod_w.shape
    nb = c_all.shape[0]

    def body(c_ref, w_ref, o_ref):
        cond = _silu(c_ref[...])
        for i in range(nl):
            o_ref[i * nb:(i + 1) * nb, :] = _nn(cond, w_ref[i])

    return pl.pallas_call(body, name="mod_part", out_shape=jax.ShapeDtypeStruct((nl * nb, ncol), F32),
                          compiler_params=_params(None, VMEM_BIG))(c_all, mod_w)


def _mod_finish(mod_nb, mod_b, norm_w, tokens):
    nl, dm = norm_w.shape

    def body(a_ref, b_ref, nw_ref, *rest):
        tok_refs, o_refs = rest[:len(tokens)], rest[len(tokens):]
        tok = sum(t[0:1, 0:1] for t in tok_refs)
        for i in range(nl):
            for k in range(3):
                cols = slice(k * dm, (k + 1) * dm)
                o_refs[4 * i + k][...] = a_ref[i:i + 1, cols] + b_ref[i:i + 1, cols]
            o_refs[4 * i + 3][...] = nw_ref[i:i + 1, :] + tok

    rows = pl.pallas_call(body, name="mod_finish", out_shape=[jax.ShapeDtypeStruct((1, dm), F32)] * (4 * nl))(
        mod_nb, mod_b, norm_w, *tokens)
    return [rows[4 * i:4 * i + 4] for i in range(nl)]


def _mod_grad(c_all, dmod_sh):
    nl, nb, ncol = dmod_sh.shape
    dm = c_all.shape[1]

    def body(c_ref, d_ref, o_ref):
        cond = _silu(c_ref[...])
        for i in range(nl):
            o_ref[i] = _tn(cond, d_ref[i])

    return pl.pallas_call(body, name="mod_grad", out_shape=jax.ShapeDtypeStruct((nl, dm, ncol), F32),
                          compiler_params=_params(None, VMEM_BIG))(c_all, dmod_sh)


PACK_ROWS = 16
PACK_COLS = 1024


def _pack_small(rows, b64, a64s, d32, extra):
    nr, na = len(rows), len(a64s)

    def body(*refs):
        o_ref = refs[-1]
        o_ref[...] = jnp.zeros_like(o_ref)
        for i in range(nr):
            o_ref[i:i + 1, :] = refs[i][...]
        b_ref, a_refs, d_ref, e_ref = refs[nr], refs[nr + 1:nr + 1 + na], refs[nr + 1 + na], refs[nr + 2 + na]
        o_ref[nr:nr + 1, 0:64] = b_ref[:, 0:64]
        o_ref[nr:nr + 1, 64:128] = sum(a[:, 0:64] for a in a_refs)
        o_ref[nr:nr + 1, 128:160] = d_ref[...]
        o_ref[nr:nr + 1, 256:384] = e_ref[...]

    return pl.pallas_call(body, name="pack_small", out_shape=jax.ShapeDtypeStruct((PACK_ROWS, PACK_COLS), F32))(
        *rows, b64, *a64s, d32, extra)


def _pack_ssd_small(cw, cb, nw):
    def body(cw_ref, cb_ref, nw_ref, o_ref):
        o_ref[...] = jnp.zeros_like(o_ref)
        o_ref[0:5, :] = cw_ref[...]
        o_ref[5:6, :] = cb_ref[...]
        o_ref[6:7, 0:256] = nw_ref[...]

    return pl.pallas_call(body, name="pack_ssd_small", out_shape=jax.ShapeDtypeStruct((8, 512), F32))(cw, cb, nw)


def _sum_parts(p_ref):
    g = p_ref[0].astype(F32)
    for s in range(1, p_ref.shape[0]):
        g = g + p_ref[s].astype(F32)
    return g


def _adam_update(w, g, m, v):
    m2 = ADAM_B1 * m + (1.0 - ADAM_B1) * g
    v2 = ADAM_B2 * v + (1.0 - ADAM_B2) * (g * g)
    m_hat = m2 / (1.0 - ADAM_B1 ** ADAM_STEP)
    v_hat = v2 / (1.0 - ADAM_B2 ** ADAM_STEP)
    return -ADAM_LR * (m_hat / (jnp.sqrt(v_hat) + ADAM_EPS) + ADAM_WD * w), m2, v2


def _adamw_windows(name, parts, params, windows, extra=None):
    n = len(params)

    def body(p_ref, *rest):
        ins, outs = rest[:3 * n], rest[3 * n:]
        g = _sum_parts(p_ref)
        for pi, rows, cols, idx in windows:
            w_ref, m_ref, v_ref = ins[3 * pi:3 * pi + 3]
            gw = g[rows, cols]
            dw, m2, v2 = _adam_update(w_ref[idx], gw, m_ref[idx], v_ref[idx])
            for o_ref, val in zip(outs[4 * pi:4 * pi + 4], (gw, dw, m2, v2), strict=True):
                o_ref[idx] = val
        if extra is not None:
            outs[4 * n][...] = g[extra[0], extra[1]]

    out_shape = [jax.ShapeDtypeStruct(w.shape, F32) for (w, _, _) in params for _ in range(4)]
    if extra is not None:
        out_shape.append(jax.ShapeDtypeStruct((extra[0].stop - extra[0].start, extra[1].stop - extra[1].start), F32))
    res = pl.pallas_call(body, name=name, out_shape=out_shape)(parts, *[a for p in params for a in p])
    return [res[4 * i:4 * i + 4] for i in range(n)] + ([res[4 * n]] if extra is not None else [])


def _adamw(name, w, parts, m, v, tr, tc=None):
    r_, c_ = w.shape
    p_ = parts.shape[0]
    tr = min(tr, r_)
    tc = c_ if tc is None else tc
    assert r_ % tr == 0 and c_ % tc == 0

    def body(w_ref, p_ref, m_ref, v_ref, g_ref, d_ref, m2_ref, v2_ref):
        g = _sum_parts(p_ref)
        g_ref[...] = g
        d_ref[...], m2_ref[...], v2_ref[...] = _adam_update(w_ref[...], g, m_ref[...], v_ref[...])

    blk = pl.BlockSpec((tr, tc), lambda i, j: (i, j))
    return pl.pallas_call(
        body, name=name, grid=(r_ // tr, c_ // tc), out_shape=[jax.ShapeDtypeStruct((r_, c_), F32)] * 4,
        in_specs=[blk, pl.BlockSpec((p_, tr, tc), lambda i, j: (0, i, j)), blk, blk], out_specs=[blk] * 4,
        compiler_params=_params(("parallel", "parallel"), VMEM_BIG),
    )(w, parts, m, v)


def _dev_index(p):
    return 4 * p[0] + 2 * p[1] + p[2]


def _all_gather(name, xs):
    n = len(xs)
    hbm = pl.BlockSpec(memory_space=pl.ANY)

    def body(*refs):
        x_refs, o_refs = refs[:n], refs[n:2 * n]
        send_sems, recv_sems, local_sems = refs[2 * n:]
        x, y, c = lax.axis_index("x"), lax.axis_index("y"), lax.axis_index("c")
        me, sibling = (x, y, c), (x, y, 1 - c)
        chips = [(1 - x, y), (x, 1 - y), (1 - x, 1 - y)]

        def copy(a, k, block, to, src=None):
            dst = o_refs[a].at[_dev_index(block)]
            return pltpu.make_async_remote_copy(
                src_ref=dst if src is None else src, dst_ref=dst, send_sem=send_sems.at[a, k],
                recv_sem=recv_sems.at[a, k], device_id=to, device_id_type=MESH)

        mine = [pltpu.make_async_copy(x_refs[a], o_refs[a].at[_dev_index(me)], local_sems.at[a]) for a in range(n)]
        for cp in mine:
            cp.start()
        first = []
        for a in range(n):
            first.append(copy(a, 0, me, sibling, src=x_refs[a]))
            first += [copy(a, 1 + j, me, (*chip, c), src=x_refs[a]) for j, chip in enumerate(chips)]
        for cp in first:
            cp.start()
        passed = []
        for j, chip in enumerate(chips):
            for a in range(n):
                copy(a, 1 + j, (*chip, c), me).wait_recv()
                cp = copy(a, 4 + j, (*chip, c), sibling)
                cp.start()
                passed.append(cp)
        for a in range(n):
            copy(a, 0, sibling, me).wait_recv()
            for j, chip in enumerate(chips):
                copy(a, 4 + j, (*chip, 1 - c), me).wait_recv()
        for cp in first + passed:
            cp.wait_send()
        for cp in mine:
            cp.wait()

    return pl.pallas_call(
        body, name=name, out_shape=[jax.ShapeDtypeStruct((NDEV, *x.shape), x.dtype) for x in xs],
        in_specs=[hbm] * n, out_specs=[hbm] * n,
        scratch_shapes=[pltpu.SemaphoreType.DMA((n, 7)), pltpu.SemaphoreType.DMA((n, 7)), pltpu.SemaphoreType.DMA((n,))],
    )(*xs)


_HBM = pl.BlockSpec(memory_space=pltpu.HBM)
_SEM = pl.BlockSpec(memory_space=pltpu.SEMAPHORE)
_EFFECT = pltpu.SideEffectType.DATAFLOW_SIDE_EFFECTING


def _mesh_position():
    return lax.axis_index("x"), lax.axis_index("y"), lax.axis_index("c")


def _peers(me):
    return [(k, tuple(1 - v if (k >> b) & 1 else v for v, b in zip(me, (2, 1, 0)))) for k in range(1, NDEV)]


EXCHANGE_COPIES = {"gather": NDEV - 1, "scatter": NDEV - 1, "pair": 4, "chips": 3}
NCHIP = NDEV // 2


def _landing_zones(name, xs, mode):
    x_, y_, c_ = _mesh_position()
    mine = (2 * x_ + y_ if mode == "chips" else _dev_index((x_, y_, c_))).astype(jnp.int32).reshape(1)
    lands = []
    for a, x in enumerate(xs):
        rows, cols = x.shape[-2:]
        if mode == "pair":
            lands.append(lax.empty((NCHIP, rows, cols), x.dtype))
            continue
        tr = 256 if rows % 256 == 0 else rows

        def body(me_ref, x_ref, o_ref):
            o_ref[...] = x_ref[...]

        if mode == "gather":
            in_spec = pl.BlockSpec((tr, cols), lambda i, me_ref: (i, 0))
        else:
            in_spec = pl.BlockSpec((None, tr, cols), lambda i, me_ref: (me_ref[0], i, 0))
        lands.append(pl.pallas_call(
            body, name=f"{name}_{a}",
            out_shape=jax.ShapeDtypeStruct((NCHIP if mode == "chips" else NDEV, rows, cols), x.dtype),
            grid_spec=pltpu.PrefetchScalarGridSpec(
                num_scalar_prefetch=1, grid=(rows // tr,), in_specs=[in_spec],
                out_specs=pl.BlockSpec((None, tr, cols), lambda i, me_ref: (me_ref[0], i, 0))),
            compiler_params=_params(("arbitrary",)),
        )(mine, x))
    return lands


def _exchange_copies(x_refs, land_refs, send_sems, recv_sems, mode):
    x_, y_, c_ = me = _mesh_position()
    per_array = EXCHANGE_COPIES[mode]
    out = []

    def add(a, k, src, dst, peer):
        sem = a * per_array + k
        out.append(pltpu.make_async_remote_copy(src_ref=src, dst_ref=dst, send_sem=send_sems.at[sem], recv_sem=recv_sems.at[sem],
                                                device_id=peer, device_id_type=MESH))

    for a, (x_ref, land_ref) in enumerate(zip(x_refs, land_refs)):
        if mode in ("gather", "scatter"):
            for k, peer in _peers(me):
                add(a, k - 1, x_ref.at[_dev_index(peer)] if mode == "scatter" else x_ref, land_ref.at[_dev_index(me)], peer)
        elif mode == "pair":
            for chip in range(NCHIP):
                add(a, chip, x_ref.at[2 * chip + 1 - c_], land_ref.at[chip], (x_, y_, 1 - c_))
        else:
            for k in range(1, NCHIP):
                px, py = (1 - x_ if k & 2 else x_), (1 - y_ if k & 1 else y_)
                add(a, k - 1, x_ref.at[2 * px + py], land_ref.at[2 * x_ + y_], (px, py, c_))
    return out


def _exchange_start(name, xs, lands, mode, dep):
    n = len(xs)

    def body(*refs):
        x_refs, land_refs = refs[:n], refs[n:2 * n]
        send_sems, recv_sems = refs[2 * n + 1], refs[2 * n + 2]
        token = refs[-1]
        for cp in _exchange_copies(x_refs, land_refs, send_sems, recv_sems, mode):
            cp.start()
        token[...] = jnp.zeros_like(token)

    sems = pltpu.SemaphoreType.DMA((n * EXCHANGE_COPIES[mode],))
    res = pl.pallas_call(
        body, name=name,
        out_shape=(sems, sems, *[pltpu.HBM(a.shape, a.dtype) for a in (*xs, *lands)], jax.ShapeDtypeStruct((8, 128), F32)),
        in_specs=[_HBM] * (2 * n) + [pl.BlockSpec(memory_space=pl.ANY)],
        out_specs=(_SEM, _SEM, *[_HBM] * (2 * n), pl.BlockSpec(memory_space=pltpu.VMEM)),
        input_output_aliases={i: 2 + i for i in range(2 * n)},
        compiler_params=pltpu.CompilerParams(has_side_effects=_EFFECT),
    )(*[pltpu.with_memory_space_constraint(a, pltpu.HBM) for a in (*xs, *lands)], dep)
    return res[:-1], res[-1]


def _exchange_wait(name, handles, mode, after):
    send_sems, recv_sems = handles[0], handles[1]
    bufs = handles[2:]
    n = len(bufs) // 2

    def body(*refs):
        x_refs, land_refs = refs[:n], refs[n:2 * n]
        s_sems, r_sems = refs[2 * n], refs[2 * n + 1]
        for cp in _exchange_copies(x_refs, land_refs, s_sems, r_sems, mode):
            cp.wait_send()
            cp.wait_recv()

    res = pl.pallas_call(
        body, name=name, out_shape=tuple(pltpu.HBM(a.shape, a.dtype) for a in bufs),
        in_specs=[_HBM] * (2 * n) + [_SEM, _SEM, pl.BlockSpec(memory_space=pl.ANY)], out_specs=tuple([_HBM] * (2 * n)),
        input_output_aliases={i: i for i in range(2 * n)},
        compiler_params=pltpu.CompilerParams(has_side_effects=_EFFECT),
    )(*bufs, send_sems, recv_sems, after)
    return res[n:]


def _pair_sum(name, x, from_sibling):
    _, rows, cols = x.shape
    tr = 256 if rows % 256 == 0 else rows
    core = lax.axis_index("c").astype(jnp.int32).reshape(1)

    def body(c_ref, x_ref, s_ref, o_ref):
        o_ref[...] = (x_ref[...].astype(F32) + s_ref[...].astype(F32)).astype(o_ref.dtype)

    return pl.pallas_call(
        body, name=name, out_shape=jax.ShapeDtypeStruct((NCHIP, rows, cols), x.dtype),
        grid_spec=pltpu.PrefetchScalarGridSpec(
            num_scalar_prefetch=1, grid=(NCHIP, rows // tr),
            in_specs=[pl.BlockSpec((None, tr, cols), lambda j, i, c_ref: (2 * j + c_ref[0], i, 0)),
                      pl.BlockSpec((None, tr, cols), lambda j, i, c_ref: (j, i, 0))],
            out_specs=pl.BlockSpec((None, tr, cols), lambda j, i, c_ref: (j, i, 0))),
        compiler_params=_params(("parallel", "parallel")),
    )(core, x, from_sibling)


def kernel(x, c, positions, norm_w, mod_w, mod_b, attn_w_in, attn_w_out, ssd_w_in, ssd_conv_w, ssd_conv_b, ssd_dt_bias, ssd_a_log, ssd_d, ssd_norm_w, ssd_w_out, final_norm_w, loss_target, m_norm_w, m_mod_w, m_mod_b, m_attn_w_in, m_attn_w_out, m_ssd_w_in, m_ssd_conv_w, m_ssd_conv_b, m_ssd_dt_bias, m_ssd_a_log, m_ssd_d, m_ssd_norm_w, m_ssd_w_out, m_final_norm_w, v_norm_w, v_mod_w, v_mod_b, v_attn_w_in, v_attn_w_out, v_ssd_w_in, v_ssd_conv_w, v_ssd_conv_b, v_ssd_dt_bias, v_ssd_a_log, v_ssd_d, v_ssd_norm_w, v_ssd_w_out, v_final_norm_w):
    s_len, dm = x.shape[1], x.shape[2]
    me = 4 * lax.axis_index("x") + 2 * lax.axis_index("y") + lax.axis_index("c")
    x0 = x.reshape(s_len, dm)
    tgt = loss_target.reshape(s_len, dm)
    aw = 3 * 512
    si = 2 * dm
    sxbc = 2 * si
    n_ssd_in = ssd_w_in.shape[2] * NDEV

    g_ai, c_all = _all_gather("gather_attn_w_in", [attn_w_in[0].astype(BF16), c])
    w_ai = g_ai.transpose(1, 0, 2).reshape(dm, 4 * aw)
    c_all = c_all.reshape(NDEV, dm)

    part = _mod_part(c_all, mod_w)
    (part_all,) = _all_gather("gather_mod", [part])
    mod_nb = jnp.stack([lax.dynamic_index_in_dim(part_all, i * NDEV + me, axis=1, keepdims=False).reshape(3 * dm)
                        for i in range(2)])

    ssd_small = _pack_ssd_small(ssd_conv_w[0], ssd_conv_b, ssd_norm_w)
    ao_shard = [attn_w_out[0].astype(BF16)]
    ao_handles, ao_token = _exchange_start("w_out_start", ao_shard, _landing_zones("w_out_place", ao_shard, "gather"), "gather",
                                           part_all)
    late_shards = [ssd_w_in[0].T.astype(BF16), ssd_w_out[0].astype(BF16), ssd_small]
    w_handles, w_token = _exchange_start("weights_start", late_shards, _landing_zones("weights_place", late_shards, "gather"),
                                         "gather", ao_token)
    (shift0, scale0, gate0, nw0), (shift1, scale1, gate1, nw1) = _mod_finish(mod_nb, mod_b, norm_w, [ao_token, w_token])
    shift, scale, gate, nw = [shift0, shift1], [scale0, scale1], [gate0, gate1], [nw0, nw1]

    hn0 = _norm_mod_fwd("norm0", x0, nw[0], scale[0], shift[0])
    inv_freq = ROPE_THETA ** (-jnp.arange(0, ROT_DIM, 2, dtype=F32) / ROT_DIM)
    lane = jnp.arange(128) % HEAD_DIM
    inv_row = jnp.where(lane < ROT_DIM, inv_freq[lane % (ROT_DIM // 2)], 0.0).reshape(1, 128).astype(F32)
    tabs = _rope_tables(positions.reshape(s_len, 1), inv_row)
    qk = _matmul("proj_qk", hn0, w_ai, "nn", F32, MM_T, MM_T, dm, epilogue=_rot_fwd, mrows=tabs, n_out=2 * aw)
    v = _matmul("proj_vz", hn0, w_ai, "nn", F32, MM_T, MM_T, dm, b_noff=2 * aw, n_out=2 * aw)
    z0 = (v, 1)
    att = [_attn_fwd(g, qk, v) for g in range(3)]
    os_, lses = [a[0] for a in att], [a[1] for a in att]
    (g_ao,) = _exchange_wait("w_out_wait", ao_handles, "gather", lses[2])
    a0, y0, x1 = _attn_out(os_, lses, z0, x0, gate[0], g_ao.reshape(aw, dm))

    hn1 = _norm_mod_fwd("norm1", x1, nw[1], scale[1], shift[1])
    g_si, g_so, g_small = _exchange_wait("weights_wait", w_handles, "gather", hn1)
    w_ao = g_ao.reshape(aw, dm)
    w_si_t = g_si.reshape(n_ssd_in, dm)
    w_so = g_so.reshape(si, dm)
    conv_w = g_small[:, 0:CONV_WIDTH, :].transpose(1, 0, 2).reshape(CONV_WIDTH, sxbc)
    conv_b = g_small[:, 5, :].reshape(1, sxbc)
    snw = g_small[:, 6, 0:si // NDEV].reshape(1, si)
    ndt = 2 * SSD_HEADS
    z1 = _matmul("ssd_proj_z", hn1, w_si_t, "nt", F32, MM_T, MM_T, dm, n_out=si)
    xpre = _matmul("ssd_proj_xbc", hn1, w_si_t, "nt", F32, MM_T, MM_T, dm, b_noff=si, n_out=sxbc)
    dt_raw = _matmul("ssd_proj_dt", hn1, w_si_t, "nt", F32, MM_T, ndt, dm, b_noff=si + sxbc, n_out=ndt)
    xbc = _conv_fwd(xpre, conv_w, conv_b)
    widen = lambda a: jnp.pad(a, ((0, 0), (0, SSD_DTW - ndt)))
    dt_raw = widen(dt_raw)
    dt_bias = widen(ssd_dt_bias.reshape(1, ndt))
    alog = widen(ssd_a_log.reshape(1, ndt))
    dt = _softplus_fwd(dt_raw, dt_bias)
    y_f, st_f = _ssd_fwd(xbc, dt, alog, 0)
    y_b, st_b = _ssd_fwd(xbc, dt, alog, 1)
    d_e = jnp.repeat(ssd_d.reshape(SSD_HEADS), HEAD_DIM).reshape(1, si)

    fnw = final_norm_w.reshape(1, dm)
    u, dx2, dy1, g_fnw, dgate1, loss_part = _ssd_tail_loss(y_f, y_b, xbc, z1, d_e, snw, w_so, x1, tgt, gate[1], fnw)
    gw_so = _matmul("ssd_out_dw", u, dy1, "tn", BF16, MM_T, MM_T, MM_T)
    dys, dz1, g_snw, g_d = _gate_norm_bwd(dy1, w_so, y_f, y_b, xbc, z1, d_e, snw)
    dxbc_f, ddt_f, dalog_f = _ssd_bwd(xbc, dt, alog, st_f, dys, d_e, 0)
    dxbc, ddt_b, dalog_b = _ssd_bwd(xbc, dt, alog, st_b, dys, d_e, 1, prior=dxbc_f)
    dpre, g_cw, g_cb = _conv_bwd(xpre, dxbc, conv_w, conv_b)
    ddt_raw, g_dtb = _softplus_bwd(ddt_f, ddt_b, dt_raw, dt_bias)
    ddt_raw = ddt_raw[:, :ndt]
    dhn1 = [_matmul("ssd_proj_z_dx", dz1, w_si_t, "nn", F32, MM_T, MM_T, MM_T),
            _matmul("ssd_proj_xbc_dx", dpre, w_si_t, "nn", F32, MM_T, MM_T, MM_T, b_koff=si)]
    gw_si_t = _matmul("ssd_proj_z_dw", dz1, hn1, "tn", BF16, MM_T, MM_T, MM_T, dest=(n_ssd_in, 0, None))
    gw_si_t = _matmul("ssd_proj_xbc_dw", dpre, hn1, "tn", BF16, MM_T, MM_T, MM_T, dest=(n_ssd_in, si, gw_si_t))
    gw_si_t = _matmul("ssd_proj_dt_dw", ddt_raw, hn1, "tn", BF16, ndt, MM_T, MM_T, dest=(n_ssd_in, si + sxbc, gw_si_t))

    l1_grads = [gw_so.reshape(NDEV, si // NDEV, dm), gw_si_t.reshape(NDEV, n_ssd_in // NDEV, dm),
                _pack_ssd_small_blocks(g_cw, g_cb, g_snw)]
    l1_handles, l1_token = _exchange_start("l1_grads_start", l1_grads, _landing_zones("l1_grads_place", l1_grads, "scatter"),
                                           "scatter", dhn1[1])
    dx1, dy0, g_nw1, dsc1, dsh1, dgate0 = _norm_mod_bwd(
        "ssd_proj_dt_dx_norm1_bwd", (ddt_raw, w_si_t, "nn", ndt, dict(b_koff=si + sxbc)), x1, dhn1, dx2,
        nw[1], scale[1], shift[1], prev=(y0, gate[0] + l1_token[0:1, 0:1]))

    gw_ao = _matmul("attn_out_dw", a0, dy0, "tn", BF16, aw // 2, MM_T, MM_T)
    dos, dls, dz0 = _mix_bwd(dy0, w_ao, os_, lses, z0)
    datt = [_attn_bwd(g, qk, v, os_[g], lses[g], dos[g], dls[g]) for g in range(3)]
    dqkv = _rot_pack_bwd([t[0] for t in datt], [t[1] for t in datt], [t[2] for t in datt], tabs)
    wcol = attn_w_in.shape[2]
    gw_ai = _matmul("proj_qkv_dw", hn0, dqkv, "tn", BF16, MM_T, wcol, MM_T, out_blocks=3 * aw // wcol, dest=(NDEV, 0, None))
    gw_ai = _matmul("proj_z_dw", hn0, dz0, "tn", BF16, MM_T, wcol, MM_T, out_blocks=aw // wcol,
                    dest=(NDEV, 3 * aw // wcol, gw_ai))
    after_start = lambda acc, t: acc + t
    zero_row = lambda token: jnp.tile(token[0:1], (1, dm // 128))
    l0_grads = [gw_ai, gw_ao.reshape(NDEV, aw // NDEV, dm)]
    pair_handles, pair_token = _exchange_start("l0_pair_start", l0_grads, _landing_zones("l0_pair_place", l0_grads, "pair"),
                                               "pair", dqkv)
    dhn0_z = _matmul("proj_z_dx", dz0, w_ai, "nt", F32, MM_T, MM_T, aw, b_koff=3 * aw, n_out=dm, epilogue=after_start,
                     ncols=(zero_row(pair_token),))
    from_sibling = _exchange_wait("l0_pair_wait", pair_handles, "pair", dhn0_z)
    chip_sums = [_pair_sum(f"l0_pair_sum_{a}", g, s) for a, (g, s) in enumerate(zip(l0_grads, from_sibling))]
    l0_handles, l0_token = _exchange_start("l0_grads_start", chip_sums, _landing_zones("l0_grads_place", chip_sums, "chips"),
                                           "chips", dhn0_z)
    dx0, g_nw0, dsc0, dsh0 = _norm_mod_bwd(
        "proj_qkv_dx_norm0_bwd", (dqkv, w_ai, "nt", aw, dict(n_out=dm)), x0, [dhn0_z], dx1,
        nw[0], scale[0], shift[0] + zero_row(l0_token))

    small_g = [_pack_small([dsh0, dsc0, dgate0, dsh1, dsc1, dgate1, g_nw0, g_nw1, g_fnw], g_dtb, [dalog_f, dalog_b], g_d, loss_part)]
    sm_handles, sm_token = _exchange_start("small_grads_start", small_g, _landing_zones("small_grads_place", small_g, "gather"),
                                           "gather", dx0)

    whole = (slice(None), slice(None))
    r_so, r_si, r_small = _exchange_wait("l1_grads_wait", l1_handles, "scatter", sm_token)
    si_out = [o.T for o in _adamw("adamw_ssd_w_in", ssd_w_in[0].T, r_si, m_ssd_w_in[0].T, v_ssd_w_in[0].T, n_ssd_in // NDEV, 256)]
    so_out = _adamw("adamw_ssd_w_out", ssd_w_out[0], r_so, m_ssd_w_out[0], v_ssd_w_out[0], 256)
    cw_cols = ssd_conv_w.shape[2]
    cw_out, cb_out, snw_out = _adamw_windows(
        "adamw_ssd_small", r_small,
        [(ssd_conv_w, m_ssd_conv_w, v_ssd_conv_w), (ssd_conv_b, m_ssd_conv_b, v_ssd_conv_b),
         (ssd_norm_w, m_ssd_norm_w, v_ssd_norm_w)],
        [(0, slice(0, CONV_WIDTH), slice(0, cw_cols), (0, slice(None), slice(None))),
         (1, slice(5, 6), slice(0, cw_cols), whole), (2, slice(6, 7), slice(0, si // NDEV), whole)])
    r_ai, r_ao = _exchange_wait("l0_grads_wait", l0_handles, "chips", so_out[0])
    ai_out = _adamw("adamw_attn_w_in", attn_w_in[0], r_ai, m_attn_w_in[0], v_attn_w_in[0], 256)
    ao_out = _adamw("adamw_attn_w_out", attn_w_out[0], r_ao, m_attn_w_out[0], v_attn_w_out[0], 192)

    (small_all,) = _exchange_wait("small_grads_wait", sm_handles, "gather", ai_out[0])
    full = slice(0, PACK_COLS)
    nhd = SSD_HEADS
    windows = [(0, slice(3 * i + k, 3 * i + k + 1), full, (slice(i, i + 1), slice(k * dm, (k + 1) * dm)))
               for i in range(2) for k in range(3)]
    windows += [(1, slice(6 + i, 7 + i), full, (slice(i, i + 1), slice(None))) for i in range(2)]
    windows += [(2, slice(8, 9), full, whole)]
    windows += [(3 + q, slice(9, 10), slice(2 * nhd * q + nhd * j, 2 * nhd * q + nhd * (j + 1)), (0, slice(j, j + 1), slice(None)))
                for q in range(2) for j in range(2)]
    windows += [(5, slice(9, 10), slice(4 * nhd, 5 * nhd), whole)]
    as_row = lambda a: a.reshape(1, dm)
    mb_out, nw_out, fnw_out, dtb_out, alog_out, d_out, loss = _adamw_windows(
        "adamw_small", small_all,
        [(mod_b, m_mod_b, v_mod_b), (norm_w, m_norm_w, v_norm_w), (fnw, as_row(m_final_norm_w), as_row(v_final_norm_w)),
         (ssd_dt_bias, m_ssd_dt_bias, v_ssd_dt_bias), (ssd_a_log, m_ssd_a_log, v_ssd_a_log), (ssd_d, m_ssd_d, v_ssd_d)],
        windows, extra=(slice(9, 10), slice(256, 257)))
    loss = loss.reshape(())

    ncol = mod_w.shape[2]
    dmod_all = small_all[:, 0:6, :].reshape(NDEV, 2, 3 * dm)
    dmod_sh = lax.dynamic_slice_in_dim(dmod_all, me * ncol, ncol, axis=2).transpose(1, 0, 2)
    g_modw = _mod_grad(c_all, dmod_sh).reshape(1, 2 * dm, ncol)
    modw_out = _adamw("adamw_mod_w", mod_w.reshape(2 * dm, ncol), g_modw, m_mod_w.reshape(2 * dm, ncol),
                      v_mod_w.reshape(2 * dm, ncol), 256)

    per_kind = []
    for k in range(4):
        per_kind.append([
            nw_out[k], modw_out[k].reshape(mod_w.shape), mb_out[k], ai_out[k][None], ao_out[k][None], si_out[k][None],
            cw_out[k], cb_out[k], dtb_out[k], alog_out[k], d_out[k], snw_out[k], so_out[k][None], fnw_out[k].reshape(dm)])
    return (loss, dx0.reshape(x.shape), *per_kind[0], *per_kind[1], *per_kind[2], *per_kind[3])


def _pack_ssd_small_blocks(g_cw, g_cb, g_nw):
    nper = g_cw.shape[1] // NDEV
    nwper = g_nw.shape[1] // NDEV

    def body(cw_ref, cb_ref, nw_ref, o_ref):
        o_ref[...] = jnp.zeros_like(o_ref)
        for d in range(NDEV):
            o_ref[d, 0:5, :] = cw_ref[:, d * nper:(d + 1) * nper]
            o_ref[d, 5:6, :] = cb_ref[:, d * nper:(d + 1) * nper]
            o_ref[d, 6:7, 0:nwper] = nw_ref[:, d * nwper:(d + 1) * nwper]

    return pl.pallas_call(body, name="pack_ssd_small_grads", out_shape=jax.ShapeDtypeStruct((NDEV, 8, nper), F32))(g_cw, g_cb, g_nw)
```

```python
import functools
import math

import jax
import jax.numpy as jnp
from jax import lax
from jax.experimental import pallas as pl
from jax.experimental.pallas import tpu as pltpu

F32 = jnp.float32
BF16 = jnp.bfloat16
HI = lax.Precision.HIGHEST
MESH = pl.DeviceIdType.MESH
NDEV = 8

NORM_EPS = 1e-6
ROPE_THETA = 500000.0
ROT_DIM = 16
HEAD_DIM = 64
DILATIONS = (1, 4, 16)
BAND = 64
NEG_BIG = -1e30
CHUNK = 128
SSD_HEADS = 32
SSD_GROUPS = 8
CONV_WIDTH = 5

ADAM_LR = 0.001
ADAM_B1 = 0.9
ADAM_B2 = 0.999
ADAM_EPS = 1e-08
ADAM_WD = 0.01
ADAM_STEP = 10

VMEM_BIG = 56 * 1024 * 1024
MM_T = 1024


def _params(sem=None, vmem=None):
    kw = {}
    if sem is not None:
        kw["dimension_semantics"] = sem
    if vmem is not None:
        kw["vmem_limit_bytes"] = vmem
    return pltpu.CompilerParams(**kw)


def _dg(a, b, ca, cb, prec=None):
    return lax.dot_general(a, b, (((ca,), (cb,)), ((), ())), preferred_element_type=F32, precision=prec)


def _nn(a, b):
    return _dg(a.astype(BF16), b.astype(BF16), 1, 0)


def _nt(a, b):
    return _dg(a.astype(BF16), b.astype(BF16), 1, 1)


def _tn(a, b):
    return _dg(a.astype(BF16), b.astype(BF16), 0, 0)


def _hnn(a, b):
    return _dg(a, b, 1, 0, HI)


@jax.custom_vjp
def _bnn(a, b):
    return _nn(a, b)


_bnn.defvjp(lambda a, b: (_nn(a, b), (a, b)), lambda r, g: (_nt(g, r[1]), _tn(r[0], g)))


@jax.custom_vjp
def _bnt(a, b):
    return _nt(a, b)


_bnt.defvjp(lambda a, b: (_nt(a, b), (a, b)), lambda r, g: (_nn(g, r[1]), _tn(g, r[0])))


@jax.custom_vjp
def _btn(a, b):
    return _tn(a, b)


_btn.defvjp(lambda a, b: (_tn(a, b), (a, b)), lambda r, g: (_nt(r[1], g), _nn(r[0], g)))


def _silu(x):
    return x * jax.nn.sigmoid(x)


def _matmul(name, a, b, mode, out_dtype, tm, tn, tk, *, epilogue=None, tiled=(), mrows=(), ncols=(),
            b_noff=0, b_koff=0, n_out=None, out_blocks=None, dest=None):
    if mode == "tn":
        K, M = a.shape
    else:
        M, K = a.shape
    N = n_out if n_out is not None else (b.shape[0] if mode == "nt" else b.shape[1])
    tm, tn, tk = min(tm, M), min(tn, N), min(tk, K)
    assert M % tm == 0 and N % tn == 0 and K % tk == 0, (name, M, N, K, tm, tn, tk)
    assert b_noff % tn == 0 and b_koff % tk == 0
    no, ko = b_noff // tn, b_koff // tk
    nk = K // tk
    if mode == "tn":
        a_spec = pl.BlockSpec((tk, tm), lambda i, j, k: (k, i))
    else:
        a_spec = pl.BlockSpec((tm, tk), lambda i, j, k: (i, k))
    if mode == "nt":
        b_spec = pl.BlockSpec((tn, tk), lambda i, j, k: (j + no, k + ko))
    else:
        b_spec = pl.BlockSpec((tk, tn), lambda i, j, k: (k + ko, j + no))
    specs = [a_spec, b_spec]
    specs += [pl.BlockSpec((tm, tn), lambda i, j, k: (i, j)) for _ in tiled]
    specs += [pl.BlockSpec((tm, r.shape[1]), lambda i, j, k: (i, 0)) for r in mrows]
    specs += [pl.BlockSpec((1, tn), lambda i, j, k: (0, j)) for _ in ncols]
    total, off, earlier = dest if dest is not None else (None, 0, None)
    if out_blocks is None:
        assert off % tm == 0
        mo = off // tm
        out_shape = jax.ShapeDtypeStruct((M if total is None else total, N), out_dtype)
        out_spec = pl.BlockSpec((tm, tn), lambda i, j, k: (i + mo, j))
    else:
        nper = N // out_blocks
        assert nper % tn == 0
        jb = nper // tn
        out_shape = jax.ShapeDtypeStruct((out_blocks if total is None else total, M, nper), out_dtype)
        out_spec = pl.BlockSpec((None, tm, tn), lambda i, j, k: (j // jb + off, i, j % jb))
    if earlier is not None:
        assert earlier.shape == out_shape.shape and earlier.dtype == out_shape.dtype
    ne = len(tiled) + len(mrows) + len(ncols)
    dot = {"nn": _nn, "nt": _nt, "tn": _tn}[mode]

    def body(a_ref, b_ref, *rest):
        extras, o_ref = rest[:ne], rest[ne]

        def finish(acc):
            if epilogue is not None:
                acc = epilogue(acc, *[e[...] for e in extras])
            o_ref[...] = acc.astype(o_ref.dtype)

        if nk == 1:
            finish(dot(a_ref[...], b_ref[...]))
        else:
            acc_ref = rest[ne + 1]
            k = pl.program_id(2)

            @pl.when(k == 0)
            def _():
                acc_ref[...] = jnp.zeros_like(acc_ref)

            acc_ref[...] += dot(a_ref[...], b_ref[...])

            @pl.when(k == nk - 1)
            def _():
                finish(acc_ref[...])

    args = [a, b, *tiled, *mrows, *ncols]
    aliases = {}
    if earlier is not None:
        specs.append(pl.BlockSpec(memory_space=pl.ANY))
        aliases = {len(args): 0}
        args.append(earlier)

    def body_with_dest(*refs):
        body(*refs[:2 + ne], *refs[2 + ne + (earlier is not None):])

    return pl.pallas_call(
        body_with_dest, name=name, out_shape=out_shape, grid=(M // tm, N // tn, nk),
        in_specs=specs, out_specs=out_spec, input_output_aliases=aliases,
        scratch_shapes=[] if nk == 1 else [pltpu.VMEM((tm, tn), F32)],
        compiler_params=_params(("parallel", "parallel", "arbitrary"), VMEM_BIG),
    )(*args)


def _matmul_rows(name, a, b, mode, tm, tk, fn, rows, consts, outs, accs, *, n_out=None, b_noff=0, b_koff=0):
    rl = [(t, t.shape[1], 0) if not isinstance(t, tuple) else t for t in rows]
    make_a = a if callable(a) else None
    M, K = (rl[0][0].shape[0], b.shape[1 if mode == "nt" else 0]) if make_a else a.shape
    N = n_out if n_out is not None else (b.shape[0] if mode == "nt" else b.shape[1])
    tm, tk = min(tm, M), min(tk, K)
    assert M % tm == 0 and K % tk == 0 and b_koff % tk == 0 and b_noff % N == 0, (name, M, N, K)
    no, ko, nk = b_noff // N, b_koff // tk, K // tk
    assert make_a is None or nk == 1
    nr, nc, no_, na = len(rl), len(consts), len(outs), len(accs)
    dot = _nt if mode == "nt" else _nn

    def body(*refs):
        a_ref, b_ref, rest = (None, refs[0], refs[1:]) if make_a else (refs[0], refs[1], refs[2:])
        r_refs, c_refs = rest[:nr], rest[nr:nr + nc]
        o_refs, acc_refs = rest[nr + nc:nr + nc + no_], rest[nr + nc + no_:nr + nc + no_ + na]
        i, k = pl.program_id(0), pl.program_id(1)

        def finish(prod, *made):
            res_o, res_a = fn(prod, *made, *[r[...] for r in r_refs], *[c[...] for c in c_refs])
            for r, v in zip(o_refs, res_o, strict=True):
                r[...] = v.astype(r.dtype)
            if acc_refs:
                @pl.when(i == 0)
                def _():
                    for r in acc_refs:
                        r[...] = jnp.zeros_like(r)

                for r, v in zip(acc_refs, res_a, strict=True):
                    r[...] += v

        if make_a:
            left = make_a(*[r[...] for r in r_refs], *[c[...] for c in c_refs])
            finish(dot(left, b_ref[...]), left)
        elif nk == 1:
            finish(dot(a_ref[...], b_ref[...]))
        else:
            prod_ref = rest[-1]

            @pl.when(k == 0)
            def _():
                prod_ref[...] = jnp.zeros_like(prod_ref)

            prod_ref[...] += dot(a_ref[...], b_ref[...])

            @pl.when(k == nk - 1)
            def _():
                finish(prod_ref[...])

    if mode == "nt":
        b_spec = pl.BlockSpec((N, tk), lambda i, k: (no, k + ko))
    else:
        b_spec = pl.BlockSpec((tk, N), lambda i, k: (k + ko, no))
    in_specs = ([] if make_a else [pl.BlockSpec((tm, tk), lambda i, k: (i, k))]) + [b_spec]
    in_specs += [pl.BlockSpec((tm, w), functools.partial(lambda i, k, cb: (i, cb), cb=cb)) for (_, w, cb) in rl]
    in_specs += [pl.BlockSpec(c.shape, lambda i, k: (0, 0)) for c in consts]
    out_specs = [pl.BlockSpec((tm, c), lambda i, k: (i, 0)) for (c, _) in outs]
    out_specs += [pl.BlockSpec(shp, lambda i, k: (0, 0)) for shp in accs]
    out_shape = [jax.ShapeDtypeStruct((M, c), dt) for (c, dt) in outs] + [jax.ShapeDtypeStruct(shp, F32) for shp in accs]
    res = pl.pallas_call(
        body, name=name, out_shape=out_shape, grid=(M // tm, nk), in_specs=in_specs, out_specs=out_specs,
        scratch_shapes=[] if nk == 1 else [pltpu.VMEM((tm, N), F32)],
        compiler_params=_params(("arbitrary" if accs else "parallel", "arbitrary"), VMEM_BIG),
    )(*([] if make_a else [a]), b, *[t[0] for t in rl], *consts)
    return res[:no_], res[no_:]


def _rowwise(name, fn, tiled, consts, outs, accs, ts):
    tl = [(t, t.shape[1], 0) if not isinstance(t, tuple) else t for t in tiled]
    s_len = tl[0][0].shape[0]
    assert s_len % ts == 0
    nt_, nc_, no_ = len(tl), len(consts), len(outs)

    def body(*refs):
        t_refs, c_refs = refs[:nt_], refs[nt_:nt_ + nc_]
        o_refs, a_refs = refs[nt_ + nc_:nt_ + nc_ + no_], refs[nt_ + nc_ + no_:]
        res_o, res_a = fn(*[r[...] for r in t_refs], *[r[...] for r in c_refs])
        for r, v in zip(o_refs, res_o, strict=True):
            r[...] = v.astype(r.dtype)
        if a_refs:
            @pl.when(pl.program_id(0) == 0)
            def _():
                for r in a_refs:
                    r[...] = jnp.zeros_like(r)

            for r, v in zip(a_refs, res_a, strict=True):
                r[...] += v

    in_specs = [pl.BlockSpec((ts, w), functools.partial(lambda i, cb: (i, cb), cb=cb)) for (_, w, cb) in tl]
    in_specs += [pl.BlockSpec(c.shape, lambda i: (0, 0)) for c in consts]
    out_specs = [pl.BlockSpec((ts, c), lambda i: (i, 0)) for (c, _) in outs]
    out_specs += [pl.BlockSpec(shp, lambda i: (0, 0)) for shp in accs]
    out_shape = [jax.ShapeDtypeStruct((s_len, c), dt) for (c, dt) in outs]
    out_shape += [jax.ShapeDtypeStruct(shp, F32) for shp in accs]
    res = pl.pallas_call(
        body, name=name, out_shape=out_shape, grid=(s_len // ts,), in_specs=in_specs, out_specs=out_specs,
        compiler_params=_params(("arbitrary",) if accs else ("parallel",), VMEM_BIG),
    )(*[t[0] for t in tl], *consts)
    return res[:no_], res[no_:]


def _norm_mod_fn(x, nw, sc, sh):
    r = lax.rsqrt(jnp.mean(x * x, axis=-1, keepdims=True) + NORM_EPS)
    return (x * r * nw) * (1.0 + sc) + sh


def _norm_mod_fwd(name, x, nw, sc, sh):
    (hn,), _ = _rowwise(name, lambda x, nw, sc, sh: ([_norm_mod_fn(x, nw, sc, sh)], []),
                        [x], [nw, sc, sh], [(x.shape[1], BF16)], [], 512)
    return hn


def _norm_mod_bwd(name, last, x, dhn_parts, dres, nw, sc, sh, prev=None):
    n = len(dhn_parts)
    d = x.shape[1]
    a, b, mode, tk, kw = last

    def fn(dhn, x, *rest):
        for p in rest[:n]:
            dhn = dhn + p
        dres, rest = rest[n], rest[n + 1:]
        y_prev, (nw, sc, sh), gate = (rest[0], rest[1:4], rest[4]) if prev is not None else (None, rest[0:3], None)
        r = lax.rsqrt(jnp.mean(x * x, axis=-1, keepdims=True) + NORM_EPS)
        xh = x * r
        dxh = dhn * (nw * (1.0 + sc))
        dx = r * (dxh - xh * jnp.mean(dxh * xh, axis=-1, keepdims=True)) + dres
        along = jnp.sum(dhn * xh, axis=0, keepdims=True)
        dnw, dsc, dsh = along * (1.0 + sc), along * nw, jnp.sum(dhn, axis=0, keepdims=True)
        if prev is None:
            return [dx], [dnw, dsc, dsh]
        return [dx, gate * dx], [dnw, dsc, dsh, jnp.sum(dx * y_prev, axis=0, keepdims=True)]

    rows = [x, *dhn_parts, dres] + ([prev[0]] if prev is not None else [])
    consts = [nw, sc, sh] + ([prev[1]] if prev is not None else [])
    outs = [(d, F32)] + ([(d, BF16)] if prev is not None else [])
    res_o, res_a = _matmul_rows(name, a, b, mode, 512, tk, fn, rows, consts, outs, [(1, d)] * (3 + (prev is not None)), **kw)
    return (*res_o, *res_a)


def _rope_tables(pos_col, inv_row):
    def fn(pos, inv):
        ang = pos.astype(F32) * inv
        e = lax.broadcasted_iota(jnp.int32, (1, 128), 1) % HEAD_DIM
        cos, sin = jnp.cos(ang), jnp.sin(ang)
        half = ROT_DIM // 2
        return [jnp.where(e < ROT_DIM, cos, 1.0), jnp.where(e < half, -sin, 0.0),
                jnp.where((e >= half) & (e < ROT_DIM), sin, 0.0)], []

    (c, sa, sb), _ = _rowwise("rope_tables", fn, [pos_col], [inv_row], [(128, F32)] * 3, [], 512)
    return c, sa, sb


def _rot_fwd(t, c, sa, sb):
    n = t.shape[1]
    rep = n // 128
    c, sa, sb = (jnp.tile(u, (1, rep)) for u in (c, sa, sb))
    return t * c + pltpu.roll(t, n - ROT_DIM // 2, 1) * sa + pltpu.roll(t, ROT_DIM // 2, 1) * sb


def _rot_bwd(g, c, sa, sb):
    n = g.shape[1]
    rep = n // 128
    c, sa, sb = (jnp.tile(u, (1, rep)) for u in (c, sa, sb))
    return g * c + pltpu.roll(g * sa, ROT_DIM // 2, 1) + pltpu.roll(g * sb, n - ROT_DIM // 2, 1)


ATT_TQ = 128


def _attn_tiles(l):
    tk = ATT_TQ + 2 * BAND
    return (l, l) if l <= tk else (ATT_TQ, tk)


def _attn_specs(g, s_len):
    def blk(off):
        return pl.BlockSpec((s_len, 128), functools.partial(lambda hp, off: (0, off + hp), off=off))

    return blk(4 * g), blk(12 + 4 * g), blk(4 * g), blk(0)


def _attn_tile_geometry(t, d, l):
    tq, tk = _attn_tiles(l)
    nts = l // tq
    r = t // nts
    ts = t % nts
    q0 = ts * tq
    ws = jnp.clip(q0 - BAND, 0, l - tk)
    kind = jnp.where(ts == 0, 0, jnp.where(ts == nts - 1, 2, 1))
    if d == 1:
        return pl.ds(pl.multiple_of(q0, tq), tq), pl.ds(pl.multiple_of(ws, BAND), tk), kind
    return pl.ds(r + d * q0, tq, stride=d), pl.ds(r + d * ws, tk, stride=d), kind


def _attn_fill_bias(bias_ref):
    _, tq2, tk = bias_ref.shape
    iq = lax.broadcasted_iota(jnp.int32, (tq2, 1), 0) % (tq2 // 2)
    ik = lax.broadcasted_iota(jnp.int32, (1, tk), 1)
    for i, off in enumerate((0, -BAND, -2 * BAND)):
        bias_ref[i] = jnp.where(jnp.abs(ik + off - iq) <= BAND, 0.0, NEG_BIG)


def _split_heads(t, in_h):
    zero = jnp.zeros_like(t)
    return jnp.concatenate([jnp.where(in_h[0], t, zero), jnp.where(in_h[1], t, zero)], axis=0)


def _attn_fwd(g, qk, v):
    s_len = qk.shape[0]
    d = DILATIONS[g]
    l = s_len // d
    tq, tk = _attn_tiles(l)
    assert l % tq == 0 and l >= tk
    q_spec, k_spec, v_spec, o_spec = _attn_specs(g, s_len)
    scale = 1.0 / math.sqrt(HEAD_DIM)

    def body(q_ref, k_ref, v_ref, o_ref, lse_ref, bias_ref):
        lane = lax.broadcasted_iota(jnp.int32, (1, 128), 1)
        in_h = [lane < HEAD_DIM, lane >= HEAD_DIM]
        _attn_fill_bias(bias_ref)

        def tile(t, carry):
            rows, win, kind = _attn_tile_geometry(t, d, l)
            q = (q_ref[rows, :] * scale).astype(BF16)
            k = k_ref[win, :].astype(BF16)
            vv = v_ref[win, :].astype(BF16)
            s = _nt(_split_heads(q, in_h), k) + bias_ref[kind]
            m = jnp.max(s, axis=1, keepdims=True)
            p = jnp.exp(s - m)
            den = jnp.sum(p, axis=1, keepdims=True)
            out = _nn(p, vv) / den
            lse = m + jnp.log(den)
            o_ref[rows, :] = jnp.where(in_h[0], out[:tq], out[tq:])
            lse_ref[rows, :] = jnp.where(in_h[0], lse[:tq], lse[tq:])
            return carry

        lax.fori_loop(0, s_len // tq, tile, 0, unroll=4 * ATT_TQ // tq)

    return pl.pallas_call(
        body, name=f"attn_fwd_g{g}", grid=(4,),
        out_shape=[jax.ShapeDtypeStruct((s_len, 512), F32)] * 2,
        in_specs=[q_spec, k_spec, v_spec], out_specs=[o_spec, o_spec],
        scratch_shapes=[pltpu.VMEM((3, 2 * tq, tk), F32)],
        compiler_params=_params(("parallel",), VMEM_BIG),
    )(qk, qk, v)


def _attn_bwd(g, qk, v, o, lse, do, dlse):
    s_len = qk.shape[0]
    d = DILATIONS[g]
    l = s_len // d
    tq, tk = _attn_tiles(l)
    q_spec, k_spec, v_spec, o_spec = _attn_specs(g, s_len)
    scale = 1.0 / math.sqrt(HEAD_DIM)

    def body(q_ref, k_ref, v_ref, o_ref, lse_ref, do_ref, dlse_ref, dq_ref, dk_ref, dv_ref, bias_ref):
        lane = lax.broadcasted_iota(jnp.int32, (1, 128), 1)
        in_h = [lane < HEAD_DIM, lane >= HEAD_DIM]
        dk_ref[...] = jnp.zeros_like(dk_ref)
        dv_ref[...] = jnp.zeros_like(dv_ref)
        _attn_fill_bias(bias_ref)

        def tile(t, carry):
            rows, win, kind = _attn_tile_geometry(t, d, l)
            k, vv = k_ref[win, :].astype(BF16), v_ref[win, :].astype(BF16)
            dout, lse_t, dlse_t = do_ref[rows, :], lse_ref[rows, :], dlse_ref[rows, :]
            od = dout * o_ref[rows, :]
            q2 = _split_heads((q_ref[rows, :] * scale).astype(BF16), in_h)
            do2 = _split_heads(dout.astype(BF16), in_h)
            head_col = lambda a: jnp.concatenate([a[:, 0:1], a[:, HEAD_DIM:HEAD_DIM + 1]], axis=0)
            delta = jnp.concatenate([jnp.sum(jnp.where(m, od, 0.0), axis=1, keepdims=True) for m in in_h], axis=0)
            p = jnp.exp(_nt(q2, k) + bias_ref[kind] - head_col(lse_t))
            ds = (p * (_nt(do2, vv) - delta + head_col(dlse_t))).astype(BF16)
            dq2 = _nn(ds, k) * scale
            dq_ref[rows, :] = jnp.where(in_h[0], dq2[:tq], dq2[tq:])
            dk_ref[win, :] += _tn(ds, q2)
            dv_ref[win, :] += _tn(p, do2)
            return carry

        lax.fori_loop(0, s_len // tq, tile, 0, unroll=4 * ATT_TQ // tq)

    return pl.pallas_call(
        body, name=f"attn_bwd_g{g}", grid=(4,),
        out_shape=[jax.ShapeDtypeStruct((s_len, 512), F32)] * 3,
        in_specs=[q_spec, k_spec, v_spec, o_spec, o_spec, o_spec, o_spec], out_specs=[o_spec] * 3,
        scratch_shapes=[pltpu.VMEM((3, 2 * tq, tk), F32)],
        compiler_params=_params(("parallel",), VMEM_BIG),
    )(qk, qk, v, o, lse, do, dlse)


def _mix_weights(ls):
    mx = jnp.maximum(jnp.maximum(ls[0], ls[1]), ls[2])
    es = [jnp.exp(x - mx) for x in ls]
    tot = es[0] + es[1] + es[2]
    return [e / tot for e in es]


def _attn_out(os_, lses, z, x, gate, w_out):
    s_len, dm = x.shape
    tm = 256
    wdt = 512
    z, z_block = z

    def body(o0, o1, o2, l0, l1, l2, z_ref, x_ref, g_ref, w_ref, a_ref, y_ref, x1_ref):
        alphas = _mix_weights([l0[...], l1[...], l2[...]])
        y = jnp.zeros((tm, dm), F32)
        for g, o_ref in enumerate((o0, o1, o2)):
            a_g = (o_ref[...] * alphas[g] * _silu(z_ref[:, g * wdt:(g + 1) * wdt])).astype(BF16)
            a_ref[:, g * wdt:(g + 1) * wdt] = a_g
            y = y + _nn(a_g, w_ref[g * wdt:(g + 1) * wdt, :])
        y_ref[...] = y
        x1_ref[...] = x_ref[...] + g_ref[...] * y

    row = lambda c: pl.BlockSpec((tm, c), lambda i: (i, 0))
    return pl.pallas_call(
        body, name="attn_out", grid=(s_len // tm,),
        out_shape=[jax.ShapeDtypeStruct((s_len, 3 * wdt), BF16), jax.ShapeDtypeStruct((s_len, dm), F32),
                   jax.ShapeDtypeStruct((s_len, dm), F32)],
        in_specs=[row(wdt)] * 6 + [pl.BlockSpec((tm, 3 * wdt), lambda i: (i, z_block)), row(dm),
                                   pl.BlockSpec((1, dm), lambda i: (0, 0)), pl.BlockSpec(w_out.shape, lambda i: (0, 0))],
        out_specs=[row(3 * wdt), row(dm), row(dm)],
        compiler_params=_params(("parallel",), VMEM_BIG),
    )(*os_, *lses, z, x, gate, w_out)


def _mix_bwd(dy, w_out, os_, lses, z):
    wdt = 512

    def fn(da, o0, o1, o2, l0, l1, l2, z):
        os_t, ls = [o0, o1, o2], [l0, l1, l2]
        alphas = _mix_weights(ls)
        hi = lax.broadcasted_iota(jnp.int32, (2 * wdt, wdt), 0) % wdt // HEAD_DIM
        hj = lax.broadcasted_iota(jnp.int32, (2 * wdt, wdt), 1) // HEAD_DIM
        seg = (hi == hj).astype(BF16)
        head_sum = lambda t: _dg(jnp.concatenate(_bf16_parts(t, 2), axis=1), seg, 1, 0)
        dos, dal, dzs = [], [], []
        for g in range(3):
            zg = z[:, g * wdt:(g + 1) * wdt]
            sig = jax.nn.sigmoid(zg)
            dag = da[:, g * wdt:(g + 1) * wdt]
            dmix = dag * zg * sig
            dzs.append(dag * os_t[g] * alphas[g] * (sig * (1.0 + zg * (1.0 - sig))))
            dos.append(dmix * alphas[g])
            dal.append(head_sum(dmix * os_t[g]))
        mean = alphas[0] * dal[0] + alphas[1] * dal[1] + alphas[2] * dal[2]
        dls = [alphas[g] * (dal[g] - mean) for g in range(3)]
        return dos + dls + [jnp.concatenate(dzs, axis=1)], []

    outs, _ = _matmul_rows("attn_out_dx_mix_bwd", dy, w_out, "nt", 256, dy.shape[1], fn, [*os_, *lses, (z[0], 3 * wdt, z[1])], [],
                           [(wdt, F32)] * 6 + [(3 * wdt, BF16)], [])
    return outs[:3], outs[3:6], outs[6]


def _rot_pack_bwd(dqs, dks, dvs, tabs):
    wdt = 512

    def fn(*args):
        grads, (c, sa, sb) = args[:9], args[9:]
        cols = [_rot_bwd(gq, c, sa, sb) for gq in grads[:6]] + list(grads[6:])
        return [jnp.concatenate(cols, axis=1)], []

    (out,), _ = _rowwise("rot_pack_bwd", fn, [*dqs, *dks, *dvs, *tabs], [], [(9 * wdt, BF16)], [], 256)
    return out


CONV_CB = 128
CONV_R = 256
CONV_PAD = 8


def _conv_taps(buf, base, off, sign):
    return [buf[pl.ds(base + off + sign * j, CONV_R), :] for j in range(CONV_WIDTH)]


def _conv_tap_sum(taps, w):
    acc = None
    for j, t in enumerate(taps):
        term = t * w[j:j + 1, :]
        acc = term if acc is None else acc + term
    return acc


def _conv_fwd(xpre, cw, cb):
    s_len, ch = xpre.shape
    nchunk = s_len // CONV_R

    def body(x_ref, w_ref, b_ref, o_ref, xp):
        zero = jnp.zeros((CONV_PAD, CONV_CB), F32)
        xp[0:CONV_PAD, :] = zero
        xp[s_len + CONV_PAD:s_len + 2 * CONV_PAD, :] = zero

        def fill(ci, carry):
            base = pl.multiple_of(ci * CONV_R, CONV_R)
            xp[pl.ds(base + CONV_PAD, CONV_R), :] = x_ref[pl.ds(base, CONV_R), :]
            return carry

        lax.fori_loop(0, nchunk, fill, 0)
        w = w_ref[...]
        b = b_ref[...]

        def chunk(ci, carry):
            base = pl.multiple_of(ci * CONV_R, CONV_R)
            u = _conv_tap_sum(_conv_taps(xp, base, CONV_PAD - CONV_WIDTH // 2, 1), w) + b
            o_ref[pl.ds(base, CONV_R), :] = _silu(u)
            return carry

        lax.fori_loop(0, nchunk, chunk, 0, unroll=2)

    col = lambda r: pl.BlockSpec((r, CONV_CB), lambda j: (0, j))
    return pl.pallas_call(
        body, name="conv_fwd", grid=(ch // CONV_CB,), out_shape=jax.ShapeDtypeStruct((s_len, ch), F32),
        in_specs=[col(s_len), col(CONV_WIDTH), col(1)], out_specs=col(s_len),
        scratch_shapes=[pltpu.VMEM((s_len + 2 * CONV_PAD, CONV_CB), F32)],
        compiler_params=_params(("parallel",), VMEM_BIG),
    )(xpre, cw, cb)


def _conv_bwd(xpre, da, cw, cb):
    s_len, ch = xpre.shape
    nchunk = s_len // CONV_R
    half = CONV_WIDTH // 2

    def body(x_ref, da_ref, w_ref, b_ref, dx_ref, gw_ref, gb_ref, xp, dcp):
        zero = jnp.zeros((CONV_PAD, CONV_CB), F32)
        for buf in (xp, dcp):
            buf[0:CONV_PAD, :] = zero
            buf[s_len + CONV_PAD:s_len + 2 * CONV_PAD, :] = zero

        def fill(ci, carry):
            base = pl.multiple_of(ci * CONV_R, CONV_R)
            xp[pl.ds(base + CONV_PAD, CONV_R), :] = x_ref[pl.ds(base, CONV_R), :]
            return carry

        lax.fori_loop(0, nchunk, fill, 0)
        w = w_ref[...]
        b = b_ref[...]

        def first(ci, carry):
            base = pl.multiple_of(ci * CONV_R, CONV_R)
            taps = _conv_taps(xp, base, CONV_PAD - half, 1)
            u = _conv_tap_sum(taps, w) + b
            sig = jax.nn.sigmoid(u)
            dc = da_ref[pl.ds(base, CONV_R), :] * (sig * (1.0 + u * (1.0 - sig)))
            dcp[pl.ds(base + CONV_PAD, CONV_R), :] = dc
            gb = carry[0] + jnp.sum(dc, axis=0, keepdims=True)
            gws = [carry[1 + j] + jnp.sum(dc * taps[j], axis=0, keepdims=True) for j in range(CONV_WIDTH)]
            return (gb, *gws)

        z1 = jnp.zeros((1, CONV_CB), F32)
        sums = lax.fori_loop(0, nchunk, first, (z1,) * (1 + CONV_WIDTH), unroll=2)
        gb_ref[...] = sums[0]
        for j in range(CONV_WIDTH):
            gw_ref[j:j + 1, :] = sums[1 + j]

        def second(ci, carry):
            base = pl.multiple_of(ci * CONV_R, CONV_R)
            dx_ref[pl.ds(base, CONV_R), :] = _conv_tap_sum(_conv_taps(dcp, base, CONV_PAD + half, -1), w).astype(dx_ref.dtype)
            return carry

        lax.fori_loop(0, nchunk, second, 0, unroll=2)

    col = lambda r: pl.BlockSpec((r, CONV_CB), lambda j: (0, j))
    return pl.pallas_call(
        body, name="conv_bwd", grid=(ch // CONV_CB,),
        out_shape=[jax.ShapeDtypeStruct((s_len, ch), BF16), jax.ShapeDtypeStruct((CONV_WIDTH, ch), F32),
                   jax.ShapeDtypeStruct((1, ch), F32)],
        in_specs=[col(s_len), col(s_len), col(CONV_WIDTH), col(1)],
        out_specs=[col(s_len), col(CONV_WIDTH), col(1)],
        scratch_shapes=[pltpu.VMEM((s_len + 2 * CONV_PAD, CONV_CB), F32)] * 2,
        compiler_params=_params(("parallel",), VMEM_BIG),
    )(xpre, da, cw, cb)


SSD_GW = 256
SSD_N = 128
SSD_DTW = 128


def _bf16_parts(x, n):
    parts, rest = [], x
    for _ in range(n):
        p = rest.astype(BF16)
        parts.append(p)
        rest = rest - p.astype(F32)
    return parts


@jax.custom_vjp
def _expand(x, e):
    eb = e.astype(BF16)
    return _dg(jnp.concatenate(_bf16_parts(x, 2), axis=1), jnp.concatenate([eb, eb], axis=0), 1, 0)


def _expand_fwd(x, e):
    return _expand(x, e), e


def _expand_bwd(e, g):
    return _dg(g.astype(BF16), e.astype(BF16), 1, 1), jnp.zeros_like(e)


_expand.defvjp(_expand_fwd, _expand_bwd)


@jax.custom_vjp
def _running_sum(tri, x):
    tb = tri.astype(BF16)
    return sum(_dg(tb, p, 1, 0) for p in _bf16_parts(x, 3))


def _running_sum_fwd(tri, x):
    return _running_sum(tri, x), tri


def _running_sum_bwd(tri, g):
    tb = tri.astype(BF16)
    return jnp.zeros_like(tri), sum(_dg(tb, p, 0, 0) for p in _bf16_parts(g, 3))


_running_sum.defvjp(_running_sum_fwd, _running_sum_bwd)


def _pick_col(a, h):
    @jax.custom_vjp
    def pick(a):
        return a[:, h:h + 1]

    pick.defvjp(lambda a: (a[:, h:h + 1], None),
                lambda _, g: (g * (lax.broadcasted_iota(jnp.int32, (1, a.shape[1]), 1) == h).astype(F32),))
    return pick(a)


def _pick_row(a, h):
    @jax.custom_vjp
    def pick(a):
        return a[h:h + 1, :]

    pick.defvjp(lambda a: (a[h:h + 1, :], None),
                lambda _, g: (g * (lax.broadcasted_iota(jnp.int32, (a.shape[0], 1), 0) == h).astype(F32),))
    return pick(a)


def _ssd_mask(dirn):
    ri = lax.broadcasted_iota(jnp.int32, (CHUNK, CHUNK), 0)
    cj = lax.broadcasted_iota(jnp.int32, (CHUNK, CHUNK), 1)
    return (cj <= ri) if dirn == 0 else (cj >= ri)


def _ssd_rowsel(dirn):
    last = CHUNK - 1 if dirn == 0 else 0
    return (lax.broadcasted_iota(jnp.int32, (CHUNK, 1), 0) == last).astype(F32)


def _ssd_chunk_pre(dirn):
    nh = SSD_DTW

    def f(dt, alog):
        da = dt * (-jnp.exp(alog))
        cum = _running_sum(_ssd_mask(dirn).astype(F32), da)
        tot = jnp.sum(cum * _ssd_rowsel(dirn), axis=0, keepdims=True)
        hh = lax.broadcasted_iota(jnp.int32, (nh, SSD_HEADS * HEAD_DIM), 0)
        jj = lax.broadcasted_iota(jnp.int32, (nh, SSD_HEADS * HEAD_DIM), 1)
        expand = (hh == dirn * SSD_HEADS + jj // HEAD_DIM).astype(F32)
        return cum, cum.T, _expand(dt, expand), _expand(jnp.exp(tot - cum), expand), _expand(jnp.exp(cum), expand)

    return f


def _ssd_group_fn(g, dirn, stacked):
    def f(xs, bm, cm, st, cum, cum_t, dt_e, w_e, ce_e):
        mask = _ssd_mask(dirn)
        xdt = xs * dt_e
        cd_e = jnp.sum(ce_e * _ssd_rowsel(dirn), axis=0, keepdims=True)
        cb = _bnt(cm, bm)
        lane_head = lax.broadcasted_iota(jnp.int32, (1, SSD_GW), 1) // HEAD_DIM
        y = _bnn(cm, st) * ce_e
        decayed, inputs = [], []
        for j in range(4):
            hidx = dirn * SSD_HEADS + 4 * g + j
            col, row = _pick_col(cum, hidx), _pick_row(cum_t, hidx)
            dec = cb * jnp.exp(jnp.where(mask, col - row, NEG_BIG))
            head = (lane_head == j).astype(F32)
            if stacked:
                decayed.append(dec)
                inputs.append(xdt * head)
            else:
                y = y + _bnn(dec, xdt) * head
        if stacked:
            y = y + _bnn(jnp.concatenate(decayed, axis=1), jnp.concatenate(inputs, axis=0))
        st_out = st * cd_e + _btn(bm, xdt * w_e)
        return y, st_out

    return f


def _ssd_in_specs(kk):
    ln = CHUNK
    return [pl.BlockSpec((ln, 2048), lambda i: (kk(i), 0)),
            pl.BlockSpec((ln, 1024), lambda i: (kk(i), 2)),
            pl.BlockSpec((ln, 1024), lambda i: (kk(i), 3)),
            pl.BlockSpec((ln, SSD_DTW), lambda i: (kk(i), 0)),
            pl.BlockSpec((1, SSD_DTW), lambda i: (0, 0))]


def _ssd_fwd(xbc, dt, alog, dirn):
    s_len = xbc.shape[0]
    nc = s_len // CHUNK
    kk = (lambda i: i) if dirn == 0 else (lambda i: nc - 1 - i)

    def body(x_ref, b_ref, c_ref, dt_ref, al_ref, y_ref, sts_ref, st):
        @pl.when(pl.program_id(0) == 0)
        def _():
            st[...] = jnp.zeros_like(st)

        sts_ref[0] = st[...]
        cum, cum_t, dt_e, w_e, ce_e = _ssd_chunk_pre(dirn)(dt_ref[...], al_ref[...])
        for g in range(SSD_GROUPS):
            xc = slice(g * SSD_GW, (g + 1) * SSD_GW)
            gc = slice(g * SSD_N, (g + 1) * SSD_N)
            y, st_new = _ssd_group_fn(g, dirn, True)(x_ref[:, xc], b_ref[:, gc], c_ref[:, gc], st[:, xc], cum, cum_t,
                                               dt_e[:, xc], w_e[:, xc], ce_e[:, xc])
            y_ref[:, xc] = y
            st[:, xc] = st_new

    return pl.pallas_call(
        body, name=f"ssd_fwd_d{dirn}", grid=(nc,),
        out_shape=[jax.ShapeDtypeStruct((s_len, 2048), F32), jax.ShapeDtypeStruct((nc, SSD_N, 2048), F32)],
        in_specs=_ssd_in_specs(kk),
        out_specs=[pl.BlockSpec((CHUNK, 2048), lambda i: (kk(i), 0)),
                   pl.BlockSpec((1, SSD_N, 2048), lambda i: (kk(i), 0, 0))],
        scratch_shapes=[pltpu.VMEM((SSD_N, 2048), F32)],
        compiler_params=_params(("arbitrary",), VMEM_BIG),
    )(xbc, xbc, xbc, dt, alog)


def _ssd_bwd(xbc, dt, alog, states, dy, d_e, dirn, prior=None):
    s_len = xbc.shape[0]
    nc = s_len // CHUNK
    kk = (lambda i: nc - 1 - i) if dirn == 0 else (lambda i: i)

    def body(x_ref, b_ref, c_ref, dt_ref, al_ref, sts_ref, dy_ref, de_ref, *rest):
        prior_ref = rest[0] if prior is not None else None
        dx_ref, ddt_ref, dal_ref, dst = rest[prior is not None:]
        plus_prior = (lambda v, cols: v + prior_ref[:, cols]) if prior is not None else (lambda v, cols: v)

        @pl.when(pl.program_id(0) == 0)
        def _():
            dst[...] = jnp.zeros_like(dst)
            dal_ref[...] = jnp.zeros_like(dal_ref)

        (cum, cum_t, dt_e, w_e, ce_e), pre_vjp = jax.vjp(_ssd_chunk_pre(dirn), dt_ref[...], al_ref[...])
        dcum = jnp.zeros_like(cum)
        dcum_t = jnp.zeros_like(cum_t)
        d_dt_e, d_w_e, d_ce_e = [], [], []
        for g in range(SSD_GROUPS):
            xc = slice(g * SSD_GW, (g + 1) * SSD_GW)
            gc = slice(g * SSD_N, (g + 1) * SSD_N)
            _, vjp = jax.vjp(_ssd_group_fn(g, dirn, False), x_ref[:, xc], b_ref[:, gc], c_ref[:, gc], sts_ref[0, :, xc], cum, cum_t,
                             dt_e[:, xc], w_e[:, xc], ce_e[:, xc])
            dyg = dy_ref[:, xc]
            dxs, dbm, dcm, dst_g, dcum_g, dcum_t_g, ddte_g, dwe_g, dcee_g = vjp((dyg, dst[:, xc]))
            if dirn == 0:
                dxs = dxs + dyg * de_ref[:, xc]
            bc, cc = slice(2048 + g * SSD_N, 2048 + (g + 1) * SSD_N), slice(3072 + g * SSD_N, 3072 + (g + 1) * SSD_N)
            dx_ref[:, xc] = plus_prior(dxs, xc)
            dx_ref[:, bc] = plus_prior(dbm, bc)
            dx_ref[:, cc] = plus_prior(dcm, cc)
            dst[:, xc] = dst_g
            dcum = dcum + dcum_g
            dcum_t = dcum_t + dcum_t_g
            d_dt_e.append(ddte_g)
            d_w_e.append(dwe_g)
            d_ce_e.append(dcee_g)
        ddt, dal = pre_vjp((dcum, dcum_t, jnp.concatenate(d_dt_e, axis=1), jnp.concatenate(d_w_e, axis=1),
                            jnp.concatenate(d_ce_e, axis=1)))
        ddt_ref[...] = ddt
        dal_ref[...] += dal

    return pl.pallas_call(
        body, name=f"ssd_bwd_d{dirn}", grid=(nc,),
        out_shape=[jax.ShapeDtypeStruct((s_len, 4096), F32), jax.ShapeDtypeStruct((s_len, SSD_DTW), F32),
                   jax.ShapeDtypeStruct((1, SSD_DTW), F32)],
        in_specs=_ssd_in_specs(kk) + [pl.BlockSpec((1, SSD_N, 2048), lambda i: (kk(i), 0, 0)),
                                      pl.BlockSpec((CHUNK, 2048), lambda i: (kk(i), 0)),
                                      pl.BlockSpec((1, 2048), lambda i: (0, 0))]
        + ([pl.BlockSpec((CHUNK, 4096), lambda i: (kk(i), 0))] if prior is not None else []),
        out_specs=[pl.BlockSpec((CHUNK, 4096), lambda i: (kk(i), 0)),
                   pl.BlockSpec((CHUNK, SSD_DTW), lambda i: (kk(i), 0)),
                   pl.BlockSpec((1, SSD_DTW), lambda i: (0, 0))],
        scratch_shapes=[pltpu.VMEM((SSD_N, 2048), F32)],
        compiler_params=_params(("arbitrary",), VMEM_BIG),
    )(xbc, xbc, xbc, dt, alog, states, dy, d_e, *([prior] if prior is not None else []))


def _gate_norm_fn(yf, yb, xs, z, d_e, nw):
    yg = (yf + yb + xs * d_e) * _silu(z)
    return yg * lax.rsqrt(jnp.mean(yg * yg, axis=-1, keepdims=True) + NORM_EPS) * nw


def _gate_norm_bwd(dy, w_out, yf, yb, xbc, z, d_e, nw):
    def fn(du, yf, yb, xs, z, d_e, nw):
        sig = jax.nn.sigmoid(z)
        gate = z * sig
        ysum = yf + yb + xs * d_e
        yg = ysum * gate
        r = lax.rsqrt(jnp.mean(yg * yg, axis=-1, keepdims=True) + NORM_EPS)
        t = du * nw
        dyg = t * r - yg * (jnp.mean(t * yg, axis=-1, keepdims=True) * (r * r * r))
        dys = dyg * gate
        dz = dyg * ysum * (sig * (1.0 + z * (1.0 - sig)))
        dnw = jnp.sum(du * yg * r, axis=0, keepdims=True)
        dde = jnp.sum(dys * xs, axis=0, keepdims=True)
        hh = lax.broadcasted_iota(jnp.int32, (2048, SSD_HEADS), 0) // HEAD_DIM
        jj = lax.broadcasted_iota(jnp.int32, (2048, SSD_HEADS), 1)
        return [dys, dz], [dnw, _hnn(jnp.broadcast_to(dde, (8, 2048)), (hh == jj).astype(F32))[0:1]]

    (dys, dz), (g_nw, g_d) = _matmul_rows("ssd_out_dx_gate_norm_bwd", dy, w_out, "nt", 256, dy.shape[1], fn,
                                          [yf, yb, (xbc, 2048, 0), z], [d_e, nw], [(2048, F32), (2048, BF16)],
                                          [(1, 2048), (1, SSD_HEADS)])
    return dys, dz, g_nw, g_d


def _ssd_tail_loss(yf, yb, xbc, z, d_e, snw, w_out, x1, tgt, gate, fnw):
    dm = x1.shape[1]
    si = yf.shape[1]

    def make_u(yf, yb, xs, z, x1, tgt, d_e, snw, gate, fnw):
        return _gate_norm_fn(yf, yb, xs, z, d_e, snw).astype(BF16)

    def fn(y1, u, yf, yb, xs, z, x1, tgt, d_e, snw, gate, fnw):
        x2 = x1 + gate * y1
        r = lax.rsqrt(jnp.mean(x2 * x2, axis=-1, keepdims=True) + NORM_EPS)
        xh = x2 * r
        err = xh * fnw - tgt
        loss = 0.5 * jnp.sum(jnp.mean(err * err, axis=-1, keepdims=True), axis=0, keepdims=True)
        dy = err * (1.0 / dm)
        dxh = dy * fnw
        dx2 = r * (dxh - xh * jnp.mean(dxh * xh, axis=-1, keepdims=True))
        dfnw = jnp.sum(dy * xh, axis=0, keepdims=True)
        return [u, dx2, gate * dx2], [dfnw, jnp.sum(dx2 * y1, axis=0, keepdims=True), jnp.broadcast_to(loss, (1, 128))]

    (u, dx2, dy1), (g_fnw, dgate, loss) = _matmul_rows(
        "ssd_out_loss", make_u, w_out, "nn", 256, si, fn, [yf, yb, (xbc, si, 0), z, x1, tgt], [d_e, snw, gate, fnw],
        [(si, BF16), (dm, F32), (dm, BF16)], [(1, dm), (1, dm), (1, 128)])
    return u, dx2, dy1, g_fnw, dgate, loss


def _softplus_fwd(dt_raw, bias):
    (dt,), _ = _rowwise("dt_softplus", lambda r, b: ([jax.nn.softplus(r + b)], []), [dt_raw], [bias],
                        [(dt_raw.shape[1], F32)], [], 512)
    return dt


def _softplus_bwd(ddt_f, ddt_b, dt_raw, bias):
    def fn(df, db, r, b):
        g = (df + db) * jax.nn.sigmoid(r + b)
        return [g], [jnp.sum(g, axis=0, keepdims=True)]

    w = dt_raw.shape[1]
    (g,), (gb,) = _rowwise("dt_softplus_bwd", fn, [ddt_f, ddt_b, dt_raw], [bias], [(w, BF16)], [(1, w)], 512)
    return g, gb


def _whole(a):
    nd = len(a.shape)
    return pl.BlockSpec(a.shape, lambda *_: (0,) * nd)


def _mod_part(c_all, mod_w):
    nl, _, ncol = mod_w.shape
    nb = c_all.shape[0]

    def body(c_ref, w_ref, o_ref):
        cond = _silu(c_ref[...])
        for i in range(nl):
            o_ref[i * nb:(i + 1) * nb, :] = _nn(cond, w_ref[i])

    return pl.pallas_call(body, name="mod_part", out_shape=jax.ShapeDtypeStruct((nl * nb, ncol), F32),
                          compiler_params=_params(None, VMEM_BIG))(c_all, mod_w)


def _mod_finish(mod_nb, mod_b, norm_w, tokens):
    nl, dm = norm_w.shape

    def body(a_ref, b_ref, nw_ref, *rest):
        tok_refs, o_refs = rest[:len(tokens)], rest[len(tokens):]
        tok = sum(t[0:1, 0:1] for t in tok_refs)
        for i in range(nl):
            for k in range(3):
                cols = slice(k * dm, (k + 1) * dm)
                o_refs[4 * i + k][...] = a_ref[i:i + 1, cols] + b_ref[i:i + 1, cols]
            o_refs[4 * i + 3][...] = nw_ref[i:i + 1, :] + tok

    rows = pl.pallas_call(body, name="mod_finish", out_shape=[jax.ShapeDtypeStruct((1, dm), F32)] * (4 * nl))(
        mod_nb, mod_b, norm_w, *tokens)
    return [rows[4 * i:4 * i + 4] for i in range(nl)]


def _mod_grad(c_all, dmod_sh):
    nl, nb, ncol = dmod_sh.shape
    dm = c_all.shape[1]

    def body(c_ref, d_ref, o_ref):
        cond = _silu(c_ref[...])
        for i in range(nl):
            o_ref[i] = _tn(cond, d_ref[i])

    return pl.pallas_call(body, name="mod_grad", out_shape=jax.ShapeDtypeStruct((nl, dm, ncol), F32),
                          compiler_params=_params(None, VMEM_BIG))(c_all, dmod_sh)


PACK_ROWS = 16
PACK_COLS = 1024


def _pack_small(rows, b64, a64s, d32, extra):
    nr, na = len(rows), len(a64s)

    def body(*refs):
        o_ref = refs[-1]
        o_ref[...] = jnp.zeros_like(o_ref)
        for i in range(nr):
            o_ref[i:i + 1, :] = refs[i][...]
        b_ref, a_refs, d_ref, e_ref = refs[nr], refs[nr + 1:nr + 1 + na], refs[nr + 1 + na], refs[nr + 2 + na]
        o_ref[nr:nr + 1, 0:64] = b_ref[:, 0:64]
        o_ref[nr:nr + 1, 64:128] = sum(a[:, 0:64] for a in a_refs)
        o_ref[nr:nr + 1, 128:160] = d_ref[...]
        o_ref[nr:nr + 1, 256:384] = e_ref[...]

    return pl.pallas_call(body, name="pack_small", out_shape=jax.ShapeDtypeStruct((PACK_ROWS, PACK_COLS), F32))(
        *rows, b64, *a64s, d32, extra)


def _pack_ssd_small(cw, cb, nw):
    def body(cw_ref, cb_ref, nw_ref, o_ref):
        o_ref[...] = jnp.zeros_like(o_ref)
        o_ref[0:5, :] = cw_ref[...]
        o_ref[5:6, :] = cb_ref[...]
        o_ref[6:7, 0:256] = nw_ref[...]

    return pl.pallas_call(body, name="pack_ssd_small", out_shape=jax.ShapeDtypeStruct((8, 512), F32))(cw, cb, nw)


def _sum_parts(p_ref):
    g = p_ref[0].astype(F32)
    for s in range(1, p_ref.shape[0]):
        g = g + p_ref[s].astype(F32)
    return g


def _adam_update(w, g, m, v):
    m2 = ADAM_B1 * m + (1.0 - ADAM_B1) * g
    v2 = ADAM_B2 * v + (1.0 - ADAM_B2) * (g * g)
    m_hat = m2 / (1.0 - ADAM_B1 ** ADAM_STEP)
    v_hat = v2 / (1.0 - ADAM_B2 ** ADAM_STEP)
    return -ADAM_LR * (m_hat / (jnp.sqrt(v_hat) + ADAM_EPS) + ADAM_WD * w), m2, v2


def _adamw_windows(name, parts, params, windows, extra=None):
    n = len(params)

    def body(p_ref, *rest):
        ins, outs = rest[:3 * n], rest[3 * n:]
        g = _sum_parts(p_ref)
        for pi, rows, cols, idx in windows:
            w_ref, m_ref, v_ref = ins[3 * pi:3 * pi + 3]
            gw = g[rows, cols]
            dw, m2, v2 = _adam_update(w_ref[idx], gw, m_ref[idx], v_ref[idx])
            for o_ref, val in zip(outs[4 * pi:4 * pi + 4], (gw, dw, m2, v2), strict=True):
                o_ref[idx] = val
        if extra is not None:
            outs[4 * n][...] = g[extra[0], extra[1]]

    out_shape = [jax.ShapeDtypeStruct(w.shape, F32) for (w, _, _) in params for _ in range(4)]
    if extra is not None:
        out_shape.append(jax.ShapeDtypeStruct((extra[0].stop - extra[0].start, extra[1].stop - extra[1].start), F32))
    res = pl.pallas_call(body, name=name, out_shape=out_shape)(parts, *[a for p in params for a in p])
    return [res[4 * i:4 * i + 4] for i in range(n)] + ([res[4 * n]] if extra is not None else [])


def _adamw(name, w, parts, m, v, tr, tc=None):
    r_, c_ = w.shape
    p_ = parts.shape[0]
    tr = min(tr, r_)
    tc = c_ if tc is None else tc
    assert r_ % tr == 0 and c_ % tc == 0

    def body(w_ref, p_ref, m_ref, v_ref, g_ref, d_ref, m2_ref, v2_ref):
        g = _sum_parts(p_ref)
        g_ref[...] = g
        d_ref[...], m2_ref[...], v2_ref[...] = _adam_update(w_ref[...], g, m_ref[...], v_ref[...])

    blk = pl.BlockSpec((tr, tc), lambda i, j: (i, j))
    return pl.pallas_call(
        body, name=name, grid=(r_ // tr, c_ // tc), out_shape=[jax.ShapeDtypeStruct((r_, c_), F32)] * 4,
        in_specs=[blk, pl.BlockSpec((p_, tr, tc), lambda i, j: (0, i, j)), blk, blk], out_specs=[blk] * 4,
        compiler_params=_params(("parallel", "parallel"), VMEM_BIG),
    )(w, parts, m, v)


def _dev_index(p):
    return 4 * p[0] + 2 * p[1] + p[2]


def _all_gather(name, xs):
    n = len(xs)
    hbm = pl.BlockSpec(memory_space=pl.ANY)

    def body(*refs):
        x_refs, o_refs = refs[:n], refs[n:2 * n]
        send_sems, recv_sems, local_sems = refs[2 * n:]
        x, y, c = lax.axis_index("x"), lax.axis_index("y"), lax.axis_index("c")
        me, sibling = (x, y, c), (x, y, 1 - c)
        chips = [(1 - x, y), (x, 1 - y), (1 - x, 1 - y)]

        def copy(a, k, block, to, src=None):
            dst = o_refs[a].at[_dev_index(block)]
            return pltpu.make_async_remote_copy(
                src_ref=dst if src is None else src, dst_ref=dst, send_sem=send_sems.at[a, k],
                recv_sem=recv_sems.at[a, k], device_id=to, device_id_type=MESH)

        mine = [pltpu.make_async_copy(x_refs[a], o_refs[a].at[_dev_index(me)], local_sems.at[a]) for a in range(n)]
        for cp in mine:
            cp.start()
        first = []
        for a in range(n):
            first.append(copy(a, 0, me, sibling, src=x_refs[a]))
            first += [copy(a, 1 + j, me, (*chip, c), src=x_refs[a]) for j, chip in enumerate(chips)]
        for cp in first:
            cp.start()
        passed = []
        for j, chip in enumerate(chips):
            for a in range(n):
                copy(a, 1 + j, (*chip, c), me).wait_recv()
                cp = copy(a, 4 + j, (*chip, c), sibling)
                cp.start()
                passed.append(cp)
        for a in range(n):
            copy(a, 0, sibling, me).wait_recv()
            for j, chip in enumerate(chips):
                copy(a, 4 + j, (*chip, 1 - c), me).wait_recv()
        for cp in first + passed:
            cp.wait_send()
        for cp in mine:
            cp.wait()

    return pl.pallas_call(
        body, name=name, out_shape=[jax.ShapeDtypeStruct((NDEV, *x.shape), x.dtype) for x in xs],
        in_specs=[hbm] * n, out_specs=[hbm] * n,
        scratch_shapes=[pltpu.SemaphoreType.DMA((n, 7)), pltpu.SemaphoreType.DMA((n, 7)), pltpu.SemaphoreType.DMA((n,))],
    )(*xs)


_HBM = pl.BlockSpec(memory_space=pltpu.HBM)
_SEM = pl.BlockSpec(memory_space=pltpu.SEMAPHORE)
_EFFECT = pltpu.SideEffectType.DATAFLOW_SIDE_EFFECTING


def _mesh_position():
    return lax.axis_index("x"), lax.axis_index("y"), lax.axis_index("c")


def _peers(me):
    return [(k, tuple(1 - v if (k >> b) & 1 else v for v, b in zip(me, (2, 1, 0)))) for k in range(1, NDEV)]


EXCHANGE_COPIES = {"gather": NDEV - 1, "scatter": NDEV - 1, "pair": 4, "chips": 3}
NCHIP = NDEV // 2


def _landing_zones(name, xs, mode):
    x_, y_, c_ = _mesh_position()
    mine = (2 * x_ + y_ if mode == "chips" else _dev_index((x_, y_, c_))).astype(jnp.int32).reshape(1)
    lands = []
    for a, x in enumerate(xs):
        rows, cols = x.shape[-2:]
        if mode == "pair":
            lands.append(lax.empty((NCHIP, rows, cols), x.dtype))
            continue
        tr = 256 if rows % 256 == 0 else rows

        def body(me_ref, x_ref, o_ref):
            o_ref[...] = x_ref[...]

        if mode == "gather":
            in_spec = pl.BlockSpec((tr, cols), lambda i, me_ref: (i, 0))
        else:
            in_spec = pl.BlockSpec((None, tr, cols), lambda i, me_ref: (me_ref[0], i, 0))
        lands.append(pl.pallas_call(
            body, name=f"{name}_{a}",
            out_shape=jax.ShapeDtypeStruct((NCHIP if mode == "chips" else NDEV, rows, cols), x.dtype),
            grid_spec=pltpu.PrefetchScalarGridSpec(
                num_scalar_prefetch=1, grid=(rows // tr,), in_specs=[in_spec],
                out_specs=pl.BlockSpec((None, tr, cols), lambda i, me_ref: (me_ref[0], i, 0))),
            compiler_params=_params(("arbitrary",)),
        )(mine, x))
    return lands


def _exchange_copies(x_refs, land_refs, send_sems, recv_sems, mode):
    x_, y_, c_ = me = _mesh_position()
    per_array = EXCHANGE_COPIES[mode]
    out = []

    def add(a, k, src, dst, peer):
        sem = a * per_array + k
        out.append(pltpu.make_async_remote_copy(src_ref=src, dst_ref=dst, send_sem=send_sems.at[sem], recv_sem=recv_sems.at[sem],
                                                device_id=peer, device_id_type=MESH))

    for a, (x_ref, land_ref) in enumerate(zip(x_refs, land_refs)):
        if mode in ("gather", "scatter"):
            for k, peer in _peers(me):
                add(a, k - 1, x_ref.at[_dev_index(peer)] if mode == "scatter" else x_ref, land_ref.at[_dev_index(me)], peer)
        elif mode == "pair":
            for chip in range(NCHIP):
                add(a, chip, x_ref.at[2 * chip + 1 - c_], land_ref.at[chip], (x_, y_, 1 - c_))
        else:
            for k in range(1, NCHIP):
                px, py = (1 - x_ if k & 2 else x_), (1 - y_ if k & 1 else y_)
                add(a, k - 1, x_ref.at[2 * px + py], land_ref.at[2 * x_ + y_], (px, py, c_))
    return out


def _exchange_start(name, xs, lands, mode, dep):
    n = len(xs)

    def body(*refs):
        x_refs, land_refs = refs[:n], refs[n:2 * n]
        send_sems, recv_sems = refs[2 * n + 1], refs[2 * n + 2]
        token = refs[-1]
        for cp in _exchange_copies(x_refs, land_refs, send_sems, recv_sems, mode):
            cp.start()
        token[...] = jnp.zeros_like(token)

    sems = pltpu.SemaphoreType.DMA((n * EXCHANGE_COPIES[mode],))
    res = pl.pallas_call(
        body, name=name,
        out_shape=(sems, sems, *[pltpu.HBM(a.shape, a.dtype) for a in (*xs, *lands)], jax.ShapeDtypeStruct((8, 128), F32)),
        in_specs=[_HBM] * (2 * n) + [pl.BlockSpec(memory_space=pl.ANY)],
        out_specs=(_SEM, _SEM, *[_HBM] * (2 * n), pl.BlockSpec(memory_space=pltpu.VMEM)),
        input_output_aliases={i: 2 + i for i in range(2 * n)},
        compiler_params=pltpu.CompilerParams(has_side_effects=_EFFECT),
    )(*[pltpu.with_memory_space_constraint(a, pltpu.HBM) for a in (*xs, *lands)], dep)
    return res[:-1], res[-1]


def _exchange_wait(name, handles, mode, after):
    send_sems, recv_sems = handles[0], handles[1]
    bufs = handles[2:]
    n = len(bufs) // 2

    def body(*refs):
        x_refs, land_refs = refs[:n], refs[n:2 * n]
        s_sems, r_sems = refs[2 * n], refs[2 * n + 1]
        for cp in _exchange_copies(x_refs, land_refs, s_sems, r_sems, mode):
            cp.wait_send()
            cp.wait_recv()

    res = pl.pallas_call(
        body, name=name, out_shape=tuple(pltpu.HBM(a.shape, a.dtype) for a in bufs),
        in_specs=[_HBM] * (2 * n) + [_SEM, _SEM, pl.BlockSpec(memory_space=pl.ANY)], out_specs=tuple([_HBM] * (2 * n)),
        input_output_aliases={i: i for i in range(2 * n)},
        compiler_params=pltpu.CompilerParams(has_side_effects=_EFFECT),
    )(*bufs, send_sems, recv_sems, after)
    return res[n:]


def _pair_sum(name, x, from_sibling):
    _, rows, cols = x.shape
    tr = 256 if rows % 256 == 0 else rows
    core = lax.axis_index("c").astype(jnp.int32).reshape(1)

    def body(c_ref, x_ref, s_ref, o_ref):
        o_ref[...] = (x_ref[...].astype(F32) + s_ref[...].astype(F32)).astype(o_ref.dtype)

    return pl.pallas_call(
        body, name=name, out_shape=jax.ShapeDtypeStruct((NCHIP, rows, cols), x.dtype),
        grid_spec=pltpu.PrefetchScalarGridSpec(
            num_scalar_prefetch=1, grid=(NCHIP, rows // tr),
            in_specs=[pl.BlockSpec((None, tr, cols), lambda j, i, c_ref: (2 * j + c_ref[0], i, 0)),
                      pl.BlockSpec((None, tr, cols), lambda j, i, c_ref: (j, i, 0))],
            out_specs=pl.BlockSpec((None, tr, cols), lambda j, i, c_ref: (j, i, 0))),
        compiler_params=_params(("parallel", "parallel")),
    )(core, x, from_sibling)


def kernel(x, c, positions, norm_w, mod_w, mod_b, attn_w_in, attn_w_out, ssd_w_in, ssd_conv_w, ssd_conv_b, ssd_dt_bias, ssd_a_log, ssd_d, ssd_norm_w, ssd_w_out, final_norm_w, loss_target, m_norm_w, m_mod_w, m_mod_b, m_attn_w_in, m_attn_w_out, m_ssd_w_in, m_ssd_conv_w, m_ssd_conv_b, m_ssd_dt_bias, m_ssd_a_log, m_ssd_d, m_ssd_norm_w, m_ssd_w_out, m_final_norm_w, v_norm_w, v_mod_w, v_mod_b, v_attn_w_in, v_attn_w_out, v_ssd_w_in, v_ssd_conv_w, v_ssd_conv_b, v_ssd_dt_bias, v_ssd_a_log, v_ssd_d, v_ssd_norm_w, v_ssd_w_out, v_final_norm_w):
    s_len, dm = x.shape[1], x.shape[2]
    me = 4 * lax.axis_index("x") + 2 * lax.axis_index("y") + lax.axis_index("c")
    x0 = x.reshape(s_len, dm)
    tgt = loss_target.reshape(s_len, dm)
    aw = 3 * 512
    si = 2 * dm
    sxbc = 2 * si
    n_ssd_in = ssd_w_in.shape[2] * NDEV

    g_ai, c_all = _all_gather("gather_attn_w_in", [attn_w_in[0].astype(BF16), c])
    w_ai = g_ai.transpose(1, 0, 2).reshape(dm, 4 * aw)
    c_all = c_all.reshape(NDEV, dm)

    part = _mod_part(c_all, mod_w)
    (part_all,) = _all_gather("gather_mod", [part])
    mod_nb = jnp.stack([lax.dynamic_index_in_dim(part_all, i * NDEV + me, axis=1, keepdims=False).reshape(3 * dm)
                        for i in range(2)])

    ssd_small = _pack_ssd_small(ssd_conv_w[0], ssd_conv_b, ssd_norm_w)
    ao_shard = [attn_w_out[0].astype(BF16)]
    ao_handles, ao_token = _exchange_start("w_out_start", ao_shard, _landing_zones("w_out_place", ao_shard, "gather"), "gather",
                                           part_all)
    late_shards = [ssd_w_in[0].T.astype(BF16), ssd_w_out[0].astype(BF16), ssd_small]
    w_handles, w_token = _exchange_start("weights_start", late_shards, _landing_zones("weights_place", late_shards, "gather"),
                                         "gather", ao_token)
    (shift0, scale0, gate0, nw0), (shift1, scale1, gate1, nw1) = _mod_finish(mod_nb, mod_b, norm_w, [ao_token, w_token])
    shift, scale, gate, nw = [shift0, shift1], [scale0, scale1], [gate0, gate1], [nw0, nw1]

    hn0 = _norm_mod_fwd("norm0", x0, nw[0], scale[0], shift[0])
    inv_freq = ROPE_THETA ** (-jnp.arange(0, ROT_DIM, 2, dtype=F32) / ROT_DIM)
    lane = jnp.arange(128) % HEAD_DIM
    inv_row = jnp.where(lane < ROT_DIM, inv_freq[lane % (ROT_DIM // 2)], 0.0).reshape(1, 128).astype(F32)
    tabs = _rope_tables(positions.reshape(s_len, 1), inv_row)
    qk = _matmul("proj_qk", hn0, w_ai, "nn", F32, MM_T, MM_T, dm, epilogue=_rot_fwd, mrows=tabs, n_out=2 * aw)
    v = _matmul("proj_vz", hn0, w_ai, "nn", F32, MM_T, MM_T, dm, b_noff=2 * aw, n_out=2 * aw)
    z0 = (v, 1)
    att = [_attn_fwd(g, qk, v) for g in range(3)]
    os_, lses = [a[0] for a in att], [a[1] for a in att]
    (g_ao,) = _exchange_wait("w_out_wait", ao_handles, "gather", lses[2])
    a0, y0, x1 = _attn_out(os_, lses, z0, x0, gate[0], g_ao.reshape(aw, dm))

    hn1 = _norm_mod_fwd("norm1", x1, nw[1], scale[1], shift[1])
    g_si, g_so, g_small = _exchange_wait("weights_wait", w_handles, "gather", hn1)
    w_ao = g_ao.reshape(aw, dm)
    w_si_t = g_si.reshape(n_ssd_in, dm)
    w_so = g_so.reshape(si, dm)
    conv_w = g_small[:, 0:CONV_WIDTH, :].transpose(1, 0, 2).reshape(CONV_WIDTH, sxbc)
    conv_b = g_small[:, 5, :].reshape(1, sxbc)
    snw = g_small[:, 6, 0:si // NDEV].reshape(1, si)
    ndt = 2 * SSD_HEADS
    z1 = _matmul("ssd_proj_z", hn1, w_si_t, "nt", F32, MM_T, MM_T, dm, n_out=si)
    xpre = _matmul("ssd_proj_xbc", hn1, w_si_t, "nt", F32, MM_T, MM_T, dm, b_noff=si, n_out=sxbc)
    dt_raw = _matmul("ssd_proj_dt", hn1, w_si_t, "nt", F32, MM_T, ndt, dm, b_noff=si + sxbc, n_out=ndt)
    xbc = _conv_fwd(xpre, conv_w, conv_b)
    widen = lambda a: jnp.pad(a, ((0, 0), (0, SSD_DTW - ndt)))
    dt_raw = widen(dt_raw)
    dt_bias = widen(ssd_dt_bias.reshape(1, ndt))
    alog = widen(ssd_a_log.reshape(1, ndt))
    dt = _softplus_fwd(dt_raw, dt_bias)
    y_f, st_f = _ssd_fwd(xbc, dt, alog, 0)
    y_b, st_b = _ssd_fwd(xbc, dt, alog, 1)
    d_e = jnp.repeat(ssd_d.reshape(SSD_HEADS), HEAD_DIM).reshape(1, si)

    fnw = final_norm_w.reshape(1, dm)
    u, dx2, dy1, g_fnw, dgate1, loss_part = _ssd_tail_loss(y_f, y_b, xbc, z1, d_e, snw, w_so, x1, tgt, gate[1], fnw)
    gw_so = _matmul("ssd_out_dw", u, dy1, "tn", BF16, MM_T, MM_T, MM_T)
    dys, dz1, g_snw, g_d = _gate_norm_bwd(dy1, w_so, y_f, y_b, xbc, z1, d_e, snw)
    dxbc_f, ddt_f, dalog_f = _ssd_bwd(xbc, dt, alog, st_f, dys, d_e, 0)
    dxbc, ddt_b, dalog_b = _ssd_bwd(xbc, dt, alog, st_b, dys, d_e, 1, prior=dxbc_f)
    dpre, g_cw, g_cb = _conv_bwd(xpre, dxbc, conv_w, conv_b)
    ddt_raw, g_dtb = _softplus_bwd(ddt_f, ddt_b, dt_raw, dt_bias)
    ddt_raw = ddt_raw[:, :ndt]
    dhn1 = [_matmul("ssd_proj_z_dx", dz1, w_si_t, "nn", F32, MM_T, MM_T, MM_T),
            _matmul("ssd_proj_xbc_dx", dpre, w_si_t, "nn", F32, MM_T, MM_T, MM_T, b_koff=si)]
    gw_si_t = _matmul("ssd_proj_z_dw", dz1, hn1, "tn", BF16, MM_T, MM_T, MM_T, dest=(n_ssd_in, 0, None))
    gw_si_t = _matmul("ssd_proj_xbc_dw", dpre, hn1, "tn", BF16, MM_T, MM_T, MM_T, dest=(n_ssd_in, si, gw_si_t))
    gw_si_t = _matmul("ssd_proj_dt_dw", ddt_raw, hn1, "tn", BF16, ndt, MM_T, MM_T, dest=(n_ssd_in, si + sxbc, gw_si_t))

    l1_grads = [gw_so.reshape(NDEV, si // NDEV, dm), gw_si_t.reshape(NDEV, n_ssd_in // NDEV, dm),
                _pack_ssd_small_blocks(g_cw, g_cb, g_snw)]
    l1_handles, l1_token = _exchange_start("l1_grads_start", l1_grads, _landing_zones("l1_grads_place", l1_grads, "scatter"),
                                           "scatter", dhn1[1])
    dx1, dy0, g_nw1, dsc1, dsh1, dgate0 = _norm_mod_bwd(
        "ssd_proj_dt_dx_norm1_bwd", (ddt_raw, w_si_t, "nn", ndt, dict(b_koff=si + sxbc)), x1, dhn1, dx2,
        nw[1], scale[1], shift[1], prev=(y0, gate[0] + l1_token[0:1, 0:1]))

    gw_ao = _matmul("attn_out_dw", a0, dy0, "tn", BF16, aw // 2, MM_T, MM_T)
    dos, dls, dz0 = _mix_bwd(dy0, w_ao, os_, lses, z0)
    datt = [_attn_bwd(g, qk, v, os_[g], lses[g], dos[g], dls[g]) for g in range(3)]
    dqkv = _rot_pack_bwd([t[0] for t in datt], [t[1] for t in datt], [t[2] for t in datt], tabs)
    wcol = attn_w_in.shape[2]
    gw_ai = _matmul("proj_qkv_dw", hn0, dqkv, "tn", BF16, MM_T, wcol, MM_T, out_blocks=3 * aw // wcol, dest=(NDEV, 0, None))
    gw_ai = _matmul("proj_z_dw", hn0, dz0, "tn", BF16, MM_T, wcol, MM_T, out_blocks=aw // wcol,
                    dest=(NDEV, 3 * aw // wcol, gw_ai))
    after_start = lambda acc, t: acc + t
    zero_row = lambda token: jnp.tile(token[0:1], (1, dm // 128))
    l0_grads = [gw_ai, gw_ao.reshape(NDEV, aw // NDEV, dm)]
    pair_handles, pair_token = _exchange_start("l0_pair_start", l0_grads, _landing_zones("l0_pair_place", l0_grads, "pair"),
                                               "pair", dqkv)
    dhn0_z = _matmul("proj_z_dx", dz0, w_ai, "nt", F32, MM_T, MM_T, aw, b_koff=3 * aw, n_out=dm, epilogue=after_start,
                     ncols=(zero_row(pair_token),))
    from_sibling = _exchange_wait("l0_pair_wait", pair_handles, "pair", dhn0_z)
    chip_sums = [_pair_sum(f"l0_pair_sum_{a}", g, s) for a, (g, s) in enumerate(zip(l0_grads, from_sibling))]
    l0_handles, l0_token = _exchange_start("l0_grads_start", chip_sums, _landing_zones("l0_grads_place", chip_sums, "chips"),
                                           "chips", dhn0_z)
    dx0, g_nw0, dsc0, dsh0 = _norm_mod_bwd(
        "proj_qkv_dx_norm0_bwd", (dqkv, w_ai, "nt", aw, dict(n_out=dm)), x0, [dhn0_z], dx1,
        nw[0], scale[0], shift[0] + zero_row(l0_token))

    small_g = [_pack_small([dsh0, dsc0, dgate0, dsh1, dsc1, dgate1, g_nw0, g_nw1, g_fnw], g_dtb, [dalog_f, dalog_b], g_d, loss_part)]
    sm_handles, sm_token = _exchange_start("small_grads_start", small_g, _landing_zones("small_grads_place", small_g, "gather"),
                                           "gather", dx0)

    whole = (slice(None), slice(None))
    r_so, r_si, r_small = _exchange_wait("l1_grads_wait", l1_handles, "scatter", sm_token)
    si_out = [o.T for o in _adamw("adamw_ssd_w_in", ssd_w_in[0].T, r_si, m_ssd_w_in[0].T, v_ssd_w_in[0].T, n_ssd_in // NDEV, 256)]
    so_out = _adamw("adamw_ssd_w_out", ssd_w_out[0], r_so, m_ssd_w_out[0], v_ssd_w_out[0], 256)
    cw_cols = ssd_conv_w.shape[2]
    cw_out, cb_out, snw_out = _adamw_windows(
        "adamw_ssd_small", r_small,
        [(ssd_conv_w, m_ssd_conv_w, v_ssd_conv_w), (ssd_conv_b, m_ssd_conv_b, v_ssd_conv_b),
         (ssd_norm_w, m_ssd_norm_w, v_ssd_norm_w)],
        [(0, slice(0, CONV_WIDTH), slice(0, cw_cols), (0, slice(None), slice(None))),
         (1, slice(5, 6), slice(0, cw_cols), whole), (2, slice(6, 7), slice(0, si // NDEV), whole)])
    r_ai, r_ao = _exchange_wait("l0_grads_wait", l0_handles, "chips", so_out[0])
    ai_out = _adamw("adamw_attn_w_in", attn_w_in[0], r_ai, m_attn_w_in[0], v_attn_w_in[0], 256)
    ao_out = _adamw("adamw_attn_w_out", attn_w_out[0], r_ao, m_attn_w_out[0], v_attn_w_out[0], 192)

    (small_all,) = _exchange_wait("small_grads_wait", sm_handles, "gather", ai_out[0])
    full = slice(0, PACK_COLS)
    nhd = SSD_HEADS
    windows = [(0, slice(3 * i + k, 3 * i + k + 1), full, (slice(i, i + 1), slice(k * dm, (k + 1) * dm)))
               for i in range(2) for k in range(3)]
    windows += [(1, slice(6 + i, 7 + i), full, (slice(i, i + 1), slice(None))) for i in range(2)]
    windows += [(2, slice(8, 9), full, whole)]
    windows += [(3 + q, slice(9, 10), slice(2 * nhd * q + nhd * j, 2 * nhd * q + nhd * (j + 1)), (0, slice(j, j + 1), slice(None)))
                for q in range(2) for j in range(2)]
    windows += [(5, slice(9, 10), slice(4 * nhd, 5 * nhd), whole)]
    as_row = lambda a: a.reshape(1, dm)
    mb_out, nw_out, fnw_out, dtb_out, alog_out, d_out, loss = _adamw_windows(
        "adamw_small", small_all,
        [(mod_b, m_mod_b, v_mod_b), (norm_w, m_norm_w, v_norm_w), (fnw, as_row(m_final_norm_w), as_row(v_final_norm_w)),
         (ssd_dt_bias, m_ssd_dt_bias, v_ssd_dt_bias), (ssd_a_log, m_ssd_a_log, v_ssd_a_log), (ssd_d, m_ssd_d, v_ssd_d)],
        windows, extra=(slice(9, 10), slice(256, 257)))
    loss = loss.reshape(())

    ncol = mod_w.shape[2]
    dmod_all = small_all[:, 0:6, :].reshape(NDEV, 2, 3 * dm)
    dmod_sh = lax.dynamic_slice_in_dim(dmod_all, me * ncol, ncol, axis=2).transpose(1, 0, 2)
    g_modw = _mod_grad(c_all, dmod_sh).reshape(1, 2 * dm, ncol)
    modw_out = _adamw("adamw_mod_w", mod_w.reshape(2 * dm, ncol), g_modw, m_mod_w.reshape(2 * dm, ncol),
                      v_mod_w.reshape(2 * dm, ncol), 256)

    per_kind = []
    for k in range(4):
        per_kind.append([
            nw_out[k], modw_out[k].reshape(mod_w.shape), mb_out[k], ai_out[k][None], ao_out[k][None], si_out[k][None],
            cw_out[k], cb_out[k], dtb_out[k], alog_out[k], d_out[k], snw_out[k], so_out[k][None], fnw_out[k].reshape(dm)])
    return (loss, dx0.reshape(x.shape), *per_kind[0], *per_kind[1], *per_kind[2], *per_kind[3])


def _pack_ssd_small_blocks(g_cw, g_cb, g_nw):
    nper = g_cw.shape[1] // NDEV
    nwper = g_nw.shape[1] // NDEV

    def body(cw_ref, cb_ref, nw_ref, o_ref):
        o_ref[...] = jnp.zeros_like(o_ref)
        for d in range(NDEV):
            o_ref[d, 0:5, :] = cw_ref[:, d * nper:(d + 1) * nper]
            o_ref[d, 5:6, :] = cb_ref[:, d * nper:(d + 1) * nper]
            o_ref[d, 6:7, 0:nwper] = nw_ref[:, d * nwper:(d + 1) * nwper]

    return pl.pallas_call(body, name="pack_ssd_small_grads", out_shape=jax.ShapeDtypeStruct((NDEV, 8, nper), F32))(g_cw, g_cb, g_nw)
```

```python
import functools
import math

import jax
import jax.numpy as jnp
from jax import lax
from jax.experimental import pallas as pl
from jax.experimental.pallas import tpu as pltpu

F32 = jnp.float32
BF16 = jnp.bfloat16
HI = lax.Precision.HIGHEST
MESH = pl.DeviceIdType.MESH
NDEV = 8

NORM_EPS = 1e-6
ROPE_THETA = 500000.0
ROT_DIM = 16
HEAD_DIM = 64
DILATIONS = (1, 4, 16)
BAND = 64
NEG_BIG = -1e30
CHUNK = 128
SSD_HEADS = 32
SSD_GROUPS = 8
CONV_WIDTH = 5

ADAM_LR = 0.001
ADAM_B1 = 0.9
ADAM_B2 = 0.999
ADAM_EPS = 1e-08
ADAM_WD = 0.01
ADAM_STEP = 10

VMEM_BIG = 56 * 1024 * 1024
MM_T = 1024


def _params(sem=None, vmem=None):
    kw = {}
    if sem is not None:
        kw["dimension_semantics"] = sem
    if vmem is not None:
        kw["vmem_limit_bytes"] = vmem
    return pltpu.CompilerParams(**kw)


def _dg(a, b, ca, cb, prec=None):
    return lax.dot_general(a, b, (((ca,), (cb,)), ((), ())), preferred_element_type=F32, precision=prec)


def _nn(a, b):
    return _dg(a.astype(BF16), b.astype(BF16), 1, 0)


def _nt(a, b):
    return _dg(a.astype(BF16), b.astype(BF16), 1, 1)


def _tn(a, b):
    return _dg(a.astype(BF16), b.astype(BF16), 0, 0)


def _hnn(a, b):
    return _dg(a, b, 1, 0, HI)


@jax.custom_vjp
def _bnn(a, b):
    return _nn(a, b)


_bnn.defvjp(lambda a, b: (_nn(a, b), (a, b)), lambda r, g: (_nt(g, r[1]), _tn(r[0], g)))


@jax.custom_vjp
def _bnt(a, b):
    return _nt(a, b)


_bnt.defvjp(lambda a, b: (_nt(a, b), (a, b)), lambda r, g: (_nn(g, r[1]), _tn(g, r[0])))


@jax.custom_vjp
def _btn(a, b):
    return _tn(a, b)


_btn.defvjp(lambda a, b: (_tn(a, b), (a, b)), lambda r, g: (_nt(r[1], g), _nn(r[0], g)))


def _silu(x):
    return x * jax.nn.sigmoid(x)


def _b_spec(b, mode, tn, tk, no, ko, jk):
    if b.ndim == 2:
        if mode == "nt":
            return pl.BlockSpec((tn, tk), lambda *g: (jk(*g)[0] + no, jk(*g)[1] + ko))
        return pl.BlockSpec((tk, tn), lambda *g: (jk(*g)[1] + ko, jk(*g)[0] + no))
    width = b.shape[2]
    if mode == "nt":
        assert width % tk == 0
        per = width // tk
        return pl.BlockSpec((None, tn, tk), lambda *g: ((jk(*g)[1] + ko) // per, jk(*g)[0] + no, (jk(*g)[1] + ko) % per))
    assert width % tn == 0
    per = width // tn
    return pl.BlockSpec((None, tk, tn), lambda *g: ((jk(*g)[0] + no) // per, jk(*g)[1] + ko, (jk(*g)[0] + no) % per))


def _matmul(name, a, b, mode, out_dtype, tm, tn, tk, *, epilogue=None, tiled=(), mrows=(), ncols=(),
            b_noff=0, b_koff=0, n_out=None, out_blocks=None, dest=None):
    if mode == "tn":
        K, M = a.shape
    else:
        M, K = a.shape
    N = n_out if n_out is not None else (b.shape[0] if mode == "nt" else b.shape[1])
    tm, tn, tk = min(tm, M), min(tn, N), min(tk, K)
    assert M % tm == 0 and N % tn == 0 and K % tk == 0, (name, M, N, K, tm, tn, tk)
    assert b_noff % tn == 0 and b_koff % tk == 0
    no, ko = b_noff // tn, b_koff // tk
    nk = K // tk
    if mode == "tn":
        a_spec = pl.BlockSpec((tk, tm), lambda i, j, k: (k, i))
    else:
        a_spec = pl.BlockSpec((tm, tk), lambda i, j, k: (i, k))
    specs = [a_spec, _b_spec(b, mode, tn, tk, no, ko, lambda i, j, k: (j, k))]
    specs += [pl.BlockSpec((tm, tn), lambda i, j, k: (i, j)) for _ in tiled]
    specs += [pl.BlockSpec((tm, r.shape[1]), lambda i, j, k: (i, 0)) for r in mrows]
    specs += [pl.BlockSpec((1, tn), lambda i, j, k: (0, j)) for _ in ncols]
    total, off, earlier = dest if dest is not None else (None, 0, None)
    if out_blocks is None:
        assert off % tm == 0
        mo = off // tm
        out_shape = jax.ShapeDtypeStruct((M if total is None else total, N), out_dtype)
        out_spec = pl.BlockSpec((tm, tn), lambda i, j, k: (i + mo, j))
    else:
        nper = N // out_blocks
        assert nper % tn == 0
        jb = nper // tn
        out_shape = jax.ShapeDtypeStruct((out_blocks if total is None else total, M, nper), out_dtype)
        out_spec = pl.BlockSpec((None, tm, tn), lambda i, j, k: (j // jb + off, i, j % jb))
    if earlier is not None:
        assert earlier.shape == out_shape.shape and earlier.dtype == out_shape.dtype
    ne = len(tiled) + len(mrows) + len(ncols)
    dot = {"nn": _nn, "nt": _nt, "tn": _tn}[mode]

    def body(a_ref, b_ref, *rest):
        extras, o_ref = rest[:ne], rest[ne]

        def finish(acc):
            if epilogue is not None:
                acc = epilogue(acc, *[e[...] for e in extras])
            o_ref[...] = acc.astype(o_ref.dtype)

        if nk == 1:
            finish(dot(a_ref[...], b_ref[...]))
        else:
            acc_ref = rest[ne + 1]
            k = pl.program_id(2)

            @pl.when(k == 0)
            def _():
                acc_ref[...] = jnp.zeros_like(acc_ref)

            acc_ref[...] += dot(a_ref[...], b_ref[...])

            @pl.when(k == nk - 1)
            def _():
                finish(acc_ref[...])

    args = [a, b, *tiled, *mrows, *ncols]
    aliases = {}
    if earlier is not None:
        specs.append(pl.BlockSpec(memory_space=pl.ANY))
        aliases = {len(args): 0}
        args.append(earlier)

    def body_with_dest(*refs):
        body(*refs[:2 + ne], *refs[2 + ne + (earlier is not None):])

    return pl.pallas_call(
        body_with_dest, name=name, out_shape=out_shape, grid=(M // tm, N // tn, nk),
        in_specs=specs, out_specs=out_spec, input_output_aliases=aliases,
        scratch_shapes=[] if nk == 1 else [pltpu.VMEM((tm, tn), F32)],
        compiler_params=_params(("parallel", "parallel", "arbitrary"), VMEM_BIG),
    )(*args)


def _matmul_rows(name, a, b, mode, tm, tk, fn, rows, consts, outs, accs, *, n_out=None, b_noff=0, b_koff=0):
    rl = [(t, t.shape[1], 0) if not isinstance(t, tuple) else t for t in rows]
    make_a = a if callable(a) else None
    M, K = (rl[0][0].shape[0], b.shape[1 if mode == "nt" else 0]) if make_a else a.shape
    N = n_out if n_out is not None else (b.shape[0] if mode == "nt" else b.shape[1])
    tm, tk = min(tm, M), min(tk, K)
    assert M % tm == 0 and K % tk == 0 and b_koff % tk == 0 and b_noff % N == 0, (name, M, N, K)
    no, ko, nk = b_noff // N, b_koff // tk, K // tk
    assert make_a is None or nk == 1
    nr, nc, no_, na = len(rl), len(consts), len(outs), len(accs)
    dot = _nt if mode == "nt" else _nn

    def body(*refs):
        a_ref, b_ref, rest = (None, refs[0], refs[1:]) if make_a else (refs[0], refs[1], refs[2:])
        r_refs, c_refs = rest[:nr], rest[nr:nr + nc]
        o_refs, acc_refs = rest[nr + nc:nr + nc + no_], rest[nr + nc + no_:nr + nc + no_ + na]
        i, k = pl.program_id(0), pl.program_id(1)

        def finish(prod, *made):
            res_o, res_a = fn(prod, *made, *[r[...] for r in r_refs], *[c[...] for c in c_refs])
            for r, v in zip(o_refs, res_o, strict=True):
                r[...] = v.astype(r.dtype)
            if acc_refs:
                @pl.when(i == 0)
                def _():
                    for r in acc_refs:
                        r[...] = jnp.zeros_like(r)

                for r, v in zip(acc_refs, res_a, strict=True):
                    r[...] += v

        if make_a:
            left = make_a(*[r[...] for r in r_refs], *[c[...] for c in c_refs])
            finish(dot(left, b_ref[...]), left)
        elif nk == 1:
            finish(dot(a_ref[...], b_ref[...]))
        else:
            prod_ref = rest[-1]

            @pl.when(k == 0)
            def _():
                prod_ref[...] = jnp.zeros_like(prod_ref)

            prod_ref[...] += dot(a_ref[...], b_ref[...])

            @pl.when(k == nk - 1)
            def _():
                finish(prod_ref[...])

    b_spec = _b_spec(b, mode, N, tk, no, ko, lambda i, k: (0, k))
    in_specs = ([] if make_a else [pl.BlockSpec((tm, tk), lambda i, k: (i, k))]) + [b_spec]
    in_specs += [pl.BlockSpec((tm, w), functools.partial(lambda i, k, cb: (i, cb), cb=cb)) for (_, w, cb) in rl]
    in_specs += [pl.BlockSpec(c.shape, lambda i, k: (0, 0)) for c in consts]
    out_specs = [pl.BlockSpec((tm, c), lambda i, k: (i, 0)) for (c, _) in outs]
    out_specs += [pl.BlockSpec(shp, lambda i, k: (0, 0)) for shp in accs]
    out_shape = [jax.ShapeDtypeStruct((M, c), dt) for (c, dt) in outs] + [jax.ShapeDtypeStruct(shp, F32) for shp in accs]
    res = pl.pallas_call(
        body, name=name, out_shape=out_shape, grid=(M // tm, nk), in_specs=in_specs, out_specs=out_specs,
        scratch_shapes=[] if nk == 1 else [pltpu.VMEM((tm, N), F32)],
        compiler_params=_params(("arbitrary" if accs else "parallel", "arbitrary"), VMEM_BIG),
    )(*([] if make_a else [a]), b, *[t[0] for t in rl], *consts)
    return res[:no_], res[no_:]


def _rowwise(name, fn, tiled, consts, outs, accs, ts):
    tl = [(t, t.shape[1], 0) if not isinstance(t, tuple) else t for t in tiled]
    s_len = tl[0][0].shape[0]
    assert s_len % ts == 0
    nt_, nc_, no_ = len(tl), len(consts), len(outs)

    def body(*refs):
        t_refs, c_refs = refs[:nt_], refs[nt_:nt_ + nc_]
        o_refs, a_refs = refs[nt_ + nc_:nt_ + nc_ + no_], refs[nt_ + nc_ + no_:]
        res_o, res_a = fn(*[r[...] for r in t_refs], *[r[...] for r in c_refs])
        for r, v in zip(o_refs, res_o, strict=True):
            r[...] = v.astype(r.dtype)
        if a_refs:
            @pl.when(pl.program_id(0) == 0)
            def _():
                for r in a_refs:
                    r[...] = jnp.zeros_like(r)

            for r, v in zip(a_refs, res_a, strict=True):
                r[...] += v

    in_specs = [pl.BlockSpec((ts, w), functools.partial(lambda i, cb: (i, cb), cb=cb)) for (_, w, cb) in tl]
    in_specs += [pl.BlockSpec(c.shape, lambda i: (0, 0)) for c in consts]
    out_specs = [pl.BlockSpec((ts, c), lambda i: (i, 0)) for (c, _) in outs]
    out_specs += [pl.BlockSpec(shp, lambda i: (0, 0)) for shp in accs]
    out_shape = [jax.ShapeDtypeStruct((s_len, c), dt) for (c, dt) in outs]
    out_shape += [jax.ShapeDtypeStruct(shp, F32) for shp in accs]
    res = pl.pallas_call(
        body, name=name, out_shape=out_shape, grid=(s_len // ts,), in_specs=in_specs, out_specs=out_specs,
        compiler_params=_params(("arbitrary",) if accs else ("parallel",), VMEM_BIG),
    )(*[t[0] for t in tl], *consts)
    return res[:no_], res[no_:]


def _norm_mod_fn(x, nw, sc, sh):
    r = lax.rsqrt(jnp.mean(x * x, axis=-1, keepdims=True) + NORM_EPS)
    return (x * r * nw) * (1.0 + sc) + sh


def _norm_mod_fwd(name, x, nw, sc, sh):
    (hn,), _ = _rowwise(name, lambda x, nw, sc, sh: ([_norm_mod_fn(x, nw, sc, sh)], []),
                        [x], [nw, sc, sh], [(x.shape[1], BF16)], [], 512)
    return hn


def _norm_mod_bwd(name, last, x, dhn_parts, dres, nw, sc, sh, prev=None):
    n = len(dhn_parts)
    d = x.shape[1]
    a, b, mode, tk, kw = last

    def fn(dhn, x, *rest):
        for p in rest[:n]:
            dhn = dhn + p
        dres, rest = rest[n], rest[n + 1:]
        y_prev, (nw, sc, sh), gate = (rest[0], rest[1:4], rest[4]) if prev is not None else (None, rest[0:3], None)
        r = lax.rsqrt(jnp.mean(x * x, axis=-1, keepdims=True) + NORM_EPS)
        xh = x * r
        dxh = dhn * (nw * (1.0 + sc))
        dx = r * (dxh - xh * jnp.mean(dxh * xh, axis=-1, keepdims=True)) + dres
        along = jnp.sum(dhn * xh, axis=0, keepdims=True)
        dnw, dsc, dsh = along * (1.0 + sc), along * nw, jnp.sum(dhn, axis=0, keepdims=True)
        if prev is None:
            return [dx], [dnw, dsc, dsh]
        return [dx, gate * dx], [dnw, dsc, dsh, jnp.sum(dx * y_prev, axis=0, keepdims=True)]

    rows = [x, *dhn_parts, dres] + ([prev[0]] if prev is not None else [])
    consts = [nw, sc, sh] + ([prev[1]] if prev is not None else [])
    outs = [(d, F32)] + ([(d, BF16)] if prev is not None else [])
    res_o, res_a = _matmul_rows(name, a, b, mode, 512, tk, fn, rows, consts, outs, [(1, d)] * (3 + (prev is not None)), **kw)
    return (*res_o, *res_a)


def _rope_tables(pos_col, inv_row):
    def fn(pos, inv):
        ang = pos.astype(F32) * inv
        e = lax.broadcasted_iota(jnp.int32, (1, 128), 1) % HEAD_DIM
        cos, sin = jnp.cos(ang), jnp.sin(ang)
        half = ROT_DIM // 2
        return [jnp.where(e < ROT_DIM, cos, 1.0), jnp.where(e < half, -sin, 0.0),
                jnp.where((e >= half) & (e < ROT_DIM), sin, 0.0)], []

    (c, sa, sb), _ = _rowwise("rope_tables", fn, [pos_col], [inv_row], [(128, F32)] * 3, [], 512)
    return c, sa, sb


def _rot_fwd(t, c, sa, sb):
    n = t.shape[1]
    rep = n // 128
    c, sa, sb = (jnp.tile(u, (1, rep)) for u in (c, sa, sb))
    return t * c + pltpu.roll(t, n - ROT_DIM // 2, 1) * sa + pltpu.roll(t, ROT_DIM // 2, 1) * sb


def _rot_bwd(g, c, sa, sb):
    n = g.shape[1]
    rep = n // 128
    c, sa, sb = (jnp.tile(u, (1, rep)) for u in (c, sa, sb))
    return g * c + pltpu.roll(g * sa, ROT_DIM // 2, 1) + pltpu.roll(g * sb, n - ROT_DIM // 2, 1)


ATT_TQ = 128


def _attn_tiles(l):
    tk = ATT_TQ + 2 * BAND
    return (l, l) if l <= tk else (ATT_TQ, tk)


def _attn_specs(g, s_len):
    def blk(off):
        return pl.BlockSpec((s_len, 128), functools.partial(lambda hp, off: (0, off + hp), off=off))

    return blk(4 * g), blk(12 + 4 * g), blk(4 * g), blk(0)


def _attn_tile_geometry(t, d, l):
    tq, tk = _attn_tiles(l)
    nts = l // tq
    r = t // nts
    ts = t % nts
    q0 = ts * tq
    ws = jnp.clip(q0 - BAND, 0, l - tk)
    kind = jnp.where(ts == 0, 0, jnp.where(ts == nts - 1, 2, 1))
    if d == 1:
        return pl.ds(pl.multiple_of(q0, tq), tq), pl.ds(pl.multiple_of(ws, BAND), tk), kind
    return pl.ds(r + d * q0, tq, stride=d), pl.ds(r + d * ws, tk, stride=d), kind


def _attn_fill_bias(bias_ref):
    _, tq2, tk = bias_ref.shape
    iq = lax.broadcasted_iota(jnp.int32, (tq2, 1), 0) % (tq2 // 2)
    ik = lax.broadcasted_iota(jnp.int32, (1, tk), 1)
    for i, off in enumerate((0, -BAND, -2 * BAND)):
        bias_ref[i] = jnp.where(jnp.abs(ik + off - iq) <= BAND, 0.0, NEG_BIG)


def _split_heads(t, in_h):
    zero = jnp.zeros_like(t)
    return jnp.concatenate([jnp.where(in_h[0], t, zero), jnp.where(in_h[1], t, zero)], axis=0)


def _attn_fwd(g, qk, v):
    s_len = qk.shape[0]
    d = DILATIONS[g]
    l = s_len // d
    tq, tk = _attn_tiles(l)
    assert l % tq == 0 and l >= tk
    q_spec, k_spec, v_spec, o_spec = _attn_specs(g, s_len)
    scale = 1.0 / math.sqrt(HEAD_DIM)

    def body(q_ref, k_ref, v_ref, o_ref, lse_ref, bias_ref):
        lane = lax.broadcasted_iota(jnp.int32, (1, 128), 1)
        in_h = [lane < HEAD_DIM, lane >= HEAD_DIM]
        _attn_fill_bias(bias_ref)

        def tile(t, carry):
            rows, win, kind = _attn_tile_geometry(t, d, l)
            q = (q_ref[rows, :] * scale).astype(BF16)
            k = k_ref[win, :].astype(BF16)
            vv = v_ref[win, :].astype(BF16)
            s = _nt(_split_heads(q, in_h), k) + bias_ref[kind]
            m = jnp.max(s, axis=1, keepdims=True)
            p = jnp.exp(s - m)
            den = jnp.sum(p, axis=1, keepdims=True)
            out = _nn(p, vv) / den
            lse = m + jnp.log(den)
            o_ref[rows, :] = jnp.where(in_h[0], out[:tq], out[tq:])
            lse_ref[rows, :] = jnp.where(in_h[0], lse[:tq], lse[tq:])
            return carry

        lax.fori_loop(0, s_len // tq, tile, 0, unroll=4 * ATT_TQ // tq)

    return pl.pallas_call(
        body, name=f"attn_fwd_g{g}", grid=(4,),
        out_shape=[jax.ShapeDtypeStruct((s_len, 512), F32)] * 2,
        in_specs=[q_spec, k_spec, v_spec], out_specs=[o_spec, o_spec],
        scratch_shapes=[pltpu.VMEM((3, 2 * tq, tk), F32)],
        compiler_params=_params(("parallel",), VMEM_BIG),
    )(qk, qk, v)


def _attn_bwd(g, qk, v, o, lse, do, dlse):
    s_len = qk.shape[0]
    d = DILATIONS[g]
    l = s_len // d
    tq, tk = _attn_tiles(l)
    q_spec, k_spec, v_spec, o_spec = _attn_specs(g, s_len)
    scale = 1.0 / math.sqrt(HEAD_DIM)

    def body(q_ref, k_ref, v_ref, o_ref, lse_ref, do_ref, dlse_ref, dq_ref, dk_ref, dv_ref, bias_ref):
        lane = lax.broadcasted_iota(jnp.int32, (1, 128), 1)
        in_h = [lane < HEAD_DIM, lane >= HEAD_DIM]
        dk_ref[...] = jnp.zeros_like(dk_ref)
        dv_ref[...] = jnp.zeros_like(dv_ref)
        _attn_fill_bias(bias_ref)

        def tile(t, carry):
            rows, win, kind = _attn_tile_geometry(t, d, l)
            k, vv = k_ref[win, :].astype(BF16), v_ref[win, :].astype(BF16)
            dout, lse_t, dlse_t = do_ref[rows, :], lse_ref[rows, :], dlse_ref[rows, :]
            od = dout * o_ref[rows, :]
            q2 = _split_heads((q_ref[rows, :] * scale).astype(BF16), in_h)
            do2 = _split_heads(dout.astype(BF16), in_h)
            head_col = lambda a: jnp.concatenate([a[:, 0:1], a[:, HEAD_DIM:HEAD_DIM + 1]], axis=0)
            delta = jnp.concatenate([jnp.sum(jnp.where(m, od, 0.0), axis=1, keepdims=True) for m in in_h], axis=0)
            p = jnp.exp(_nt(q2, k) + bias_ref[kind] - head_col(lse_t))
            ds = (p * (_nt(do2, vv) - delta + head_col(dlse_t))).astype(BF16)
            dq2 = _nn(ds, k) * scale
            dq_ref[rows, :] = jnp.where(in_h[0], dq2[:tq], dq2[tq:])
            dk_ref[win, :] += _tn(ds, q2)
            dv_ref[win, :] += _tn(p, do2)
            return carry

        lax.fori_loop(0, s_len // tq, tile, 0, unroll=4 * ATT_TQ // tq)

    return pl.pallas_call(
        body, name=f"attn_bwd_g{g}", grid=(4,),
        out_shape=[jax.ShapeDtypeStruct((s_len, 512), F32)] * 3,
        in_specs=[q_spec, k_spec, v_spec, o_spec, o_spec, o_spec, o_spec], out_specs=[o_spec] * 3,
        scratch_shapes=[pltpu.VMEM((3, 2 * tq, tk), F32)],
        compiler_params=_params(("parallel",), VMEM_BIG),
    )(qk, qk, v, o, lse, do, dlse)


def _mix_weights(ls):
    mx = jnp.maximum(jnp.maximum(ls[0], ls[1]), ls[2])
    es = [jnp.exp(x - mx) for x in ls]
    tot = es[0] + es[1] + es[2]
    return [e / tot for e in es]


def _attn_out(os_, lses, z, x, gate, w_out):
    s_len, dm = x.shape
    tm = 256
    wdt = 512
    z, z_block = z

    def body(o0, o1, o2, l0, l1, l2, z_ref, x_ref, g_ref, w_ref, a_ref, y_ref, x1_ref):
        alphas = _mix_weights([l0[...], l1[...], l2[...]])
        y = jnp.zeros((tm, dm), F32)
        for g, o_ref in enumerate((o0, o1, o2)):
            a_g = (o_ref[...] * alphas[g] * _silu(z_ref[:, g * wdt:(g + 1) * wdt])).astype(BF16)
            a_ref[:, g * wdt:(g + 1) * wdt] = a_g
            y = y + _nn(a_g, w_ref[g * wdt:(g + 1) * wdt, :])
        y_ref[...] = y
        x1_ref[...] = x_ref[...] + g_ref[...] * y

    row = lambda c: pl.BlockSpec((tm, c), lambda i: (i, 0))
    return pl.pallas_call(
        body, name="attn_out", grid=(s_len // tm,),
        out_shape=[jax.ShapeDtypeStruct((s_len, 3 * wdt), BF16), jax.ShapeDtypeStruct((s_len, dm), F32),
                   jax.ShapeDtypeStruct((s_len, dm), F32)],
        in_specs=[row(wdt)] * 6 + [pl.BlockSpec((tm, 3 * wdt), lambda i: (i, z_block)), row(dm),
                                   pl.BlockSpec((1, dm), lambda i: (0, 0)), pl.BlockSpec(w_out.shape, lambda i: (0, 0))],
        out_specs=[row(3 * wdt), row(dm), row(dm)],
        compiler_params=_params(("parallel",), VMEM_BIG),
    )(*os_, *lses, z, x, gate, w_out)


def _mix_bwd(dy, w_out, os_, lses, z):
    wdt = 512

    def fn(da, o0, o1, o2, l0, l1, l2, z):
        os_t, ls = [o0, o1, o2], [l0, l1, l2]
        alphas = _mix_weights(ls)
        hi = lax.broadcasted_iota(jnp.int32, (2 * wdt, wdt), 0) % wdt // HEAD_DIM
        hj = lax.broadcasted_iota(jnp.int32, (2 * wdt, wdt), 1) // HEAD_DIM
        seg = (hi == hj).astype(BF16)
        head_sum = lambda t: _dg(jnp.concatenate(_bf16_parts(t, 2), axis=1), seg, 1, 0)
        dos, dal, dzs = [], [], []
        for g in range(3):
            zg = z[:, g * wdt:(g + 1) * wdt]
            sig = jax.nn.sigmoid(zg)
            dag = da[:, g * wdt:(g + 1) * wdt]
            dmix = dag * zg * sig
            dzs.append(dag * os_t[g] * alphas[g] * (sig * (1.0 + zg * (1.0 - sig))))
            dos.append(dmix * alphas[g])
            dal.append(head_sum(dmix * os_t[g]))
        mean = alphas[0] * dal[0] + alphas[1] * dal[1] + alphas[2] * dal[2]
        dls = [alphas[g] * (dal[g] - mean) for g in range(3)]
        return dos + dls + [jnp.concatenate(dzs, axis=1)], []

    outs, _ = _matmul_rows("attn_out_dx_mix_bwd", dy, w_out, "nt", 256, dy.shape[1], fn, [*os_, *lses, (z[0], 3 * wdt, z[1])], [],
                           [(wdt, F32)] * 6 + [(3 * wdt, BF16)], [])
    return outs[:3], outs[3:6], outs[6]


def _rot_pack_bwd(dqs, dks, dvs, tabs):
    wdt = 512

    def fn(*args):
        grads, (c, sa, sb) = args[:9], args[9:]
        cols = [_rot_bwd(gq, c, sa, sb) for gq in grads[:6]] + list(grads[6:])
        return [jnp.concatenate(cols, axis=1)], []

    (out,), _ = _rowwise("rot_pack_bwd", fn, [*dqs, *dks, *dvs, *tabs], [], [(9 * wdt, BF16)], [], 256)
    return out


CONV_CB = 128
CONV_R = 256
CONV_PAD = 8


def _conv_taps(buf, base, off, sign):
    return [buf[pl.ds(base + off + sign * j, CONV_R), :] for j in range(CONV_WIDTH)]


def _conv_tap_sum(taps, w):
    acc = None
    for j, t in enumerate(taps):
        term = t * w[j:j + 1, :]
        acc = term if acc is None else acc + term
    return acc


def _conv_fwd(xpre, cw, cb):
    s_len, ch = xpre.shape
    nchunk = s_len // CONV_R

    def body(x_ref, w_ref, b_ref, o_ref, xp):
        zero = jnp.zeros((CONV_PAD, CONV_CB), F32)
        xp[0:CONV_PAD, :] = zero
        xp[s_len + CONV_PAD:s_len + 2 * CONV_PAD, :] = zero

        def fill(ci, carry):
            base = pl.multiple_of(ci * CONV_R, CONV_R)
            xp[pl.ds(base + CONV_PAD, CONV_R), :] = x_ref[pl.ds(base, CONV_R), :]
            return carry

        lax.fori_loop(0, nchunk, fill, 0)
        w = w_ref[...]
        b = b_ref[...]

        def chunk(ci, carry):
            base = pl.multiple_of(ci * CONV_R, CONV_R)
            u = _conv_tap_sum(_conv_taps(xp, base, CONV_PAD - CONV_WIDTH // 2, 1), w) + b
            o_ref[pl.ds(base, CONV_R), :] = _silu(u)
            return carry

        lax.fori_loop(0, nchunk, chunk, 0, unroll=2)

    col = lambda r: pl.BlockSpec((r, CONV_CB), lambda j: (0, j))
    return pl.pallas_call(
        body, name="conv_fwd", grid=(ch // CONV_CB,), out_shape=jax.ShapeDtypeStruct((s_len, ch), F32),
        in_specs=[col(s_len), col(CONV_WIDTH), col(1)], out_specs=col(s_len),
        scratch_shapes=[pltpu.VMEM((s_len + 2 * CONV_PAD, CONV_CB), F32)],
        compiler_params=_params(("parallel",), VMEM_BIG),
    )(xpre, cw, cb)


def _conv_bwd(xpre, da, cw, cb):
    s_len, ch = xpre.shape
    nchunk = s_len // CONV_R
    half = CONV_WIDTH // 2

    def body(x_ref, da_ref, w_ref, b_ref, dx_ref, gw_ref, gb_ref, xp, dcp):
        zero = jnp.zeros((CONV_PAD, CONV_CB), F32)
        for buf in (xp, dcp):
            buf[0:CONV_PAD, :] = zero
            buf[s_len + CONV_PAD:s_len + 2 * CONV_PAD, :] = zero

        def fill(ci, carry):
            base = pl.multiple_of(ci * CONV_R, CONV_R)
            xp[pl.ds(base + CONV_PAD, CONV_R), :] = x_ref[pl.ds(base, CONV_R), :]
            return carry

        lax.fori_loop(0, nchunk, fill, 0)
        w = w_ref[...]
        b = b_ref[...]

        def first(ci, carry):
            base = pl.multiple_of(ci * CONV_R, CONV_R)
            taps = _conv_taps(xp, base, CONV_PAD - half, 1)
            u = _conv_tap_sum(taps, w) + b
            sig = jax.nn.sigmoid(u)
            dc = da_ref[pl.ds(base, CONV_R), :] * (sig * (1.0 + u * (1.0 - sig)))
            dcp[pl.ds(base + CONV_PAD, CONV_R), :] = dc
            gb = carry[0] + jnp.sum(dc, axis=0, keepdims=True)
            gws = [carry[1 + j] + jnp.sum(dc * taps[j], axis=0, keepdims=True) for j in range(CONV_WIDTH)]
            return (gb, *gws)

        z1 = jnp.zeros((1, CONV_CB), F32)
        sums = lax.fori_loop(0, nchunk, first, (z1,) * (1 + CONV_WIDTH), unroll=2)
        gb_ref[...] = sums[0]
        for j in range(CONV_WIDTH):
            gw_ref[j:j + 1, :] = sums[1 + j]

        def second(ci, carry):
            base = pl.multiple_of(ci * CONV_R, CONV_R)
            dx_ref[pl.ds(base, CONV_R), :] = _conv_tap_sum(_conv_taps(dcp, base, CONV_PAD + half, -1), w).astype(dx_ref.dtype)
            return carry

        lax.fori_loop(0, nchunk, second, 0, unroll=2)

    col = lambda r: pl.BlockSpec((r, CONV_CB), lambda j: (0, j))
    return pl.pallas_call(
        body, name="conv_bwd", grid=(ch // CONV_CB,),
        out_shape=[jax.ShapeDtypeStruct((s_len, ch), BF16), jax.ShapeDtypeStruct((CONV_WIDTH, ch), F32),
                   jax.ShapeDtypeStruct((1, ch), F32)],
        in_specs=[col(s_len), col(s_len), col(CONV_WIDTH), col(1)],
        out_specs=[col(s_len), col(CONV_WIDTH), col(1)],
        scratch_shapes=[pltpu.VMEM((s_len + 2 * CONV_PAD, CONV_CB), F32)] * 2,
        compiler_params=_params(("parallel",), VMEM_BIG),
    )(xpre, da, cw, cb)


SSD_GW = 256
SSD_N = 128
SSD_DTW = 128


def _bf16_parts(x, n):
    parts, rest = [], x
    for _ in range(n):
        p = rest.astype(BF16)
        parts.append(p)
        rest = rest - p.astype(F32)
    return parts


@jax.custom_vjp
def _expand(x, e):
    eb = e.astype(BF16)
    return _dg(jnp.concatenate(_bf16_parts(x, 2), axis=1), jnp.concatenate([eb, eb], axis=0), 1, 0)


def _expand_fwd(x, e):
    return _expand(x, e), e


def _expand_bwd(e, g):
    return _dg(g.astype(BF16), e.astype(BF16), 1, 1), jnp.zeros_like(e)


_expand.defvjp(_expand_fwd, _expand_bwd)


@jax.custom_vjp
def _running_sum(tri, x):
    tb = tri.astype(BF16)
    return sum(_dg(tb, p, 1, 0) for p in _bf16_parts(x, 3))


def _running_sum_fwd(tri, x):
    return _running_sum(tri, x), tri


def _running_sum_bwd(tri, g):
    tb = tri.astype(BF16)
    return jnp.zeros_like(tri), sum(_dg(tb, p, 0, 0) for p in _bf16_parts(g, 3))


_running_sum.defvjp(_running_sum_fwd, _running_sum_bwd)


def _pick_col(a, h):
    @jax.custom_vjp
    def pick(a):
        return a[:, h:h + 1]

    pick.defvjp(lambda a: (a[:, h:h + 1], None),
                lambda _, g: (g * (lax.broadcasted_iota(jnp.int32, (1, a.shape[1]), 1) == h).astype(F32),))
    return pick(a)


def _pick_row(a, h):
    @jax.custom_vjp
    def pick(a):
        return a[h:h + 1, :]

    pick.defvjp(lambda a: (a[h:h + 1, :], None),
                lambda _, g: (g * (lax.broadcasted_iota(jnp.int32, (a.shape[0], 1), 0) == h).astype(F32),))
    return pick(a)


def _ssd_mask(dirn):
    ri = lax.broadcasted_iota(jnp.int32, (CHUNK, CHUNK), 0)
    cj = lax.broadcasted_iota(jnp.int32, (CHUNK, CHUNK), 1)
    return (cj <= ri) if dirn == 0 else (cj >= ri)


def _ssd_rowsel(dirn):
    last = CHUNK - 1 if dirn == 0 else 0
    return (lax.broadcasted_iota(jnp.int32, (CHUNK, 1), 0) == last).astype(F32)


def _ssd_chunk_pre(dirn):
    nh = SSD_DTW

    def f(dt, alog):
        da = dt * (-jnp.exp(alog))
        cum = _running_sum(_ssd_mask(dirn).astype(F32), da)
        tot = jnp.sum(cum * _ssd_rowsel(dirn), axis=0, keepdims=True)
        hh = lax.broadcasted_iota(jnp.int32, (nh, SSD_HEADS * HEAD_DIM), 0)
        jj = lax.broadcasted_iota(jnp.int32, (nh, SSD_HEADS * HEAD_DIM), 1)
        expand = (hh == dirn * SSD_HEADS + jj // HEAD_DIM).astype(F32)
        return cum, cum.T, _expand(dt, expand), _expand(jnp.exp(tot - cum), expand), _expand(jnp.exp(cum), expand)

    return f


def _ssd_group_fn(g, dirn, stacked):
    def f(xs, bm, cm, st, cum, cum_t, dt_e, w_e, ce_e):
        mask = _ssd_mask(dirn)
        xdt = xs * dt_e
        cd_e = jnp.sum(ce_e * _ssd_rowsel(dirn), axis=0, keepdims=True)
        cb = _bnt(cm, bm)
        lane_head = lax.broadcasted_iota(jnp.int32, (1, SSD_GW), 1) // HEAD_DIM
        y = _bnn(cm, st) * ce_e
        decayed, inputs = [], []
        for j in range(4):
            hidx = dirn * SSD_HEADS + 4 * g + j
            col, row = _pick_col(cum, hidx), _pick_row(cum_t, hidx)
            dec = cb * jnp.exp(jnp.where(mask, col - row, NEG_BIG))
            head = (lane_head == j).astype(F32)
            if stacked:
                decayed.append(dec)
                inputs.append(xdt * head)
            else:
                y = y + _bnn(dec, xdt) * head
        if stacked:
            y = y + _bnn(jnp.concatenate(decayed, axis=1), jnp.concatenate(inputs, axis=0))
        st_out = st * cd_e + _btn(bm, xdt * w_e)
        return y, st_out

    return f


def _ssd_in_specs(kk):
    ln = CHUNK
    return [pl.BlockSpec((ln, 2048), lambda i: (kk(i), 0)),
            pl.BlockSpec((ln, 1024), lambda i: (kk(i), 2)),
            pl.BlockSpec((ln, 1024), lambda i: (kk(i), 3)),
            pl.BlockSpec((ln, SSD_DTW), lambda i: (kk(i), 0)),
            pl.BlockSpec((1, SSD_DTW), lambda i: (0, 0))]


def _ssd_fwd(xbc, dt, alog, dirn, prior=None):
    s_len = xbc.shape[0]
    nc = s_len // CHUNK
    kk = (lambda i: i) if dirn == 0 else (lambda i: nc - 1 - i)

    def body(x_ref, b_ref, c_ref, dt_ref, al_ref, *rest):
        prior_ref = rest[0] if prior is not None else None
        y_ref, sts_ref, st = rest[prior is not None:]

        @pl.when(pl.program_id(0) == 0)
        def _():
            st[...] = jnp.zeros_like(st)

        sts_ref[0] = st[...]
        cum, cum_t, dt_e, w_e, ce_e = _ssd_chunk_pre(dirn)(dt_ref[...], al_ref[...])
        for g in range(SSD_GROUPS):
            xc = slice(g * SSD_GW, (g + 1) * SSD_GW)
            gc = slice(g * SSD_N, (g + 1) * SSD_N)
            y, st_new = _ssd_group_fn(g, dirn, True)(x_ref[:, xc], b_ref[:, gc], c_ref[:, gc], st[:, xc], cum, cum_t,
                                               dt_e[:, xc], w_e[:, xc], ce_e[:, xc])
            y_ref[:, xc] = y if prior is None else y + prior_ref[:, xc]
            st[:, xc] = st_new

    return pl.pallas_call(
        body, name=f"ssd_fwd_d{dirn}", grid=(nc,),
        out_shape=[jax.ShapeDtypeStruct((s_len, 2048), F32), jax.ShapeDtypeStruct((nc, SSD_N, 2048), F32)],
        in_specs=_ssd_in_specs(kk) + ([pl.BlockSpec((CHUNK, 2048), lambda i: (kk(i), 0))] if prior is not None else []),
        out_specs=[pl.BlockSpec((CHUNK, 2048), lambda i: (kk(i), 0)),
                   pl.BlockSpec((1, SSD_N, 2048), lambda i: (kk(i), 0, 0))],
        scratch_shapes=[pltpu.VMEM((SSD_N, 2048), F32)],
        compiler_params=_params(("arbitrary",), VMEM_BIG),
    )(xbc, xbc, xbc, dt, alog, *([prior] if prior is not None else []))


def _ssd_bwd(xbc, dt, alog, states, dy, d_e, dirn, prior=None):
    s_len = xbc.shape[0]
    nc = s_len // CHUNK
    kk = (lambda i: nc - 1 - i) if dirn == 0 else (lambda i: i)

    def body(x_ref, b_ref, c_ref, dt_ref, al_ref, sts_ref, dy_ref, de_ref, *rest):
        prior_ref = rest[0] if prior is not None else None
        dx_ref, ddt_ref, dal_ref, dst = rest[prior is not None:]
        plus_prior = (lambda v, cols: v + prior_ref[:, cols]) if prior is not None else (lambda v, cols: v)

        @pl.when(pl.program_id(0) == 0)
        def _():
            dst[...] = jnp.zeros_like(dst)
            dal_ref[...] = jnp.zeros_like(dal_ref)

        (cum, cum_t, dt_e, w_e, ce_e), pre_vjp = jax.vjp(_ssd_chunk_pre(dirn), dt_ref[...], al_ref[...])
        dcum = jnp.zeros_like(cum)
        dcum_t = jnp.zeros_like(cum_t)
        d_dt_e, d_w_e, d_ce_e = [], [], []
        for g in range(SSD_GROUPS):
            xc = slice(g * SSD_GW, (g + 1) * SSD_GW)
            gc = slice(g * SSD_N, (g + 1) * SSD_N)
            _, vjp = jax.vjp(_ssd_group_fn(g, dirn, False), x_ref[:, xc], b_ref[:, gc], c_ref[:, gc], sts_ref[0, :, xc], cum, cum_t,
                             dt_e[:, xc], w_e[:, xc], ce_e[:, xc])
            dyg = dy_ref[:, xc]
            dxs, dbm, dcm, dst_g, dcum_g, dcum_t_g, ddte_g, dwe_g, dcee_g = vjp((dyg, dst[:, xc]))
            if dirn == 0:
                dxs = dxs + dyg * de_ref[:, xc]
            bc, cc = slice(2048 + g * SSD_N, 2048 + (g + 1) * SSD_N), slice(3072 + g * SSD_N, 3072 + (g + 1) * SSD_N)
            dx_ref[:, xc] = plus_prior(dxs, xc)
            dx_ref[:, bc] = plus_prior(dbm, bc)
            dx_ref[:, cc] = plus_prior(dcm, cc)
            dst[:, xc] = dst_g
            dcum = dcum + dcum_g
            dcum_t = dcum_t + dcum_t_g
            d_dt_e.append(ddte_g)
            d_w_e.append(dwe_g)
            d_ce_e.append(dcee_g)
        ddt, dal = pre_vjp((dcum, dcum_t, jnp.concatenate(d_dt_e, axis=1), jnp.concatenate(d_w_e, axis=1),
                            jnp.concatenate(d_ce_e, axis=1)))
        ddt_ref[...] = ddt
        dal_ref[...] += dal

    return pl.pallas_call(
        body, name=f"ssd_bwd_d{dirn}", grid=(nc,),
        out_shape=[jax.ShapeDtypeStruct((s_len, 4096), F32), jax.ShapeDtypeStruct((s_len, SSD_DTW), F32),
                   jax.ShapeDtypeStruct((1, SSD_DTW), F32)],
        in_specs=_ssd_in_specs(kk) + [pl.BlockSpec((1, SSD_N, 2048), lambda i: (kk(i), 0, 0)),
                                      pl.BlockSpec((CHUNK, 2048), lambda i: (kk(i), 0)),
                                      pl.BlockSpec((1, 2048), lambda i: (0, 0))]
        + ([pl.BlockSpec((CHUNK, 4096), lambda i: (kk(i), 0))] if prior is not None else []),
        out_specs=[pl.BlockSpec((CHUNK, 4096), lambda i: (kk(i), 0)),
                   pl.BlockSpec((CHUNK, SSD_DTW), lambda i: (kk(i), 0)),
                   pl.BlockSpec((1, SSD_DTW), lambda i: (0, 0))],
        scratch_shapes=[pltpu.VMEM((SSD_N, 2048), F32)],
        compiler_params=_params(("arbitrary",), VMEM_BIG),
    )(xbc, xbc, xbc, dt, alog, states, dy, d_e, *([prior] if prior is not None else []))


def _gate_norm_fn(y, xs, z, d_e, nw):
    yg = (y + xs * d_e) * _silu(z)
    return yg * lax.rsqrt(jnp.mean(yg * yg, axis=-1, keepdims=True) + NORM_EPS) * nw


def _gate_norm_bwd(dy, w_out, y, xbc, z, d_e, nw):
    def fn(du, y, xs, z, d_e, nw):
        sig = jax.nn.sigmoid(z)
        gate = z * sig
        ysum = y + xs * d_e
        yg = ysum * gate
        r = lax.rsqrt(jnp.mean(yg * yg, axis=-1, keepdims=True) + NORM_EPS)
        t = du * nw
        dyg = t * r - yg * (jnp.mean(t * yg, axis=-1, keepdims=True) * (r * r * r))
        dys = dyg * gate
        dz = dyg * ysum * (sig * (1.0 + z * (1.0 - sig)))
        dnw = jnp.sum(du * yg * r, axis=0, keepdims=True)
        dde = jnp.sum(dys * xs, axis=0, keepdims=True)
        hh = lax.broadcasted_iota(jnp.int32, (2048, SSD_HEADS), 0) // HEAD_DIM
        jj = lax.broadcasted_iota(jnp.int32, (2048, SSD_HEADS), 1)
        return [dys, dz], [dnw, _hnn(jnp.broadcast_to(dde, (8, 2048)), (hh == jj).astype(F32))[0:1]]

    (dys, dz), (g_nw, g_d) = _matmul_rows("ssd_out_dx_gate_norm_bwd", dy, w_out, "nt", 256, dy.shape[1], fn,
                                          [y, (xbc, 2048, 0), z], [d_e, nw], [(2048, F32), (2048, BF16)],
                                          [(1, 2048), (1, SSD_HEADS)])
    return dys, dz, g_nw, g_d


def _ssd_tail_loss(y, xbc, z, d_e, snw, w_out, x1, tgt, gate, fnw):
    dm = x1.shape[1]
    si = y.shape[1]

    def make_u(y, xs, z, x1, tgt, d_e, snw, gate, fnw):
        return _gate_norm_fn(y, xs, z, d_e, snw).astype(BF16)

    def fn(y1, u, y, xs, z, x1, tgt, d_e, snw, gate, fnw):
        x2 = x1 + gate * y1
        r = lax.rsqrt(jnp.mean(x2 * x2, axis=-1, keepdims=True) + NORM_EPS)
        xh = x2 * r
        err = xh * fnw - tgt
        loss = 0.5 * jnp.sum(jnp.mean(err * err, axis=-1, keepdims=True), axis=0, keepdims=True)
        dy = err * (1.0 / dm)
        dxh = dy * fnw
        dx2 = r * (dxh - xh * jnp.mean(dxh * xh, axis=-1, keepdims=True))
        dfnw = jnp.sum(dy * xh, axis=0, keepdims=True)
        return [u, dx2, gate * dx2], [dfnw, jnp.sum(dx2 * y1, axis=0, keepdims=True), jnp.broadcast_to(loss, (1, 128))]

    (u, dx2, dy1), (g_fnw, dgate, loss) = _matmul_rows(
        "ssd_out_loss", make_u, w_out, "nn", 256, si, fn, [y, (xbc, si, 0), z, x1, tgt], [d_e, snw, gate, fnw],
        [(si, BF16), (dm, F32), (dm, BF16)], [(1, dm), (1, dm), (1, 128)])
    return u, dx2, dy1, g_fnw, dgate, loss


def _softplus_fwd(dt_raw, bias):
    (dt,), _ = _rowwise("dt_softplus", lambda r, b: ([jax.nn.softplus(r + b)], []), [dt_raw], [bias],
                        [(dt_raw.shape[1], F32)], [], 512)
    return dt


def _softplus_bwd(ddt_f, ddt_b, dt_raw, bias):
    def fn(df, db, r, b):
        g = (df + db) * jax.nn.sigmoid(r + b)
        return [g], [jnp.sum(g, axis=0, keepdims=True)]

    w = dt_raw.shape[1]
    (g,), (gb,) = _rowwise("dt_softplus_bwd", fn, [ddt_f, ddt_b, dt_raw], [bias], [(w, BF16)], [(1, w)], 512)
    return g, gb


def _whole(a):
    nd = len(a.shape)
    return pl.BlockSpec(a.shape, lambda *_: (0,) * nd)


def _mod_part(c_all, mod_w):
    nl, _, ncol = mod_w.shape
    nb = c_all.shape[0]

    def body(c_ref, w_ref, o_ref):
        cond = _silu(c_ref[...])
        for i in range(nl):
            o_ref[i * nb:(i + 1) * nb, :] = _nn(cond, w_ref[i])

    return pl.pallas_call(body, name="mod_part", out_shape=jax.ShapeDtypeStruct((nl * nb, ncol), F32),
                          compiler_params=_params(None, VMEM_BIG))(c_all, mod_w)


def _mod_finish(mod_nb, mod_b, norm_w, tokens):
    nl, dm = norm_w.shape

    def body(a_ref, b_ref, nw_ref, *rest):
        tok_refs, o_refs = rest[:len(tokens)], rest[len(tokens):]
        tok = sum(t[0:1, 0:1] for t in tok_refs)
        for i in range(nl):
            for k in range(3):
                cols = slice(k * dm, (k + 1) * dm)
                o_refs[4 * i + k][...] = a_ref[i:i + 1, cols] + b_ref[i:i + 1, cols]
            o_refs[4 * i + 3][...] = nw_ref[i:i + 1, :] + tok

    rows = pl.pallas_call(body, name="mod_finish", out_shape=[jax.ShapeDtypeStruct((1, dm), F32)] * (4 * nl))(
        mod_nb, mod_b, norm_w, *tokens)
    return [rows[4 * i:4 * i + 4] for i in range(nl)]


def _mod_grad(c_all, dmod_sh):
    nl, nb, ncol = dmod_sh.shape
    dm = c_all.shape[1]

    def body(c_ref, d_ref, o_ref):
        cond = _silu(c_ref[...])
        for i in range(nl):
            o_ref[i] = _tn(cond, d_ref[i])

    return pl.pallas_call(body, name="mod_grad", out_shape=jax.ShapeDtypeStruct((nl, dm, ncol), F32),
                          compiler_params=_params(None, VMEM_BIG))(c_all, dmod_sh)


PACK_ROWS = 16
PACK_COLS = 1024


def _pack_small(rows, b64, a64s, d32, extra):
    nr, na = len(rows), len(a64s)

    def body(*refs):
        o_ref = refs[-1]
        o_ref[...] = jnp.zeros_like(o_ref)
        for i in range(nr):
            o_ref[i:i + 1, :] = refs[i][...]
        b_ref, a_refs, d_ref, e_ref = refs[nr], refs[nr + 1:nr + 1 + na], refs[nr + 1 + na], refs[nr + 2 + na]
        o_ref[nr:nr + 1, 0:64] = b_ref[:, 0:64]
        o_ref[nr:nr + 1, 64:128] = sum(a[:, 0:64] for a in a_refs)
        o_ref[nr:nr + 1, 128:160] = d_ref[...]
        o_ref[nr:nr + 1, 256:384] = e_ref[...]

    return pl.pallas_call(body, name="pack_small", out_shape=jax.ShapeDtypeStruct((PACK_ROWS, PACK_COLS), F32))(
        *rows, b64, *a64s, d32, extra)


def _pack_ssd_small(cw, cb, nw):
    def body(cw_ref, cb_ref, nw_ref, o_ref):
        o_ref[...] = jnp.zeros_like(o_ref)
        o_ref[0:5, :] = cw_ref[...]
        o_ref[5:6, :] = cb_ref[...]
        o_ref[6:7, 0:256] = nw_ref[...]

    return pl.pallas_call(body, name="pack_ssd_small", out_shape=jax.ShapeDtypeStruct((8, 512), F32))(cw, cb, nw)


def _sum_parts(p_ref):
    g = p_ref[0].astype(F32)
    for s in range(1, p_ref.shape[0]):
        g = g + p_ref[s].astype(F32)
    return g


def _adam_update(w, g, m, v):
    m2 = ADAM_B1 * m + (1.0 - ADAM_B1) * g
    v2 = ADAM_B2 * v + (1.0 - ADAM_B2) * (g * g)
    m_hat = m2 / (1.0 - ADAM_B1 ** ADAM_STEP)
    v_hat = v2 / (1.0 - ADAM_B2 ** ADAM_STEP)
    return -ADAM_LR * (m_hat / (jnp.sqrt(v_hat) + ADAM_EPS) + ADAM_WD * w), m2, v2


def _adamw_windows(name, parts, params, windows, extra=None):
    n = len(params)

    def body(p_ref, *rest):
        ins, outs = rest[:3 * n], rest[3 * n:]
        g = _sum_parts(p_ref)
        for pi, rows, cols, idx in windows:
            w_ref, m_ref, v_ref = ins[3 * pi:3 * pi + 3]
            gw = g[rows, cols]
            dw, m2, v2 = _adam_update(w_ref[idx], gw, m_ref[idx], v_ref[idx])
            for o_ref, val in zip(outs[4 * pi:4 * pi + 4], (gw, dw, m2, v2), strict=True):
                o_ref[idx] = val
        if extra is not None:
            outs[4 * n][...] = g[extra[0], extra[1]]

    out_shape = [jax.ShapeDtypeStruct(w.shape, F32) for (w, _, _) in params for _ in range(4)]
    if extra is not None:
        out_shape.append(jax.ShapeDtypeStruct((extra[0].stop - extra[0].start, extra[1].stop - extra[1].start), F32))
    res = pl.pallas_call(body, name=name, out_shape=out_shape)(parts, *[a for p in params for a in p])
    return [res[4 * i:4 * i + 4] for i in range(n)] + ([res[4 * n]] if extra is not None else [])


def _adamw(name, w, parts, m, v, tr, tc=None):
    r_, c_ = w.shape
    p_ = parts.shape[0]
    tr = min(tr, r_)
    tc = c_ if tc is None else tc
    assert r_ % tr == 0 and c_ % tc == 0

    def body(w_ref, p_ref, m_ref, v_ref, g_ref, d_ref, m2_ref, v2_ref):
        g = _sum_parts(p_ref)
        g_ref[...] = g
        d_ref[...], m2_ref[...], v2_ref[...] = _adam_update(w_ref[...], g, m_ref[...], v_ref[...])

    blk = pl.BlockSpec((tr, tc), lambda i, j: (i, j))
    return pl.pallas_call(
        body, name=name, grid=(r_ // tr, c_ // tc), out_shape=[jax.ShapeDtypeStruct((r_, c_), F32)] * 4,
        in_specs=[blk, pl.BlockSpec((p_, tr, tc), lambda i, j: (0, i, j)), blk, blk], out_specs=[blk] * 4,
        compiler_params=_params(("parallel", "parallel"), VMEM_BIG),
    )(w, parts, m, v)


def _dev_index(p):
    return 4 * p[0] + 2 * p[1] + p[2]


def _all_gather(name, xs):
    n = len(xs)
    hbm = pl.BlockSpec(memory_space=pl.ANY)

    def body(*refs):
        x_refs, o_refs = refs[:n], refs[n:2 * n]
        send_sems, recv_sems, local_sems = refs[2 * n:]
        x, y, c = lax.axis_index("x"), lax.axis_index("y"), lax.axis_index("c")
        me, sibling = (x, y, c), (x, y, 1 - c)
        chips = [(1 - x, y), (x, 1 - y), (1 - x, 1 - y)]

        def copy(a, k, block, to, src=None):
            dst = o_refs[a].at[_dev_index(block)]
            return pltpu.make_async_remote_copy(
                src_ref=dst if src is None else src, dst_ref=dst, send_sem=send_sems.at[a, k],
                recv_sem=recv_sems.at[a, k], device_id=to, device_id_type=MESH)

        mine = [pltpu.make_async_copy(x_refs[a], o_refs[a].at[_dev_index(me)], local_sems.at[a]) for a in range(n)]
        for cp in mine:
            cp.start()
        first = []
        for a in range(n):
            first.append(copy(a, 0, me, sibling, src=x_refs[a]))
            first += [copy(a, 1 + j, me, (*chip, c), src=x_refs[a]) for j, chip in enumerate(chips)]
        for cp in first:
            cp.start()
        passed = []
        for j, chip in enumerate(chips):
            for a in range(n):
                copy(a, 1 + j, (*chip, c), me).wait_recv()
                cp = copy(a, 4 + j, (*chip, c), sibling)
                cp.start()
                passed.append(cp)
        for a in range(n):
            copy(a, 0, sibling, me).wait_recv()
            for j, chip in enumerate(chips):
                copy(a, 4 + j, (*chip, 1 - c), me).wait_recv()
        for cp in first + passed:
            cp.wait_send()
        for cp in mine:
            cp.wait()

    return pl.pallas_call(
        body, name=name, out_shape=[jax.ShapeDtypeStruct((NDEV, *x.shape), x.dtype) for x in xs],
        in_specs=[hbm] * n, out_specs=[hbm] * n,
        scratch_shapes=[pltpu.SemaphoreType.DMA((n, 7)), pltpu.SemaphoreType.DMA((n, 7)), pltpu.SemaphoreType.DMA((n,))],
    )(*xs)


_HBM = pl.BlockSpec(memory_space=pltpu.HBM)
_SEM = pl.BlockSpec(memory_space=pltpu.SEMAPHORE)
_EFFECT = pltpu.SideEffectType.DATAFLOW_SIDE_EFFECTING


def _mesh_position():
    return lax.axis_index("x"), lax.axis_index("y"), lax.axis_index("c")


def _peers(me):
    return [(k, tuple(1 - v if (k >> b) & 1 else v for v, b in zip(me, (2, 1, 0)))) for k in range(1, NDEV)]


EXCHANGE_COPIES = {"gather": NDEV - 1, "scatter": NDEV - 1, "pair": 4, "chips": 3}
NCHIP = NDEV // 2


def _landing_zones(name, xs, mode):
    x_, y_, c_ = _mesh_position()
    mine = (2 * x_ + y_ if mode == "chips" else _dev_index((x_, y_, c_))).astype(jnp.int32).reshape(1)
    lands = []
    for a, x in enumerate(xs):
        rows, cols = x.shape[-2:]
        if mode == "pair":
            lands.append(lax.empty((NCHIP, rows, cols), x.dtype))
            continue
        tr = 256 if rows % 256 == 0 else rows

        def body(me_ref, x_ref, o_ref):
            o_ref[...] = x_ref[...]

        if mode == "gather":
            in_spec = pl.BlockSpec((tr, cols), lambda i, me_ref: (i, 0))
        else:
            in_spec = pl.BlockSpec((None, tr, cols), lambda i, me_ref: (me_ref[0], i, 0))
        lands.append(pl.pallas_call(
            body, name=f"{name}_{a}",
            out_shape=jax.ShapeDtypeStruct((NCHIP if mode == "chips" else NDEV, rows, cols), x.dtype),
            grid_spec=pltpu.PrefetchScalarGridSpec(
                num_scalar_prefetch=1, grid=(rows // tr,), in_specs=[in_spec],
                out_specs=pl.BlockSpec((None, tr, cols), lambda i, me_ref: (me_ref[0], i, 0))),
            compiler_params=_params(("arbitrary",)),
        )(mine, x))
    return lands


def _exchange_copies(x_refs, land_refs, send_sems, recv_sems, mode):
    x_, y_, c_ = me = _mesh_position()
    per_array = EXCHANGE_COPIES[mode]
    out = []

    def add(a, k, src, dst, peer):
        sem = a * per_array + k
        out.append(pltpu.make_async_remote_copy(src_ref=src, dst_ref=dst, send_sem=send_sems.at[sem], recv_sem=recv_sems.at[sem],
                                                device_id=peer, device_id_type=MESH))

    for a, (x_ref, land_ref) in enumerate(zip(x_refs, land_refs)):
        if mode in ("gather", "scatter"):
            for k, peer in _peers(me):
                add(a, k - 1, x_ref.at[_dev_index(peer)] if mode == "scatter" else x_ref, land_ref.at[_dev_index(me)], peer)
        elif mode == "pair":
            for chip in range(NCHIP):
                add(a, chip, x_ref.at[2 * chip + 1 - c_], land_ref.at[chip], (x_, y_, 1 - c_))
        else:
            for k in range(1, NCHIP):
                px, py = (1 - x_ if k & 2 else x_), (1 - y_ if k & 1 else y_)
                add(a, k - 1, x_ref.at[2 * px + py], land_ref.at[2 * x_ + y_], (px, py, c_))
    return out


def _exchange_start(name, xs, lands, mode, dep):
    n = len(xs)

    def body(*refs):
        x_refs, land_refs = refs[:n], refs[n:2 * n]
        send_sems, recv_sems = refs[2 * n + 1], refs[2 * n + 2]
        token = refs[-1]
        for cp in _exchange_copies(x_refs, land_refs, send_sems, recv_sems, mode):
            cp.start()
        token[...] = jnp.zeros_like(token)

    sems = pltpu.SemaphoreType.DMA((n * EXCHANGE_COPIES[mode],))
    res = pl.pallas_call(
        body, name=name,
        out_shape=(sems, sems, *[pltpu.HBM(a.shape, a.dtype) for a in (*xs, *lands)], jax.ShapeDtypeStruct((8, 128), F32)),
        in_specs=[_HBM] * (2 * n) + [pl.BlockSpec(memory_space=pl.ANY)],
        out_specs=(_SEM, _SEM, *[_HBM] * (2 * n), pl.BlockSpec(memory_space=pltpu.VMEM)),
        input_output_aliases={i: 2 + i for i in range(2 * n)},
        compiler_params=pltpu.CompilerParams(has_side_effects=_EFFECT),
    )(*[pltpu.with_memory_space_constraint(a, pltpu.HBM) for a in (*xs, *lands)], dep)
    return res[:-1], res[-1]


def _exchange_wait(name, handles, mode, after):
    send_sems, recv_sems = handles[0], handles[1]
    bufs = handles[2:]
    n = len(bufs) // 2

    def body(*refs):
        x_refs, land_refs = refs[:n], refs[n:2 * n]
        s_sems, r_sems = refs[2 * n], refs[2 * n + 1]
        for cp in _exchange_copies(x_refs, land_refs, s_sems, r_sems, mode):
            cp.wait_send()
            cp.wait_recv()

    res = pl.pallas_call(
        body, name=name, out_shape=tuple(pltpu.HBM(a.shape, a.dtype) for a in bufs),
        in_specs=[_HBM] * (2 * n) + [_SEM, _SEM, pl.BlockSpec(memory_space=pl.ANY)], out_specs=tuple([_HBM] * (2 * n)),
        input_output_aliases={i: i for i in range(2 * n)},
        compiler_params=pltpu.CompilerParams(has_side_effects=_EFFECT),
    )(*bufs, send_sems, recv_sems, after)
    return res[n:]


def _pair_sum(name, x, from_sibling):
    _, rows, cols = x.shape
    tr = 256 if rows % 256 == 0 else rows
    core = lax.axis_index("c").astype(jnp.int32).reshape(1)

    def body(c_ref, x_ref, s_ref, o_ref):
        o_ref[...] = (x_ref[...].astype(F32) + s_ref[...].astype(F32)).astype(o_ref.dtype)

    return pl.pallas_call(
        body, name=name, out_shape=jax.ShapeDtypeStruct((NCHIP, rows, cols), x.dtype),
        grid_spec=pltpu.PrefetchScalarGridSpec(
            num_scalar_prefetch=1, grid=(NCHIP, rows // tr),
            in_specs=[pl.BlockSpec((None, tr, cols), lambda j, i, c_ref: (2 * j + c_ref[0], i, 0)),
                      pl.BlockSpec((None, tr, cols), lambda j, i, c_ref: (j, i, 0))],
            out_specs=pl.BlockSpec((None, tr, cols), lambda j, i, c_ref: (j, i, 0))),
        compiler_params=_params(("parallel", "parallel")),
    )(core, x, from_sibling)


def kernel(x, c, positions, norm_w, mod_w, mod_b, attn_w_in, attn_w_out, ssd_w_in, ssd_conv_w, ssd_conv_b, ssd_dt_bias, ssd_a_log, ssd_d, ssd_norm_w, ssd_w_out, final_norm_w, loss_target, m_norm_w, m_mod_w, m_mod_b, m_attn_w_in, m_attn_w_out, m_ssd_w_in, m_ssd_conv_w, m_ssd_conv_b, m_ssd_dt_bias, m_ssd_a_log, m_ssd_d, m_ssd_norm_w, m_ssd_w_out, m_final_norm_w, v_norm_w, v_mod_w, v_mod_b, v_attn_w_in, v_attn_w_out, v_ssd_w_in, v_ssd_conv_w, v_ssd_conv_b, v_ssd_dt_bias, v_ssd_a_log, v_ssd_d, v_ssd_norm_w, v_ssd_w_out, v_final_norm_w):
    s_len, dm = x.shape[1], x.shape[2]
    me = 4 * lax.axis_index("x") + 2 * lax.axis_index("y") + lax.axis_index("c")
    x0 = x.reshape(s_len, dm)
    tgt = loss_target.reshape(s_len, dm)
    aw = 3 * 512
    si = 2 * dm
    sxbc = 2 * si
    n_ssd_in = ssd_w_in.shape[2] * NDEV

    g_ai, c_all = _all_gather("gather_attn_w_in", [attn_w_in[0].astype(BF16), c])
    w_ai = g_ai
    wcol = attn_w_in.shape[2]
    c_all = c_all.reshape(NDEV, dm)

    part = _mod_part(c_all, mod_w)
    (part_all,) = _all_gather("gather_mod", [part])
    mod_nb = jnp.stack([lax.dynamic_index_in_dim(part_all, i * NDEV + me, axis=1, keepdims=False).reshape(3 * dm)
                        for i in range(2)])

    ssd_small = _pack_ssd_small(ssd_conv_w[0], ssd_conv_b, ssd_norm_w)
    ao_shard = [attn_w_out[0].astype(BF16)]
    ao_handles, ao_token = _exchange_start("w_out_start", ao_shard, _landing_zones("w_out_place", ao_shard, "gather"), "gather",
                                           part_all)
    late_shards = [ssd_w_in[0].T.astype(BF16), ssd_w_out[0].astype(BF16), ssd_small]
    w_handles, w_token = _exchange_start("weights_start", late_shards, _landing_zones("weights_place", late_shards, "gather"),
                                         "gather", ao_token)
    (shift0, scale0, gate0, nw0), (shift1, scale1, gate1, nw1) = _mod_finish(mod_nb, mod_b, norm_w, [ao_token, w_token])
    shift, scale, gate, nw = [shift0, shift1], [scale0, scale1], [gate0, gate1], [nw0, nw1]

    hn0 = _norm_mod_fwd("norm0", x0, nw[0], scale[0], shift[0])
    inv_freq = ROPE_THETA ** (-jnp.arange(0, ROT_DIM, 2, dtype=F32) / ROT_DIM)
    lane = jnp.arange(128) % HEAD_DIM
    inv_row = jnp.where(lane < ROT_DIM, inv_freq[lane % (ROT_DIM // 2)], 0.0).reshape(1, 128).astype(F32)
    tabs = _rope_tables(positions.reshape(s_len, 1), inv_row)
    qk = _matmul("proj_qk", hn0, w_ai, "nn", F32, MM_T, wcol, dm, epilogue=_rot_fwd, mrows=tabs, n_out=2 * aw)
    v = _matmul("proj_vz", hn0, w_ai, "nn", F32, MM_T, wcol, dm, b_noff=2 * aw, n_out=2 * aw)
    z0 = (v, 1)
    att = [_attn_fwd(g, qk, v) for g in range(3)]
    os_, lses = [a[0] for a in att], [a[1] for a in att]
    (g_ao,) = _exchange_wait("w_out_wait", ao_handles, "gather", lses[2])
    a0, y0, x1 = _attn_out(os_, lses, z0, x0, gate[0], g_ao.reshape(aw, dm))

    hn1 = _norm_mod_fwd("norm1", x1, nw[1], scale[1], shift[1])
    g_si, g_so, g_small = _exchange_wait("weights_wait", w_handles, "gather", hn1)
    w_ao = g_ao.reshape(aw, dm)
    w_si_t = g_si.reshape(n_ssd_in, dm)
    w_so = g_so.reshape(si, dm)
    conv_w = g_small[:, 0:CONV_WIDTH, :].transpose(1, 0, 2).reshape(CONV_WIDTH, sxbc)
    conv_b = g_small[:, 5, :].reshape(1, sxbc)
    snw = g_small[:, 6, 0:si // NDEV].reshape(1, si)
    ndt = 2 * SSD_HEADS
    z1 = _matmul("ssd_proj_z", hn1, w_si_t, "nt", F32, MM_T, MM_T, dm, n_out=si)
    xpre = _matmul("ssd_proj_xbc", hn1, w_si_t, "nt", F32, MM_T, MM_T, dm, b_noff=si, n_out=sxbc)
    dt_raw = _matmul("ssd_proj_dt", hn1, w_si_t, "nt", F32, MM_T, ndt, dm, b_noff=si + sxbc, n_out=ndt)
    xbc = _conv_fwd(xpre, conv_w, conv_b)
    widen = lambda a: jnp.pad(a, ((0, 0), (0, SSD_DTW - ndt)))
    dt_raw = widen(dt_raw)
    dt_bias = widen(ssd_dt_bias.reshape(1, ndt))
    alog = widen(ssd_a_log.reshape(1, ndt))
    dt = _softplus_fwd(dt_raw, dt_bias)
    y_f, st_f = _ssd_fwd(xbc, dt, alog, 0)
    y_fb, st_b = _ssd_fwd(xbc, dt, alog, 1, prior=y_f)
    d_e = jnp.repeat(ssd_d.reshape(SSD_HEADS), HEAD_DIM).reshape(1, si)

    fnw = final_norm_w.reshape(1, dm)
    u, dx2, dy1, g_fnw, dgate1, loss_part = _ssd_tail_loss(y_fb, xbc, z1, d_e, snw, w_so, x1, tgt, gate[1], fnw)
    gw_so = _matmul("ssd_out_dw", u, dy1, "tn", BF16, MM_T, MM_T, MM_T)
    dys, dz1, g_snw, g_d = _gate_norm_bwd(dy1, w_so, y_fb, xbc, z1, d_e, snw)
    dxbc_f, ddt_f, dalog_f = _ssd_bwd(xbc, dt, alog, st_f, dys, d_e, 0)
    dxbc, ddt_b, dalog_b = _ssd_bwd(xbc, dt, alog, st_b, dys, d_e, 1, prior=dxbc_f)
    dpre, g_cw, g_cb = _conv_bwd(xpre, dxbc, conv_w, conv_b)
    ddt_raw, g_dtb = _softplus_bwd(ddt_f, ddt_b, dt_raw, dt_bias)
    ddt_raw = ddt_raw[:, :ndt]
    dhn1 = [_matmul("ssd_proj_z_dx", dz1, w_si_t, "nn", F32, MM_T, MM_T, MM_T),
            _matmul("ssd_proj_xbc_dx", dpre, w_si_t, "nn", F32, MM_T, MM_T, MM_T, b_koff=si)]
    gw_si_t = _matmul("ssd_proj_z_dw", dz1, hn1, "tn", BF16, MM_T, MM_T, MM_T, dest=(n_ssd_in, 0, None))
    gw_si_t = _matmul("ssd_proj_xbc_dw", dpre, hn1, "tn", BF16, MM_T, MM_T, MM_T, dest=(n_ssd_in, si, gw_si_t))
    gw_si_t = _matmul("ssd_proj_dt_dw", ddt_raw, hn1, "tn", BF16, ndt, MM_T, MM_T, dest=(n_ssd_in, si + sxbc, gw_si_t))

    l1_grads = [gw_so.reshape(NDEV, si // NDEV, dm), gw_si_t.reshape(NDEV, n_ssd_in // NDEV, dm),
                _pack_ssd_small_blocks(g_cw, g_cb, g_snw)]
    l1_handles, l1_token = _exchange_start("l1_grads_start", l1_grads, _landing_zones("l1_grads_place", l1_grads, "scatter"),
                                           "scatter", dhn1[1])
    dx1, dy0, g_nw1, dsc1, dsh1, dgate0 = _norm_mod_bwd(
        "ssd_proj_dt_dx_norm1_bwd", (ddt_raw, w_si_t, "nn", ndt, dict(b_koff=si + sxbc)), x1, dhn1, dx2,
        nw[1], scale[1], shift[1], prev=(y0, gate[0] + l1_token[0:1, 0:1]))

    gw_ao = _matmul("attn_out_dw", a0, dy0, "tn", BF16, aw // 2, MM_T, MM_T)
    dos, dls, dz0 = _mix_bwd(dy0, w_ao, os_, lses, z0)
    datt = [_attn_bwd(g, qk, v, os_[g], lses[g], dos[g], dls[g]) for g in range(3)]
    dqkv = _rot_pack_bwd([t[0] for t in datt], [t[1] for t in datt], [t[2] for t in datt], tabs)
    gw_ai = _matmul("proj_qkv_dw", hn0, dqkv, "tn", BF16, MM_T, wcol, MM_T, out_blocks=3 * aw // wcol, dest=(NDEV, 0, None))
    gw_ai = _matmul("proj_z_dw", hn0, dz0, "tn", BF16, MM_T, wcol, MM_T, out_blocks=aw // wcol,
                    dest=(NDEV, 3 * aw // wcol, gw_ai))
    after_start = lambda acc, t: acc + t
    zero_row = lambda token: jnp.tile(token[0:1], (1, dm // 128))
    l0_grads = [gw_ai, gw_ao.reshape(NDEV, aw // NDEV, dm)]
    pair_handles, pair_token = _exchange_start("l0_pair_start", l0_grads, _landing_zones("l0_pair_place", l0_grads, "pair"),
                                               "pair", dqkv)
    dhn0_z = _matmul("proj_z_dx", dz0, w_ai, "nt", F32, MM_T, MM_T, wcol, b_koff=3 * aw, n_out=dm, epilogue=after_start,
                     ncols=(zero_row(pair_token),))
    from_sibling = _exchange_wait("l0_pair_wait", pair_handles, "pair", dhn0_z)
    chip_sums = [_pair_sum(f"l0_pair_sum_{a}", g, s) for a, (g, s) in enumerate(zip(l0_grads, from_sibling))]
    l0_handles, l0_token = _exchange_start("l0_grads_start", chip_sums, _landing_zones("l0_grads_place", chip_sums, "chips"),
                                           "chips", dhn0_z)
    dx0, g_nw0, dsc0, dsh0 = _norm_mod_bwd(
        "proj_qkv_dx_norm0_bwd", (dqkv, w_ai, "nt", wcol, dict(n_out=dm)), x0, [dhn0_z], dx1,
        nw[0], scale[0], shift[0] + zero_row(l0_token))

    small_g = [_pack_small([dsh0, dsc0, dgate0, dsh1, dsc1, dgate1, g_nw0, g_nw1, g_fnw], g_dtb, [dalog_f, dalog_b], g_d, loss_part)]
    sm_handles, sm_token = _exchange_start("small_grads_start", small_g, _landing_zones("small_grads_place", small_g, "gather"),
                                           "gather", dx0)

    whole = (slice(None), slice(None))
    r_so, r_si, r_small = _exchange_wait("l1_grads_wait", l1_handles, "scatter", sm_token)
    si_out = [o.T for o in _adamw("adamw_ssd_w_in", ssd_w_in[0].T, r_si, m_ssd_w_in[0].T, v_ssd_w_in[0].T, n_ssd_in // NDEV, 256)]
    so_out = _adamw("adamw_ssd_w_out", ssd_w_out[0], r_so, m_ssd_w_out[0], v_ssd_w_out[0], 256)
    cw_cols = ssd_conv_w.shape[2]
    cw_out, cb_out, snw_out = _adamw_windows(
        "adamw_ssd_small", r_small,
        [(ssd_conv_w, m_ssd_conv_w, v_ssd_conv_w), (ssd_conv_b, m_ssd_conv_b, v_ssd_conv_b),
         (ssd_norm_w, m_ssd_norm_w, v_ssd_norm_w)],
        [(0, slice(0, CONV_WIDTH), slice(0, cw_cols), (0, slice(None), slice(None))),
         (1, slice(5, 6), slice(0, cw_cols), whole), (2, slice(6, 7), slice(0, si // NDEV), whole)])
    r_ai, r_ao = _exchange_wait("l0_grads_wait", l0_handles, "chips", so_out[0])
    ai_out = _adamw("adamw_attn_w_in", attn_w_in[0], r_ai, m_attn_w_in[0], v_attn_w_in[0], 256)
    ao_out = _adamw("adamw_attn_w_out", attn_w_out[0], r_ao, m_attn_w_out[0], v_attn_w_out[0], 192)

    (small_all,) = _exchange_wait("small_grads_wait", sm_handles, "gather", ai_out[0])
    full = slice(0, PACK_COLS)
    nhd = SSD_HEADS
    windows = [(0, slice(3 * i + k, 3 * i + k + 1), full, (slice(i, i + 1), slice(k * dm, (k + 1) * dm)))
               for i in range(2) for k in range(3)]
    windows += [(1, slice(6 + i, 7 + i), full, (slice(i, i + 1), slice(None))) for i in range(2)]
    windows += [(2, slice(8, 9), full, whole)]
    windows += [(3 + q, slice(9, 10), slice(2 * nhd * q + nhd * j, 2 * nhd * q + nhd * (j + 1)), (0, slice(j, j + 1), slice(None)))
                for q in range(2) for j in range(2)]
    windows += [(5, slice(9, 10), slice(4 * nhd, 5 * nhd), whole)]
    as_row = lambda a: a.reshape(1, dm)
    mb_out, nw_out, fnw_out, dtb_out, alog_out, d_out, loss = _adamw_windows(
        "adamw_small", small_all,
        [(mod_b, m_mod_b, v_mod_b), (norm_w, m_norm_w, v_norm_w), (fnw, as_row(m_final_norm_w), as_row(v_final_norm_w)),
         (ssd_dt_bias, m_ssd_dt_bias, v_ssd_dt_bias), (ssd_a_log, m_ssd_a_log, v_ssd_a_log), (ssd_d, m_ssd_d, v_ssd_d)],
        windows, extra=(slice(9, 10), slice(256, 257)))
    loss = loss.reshape(())

    ncol = mod_w.shape[2]
    dmod_all = small_all[:, 0:6, :].reshape(NDEV, 2, 3 * dm)
    dmod_sh = lax.dynamic_slice_in_dim(dmod_all, me * ncol, ncol, axis=2).transpose(1, 0, 2)
    g_modw = _mod_grad(c_all, dmod_sh).reshape(1, 2 * dm, ncol)
    modw_out = _adamw("adamw_mod_w", mod_w.reshape(2 * dm, ncol), g_modw, m_mod_w.reshape(2 * dm, ncol),
                      v_mod_w.reshape(2 * dm, ncol), 256)

    per_kind = []
    for k in range(4):
        per_kind.append([
            nw_out[k], modw_out[k].reshape(mod_w.shape), mb_out[k], ai_out[k][None], ao_out[k][None], si_out[k][None],
            cw_out[k], cb_out[k], dtb_out[k], alog_out[k], d_out[k], snw_out[k], so_out[k][None], fnw_out[k].reshape(dm)])
    return (loss, dx0.reshape(x.shape), *per_kind[0], *per_kind[1], *per_kind[2], *per_kind[3])


def _pack_ssd_small_blocks(g_cw, g_cb, g_nw):
    nper = g_cw.shape[1] // NDEV
    nwper = g_nw.shape[1] // NDEV

    def body(cw_ref, cb_ref, nw_ref, o_ref):
        o_ref[...] = jnp.zeros_like(o_ref)
        for d in range(NDEV):
            o_ref[d, 0:5, :] = cw_ref[:, d * nper:(d + 1) * nper]
            o_ref[d, 5:6, :] = cb_ref[:, d * nper:(d + 1) * nper]
            o_ref[d, 6:7, 0:nwper] = nw_ref[:, d * nwper:(d + 1) * nwper]

    return pl.pallas_call(body, name="pack_ssd_small_grads", out_shape=jax.ShapeDtypeStruct((NDEV, 8, nper), F32))(g_cw, g_cb, g_nw)
```

```python
import functools
import math

import jax
import jax.numpy as jnp
from jax import lax
from jax.experimental import pallas as pl
from jax.experimental.pallas import tpu as pltpu

F32 = jnp.float32
BF16 = jnp.bfloat16
HI = lax.Precision.HIGHEST
MESH = pl.DeviceIdType.MESH
NDEV = 8

NORM_EPS = 1e-6
ROPE_THETA = 500000.0
ROT_DIM = 16
HEAD_DIM = 64
DILATIONS = (1, 4, 16)
BAND = 64
NEG_BIG = -1e30
CHUNK = 128
SSD_HEADS = 32
SSD_GROUPS = 8
CONV_WIDTH = 5

ADAM_LR = 0.001
ADAM_B1 = 0.9
ADAM_B2 = 0.999
ADAM_EPS = 1e-08
ADAM_WD = 0.01
ADAM_STEP = 10

VMEM_BIG = 56 * 1024 * 1024
MM_T = 1024


def _params(sem=None, vmem=None):
    kw = {}
    if sem is not None:
        kw["dimension_semantics"] = sem
    if vmem is not None:
        kw["vmem_limit_bytes"] = vmem
    return pltpu.CompilerParams(**kw)


def _dg(a, b, ca, cb, prec=None):
    return lax.dot_general(a, b, (((ca,), (cb,)), ((), ())), preferred_element_type=F32, precision=prec)


def _nn(a, b):
    return _dg(a.astype(BF16), b.astype(BF16), 1, 0)


def _nt(a, b):
    return _dg(a.astype(BF16), b.astype(BF16), 1, 1)


def _tn(a, b):
    return _dg(a.astype(BF16), b.astype(BF16), 0, 0)


def _hnn(a, b):
    return _dg(a, b, 1, 0, HI)


@jax.custom_vjp
def _bnn(a, b):
    return _nn(a, b)


_bnn.defvjp(lambda a, b: (_nn(a, b), (a, b)), lambda r, g: (_nt(g, r[1]), _tn(r[0], g)))


@jax.custom_vjp
def _bnt(a, b):
    return _nt(a, b)


_bnt.defvjp(lambda a, b: (_nt(a, b), (a, b)), lambda r, g: (_nn(g, r[1]), _tn(g, r[0])))


@jax.custom_vjp
def _btn(a, b):
    return _tn(a, b)


_btn.defvjp(lambda a, b: (_tn(a, b), (a, b)), lambda r, g: (_nt(r[1], g), _nn(r[0], g)))


def _silu(x):
    return x * jax.nn.sigmoid(x)


def _b_spec(b, mode, tn, tk, no, ko, jk):
    if mode == "nt":
        return pl.BlockSpec((tn, tk), lambda *g: (jk(*g)[0] + no, jk(*g)[1] + ko))
    return pl.BlockSpec((tk, tn), lambda *g: (jk(*g)[1] + ko, jk(*g)[0] + no))


def _matmul(name, a, b, mode, out_dtype, tm, tn, tk, *, epilogue=None, tiled=(), mrows=(), ncols=(),
            b_noff=0, b_koff=0, n_out=None, out_blocks=None, dest=None):
    if mode == "tn":
        K, M = a.shape
    else:
        M, K = a.shape
    N = n_out if n_out is not None else (b.shape[0] if mode == "nt" else b.shape[1])
    tm, tn, tk = min(tm, M), min(tn, N), min(tk, K)
    assert M % tm == 0 and N % tn == 0 and K % tk == 0, (name, M, N, K, tm, tn, tk)
    assert b_noff % tn == 0 and b_koff % tk == 0
    no, ko = b_noff // tn, b_koff // tk
    nk = K // tk
    if mode == "tn":
        a_spec = pl.BlockSpec((tk, tm), lambda i, j, k: (k, i))
    else:
        a_spec = pl.BlockSpec((tm, tk), lambda i, j, k: (i, k))
    specs = [a_spec, _b_spec(b, mode, tn, tk, no, ko, lambda i, j, k: (j, k))]
    specs += [pl.BlockSpec((tm, tn), lambda i, j, k: (i, j)) for _ in tiled]
    specs += [pl.BlockSpec((tm, r.shape[1]), lambda i, j, k: (i, 0)) for r in mrows]
    specs += [pl.BlockSpec((1, tn), lambda i, j, k: (0, j)) for _ in ncols]
    total, off, earlier = dest if dest is not None else (None, 0, None)
    if out_blocks is None:
        assert off % tm == 0
        mo = off // tm
        out_shape = jax.ShapeDtypeStruct((M if total is None else total, N), out_dtype)
        out_spec = pl.BlockSpec((tm, tn), lambda i, j, k: (i + mo, j))
    else:
        nper = N // out_blocks
        assert nper % tn == 0
        jb = nper // tn
        out_shape = jax.ShapeDtypeStruct((out_blocks if total is None else total, M, nper), out_dtype)
        out_spec = pl.BlockSpec((None, tm, tn), lambda i, j, k: (j // jb + off, i, j % jb))
    if earlier is not None:
        assert earlier.shape == out_shape.shape and earlier.dtype == out_shape.dtype
    ne = len(tiled) + len(mrows) + len(ncols)
    dot = {"nn": _nn, "nt": _nt, "tn": _tn}[mode]

    def body(a_ref, b_ref, *rest):
        extras, o_ref = rest[:ne], rest[ne]

        def finish(acc):
            if epilogue is not None:
                acc = epilogue(acc, *[e[...] for e in extras])
            o_ref[...] = acc.astype(o_ref.dtype)

        if nk == 1:
            finish(dot(a_ref[...], b_ref[...]))
        else:
            acc_ref = rest[ne + 1]
            k = pl.program_id(2)

            @pl.when(k == 0)
            def _():
                acc_ref[...] = jnp.zeros_like(acc_ref)

            acc_ref[...] += dot(a_ref[...], b_ref[...])

            @pl.when(k == nk - 1)
            def _():
                finish(acc_ref[...])

    args = [a, b, *tiled, *mrows, *ncols]
    aliases = {}
    if earlier is not None:
        specs.append(pl.BlockSpec(memory_space=pl.ANY))
        aliases = {len(args): 0}
        args.append(earlier)

    def body_with_dest(*refs):
        body(*refs[:2 + ne], *refs[2 + ne + (earlier is not None):])

    return pl.pallas_call(
        body_with_dest, name=name, out_shape=out_shape, grid=(M // tm, N // tn, nk),
        in_specs=specs, out_specs=out_spec, input_output_aliases=aliases,
        scratch_shapes=[] if nk == 1 else [pltpu.VMEM((tm, tn), F32)],
        compiler_params=_params(("parallel", "parallel", "arbitrary"), VMEM_BIG),
    )(*args)


def _matmul_rows(name, a, b, mode, tm, tk, fn, rows, consts, outs, accs, *, n_out=None, b_noff=0, b_koff=0):
    rl = [(t, t.shape[1], 0) if not isinstance(t, tuple) else t for t in rows]
    make_a = a if callable(a) else None
    M, K = (rl[0][0].shape[0], b.shape[1 if mode == "nt" else 0]) if make_a else a.shape
    N = n_out if n_out is not None else (b.shape[0] if mode == "nt" else b.shape[1])
    tm, tk = min(tm, M), min(tk, K)
    assert M % tm == 0 and K % tk == 0 and b_koff % tk == 0 and b_noff % N == 0, (name, M, N, K)
    no, ko, nk = b_noff // N, b_koff // tk, K // tk
    assert make_a is None or nk == 1
    nr, nc, no_, na = len(rl), len(consts), len(outs), len(accs)
    dot = _nt if mode == "nt" else _nn

    def body(*refs):
        a_ref, b_ref, rest = (None, refs[0], refs[1:]) if make_a else (refs[0], refs[1], refs[2:])
        r_refs, c_refs = rest[:nr], rest[nr:nr + nc]
        o_refs, acc_refs = rest[nr + nc:nr + nc + no_], rest[nr + nc + no_:nr + nc + no_ + na]
        i, k = pl.program_id(0), pl.program_id(1)

        def finish(prod, *made):
            res_o, res_a = fn(prod, *made, *[r[...] for r in r_refs], *[c[...] for c in c_refs])
            for r, v in zip(o_refs, res_o, strict=True):
                r[...] = v.astype(r.dtype)
            if acc_refs:
                @pl.when(i == 0)
                def _():
                    for r in acc_refs:
                        r[...] = jnp.zeros_like(r)

                for r, v in zip(acc_refs, res_a, strict=True):
                    r[...] += v

        if make_a:
            left = make_a(*[r[...] for r in r_refs], *[c[...] for c in c_refs])
            finish(dot(left, b_ref[...]), left)
        elif nk == 1:
            finish(dot(a_ref[...], b_ref[...]))
        else:
            prod_ref = rest[-1]

            @pl.when(k == 0)
            def _():
                prod_ref[...] = jnp.zeros_like(prod_ref)

            prod_ref[...] += dot(a_ref[...], b_ref[...])

            @pl.when(k == nk - 1)
            def _():
                finish(prod_ref[...])

    b_spec = _b_spec(b, mode, N, tk, no, ko, lambda i, k: (0, k))
    in_specs = ([] if make_a else [pl.BlockSpec((tm, tk), lambda i, k: (i, k))]) + [b_spec]
    in_specs += [pl.BlockSpec((tm, w), functools.partial(lambda i, k, cb: (i, cb), cb=cb)) for (_, w, cb) in rl]
    in_specs += [pl.BlockSpec(c.shape, lambda i, k: (0, 0)) for c in consts]
    out_specs = [pl.BlockSpec((tm, c), lambda i, k: (i, 0)) for (c, _) in outs]
    out_specs += [pl.BlockSpec(shp, lambda i, k: (0, 0)) for shp in accs]
    out_shape = [jax.ShapeDtypeStruct((M, c), dt) for (c, dt) in outs] + [jax.ShapeDtypeStruct(shp, F32) for shp in accs]
    res = pl.pallas_call(
        body, name=name, out_shape=out_shape, grid=(M // tm, nk), in_specs=in_specs, out_specs=out_specs,
        scratch_shapes=[] if nk == 1 else [pltpu.VMEM((tm, N), F32)],
        compiler_params=_params(("arbitrary" if accs else "parallel", "arbitrary"), VMEM_BIG),
    )(*([] if make_a else [a]), b, *[t[0] for t in rl], *consts)
    return res[:no_], res[no_:]


def _rowwise(name, fn, tiled, consts, outs, accs, ts):
    tl = [(t, t.shape[1], 0) if not isinstance(t, tuple) else t for t in tiled]
    s_len = tl[0][0].shape[0]
    assert s_len % ts == 0
    nt_, nc_, no_ = len(tl), len(consts), len(outs)

    def body(*refs):
        t_refs, c_refs = refs[:nt_], refs[nt_:nt_ + nc_]
        o_refs, a_refs = refs[nt_ + nc_:nt_ + nc_ + no_], refs[nt_ + nc_ + no_:]
        res_o, res_a = fn(*[r[...] for r in t_refs], *[r[...] for r in c_refs])
        for r, v in zip(o_refs, res_o, strict=True):
            r[...] = v.astype(r.dtype)
        if a_refs:
            @pl.when(pl.program_id(0) == 0)
            def _():
                for r in a_refs:
                    r[...] = jnp.zeros_like(r)

            for r, v in zip(a_refs, res_a, strict=True):
                r[...] += v

    in_specs = [pl.BlockSpec((ts, w), functools.partial(lambda i, cb: (i, cb), cb=cb)) for (_, w, cb) in tl]
    in_specs += [pl.BlockSpec(c.shape, lambda i: (0, 0)) for c in consts]
    out_specs = [pl.BlockSpec((ts, c), lambda i: (i, 0)) for (c, _) in outs]
    out_specs += [pl.BlockSpec(shp, lambda i: (0, 0)) for shp in accs]
    out_shape = [jax.ShapeDtypeStruct((s_len, c), dt) for (c, dt) in outs]
    out_shape += [jax.ShapeDtypeStruct(shp, F32) for shp in accs]
    res = pl.pallas_call(
        body, name=name, out_shape=out_shape, grid=(s_len // ts,), in_specs=in_specs, out_specs=out_specs,
        compiler_params=_params(("arbitrary",) if accs else ("parallel",), VMEM_BIG),
    )(*[t[0] for t in tl], *consts)
    return res[:no_], res[no_:]


def _norm_mod_fn(x, nw, sc, sh):
    r = lax.rsqrt(jnp.mean(x * x, axis=-1, keepdims=True) + NORM_EPS)
    return (x * r * nw) * (1.0 + sc) + sh


def _norm_mod_fwd(name, x, nw, sc, sh):
    (hn,), _ = _rowwise(name, lambda x, nw, sc, sh: ([_norm_mod_fn(x, nw, sc, sh)], []),
                        [x], [nw, sc, sh], [(x.shape[1], BF16)], [], 512)
    return hn


def _norm_mod_bwd(name, last, x, dhn_parts, dres, nw, sc, sh, prev=None):
    n = len(dhn_parts)
    d = x.shape[1]
    a, b, mode, tk, kw = last

    def fn(dhn, x, *rest):
        for p in rest[:n]:
            dhn = dhn + p
        dres, rest = rest[n], rest[n + 1:]
        y_prev, (nw, sc, sh), gate = (rest[0], rest[1:4], rest[4]) if prev is not None else (None, rest[0:3], None)
        r = lax.rsqrt(jnp.mean(x * x, axis=-1, keepdims=True) + NORM_EPS)
        xh = x * r
        dxh = dhn * (nw * (1.0 + sc))
        dx = r * (dxh - xh * jnp.mean(dxh * xh, axis=-1, keepdims=True)) + dres
        along = jnp.sum(dhn * xh, axis=0, keepdims=True)
        dnw, dsc, dsh = along * (1.0 + sc), along * nw, jnp.sum(dhn, axis=0, keepdims=True)
        if prev is None:
            return [dx], [dnw, dsc, dsh]
        return [dx, gate * dx], [dnw, dsc, dsh, jnp.sum(dx * y_prev, axis=0, keepdims=True)]

    rows = [x, *dhn_parts, dres] + ([prev[0]] if prev is not None else [])
    consts = [nw, sc, sh] + ([prev[1]] if prev is not None else [])
    outs = [(d, F32)] + ([(d, BF16)] if prev is not None else [])
    res_o, res_a = _matmul_rows(name, a, b, mode, 512, tk, fn, rows, consts, outs, [(1, d)] * (3 + (prev is not None)), **kw)
    return (*res_o, *res_a)


def _rope_tables(pos_col, inv_row):
    def fn(pos, inv):
        ang = pos.astype(F32) * inv
        e = lax.broadcasted_iota(jnp.int32, (1, 128), 1) % HEAD_DIM
        cos, sin = jnp.cos(ang), jnp.sin(ang)
        half = ROT_DIM // 2
        return [jnp.where(e < ROT_DIM, cos, 1.0), jnp.where(e < half, -sin, 0.0),
                jnp.where((e >= half) & (e < ROT_DIM), sin, 0.0)], []

    (c, sa, sb), _ = _rowwise("rope_tables", fn, [pos_col], [inv_row], [(128, F32)] * 3, [], 512)
    return c, sa, sb


def _rot_fwd(t, c, sa, sb):
    n = t.shape[1]
    rep = n // 128
    c, sa, sb = (jnp.tile(u, (1, rep)) for u in (c, sa, sb))
    return t * c + pltpu.roll(t, n - ROT_DIM // 2, 1) * sa + pltpu.roll(t, ROT_DIM // 2, 1) * sb


def _rot_bwd(g, c, sa, sb):
    n = g.shape[1]
    rep = n // 128
    c, sa, sb = (jnp.tile(u, (1, rep)) for u in (c, sa, sb))
    return g * c + pltpu.roll(g * sa, ROT_DIM // 2, 1) + pltpu.roll(g * sb, n - ROT_DIM // 2, 1)


ATT_TQ = 128


def _attn_tiles(l):
    tk = ATT_TQ + 2 * BAND
    return (l, l) if l <= tk else (ATT_TQ, tk)


def _attn_specs(g, s_len):
    def blk(off):
        return pl.BlockSpec((s_len, 128), functools.partial(lambda hp, off: (0, off + hp), off=off))

    return blk(4 * g), blk(12 + 4 * g), blk(4 * g), blk(0)


def _attn_tile_geometry(t, d, l):
    tq, tk = _attn_tiles(l)
    nts = l // tq
    r = t // nts
    ts = t % nts
    q0 = ts * tq
    ws = jnp.clip(q0 - BAND, 0, l - tk)
    kind = jnp.where(ts == 0, 0, jnp.where(ts == nts - 1, 2, 1))
    if d == 1:
        return pl.ds(pl.multiple_of(q0, tq), tq), pl.ds(pl.multiple_of(ws, BAND), tk), kind
    return pl.ds(r + d * q0, tq, stride=d), pl.ds(r + d * ws, tk, stride=d), kind


def _attn_fill_bias(bias_ref):
    _, tq2, tk = bias_ref.shape
    iq = lax.broadcasted_iota(jnp.int32, (tq2, 1), 0) % (tq2 // 2)
    ik = lax.broadcasted_iota(jnp.int32, (1, tk), 1)
    for i, off in enumerate((0, -BAND, -2 * BAND)):
        bias_ref[i] = jnp.where(jnp.abs(ik + off - iq) <= BAND, 0.0, NEG_BIG)


def _split_heads(t, in_h):
    zero = jnp.zeros_like(t)
    return jnp.concatenate([jnp.where(in_h[0], t, zero), jnp.where(in_h[1], t, zero)], axis=0)


def _attn_fwd(g, qk, v):
    s_len = qk.shape[0]
    d = DILATIONS[g]
    l = s_len // d
    tq, tk = _attn_tiles(l)
    assert l % tq == 0 and l >= tk
    q_spec, k_spec, v_spec, o_spec = _attn_specs(g, s_len)
    scale = 1.0 / math.sqrt(HEAD_DIM)

    def body(q_ref, k_ref, v_ref, o_ref, lse_ref, bias_ref):
        lane = lax.broadcasted_iota(jnp.int32, (1, 128), 1)
        in_h = [lane < HEAD_DIM, lane >= HEAD_DIM]
        _attn_fill_bias(bias_ref)

        def tile(t, carry):
            rows, win, kind = _attn_tile_geometry(t, d, l)
            q = (q_ref[rows, :] * scale).astype(BF16)
            k = k_ref[win, :].astype(BF16)
            vv = v_ref[win, :].astype(BF16)
            s = _nt(_split_heads(q, in_h), k) + bias_ref[kind]
            m = jnp.max(s, axis=1, keepdims=True)
            p = jnp.exp(s - m)
            den = jnp.sum(p, axis=1, keepdims=True)
            out = _nn(p, vv) / den
            lse = m + jnp.log(den)
            o_ref[rows, :] = jnp.where(in_h[0], out[:tq], out[tq:])
            lse_ref[rows, :] = jnp.where(in_h[0], lse[:tq], lse[tq:])
            return carry

        lax.fori_loop(0, s_len // tq, tile, 0, unroll=4 * ATT_TQ // tq)

    return pl.pallas_call(
        body, name=f"attn_fwd_g{g}", grid=(4,),
        out_shape=[jax.ShapeDtypeStruct((s_len, 512), F32)] * 2,
        in_specs=[q_spec, k_spec, v_spec], out_specs=[o_spec, o_spec],
        scratch_shapes=[pltpu.VMEM((3, 2 * tq, tk), F32)],
        compiler_params=_params(("parallel",), VMEM_BIG),
    )(qk, qk, v)


def _attn_bwd(g, qk, v, o, lse, do, dlse):
    s_len = qk.shape[0]
    d = DILATIONS[g]
    l = s_len // d
    tq, tk = _attn_tiles(l)
    q_spec, k_spec, v_spec, o_spec = _attn_specs(g, s_len)
    scale = 1.0 / math.sqrt(HEAD_DIM)

    def body(q_ref, k_ref, v_ref, o_ref, lse_ref, do_ref, dlse_ref, dq_ref, dk_ref, dv_ref, bias_ref):
        lane = lax.broadcasted_iota(jnp.int32, (1, 128), 1)
        in_h = [lane < HEAD_DIM, lane >= HEAD_DIM]
        dk_ref[...] = jnp.zeros_like(dk_ref)
        dv_ref[...] = jnp.zeros_like(dv_ref)
        _attn_fill_bias(bias_ref)

        def tile(t, carry):
            rows, win, kind = _attn_tile_geometry(t, d, l)
            k, vv = k_ref[win, :].astype(BF16), v_ref[win, :].astype(BF16)
            dout, lse_t, dlse_t = do_ref[rows, :], lse_ref[rows, :], dlse_ref[rows, :]
            od = dout * o_ref[rows, :]
            q2 = _split_heads((q_ref[rows, :] * scale).astype(BF16), in_h)
            do2 = _split_heads(dout.astype(BF16), in_h)
            head_col = lambda a: jnp.concatenate([a[:, 0:1], a[:, HEAD_DIM:HEAD_DIM + 1]], axis=0)
            delta = jnp.concatenate([jnp.sum(jnp.where(m, od, 0.0), axis=1, keepdims=True) for m in in_h], axis=0)
            p = jnp.exp(_nt(q2, k) + bias_ref[kind] - head_col(lse_t))
            ds = (p * (_nt(do2, vv) - delta + head_col(dlse_t))).astype(BF16)
            dq2 = _nn(ds, k) * scale
            dq_ref[rows, :] = jnp.where(in_h[0], dq2[:tq], dq2[tq:])
            dk_ref[win, :] += _tn(ds, q2)
            dv_ref[win, :] += _tn(p, do2)
            return carry

        lax.fori_loop(0, s_len // tq, tile, 0, unroll=4 * ATT_TQ // tq)

    return pl.pallas_call(
        body, name=f"attn_bwd_g{g}", grid=(4,),
        out_shape=[jax.ShapeDtypeStruct((s_len, 512), F32)] * 3,
        in_specs=[q_spec, k_spec, v_spec, o_spec, o_spec, o_spec, o_spec], out_specs=[o_spec] * 3,
        scratch_shapes=[pltpu.VMEM((3, 2 * tq, tk), F32)],
        compiler_params=_params(("parallel",), VMEM_BIG),
    )(qk, qk, v, o, lse, do, dlse)


def _mix_weights(ls):
    mx = jnp.maximum(jnp.maximum(ls[0], ls[1]), ls[2])
    es = [jnp.exp(x - mx) for x in ls]
    tot = es[0] + es[1] + es[2]
    return [e / tot for e in es]


def _attn_out(os_, lses, z, x, gate, w_out):
    s_len, dm = x.shape
    tm = 256
    wdt = 512
    z, z_block = z

    def body(o0, o1, o2, l0, l1, l2, z_ref, x_ref, g_ref, w_ref, a_ref, y_ref, x1_ref):
        alphas = _mix_weights([l0[...], l1[...], l2[...]])
        y = jnp.zeros((tm, dm), F32)
        for g, o_ref in enumerate((o0, o1, o2)):
            a_g = (o_ref[...] * alphas[g] * _silu(z_ref[:, g * wdt:(g + 1) * wdt])).astype(BF16)
            a_ref[:, g * wdt:(g + 1) * wdt] = a_g
            y = y + _nn(a_g, w_ref[g * wdt:(g + 1) * wdt, :])
        y_ref[...] = y
        x1_ref[...] = x_ref[...] + g_ref[...] * y

    row = lambda c: pl.BlockSpec((tm, c), lambda i: (i, 0))
    return pl.pallas_call(
        body, name="attn_out", grid=(s_len // tm,),
        out_shape=[jax.ShapeDtypeStruct((s_len, 3 * wdt), BF16), jax.ShapeDtypeStruct((s_len, dm), F32),
                   jax.ShapeDtypeStruct((s_len, dm), F32)],
        in_specs=[row(wdt)] * 6 + [pl.BlockSpec((tm, 3 * wdt), lambda i: (i, z_block)), row(dm),
                                   pl.BlockSpec((1, dm), lambda i: (0, 0)), pl.BlockSpec(w_out.shape, lambda i: (0, 0))],
        out_specs=[row(3 * wdt), row(dm), row(dm)],
        compiler_params=_params(("parallel",), VMEM_BIG),
    )(*os_, *lses, z, x, gate, w_out)


def _mix_bwd(dy, w_out, os_, lses, z):
    wdt = 512

    def fn(da, o0, o1, o2, l0, l1, l2, z):
        os_t, ls = [o0, o1, o2], [l0, l1, l2]
        alphas = _mix_weights(ls)
        hi = lax.broadcasted_iota(jnp.int32, (2 * wdt, wdt), 0) % wdt // HEAD_DIM
        hj = lax.broadcasted_iota(jnp.int32, (2 * wdt, wdt), 1) // HEAD_DIM
        seg = (hi == hj).astype(BF16)
        head_sum = lambda t: _dg(jnp.concatenate(_bf16_parts(t, 2), axis=1), seg, 1, 0)
        dos, dal, dzs = [], [], []
        for g in range(3):
            zg = z[:, g * wdt:(g + 1) * wdt]
            sig = jax.nn.sigmoid(zg)
            dag = da[:, g * wdt:(g + 1) * wdt]
            dmix = dag * zg * sig
            dzs.append(dag * os_t[g] * alphas[g] * (sig * (1.0 + zg * (1.0 - sig))))
            dos.append(dmix * alphas[g])
            dal.append(head_sum(dmix * os_t[g]))
        mean = alphas[0] * dal[0] + alphas[1] * dal[1] + alphas[2] * dal[2]
        dls = [alphas[g] * (dal[g] - mean) for g in range(3)]
        return dos + dls + [jnp.concatenate(dzs, axis=1)], []

    outs, _ = _matmul_rows("attn_out_dx_mix_bwd", dy, w_out, "nt", 256, dy.shape[1], fn, [*os_, *lses, (z[0], 3 * wdt, z[1])], [],
                           [(wdt, F32)] * 6 + [(3 * wdt, BF16)], [])
    return outs[:3], outs[3:6], outs[6]


def _rot_pack_bwd(dqs, dks, dvs, tabs):
    wdt = 512

    def fn(*args):
        grads, (c, sa, sb) = args[:9], args[9:]
        cols = [_rot_bwd(gq, c, sa, sb) for gq in grads[:6]] + list(grads[6:])
        return [jnp.concatenate(cols, axis=1)], []

    (out,), _ = _rowwise("rot_pack_bwd", fn, [*dqs, *dks, *dvs, *tabs], [], [(9 * wdt, BF16)], [], 512)
    return out


CONV_CB = 128
CONV_R = 256
CONV_PAD = 8


def _conv_taps(buf, base, off, sign):
    return [buf[pl.ds(base + off + sign * j, CONV_R), :] for j in range(CONV_WIDTH)]


def _conv_tap_sum(taps, w):
    acc = None
    for j, t in enumerate(taps):
        term = t * w[j:j + 1, :]
        acc = term if acc is None else acc + term
    return acc


def _conv_fwd(xpre, cw, cb):
    s_len, ch = xpre.shape
    nchunk = s_len // CONV_R

    def body(x_ref, w_ref, b_ref, o_ref, xp):
        zero = jnp.zeros((CONV_PAD, CONV_CB), F32)
        xp[0:CONV_PAD, :] = zero
        xp[s_len + CONV_PAD:s_len + 2 * CONV_PAD, :] = zero

        def fill(ci, carry):
            base = pl.multiple_of(ci * CONV_R, CONV_R)
            xp[pl.ds(base + CONV_PAD, CONV_R), :] = x_ref[pl.ds(base, CONV_R), :]
            return carry

        lax.fori_loop(0, nchunk, fill, 0)
        w = w_ref[...]
        b = b_ref[...]

        def chunk(ci, carry):
            base = pl.multiple_of(ci * CONV_R, CONV_R)
            u = _conv_tap_sum(_conv_taps(xp, base, CONV_PAD - CONV_WIDTH // 2, 1), w) + b
            o_ref[pl.ds(base, CONV_R), :] = _silu(u)
            return carry

        lax.fori_loop(0, nchunk, chunk, 0, unroll=2)

    col = lambda r: pl.BlockSpec((r, CONV_CB), lambda j: (0, j))
    return pl.pallas_call(
        body, name="conv_fwd", grid=(ch // CONV_CB,), out_shape=jax.ShapeDtypeStruct((s_len, ch), F32),
        in_specs=[col(s_len), col(CONV_WIDTH), col(1)], out_specs=col(s_len),
        scratch_shapes=[pltpu.VMEM((s_len + 2 * CONV_PAD, CONV_CB), F32)],
        compiler_params=_params(("parallel",), VMEM_BIG),
    )(xpre, cw, cb)


def _conv_bwd(xpre, da, cw, cb):
    s_len, ch = xpre.shape
    nchunk = s_len // CONV_R
    half = CONV_WIDTH // 2

    def body(x_ref, da_ref, w_ref, b_ref, dx_ref, gw_ref, gb_ref, xp, dcp):
        zero = jnp.zeros((CONV_PAD, CONV_CB), F32)
        for buf in (xp, dcp):
            buf[0:CONV_PAD, :] = zero
            buf[s_len + CONV_PAD:s_len + 2 * CONV_PAD, :] = zero

        def fill(ci, carry):
            base = pl.multiple_of(ci * CONV_R, CONV_R)
            xp[pl.ds(base + CONV_PAD, CONV_R), :] = x_ref[pl.ds(base, CONV_R), :]
            return carry

        lax.fori_loop(0, nchunk, fill, 0)
        w = w_ref[...]
        b = b_ref[...]

        def first(ci, carry):
            base = pl.multiple_of(ci * CONV_R, CONV_R)
            taps = _conv_taps(xp, base, CONV_PAD - half, 1)
            u = _conv_tap_sum(taps, w) + b
            sig = jax.nn.sigmoid(u)
            dc = da_ref[pl.ds(base, CONV_R), :] * (sig * (1.0 + u * (1.0 - sig)))
            dcp[pl.ds(base + CONV_PAD, CONV_R), :] = dc
            gb = carry[0] + jnp.sum(dc, axis=0, keepdims=True)
            gws = [carry[1 + j] + jnp.sum(dc * taps[j], axis=0, keepdims=True) for j in range(CONV_WIDTH)]
            return (gb, *gws)

        z1 = jnp.zeros((1, CONV_CB), F32)
        sums = lax.fori_loop(0, nchunk, first, (z1,) * (1 + CONV_WIDTH), unroll=2)
        gb_ref[...] = sums[0]
        for j in range(CONV_WIDTH):
            gw_ref[j:j + 1, :] = sums[1 + j]

        def second(ci, carry):
            base = pl.multiple_of(ci * CONV_R, CONV_R)
            dx_ref[pl.ds(base, CONV_R), :] = _conv_tap_sum(_conv_taps(dcp, base, CONV_PAD + half, -1), w).astype(dx_ref.dtype)
            return carry

        lax.fori_loop(0, nchunk, second, 0, unroll=2)

    col = lambda r: pl.BlockSpec((r, CONV_CB), lambda j: (0, j))
    return pl.pallas_call(
        body, name="conv_bwd", grid=(ch // CONV_CB,),
        out_shape=[jax.ShapeDtypeStruct((s_len, ch), BF16), jax.ShapeDtypeStruct((CONV_WIDTH, ch), F32),
                   jax.ShapeDtypeStruct((1, ch), F32)],
        in_specs=[col(s_len), col(s_len), col(CONV_WIDTH), col(1)],
        out_specs=[col(s_len), col(CONV_WIDTH), col(1)],
        scratch_shapes=[pltpu.VMEM((s_len + 2 * CONV_PAD, CONV_CB), F32)] * 2,
        compiler_params=_params(("parallel",), VMEM_BIG),
    )(xpre, da, cw, cb)


SSD_GW = 256
SSD_N = 128
SSD_DTW = 128


def _bf16_parts(x, n):
    parts, rest = [], x
    for _ in range(n):
        p = rest.astype(BF16)
        parts.append(p)
        rest = rest - p.astype(F32)
    return parts


@jax.custom_vjp
def _expand(x, e):
    eb = e.astype(BF16)
    return _dg(jnp.concatenate(_bf16_parts(x, 2), axis=1), jnp.concatenate([eb, eb], axis=0), 1, 0)


def _expand_fwd(x, e):
    return _expand(x, e), e


def _expand_bwd(e, g):
    return _dg(g.astype(BF16), e.astype(BF16), 1, 1), jnp.zeros_like(e)


_expand.defvjp(_expand_fwd, _expand_bwd)


@jax.custom_vjp
def _running_sum(tri, x):
    tb = tri.astype(BF16)
    return sum(_dg(tb, p, 1, 0) for p in _bf16_parts(x, 3))


def _running_sum_fwd(tri, x):
    return _running_sum(tri, x), tri


def _running_sum_bwd(tri, g):
    tb = tri.astype(BF16)
    return jnp.zeros_like(tri), sum(_dg(tb, p, 0, 0) for p in _bf16_parts(g, 3))


_running_sum.defvjp(_running_sum_fwd, _running_sum_bwd)


def _pick_col(a, h):
    @jax.custom_vjp
    def pick(a):
        return a[:, h:h + 1]

    pick.defvjp(lambda a: (a[:, h:h + 1], None),
                lambda _, g: (g * (lax.broadcasted_iota(jnp.int32, (1, a.shape[1]), 1) == h).astype(F32),))
    return pick(a)


def _pick_row(a, h):
    @jax.custom_vjp
    def pick(a):
        return a[h:h + 1, :]

    pick.defvjp(lambda a: (a[h:h + 1, :], None),
                lambda _, g: (g * (lax.broadcasted_iota(jnp.int32, (a.shape[0], 1), 0) == h).astype(F32),))
    return pick(a)


def _ssd_mask(dirn):
    ri = lax.broadcasted_iota(jnp.int32, (CHUNK, CHUNK), 0)
    cj = lax.broadcasted_iota(jnp.int32, (CHUNK, CHUNK), 1)
    return (cj <= ri) if dirn == 0 else (cj >= ri)


def _ssd_rowsel(dirn):
    last = CHUNK - 1 if dirn == 0 else 0
    return (lax.broadcasted_iota(jnp.int32, (CHUNK, 1), 0) == last).astype(F32)


def _ssd_chunk_pre(dirn):
    nh = SSD_DTW

    def f(dt, alog):
        da = dt * (-jnp.exp(alog))
        cum = _running_sum(_ssd_mask(dirn).astype(F32), da)
        tot = jnp.sum(cum * _ssd_rowsel(dirn), axis=0, keepdims=True)
        hh = lax.broadcasted_iota(jnp.int32, (nh, SSD_HEADS * HEAD_DIM), 0)
        jj = lax.broadcasted_iota(jnp.int32, (nh, SSD_HEADS * HEAD_DIM), 1)
        expand = (hh == dirn * SSD_HEADS + jj // HEAD_DIM).astype(F32)
        return cum, cum.T, _expand(dt, expand), _expand(jnp.exp(tot - cum), expand), _expand(jnp.exp(cum), expand)

    return f


def _ssd_group_fn(g, dirn, stacked):
    def f(xs, bm, cm, st, cum, cum_t, dt_e, w_e, ce_e):
        mask = _ssd_mask(dirn)
        xdt = xs * dt_e
        cd_e = jnp.sum(ce_e * _ssd_rowsel(dirn), axis=0, keepdims=True)
        cb = _bnt(cm, bm)
        lane_head = lax.broadcasted_iota(jnp.int32, (1, SSD_GW), 1) // HEAD_DIM
        y = _bnn(cm, st) * ce_e
        decayed, inputs = [], []
        for j in range(4):
            hidx = dirn * SSD_HEADS + 4 * g + j
            col, row = _pick_col(cum, hidx), _pick_row(cum_t, hidx)
            dec = cb * jnp.exp(jnp.where(mask, col - row, NEG_BIG))
            head = (lane_head == j).astype(F32)
            if stacked:
                decayed.append(dec)
                inputs.append(xdt * head)
            else:
                y = y + _bnn(dec, xdt) * head
        if stacked:
            y = y + _bnn(jnp.concatenate(decayed, axis=1), jnp.concatenate(inputs, axis=0))
        st_out = st * cd_e + _btn(bm, xdt * w_e)
        return y, st_out

    return f


def _ssd_in_specs(kk):
    ln = CHUNK
    return [pl.BlockSpec((ln, 2048), lambda i: (kk(i), 0)),
            pl.BlockSpec((ln, 1024), lambda i: (kk(i), 2)),
            pl.BlockSpec((ln, 1024), lambda i: (kk(i), 3)),
            pl.BlockSpec((ln, SSD_DTW), lambda i: (kk(i), 0)),
            pl.BlockSpec((1, SSD_DTW), lambda i: (0, 0))]


def _ssd_fwd(xbc, dt, alog, dirn, prior=None):
    s_len = xbc.shape[0]
    nc = s_len // CHUNK
    kk = (lambda i: i) if dirn == 0 else (lambda i: nc - 1 - i)

    def body(x_ref, b_ref, c_ref, dt_ref, al_ref, *rest):
        prior_ref = rest[0] if prior is not None else None
        y_ref, sts_ref, st = rest[prior is not None:]

        @pl.when(pl.program_id(0) == 0)
        def _():
            st[...] = jnp.zeros_like(st)

        sts_ref[0] = st[...]
        cum, cum_t, dt_e, w_e, ce_e = _ssd_chunk_pre(dirn)(dt_ref[...], al_ref[...])
        for g in range(SSD_GROUPS):
            xc = slice(g * SSD_GW, (g + 1) * SSD_GW)
            gc = slice(g * SSD_N, (g + 1) * SSD_N)
            y, st_new = _ssd_group_fn(g, dirn, True)(x_ref[:, xc], b_ref[:, gc], c_ref[:, gc], st[:, xc], cum, cum_t,
                                               dt_e[:, xc], w_e[:, xc], ce_e[:, xc])
            y_ref[:, xc] = y if prior is None else y + prior_ref[:, xc]
            st[:, xc] = st_new

    return pl.pallas_call(
        body, name=f"ssd_fwd_d{dirn}", grid=(nc,),
        out_shape=[jax.ShapeDtypeStruct((s_len, 2048), F32), jax.ShapeDtypeStruct((nc, SSD_N, 2048), F32)],
        in_specs=_ssd_in_specs(kk) + ([pl.BlockSpec((CHUNK, 2048), lambda i: (kk(i), 0))] if prior is not None else []),
        out_specs=[pl.BlockSpec((CHUNK, 2048), lambda i: (kk(i), 0)),
                   pl.BlockSpec((1, SSD_N, 2048), lambda i: (kk(i), 0, 0))],
        scratch_shapes=[pltpu.VMEM((SSD_N, 2048), F32)],
        compiler_params=_params(("arbitrary",), VMEM_BIG),
    )(xbc, xbc, xbc, dt, alog, *([prior] if prior is not None else []))


def _ssd_bwd(xbc, dt, alog, states, dy, d_e, dirn, prior=None):
    s_len = xbc.shape[0]
    nc = s_len // CHUNK
    kk = (lambda i: nc - 1 - i) if dirn == 0 else (lambda i: i)

    def body(x_ref, b_ref, c_ref, dt_ref, al_ref, sts_ref, dy_ref, de_ref, *rest):
        prior_ref = rest[0] if prior is not None else None
        dx_ref, ddt_ref, dal_ref, dst = rest[prior is not None:]
        plus_prior = (lambda v, cols: v + prior_ref[:, cols]) if prior is not None else (lambda v, cols: v)

        @pl.when(pl.program_id(0) == 0)
        def _():
            dst[...] = jnp.zeros_like(dst)
            dal_ref[...] = jnp.zeros_like(dal_ref)

        (cum, cum_t, dt_e, w_e, ce_e), pre_vjp = jax.vjp(_ssd_chunk_pre(dirn), dt_ref[...], al_ref[...])
        dcum = jnp.zeros_like(cum)
        dcum_t = jnp.zeros_like(cum_t)
        d_dt_e, d_w_e, d_ce_e = [], [], []
        for g in range(SSD_GROUPS):
            xc = slice(g * SSD_GW, (g + 1) * SSD_GW)
            gc = slice(g * SSD_N, (g + 1) * SSD_N)
            _, vjp = jax.vjp(_ssd_group_fn(g, dirn, False), x_ref[:, xc], b_ref[:, gc], c_ref[:, gc], sts_ref[0, :, xc], cum, cum_t,
                             dt_e[:, xc], w_e[:, xc], ce_e[:, xc])
            dyg = dy_ref[:, xc]
            dxs, dbm, dcm, dst_g, dcum_g, dcum_t_g, ddte_g, dwe_g, dcee_g = vjp((dyg, dst[:, xc]))
            if dirn == 0:
                dxs = dxs + dyg * de_ref[:, xc]
            bc, cc = slice(2048 + g * SSD_N, 2048 + (g + 1) * SSD_N), slice(3072 + g * SSD_N, 3072 + (g + 1) * SSD_N)
            dx_ref[:, xc] = plus_prior(dxs, xc)
            dx_ref[:, bc] = plus_prior(dbm, bc)
            dx_ref[:, cc] = plus_prior(dcm, cc)
            dst[:, xc] = dst_g
            dcum = dcum + dcum_g
            dcum_t = dcum_t + dcum_t_g
            d_dt_e.append(ddte_g)
            d_w_e.append(dwe_g)
            d_ce_e.append(dcee_g)
        ddt, dal = pre_vjp((dcum, dcum_t, jnp.concatenate(d_dt_e, axis=1), jnp.concatenate(d_w_e, axis=1),
                            jnp.concatenate(d_ce_e, axis=1)))
        ddt_ref[...] = ddt
        dal_ref[...] += dal

    return pl.pallas_call(
        body, name=f"ssd_bwd_d{dirn}", grid=(nc,),
        out_shape=[jax.ShapeDtypeStruct((s_len, 4096), F32), jax.ShapeDtypeStruct((s_len, SSD_DTW), F32),
                   jax.ShapeDtypeStruct((1, SSD_DTW), F32)],
        in_specs=_ssd_in_specs(kk) + [pl.BlockSpec((1, SSD_N, 2048), lambda i: (kk(i), 0, 0)),
                                      pl.BlockSpec((CHUNK, 2048), lambda i: (kk(i), 0)),
                                      pl.BlockSpec((1, 2048), lambda i: (0, 0))]
        + ([pl.BlockSpec((CHUNK, 4096), lambda i: (kk(i), 0))] if prior is not None else []),
        out_specs=[pl.BlockSpec((CHUNK, 4096), lambda i: (kk(i), 0)),
                   pl.BlockSpec((CHUNK, SSD_DTW), lambda i: (kk(i), 0)),
                   pl.BlockSpec((1, SSD_DTW), lambda i: (0, 0))],
        scratch_shapes=[pltpu.VMEM((SSD_N, 2048), F32)],
        compiler_params=_params(("arbitrary",), VMEM_BIG),
    )(xbc, xbc, xbc, dt, alog, states, dy, d_e, *([prior] if prior is not None else []))


def _gate_norm_fn(y, xs, z, d_e, nw):
    yg = (y + xs * d_e) * _silu(z)
    return yg * lax.rsqrt(jnp.mean(yg * yg, axis=-1, keepdims=True) + NORM_EPS) * nw


def _gate_norm_bwd(dy, w_out, y, xbc, z, d_e, nw):
    def fn(du, y, xs, z, d_e, nw):
        sig = jax.nn.sigmoid(z)
        gate = z * sig
        ysum = y + xs * d_e
        yg = ysum * gate
        r = lax.rsqrt(jnp.mean(yg * yg, axis=-1, keepdims=True) + NORM_EPS)
        t = du * nw
        dyg = t * r - yg * (jnp.mean(t * yg, axis=-1, keepdims=True) * (r * r * r))
        dys = dyg * gate
        dz = dyg * ysum * (sig * (1.0 + z * (1.0 - sig)))
        dnw = jnp.sum(du * yg * r, axis=0, keepdims=True)
        dde = jnp.sum(dys * xs, axis=0, keepdims=True)
        hh = lax.broadcasted_iota(jnp.int32, (2048, SSD_HEADS), 0) // HEAD_DIM
        jj = lax.broadcasted_iota(jnp.int32, (2048, SSD_HEADS), 1)
        return [dys, dz], [dnw, _hnn(jnp.broadcast_to(dde, (8, 2048)), (hh == jj).astype(F32))[0:1]]

    (dys, dz), (g_nw, g_d) = _matmul_rows("ssd_out_dx_gate_norm_bwd", dy, w_out, "nt", 256, dy.shape[1], fn,
                                          [y, (xbc, 2048, 0), z], [d_e, nw], [(2048, F32), (2048, BF16)],
                                          [(1, 2048), (1, SSD_HEADS)])
    return dys, dz, g_nw, g_d


def _ssd_tail_loss(y, xbc, z, d_e, snw, w_out, x1, tgt, gate, fnw):
    dm = x1.shape[1]
    si = y.shape[1]

    def make_u(y, xs, z, x1, tgt, d_e, snw, gate, fnw):
        return _gate_norm_fn(y, xs, z, d_e, snw).astype(BF16)

    def fn(y1, u, y, xs, z, x1, tgt, d_e, snw, gate, fnw):
        x2 = x1 + gate * y1
        r = lax.rsqrt(jnp.mean(x2 * x2, axis=-1, keepdims=True) + NORM_EPS)
        xh = x2 * r
        err = xh * fnw - tgt
        loss = 0.5 * jnp.sum(jnp.mean(err * err, axis=-1, keepdims=True), axis=0, keepdims=True)
        dy = err * (1.0 / dm)
        dxh = dy * fnw
        dx2 = r * (dxh - xh * jnp.mean(dxh * xh, axis=-1, keepdims=True))
        dfnw = jnp.sum(dy * xh, axis=0, keepdims=True)
        return [u, dx2, gate * dx2], [dfnw, jnp.sum(dx2 * y1, axis=0, keepdims=True), jnp.broadcast_to(loss, (1, 128))]

    (u, dx2, dy1), (g_fnw, dgate, loss) = _matmul_rows(
        "ssd_out_loss", make_u, w_out, "nn", 256, si, fn, [y, (xbc, si, 0), z, x1, tgt], [d_e, snw, gate, fnw],
        [(si, BF16), (dm, F32), (dm, BF16)], [(1, dm), (1, dm), (1, 128)])
    return u, dx2, dy1, g_fnw, dgate, loss


def _softplus_fwd(dt_raw, bias):
    (dt,), _ = _rowwise("dt_softplus", lambda r, b: ([jax.nn.softplus(r + b)], []), [dt_raw], [bias],
                        [(dt_raw.shape[1], F32)], [], 512)
    return dt


def _softplus_bwd(ddt_f, ddt_b, dt_raw, bias):
    def fn(df, db, r, b):
        g = (df + db) * jax.nn.sigmoid(r + b)
        return [g], [jnp.sum(g, axis=0, keepdims=True)]

    w = dt_raw.shape[1]
    (g,), (gb,) = _rowwise("dt_softplus_bwd", fn, [ddt_f, ddt_b, dt_raw], [bias], [(w, BF16)], [(1, w)], 512)
    return g, gb


def _whole(a):
    nd = len(a.shape)
    return pl.BlockSpec(a.shape, lambda *_: (0,) * nd)


def _mod_part(c_all, mod_w):
    nl, _, ncol = mod_w.shape
    nb = c_all.shape[0]

    def body(c_ref, w_ref, o_ref):
        cond = _silu(c_ref[...])
        for i in range(nl):
            o_ref[i * nb:(i + 1) * nb, :] = _nn(cond, w_ref[i])

    return pl.pallas_call(body, name="mod_part", out_shape=jax.ShapeDtypeStruct((nl * nb, ncol), F32),
                          compiler_params=_params(None, VMEM_BIG))(c_all, mod_w)


def _mod_finish(mod_nb, mod_b, norm_w, tokens):
    nl, dm = norm_w.shape

    def body(a_ref, b_ref, nw_ref, *rest):
        tok_refs, o_refs = rest[:len(tokens)], rest[len(tokens):]
        tok = sum(t[0:1, 0:1] for t in tok_refs)
        for i in range(nl):
            for k in range(3):
                cols = slice(k * dm, (k + 1) * dm)
                o_refs[4 * i + k][...] = a_ref[i:i + 1, cols] + b_ref[i:i + 1, cols]
            o_refs[4 * i + 3][...] = nw_ref[i:i + 1, :] + tok

    rows = pl.pallas_call(body, name="mod_finish", out_shape=[jax.ShapeDtypeStruct((1, dm), F32)] * (4 * nl))(
        mod_nb, mod_b, norm_w, *tokens)
    return [rows[4 * i:4 * i + 4] for i in range(nl)]


def _mod_grad(c_all, dmod_sh):
    nl, nb, ncol = dmod_sh.shape
    dm = c_all.shape[1]

    def body(c_ref, d_ref, o_ref):
        cond = _silu(c_ref[...])
        for i in range(nl):
            o_ref[i] = _tn(cond, d_ref[i])

    return pl.pallas_call(body, name="mod_grad", out_shape=jax.ShapeDtypeStruct((nl, dm, ncol), F32),
                          compiler_params=_params(None, VMEM_BIG))(c_all, dmod_sh)


PACK_ROWS = 16
PACK_COLS = 1024


def _pack_small(rows, b64, a64s, d32, extra):
    nr, na = len(rows), len(a64s)

    def body(*refs):
        o_ref = refs[-1]
        o_ref[...] = jnp.zeros_like(o_ref)
        for i in range(nr):
            o_ref[i:i + 1, :] = refs[i][...]
        b_ref, a_refs, d_ref, e_ref = refs[nr], refs[nr + 1:nr + 1 + na], refs[nr + 1 + na], refs[nr + 2 + na]
        o_ref[nr:nr + 1, 0:64] = b_ref[:, 0:64]
        o_ref[nr:nr + 1, 64:128] = sum(a[:, 0:64] for a in a_refs)
        o_ref[nr:nr + 1, 128:160] = d_ref[...]
        o_ref[nr:nr + 1, 256:384] = e_ref[...]

    return pl.pallas_call(body, name="pack_small", out_shape=jax.ShapeDtypeStruct((PACK_ROWS, PACK_COLS), F32))(
        *rows, b64, *a64s, d32, extra)


def _pack_ssd_small(cw, cb, nw):
    def body(cw_ref, cb_ref, nw_ref, o_ref):
        o_ref[...] = jnp.zeros_like(o_ref)
        o_ref[0:5, :] = cw_ref[...]
        o_ref[5:6, :] = cb_ref[...]
        o_ref[6:7, 0:256] = nw_ref[...]

    return pl.pallas_call(body, name="pack_ssd_small", out_shape=jax.ShapeDtypeStruct((8, 512), F32))(cw, cb, nw)


def _sum_parts(p_ref):
    g = p_ref[0].astype(F32)
    for s in range(1, p_ref.shape[0]):
        g = g + p_ref[s].astype(F32)
    return g


def _adam_update(w, g, m, v):
    m2 = ADAM_B1 * m + (1.0 - ADAM_B1) * g
    v2 = ADAM_B2 * v + (1.0 - ADAM_B2) * (g * g)
    m_hat = m2 / (1.0 - ADAM_B1 ** ADAM_STEP)
    v_hat = v2 / (1.0 - ADAM_B2 ** ADAM_STEP)
    return -ADAM_LR * (m_hat / (jnp.sqrt(v_hat) + ADAM_EPS) + ADAM_WD * w), m2, v2


def _adamw_windows(name, parts, params, windows, extra=None):
    n = len(params)

    def body(p_ref, *rest):
        ins, outs = rest[:3 * n], rest[3 * n:]
        g = _sum_parts(p_ref)
        for pi, rows, cols, idx in windows:
            w_ref, m_ref, v_ref = ins[3 * pi:3 * pi + 3]
            gw = g[rows, cols]
            dw, m2, v2 = _adam_update(w_ref[idx], gw, m_ref[idx], v_ref[idx])
            for o_ref, val in zip(outs[4 * pi:4 * pi + 4], (gw, dw, m2, v2), strict=True):
                o_ref[idx] = val
        if extra is not None:
            outs[4 * n][...] = g[extra[0], extra[1]]

    out_shape = [jax.ShapeDtypeStruct(w.shape, F32) for (w, _, _) in params for _ in range(4)]
    if extra is not None:
        out_shape.append(jax.ShapeDtypeStruct((extra[0].stop - extra[0].start, extra[1].stop - extra[1].start), F32))
    res = pl.pallas_call(body, name=name, out_shape=out_shape)(parts, *[a for p in params for a in p])
    return [res[4 * i:4 * i + 4] for i in range(n)] + ([res[4 * n]] if extra is not None else [])


def _adamw(name, w, parts, m, v, tr, tc=None):
    r_, c_ = w.shape
    p_ = parts.shape[0]
    tr = min(tr, r_)
    tc = c_ if tc is None else tc
    assert r_ % tr == 0 and c_ % tc == 0

    def body(w_ref, p_ref, m_ref, v_ref, g_ref, d_ref, m2_ref, v2_ref):
        g = _sum_parts(p_ref)
        g_ref[...] = g
        d_ref[...], m2_ref[...], v2_ref[...] = _adam_update(w_ref[...], g, m_ref[...], v_ref[...])

    blk = pl.BlockSpec((tr, tc), lambda i, j: (i, j))
    return pl.pallas_call(
        body, name=name, grid=(r_ // tr, c_ // tc), out_shape=[jax.ShapeDtypeStruct((r_, c_), F32)] * 4,
        in_specs=[blk, pl.BlockSpec((p_, tr, tc), lambda i, j: (0, i, j)), blk, blk], out_specs=[blk] * 4,
        compiler_params=_params(("parallel", "parallel"), VMEM_BIG),
    )(w, parts, m, v)


def _dev_index(p):
    return 4 * p[0] + 2 * p[1] + p[2]


def _all_gather(name, xs):
    n = len(xs)
    hbm = pl.BlockSpec(memory_space=pl.ANY)

    def body(*refs):
        x_refs, o_refs = refs[:n], refs[n:2 * n]
        send_sems, recv_sems, local_sems = refs[2 * n:]
        x, y, c = lax.axis_index("x"), lax.axis_index("y"), lax.axis_index("c")
        me, sibling = (x, y, c), (x, y, 1 - c)
        chips = [(1 - x, y), (x, 1 - y), (1 - x, 1 - y)]

        def copy(a, k, block, to, src=None):
            dst = o_refs[a].at[_dev_index(block)]
            return pltpu.make_async_remote_copy(
                src_ref=dst if src is None else src, dst_ref=dst, send_sem=send_sems.at[a, k],
                recv_sem=recv_sems.at[a, k], device_id=to, device_id_type=MESH)

        mine = [pltpu.make_async_copy(x_refs[a], o_refs[a].at[_dev_index(me)], local_sems.at[a]) for a in range(n)]
        for cp in mine:
            cp.start()
        first = []
        for a in range(n):
            first.append(copy(a, 0, me, sibling, src=x_refs[a]))
            first += [copy(a, 1 + j, me, (*chip, c), src=x_refs[a]) for j, chip in enumerate(chips)]
        for cp in first:
            cp.start()
        passed = []
        for j, chip in enumerate(chips):
            for a in range(n):
                copy(a, 1 + j, (*chip, c), me).wait_recv()
                cp = copy(a, 4 + j, (*chip, c), sibling)
                cp.start()
                passed.append(cp)
        for a in range(n):
            copy(a, 0, sibling, me).wait_recv()
            for j, chip in enumerate(chips):
                copy(a, 4 + j, (*chip, 1 - c), me).wait_recv()
        for cp in first + passed:
            cp.wait_send()
        for cp in mine:
            cp.wait()

    return pl.pallas_call(
        body, name=name, out_shape=[jax.ShapeDtypeStruct((NDEV, *x.shape), x.dtype) for x in xs],
        in_specs=[hbm] * n, out_specs=[hbm] * n,
        scratch_shapes=[pltpu.SemaphoreType.DMA((n, 7)), pltpu.SemaphoreType.DMA((n, 7)), pltpu.SemaphoreType.DMA((n,))],
    )(*xs)


_HBM = pl.BlockSpec(memory_space=pltpu.HBM)
_SEM = pl.BlockSpec(memory_space=pltpu.SEMAPHORE)
_EFFECT = pltpu.SideEffectType.DATAFLOW_SIDE_EFFECTING


def _mesh_position():
    return lax.axis_index("x"), lax.axis_index("y"), lax.axis_index("c")


def _peers(me):
    return [(k, tuple(1 - v if (k >> b) & 1 else v for v, b in zip(me, (2, 1, 0)))) for k in range(1, NDEV)]


EXCHANGE_COPIES = {"gather": NDEV - 1, "scatter": NDEV - 1, "pair": 4, "chips": 3}
NCHIP = NDEV // 2


def _landing_zones(name, xs, mode):
    x_, y_, c_ = _mesh_position()
    mine = (2 * x_ + y_ if mode == "chips" else _dev_index((x_, y_, c_))).astype(jnp.int32).reshape(1)
    lands = []
    for a, x in enumerate(xs):
        rows, cols = x.shape[-2:]
        if mode == "pair":
            lands.append(lax.empty((NCHIP, rows, cols), x.dtype))
            continue
        tr = 256 if rows % 256 == 0 else rows

        def body(me_ref, x_ref, o_ref):
            o_ref[...] = x_ref[...]

        if mode == "gather":
            in_spec = pl.BlockSpec((tr, cols), lambda i, me_ref: (i, 0))
        else:
            in_spec = pl.BlockSpec((None, tr, cols), lambda i, me_ref: (me_ref[0], i, 0))
        lands.append(pl.pallas_call(
            body, name=f"{name}_{a}",
            out_shape=jax.ShapeDtypeStruct((NCHIP if mode == "chips" else NDEV, rows, cols), x.dtype),
            grid_spec=pltpu.PrefetchScalarGridSpec(
                num_scalar_prefetch=1, grid=(rows // tr,), in_specs=[in_spec],
                out_specs=pl.BlockSpec((None, tr, cols), lambda i, me_ref: (me_ref[0], i, 0))),
            compiler_params=_params(("arbitrary",)),
        )(mine, x))
    return lands


def _exchange_copies(x_refs, land_refs, send_sems, recv_sems, mode):
    x_, y_, c_ = me = _mesh_position()
    per_array = EXCHANGE_COPIES[mode]
    out = []

    def add(a, k, src, dst, peer):
        sem = a * per_array + k
        out.append(pltpu.make_async_remote_copy(src_ref=src, dst_ref=dst, send_sem=send_sems.at[sem], recv_sem=recv_sems.at[sem],
                                                device_id=peer, device_id_type=MESH))

    for a, (x_ref, land_ref) in enumerate(zip(x_refs, land_refs)):
        if mode in ("gather", "scatter"):
            for k, peer in _peers(me):
                add(a, k - 1, x_ref.at[_dev_index(peer)] if mode == "scatter" else x_ref, land_ref.at[_dev_index(me)], peer)
        elif mode == "pair":
            for chip in range(NCHIP):
                add(a, chip, x_ref.at[2 * chip + 1 - c_], land_ref.at[chip], (x_, y_, 1 - c_))
        else:
            for k in range(1, NCHIP):
                px, py = (1 - x_ if k & 2 else x_), (1 - y_ if k & 1 else y_)
                add(a, k - 1, x_ref.at[2 * px + py], land_ref.at[2 * x_ + y_], (px, py, c_))
    return out


def _exchange_start(name, xs, lands, mode, dep):
    n = len(xs)

    def body(*refs):
        x_refs, land_refs = refs[:n], refs[n:2 * n]
        send_sems, recv_sems = refs[2 * n + 1], refs[2 * n + 2]
        token = refs[-1]
        for cp in _exchange_copies(x_refs, land_refs, send_sems, recv_sems, mode):
            cp.start()
        token[...] = jnp.zeros_like(token)

    sems = pltpu.SemaphoreType.DMA((n * EXCHANGE_COPIES[mode],))
    res = pl.pallas_call(
        body, name=name,
        out_shape=(sems, sems, *[pltpu.HBM(a.shape, a.dtype) for a in (*xs, *lands)], jax.ShapeDtypeStruct((8, 128), F32)),
        in_specs=[_HBM] * (2 * n) + [pl.BlockSpec(memory_space=pl.ANY)],
        out_specs=(_SEM, _SEM, *[_HBM] * (2 * n), pl.BlockSpec(memory_space=pltpu.VMEM)),
        input_output_aliases={i: 2 + i for i in range(2 * n)},
        compiler_params=pltpu.CompilerParams(has_side_effects=_EFFECT),
    )(*[pltpu.with_memory_space_constraint(a, pltpu.HBM) for a in (*xs, *lands)], dep)
    return res[:-1], res[-1]


def _exchange_wait(name, handles, mode, after):
    send_sems, recv_sems = handles[0], handles[1]
    bufs = handles[2:]
    n = len(bufs) // 2

    def body(*refs):
        x_refs, land_refs = refs[:n], refs[n:2 * n]
        s_sems, r_sems = refs[2 * n], refs[2 * n + 1]
        for cp in _exchange_copies(x_refs, land_refs, s_sems, r_sems, mode):
            cp.wait_send()
            cp.wait_recv()

    res = pl.pallas_call(
        body, name=name, out_shape=tuple(pltpu.HBM(a.shape, a.dtype) for a in bufs),
        in_specs=[_HBM] * (2 * n) + [_SEM, _SEM, pl.BlockSpec(memory_space=pl.ANY)], out_specs=tuple([_HBM] * (2 * n)),
        input_output_aliases={i: i for i in range(2 * n)},
        compiler_params=pltpu.CompilerParams(has_side_effects=_EFFECT),
    )(*bufs, send_sems, recv_sems, after)
    return res[n:]


def _pair_sum(name, x, from_sibling):
    _, rows, cols = x.shape
    tr = rows
    core = lax.axis_index("c").astype(jnp.int32).reshape(1)

    def body(c_ref, x_ref, s_ref, o_ref):
        o_ref[...] = (x_ref[...].astype(F32) + s_ref[...].astype(F32)).astype(o_ref.dtype)

    return pl.pallas_call(
        body, name=name, out_shape=jax.ShapeDtypeStruct((NCHIP, rows, cols), x.dtype),
        grid_spec=pltpu.PrefetchScalarGridSpec(
            num_scalar_prefetch=1, grid=(NCHIP, rows // tr),
            in_specs=[pl.BlockSpec((None, tr, cols), lambda j, i, c_ref: (2 * j + c_ref[0], i, 0)),
                      pl.BlockSpec((None, tr, cols), lambda j, i, c_ref: (j, i, 0))],
            out_specs=pl.BlockSpec((None, tr, cols), lambda j, i, c_ref: (j, i, 0))),
        compiler_params=_params(("parallel", "parallel")),
    )(core, x, from_sibling)


def kernel(x, c, positions, norm_w, mod_w, mod_b, attn_w_in, attn_w_out, ssd_w_in, ssd_conv_w, ssd_conv_b, ssd_dt_bias, ssd_a_log, ssd_d, ssd_norm_w, ssd_w_out, final_norm_w, loss_target, m_norm_w, m_mod_w, m_mod_b, m_attn_w_in, m_attn_w_out, m_ssd_w_in, m_ssd_conv_w, m_ssd_conv_b, m_ssd_dt_bias, m_ssd_a_log, m_ssd_d, m_ssd_norm_w, m_ssd_w_out, m_final_norm_w, v_norm_w, v_mod_w, v_mod_b, v_attn_w_in, v_attn_w_out, v_ssd_w_in, v_ssd_conv_w, v_ssd_conv_b, v_ssd_dt_bias, v_ssd_a_log, v_ssd_d, v_ssd_norm_w, v_ssd_w_out, v_final_norm_w):
    s_len, dm = x.shape[1], x.shape[2]
    me = 4 * lax.axis_index("x") + 2 * lax.axis_index("y") + lax.axis_index("c")
    x0 = x.reshape(s_len, dm)
    tgt = loss_target.reshape(s_len, dm)
    aw = 3 * 512
    si = 2 * dm
    sxbc = 2 * si
    n_ssd_in = ssd_w_in.shape[2] * NDEV

    g_ai, c_all = _all_gather("gather_attn_w_in", [attn_w_in[0].astype(BF16), c])
    w_ai = g_ai.transpose(1, 0, 2).reshape(dm, 4 * aw)
    wcol = attn_w_in.shape[2]
    c_all = c_all.reshape(NDEV, dm)

    part = _mod_part(c_all, mod_w)
    (part_all,) = _all_gather("gather_mod", [part])
    mod_nb = jnp.stack([lax.dynamic_index_in_dim(part_all, i * NDEV + me, axis=1, keepdims=False).reshape(3 * dm)
                        for i in range(2)])

    ssd_small = _pack_ssd_small(ssd_conv_w[0], ssd_conv_b, ssd_norm_w)
    ao_shard = [attn_w_out[0].astype(BF16)]
    ao_handles, ao_token = _exchange_start("w_out_start", ao_shard, _landing_zones("w_out_place", ao_shard, "gather"), "gather",
                                           part_all)
    late_shards = [ssd_w_in[0].T.astype(BF16), ssd_w_out[0].astype(BF16), ssd_small]
    w_handles, w_token = _exchange_start("weights_start", late_shards, _landing_zones("weights_place", late_shards, "gather"),
                                         "gather", ao_token)
    (shift0, scale0, gate0, nw0), (shift1, scale1, gate1, nw1) = _mod_finish(mod_nb, mod_b, norm_w, [ao_token, w_token])
    shift, scale, gate, nw = [shift0, shift1], [scale0, scale1], [gate0, gate1], [nw0, nw1]

    hn0 = _norm_mod_fwd("norm0", x0, nw[0], scale[0], shift[0])
    inv_freq = ROPE_THETA ** (-jnp.arange(0, ROT_DIM, 2, dtype=F32) / ROT_DIM)
    lane = jnp.arange(128) % HEAD_DIM
    inv_row = jnp.where(lane < ROT_DIM, inv_freq[lane % (ROT_DIM // 2)], 0.0).reshape(1, 128).astype(F32)
    tabs = _rope_tables(positions.reshape(s_len, 1), inv_row)
    qk = _matmul("proj_qk", hn0, w_ai, "nn", F32, MM_T, MM_T, dm, epilogue=_rot_fwd, mrows=tabs, n_out=2 * aw)
    v = _matmul("proj_vz", hn0, w_ai, "nn", F32, MM_T, MM_T, dm, b_noff=2 * aw, n_out=2 * aw)
    z0 = (v, 1)
    att = [_attn_fwd(g, qk, v) for g in range(3)]
    os_, lses = [a[0] for a in att], [a[1] for a in att]
    (g_ao,) = _exchange_wait("w_out_wait", ao_handles, "gather", lses[2])
    a0, y0, x1 = _attn_out(os_, lses, z0, x0, gate[0], g_ao.reshape(aw, dm))

    hn1 = _norm_mod_fwd("norm1", x1, nw[1], scale[1], shift[1])
    g_si, g_so, g_small = _exchange_wait("weights_wait", w_handles, "gather", hn1)
    w_ao = g_ao.reshape(aw, dm)
    w_si_t = g_si.reshape(n_ssd_in, dm)
    w_so = g_so.reshape(si, dm)
    conv_w = g_small[:, 0:CONV_WIDTH, :].transpose(1, 0, 2).reshape(CONV_WIDTH, sxbc)
    conv_b = g_small[:, 5, :].reshape(1, sxbc)
    snw = g_small[:, 6, 0:si // NDEV].reshape(1, si)
    ndt = 2 * SSD_HEADS
    z1 = _matmul("ssd_proj_z", hn1, w_si_t, "nt", F32, MM_T, MM_T, dm, n_out=si)
    xpre = _matmul("ssd_proj_xbc", hn1, w_si_t, "nt", F32, MM_T, MM_T, dm, b_noff=si, n_out=sxbc)
    dt_raw = _matmul("ssd_proj_dt", hn1, w_si_t, "nt", F32, MM_T, ndt, dm, b_noff=si + sxbc, n_out=ndt)
    xbc = _conv_fwd(xpre, conv_w, conv_b)
    widen = lambda a: jnp.pad(a, ((0, 0), (0, SSD_DTW - ndt)))
    dt_raw = widen(dt_raw)
    dt_bias = widen(ssd_dt_bias.reshape(1, ndt))
    alog = widen(ssd_a_log.reshape(1, ndt))
    dt = _softplus_fwd(dt_raw, dt_bias)
    y_f, st_f = _ssd_fwd(xbc, dt, alog, 0)
    y_fb, st_b = _ssd_fwd(xbc, dt, alog, 1, prior=y_f)
    d_e = jnp.repeat(ssd_d.reshape(SSD_HEADS), HEAD_DIM).reshape(1, si)

    fnw = final_norm_w.reshape(1, dm)
    u, dx2, dy1, g_fnw, dgate1, loss_part = _ssd_tail_loss(y_fb, xbc, z1, d_e, snw, w_so, x1, tgt, gate[1], fnw)
    gw_so = _matmul("ssd_out_dw", u, dy1, "tn", BF16, MM_T, MM_T, MM_T)
    dys, dz1, g_snw, g_d = _gate_norm_bwd(dy1, w_so, y_fb, xbc, z1, d_e, snw)
    dxbc_f, ddt_f, dalog_f = _ssd_bwd(xbc, dt, alog, st_f, dys, d_e, 0)
    dxbc, ddt_b, dalog_b = _ssd_bwd(xbc, dt, alog, st_b, dys, d_e, 1, prior=dxbc_f)
    dpre, g_cw, g_cb = _conv_bwd(xpre, dxbc, conv_w, conv_b)
    ddt_raw, g_dtb = _softplus_bwd(ddt_f, ddt_b, dt_raw, dt_bias)
    ddt_raw = ddt_raw[:, :ndt]
    dhn1 = [_matmul("ssd_proj_z_dx", dz1, w_si_t, "nn", F32, MM_T, MM_T, MM_T),
            _matmul("ssd_proj_xbc_dx", dpre, w_si_t, "nn", F32, MM_T, MM_T, MM_T, b_koff=si)]
    gw_si_t = _matmul("ssd_proj_z_dw", dz1, hn1, "tn", BF16, MM_T, MM_T, MM_T, dest=(n_ssd_in, 0, None))
    gw_si_t = _matmul("ssd_proj_xbc_dw", dpre, hn1, "tn", BF16, MM_T, MM_T, MM_T, dest=(n_ssd_in, si, gw_si_t))
    gw_si_t = _matmul("ssd_proj_dt_dw", ddt_raw, hn1, "tn", BF16, ndt, MM_T, MM_T, dest=(n_ssd_in, si + sxbc, gw_si_t))

    l1_grads = [gw_so.reshape(NDEV, si // NDEV, dm), gw_si_t.reshape(NDEV, n_ssd_in // NDEV, dm),
                _pack_ssd_small_blocks(g_cw, g_cb, g_snw)]
    l1_handles, l1_token = _exchange_start("l1_grads_start", l1_grads, _landing_zones("l1_grads_place", l1_grads, "scatter"),
                                           "scatter", dhn1[1])
    dx1, dy0, g_nw1, dsc1, dsh1, dgate0 = _norm_mod_bwd(
        "ssd_proj_dt_dx_norm1_bwd", (ddt_raw, w_si_t, "nn", ndt, dict(b_koff=si + sxbc)), x1, dhn1, dx2,
        nw[1], scale[1], shift[1], prev=(y0, gate[0] + l1_token[0:1, 0:1]))

    gw_ao = _matmul("attn_out_dw", a0, dy0, "tn", BF16, aw // 2, MM_T, MM_T)
    dos, dls, dz0 = _mix_bwd(dy0, w_ao, os_, lses, z0)
    datt = [_attn_bwd(g, qk, v, os_[g], lses[g], dos[g], dls[g]) for g in range(3)]
    dqkv = _rot_pack_bwd([t[0] for t in datt], [t[1] for t in datt], [t[2] for t in datt], tabs)
    gw_ai = _matmul("proj_qkv_dw", hn0, dqkv, "tn", BF16, MM_T, wcol, MM_T, out_blocks=3 * aw // wcol, dest=(NDEV, 0, None))
    gw_ai = _matmul("proj_z_dw", hn0, dz0, "tn", BF16, MM_T, wcol, MM_T, out_blocks=aw // wcol,
                    dest=(NDEV, 3 * aw // wcol, gw_ai))
    after_start = lambda acc, t: acc + t
    zero_row = lambda token: jnp.tile(token[0:1], (1, dm // 128))
    l0_grads = [gw_ai, gw_ao.reshape(NDEV, aw // NDEV, dm)]
    pair_handles, pair_token = _exchange_start("l0_pair_start", l0_grads, _landing_zones("l0_pair_place", l0_grads, "pair"),
                                               "pair", dqkv)
    dhn0_z = _matmul("proj_z_dx", dz0, w_ai, "nt", F32, MM_T, MM_T, aw, b_koff=3 * aw, n_out=dm, epilogue=after_start,
                     ncols=(zero_row(pair_token),))
    from_sibling = _exchange_wait("l0_pair_wait", pair_handles, "pair", dhn0_z)
    chip_sums = [_pair_sum(f"l0_pair_sum_{a}", g, s) for a, (g, s) in enumerate(zip(l0_grads, from_sibling))]
    l0_handles, l0_token = _exchange_start("l0_grads_start", chip_sums, _landing_zones("l0_grads_place", chip_sums, "chips"),
                                           "chips", dhn0_z)
    dx0, g_nw0, dsc0, dsh0 = _norm_mod_bwd(
        "proj_qkv_dx_norm0_bwd", (dqkv, w_ai, "nt", aw, dict(n_out=dm)), x0, [dhn0_z], dx1,
        nw[0], scale[0], shift[0] + zero_row(l0_token))

    small_g = [_pack_small([dsh0, dsc0, dgate0, dsh1, dsc1, dgate1, g_nw0, g_nw1, g_fnw], g_dtb, [dalog_f, dalog_b], g_d, loss_part)]
    sm_handles, sm_token = _exchange_start("small_grads_start", small_g, _landing_zones("small_grads_place", small_g, "gather"),
                                           "gather", dx0)

    whole = (slice(None), slice(None))
    r_so, r_si, r_small = _exchange_wait("l1_grads_wait", l1_handles, "scatter", sm_token)
    si_out = [o.T for o in _adamw("adamw_ssd_w_in", ssd_w_in[0].T, r_si, m_ssd_w_in[0].T, v_ssd_w_in[0].T, n_ssd_in // NDEV, 256)]
    so_out = _adamw("adamw_ssd_w_out", ssd_w_out[0], r_so, m_ssd_w_out[0], v_ssd_w_out[0], 256)
    cw_cols = ssd_conv_w.shape[2]
    cw_out, cb_out, snw_out = _adamw_windows(
        "adamw_ssd_small", r_small,
        [(ssd_conv_w, m_ssd_conv_w, v_ssd_conv_w), (ssd_conv_b, m_ssd_conv_b, v_ssd_conv_b),
         (ssd_norm_w, m_ssd_norm_w, v_ssd_norm_w)],
        [(0, slice(0, CONV_WIDTH), slice(0, cw_cols), (0, slice(None), slice(None))),
         (1, slice(5, 6), slice(0, cw_cols), whole), (2, slice(6, 7), slice(0, si // NDEV), whole)])
    r_ai, r_ao = _exchange_wait("l0_grads_wait", l0_handles, "chips", so_out[0])
    ai_out = _adamw("adamw_attn_w_in", attn_w_in[0], r_ai, m_attn_w_in[0], v_attn_w_in[0], 256)
    ao_out = _adamw("adamw_attn_w_out", attn_w_out[0], r_ao, m_attn_w_out[0], v_attn_w_out[0], 192)

    (small_all,) = _exchange_wait("small_grads_wait", sm_handles, "gather", ai_out[0])
    full = slice(0, PACK_COLS)
    nhd = SSD_HEADS
    windows = [(0, slice(3 * i + k, 3 * i + k + 1), full, (slice(i, i + 1), slice(k * dm, (k + 1) * dm)))
               for i in range(2) for k in range(3)]
    windows += [(1, slice(6 + i, 7 + i), full, (slice(i, i + 1), slice(None))) for i in range(2)]
    windows += [(2, slice(8, 9), full, whole)]
    windows += [(3 + q, slice(9, 10), slice(2 * nhd * q + nhd * j, 2 * nhd * q + nhd * (j + 1)), (0, slice(j, j + 1), slice(None)))
                for q in range(2) for j in range(2)]
    windows += [(5, slice(9, 10), slice(4 * nhd, 5 * nhd), whole)]
    as_row = lambda a: a.reshape(1, dm)
    mb_out, nw_out, fnw_out, dtb_out, alog_out, d_out, loss = _adamw_windows(
        "adamw_small", small_all,
        [(mod_b, m_mod_b, v_mod_b), (norm_w, m_norm_w, v_norm_w), (fnw, as_row(m_final_norm_w), as_row(v_final_norm_w)),
         (ssd_dt_bias, m_ssd_dt_bias, v_ssd_dt_bias), (ssd_a_log, m_ssd_a_log, v_ssd_a_log), (ssd_d, m_ssd_d, v_ssd_d)],
        windows, extra=(slice(9, 10), slice(256, 257)))
    loss = loss.reshape(())

    ncol = mod_w.shape[2]
    dmod_all = small_all[:, 0:6, :].reshape(NDEV, 2, 3 * dm)
    dmod_sh = lax.dynamic_slice_in_dim(dmod_all, me * ncol, ncol, axis=2).transpose(1, 0, 2)
    g_modw = _mod_grad(c_all, dmod_sh).reshape(1, 2 * dm, ncol)
    modw_out = _adamw("adamw_mod_w", mod_w.reshape(2 * dm, ncol), g_modw, m_mod_w.reshape(2 * dm, ncol),
                      v_mod_w.reshape(2 * dm, ncol), 256)

    per_kind = []
    for k in range(4):
        per_kind.append([
            nw_out[k], modw_out[k].reshape(mod_w.shape), mb_out[k], ai_out[k][None], ao_out[k][None], si_out[k][None],
            cw_out[k], cb_out[k], dtb_out[k], alog_out[k], d_out[k], snw_out[k], so_out[k][None], fnw_out[k].reshape(dm)])
    return (loss, dx0.reshape(x.shape), *per_kind[0], *per_kind[1], *per_kind[2], *per_kind[3])


def _pack_ssd_small_blocks(g_cw, g_cb, g_nw):
    nper = g_cw.shape[1] // NDEV
    nwper = g_nw.shape[1] // NDEV

    def body(cw_ref, cb_ref, nw_ref, o_ref):
        o_ref[...] = jnp.zeros_like(o_ref)
        for d in range(NDEV):
            o_ref[d, 0:5, :] = cw_ref[:, d * nper:(d + 1) * nper]
            o_ref[d, 5:6, :] = cb_ref[:, d * nper:(d + 1) * nper]
            o_ref[d, 6:7, 0:nwper] = nw_ref[:, d * nwper:(d + 1) * nwper]

    return pl.pallas_call(body, name="pack_ssd_small_grads", out_shape=jax.ShapeDtypeStruct((NDEV, 8, nper), F32))(g_cw, g_cb, g_nw)
```

```python
import functools
import math

import jax
import jax.numpy as jnp
from jax import lax
from jax.experimental import pallas as pl
from jax.experimental.pallas import tpu as pltpu

F32 = jnp.float32
BF16 = jnp.bfloat16
HI = lax.Precision.HIGHEST
MESH = pl.DeviceIdType.MESH
NDEV = 8

NORM_EPS = 1e-6
ROPE_THETA = 500000.0
ROT_DIM = 16
HEAD_DIM = 64
DILATIONS = (1, 4, 16)
BAND = 64
NEG_BIG = -1e30
CHUNK = 128
SSD_HEADS = 32
SSD_GROUPS = 8
CONV_WIDTH = 5

ADAM_LR = 0.001
ADAM_B1 = 0.9
ADAM_B2 = 0.999
ADAM_EPS = 1e-08
ADAM_WD = 0.01
ADAM_STEP = 10

VMEM_BIG = 56 * 1024 * 1024
MM_T = 1024


def _params(sem=None, vmem=None):
    kw = {}
    if sem is not None:
        kw["dimension_semantics"] = sem
    if vmem is not None:
        kw["vmem_limit_bytes"] = vmem
    return pltpu.CompilerParams(**kw)


def _dg(a, b, ca, cb, prec=None):
    return lax.dot_general(a, b, (((ca,), (cb,)), ((), ())), preferred_element_type=F32, precision=prec)


def _nn(a, b):
    return _dg(a.astype(BF16), b.astype(BF16), 1, 0)


def _nt(a, b):
    return _dg(a.astype(BF16), b.astype(BF16), 1, 1)


def _tn(a, b):
    return _dg(a.astype(BF16), b.astype(BF16), 0, 0)


def _hnn(a, b):
    return _dg(a, b, 1, 0, HI)


@jax.custom_vjp
def _bnn(a, b):
    return _nn(a, b)


_bnn.defvjp(lambda a, b: (_nn(a, b), (a, b)), lambda r, g: (_nt(g, r[1]), _tn(r[0], g)))


@jax.custom_vjp
def _bnt(a, b):
    return _nt(a, b)


_bnt.defvjp(lambda a, b: (_nt(a, b), (a, b)), lambda r, g: (_nn(g, r[1]), _tn(g, r[0])))


@jax.custom_vjp
def _btn(a, b):
    return _tn(a, b)


_btn.defvjp(lambda a, b: (_tn(a, b), (a, b)), lambda r, g: (_nt(r[1], g), _nn(r[0], g)))


def _silu(x):
    return x * jax.nn.sigmoid(x)


def _b_spec(b, mode, tn, tk, no, ko, jk):
    if mode == "nt":
        return pl.BlockSpec((tn, tk), lambda *g: (jk(*g)[0] + no, jk(*g)[1] + ko))
    return pl.BlockSpec((tk, tn), lambda *g: (jk(*g)[1] + ko, jk(*g)[0] + no))


def _matmul(name, a, b, mode, out_dtype, tm, tn, tk, *, epilogue=None, tiled=(), mrows=(), ncols=(),
            b_noff=0, b_koff=0, n_out=None, out_blocks=None, dest=None):
    if mode == "tn":
        K, M = a.shape
    else:
        M, K = a.shape
    N = n_out if n_out is not None else (b.shape[0] if mode == "nt" else b.shape[1])
    tm, tn, tk = min(tm, M), min(tn, N), min(tk, K)
    assert M % tm == 0 and N % tn == 0 and K % tk == 0, (name, M, N, K, tm, tn, tk)
    assert b_noff % tn == 0 and b_koff % tk == 0
    no, ko = b_noff // tn, b_koff // tk
    nk = K // tk
    if mode == "tn":
        a_spec = pl.BlockSpec((tk, tm), lambda i, j, k: (k, i))
    else:
        a_spec = pl.BlockSpec((tm, tk), lambda i, j, k: (i, k))
    specs = [a_spec, _b_spec(b, mode, tn, tk, no, ko, lambda i, j, k: (j, k))]
    specs += [pl.BlockSpec((tm, tn), lambda i, j, k: (i, j)) for _ in tiled]
    specs += [pl.BlockSpec((tm, r.shape[1]), lambda i, j, k: (i, 0)) for r in mrows]
    specs += [pl.BlockSpec((1, tn), lambda i, j, k: (0, j)) for _ in ncols]
    total, off, earlier = dest if dest is not None else (None, 0, None)
    if out_blocks is None:
        assert off % tm == 0
        mo = off // tm
        out_shape = jax.ShapeDtypeStruct((M if total is None else total, N), out_dtype)
        out_spec = pl.BlockSpec((tm, tn), lambda i, j, k: (i + mo, j))
    else:
        nper = N // out_blocks
        assert nper % tn == 0
        jb = nper // tn
        out_shape = jax.ShapeDtypeStruct((out_blocks if total is None else total, M, nper), out_dtype)
        out_spec = pl.BlockSpec((None, tm, tn), lambda i, j, k: (j // jb + off, i, j % jb))
    if earlier is not None:
        assert earlier.shape == out_shape.shape and earlier.dtype == out_shape.dtype
    ne = len(tiled) + len(mrows) + len(ncols)
    dot = {"nn": _nn, "nt": _nt, "tn": _tn}[mode]

    def body(a_ref, b_ref, *rest):
        extras, o_ref = rest[:ne], rest[ne]

        def finish(acc):
            if epilogue is not None:
                acc = epilogue(acc, *[e[...] for e in extras])
            o_ref[...] = acc.astype(o_ref.dtype)

        if nk == 1:
            finish(dot(a_ref[...], b_ref[...]))
        else:
            acc_ref = rest[ne + 1]
            k = pl.program_id(2)

            @pl.when(k == 0)
            def _():
                acc_ref[...] = jnp.zeros_like(acc_ref)

            acc_ref[...] += dot(a_ref[...], b_ref[...])

            @pl.when(k == nk - 1)
            def _():
                finish(acc_ref[...])

    args = [a, b, *tiled, *mrows, *ncols]
    aliases = {}
    if earlier is not None:
        specs.append(pl.BlockSpec(memory_space=pl.ANY))
        aliases = {len(args): 0}
        args.append(earlier)

    def body_with_dest(*refs):
        body(*refs[:2 + ne], *refs[2 + ne + (earlier is not None):])

    return pl.pallas_call(
        body_with_dest, name=name, out_shape=out_shape, grid=(M // tm, N // tn, nk),
        in_specs=specs, out_specs=out_spec, input_output_aliases=aliases,
        scratch_shapes=[] if nk == 1 else [pltpu.VMEM((tm, tn), F32)],
        compiler_params=_params(("parallel", "parallel", "arbitrary"), VMEM_BIG),
    )(*args)


def _matmul_rows(name, a, b, mode, tm, tk, fn, rows, consts, outs, accs, *, n_out=None, b_noff=0, b_koff=0):
    rl = [(t, t.shape[1], 0) if not isinstance(t, tuple) else t for t in rows]
    make_a = a if callable(a) else None
    M, K = (rl[0][0].shape[0], b.shape[1 if mode == "nt" else 0]) if make_a else a.shape
    N = n_out if n_out is not None else (b.shape[0] if mode == "nt" else b.shape[1])
    tm, tk = min(tm, M), min(tk, K)
    assert M % tm == 0 and K % tk == 0 and b_koff % tk == 0 and b_noff % N == 0, (name, M, N, K)
    no, ko, nk = b_noff // N, b_koff // tk, K // tk
    assert make_a is None or nk == 1
    nr, nc, no_, na = len(rl), len(consts), len(outs), len(accs)
    dot = _nt if mode == "nt" else _nn

    def body(*refs):
        a_ref, b_ref, rest = (None, refs[0], refs[1:]) if make_a else (refs[0], refs[1], refs[2:])
        r_refs, c_refs = rest[:nr], rest[nr:nr + nc]
        o_refs, acc_refs = rest[nr + nc:nr + nc + no_], rest[nr + nc + no_:nr + nc + no_ + na]
        i, k = pl.program_id(0), pl.program_id(1)

        def finish(prod, *made):
            res_o, res_a = fn(prod, *made, *[r[...] for r in r_refs], *[c[...] for c in c_refs])
            for r, v in zip(o_refs, res_o, strict=True):
                r[...] = v.astype(r.dtype)
            if acc_refs:
                @pl.when(i == 0)
                def _():
                    for r in acc_refs:
                        r[...] = jnp.zeros_like(r)

                for r, v in zip(acc_refs, res_a, strict=True):
                    r[...] += v

        if make_a:
            left = make_a(*[r[...] for r in r_refs], *[c[...] for c in c_refs])
            finish(dot(left, b_ref[...]), left)
        elif nk == 1:
            finish(dot(a_ref[...], b_ref[...]))
        else:
            prod_ref = rest[-1]

            @pl.when(k == 0)
            def _():
                prod_ref[...] = jnp.zeros_like(prod_ref)

            prod_ref[...] += dot(a_ref[...], b_ref[...])

            @pl.when(k == nk - 1)
            def _():
                finish(prod_ref[...])

    b_spec = _b_spec(b, mode, N, tk, no, ko, lambda i, k: (0, k))
    in_specs = ([] if make_a else [pl.BlockSpec((tm, tk), lambda i, k: (i, k))]) + [b_spec]
    in_specs += [pl.BlockSpec((tm, w), functools.partial(lambda i, k, cb: (i, cb), cb=cb)) for (_, w, cb) in rl]
    in_specs += [pl.BlockSpec(c.shape, lambda i, k: (0, 0)) for c in consts]
    out_specs = [pl.BlockSpec((tm, c), lambda i, k: (i, 0)) for (c, _) in outs]
    out_specs += [pl.BlockSpec(shp, lambda i, k: (0, 0)) for shp in accs]
    out_shape = [jax.ShapeDtypeStruct((M, c), dt) for (c, dt) in outs] + [jax.ShapeDtypeStruct(shp, F32) for shp in accs]
    res = pl.pallas_call(
        body, name=name, out_shape=out_shape, grid=(M // tm, nk), in_specs=in_specs, out_specs=out_specs,
        scratch_shapes=[] if nk == 1 else [pltpu.VMEM((tm, N), F32)],
        compiler_params=_params(("arbitrary" if accs else "parallel", "arbitrary"), VMEM_BIG),
    )(*([] if make_a else [a]), b, *[t[0] for t in rl], *consts)
    return res[:no_], res[no_:]


def _rowwise(name, fn, tiled, consts, outs, accs, ts):
    tl = [(t, t.shape[1], 0) if not isinstance(t, tuple) else t for t in tiled]
    s_len = tl[0][0].shape[0]
    assert s_len % ts == 0
    nt_, nc_, no_ = len(tl), len(consts), len(outs)

    def body(*refs):
        t_refs, c_refs = refs[:nt_], refs[nt_:nt_ + nc_]
        o_refs, a_refs = refs[nt_ + nc_:nt_ + nc_ + no_], refs[nt_ + nc_ + no_:]
        res_o, res_a = fn(*[r[...] for r in t_refs], *[r[...] for r in c_refs])
        for r, v in zip(o_refs, res_o, strict=True):
            r[...] = v.astype(r.dtype)
        if a_refs:
            @pl.when(pl.program_id(0) == 0)
            def _():
                for r in a_refs:
                    r[...] = jnp.zeros_like(r)

            for r, v in zip(a_refs, res_a, strict=True):
                r[...] += v

    in_specs = [pl.BlockSpec((ts, w), functools.partial(lambda i, cb: (i, cb), cb=cb)) for (_, w, cb) in tl]
    in_specs += [pl.BlockSpec(c.shape, lambda i: (0, 0)) for c in consts]
    out_specs = [pl.BlockSpec((ts, c), lambda i: (i, 0)) for (c, _) in outs]
    out_specs += [pl.BlockSpec(shp, lambda i: (0, 0)) for shp in accs]
    out_shape = [jax.ShapeDtypeStruct((s_len, c), dt) for (c, dt) in outs]
    out_shape += [jax.ShapeDtypeStruct(shp, F32) for shp in accs]
    res = pl.pallas_call(
        body, name=name, out_shape=out_shape, grid=(s_len // ts,), in_specs=in_specs, out_specs=out_specs,
        compiler_params=_params(("arbitrary",) if accs else ("parallel",), VMEM_BIG),
    )(*[t[0] for t in tl], *consts)
    return res[:no_], res[no_:]


def _norm_mod_fn(x, nw, sc, sh):
    r = lax.rsqrt(jnp.mean(x * x, axis=-1, keepdims=True) + NORM_EPS)
    return (x * r * nw) * (1.0 + sc) + sh


def _norm_mod_fwd(name, x, nw, sc, sh):
    (hn,), _ = _rowwise(name, lambda x, nw, sc, sh: ([_norm_mod_fn(x, nw, sc, sh)], []),
                        [x], [nw, sc, sh], [(x.shape[1], BF16)], [], 512)
    return hn


def _norm_mod_bwd(name, last, x, dhn_parts, dres, nw, sc, sh, prev=None):
    n = len(dhn_parts)
    d = x.shape[1]
    a, b, mode, tk, kw = last

    def fn(dhn, x, *rest):
        for p in rest[:n]:
            dhn = dhn + p
        dres, rest = rest[n], rest[n + 1:]
        y_prev, (nw, sc, sh), gate = (rest[0], rest[1:4], rest[4]) if prev is not None else (None, rest[0:3], None)
        r = lax.rsqrt(jnp.mean(x * x, axis=-1, keepdims=True) + NORM_EPS)
        xh = x * r
        dxh = dhn * (nw * (1.0 + sc))
        dx = r * (dxh - xh * jnp.mean(dxh * xh, axis=-1, keepdims=True)) + dres
        along = jnp.sum(dhn * xh, axis=0, keepdims=True)
        dnw, dsc, dsh = along * (1.0 + sc), along * nw, jnp.sum(dhn, axis=0, keepdims=True)
        if prev is None:
            return [dx], [dnw, dsc, dsh]
        return [dx, gate * dx], [dnw, dsc, dsh, jnp.sum(dx * y_prev, axis=0, keepdims=True)]

    rows = [x, *dhn_parts, dres] + ([prev[0]] if prev is not None else [])
    consts = [nw, sc, sh] + ([prev[1]] if prev is not None else [])
    outs = [(d, F32)] + ([(d, BF16)] if prev is not None else [])
    res_o, res_a = _matmul_rows(name, a, b, mode, 512, tk, fn, rows, consts, outs, [(1, d)] * (3 + (prev is not None)), **kw)
    return (*res_o, *res_a)


def _rope_tables(pos_col, inv_row):
    def fn(pos, inv):
        ang = pos.astype(F32) * inv
        e = lax.broadcasted_iota(jnp.int32, (1, 128), 1) % HEAD_DIM
        cos, sin = jnp.cos(ang), jnp.sin(ang)
        half = ROT_DIM // 2
        return [jnp.where(e < ROT_DIM, cos, 1.0), jnp.where(e < half, -sin, 0.0),
                jnp.where((e >= half) & (e < ROT_DIM), sin, 0.0)], []

    (c, sa, sb), _ = _rowwise("rope_tables", fn, [pos_col], [inv_row], [(128, F32)] * 3, [], 512)
    return c, sa, sb


def _rot_fwd(t, c, sa, sb):
    n = t.shape[1]
    rep = n // 128
    c, sa, sb = (jnp.tile(u, (1, rep)) for u in (c, sa, sb))
    return t * c + pltpu.roll(t, n - ROT_DIM // 2, 1) * sa + pltpu.roll(t, ROT_DIM // 2, 1) * sb


def _rot_bwd(g, c, sa, sb):
    n = g.shape[1]
    rep = n // 128
    c, sa, sb = (jnp.tile(u, (1, rep)) for u in (c, sa, sb))
    return g * c + pltpu.roll(g * sa, ROT_DIM // 2, 1) + pltpu.roll(g * sb, n - ROT_DIM // 2, 1)


ATT_TQ = 128


def _attn_tiles(l):
    tk = ATT_TQ + 2 * BAND
    return (l, l) if l <= tk else (ATT_TQ, tk)


def _attn_specs(g, s_len):
    def blk(off):
        return pl.BlockSpec((s_len, 128), functools.partial(lambda hp, off: (0, off + hp), off=off))

    return blk(4 * g), blk(12 + 4 * g), blk(4 * g), blk(0)


def _attn_tile_geometry(t, d, l):
    tq, tk = _attn_tiles(l)
    nts = l // tq
    r = t // nts
    ts = t % nts
    q0 = ts * tq
    ws = jnp.clip(q0 - BAND, 0, l - tk)
    kind = jnp.where(ts == 0, 0, jnp.where(ts == nts - 1, 2, 1))
    if d == 1:
        return pl.ds(pl.multiple_of(q0, tq), tq), pl.ds(pl.multiple_of(ws, BAND), tk), kind
    return pl.ds(r + d * q0, tq, stride=d), pl.ds(r + d * ws, tk, stride=d), kind


def _attn_fill_bias(bias_ref):
    _, tq2, tk = bias_ref.shape
    iq = lax.broadcasted_iota(jnp.int32, (tq2, 1), 0) % (tq2 // 2)
    ik = lax.broadcasted_iota(jnp.int32, (1, tk), 1)
    for i, off in enumerate((0, -BAND, -2 * BAND)):
        bias_ref[i] = jnp.where(jnp.abs(ik + off - iq) <= BAND, 0.0, NEG_BIG)


def _split_heads(t, in_h):
    zero = jnp.zeros_like(t)
    return jnp.concatenate([jnp.where(in_h[0], t, zero), jnp.where(in_h[1], t, zero)], axis=0)


def _attn_fwd(g, qk, v):
    s_len = qk.shape[0]
    d = DILATIONS[g]
    l = s_len // d
    tq, tk = _attn_tiles(l)
    assert l % tq == 0 and l >= tk
    q_spec, k_spec, v_spec, o_spec = _attn_specs(g, s_len)
    scale = 1.0 / math.sqrt(HEAD_DIM)

    def body(q_ref, k_ref, v_ref, o_ref, lse_ref, bias_ref):
        lane = lax.broadcasted_iota(jnp.int32, (1, 128), 1)
        in_h = [lane < HEAD_DIM, lane >= HEAD_DIM]
        _attn_fill_bias(bias_ref)

        def tile(t, carry):
            rows, win, kind = _attn_tile_geometry(t, d, l)
            q = (q_ref[rows, :] * scale).astype(BF16)
            k = k_ref[win, :].astype(BF16)
            vv = v_ref[win, :].astype(BF16)
            s = _nt(_split_heads(q, in_h), k) + bias_ref[kind]
            m = jnp.max(s, axis=1, keepdims=True)
            p = jnp.exp(s - m)
            den = jnp.sum(p, axis=1, keepdims=True)
            out = _nn(p, vv) / den
            lse = m + jnp.log(den)
            o_ref[rows, :] = jnp.where(in_h[0], out[:tq], out[tq:])
            lse_ref[rows, :] = jnp.where(in_h[0], lse[:tq], lse[tq:])
            return carry

        lax.fori_loop(0, s_len // tq, tile, 0, unroll=8 * ATT_TQ // tq)

    return pl.pallas_call(
        body, name=f"attn_fwd_g{g}", grid=(4,),
        out_shape=[jax.ShapeDtypeStruct((s_len, 512), F32)] * 2,
        in_specs=[q_spec, k_spec, v_spec], out_specs=[o_spec, o_spec],
        scratch_shapes=[pltpu.VMEM((3, 2 * tq, tk), F32)],
        compiler_params=_params(("parallel",), VMEM_BIG),
    )(qk, qk, v)


def _attn_bwd(g, qk, v, o, lse, do, dlse):
    s_len = qk.shape[0]
    d = DILATIONS[g]
    l = s_len // d
    tq, tk = _attn_tiles(l)
    q_spec, k_spec, v_spec, o_spec = _attn_specs(g, s_len)
    scale = 1.0 / math.sqrt(HEAD_DIM)

    def body(q_ref, k_ref, v_ref, o_ref, lse_ref, do_ref, dlse_ref, dq_ref, dk_ref, dv_ref, bias_ref):
        lane = lax.broadcasted_iota(jnp.int32, (1, 128), 1)
        in_h = [lane < HEAD_DIM, lane >= HEAD_DIM]
        dk_ref[...] = jnp.zeros_like(dk_ref)
        dv_ref[...] = jnp.zeros_like(dv_ref)
        _attn_fill_bias(bias_ref)

        def tile(t, carry):
            rows, win, kind = _attn_tile_geometry(t, d, l)
            k, vv = k_ref[win, :].astype(BF16), v_ref[win, :].astype(BF16)
            dout, lse_t, dlse_t = do_ref[rows, :], lse_ref[rows, :], dlse_ref[rows, :]
            od = dout * o_ref[rows, :]
            q2 = _split_heads((q_ref[rows, :] * scale).astype(BF16), in_h)
            do2 = _split_heads(dout.astype(BF16), in_h)
            head_col = lambda a: jnp.concatenate([a[:, 0:1], a[:, HEAD_DIM:HEAD_DIM + 1]], axis=0)
            delta = jnp.concatenate([jnp.sum(jnp.where(m, od, 0.0), axis=1, keepdims=True) for m in in_h], axis=0)
            p = jnp.exp(_nt(q2, k) + bias_ref[kind] - head_col(lse_t))
            ds = (p * (_nt(do2, vv) - delta + head_col(dlse_t))).astype(BF16)
            dq2 = _nn(ds, k) * scale
            dq_ref[rows, :] = jnp.where(in_h[0], dq2[:tq], dq2[tq:])
            dk_ref[win, :] += _tn(ds, q2)
            dv_ref[win, :] += _tn(p, do2)
            return carry

        lax.fori_loop(0, s_len // tq, tile, 0, unroll=8 * ATT_TQ // tq)

    return pl.pallas_call(
        body, name=f"attn_bwd_g{g}", grid=(4,),
        out_shape=[jax.ShapeDtypeStruct((s_len, 512), F32)] * 3,
        in_specs=[q_spec, k_spec, v_spec, o_spec, o_spec, o_spec, o_spec], out_specs=[o_spec] * 3,
        scratch_shapes=[pltpu.VMEM((3, 2 * tq, tk), F32)],
        compiler_params=_params(("parallel",), VMEM_BIG),
    )(qk, qk, v, o, lse, do, dlse)


def _mix_weights(ls):
    mx = jnp.maximum(jnp.maximum(ls[0], ls[1]), ls[2])
    es = [jnp.exp(x - mx) for x in ls]
    tot = es[0] + es[1] + es[2]
    return [e / tot for e in es]


def _attn_out(os_, lses, z, x, gate, w_out):
    s_len, dm = x.shape
    tm = 256
    wdt = 512
    z, z_block = z

    def body(o0, o1, o2, l0, l1, l2, z_ref, x_ref, g_ref, w_ref, a_ref, y_ref, x1_ref):
        alphas = _mix_weights([l0[...], l1[...], l2[...]])
        y = jnp.zeros((tm, dm), F32)
        for g, o_ref in enumerate((o0, o1, o2)):
            a_g = (o_ref[...] * alphas[g] * _silu(z_ref[:, g * wdt:(g + 1) * wdt])).astype(BF16)
            a_ref[:, g * wdt:(g + 1) * wdt] = a_g
            y = y + _nn(a_g, w_ref[g * wdt:(g + 1) * wdt, :])
        y_ref[...] = y
        x1_ref[...] = x_ref[...] + g_ref[...] * y

    row = lambda c: pl.BlockSpec((tm, c), lambda i: (i, 0))
    return pl.pallas_call(
        body, name="attn_out", grid=(s_len // tm,),
        out_shape=[jax.ShapeDtypeStruct((s_len, 3 * wdt), BF16), jax.ShapeDtypeStruct((s_len, dm), F32),
                   jax.ShapeDtypeStruct((s_len, dm), F32)],
        in_specs=[row(wdt)] * 6 + [pl.BlockSpec((tm, 3 * wdt), lambda i: (i, z_block)), row(dm),
                                   pl.BlockSpec((1, dm), lambda i: (0, 0)), pl.BlockSpec(w_out.shape, lambda i: (0, 0))],
        out_specs=[row(3 * wdt), row(dm), row(dm)],
        compiler_params=_params(("parallel",), VMEM_BIG),
    )(*os_, *lses, z, x, gate, w_out)


def _mix_bwd(dy, w_out, os_, lses, z):
    wdt = 512

    def fn(da, o0, o1, o2, l0, l1, l2, z):
        os_t, ls = [o0, o1, o2], [l0, l1, l2]
        alphas = _mix_weights(ls)
        hi = lax.broadcasted_iota(jnp.int32, (2 * wdt, wdt), 0) % wdt // HEAD_DIM
        hj = lax.broadcasted_iota(jnp.int32, (2 * wdt, wdt), 1) // HEAD_DIM
        seg = (hi == hj).astype(BF16)
        head_sum = lambda t: _dg(jnp.concatenate(_bf16_parts(t, 2), axis=1), seg, 1, 0)
        dos, dal, dzs = [], [], []
        for g in range(3):
            zg = z[:, g * wdt:(g + 1) * wdt]
            sig = jax.nn.sigmoid(zg)
            dag = da[:, g * wdt:(g + 1) * wdt]
            dmix = dag * zg * sig
            dzs.append(dag * os_t[g] * alphas[g] * (sig * (1.0 + zg * (1.0 - sig))))
            dos.append(dmix * alphas[g])
            dal.append(head_sum(dmix * os_t[g]))
        mean = alphas[0] * dal[0] + alphas[1] * dal[1] + alphas[2] * dal[2]
        dls = [alphas[g] * (dal[g] - mean) for g in range(3)]
        return dos + dls + [jnp.concatenate(dzs, axis=1)], []

    outs, _ = _matmul_rows("attn_out_dx_mix_bwd", dy, w_out, "nt", 256, dy.shape[1], fn, [*os_, *lses, (z[0], 3 * wdt, z[1])], [],
                           [(wdt, F32)] * 6 + [(3 * wdt, BF16)], [])
    return outs[:3], outs[3:6], outs[6]


def _rot_pack_bwd(dqs, dks, dvs, tabs):
    wdt = 512

    def fn(*args):
        grads, (c, sa, sb) = args[:9], args[9:]
        cols = [_rot_bwd(gq, c, sa, sb) for gq in grads[:6]] + list(grads[6:])
        return [jnp.concatenate(cols, axis=1)], []

    (out,), _ = _rowwise("rot_pack_bwd", fn, [*dqs, *dks, *dvs, *tabs], [], [(9 * wdt, BF16)], [], 512)
    return out


CONV_CB = 128
CONV_R = 256
CONV_PAD = 8


def _conv_taps(buf, base, off, sign):
    return [buf[pl.ds(base + off + sign * j, CONV_R), :] for j in range(CONV_WIDTH)]


def _conv_tap_sum(taps, w):
    acc = None
    for j, t in enumerate(taps):
        term = t * w[j:j + 1, :]
        acc = term if acc is None else acc + term
    return acc


def _conv_fwd(xpre, cw, cb):
    s_len, ch = xpre.shape
    nchunk = s_len // CONV_R

    def body(x_ref, w_ref, b_ref, o_ref, xp):
        zero = jnp.zeros((CONV_PAD, CONV_CB), F32)
        xp[0:CONV_PAD, :] = zero
        xp[s_len + CONV_PAD:s_len + 2 * CONV_PAD, :] = zero

        def fill(ci, carry):
            base = pl.multiple_of(ci * CONV_R, CONV_R)
            xp[pl.ds(base + CONV_PAD, CONV_R), :] = x_ref[pl.ds(base, CONV_R), :]
            return carry

        lax.fori_loop(0, nchunk, fill, 0)
        w = w_ref[...]
        b = b_ref[...]

        def chunk(ci, carry):
            base = pl.multiple_of(ci * CONV_R, CONV_R)
            u = _conv_tap_sum(_conv_taps(xp, base, CONV_PAD - CONV_WIDTH // 2, 1), w) + b
            o_ref[pl.ds(base, CONV_R), :] = _silu(u)
            return carry

        lax.fori_loop(0, nchunk, chunk, 0, unroll=2)

    col = lambda r: pl.BlockSpec((r, CONV_CB), lambda j: (0, j))
    return pl.pallas_call(
        body, name="conv_fwd", grid=(ch // CONV_CB,), out_shape=jax.ShapeDtypeStruct((s_len, ch), F32),
        in_specs=[col(s_len), col(CONV_WIDTH), col(1)], out_specs=col(s_len),
        scratch_shapes=[pltpu.VMEM((s_len + 2 * CONV_PAD, CONV_CB), F32)],
        compiler_params=_params(("parallel",), VMEM_BIG),
    )(xpre, cw, cb)


def _conv_bwd(xpre, da, cw, cb):
    s_len, ch = xpre.shape
    nchunk = s_len // CONV_R
    half = CONV_WIDTH // 2

    def body(x_ref, da_ref, w_ref, b_ref, dx_ref, gw_ref, gb_ref, xp, dcp):
        zero = jnp.zeros((CONV_PAD, CONV_CB), F32)
        for buf in (xp, dcp):
            buf[0:CONV_PAD, :] = zero
            buf[s_len + CONV_PAD:s_len + 2 * CONV_PAD, :] = zero

        def fill(ci, carry):
            base = pl.multiple_of(ci * CONV_R, CONV_R)
            xp[pl.ds(base + CONV_PAD, CONV_R), :] = x_ref[pl.ds(base, CONV_R), :]
            return carry

        lax.fori_loop(0, nchunk, fill, 0)
        w = w_ref[...]
        b = b_ref[...]

        def first(ci, carry):
            base = pl.multiple_of(ci * CONV_R, CONV_R)
            taps = _conv_taps(xp, base, CONV_PAD - half, 1)
            u = _conv_tap_sum(taps, w) + b
            sig = jax.nn.sigmoid(u)
            dc = da_ref[pl.ds(base, CONV_R), :] * (sig * (1.0 + u * (1.0 - sig)))
            dcp[pl.ds(base + CONV_PAD, CONV_R), :] = dc
            gb = carry[0] + jnp.sum(dc, axis=0, keepdims=True)
            gws = [carry[1 + j] + jnp.sum(dc * taps[j], axis=0, keepdims=True) for j in range(CONV_WIDTH)]
            return (gb, *gws)

        z1 = jnp.zeros((1, CONV_CB), F32)
        sums = lax.fori_loop(0, nchunk, first, (z1,) * (1 + CONV_WIDTH), unroll=2)
        gb_ref[...] = sums[0]
        for j in range(CONV_WIDTH):
            gw_ref[j:j + 1, :] = sums[1 + j]

        def second(ci, carry):
            base = pl.multiple_of(ci * CONV_R, CONV_R)
            dx_ref[pl.ds(base, CONV_R), :] = _conv_tap_sum(_conv_taps(dcp, base, CONV_PAD + half, -1), w).astype(dx_ref.dtype)
            return carry

        lax.fori_loop(0, nchunk, second, 0, unroll=2)

    col = lambda r: pl.BlockSpec((r, CONV_CB), lambda j: (0, j))
    return pl.pallas_call(
        body, name="conv_bwd", grid=(ch // CONV_CB,),
        out_shape=[jax.ShapeDtypeStruct((s_len, ch), BF16), jax.ShapeDtypeStruct((CONV_WIDTH, ch), F32),
                   jax.ShapeDtypeStruct((1, ch), F32)],
        in_specs=[col(s_len), col(s_len), col(CONV_WIDTH), col(1)],
        out_specs=[col(s_len), col(CONV_WIDTH), col(1)],
        scratch_shapes=[pltpu.VMEM((s_len + 2 * CONV_PAD, CONV_CB), F32)] * 2,
        compiler_params=_params(("parallel",), VMEM_BIG),
    )(xpre, da, cw, cb)


SSD_GW = 256
SSD_N = 128
SSD_DTW = 128


def _bf16_parts(x, n):
    parts, rest = [], x
    for _ in range(n):
        p = rest.astype(BF16)
        parts.append(p)
        rest = rest - p.astype(F32)
    return parts


@jax.custom_vjp
def _expand(x, e):
    eb = e.astype(BF16)
    return _dg(jnp.concatenate(_bf16_parts(x, 2), axis=1), jnp.concatenate([eb, eb], axis=0), 1, 0)


def _expand_fwd(x, e):
    return _expand(x, e), e


def _expand_bwd(e, g):
    return _dg(g.astype(BF16), e.astype(BF16), 1, 1), jnp.zeros_like(e)


_expand.defvjp(_expand_fwd, _expand_bwd)


@jax.custom_vjp
def _running_sum(tri, x):
    tb = tri.astype(BF16)
    return sum(_dg(tb, p, 1, 0) for p in _bf16_parts(x, 3))


def _running_sum_fwd(tri, x):
    return _running_sum(tri, x), tri


def _running_sum_bwd(tri, g):
    tb = tri.astype(BF16)
    return jnp.zeros_like(tri), sum(_dg(tb, p, 0, 0) for p in _bf16_parts(g, 3))


_running_sum.defvjp(_running_sum_fwd, _running_sum_bwd)


def _pick_col(a, h):
    @jax.custom_vjp
    def pick(a):
        return a[:, h:h + 1]

    pick.defvjp(lambda a: (a[:, h:h + 1], None),
                lambda _, g: (g * (lax.broadcasted_iota(jnp.int32, (1, a.shape[1]), 1) == h).astype(F32),))
    return pick(a)


def _pick_row(a, h):
    @jax.custom_vjp
    def pick(a):
        return a[h:h + 1, :]

    pick.defvjp(lambda a: (a[h:h + 1, :], None),
                lambda _, g: (g * (lax.broadcasted_iota(jnp.int32, (a.shape[0], 1), 0) == h).astype(F32),))
    return pick(a)


def _ssd_mask(dirn):
    ri = lax.broadcasted_iota(jnp.int32, (CHUNK, CHUNK), 0)
    cj = lax.broadcasted_iota(jnp.int32, (CHUNK, CHUNK), 1)
    return (cj <= ri) if dirn == 0 else (cj >= ri)


def _ssd_rowsel(dirn):
    last = CHUNK - 1 if dirn == 0 else 0
    return (lax.broadcasted_iota(jnp.int32, (CHUNK, 1), 0) == last).astype(F32)


def _ssd_chunk_pre(dirn):
    nh = SSD_DTW

    def f(dt, alog):
        da = dt * (-jnp.exp(alog))
        cum = _running_sum(_ssd_mask(dirn).astype(F32), da)
        tot = jnp.sum(cum * _ssd_rowsel(dirn), axis=0, keepdims=True)
        hh = lax.broadcasted_iota(jnp.int32, (nh, SSD_HEADS * HEAD_DIM), 0)
        jj = lax.broadcasted_iota(jnp.int32, (nh, SSD_HEADS * HEAD_DIM), 1)
        expand = (hh == dirn * SSD_HEADS + jj // HEAD_DIM).astype(F32)
        return cum, cum.T, _expand(dt, expand), _expand(jnp.exp(tot - cum), expand), _expand(jnp.exp(cum), expand)

    return f


def _ssd_group_fn(g, dirn, stacked):
    def f(xs, bm, cm, st, cum, cum_t, dt_e, w_e, ce_e):
        mask = _ssd_mask(dirn)
        xdt = xs * dt_e
        cd_e = jnp.sum(ce_e * _ssd_rowsel(dirn), axis=0, keepdims=True)
        cb = _bnt(cm, bm)
        lane_head = lax.broadcasted_iota(jnp.int32, (1, SSD_GW), 1) // HEAD_DIM
        y = _bnn(cm, st) * ce_e
        decayed, inputs = [], []
        for j in range(4):
            hidx = dirn * SSD_HEADS + 4 * g + j
            col, row = _pick_col(cum, hidx), _pick_row(cum_t, hidx)
            dec = cb * jnp.exp(jnp.where(mask, col - row, NEG_BIG))
            head = (lane_head == j).astype(F32)
            if stacked:
                decayed.append(dec)
                inputs.append(xdt * head)
            else:
                y = y + _bnn(dec, xdt) * head
        if stacked:
            y = y + _bnn(jnp.concatenate(decayed, axis=1), jnp.concatenate(inputs, axis=0))
        st_out = st * cd_e + _btn(bm, xdt * w_e)
        return y, st_out

    return f


def _ssd_in_specs(kk):
    ln = CHUNK
    return [pl.BlockSpec((ln, 2048), lambda i: (kk(i), 0)),
            pl.BlockSpec((ln, 1024), lambda i: (kk(i), 2)),
            pl.BlockSpec((ln, 1024), lambda i: (kk(i), 3)),
            pl.BlockSpec((ln, SSD_DTW), lambda i: (kk(i), 0)),
            pl.BlockSpec((1, SSD_DTW), lambda i: (0, 0))]


def _ssd_fwd(xbc, dt, alog, dirn, prior=None):
    s_len = xbc.shape[0]
    nc = s_len // CHUNK
    kk = (lambda i: i) if dirn == 0 else (lambda i: nc - 1 - i)

    def body(x_ref, b_ref, c_ref, dt_ref, al_ref, *rest):
        prior_ref = rest[0] if prior is not None else None
        y_ref, sts_ref, st = rest[prior is not None:]

        @pl.when(pl.program_id(0) == 0)
        def _():
            st[...] = jnp.zeros_like(st)

        sts_ref[0] = st[...]
        cum, cum_t, dt_e, w_e, ce_e = _ssd_chunk_pre(dirn)(dt_ref[...], al_ref[...])
        for g in range(SSD_GROUPS):
            xc = slice(g * SSD_GW, (g + 1) * SSD_GW)
            gc = slice(g * SSD_N, (g + 1) * SSD_N)
            y, st_new = _ssd_group_fn(g, dirn, True)(x_ref[:, xc], b_ref[:, gc], c_ref[:, gc], st[:, xc], cum, cum_t,
                                               dt_e[:, xc], w_e[:, xc], ce_e[:, xc])
            y_ref[:, xc] = y if prior is None else y + prior_ref[:, xc]
            st[:, xc] = st_new

    return pl.pallas_call(
        body, name=f"ssd_fwd_d{dirn}", grid=(nc,),
        out_shape=[jax.ShapeDtypeStruct((s_len, 2048), F32), jax.ShapeDtypeStruct((nc, SSD_N, 2048), F32)],
        in_specs=_ssd_in_specs(kk) + ([pl.BlockSpec((CHUNK, 2048), lambda i: (kk(i), 0))] if prior is not None else []),
        out_specs=[pl.BlockSpec((CHUNK, 2048), lambda i: (kk(i), 0)),
                   pl.BlockSpec((1, SSD_N, 2048), lambda i: (kk(i), 0, 0))],
        scratch_shapes=[pltpu.VMEM((SSD_N, 2048), F32)],
        compiler_params=_params(("arbitrary",), VMEM_BIG),
    )(xbc, xbc, xbc, dt, alog, *([prior] if prior is not None else []))


def _ssd_bwd(xbc, dt, alog, states, dy, d_e, dirn, prior=None):
    s_len = xbc.shape[0]
    nc = s_len // CHUNK
    kk = (lambda i: nc - 1 - i) if dirn == 0 else (lambda i: i)

    def body(x_ref, b_ref, c_ref, dt_ref, al_ref, sts_ref, dy_ref, de_ref, *rest):
        prior_ref = rest[0] if prior is not None else None
        dx_ref, ddt_ref, dal_ref, dst = rest[prior is not None:]
        plus_prior = (lambda v, cols: v + prior_ref[:, cols]) if prior is not None else (lambda v, cols: v)

        @pl.when(pl.program_id(0) == 0)
        def _():
            dst[...] = jnp.zeros_like(dst)
            dal_ref[...] = jnp.zeros_like(dal_ref)

        (cum, cum_t, dt_e, w_e, ce_e), pre_vjp = jax.vjp(_ssd_chunk_pre(dirn), dt_ref[...], al_ref[...])
        dcum = jnp.zeros_like(cum)
        dcum_t = jnp.zeros_like(cum_t)
        d_dt_e, d_w_e, d_ce_e = [], [], []
        for g in range(SSD_GROUPS):
            xc = slice(g * SSD_GW, (g + 1) * SSD_GW)
            gc = slice(g * SSD_N, (g + 1) * SSD_N)
            _, vjp = jax.vjp(_ssd_group_fn(g, dirn, False), x_ref[:, xc], b_ref[:, gc], c_ref[:, gc], sts_ref[0, :, xc], cum, cum_t,
                             dt_e[:, xc], w_e[:, xc], ce_e[:, xc])
            dyg = dy_ref[:, xc]
            dxs, dbm, dcm, dst_g, dcum_g, dcum_t_g, ddte_g, dwe_g, dcee_g = vjp((dyg, dst[:, xc]))
            if dirn == 0:
                dxs = dxs + dyg * de_ref[:, xc]
            bc, cc = slice(2048 + g * SSD_N, 2048 + (g + 1) * SSD_N), slice(3072 + g * SSD_N, 3072 + (g + 1) * SSD_N)
            dx_ref[:, xc] = plus_prior(dxs, xc)
            dx_ref[:, bc] = plus_prior(dbm, bc)
            dx_ref[:, cc] = plus_prior(dcm, cc)
            dst[:, xc] = dst_g
            dcum = dcum + dcum_g
            dcum_t = dcum_t + dcum_t_g
            d_dt_e.append(ddte_g)
            d_w_e.append(dwe_g)
            d_ce_e.append(dcee_g)
        ddt, dal = pre_vjp((dcum, dcum_t, jnp.concatenate(d_dt_e, axis=1), jnp.concatenate(d_w_e, axis=1),
                            jnp.concatenate(d_ce_e, axis=1)))
        ddt_ref[...] = ddt
        dal_ref[...] += dal

    return pl.pallas_call(
        body, name=f"ssd_bwd_d{dirn}", grid=(nc,),
        out_shape=[jax.ShapeDtypeStruct((s_len, 4096), F32), jax.ShapeDtypeStruct((s_len, SSD_DTW), F32),
                   jax.ShapeDtypeStruct((1, SSD_DTW), F32)],
        in_specs=_ssd_in_specs(kk) + [pl.BlockSpec((1, SSD_N, 2048), lambda i: (kk(i), 0, 0)),
                                      pl.BlockSpec((CHUNK, 2048), lambda i: (kk(i), 0)),
                                      pl.BlockSpec((1, 2048), lambda i: (0, 0))]
        + ([pl.BlockSpec((CHUNK, 4096), lambda i: (kk(i), 0))] if prior is not None else []),
        out_specs=[pl.BlockSpec((CHUNK, 4096), lambda i: (kk(i), 0)),
                   pl.BlockSpec((CHUNK, SSD_DTW), lambda i: (kk(i), 0)),
                   pl.BlockSpec((1, SSD_DTW), lambda i: (0, 0))],
        scratch_shapes=[pltpu.VMEM((SSD_N, 2048), F32)],
        compiler_params=_params(("arbitrary",), VMEM_BIG),
    )(xbc, xbc, xbc, dt, alog, states, dy, d_e, *([prior] if prior is not None else []))


def _gate_norm_fn(y, xs, z, d_e, nw):
    yg = (y + xs * d_e) * _silu(z)
    return yg * lax.rsqrt(jnp.mean(yg * yg, axis=-1, keepdims=True) + NORM_EPS) * nw


def _gate_norm_bwd(dy, w_out, y, xbc, z, d_e, nw):
    def fn(du, y, xs, z, d_e, nw):
        sig = jax.nn.sigmoid(z)
        gate = z * sig
        ysum = y + xs * d_e
        yg = ysum * gate
        r = lax.rsqrt(jnp.mean(yg * yg, axis=-1, keepdims=True) + NORM_EPS)
        t = du * nw
        dyg = t * r - yg * (jnp.mean(t * yg, axis=-1, keepdims=True) * (r * r * r))
        dys = dyg * gate
        dz = dyg * ysum * (sig * (1.0 + z * (1.0 - sig)))
        dnw = jnp.sum(du * yg * r, axis=0, keepdims=True)
        dde = jnp.sum(dys * xs, axis=0, keepdims=True)
        hh = lax.broadcasted_iota(jnp.int32, (2048, SSD_HEADS), 0) // HEAD_DIM
        jj = lax.broadcasted_iota(jnp.int32, (2048, SSD_HEADS), 1)
        return [dys, dz], [dnw, _hnn(jnp.broadcast_to(dde, (8, 2048)), (hh == jj).astype(F32))[0:1]]

    (dys, dz), (g_nw, g_d) = _matmul_rows("ssd_out_dx_gate_norm_bwd", dy, w_out, "nt", 256, dy.shape[1], fn,
                                          [y, (xbc, 2048, 0), z], [d_e, nw], [(2048, F32), (2048, BF16)],
                                          [(1, 2048), (1, SSD_HEADS)])
    return dys, dz, g_nw, g_d


def _ssd_tail_loss(y, xbc, z, d_e, snw, w_out, x1, tgt, gate, fnw):
    dm = x1.shape[1]
    si = y.shape[1]

    def make_u(y, xs, z, x1, tgt, d_e, snw, gate, fnw):
        return _gate_norm_fn(y, xs, z, d_e, snw).astype(BF16)

    def fn(y1, u, y, xs, z, x1, tgt, d_e, snw, gate, fnw):
        x2 = x1 + gate * y1
        r = lax.rsqrt(jnp.mean(x2 * x2, axis=-1, keepdims=True) + NORM_EPS)
        xh = x2 * r
        err = xh * fnw - tgt
        loss = 0.5 * jnp.sum(jnp.mean(err * err, axis=-1, keepdims=True), axis=0, keepdims=True)
        dy = err * (1.0 / dm)
        dxh = dy * fnw
        dx2 = r * (dxh - xh * jnp.mean(dxh * xh, axis=-1, keepdims=True))
        dfnw = jnp.sum(dy * xh, axis=0, keepdims=True)
        return [u, dx2, gate * dx2], [dfnw, jnp.sum(dx2 * y1, axis=0, keepdims=True), jnp.broadcast_to(loss, (1, 128))]

    (u, dx2, dy1), (g_fnw, dgate, loss) = _matmul_rows(
        "ssd_out_loss", make_u, w_out, "nn", 256, si, fn, [y, (xbc, si, 0), z, x1, tgt], [d_e, snw, gate, fnw],
        [(si, BF16), (dm, F32), (dm, BF16)], [(1, dm), (1, dm), (1, 128)])
    return u, dx2, dy1, g_fnw, dgate, loss


def _softplus_fwd(dt_raw, bias):
    (dt,), _ = _rowwise("dt_softplus", lambda r, b: ([jax.nn.softplus(r + b)], []), [dt_raw], [bias],
                        [(dt_raw.shape[1], F32)], [], 512)
    return dt


def _softplus_bwd(ddt_f, ddt_b, dt_raw, bias):
    def fn(df, db, r, b):
        g = (df + db) * jax.nn.sigmoid(r + b)
        return [g], [jnp.sum(g, axis=0, keepdims=True)]

    w = dt_raw.shape[1]
    (g,), (gb,) = _rowwise("dt_softplus_bwd", fn, [ddt_f, ddt_b, dt_raw], [bias], [(w, BF16)], [(1, w)], 512)
    return g, gb


def _mod_part(c_all, mod_w):
    nl, _, ncol = mod_w.shape
    nb = c_all.shape[0]

    def body(c_ref, w_ref, o_ref):
        cond = _silu(c_ref[...])
        for i in range(nl):
            o_ref[i * nb:(i + 1) * nb, :] = _nn(cond, w_ref[i])

    return pl.pallas_call(body, name="mod_part", out_shape=jax.ShapeDtypeStruct((nl * nb, ncol), F32),
                          compiler_params=_params(None, VMEM_BIG))(c_all, mod_w)


def _mod_finish(mod_nb, mod_b, norm_w, tokens):
    nl, dm = norm_w.shape

    def body(a_ref, b_ref, nw_ref, *rest):
        tok_refs, o_refs = rest[:len(tokens)], rest[len(tokens):]
        tok = sum(t[0:1, 0:1] for t in tok_refs)
        for i in range(nl):
            for k in range(3):
                cols = slice(k * dm, (k + 1) * dm)
                o_refs[4 * i + k][...] = a_ref[i:i + 1, cols] + b_ref[i:i + 1, cols]
            o_refs[4 * i + 3][...] = nw_ref[i:i + 1, :] + tok

    rows = pl.pallas_call(body, name="mod_finish", out_shape=[jax.ShapeDtypeStruct((1, dm), F32)] * (4 * nl))(
        mod_nb, mod_b, norm_w, *tokens)
    return [rows[4 * i:4 * i + 4] for i in range(nl)]


def _mod_grad(c_all, dmod_sh):
    nl, nb, ncol = dmod_sh.shape
    dm = c_all.shape[1]

    def body(c_ref, d_ref, o_ref):
        cond = _silu(c_ref[...])
        for i in range(nl):
            o_ref[i] = _tn(cond, d_ref[i])

    return pl.pallas_call(body, name="mod_grad", out_shape=jax.ShapeDtypeStruct((nl, dm, ncol), F32),
                          compiler_params=_params(None, VMEM_BIG))(c_all, dmod_sh)


PACK_ROWS = 16
PACK_COLS = 1024


def _pack_small(rows, b64, a64s, d32, extra):
    nr, na = len(rows), len(a64s)

    def body(*refs):
        o_ref = refs[-1]
        o_ref[...] = jnp.zeros_like(o_ref)
        for i in range(nr):
            o_ref[i:i + 1, :] = refs[i][...]
        b_ref, a_refs, d_ref, e_ref = refs[nr], refs[nr + 1:nr + 1 + na], refs[nr + 1 + na], refs[nr + 2 + na]
        o_ref[nr:nr + 1, 0:64] = b_ref[:, 0:64]
        o_ref[nr:nr + 1, 64:128] = sum(a[:, 0:64] for a in a_refs)
        o_ref[nr:nr + 1, 128:160] = d_ref[...]
        o_ref[nr:nr + 1, 256:384] = e_ref[...]

    return pl.pallas_call(body, name="pack_small", out_shape=jax.ShapeDtypeStruct((PACK_ROWS, PACK_COLS), F32))(
        *rows, b64, *a64s, d32, extra)


def _pack_ssd_small(cw, cb, nw):
    def body(cw_ref, cb_ref, nw_ref, o_ref):
        o_ref[...] = jnp.zeros_like(o_ref)
        o_ref[0:5, :] = cw_ref[...]
        o_ref[5:6, :] = cb_ref[...]
        o_ref[6:7, 0:256] = nw_ref[...]

    return pl.pallas_call(body, name="pack_ssd_small", out_shape=jax.ShapeDtypeStruct((8, 512), F32))(cw, cb, nw)


def _sum_parts(p_ref):
    g = p_ref[0].astype(F32)
    for s in range(1, p_ref.shape[0]):
        g = g + p_ref[s].astype(F32)
    return g


def _adam_update(w, g, m, v):
    m2 = ADAM_B1 * m + (1.0 - ADAM_B1) * g
    v2 = ADAM_B2 * v + (1.0 - ADAM_B2) * (g * g)
    m_hat = m2 / (1.0 - ADAM_B1 ** ADAM_STEP)
    v_hat = v2 / (1.0 - ADAM_B2 ** ADAM_STEP)
    return -ADAM_LR * (m_hat / (jnp.sqrt(v_hat) + ADAM_EPS) + ADAM_WD * w), m2, v2


def _adamw_windows(name, parts, params, windows, extra=None):
    n = len(params)

    def body(p_ref, *rest):
        ins, outs = rest[:3 * n], rest[3 * n:]
        g = _sum_parts(p_ref)
        for pi, rows, cols, idx in windows:
            w_ref, m_ref, v_ref = ins[3 * pi:3 * pi + 3]
            gw = g[rows, cols]
            dw, m2, v2 = _adam_update(w_ref[idx], gw, m_ref[idx], v_ref[idx])
            for o_ref, val in zip(outs[4 * pi:4 * pi + 4], (gw, dw, m2, v2), strict=True):
                o_ref[idx] = val
        if extra is not None:
            outs[4 * n][...] = g[extra[0], extra[1]]

    out_shape = [jax.ShapeDtypeStruct(w.shape, F32) for (w, _, _) in params for _ in range(4)]
    if extra is not None:
        out_shape.append(jax.ShapeDtypeStruct((extra[0].stop - extra[0].start, extra[1].stop - extra[1].start), F32))
    res = pl.pallas_call(body, name=name, out_shape=out_shape)(parts, *[a for p in params for a in p])
    return [res[4 * i:4 * i + 4] for i in range(n)] + ([res[4 * n]] if extra is not None else [])


def _adamw(name, w, parts, m, v, tr, tc=None):
    r_, c_ = w.shape
    p_ = parts.shape[0]
    tr = min(tr, r_)
    tc = c_ if tc is None else tc
    assert r_ % tr == 0 and c_ % tc == 0

    def body(w_ref, p_ref, m_ref, v_ref, g_ref, d_ref, m2_ref, v2_ref):
        g = _sum_parts(p_ref)
        g_ref[...] = g
        d_ref[...], m2_ref[...], v2_ref[...] = _adam_update(w_ref[...], g, m_ref[...], v_ref[...])

    blk = pl.BlockSpec((tr, tc), lambda i, j: (i, j))
    return pl.pallas_call(
        body, name=name, grid=(r_ // tr, c_ // tc), out_shape=[jax.ShapeDtypeStruct((r_, c_), F32)] * 4,
        in_specs=[blk, pl.BlockSpec((p_, tr, tc), lambda i, j: (0, i, j)), blk, blk], out_specs=[blk] * 4,
        compiler_params=_params(("parallel", "parallel"), VMEM_BIG),
    )(w, parts, m, v)


def _dev_index(p):
    return 4 * p[0] + 2 * p[1] + p[2]


def _all_gather(name, xs):
    n = len(xs)
    hbm = pl.BlockSpec(memory_space=pl.ANY)

    def body(*refs):
        x_refs, o_refs = refs[:n], refs[n:2 * n]
        send_sems, recv_sems, local_sems = refs[2 * n:]
        x, y, c = lax.axis_index("x"), lax.axis_index("y"), lax.axis_index("c")
        me, sibling = (x, y, c), (x, y, 1 - c)
        chips = [(1 - x, y), (x, 1 - y), (1 - x, 1 - y)]

        def copy(a, k, block, to, src=None):
            dst = o_refs[a].at[_dev_index(block)]
            return pltpu.make_async_remote_copy(
                src_ref=dst if src is None else src, dst_ref=dst, send_sem=send_sems.at[a, k],
                recv_sem=recv_sems.at[a, k], device_id=to, device_id_type=MESH)

        mine = [pltpu.make_async_copy(x_refs[a], o_refs[a].at[_dev_index(me)], local_sems.at[a]) for a in range(n)]
        for cp in mine:
            cp.start()
        first = []
        for a in range(n):
            first.append(copy(a, 0, me, sibling, src=x_refs[a]))
            first += [copy(a, 1 + j, me, (*chip, c), src=x_refs[a]) for j, chip in enumerate(chips)]
        for cp in first:
            cp.start()
        passed = []
        for j, chip in enumerate(chips):
            for a in range(n):
                copy(a, 1 + j, (*chip, c), me).wait_recv()
                cp = copy(a, 4 + j, (*chip, c), sibling)
                cp.start()
                passed.append(cp)
        for a in range(n):
            copy(a, 0, sibling, me).wait_recv()
            for j, chip in enumerate(chips):
                copy(a, 4 + j, (*chip, 1 - c), me).wait_recv()
        for cp in first + passed:
            cp.wait_send()
        for cp in mine:
            cp.wait()

    return pl.pallas_call(
        body, name=name, out_shape=[jax.ShapeDtypeStruct((NDEV, *x.shape), x.dtype) for x in xs],
        in_specs=[hbm] * n, out_specs=[hbm] * n,
        scratch_shapes=[pltpu.SemaphoreType.DMA((n, 7)), pltpu.SemaphoreType.DMA((n, 7)), pltpu.SemaphoreType.DMA((n,))],
    )(*xs)


_HBM = pl.BlockSpec(memory_space=pltpu.HBM)
_SEM = pl.BlockSpec(memory_space=pltpu.SEMAPHORE)
_EFFECT = pltpu.SideEffectType.DATAFLOW_SIDE_EFFECTING


def _mesh_position():
    return lax.axis_index("x"), lax.axis_index("y"), lax.axis_index("c")


def _peers(me):
    return [(k, tuple(1 - v if (k >> b) & 1 else v for v, b in zip(me, (2, 1, 0)))) for k in range(1, NDEV)]


EXCHANGE_COPIES = {"gather": NDEV - 1, "scatter": NDEV - 1, "pair": 4, "chips": 3}
NCHIP = NDEV // 2


def _landing_zones(name, xs, mode):
    x_, y_, c_ = _mesh_position()
    mine = (2 * x_ + y_ if mode == "chips" else _dev_index((x_, y_, c_))).astype(jnp.int32).reshape(1)
    lands = []
    for a, x in enumerate(xs):
        rows, cols = x.shape[-2:]
        if mode == "pair":
            lands.append(lax.empty((NCHIP, rows, cols), x.dtype))
            continue
        tr = 256 if rows % 256 == 0 else rows

        def body(me_ref, x_ref, o_ref):
            o_ref[...] = x_ref[...]

        if mode == "gather":
            in_spec = pl.BlockSpec((tr, cols), lambda i, me_ref: (i, 0))
        else:
            in_spec = pl.BlockSpec((None, tr, cols), lambda i, me_ref: (me_ref[0], i, 0))
        lands.append(pl.pallas_call(
            body, name=f"{name}_{a}",
            out_shape=jax.ShapeDtypeStruct((NCHIP if mode == "chips" else NDEV, rows, cols), x.dtype),
            grid_spec=pltpu.PrefetchScalarGridSpec(
                num_scalar_prefetch=1, grid=(rows // tr,), in_specs=[in_spec],
                out_specs=pl.BlockSpec((None, tr, cols), lambda i, me_ref: (me_ref[0], i, 0))),
            compiler_params=_params(("arbitrary",)),
        )(mine, x))
    return lands


def _exchange_copies(x_refs, land_refs, send_sems, recv_sems, mode):
    x_, y_, c_ = me = _mesh_position()
    per_array = EXCHANGE_COPIES[mode]
    out = []

    def add(a, k, src, dst, peer):
        sem = a * per_array + k
        out.append(pltpu.make_async_remote_copy(src_ref=src, dst_ref=dst, send_sem=send_sems.at[sem], recv_sem=recv_sems.at[sem],
                                                device_id=peer, device_id_type=MESH))

    for a, (x_ref, land_ref) in enumerate(zip(x_refs, land_refs)):
        if mode in ("gather", "scatter"):
            for k, peer in _peers(me):
                add(a, k - 1, x_ref.at[_dev_index(peer)] if mode == "scatter" else x_ref, land_ref.at[_dev_index(me)], peer)
        elif mode == "pair":
            for chip in range(NCHIP):
                add(a, chip, x_ref.at[2 * chip + 1 - c_], land_ref.at[chip], (x_, y_, 1 - c_))
        else:
            for k in range(1, NCHIP):
                px, py = (1 - x_ if k & 2 else x_), (1 - y_ if k & 1 else y_)
                add(a, k - 1, x_ref.at[2 * px + py], land_ref.at[2 * x_ + y_], (px, py, c_))
    return out


def _exchange_start(name, xs, lands, mode, dep):
    n = len(xs)

    def body(*refs):
        x_refs, land_refs = refs[:n], refs[n:2 * n]
        send_sems, recv_sems = refs[2 * n + 1], refs[2 * n + 2]
        token = refs[-1]
        for cp in _exchange_copies(x_refs, land_refs, send_sems, recv_sems, mode):
            cp.start()
        token[...] = jnp.zeros_like(token)

    sems = pltpu.SemaphoreType.DMA((n * EXCHANGE_COPIES[mode],))
    res = pl.pallas_call(
        body, name=name,
        out_shape=(sems, sems, *[pltpu.HBM(a.shape, a.dtype) for a in (*xs, *lands)], jax.ShapeDtypeStruct((8, 128), F32)),
        in_specs=[_HBM] * (2 * n) + [pl.BlockSpec(memory_space=pl.ANY)],
        out_specs=(_SEM, _SEM, *[_HBM] * (2 * n), pl.BlockSpec(memory_space=pltpu.VMEM)),
        input_output_aliases={i: 2 + i for i in range(2 * n)},
        compiler_params=pltpu.CompilerParams(has_side_effects=_EFFECT),
    )(*[pltpu.with_memory_space_constraint(a, pltpu.HBM) for a in (*xs, *lands)], dep)
    return res[:-1], res[-1]


def _exchange_wait(name, handles, mode, after):
    send_sems, recv_sems = handles[0], handles[1]
    bufs = handles[2:]
    n = len(bufs) // 2

    def body(*refs):
        x_refs, land_refs = refs[:n], refs[n:2 * n]
        s_sems, r_sems = refs[2 * n], refs[2 * n + 1]
        for cp in _exchange_copies(x_refs, land_refs, s_sems, r_sems, mode):
            cp.wait_send()
            cp.wait_recv()

    res = pl.pallas_call(
        body, name=name, out_shape=tuple(pltpu.HBM(a.shape, a.dtype) for a in bufs),
        in_specs=[_HBM] * (2 * n) + [_SEM, _SEM, pl.BlockSpec(memory_space=pl.ANY)], out_specs=tuple([_HBM] * (2 * n)),
        input_output_aliases={i: i for i in range(2 * n)},
        compiler_params=pltpu.CompilerParams(has_side_effects=_EFFECT),
    )(*bufs, send_sems, recv_sems, after)
    return res[n:]


def _pair_sum(name, x, from_sibling):
    _, rows, cols = x.shape
    tr = rows
    core = lax.axis_index("c").astype(jnp.int32).reshape(1)

    def body(c_ref, x_ref, s_ref, o_ref):
        o_ref[...] = (x_ref[...].astype(F32) + s_ref[...].astype(F32)).astype(o_ref.dtype)

    return pl.pallas_call(
        body, name=name, out_shape=jax.ShapeDtypeStruct((NCHIP, rows, cols), x.dtype),
        grid_spec=pltpu.PrefetchScalarGridSpec(
            num_scalar_prefetch=1, grid=(NCHIP, rows // tr),
            in_specs=[pl.BlockSpec((None, tr, cols), lambda j, i, c_ref: (2 * j + c_ref[0], i, 0)),
                      pl.BlockSpec((None, tr, cols), lambda j, i, c_ref: (j, i, 0))],
            out_specs=pl.BlockSpec((None, tr, cols), lambda j, i, c_ref: (j, i, 0))),
        compiler_params=_params(("parallel", "parallel")),
    )(core, x, from_sibling)


def kernel(x, c, positions, norm_w, mod_w, mod_b, attn_w_in, attn_w_out, ssd_w_in, ssd_conv_w, ssd_conv_b, ssd_dt_bias, ssd_a_log, ssd_d, ssd_norm_w, ssd_w_out, final_norm_w, loss_target, m_norm_w, m_mod_w, m_mod_b, m_attn_w_in, m_attn_w_out, m_ssd_w_in, m_ssd_conv_w, m_ssd_conv_b, m_ssd_dt_bias, m_ssd_a_log, m_ssd_d, m_ssd_norm_w, m_ssd_w_out, m_final_norm_w, v_norm_w, v_mod_w, v_mod_b, v_attn_w_in, v_attn_w_out, v_ssd_w_in, v_ssd_conv_w, v_ssd_conv_b, v_ssd_dt_bias, v_ssd_a_log, v_ssd_d, v_ssd_norm_w, v_ssd_w_out, v_final_norm_w):
    s_len, dm = x.shape[1], x.shape[2]
    me = 4 * lax.axis_index("x") + 2 * lax.axis_index("y") + lax.axis_index("c")
    x0 = x.reshape(s_len, dm)
    tgt = loss_target.reshape(s_len, dm)
    aw = 3 * 512
    si = 2 * dm
    sxbc = 2 * si
    n_ssd_in = ssd_w_in.shape[2] * NDEV

    g_ai, c_all = _all_gather("gather_attn_w_in", [attn_w_in[0].astype(BF16), c])
    w_ai = g_ai.transpose(1, 0, 2).reshape(dm, 4 * aw)
    wcol = attn_w_in.shape[2]
    c_all = c_all.reshape(NDEV, dm)

    part = _mod_part(c_all, mod_w)
    (part_all,) = _all_gather("gather_mod", [part])
    mod_nb = jnp.stack([lax.dynamic_index_in_dim(part_all, i * NDEV + me, axis=1, keepdims=False).reshape(3 * dm)
                        for i in range(2)])

    ssd_small = _pack_ssd_small(ssd_conv_w[0], ssd_conv_b, ssd_norm_w)
    ao_shard = [attn_w_out[0].astype(BF16)]
    ao_handles, ao_token = _exchange_start("w_out_start", ao_shard, _landing_zones("w_out_place", ao_shard, "gather"), "gather",
                                           part_all)
    late_shards = [ssd_w_in[0].T.astype(BF16), ssd_w_out[0].astype(BF16), ssd_small]
    w_handles, w_token = _exchange_start("weights_start", late_shards, _landing_zones("weights_place", late_shards, "gather"),
                                         "gather", ao_token)
    (shift0, scale0, gate0, nw0), (shift1, scale1, gate1, nw1) = _mod_finish(mod_nb, mod_b, norm_w, [ao_token, w_token])
    shift, scale, gate, nw = [shift0, shift1], [scale0, scale1], [gate0, gate1], [nw0, nw1]

    hn0 = _norm_mod_fwd("norm0", x0, nw[0], scale[0], shift[0])
    inv_freq = ROPE_THETA ** (-jnp.arange(0, ROT_DIM, 2, dtype=F32) / ROT_DIM)
    lane = jnp.arange(128) % HEAD_DIM
    inv_row = jnp.where(lane < ROT_DIM, inv_freq[lane % (ROT_DIM // 2)], 0.0).reshape(1, 128).astype(F32)
    tabs = _rope_tables(positions.reshape(s_len, 1), inv_row)
    qk = _matmul("proj_qk", hn0, w_ai, "nn", F32, MM_T, MM_T, dm, epilogue=_rot_fwd, mrows=tabs, n_out=2 * aw)
    v = _matmul("proj_vz", hn0, w_ai, "nn", F32, MM_T, MM_T, dm, b_noff=2 * aw, n_out=2 * aw)
    z0 = (v, 1)
    att = [_attn_fwd(g, qk, v) for g in range(3)]
    os_, lses = [a[0] for a in att], [a[1] for a in att]
    (g_ao,) = _exchange_wait("w_out_wait", ao_handles, "gather", lses[2])
    a0, y0, x1 = _attn_out(os_, lses, z0, x0, gate[0], g_ao.reshape(aw, dm))

    hn1 = _norm_mod_fwd("norm1", x1, nw[1], scale[1], shift[1])
    g_si, g_so, g_small = _exchange_wait("weights_wait", w_handles, "gather", hn1)
    w_ao = g_ao.reshape(aw, dm)
    w_si_t = g_si.reshape(n_ssd_in, dm)
    w_so = g_so.reshape(si, dm)
    conv_w = g_small[:, 0:CONV_WIDTH, :].transpose(1, 0, 2).reshape(CONV_WIDTH, sxbc)
    conv_b = g_small[:, 5, :].reshape(1, sxbc)
    snw = g_small[:, 6, 0:si // NDEV].reshape(1, si)
    ndt = 2 * SSD_HEADS
    z1 = _matmul("ssd_proj_z", hn1, w_si_t, "nt", F32, MM_T, MM_T, dm, n_out=si)
    xpre = _matmul("ssd_proj_xbc", hn1, w_si_t, "nt", F32, MM_T, MM_T, dm, b_noff=si, n_out=sxbc)
    dt_raw = _matmul("ssd_proj_dt", hn1, w_si_t, "nt", F32, MM_T, ndt, dm, b_noff=si + sxbc, n_out=ndt)
    xbc = _conv_fwd(xpre, conv_w, conv_b)
    widen = lambda a: jnp.pad(a, ((0, 0), (0, SSD_DTW - ndt)))
    dt_raw = widen(dt_raw)
    dt_bias = widen(ssd_dt_bias.reshape(1, ndt))
    alog = widen(ssd_a_log.reshape(1, ndt))
    dt = _softplus_fwd(dt_raw, dt_bias)
    y_f, st_f = _ssd_fwd(xbc, dt, alog, 0)
    y_fb, st_b = _ssd_fwd(xbc, dt, alog, 1, prior=y_f)
    d_e = jnp.repeat(ssd_d.reshape(SSD_HEADS), HEAD_DIM).reshape(1, si)

    fnw = final_norm_w.reshape(1, dm)
    u, dx2, dy1, g_fnw, dgate1, loss_part = _ssd_tail_loss(y_fb, xbc, z1, d_e, snw, w_so, x1, tgt, gate[1], fnw)
    gw_so = _matmul("ssd_out_dw", u, dy1, "tn", BF16, MM_T, MM_T, MM_T)
    dys, dz1, g_snw, g_d = _gate_norm_bwd(dy1, w_so, y_fb, xbc, z1, d_e, snw)
    dxbc_f, ddt_f, dalog_f = _ssd_bwd(xbc, dt, alog, st_f, dys, d_e, 0)
    dxbc, ddt_b, dalog_b = _ssd_bwd(xbc, dt, alog, st_b, dys, d_e, 1, prior=dxbc_f)
    dpre, g_cw, g_cb = _conv_bwd(xpre, dxbc, conv_w, conv_b)
    ddt_raw, g_dtb = _softplus_bwd(ddt_f, ddt_b, dt_raw, dt_bias)
    ddt_raw = ddt_raw[:, :ndt]
    dhn1 = [_matmul("ssd_proj_z_dx", dz1, w_si_t, "nn", F32, MM_T, MM_T, MM_T),
            _matmul("ssd_proj_xbc_dx", dpre, w_si_t, "nn", F32, MM_T, MM_T, MM_T, b_koff=si)]
    gw_si_t = _matmul("ssd_proj_z_dw", dz1, hn1, "tn", BF16, MM_T, MM_T, MM_T, dest=(n_ssd_in, 0, None))
    gw_si_t = _matmul("ssd_proj_xbc_dw", dpre, hn1, "tn", BF16, MM_T, MM_T, MM_T, dest=(n_ssd_in, si, gw_si_t))
    gw_si_t = _matmul("ssd_proj_dt_dw", ddt_raw, hn1, "tn", BF16, ndt, MM_T, MM_T, dest=(n_ssd_in, si + sxbc, gw_si_t))

    l1_grads = [gw_so.reshape(NDEV, si // NDEV, dm), gw_si_t.reshape(NDEV, n_ssd_in // NDEV, dm),
                _pack_ssd_small_blocks(g_cw, g_cb, g_snw)]
    l1_handles, l1_token = _exchange_start("l1_grads_start", l1_grads, _landing_zones("l1_grads_place", l1_grads, "scatter"),
                                           "scatter", dhn1[1])
    dx1, dy0, g_nw1, dsc1, dsh1, dgate0 = _norm_mod_bwd(
        "ssd_proj_dt_dx_norm1_bwd", (ddt_raw, w_si_t, "nn", ndt, dict(b_koff=si + sxbc)), x1, dhn1, dx2,
        nw[1], scale[1], shift[1], prev=(y0, gate[0] + l1_token[0:1, 0:1]))

    gw_ao = _matmul("attn_out_dw", a0, dy0, "tn", BF16, aw // 2, MM_T, MM_T)
    dos, dls, dz0 = _mix_bwd(dy0, w_ao, os_, lses, z0)
    datt = [_attn_bwd(g, qk, v, os_[g], lses[g], dos[g], dls[g]) for g in range(3)]
    dqkv = _rot_pack_bwd([t[0] for t in datt], [t[1] for t in datt], [t[2] for t in datt], tabs)
    gw_ai = _matmul("proj_qkv_dw", hn0, dqkv, "tn", BF16, MM_T, wcol, MM_T, out_blocks=3 * aw // wcol, dest=(NDEV, 0, None))
    gw_ai = _matmul("proj_z_dw", hn0, dz0, "tn", BF16, MM_T, wcol, MM_T, out_blocks=aw // wcol,
                    dest=(NDEV, 3 * aw // wcol, gw_ai))
    after_start = lambda acc, t: acc + t
    zero_row = lambda token: jnp.tile(token[0:1], (1, dm // 128))
    l0_grads = [gw_ai, gw_ao.reshape(NDEV, aw // NDEV, dm)]
    pair_handles, pair_token = _exchange_start("l0_pair_start", l0_grads, _landing_zones("l0_pair_place", l0_grads, "pair"),
                                               "pair", dqkv)
    dhn0_z = _matmul("proj_z_dx", dz0, w_ai, "nt", F32, MM_T, MM_T, aw, b_koff=3 * aw, n_out=dm, epilogue=after_start,
                     ncols=(zero_row(pair_token),))
    from_sibling = _exchange_wait("l0_pair_wait", pair_handles, "pair", dhn0_z)
    chip_sums = [_pair_sum(f"l0_pair_sum_{a}", g, s) for a, (g, s) in enumerate(zip(l0_grads, from_sibling))]
    l0_handles, l0_token = _exchange_start("l0_grads_start", chip_sums, _landing_zones("l0_grads_place", chip_sums, "chips"),
                                           "chips", dhn0_z)
    dx0, g_nw0, dsc0, dsh0 = _norm_mod_bwd(
        "proj_qkv_dx_norm0_bwd", (dqkv, w_ai, "nt", aw, dict(n_out=dm)), x0, [dhn0_z], dx1,
        nw[0], scale[0], shift[0] + zero_row(l0_token))

    small_g = [_pack_small([dsh0, dsc0, dgate0, dsh1, dsc1, dgate1, g_nw0, g_nw1, g_fnw], g_dtb, [dalog_f, dalog_b], g_d, loss_part)]
    sm_handles, sm_token = _exchange_start("small_grads_start", small_g, _landing_zones("small_grads_place", small_g, "gather"),
                                           "gather", dx0)

    whole = (slice(None), slice(None))
    r_so, r_si, r_small = _exchange_wait("l1_grads_wait", l1_handles, "scatter", sm_token)
    si_out = [o.T for o in _adamw("adamw_ssd_w_in", ssd_w_in[0].T, r_si, m_ssd_w_in[0].T, v_ssd_w_in[0].T, n_ssd_in // NDEV, 256)]
    so_out = _adamw("adamw_ssd_w_out", ssd_w_out[0], r_so, m_ssd_w_out[0], v_ssd_w_out[0], 256)
    cw_cols = ssd_conv_w.shape[2]
    cw_out, cb_out, snw_out = _adamw_windows(
        "adamw_ssd_small", r_small,
        [(ssd_conv_w, m_ssd_conv_w, v_ssd_conv_w), (ssd_conv_b, m_ssd_conv_b, v_ssd_conv_b),
         (ssd_norm_w, m_ssd_norm_w, v_ssd_norm_w)],
        [(0, slice(0, CONV_WIDTH), slice(0, cw_cols), (0, slice(None), slice(None))),
         (1, slice(5, 6), slice(0, cw_cols), whole), (2, slice(6, 7), slice(0, si // NDEV), whole)])
    r_ai, r_ao = _exchange_wait("l0_grads_wait", l0_handles, "chips", so_out[0])
    ai_out = _adamw("adamw_attn_w_in", attn_w_in[0], r_ai, m_attn_w_in[0], v_attn_w_in[0], 256)
    ao_out = _adamw("adamw_attn_w_out", attn_w_out[0], r_ao, m_attn_w_out[0], v_attn_w_out[0], 192)

    (small_all,) = _exchange_wait("small_grads_wait", sm_handles, "gather", ai_out[0])
    full = slice(0, PACK_COLS)
    nhd = SSD_HEADS
    windows = [(0, slice(3 * i + k, 3 * i + k + 1), full, (slice(i, i + 1), slice(k * dm, (k + 1) * dm)))
               for i in range(2) for k in range(3)]
    windows += [(1, slice(6 + i, 7 + i), full, (slice(i, i + 1), slice(None))) for i in range(2)]
    windows += [(2, slice(8, 9), full, whole)]
    windows += [(3 + q, slice(9, 10), slice(2 * nhd * q + nhd * j, 2 * nhd * q + nhd * (j + 1)), (0, slice(j, j + 1), slice(None)))
                for q in range(2) for j in range(2)]
    windows += [(5, slice(9, 10), slice(4 * nhd, 5 * nhd), whole)]
    as_row = lambda a: a.reshape(1, dm)
    mb_out, nw_out, fnw_out, dtb_out, alog_out, d_out, loss = _adamw_windows(
        "adamw_small", small_all,
        [(mod_b, m_mod_b, v_mod_b), (norm_w, m_norm_w, v_norm_w), (fnw, as_row(m_final_norm_w), as_row(v_final_norm_w)),
         (ssd_dt_bias, m_ssd_dt_bias, v_ssd_dt_bias), (ssd_a_log, m_ssd_a_log, v_ssd_a_log), (ssd_d, m_ssd_d, v_ssd_d)],
        windows, extra=(slice(9, 10), slice(256, 257)))
    loss = loss.reshape(())

    ncol = mod_w.shape[2]
    dmod_all = small_all[:, 0:6, :].reshape(NDEV, 2, 3 * dm)
    dmod_sh = lax.dynamic_slice_in_dim(dmod_all, me * ncol, ncol, axis=2).transpose(1, 0, 2)
    g_modw = _mod_grad(c_all, dmod_sh).reshape(1, 2 * dm, ncol)
    modw_out = _adamw("adamw_mod_w", mod_w.reshape(2 * dm, ncol), g_modw, m_mod_w.reshape(2 * dm, ncol),
                      v_mod_w.reshape(2 * dm, ncol), 256)

    per_kind = []
    for k in range(4):
        per_kind.append([
            nw_out[k], modw_out[k].reshape(mod_w.shape), mb_out[k], ai_out[k][None], ao_out[k][None], si_out[k][None],
            cw_out[k], cb_out[k], dtb_out[k], alog_out[k], d_out[k], snw_out[k], so_out[k][None], fnw_out[k].reshape(dm)])
    return (loss, dx0.reshape(x.shape), *per_kind[0], *per_kind[1], *per_kind[2], *per_kind[3])


def _pack_ssd_small_blocks(g_cw, g_cb, g_nw):
    nper = g_cw.shape[1] // NDEV
    nwper = g_nw.shape[1] // NDEV

    def body(cw_ref, cb_ref, nw_ref, o_ref):
        o_ref[...] = jnp.zeros_like(o_ref)
        for d in range(NDEV):
            o_ref[d, 0:5, :] = cw_ref[:, d * nper:(d + 1) * nper]
            o_ref[d, 5:6, :] = cb_ref[:, d * nper:(d + 1) * nper]
            o_ref[d, 6:7, 0:nwper] = nw_ref[:, d * nwper:(d + 1) * nwper]

    return pl.pallas_call(body, name="pack_ssd_small_grads", out_shape=jax.ShapeDtypeStruct((NDEV, 8, nper), F32))(g_cw, g_cb, g_nw)
```

```python
import functools
import math

import jax
import jax.numpy as jnp
from jax import lax
from jax.experimental import pallas as pl
from jax.experimental.pallas import tpu as pltpu

F32 = jnp.float32
BF16 = jnp.bfloat16
HI = lax.Precision.HIGHEST
MESH = pl.DeviceIdType.MESH
NDEV = 8

NORM_EPS = 1e-6
ROPE_THETA = 500000.0
ROT_DIM = 16
HEAD_DIM = 64
DILATIONS = (1, 4, 16)
BAND = 64
NEG_BIG = -1e30
CHUNK = 128
SSD_HEADS = 32
SSD_GROUPS = 8
CONV_WIDTH = 5

ADAM_LR = 0.001
ADAM_B1 = 0.9
ADAM_B2 = 0.999
ADAM_EPS = 1e-08
ADAM_WD = 0.01
ADAM_STEP = 10

VMEM_BIG = 56 * 1024 * 1024
MM_T = 1024


def _params(sem=None, vmem=None):
    kw = {}
    if sem is not None:
        kw["dimension_semantics"] = sem
    if vmem is not None:
        kw["vmem_limit_bytes"] = vmem
    return pltpu.CompilerParams(**kw)


def _dg(a, b, ca, cb, prec=None):
    return lax.dot_general(a, b, (((ca,), (cb,)), ((), ())), preferred_element_type=F32, precision=prec)


def _nn(a, b):
    return _dg(a.astype(BF16), b.astype(BF16), 1, 0)


def _nt(a, b):
    return _dg(a.astype(BF16), b.astype(BF16), 1, 1)


def _tn(a, b):
    return _dg(a.astype(BF16), b.astype(BF16), 0, 0)


def _hnn(a, b):
    return _dg(a, b, 1, 0, HI)


@jax.custom_vjp
def _bnn(a, b):
    return _nn(a, b)


_bnn.defvjp(lambda a, b: (_nn(a, b), (a, b)), lambda r, g: (_nt(g, r[1]), _tn(r[0], g)))


@jax.custom_vjp
def _bnt(a, b):
    return _nt(a, b)


_bnt.defvjp(lambda a, b: (_nt(a, b), (a, b)), lambda r, g: (_nn(g, r[1]), _tn(g, r[0])))


@jax.custom_vjp
def _btn(a, b):
    return _tn(a, b)


_btn.defvjp(lambda a, b: (_tn(a, b), (a, b)), lambda r, g: (_nt(r[1], g), _nn(r[0], g)))


def _silu(x):
    return x * jax.nn.sigmoid(x)


def _b_spec(b, mode, tn, tk, no, ko, jk):
    if mode == "nt":
        return pl.BlockSpec((tn, tk), lambda *g: (jk(*g)[0] + no, jk(*g)[1] + ko))
    return pl.BlockSpec((tk, tn), lambda *g: (jk(*g)[1] + ko, jk(*g)[0] + no))


def _matmul(name, a, b, mode, out_dtype, tm, tn, tk, *, epilogue=None, tiled=(), mrows=(), ncols=(),
            b_noff=0, b_koff=0, n_out=None, out_blocks=None, dest=None):
    if mode == "tn":
        K, M = a.shape
    else:
        M, K = a.shape
    N = n_out if n_out is not None else (b.shape[0] if mode == "nt" else b.shape[1])
    tm, tn, tk = min(tm, M), min(tn, N), min(tk, K)
    assert M % tm == 0 and N % tn == 0 and K % tk == 0, (name, M, N, K, tm, tn, tk)
    assert b_noff % tn == 0 and b_koff % tk == 0
    no, ko = b_noff // tn, b_koff // tk
    nk = K // tk
    if mode == "tn":
        a_spec = pl.BlockSpec((tk, tm), lambda i, j, k: (k, i))
    else:
        a_spec = pl.BlockSpec((tm, tk), lambda i, j, k: (i, k))
    specs = [a_spec, _b_spec(b, mode, tn, tk, no, ko, lambda i, j, k: (j, k))]
    specs += [pl.BlockSpec((tm, tn), lambda i, j, k: (i, j)) for _ in tiled]
    specs += [pl.BlockSpec((tm, r.shape[1]), lambda i, j, k: (i, 0)) for r in mrows]
    specs += [pl.BlockSpec((1, tn), lambda i, j, k: (0, j)) for _ in ncols]
    total, off, earlier = dest if dest is not None else (None, 0, None)
    if out_blocks is None:
        assert off % tm == 0
        mo = off // tm
        out_shape = jax.ShapeDtypeStruct((M if total is None else total, N), out_dtype)
        out_spec = pl.BlockSpec((tm, tn), lambda i, j, k: (i + mo, j))
    else:
        nper = N // out_blocks
        assert nper % tn == 0
        jb = nper // tn
        out_shape = jax.ShapeDtypeStruct((out_blocks if total is None else total, M, nper), out_dtype)
        out_spec = pl.BlockSpec((None, tm, tn), lambda i, j, k: (j // jb + off, i, j % jb))
    if earlier is not None:
        assert earlier.shape == out_shape.shape and earlier.dtype == out_shape.dtype
    ne = len(tiled) + len(mrows) + len(ncols)
    dot = {"nn": _nn, "nt": _nt, "tn": _tn}[mode]

    def body(a_ref, b_ref, *rest):
        extras, o_ref = rest[:ne], rest[ne]

        def finish(acc):
            if epilogue is not None:
                acc = epilogue(acc, *[e[...] for e in extras])
            o_ref[...] = acc.astype(o_ref.dtype)

        if nk == 1:
            finish(dot(a_ref[...], b_ref[...]))
        else:
            acc_ref = rest[ne + 1]
            k = pl.program_id(2)

            @pl.when(k == 0)
            def _():
                acc_ref[...] = jnp.zeros_like(acc_ref)

            acc_ref[...] += dot(a_ref[...], b_ref[...])

            @pl.when(k == nk - 1)
            def _():
                finish(acc_ref[...])

    args = [a, b, *tiled, *mrows, *ncols]
    aliases = {}
    if earlier is not None:
        specs.append(pl.BlockSpec(memory_space=pl.ANY))
        aliases = {len(args): 0}
        args.append(earlier)

    def body_with_dest(*refs):
        body(*refs[:2 + ne], *refs[2 + ne + (earlier is not None):])

    return pl.pallas_call(
        body_with_dest, name=name, out_shape=out_shape, grid=(M // tm, N // tn, nk),
        in_specs=specs, out_specs=out_spec, input_output_aliases=aliases,
        scratch_shapes=[] if nk == 1 else [pltpu.VMEM((tm, tn), F32)],
        compiler_params=_params(("parallel", "parallel", "arbitrary"), VMEM_BIG),
    )(*args)


def _matmul_rows(name, a, b, mode, tm, tk, fn, rows, consts, outs, accs, *, n_out=None, b_noff=0, b_koff=0):
    rl = [(t, t.shape[1], 0) if not isinstance(t, tuple) else t for t in rows]
    make_a = a if callable(a) else None
    M, K = (rl[0][0].shape[0], b.shape[1 if mode == "nt" else 0]) if make_a else a.shape
    N = n_out if n_out is not None else (b.shape[0] if mode == "nt" else b.shape[1])
    tm, tk = min(tm, M), min(tk, K)
    assert M % tm == 0 and K % tk == 0 and b_koff % tk == 0 and b_noff % N == 0, (name, M, N, K)
    no, ko, nk = b_noff // N, b_koff // tk, K // tk
    assert make_a is None or nk == 1
    nr, nc, no_, na = len(rl), len(consts), len(outs), len(accs)
    dot = _nt if mode == "nt" else _nn

    def body(*refs):
        a_ref, b_ref, rest = (None, refs[0], refs[1:]) if make_a else (refs[0], refs[1], refs[2:])
        r_refs, c_refs = rest[:nr], rest[nr:nr + nc]
        o_refs, acc_refs = rest[nr + nc:nr + nc + no_], rest[nr + nc + no_:nr + nc + no_ + na]
        i, k = pl.program_id(0), pl.program_id(1)

        def finish(prod, *made):
            res_o, res_a = fn(prod, *made, *[r[...] for r in r_refs], *[c[...] for c in c_refs])
            for r, v in zip(o_refs, res_o, strict=True):
                r[...] = v.astype(r.dtype)
            if acc_refs:
                @pl.when(i == 0)
                def _():
                    for r in acc_refs:
                        r[...] = jnp.zeros_like(r)

                for r, v in zip(acc_refs, res_a, strict=True):
                    r[...] += v

        if make_a:
            left = make_a(*[r[...] for r in r_refs], *[c[...] for c in c_refs])
            finish(dot(left, b_ref[...]), left)
        elif nk == 1:
            finish(dot(a_ref[...], b_ref[...]))
        else:
            prod_ref = rest[-1]

            @pl.when(k == 0)
            def _():
                prod_ref[...] = jnp.zeros_like(prod_ref)

            prod_ref[...] += dot(a_ref[...], b_ref[...])

            @pl.when(k == nk - 1)
            def _():
                finish(prod_ref[...])

    b_spec = _b_spec(b, mode, N, tk, no, ko, lambda i, k: (0, k))
    in_specs = ([] if make_a else [pl.BlockSpec((tm, tk), lambda i, k: (i, k))]) + [b_spec]
    in_specs += [pl.BlockSpec((tm, w), functools.partial(lambda i, k, cb: (i, cb), cb=cb)) for (_, w, cb) in rl]
    in_specs += [pl.BlockSpec(c.shape, lambda i, k: (0, 0)) for c in consts]
    out_specs = [pl.BlockSpec((tm, c), lambda i, k: (i, 0)) for (c, _) in outs]
    out_specs += [pl.BlockSpec(shp, lambda i, k: (0, 0)) for shp in accs]
    out_shape = [jax.ShapeDtypeStruct((M, c), dt) for (c, dt) in outs] + [jax.ShapeDtypeStruct(shp, F32) for shp in accs]
    res = pl.pallas_call(
        body, name=name, out_shape=out_shape, grid=(M // tm, nk), in_specs=in_specs, out_specs=out_specs,
        scratch_shapes=[] if nk == 1 else [pltpu.VMEM((tm, N), F32)],
        compiler_params=_params(("arbitrary" if accs else "parallel", "arbitrary"), VMEM_BIG),
    )(*([] if make_a else [a]), b, *[t[0] for t in rl], *consts)
    return res[:no_], res[no_:]


def _rowwise(name, fn, tiled, consts, outs, accs, ts):
    tl = [(t, t.shape[1], 0) if not isinstance(t, tuple) else t for t in tiled]
    s_len = tl[0][0].shape[0]
    assert s_len % ts == 0
    nt_, nc_, no_ = len(tl), len(consts), len(outs)

    def body(*refs):
        t_refs, c_refs = refs[:nt_], refs[nt_:nt_ + nc_]
        o_refs, a_refs = refs[nt_ + nc_:nt_ + nc_ + no_], refs[nt_ + nc_ + no_:]
        res_o, res_a = fn(*[r[...] for r in t_refs], *[r[...] for r in c_refs])
        for r, v in zip(o_refs, res_o, strict=True):
            r[...] = v.astype(r.dtype)
        if a_refs:
            @pl.when(pl.program_id(0) == 0)
            def _():
                for r in a_refs:
                    r[...] = jnp.zeros_like(r)

            for r, v in zip(a_refs, res_a, strict=True):
                r[...] += v

    in_specs = [pl.BlockSpec((ts, w), functools.partial(lambda i, cb: (i, cb), cb=cb)) for (_, w, cb) in tl]
    in_specs += [pl.BlockSpec(c.shape, lambda i: (0, 0)) for c in consts]
    out_specs = [pl.BlockSpec((ts, c), lambda i: (i, 0)) for (c, _) in outs]
    out_specs += [pl.BlockSpec(shp, lambda i: (0, 0)) for shp in accs]
    out_shape = [jax.ShapeDtypeStruct((s_len, c), dt) for (c, dt) in outs]
    out_shape += [jax.ShapeDtypeStruct(shp, F32) for shp in accs]
    res = pl.pallas_call(
        body, name=name, out_shape=out_shape, grid=(s_len // ts,), in_specs=in_specs, out_specs=out_specs,
        compiler_params=_params(("arbitrary",) if accs else ("parallel",), VMEM_BIG),
    )(*[t[0] for t in tl], *consts)
    return res[:no_], res[no_:]


def _norm_mod_fn(x, nw, sc, sh):
    r = lax.rsqrt(jnp.mean(x * x, axis=-1, keepdims=True) + NORM_EPS)
    return (x * r * nw) * (1.0 + sc) + sh


def _norm_mod_fwd(name, x, nw, sc, sh):
    (hn,), _ = _rowwise(name, lambda x, nw, sc, sh: ([_norm_mod_fn(x, nw, sc, sh)], []),
                        [x], [nw, sc, sh], [(x.shape[1], BF16)], [], 512)
    return hn


def _norm_mod_bwd(name, last, x, dhn_parts, dres, nw, sc, sh, prev=None):
    n = len(dhn_parts)
    d = x.shape[1]
    a, b, mode, tk, kw = last

    def fn(dhn, x, *rest):
        for p in rest[:n]:
            dhn = dhn + p
        dres, rest = rest[n], rest[n + 1:]
        y_prev, (nw, sc, sh), gate = (rest[0], rest[1:4], rest[4]) if prev is not None else (None, rest[0:3], None)
        r = lax.rsqrt(jnp.mean(x * x, axis=-1, keepdims=True) + NORM_EPS)
        xh = x * r
        dxh = dhn * (nw * (1.0 + sc))
        dx = r * (dxh - xh * jnp.mean(dxh * xh, axis=-1, keepdims=True)) + dres
        along = jnp.sum(dhn * xh, axis=0, keepdims=True)
        dnw, dsc, dsh = along * (1.0 + sc), along * nw, jnp.sum(dhn, axis=0, keepdims=True)
        if prev is None:
            return [dx], [dnw, dsc, dsh]
        return [dx, gate * dx], [dnw, dsc, dsh, jnp.sum(dx * y_prev, axis=0, keepdims=True)]

    rows = [x, *dhn_parts, dres] + ([prev[0]] if prev is not None else [])
    consts = [nw, sc, sh] + ([prev[1]] if prev is not None else [])
    outs = [(d, F32)] + ([(d, BF16)] if prev is not None else [])
    res_o, res_a = _matmul_rows(name, a, b, mode, 512, tk, fn, rows, consts, outs, [(1, d)] * (3 + (prev is not None)), **kw)
    return (*res_o, *res_a)


def _rope_tables(pos_col, inv_row):
    def fn(pos, inv):
        ang = pos.astype(F32) * inv
        e = lax.broadcasted_iota(jnp.int32, (1, 128), 1) % HEAD_DIM
        cos, sin = jnp.cos(ang), jnp.sin(ang)
        half = ROT_DIM // 2
        return [jnp.where(e < ROT_DIM, cos, 1.0), jnp.where(e < half, -sin, 0.0),
                jnp.where((e >= half) & (e < ROT_DIM), sin, 0.0)], []

    (c, sa, sb), _ = _rowwise("rope_tables", fn, [pos_col], [inv_row], [(128, F32)] * 3, [], 512)
    return c, sa, sb


def _rot_fwd(t, c, sa, sb):
    n = t.shape[1]
    rep = n // 128
    c, sa, sb = (jnp.tile(u, (1, rep)) for u in (c, sa, sb))
    return t * c + pltpu.roll(t, n - ROT_DIM // 2, 1) * sa + pltpu.roll(t, ROT_DIM // 2, 1) * sb


def _rot_bwd(g, c, sa, sb):
    n = g.shape[1]
    rep = n // 128
    c, sa, sb = (jnp.tile(u, (1, rep)) for u in (c, sa, sb))
    return g * c + pltpu.roll(g * sa, ROT_DIM // 2, 1) + pltpu.roll(g * sb, n - ROT_DIM // 2, 1)


ATT_TQ = 128


def _attn_tiles(l):
    tk = ATT_TQ + 2 * BAND
    return (l, l) if l <= tk else (ATT_TQ, tk)


def _attn_specs(g, s_len):
    def blk(off):
        return pl.BlockSpec((s_len, 128), functools.partial(lambda hp, off: (0, off + hp), off=off))

    return blk(4 * g), blk(12 + 4 * g), blk(4 * g), blk(0)


def _attn_tile_geometry(t, d, l):
    tq, tk = _attn_tiles(l)
    nts = l // tq
    r = t // nts
    ts = t % nts
    q0 = ts * tq
    ws = jnp.clip(q0 - BAND, 0, l - tk)
    kind = jnp.where(ts == 0, 0, jnp.where(ts == nts - 1, 2, 1))
    if d == 1:
        return pl.ds(pl.multiple_of(q0, tq), tq), pl.ds(pl.multiple_of(ws, BAND), tk), kind
    return pl.ds(r + d * q0, tq, stride=d), pl.ds(r + d * ws, tk, stride=d), kind


def _attn_fill_bias(bias_ref):
    _, tq2, tk = bias_ref.shape
    iq = lax.broadcasted_iota(jnp.int32, (tq2, 1), 0) % (tq2 // 2)
    ik = lax.broadcasted_iota(jnp.int32, (1, tk), 1)
    for i, off in enumerate((0, -BAND, -2 * BAND)):
        bias_ref[i] = jnp.where(jnp.abs(ik + off - iq) <= BAND, 0.0, NEG_BIG)


def _split_heads(t, in_h):
    zero = jnp.zeros_like(t)
    return jnp.concatenate([jnp.where(in_h[0], t, zero), jnp.where(in_h[1], t, zero)], axis=0)


def _attn_fwd(g, qk, v):
    s_len = qk.shape[0]
    d = DILATIONS[g]
    l = s_len // d
    tq, tk = _attn_tiles(l)
    assert l % tq == 0 and l >= tk
    q_spec, k_spec, v_spec, o_spec = _attn_specs(g, s_len)
    scale = 1.0 / math.sqrt(HEAD_DIM)

    def body(q_ref, k_ref, v_ref, o_ref, lse_ref, bias_ref):
        lane = lax.broadcasted_iota(jnp.int32, (1, 128), 1)
        in_h = [lane < HEAD_DIM, lane >= HEAD_DIM]
        _attn_fill_bias(bias_ref)

        def tile(t, carry):
            rows, win, kind = _attn_tile_geometry(t, d, l)
            q = (q_ref[rows, :] * scale).astype(BF16)
            k = k_ref[win, :].astype(BF16)
            vv = v_ref[win, :].astype(BF16)
            s = _nt(_split_heads(q, in_h), k) + bias_ref[kind]
            m = jnp.max(s, axis=1, keepdims=True)
            p = jnp.exp(s - m)
            den = jnp.sum(p, axis=1, keepdims=True)
            out = _nn(p, vv) / den
            lse = m + jnp.log(den)
            o_ref[rows, :] = jnp.where(in_h[0], out[:tq], out[tq:])
            lse_ref[rows, :] = jnp.where(in_h[0], lse[:tq], lse[tq:])
            return carry

        lax.fori_loop(0, s_len // tq, tile, 0, unroll=8 * ATT_TQ // tq)

    return pl.pallas_call(
        body, name=f"attn_fwd_g{g}", grid=(4,),
        out_shape=[jax.ShapeDtypeStruct((s_len, 512), F32)] * 2,
        in_specs=[q_spec, k_spec, v_spec], out_specs=[o_spec, o_spec],
        scratch_shapes=[pltpu.VMEM((3, 2 * tq, tk), F32)],
        compiler_params=_params(("parallel",), VMEM_BIG),
    )(qk, qk, v)


def _attn_bwd(g, qk, v, o, lse, do, dlse):
    s_len = qk.shape[0]
    d = DILATIONS[g]
    l = s_len // d
    tq, tk = _attn_tiles(l)
    q_spec, k_spec, v_spec, o_spec = _attn_specs(g, s_len)
    scale = 1.0 / math.sqrt(HEAD_DIM)

    def body(q_ref, k_ref, v_ref, o_ref, lse_ref, do_ref, dlse_ref, dq_ref, dk_ref, dv_ref, bias_ref):
        lane = lax.broadcasted_iota(jnp.int32, (1, 128), 1)
        in_h = [lane < HEAD_DIM, lane >= HEAD_DIM]
        dk_ref[...] = jnp.zeros_like(dk_ref)
        dv_ref[...] = jnp.zeros_like(dv_ref)
        _attn_fill_bias(bias_ref)

        def tile(t, carry):
            rows, win, kind = _attn_tile_geometry(t, d, l)
            k, vv = k_ref[win, :].astype(BF16), v_ref[win, :].astype(BF16)
            dout, lse_t, dlse_t = do_ref[rows, :], lse_ref[rows, :], dlse_ref[rows, :]
            od = dout * o_ref[rows, :]
            q2 = _split_heads((q_ref[rows, :] * scale).astype(BF16), in_h)
            do2 = _split_heads(dout.astype(BF16), in_h)
            head_col = lambda a: jnp.concatenate([a[:, 0:1], a[:, HEAD_DIM:HEAD_DIM + 1]], axis=0)
            delta = jnp.concatenate([jnp.sum(jnp.where(m, od, 0.0), axis=1, keepdims=True) for m in in_h], axis=0)
            p = jnp.exp(_nt(q2, k) + bias_ref[kind] - head_col(lse_t))
            ds = (p * (_nt(do2, vv) - delta + head_col(dlse_t))).astype(BF16)
            dq2 = _nn(ds, k) * scale
            dq_ref[rows, :] = jnp.where(in_h[0], dq2[:tq], dq2[tq:])
            dk_ref[win, :] += _tn(ds, q2)
            dv_ref[win, :] += _tn(p, do2)
            return carry

        lax.fori_loop(0, s_len // tq, tile, 0, unroll=8 * ATT_TQ // tq)

    return pl.pallas_call(
        body, name=f"attn_bwd_g{g}", grid=(4,),
        out_shape=[jax.ShapeDtypeStruct((s_len, 512), F32)] * 3,
        in_specs=[q_spec, k_spec, v_spec, o_spec, o_spec, o_spec, o_spec], out_specs=[o_spec] * 3,
        scratch_shapes=[pltpu.VMEM((3, 2 * tq, tk), F32)],
        compiler_params=_params(("parallel",), VMEM_BIG),
    )(qk, qk, v, o, lse, do, dlse)


def _mix_weights(ls):
    mx = jnp.maximum(jnp.maximum(ls[0], ls[1]), ls[2])
    es = [jnp.exp(x - mx) for x in ls]
    tot = es[0] + es[1] + es[2]
    return [e / tot for e in es]


def _attn_out(os_, lses, z, x, gate, w_out):
    s_len, dm = x.shape
    tm = 256
    wdt = 512
    z, z_block = z

    def body(o0, o1, o2, l0, l1, l2, z_ref, x_ref, g_ref, w_ref, a_ref, y_ref, x1_ref):
        alphas = _mix_weights([l0[...], l1[...], l2[...]])
        y = jnp.zeros((tm, dm), F32)
        for g, o_ref in enumerate((o0, o1, o2)):
            a_g = (o_ref[...] * alphas[g] * _silu(z_ref[:, g * wdt:(g + 1) * wdt])).astype(BF16)
            a_ref[:, g * wdt:(g + 1) * wdt] = a_g
            y = y + _nn(a_g, w_ref[g * wdt:(g + 1) * wdt, :])
        y_ref[...] = y
        x1_ref[...] = x_ref[...] + g_ref[...] * y

    row = lambda c: pl.BlockSpec((tm, c), lambda i: (i, 0))
    return pl.pallas_call(
        body, name="attn_out", grid=(s_len // tm,),
        out_shape=[jax.ShapeDtypeStruct((s_len, 3 * wdt), BF16), jax.ShapeDtypeStruct((s_len, dm), F32),
                   jax.ShapeDtypeStruct((s_len, dm), F32)],
        in_specs=[row(wdt)] * 6 + [pl.BlockSpec((tm, 3 * wdt), lambda i: (i, z_block)), row(dm),
                                   pl.BlockSpec((1, dm), lambda i: (0, 0)), pl.BlockSpec(w_out.shape, lambda i: (0, 0))],
        out_specs=[row(3 * wdt), row(dm), row(dm)],
        compiler_params=_params(("parallel",), VMEM_BIG),
    )(*os_, *lses, z, x, gate, w_out)


def _mix_bwd(dy, w_out, os_, lses, z):
    wdt = 512

    def fn(da, o0, o1, o2, l0, l1, l2, z):
        os_t, ls = [o0, o1, o2], [l0, l1, l2]
        alphas = _mix_weights(ls)
        hi = lax.broadcasted_iota(jnp.int32, (2 * wdt, wdt), 0) % wdt // HEAD_DIM
        hj = lax.broadcasted_iota(jnp.int32, (2 * wdt, wdt), 1) // HEAD_DIM
        seg = (hi == hj).astype(BF16)
        head_sum = lambda t: _dg(jnp.concatenate(_bf16_parts(t, 2), axis=1), seg, 1, 0)
        dos, dal, dzs = [], [], []
        for g in range(3):
            zg = z[:, g * wdt:(g + 1) * wdt]
            sig = jax.nn.sigmoid(zg)
            dag = da[:, g * wdt:(g + 1) * wdt]
            dmix = dag * zg * sig
            dzs.append(dag * os_t[g] * alphas[g] * (sig * (1.0 + zg * (1.0 - sig))))
            dos.append(dmix * alphas[g])
            dal.append(head_sum(dmix * os_t[g]))
        mean = alphas[0] * dal[0] + alphas[1] * dal[1] + alphas[2] * dal[2]
        dls = [alphas[g] * (dal[g] - mean) for g in range(3)]
        return dos + dls + [jnp.concatenate(dzs, axis=1)], []

    outs, _ = _matmul_rows("attn_out_dx_mix_bwd", dy, w_out, "nt", 256, dy.shape[1], fn, [*os_, *lses, (z[0], 3 * wdt, z[1])], [],
                           [(wdt, F32)] * 6 + [(3 * wdt, BF16)], [])
    return outs[:3], outs[3:6], outs[6]


def _rot_pack_bwd(dqs, dks, dvs, tabs):
    wdt = 512

    def fn(*args):
        grads, (c, sa, sb) = args[:9], args[9:]
        cols = [_rot_bwd(gq, c, sa, sb) for gq in grads[:6]] + list(grads[6:])
        return [jnp.concatenate(cols, axis=1)], []

    (out,), _ = _rowwise("rot_pack_bwd", fn, [*dqs, *dks, *dvs, *tabs], [], [(9 * wdt, BF16)], [], 512)
    return out


CONV_CB = 128
CONV_R = 256
CONV_PAD = 8


def _conv_taps(buf, base, off, sign):
    return [buf[pl.ds(base + off + sign * j, CONV_R), :] for j in range(CONV_WIDTH)]


def _conv_tap_sum(taps, w):
    acc = None
    for j, t in enumerate(taps):
        term = t * w[j:j + 1, :]
        acc = term if acc is None else acc + term
    return acc


def _conv_fwd(xpre, cw, cb):
    s_len, ch = xpre.shape
    nchunk = s_len // CONV_R

    def body(x_ref, w_ref, b_ref, o_ref, xp):
        zero = jnp.zeros((CONV_PAD, CONV_CB), F32)
        xp[0:CONV_PAD, :] = zero
        xp[s_len + CONV_PAD:s_len + 2 * CONV_PAD, :] = zero

        def fill(ci, carry):
            base = pl.multiple_of(ci * CONV_R, CONV_R)
            xp[pl.ds(base + CONV_PAD, CONV_R), :] = x_ref[pl.ds(base, CONV_R), :]
            return carry

        lax.fori_loop(0, nchunk, fill, 0)
        w = w_ref[...]
        b = b_ref[...]

        def chunk(ci, carry):
            base = pl.multiple_of(ci * CONV_R, CONV_R)
            u = _conv_tap_sum(_conv_taps(xp, base, CONV_PAD - CONV_WIDTH // 2, 1), w) + b
            o_ref[pl.ds(base, CONV_R), :] = _silu(u)
            return carry

        lax.fori_loop(0, nchunk, chunk, 0, unroll=2)

    col = lambda r: pl.BlockSpec((r, CONV_CB), lambda j: (0, j))
    return pl.pallas_call(
        body, name="conv_fwd", grid=(ch // CONV_CB,), out_shape=jax.ShapeDtypeStruct((s_len, ch), F32),
        in_specs=[col(s_len), col(CONV_WIDTH), col(1)], out_specs=col(s_len),
        scratch_shapes=[pltpu.VMEM((s_len + 2 * CONV_PAD, CONV_CB), F32)],
        compiler_params=_params(("parallel",), VMEM_BIG),
    )(xpre, cw, cb)


def _conv_bwd(xpre, da, cw, cb):
    s_len, ch = xpre.shape
    nchunk = s_len // CONV_R
    half = CONV_WIDTH // 2

    def body(x_ref, da_ref, w_ref, b_ref, dx_ref, gw_ref, gb_ref, xp, dcp):
        zero = jnp.zeros((CONV_PAD, CONV_CB), F32)
        for buf in (xp, dcp):
            buf[0:CONV_PAD, :] = zero
            buf[s_len + CONV_PAD:s_len + 2 * CONV_PAD, :] = zero

        def fill(ci, carry):
            base = pl.multiple_of(ci * CONV_R, CONV_R)
            xp[pl.ds(base + CONV_PAD, CONV_R), :] = x_ref[pl.ds(base, CONV_R), :]
            return carry

        lax.fori_loop(0, nchunk, fill, 0)
        w = w_ref[...]
        b = b_ref[...]

        def first(ci, carry):
            base = pl.multiple_of(ci * CONV_R, CONV_R)
            taps = _conv_taps(xp, base, CONV_PAD - half, 1)
            u = _conv_tap_sum(taps, w) + b
            sig = jax.nn.sigmoid(u)
            dc = da_ref[pl.ds(base, CONV_R), :] * (sig * (1.0 + u * (1.0 - sig)))
            dcp[pl.ds(base + CONV_PAD, CONV_R), :] = dc
            gb = carry[0] + jnp.sum(dc, axis=0, keepdims=True)
            gws = [carry[1 + j] + jnp.sum(dc * taps[j], axis=0, keepdims=True) for j in range(CONV_WIDTH)]
            return (gb, *gws)

        z1 = jnp.zeros((1, CONV_CB), F32)
        sums = lax.fori_loop(0, nchunk, first, (z1,) * (1 + CONV_WIDTH), unroll=2)
        gb_ref[...] = sums[0]
        for j in range(CONV_WIDTH):
            gw_ref[j:j + 1, :] = sums[1 + j]

        def second(ci, carry):
            base = pl.multiple_of(ci * CONV_R, CONV_R)
            dx_ref[pl.ds(base, CONV_R), :] = _conv_tap_sum(_conv_taps(dcp, base, CONV_PAD + half, -1), w).astype(dx_ref.dtype)
            return carry

        lax.fori_loop(0, nchunk, second, 0, unroll=2)

    col = lambda r: pl.BlockSpec((r, CONV_CB), lambda j: (0, j))
    return pl.pallas_call(
        body, name="conv_bwd", grid=(ch // CONV_CB,),
        out_shape=[jax.ShapeDtypeStruct((s_len, ch), BF16), jax.ShapeDtypeStruct((CONV_WIDTH, ch), F32),
                   jax.ShapeDtypeStruct((1, ch), F32)],
        in_specs=[col(s_len), col(s_len), col(CONV_WIDTH), col(1)],
        out_specs=[col(s_len), col(CONV_WIDTH), col(1)],
        scratch_shapes=[pltpu.VMEM((s_len + 2 * CONV_PAD, CONV_CB), F32)] * 2,
        compiler_params=_params(("parallel",), VMEM_BIG),
    )(xpre, da, cw, cb)


SSD_GW = 256
SSD_N = 128
SSD_DTW = 128


def _bf16_parts(x, n):
    parts, rest = [], x
    for _ in range(n):
        p = rest.astype(BF16)
        parts.append(p)
        rest = rest - p.astype(F32)
    return parts


@jax.custom_vjp
def _expand(x, e):
    eb = e.astype(BF16)
    return _dg(jnp.concatenate(_bf16_parts(x, 2), axis=1), jnp.concatenate([eb, eb], axis=0), 1, 0)


def _expand_fwd(x, e):
    return _expand(x, e), e


def _expand_bwd(e, g):
    return _dg(g.astype(BF16), e.astype(BF16), 1, 1), jnp.zeros_like(e)


_expand.defvjp(_expand_fwd, _expand_bwd)


@jax.custom_vjp
def _running_sum(tri, x):
    tb = tri.astype(BF16)
    return sum(_dg(tb, p, 1, 0) for p in _bf16_parts(x, 3))


def _running_sum_fwd(tri, x):
    return _running_sum(tri, x), tri


def _running_sum_bwd(tri, g):
    tb = tri.astype(BF16)
    return jnp.zeros_like(tri), sum(_dg(tb, p, 0, 0) for p in _bf16_parts(g, 3))


_running_sum.defvjp(_running_sum_fwd, _running_sum_bwd)


def _pick_col(a, h):
    @jax.custom_vjp
    def pick(a):
        return a[:, h:h + 1]

    pick.defvjp(lambda a: (a[:, h:h + 1], None),
                lambda _, g: (g * (lax.broadcasted_iota(jnp.int32, (1, a.shape[1]), 1) == h).astype(F32),))
    return pick(a)


def _pick_row(a, h):
    @jax.custom_vjp
    def pick(a):
        return a[h:h + 1, :]

    pick.defvjp(lambda a: (a[h:h + 1, :], None),
                lambda _, g: (g * (lax.broadcasted_iota(jnp.int32, (a.shape[0], 1), 0) == h).astype(F32),))
    return pick(a)


def _ssd_mask(dirn):
    ri = lax.broadcasted_iota(jnp.int32, (CHUNK, CHUNK), 0)
    cj = lax.broadcasted_iota(jnp.int32, (CHUNK, CHUNK), 1)
    return (cj <= ri) if dirn == 0 else (cj >= ri)


def _ssd_rowsel(dirn):
    last = CHUNK - 1 if dirn == 0 else 0
    return (lax.broadcasted_iota(jnp.int32, (CHUNK, 1), 0) == last).astype(F32)


def _ssd_chunk_pre(dirn):
    nh = SSD_DTW

    def f(dt, alog):
        da = dt * (-jnp.exp(alog))
        cum = _running_sum(_ssd_mask(dirn).astype(F32), da)
        tot = jnp.sum(cum * _ssd_rowsel(dirn), axis=0, keepdims=True)
        hh = lax.broadcasted_iota(jnp.int32, (nh, SSD_HEADS * HEAD_DIM), 0)
        jj = lax.broadcasted_iota(jnp.int32, (nh, SSD_HEADS * HEAD_DIM), 1)
        expand = (hh == dirn * SSD_HEADS + jj // HEAD_DIM).astype(F32)
        return cum, cum.T, _expand(dt, expand), _expand(jnp.exp(tot - cum), expand), _expand(jnp.exp(cum), expand)

    return f


def _ssd_group_fn(g, dirn, stacked):
    def f(xs, bm, cm, st, cum, cum_t, dt_e, w_e, ce_e):
        mask = _ssd_mask(dirn)
        xdt = xs * dt_e
        cd_e = jnp.sum(ce_e * _ssd_rowsel(dirn), axis=0, keepdims=True)
        cb = _bnt(cm, bm)
        lane_head = lax.broadcasted_iota(jnp.int32, (1, SSD_GW), 1) // HEAD_DIM
        y = _bnn(cm, st) * ce_e
        decayed, inputs = [], []
        for j in range(4):
            hidx = dirn * SSD_HEADS + 4 * g + j
            col, row = _pick_col(cum, hidx), _pick_row(cum_t, hidx)
            dec = cb * jnp.exp(jnp.where(mask, col - row, NEG_BIG))
            head = (lane_head == j).astype(F32)
            if stacked:
                decayed.append(dec)
                inputs.append(xdt * head)
            else:
                y = y + _bnn(dec, xdt) * head
        if stacked:
            y = y + _bnn(jnp.concatenate(decayed, axis=1), jnp.concatenate(inputs, axis=0))
        st_out = st * cd_e + _btn(bm, xdt * w_e)
        return y, st_out

    return f


def _ssd_in_specs(kk):
    ln = CHUNK
    return [pl.BlockSpec((ln, 2048), lambda i: (kk(i), 0)),
            pl.BlockSpec((ln, 1024), lambda i: (kk(i), 2)),
            pl.BlockSpec((ln, 1024), lambda i: (kk(i), 3)),
            pl.BlockSpec((ln, SSD_DTW), lambda i: (kk(i), 0)),
            pl.BlockSpec((1, SSD_DTW), lambda i: (0, 0))]


def _ssd_fwd(xbc, dt, alog, dirn, prior=None):
    s_len = xbc.shape[0]
    nc = s_len // CHUNK
    kk = (lambda i: i) if dirn == 0 else (lambda i: nc - 1 - i)

    def body(x_ref, b_ref, c_ref, dt_ref, al_ref, *rest):
        prior_ref = rest[0] if prior is not None else None
        y_ref, sts_ref, st = rest[prior is not None:]

        @pl.when(pl.program_id(0) == 0)
        def _():
            st[...] = jnp.zeros_like(st)

        sts_ref[0] = st[...]
        cum, cum_t, dt_e, w_e, ce_e = _ssd_chunk_pre(dirn)(dt_ref[...], al_ref[...])
        for g in range(SSD_GROUPS):
            xc = slice(g * SSD_GW, (g + 1) * SSD_GW)
            gc = slice(g * SSD_N, (g + 1) * SSD_N)
            y, st_new = _ssd_group_fn(g, dirn, True)(x_ref[:, xc], b_ref[:, gc], c_ref[:, gc], st[:, xc], cum, cum_t,
                                               dt_e[:, xc], w_e[:, xc], ce_e[:, xc])
            y_ref[:, xc] = y if prior is None else y + prior_ref[:, xc]
            st[:, xc] = st_new

    return pl.pallas_call(
        body, name=f"ssd_fwd_d{dirn}", grid=(nc,),
        out_shape=[jax.ShapeDtypeStruct((s_len, 2048), F32), jax.ShapeDtypeStruct((nc, SSD_N, 2048), F32)],
        in_specs=_ssd_in_specs(kk) + ([pl.BlockSpec((CHUNK, 2048), lambda i: (kk(i), 0))] if prior is not None else []),
        out_specs=[pl.BlockSpec((CHUNK, 2048), lambda i: (kk(i), 0)),
                   pl.BlockSpec((1, SSD_N, 2048), lambda i: (kk(i), 0, 0))],
        scratch_shapes=[pltpu.VMEM((SSD_N, 2048), F32)],
        compiler_params=_params(("arbitrary",), VMEM_BIG),
    )(xbc, xbc, xbc, dt, alog, *([prior] if prior is not None else []))


def _ssd_bwd(xbc, dt, alog, states, dy, d_e, dirn, prior=None):
    s_len = xbc.shape[0]
    nc = s_len // CHUNK
    kk = (lambda i: nc - 1 - i) if dirn == 0 else (lambda i: i)

    def body(x_ref, b_ref, c_ref, dt_ref, al_ref, sts_ref, dy_ref, de_ref, *rest):
        prior_ref = rest[0] if prior is not None else None
        dx_ref, ddt_ref, dal_ref, dst = rest[prior is not None:]
        plus_prior = (lambda v, cols: v + prior_ref[:, cols]) if prior is not None else (lambda v, cols: v)

        @pl.when(pl.program_id(0) == 0)
        def _():
            dst[...] = jnp.zeros_like(dst)
            dal_ref[...] = jnp.zeros_like(dal_ref)

        (cum, cum_t, dt_e, w_e, ce_e), pre_vjp = jax.vjp(_ssd_chunk_pre(dirn), dt_ref[...], al_ref[...])
        dcum = jnp.zeros_like(cum)
        dcum_t = jnp.zeros_like(cum_t)
        d_dt_e, d_w_e, d_ce_e = [], [], []
        for g in range(SSD_GROUPS):
            xc = slice(g * SSD_GW, (g + 1) * SSD_GW)
            gc = slice(g * SSD_N, (g + 1) * SSD_N)
            _, vjp = jax.vjp(_ssd_group_fn(g, dirn, False), x_ref[:, xc], b_ref[:, gc], c_ref[:, gc], sts_ref[0, :, xc], cum, cum_t,
                             dt_e[:, xc], w_e[:, xc], ce_e[:, xc])
            dyg = dy_ref[:, xc]
            dxs, dbm, dcm, dst_g, dcum_g, dcum_t_g, ddte_g, dwe_g, dcee_g = vjp((dyg, dst[:, xc]))
            if dirn == 0:
                dxs = dxs + dyg * de_ref[:, xc]
            bc, cc = slice(2048 + g * SSD_N, 2048 + (g + 1) * SSD_N), slice(3072 + g * SSD_N, 3072 + (g + 1) * SSD_N)
            dx_ref[:, xc] = plus_prior(dxs, xc)
            dx_ref[:, bc] = plus_prior(dbm, bc)
            dx_ref[:, cc] = plus_prior(dcm, cc)
            dst[:, xc] = dst_g
            dcum = dcum + dcum_g
            dcum_t = dcum_t + dcum_t_g
            d_dt_e.append(ddte_g)
            d_w_e.append(dwe_g)
            d_ce_e.append(dcee_g)
        ddt, dal = pre_vjp((dcum, dcum_t, jnp.concatenate(d_dt_e, axis=1), jnp.concatenate(d_w_e, axis=1),
                            jnp.concatenate(d_ce_e, axis=1)))
        ddt_ref[...] = ddt
        dal_ref[...] += dal

    return pl.pallas_call(
        body, name=f"ssd_bwd_d{dirn}", grid=(nc,),
        out_shape=[jax.ShapeDtypeStruct((s_len, 4096), F32), jax.ShapeDtypeStruct((s_len, SSD_DTW), F32),
                   jax.ShapeDtypeStruct((1, SSD_DTW), F32)],
        in_specs=_ssd_in_specs(kk) + [pl.BlockSpec((1, SSD_N, 2048), lambda i: (kk(i), 0, 0)),
                                      pl.BlockSpec((CHUNK, 2048), lambda i: (kk(i), 0)),
                                      pl.BlockSpec((1, 2048), lambda i: (0, 0))]
        + ([pl.BlockSpec((CHUNK, 4096), lambda i: (kk(i), 0))] if prior is not None else []),
        out_specs=[pl.BlockSpec((CHUNK, 4096), lambda i: (kk(i), 0)),
                   pl.BlockSpec((CHUNK, SSD_DTW), lambda i: (kk(i), 0)),
                   pl.BlockSpec((1, SSD_DTW), lambda i: (0, 0))],
        scratch_shapes=[pltpu.VMEM((SSD_N, 2048), F32)],
        compiler_params=_params(("arbitrary",), VMEM_BIG),
    )(xbc, xbc, xbc, dt, alog, states, dy, d_e, *([prior] if prior is not None else []))


def _gate_norm_fn(y, xs, z, d_e, nw):
    yg = (y + xs * d_e) * _silu(z)
    return yg * lax.rsqrt(jnp.mean(yg * yg, axis=-1, keepdims=True) + NORM_EPS) * nw


def _gate_norm_bwd(dy, w_out, y, xbc, z, d_e, nw):
    def fn(du, y, xs, z, d_e, nw):
        sig = jax.nn.sigmoid(z)
        gate = z * sig
        ysum = y + xs * d_e
        yg = ysum * gate
        r = lax.rsqrt(jnp.mean(yg * yg, axis=-1, keepdims=True) + NORM_EPS)
        t = du * nw
        dyg = t * r - yg * (jnp.mean(t * yg, axis=-1, keepdims=True) * (r * r * r))
        dys = dyg * gate
        dz = dyg * ysum * (sig * (1.0 + z * (1.0 - sig)))
        dnw = jnp.sum(du * yg * r, axis=0, keepdims=True)
        dde = jnp.sum(dys * xs, axis=0, keepdims=True)
        hh = lax.broadcasted_iota(jnp.int32, (2048, SSD_HEADS), 0) // HEAD_DIM
        jj = lax.broadcasted_iota(jnp.int32, (2048, SSD_HEADS), 1)
        return [dys, dz], [dnw, _hnn(jnp.broadcast_to(dde, (8, 2048)), (hh == jj).astype(F32))[0:1]]

    (dys, dz), (g_nw, g_d) = _matmul_rows("ssd_out_dx_gate_norm_bwd", dy, w_out, "nt", 256, dy.shape[1], fn,
                                          [y, (xbc, 2048, 0), z], [d_e, nw], [(2048, F32), (2048, BF16)],
                                          [(1, 2048), (1, SSD_HEADS)])
    return dys, dz, g_nw, g_d


def _ssd_tail_loss(y, xbc, z, d_e, snw, w_out, x1, tgt, gate, fnw):
    dm = x1.shape[1]
    si = y.shape[1]

    def make_u(y, xs, z, x1, tgt, d_e, snw, gate, fnw):
        return _gate_norm_fn(y, xs, z, d_e, snw).astype(BF16)

    def fn(y1, u, y, xs, z, x1, tgt, d_e, snw, gate, fnw):
        x2 = x1 + gate * y1
        r = lax.rsqrt(jnp.mean(x2 * x2, axis=-1, keepdims=True) + NORM_EPS)
        xh = x2 * r
        err = xh * fnw - tgt
        loss = 0.5 * jnp.sum(jnp.mean(err * err, axis=-1, keepdims=True), axis=0, keepdims=True)
        dy = err * (1.0 / dm)
        dxh = dy * fnw
        dx2 = r * (dxh - xh * jnp.mean(dxh * xh, axis=-1, keepdims=True))
        dfnw = jnp.sum(dy * xh, axis=0, keepdims=True)
        return [u, dx2, gate * dx2], [dfnw, jnp.sum(dx2 * y1, axis=0, keepdims=True), jnp.broadcast_to(loss, (1, 128))]

    (u, dx2, dy1), (g_fnw, dgate, loss) = _matmul_rows(
        "ssd_out_loss", make_u, w_out, "nn", 256, si, fn, [y, (xbc, si, 0), z, x1, tgt], [d_e, snw, gate, fnw],
        [(si, BF16), (dm, F32), (dm, BF16)], [(1, dm), (1, dm), (1, 128)])
    return u, dx2, dy1, g_fnw, dgate, loss


def _softplus_fwd(dt_raw, bias):
    (dt,), _ = _rowwise("dt_softplus", lambda r, b: ([jax.nn.softplus(r + b)], []), [dt_raw], [bias],
                        [(dt_raw.shape[1], F32)], [], 512)
    return dt


def _softplus_bwd(ddt_f, ddt_b, dt_raw, bias):
    def fn(df, db, r, b):
        g = (df + db) * jax.nn.sigmoid(r + b)
        return [g], [jnp.sum(g, axis=0, keepdims=True)]

    w = dt_raw.shape[1]
    (g,), (gb,) = _rowwise("dt_softplus_bwd", fn, [ddt_f, ddt_b, dt_raw], [bias], [(w, BF16)], [(1, w)], 512)
    return g, gb


def _mod_part(c_all, mod_w):
    nl, _, ncol = mod_w.shape
    nb = c_all.shape[0]

    def body(c_ref, w_ref, o_ref):
        cond = _silu(c_ref[...])
        for i in range(nl):
            o_ref[i * nb:(i + 1) * nb, :] = _nn(cond, w_ref[i])

    return pl.pallas_call(body, name="mod_part", out_shape=jax.ShapeDtypeStruct((nl * nb, ncol), F32),
                          compiler_params=_params(None, VMEM_BIG))(c_all, mod_w)


def _mod_finish(mod_nb, mod_b, norm_w, tokens):
    nl, dm = norm_w.shape

    def body(a_ref, b_ref, nw_ref, *rest):
        tok_refs, o_refs = rest[:len(tokens)], rest[len(tokens):]
        tok = sum(t[0:1, 0:1] for t in tok_refs)
        for i in range(nl):
            for k in range(3):
                cols = slice(k * dm, (k + 1) * dm)
                o_refs[4 * i + k][...] = a_ref[i:i + 1, cols] + b_ref[i:i + 1, cols]
            o_refs[4 * i + 3][...] = nw_ref[i:i + 1, :] + tok

    rows = pl.pallas_call(body, name="mod_finish", out_shape=[jax.ShapeDtypeStruct((1, dm), F32)] * (4 * nl))(
        mod_nb, mod_b, norm_w, *tokens)
    return [rows[4 * i:4 * i + 4] for i in range(nl)]


def _mod_grad(c_all, dmod_sh):
    nl, nb, ncol = dmod_sh.shape
    dm = c_all.shape[1]

    def body(c_ref, d_ref, o_ref):
        cond = _silu(c_ref[...])
        for i in range(nl):
            o_ref[i] = _tn(cond, d_ref[i])

    return pl.pallas_call(body, name="mod_grad", out_shape=jax.ShapeDtypeStruct((nl, dm, ncol), F32),
                          compiler_params=_params(None, VMEM_BIG))(c_all, dmod_sh)


PACK_ROWS = 16
PACK_COLS = 1024


def _pack_small(rows, b64, a64s, d32, extra):
    nr, na = len(rows), len(a64s)

    def body(*refs):
        o_ref = refs[-1]
        o_ref[...] = jnp.zeros_like(o_ref)
        for i in range(nr):
            o_ref[i:i + 1, :] = refs[i][...]
        b_ref, a_refs, d_ref, e_ref = refs[nr], refs[nr + 1:nr + 1 + na], refs[nr + 1 + na], refs[nr + 2 + na]
        o_ref[nr:nr + 1, 0:64] = b_ref[:, 0:64]
        o_ref[nr:nr + 1, 64:128] = sum(a[:, 0:64] for a in a_refs)
        o_ref[nr:nr + 1, 128:160] = d_ref[...]
        o_ref[nr:nr + 1, 256:384] = e_ref[...]

    return pl.pallas_call(body, name="pack_small", out_shape=jax.ShapeDtypeStruct((PACK_ROWS, PACK_COLS), F32))(
        *rows, b64, *a64s, d32, extra)


def _pack_ssd_small(cw, cb, nw):
    def body(cw_ref, cb_ref, nw_ref, o_ref):
        o_ref[...] = jnp.zeros_like(o_ref)
        o_ref[0:5, :] = cw_ref[...]
        o_ref[5:6, :] = cb_ref[...]
        o_ref[6:7, 0:256] = nw_ref[...]

    return pl.pallas_call(body, name="pack_ssd_small", out_shape=jax.ShapeDtypeStruct((8, 512), F32))(cw, cb, nw)


def _sum_parts(p_ref):
    g = p_ref[0].astype(F32)
    for s in range(1, p_ref.shape[0]):
        g = g + p_ref[s].astype(F32)
    return g


def _adam_update(w, g, m, v):
    m2 = ADAM_B1 * m + (1.0 - ADAM_B1) * g
    v2 = ADAM_B2 * v + (1.0 - ADAM_B2) * (g * g)
    m_hat = m2 / (1.0 - ADAM_B1 ** ADAM_STEP)
    v_hat = v2 / (1.0 - ADAM_B2 ** ADAM_STEP)
    return -ADAM_LR * (m_hat / (jnp.sqrt(v_hat) + ADAM_EPS) + ADAM_WD * w), m2, v2


def _adamw_windows(name, parts, params, windows, extra=None):
    n = len(params)

    def body(p_ref, *rest):
        ins, outs = rest[:3 * n], rest[3 * n:]
        g = _sum_parts(p_ref)
        for pi, rows, cols, idx in windows:
            w_ref, m_ref, v_ref = ins[3 * pi:3 * pi + 3]
            gw = g[rows, cols]
            dw, m2, v2 = _adam_update(w_ref[idx], gw, m_ref[idx], v_ref[idx])
            for o_ref, val in zip(outs[4 * pi:4 * pi + 4], (gw, dw, m2, v2), strict=True):
                o_ref[idx] = val
        if extra is not None:
            outs[4 * n][...] = g[extra[0], extra[1]]

    out_shape = [jax.ShapeDtypeStruct(w.shape, F32) for (w, _, _) in params for _ in range(4)]
    if extra is not None:
        out_shape.append(jax.ShapeDtypeStruct((extra[0].stop - extra[0].start, extra[1].stop - extra[1].start), F32))
    res = pl.pallas_call(body, name=name, out_shape=out_shape)(parts, *[a for p in params for a in p])
    return [res[4 * i:4 * i + 4] for i in range(n)] + ([res[4 * n]] if extra is not None else [])


def _adamw(name, w, parts, m, v, tr, tc=None):
    r_, c_ = w.shape
    p_ = parts.shape[0]
    tr = min(tr, r_)
    tc = c_ if tc is None else tc
    assert r_ % tr == 0 and c_ % tc == 0

    def body(w_ref, p_ref, m_ref, v_ref, g_ref, d_ref, m2_ref, v2_ref):
        g = _sum_parts(p_ref)
        g_ref[...] = g
        d_ref[...], m2_ref[...], v2_ref[...] = _adam_update(w_ref[...], g, m_ref[...], v_ref[...])

    blk = pl.BlockSpec((tr, tc), lambda i, j: (i, j))
    return pl.pallas_call(
        body, name=name, grid=(r_ // tr, c_ // tc), out_shape=[jax.ShapeDtypeStruct((r_, c_), F32)] * 4,
        in_specs=[blk, pl.BlockSpec((p_, tr, tc), lambda i, j: (0, i, j)), blk, blk], out_specs=[blk] * 4,
        compiler_params=_params(("parallel", "parallel"), VMEM_BIG),
    )(w, parts, m, v)


def _dev_index(p):
    return 4 * p[0] + 2 * p[1] + p[2]


def _all_gather(name, xs, by_columns=()):
    n = len(xs)
    hbm = pl.BlockSpec(memory_space=pl.ANY)

    def body(*refs):
        x_refs, o_refs = refs[:n], refs[n:2 * n]
        send_sems, recv_sems, local_sems = refs[2 * n:]
        x, y, c = lax.axis_index("x"), lax.axis_index("y"), lax.axis_index("c")
        me, sibling = (x, y, c), (x, y, 1 - c)
        chips = [(1 - x, y), (x, 1 - y), (1 - x, 1 - y)]

        def place(a, block):
            if a in by_columns:
                width = x_refs[a].shape[1]
                return o_refs[a].at[:, pl.ds(pl.multiple_of(_dev_index(block) * width, 128), width)]
            return o_refs[a].at[_dev_index(block)]

        def copy(a, k, block, to, src=None):
            dst = place(a, block)
            return pltpu.make_async_remote_copy(
                src_ref=dst if src is None else src, dst_ref=dst, send_sem=send_sems.at[a, k],
                recv_sem=recv_sems.at[a, k], device_id=to, device_id_type=MESH)

        mine = [pltpu.make_async_copy(x_refs[a], place(a, me), local_sems.at[a]) for a in range(n)]
        for cp in mine:
            cp.start()
        first = []
        for a in range(n):
            first.append(copy(a, 0, me, sibling, src=x_refs[a]))
            first += [copy(a, 1 + j, me, (*chip, c), src=x_refs[a]) for j, chip in enumerate(chips)]
        for cp in first:
            cp.start()
        passed = []
        for j, chip in enumerate(chips):
            for a in range(n):
                copy(a, 1 + j, (*chip, c), me).wait_recv()
                cp = copy(a, 4 + j, (*chip, c), sibling)
                cp.start()
                passed.append(cp)
        for a in range(n):
            copy(a, 0, sibling, me).wait_recv()
            for j, chip in enumerate(chips):
                copy(a, 4 + j, (*chip, 1 - c), me).wait_recv()
        for cp in first + passed:
            cp.wait_send()
        for cp in mine:
            cp.wait()

    shapes = [(x.shape[0], NDEV * x.shape[1]) if a in by_columns else (NDEV, *x.shape) for a, x in enumerate(xs)]
    return pl.pallas_call(
        body, name=name, out_shape=[jax.ShapeDtypeStruct(s, x.dtype) for s, x in zip(shapes, xs)],
        in_specs=[hbm] * n, out_specs=[hbm] * n,
        scratch_shapes=[pltpu.SemaphoreType.DMA((n, 7)), pltpu.SemaphoreType.DMA((n, 7)), pltpu.SemaphoreType.DMA((n,))],
    )(*xs)


_HBM = pl.BlockSpec(memory_space=pltpu.HBM)
_SEM = pl.BlockSpec(memory_space=pltpu.SEMAPHORE)
_EFFECT = pltpu.SideEffectType.DATAFLOW_SIDE_EFFECTING


def _mesh_position():
    return lax.axis_index("x"), lax.axis_index("y"), lax.axis_index("c")


def _peers(me):
    return [(k, tuple(1 - v if (k >> b) & 1 else v for v, b in zip(me, (2, 1, 0)))) for k in range(1, NDEV)]


EXCHANGE_COPIES = {"gather": NDEV - 1, "scatter": NDEV - 1, "pair": 4, "chips": 3}
NCHIP = NDEV // 2


def _landing_zones(name, xs, mode):
    x_, y_, c_ = _mesh_position()
    mine = (2 * x_ + y_ if mode == "chips" else _dev_index((x_, y_, c_))).astype(jnp.int32).reshape(1)
    lands = []
    for a, x in enumerate(xs):
        rows, cols = x.shape[-2:]
        if mode == "pair":
            lands.append(lax.empty((NCHIP, rows, cols), x.dtype))
            continue
        tr = 256 if rows % 256 == 0 else rows

        def body(me_ref, x_ref, o_ref):
            o_ref[...] = x_ref[...]

        if mode == "gather":
            in_spec = pl.BlockSpec((tr, cols), lambda i, me_ref: (i, 0))
        else:
            in_spec = pl.BlockSpec((None, tr, cols), lambda i, me_ref: (me_ref[0], i, 0))
        lands.append(pl.pallas_call(
            body, name=f"{name}_{a}",
            out_shape=jax.ShapeDtypeStruct((NCHIP if mode == "chips" else NDEV, rows, cols), x.dtype),
            grid_spec=pltpu.PrefetchScalarGridSpec(
                num_scalar_prefetch=1, grid=(rows // tr,), in_specs=[in_spec],
                out_specs=pl.BlockSpec((None, tr, cols), lambda i, me_ref: (me_ref[0], i, 0))),
            compiler_params=_params(("arbitrary",)),
        )(mine, x))
    return lands


def _exchange_copies(x_refs, land_refs, send_sems, recv_sems, mode):
    x_, y_, c_ = me = _mesh_position()
    per_array = EXCHANGE_COPIES[mode]
    out = []

    def add(a, k, src, dst, peer):
        sem = a * per_array + k
        out.append(pltpu.make_async_remote_copy(src_ref=src, dst_ref=dst, send_sem=send_sems.at[sem], recv_sem=recv_sems.at[sem],
                                                device_id=peer, device_id_type=MESH))

    for a, (x_ref, land_ref) in enumerate(zip(x_refs, land_refs)):
        if mode in ("gather", "scatter"):
            for k, peer in _peers(me):
                add(a, k - 1, x_ref.at[_dev_index(peer)] if mode == "scatter" else x_ref, land_ref.at[_dev_index(me)], peer)
        elif mode == "pair":
            for chip in range(NCHIP):
                add(a, chip, x_ref.at[2 * chip + 1 - c_], land_ref.at[chip], (x_, y_, 1 - c_))
        else:
            for k in range(1, NCHIP):
                px, py = (1 - x_ if k & 2 else x_), (1 - y_ if k & 1 else y_)
                add(a, k - 1, x_ref.at[2 * px + py], land_ref.at[2 * x_ + y_], (px, py, c_))
    return out


def _exchange_start(name, xs, lands, mode, dep):
    n = len(xs)

    def body(*refs):
        x_refs, land_refs = refs[:n], refs[n:2 * n]
        send_sems, recv_sems = refs[2 * n + 1], refs[2 * n + 2]
        token = refs[-1]
        for cp in _exchange_copies(x_refs, land_refs, send_sems, recv_sems, mode):
            cp.start()
        token[...] = jnp.zeros_like(token)

    sems = pltpu.SemaphoreType.DMA((n * EXCHANGE_COPIES[mode],))
    res = pl.pallas_call(
        body, name=name,
        out_shape=(sems, sems, *[pltpu.HBM(a.shape, a.dtype) for a in (*xs, *lands)], jax.ShapeDtypeStruct((8, 128), F32)),
        in_specs=[_HBM] * (2 * n) + [pl.BlockSpec(memory_space=pl.ANY)],
        out_specs=(_SEM, _SEM, *[_HBM] * (2 * n), pl.BlockSpec(memory_space=pltpu.VMEM)),
        input_output_aliases={i: 2 + i for i in range(2 * n)},
        compiler_params=pltpu.CompilerParams(has_side_effects=_EFFECT),
    )(*[pltpu.with_memory_space_constraint(a, pltpu.HBM) for a in (*xs, *lands)], dep)
    return res[:-1], res[-1]


def _exchange_wait(name, handles, mode, after):
    send_sems, recv_sems = handles[0], handles[1]
    bufs = handles[2:]
    n = len(bufs) // 2

    def body(*refs):
        x_refs, land_refs = refs[:n], refs[n:2 * n]
        s_sems, r_sems = refs[2 * n], refs[2 * n + 1]
        for cp in _exchange_copies(x_refs, land_refs, s_sems, r_sems, mode):
            cp.wait_send()
            cp.wait_recv()

    res = pl.pallas_call(
        body, name=name, out_shape=tuple(pltpu.HBM(a.shape, a.dtype) for a in bufs),
        in_specs=[_HBM] * (2 * n) + [_SEM, _SEM, pl.BlockSpec(memory_space=pl.ANY)], out_specs=tuple([_HBM] * (2 * n)),
        input_output_aliases={i: i for i in range(2 * n)},
        compiler_params=pltpu.CompilerParams(has_side_effects=_EFFECT),
    )(*bufs, send_sems, recv_sems, after)
    return res[n:]


def _pair_sum(name, x, from_sibling):
    _, rows, cols = x.shape
    tr = rows
    core = lax.axis_index("c").astype(jnp.int32).reshape(1)

    def body(c_ref, x_ref, s_ref, o_ref):
        o_ref[...] = (x_ref[...].astype(F32) + s_ref[...].astype(F32)).astype(o_ref.dtype)

    return pl.pallas_call(
        body, name=name, out_shape=jax.ShapeDtypeStruct((NCHIP, rows, cols), x.dtype),
        grid_spec=pltpu.PrefetchScalarGridSpec(
            num_scalar_prefetch=1, grid=(NCHIP, rows // tr),
            in_specs=[pl.BlockSpec((None, tr, cols), lambda j, i, c_ref: (2 * j + c_ref[0], i, 0)),
                      pl.BlockSpec((None, tr, cols), lambda j, i, c_ref: (j, i, 0))],
            out_specs=pl.BlockSpec((None, tr, cols), lambda j, i, c_ref: (j, i, 0))),
        compiler_params=_params(("parallel", "parallel")),
    )(core, x, from_sibling)


def kernel(x, c, positions, norm_w, mod_w, mod_b, attn_w_in, attn_w_out, ssd_w_in, ssd_conv_w, ssd_conv_b, ssd_dt_bias, ssd_a_log, ssd_d, ssd_norm_w, ssd_w_out, final_norm_w, loss_target, m_norm_w, m_mod_w, m_mod_b, m_attn_w_in, m_attn_w_out, m_ssd_w_in, m_ssd_conv_w, m_ssd_conv_b, m_ssd_dt_bias, m_ssd_a_log, m_ssd_d, m_ssd_norm_w, m_ssd_w_out, m_final_norm_w, v_norm_w, v_mod_w, v_mod_b, v_attn_w_in, v_attn_w_out, v_ssd_w_in, v_ssd_conv_w, v_ssd_conv_b, v_ssd_dt_bias, v_ssd_a_log, v_ssd_d, v_ssd_norm_w, v_ssd_w_out, v_final_norm_w):
    s_len, dm = x.shape[1], x.shape[2]
    me = 4 * lax.axis_index("x") + 2 * lax.axis_index("y") + lax.axis_index("c")
    x0 = x.reshape(s_len, dm)
    tgt = loss_target.reshape(s_len, dm)
    aw = 3 * 512
    si = 2 * dm
    sxbc = 2 * si
    n_ssd_in = ssd_w_in.shape[2] * NDEV

    w_ai, c_all = _all_gather("gather_attn_w_in", [attn_w_in[0].astype(BF16), c], by_columns=(0,))
    wcol = attn_w_in.shape[2]
    c_all = c_all.reshape(NDEV, dm)

    part = _mod_part(c_all, mod_w)
    (part_all,) = _all_gather("gather_mod", [part])
    mod_nb = jnp.stack([lax.dynamic_index_in_dim(part_all, i * NDEV + me, axis=1, keepdims=False).reshape(3 * dm)
                        for i in range(2)])

    ssd_small = _pack_ssd_small(ssd_conv_w[0], ssd_conv_b, ssd_norm_w)
    ao_shard = [attn_w_out[0].astype(BF16)]
    ao_handles, ao_token = _exchange_start("w_out_start", ao_shard, _landing_zones("w_out_place", ao_shard, "gather"), "gather",
                                           part_all)
    late_shards = [ssd_w_in[0].T.astype(BF16), ssd_w_out[0].astype(BF16), ssd_small]
    w_handles, w_token = _exchange_start("weights_start", late_shards, _landing_zones("weights_place", late_shards, "gather"),
                                         "gather", ao_token)
    (shift0, scale0, gate0, nw0), (shift1, scale1, gate1, nw1) = _mod_finish(mod_nb, mod_b, norm_w, [ao_token, w_token])
    shift, scale, gate, nw = [shift0, shift1], [scale0, scale1], [gate0, gate1], [nw0, nw1]

    hn0 = _norm_mod_fwd("norm0", x0, nw[0], scale[0], shift[0])
    inv_freq = ROPE_THETA ** (-jnp.arange(0, ROT_DIM, 2, dtype=F32) / ROT_DIM)
    per_head = jnp.concatenate([inv_freq, inv_freq, jnp.zeros(HEAD_DIM - ROT_DIM, F32)])
    inv_row = jnp.tile(per_head, 128 // HEAD_DIM).reshape(1, 128)
    tabs = _rope_tables(positions.reshape(s_len, 1), inv_row)
    qk = _matmul("proj_qk", hn0, w_ai, "nn", F32, MM_T, MM_T, dm, epilogue=_rot_fwd, mrows=tabs, n_out=2 * aw)
    v = _matmul("proj_vz", hn0, w_ai, "nn", F32, MM_T, MM_T, dm, b_noff=2 * aw, n_out=2 * aw)
    z0 = (v, 1)
    att = [_attn_fwd(g, qk, v) for g in range(3)]
    os_, lses = [a[0] for a in att], [a[1] for a in att]
    (g_ao,) = _exchange_wait("w_out_wait", ao_handles, "gather", lses[2])
    a0, y0, x1 = _attn_out(os_, lses, z0, x0, gate[0], g_ao.reshape(aw, dm))

    hn1 = _norm_mod_fwd("norm1", x1, nw[1], scale[1], shift[1])
    g_si, g_so, g_small = _exchange_wait("weights_wait", w_handles, "gather", hn1)
    w_ao = g_ao.reshape(aw, dm)
    w_si_t = g_si.reshape(n_ssd_in, dm)
    w_so = g_so.reshape(si, dm)
    conv_w = g_small[:, 0:CONV_WIDTH, :].transpose(1, 0, 2).reshape(CONV_WIDTH, sxbc)
    conv_b = g_small[:, 5, :].reshape(1, sxbc)
    snw = g_small[:, 6, 0:si // NDEV].reshape(1, si)
    ndt = 2 * SSD_HEADS
    z1 = _matmul("ssd_proj_z", hn1, w_si_t, "nt", F32, MM_T, MM_T, dm, n_out=si)
    xpre = _matmul("ssd_proj_xbc", hn1, w_si_t, "nt", F32, MM_T, MM_T, dm, b_noff=si, n_out=sxbc)
    dt_raw = _matmul("ssd_proj_dt", hn1, w_si_t, "nt", F32, MM_T, ndt, dm, b_noff=si + sxbc, n_out=ndt)
    xbc = _conv_fwd(xpre, conv_w, conv_b)
    widen = lambda a: jnp.pad(a, ((0, 0), (0, SSD_DTW - ndt)))
    dt_raw = widen(dt_raw)
    dt_bias = widen(ssd_dt_bias.reshape(1, ndt))
    alog = widen(ssd_a_log.reshape(1, ndt))
    dt = _softplus_fwd(dt_raw, dt_bias)
    y_f, st_f = _ssd_fwd(xbc, dt, alog, 0)
    y_fb, st_b = _ssd_fwd(xbc, dt, alog, 1, prior=y_f)
    d_e = jnp.repeat(ssd_d.reshape(SSD_HEADS), HEAD_DIM).reshape(1, si)

    fnw = final_norm_w.reshape(1, dm)
    u, dx2, dy1, g_fnw, dgate1, loss_part = _ssd_tail_loss(y_fb, xbc, z1, d_e, snw, w_so, x1, tgt, gate[1], fnw)
    gw_so = _matmul("ssd_out_dw", u, dy1, "tn", BF16, MM_T, MM_T, MM_T)
    dys, dz1, g_snw, g_d = _gate_norm_bwd(dy1, w_so, y_fb, xbc, z1, d_e, snw)
    dxbc_f, ddt_f, dalog_f = _ssd_bwd(xbc, dt, alog, st_f, dys, d_e, 0)
    dxbc, ddt_b, dalog_b = _ssd_bwd(xbc, dt, alog, st_b, dys, d_e, 1, prior=dxbc_f)
    dpre, g_cw, g_cb = _conv_bwd(xpre, dxbc, conv_w, conv_b)
    ddt_raw, g_dtb = _softplus_bwd(ddt_f, ddt_b, dt_raw, dt_bias)
    ddt_raw = ddt_raw[:, :ndt]
    dhn1 = [_matmul("ssd_proj_z_dx", dz1, w_si_t, "nn", F32, MM_T, MM_T, MM_T),
            _matmul("ssd_proj_xbc_dx", dpre, w_si_t, "nn", F32, MM_T, MM_T, MM_T, b_koff=si)]
    gw_si_t = _matmul("ssd_proj_z_dw", dz1, hn1, "tn", BF16, MM_T, MM_T, MM_T, dest=(n_ssd_in, 0, None))
    gw_si_t = _matmul("ssd_proj_xbc_dw", dpre, hn1, "tn", BF16, MM_T, MM_T, MM_T, dest=(n_ssd_in, si, gw_si_t))
    gw_si_t = _matmul("ssd_proj_dt_dw", ddt_raw, hn1, "tn", BF16, ndt, MM_T, MM_T, dest=(n_ssd_in, si + sxbc, gw_si_t))

    l1_grads = [gw_so.reshape(NDEV, si // NDEV, dm), gw_si_t.reshape(NDEV, n_ssd_in // NDEV, dm),
                _pack_ssd_small_blocks(g_cw, g_cb, g_snw)]
    l1_handles, l1_token = _exchange_start("l1_grads_start", l1_grads, _landing_zones("l1_grads_place", l1_grads, "scatter"),
                                           "scatter", dhn1[1])
    dx1, dy0, g_nw1, dsc1, dsh1, dgate0 = _norm_mod_bwd(
        "ssd_proj_dt_dx_norm1_bwd", (ddt_raw, w_si_t, "nn", ndt, dict(b_koff=si + sxbc)), x1, dhn1, dx2,
        nw[1], scale[1], shift[1], prev=(y0, gate[0] + l1_token[0:1, 0:1]))

    gw_ao = _matmul("attn_out_dw", a0, dy0, "tn", BF16, aw // 2, MM_T, MM_T)
    dos, dls, dz0 = _mix_bwd(dy0, w_ao, os_, lses, z0)
    datt = [_attn_bwd(g, qk, v, os_[g], lses[g], dos[g], dls[g]) for g in range(3)]
    dqkv = _rot_pack_bwd([t[0] for t in datt], [t[1] for t in datt], [t[2] for t in datt], tabs)
    gw_ai = _matmul("proj_qkv_dw", hn0, dqkv, "tn", BF16, MM_T, wcol, MM_T, out_blocks=3 * aw // wcol, dest=(NDEV, 0, None))
    gw_ai = _matmul("proj_z_dw", hn0, dz0, "tn", BF16, MM_T, wcol, MM_T, out_blocks=aw // wcol,
                    dest=(NDEV, 3 * aw // wcol, gw_ai))
    after_start = lambda acc, t: acc + t
    zero_row = lambda token: jnp.tile(token[0:1], (1, dm // 128))
    l0_grads = [gw_ai, gw_ao.reshape(NDEV, aw // NDEV, dm)]
    pair_handles, pair_token = _exchange_start("l0_pair_start", l0_grads, _landing_zones("l0_pair_place", l0_grads, "pair"),
                                               "pair", dqkv)
    dhn0_z = _matmul("proj_z_dx", dz0, w_ai, "nt", F32, MM_T, MM_T, aw, b_koff=3 * aw, n_out=dm, epilogue=after_start,
                     ncols=(zero_row(pair_token),))
    from_sibling = _exchange_wait("l0_pair_wait", pair_handles, "pair", dhn0_z)
    chip_sums = [_pair_sum(f"l0_pair_sum_{a}", g, s) for a, (g, s) in enumerate(zip(l0_grads, from_sibling))]
    l0_handles, l0_token = _exchange_start("l0_grads_start", chip_sums, _landing_zones("l0_grads_place", chip_sums, "chips"),
                                           "chips", dhn0_z)
    dx0, g_nw0, dsc0, dsh0 = _norm_mod_bwd(
        "proj_qkv_dx_norm0_bwd", (dqkv, w_ai, "nt", aw, dict(n_out=dm)), x0, [dhn0_z], dx1,
        nw[0], scale[0], shift[0] + zero_row(l0_token))

    small_g = [_pack_small([dsh0, dsc0, dgate0, dsh1, dsc1, dgate1, g_nw0, g_nw1, g_fnw], g_dtb, [dalog_f, dalog_b], g_d, loss_part)]
    sm_handles, sm_token = _exchange_start("small_grads_start", small_g, _landing_zones("small_grads_place", small_g, "gather"),
                                           "gather", dx0)

    whole = (slice(None), slice(None))
    r_so, r_si, r_small = _exchange_wait("l1_grads_wait", l1_handles, "scatter", sm_token)
    si_out = [o.T for o in _adamw("adamw_ssd_w_in", ssd_w_in[0].T, r_si, m_ssd_w_in[0].T, v_ssd_w_in[0].T, n_ssd_in // NDEV, 256)]
    so_out = _adamw("adamw_ssd_w_out", ssd_w_out[0], r_so, m_ssd_w_out[0], v_ssd_w_out[0], 256)
    cw_cols = ssd_conv_w.shape[2]
    cw_out, cb_out, snw_out = _adamw_windows(
        "adamw_ssd_small", r_small,
        [(ssd_conv_w, m_ssd_conv_w, v_ssd_conv_w), (ssd_conv_b, m_ssd_conv_b, v_ssd_conv_b),
         (ssd_norm_w, m_ssd_norm_w, v_ssd_norm_w)],
        [(0, slice(0, CONV_WIDTH), slice(0, cw_cols), (0, slice(None), slice(None))),
         (1, slice(5, 6), slice(0, cw_cols), whole), (2, slice(6, 7), slice(0, si // NDEV), whole)])
    r_ai, r_ao = _exchange_wait("l0_grads_wait", l0_handles, "chips", so_out[0])
    ai_out = _adamw("adamw_attn_w_in", attn_w_in[0], r_ai, m_attn_w_in[0], v_attn_w_in[0], 256)
    ao_out = _adamw("adamw_attn_w_out", attn_w_out[0], r_ao, m_attn_w_out[0], v_attn_w_out[0], 192)

    (small_all,) = _exchange_wait("small_grads_wait", sm_handles, "gather", ai_out[0])
    full = slice(0, PACK_COLS)
    nhd = SSD_HEADS
    windows = [(0, slice(3 * i + k, 3 * i + k + 1), full, (slice(i, i + 1), slice(k * dm, (k + 1) * dm)))
               for i in range(2) for k in range(3)]
    windows += [(1, slice(6 + i, 7 + i), full, (slice(i, i + 1), slice(None))) for i in range(2)]
    windows += [(2, slice(8, 9), full, whole)]
    windows += [(3 + q, slice(9, 10), slice(2 * nhd * q + nhd * j, 2 * nhd * q + nhd * (j + 1)), (0, slice(j, j + 1), slice(None)))
                for q in range(2) for j in range(2)]
    windows += [(5, slice(9, 10), slice(4 * nhd, 5 * nhd), whole)]
    as_row = lambda a: a.reshape(1, dm)
    mb_out, nw_out, fnw_out, dtb_out, alog_out, d_out, loss = _adamw_windows(
        "adamw_small", small_all,
        [(mod_b, m_mod_b, v_mod_b), (norm_w, m_norm_w, v_norm_w), (fnw, as_row(m_final_norm_w), as_row(v_final_norm_w)),
         (ssd_dt_bias, m_ssd_dt_bias, v_ssd_dt_bias), (ssd_a_log, m_ssd_a_log, v_ssd_a_log), (ssd_d, m_ssd_d, v_ssd_d)],
        windows, extra=(slice(9, 10), slice(256, 257)))
    loss = loss.reshape(())

    ncol = mod_w.shape[2]
    dmod_all = small_all[:, 0:6, :].reshape(NDEV, 2, 3 * dm)
    dmod_sh = lax.dynamic_slice_in_dim(dmod_all, me * ncol, ncol, axis=2).transpose(1, 0, 2)
    g_modw = _mod_grad(c_all, dmod_sh).reshape(1, 2 * dm, ncol)
    modw_out = _adamw("adamw_mod_w", mod_w.reshape(2 * dm, ncol), g_modw, m_mod_w.reshape(2 * dm, ncol),
                      v_mod_w.reshape(2 * dm, ncol), 256)

    per_kind = []
    for k in range(4):
        per_kind.append([
            nw_out[k], modw_out[k].reshape(mod_w.shape), mb_out[k], ai_out[k][None], ao_out[k][None], si_out[k][None],
            cw_out[k], cb_out[k], dtb_out[k], alog_out[k], d_out[k], snw_out[k], so_out[k][None], fnw_out[k].reshape(dm)])
    return (loss, dx0.reshape(x.shape), *per_kind[0], *per_kind[1], *per_kind[2], *per_kind[3])


def _pack_ssd_small_blocks(g_cw, g_cb, g_nw):
    nper = g_cw.shape[1] // NDEV
    nwper = g_nw.shape[1] // NDEV

    def body(cw_ref, cb_ref, nw_ref, o_ref):
        o_ref[...] = jnp.zeros_like(o_ref)
        for d in range(NDEV):
            o_ref[d, 0:5, :] = cw_ref[:, d * nper:(d + 1) * nper]
            o_ref[d, 5:6, :] = cb_ref[:, d * nper:(d + 1) * nper]
            o_ref[d, 6:7, 0:nwper] = nw_ref[:, d * nwper:(d + 1) * nwper]

    return pl.pallas_call(body, name="pack_ssd_small_grads", out_shape=jax.ShapeDtypeStruct((NDEV, 8, nper), F32))(g_cw, g_cb, g_nw)
```

```python
import functools
import math

import jax
import jax.numpy as jnp
from jax import lax
from jax.experimental import pallas as pl
from jax.experimental.pallas import tpu as pltpu

F32 = jnp.float32
BF16 = jnp.bfloat16
HI = lax.Precision.HIGHEST
MESH = pl.DeviceIdType.MESH
NDEV = 8

NORM_EPS = 1e-6
ROPE_THETA = 500000.0
ROT_DIM = 16
HEAD_DIM = 64
DILATIONS = (1, 4, 16)
BAND = 64
NEG_BIG = -1e30
CHUNK = 128
SSD_HEADS = 32
SSD_GROUPS = 8
CONV_WIDTH = 5

ADAM_LR = 0.001
ADAM_B1 = 0.9
ADAM_B2 = 0.999
ADAM_EPS = 1e-08
ADAM_WD = 0.01
ADAM_STEP = 10

VMEM_BIG = 56 * 1024 * 1024
MM_T = 1024


def _params(sem=None, vmem=None):
    kw = {}
    if sem is not None:
        kw["dimension_semantics"] = sem
    if vmem is not None:
        kw["vmem_limit_bytes"] = vmem
    return pltpu.CompilerParams(**kw)


def _dg(a, b, ca, cb, prec=None):
    return lax.dot_general(a, b, (((ca,), (cb,)), ((), ())), preferred_element_type=F32, precision=prec)


def _nn(a, b):
    return _dg(a.astype(BF16), b.astype(BF16), 1, 0)


def _nt(a, b):
    return _dg(a.astype(BF16), b.astype(BF16), 1, 1)


def _tn(a, b):
    return _dg(a.astype(BF16), b.astype(BF16), 0, 0)


def _hnn(a, b):
    return _dg(a, b, 1, 0, HI)


@jax.custom_vjp
def _bnn(a, b):
    return _nn(a, b)


_bnn.defvjp(lambda a, b: (_nn(a, b), (a, b)), lambda r, g: (_nt(g, r[1]), _tn(r[0], g)))


@jax.custom_vjp
def _bnt(a, b):
    return _nt(a, b)


_bnt.defvjp(lambda a, b: (_nt(a, b), (a, b)), lambda r, g: (_nn(g, r[1]), _tn(g, r[0])))


@jax.custom_vjp
def _btn(a, b):
    return _tn(a, b)


_btn.defvjp(lambda a, b: (_tn(a, b), (a, b)), lambda r, g: (_nt(r[1], g), _nn(r[0], g)))


def _silu(x):
    return x * jax.nn.sigmoid(x)


def _b_spec(b, mode, tn, tk, no, ko, jk):
    if mode == "nt":
        return pl.BlockSpec((tn, tk), lambda *g: (jk(*g)[0] + no, jk(*g)[1] + ko))
    return pl.BlockSpec((tk, tn), lambda *g: (jk(*g)[1] + ko, jk(*g)[0] + no))


def _matmul(name, a, b, mode, out_dtype, tm, tn, tk, *, epilogue=None, tiled=(), mrows=(), ncols=(),
            b_noff=0, b_koff=0, n_out=None, out_blocks=None, dest=None):
    if mode == "tn":
        K, M = a.shape
    else:
        M, K = a.shape
    N = n_out if n_out is not None else (b.shape[0] if mode == "nt" else b.shape[1])
    tm, tn, tk = min(tm, M), min(tn, N), min(tk, K)
    assert M % tm == 0 and N % tn == 0 and K % tk == 0, (name, M, N, K, tm, tn, tk)
    assert b_noff % tn == 0 and b_koff % tk == 0
    no, ko = b_noff // tn, b_koff // tk
    nk = K // tk
    if mode == "tn":
        a_spec = pl.BlockSpec((tk, tm), lambda i, j, k: (k, i))
    else:
        a_spec = pl.BlockSpec((tm, tk), lambda i, j, k: (i, k))
    specs = [a_spec, _b_spec(b, mode, tn, tk, no, ko, lambda i, j, k: (j, k))]
    specs += [pl.BlockSpec((tm, tn), lambda i, j, k: (i, j)) for _ in tiled]
    specs += [pl.BlockSpec((tm, r.shape[1]), lambda i, j, k: (i, 0)) for r in mrows]
    specs += [pl.BlockSpec((1, tn), lambda i, j, k: (0, j)) for _ in ncols]
    total, off, earlier = dest if dest is not None else (None, 0, None)
    if out_blocks is None:
        assert off % tm == 0
        mo = off // tm
        out_shape = jax.ShapeDtypeStruct((M if total is None else total, N), out_dtype)
        out_spec = pl.BlockSpec((tm, tn), lambda i, j, k: (i + mo, j))
    else:
        nper = N // out_blocks
        assert nper % tn == 0
        jb = nper // tn
        out_shape = jax.ShapeDtypeStruct((out_blocks if total is None else total, M, nper), out_dtype)
        out_spec = pl.BlockSpec((None, tm, tn), lambda i, j, k: (j // jb + off, i, j % jb))
    if earlier is not None:
        assert earlier.shape == out_shape.shape and earlier.dtype == out_shape.dtype
    ne = len(tiled) + len(mrows) + len(ncols)
    dot = {"nn": _nn, "nt": _nt, "tn": _tn}[mode]

    def body(a_ref, b_ref, *rest):
        extras, o_ref = rest[:ne], rest[ne]

        def finish(acc):
            if epilogue is not None:
                acc = epilogue(acc, *[e[...] for e in extras])
            o_ref[...] = acc.astype(o_ref.dtype)

        if nk == 1:
            finish(dot(a_ref[...], b_ref[...]))
        else:
            acc_ref = rest[ne + 1]
            k = pl.program_id(2)

            @pl.when(k == 0)
            def _():
                acc_ref[...] = jnp.zeros_like(acc_ref)

            acc_ref[...] += dot(a_ref[...], b_ref[...])

            @pl.when(k == nk - 1)
            def _():
                finish(acc_ref[...])

    args = [a, b, *tiled, *mrows, *ncols]
    aliases = {}
    if earlier is not None:
        specs.append(pl.BlockSpec(memory_space=pl.ANY))
        aliases = {len(args): 0}
        args.append(earlier)

    def body_with_dest(*refs):
        body(*refs[:2 + ne], *refs[2 + ne + (earlier is not None):])

    return pl.pallas_call(
        body_with_dest, name=name, out_shape=out_shape, grid=(M // tm, N // tn, nk),
        in_specs=specs, out_specs=out_spec, input_output_aliases=aliases,
        scratch_shapes=[] if nk == 1 else [pltpu.VMEM((tm, tn), F32)],
        compiler_params=_params(("parallel", "parallel", "arbitrary"), VMEM_BIG),
    )(*args)


def _matmul_rows(name, a, b, mode, tm, tk, fn, rows, consts, outs, accs, *, n_out=None, b_noff=0, b_koff=0):
    rl = [(t, t.shape[1], 0) if not isinstance(t, tuple) else t for t in rows]
    make_a = a if callable(a) else None
    M, K = (rl[0][0].shape[0], b.shape[1 if mode == "nt" else 0]) if make_a else a.shape
    N = n_out if n_out is not None else (b.shape[0] if mode == "nt" else b.shape[1])
    tm, tk = min(tm, M), min(tk, K)
    assert M % tm == 0 and K % tk == 0 and b_koff % tk == 0 and b_noff % N == 0, (name, M, N, K)
    no, ko, nk = b_noff // N, b_koff // tk, K // tk
    assert make_a is None or nk == 1
    nr, nc, no_, na = len(rl), len(consts), len(outs), len(accs)
    dot = _nt if mode == "nt" else _nn

    def body(*refs):
        a_ref, b_ref, rest = (None, refs[0], refs[1:]) if make_a else (refs[0], refs[1], refs[2:])
        r_refs, c_refs = rest[:nr], rest[nr:nr + nc]
        o_refs, acc_refs = rest[nr + nc:nr + nc + no_], rest[nr + nc + no_:nr + nc + no_ + na]
        i, k = pl.program_id(0), pl.program_id(1)

        def finish(prod, *made):
            res_o, res_a = fn(prod, *made, *[r[...] for r in r_refs], *[c[...] for c in c_refs])
            for r, v in zip(o_refs, res_o, strict=True):
                r[...] = v.astype(r.dtype)
            if acc_refs:
                @pl.when(i == 0)
                def _():
                    for r in acc_refs:
                        r[...] = jnp.zeros_like(r)

                for r, v in zip(acc_refs, res_a, strict=True):
                    r[...] += v

        if make_a:
            left = make_a(*[r[...] for r in r_refs], *[c[...] for c in c_refs])
            finish(dot(left, b_ref[...]), left)
        elif nk == 1:
            finish(dot(a_ref[...], b_ref[...]))
        else:
            prod_ref = rest[-1]

            @pl.when(k == 0)
            def _():
                prod_ref[...] = jnp.zeros_like(prod_ref)

            prod_ref[...] += dot(a_ref[...], b_ref[...])

            @pl.when(k == nk - 1)
            def _():
                finish(prod_ref[...])

    b_spec = _b_spec(b, mode, N, tk, no, ko, lambda i, k: (0, k))
    in_specs = ([] if make_a else [pl.BlockSpec((tm, tk), lambda i, k: (i, k))]) + [b_spec]
    in_specs += [pl.BlockSpec((tm, w), functools.partial(lambda i, k, cb: (i, cb), cb=cb)) for (_, w, cb) in rl]
    in_specs += [pl.BlockSpec(c.shape, lambda i, k: (0, 0)) for c in consts]
    out_specs = [pl.BlockSpec((tm, c), lambda i, k: (i, 0)) for (c, _) in outs]
    out_specs += [pl.BlockSpec(shp, lambda i, k: (0, 0)) for shp in accs]
    out_shape = [jax.ShapeDtypeStruct((M, c), dt) for (c, dt) in outs] + [jax.ShapeDtypeStruct(shp, F32) for shp in accs]
    res = pl.pallas_call(
        body, name=name, out_shape=out_shape, grid=(M // tm, nk), in_specs=in_specs, out_specs=out_specs,
        scratch_shapes=[] if nk == 1 else [pltpu.VMEM((tm, N), F32)],
        compiler_params=_params(("arbitrary" if accs else "parallel", "arbitrary"), VMEM_BIG),
    )(*([] if make_a else [a]), b, *[t[0] for t in rl], *consts)
    return res[:no_], res[no_:]


def _rowwise(name, fn, tiled, consts, outs, accs, ts):
    tl = [(t, t.shape[1], 0) if not isinstance(t, tuple) else t for t in tiled]
    s_len = tl[0][0].shape[0]
    assert s_len % ts == 0
    nt_, nc_, no_ = len(tl), len(consts), len(outs)

    def body(*refs):
        t_refs, c_refs = refs[:nt_], refs[nt_:nt_ + nc_]
        o_refs, a_refs = refs[nt_ + nc_:nt_ + nc_ + no_], refs[nt_ + nc_ + no_:]
        res_o, res_a = fn(*[r[...] for r in t_refs], *[r[...] for r in c_refs])
        for r, v in zip(o_refs, res_o, strict=True):
            r[...] = v.astype(r.dtype)
        if a_refs:
            @pl.when(pl.program_id(0) == 0)
            def _():
                for r in a_refs:
                    r[...] = jnp.zeros_like(r)

            for r, v in zip(a_refs, res_a, strict=True):
                r[...] += v

    in_specs = [pl.BlockSpec((ts, w), functools.partial(lambda i, cb: (i, cb), cb=cb)) for (_, w, cb) in tl]
    in_specs += [pl.BlockSpec(c.shape, lambda i: (0, 0)) for c in consts]
    out_specs = [pl.BlockSpec((ts, c), lambda i: (i, 0)) for (c, _) in outs]
    out_specs += [pl.BlockSpec(shp, lambda i: (0, 0)) for shp in accs]
    out_shape = [jax.ShapeDtypeStruct((s_len, c), dt) for (c, dt) in outs]
    out_shape += [jax.ShapeDtypeStruct(shp, F32) for shp in accs]
    res = pl.pallas_call(
        body, name=name, out_shape=out_shape, grid=(s_len // ts,), in_specs=in_specs, out_specs=out_specs,
        compiler_params=_params(("arbitrary",) if accs else ("parallel",), VMEM_BIG),
    )(*[t[0] for t in tl], *consts)
    return res[:no_], res[no_:]


def _norm_mod_fn(x, nw, sc, sh):
    r = lax.rsqrt(jnp.mean(x * x, axis=-1, keepdims=True) + NORM_EPS)
    return (x * r * nw) * (1.0 + sc) + sh


def _norm_mod_fwd(name, x, nw, sc, sh):
    (hn,), _ = _rowwise(name, lambda x, nw, sc, sh: ([_norm_mod_fn(x, nw, sc, sh)], []),
                        [x], [nw, sc, sh], [(x.shape[1], BF16)], [], 512)
    return hn


def _norm_mod_bwd(name, last, x, dhn_parts, dres, nw, sc, sh, prev=None):
    n = len(dhn_parts)
    d = x.shape[1]
    a, b, mode, tk, kw = last

    def fn(dhn, x, *rest):
        for p in rest[:n]:
            dhn = dhn + p
        dres, rest = rest[n], rest[n + 1:]
        y_prev, (nw, sc, sh), gate = (rest[0], rest[1:4], rest[4]) if prev is not None else (None, rest[0:3], None)
        r = lax.rsqrt(jnp.mean(x * x, axis=-1, keepdims=True) + NORM_EPS)
        xh = x * r
        dxh = dhn * (nw * (1.0 + sc))
        dx = r * (dxh - xh * jnp.mean(dxh * xh, axis=-1, keepdims=True)) + dres
        along = jnp.sum(dhn * xh, axis=0, keepdims=True)
        dnw, dsc, dsh = along * (1.0 + sc), along * nw, jnp.sum(dhn, axis=0, keepdims=True)
        if prev is None:
            return [dx], [dnw, dsc, dsh]
        return [dx, gate * dx], [dnw, dsc, dsh, jnp.sum(dx * y_prev, axis=0, keepdims=True)]

    rows = [x, *dhn_parts, dres] + ([prev[0]] if prev is not None else [])
    consts = [nw, sc, sh] + ([prev[1]] if prev is not None else [])
    outs = [(d, F32)] + ([(d, BF16)] if prev is not None else [])
    res_o, res_a = _matmul_rows(name, a, b, mode, 512, tk, fn, rows, consts, outs, [(1, d)] * (3 + (prev is not None)), **kw)
    return (*res_o, *res_a)


def _rope_tables(pos_col, inv_row):
    def fn(pos, inv):
        ang = pos.astype(F32) * inv
        e = lax.broadcasted_iota(jnp.int32, (1, 128), 1) % HEAD_DIM
        cos, sin = jnp.cos(ang), jnp.sin(ang)
        half = ROT_DIM // 2
        return [jnp.where(e < ROT_DIM, cos, 1.0), jnp.where(e < half, -sin, 0.0),
                jnp.where((e >= half) & (e < ROT_DIM), sin, 0.0)], []

    (c, sa, sb), _ = _rowwise("rope_tables", fn, [pos_col], [inv_row], [(128, F32)] * 3, [], 512)
    return c, sa, sb


def _rot_fwd(t, c, sa, sb):
    n = t.shape[1]
    rep = n // 128
    c, sa, sb = (jnp.tile(u, (1, rep)) for u in (c, sa, sb))
    return t * c + pltpu.roll(t, n - ROT_DIM // 2, 1) * sa + pltpu.roll(t, ROT_DIM // 2, 1) * sb


def _rot_bwd(g, c, sa, sb):
    n = g.shape[1]
    rep = n // 128
    c, sa, sb = (jnp.tile(u, (1, rep)) for u in (c, sa, sb))
    return g * c + pltpu.roll(g * sa, ROT_DIM // 2, 1) + pltpu.roll(g * sb, n - ROT_DIM // 2, 1)


ATT_TQ = 128


def _attn_tiles(l):
    tk = ATT_TQ + 2 * BAND
    return (l, l) if l <= tk else (ATT_TQ, tk)


def _attn_specs(g, s_len):
    def blk(off):
        return pl.BlockSpec((s_len, 128), functools.partial(lambda hp, off: (0, off + hp), off=off))

    return blk(4 * g), blk(12 + 4 * g), blk(4 * g), blk(0)


def _attn_tile_geometry(t, d, l):
    tq, tk = _attn_tiles(l)
    nts = l // tq
    r = t // nts
    ts = t % nts
    q0 = ts * tq
    ws = jnp.clip(q0 - BAND, 0, l - tk)
    kind = jnp.where(ts == 0, 0, jnp.where(ts == nts - 1, 2, 1))
    if d == 1:
        return pl.ds(pl.multiple_of(q0, tq), tq), pl.ds(pl.multiple_of(ws, BAND), tk), kind
    return pl.ds(r + d * q0, tq, stride=d), pl.ds(r + d * ws, tk, stride=d), kind


def _attn_fill_bias(bias_ref):
    _, tq2, tk = bias_ref.shape
    iq = lax.broadcasted_iota(jnp.int32, (tq2, 1), 0) % (tq2 // 2)
    ik = lax.broadcasted_iota(jnp.int32, (1, tk), 1)
    for i, off in enumerate((0, -BAND, -2 * BAND)):
        bias_ref[i] = jnp.where(jnp.abs(ik + off - iq) <= BAND, 0.0, NEG_BIG)


def _split_heads(t, in_h):
    zero = jnp.zeros_like(t)
    return jnp.concatenate([jnp.where(in_h[0], t, zero), jnp.where(in_h[1], t, zero)], axis=0)


def _attn_fwd(g, qk, v):
    s_len = qk.shape[0]
    d = DILATIONS[g]
    l = s_len // d
    tq, tk = _attn_tiles(l)
    assert l % tq == 0 and l >= tk
    q_spec, k_spec, v_spec, o_spec = _attn_specs(g, s_len)
    scale = 1.0 / math.sqrt(HEAD_DIM)

    def body(q_ref, k_ref, v_ref, o_ref, lse_ref, bias_ref):
        lane = lax.broadcasted_iota(jnp.int32, (1, 128), 1)
        in_h = [lane < HEAD_DIM, lane >= HEAD_DIM]
        _attn_fill_bias(bias_ref)

        def tile(t, carry):
            rows, win, kind = _attn_tile_geometry(t, d, l)
            q = (q_ref[rows, :] * scale).astype(BF16)
            k = k_ref[win, :].astype(BF16)
            vv = v_ref[win, :].astype(BF16)
            s = _nt(_split_heads(q, in_h), k) + bias_ref[kind]
            m = jnp.max(s, axis=1, keepdims=True)
            p = jnp.exp(s - m)
            den = jnp.sum(p, axis=1, keepdims=True)
            out = _nn(p, vv) / den
            lse = m + jnp.log(den)
            o_ref[rows, :] = jnp.where(in_h[0], out[:tq], out[tq:])
            lse_ref[rows, :] = jnp.where(in_h[0], lse[:tq], lse[tq:])
            return carry

        lax.fori_loop(0, s_len // tq, tile, 0, unroll=8 * ATT_TQ // tq)

    return pl.pallas_call(
        body, name=f"attn_fwd_g{g}", grid=(4,),
        out_shape=[jax.ShapeDtypeStruct((s_len, 512), F32)] * 2,
        in_specs=[q_spec, k_spec, v_spec], out_specs=[o_spec, o_spec],
        scratch_shapes=[pltpu.VMEM((3, 2 * tq, tk), F32)],
        compiler_params=_params(("parallel",), VMEM_BIG),
    )(qk, qk, v)


def _attn_bwd(g, qk, v, o, lse, do, dlse):
    s_len = qk.shape[0]
    d = DILATIONS[g]
    l = s_len // d
    tq, tk = _attn_tiles(l)
    q_spec, k_spec, v_spec, o_spec = _attn_specs(g, s_len)
    scale = 1.0 / math.sqrt(HEAD_DIM)

    def body(q_ref, k_ref, v_ref, o_ref, lse_ref, do_ref, dlse_ref, dq_ref, dk_ref, dv_ref, bias_ref):
        lane = lax.broadcasted_iota(jnp.int32, (1, 128), 1)
        in_h = [lane < HEAD_DIM, lane >= HEAD_DIM]
        dk_ref[...] = jnp.zeros_like(dk_ref)
        dv_ref[...] = jnp.zeros_like(dv_ref)
        _attn_fill_bias(bias_ref)

        def tile(t, carry):
            rows, win, kind = _attn_tile_geometry(t, d, l)
            k, vv = k_ref[win, :].astype(BF16), v_ref[win, :].astype(BF16)
            dout, lse_t, dlse_t = do_ref[rows, :], lse_ref[rows, :], dlse_ref[rows, :]
            od = dout * o_ref[rows, :]
            q2 = _split_heads((q_ref[rows, :] * scale).astype(BF16), in_h)
            do2 = _split_heads(dout.astype(BF16), in_h)
            head_col = lambda a: jnp.concatenate([a[:, 0:1], a[:, HEAD_DIM:HEAD_DIM + 1]], axis=0)
            delta = jnp.concatenate([jnp.sum(jnp.where(m, od, 0.0), axis=1, keepdims=True) for m in in_h], axis=0)
            p = jnp.exp(_nt(q2, k) + bias_ref[kind] - head_col(lse_t))
            ds = (p * (_nt(do2, vv) - delta + head_col(dlse_t))).astype(BF16)
            dq2 = _nn(ds, k) * scale
            dq_ref[rows, :] = jnp.where(in_h[0], dq2[:tq], dq2[tq:])
            dk_ref[win, :] += _tn(ds, q2)
            dv_ref[win, :] += _tn(p, do2)
            return carry

        lax.fori_loop(0, s_len // tq, tile, 0, unroll=8 * ATT_TQ // tq)

    return pl.pallas_call(
        body, name=f"attn_bwd_g{g}", grid=(4,),
        out_shape=[jax.ShapeDtypeStruct((s_len, 512), F32)] * 3,
        in_specs=[q_spec, k_spec, v_spec, o_spec, o_spec, o_spec, o_spec], out_specs=[o_spec] * 3,
        scratch_shapes=[pltpu.VMEM((3, 2 * tq, tk), F32)],
        compiler_params=_params(("parallel",), VMEM_BIG),
    )(qk, qk, v, o, lse, do, dlse)


def _mix_weights(ls):
    mx = jnp.maximum(jnp.maximum(ls[0], ls[1]), ls[2])
    es = [jnp.exp(x - mx) for x in ls]
    tot = es[0] + es[1] + es[2]
    return [e / tot for e in es]


def _attn_out(os_, lses, z, x, gate, w_out):
    s_len, dm = x.shape
    tm = 256
    wdt = 512
    z, z_block = z

    def body(o0, o1, o2, l0, l1, l2, z_ref, x_ref, g_ref, w_ref, a_ref, y_ref, x1_ref):
        alphas = _mix_weights([l0[...], l1[...], l2[...]])
        y = jnp.zeros((tm, dm), F32)
        for g, o_ref in enumerate((o0, o1, o2)):
            a_g = (o_ref[...] * alphas[g] * _silu(z_ref[:, g * wdt:(g + 1) * wdt])).astype(BF16)
            a_ref[:, g * wdt:(g + 1) * wdt] = a_g
            y = y + _nn(a_g, w_ref[g * wdt:(g + 1) * wdt, :])
        y_ref[...] = y
        x1_ref[...] = x_ref[...] + g_ref[...] * y

    row = lambda c: pl.BlockSpec((tm, c), lambda i: (i, 0))
    return pl.pallas_call(
        body, name="attn_out", grid=(s_len // tm,),
        out_shape=[jax.ShapeDtypeStruct((s_len, 3 * wdt), BF16), jax.ShapeDtypeStruct((s_len, dm), F32),
                   jax.ShapeDtypeStruct((s_len, dm), F32)],
        in_specs=[row(wdt)] * 6 + [pl.BlockSpec((tm, 3 * wdt), lambda i: (i, z_block)), row(dm),
                                   pl.BlockSpec((1, dm), lambda i: (0, 0)), pl.BlockSpec(w_out.shape, lambda i: (0, 0))],
        out_specs=[row(3 * wdt), row(dm), row(dm)],
        compiler_params=_params(("parallel",), VMEM_BIG),
    )(*os_, *lses, z, x, gate, w_out)


def _mix_bwd(dy, w_out, os_, lses, z):
    wdt = 512

    def fn(da, o0, o1, o2, l0, l1, l2, z):
        os_t, ls = [o0, o1, o2], [l0, l1, l2]
        alphas = _mix_weights(ls)
        hi = lax.broadcasted_iota(jnp.int32, (2 * wdt, wdt), 0) % wdt // HEAD_DIM
        hj = lax.broadcasted_iota(jnp.int32, (2 * wdt, wdt), 1) // HEAD_DIM
        seg = (hi == hj).astype(BF16)
        head_sum = lambda t: _dg(jnp.concatenate(_bf16_parts(t, 2), axis=1), seg, 1, 0)
        dos, dal, dzs = [], [], []
        for g in range(3):
            zg = z[:, g * wdt:(g + 1) * wdt]
            sig = jax.nn.sigmoid(zg)
            dag = da[:, g * wdt:(g + 1) * wdt]
            dmix = dag * zg * sig
            dzs.append(dag * os_t[g] * alphas[g] * (sig * (1.0 + zg * (1.0 - sig))))
            dos.append(dmix * alphas[g])
            dal.append(head_sum(dmix * os_t[g]))
        mean = alphas[0] * dal[0] + alphas[1] * dal[1] + alphas[2] * dal[2]
        dls = [alphas[g] * (dal[g] - mean) for g in range(3)]
        return dos + dls + [jnp.concatenate(dzs, axis=1)], []

    outs, _ = _matmul_rows("attn_out_dx_mix_bwd", dy, w_out, "nt", 256, dy.shape[1], fn, [*os_, *lses, (z[0], 3 * wdt, z[1])], [],
                           [(wdt, F32)] * 6 + [(3 * wdt, BF16)], [])
    return outs[:3], outs[3:6], outs[6]


def _rot_pack_bwd(dqs, dks, dvs, tabs):
    wdt = 512

    def fn(*args):
        grads, (c, sa, sb) = args[:9], args[9:]
        cols = [_rot_bwd(gq, c, sa, sb) for gq in grads[:6]] + list(grads[6:])
        return [jnp.concatenate(cols, axis=1)], []

    (out,), _ = _rowwise("rot_pack_bwd", fn, [*dqs, *dks, *dvs, *tabs], [], [(9 * wdt, BF16)], [], 512)
    return out


CONV_CB = 128
CONV_R = 256
CONV_PAD = 8


def _conv_taps(buf, base, off, sign):
    return [buf[pl.ds(base + off + sign * j, CONV_R), :] for j in range(CONV_WIDTH)]


def _conv_tap_sum(taps, w):
    acc = None
    for j, t in enumerate(taps):
        term = t * w[j:j + 1, :]
        acc = term if acc is None else acc + term
    return acc


def _conv_fwd(xpre, cw, cb):
    s_len, ch = xpre.shape
    nchunk = s_len // CONV_R

    def body(x_ref, w_ref, b_ref, o_ref, xp):
        zero = jnp.zeros((CONV_PAD, CONV_CB), F32)
        xp[0:CONV_PAD, :] = zero
        xp[s_len + CONV_PAD:s_len + 2 * CONV_PAD, :] = zero

        def fill(ci, carry):
            base = pl.multiple_of(ci * CONV_R, CONV_R)
            xp[pl.ds(base + CONV_PAD, CONV_R), :] = x_ref[pl.ds(base, CONV_R), :]
            return carry

        lax.fori_loop(0, nchunk, fill, 0)
        w = w_ref[...]
        b = b_ref[...]

        def chunk(ci, carry):
            base = pl.multiple_of(ci * CONV_R, CONV_R)
            u = _conv_tap_sum(_conv_taps(xp, base, CONV_PAD - CONV_WIDTH // 2, 1), w) + b
            o_ref[pl.ds(base, CONV_R), :] = _silu(u)
            return carry

        lax.fori_loop(0, nchunk, chunk, 0, unroll=2)

    col = lambda r: pl.BlockSpec((r, CONV_CB), lambda j: (0, j))
    return pl.pallas_call(
        body, name="conv_fwd", grid=(ch // CONV_CB,), out_shape=jax.ShapeDtypeStruct((s_len, ch), F32),
        in_specs=[col(s_len), col(CONV_WIDTH), col(1)], out_specs=col(s_len),
        scratch_shapes=[pltpu.VMEM((s_len + 2 * CONV_PAD, CONV_CB), F32)],
        compiler_params=_params(("parallel",), VMEM_BIG),
    )(xpre, cw, cb)


def _conv_bwd(xpre, da, cw, cb):
    s_len, ch = xpre.shape
    nchunk = s_len // CONV_R
    half = CONV_WIDTH // 2

    def body(x_ref, da_ref, w_ref, b_ref, dx_ref, gw_ref, gb_ref, xp, dcp):
        zero = jnp.zeros((CONV_PAD, CONV_CB), F32)
        for buf in (xp, dcp):
            buf[0:CONV_PAD, :] = zero
            buf[s_len + CONV_PAD:s_len + 2 * CONV_PAD, :] = zero

        def fill(ci, carry):
            base = pl.multiple_of(ci * CONV_R, CONV_R)
            xp[pl.ds(base + CONV_PAD, CONV_R), :] = x_ref[pl.ds(base, CONV_R), :]
            return carry

        lax.fori_loop(0, nchunk, fill, 0)
        w = w_ref[...]
        b = b_ref[...]

        def first(ci, carry):
            base = pl.multiple_of(ci * CONV_R, CONV_R)
            taps = _conv_taps(xp, base, CONV_PAD - half, 1)
            u = _conv_tap_sum(taps, w) + b
            sig = jax.nn.sigmoid(u)
            dc = da_ref[pl.ds(base, CONV_R), :] * (sig * (1.0 + u * (1.0 - sig)))
            dcp[pl.ds(base + CONV_PAD, CONV_R), :] = dc
            gb = carry[0] + jnp.sum(dc, axis=0, keepdims=True)
            gws = [carry[1 + j] + jnp.sum(dc * taps[j], axis=0, keepdims=True) for j in range(CONV_WIDTH)]
            return (gb, *gws)

        z1 = jnp.zeros((1, CONV_CB), F32)
        sums = lax.fori_loop(0, nchunk, first, (z1,) * (1 + CONV_WIDTH), unroll=2)
        gb_ref[...] = sums[0]
        for j in range(CONV_WIDTH):
            gw_ref[j:j + 1, :] = sums[1 + j]

        def second(ci, carry):
            base = pl.multiple_of(ci * CONV_R, CONV_R)
            dx_ref[pl.ds(base, CONV_R), :] = _conv_tap_sum(_conv_taps(dcp, base, CONV_PAD + half, -1), w).astype(dx_ref.dtype)
            return carry

        lax.fori_loop(0, nchunk, second, 0, unroll=2)

    col = lambda r: pl.BlockSpec((r, CONV_CB), lambda j: (0, j))
    return pl.pallas_call(
        body, name="conv_bwd", grid=(ch // CONV_CB,),
        out_shape=[jax.ShapeDtypeStruct((s_len, ch), BF16), jax.ShapeDtypeStruct((CONV_WIDTH, ch), F32),
                   jax.ShapeDtypeStruct((1, ch), F32)],
        in_specs=[col(s_len), col(s_len), col(CONV_WIDTH), col(1)],
        out_specs=[col(s_len), col(CONV_WIDTH), col(1)],
        scratch_shapes=[pltpu.VMEM((s_len + 2 * CONV_PAD, CONV_CB), F32)] * 2,
        compiler_params=_params(("parallel",), VMEM_BIG),
    )(xpre, da, cw, cb)


SSD_GW = 256
SSD_N = 128
SSD_DTW = 128


def _bf16_parts(x, n):
    parts, rest = [], x
    for _ in range(n):
        p = rest.astype(BF16)
        parts.append(p)
        rest = rest - p.astype(F32)
    return parts


@jax.custom_vjp
def _expand(x, e):
    eb = e.astype(BF16)
    return _dg(jnp.concatenate(_bf16_parts(x, 2), axis=1), jnp.concatenate([eb, eb], axis=0), 1, 0)


def _expand_fwd(x, e):
    return _expand(x, e), e


def _expand_bwd(e, g):
    return _dg(g.astype(BF16), e.astype(BF16), 1, 1), jnp.zeros_like(e)


_expand.defvjp(_expand_fwd, _expand_bwd)


@jax.custom_vjp
def _running_sum(tri, x):
    tb = tri.astype(BF16)
    return sum(_dg(tb, p, 1, 0) for p in _bf16_parts(x, 3))


def _running_sum_fwd(tri, x):
    return _running_sum(tri, x), tri


def _running_sum_bwd(tri, g):
    tb = tri.astype(BF16)
    return jnp.zeros_like(tri), sum(_dg(tb, p, 0, 0) for p in _bf16_parts(g, 3))


_running_sum.defvjp(_running_sum_fwd, _running_sum_bwd)


def _pick_col(a, h):
    @jax.custom_vjp
    def pick(a):
        return a[:, h:h + 1]

    pick.defvjp(lambda a: (a[:, h:h + 1], None),
                lambda _, g: (g * (lax.broadcasted_iota(jnp.int32, (1, a.shape[1]), 1) == h).astype(F32),))
    return pick(a)


def _pick_row(a, h):
    @jax.custom_vjp
    def pick(a):
        return a[h:h + 1, :]

    pick.defvjp(lambda a: (a[h:h + 1, :], None),
                lambda _, g: (g * (lax.broadcasted_iota(jnp.int32, (a.shape[0], 1), 0) == h).astype(F32),))
    return pick(a)


def _ssd_mask(dirn):
    ri = lax.broadcasted_iota(jnp.int32, (CHUNK, CHUNK), 0)
    cj = lax.broadcasted_iota(jnp.int32, (CHUNK, CHUNK), 1)
    return (cj <= ri) if dirn == 0 else (cj >= ri)


def _ssd_rowsel(dirn):
    last = CHUNK - 1 if dirn == 0 else 0
    return (lax.broadcasted_iota(jnp.int32, (CHUNK, 1), 0) == last).astype(F32)


def _ssd_chunk_pre(dirn):
    nh = SSD_DTW

    def f(dt, alog):
        da = dt * (-jnp.exp(alog))
        cum = _running_sum(_ssd_mask(dirn).astype(F32), da)
        tot = jnp.sum(cum * _ssd_rowsel(dirn), axis=0, keepdims=True)
        hh = lax.broadcasted_iota(jnp.int32, (nh, SSD_HEADS * HEAD_DIM), 0)
        jj = lax.broadcasted_iota(jnp.int32, (nh, SSD_HEADS * HEAD_DIM), 1)
        expand = (hh == dirn * SSD_HEADS + jj // HEAD_DIM).astype(F32)
        return cum, cum.T, _expand(dt, expand), _expand(jnp.exp(tot - cum), expand), _expand(jnp.exp(cum), expand)

    return f


def _ssd_group_fn(g, dirn, stacked):
    def f(xs, bm, cm, st, cum, cum_t, dt_e, w_e, ce_e):
        mask = _ssd_mask(dirn)
        xdt = xs * dt_e
        cd_e = jnp.sum(ce_e * _ssd_rowsel(dirn), axis=0, keepdims=True)
        cb = _bnt(cm, bm)
        lane_head = lax.broadcasted_iota(jnp.int32, (1, SSD_GW), 1) // HEAD_DIM
        y = _bnn(cm, st) * ce_e
        decayed, inputs = [], []
        for j in range(4):
            hidx = dirn * SSD_HEADS + 4 * g + j
            col, row = _pick_col(cum, hidx), _pick_row(cum_t, hidx)
            dec = cb * jnp.exp(jnp.where(mask, col - row, NEG_BIG))
            head = (lane_head == j).astype(F32)
            if stacked:
                decayed.append(dec)
                inputs.append(xdt * head)
            else:
                y = y + _bnn(dec, xdt) * head
        if stacked:
            y = y + _bnn(jnp.concatenate(decayed, axis=1), jnp.concatenate(inputs, axis=0))
        st_out = st * cd_e + _btn(bm, xdt * w_e)
        return y, st_out

    return f


def _ssd_in_specs(kk):
    ln = CHUNK
    return [pl.BlockSpec((ln, 2048), lambda i: (kk(i), 0)),
            pl.BlockSpec((ln, 1024), lambda i: (kk(i), 2)),
            pl.BlockSpec((ln, 1024), lambda i: (kk(i), 3)),
            pl.BlockSpec((ln, SSD_DTW), lambda i: (kk(i), 0)),
            pl.BlockSpec((1, SSD_DTW), lambda i: (0, 0))]


def _ssd_fwd(xbc, dt, alog, dirn, prior=None):
    s_len = xbc.shape[0]
    nc = s_len // CHUNK
    kk = (lambda i: i) if dirn == 0 else (lambda i: nc - 1 - i)

    def body(x_ref, b_ref, c_ref, dt_ref, al_ref, *rest):
        prior_ref = rest[0] if prior is not None else None
        y_ref, sts_ref, st = rest[prior is not None:]

        @pl.when(pl.program_id(0) == 0)
        def _():
            st[...] = jnp.zeros_like(st)

        sts_ref[0] = st[...]
        cum, cum_t, dt_e, w_e, ce_e = _ssd_chunk_pre(dirn)(dt_ref[...], al_ref[...])
        for g in range(SSD_GROUPS):
            xc = slice(g * SSD_GW, (g + 1) * SSD_GW)
            gc = slice(g * SSD_N, (g + 1) * SSD_N)
            y, st_new = _ssd_group_fn(g, dirn, True)(x_ref[:, xc], b_ref[:, gc], c_ref[:, gc], st[:, xc], cum, cum_t,
                                               dt_e[:, xc], w_e[:, xc], ce_e[:, xc])
            y_ref[:, xc] = y if prior is None else y + prior_ref[:, xc]
            st[:, xc] = st_new

    return pl.pallas_call(
        body, name=f"ssd_fwd_d{dirn}", grid=(nc,),
        out_shape=[jax.ShapeDtypeStruct((s_len, 2048), F32), jax.ShapeDtypeStruct((nc, SSD_N, 2048), F32)],
        in_specs=_ssd_in_specs(kk) + ([pl.BlockSpec((CHUNK, 2048), lambda i: (kk(i), 0))] if prior is not None else []),
        out_specs=[pl.BlockSpec((CHUNK, 2048), lambda i: (kk(i), 0)),
                   pl.BlockSpec((1, SSD_N, 2048), lambda i: (kk(i), 0, 0))],
        scratch_shapes=[pltpu.VMEM((SSD_N, 2048), F32)],
        compiler_params=_params(("arbitrary",), VMEM_BIG),
    )(xbc, xbc, xbc, dt, alog, *([prior] if prior is not None else []))


def _ssd_bwd(xbc, dt, alog, states, dy, d_e, dirn, prior=None):
    s_len = xbc.shape[0]
    nc = s_len // CHUNK
    kk = (lambda i: nc - 1 - i) if dirn == 0 else (lambda i: i)

    def body(x_ref, b_ref, c_ref, dt_ref, al_ref, sts_ref, dy_ref, de_ref, *rest):
        prior_ref = rest[0] if prior is not None else None
        dx_ref, ddt_ref, dal_ref, dst = rest[prior is not None:]
        plus_prior = (lambda v, cols: v + prior_ref[:, cols]) if prior is not None else (lambda v, cols: v)

        @pl.when(pl.program_id(0) == 0)
        def _():
            dst[...] = jnp.zeros_like(dst)
            dal_ref[...] = jnp.zeros_like(dal_ref)

        (cum, cum_t, dt_e, w_e, ce_e), pre_vjp = jax.vjp(_ssd_chunk_pre(dirn), dt_ref[...], al_ref[...])
        dcum = jnp.zeros_like(cum)
        dcum_t = jnp.zeros_like(cum_t)
        d_dt_e, d_w_e, d_ce_e = [], [], []
        for g in range(SSD_GROUPS):
            xc = slice(g * SSD_GW, (g + 1) * SSD_GW)
            gc = slice(g * SSD_N, (g + 1) * SSD_N)
            _, vjp = jax.vjp(_ssd_group_fn(g, dirn, False), x_ref[:, xc], b_ref[:, gc], c_ref[:, gc], sts_ref[0, :, xc], cum, cum_t,
                             dt_e[:, xc], w_e[:, xc], ce_e[:, xc])
            dyg = dy_ref[:, xc]
            dxs, dbm, dcm, dst_g, dcum_g, dcum_t_g, ddte_g, dwe_g, dcee_g = vjp((dyg, dst[:, xc]))
            if dirn == 0:
                dxs = dxs + dyg * de_ref[:, xc]
            bc, cc = slice(2048 + g * SSD_N, 2048 + (g + 1) * SSD_N), slice(3072 + g * SSD_N, 3072 + (g + 1) * SSD_N)
            dx_ref[:, xc] = plus_prior(dxs, xc)
            dx_ref[:, bc] = plus_prior(dbm, bc)
            dx_ref[:, cc] = plus_prior(dcm, cc)
            dst[:, xc] = dst_g
            dcum = dcum + dcum_g
            dcum_t = dcum_t + dcum_t_g
            d_dt_e.append(ddte_g)
            d_w_e.append(dwe_g)
            d_ce_e.append(dcee_g)
        ddt, dal = pre_vjp((dcum, dcum_t, jnp.concatenate(d_dt_e, axis=1), jnp.concatenate(d_w_e, axis=1),
                            jnp.concatenate(d_ce_e, axis=1)))
        ddt_ref[...] = ddt
        dal_ref[...] += dal

    return pl.pallas_call(
        body, name=f"ssd_bwd_d{dirn}", grid=(nc,),
        out_shape=[jax.ShapeDtypeStruct((s_len, 4096), F32), jax.ShapeDtypeStruct((s_len, SSD_DTW), F32),
                   jax.ShapeDtypeStruct((1, SSD_DTW), F32)],
        in_specs=_ssd_in_specs(kk) + [pl.BlockSpec((1, SSD_N, 2048), lambda i: (kk(i), 0, 0)),
                                      pl.BlockSpec((CHUNK, 2048), lambda i: (kk(i), 0)),
                                      pl.BlockSpec((1, 2048), lambda i: (0, 0))]
        + ([pl.BlockSpec((CHUNK, 4096), lambda i: (kk(i), 0))] if prior is not None else []),
        out_specs=[pl.BlockSpec((CHUNK, 4096), lambda i: (kk(i), 0)),
                   pl.BlockSpec((CHUNK, SSD_DTW), lambda i: (kk(i), 0)),
                   pl.BlockSpec((1, SSD_DTW), lambda i: (0, 0))],
        scratch_shapes=[pltpu.VMEM((SSD_N, 2048), F32)],
        compiler_params=_params(("arbitrary",), VMEM_BIG),
    )(xbc, xbc, xbc, dt, alog, states, dy, d_e, *([prior] if prior is not None else []))


def _gate_norm_fn(y, xs, z, d_e, nw):
    yg = (y + xs * d_e) * _silu(z)
    return yg * lax.rsqrt(jnp.mean(yg * yg, axis=-1, keepdims=True) + NORM_EPS) * nw


def _gate_norm_bwd(dy, w_out, y, xbc, z, d_e, nw):
    def fn(du, y, xs, z, d_e, nw):
        sig = jax.nn.sigmoid(z)
        gate = z * sig
        ysum = y + xs * d_e
        yg = ysum * gate
        r = lax.rsqrt(jnp.mean(yg * yg, axis=-1, keepdims=True) + NORM_EPS)
        t = du * nw
        dyg = t * r - yg * (jnp.mean(t * yg, axis=-1, keepdims=True) * (r * r * r))
        dys = dyg * gate
        dz = dyg * ysum * (sig * (1.0 + z * (1.0 - sig)))
        dnw = jnp.sum(du * yg * r, axis=0, keepdims=True)
        dde = jnp.sum(dys * xs, axis=0, keepdims=True)
        hh = lax.broadcasted_iota(jnp.int32, (2048, SSD_HEADS), 0) // HEAD_DIM
        jj = lax.broadcasted_iota(jnp.int32, (2048, SSD_HEADS), 1)
        return [dys, dz], [dnw, _hnn(jnp.broadcast_to(dde, (8, 2048)), (hh == jj).astype(F32))[0:1]]

    (dys, dz), (g_nw, g_d) = _matmul_rows("ssd_out_dx_gate_norm_bwd", dy, w_out, "nt", 256, dy.shape[1], fn,
                                          [y, (xbc, 2048, 0), z], [d_e, nw], [(2048, F32), (2048, BF16)],
                                          [(1, 2048), (1, SSD_HEADS)])
    return dys, dz, g_nw, g_d


def _ssd_tail_loss(y, xbc, z, d_e, snw, w_out, x1, tgt, gate, fnw):
    dm = x1.shape[1]
    si = y.shape[1]

    def make_u(y, xs, z, x1, tgt, d_e, snw, gate, fnw):
        return _gate_norm_fn(y, xs, z, d_e, snw).astype(BF16)

    def fn(y1, u, y, xs, z, x1, tgt, d_e, snw, gate, fnw):
        x2 = x1 + gate * y1
        r = lax.rsqrt(jnp.mean(x2 * x2, axis=-1, keepdims=True) + NORM_EPS)
        xh = x2 * r
        err = xh * fnw - tgt
        loss = 0.5 * jnp.sum(jnp.mean(err * err, axis=-1, keepdims=True), axis=0, keepdims=True)
        dy = err * (1.0 / dm)
        dxh = dy * fnw
        dx2 = r * (dxh - xh * jnp.mean(dxh * xh, axis=-1, keepdims=True))
        dfnw = jnp.sum(dy * xh, axis=0, keepdims=True)
        return [u, dx2, gate * dx2], [dfnw, jnp.sum(dx2 * y1, axis=0, keepdims=True), jnp.broadcast_to(loss, (1, 128))]

    (u, dx2, dy1), (g_fnw, dgate, loss) = _matmul_rows(
        "ssd_out_loss", make_u, w_out, "nn", 256, si, fn, [y, (xbc, si, 0), z, x1, tgt], [d_e, snw, gate, fnw],
        [(si, BF16), (dm, F32), (dm, BF16)], [(1, dm), (1, dm), (1, 128)])
    return u, dx2, dy1, g_fnw, dgate, loss


def _softplus_fwd(dt_raw, bias):
    (dt,), _ = _rowwise("dt_softplus", lambda r, b: ([jax.nn.softplus(r + b)], []), [dt_raw], [bias],
                        [(dt_raw.shape[1], F32)], [], 512)
    return dt


def _softplus_bwd(ddt_f, ddt_b, dt_raw, bias):
    def fn(df, db, r, b):
        g = (df + db) * jax.nn.sigmoid(r + b)
        return [g], [jnp.sum(g, axis=0, keepdims=True)]

    w = dt_raw.shape[1]
    (g,), (gb,) = _rowwise("dt_softplus_bwd", fn, [ddt_f, ddt_b, dt_raw], [bias], [(w, BF16)], [(1, w)], 512)
    return g, gb


def _mod_part(c_all, mod_w):
    nl, _, ncol = mod_w.shape
    nb = c_all.shape[0]

    def body(c_ref, w_ref, o_ref):
        cond = _silu(c_ref[...])
        for i in range(nl):
            o_ref[i * nb:(i + 1) * nb, :] = _nn(cond, w_ref[i])

    return pl.pallas_call(body, name="mod_part", out_shape=jax.ShapeDtypeStruct((nl * nb, ncol), F32),
                          compiler_params=_params(None, VMEM_BIG))(c_all, mod_w)


def _mod_finish(mod_nb, mod_b, norm_w, tokens):
    nl, dm = norm_w.shape

    def body(a_ref, b_ref, nw_ref, *rest):
        tok_refs, o_refs = rest[:len(tokens)], rest[len(tokens):]
        tok = sum(t[0:1, 0:1] for t in tok_refs)
        for i in range(nl):
            for k in range(3):
                cols = slice(k * dm, (k + 1) * dm)
                o_refs[4 * i + k][...] = a_ref[i:i + 1, cols] + b_ref[i:i + 1, cols]
            o_refs[4 * i + 3][...] = nw_ref[i:i + 1, :] + tok

    rows = pl.pallas_call(body, name="mod_finish", out_shape=[jax.ShapeDtypeStruct((1, dm), F32)] * (4 * nl))(
        mod_nb, mod_b, norm_w, *tokens)
    return [rows[4 * i:4 * i + 4] for i in range(nl)]


def _mod_grad(c_all, dmod_sh):
    nl, nb, ncol = dmod_sh.shape
    dm = c_all.shape[1]

    def body(c_ref, d_ref, o_ref):
        cond = _silu(c_ref[...])
        for i in range(nl):
            o_ref[i] = _tn(cond, d_ref[i])

    return pl.pallas_call(body, name="mod_grad", out_shape=jax.ShapeDtypeStruct((nl, dm, ncol), F32),
                          compiler_params=_params(None, VMEM_BIG))(c_all, dmod_sh)


PACK_ROWS = 16
PACK_COLS = 1024


def _pack_small(rows, b64, a64s, d32, extra):
    nr, na = len(rows), len(a64s)

    def body(*refs):
        o_ref = refs[-1]
        o_ref[...] = jnp.zeros_like(o_ref)
        for i in range(nr):
            o_ref[i:i + 1, :] = refs[i][...]
        b_ref, a_refs, d_ref, e_ref = refs[nr], refs[nr + 1:nr + 1 + na], refs[nr + 1 + na], refs[nr + 2 + na]
        o_ref[nr:nr + 1, 0:64] = b_ref[:, 0:64]
        o_ref[nr:nr + 1, 64:128] = sum(a[:, 0:64] for a in a_refs)
        o_ref[nr:nr + 1, 128:160] = d_ref[...]
        o_ref[nr:nr + 1, 256:384] = e_ref[...]

    return pl.pallas_call(body, name="pack_small", out_shape=jax.ShapeDtypeStruct((PACK_ROWS, PACK_COLS), F32))(
        *rows, b64, *a64s, d32, extra)


def _pack_ssd_small(cw, cb, nw):
    def body(cw_ref, cb_ref, nw_ref, o_ref):
        o_ref[...] = jnp.zeros_like(o_ref)
        o_ref[0:5, :] = cw_ref[...]
        o_ref[5:6, :] = cb_ref[...]
        o_ref[6:7, 0:256] = nw_ref[...]

    return pl.pallas_call(body, name="pack_ssd_small", out_shape=jax.ShapeDtypeStruct((8, 512), F32))(cw, cb, nw)


def _sum_parts(p_ref):
    g = p_ref[0].astype(F32)
    for s in range(1, p_ref.shape[0]):
        g = g + p_ref[s].astype(F32)
    return g


def _adam_update(w, g, m, v):
    m2 = ADAM_B1 * m + (1.0 - ADAM_B1) * g
    v2 = ADAM_B2 * v + (1.0 - ADAM_B2) * (g * g)
    m_hat = m2 / (1.0 - ADAM_B1 ** ADAM_STEP)
    v_hat = v2 / (1.0 - ADAM_B2 ** ADAM_STEP)
    return -ADAM_LR * (m_hat / (jnp.sqrt(v_hat) + ADAM_EPS) + ADAM_WD * w), m2, v2


def _adamw_windows(name, parts, params, windows, extra=None):
    n = len(params)

    def body(p_ref, *rest):
        ins, outs = rest[:3 * n], rest[3 * n:]
        g = _sum_parts(p_ref)
        for pi, rows, cols, idx in windows:
            w_ref, m_ref, v_ref = ins[3 * pi:3 * pi + 3]
            gw = g[rows, cols]
            dw, m2, v2 = _adam_update(w_ref[idx], gw, m_ref[idx], v_ref[idx])
            for o_ref, val in zip(outs[4 * pi:4 * pi + 4], (gw, dw, m2, v2), strict=True):
                o_ref[idx] = val
        if extra is not None:
            outs[4 * n][...] = g[extra[0], extra[1]]

    out_shape = [jax.ShapeDtypeStruct(w.shape, F32) for (w, _, _) in params for _ in range(4)]
    if extra is not None:
        out_shape.append(jax.ShapeDtypeStruct((extra[0].stop - extra[0].start, extra[1].stop - extra[1].start), F32))
    res = pl.pallas_call(body, name=name, out_shape=out_shape)(parts, *[a for p in params for a in p])
    return [res[4 * i:4 * i + 4] for i in range(n)] + ([res[4 * n]] if extra is not None else [])


def _adamw(name, w, parts, m, v, tr, tc=None):
    r_, c_ = w.shape
    p_ = parts.shape[0]
    tr = min(tr, r_)
    tc = c_ if tc is None else tc
    assert r_ % tr == 0 and c_ % tc == 0

    def body(w_ref, p_ref, m_ref, v_ref, g_ref, d_ref, m2_ref, v2_ref):
        g = _sum_parts(p_ref)
        g_ref[...] = g
        d_ref[...], m2_ref[...], v2_ref[...] = _adam_update(w_ref[...], g, m_ref[...], v_ref[...])

    blk = pl.BlockSpec((tr, tc), lambda i, j: (i, j))
    return pl.pallas_call(
        body, name=name, grid=(r_ // tr, c_ // tc), out_shape=[jax.ShapeDtypeStruct((r_, c_), F32)] * 4,
        in_specs=[blk, pl.BlockSpec((p_, tr, tc), lambda i, j: (0, i, j)), blk, blk], out_specs=[blk] * 4,
        compiler_params=_params(("parallel", "parallel"), VMEM_BIG),
    )(w, parts, m, v)


def _dev_index(p):
    return 4 * p[0] + 2 * p[1] + p[2]


def _all_gather(name, xs, by_columns=()):
    n = len(xs)
    hbm = pl.BlockSpec(memory_space=pl.ANY)

    def body(*refs):
        x_refs, o_refs = refs[:n], refs[n:2 * n]
        send_sems, recv_sems, local_sems = refs[2 * n:]
        x, y, c = lax.axis_index("x"), lax.axis_index("y"), lax.axis_index("c")
        me, sibling = (x, y, c), (x, y, 1 - c)
        chips = [(1 - x, y), (x, 1 - y), (1 - x, 1 - y)]

        def place(a, block):
            if a in by_columns:
                width = x_refs[a].shape[1]
                return o_refs[a].at[:, pl.ds(pl.multiple_of(_dev_index(block) * width, 128), width)]
            return o_refs[a].at[_dev_index(block)]

        def copy(a, k, block, to, src=None):
            dst = place(a, block)
            return pltpu.make_async_remote_copy(
                src_ref=dst if src is None else src, dst_ref=dst, send_sem=send_sems.at[a, k],
                recv_sem=recv_sems.at[a, k], device_id=to, device_id_type=MESH)

        mine = [pltpu.make_async_copy(x_refs[a], place(a, me), local_sems.at[a]) for a in range(n)]
        for cp in mine:
            cp.start()
        first = []
        for a in range(n):
            first.append(copy(a, 0, me, sibling, src=x_refs[a]))
            first += [copy(a, 1 + j, me, (*chip, c), src=x_refs[a]) for j, chip in enumerate(chips)]
        for cp in first:
            cp.start()
        passed = []
        for j, chip in enumerate(chips):
            for a in range(n):
                copy(a, 1 + j, (*chip, c), me).wait_recv()
                cp = copy(a, 4 + j, (*chip, c), sibling)
                cp.start()
                passed.append(cp)
        for a in range(n):
            copy(a, 0, sibling, me).wait_recv()
            for j, chip in enumerate(chips):
                copy(a, 4 + j, (*chip, 1 - c), me).wait_recv()
        for cp in first + passed:
            cp.wait_send()
        for cp in mine:
            cp.wait()

    shapes = [(x.shape[0], NDEV * x.shape[1]) if a in by_columns else (NDEV, *x.shape) for a, x in enumerate(xs)]
    return pl.pallas_call(
        body, name=name, out_shape=[jax.ShapeDtypeStruct(s, x.dtype) for s, x in zip(shapes, xs)],
        in_specs=[hbm] * n, out_specs=[hbm] * n,
        scratch_shapes=[pltpu.SemaphoreType.DMA((n, 7)), pltpu.SemaphoreType.DMA((n, 7)), pltpu.SemaphoreType.DMA((n,))],
    )(*xs)


_HBM = pl.BlockSpec(memory_space=pltpu.HBM)
_SEM = pl.BlockSpec(memory_space=pltpu.SEMAPHORE)
_EFFECT = pltpu.SideEffectType.DATAFLOW_SIDE_EFFECTING


def _mesh_position():
    return lax.axis_index("x"), lax.axis_index("y"), lax.axis_index("c")


def _peers(me):
    return [(k, tuple(1 - v if (k >> b) & 1 else v for v, b in zip(me, (2, 1, 0)))) for k in range(1, NDEV)]


EXCHANGE_COPIES = {"gather": NDEV - 1, "scatter": NDEV - 1, "pair": 4, "chips": 3}
NCHIP = NDEV // 2


def _landing_zones(name, xs, mode):
    x_, y_, c_ = _mesh_position()
    mine = (2 * x_ + y_ if mode == "chips" else _dev_index((x_, y_, c_))).astype(jnp.int32).reshape(1)
    lands = []
    for a, x in enumerate(xs):
        rows, cols = x.shape[-2:]
        if mode == "pair":
            lands.append(lax.empty((NCHIP, rows, cols), x.dtype))
            continue
        tr = 256 if rows % 256 == 0 else rows

        def body(me_ref, x_ref, o_ref):
            o_ref[...] = x_ref[...]

        if mode == "gather":
            in_spec = pl.BlockSpec((tr, cols), lambda i, me_ref: (i, 0))
        else:
            in_spec = pl.BlockSpec((None, tr, cols), lambda i, me_ref: (me_ref[0], i, 0))
        lands.append(pl.pallas_call(
            body, name=f"{name}_{a}",
            out_shape=jax.ShapeDtypeStruct((NCHIP if mode == "chips" else NDEV, rows, cols), x.dtype),
            grid_spec=pltpu.PrefetchScalarGridSpec(
                num_scalar_prefetch=1, grid=(rows // tr,), in_specs=[in_spec],
                out_specs=pl.BlockSpec((None, tr, cols), lambda i, me_ref: (me_ref[0], i, 0))),
            compiler_params=_params(("arbitrary",)),
        )(mine, x))
    return lands


def _exchange_copies(x_refs, land_refs, send_sems, recv_sems, mode):
    x_, y_, c_ = me = _mesh_position()
    per_array = EXCHANGE_COPIES[mode]
    out = []

    def add(a, k, src, dst, peer):
        sem = a * per_array + k
        out.append(pltpu.make_async_remote_copy(src_ref=src, dst_ref=dst, send_sem=send_sems.at[sem], recv_sem=recv_sems.at[sem],
                                                device_id=peer, device_id_type=MESH))

    for a, (x_ref, land_ref) in enumerate(zip(x_refs, land_refs)):
        if mode in ("gather", "scatter"):
            for k, peer in _peers(me):
                add(a, k - 1, x_ref.at[_dev_index(peer)] if mode == "scatter" else x_ref, land_ref.at[_dev_index(me)], peer)
        elif mode == "pair":
            for chip in range(NCHIP):
                add(a, chip, x_ref.at[2 * chip + 1 - c_], land_ref.at[chip], (x_, y_, 1 - c_))
        else:
            for k in range(1, NCHIP):
                px, py = (1 - x_ if k & 2 else x_), (1 - y_ if k & 1 else y_)
                add(a, k - 1, x_ref.at[2 * px + py], land_ref.at[2 * x_ + y_], (px, py, c_))
    return out


def _exchange_start(name, xs, lands, mode, dep):
    n = len(xs)

    def body(*refs):
        x_refs, land_refs = refs[:n], refs[n:2 * n]
        send_sems, recv_sems = refs[2 * n + 1], refs[2 * n + 2]
        token = refs[-1]
        for cp in _exchange_copies(x_refs, land_refs, send_sems, recv_sems, mode):
            cp.start()
        token[...] = jnp.zeros_like(token)

    sems = pltpu.SemaphoreType.DMA((n * EXCHANGE_COPIES[mode],))
    res = pl.pallas_call(
        body, name=name,
        out_shape=(sems, sems, *[pltpu.HBM(a.shape, a.dtype) for a in (*xs, *lands)], jax.ShapeDtypeStruct((8, 128), F32)),
        in_specs=[_HBM] * (2 * n) + [pl.BlockSpec(memory_space=pl.ANY)],
        out_specs=(_SEM, _SEM, *[_HBM] * (2 * n), pl.BlockSpec(memory_space=pltpu.VMEM)),
        input_output_aliases={i: 2 + i for i in range(2 * n)},
        compiler_params=pltpu.CompilerParams(has_side_effects=_EFFECT),
    )(*[pltpu.with_memory_space_constraint(a, pltpu.HBM) for a in (*xs, *lands)], dep)
    return res[:-1], res[-1]


def _exchange_wait(name, handles, mode, after, with_sources=False):
    send_sems, recv_sems = handles[0], handles[1]
    bufs = handles[2:]
    n = len(bufs) // 2

    def body(*refs):
        x_refs, land_refs = refs[:n], refs[n:2 * n]
        s_sems, r_sems = refs[2 * n], refs[2 * n + 1]
        for cp in _exchange_copies(x_refs, land_refs, s_sems, r_sems, mode):
            cp.wait_send()
            cp.wait_recv()

    res = pl.pallas_call(
        body, name=name, out_shape=tuple(pltpu.HBM(a.shape, a.dtype) for a in bufs),
        in_specs=[_HBM] * (2 * n) + [_SEM, _SEM, pl.BlockSpec(memory_space=pl.ANY)], out_specs=tuple([_HBM] * (2 * n)),
        input_output_aliases={i: i for i in range(2 * n)},
        compiler_params=pltpu.CompilerParams(has_side_effects=_EFFECT),
    )(*bufs, send_sems, recv_sems, after)
    return (res[n:], res[:n]) if with_sources else res[n:]


def _pair_sum(name, x, from_sibling):
    _, rows, cols = x.shape
    tr = rows
    core = lax.axis_index("c").astype(jnp.int32).reshape(1)

    def body(c_ref, x_ref, s_ref, o_ref):
        o_ref[...] = (x_ref[...].astype(F32) + s_ref[...].astype(F32)).astype(o_ref.dtype)

    return pl.pallas_call(
        body, name=name, out_shape=jax.ShapeDtypeStruct((NCHIP, rows, cols), x.dtype),
        grid_spec=pltpu.PrefetchScalarGridSpec(
            num_scalar_prefetch=1, grid=(NCHIP, rows // tr),
            in_specs=[pl.BlockSpec((None, tr, cols), lambda j, i, c_ref: (2 * j + c_ref[0], i, 0)),
                      pl.BlockSpec((None, tr, cols), lambda j, i, c_ref: (j, i, 0))],
            out_specs=pl.BlockSpec((None, tr, cols), lambda j, i, c_ref: (j, i, 0))),
        compiler_params=_params(("parallel", "parallel")),
    )(core, x, from_sibling)


def kernel(x, c, positions, norm_w, mod_w, mod_b, attn_w_in, attn_w_out, ssd_w_in, ssd_conv_w, ssd_conv_b, ssd_dt_bias, ssd_a_log, ssd_d, ssd_norm_w, ssd_w_out, final_norm_w, loss_target, m_norm_w, m_mod_w, m_mod_b, m_attn_w_in, m_attn_w_out, m_ssd_w_in, m_ssd_conv_w, m_ssd_conv_b, m_ssd_dt_bias, m_ssd_a_log, m_ssd_d, m_ssd_norm_w, m_ssd_w_out, m_final_norm_w, v_norm_w, v_mod_w, v_mod_b, v_attn_w_in, v_attn_w_out, v_ssd_w_in, v_ssd_conv_w, v_ssd_conv_b, v_ssd_dt_bias, v_ssd_a_log, v_ssd_d, v_ssd_norm_w, v_ssd_w_out, v_final_norm_w):
    s_len, dm = x.shape[1], x.shape[2]
    me = 4 * lax.axis_index("x") + 2 * lax.axis_index("y") + lax.axis_index("c")
    x0 = x.reshape(s_len, dm)
    tgt = loss_target.reshape(s_len, dm)
    aw = 3 * 512
    si = 2 * dm
    sxbc = 2 * si
    n_ssd_in = ssd_w_in.shape[2] * NDEV

    w_ai, c_all = _all_gather("gather_attn_w_in", [attn_w_in[0].astype(BF16), c], by_columns=(0,))
    wcol = attn_w_in.shape[2]
    c_all = c_all.reshape(NDEV, dm)

    part = _mod_part(c_all, mod_w)
    (part_all,) = _all_gather("gather_mod", [part])
    mod_nb = jnp.stack([lax.dynamic_index_in_dim(part_all, i * NDEV + me, axis=1, keepdims=False).reshape(3 * dm)
                        for i in range(2)])

    ssd_small = _pack_ssd_small(ssd_conv_w[0], ssd_conv_b, ssd_norm_w)
    ao_shard = [attn_w_out[0].astype(BF16)]
    ao_handles, ao_token = _exchange_start("w_out_start", ao_shard, _landing_zones("w_out_place", ao_shard, "gather"), "gather",
                                           part_all)
    late_shards = [ssd_w_in[0].T.astype(BF16), ssd_w_out[0].astype(BF16), ssd_small]
    w_handles, w_token = _exchange_start("weights_start", late_shards, _landing_zones("weights_place", late_shards, "gather"),
                                         "gather", ao_token)
    (shift0, scale0, gate0, nw0), (shift1, scale1, gate1, nw1) = _mod_finish(mod_nb, mod_b, norm_w, [ao_token, w_token])
    shift, scale, gate, nw = [shift0, shift1], [scale0, scale1], [gate0, gate1], [nw0, nw1]

    hn0 = _norm_mod_fwd("norm0", x0, nw[0], scale[0], shift[0])
    inv_freq = ROPE_THETA ** (-jnp.arange(0, ROT_DIM, 2, dtype=F32) / ROT_DIM)
    per_head = jnp.concatenate([inv_freq, inv_freq, jnp.zeros(HEAD_DIM - ROT_DIM, F32)])
    inv_row = jnp.tile(per_head, 128 // HEAD_DIM).reshape(1, 128)
    tabs = _rope_tables(positions.reshape(s_len, 1), inv_row)
    qk = _matmul("proj_qk", hn0, w_ai, "nn", F32, MM_T, MM_T, dm, epilogue=_rot_fwd, mrows=tabs, n_out=2 * aw)
    v = _matmul("proj_vz", hn0, w_ai, "nn", F32, MM_T, MM_T, dm, b_noff=2 * aw, n_out=2 * aw)
    z0 = (v, 1)
    att = [_attn_fwd(g, qk, v) for g in range(3)]
    os_, lses = [a[0] for a in att], [a[1] for a in att]
    (g_ao,) = _exchange_wait("w_out_wait", ao_handles, "gather", lses[2])
    a0, y0, x1 = _attn_out(os_, lses, z0, x0, gate[0], g_ao.reshape(aw, dm))

    hn1 = _norm_mod_fwd("norm1", x1, nw[1], scale[1], shift[1])
    g_si, g_so, g_small = _exchange_wait("weights_wait", w_handles, "gather", hn1)
    w_ao = g_ao.reshape(aw, dm)
    w_si_t = g_si.reshape(n_ssd_in, dm)
    w_so = g_so.reshape(si, dm)
    conv_w = g_small[:, 0:CONV_WIDTH, :].transpose(1, 0, 2).reshape(CONV_WIDTH, sxbc)
    conv_b = g_small[:, 5, :].reshape(1, sxbc)
    snw = g_small[:, 6, 0:si // NDEV].reshape(1, si)
    ndt = 2 * SSD_HEADS
    z1 = _matmul("ssd_proj_z", hn1, w_si_t, "nt", F32, MM_T, MM_T, dm, n_out=si)
    xpre = _matmul("ssd_proj_xbc", hn1, w_si_t, "nt", F32, MM_T, MM_T, dm, b_noff=si, n_out=sxbc)
    dt_raw = _matmul("ssd_proj_dt", hn1, w_si_t, "nt", F32, MM_T, ndt, dm, b_noff=si + sxbc, n_out=ndt)
    xbc = _conv_fwd(xpre, conv_w, conv_b)
    widen = lambda a: jnp.pad(a, ((0, 0), (0, SSD_DTW - ndt)))
    dt_raw = widen(dt_raw)
    dt_bias = widen(ssd_dt_bias.reshape(1, ndt))
    alog = widen(ssd_a_log.reshape(1, ndt))
    dt = _softplus_fwd(dt_raw, dt_bias)
    y_f, st_f = _ssd_fwd(xbc, dt, alog, 0)
    y_fb, st_b = _ssd_fwd(xbc, dt, alog, 1, prior=y_f)
    d_e = jnp.repeat(ssd_d.reshape(SSD_HEADS), HEAD_DIM).reshape(1, si)

    fnw = final_norm_w.reshape(1, dm)
    u, dx2, dy1, g_fnw, dgate1, loss_part = _ssd_tail_loss(y_fb, xbc, z1, d_e, snw, w_so, x1, tgt, gate[1], fnw)
    gw_so = _matmul("ssd_out_dw", u, dy1, "tn", BF16, MM_T, MM_T, MM_T)
    dys, dz1, g_snw, g_d = _gate_norm_bwd(dy1, w_so, y_fb, xbc, z1, d_e, snw)
    dxbc_f, ddt_f, dalog_f = _ssd_bwd(xbc, dt, alog, st_f, dys, d_e, 0)
    dxbc, ddt_b, dalog_b = _ssd_bwd(xbc, dt, alog, st_b, dys, d_e, 1, prior=dxbc_f)
    dpre, g_cw, g_cb = _conv_bwd(xpre, dxbc, conv_w, conv_b)
    ddt_raw, g_dtb = _softplus_bwd(ddt_f, ddt_b, dt_raw, dt_bias)
    ddt_raw = ddt_raw[:, :ndt]
    dhn1 = [_matmul("ssd_proj_z_dx", dz1, w_si_t, "nn", F32, MM_T, MM_T, MM_T),
            _matmul("ssd_proj_xbc_dx", dpre, w_si_t, "nn", F32, MM_T, MM_T, MM_T, b_koff=si)]
    gw_si_t = _matmul("ssd_proj_z_dw", dz1, hn1, "tn", BF16, MM_T, MM_T, MM_T, dest=(n_ssd_in, 0, None))
    gw_si_t = _matmul("ssd_proj_xbc_dw", dpre, hn1, "tn", BF16, MM_T, MM_T, MM_T, dest=(n_ssd_in, si, gw_si_t))
    gw_si_t = _matmul("ssd_proj_dt_dw", ddt_raw, hn1, "tn", BF16, ndt, MM_T, MM_T, dest=(n_ssd_in, si + sxbc, gw_si_t))

    l1_grads = [gw_so.reshape(NDEV, si // NDEV, dm), gw_si_t.reshape(NDEV, n_ssd_in // NDEV, dm),
                _pack_ssd_small_blocks(g_cw, g_cb, g_snw)]
    l1_handles, l1_token = _exchange_start("l1_grads_start", l1_grads, _landing_zones("l1_grads_place", l1_grads, "scatter"),
                                           "scatter", dhn1[1])
    dx1, dy0, g_nw1, dsc1, dsh1, dgate0 = _norm_mod_bwd(
        "ssd_proj_dt_dx_norm1_bwd", (ddt_raw, w_si_t, "nn", ndt, dict(b_koff=si + sxbc)), x1, dhn1, dx2,
        nw[1], scale[1], shift[1], prev=(y0, gate[0] + l1_token[0:1, 0:1]))

    gw_ao = _matmul("attn_out_dw", a0, dy0, "tn", BF16, aw // 2, MM_T, MM_T)
    dos, dls, dz0 = _mix_bwd(dy0, w_ao, os_, lses, z0)
    datt = [_attn_bwd(g, qk, v, os_[g], lses[g], dos[g], dls[g]) for g in range(3)]
    dqkv = _rot_pack_bwd([t[0] for t in datt], [t[1] for t in datt], [t[2] for t in datt], tabs)
    gw_ai = _matmul("proj_qkv_dw", hn0, dqkv, "tn", BF16, MM_T, wcol, MM_T, out_blocks=3 * aw // wcol, dest=(NDEV, 0, None))
    gw_ai = _matmul("proj_z_dw", hn0, dz0, "tn", BF16, MM_T, wcol, MM_T, out_blocks=aw // wcol,
                    dest=(NDEV, 3 * aw // wcol, gw_ai))
    after_start = lambda acc, t: acc + t
    zero_row = lambda token: jnp.tile(token[0:1], (1, dm // 128))
    l0_grads = [gw_ai, gw_ao.reshape(NDEV, aw // NDEV, dm)]
    pair_handles, pair_token = _exchange_start("l0_pair_start", l0_grads, _landing_zones("l0_pair_place", l0_grads, "pair"),
                                               "pair", dqkv)
    dhn0_z = _matmul("proj_z_dx", dz0, w_ai, "nt", F32, MM_T, MM_T, aw, b_koff=3 * aw, n_out=dm, epilogue=after_start,
                     ncols=(zero_row(pair_token),))
    from_sibling, l0_grads = _exchange_wait("l0_pair_wait", pair_handles, "pair", dhn0_z, with_sources=True)
    chip_sums = [_pair_sum(f"l0_pair_sum_{a}", g, s) for a, (g, s) in enumerate(zip(l0_grads, from_sibling))]
    l0_handles, l0_token = _exchange_start("l0_grads_start", chip_sums, _landing_zones("l0_grads_place", chip_sums, "chips"),
                                           "chips", dhn0_z)
    dx0, g_nw0, dsc0, dsh0 = _norm_mod_bwd(
        "proj_qkv_dx_norm0_bwd", (dqkv, w_ai, "nt", aw, dict(n_out=dm)), x0, [dhn0_z], dx1,
        nw[0], scale[0], shift[0] + zero_row(l0_token))

    small_g = [_pack_small([dsh0, dsc0, dgate0, dsh1, dsc1, dgate1, g_nw0, g_nw1, g_fnw], g_dtb, [dalog_f, dalog_b], g_d, loss_part)]
    sm_handles, sm_token = _exchange_start("small_grads_start", small_g, _landing_zones("small_grads_place", small_g, "gather"),
                                           "gather", dx0)

    whole = (slice(None), slice(None))
    r_so, r_si, r_small = _exchange_wait("l1_grads_wait", l1_handles, "scatter", sm_token)
    si_out = [o.T for o in _adamw("adamw_ssd_w_in", ssd_w_in[0].T, r_si, m_ssd_w_in[0].T, v_ssd_w_in[0].T, n_ssd_in // NDEV, 256)]
    so_out = _adamw("adamw_ssd_w_out", ssd_w_out[0], r_so, m_ssd_w_out[0], v_ssd_w_out[0], 256)
    cw_cols = ssd_conv_w.shape[2]
    cw_out, cb_out, snw_out = _adamw_windows(
        "adamw_ssd_small", r_small,
        [(ssd_conv_w, m_ssd_conv_w, v_ssd_conv_w), (ssd_conv_b, m_ssd_conv_b, v_ssd_conv_b),
         (ssd_norm_w, m_ssd_norm_w, v_ssd_norm_w)],
        [(0, slice(0, CONV_WIDTH), slice(0, cw_cols), (0, slice(None), slice(None))),
         (1, slice(5, 6), slice(0, cw_cols), whole), (2, slice(6, 7), slice(0, si // NDEV), whole)])
    r_ai, r_ao = _exchange_wait("l0_grads_wait", l0_handles, "chips", so_out[0])
    ai_out = _adamw("adamw_attn_w_in", attn_w_in[0], r_ai, m_attn_w_in[0], v_attn_w_in[0], 256)
    ao_out = _adamw("adamw_attn_w_out", attn_w_out[0], r_ao, m_attn_w_out[0], v_attn_w_out[0], 192)

    (small_all,) = _exchange_wait("small_grads_wait", sm_handles, "gather", ai_out[0])
    full = slice(0, PACK_COLS)
    nhd = SSD_HEADS
    windows = [(0, slice(3 * i + k, 3 * i + k + 1), full, (slice(i, i + 1), slice(k * dm, (k + 1) * dm)))
               for i in range(2) for k in range(3)]
    windows += [(1, slice(6 + i, 7 + i), full, (slice(i, i + 1), slice(None))) for i in range(2)]
    windows += [(2, slice(8, 9), full, whole)]
    windows += [(3 + q, slice(9, 10), slice(2 * nhd * q + nhd * j, 2 * nhd * q + nhd * (j + 1)), (0, slice(j, j + 1), slice(None)))
                for q in range(2) for j in range(2)]
    windows += [(5, slice(9, 10), slice(4 * nhd, 5 * nhd), whole)]
    as_row = lambda a: a.reshape(1, dm)
    mb_out, nw_out, fnw_out, dtb_out, alog_out, d_out, loss = _adamw_windows(
        "adamw_small", small_all,
        [(mod_b, m_mod_b, v_mod_b), (norm_w, m_norm_w, v_norm_w), (fnw, as_row(m_final_norm_w), as_row(v_final_norm_w)),
         (ssd_dt_bias, m_ssd_dt_bias, v_ssd_dt_bias), (ssd_a_log, m_ssd_a_log, v_ssd_a_log), (ssd_d, m_ssd_d, v_ssd_d)],
        windows, extra=(slice(9, 10), slice(256, 257)))
    loss = loss.reshape(())

    ncol = mod_w.shape[2]
    dmod_all = small_all[:, 0:6, :].reshape(NDEV, 2, 3 * dm)
    dmod_sh = lax.dynamic_slice_in_dim(dmod_all, me * ncol, ncol, axis=2).transpose(1, 0, 2)
    g_modw = _mod_grad(c_all, dmod_sh).reshape(1, 2 * dm, ncol)
    modw_out = _adamw("adamw_mod_w", mod_w.reshape(2 * dm, ncol), g_modw, m_mod_w.reshape(2 * dm, ncol),
                      v_mod_w.reshape(2 * dm, ncol), 256)

    per_kind = []
    for k in range(4):
        per_kind.append([
            nw_out[k], modw_out[k].reshape(mod_w.shape), mb_out[k], ai_out[k][None], ao_out[k][None], si_out[k][None],
            cw_out[k], cb_out[k], dtb_out[k], alog_out[k], d_out[k], snw_out[k], so_out[k][None], fnw_out[k].reshape(dm)])
    return (loss, dx0.reshape(x.shape), *per_kind[0], *per_kind[1], *per_kind[2], *per_kind[3])


def _pack_ssd_small_blocks(g_cw, g_cb, g_nw):
    nper = g_cw.shape[1] // NDEV
    nwper = g_nw.shape[1] // NDEV

    def body(cw_ref, cb_ref, nw_ref, o_ref):
        o_ref[...] = jnp.zeros_like(o_ref)
        for d in range(NDEV):
            o_ref[d, 0:5, :] = cw_ref[:, d * nper:(d + 1) * nper]
            o_ref[d, 5:6, :] = cb_ref[:, d * nper:(d + 1) * nper]
            o_ref[d, 6:7, 0:nwper] = nw_ref[:, d * nwper:(d + 1) * nwper]

    return pl.pallas_call(body, name="pack_ssd_small_grads", out_shape=jax.ShapeDtypeStruct((NDEV, 8, nper), F32))(g_cw, g_cb, g_nw)
```

```python
import functools
import math

import jax
import jax.numpy as jnp
from jax import lax
from jax.experimental import pallas as pl
from jax.experimental.pallas import tpu as pltpu

F32 = jnp.float32
BF16 = jnp.bfloat16
HI = lax.Precision.HIGHEST
MESH = pl.DeviceIdType.MESH
NDEV = 8

NORM_EPS = 1e-6
ROPE_THETA = 500000.0
ROT_DIM = 16
HEAD_DIM = 64
DILATIONS = (1, 4, 16)
BAND = 64
NEG_BIG = -1e30
CHUNK = 128
SSD_HEADS = 32
SSD_GROUPS = 8
CONV_WIDTH = 5

ADAM_LR = 0.001
ADAM_B1 = 0.9
ADAM_B2 = 0.999
ADAM_EPS = 1e-08
ADAM_WD = 0.01
ADAM_STEP = 10

VMEM_BIG = 56 * 1024 * 1024
MM_T = 1024


def _params(sem=None, vmem=None):
    kw = {}
    if sem is not None:
        kw["dimension_semantics"] = sem
    if vmem is not None:
        kw["vmem_limit_bytes"] = vmem
    return pltpu.CompilerParams(**kw)


def _dg(a, b, ca, cb, prec=None):
    return lax.dot_general(a, b, (((ca,), (cb,)), ((), ())), preferred_element_type=F32, precision=prec)


def _nn(a, b):
    return _dg(a.astype(BF16), b.astype(BF16), 1, 0)


def _nt(a, b):
    return _dg(a.astype(BF16), b.astype(BF16), 1, 1)


def _tn(a, b):
    return _dg(a.astype(BF16), b.astype(BF16), 0, 0)


def _hnn(a, b):
    return _dg(a, b, 1, 0, HI)


@jax.custom_vjp
def _bnn(a, b):
    return _nn(a, b)


_bnn.defvjp(lambda a, b: (_nn(a, b), (a, b)), lambda r, g: (_nt(g, r[1]), _tn(r[0], g)))


@jax.custom_vjp
def _bnt(a, b):
    return _nt(a, b)


_bnt.defvjp(lambda a, b: (_nt(a, b), (a, b)), lambda r, g: (_nn(g, r[1]), _tn(g, r[0])))


@jax.custom_vjp
def _btn(a, b):
    return _tn(a, b)


_btn.defvjp(lambda a, b: (_tn(a, b), (a, b)), lambda r, g: (_nt(r[1], g), _nn(r[0], g)))


def _silu(x):
    return x * jax.nn.sigmoid(x)


def _b_spec(b, mode, tn, tk, no, ko, jk):
    if mode == "nt":
        return pl.BlockSpec((tn, tk), lambda *g: (jk(*g)[0] + no, jk(*g)[1] + ko))
    return pl.BlockSpec((tk, tn), lambda *g: (jk(*g)[1] + ko, jk(*g)[0] + no))


def _matmul(name, a, b, mode, out_dtype, tm, tn, tk, *, epilogue=None, tiled=(), mrows=(), ncols=(),
            b_noff=0, b_koff=0, n_out=None, out_blocks=None, dest=None):
    if mode == "tn":
        K, M = a.shape
    else:
        M, K = a.shape
    N = n_out if n_out is not None else (b.shape[0] if mode == "nt" else b.shape[1])
    tm, tn, tk = min(tm, M), min(tn, N), min(tk, K)
    assert M % tm == 0 and N % tn == 0 and K % tk == 0, (name, M, N, K, tm, tn, tk)
    assert b_noff % tn == 0 and b_koff % tk == 0
    no, ko = b_noff // tn, b_koff // tk
    nk = K // tk
    if mode == "tn":
        a_spec = pl.BlockSpec((tk, tm), lambda i, j, k: (k, i))
    else:
        a_spec = pl.BlockSpec((tm, tk), lambda i, j, k: (i, k))
    specs = [a_spec, _b_spec(b, mode, tn, tk, no, ko, lambda i, j, k: (j, k))]
    specs += [pl.BlockSpec((tm, tn), lambda i, j, k: (i, j)) for _ in tiled]
    specs += [pl.BlockSpec((tm, r.shape[1]), lambda i, j, k: (i, 0)) for r in mrows]
    specs += [pl.BlockSpec((1, tn), lambda i, j, k: (0, j)) for _ in ncols]
    total, off, earlier = dest if dest is not None else (None, 0, None)
    if out_blocks is None:
        assert off % tm == 0
        mo = off // tm
        out_shape = jax.ShapeDtypeStruct((M if total is None else total, N), out_dtype)
        out_spec = pl.BlockSpec((tm, tn), lambda i, j, k: (i + mo, j))
    else:
        nper = N // out_blocks
        assert nper % tn == 0
        jb = nper // tn
        out_shape = jax.ShapeDtypeStruct((out_blocks if total is None else total, M, nper), out_dtype)
        out_spec = pl.BlockSpec((None, tm, tn), lambda i, j, k: (j // jb + off, i, j % jb))
    if earlier is not None:
        assert earlier.shape == out_shape.shape and earlier.dtype == out_shape.dtype
    ne = len(tiled) + len(mrows) + len(ncols)
    dot = {"nn": _nn, "nt": _nt, "tn": _tn}[mode]

    def body(a_ref, b_ref, *rest):
        extras, o_ref = rest[:ne], rest[ne]

        def finish(acc):
            if epilogue is not None:
                acc = epilogue(acc, *[e[...] for e in extras])
            o_ref[...] = acc.astype(o_ref.dtype)

        if nk == 1:
            finish(dot(a_ref[...], b_ref[...]))
        else:
            acc_ref = rest[ne + 1]
            k = pl.program_id(2)

            @pl.when(k == 0)
            def _():
                acc_ref[...] = jnp.zeros_like(acc_ref)

            acc_ref[...] += dot(a_ref[...], b_ref[...])

            @pl.when(k == nk - 1)
            def _():
                finish(acc_ref[...])

    args = [a, b, *tiled, *mrows, *ncols]
    aliases = {}
    if earlier is not None:
        specs.append(pl.BlockSpec(memory_space=pl.ANY))
        aliases = {len(args): 0}
        args.append(earlier)

    def body_with_dest(*refs):
        body(*refs[:2 + ne], *refs[2 + ne + (earlier is not None):])

    return pl.pallas_call(
        body_with_dest, name=name, out_shape=out_shape, grid=(M // tm, N // tn, nk),
        in_specs=specs, out_specs=out_spec, input_output_aliases=aliases,
        scratch_shapes=[] if nk == 1 else [pltpu.VMEM((tm, tn), F32)],
        compiler_params=_params(("parallel", "parallel", "arbitrary"), VMEM_BIG),
    )(*args)


def _matmul_rows(name, a, b, mode, tm, tk, fn, rows, consts, outs, accs, *, n_out=None, b_noff=0, b_koff=0):
    rl = [(t, t.shape[1], 0) if not isinstance(t, tuple) else t for t in rows]
    make_a = a if callable(a) else None
    M, K = (rl[0][0].shape[0], b.shape[1 if mode == "nt" else 0]) if make_a else a.shape
    N = n_out if n_out is not None else (b.shape[0] if mode == "nt" else b.shape[1])
    tm, tk = min(tm, M), min(tk, K)
    assert M % tm == 0 and K % tk == 0 and b_koff % tk == 0 and b_noff % N == 0, (name, M, N, K)
    no, ko, nk = b_noff // N, b_koff // tk, K // tk
    assert make_a is None or nk == 1
    nr, nc, no_, na = len(rl), len(consts), len(outs), len(accs)
    dot = _nt if mode == "nt" else _nn

    def body(*refs):
        a_ref, b_ref, rest = (None, refs[0], refs[1:]) if make_a else (refs[0], refs[1], refs[2:])
        r_refs, c_refs = rest[:nr], rest[nr:nr + nc]
        o_refs, acc_refs = rest[nr + nc:nr + nc + no_], rest[nr + nc + no_:nr + nc + no_ + na]
        i, k = pl.program_id(0), pl.program_id(1)

        def finish(prod, *made):
            res_o, res_a = fn(prod, *made, *[r[...] for r in r_refs], *[c[...] for c in c_refs])
            for r, v in zip(o_refs, res_o, strict=True):
                r[...] = v.astype(r.dtype)
            if acc_refs:
                @pl.when(i == 0)
                def _():
                    for r in acc_refs:
                        r[...] = jnp.zeros_like(r)

                for r, v in zip(acc_refs, res_a, strict=True):
                    r[...] += v

        if make_a:
            left = make_a(*[r[...] for r in r_refs], *[c[...] for c in c_refs])
            finish(dot(left, b_ref[...]), left)
        elif nk == 1:
            finish(dot(a_ref[...], b_ref[...]))
        else:
            prod_ref = rest[-1]

            @pl.when(k == 0)
            def _():
                prod_ref[...] = jnp.zeros_like(prod_ref)

            prod_ref[...] += dot(a_ref[...], b_ref[...])

            @pl.when(k == nk - 1)
            def _():
                finish(prod_ref[...])

    b_spec = _b_spec(b, mode, N, tk, no, ko, lambda i, k: (0, k))
    in_specs = ([] if make_a else [pl.BlockSpec((tm, tk), lambda i, k: (i, k))]) + [b_spec]
    in_specs += [pl.BlockSpec((tm, w), functools.partial(lambda i, k, cb: (i, cb), cb=cb)) for (_, w, cb) in rl]
    in_specs += [pl.BlockSpec(c.shape, lambda i, k: (0, 0)) for c in consts]
    out_specs = [pl.BlockSpec((tm, c), lambda i, k: (i, 0)) for (c, _) in outs]
    out_specs += [pl.BlockSpec(shp, lambda i, k: (0, 0)) for shp in accs]
    out_shape = [jax.ShapeDtypeStruct((M, c), dt) for (c, dt) in outs] + [jax.ShapeDtypeStruct(shp, F32) for shp in accs]
    res = pl.pallas_call(
        body, name=name, out_shape=out_shape, grid=(M // tm, nk), in_specs=in_specs, out_specs=out_specs,
        scratch_shapes=[] if nk == 1 else [pltpu.VMEM((tm, N), F32)],
        compiler_params=_params(("arbitrary" if accs else "parallel", "arbitrary"), VMEM_BIG),
    )(*([] if make_a else [a]), b, *[t[0] for t in rl], *consts)
    return res[:no_], res[no_:]


def _rowwise(name, fn, tiled, consts, outs, accs, ts):
    tl = [(t, t.shape[1], 0) if not isinstance(t, tuple) else t for t in tiled]
    s_len = tl[0][0].shape[0]
    assert s_len % ts == 0
    nt_, nc_, no_ = len(tl), len(consts), len(outs)

    def body(*refs):
        t_refs, c_refs = refs[:nt_], refs[nt_:nt_ + nc_]
        o_refs, a_refs = refs[nt_ + nc_:nt_ + nc_ + no_], refs[nt_ + nc_ + no_:]
        res_o, res_a = fn(*[r[...] for r in t_refs], *[r[...] for r in c_refs])
        for r, v in zip(o_refs, res_o, strict=True):
            r[...] = v.astype(r.dtype)
        if a_refs:
            @pl.when(pl.program_id(0) == 0)
            def _():
                for r in a_refs:
                    r[...] = jnp.zeros_like(r)

            for r, v in zip(a_refs, res_a, strict=True):
                r[...] += v

    in_specs = [pl.BlockSpec((ts, w), functools.partial(lambda i, cb: (i, cb), cb=cb)) for (_, w, cb) in tl]
    in_specs += [pl.BlockSpec(c.shape, lambda i: (0, 0)) for c in consts]
    out_specs = [pl.BlockSpec((ts, c), lambda i: (i, 0)) for (c, _) in outs]
    out_specs += [pl.BlockSpec(shp, lambda i: (0, 0)) for shp in accs]
    out_shape = [jax.ShapeDtypeStruct((s_len, c), dt) for (c, dt) in outs]
    out_shape += [jax.ShapeDtypeStruct(shp, F32) for shp in accs]
    res = pl.pallas_call(
        body, name=name, out_shape=out_shape, grid=(s_len // ts,), in_specs=in_specs, out_specs=out_specs,
        compiler_params=_params(("arbitrary",) if accs else ("parallel",), VMEM_BIG),
    )(*[t[0] for t in tl], *consts)
    return res[:no_], res[no_:]


def _norm_mod_fn(x, nw, sc, sh):
    r = lax.rsqrt(jnp.mean(x * x, axis=-1, keepdims=True) + NORM_EPS)
    return (x * r * nw) * (1.0 + sc) + sh


def _norm_mod_fwd(name, x, nw, sc, sh):
    (hn,), _ = _rowwise(name, lambda x, nw, sc, sh: ([_norm_mod_fn(x, nw, sc, sh)], []),
                        [x], [nw, sc, sh], [(x.shape[1], BF16)], [], 512)
    return hn


def _norm_mod_bwd(name, last, x, dhn_parts, dres, nw, sc, sh, prev=None):
    n = len(dhn_parts)
    d = x.shape[1]
    a, b, mode, tk, kw = last

    def fn(dhn, x, *rest):
        for p in rest[:n]:
            dhn = dhn + p
        dres, rest = rest[n], rest[n + 1:]
        y_prev, (nw, sc, sh), gate = (rest[0], rest[1:4], rest[4]) if prev is not None else (None, rest[0:3], None)
        r = lax.rsqrt(jnp.mean(x * x, axis=-1, keepdims=True) + NORM_EPS)
        xh = x * r
        dxh = dhn * (nw * (1.0 + sc))
        dx = r * (dxh - xh * jnp.mean(dxh * xh, axis=-1, keepdims=True)) + dres
        along = jnp.sum(dhn * xh, axis=0, keepdims=True)
        dnw, dsc, dsh = along * (1.0 + sc), along * nw, jnp.sum(dhn, axis=0, keepdims=True)
        if prev is None:
            return [dx], [dnw, dsc, dsh]
        return [dx, gate * dx], [dnw, dsc, dsh, jnp.sum(dx * y_prev, axis=0, keepdims=True)]

    rows = [x, *dhn_parts, dres] + ([prev[0]] if prev is not None else [])
    consts = [nw, sc, sh] + ([prev[1]] if prev is not None else [])
    outs = [(d, F32)] + ([(d, BF16)] if prev is not None else [])
    res_o, res_a = _matmul_rows(name, a, b, mode, 512, tk, fn, rows, consts, outs, [(1, d)] * (3 + (prev is not None)), **kw)
    return (*res_o, *res_a)


def _rope_tables(pos_col, inv_row):
    def fn(pos, inv):
        ang = pos.astype(F32) * inv
        e = lax.broadcasted_iota(jnp.int32, (1, 128), 1) % HEAD_DIM
        cos, sin = jnp.cos(ang), jnp.sin(ang)
        half = ROT_DIM // 2
        return [jnp.where(e < ROT_DIM, cos, 1.0), jnp.where(e < half, -sin, 0.0),
                jnp.where((e >= half) & (e < ROT_DIM), sin, 0.0)], []

    (c, sa, sb), _ = _rowwise("rope_tables", fn, [pos_col], [inv_row], [(128, F32)] * 3, [], 512)
    return c, sa, sb


def _rot_fwd(t, c, sa, sb):
    n = t.shape[1]
    rep = n // 128
    c, sa, sb = (jnp.tile(u, (1, rep)) for u in (c, sa, sb))
    return t * c + pltpu.roll(t, n - ROT_DIM // 2, 1) * sa + pltpu.roll(t, ROT_DIM // 2, 1) * sb


def _rot_bwd(g, c, sa, sb):
    n = g.shape[1]
    rep = n // 128
    c, sa, sb = (jnp.tile(u, (1, rep)) for u in (c, sa, sb))
    return g * c + pltpu.roll(g * sa, ROT_DIM // 2, 1) + pltpu.roll(g * sb, n - ROT_DIM // 2, 1)


ATT_TQ = 128


def _attn_tiles(l):
    tk = ATT_TQ + 2 * BAND
    return (l, l) if l <= tk else (ATT_TQ, tk)


def _attn_specs(g, s_len):
    def blk(off):
        return pl.BlockSpec((s_len, 128), functools.partial(lambda hp, off: (0, off + hp), off=off))

    return blk(4 * g), blk(12 + 4 * g), blk(4 * g), blk(0)


def _attn_tile_geometry(t, d, l):
    tq, tk = _attn_tiles(l)
    nts = l // tq
    r = t // nts
    ts = t % nts
    q0 = ts * tq
    ws = jnp.clip(q0 - BAND, 0, l - tk)
    kind = jnp.where(ts == 0, 0, jnp.where(ts == nts - 1, 2, 1))
    if d == 1:
        return pl.ds(pl.multiple_of(q0, tq), tq), pl.ds(pl.multiple_of(ws, BAND), tk), kind
    return pl.ds(r + d * q0, tq, stride=d), pl.ds(r + d * ws, tk, stride=d), kind


def _attn_fill_bias(bias_ref):
    _, tq2, tk = bias_ref.shape
    iq = lax.broadcasted_iota(jnp.int32, (tq2, 1), 0) % (tq2 // 2)
    ik = lax.broadcasted_iota(jnp.int32, (1, tk), 1)
    for i, off in enumerate((0, -BAND, -2 * BAND)):
        bias_ref[i] = jnp.where(jnp.abs(ik + off - iq) <= BAND, 0.0, NEG_BIG)


def _split_heads(t, in_h):
    zero = jnp.zeros_like(t)
    return jnp.concatenate([jnp.where(in_h[0], t, zero), jnp.where(in_h[1], t, zero)], axis=0)


def _attn_fwd(g, qk, v):
    s_len = qk.shape[0]
    d = DILATIONS[g]
    l = s_len // d
    tq, tk = _attn_tiles(l)
    assert l % tq == 0 and l >= tk
    q_spec, k_spec, v_spec, o_spec = _attn_specs(g, s_len)
    scale = 1.0 / math.sqrt(HEAD_DIM)

    def body(q_ref, k_ref, v_ref, o_ref, lse_ref, bias_ref):
        lane = lax.broadcasted_iota(jnp.int32, (1, 128), 1)
        in_h = [lane < HEAD_DIM, lane >= HEAD_DIM]
        _attn_fill_bias(bias_ref)

        def tile(t, carry):
            rows, win, kind = _attn_tile_geometry(t, d, l)
            q = (q_ref[rows, :] * scale).astype(BF16)
            k = k_ref[win, :].astype(BF16)
            vv = v_ref[win, :].astype(BF16)
            s = _nt(_split_heads(q, in_h), k) + bias_ref[kind]
            m = jnp.max(s, axis=1, keepdims=True)
            p = jnp.exp(s - m)
            den = jnp.sum(p, axis=1, keepdims=True)
            out = _nn(p, vv) / den
            lse = m + jnp.log(den)
            o_ref[rows, :] = jnp.where(in_h[0], out[:tq], out[tq:])
            lse_ref[rows, :] = jnp.where(in_h[0], lse[:tq], lse[tq:])
            return carry

        lax.fori_loop(0, s_len // tq, tile, 0, unroll=8 * ATT_TQ // tq)

    return pl.pallas_call(
        body, name=f"attn_fwd_g{g}", grid=(4,),
        out_shape=[jax.ShapeDtypeStruct((s_len, 512), F32)] * 2,
        in_specs=[q_spec, k_spec, v_spec], out_specs=[o_spec, o_spec],
        scratch_shapes=[pltpu.VMEM((3, 2 * tq, tk), F32)],
        compiler_params=_params(("parallel",), VMEM_BIG),
    )(qk, qk, v)


def _attn_bwd(g, qk, v, o, lse, do, dlse):
    s_len = qk.shape[0]
    d = DILATIONS[g]
    l = s_len // d
    tq, tk = _attn_tiles(l)
    q_spec, k_spec, v_spec, o_spec = _attn_specs(g, s_len)
    scale = 1.0 / math.sqrt(HEAD_DIM)

    def body(q_ref, k_ref, v_ref, o_ref, lse_ref, do_ref, dlse_ref, dq_ref, dk_ref, dv_ref, bias_ref):
        lane = lax.broadcasted_iota(jnp.int32, (1, 128), 1)
        in_h = [lane < HEAD_DIM, lane >= HEAD_DIM]
        dk_ref[...] = jnp.zeros_like(dk_ref)
        dv_ref[...] = jnp.zeros_like(dv_ref)
        _attn_fill_bias(bias_ref)

        def tile(t, carry):
            rows, win, kind = _attn_tile_geometry(t, d, l)
            k, vv = k_ref[win, :].astype(BF16), v_ref[win, :].astype(BF16)
            dout, lse_t, dlse_t = do_ref[rows, :], lse_ref[rows, :], dlse_ref[rows, :]
            od = dout * o_ref[rows, :]
            q2 = _split_heads((q_ref[rows, :] * scale).astype(BF16), in_h)
            do2 = _split_heads(dout.astype(BF16), in_h)
            head_col = lambda a: jnp.concatenate([a[:, 0:1], a[:, HEAD_DIM:HEAD_DIM + 1]], axis=0)
            delta = jnp.concatenate([jnp.sum(jnp.where(m, od, 0.0), axis=1, keepdims=True) for m in in_h], axis=0)
            p = jnp.exp(_nt(q2, k) + bias_ref[kind] - head_col(lse_t))
            ds = (p * (_nt(do2, vv) - delta + head_col(dlse_t))).astype(BF16)
            dq2 = _nn(ds, k) * scale
            dq_ref[rows, :] = jnp.where(in_h[0], dq2[:tq], dq2[tq:])
            dk_ref[win, :] += _tn(ds, q2)
            dv_ref[win, :] += _tn(p, do2)
            return carry

        lax.fori_loop(0, s_len // tq, tile, 0, unroll=8 * ATT_TQ // tq)

    return pl.pallas_call(
        body, name=f"attn_bwd_g{g}", grid=(4,),
        out_shape=[jax.ShapeDtypeStruct((s_len, 512), F32)] * 3,
        in_specs=[q_spec, k_spec, v_spec, o_spec, o_spec, o_spec, o_spec], out_specs=[o_spec] * 3,
        scratch_shapes=[pltpu.VMEM((3, 2 * tq, tk), F32)],
        compiler_params=_params(("parallel",), VMEM_BIG),
    )(qk, qk, v, o, lse, do, dlse)


def _mix_weights(ls):
    mx = jnp.maximum(jnp.maximum(ls[0], ls[1]), ls[2])
    es = [jnp.exp(x - mx) for x in ls]
    tot = es[0] + es[1] + es[2]
    return [e / tot for e in es]


def _attn_out(os_, lses, z, x, gate, w_out):
    s_len, dm = x.shape
    tm = 256
    wdt = 512
    z, z_block = z

    def body(o0, o1, o2, l0, l1, l2, z_ref, x_ref, g_ref, w_ref, a_ref, y_ref, x1_ref):
        alphas = _mix_weights([l0[...], l1[...], l2[...]])
        y = jnp.zeros((tm, dm), F32)
        for g, o_ref in enumerate((o0, o1, o2)):
            a_g = (o_ref[...] * alphas[g] * _silu(z_ref[:, g * wdt:(g + 1) * wdt])).astype(BF16)
            a_ref[:, g * wdt:(g + 1) * wdt] = a_g
            y = y + _nn(a_g, w_ref[g * wdt:(g + 1) * wdt, :])
        y_ref[...] = y
        x1_ref[...] = x_ref[...] + g_ref[...] * y

    row = lambda c: pl.BlockSpec((tm, c), lambda i: (i, 0))
    return pl.pallas_call(
        body, name="attn_out", grid=(s_len // tm,),
        out_shape=[jax.ShapeDtypeStruct((s_len, 3 * wdt), BF16), jax.ShapeDtypeStruct((s_len, dm), F32),
                   jax.ShapeDtypeStruct((s_len, dm), F32)],
        in_specs=[row(wdt)] * 6 + [pl.BlockSpec((tm, 3 * wdt), lambda i: (i, z_block)), row(dm),
                                   pl.BlockSpec((1, dm), lambda i: (0, 0)), pl.BlockSpec(w_out.shape, lambda i: (0, 0))],
        out_specs=[row(3 * wdt), row(dm), row(dm)],
        compiler_params=_params(("parallel",), VMEM_BIG),
    )(*os_, *lses, z, x, gate, w_out)


def _mix_bwd(dy, w_out, os_, lses, z):
    wdt = 512

    def fn(da, o0, o1, o2, l0, l1, l2, z):
        os_t, ls = [o0, o1, o2], [l0, l1, l2]
        alphas = _mix_weights(ls)
        hi = lax.broadcasted_iota(jnp.int32, (2 * wdt, wdt), 0) % wdt // HEAD_DIM
        hj = lax.broadcasted_iota(jnp.int32, (2 * wdt, wdt), 1) // HEAD_DIM
        seg = (hi == hj).astype(BF16)
        head_sum = lambda t: _dg(jnp.concatenate(_bf16_parts(t, 2), axis=1), seg, 1, 0)
        dos, dal, dzs = [], [], []
        for g in range(3):
            zg = z[:, g * wdt:(g + 1) * wdt]
            sig = jax.nn.sigmoid(zg)
            dag = da[:, g * wdt:(g + 1) * wdt]
            dmix = dag * zg * sig
            dzs.append(dag * os_t[g] * alphas[g] * (sig * (1.0 + zg * (1.0 - sig))))
            dos.append(dmix * alphas[g])
            dal.append(head_sum(dmix * os_t[g]))
        mean = alphas[0] * dal[0] + alphas[1] * dal[1] + alphas[2] * dal[2]
        dls = [alphas[g] * (dal[g] - mean) for g in range(3)]
        return dos + dls + [jnp.concatenate(dzs, axis=1)], []

    outs, _ = _matmul_rows("attn_out_dx_mix_bwd", dy, w_out, "nt", 256, dy.shape[1], fn, [*os_, *lses, (z[0], 3 * wdt, z[1])], [],
                           [(wdt, F32)] * 6 + [(3 * wdt, BF16)], [])
    return outs[:3], outs[3:6], outs[6]


def _rot_pack_bwd(dqs, dks, dvs, tabs):
    wdt = 512

    def fn(*args):
        grads, (c, sa, sb) = args[:9], args[9:]
        cols = [_rot_bwd(gq, c, sa, sb) for gq in grads[:6]] + list(grads[6:])
        return [jnp.concatenate(cols, axis=1)], []

    (out,), _ = _rowwise("rot_pack_bwd", fn, [*dqs, *dks, *dvs, *tabs], [], [(9 * wdt, BF16)], [], 512)
    return out


CONV_CB = 128
CONV_R = 256
CONV_PAD = 8


def _conv_taps(buf, base, off, sign):
    return [buf[pl.ds(base + off + sign * j, CONV_R), :] for j in range(CONV_WIDTH)]


def _conv_tap_sum(taps, w):
    acc = None
    for j, t in enumerate(taps):
        term = t * w[j:j + 1, :]
        acc = term if acc is None else acc + term
    return acc


def _conv_fwd(xpre, cw, cb):
    s_len, ch = xpre.shape
    nchunk = s_len // CONV_R

    def body(x_ref, w_ref, b_ref, o_ref, xp):
        zero = jnp.zeros((CONV_PAD, CONV_CB), F32)
        xp[0:CONV_PAD, :] = zero
        xp[s_len + CONV_PAD:s_len + 2 * CONV_PAD, :] = zero

        def fill(ci, carry):
            base = pl.multiple_of(ci * CONV_R, CONV_R)
            xp[pl.ds(base + CONV_PAD, CONV_R), :] = x_ref[pl.ds(base, CONV_R), :]
            return carry

        lax.fori_loop(0, nchunk, fill, 0)
        w = w_ref[...]
        b = b_ref[...]

        def chunk(ci, carry):
            base = pl.multiple_of(ci * CONV_R, CONV_R)
            u = _conv_tap_sum(_conv_taps(xp, base, CONV_PAD - CONV_WIDTH // 2, 1), w) + b
            o_ref[pl.ds(base, CONV_R), :] = _silu(u)
            return carry

        lax.fori_loop(0, nchunk, chunk, 0, unroll=2)

    col = lambda r: pl.BlockSpec((r, CONV_CB), lambda j: (0, j))
    return pl.pallas_call(
        body, name="conv_fwd", grid=(ch // CONV_CB,), out_shape=jax.ShapeDtypeStruct((s_len, ch), F32),
        in_specs=[col(s_len), col(CONV_WIDTH), col(1)], out_specs=col(s_len),
        scratch_shapes=[pltpu.VMEM((s_len + 2 * CONV_PAD, CONV_CB), F32)],
        compiler_params=_params(("parallel",), VMEM_BIG),
    )(xpre, cw, cb)


def _conv_bwd(xpre, da, cw, cb):
    s_len, ch = xpre.shape
    nchunk = s_len // CONV_R
    half = CONV_WIDTH // 2

    def body(x_ref, da_ref, w_ref, b_ref, dx_ref, gw_ref, gb_ref, xp, dcp):
        zero = jnp.zeros((CONV_PAD, CONV_CB), F32)
        for buf in (xp, dcp):
            buf[0:CONV_PAD, :] = zero
            buf[s_len + CONV_PAD:s_len + 2 * CONV_PAD, :] = zero

        def fill(ci, carry):
            base = pl.multiple_of(ci * CONV_R, CONV_R)
            xp[pl.ds(base + CONV_PAD, CONV_R), :] = x_ref[pl.ds(base, CONV_R), :]
            return carry

        lax.fori_loop(0, nchunk, fill, 0)
        w = w_ref[...]
        b = b_ref[...]

        def first(ci, carry):
            base = pl.multiple_of(ci * CONV_R, CONV_R)
            taps = _conv_taps(xp, base, CONV_PAD - half, 1)
            u = _conv_tap_sum(taps, w) + b
            sig = jax.nn.sigmoid(u)
            dc = da_ref[pl.ds(base, CONV_R), :] * (sig * (1.0 + u * (1.0 - sig)))
            dcp[pl.ds(base + CONV_PAD, CONV_R), :] = dc
            gb = carry[0] + jnp.sum(dc, axis=0, keepdims=True)
            gws = [carry[1 + j] + jnp.sum(dc * taps[j], axis=0, keepdims=True) for j in range(CONV_WIDTH)]
            return (gb, *gws)

        z1 = jnp.zeros((1, CONV_CB), F32)
        sums = lax.fori_loop(0, nchunk, first, (z1,) * (1 + CONV_WIDTH), unroll=2)
        gb_ref[...] = sums[0]
        for j in range(CONV_WIDTH):
            gw_ref[j:j + 1, :] = sums[1 + j]

        def second(ci, carry):
            base = pl.multiple_of(ci * CONV_R, CONV_R)
            dx_ref[pl.ds(base, CONV_R), :] = _conv_tap_sum(_conv_taps(dcp, base, CONV_PAD + half, -1), w).astype(dx_ref.dtype)
            return carry

        lax.fori_loop(0, nchunk, second, 0, unroll=2)

    col = lambda r: pl.BlockSpec((r, CONV_CB), lambda j: (0, j))
    return pl.pallas_call(
        body, name="conv_bwd", grid=(ch // CONV_CB,),
        out_shape=[jax.ShapeDtypeStruct((s_len, ch), BF16), jax.ShapeDtypeStruct((CONV_WIDTH, ch), F32),
                   jax.ShapeDtypeStruct((1, ch), F32)],
        in_specs=[col(s_len), col(s_len), col(CONV_WIDTH), col(1)],
        out_specs=[col(s_len), col(CONV_WIDTH), col(1)],
        scratch_shapes=[pltpu.VMEM((s_len + 2 * CONV_PAD, CONV_CB), F32)] * 2,
        compiler_params=_params(("parallel",), VMEM_BIG),
    )(xpre, da, cw, cb)


SSD_GW = 256
SSD_N = 128
SSD_DTW = 128


def _bf16_parts(x, n):
    parts, rest = [], x
    for _ in range(n):
        p = rest.astype(BF16)
        parts.append(p)
        rest = rest - p.astype(F32)
    return parts


@jax.custom_vjp
def _expand(x, e):
    eb = e.astype(BF16)
    return _dg(jnp.concatenate(_bf16_parts(x, 2), axis=1), jnp.concatenate([eb, eb], axis=0), 1, 0)


def _expand_fwd(x, e):
    return _expand(x, e), e


def _expand_bwd(e, g):
    return _dg(g.astype(BF16), e.astype(BF16), 1, 1), jnp.zeros_like(e)


_expand.defvjp(_expand_fwd, _expand_bwd)


@jax.custom_vjp
def _running_sum(tri, x):
    tb = tri.astype(BF16)
    return sum(_dg(tb, p, 1, 0) for p in _bf16_parts(x, 3))


def _running_sum_fwd(tri, x):
    return _running_sum(tri, x), tri


def _running_sum_bwd(tri, g):
    tb = tri.astype(BF16)
    return jnp.zeros_like(tri), sum(_dg(tb, p, 0, 0) for p in _bf16_parts(g, 3))


_running_sum.defvjp(_running_sum_fwd, _running_sum_bwd)


def _pick_col(a, h):
    @jax.custom_vjp
    def pick(a):
        return a[:, h:h + 1]

    pick.defvjp(lambda a: (a[:, h:h + 1], None),
                lambda _, g: (g * (lax.broadcasted_iota(jnp.int32, (1, a.shape[1]), 1) == h).astype(F32),))
    return pick(a)


def _pick_row(a, h):
    @jax.custom_vjp
    def pick(a):
        return a[h:h + 1, :]

    pick.defvjp(lambda a: (a[h:h + 1, :], None),
                lambda _, g: (g * (lax.broadcasted_iota(jnp.int32, (a.shape[0], 1), 0) == h).astype(F32),))
    return pick(a)


def _ssd_mask(dirn):
    ri = lax.broadcasted_iota(jnp.int32, (CHUNK, CHUNK), 0)
    cj = lax.broadcasted_iota(jnp.int32, (CHUNK, CHUNK), 1)
    return (cj <= ri) if dirn == 0 else (cj >= ri)


def _ssd_rowsel(dirn):
    last = CHUNK - 1 if dirn == 0 else 0
    return (lax.broadcasted_iota(jnp.int32, (CHUNK, 1), 0) == last).astype(F32)


def _ssd_chunk_pre(dirn):
    nh = SSD_DTW

    def f(dt, alog):
        da = dt * (-jnp.exp(alog))
        cum = _running_sum(_ssd_mask(dirn).astype(F32), da)
        tot = jnp.sum(cum * _ssd_rowsel(dirn), axis=0, keepdims=True)
        hh = lax.broadcasted_iota(jnp.int32, (nh, SSD_HEADS * HEAD_DIM), 0)
        jj = lax.broadcasted_iota(jnp.int32, (nh, SSD_HEADS * HEAD_DIM), 1)
        expand = (hh == dirn * SSD_HEADS + jj // HEAD_DIM).astype(F32)
        return cum, cum.T, _expand(dt, expand), _expand(jnp.exp(tot - cum), expand), _expand(jnp.exp(cum), expand)

    return f


def _ssd_group_fn(g, dirn, stacked):
    def f(xs, bm, cm, st, cum, cum_t, dt_e, w_e, ce_e):
        mask = _ssd_mask(dirn)
        xdt = xs * dt_e
        cd_e = jnp.sum(ce_e * _ssd_rowsel(dirn), axis=0, keepdims=True)
        cb = _bnt(cm, bm)
        lane_head = lax.broadcasted_iota(jnp.int32, (1, SSD_GW), 1) // HEAD_DIM
        y = _bnn(cm, st) * ce_e
        decayed, inputs = [], []
        for j in range(4):
            hidx = dirn * SSD_HEADS + 4 * g + j
            col, row = _pick_col(cum, hidx), _pick_row(cum_t, hidx)
            dec = cb * jnp.exp(jnp.where(mask, col - row, NEG_BIG))
            head = (lane_head == j).astype(F32)
            if stacked:
                decayed.append(dec)
                inputs.append(xdt * head)
            else:
                y = y + _bnn(dec, xdt) * head
        if stacked:
            y = y + _bnn(jnp.concatenate(decayed, axis=1), jnp.concatenate(inputs, axis=0))
        st_out = st * cd_e + _btn(bm, xdt * w_e)
        return y, st_out

    return f


def _ssd_in_specs(kk):
    ln = CHUNK
    return [pl.BlockSpec((ln, 2048), lambda i: (kk(i), 0)),
            pl.BlockSpec((ln, 1024), lambda i: (kk(i), 2)),
            pl.BlockSpec((ln, 1024), lambda i: (kk(i), 3)),
            pl.BlockSpec((ln, SSD_DTW), lambda i: (kk(i), 0)),
            pl.BlockSpec((1, SSD_DTW), lambda i: (0, 0))]


def _ssd_fwd(xbc, dt, alog, dirn, prior=None):
    s_len = xbc.shape[0]
    nc = s_len // CHUNK
    kk = (lambda i: i) if dirn == 0 else (lambda i: nc - 1 - i)

    def body(x_ref, b_ref, c_ref, dt_ref, al_ref, *rest):
        prior_ref = rest[0] if prior is not None else None
        y_ref, sts_ref, st = rest[prior is not None:]

        @pl.when(pl.program_id(0) == 0)
        def _():
            st[...] = jnp.zeros_like(st)

        sts_ref[0] = st[...]
        cum, cum_t, dt_e, w_e, ce_e = _ssd_chunk_pre(dirn)(dt_ref[...], al_ref[...])
        for g in range(SSD_GROUPS):
            xc = slice(g * SSD_GW, (g + 1) * SSD_GW)
            gc = slice(g * SSD_N, (g + 1) * SSD_N)
            y, st_new = _ssd_group_fn(g, dirn, True)(x_ref[:, xc], b_ref[:, gc], c_ref[:, gc], st[:, xc], cum, cum_t,
                                               dt_e[:, xc], w_e[:, xc], ce_e[:, xc])
            y_ref[:, xc] = y if prior is None else y + prior_ref[:, xc]
            st[:, xc] = st_new

    return pl.pallas_call(
        body, name=f"ssd_fwd_d{dirn}", grid=(nc,),
        out_shape=[jax.ShapeDtypeStruct((s_len, 2048), F32), jax.ShapeDtypeStruct((nc, SSD_N, 2048), F32)],
        in_specs=_ssd_in_specs(kk) + ([pl.BlockSpec((CHUNK, 2048), lambda i: (kk(i), 0))] if prior is not None else []),
        out_specs=[pl.BlockSpec((CHUNK, 2048), lambda i: (kk(i), 0)),
                   pl.BlockSpec((1, SSD_N, 2048), lambda i: (kk(i), 0, 0))],
        scratch_shapes=[pltpu.VMEM((SSD_N, 2048), F32)],
        compiler_params=_params(("arbitrary",), VMEM_BIG),
    )(xbc, xbc, xbc, dt, alog, *([prior] if prior is not None else []))


def _ssd_bwd(xbc, dt, alog, states, dy, d_e, dirn, prior=None):
    s_len = xbc.shape[0]
    nc = s_len // CHUNK
    kk = (lambda i: nc - 1 - i) if dirn == 0 else (lambda i: i)

    def body(x_ref, b_ref, c_ref, dt_ref, al_ref, sts_ref, dy_ref, de_ref, *rest):
        prior_ref = rest[0] if prior is not None else None
        dx_ref, ddt_ref, dal_ref, dst = rest[prior is not None:]
        plus_prior = (lambda v, cols: v + prior_ref[:, cols]) if prior is not None else (lambda v, cols: v)

        @pl.when(pl.program_id(0) == 0)
        def _():
            dst[...] = jnp.zeros_like(dst)
            dal_ref[...] = jnp.zeros_like(dal_ref)

        (cum, cum_t, dt_e, w_e, ce_e), pre_vjp = jax.vjp(_ssd_chunk_pre(dirn), dt_ref[...], al_ref[...])
        dcum = jnp.zeros_like(cum)
        dcum_t = jnp.zeros_like(cum_t)
        d_dt_e, d_w_e, d_ce_e = [], [], []
        for g in range(SSD_GROUPS):
            xc = slice(g * SSD_GW, (g + 1) * SSD_GW)
            gc = slice(g * SSD_N, (g + 1) * SSD_N)
            _, vjp = jax.vjp(_ssd_group_fn(g, dirn, False), x_ref[:, xc], b_ref[:, gc], c_ref[:, gc], sts_ref[0, :, xc], cum, cum_t,
                             dt_e[:, xc], w_e[:, xc], ce_e[:, xc])
            dyg = dy_ref[:, xc]
            dxs, dbm, dcm, dst_g, dcum_g, dcum_t_g, ddte_g, dwe_g, dcee_g = vjp((dyg, dst[:, xc]))
            if dirn == 0:
                dxs = dxs + dyg * de_ref[:, xc]
            bc, cc = slice(2048 + g * SSD_N, 2048 + (g + 1) * SSD_N), slice(3072 + g * SSD_N, 3072 + (g + 1) * SSD_N)
            dx_ref[:, xc] = plus_prior(dxs, xc)
            dx_ref[:, bc] = plus_prior(dbm, bc)
            dx_ref[:, cc] = plus_prior(dcm, cc)
            dst[:, xc] = dst_g
            dcum = dcum + dcum_g
            dcum_t = dcum_t + dcum_t_g
            d_dt_e.append(ddte_g)
            d_w_e.append(dwe_g)
            d_ce_e.append(dcee_g)
        ddt, dal = pre_vjp((dcum, dcum_t, jnp.concatenate(d_dt_e, axis=1), jnp.concatenate(d_w_e, axis=1),
                            jnp.concatenate(d_ce_e, axis=1)))
        ddt_ref[...] = ddt
        dal_ref[...] += dal

    return pl.pallas_call(
        body, name=f"ssd_bwd_d{dirn}", grid=(nc,),
        out_shape=[jax.ShapeDtypeStruct((s_len, 4096), F32), jax.ShapeDtypeStruct((s_len, SSD_DTW), F32),
                   jax.ShapeDtypeStruct((1, SSD_DTW), F32)],
        in_specs=_ssd_in_specs(kk) + [pl.BlockSpec((1, SSD_N, 2048), lambda i: (kk(i), 0, 0)),
                                      pl.BlockSpec((CHUNK, 2048), lambda i: (kk(i), 0)),
                                      pl.BlockSpec((1, 2048), lambda i: (0, 0))]
        + ([pl.BlockSpec((CHUNK, 4096), lambda i: (kk(i), 0))] if prior is not None else []),
        out_specs=[pl.BlockSpec((CHUNK, 4096), lambda i: (kk(i), 0)),
                   pl.BlockSpec((CHUNK, SSD_DTW), lambda i: (kk(i), 0)),
                   pl.BlockSpec((1, SSD_DTW), lambda i: (0, 0))],
        scratch_shapes=[pltpu.VMEM((SSD_N, 2048), F32)],
        compiler_params=_params(("arbitrary",), VMEM_BIG),
    )(xbc, xbc, xbc, dt, alog, states, dy, d_e, *([prior] if prior is not None else []))


def _gate_norm_fn(y, xs, z, d_e, nw):
    yg = (y + xs * d_e) * _silu(z)
    return yg * lax.rsqrt(jnp.mean(yg * yg, axis=-1, keepdims=True) + NORM_EPS) * nw


def _gate_norm_bwd(dy, w_out, y, xbc, z, d_e, nw):
    def fn(du, y, xs, z, d_e, nw):
        sig = jax.nn.sigmoid(z)
        gate = z * sig
        ysum = y + xs * d_e
        yg = ysum * gate
        r = lax.rsqrt(jnp.mean(yg * yg, axis=-1, keepdims=True) + NORM_EPS)
        t = du * nw
        dyg = t * r - yg * (jnp.mean(t * yg, axis=-1, keepdims=True) * (r * r * r))
        dys = dyg * gate
        dz = dyg * ysum * (sig * (1.0 + z * (1.0 - sig)))
        dnw = jnp.sum(du * yg * r, axis=0, keepdims=True)
        dde = jnp.sum(dys * xs, axis=0, keepdims=True)
        hh = lax.broadcasted_iota(jnp.int32, (2048, SSD_HEADS), 0) // HEAD_DIM
        jj = lax.broadcasted_iota(jnp.int32, (2048, SSD_HEADS), 1)
        return [dys, dz], [dnw, _hnn(jnp.broadcast_to(dde, (8, 2048)), (hh == jj).astype(F32))[0:1]]

    (dys, dz), (g_nw, g_d) = _matmul_rows("ssd_out_dx_gate_norm_bwd", dy, w_out, "nt", 256, dy.shape[1], fn,
                                          [y, (xbc, 2048, 0), z], [d_e, nw], [(2048, F32), (2048, BF16)],
                                          [(1, 2048), (1, SSD_HEADS)])
    return dys, dz, g_nw, g_d


def _ssd_tail_loss(y, xbc, z, d_e, snw, w_out, x1, tgt, gate, fnw):
    dm = x1.shape[1]
    si = y.shape[1]

    def make_u(y, xs, z, x1, tgt, d_e, snw, gate, fnw):
        return _gate_norm_fn(y, xs, z, d_e, snw).astype(BF16)

    def fn(y1, u, y, xs, z, x1, tgt, d_e, snw, gate, fnw):
        x2 = x1 + gate * y1
        r = lax.rsqrt(jnp.mean(x2 * x2, axis=-1, keepdims=True) + NORM_EPS)
        xh = x2 * r
        err = xh * fnw - tgt
        loss = 0.5 * jnp.sum(jnp.mean(err * err, axis=-1, keepdims=True), axis=0, keepdims=True)
        dy = err * (1.0 / dm)
        dxh = dy * fnw
        dx2 = r * (dxh - xh * jnp.mean(dxh * xh, axis=-1, keepdims=True))
        dfnw = jnp.sum(dy * xh, axis=0, keepdims=True)
        return [u, dx2, gate * dx2], [dfnw, jnp.sum(dx2 * y1, axis=0, keepdims=True), jnp.broadcast_to(loss, (1, 128))]

    (u, dx2, dy1), (g_fnw, dgate, loss) = _matmul_rows(
        "ssd_out_loss", make_u, w_out, "nn", 256, si, fn, [y, (xbc, si, 0), z, x1, tgt], [d_e, snw, gate, fnw],
        [(si, BF16), (dm, F32), (dm, BF16)], [(1, dm), (1, dm), (1, 128)])
    return u, dx2, dy1, g_fnw, dgate, loss


def _softplus_fwd(dt_raw, bias):
    (dt,), _ = _rowwise("dt_softplus", lambda r, b: ([jax.nn.softplus(r + b)], []), [dt_raw], [bias],
                        [(dt_raw.shape[1], F32)], [], 512)
    return dt


def _softplus_bwd(ddt_f, ddt_b, dt_raw, bias):
    def fn(df, db, r, b):
        g = (df + db) * jax.nn.sigmoid(r + b)
        return [g], [jnp.sum(g, axis=0, keepdims=True)]

    w = dt_raw.shape[1]
    (g,), (gb,) = _rowwise("dt_softplus_bwd", fn, [ddt_f, ddt_b, dt_raw], [bias], [(w, BF16)], [(1, w)], 512)
    return g, gb


def _mod_part(c_all, mod_w):
    nl, _, ncol = mod_w.shape
    nb = c_all.shape[0]

    def body(c_ref, w_ref, o_ref):
        cond = _silu(c_ref[...])
        for i in range(nl):
            o_ref[i * nb:(i + 1) * nb, :] = _nn(cond, w_ref[i])

    return pl.pallas_call(body, name="mod_part", out_shape=jax.ShapeDtypeStruct((nl * nb, ncol), F32),
                          compiler_params=_params(None, VMEM_BIG))(c_all, mod_w)


def _mod_finish(mod_nb, mod_b, norm_w, tokens):
    nl, dm = norm_w.shape

    def body(a_ref, b_ref, nw_ref, *rest):
        tok_refs, o_refs = rest[:len(tokens)], rest[len(tokens):]
        tok = sum(t[0:1, 0:1] for t in tok_refs)
        for i in range(nl):
            for k in range(3):
                cols = slice(k * dm, (k + 1) * dm)
                o_refs[4 * i + k][...] = a_ref[i:i + 1, cols] + b_ref[i:i + 1, cols]
            o_refs[4 * i + 3][...] = nw_ref[i:i + 1, :] + tok

    rows = pl.pallas_call(body, name="mod_finish", out_shape=[jax.ShapeDtypeStruct((1, dm), F32)] * (4 * nl))(
        mod_nb, mod_b, norm_w, *tokens)
    return [rows[4 * i:4 * i + 4] for i in range(nl)]


def _mod_grad(c_all, dmod_sh):
    nl, nb, ncol = dmod_sh.shape
    dm = c_all.shape[1]

    def body(c_ref, d_ref, o_ref):
        cond = _silu(c_ref[...])
        for i in range(nl):
            o_ref[i] = _tn(cond, d_ref[i])

    return pl.pallas_call(body, name="mod_grad", out_shape=jax.ShapeDtypeStruct((nl, dm, ncol), F32),
                          compiler_params=_params(None, VMEM_BIG))(c_all, dmod_sh)


PACK_ROWS = 16
PACK_COLS = 1024


def _pack_small(rows, b64, a64s, d32, extra):
    nr, na = len(rows), len(a64s)

    def body(*refs):
        o_ref = refs[-1]
        o_ref[...] = jnp.zeros_like(o_ref)
        for i in range(nr):
            o_ref[i:i + 1, :] = refs[i][...]
        b_ref, a_refs, d_ref, e_ref = refs[nr], refs[nr + 1:nr + 1 + na], refs[nr + 1 + na], refs[nr + 2 + na]
        o_ref[nr:nr + 1, 0:64] = b_ref[:, 0:64]
        o_ref[nr:nr + 1, 64:128] = sum(a[:, 0:64] for a in a_refs)
        o_ref[nr:nr + 1, 128:160] = d_ref[...]
        o_ref[nr:nr + 1, 256:384] = e_ref[...]

    return pl.pallas_call(body, name="pack_small", out_shape=jax.ShapeDtypeStruct((PACK_ROWS, PACK_COLS), F32))(
        *rows, b64, *a64s, d32, extra)


def _pack_ssd_small(cw, cb, nw):
    def body(cw_ref, cb_ref, nw_ref, o_ref):
        o_ref[...] = jnp.zeros_like(o_ref)
        o_ref[0:5, :] = cw_ref[...]
        o_ref[5:6, :] = cb_ref[...]
        o_ref[6:7, 0:256] = nw_ref[...]

    return pl.pallas_call(body, name="pack_ssd_small", out_shape=jax.ShapeDtypeStruct((8, 512), F32))(cw, cb, nw)


def _sum_parts(p_ref):
    g = p_ref[0].astype(F32)
    for s in range(1, p_ref.shape[0]):
        g = g + p_ref[s].astype(F32)
    return g


def _adam_update(w, g, m, v):
    m2 = ADAM_B1 * m + (1.0 - ADAM_B1) * g
    v2 = ADAM_B2 * v + (1.0 - ADAM_B2) * (g * g)
    m_hat = m2 / (1.0 - ADAM_B1 ** ADAM_STEP)
    v_hat = v2 / (1.0 - ADAM_B2 ** ADAM_STEP)
    return -ADAM_LR * (m_hat / (jnp.sqrt(v_hat) + ADAM_EPS) + ADAM_WD * w), m2, v2


def _adamw_windows(name, parts, params, windows, extra=None):
    n = len(params)

    def body(p_ref, *rest):
        ins, outs = rest[:3 * n], rest[3 * n:]
        g = _sum_parts(p_ref)
        for pi, rows, cols, idx in windows:
            w_ref, m_ref, v_ref = ins[3 * pi:3 * pi + 3]
            gw = g[rows, cols]
            dw, m2, v2 = _adam_update(w_ref[idx], gw, m_ref[idx], v_ref[idx])
            for o_ref, val in zip(outs[4 * pi:4 * pi + 4], (gw, dw, m2, v2), strict=True):
                o_ref[idx] = val
        if extra is not None:
            outs[4 * n][...] = g[extra[0], extra[1]]

    out_shape = [jax.ShapeDtypeStruct(w.shape, F32) for (w, _, _) in params for _ in range(4)]
    if extra is not None:
        out_shape.append(jax.ShapeDtypeStruct((extra[0].stop - extra[0].start, extra[1].stop - extra[1].start), F32))
    res = pl.pallas_call(body, name=name, out_shape=out_shape)(parts, *[a for p in params for a in p])
    return [res[4 * i:4 * i + 4] for i in range(n)] + ([res[4 * n]] if extra is not None else [])


def _adamw(name, w, parts, m, v, tr, tc=None):
    r_, c_ = w.shape
    p_ = parts.shape[0]
    tr = min(tr, r_)
    tc = c_ if tc is None else tc
    assert r_ % tr == 0 and c_ % tc == 0

    def body(w_ref, p_ref, m_ref, v_ref, g_ref, d_ref, m2_ref, v2_ref):
        g = _sum_parts(p_ref)
        g_ref[...] = g
        d_ref[...], m2_ref[...], v2_ref[...] = _adam_update(w_ref[...], g, m_ref[...], v_ref[...])

    blk = pl.BlockSpec((tr, tc), lambda i, j: (i, j))
    return pl.pallas_call(
        body, name=name, grid=(r_ // tr, c_ // tc), out_shape=[jax.ShapeDtypeStruct((r_, c_), F32)] * 4,
        in_specs=[blk, pl.BlockSpec((p_, tr, tc), lambda i, j: (0, i, j)), blk, blk], out_specs=[blk] * 4,
        compiler_params=_params(("parallel", "parallel"), VMEM_BIG),
    )(w, parts, m, v)


def _dev_index(p):
    return 4 * p[0] + 2 * p[1] + p[2]


def _all_gather(name, xs):
    n = len(xs)
    hbm = pl.BlockSpec(memory_space=pl.ANY)

    def body(*refs):
        x_refs, o_refs = refs[:n], refs[n:2 * n]
        send_sems, recv_sems, local_sems = refs[2 * n:]
        x, y, c = lax.axis_index("x"), lax.axis_index("y"), lax.axis_index("c")
        me, sibling = (x, y, c), (x, y, 1 - c)
        chips = [(1 - x, y), (x, 1 - y), (1 - x, 1 - y)]

        def place(a, block):
            return o_refs[a].at[_dev_index(block)]

        def copy(a, k, block, to, src=None):
            dst = place(a, block)
            return pltpu.make_async_remote_copy(
                src_ref=dst if src is None else src, dst_ref=dst, send_sem=send_sems.at[a, k],
                recv_sem=recv_sems.at[a, k], device_id=to, device_id_type=MESH)

        mine = [pltpu.make_async_copy(x_refs[a], place(a, me), local_sems.at[a]) for a in range(n)]
        for cp in mine:
            cp.start()
        first = []
        for a in range(n):
            first.append(copy(a, 0, me, sibling, src=x_refs[a]))
            first += [copy(a, 1 + j, me, (*chip, c), src=x_refs[a]) for j, chip in enumerate(chips)]
        for cp in first:
            cp.start()
        passed = []
        for j, chip in enumerate(chips):
            for a in range(n):
                copy(a, 1 + j, (*chip, c), me).wait_recv()
                cp = copy(a, 4 + j, (*chip, c), sibling)
                cp.start()
                passed.append(cp)
        for a in range(n):
            copy(a, 0, sibling, me).wait_recv()
            for j, chip in enumerate(chips):
                copy(a, 4 + j, (*chip, 1 - c), me).wait_recv()
        for cp in first + passed:
            cp.wait_send()
        for cp in mine:
            cp.wait()

    return pl.pallas_call(
        body, name=name, out_shape=[jax.ShapeDtypeStruct((NDEV, *x.shape), x.dtype) for x in xs],
        in_specs=[hbm] * n, out_specs=[hbm] * n,
        scratch_shapes=[pltpu.SemaphoreType.DMA((n, 7)), pltpu.SemaphoreType.DMA((n, 7)), pltpu.SemaphoreType.DMA((n,))],
    )(*xs)


_HBM = pl.BlockSpec(memory_space=pltpu.HBM)
_SEM = pl.BlockSpec(memory_space=pltpu.SEMAPHORE)
_EFFECT = pltpu.SideEffectType.DATAFLOW_SIDE_EFFECTING


def _mesh_position():
    return lax.axis_index("x"), lax.axis_index("y"), lax.axis_index("c")


def _peers(me):
    return [(k, tuple(1 - v if (k >> b) & 1 else v for v, b in zip(me, (2, 1, 0)))) for k in range(1, NDEV)]


def _column_window(ref, block, width):
    return ref.at[:, pl.ds(pl.multiple_of(_dev_index(block) * width, 128), width)]


def _columns_pass_on(name, gathered):
    width = gathered.shape[1] // NDEV
    hbm = pl.BlockSpec(memory_space=pl.ANY)

    def body(g_ref, o_ref, send_sems, recv_sems):
        x, y, c = _mesh_position()
        chips = [(1 - x, y), (x, 1 - y), (1 - x, 1 - y)]

        def copy(k, block, to):
            window = _column_window(o_ref, block, width)
            return pltpu.make_async_remote_copy(src_ref=window, dst_ref=window, send_sem=send_sems.at[k],
                                                recv_sem=recv_sems.at[k], device_id=to, device_id_type=MESH)

        onward = [copy(j, (*chip, c), (x, y, 1 - c)) for j, chip in enumerate(chips)]
        for cp in onward:
            cp.start()
        for j, chip in enumerate(chips):
            copy(j, (*chip, 1 - c), (x, y, c)).wait_recv()
        for cp in onward:
            cp.wait_send()

    return pl.pallas_call(
        body, name=name, out_shape=jax.ShapeDtypeStruct(gathered.shape, gathered.dtype), in_specs=[hbm], out_specs=hbm,
        input_output_aliases={0: 0},
        scratch_shapes=[pltpu.SemaphoreType.DMA((NCHIP - 1,)), pltpu.SemaphoreType.DMA((NCHIP - 1,))],
    )(gathered)


NCHIP = NDEV // 2
EXCHANGE_COPIES = {"columns": NCHIP, "gather": NDEV - 1, "scatter": NDEV - 1, "pair": NCHIP, "chips": NCHIP - 1}


def _landing_zones(name, xs, mode):
    x_, y_, c_ = _mesh_position()
    mine = (2 * x_ + y_ if mode == "chips" else _dev_index((x_, y_, c_))).astype(jnp.int32).reshape(1)
    lands = []
    for a, x in enumerate(xs):
        rows, cols = x.shape[-2:]
        if mode == "pair":
            lands.append(lax.empty((NCHIP, rows, cols), x.dtype))
            continue
        tr = 256 if rows % 256 == 0 else rows

        def body(me_ref, x_ref, o_ref):
            o_ref[...] = x_ref[...]

        if mode in ("gather", "columns"):
            in_spec = pl.BlockSpec((tr, cols), lambda i, me_ref: (i, 0))
        else:
            in_spec = pl.BlockSpec((None, tr, cols), lambda i, me_ref: (me_ref[0], i, 0))
        if mode == "columns":
            out_shape, out_spec = (rows, NDEV * cols), pl.BlockSpec((tr, cols), lambda i, me_ref: (i, me_ref[0]))
        else:
            out_shape = (NCHIP if mode == "chips" else NDEV, rows, cols)
            out_spec = pl.BlockSpec((None, tr, cols), lambda i, me_ref: (me_ref[0], i, 0))
        lands.append(pl.pallas_call(
            body, name=f"{name}_{a}", out_shape=jax.ShapeDtypeStruct(out_shape, x.dtype),
            grid_spec=pltpu.PrefetchScalarGridSpec(num_scalar_prefetch=1, grid=(rows // tr,), in_specs=[in_spec],
                                                   out_specs=out_spec),
            compiler_params=_params(("arbitrary",)),
        )(mine, x))
    return lands


def _exchange_copies(x_refs, land_refs, send_sems, recv_sems, mode):
    x_, y_, c_ = me = _mesh_position()
    per_array = EXCHANGE_COPIES[mode]
    out = []

    def add(a, k, src, dst, peer):
        sem = a * per_array + k
        out.append(pltpu.make_async_remote_copy(src_ref=src, dst_ref=dst, send_sem=send_sems.at[sem], recv_sem=recv_sems.at[sem],
                                                device_id=peer, device_id_type=MESH))

    for a, (x_ref, land_ref) in enumerate(zip(x_refs, land_refs)):
        if mode == "columns":
            for k, peer in enumerate([(x_, y_, 1 - c_), (1 - x_, y_, c_), (x_, 1 - y_, c_), (1 - x_, 1 - y_, c_)]):
                add(a, k, x_ref, _column_window(land_ref, me, x_ref.shape[1]), peer)
        elif mode in ("gather", "scatter"):
            for k, peer in _peers(me):
                add(a, k - 1, x_ref.at[_dev_index(peer)] if mode == "scatter" else x_ref, land_ref.at[_dev_index(me)], peer)
        elif mode == "pair":
            for chip in range(NCHIP):
                add(a, chip, x_ref.at[2 * chip + 1 - c_], land_ref.at[chip], (x_, y_, 1 - c_))
        else:
            for k in range(1, NCHIP):
                px, py = (1 - x_ if k & 2 else x_), (1 - y_ if k & 1 else y_)
                add(a, k - 1, x_ref.at[2 * px + py], land_ref.at[2 * x_ + y_], (px, py, c_))
    return out


def _exchange_start(name, xs, lands, mode, dep, carry=False):
    n = len(xs)

    def body(*refs):
        x_refs, land_refs = refs[:n], refs[n:2 * n]
        send_sems, recv_sems = refs[2 * n + 1], refs[2 * n + 2]
        for cp in _exchange_copies(x_refs, land_refs, send_sems, recv_sems, mode):
            cp.start()
        if not carry:
            refs[-1][...] = jnp.zeros_like(refs[-1])

    sems = pltpu.SemaphoreType.DMA((n * EXCHANGE_COPIES[mode],))
    moved = [pltpu.with_memory_space_constraint(a, pltpu.HBM) for a in (*xs, *lands, *([dep] if carry else []))]
    res = pl.pallas_call(
        body, name=name,
        out_shape=(sems, sems, *[pltpu.HBM(a.shape, a.dtype) for a in moved],
                   *([] if carry else [jax.ShapeDtypeStruct((8, 128), F32)])),
        in_specs=[_HBM] * len(moved) + ([] if carry else [pl.BlockSpec(memory_space=pl.ANY)]),
        out_specs=(_SEM, _SEM, *[_HBM] * len(moved), *([] if carry else [pl.BlockSpec(memory_space=pltpu.VMEM)])),
        input_output_aliases={i: 2 + i for i in range(len(moved))},
        compiler_params=pltpu.CompilerParams(has_side_effects=_EFFECT),
    )(*moved, *([] if carry else [dep]))
    return res[:-1], res[-1]


def _exchange_wait(name, handles, mode, after, with_sources=False):
    send_sems, recv_sems = handles[0], handles[1]
    bufs = handles[2:]
    n = len(bufs) // 2
    afters = list(after) if isinstance(after, (list, tuple)) else [after]

    def body(*refs):
        x_refs, land_refs = refs[:n], refs[n:2 * n]
        s_sems, r_sems = refs[2 * n], refs[2 * n + 1]
        for cp in _exchange_copies(x_refs, land_refs, s_sems, r_sems, mode):
            cp.wait_send()
            cp.wait_recv()

    res = pl.pallas_call(
        body, name=name, out_shape=tuple(pltpu.HBM(a.shape, a.dtype) for a in bufs),
        in_specs=[_HBM] * (2 * n) + [_SEM, _SEM] + [pl.BlockSpec(memory_space=pl.ANY)] * len(afters),
        out_specs=tuple([_HBM] * (2 * n)), input_output_aliases={i: i for i in range(2 * n)},
        compiler_params=pltpu.CompilerParams(has_side_effects=_EFFECT),
    )(*bufs, send_sems, recv_sems, *afters)
    return (res[n:], res[:n]) if with_sources else res[n:]


def _pair_sum(name, x, from_sibling):
    _, rows, cols = x.shape
    tr = rows
    core = lax.axis_index("c").astype(jnp.int32).reshape(1)

    def body(c_ref, x_ref, s_ref, o_ref):
        o_ref[...] = (x_ref[...].astype(F32) + s_ref[...].astype(F32)).astype(o_ref.dtype)

    return pl.pallas_call(
        body, name=name, out_shape=jax.ShapeDtypeStruct((NCHIP, rows, cols), x.dtype),
        grid_spec=pltpu.PrefetchScalarGridSpec(
            num_scalar_prefetch=1, grid=(NCHIP, rows // tr),
            in_specs=[pl.BlockSpec((None, tr, cols), lambda j, i, c_ref: (2 * j + c_ref[0], i, 0)),
                      pl.BlockSpec((None, tr, cols), lambda j, i, c_ref: (j, i, 0))],
            out_specs=pl.BlockSpec((None, tr, cols), lambda j, i, c_ref: (j, i, 0))),
        compiler_params=_params(("parallel", "parallel")),
    )(core, x, from_sibling)


def kernel(x, c, positions, norm_w, mod_w, mod_b, attn_w_in, attn_w_out, ssd_w_in, ssd_conv_w, ssd_conv_b, ssd_dt_bias, ssd_a_log, ssd_d, ssd_norm_w, ssd_w_out, final_norm_w, loss_target, m_norm_w, m_mod_w, m_mod_b, m_attn_w_in, m_attn_w_out, m_ssd_w_in, m_ssd_conv_w, m_ssd_conv_b, m_ssd_dt_bias, m_ssd_a_log, m_ssd_d, m_ssd_norm_w, m_ssd_w_out, m_final_norm_w, v_norm_w, v_mod_w, v_mod_b, v_attn_w_in, v_attn_w_out, v_ssd_w_in, v_ssd_conv_w, v_ssd_conv_b, v_ssd_dt_bias, v_ssd_a_log, v_ssd_d, v_ssd_norm_w, v_ssd_w_out, v_final_norm_w):
    s_len, dm = x.shape[1], x.shape[2]
    me = 4 * lax.axis_index("x") + 2 * lax.axis_index("y") + lax.axis_index("c")
    x0 = x.reshape(s_len, dm)
    tgt = loss_target.reshape(s_len, dm)
    aw = 3 * 512
    si = 2 * dm
    sxbc = 2 * si
    n_ssd_in = ssd_w_in.shape[2] * NDEV

    (c_all,) = _all_gather("gather_c", [c])
    c_all = c_all.reshape(NDEV, dm)
    part = _mod_part(c_all, mod_w)
    (part_all,) = _all_gather("gather_mod", [part])
    mod_nb = jnp.stack([lax.dynamic_index_in_dim(part_all, i * NDEV + me, axis=1, keepdims=False).reshape(3 * dm)
                        for i in range(2)])

    wcol = attn_w_in.shape[2]
    ai_shard = [attn_w_in[0].astype(BF16)]
    ai_handles, ai_token = _exchange_start("attn_w_in_start", ai_shard, _landing_zones("attn_w_in_place", ai_shard, "columns"),
                                           "columns", part_all)
    (shift0, scale0, gate0, nw0), (shift1, scale1, gate1, nw1) = _mod_finish(mod_nb, mod_b, norm_w, [ai_token])
    shift, scale, gate, nw = [shift0, shift1], [scale0, scale1], [gate0, gate1], [nw0, nw1]
    hn0 = _norm_mod_fwd("norm0", x0, nw[0], scale[0], shift[0])
    inv_freq = ROPE_THETA ** (-jnp.arange(0, ROT_DIM, 2, dtype=F32) / ROT_DIM)
    per_head = jnp.concatenate([inv_freq, inv_freq, jnp.zeros(HEAD_DIM - ROT_DIM, F32)])
    inv_row = jnp.tile(per_head, 128 // HEAD_DIM).reshape(1, 128)
    tabs = _rope_tables(positions.reshape(s_len, 1), inv_row)
    ssd_small = _pack_ssd_small(ssd_conv_w[0], ssd_conv_b, ssd_norm_w)
    ao_shard = [attn_w_out[0].astype(BF16)]
    late_shards = [ssd_w_in[0].T.astype(BF16), ssd_w_out[0].astype(BF16), ssd_small]
    ao_lands = _landing_zones("w_out_place", ao_shard, "gather")
    late_lands = _landing_zones("weights_place", late_shards, "gather")
    (w_ai,) = _exchange_wait("attn_w_in_wait", ai_handles, "columns", [hn0, *tabs, *ao_lands, *late_lands])
    w_ai = _columns_pass_on("gather_attn_w_in_rest", w_ai)

    ao_handles, w_ai = _exchange_start("w_out_start", ao_shard, ao_lands, "gather", w_ai, carry=True)
    w_handles, w_ai = _exchange_start("weights_start", late_shards, late_lands, "gather", w_ai, carry=True)

    qk = _matmul("proj_qk", hn0, w_ai, "nn", F32, MM_T, MM_T, dm, epilogue=_rot_fwd, mrows=tabs, n_out=2 * aw)
    v = _matmul("proj_vz", hn0, w_ai, "nn", F32, MM_T, MM_T, dm, b_noff=2 * aw, n_out=2 * aw)
    z0 = (v, 1)
    att = [_attn_fwd(g, qk, v) for g in range(3)]
    os_, lses = [a[0] for a in att], [a[1] for a in att]
    (g_ao,) = _exchange_wait("w_out_wait", ao_handles, "gather", lses[2])
    a0, y0, x1 = _attn_out(os_, lses, z0, x0, gate[0], g_ao.reshape(aw, dm))

    hn1 = _norm_mod_fwd("norm1", x1, nw[1], scale[1], shift[1])
    g_si, g_so, g_small = _exchange_wait("weights_wait", w_handles, "gather", hn1)
    w_ao = g_ao.reshape(aw, dm)
    w_si_t = g_si.reshape(n_ssd_in, dm)
    w_so = g_so.reshape(si, dm)
    conv_w = g_small[:, 0:CONV_WIDTH, :].transpose(1, 0, 2).reshape(CONV_WIDTH, sxbc)
    conv_b = g_small[:, 5, :].reshape(1, sxbc)
    snw = g_small[:, 6, 0:si // NDEV].reshape(1, si)
    ndt = 2 * SSD_HEADS
    z1 = _matmul("ssd_proj_z", hn1, w_si_t, "nt", F32, MM_T, MM_T, dm, n_out=si)
    xpre = _matmul("ssd_proj_xbc", hn1, w_si_t, "nt", F32, MM_T, MM_T, dm, b_noff=si, n_out=sxbc)
    dt_raw = _matmul("ssd_proj_dt", hn1, w_si_t, "nt", F32, MM_T, ndt, dm, b_noff=si + sxbc, n_out=ndt)
    xbc = _conv_fwd(xpre, conv_w, conv_b)
    widen = lambda a: jnp.pad(a, ((0, 0), (0, SSD_DTW - ndt)))
    dt_raw = widen(dt_raw)
    dt_bias = widen(ssd_dt_bias.reshape(1, ndt))
    alog = widen(ssd_a_log.reshape(1, ndt))
    dt = _softplus_fwd(dt_raw, dt_bias)
    y_f, st_f = _ssd_fwd(xbc, dt, alog, 0)
    y_fb, st_b = _ssd_fwd(xbc, dt, alog, 1, prior=y_f)
    d_e = jnp.repeat(ssd_d.reshape(SSD_HEADS), HEAD_DIM).reshape(1, si)

    fnw = final_norm_w.reshape(1, dm)
    u, dx2, dy1, g_fnw, dgate1, loss_part = _ssd_tail_loss(y_fb, xbc, z1, d_e, snw, w_so, x1, tgt, gate[1], fnw)
    gw_so = _matmul("ssd_out_dw", u, dy1, "tn", BF16, MM_T, MM_T, MM_T)
    dys, dz1, g_snw, g_d = _gate_norm_bwd(dy1, w_so, y_fb, xbc, z1, d_e, snw)
    dxbc_f, ddt_f, dalog_f = _ssd_bwd(xbc, dt, alog, st_f, dys, d_e, 0)
    dxbc, ddt_b, dalog_b = _ssd_bwd(xbc, dt, alog, st_b, dys, d_e, 1, prior=dxbc_f)
    dpre, g_cw, g_cb = _conv_bwd(xpre, dxbc, conv_w, conv_b)
    ddt_raw, g_dtb = _softplus_bwd(ddt_f, ddt_b, dt_raw, dt_bias)
    ddt_raw = ddt_raw[:, :ndt]
    dhn1 = [_matmul("ssd_proj_z_dx", dz1, w_si_t, "nn", F32, MM_T, MM_T, MM_T),
            _matmul("ssd_proj_xbc_dx", dpre, w_si_t, "nn", F32, MM_T, MM_T, MM_T, b_koff=si)]
    gw_si_t = _matmul("ssd_proj_z_dw", dz1, hn1, "tn", BF16, MM_T, MM_T, MM_T, dest=(n_ssd_in, 0, None))
    gw_si_t = _matmul("ssd_proj_xbc_dw", dpre, hn1, "tn", BF16, MM_T, MM_T, MM_T, dest=(n_ssd_in, si, gw_si_t))
    gw_si_t = _matmul("ssd_proj_dt_dw", ddt_raw, hn1, "tn", BF16, ndt, MM_T, MM_T, dest=(n_ssd_in, si + sxbc, gw_si_t))

    l1_grads = [gw_so.reshape(NDEV, si // NDEV, dm), gw_si_t.reshape(NDEV, n_ssd_in // NDEV, dm),
                _pack_ssd_small_blocks(g_cw, g_cb, g_snw)]
    l1_handles, l1_token = _exchange_start("l1_grads_start", l1_grads, _landing_zones("l1_grads_place", l1_grads, "scatter"),
                                           "scatter", dhn1[1])
    dx1, dy0, g_nw1, dsc1, dsh1, dgate0 = _norm_mod_bwd(
        "ssd_proj_dt_dx_norm1_bwd", (ddt_raw, w_si_t, "nn", ndt, dict(b_koff=si + sxbc)), x1, dhn1, dx2,
        nw[1], scale[1], shift[1], prev=(y0, gate[0] + l1_token[0:1, 0:1]))

    gw_ao = _matmul("attn_out_dw", a0, dy0, "tn", BF16, aw // 2, MM_T, MM_T)
    dos, dls, dz0 = _mix_bwd(dy0, w_ao, os_, lses, z0)
    datt = [_attn_bwd(g, qk, v, os_[g], lses[g], dos[g], dls[g]) for g in range(3)]
    dqkv = _rot_pack_bwd([t[0] for t in datt], [t[1] for t in datt], [t[2] for t in datt], tabs)
    gw_ai = _matmul("proj_qkv_dw", hn0, dqkv, "tn", BF16, MM_T, wcol, MM_T, out_blocks=3 * aw // wcol, dest=(NDEV, 0, None))
    gw_ai = _matmul("proj_z_dw", hn0, dz0, "tn", BF16, MM_T, wcol, MM_T, out_blocks=aw // wcol,
                    dest=(NDEV, 3 * aw // wcol, gw_ai))
    after_start = lambda acc, t: acc + t
    zero_row = lambda token: jnp.tile(token[0:1], (1, dm // 128))
    l0_grads = [gw_ai, gw_ao.reshape(NDEV, aw // NDEV, dm)]
    pair_handles, pair_token = _exchange_start("l0_pair_start", l0_grads, _landing_zones("l0_pair_place", l0_grads, "pair"),
                                               "pair", dqkv)
    dhn0_z = _matmul("proj_z_dx", dz0, w_ai, "nt", F32, MM_T, MM_T, aw, b_koff=3 * aw, n_out=dm, epilogue=after_start,
                     ncols=(zero_row(pair_token),))
    from_sibling, l0_grads = _exchange_wait("l0_pair_wait", pair_handles, "pair", dhn0_z, with_sources=True)
    chip_sums = [_pair_sum(f"l0_pair_sum_{a}", g, s) for a, (g, s) in enumerate(zip(l0_grads, from_sibling))]
    l0_handles, l0_token = _exchange_start("l0_grads_start", chip_sums, _landing_zones("l0_grads_place", chip_sums, "chips"),
                                           "chips", dhn0_z)
    dx0, g_nw0, dsc0, dsh0 = _norm_mod_bwd(
        "proj_qkv_dx_norm0_bwd", (dqkv, w_ai, "nt", aw, dict(n_out=dm)), x0, [dhn0_z], dx1,
        nw[0], scale[0], shift[0] + zero_row(l0_token))

    small_g = [_pack_small([dsh0, dsc0, dgate0, dsh1, dsc1, dgate1, g_nw0, g_nw1, g_fnw], g_dtb, [dalog_f, dalog_b], g_d, loss_part)]
    sm_handles, sm_token = _exchange_start("small_grads_start", small_g, _landing_zones("small_grads_place", small_g, "gather"),
                                           "gather", dx0)

    whole = (slice(None), slice(None))
    r_so, r_si, r_small = _exchange_wait("l1_grads_wait", l1_handles, "scatter", sm_token)
    si_out = [o.T for o in _adamw("adamw_ssd_w_in", ssd_w_in[0].T, r_si, m_ssd_w_in[0].T, v_ssd_w_in[0].T, n_ssd_in // NDEV, 256)]
    so_out = _adamw("adamw_ssd_w_out", ssd_w_out[0], r_so, m_ssd_w_out[0], v_ssd_w_out[0], 256)
    cw_cols = ssd_conv_w.shape[2]
    cw_out, cb_out, snw_out = _adamw_windows(
        "adamw_ssd_small", r_small,
        [(ssd_conv_w, m_ssd_conv_w, v_ssd_conv_w), (ssd_conv_b, m_ssd_conv_b, v_ssd_conv_b),
         (ssd_norm_w, m_ssd_norm_w, v_ssd_norm_w)],
        [(0, slice(0, CONV_WIDTH), slice(0, cw_cols), (0, slice(None), slice(None))),
         (1, slice(5, 6), slice(0, cw_cols), whole), (2, slice(6, 7), slice(0, si // NDEV), whole)])
    r_ai, r_ao = _exchange_wait("l0_grads_wait", l0_handles, "chips", so_out[0])
    ai_out = _adamw("adamw_attn_w_in", attn_w_in[0], r_ai, m_attn_w_in[0], v_attn_w_in[0], 256)
    ao_out = _adamw("adamw_attn_w_out", attn_w_out[0], r_ao, m_attn_w_out[0], v_attn_w_out[0], 192)

    (small_all,) = _exchange_wait("small_grads_wait", sm_handles, "gather", ai_out[0])
    full = slice(0, PACK_COLS)
    nhd = SSD_HEADS
    windows = [(0, slice(3 * i + k, 3 * i + k + 1), full, (slice(i, i + 1), slice(k * dm, (k + 1) * dm)))
               for i in range(2) for k in range(3)]
    windows += [(1, slice(6 + i, 7 + i), full, (slice(i, i + 1), slice(None))) for i in range(2)]
    windows += [(2, slice(8, 9), full, whole)]
    windows += [(3 + q, slice(9, 10), slice(2 * nhd * q + nhd * j, 2 * nhd * q + nhd * (j + 1)), (0, slice(j, j + 1), slice(None)))
                for q in range(2) for j in range(2)]
    windows += [(5, slice(9, 10), slice(4 * nhd, 5 * nhd), whole)]
    as_row = lambda a: a.reshape(1, dm)
    mb_out, nw_out, fnw_out, dtb_out, alog_out, d_out, loss = _adamw_windows(
        "adamw_small", small_all,
        [(mod_b, m_mod_b, v_mod_b), (norm_w, m_norm_w, v_norm_w), (fnw, as_row(m_final_norm_w), as_row(v_final_norm_w)),
         (ssd_dt_bias, m_ssd_dt_bias, v_ssd_dt_bias), (ssd_a_log, m_ssd_a_log, v_ssd_a_log), (ssd_d, m_ssd_d, v_ssd_d)],
        windows, extra=(slice(9, 10), slice(256, 257)))
    loss = loss.reshape(())

    ncol = mod_w.shape[2]
    dmod_all = small_all[:, 0:6, :].reshape(NDEV, 2, 3 * dm)
    dmod_sh = lax.dynamic_slice_in_dim(dmod_all, me * ncol, ncol, axis=2).transpose(1, 0, 2)
    g_modw = _mod_grad(c_all, dmod_sh).reshape(1, 2 * dm, ncol)
    modw_out = _adamw("adamw_mod_w", mod_w.reshape(2 * dm, ncol), g_modw, m_mod_w.reshape(2 * dm, ncol),
                      v_mod_w.reshape(2 * dm, ncol), 256)

    per_kind = []
    for k in range(4):
        per_kind.append([
            nw_out[k], modw_out[k].reshape(mod_w.shape), mb_out[k], ai_out[k][None], ao_out[k][None], si_out[k][None],
            cw_out[k], cb_out[k], dtb_out[k], alog_out[k], d_out[k], snw_out[k], so_out[k][None], fnw_out[k].reshape(dm)])
    return (loss, dx0.reshape(x.shape), *per_kind[0], *per_kind[1], *per_kind[2], *per_kind[3])


def _pack_ssd_small_blocks(g_cw, g_cb, g_nw):
    nper = g_cw.shape[1] // NDEV
    nwper = g_nw.shape[1] // NDEV

    def body(cw_ref, cb_ref, nw_ref, o_ref):
        o_ref[...] = jnp.zeros_like(o_ref)
        for d in range(NDEV):
            o_ref[d, 0:5, :] = cw_ref[:, d * nper:(d + 1) * nper]
            o_ref[d, 5:6, :] = cb_ref[:, d * nper:(d + 1) * nper]
            o_ref[d, 6:7, 0:nwper] = nw_ref[:, d * nwper:(d + 1) * nwper]

    return pl.pallas_call(body, name="pack_ssd_small_grads", out_shape=jax.ShapeDtypeStruct((NDEV, 8, nper), F32))(g_cw, g_cb, g_nw)
```

```python
import functools
import math

import jax
import jax.numpy as jnp
from jax import lax
from jax.experimental import pallas as pl
from jax.experimental.pallas import tpu as pltpu

F32 = jnp.float32
BF16 = jnp.bfloat16
HI = lax.Precision.HIGHEST
MESH = pl.DeviceIdType.MESH
NDEV = 8

NORM_EPS = 1e-6
ROPE_THETA = 500000.0
ROT_DIM = 16
HEAD_DIM = 64
DILATIONS = (1, 4, 16)
BAND = 64
NEG_BIG = -1e30
CHUNK = 128
SSD_HEADS = 32
SSD_GROUPS = 8
CONV_WIDTH = 5

ADAM_LR = 0.001
ADAM_B1 = 0.9
ADAM_B2 = 0.999
ADAM_EPS = 1e-08
ADAM_WD = 0.01
ADAM_STEP = 10

VMEM_BIG = 56 * 1024 * 1024
MM_T = 1024


def _params(sem=None, vmem=None):
    kw = {}
    if sem is not None:
        kw["dimension_semantics"] = sem
    if vmem is not None:
        kw["vmem_limit_bytes"] = vmem
    return pltpu.CompilerParams(**kw)


def _dg(a, b, ca, cb, prec=None):
    return lax.dot_general(a, b, (((ca,), (cb,)), ((), ())), preferred_element_type=F32, precision=prec)


def _nn(a, b):
    return _dg(a.astype(BF16), b.astype(BF16), 1, 0)


def _nt(a, b):
    return _dg(a.astype(BF16), b.astype(BF16), 1, 1)


def _tn(a, b):
    return _dg(a.astype(BF16), b.astype(BF16), 0, 0)


def _hnn(a, b):
    return _dg(a, b, 1, 0, HI)


@jax.custom_vjp
def _bnn(a, b):
    return _nn(a, b)


_bnn.defvjp(lambda a, b: (_nn(a, b), (a, b)), lambda r, g: (_nt(g, r[1]), _tn(r[0], g)))


@jax.custom_vjp
def _bnt(a, b):
    return _nt(a, b)


_bnt.defvjp(lambda a, b: (_nt(a, b), (a, b)), lambda r, g: (_nn(g, r[1]), _tn(g, r[0])))


@jax.custom_vjp
def _btn(a, b):
    return _tn(a, b)


_btn.defvjp(lambda a, b: (_tn(a, b), (a, b)), lambda r, g: (_nt(r[1], g), _nn(r[0], g)))


def _silu(x):
    return x * jax.nn.sigmoid(x)


def _b_spec(b, mode, tn, tk, no, ko, jk):
    if mode == "nt":
        return pl.BlockSpec((tn, tk), lambda *g: (jk(*g)[0] + no, jk(*g)[1] + ko))
    return pl.BlockSpec((tk, tn), lambda *g: (jk(*g)[1] + ko, jk(*g)[0] + no))


def _matmul(name, a, b, mode, out_dtype, tm, tn, tk, *, epilogue=None, tiled=(), mrows=(), ncols=(),
            b_noff=0, b_koff=0, n_out=None, out_blocks=None, dest=None):
    if mode == "tn":
        K, M = a.shape
    else:
        M, K = a.shape
    N = n_out if n_out is not None else (b.shape[0] if mode == "nt" else b.shape[1])
    tm, tn, tk = min(tm, M), min(tn, N), min(tk, K)
    assert M % tm == 0 and N % tn == 0 and K % tk == 0, (name, M, N, K, tm, tn, tk)
    assert b_noff % tn == 0 and b_koff % tk == 0
    no, ko = b_noff // tn, b_koff // tk
    nk = K // tk
    if mode == "tn":
        a_spec = pl.BlockSpec((tk, tm), lambda i, j, k: (k, i))
    else:
        a_spec = pl.BlockSpec((tm, tk), lambda i, j, k: (i, k))
    specs = [a_spec, _b_spec(b, mode, tn, tk, no, ko, lambda i, j, k: (j, k))]
    specs += [pl.BlockSpec((tm, tn), lambda i, j, k: (i, j)) for _ in tiled]
    specs += [pl.BlockSpec((tm, r.shape[1]), lambda i, j, k: (i, 0)) for r in mrows]
    specs += [pl.BlockSpec((1, tn), lambda i, j, k: (0, j)) for _ in ncols]
    total, off, earlier = dest if dest is not None else (None, 0, None)
    if out_blocks is None:
        assert off % tm == 0
        mo = off // tm
        out_shape = jax.ShapeDtypeStruct((M if total is None else total, N), out_dtype)
        out_spec = pl.BlockSpec((tm, tn), lambda i, j, k: (i + mo, j))
    else:
        nper = N // out_blocks
        assert nper % tn == 0
        jb = nper // tn
        out_shape = jax.ShapeDtypeStruct((out_blocks if total is None else total, M, nper), out_dtype)
        out_spec = pl.BlockSpec((None, tm, tn), lambda i, j, k: (j // jb + off, i, j % jb))
    if earlier is not None:
        assert earlier.shape == out_shape.shape and earlier.dtype == out_shape.dtype
    ne = len(tiled) + len(mrows) + len(ncols)
    dot = {"nn": _nn, "nt": _nt, "tn": _tn}[mode]

    def body(a_ref, b_ref, *rest):
        extras, o_ref = rest[:ne], rest[ne]

        def finish(acc):
            if epilogue is not None:
                acc = epilogue(acc, *[e[...] for e in extras])
            o_ref[...] = acc.astype(o_ref.dtype)

        if nk == 1:
            finish(dot(a_ref[...], b_ref[...]))
        else:
            acc_ref = rest[ne + 1]
            k = pl.program_id(2)

            @pl.when(k == 0)
            def _():
                acc_ref[...] = jnp.zeros_like(acc_ref)

            acc_ref[...] += dot(a_ref[...], b_ref[...])

            @pl.when(k == nk - 1)
            def _():
                finish(acc_ref[...])

    args = [a, b, *tiled, *mrows, *ncols]
    aliases = {}
    if earlier is not None:
        specs.append(pl.BlockSpec(memory_space=pl.ANY))
        aliases = {len(args): 0}
        args.append(earlier)

    def body_with_dest(*refs):
        body(*refs[:2 + ne], *refs[2 + ne + (earlier is not None):])

    return pl.pallas_call(
        body_with_dest, name=name, out_shape=out_shape, grid=(M // tm, N // tn, nk),
        in_specs=specs, out_specs=out_spec, input_output_aliases=aliases,
        scratch_shapes=[] if nk == 1 else [pltpu.VMEM((tm, tn), F32)],
        compiler_params=_params(("parallel", "parallel", "arbitrary"), VMEM_BIG),
    )(*args)


def _matmul_rows(name, a, b, mode, tm, tk, fn, rows, consts, outs, accs, *, n_out=None, b_noff=0, b_koff=0):
    rl = [(t, t.shape[1], 0) if not isinstance(t, tuple) else t for t in rows]
    make_a = a if callable(a) else None
    M, K = (rl[0][0].shape[0], b.shape[1 if mode == "nt" else 0]) if make_a else a.shape
    N = n_out if n_out is not None else (b.shape[0] if mode == "nt" else b.shape[1])
    tm, tk = min(tm, M), min(tk, K)
    assert M % tm == 0 and K % tk == 0 and b_koff % tk == 0 and b_noff % N == 0, (name, M, N, K)
    no, ko, nk = b_noff // N, b_koff // tk, K // tk
    assert make_a is None or nk == 1
    nr, nc, no_, na = len(rl), len(consts), len(outs), len(accs)
    dot = _nt if mode == "nt" else _nn

    def body(*refs):
        a_ref, b_ref, rest = (None, refs[0], refs[1:]) if make_a else (refs[0], refs[1], refs[2:])
        r_refs, c_refs = rest[:nr], rest[nr:nr + nc]
        o_refs, acc_refs = rest[nr + nc:nr + nc + no_], rest[nr + nc + no_:nr + nc + no_ + na]
        i, k = pl.program_id(0), pl.program_id(1)

        def finish(prod, *made):
            res_o, res_a = fn(prod, *made, *[r[...] for r in r_refs], *[c[...] for c in c_refs])
            for r, v in zip(o_refs, res_o, strict=True):
                r[...] = v.astype(r.dtype)
            if acc_refs:
                @pl.when(i == 0)
                def _():
                    for r in acc_refs:
                        r[...] = jnp.zeros_like(r)

                for r, v in zip(acc_refs, res_a, strict=True):
                    r[...] += v

        if make_a:
            left = make_a(*[r[...] for r in r_refs], *[c[...] for c in c_refs])
            finish(dot(left, b_ref[...]), left)
        elif nk == 1:
            finish(dot(a_ref[...], b_ref[...]))
        else:
            prod_ref = rest[-1]

            @pl.when(k == 0)
            def _():
                prod_ref[...] = jnp.zeros_like(prod_ref)

            prod_ref[...] += dot(a_ref[...], b_ref[...])

            @pl.when(k == nk - 1)
            def _():
                finish(prod_ref[...])

    b_spec = _b_spec(b, mode, N, tk, no, ko, lambda i, k: (0, k))
    in_specs = ([] if make_a else [pl.BlockSpec((tm, tk), lambda i, k: (i, k))]) + [b_spec]
    in_specs += [pl.BlockSpec((tm, w), functools.partial(lambda i, k, cb: (i, cb), cb=cb)) for (_, w, cb) in rl]
    in_specs += [pl.BlockSpec(c.shape, lambda i, k: (0, 0)) for c in consts]
    out_specs = [pl.BlockSpec((tm, c), lambda i, k: (i, 0)) for (c, _) in outs]
    out_specs += [pl.BlockSpec(shp, lambda i, k: (0, 0)) for shp in accs]
    out_shape = [jax.ShapeDtypeStruct((M, c), dt) for (c, dt) in outs] + [jax.ShapeDtypeStruct(shp, F32) for shp in accs]
    res = pl.pallas_call(
        body, name=name, out_shape=out_shape, grid=(M // tm, nk), in_specs=in_specs, out_specs=out_specs,
        scratch_shapes=[] if nk == 1 else [pltpu.VMEM((tm, N), F32)],
        compiler_params=_params(("arbitrary" if accs else "parallel", "arbitrary"), VMEM_BIG),
    )(*([] if make_a else [a]), b, *[t[0] for t in rl], *consts)
    return res[:no_], res[no_:]


def _rowwise(name, fn, tiled, consts, outs, accs, ts):
    tl = [(t, t.shape[1], 0) if not isinstance(t, tuple) else t for t in tiled]
    s_len = tl[0][0].shape[0]
    assert s_len % ts == 0
    nt_, nc_, no_ = len(tl), len(consts), len(outs)

    def body(*refs):
        t_refs, c_refs = refs[:nt_], refs[nt_:nt_ + nc_]
        o_refs, a_refs = refs[nt_ + nc_:nt_ + nc_ + no_], refs[nt_ + nc_ + no_:]
        res_o, res_a = fn(*[r[...] for r in t_refs], *[r[...] for r in c_refs])
        for r, v in zip(o_refs, res_o, strict=True):
            r[...] = v.astype(r.dtype)
        if a_refs:
            @pl.when(pl.program_id(0) == 0)
            def _():
                for r in a_refs:
                    r[...] = jnp.zeros_like(r)

            for r, v in zip(a_refs, res_a, strict=True):
                r[...] += v

    in_specs = [pl.BlockSpec((ts, w), functools.partial(lambda i, cb: (i, cb), cb=cb)) for (_, w, cb) in tl]
    in_specs += [pl.BlockSpec(c.shape, lambda i: (0, 0)) for c in consts]
    out_specs = [pl.BlockSpec((ts, c), lambda i: (i, 0)) for (c, _) in outs]
    out_specs += [pl.BlockSpec(shp, lambda i: (0, 0)) for shp in accs]
    out_shape = [jax.ShapeDtypeStruct((s_len, c), dt) for (c, dt) in outs]
    out_shape += [jax.ShapeDtypeStruct(shp, F32) for shp in accs]
    res = pl.pallas_call(
        body, name=name, out_shape=out_shape, grid=(s_len // ts,), in_specs=in_specs, out_specs=out_specs,
        compiler_params=_params(("arbitrary",) if accs else ("parallel",), VMEM_BIG),
    )(*[t[0] for t in tl], *consts)
    return res[:no_], res[no_:]


def _norm_mod_fn(x, nw, sc, sh):
    r = lax.rsqrt(jnp.mean(x * x, axis=-1, keepdims=True) + NORM_EPS)
    return (x * r * nw) * (1.0 + sc) + sh


def _norm_mod_fwd(name, x, nw, sc, sh):
    (hn,), _ = _rowwise(name, lambda x, nw, sc, sh: ([_norm_mod_fn(x, nw, sc, sh)], []),
                        [x], [nw, sc, sh], [(x.shape[1], BF16)], [], 512)
    return hn


def _norm_mod_bwd(name, last, x, dhn_parts, dres, nw, sc, sh, prev=None):
    n = len(dhn_parts)
    d = x.shape[1]
    a, b, mode, tk, kw = last

    def fn(dhn, x, *rest):
        for p in rest[:n]:
            dhn = dhn + p
        dres, rest = rest[n], rest[n + 1:]
        y_prev, (nw, sc, sh), gate = (rest[0], rest[1:4], rest[4]) if prev is not None else (None, rest[0:3], None)
        r = lax.rsqrt(jnp.mean(x * x, axis=-1, keepdims=True) + NORM_EPS)
        xh = x * r
        dxh = dhn * (nw * (1.0 + sc))
        dx = r * (dxh - xh * jnp.mean(dxh * xh, axis=-1, keepdims=True)) + dres
        along = jnp.sum(dhn * xh, axis=0, keepdims=True)
        dnw, dsc, dsh = along * (1.0 + sc), along * nw, jnp.sum(dhn, axis=0, keepdims=True)
        if prev is None:
            return [dx], [dnw, dsc, dsh]
        return [dx, gate * dx], [dnw, dsc, dsh, jnp.sum(dx * y_prev, axis=0, keepdims=True)]

    rows = [x, *dhn_parts, dres] + ([prev[0]] if prev is not None else [])
    consts = [nw, sc, sh] + ([prev[1]] if prev is not None else [])
    outs = [(d, F32)] + ([(d, BF16)] if prev is not None else [])
    res_o, res_a = _matmul_rows(name, a, b, mode, 512, tk, fn, rows, consts, outs, [(1, d)] * (3 + (prev is not None)), **kw)
    return (*res_o, *res_a)


def _rope_tables(pos_col, inv_row):
    def fn(pos, inv):
        ang = pos.astype(F32) * inv
        e = lax.broadcasted_iota(jnp.int32, (1, 128), 1) % HEAD_DIM
        cos, sin = jnp.cos(ang), jnp.sin(ang)
        half = ROT_DIM // 2
        return [jnp.where(e < ROT_DIM, cos, 1.0), jnp.where(e < half, -sin, 0.0),
                jnp.where((e >= half) & (e < ROT_DIM), sin, 0.0)], []

    (c, sa, sb), _ = _rowwise("rope_tables", fn, [pos_col], [inv_row], [(128, F32)] * 3, [], 512)
    return c, sa, sb


def _rot_fwd(t, c, sa, sb):
    n = t.shape[1]
    rep = n // 128
    c, sa, sb = (jnp.tile(u, (1, rep)) for u in (c, sa, sb))
    return t * c + pltpu.roll(t, n - ROT_DIM // 2, 1) * sa + pltpu.roll(t, ROT_DIM // 2, 1) * sb


def _rot_bwd(g, c, sa, sb):
    n = g.shape[1]
    rep = n // 128
    c, sa, sb = (jnp.tile(u, (1, rep)) for u in (c, sa, sb))
    return g * c + pltpu.roll(g * sa, ROT_DIM // 2, 1) + pltpu.roll(g * sb, n - ROT_DIM // 2, 1)


ATT_TQ = 128


def _attn_tiles(l):
    tk = ATT_TQ + 2 * BAND
    return (l, l) if l <= tk else (ATT_TQ, tk)


def _attn_specs(g, s_len):
    def blk(off):
        return pl.BlockSpec((s_len, 128), functools.partial(lambda hp, off: (0, off + hp), off=off))

    return blk(4 * g), blk(12 + 4 * g), blk(4 * g), blk(0)


def _attn_tile_geometry(t, d, l):
    tq, tk = _attn_tiles(l)
    nts = l // tq
    r = t // nts
    ts = t % nts
    q0 = ts * tq
    ws = jnp.clip(q0 - BAND, 0, l - tk)
    kind = jnp.where(ts == 0, 0, jnp.where(ts == nts - 1, 2, 1))
    if d == 1:
        return pl.ds(pl.multiple_of(q0, tq), tq), pl.ds(pl.multiple_of(ws, BAND), tk), kind
    return pl.ds(r + d * q0, tq, stride=d), pl.ds(r + d * ws, tk, stride=d), kind


def _attn_fill_bias(bias_ref):
    _, tq2, tk = bias_ref.shape
    iq = lax.broadcasted_iota(jnp.int32, (tq2, 1), 0) % (tq2 // 2)
    ik = lax.broadcasted_iota(jnp.int32, (1, tk), 1)
    for i, off in enumerate((0, -BAND, -2 * BAND)):
        bias_ref[i] = jnp.where(jnp.abs(ik + off - iq) <= BAND, 0.0, NEG_BIG)


def _split_heads(t, in_h):
    zero = jnp.zeros_like(t)
    return jnp.concatenate([jnp.where(in_h[0], t, zero), jnp.where(in_h[1], t, zero)], axis=0)


def _attn_fwd(g, qk, v):
    s_len = qk.shape[0]
    d = DILATIONS[g]
    l = s_len // d
    tq, tk = _attn_tiles(l)
    assert l % tq == 0 and l >= tk
    q_spec, k_spec, v_spec, o_spec = _attn_specs(g, s_len)
    scale = 1.0 / math.sqrt(HEAD_DIM)

    def body(q_ref, k_ref, v_ref, o_ref, lse_ref, bias_ref):
        lane = lax.broadcasted_iota(jnp.int32, (1, 128), 1)
        in_h = [lane < HEAD_DIM, lane >= HEAD_DIM]
        _attn_fill_bias(bias_ref)

        def tile(t, carry):
            rows, win, kind = _attn_tile_geometry(t, d, l)
            q = (q_ref[rows, :] * scale).astype(BF16)
            k = k_ref[win, :].astype(BF16)
            vv = v_ref[win, :].astype(BF16)
            s = _nt(_split_heads(q, in_h), k) + bias_ref[kind]
            m = jnp.max(s, axis=1, keepdims=True)
            p = jnp.exp(s - m)
            den = jnp.sum(p, axis=1, keepdims=True)
            out = _nn(p, vv) / den
            lse = m + jnp.log(den)
            o_ref[rows, :] = jnp.where(in_h[0], out[:tq], out[tq:])
            lse_ref[rows, :] = jnp.where(in_h[0], lse[:tq], lse[tq:])
            return carry

        lax.fori_loop(0, s_len // tq, tile, 0, unroll=8 * ATT_TQ // tq)

    return pl.pallas_call(
        body, name=f"attn_fwd_g{g}", grid=(4,),
        out_shape=[jax.ShapeDtypeStruct((s_len, 512), F32)] * 2,
        in_specs=[q_spec, k_spec, v_spec], out_specs=[o_spec, o_spec],
        scratch_shapes=[pltpu.VMEM((3, 2 * tq, tk), F32)],
        compiler_params=_params(("parallel",), VMEM_BIG),
    )(qk, qk, v)


def _attn_bwd(g, qk, v, o, lse, do, dlse):
    s_len = qk.shape[0]
    d = DILATIONS[g]
    l = s_len // d
    tq, tk = _attn_tiles(l)
    q_spec, k_spec, v_spec, o_spec = _attn_specs(g, s_len)
    scale = 1.0 / math.sqrt(HEAD_DIM)

    def body(q_ref, k_ref, v_ref, o_ref, lse_ref, do_ref, dlse_ref, dq_ref, dk_ref, dv_ref, bias_ref):
        lane = lax.broadcasted_iota(jnp.int32, (1, 128), 1)
        in_h = [lane < HEAD_DIM, lane >= HEAD_DIM]
        dk_ref[...] = jnp.zeros_like(dk_ref)
        dv_ref[...] = jnp.zeros_like(dv_ref)
        _attn_fill_bias(bias_ref)

        def tile(t, carry):
            rows, win, kind = _attn_tile_geometry(t, d, l)
            k, vv = k_ref[win, :].astype(BF16), v_ref[win, :].astype(BF16)
            dout, lse_t, dlse_t = do_ref[rows, :], lse_ref[rows, :], dlse_ref[rows, :]
            od = dout * o_ref[rows, :]
            q2 = _split_heads((q_ref[rows, :] * scale).astype(BF16), in_h)
            do2 = _split_heads(dout.astype(BF16), in_h)
            head_col = lambda a: jnp.concatenate([a[:, 0:1], a[:, HEAD_DIM:HEAD_DIM + 1]], axis=0)
            delta = jnp.concatenate([jnp.sum(jnp.where(m, od, 0.0), axis=1, keepdims=True) for m in in_h], axis=0)
            p = jnp.exp(_nt(q2, k) + bias_ref[kind] - head_col(lse_t))
            ds = (p * (_nt(do2, vv) - delta + head_col(dlse_t))).astype(BF16)
            dq2 = _nn(ds, k) * scale
            dq_ref[rows, :] = jnp.where(in_h[0], dq2[:tq], dq2[tq:])
            dk_ref[win, :] += _tn(ds, q2)
            dv_ref[win, :] += _tn(p, do2)
            return carry

        lax.fori_loop(0, s_len // tq, tile, 0, unroll=8 * ATT_TQ // tq)

    return pl.pallas_call(
        body, name=f"attn_bwd_g{g}", grid=(4,),
        out_shape=[jax.ShapeDtypeStruct((s_len, 512), F32)] * 3,
        in_specs=[q_spec, k_spec, v_spec, o_spec, o_spec, o_spec, o_spec], out_specs=[o_spec] * 3,
        scratch_shapes=[pltpu.VMEM((3, 2 * tq, tk), F32)],
        compiler_params=_params(("parallel",), VMEM_BIG),
    )(qk, qk, v, o, lse, do, dlse)


def _mix_weights(ls):
    mx = jnp.maximum(jnp.maximum(ls[0], ls[1]), ls[2])
    es = [jnp.exp(x - mx) for x in ls]
    tot = es[0] + es[1] + es[2]
    return [e / tot for e in es]


def _attn_out(os_, lses, z, x, gate, w_out):
    s_len, dm = x.shape
    tm = 256
    wdt = 512
    z, z_block = z

    def body(o0, o1, o2, l0, l1, l2, z_ref, x_ref, g_ref, w_ref, a_ref, y_ref, x1_ref):
        alphas = _mix_weights([l0[...], l1[...], l2[...]])
        y = jnp.zeros((tm, dm), F32)
        for g, o_ref in enumerate((o0, o1, o2)):
            a_g = (o_ref[...] * alphas[g] * _silu(z_ref[:, g * wdt:(g + 1) * wdt])).astype(BF16)
            a_ref[:, g * wdt:(g + 1) * wdt] = a_g
            y = y + _nn(a_g, w_ref[g * wdt:(g + 1) * wdt, :])
        y_ref[...] = y
        x1_ref[...] = x_ref[...] + g_ref[...] * y

    row = lambda c: pl.BlockSpec((tm, c), lambda i: (i, 0))
    return pl.pallas_call(
        body, name="attn_out", grid=(s_len // tm,),
        out_shape=[jax.ShapeDtypeStruct((s_len, 3 * wdt), BF16), jax.ShapeDtypeStruct((s_len, dm), F32),
                   jax.ShapeDtypeStruct((s_len, dm), F32)],
        in_specs=[row(wdt)] * 6 + [pl.BlockSpec((tm, 3 * wdt), lambda i: (i, z_block)), row(dm),
                                   pl.BlockSpec((1, dm), lambda i: (0, 0)), pl.BlockSpec(w_out.shape, lambda i: (0, 0))],
        out_specs=[row(3 * wdt), row(dm), row(dm)],
        compiler_params=_params(("parallel",), VMEM_BIG),
    )(*os_, *lses, z, x, gate, w_out)


def _mix_bwd(dy, w_out, os_, lses, z):
    wdt = 512

    def fn(da, o0, o1, o2, l0, l1, l2, z):
        os_t, ls = [o0, o1, o2], [l0, l1, l2]
        alphas = _mix_weights(ls)
        hi = lax.broadcasted_iota(jnp.int32, (2 * wdt, wdt), 0) % wdt // HEAD_DIM
        hj = lax.broadcasted_iota(jnp.int32, (2 * wdt, wdt), 1) // HEAD_DIM
        seg = (hi == hj).astype(BF16)
        head_sum = lambda t: _dg(jnp.concatenate(_bf16_parts(t, 2), axis=1), seg, 1, 0)
        dos, dal, dzs = [], [], []
        for g in range(3):
            zg = z[:, g * wdt:(g + 1) * wdt]
            sig = jax.nn.sigmoid(zg)
            dag = da[:, g * wdt:(g + 1) * wdt]
            dmix = dag * zg * sig
            dzs.append(dag * os_t[g] * alphas[g] * (sig * (1.0 + zg * (1.0 - sig))))
            dos.append(dmix * alphas[g])
            dal.append(head_sum(dmix * os_t[g]))
        mean = alphas[0] * dal[0] + alphas[1] * dal[1] + alphas[2] * dal[2]
        dls = [alphas[g] * (dal[g] - mean) for g in range(3)]
        return dos + dls + [jnp.concatenate(dzs, axis=1)], []

    outs, _ = _matmul_rows("attn_out_dx_mix_bwd", dy, w_out, "nt", 256, dy.shape[1], fn, [*os_, *lses, (z[0], 3 * wdt, z[1])], [],
                           [(wdt, F32)] * 6 + [(3 * wdt, BF16)], [])
    return outs[:3], outs[3:6], outs[6]


def _rot_pack_bwd(dqs, dks, dvs, tabs):
    wdt = 512

    def fn(*args):
        grads, (c, sa, sb) = args[:9], args[9:]
        cols = [_rot_bwd(gq, c, sa, sb) for gq in grads[:6]] + list(grads[6:])
        return [jnp.concatenate(cols, axis=1)], []

    (out,), _ = _rowwise("rot_pack_bwd", fn, [*dqs, *dks, *dvs, *tabs], [], [(9 * wdt, BF16)], [], 512)
    return out


CONV_CB = 128
CONV_R = 256
CONV_PAD = 8


def _conv_taps(buf, base, off, sign):
    return [buf[pl.ds(base + off + sign * j, CONV_R), :] for j in range(CONV_WIDTH)]


def _conv_tap_sum(taps, w):
    acc = None
    for j, t in enumerate(taps):
        term = t * w[j:j + 1, :]
        acc = term if acc is None else acc + term
    return acc


def _conv_fwd(xpre, cw, cb):
    s_len, ch = xpre.shape
    nchunk = s_len // CONV_R

    def body(x_ref, w_ref, b_ref, o_ref, xp):
        zero = jnp.zeros((CONV_PAD, CONV_CB), F32)
        xp[0:CONV_PAD, :] = zero
        xp[s_len + CONV_PAD:s_len + 2 * CONV_PAD, :] = zero

        def fill(ci, carry):
            base = pl.multiple_of(ci * CONV_R, CONV_R)
            xp[pl.ds(base + CONV_PAD, CONV_R), :] = x_ref[pl.ds(base, CONV_R), :]
            return carry

        lax.fori_loop(0, nchunk, fill, 0)
        w = w_ref[...]
        b = b_ref[...]

        def chunk(ci, carry):
            base = pl.multiple_of(ci * CONV_R, CONV_R)
            u = _conv_tap_sum(_conv_taps(xp, base, CONV_PAD - CONV_WIDTH // 2, 1), w) + b
            o_ref[pl.ds(base, CONV_R), :] = _silu(u)
            return carry

        lax.fori_loop(0, nchunk, chunk, 0, unroll=2)

    col = lambda r: pl.BlockSpec((r, CONV_CB), lambda j: (0, j))
    return pl.pallas_call(
        body, name="conv_fwd", grid=(ch // CONV_CB,), out_shape=jax.ShapeDtypeStruct((s_len, ch), F32),
        in_specs=[col(s_len), col(CONV_WIDTH), col(1)], out_specs=col(s_len),
        scratch_shapes=[pltpu.VMEM((s_len + 2 * CONV_PAD, CONV_CB), F32)],
        compiler_params=_params(("parallel",), VMEM_BIG),
    )(xpre, cw, cb)


def _conv_bwd(xpre, da, cw, cb):
    s_len, ch = xpre.shape
    nchunk = s_len // CONV_R
    half = CONV_WIDTH // 2

    def body(x_ref, da_ref, w_ref, b_ref, dx_ref, gw_ref, gb_ref, xp, dcp):
        zero = jnp.zeros((CONV_PAD, CONV_CB), F32)
        for buf in (xp, dcp):
            buf[0:CONV_PAD, :] = zero
            buf[s_len + CONV_PAD:s_len + 2 * CONV_PAD, :] = zero

        def fill(ci, carry):
            base = pl.multiple_of(ci * CONV_R, CONV_R)
            xp[pl.ds(base + CONV_PAD, CONV_R), :] = x_ref[pl.ds(base, CONV_R), :]
            return carry

        lax.fori_loop(0, nchunk, fill, 0)
        w = w_ref[...]
        b = b_ref[...]

        def first(ci, carry):
            base = pl.multiple_of(ci * CONV_R, CONV_R)
            taps = _conv_taps(xp, base, CONV_PAD - half, 1)
            u = _conv_tap_sum(taps, w) + b
            sig = jax.nn.sigmoid(u)
            dc = da_ref[pl.ds(base, CONV_R), :] * (sig * (1.0 + u * (1.0 - sig)))
            dcp[pl.ds(base + CONV_PAD, CONV_R), :] = dc
            gb = carry[0] + jnp.sum(dc, axis=0, keepdims=True)
            gws = [carry[1 + j] + jnp.sum(dc * taps[j], axis=0, keepdims=True) for j in range(CONV_WIDTH)]
            return (gb, *gws)

        z1 = jnp.zeros((1, CONV_CB), F32)
        sums = lax.fori_loop(0, nchunk, first, (z1,) * (1 + CONV_WIDTH), unroll=2)
        gb_ref[...] = sums[0]
        for j in range(CONV_WIDTH):
            gw_ref[j:j + 1, :] = sums[1 + j]

        def second(ci, carry):
            base = pl.multiple_of(ci * CONV_R, CONV_R)
            dx_ref[pl.ds(base, CONV_R), :] = _conv_tap_sum(_conv_taps(dcp, base, CONV_PAD + half, -1), w).astype(dx_ref.dtype)
            return carry

        lax.fori_loop(0, nchunk, second, 0, unroll=2)

    col = lambda r: pl.BlockSpec((r, CONV_CB), lambda j: (0, j))
    return pl.pallas_call(
        body, name="conv_bwd", grid=(ch // CONV_CB,),
        out_shape=[jax.ShapeDtypeStruct((s_len, ch), BF16), jax.ShapeDtypeStruct((CONV_WIDTH, ch), F32),
                   jax.ShapeDtypeStruct((1, ch), F32)],
        in_specs=[col(s_len), col(s_len), col(CONV_WIDTH), col(1)],
        out_specs=[col(s_len), col(CONV_WIDTH), col(1)],
        scratch_shapes=[pltpu.VMEM((s_len + 2 * CONV_PAD, CONV_CB), F32)] * 2,
        compiler_params=_params(("parallel",), VMEM_BIG),
    )(xpre, da, cw, cb)


SSD_GW = 256
SSD_N = 128
SSD_DTW = 128


def _bf16_parts(x, n):
    parts, rest = [], x
    for _ in range(n):
        p = rest.astype(BF16)
        parts.append(p)
        rest = rest - p.astype(F32)
    return parts


@jax.custom_vjp
def _expand(x, e):
    eb = e.astype(BF16)
    return _dg(jnp.concatenate(_bf16_parts(x, 2), axis=1), jnp.concatenate([eb, eb], axis=0), 1, 0)


def _expand_fwd(x, e):
    return _expand(x, e), e


def _expand_bwd(e, g):
    return _dg(g.astype(BF16), e.astype(BF16), 1, 1), jnp.zeros_like(e)


_expand.defvjp(_expand_fwd, _expand_bwd)


@jax.custom_vjp
def _running_sum(tri, x):
    tb = tri.astype(BF16)
    return sum(_dg(tb, p, 1, 0) for p in _bf16_parts(x, 3))


def _running_sum_fwd(tri, x):
    return _running_sum(tri, x), tri


def _running_sum_bwd(tri, g):
    tb = tri.astype(BF16)
    return jnp.zeros_like(tri), sum(_dg(tb, p, 0, 0) for p in _bf16_parts(g, 3))


_running_sum.defvjp(_running_sum_fwd, _running_sum_bwd)


def _pick_col(a, h):
    @jax.custom_vjp
    def pick(a):
        return a[:, h:h + 1]

    pick.defvjp(lambda a: (a[:, h:h + 1], None),
                lambda _, g: (g * (lax.broadcasted_iota(jnp.int32, (1, a.shape[1]), 1) == h).astype(F32),))
    return pick(a)


def _pick_row(a, h):
    @jax.custom_vjp
    def pick(a):
        return a[h:h + 1, :]

    pick.defvjp(lambda a: (a[h:h + 1, :], None),
                lambda _, g: (g * (lax.broadcasted_iota(jnp.int32, (a.shape[0], 1), 0) == h).astype(F32),))
    return pick(a)


def _ssd_mask(dirn):
    ri = lax.broadcasted_iota(jnp.int32, (CHUNK, CHUNK), 0)
    cj = lax.broadcasted_iota(jnp.int32, (CHUNK, CHUNK), 1)
    return (cj <= ri) if dirn == 0 else (cj >= ri)


def _ssd_rowsel(dirn):
    last = CHUNK - 1 if dirn == 0 else 0
    return (lax.broadcasted_iota(jnp.int32, (CHUNK, 1), 0) == last).astype(F32)


def _ssd_chunk_pre(dirn):
    nh = SSD_DTW

    def f(dt, alog):
        da = dt * (-jnp.exp(alog))
        cum = _running_sum(_ssd_mask(dirn).astype(F32), da)
        tot = jnp.sum(cum * _ssd_rowsel(dirn), axis=0, keepdims=True)
        hh = lax.broadcasted_iota(jnp.int32, (nh, SSD_HEADS * HEAD_DIM), 0)
        jj = lax.broadcasted_iota(jnp.int32, (nh, SSD_HEADS * HEAD_DIM), 1)
        expand = (hh == dirn * SSD_HEADS + jj // HEAD_DIM).astype(F32)
        return cum, cum.T, _expand(dt, expand), _expand(jnp.exp(tot - cum), expand), _expand(jnp.exp(cum), expand)

    return f


def _ssd_group_fn(g, dirn, stacked):
    def f(xs, bm, cm, st, cum, cum_t, dt_e, w_e, ce_e):
        mask = _ssd_mask(dirn)
        xdt = xs * dt_e
        cd_e = jnp.sum(ce_e * _ssd_rowsel(dirn), axis=0, keepdims=True)
        cb = _bnt(cm, bm)
        lane_head = lax.broadcasted_iota(jnp.int32, (1, SSD_GW), 1) // HEAD_DIM
        y = _bnn(cm, st) * ce_e
        decayed, inputs = [], []
        for j in range(4):
            hidx = dirn * SSD_HEADS + 4 * g + j
            col, row = _pick_col(cum, hidx), _pick_row(cum_t, hidx)
            dec = cb * jnp.exp(jnp.where(mask, col - row, NEG_BIG))
            head = (lane_head == j).astype(F32)
            if stacked:
                decayed.append(dec)
                inputs.append(xdt * head)
            else:
                y = y + _bnn(dec, xdt) * head
        if stacked:
            y = y + _bnn(jnp.concatenate(decayed, axis=1), jnp.concatenate(inputs, axis=0))
        st_out = st * cd_e + _btn(bm, xdt * w_e)
        return y, st_out

    return f


def _ssd_in_specs(kk):
    ln = CHUNK
    return [pl.BlockSpec((ln, 2048), lambda i: (kk(i), 0)),
            pl.BlockSpec((ln, 1024), lambda i: (kk(i), 2)),
            pl.BlockSpec((ln, 1024), lambda i: (kk(i), 3)),
            pl.BlockSpec((ln, SSD_DTW), lambda i: (kk(i), 0)),
            pl.BlockSpec((1, SSD_DTW), lambda i: (0, 0))]


def _ssd_fwd(xbc, dt, alog, dirn, prior=None):
    s_len = xbc.shape[0]
    nc = s_len // CHUNK
    kk = (lambda i: i) if dirn == 0 else (lambda i: nc - 1 - i)

    def body(x_ref, b_ref, c_ref, dt_ref, al_ref, *rest):
        prior_ref = rest[0] if prior is not None else None
        y_ref, sts_ref, st = rest[prior is not None:]

        @pl.when(pl.program_id(0) == 0)
        def _():
            st[...] = jnp.zeros_like(st)

        sts_ref[0] = st[...]
        cum, cum_t, dt_e, w_e, ce_e = _ssd_chunk_pre(dirn)(dt_ref[...], al_ref[...])
        for g in range(SSD_GROUPS):
            xc = slice(g * SSD_GW, (g + 1) * SSD_GW)
            gc = slice(g * SSD_N, (g + 1) * SSD_N)
            y, st_new = _ssd_group_fn(g, dirn, True)(x_ref[:, xc], b_ref[:, gc], c_ref[:, gc], st[:, xc], cum, cum_t,
                                               dt_e[:, xc], w_e[:, xc], ce_e[:, xc])
            y_ref[:, xc] = y if prior is None else y + prior_ref[:, xc]
            st[:, xc] = st_new

    return pl.pallas_call(
        body, name=f"ssd_fwd_d{dirn}", grid=(nc,),
        out_shape=[jax.ShapeDtypeStruct((s_len, 2048), F32), jax.ShapeDtypeStruct((nc, SSD_N, 2048), F32)],
        in_specs=_ssd_in_specs(kk) + ([pl.BlockSpec((CHUNK, 2048), lambda i: (kk(i), 0))] if prior is not None else []),
        out_specs=[pl.BlockSpec((CHUNK, 2048), lambda i: (kk(i), 0)),
                   pl.BlockSpec((1, SSD_N, 2048), lambda i: (kk(i), 0, 0))],
        scratch_shapes=[pltpu.VMEM((SSD_N, 2048), F32)],
        compiler_params=_params(("arbitrary",), VMEM_BIG),
    )(xbc, xbc, xbc, dt, alog, *([prior] if prior is not None else []))


def _ssd_bwd(xbc, dt, alog, states, dy, d_e, dirn, prior=None):
    s_len = xbc.shape[0]
    nc = s_len // CHUNK
    kk = (lambda i: nc - 1 - i) if dirn == 0 else (lambda i: i)

    def body(x_ref, b_ref, c_ref, dt_ref, al_ref, sts_ref, dy_ref, de_ref, *rest):
        prior_ref = rest[0] if prior is not None else None
        dx_ref, ddt_ref, dal_ref, dst = rest[prior is not None:]
        plus_prior = (lambda v, cols: v + prior_ref[:, cols]) if prior is not None else (lambda v, cols: v)

        @pl.when(pl.program_id(0) == 0)
        def _():
            dst[...] = jnp.zeros_like(dst)
            dal_ref[...] = jnp.zeros_like(dal_ref)

        (cum, cum_t, dt_e, w_e, ce_e), pre_vjp = jax.vjp(_ssd_chunk_pre(dirn), dt_ref[...], al_ref[...])
        dcum = jnp.zeros_like(cum)
        dcum_t = jnp.zeros_like(cum_t)
        d_dt_e, d_w_e, d_ce_e = [], [], []
        for g in range(SSD_GROUPS):
            xc = slice(g * SSD_GW, (g + 1) * SSD_GW)
            gc = slice(g * SSD_N, (g + 1) * SSD_N)
            _, vjp = jax.vjp(_ssd_group_fn(g, dirn, False), x_ref[:, xc], b_ref[:, gc], c_ref[:, gc], sts_ref[0, :, xc], cum, cum_t,
                             dt_e[:, xc], w_e[:, xc], ce_e[:, xc])
            dyg = dy_ref[:, xc]
            dxs, dbm, dcm, dst_g, dcum_g, dcum_t_g, ddte_g, dwe_g, dcee_g = vjp((dyg, dst[:, xc]))
            if dirn == 0:
                dxs = dxs + dyg * de_ref[:, xc]
            bc, cc = slice(2048 + g * SSD_N, 2048 + (g + 1) * SSD_N), slice(3072 + g * SSD_N, 3072 + (g + 1) * SSD_N)
            dx_ref[:, xc] = plus_prior(dxs, xc)
            dx_ref[:, bc] = plus_prior(dbm, bc)
            dx_ref[:, cc] = plus_prior(dcm, cc)
            dst[:, xc] = dst_g
            dcum = dcum + dcum_g
            dcum_t = dcum_t + dcum_t_g
            d_dt_e.append(ddte_g)
            d_w_e.append(dwe_g)
            d_ce_e.append(dcee_g)
        ddt, dal = pre_vjp((dcum, dcum_t, jnp.concatenate(d_dt_e, axis=1), jnp.concatenate(d_w_e, axis=1),
                            jnp.concatenate(d_ce_e, axis=1)))
        ddt_ref[...] = ddt
        dal_ref[...] += dal

    return pl.pallas_call(
        body, name=f"ssd_bwd_d{dirn}", grid=(nc,),
        out_shape=[jax.ShapeDtypeStruct((s_len, 4096), F32), jax.ShapeDtypeStruct((s_len, SSD_DTW), F32),
                   jax.ShapeDtypeStruct((1, SSD_DTW), F32)],
        in_specs=_ssd_in_specs(kk) + [pl.BlockSpec((1, SSD_N, 2048), lambda i: (kk(i), 0, 0)),
                                      pl.BlockSpec((CHUNK, 2048), lambda i: (kk(i), 0)),
                                      pl.BlockSpec((1, 2048), lambda i: (0, 0))]
        + ([pl.BlockSpec((CHUNK, 4096), lambda i: (kk(i), 0))] if prior is not None else []),
        out_specs=[pl.BlockSpec((CHUNK, 4096), lambda i: (kk(i), 0)),
                   pl.BlockSpec((CHUNK, SSD_DTW), lambda i: (kk(i), 0)),
                   pl.BlockSpec((1, SSD_DTW), lambda i: (0, 0))],
        scratch_shapes=[pltpu.VMEM((SSD_N, 2048), F32)],
        compiler_params=_params(("arbitrary",), VMEM_BIG),
    )(xbc, xbc, xbc, dt, alog, states, dy, d_e, *([prior] if prior is not None else []))


def _gate_norm_fn(y, xs, z, d_e, nw):
    yg = (y + xs * d_e) * _silu(z)
    return yg * lax.rsqrt(jnp.mean(yg * yg, axis=-1, keepdims=True) + NORM_EPS) * nw


def _gate_norm_bwd(dy, w_out, y, xbc, z, d_e, nw):
    def fn(du, y, xs, z, d_e, nw):
        sig = jax.nn.sigmoid(z)
        gate = z * sig
        ysum = y + xs * d_e
        yg = ysum * gate
        r = lax.rsqrt(jnp.mean(yg * yg, axis=-1, keepdims=True) + NORM_EPS)
        t = du * nw
        dyg = t * r - yg * (jnp.mean(t * yg, axis=-1, keepdims=True) * (r * r * r))
        dys = dyg * gate
        dz = dyg * ysum * (sig * (1.0 + z * (1.0 - sig)))
        dnw = jnp.sum(du * yg * r, axis=0, keepdims=True)
        dde = jnp.sum(dys * xs, axis=0, keepdims=True)
        hh = lax.broadcasted_iota(jnp.int32, (2048, SSD_HEADS), 0) // HEAD_DIM
        jj = lax.broadcasted_iota(jnp.int32, (2048, SSD_HEADS), 1)
        return [dys, dz], [dnw, _hnn(jnp.broadcast_to(dde, (8, 2048)), (hh == jj).astype(F32))[0:1]]

    (dys, dz), (g_nw, g_d) = _matmul_rows("ssd_out_dx_gate_norm_bwd", dy, w_out, "nt", 256, dy.shape[1], fn,
                                          [y, (xbc, 2048, 0), z], [d_e, nw], [(2048, F32), (2048, BF16)],
                                          [(1, 2048), (1, SSD_HEADS)])
    return dys, dz, g_nw, g_d


def _ssd_tail_loss(y, xbc, z, d_e, snw, w_out, x1, tgt, gate, fnw):
    dm = x1.shape[1]
    si = y.shape[1]

    def make_u(y, xs, z, x1, tgt, d_e, snw, gate, fnw):
        return _gate_norm_fn(y, xs, z, d_e, snw).astype(BF16)

    def fn(y1, u, y, xs, z, x1, tgt, d_e, snw, gate, fnw):
        x2 = x1 + gate * y1
        r = lax.rsqrt(jnp.mean(x2 * x2, axis=-1, keepdims=True) + NORM_EPS)
        xh = x2 * r
        err = xh * fnw - tgt
        loss = 0.5 * jnp.sum(jnp.mean(err * err, axis=-1, keepdims=True), axis=0, keepdims=True)
        dy = err * (1.0 / dm)
        dxh = dy * fnw
        dx2 = r * (dxh - xh * jnp.mean(dxh * xh, axis=-1, keepdims=True))
        dfnw = jnp.sum(dy * xh, axis=0, keepdims=True)
        return [u, dx2, gate * dx2], [dfnw, jnp.sum(dx2 * y1, axis=0, keepdims=True), jnp.broadcast_to(loss, (1, 128))]

    (u, dx2, dy1), (g_fnw, dgate, loss) = _matmul_rows(
        "ssd_out_loss", make_u, w_out, "nn", 256, si, fn, [y, (xbc, si, 0), z, x1, tgt], [d_e, snw, gate, fnw],
        [(si, BF16), (dm, F32), (dm, BF16)], [(1, dm), (1, dm), (1, 128)])
    return u, dx2, dy1, g_fnw, dgate, loss


def _softplus_fwd(dt_raw, bias):
    (dt,), _ = _rowwise("dt_softplus", lambda r, b: ([jax.nn.softplus(r + b)], []), [dt_raw], [bias],
                        [(dt_raw.shape[1], F32)], [], 512)
    return dt


def _softplus_bwd(ddt_f, ddt_b, dt_raw, bias):
    def fn(df, db, r, b):
        g = (df + db) * jax.nn.sigmoid(r + b)
        return [g], [jnp.sum(g, axis=0, keepdims=True)]

    w = dt_raw.shape[1]
    (g,), (gb,) = _rowwise("dt_softplus_bwd", fn, [ddt_f, ddt_b, dt_raw], [bias], [(w, BF16)], [(1, w)], 512)
    return g, gb


def _mod_part(c_all, mod_w):
    nl, _, ncol = mod_w.shape
    nb = c_all.shape[0]

    def body(c_ref, w_ref, o_ref):
        cond = _silu(c_ref[...])
        for i in range(nl):
            o_ref[i * nb:(i + 1) * nb, :] = _nn(cond, w_ref[i])

    return pl.pallas_call(body, name="mod_part", out_shape=jax.ShapeDtypeStruct((nl * nb, ncol), F32),
                          compiler_params=_params(None, VMEM_BIG))(c_all, mod_w)


def _mod_finish(mod_nb, mod_b, norm_w, tokens):
    nl, dm = norm_w.shape

    def body(a_ref, b_ref, nw_ref, *rest):
        tok_refs, o_refs = rest[:len(tokens)], rest[len(tokens):]
        tok = sum(t[0:1, 0:1] for t in tok_refs)
        for i in range(nl):
            for k in range(3):
                cols = slice(k * dm, (k + 1) * dm)
                o_refs[4 * i + k][...] = a_ref[i:i + 1, cols] + b_ref[i:i + 1, cols]
            o_refs[4 * i + 3][...] = nw_ref[i:i + 1, :] + tok

    rows = pl.pallas_call(body, name="mod_finish", out_shape=[jax.ShapeDtypeStruct((1, dm), F32)] * (4 * nl))(
        mod_nb, mod_b, norm_w, *tokens)
    return [rows[4 * i:4 * i + 4] for i in range(nl)]


def _mod_grad(c_all, dmod_sh):
    nl, nb, ncol = dmod_sh.shape
    dm = c_all.shape[1]

    def body(c_ref, d_ref, o_ref):
        cond = _silu(c_ref[...])
        for i in range(nl):
            o_ref[i] = _tn(cond, d_ref[i])

    return pl.pallas_call(body, name="mod_grad", out_shape=jax.ShapeDtypeStruct((nl, dm, ncol), F32),
                          compiler_params=_params(None, VMEM_BIG))(c_all, dmod_sh)


PACK_ROWS = 16
PACK_COLS = 1024


def _pack_small(rows, b64, a64s, d32, extra):
    nr, na = len(rows), len(a64s)

    def body(*refs):
        o_ref = refs[-1]
        o_ref[...] = jnp.zeros_like(o_ref)
        for i in range(nr):
            o_ref[i:i + 1, :] = refs[i][...]
        b_ref, a_refs, d_ref, e_ref = refs[nr], refs[nr + 1:nr + 1 + na], refs[nr + 1 + na], refs[nr + 2 + na]
        o_ref[nr:nr + 1, 0:64] = b_ref[:, 0:64]
        o_ref[nr:nr + 1, 64:128] = sum(a[:, 0:64] for a in a_refs)
        o_ref[nr:nr + 1, 128:160] = d_ref[...]
        o_ref[nr:nr + 1, 256:384] = e_ref[...]

    return pl.pallas_call(body, name="pack_small", out_shape=jax.ShapeDtypeStruct((PACK_ROWS, PACK_COLS), F32))(
        *rows, b64, *a64s, d32, extra)


def _pack_ssd_small(cw, cb, nw):
    def body(cw_ref, cb_ref, nw_ref, o_ref):
        o_ref[...] = jnp.zeros_like(o_ref)
        o_ref[0:5, :] = cw_ref[...]
        o_ref[5:6, :] = cb_ref[...]
        o_ref[6:7, 0:256] = nw_ref[...]

    return pl.pallas_call(body, name="pack_ssd_small", out_shape=jax.ShapeDtypeStruct((8, 512), F32))(cw, cb, nw)


def _sum_parts(p_ref):
    g = p_ref[0].astype(F32)
    for s in range(1, p_ref.shape[0]):
        g = g + p_ref[s].astype(F32)
    return g


def _adam_update(w, g, m, v):
    m2 = ADAM_B1 * m + (1.0 - ADAM_B1) * g
    v2 = ADAM_B2 * v + (1.0 - ADAM_B2) * (g * g)
    m_hat = m2 / (1.0 - ADAM_B1 ** ADAM_STEP)
    v_hat = v2 / (1.0 - ADAM_B2 ** ADAM_STEP)
    return -ADAM_LR * (m_hat / (jnp.sqrt(v_hat) + ADAM_EPS) + ADAM_WD * w), m2, v2


def _adamw_windows(name, parts, params, windows, extra=None):
    n = len(params)

    def body(p_ref, *rest):
        ins, outs = rest[:3 * n], rest[3 * n:]
        g = _sum_parts(p_ref)
        for pi, rows, cols, idx in windows:
            w_ref, m_ref, v_ref = ins[3 * pi:3 * pi + 3]
            gw = g[rows, cols]
            dw, m2, v2 = _adam_update(w_ref[idx], gw, m_ref[idx], v_ref[idx])
            for o_ref, val in zip(outs[4 * pi:4 * pi + 4], (gw, dw, m2, v2), strict=True):
                o_ref[idx] = val
        if extra is not None:
            outs[4 * n][...] = g[extra[0], extra[1]]

    out_shape = [jax.ShapeDtypeStruct(w.shape, F32) for (w, _, _) in params for _ in range(4)]
    if extra is not None:
        out_shape.append(jax.ShapeDtypeStruct((extra[0].stop - extra[0].start, extra[1].stop - extra[1].start), F32))
    res = pl.pallas_call(body, name=name, out_shape=out_shape)(parts, *[a for p in params for a in p])
    return [res[4 * i:4 * i + 4] for i in range(n)] + ([res[4 * n]] if extra is not None else [])


def _adamw(name, w, parts, m, v, tr, tc=None):
    r_, c_ = w.shape
    p_ = parts.shape[0]
    tr = min(tr, r_)
    tc = c_ if tc is None else tc
    assert r_ % tr == 0 and c_ % tc == 0

    def body(w_ref, p_ref, m_ref, v_ref, g_ref, d_ref, m2_ref, v2_ref):
        g = _sum_parts(p_ref)
        g_ref[...] = g
        d_ref[...], m2_ref[...], v2_ref[...] = _adam_update(w_ref[...], g, m_ref[...], v_ref[...])

    blk = pl.BlockSpec((tr, tc), lambda i, j: (i, j))
    return pl.pallas_call(
        body, name=name, grid=(r_ // tr, c_ // tc), out_shape=[jax.ShapeDtypeStruct((r_, c_), F32)] * 4,
        in_specs=[blk, pl.BlockSpec((p_, tr, tc), lambda i, j: (0, i, j)), blk, blk], out_specs=[blk] * 4,
        compiler_params=_params(("parallel", "parallel"), VMEM_BIG),
    )(w, parts, m, v)


def _dev_index(p):
    return 4 * p[0] + 2 * p[1] + p[2]


def _all_gather(name, xs):
    n = len(xs)
    hbm = pl.BlockSpec(memory_space=pl.ANY)

    def body(*refs):
        x_refs, o_refs = refs[:n], refs[n:2 * n]
        send_sems, recv_sems, local_sems = refs[2 * n:]
        x, y, c = lax.axis_index("x"), lax.axis_index("y"), lax.axis_index("c")
        me, sibling = (x, y, c), (x, y, 1 - c)
        chips = [(1 - x, y), (x, 1 - y), (1 - x, 1 - y)]

        def place(a, block):
            return o_refs[a].at[_dev_index(block)]

        def copy(a, k, block, to, src=None):
            dst = place(a, block)
            return pltpu.make_async_remote_copy(
                src_ref=dst if src is None else src, dst_ref=dst, send_sem=send_sems.at[a, k],
                recv_sem=recv_sems.at[a, k], device_id=to, device_id_type=MESH)

        mine = [pltpu.make_async_copy(x_refs[a], place(a, me), local_sems.at[a]) for a in range(n)]
        for cp in mine:
            cp.start()
        first = []
        for a in range(n):
            first.append(copy(a, 0, me, sibling, src=x_refs[a]))
            first += [copy(a, 1 + j, me, (*chip, c), src=x_refs[a]) for j, chip in enumerate(chips)]
        for cp in first:
            cp.start()
        passed = []
        for j, chip in enumerate(chips):
            for a in range(n):
                copy(a, 1 + j, (*chip, c), me).wait_recv()
                cp = copy(a, 4 + j, (*chip, c), sibling)
                cp.start()
                passed.append(cp)
        for a in range(n):
            copy(a, 0, sibling, me).wait_recv()
            for j, chip in enumerate(chips):
                copy(a, 4 + j, (*chip, 1 - c), me).wait_recv()
        for cp in first + passed:
            cp.wait_send()
        for cp in mine:
            cp.wait()

    return pl.pallas_call(
        body, name=name, out_shape=[jax.ShapeDtypeStruct((NDEV, *x.shape), x.dtype) for x in xs],
        in_specs=[hbm] * n, out_specs=[hbm] * n,
        scratch_shapes=[pltpu.SemaphoreType.DMA((n, 7)), pltpu.SemaphoreType.DMA((n, 7)), pltpu.SemaphoreType.DMA((n,))],
    )(*xs)


_HBM = pl.BlockSpec(memory_space=pltpu.HBM)
_SEM = pl.BlockSpec(memory_space=pltpu.SEMAPHORE)
_EFFECT = pltpu.SideEffectType.DATAFLOW_SIDE_EFFECTING


def _mesh_position():
    return lax.axis_index("x"), lax.axis_index("y"), lax.axis_index("c")


def _peers(me):
    return [(k, tuple(1 - v if (k >> b) & 1 else v for v, b in zip(me, (2, 1, 0)))) for k in range(1, NDEV)]


def _column_window(ref, block, width):
    return ref.at[:, pl.ds(pl.multiple_of(_dev_index(block) * width, 128), width)]


def _columns_pass_on(name, gathered):
    width = gathered.shape[1] // NDEV
    hbm = pl.BlockSpec(memory_space=pl.ANY)

    def body(g_ref, o_ref, send_sems, recv_sems):
        x, y, c = _mesh_position()
        sibling = (x, y, 1 - c)
        x_side, y_side, diagonal = (1 - x, y), (x, 1 - y), (1 - x, 1 - y)
        first = c == 0
        via = (jnp.where(first, 1 - x, x), jnp.where(first, y, 1 - y))
        to = (jnp.where(first, x, 1 - x), jnp.where(first, 1 - y, y))

        def copy(k, block, device):
            window = _column_window(o_ref, block, width)
            return pltpu.make_async_remote_copy(src_ref=window, dst_ref=window, send_sem=send_sems.at[k],
                                                recv_sem=recv_sems.at[k], device_id=device, device_id_type=MESH)

        across = copy(0, (*via, c), (*to, c))
        onward = [copy(1 + j, (*chip, c), sibling) for j, chip in enumerate((x_side, y_side))]
        for cp in [across] + onward:
            cp.start()
        copy(0, (*diagonal, c), (x, y, c)).wait_recv()
        last = copy(3, (*diagonal, c), sibling)
        last.start()
        for j, chip in enumerate((x_side, y_side, diagonal)):
            copy(1 + j, (*chip, 1 - c), (x, y, c)).wait_recv()
        for cp in [across, last] + onward:
            cp.wait_send()

    return pl.pallas_call(
        body, name=name, out_shape=jax.ShapeDtypeStruct(gathered.shape, gathered.dtype), in_specs=[hbm], out_specs=hbm,
        input_output_aliases={0: 0},
        scratch_shapes=[pltpu.SemaphoreType.DMA((NCHIP,)), pltpu.SemaphoreType.DMA((NCHIP,))],
    )(gathered)


NCHIP = NDEV // 2
EXCHANGE_COPIES = {"columns": NCHIP - 1, "gather": NDEV - 1, "scatter": NDEV - 1, "pair": NCHIP, "chips": NCHIP - 1}


def _landing_zones(name, xs, mode):
    x_, y_, c_ = _mesh_position()
    mine = (2 * x_ + y_ if mode == "chips" else _dev_index((x_, y_, c_))).astype(jnp.int32).reshape(1)
    lands = []
    for a, x in enumerate(xs):
        rows, cols = x.shape[-2:]
        if mode == "pair":
            lands.append(lax.empty((NCHIP, rows, cols), x.dtype))
            continue
        tr = 256 if rows % 256 == 0 else rows

        def body(me_ref, x_ref, o_ref):
            o_ref[...] = x_ref[...]

        if mode in ("gather", "columns"):
            in_spec = pl.BlockSpec((tr, cols), lambda i, me_ref: (i, 0))
        else:
            in_spec = pl.BlockSpec((None, tr, cols), lambda i, me_ref: (me_ref[0], i, 0))
        if mode == "columns":
            out_shape, out_spec = (rows, NDEV * cols), pl.BlockSpec((tr, cols), lambda i, me_ref: (i, me_ref[0]))
        else:
            out_shape = (NCHIP if mode == "chips" else NDEV, rows, cols)
            out_spec = pl.BlockSpec((None, tr, cols), lambda i, me_ref: (me_ref[0], i, 0))
        lands.append(pl.pallas_call(
            body, name=f"{name}_{a}", out_shape=jax.ShapeDtypeStruct(out_shape, x.dtype),
            grid_spec=pltpu.PrefetchScalarGridSpec(num_scalar_prefetch=1, grid=(rows // tr,), in_specs=[in_spec],
                                                   out_specs=out_spec),
            compiler_params=_params(("arbitrary",)),
        )(mine, x))
    return lands


def _exchange_copies(x_refs, land_refs, send_sems, recv_sems, mode):
    x_, y_, c_ = me = _mesh_position()
    per_array = EXCHANGE_COPIES[mode]
    out = []

    def add(a, k, src, dst, peer):
        sem = a * per_array + k
        out.append(pltpu.make_async_remote_copy(src_ref=src, dst_ref=dst, send_sem=send_sems.at[sem], recv_sem=recv_sems.at[sem],
                                                device_id=peer, device_id_type=MESH))

    for a, (x_ref, land_ref) in enumerate(zip(x_refs, land_refs)):
        if mode == "columns":
            for k, peer in enumerate([(x_, y_, 1 - c_), (1 - x_, y_, c_), (x_, 1 - y_, c_)]):
                add(a, k, x_ref, _column_window(land_ref, me, x_ref.shape[1]), peer)
        elif mode in ("gather", "scatter"):
            for k, peer in _peers(me):
                add(a, k - 1, x_ref.at[_dev_index(peer)] if mode == "scatter" else x_ref, land_ref.at[_dev_index(me)], peer)
        elif mode == "pair":
            for chip in range(NCHIP):
                add(a, chip, x_ref.at[2 * chip + 1 - c_], land_ref.at[chip], (x_, y_, 1 - c_))
        else:
            for k in range(1, NCHIP):
                px, py = (1 - x_ if k & 2 else x_), (1 - y_ if k & 1 else y_)
                add(a, k - 1, x_ref.at[2 * px + py], land_ref.at[2 * x_ + y_], (px, py, c_))
    return out


def _exchange_start(name, xs, lands, mode, dep, carry=False):
    n = len(xs)

    def body(*refs):
        x_refs, land_refs = refs[:n], refs[n:2 * n]
        send_sems, recv_sems = refs[2 * n + 1], refs[2 * n + 2]
        for cp in _exchange_copies(x_refs, land_refs, send_sems, recv_sems, mode):
            cp.start()
        if not carry:
            refs[-1][...] = jnp.zeros_like(refs[-1])

    sems = pltpu.SemaphoreType.DMA((n * EXCHANGE_COPIES[mode],))
    moved = [pltpu.with_memory_space_constraint(a, pltpu.HBM) for a in (*xs, *lands, *([dep] if carry else []))]
    res = pl.pallas_call(
        body, name=name,
        out_shape=(sems, sems, *[pltpu.HBM(a.shape, a.dtype) for a in moved],
                   *([] if carry else [jax.ShapeDtypeStruct((8, 128), F32)])),
        in_specs=[_HBM] * len(moved) + ([] if carry else [pl.BlockSpec(memory_space=pl.ANY)]),
        out_specs=(_SEM, _SEM, *[_HBM] * len(moved), *([] if carry else [pl.BlockSpec(memory_space=pltpu.VMEM)])),
        input_output_aliases={i: 2 + i for i in range(len(moved))},
        compiler_params=pltpu.CompilerParams(has_side_effects=_EFFECT),
    )(*moved, *([] if carry else [dep]))
    return res[:-1], res[-1]


def _exchange_wait(name, handles, mode, after, with_sources=False):
    send_sems, recv_sems = handles[0], handles[1]
    bufs = handles[2:]
    n = len(bufs) // 2
    afters = list(after) if isinstance(after, (list, tuple)) else [after]

    def body(*refs):
        x_refs, land_refs = refs[:n], refs[n:2 * n]
        s_sems, r_sems = refs[2 * n], refs[2 * n + 1]
        for cp in _exchange_copies(x_refs, land_refs, s_sems, r_sems, mode):
            cp.wait_send()
            cp.wait_recv()

    res = pl.pallas_call(
        body, name=name, out_shape=tuple(pltpu.HBM(a.shape, a.dtype) for a in bufs),
        in_specs=[_HBM] * (2 * n) + [_SEM, _SEM] + [pl.BlockSpec(memory_space=pl.ANY)] * len(afters),
        out_specs=tuple([_HBM] * (2 * n)), input_output_aliases={i: i for i in range(2 * n)},
        compiler_params=pltpu.CompilerParams(has_side_effects=_EFFECT),
    )(*bufs, send_sems, recv_sems, *afters)
    return (res[n:], res[:n]) if with_sources else res[n:]


def _pair_sum(name, x, from_sibling):
    _, rows, cols = x.shape
    tr = rows
    core = lax.axis_index("c").astype(jnp.int32).reshape(1)

    def body(c_ref, x_ref, s_ref, o_ref):
        o_ref[...] = (x_ref[...].astype(F32) + s_ref[...].astype(F32)).astype(o_ref.dtype)

    return pl.pallas_call(
        body, name=name, out_shape=jax.ShapeDtypeStruct((NCHIP, rows, cols), x.dtype),
        grid_spec=pltpu.PrefetchScalarGridSpec(
            num_scalar_prefetch=1, grid=(NCHIP, rows // tr),
            in_specs=[pl.BlockSpec((None, tr, cols), lambda j, i, c_ref: (2 * j + c_ref[0], i, 0)),
                      pl.BlockSpec((None, tr, cols), lambda j, i, c_ref: (j, i, 0))],
            out_specs=pl.BlockSpec((None, tr, cols), lambda j, i, c_ref: (j, i, 0))),
        compiler_params=_params(("parallel", "parallel")),
    )(core, x, from_sibling)


def kernel(x, c, positions, norm_w, mod_w, mod_b, attn_w_in, attn_w_out, ssd_w_in, ssd_conv_w, ssd_conv_b, ssd_dt_bias, ssd_a_log, ssd_d, ssd_norm_w, ssd_w_out, final_norm_w, loss_target, m_norm_w, m_mod_w, m_mod_b, m_attn_w_in, m_attn_w_out, m_ssd_w_in, m_ssd_conv_w, m_ssd_conv_b, m_ssd_dt_bias, m_ssd_a_log, m_ssd_d, m_ssd_norm_w, m_ssd_w_out, m_final_norm_w, v_norm_w, v_mod_w, v_mod_b, v_attn_w_in, v_attn_w_out, v_ssd_w_in, v_ssd_conv_w, v_ssd_conv_b, v_ssd_dt_bias, v_ssd_a_log, v_ssd_d, v_ssd_norm_w, v_ssd_w_out, v_final_norm_w):
    s_len, dm = x.shape[1], x.shape[2]
    me = 4 * lax.axis_index("x") + 2 * lax.axis_index("y") + lax.axis_index("c")
    x0 = x.reshape(s_len, dm)
    tgt = loss_target.reshape(s_len, dm)
    aw = 3 * 512
    si = 2 * dm
    sxbc = 2 * si
    n_ssd_in = ssd_w_in.shape[2] * NDEV

    (c_all,) = _all_gather("gather_c", [c])
    c_all = c_all.reshape(NDEV, dm)
    part = _mod_part(c_all, mod_w)
    (part_all,) = _all_gather("gather_mod", [part])
    mod_nb = jnp.stack([lax.dynamic_index_in_dim(part_all, i * NDEV + me, axis=1, keepdims=False).reshape(3 * dm)
                        for i in range(2)])

    wcol = attn_w_in.shape[2]
    ai_shard = [attn_w_in[0].astype(BF16)]
    ai_handles, ai_token = _exchange_start("attn_w_in_start", ai_shard, _landing_zones("attn_w_in_place", ai_shard, "columns"),
                                           "columns", part_all)
    (shift0, scale0, gate0, nw0), (shift1, scale1, gate1, nw1) = _mod_finish(mod_nb, mod_b, norm_w, [ai_token])
    shift, scale, gate, nw = [shift0, shift1], [scale0, scale1], [gate0, gate1], [nw0, nw1]
    hn0 = _norm_mod_fwd("norm0", x0, nw[0], scale[0], shift[0])
    inv_freq = ROPE_THETA ** (-jnp.arange(0, ROT_DIM, 2, dtype=F32) / ROT_DIM)
    per_head = jnp.concatenate([inv_freq, inv_freq, jnp.zeros(HEAD_DIM - ROT_DIM, F32)])
    inv_row = jnp.tile(per_head, 128 // HEAD_DIM).reshape(1, 128)
    tabs = _rope_tables(positions.reshape(s_len, 1), inv_row)
    ssd_small = _pack_ssd_small(ssd_conv_w[0], ssd_conv_b, ssd_norm_w)
    ao_shard = [attn_w_out[0].astype(BF16)]
    late_shards = [ssd_w_in[0].T.astype(BF16), ssd_w_out[0].astype(BF16), ssd_small]
    ao_lands = _landing_zones("w_out_place", ao_shard, "gather")
    late_lands = _landing_zones("weights_place", late_shards, "gather")
    (w_ai,) = _exchange_wait("attn_w_in_wait", ai_handles, "columns", [hn0, *tabs, *ao_lands, *late_lands])
    w_ai = _columns_pass_on("gather_attn_w_in_rest", w_ai)

    ao_handles, w_ai = _exchange_start("w_out_start", ao_shard, ao_lands, "gather", w_ai, carry=True)
    w_handles, w_ai = _exchange_start("weights_start", late_shards, late_lands, "gather", w_ai, carry=True)

    qk = _matmul("proj_qk", hn0, w_ai, "nn", F32, MM_T, MM_T, dm, epilogue=_rot_fwd, mrows=tabs, n_out=2 * aw)
    v = _matmul("proj_vz", hn0, w_ai, "nn", F32, MM_T, MM_T, dm, b_noff=2 * aw, n_out=2 * aw)
    z0 = (v, 1)
    att = [_attn_fwd(g, qk, v) for g in range(3)]
    os_, lses = [a[0] for a in att], [a[1] for a in att]
    (g_ao,) = _exchange_wait("w_out_wait", ao_handles, "gather", lses[2])
    a0, y0, x1 = _attn_out(os_, lses, z0, x0, gate[0], g_ao.reshape(aw, dm))

    hn1 = _norm_mod_fwd("norm1", x1, nw[1], scale[1], shift[1])
    g_si, g_so, g_small = _exchange_wait("weights_wait", w_handles, "gather", hn1)
    w_ao = g_ao.reshape(aw, dm)
    w_si_t = g_si.reshape(n_ssd_in, dm)
    w_so = g_so.reshape(si, dm)
    conv_w = g_small[:, 0:CONV_WIDTH, :].transpose(1, 0, 2).reshape(CONV_WIDTH, sxbc)
    conv_b = g_small[:, 5, :].reshape(1, sxbc)
    snw = g_small[:, 6, 0:si // NDEV].reshape(1, si)
    ndt = 2 * SSD_HEADS
    z1 = _matmul("ssd_proj_z", hn1, w_si_t, "nt", F32, MM_T, MM_T, dm, n_out=si)
    xpre = _matmul("ssd_proj_xbc", hn1, w_si_t, "nt", F32, MM_T, MM_T, dm, b_noff=si, n_out=sxbc)
    dt_raw = _matmul("ssd_proj_dt", hn1, w_si_t, "nt", F32, MM_T, ndt, dm, b_noff=si + sxbc, n_out=ndt)
    xbc = _conv_fwd(xpre, conv_w, conv_b)
    widen = lambda a: jnp.pad(a, ((0, 0), (0, SSD_DTW - ndt)))
    dt_raw = widen(dt_raw)
    dt_bias = widen(ssd_dt_bias.reshape(1, ndt))
    alog = widen(ssd_a_log.reshape(1, ndt))
    dt = _softplus_fwd(dt_raw, dt_bias)
    y_f, st_f = _ssd_fwd(xbc, dt, alog, 0)
    y_fb, st_b = _ssd_fwd(xbc, dt, alog, 1, prior=y_f)
    d_e = jnp.repeat(ssd_d.reshape(SSD_HEADS), HEAD_DIM).reshape(1, si)

    fnw = final_norm_w.reshape(1, dm)
    u, dx2, dy1, g_fnw, dgate1, loss_part = _ssd_tail_loss(y_fb, xbc, z1, d_e, snw, w_so, x1, tgt, gate[1], fnw)
    gw_so = _matmul("ssd_out_dw", u, dy1, "tn", BF16, MM_T, MM_T, MM_T)
    dys, dz1, g_snw, g_d = _gate_norm_bwd(dy1, w_so, y_fb, xbc, z1, d_e, snw)
    dxbc_f, ddt_f, dalog_f = _ssd_bwd(xbc, dt, alog, st_f, dys, d_e, 0)
    dxbc, ddt_b, dalog_b = _ssd_bwd(xbc, dt, alog, st_b, dys, d_e, 1, prior=dxbc_f)
    dpre, g_cw, g_cb = _conv_bwd(xpre, dxbc, conv_w, conv_b)
    ddt_raw, g_dtb = _softplus_bwd(ddt_f, ddt_b, dt_raw, dt_bias)
    ddt_raw = ddt_raw[:, :ndt]
    dhn1 = [_matmul("ssd_proj_z_dx", dz1, w_si_t, "nn", F32, MM_T, MM_T, MM_T),
            _matmul("ssd_proj_xbc_dx", dpre, w_si_t, "nn", F32, MM_T, MM_T, MM_T, b_koff=si)]
    gw_si_t = _matmul("ssd_proj_z_dw", dz1, hn1, "tn", BF16, MM_T, MM_T, MM_T, dest=(n_ssd_in, 0, None))
    gw_si_t = _matmul("ssd_proj_xbc_dw", dpre, hn1, "tn", BF16, MM_T, MM_T, MM_T, dest=(n_ssd_in, si, gw_si_t))
    gw_si_t = _matmul("ssd_proj_dt_dw", ddt_raw, hn1, "tn", BF16, ndt, MM_T, MM_T, dest=(n_ssd_in, si + sxbc, gw_si_t))

    l1_grads = [gw_so.reshape(NDEV, si // NDEV, dm), gw_si_t.reshape(NDEV, n_ssd_in // NDEV, dm),
                _pack_ssd_small_blocks(g_cw, g_cb, g_snw)]
    l1_handles, l1_token = _exchange_start("l1_grads_start", l1_grads, _landing_zones("l1_grads_place", l1_grads, "scatter"),
                                           "scatter", dhn1[1])
    dx1, dy0, g_nw1, dsc1, dsh1, dgate0 = _norm_mod_bwd(
        "ssd_proj_dt_dx_norm1_bwd", (ddt_raw, w_si_t, "nn", ndt, dict(b_koff=si + sxbc)), x1, dhn1, dx2,
        nw[1], scale[1], shift[1], prev=(y0, gate[0] + l1_token[0:1, 0:1]))

    gw_ao = _matmul("attn_out_dw", a0, dy0, "tn", BF16, aw // 2, MM_T, MM_T)
    dos, dls, dz0 = _mix_bwd(dy0, w_ao, os_, lses, z0)
    datt = [_attn_bwd(g, qk, v, os_[g], lses[g], dos[g], dls[g]) for g in range(3)]
    dqkv = _rot_pack_bwd([t[0] for t in datt], [t[1] for t in datt], [t[2] for t in datt], tabs)
    gw_ai = _matmul("proj_qkv_dw", hn0, dqkv, "tn", BF16, MM_T, wcol, MM_T, out_blocks=3 * aw // wcol, dest=(NDEV, 0, None))
    gw_ai = _matmul("proj_z_dw", hn0, dz0, "tn", BF16, MM_T, wcol, MM_T, out_blocks=aw // wcol,
                    dest=(NDEV, 3 * aw // wcol, gw_ai))
    after_start = lambda acc, t: acc + t
    zero_row = lambda token: jnp.tile(token[0:1], (1, dm // 128))
    l0_grads = [gw_ai, gw_ao.reshape(NDEV, aw // NDEV, dm)]
    pair_handles, pair_token = _exchange_start("l0_pair_start", l0_grads, _landing_zones("l0_pair_place", l0_grads, "pair"),
                                               "pair", dqkv)
    dhn0_z = _matmul("proj_z_dx", dz0, w_ai, "nt", F32, MM_T, MM_T, aw, b_koff=3 * aw, n_out=dm, epilogue=after_start,
                     ncols=(zero_row(pair_token),))
    from_sibling, l0_grads = _exchange_wait("l0_pair_wait", pair_handles, "pair", dhn0_z, with_sources=True)
    chip_sums = [_pair_sum(f"l0_pair_sum_{a}", g, s) for a, (g, s) in enumerate(zip(l0_grads, from_sibling))]
    l0_handles, l0_token = _exchange_start("l0_grads_start", chip_sums, _landing_zones("l0_grads_place", chip_sums, "chips"),
                                           "chips", dhn0_z)
    dx0, g_nw0, dsc0, dsh0 = _norm_mod_bwd(
        "proj_qkv_dx_norm0_bwd", (dqkv, w_ai, "nt", aw, dict(n_out=dm)), x0, [dhn0_z], dx1,
        nw[0], scale[0], shift[0] + zero_row(l0_token))

    small_g = [_pack_small([dsh0, dsc0, dgate0, dsh1, dsc1, dgate1, g_nw0, g_nw1, g_fnw], g_dtb, [dalog_f, dalog_b], g_d, loss_part)]
    sm_handles, sm_token = _exchange_start("small_grads_start", small_g, _landing_zones("small_grads_place", small_g, "gather"),
                                           "gather", dx0)

    whole = (slice(None), slice(None))
    r_so, r_si, r_small = _exchange_wait("l1_grads_wait", l1_handles, "scatter", sm_token)
    si_out = [o.T for o in _adamw("adamw_ssd_w_in", ssd_w_in[0].T, r_si, m_ssd_w_in[0].T, v_ssd_w_in[0].T, n_ssd_in // NDEV, 256)]
    so_out = _adamw("adamw_ssd_w_out", ssd_w_out[0], r_so, m_ssd_w_out[0], v_ssd_w_out[0], 256)
    cw_cols = ssd_conv_w.shape[2]
    cw_out, cb_out, snw_out = _adamw_windows(
        "adamw_ssd_small", r_small,
        [(ssd_conv_w, m_ssd_conv_w, v_ssd_conv_w), (ssd_conv_b, m_ssd_conv_b, v_ssd_conv_b),
         (ssd_norm_w, m_ssd_norm_w, v_ssd_norm_w)],
        [(0, slice(0, CONV_WIDTH), slice(0, cw_cols), (0, slice(None), slice(None))),
         (1, slice(5, 6), slice(0, cw_cols), whole), (2, slice(6, 7), slice(0, si // NDEV), whole)])
    r_ai, r_ao = _exchange_wait("l0_grads_wait", l0_handles, "chips", so_out[0])
    ai_out = _adamw("adamw_attn_w_in", attn_w_in[0], r_ai, m_attn_w_in[0], v_attn_w_in[0], 256)
    ao_out = _adamw("adamw_attn_w_out", attn_w_out[0], r_ao, m_attn_w_out[0], v_attn_w_out[0], 192)

    (small_all,) = _exchange_wait("small_grads_wait", sm_handles, "gather", ai_out[0])
    full = slice(0, PACK_COLS)
    nhd = SSD_HEADS
    windows = [(0, slice(3 * i + k, 3 * i + k + 1), full, (slice(i, i + 1), slice(k * dm, (k + 1) * dm)))
               for i in range(2) for k in range(3)]
    windows += [(1, slice(6 + i, 7 + i), full, (slice(i, i + 1), slice(None))) for i in range(2)]
    windows += [(2, slice(8, 9), full, whole)]
    windows += [(3 + q, slice(9, 10), slice(2 * nhd * q + nhd * j, 2 * nhd * q + nhd * (j + 1)), (0, slice(j, j + 1), slice(None)))
                for q in range(2) for j in range(2)]
    windows += [(5, slice(9, 10), slice(4 * nhd, 5 * nhd), whole)]
    as_row = lambda a: a.reshape(1, dm)
    mb_out, nw_out, fnw_out, dtb_out, alog_out, d_out, loss = _adamw_windows(
        "adamw_small", small_all,
        [(mod_b, m_mod_b, v_mod_b), (norm_w, m_norm_w, v_norm_w), (fnw, as_row(m_final_norm_w), as_row(v_final_norm_w)),
         (ssd_dt_bias, m_ssd_dt_bias, v_ssd_dt_bias), (ssd_a_log, m_ssd_a_log, v_ssd_a_log), (ssd_d, m_ssd_d, v_ssd_d)],
        windows, extra=(slice(9, 10), slice(256, 257)))
    loss = loss.reshape(())

    ncol = mod_w.shape[2]
    dmod_all = small_all[:, 0:6, :].reshape(NDEV, 2, 3 * dm)
    dmod_sh = lax.dynamic_slice_in_dim(dmod_all, me * ncol, ncol, axis=2).transpose(1, 0, 2)
    g_modw = _mod_grad(c_all, dmod_sh).reshape(1, 2 * dm, ncol)
    modw_out = _adamw("adamw_mod_w", mod_w.reshape(2 * dm, ncol), g_modw, m_mod_w.reshape(2 * dm, ncol),
                      v_mod_w.reshape(2 * dm, ncol), 256)

    per_kind = []
    for k in range(4):
        per_kind.append([
            nw_out[k], modw_out[k].reshape(mod_w.shape), mb_out[k], ai_out[k][None], ao_out[k][None], si_out[k][None],
            cw_out[k], cb_out[k], dtb_out[k], alog_out[k], d_out[k], snw_out[k], so_out[k][None], fnw_out[k].reshape(dm)])
    return (loss, dx0.reshape(x.shape), *per_kind[0], *per_kind[1], *per_kind[2], *per_kind[3])


def _pack_ssd_small_blocks(g_cw, g_cb, g_nw):
    nper = g_cw.shape[1] // NDEV
    nwper = g_nw.shape[1] // NDEV

    def body(cw_ref, cb_ref, nw_ref, o_ref):
        o_ref[...] = jnp.zeros_like(o_ref)
        for d in range(NDEV):
            o_ref[d, 0:5, :] = cw_ref[:, d * nper:(d + 1) * nper]
            o_ref[d, 5:6, :] = cb_ref[:, d * nper:(d + 1) * nper]
            o_ref[d, 6:7, 0:nwper] = nw_ref[:, d * nwper:(d + 1) * nwper]

    return pl.pallas_call(body, name="pack_ssd_small_grads", out_shape=jax.ShapeDtypeStruct((NDEV, 8, nper), F32))(g_cw, g_cb, g_nw)
```

```python
import functools
import math

import jax
import jax.numpy as jnp
from jax import lax
from jax.experimental import pallas as pl
from jax.experimental.pallas import tpu as pltpu

F32 = jnp.float32
BF16 = jnp.bfloat16
HI = lax.Precision.HIGHEST
MESH = pl.DeviceIdType.MESH
NDEV = 8

NORM_EPS = 1e-6
ROPE_THETA = 500000.0
ROT_DIM = 16
HEAD_DIM = 64
DILATIONS = (1, 4, 16)
BAND = 64
NEG_BIG = -1e30
CHUNK = 128
SSD_HEADS = 32
SSD_GROUPS = 8
CONV_WIDTH = 5

ADAM_LR = 0.001
ADAM_B1 = 0.9
ADAM_B2 = 0.999
ADAM_EPS = 1e-08
ADAM_WD = 0.01
ADAM_STEP = 10

VMEM_BIG = 56 * 1024 * 1024
MM_T = 1024


def _params(sem=None, vmem=None):
    kw = {}
    if sem is not None:
        kw["dimension_semantics"] = sem
    if vmem is not None:
        kw["vmem_limit_bytes"] = vmem
    return pltpu.CompilerParams(**kw)


def _dg(a, b, ca, cb, prec=None):
    return lax.dot_general(a, b, (((ca,), (cb,)), ((), ())), preferred_element_type=F32, precision=prec)


def _nn(a, b):
    return _dg(a.astype(BF16), b.astype(BF16), 1, 0)


def _nt(a, b):
    return _dg(a.astype(BF16), b.astype(BF16), 1, 1)


def _tn(a, b):
    return _dg(a.astype(BF16), b.astype(BF16), 0, 0)


def _hnn(a, b):
    return _dg(a, b, 1, 0, HI)


@jax.custom_vjp
def _bnn(a, b):
    return _nn(a, b)


_bnn.defvjp(lambda a, b: (_nn(a, b), (a, b)), lambda r, g: (_nt(g, r[1]), _tn(r[0], g)))


@jax.custom_vjp
def _bnt(a, b):
    return _nt(a, b)


_bnt.defvjp(lambda a, b: (_nt(a, b), (a, b)), lambda r, g: (_nn(g, r[1]), _tn(g, r[0])))


@jax.custom_vjp
def _btn(a, b):
    return _tn(a, b)


_btn.defvjp(lambda a, b: (_tn(a, b), (a, b)), lambda r, g: (_nt(r[1], g), _nn(r[0], g)))


def _silu(x):
    return x * jax.nn.sigmoid(x)


def _b_spec(b, mode, tn, tk, no, ko, jk):
    if mode == "nt":
        return pl.BlockSpec((tn, tk), lambda *g: (jk(*g)[0] + no, jk(*g)[1] + ko))
    return pl.BlockSpec((tk, tn), lambda *g: (jk(*g)[1] + ko, jk(*g)[0] + no))


def _matmul(name, a, b, mode, out_dtype, tm, tn, tk, *, epilogue=None, tiled=(), mrows=(), ncols=(),
            b_noff=0, b_koff=0, n_out=None, out_blocks=None, dest=None):
    if mode == "tn":
        K, M = a.shape
    else:
        M, K = a.shape
    N = n_out if n_out is not None else (b.shape[0] if mode == "nt" else b.shape[1])
    tm, tn, tk = min(tm, M), min(tn, N), min(tk, K)
    assert M % tm == 0 and N % tn == 0 and K % tk == 0, (name, M, N, K, tm, tn, tk)
    assert b_noff % tn == 0 and b_koff % tk == 0
    no, ko = b_noff // tn, b_koff // tk
    nk = K // tk
    if mode == "tn":
        a_spec = pl.BlockSpec((tk, tm), lambda i, j, k: (k, i))
    else:
        a_spec = pl.BlockSpec((tm, tk), lambda i, j, k: (i, k))
    specs = [a_spec, _b_spec(b, mode, tn, tk, no, ko, lambda i, j, k: (j, k))]
    specs += [pl.BlockSpec((tm, tn), lambda i, j, k: (i, j)) for _ in tiled]
    specs += [pl.BlockSpec((tm, r.shape[1]), lambda i, j, k: (i, 0)) for r in mrows]
    specs += [pl.BlockSpec((1, tn), lambda i, j, k: (0, j)) for _ in ncols]
    total, off, earlier = dest if dest is not None else (None, 0, None)
    if out_blocks is None:
        assert off % tm == 0
        mo = off // tm
        out_shape = jax.ShapeDtypeStruct((M if total is None else total, N), out_dtype)
        out_spec = pl.BlockSpec((tm, tn), lambda i, j, k: (i + mo, j))
    else:
        nper = N // out_blocks
        assert nper % tn == 0
        jb = nper // tn
        out_shape = jax.ShapeDtypeStruct((out_blocks if total is None else total, M, nper), out_dtype)
        out_spec = pl.BlockSpec((None, tm, tn), lambda i, j, k: (j // jb + off, i, j % jb))
    if earlier is not None:
        assert earlier.shape == out_shape.shape and earlier.dtype == out_shape.dtype
    ne = len(tiled) + len(mrows) + len(ncols)
    dot = {"nn": _nn, "nt": _nt, "tn": _tn}[mode]

    def body(a_ref, b_ref, *rest):
        extras, o_ref = rest[:ne], rest[ne]

        def finish(acc):
            if epilogue is not None:
                acc = epilogue(acc, *[e[...] for e in extras])
            o_ref[...] = acc.astype(o_ref.dtype)

        if nk == 1:
            finish(dot(a_ref[...], b_ref[...]))
        else:
            acc_ref = rest[ne + 1]
            k = pl.program_id(2)

            @pl.when(k == 0)
            def _():
                acc_ref[...] = jnp.zeros_like(acc_ref)

            acc_ref[...] += dot(a_ref[...], b_ref[...])

            @pl.when(k == nk - 1)
            def _():
                finish(acc_ref[...])

    args = [a, b, *tiled, *mrows, *ncols]
    aliases = {}
    if earlier is not None:
        specs.append(pl.BlockSpec(memory_space=pl.ANY))
        aliases = {len(args): 0}
        args.append(earlier)

    def body_with_dest(*refs):
        body(*refs[:2 + ne], *refs[2 + ne + (earlier is not None):])

    return pl.pallas_call(
        body_with_dest, name=name, out_shape=out_shape, grid=(M // tm, N // tn, nk),
        in_specs=specs, out_specs=out_spec, input_output_aliases=aliases,
        scratch_shapes=[] if nk == 1 else [pltpu.VMEM((tm, tn), F32)],
        compiler_params=_params(("parallel", "parallel", "arbitrary"), VMEM_BIG),
    )(*args)


def _matmul_rows(name, a, b, mode, tm, tk, fn, rows, consts, outs, accs, *, n_out=None, b_noff=0, b_koff=0):
    rl = [(t, t.shape[1], 0) if not isinstance(t, tuple) else t for t in rows]
    make_a = a if callable(a) else None
    M, K = (rl[0][0].shape[0], b.shape[1 if mode == "nt" else 0]) if make_a else a.shape
    N = n_out if n_out is not None else (b.shape[0] if mode == "nt" else b.shape[1])
    tm, tk = min(tm, M), min(tk, K)
    assert M % tm == 0 and K % tk == 0 and b_koff % tk == 0 and b_noff % N == 0, (name, M, N, K)
    no, ko, nk = b_noff // N, b_koff // tk, K // tk
    assert make_a is None or nk == 1
    nr, nc, no_, na = len(rl), len(consts), len(outs), len(accs)
    dot = _nt if mode == "nt" else _nn

    def body(*refs):
        a_ref, b_ref, rest = (None, refs[0], refs[1:]) if make_a else (refs[0], refs[1], refs[2:])
        r_refs, c_refs = rest[:nr], rest[nr:nr + nc]
        o_refs, acc_refs = rest[nr + nc:nr + nc + no_], rest[nr + nc + no_:nr + nc + no_ + na]
        i, k = pl.program_id(0), pl.program_id(1)

        def finish(prod, *made):
            res_o, res_a = fn(prod, *made, *[r[...] for r in r_refs], *[c[...] for c in c_refs])
            for r, v in zip(o_refs, res_o, strict=True):
                r[...] = v.astype(r.dtype)
            if acc_refs:
                @pl.when(i == 0)
                def _():
                    for r in acc_refs:
                        r[...] = jnp.zeros_like(r)

                for r, v in zip(acc_refs, res_a, strict=True):
                    r[...] += v

        if make_a:
            left = make_a(*[r[...] for r in r_refs], *[c[...] for c in c_refs])
            finish(dot(left, b_ref[...]), left)
        elif nk == 1:
            finish(dot(a_ref[...], b_ref[...]))
        else:
            prod_ref = rest[-1]

            @pl.when(k == 0)
            def _():
                prod_ref[...] = jnp.zeros_like(prod_ref)

            prod_ref[...] += dot(a_ref[...], b_ref[...])

            @pl.when(k == nk - 1)
            def _():
                finish(prod_ref[...])

    b_spec = _b_spec(b, mode, N, tk, no, ko, lambda i, k: (0, k))
    in_specs = ([] if make_a else [pl.BlockSpec((tm, tk), lambda i, k: (i, k))]) + [b_spec]
    in_specs += [pl.BlockSpec((tm, w), functools.partial(lambda i, k, cb: (i, cb), cb=cb)) for (_, w, cb) in rl]
    in_specs += [pl.BlockSpec(c.shape, lambda i, k: (0, 0)) for c in consts]
    out_specs = [pl.BlockSpec((tm, c), lambda i, k: (i, 0)) for (c, _) in outs]
    out_specs += [pl.BlockSpec(shp, lambda i, k: (0, 0)) for shp in accs]
    out_shape = [jax.ShapeDtypeStruct((M, c), dt) for (c, dt) in outs] + [jax.ShapeDtypeStruct(shp, F32) for shp in accs]
    res = pl.pallas_call(
        body, name=name, out_shape=out_shape, grid=(M // tm, nk), in_specs=in_specs, out_specs=out_specs,
        scratch_shapes=[] if nk == 1 else [pltpu.VMEM((tm, N), F32)],
        compiler_params=_params(("arbitrary" if accs else "parallel", "arbitrary"), VMEM_BIG),
    )(*([] if make_a else [a]), b, *[t[0] for t in rl], *consts)
    return res[:no_], res[no_:]


def _rowwise(name, fn, tiled, consts, outs, accs, ts):
    tl = [(t, t.shape[1], 0) if not isinstance(t, tuple) else t for t in tiled]
    s_len = tl[0][0].shape[0]
    assert s_len % ts == 0
    nt_, nc_, no_ = len(tl), len(consts), len(outs)

    def body(*refs):
        t_refs, c_refs = refs[:nt_], refs[nt_:nt_ + nc_]
        o_refs, a_refs = refs[nt_ + nc_:nt_ + nc_ + no_], refs[nt_ + nc_ + no_:]
        res_o, res_a = fn(*[r[...] for r in t_refs], *[r[...] for r in c_refs])
        for r, v in zip(o_refs, res_o, strict=True):
            r[...] = v.astype(r.dtype)
        if a_refs:
            @pl.when(pl.program_id(0) == 0)
            def _():
                for r in a_refs:
                    r[...] = jnp.zeros_like(r)

            for r, v in zip(a_refs, res_a, strict=True):
                r[...] += v

    in_specs = [pl.BlockSpec((ts, w), functools.partial(lambda i, cb: (i, cb), cb=cb)) for (_, w, cb) in tl]
    in_specs += [pl.BlockSpec(c.shape, lambda i: (0, 0)) for c in consts]
    out_specs = [pl.BlockSpec((ts, c), lambda i: (i, 0)) for (c, _) in outs]
    out_specs += [pl.BlockSpec(shp, lambda i: (0, 0)) for shp in accs]
    out_shape = [jax.ShapeDtypeStruct((s_len, c), dt) for (c, dt) in outs]
    out_shape += [jax.ShapeDtypeStruct(shp, F32) for shp in accs]
    res = pl.pallas_call(
        body, name=name, out_shape=out_shape, grid=(s_len // ts,), in_specs=in_specs, out_specs=out_specs,
        compiler_params=_params(("arbitrary",) if accs else ("parallel",), VMEM_BIG),
    )(*[t[0] for t in tl], *consts)
    return res[:no_], res[no_:]


def _norm_mod_fn(x, nw, sc, sh):
    r = lax.rsqrt(jnp.mean(x * x, axis=-1, keepdims=True) + NORM_EPS)
    return (x * r * nw) * (1.0 + sc) + sh


def _norm_mod_fwd(name, x, nw, sc, sh):
    (hn,), _ = _rowwise(name, lambda x, nw, sc, sh: ([_norm_mod_fn(x, nw, sc, sh)], []),
                        [x], [nw, sc, sh], [(x.shape[1], BF16)], [], 512)
    return hn


def _norm_mod_bwd(name, last, x, dhn_parts, dres, nw, sc, sh, prev=None):
    n = len(dhn_parts)
    d = x.shape[1]
    a, b, mode, tk, kw = last

    def fn(dhn, x, *rest):
        for p in rest[:n]:
            dhn = dhn + p
        dres, rest = rest[n], rest[n + 1:]
        y_prev, (nw, sc, sh), gate = (rest[0], rest[1:4], rest[4]) if prev is not None else (None, rest[0:3], None)
        r = lax.rsqrt(jnp.mean(x * x, axis=-1, keepdims=True) + NORM_EPS)
        xh = x * r
        dxh = dhn * (nw * (1.0 + sc))
        dx = r * (dxh - xh * jnp.mean(dxh * xh, axis=-1, keepdims=True)) + dres
        along = jnp.sum(dhn * xh, axis=0, keepdims=True)
        dnw, dsc, dsh = along * (1.0 + sc), along * nw, jnp.sum(dhn, axis=0, keepdims=True)
        if prev is None:
            return [dx], [dnw, dsc, dsh]
        return [dx, gate * dx], [dnw, dsc, dsh, jnp.sum(dx * y_prev, axis=0, keepdims=True)]

    rows = [x, *dhn_parts, dres] + ([prev[0]] if prev is not None else [])
    consts = [nw, sc, sh] + ([prev[1]] if prev is not None else [])
    outs = [(d, F32)] + ([(d, BF16)] if prev is not None else [])
    res_o, res_a = _matmul_rows(name, a, b, mode, 512, tk, fn, rows, consts, outs, [(1, d)] * (3 + (prev is not None)), **kw)
    return (*res_o, *res_a)


def _rope_tables(pos_col, inv_row):
    def fn(pos, inv):
        ang = pos.astype(F32) * inv
        e = lax.broadcasted_iota(jnp.int32, (1, 128), 1) % HEAD_DIM
        cos, sin = jnp.cos(ang), jnp.sin(ang)
        half = ROT_DIM // 2
        return [jnp.where(e < ROT_DIM, cos, 1.0), jnp.where(e < half, -sin, 0.0),
                jnp.where((e >= half) & (e < ROT_DIM), sin, 0.0)], []

    (c, sa, sb), _ = _rowwise("rope_tables", fn, [pos_col], [inv_row], [(128, F32)] * 3, [], 512)
    return c, sa, sb


def _rot_fwd(t, c, sa, sb):
    n = t.shape[1]
    rep = n // 128
    c, sa, sb = (jnp.tile(u, (1, rep)) for u in (c, sa, sb))
    return t * c + pltpu.roll(t, n - ROT_DIM // 2, 1) * sa + pltpu.roll(t, ROT_DIM // 2, 1) * sb


def _rot_bwd(g, c, sa, sb):
    n = g.shape[1]
    rep = n // 128
    c, sa, sb = (jnp.tile(u, (1, rep)) for u in (c, sa, sb))
    return g * c + pltpu.roll(g * sa, ROT_DIM // 2, 1) + pltpu.roll(g * sb, n - ROT_DIM // 2, 1)


ATT_TQ = 128


def _attn_tiles(l):
    tk = ATT_TQ + 2 * BAND
    return (l, l) if l <= tk else (ATT_TQ, tk)


def _attn_specs(g, s_len):
    def blk(off):
        return pl.BlockSpec((s_len, 128), functools.partial(lambda hp, off: (0, off + hp), off=off))

    return blk(4 * g), blk(12 + 4 * g), blk(4 * g), blk(0)


def _attn_tile_geometry(t, d, l):
    tq, tk = _attn_tiles(l)
    nts = l // tq
    r = t // nts
    ts = t % nts
    q0 = ts * tq
    ws = jnp.clip(q0 - BAND, 0, l - tk)
    kind = jnp.where(ts == 0, 0, jnp.where(ts == nts - 1, 2, 1))
    if d == 1:
        return pl.ds(pl.multiple_of(q0, tq), tq), pl.ds(pl.multiple_of(ws, BAND), tk), kind
    return pl.ds(r + d * q0, tq, stride=d), pl.ds(r + d * ws, tk, stride=d), kind


def _attn_fill_bias(bias_ref):
    _, tq2, tk = bias_ref.shape
    iq = lax.broadcasted_iota(jnp.int32, (tq2, 1), 0) % (tq2 // 2)
    ik = lax.broadcasted_iota(jnp.int32, (1, tk), 1)
    for i, off in enumerate((0, -BAND, -2 * BAND)):
        bias_ref[i] = jnp.where(jnp.abs(ik + off - iq) <= BAND, 0.0, NEG_BIG)


def _split_heads(t, in_h):
    zero = jnp.zeros_like(t)
    return jnp.concatenate([jnp.where(in_h[0], t, zero), jnp.where(in_h[1], t, zero)], axis=0)


def _attn_fwd(g, qk, v):
    s_len = qk.shape[0]
    d = DILATIONS[g]
    l = s_len // d
    tq, tk = _attn_tiles(l)
    assert l % tq == 0 and l >= tk
    q_spec, k_spec, v_spec, o_spec = _attn_specs(g, s_len)
    scale = 1.0 / math.sqrt(HEAD_DIM)

    def body(q_ref, k_ref, v_ref, o_ref, lse_ref, bias_ref):
        lane = lax.broadcasted_iota(jnp.int32, (1, 128), 1)
        in_h = [lane < HEAD_DIM, lane >= HEAD_DIM]
        _attn_fill_bias(bias_ref)

        def tile(t, carry):
            rows, win, kind = _attn_tile_geometry(t, d, l)
            q = (q_ref[rows, :] * scale).astype(BF16)
            k = k_ref[win, :].astype(BF16)
            vv = v_ref[win, :].astype(BF16)
            s = _nt(_split_heads(q, in_h), k) + bias_ref[kind]
            m = jnp.max(s, axis=1, keepdims=True)
            p = jnp.exp(s - m)
            den = jnp.sum(p, axis=1, keepdims=True)
            out = _nn(p, vv) / den
            lse = m + jnp.log(den)
            o_ref[rows, :] = jnp.where(in_h[0], out[:tq], out[tq:])
            lse_ref[rows, :] = jnp.where(in_h[0], lse[:tq], lse[tq:])
            return carry

        lax.fori_loop(0, s_len // tq, tile, 0, unroll=8 * ATT_TQ // tq)

    return pl.pallas_call(
        body, name=f"attn_fwd_g{g}", grid=(4,),
        out_shape=[jax.ShapeDtypeStruct((s_len, 512), F32)] * 2,
        in_specs=[q_spec, k_spec, v_spec], out_specs=[o_spec, o_spec],
        scratch_shapes=[pltpu.VMEM((3, 2 * tq, tk), F32)],
        compiler_params=_params(("parallel",), VMEM_BIG),
    )(qk, qk, v)


def _attn_bwd(g, qk, v, o, lse, do, dlse):
    s_len = qk.shape[0]
    d = DILATIONS[g]
    l = s_len // d
    tq, tk = _attn_tiles(l)
    q_spec, k_spec, v_spec, o_spec = _attn_specs(g, s_len)
    scale = 1.0 / math.sqrt(HEAD_DIM)

    def body(q_ref, k_ref, v_ref, o_ref, lse_ref, do_ref, dlse_ref, dq_ref, dk_ref, dv_ref, bias_ref):
        lane = lax.broadcasted_iota(jnp.int32, (1, 128), 1)
        in_h = [lane < HEAD_DIM, lane >= HEAD_DIM]
        dk_ref[...] = jnp.zeros_like(dk_ref)
        dv_ref[...] = jnp.zeros_like(dv_ref)
        _attn_fill_bias(bias_ref)

        def tile(t, carry):
            rows, win, kind = _attn_tile_geometry(t, d, l)
            k, vv = k_ref[win, :].astype(BF16), v_ref[win, :].astype(BF16)
            dout, lse_t, dlse_t = do_ref[rows, :], lse_ref[rows, :], dlse_ref[rows, :]
            od = dout * o_ref[rows, :]
            q2 = _split_heads((q_ref[rows, :] * scale).astype(BF16), in_h)
            do2 = _split_heads(dout.astype(BF16), in_h)
            head_col = lambda a: jnp.concatenate([a[:, 0:1], a[:, HEAD_DIM:HEAD_DIM + 1]], axis=0)
            delta = jnp.concatenate([jnp.sum(jnp.where(m, od, 0.0), axis=1, keepdims=True) for m in in_h], axis=0)
            p = jnp.exp(_nt(q2, k) + bias_ref[kind] - head_col(lse_t))
            ds = (p * (_nt(do2, vv) - delta + head_col(dlse_t))).astype(BF16)
            dq2 = _nn(ds, k) * scale
            dq_ref[rows, :] = jnp.where(in_h[0], dq2[:tq], dq2[tq:])
            dk_ref[win, :] += _tn(ds, q2)
            dv_ref[win, :] += _tn(p, do2)
            return carry

        lax.fori_loop(0, s_len // tq, tile, 0, unroll=8 * ATT_TQ // tq)

    return pl.pallas_call(
        body, name=f"attn_bwd_g{g}", grid=(4,),
        out_shape=[jax.ShapeDtypeStruct((s_len, 512), F32)] * 3,
        in_specs=[q_spec, k_spec, v_spec, o_spec, o_spec, o_spec, o_spec], out_specs=[o_spec] * 3,
        scratch_shapes=[pltpu.VMEM((3, 2 * tq, tk), F32)],
        compiler_params=_params(("parallel",), VMEM_BIG),
    )(qk, qk, v, o, lse, do, dlse)


def _mix_weights(ls):
    mx = jnp.maximum(jnp.maximum(ls[0], ls[1]), ls[2])
    es = [jnp.exp(x - mx) for x in ls]
    tot = es[0] + es[1] + es[2]
    return [e / tot for e in es]


def _attn_out(os_, lses, z, x, gate, w_out):
    s_len, dm = x.shape
    tm = 256
    wdt = 512
    z, z_block = z

    def body(o0, o1, o2, l0, l1, l2, z_ref, x_ref, g_ref, w_ref, a_ref, y_ref, x1_ref):
        alphas = _mix_weights([l0[...], l1[...], l2[...]])
        y = jnp.zeros((tm, dm), F32)
        for g, o_ref in enumerate((o0, o1, o2)):
            a_g = (o_ref[...] * alphas[g] * _silu(z_ref[:, g * wdt:(g + 1) * wdt])).astype(BF16)
            a_ref[:, g * wdt:(g + 1) * wdt] = a_g
            y = y + _nn(a_g, w_ref[g * wdt:(g + 1) * wdt, :])
        y_ref[...] = y
        x1_ref[...] = x_ref[...] + g_ref[...] * y

    row = lambda c: pl.BlockSpec((tm, c), lambda i: (i, 0))
    return pl.pallas_call(
        body, name="attn_out", grid=(s_len // tm,),
        out_shape=[jax.ShapeDtypeStruct((s_len, 3 * wdt), BF16), jax.ShapeDtypeStruct((s_len, dm), F32),
                   jax.ShapeDtypeStruct((s_len, dm), F32)],
        in_specs=[row(wdt)] * 6 + [pl.BlockSpec((tm, 3 * wdt), lambda i: (i, z_block)), row(dm),
                                   pl.BlockSpec((1, dm), lambda i: (0, 0)), pl.BlockSpec(w_out.shape, lambda i: (0, 0))],
        out_specs=[row(3 * wdt), row(dm), row(dm)],
        compiler_params=_params(("parallel",), VMEM_BIG),
    )(*os_, *lses, z, x, gate, w_out)


def _mix_bwd(dy, w_out, os_, lses, z):
    wdt = 512

    def fn(da, o0, o1, o2, l0, l1, l2, z):
        os_t, ls = [o0, o1, o2], [l0, l1, l2]
        alphas = _mix_weights(ls)
        hi = lax.broadcasted_iota(jnp.int32, (2 * wdt, wdt), 0) % wdt // HEAD_DIM
        hj = lax.broadcasted_iota(jnp.int32, (2 * wdt, wdt), 1) // HEAD_DIM
        seg = (hi == hj).astype(BF16)
        head_sum = lambda t: _dg(jnp.concatenate(_bf16_parts(t, 2), axis=1), seg, 1, 0)
        dos, dal, dzs = [], [], []
        for g in range(3):
            zg = z[:, g * wdt:(g + 1) * wdt]
            sig = jax.nn.sigmoid(zg)
            dag = da[:, g * wdt:(g + 1) * wdt]
            dmix = dag * zg * sig
            dzs.append(dag * os_t[g] * alphas[g] * (sig * (1.0 + zg * (1.0 - sig))))
            dos.append(dmix * alphas[g])
            dal.append(head_sum(dmix * os_t[g]))
        mean = alphas[0] * dal[0] + alphas[1] * dal[1] + alphas[2] * dal[2]
        dls = [alphas[g] * (dal[g] - mean) for g in range(3)]
        return dos + dls + [jnp.concatenate(dzs, axis=1)], []

    outs, _ = _matmul_rows("attn_out_dx_mix_bwd", dy, w_out, "nt", 256, dy.shape[1], fn, [*os_, *lses, (z[0], 3 * wdt, z[1])], [],
                           [(wdt, F32)] * 6 + [(3 * wdt, BF16)], [])
    return outs[:3], outs[3:6], outs[6]


def _rot_pack_bwd(dqs, dks, dvs, tabs):
    wdt = 512

    def fn(*args):
        grads, (c, sa, sb) = args[:9], args[9:]
        cols = [_rot_bwd(gq, c, sa, sb) for gq in grads[:6]] + list(grads[6:])
        return [jnp.concatenate(cols, axis=1)], []

    (out,), _ = _rowwise("rot_pack_bwd", fn, [*dqs, *dks, *dvs, *tabs], [], [(9 * wdt, BF16)], [], 512)
    return out


CONV_CB = 128
CONV_R = 256
CONV_PAD = 8


def _conv_taps(buf, base, off, sign):
    return [buf[pl.ds(base + off + sign * j, CONV_R), :] for j in range(CONV_WIDTH)]


def _conv_tap_sum(taps, w):
    acc = None
    for j, t in enumerate(taps):
        term = t * w[j:j + 1, :]
        acc = term if acc is None else acc + term
    return acc


def _conv_fwd(xpre, cw, cb):
    s_len, ch = xpre.shape
    nchunk = s_len // CONV_R

    def body(x_ref, w_ref, b_ref, o_ref, xp):
        zero = jnp.zeros((CONV_PAD, CONV_CB), F32)
        xp[0:CONV_PAD, :] = zero
        xp[s_len + CONV_PAD:s_len + 2 * CONV_PAD, :] = zero

        def fill(ci, carry):
            base = pl.multiple_of(ci * CONV_R, CONV_R)
            xp[pl.ds(base + CONV_PAD, CONV_R), :] = x_ref[pl.ds(base, CONV_R), :]
            return carry

        lax.fori_loop(0, nchunk, fill, 0)
        w = w_ref[...]
        b = b_ref[...]

        def chunk(ci, carry):
            base = pl.multiple_of(ci * CONV_R, CONV_R)
            u = _conv_tap_sum(_conv_taps(xp, base, CONV_PAD - CONV_WIDTH // 2, 1), w) + b
            o_ref[pl.ds(base, CONV_R), :] = _silu(u)
            return carry

        lax.fori_loop(0, nchunk, chunk, 0, unroll=2)

    col = lambda r: pl.BlockSpec((r, CONV_CB), lambda j: (0, j))
    return pl.pallas_call(
        body, name="conv_fwd", grid=(ch // CONV_CB,), out_shape=jax.ShapeDtypeStruct((s_len, ch), F32),
        in_specs=[col(s_len), col(CONV_WIDTH), col(1)], out_specs=col(s_len),
        scratch_shapes=[pltpu.VMEM((s_len + 2 * CONV_PAD, CONV_CB), F32)],
        compiler_params=_params(("parallel",), VMEM_BIG),
    )(xpre, cw, cb)


def _conv_bwd(xpre, da, cw, cb):
    s_len, ch = xpre.shape
    nchunk = s_len // CONV_R
    half = CONV_WIDTH // 2

    def body(x_ref, da_ref, w_ref, b_ref, dx_ref, gw_ref, gb_ref, xp, dcp):
        zero = jnp.zeros((CONV_PAD, CONV_CB), F32)
        for buf in (xp, dcp):
            buf[0:CONV_PAD, :] = zero
            buf[s_len + CONV_PAD:s_len + 2 * CONV_PAD, :] = zero

        def fill(ci, carry):
            base = pl.multiple_of(ci * CONV_R, CONV_R)
            xp[pl.ds(base + CONV_PAD, CONV_R), :] = x_ref[pl.ds(base, CONV_R), :]
            return carry

        lax.fori_loop(0, nchunk, fill, 0)
        w = w_ref[...]
        b = b_ref[...]

        def first(ci, carry):
            base = pl.multiple_of(ci * CONV_R, CONV_R)
            taps = _conv_taps(xp, base, CONV_PAD - half, 1)
            u = _conv_tap_sum(taps, w) + b
            sig = jax.nn.sigmoid(u)
            dc = da_ref[pl.ds(base, CONV_R), :] * (sig * (1.0 + u * (1.0 - sig)))
            dcp[pl.ds(base + CONV_PAD, CONV_R), :] = dc
            gb = carry[0] + jnp.sum(dc, axis=0, keepdims=True)
            gws = [carry[1 + j] + jnp.sum(dc * taps[j], axis=0, keepdims=True) for j in range(CONV_WIDTH)]
            return (gb, *gws)

        z1 = jnp.zeros((1, CONV_CB), F32)
        sums = lax.fori_loop(0, nchunk, first, (z1,) * (1 + CONV_WIDTH), unroll=2)
        gb_ref[...] = sums[0]
        for j in range(CONV_WIDTH):
            gw_ref[j:j + 1, :] = sums[1 + j]

        def second(ci, carry):
            base = pl.multiple_of(ci * CONV_R, CONV_R)
            dx_ref[pl.ds(base, CONV_R), :] = _conv_tap_sum(_conv_taps(dcp, base, CONV_PAD + half, -1), w).astype(dx_ref.dtype)
            return carry

        lax.fori_loop(0, nchunk, second, 0, unroll=2)

    col = lambda r: pl.BlockSpec((r, CONV_CB), lambda j: (0, j))
    return pl.pallas_call(
        body, name="conv_bwd", grid=(ch // CONV_CB,),
        out_shape=[jax.ShapeDtypeStruct((s_len, ch), BF16), jax.ShapeDtypeStruct((CONV_WIDTH, ch), F32),
                   jax.ShapeDtypeStruct((1, ch), F32)],
        in_specs=[col(s_len), col(s_len), col(CONV_WIDTH), col(1)],
        out_specs=[col(s_len), col(CONV_WIDTH), col(1)],
        scratch_shapes=[pltpu.VMEM((s_len + 2 * CONV_PAD, CONV_CB), F32)] * 2,
        compiler_params=_params(("parallel",), VMEM_BIG),
    )(xpre, da, cw, cb)


SSD_GW = 256
SSD_N = 128
SSD_DTW = 128


def _bf16_parts(x, n):
    parts, rest = [], x
    for _ in range(n):
        p = rest.astype(BF16)
        parts.append(p)
        rest = rest - p.astype(F32)
    return parts


@jax.custom_vjp
def _expand(x, e):
    eb = e.astype(BF16)
    return _dg(jnp.concatenate(_bf16_parts(x, 2), axis=1), jnp.concatenate([eb, eb], axis=0), 1, 0)


def _expand_fwd(x, e):
    return _expand(x, e), e


def _expand_bwd(e, g):
    return _dg(g.astype(BF16), e.astype(BF16), 1, 1), jnp.zeros_like(e)


_expand.defvjp(_expand_fwd, _expand_bwd)


@jax.custom_vjp
def _running_sum(tri, x):
    tb = tri.astype(BF16)
    return sum(_dg(tb, p, 1, 0) for p in _bf16_parts(x, 3))


def _running_sum_fwd(tri, x):
    return _running_sum(tri, x), tri


def _running_sum_bwd(tri, g):
    tb = tri.astype(BF16)
    return jnp.zeros_like(tri), sum(_dg(tb, p, 0, 0) for p in _bf16_parts(g, 3))


_running_sum.defvjp(_running_sum_fwd, _running_sum_bwd)


def _pick_col(a, h):
    @jax.custom_vjp
    def pick(a):
        return a[:, h:h + 1]

    pick.defvjp(lambda a: (a[:, h:h + 1], None),
                lambda _, g: (g * (lax.broadcasted_iota(jnp.int32, (1, a.shape[1]), 1) == h).astype(F32),))
    return pick(a)


def _pick_row(a, h):
    @jax.custom_vjp
    def pick(a):
        return a[h:h + 1, :]

    pick.defvjp(lambda a: (a[h:h + 1, :], None),
                lambda _, g: (g * (lax.broadcasted_iota(jnp.int32, (a.shape[0], 1), 0) == h).astype(F32),))
    return pick(a)


def _ssd_mask(dirn):
    ri = lax.broadcasted_iota(jnp.int32, (CHUNK, CHUNK), 0)
    cj = lax.broadcasted_iota(jnp.int32, (CHUNK, CHUNK), 1)
    return (cj <= ri) if dirn == 0 else (cj >= ri)


def _ssd_rowsel(dirn):
    last = CHUNK - 1 if dirn == 0 else 0
    return (lax.broadcasted_iota(jnp.int32, (CHUNK, 1), 0) == last).astype(F32)


def _ssd_chunk_pre(dirn):
    nh = SSD_DTW

    def f(dt, alog):
        da = dt * (-jnp.exp(alog))
        cum = _running_sum(_ssd_mask(dirn).astype(F32), da)
        tot = jnp.sum(cum * _ssd_rowsel(dirn), axis=0, keepdims=True)
        hh = lax.broadcasted_iota(jnp.int32, (nh, SSD_HEADS * HEAD_DIM), 0)
        jj = lax.broadcasted_iota(jnp.int32, (nh, SSD_HEADS * HEAD_DIM), 1)
        expand = (hh == dirn * SSD_HEADS + jj // HEAD_DIM).astype(F32)
        return cum, cum.T, _expand(dt, expand), _expand(jnp.exp(tot - cum), expand), _expand(jnp.exp(cum), expand)

    return f


def _ssd_group_fn(g, dirn, stacked):
    def f(xs, bm, cm, st, cum, cum_t, dt_e, w_e, ce_e):
        mask = _ssd_mask(dirn)
        xdt = xs * dt_e
        cd_e = jnp.sum(ce_e * _ssd_rowsel(dirn), axis=0, keepdims=True)
        cb = _bnt(cm, bm)
        lane_head = lax.broadcasted_iota(jnp.int32, (1, SSD_GW), 1) // HEAD_DIM
        y = _bnn(cm, st) * ce_e
        decayed, inputs = [], []
        for j in range(4):
            hidx = dirn * SSD_HEADS + 4 * g + j
            col, row = _pick_col(cum, hidx), _pick_row(cum_t, hidx)
            dec = cb * jnp.exp(jnp.where(mask, col - row, NEG_BIG))
            head = (lane_head == j).astype(F32)
            if stacked:
                decayed.append(dec)
                inputs.append(xdt * head)
            else:
                y = y + _bnn(dec, xdt) * head
        if stacked:
            y = y + _bnn(jnp.concatenate(decayed, axis=1), jnp.concatenate(inputs, axis=0))
        st_out = st * cd_e + _btn(bm, xdt * w_e)
        return y, st_out

    return f


def _ssd_in_specs(kk):
    ln = CHUNK
    return [pl.BlockSpec((ln, 2048), lambda i: (kk(i), 0)),
            pl.BlockSpec((ln, 1024), lambda i: (kk(i), 2)),
            pl.BlockSpec((ln, 1024), lambda i: (kk(i), 3)),
            pl.BlockSpec((ln, SSD_DTW), lambda i: (kk(i), 0)),
            pl.BlockSpec((1, SSD_DTW), lambda i: (0, 0))]


def _ssd_fwd(xbc, dt, alog, dirn, prior=None):
    s_len = xbc.shape[0]
    nc = s_len // CHUNK
    kk = (lambda i: i) if dirn == 0 else (lambda i: nc - 1 - i)

    def body(x_ref, b_ref, c_ref, dt_ref, al_ref, *rest):
        prior_ref = rest[0] if prior is not None else None
        y_ref, sts_ref, st = rest[prior is not None:]

        @pl.when(pl.program_id(0) == 0)
        def _():
            st[...] = jnp.zeros_like(st)

        sts_ref[0] = st[...]
        cum, cum_t, dt_e, w_e, ce_e = _ssd_chunk_pre(dirn)(dt_ref[...], al_ref[...])
        for g in range(SSD_GROUPS):
            xc = slice(g * SSD_GW, (g + 1) * SSD_GW)
            gc = slice(g * SSD_N, (g + 1) * SSD_N)
            y, st_new = _ssd_group_fn(g, dirn, True)(x_ref[:, xc], b_ref[:, gc], c_ref[:, gc], st[:, xc], cum, cum_t,
                                               dt_e[:, xc], w_e[:, xc], ce_e[:, xc])
            y_ref[:, xc] = y if prior is None else y + prior_ref[:, xc]
            st[:, xc] = st_new

    return pl.pallas_call(
        body, name=f"ssd_fwd_d{dirn}", grid=(nc,),
        out_shape=[jax.ShapeDtypeStruct((s_len, 2048), F32), jax.ShapeDtypeStruct((nc, SSD_N, 2048), F32)],
        in_specs=_ssd_in_specs(kk) + ([pl.BlockSpec((CHUNK, 2048), lambda i: (kk(i), 0))] if prior is not None else []),
        out_specs=[pl.BlockSpec((CHUNK, 2048), lambda i: (kk(i), 0)),
                   pl.BlockSpec((1, SSD_N, 2048), lambda i: (kk(i), 0, 0))],
        scratch_shapes=[pltpu.VMEM((SSD_N, 2048), F32)],
        compiler_params=_params(("arbitrary",), VMEM_BIG),
    )(xbc, xbc, xbc, dt, alog, *([prior] if prior is not None else []))


def _ssd_bwd(xbc, dt, alog, states, dy, d_e, dirn, prior=None):
    s_len = xbc.shape[0]
    nc = s_len // CHUNK
    kk = (lambda i: nc - 1 - i) if dirn == 0 else (lambda i: i)

    def body(x_ref, b_ref, c_ref, dt_ref, al_ref, sts_ref, dy_ref, de_ref, *rest):
        prior_ref = rest[0] if prior is not None else None
        dx_ref, ddt_ref, dal_ref, dst = rest[prior is not None:]
        plus_prior = (lambda v, cols: v + prior_ref[:, cols]) if prior is not None else (lambda v, cols: v)

        @pl.when(pl.program_id(0) == 0)
        def _():
            dst[...] = jnp.zeros_like(dst)
            dal_ref[...] = jnp.zeros_like(dal_ref)

        (cum, cum_t, dt_e, w_e, ce_e), pre_vjp = jax.vjp(_ssd_chunk_pre(dirn), dt_ref[...], al_ref[...])
        dcum = jnp.zeros_like(cum)
        dcum_t = jnp.zeros_like(cum_t)
        d_dt_e, d_w_e, d_ce_e = [], [], []
        for g in range(SSD_GROUPS):
            xc = slice(g * SSD_GW, (g + 1) * SSD_GW)
            gc = slice(g * SSD_N, (g + 1) * SSD_N)
            _, vjp = jax.vjp(_ssd_group_fn(g, dirn, False), x_ref[:, xc], b_ref[:, gc], c_ref[:, gc], sts_ref[0, :, xc], cum, cum_t,
                             dt_e[:, xc], w_e[:, xc], ce_e[:, xc])
            dyg = dy_ref[:, xc]
            dxs, dbm, dcm, dst_g, dcum_g, dcum_t_g, ddte_g, dwe_g, dcee_g = vjp((dyg, dst[:, xc]))
            if dirn == 0:
                dxs = dxs + dyg * de_ref[:, xc]
            bc, cc = slice(2048 + g * SSD_N, 2048 + (g + 1) * SSD_N), slice(3072 + g * SSD_N, 3072 + (g + 1) * SSD_N)
            dx_ref[:, xc] = plus_prior(dxs, xc)
            dx_ref[:, bc] = plus_prior(dbm, bc)
            dx_ref[:, cc] = plus_prior(dcm, cc)
            dst[:, xc] = dst_g
            dcum = dcum + dcum_g
            dcum_t = dcum_t + dcum_t_g
            d_dt_e.append(ddte_g)
            d_w_e.append(dwe_g)
            d_ce_e.append(dcee_g)
        ddt, dal = pre_vjp((dcum, dcum_t, jnp.concatenate(d_dt_e, axis=1), jnp.concatenate(d_w_e, axis=1),
                            jnp.concatenate(d_ce_e, axis=1)))
        ddt_ref[...] = ddt
        dal_ref[...] += dal

    return pl.pallas_call(
        body, name=f"ssd_bwd_d{dirn}", grid=(nc,),
        out_shape=[jax.ShapeDtypeStruct((s_len, 4096), F32), jax.ShapeDtypeStruct((s_len, SSD_DTW), F32),
                   jax.ShapeDtypeStruct((1, SSD_DTW), F32)],
        in_specs=_ssd_in_specs(kk) + [pl.BlockSpec((1, SSD_N, 2048), lambda i: (kk(i), 0, 0)),
                                      pl.BlockSpec((CHUNK, 2048), lambda i: (kk(i), 0)),
                                      pl.BlockSpec((1, 2048), lambda i: (0, 0))]
        + ([pl.BlockSpec((CHUNK, 4096), lambda i: (kk(i), 0))] if prior is not None else []),
        out_specs=[pl.BlockSpec((CHUNK, 4096), lambda i: (kk(i), 0)),
                   pl.BlockSpec((CHUNK, SSD_DTW), lambda i: (kk(i), 0)),
                   pl.BlockSpec((1, SSD_DTW), lambda i: (0, 0))],
        scratch_shapes=[pltpu.VMEM((SSD_N, 2048), F32)],
        compiler_params=_params(("arbitrary",), VMEM_BIG),
    )(xbc, xbc, xbc, dt, alog, states, dy, d_e, *([prior] if prior is not None else []))


def _gate_norm_fn(y, xs, z, d_e, nw):
    yg = (y + xs * d_e) * _silu(z)
    return yg * lax.rsqrt(jnp.mean(yg * yg, axis=-1, keepdims=True) + NORM_EPS) * nw


def _gate_norm_bwd(dy, w_out, y, xbc, z, d_e, nw):
    def fn(du, y, xs, z, d_e, nw):
        sig = jax.nn.sigmoid(z)
        gate = z * sig
        ysum = y + xs * d_e
        yg = ysum * gate
        r = lax.rsqrt(jnp.mean(yg * yg, axis=-1, keepdims=True) + NORM_EPS)
        t = du * nw
        dyg = t * r - yg * (jnp.mean(t * yg, axis=-1, keepdims=True) * (r * r * r))
        dys = dyg * gate
        dz = dyg * ysum * (sig * (1.0 + z * (1.0 - sig)))
        dnw = jnp.sum(du * yg * r, axis=0, keepdims=True)
        dde = jnp.sum(dys * xs, axis=0, keepdims=True)
        hh = lax.broadcasted_iota(jnp.int32, (2048, SSD_HEADS), 0) // HEAD_DIM
        jj = lax.broadcasted_iota(jnp.int32, (2048, SSD_HEADS), 1)
        return [dys, dz], [dnw, _hnn(jnp.broadcast_to(dde, (8, 2048)), (hh == jj).astype(F32))[0:1]]

    (dys, dz), (g_nw, g_d) = _matmul_rows("ssd_out_dx_gate_norm_bwd", dy, w_out, "nt", 256, dy.shape[1], fn,
                                          [y, (xbc, 2048, 0), z], [d_e, nw], [(2048, F32), (2048, BF16)],
                                          [(1, 2048), (1, SSD_HEADS)])
    return dys, dz, g_nw, g_d


def _ssd_tail_loss(y, xbc, z, d_e, snw, w_out, x1, tgt, gate, fnw):
    dm = x1.shape[1]
    si = y.shape[1]

    def make_u(y, xs, z, x1, tgt, d_e, snw, gate, fnw):
        return _gate_norm_fn(y, xs, z, d_e, snw).astype(BF16)

    def fn(y1, u, y, xs, z, x1, tgt, d_e, snw, gate, fnw):
        x2 = x1 + gate * y1
        r = lax.rsqrt(jnp.mean(x2 * x2, axis=-1, keepdims=True) + NORM_EPS)
        xh = x2 * r
        err = xh * fnw - tgt
        loss = 0.5 * jnp.sum(jnp.mean(err * err, axis=-1, keepdims=True), axis=0, keepdims=True)
        dy = err * (1.0 / dm)
        dxh = dy * fnw
        dx2 = r * (dxh - xh * jnp.mean(dxh * xh, axis=-1, keepdims=True))
        dfnw = jnp.sum(dy * xh, axis=0, keepdims=True)
        return [u, dx2, gate * dx2], [dfnw, jnp.sum(dx2 * y1, axis=0, keepdims=True), jnp.broadcast_to(loss, (1, 128))]

    (u, dx2, dy1), (g_fnw, dgate, loss) = _matmul_rows(
        "ssd_out_loss", make_u, w_out, "nn", 256, si, fn, [y, (xbc, si, 0), z, x1, tgt], [d_e, snw, gate, fnw],
        [(si, BF16), (dm, F32), (dm, BF16)], [(1, dm), (1, dm), (1, 128)])
    return u, dx2, dy1, g_fnw, dgate, loss


def _softplus_fwd(dt_raw, bias):
    (dt,), _ = _rowwise("dt_softplus", lambda r, b: ([jax.nn.softplus(r + b)], []), [dt_raw], [bias],
                        [(dt_raw.shape[1], F32)], [], 512)
    return dt


def _softplus_bwd(ddt_f, ddt_b, dt_raw, bias):
    def fn(df, db, r, b):
        g = (df + db) * jax.nn.sigmoid(r + b)
        return [g], [jnp.sum(g, axis=0, keepdims=True)]

    w = dt_raw.shape[1]
    (g,), (gb,) = _rowwise("dt_softplus_bwd", fn, [ddt_f, ddt_b, dt_raw], [bias], [(w, BF16)], [(1, w)], 512)
    return g, gb


def _mod_part(c_all, mod_w):
    nl, _, ncol = mod_w.shape
    nb = c_all.shape[0]

    def body(c_ref, w_ref, o_ref):
        cond = _silu(c_ref[...])
        for i in range(nl):
            o_ref[i * nb:(i + 1) * nb, :] = _nn(cond, w_ref[i])

    return pl.pallas_call(body, name="mod_part", out_shape=jax.ShapeDtypeStruct((nl * nb, ncol), F32),
                          compiler_params=_params(None, VMEM_BIG))(c_all, mod_w)


def _mod_finish(mod_nb, mod_b, norm_w, tokens):
    nl, dm = norm_w.shape

    def body(a_ref, b_ref, nw_ref, *rest):
        tok_refs, o_refs = rest[:len(tokens)], rest[len(tokens):]
        tok = sum(t[0:1, 0:1] for t in tok_refs)
        for i in range(nl):
            for k in range(3):
                cols = slice(k * dm, (k + 1) * dm)
                o_refs[4 * i + k][...] = a_ref[i:i + 1, cols] + b_ref[i:i + 1, cols]
            o_refs[4 * i + 3][...] = nw_ref[i:i + 1, :] + tok

    rows = pl.pallas_call(body, name="mod_finish", out_shape=[jax.ShapeDtypeStruct((1, dm), F32)] * (4 * nl))(
        mod_nb, mod_b, norm_w, *tokens)
    return [rows[4 * i:4 * i + 4] for i in range(nl)]


def _mod_grad(c_all, dmod_sh):
    nl, nb, ncol = dmod_sh.shape
    dm = c_all.shape[1]

    def body(c_ref, d_ref, o_ref):
        cond = _silu(c_ref[...])
        for i in range(nl):
            o_ref[i] = _tn(cond, d_ref[i])

    return pl.pallas_call(body, name="mod_grad", out_shape=jax.ShapeDtypeStruct((nl, dm, ncol), F32),
                          compiler_params=_params(None, VMEM_BIG))(c_all, dmod_sh)


PACK_ROWS = 16
PACK_COLS = 1024


def _pack_small(rows, b64, a64s, d32, extra):
    nr, na = len(rows), len(a64s)

    def body(*refs):
        o_ref = refs[-1]
        o_ref[...] = jnp.zeros_like(o_ref)
        for i in range(nr):
            o_ref[i:i + 1, :] = refs[i][...]
        b_ref, a_refs, d_ref, e_ref = refs[nr], refs[nr + 1:nr + 1 + na], refs[nr + 1 + na], refs[nr + 2 + na]
        o_ref[nr:nr + 1, 0:64] = b_ref[:, 0:64]
        o_ref[nr:nr + 1, 64:128] = sum(a[:, 0:64] for a in a_refs)
        o_ref[nr:nr + 1, 128:160] = d_ref[...]
        o_ref[nr:nr + 1, 256:384] = e_ref[...]

    return pl.pallas_call(body, name="pack_small", out_shape=jax.ShapeDtypeStruct((PACK_ROWS, PACK_COLS), F32))(
        *rows, b64, *a64s, d32, extra)


def _pack_ssd_small(cw, cb, nw):
    def body(cw_ref, cb_ref, nw_ref, o_ref):
        o_ref[...] = jnp.zeros_like(o_ref)
        o_ref[0:5, :] = cw_ref[...]
        o_ref[5:6, :] = cb_ref[...]
        o_ref[6:7, 0:256] = nw_ref[...]

    return pl.pallas_call(body, name="pack_ssd_small", out_shape=jax.ShapeDtypeStruct((8, 512), F32))(cw, cb, nw)


def _sum_parts(p_ref):
    g = p_ref[0].astype(F32)
    for s in range(1, p_ref.shape[0]):
        g = g + p_ref[s].astype(F32)
    return g


def _adam_update(w, g, m, v):
    m2 = ADAM_B1 * m + (1.0 - ADAM_B1) * g
    v2 = ADAM_B2 * v + (1.0 - ADAM_B2) * (g * g)
    m_hat = m2 / (1.0 - ADAM_B1 ** ADAM_STEP)
    v_hat = v2 / (1.0 - ADAM_B2 ** ADAM_STEP)
    return -ADAM_LR * (m_hat / (jnp.sqrt(v_hat) + ADAM_EPS) + ADAM_WD * w), m2, v2


def _adamw_windows(name, parts, params, windows, extra=None):
    n = len(params)

    def body(p_ref, *rest):
        ins, outs = rest[:3 * n], rest[3 * n:]
        g = _sum_parts(p_ref)
        for pi, rows, cols, idx in windows:
            w_ref, m_ref, v_ref = ins[3 * pi:3 * pi + 3]
            gw = g[rows, cols]
            dw, m2, v2 = _adam_update(w_ref[idx], gw, m_ref[idx], v_ref[idx])
            for o_ref, val in zip(outs[4 * pi:4 * pi + 4], (gw, dw, m2, v2), strict=True):
                o_ref[idx] = val
        if extra is not None:
            outs[4 * n][...] = g[extra[0], extra[1]]

    out_shape = [jax.ShapeDtypeStruct(w.shape, F32) for (w, _, _) in params for _ in range(4)]
    if extra is not None:
        out_shape.append(jax.ShapeDtypeStruct((extra[0].stop - extra[0].start, extra[1].stop - extra[1].start), F32))
    res = pl.pallas_call(body, name=name, out_shape=out_shape)(parts, *[a for p in params for a in p])
    return [res[4 * i:4 * i + 4] for i in range(n)] + ([res[4 * n]] if extra is not None else [])


def _adamw(name, w, parts, m, v, tr, tc=None):
    r_, c_ = w.shape
    p_ = parts.shape[0]
    tr = min(tr, r_)
    tc = c_ if tc is None else tc
    assert r_ % tr == 0 and c_ % tc == 0

    def body(w_ref, p_ref, m_ref, v_ref, g_ref, d_ref, m2_ref, v2_ref):
        g = _sum_parts(p_ref)
        g_ref[...] = g
        d_ref[...], m2_ref[...], v2_ref[...] = _adam_update(w_ref[...], g, m_ref[...], v_ref[...])

    blk = pl.BlockSpec((tr, tc), lambda i, j: (i, j))
    return pl.pallas_call(
        body, name=name, grid=(r_ // tr, c_ // tc), out_shape=[jax.ShapeDtypeStruct((r_, c_), F32)] * 4,
        in_specs=[blk, pl.BlockSpec((p_, tr, tc), lambda i, j: (0, i, j)), blk, blk], out_specs=[blk] * 4,
        compiler_params=_params(("parallel", "parallel"), VMEM_BIG),
    )(w, parts, m, v)


def _dev_index(p):
    return 4 * p[0] + 2 * p[1] + p[2]


def _all_gather(name, xs):
    n = len(xs)
    hbm = pl.BlockSpec(memory_space=pl.ANY)

    def body(*refs):
        x_refs, o_refs = refs[:n], refs[n:2 * n]
        send_sems, recv_sems, local_sems = refs[2 * n:]
        x, y, c = lax.axis_index("x"), lax.axis_index("y"), lax.axis_index("c")
        me, sibling = (x, y, c), (x, y, 1 - c)
        chips = [(1 - x, y), (x, 1 - y), (1 - x, 1 - y)]

        def place(a, block):
            return o_refs[a].at[_dev_index(block)]

        def copy(a, k, block, to, src=None):
            dst = place(a, block)
            return pltpu.make_async_remote_copy(
                src_ref=dst if src is None else src, dst_ref=dst, send_sem=send_sems.at[a, k],
                recv_sem=recv_sems.at[a, k], device_id=to, device_id_type=MESH)

        mine = [pltpu.make_async_copy(x_refs[a], place(a, me), local_sems.at[a]) for a in range(n)]
        for cp in mine:
            cp.start()
        first = []
        for a in range(n):
            first.append(copy(a, 0, me, sibling, src=x_refs[a]))
            first += [copy(a, 1 + j, me, (*chip, c), src=x_refs[a]) for j, chip in enumerate(chips)]
        for cp in first:
            cp.start()
        passed = []
        for j, chip in enumerate(chips):
            for a in range(n):
                copy(a, 1 + j, (*chip, c), me).wait_recv()
                cp = copy(a, 4 + j, (*chip, c), sibling)
                cp.start()
                passed.append(cp)
        for a in range(n):
            copy(a, 0, sibling, me).wait_recv()
            for j, chip in enumerate(chips):
                copy(a, 4 + j, (*chip, 1 - c), me).wait_recv()
        for cp in first + passed:
            cp.wait_send()
        for cp in mine:
            cp.wait()

    return pl.pallas_call(
        body, name=name, out_shape=[jax.ShapeDtypeStruct((NDEV, *x.shape), x.dtype) for x in xs],
        in_specs=[hbm] * n, out_specs=[hbm] * n,
        scratch_shapes=[pltpu.SemaphoreType.DMA((n, 7)), pltpu.SemaphoreType.DMA((n, 7)), pltpu.SemaphoreType.DMA((n,))],
    )(*xs)


_HBM = pl.BlockSpec(memory_space=pltpu.HBM)
_SEM = pl.BlockSpec(memory_space=pltpu.SEMAPHORE)
_EFFECT = pltpu.SideEffectType.DATAFLOW_SIDE_EFFECTING


def _mesh_position():
    return lax.axis_index("x"), lax.axis_index("y"), lax.axis_index("c")


def _peers(me):
    return [(k, tuple(1 - v if (k >> b) & 1 else v for v, b in zip(me, (2, 1, 0)))) for k in range(1, NDEV)]


def _column_window(ref, block, width):
    return ref.at[:, pl.ds(pl.multiple_of(_dev_index(block) * width, 128), width)]


RELAY_COPIES = 3


def _relay_copies(o_ref, send_sems, recv_sems, with_arrivals):
    width = o_ref.shape[1] // NDEV
    x, y, c = me = _mesh_position()
    sibling = (x, y, 1 - c)
    x_side, y_side, diagonal = (1 - x, y), (x, 1 - y), (1 - x, 1 - y)
    first = c == 0
    via = (jnp.where(first, 1 - x, x), jnp.where(first, y, 1 - y))
    to = (jnp.where(first, x, 1 - x), jnp.where(first, 1 - y, y))

    def copy(k, block, device):
        window = _column_window(o_ref, block, width)
        return pltpu.make_async_remote_copy(src_ref=window, dst_ref=window, send_sem=send_sems.at[k],
                                            recv_sem=recv_sems.at[k], device_id=device, device_id_type=MESH)

    sent = [copy(0, (*via, c), (*to, c)), copy(1, (*x_side, c), sibling), copy(2, (*y_side, c), sibling)]
    if not with_arrivals:
        return sent
    arrivals =[copy(0, (*diagonal, c), me), copy(1, (*x_side, 1 - c), me), copy(2, (*y_side, 1 - c), me)]
    return sent, arrivals


def _relay_start(name, gathered, dep):
    def body(g_ref, dep_ref, send_sems, recv_sems, o_ref, token):
        for cp in _relay_copies(g_ref, send_sems, recv_sems, with_arrivals=False):
            cp.start()
        token[...] = jnp.zeros_like(token)

    sems = pltpu.SemaphoreType.DMA((RELAY_COPIES,))
    res = pl.pallas_call(
        body, name=name,
        out_shape=(sems, sems, pltpu.HBM(gathered.shape, gathered.dtype), jax.ShapeDtypeStruct((8, 128), F32)),
        in_specs=[_HBM, pl.BlockSpec(memory_space=pl.ANY)],
        out_specs=(_SEM, _SEM, _HBM, pl.BlockSpec(memory_space=pltpu.VMEM)),
        input_output_aliases={0: 2},
        compiler_params=pltpu.CompilerParams(has_side_effects=_EFFECT),
    )(gathered, dep)
    return res[:-1], res[-1]


def _relay_wait(name, handles, after):
    send_sems, recv_sems, gathered = handles

    def body(g_ref, s_sems, r_sems, after_ref, o_ref):
        sent, arrivals = _relay_copies(g_ref, s_sems, r_sems, with_arrivals=True)
        for cp, arrival in zip(sent, arrivals):
            cp.wait_send()
            arrival.wait_recv()

    return pl.pallas_call(
        body, name=name, out_shape=pltpu.HBM(gathered.shape, gathered.dtype),
        in_specs=[_HBM, _SEM, _SEM, pl.BlockSpec(memory_space=pl.ANY)], out_specs=_HBM, input_output_aliases={0: 0},
        compiler_params=pltpu.CompilerParams(has_side_effects=_EFFECT),
    )(gathered, send_sems, recv_sems, after)


def _columns_last(name, gathered):
    width = gathered.shape[1] // NDEV
    hbm = pl.BlockSpec(memory_space=pl.ANY)

    def body(g_ref, o_ref, send_sem, recv_sem):
        x, y, c = _mesh_position()

        def copy(core, device):
            window = _column_window(o_ref, (1 - x, 1 - y, core), width)
            return pltpu.make_async_remote_copy(src_ref=window, dst_ref=window, send_sem=send_sem, recv_sem=recv_sem,
                                                device_id=device, device_id_type=MESH)

        onward = copy(c, (x, y, 1 - c))
        onward.start()
        copy(1 - c, (x, y, c)).wait_recv()
        onward.wait_send()

    return pl.pallas_call(
        body, name=name, out_shape=jax.ShapeDtypeStruct(gathered.shape, gathered.dtype), in_specs=[hbm], out_specs=hbm,
        input_output_aliases={0: 0}, scratch_shapes=[pltpu.SemaphoreType.DMA, pltpu.SemaphoreType.DMA],
    )(gathered)


NCHIP = NDEV // 2
EXCHANGE_COPIES = {"columns": NCHIP - 1, "gather": NDEV - 1, "scatter": NDEV - 1, "pair": NCHIP, "chips": NCHIP - 1}


def _landing_zones(name, xs, mode):
    x_, y_, c_ = _mesh_position()
    mine = (2 * x_ + y_ if mode == "chips" else _dev_index((x_, y_, c_))).astype(jnp.int32).reshape(1)
    lands = []
    for a, x in enumerate(xs):
        rows, cols = x.shape[-2:]
        if mode == "pair":
            lands.append(lax.empty((NCHIP, rows, cols), x.dtype))
            continue
        tr = 256 if rows % 256 == 0 else rows

        def body(me_ref, x_ref, o_ref):
            o_ref[...] = x_ref[...]

        if mode in ("gather", "columns"):
            in_spec = pl.BlockSpec((tr, cols), lambda i, me_ref: (i, 0))
        else:
            in_spec = pl.BlockSpec((None, tr, cols), lambda i, me_ref: (me_ref[0], i, 0))
        if mode == "columns":
            out_shape, out_spec = (rows, NDEV * cols), pl.BlockSpec((tr, cols), lambda i, me_ref: (i, me_ref[0]))
        else:
            out_shape = (NCHIP if mode == "chips" else NDEV, rows, cols)
            out_spec = pl.BlockSpec((None, tr, cols), lambda i, me_ref: (me_ref[0], i, 0))
        lands.append(pl.pallas_call(
            body, name=f"{name}_{a}", out_shape=jax.ShapeDtypeStruct(out_shape, x.dtype),
            grid_spec=pltpu.PrefetchScalarGridSpec(num_scalar_prefetch=1, grid=(rows // tr,), in_specs=[in_spec],
                                                   out_specs=out_spec),
            compiler_params=_params(("arbitrary",)),
        )(mine, x))
    return lands


def _exchange_copies(x_refs, land_refs, send_sems, recv_sems, mode):
    x_, y_, c_ = me = _mesh_position()
    per_array = EXCHANGE_COPIES[mode]
    out = []

    def add(a, k, src, dst, peer):
        sem = a * per_array + k
        out.append(pltpu.make_async_remote_copy(src_ref=src, dst_ref=dst, send_sem=send_sems.at[sem], recv_sem=recv_sems.at[sem],
                                                device_id=peer, device_id_type=MESH))

    for a, (x_ref, land_ref) in enumerate(zip(x_refs, land_refs)):
        if mode == "columns":
            for k, peer in enumerate([(x_, y_, 1 - c_), (1 - x_, y_, c_), (x_, 1 - y_, c_)]):
                add(a, k, x_ref, _column_window(land_ref, me, x_ref.shape[1]), peer)
        elif mode in ("gather", "scatter"):
            for k, peer in _peers(me):
                add(a, k - 1, x_ref.at[_dev_index(peer)] if mode == "scatter" else x_ref, land_ref.at[_dev_index(me)], peer)
        elif mode == "pair":
            for chip in range(NCHIP):
                add(a, chip, x_ref.at[2 * chip + 1 - c_], land_ref.at[chip], (x_, y_, 1 - c_))
        else:
            for k in range(1, NCHIP):
                px, py = (1 - x_ if k & 2 else x_), (1 - y_ if k & 1 else y_)
                add(a, k - 1, x_ref.at[2 * px + py], land_ref.at[2 * x_ + y_], (px, py, c_))
    return out


def _exchange_start(name, xs, lands, mode, dep, carry=False):
    n = len(xs)

    def body(*refs):
        x_refs, land_refs = refs[:n], refs[n:2 * n]
        send_sems, recv_sems = refs[2 * n + 1], refs[2 * n + 2]
        for cp in _exchange_copies(x_refs, land_refs, send_sems, recv_sems, mode):
            cp.start()
        if not carry:
            refs[-1][...] = jnp.zeros_like(refs[-1])

    sems = pltpu.SemaphoreType.DMA((n * EXCHANGE_COPIES[mode],))
    moved = [pltpu.with_memory_space_constraint(a, pltpu.HBM) for a in (*xs, *lands, *([dep] if carry else []))]
    res = pl.pallas_call(
        body, name=name,
        out_shape=(sems, sems, *[pltpu.HBM(a.shape, a.dtype) for a in moved],
                   *([] if carry else [jax.ShapeDtypeStruct((8, 128), F32)])),
        in_specs=[_HBM] * len(moved) + ([] if carry else [pl.BlockSpec(memory_space=pl.ANY)]),
        out_specs=(_SEM, _SEM, *[_HBM] * len(moved), *([] if carry else [pl.BlockSpec(memory_space=pltpu.VMEM)])),
        input_output_aliases={i: 2 + i for i in range(len(moved))},
        compiler_params=pltpu.CompilerParams(has_side_effects=_EFFECT),
    )(*moved, *([] if carry else [dep]))
    return res[:-1], res[-1]


def _exchange_wait(name, handles, mode, after, with_sources=False):
    send_sems, recv_sems = handles[0], handles[1]
    bufs = handles[2:]
    n = len(bufs) // 2
    afters = list(after) if isinstance(after, (list, tuple)) else [after]

    def body(*refs):
        x_refs, land_refs = refs[:n], refs[n:2 * n]
        s_sems, r_sems = refs[2 * n], refs[2 * n + 1]
        for cp in _exchange_copies(x_refs, land_refs, s_sems, r_sems, mode):
            cp.wait_send()
            cp.wait_recv()

    res = pl.pallas_call(
        body, name=name, out_shape=tuple(pltpu.HBM(a.shape, a.dtype) for a in bufs),
        in_specs=[_HBM] * (2 * n) + [_SEM, _SEM] + [pl.BlockSpec(memory_space=pl.ANY)] * len(afters),
        out_specs=tuple([_HBM] * (2 * n)), input_output_aliases={i: i for i in range(2 * n)},
        compiler_params=pltpu.CompilerParams(has_side_effects=_EFFECT),
    )(*bufs, send_sems, recv_sems, *afters)
    return (res[n:], res[:n]) if with_sources else res[n:]


def _pair_sum(name, x, from_sibling):
    _, rows, cols = x.shape
    tr = rows
    core = lax.axis_index("c").astype(jnp.int32).reshape(1)

    def body(c_ref, x_ref, s_ref, o_ref):
        o_ref[...] = (x_ref[...].astype(F32) + s_ref[...].astype(F32)).astype(o_ref.dtype)

    return pl.pallas_call(
        body, name=name, out_shape=jax.ShapeDtypeStruct((NCHIP, rows, cols), x.dtype),
        grid_spec=pltpu.PrefetchScalarGridSpec(
            num_scalar_prefetch=1, grid=(NCHIP, rows // tr),
            in_specs=[pl.BlockSpec((None, tr, cols), lambda j, i, c_ref: (2 * j + c_ref[0], i, 0)),
                      pl.BlockSpec((None, tr, cols), lambda j, i, c_ref: (j, i, 0))],
            out_specs=pl.BlockSpec((None, tr, cols), lambda j, i, c_ref: (j, i, 0))),
        compiler_params=_params(("parallel", "parallel")),
    )(core, x, from_sibling)


def kernel(x, c, positions, norm_w, mod_w, mod_b, attn_w_in, attn_w_out, ssd_w_in, ssd_conv_w, ssd_conv_b, ssd_dt_bias, ssd_a_log, ssd_d, ssd_norm_w, ssd_w_out, final_norm_w, loss_target, m_norm_w, m_mod_w, m_mod_b, m_attn_w_in, m_attn_w_out, m_ssd_w_in, m_ssd_conv_w, m_ssd_conv_b, m_ssd_dt_bias, m_ssd_a_log, m_ssd_d, m_ssd_norm_w, m_ssd_w_out, m_final_norm_w, v_norm_w, v_mod_w, v_mod_b, v_attn_w_in, v_attn_w_out, v_ssd_w_in, v_ssd_conv_w, v_ssd_conv_b, v_ssd_dt_bias, v_ssd_a_log, v_ssd_d, v_ssd_norm_w, v_ssd_w_out, v_final_norm_w):
    s_len, dm = x.shape[1], x.shape[2]
    me = 4 * lax.axis_index("x") + 2 * lax.axis_index("y") + lax.axis_index("c")
    x0 = x.reshape(s_len, dm)
    tgt = loss_target.reshape(s_len, dm)
    aw = 3 * 512
    si = 2 * dm
    sxbc = 2 * si
    n_ssd_in = ssd_w_in.shape[2] * NDEV

    (c_all,) = _all_gather("gather_c", [c])
    c_all = c_all.reshape(NDEV, dm)
    part = _mod_part(c_all, mod_w)
    (part_all,) = _all_gather("gather_mod", [part])
    mod_nb = jnp.stack([lax.dynamic_index_in_dim(part_all, i * NDEV + me, axis=1, keepdims=False).reshape(3 * dm)
                        for i in range(2)])

    wcol = attn_w_in.shape[2]
    ai_shard = [attn_w_in[0].astype(BF16)]
    ai_handles, ai_token = _exchange_start("attn_w_in_start", ai_shard, _landing_zones("attn_w_in_place", ai_shard, "columns"),
                                           "columns", part_all)
    inv_freq = ROPE_THETA ** (-jnp.arange(0, ROT_DIM, 2, dtype=F32) / ROT_DIM)
    per_head = jnp.concatenate([inv_freq, inv_freq, jnp.zeros(HEAD_DIM - ROT_DIM, F32)])
    inv_row = jnp.tile(per_head, 128 // HEAD_DIM).reshape(1, 128) + ai_token[0:1]
    tabs = _rope_tables(positions.reshape(s_len, 1), inv_row)
    ssd_small = _pack_ssd_small(ssd_conv_w[0], ssd_conv_b, ssd_norm_w)
    ao_shard = [attn_w_out[0].astype(BF16)]
    late_shards = [ssd_w_in[0].T.astype(BF16), ssd_w_out[0].astype(BF16), ssd_small]
    ao_lands = _landing_zones("w_out_place", ao_shard, "gather")
    late_lands = _landing_zones("weights_place", late_shards, "gather")
    (w_ai,) = _exchange_wait("attn_w_in_wait", ai_handles, "columns", [*tabs, *ao_lands, *late_lands])
    relay_handles, relay_token = _relay_start("attn_w_in_relay_start", w_ai, tabs[0])
    (shift0, scale0, gate0, nw0), (shift1, scale1, gate1, nw1) = _mod_finish(mod_nb, mod_b, norm_w, [relay_token])
    shift, scale, gate, nw = [shift0, shift1], [scale0, scale1], [gate0, gate1], [nw0, nw1]
    hn0 = _norm_mod_fwd("norm0", x0, nw[0], scale[0], shift[0])
    w_ai = _columns_last("gather_attn_w_in_last", _relay_wait("attn_w_in_relay_wait", relay_handles, hn0))

    ao_handles, w_ai = _exchange_start("w_out_start", ao_shard, ao_lands, "gather", w_ai, carry=True)
    w_handles, w_ai = _exchange_start("weights_start", late_shards, late_lands, "gather", w_ai, carry=True)

    qk = _matmul("proj_qk", hn0, w_ai, "nn", F32, MM_T, MM_T, dm, epilogue=_rot_fwd, mrows=tabs, n_out=2 * aw)
    v = _matmul("proj_vz", hn0, w_ai, "nn", F32, MM_T, MM_T, dm, b_noff=2 * aw, n_out=2 * aw)
    z0 = (v, 1)
    att = [_attn_fwd(g, qk, v) for g in range(3)]
    os_, lses = [a[0] for a in att], [a[1] for a in att]
    (g_ao,) = _exchange_wait("w_out_wait", ao_handles, "gather", lses[2])
    a0, y0, x1 = _attn_out(os_, lses, z0, x0, gate[0], g_ao.reshape(aw, dm))

    hn1 = _norm_mod_fwd("norm1", x1, nw[1], scale[1], shift[1])
    g_si, g_so, g_small = _exchange_wait("weights_wait", w_handles, "gather", hn1)
    w_ao = g_ao.reshape(aw, dm)
    w_si_t = g_si.reshape(n_ssd_in, dm)
    w_so = g_so.reshape(si, dm)
    conv_w = g_small[:, 0:CONV_WIDTH, :].transpose(1, 0, 2).reshape(CONV_WIDTH, sxbc)
    conv_b = g_small[:, 5, :].reshape(1, sxbc)
    snw = g_small[:, 6, 0:si // NDEV].reshape(1, si)
    ndt = 2 * SSD_HEADS
    z1 = _matmul("ssd_proj_z", hn1, w_si_t, "nt", F32, MM_T, MM_T, dm, n_out=si)
    xpre = _matmul("ssd_proj_xbc", hn1, w_si_t, "nt", F32, MM_T, MM_T, dm, b_noff=si, n_out=sxbc)
    dt_raw = _matmul("ssd_proj_dt", hn1, w_si_t, "nt", F32, MM_T, ndt, dm, b_noff=si + sxbc, n_out=ndt)
    xbc = _conv_fwd(xpre, conv_w, conv_b)
    widen = lambda a: jnp.pad(a, ((0, 0), (0, SSD_DTW - ndt)))
    dt_raw = widen(dt_raw)
    dt_bias = widen(ssd_dt_bias.reshape(1, ndt))
    alog = widen(ssd_a_log.reshape(1, ndt))
    dt = _softplus_fwd(dt_raw, dt_bias)
    y_f, st_f = _ssd_fwd(xbc, dt, alog, 0)
    y_fb, st_b = _ssd_fwd(xbc, dt, alog, 1, prior=y_f)
    d_e = jnp.repeat(ssd_d.reshape(SSD_HEADS), HEAD_DIM).reshape(1, si)

    fnw = final_norm_w.reshape(1, dm)
    u, dx2, dy1, g_fnw, dgate1, loss_part = _ssd_tail_loss(y_fb, xbc, z1, d_e, snw, w_so, x1, tgt, gate[1], fnw)
    gw_so = _matmul("ssd_out_dw", u, dy1, "tn", BF16, MM_T, MM_T, MM_T)
    dys, dz1, g_snw, g_d = _gate_norm_bwd(dy1, w_so, y_fb, xbc, z1, d_e, snw)
    dxbc_f, ddt_f, dalog_f = _ssd_bwd(xbc, dt, alog, st_f, dys, d_e, 0)
    dxbc, ddt_b, dalog_b = _ssd_bwd(xbc, dt, alog, st_b, dys, d_e, 1, prior=dxbc_f)
    dpre, g_cw, g_cb = _conv_bwd(xpre, dxbc, conv_w, conv_b)
    ddt_raw, g_dtb = _softplus_bwd(ddt_f, ddt_b, dt_raw, dt_bias)
    ddt_raw = ddt_raw[:, :ndt]
    dhn1 = [_matmul("ssd_proj_z_dx", dz1, w_si_t, "nn", F32, MM_T, MM_T, MM_T),
            _matmul("ssd_proj_xbc_dx", dpre, w_si_t, "nn", F32, MM_T, MM_T, MM_T, b_koff=si)]
    gw_si_t = _matmul("ssd_proj_z_dw", dz1, hn1, "tn", BF16, MM_T, MM_T, MM_T, dest=(n_ssd_in, 0, None))
    gw_si_t = _matmul("ssd_proj_xbc_dw", dpre, hn1, "tn", BF16, MM_T, MM_T, MM_T, dest=(n_ssd_in, si, gw_si_t))
    gw_si_t = _matmul("ssd_proj_dt_dw", ddt_raw, hn1, "tn", BF16, ndt, MM_T, MM_T, dest=(n_ssd_in, si + sxbc, gw_si_t))

    l1_grads = [gw_so.reshape(NDEV, si // NDEV, dm), gw_si_t.reshape(NDEV, n_ssd_in // NDEV, dm),
                _pack_ssd_small_blocks(g_cw, g_cb, g_snw)]
    l1_handles, l1_token = _exchange_start("l1_grads_start", l1_grads, _landing_zones("l1_grads_place", l1_grads, "scatter"),
                                           "scatter", dhn1[1])
    dx1, dy0, g_nw1, dsc1, dsh1, dgate0 = _norm_mod_bwd(
        "ssd_proj_dt_dx_norm1_bwd", (ddt_raw, w_si_t, "nn", ndt, dict(b_koff=si + sxbc)), x1, dhn1, dx2,
        nw[1], scale[1], shift[1], prev=(y0, gate[0] + l1_token[0:1, 0:1]))

    gw_ao = _matmul("attn_out_dw", a0, dy0, "tn", BF16, aw // 2, MM_T, MM_T)
    dos, dls, dz0 = _mix_bwd(dy0, w_ao, os_, lses, z0)
    datt = [_attn_bwd(g, qk, v, os_[g], lses[g], dos[g], dls[g]) for g in range(3)]
    dqkv = _rot_pack_bwd([t[0] for t in datt], [t[1] for t in datt], [t[2] for t in datt], tabs)
    gw_ai = _matmul("proj_qkv_dw", hn0, dqkv, "tn", BF16, MM_T, wcol, MM_T, out_blocks=3 * aw // wcol, dest=(NDEV, 0, None))
    gw_ai = _matmul("proj_z_dw", hn0, dz0, "tn", BF16, MM_T, wcol, MM_T, out_blocks=aw // wcol,
                    dest=(NDEV, 3 * aw // wcol, gw_ai))
    after_start = lambda acc, t: acc + t
    zero_row = lambda token: jnp.tile(token[0:1], (1, dm // 128))
    l0_grads = [gw_ai, gw_ao.reshape(NDEV, aw // NDEV, dm)]
    pair_handles, pair_token = _exchange_start("l0_pair_start", l0_grads, _landing_zones("l0_pair_place", l0_grads, "pair"),
                                               "pair", dqkv)
    dhn0_z = _matmul("proj_z_dx", dz0, w_ai, "nt", F32, MM_T, MM_T, aw, b_koff=3 * aw, n_out=dm, epilogue=after_start,
                     ncols=(zero_row(pair_token),))
    from_sibling, l0_grads = _exchange_wait("l0_pair_wait", pair_handles, "pair", dhn0_z, with_sources=True)
    chip_sums = [_pair_sum(f"l0_pair_sum_{a}", g, s) for a, (g, s) in enumerate(zip(l0_grads, from_sibling))]
    l0_handles, l0_token = _exchange_start("l0_grads_start", chip_sums, _landing_zones("l0_grads_place", chip_sums, "chips"),
                                           "chips", dhn0_z)
    dx0, g_nw0, dsc0, dsh0 = _norm_mod_bwd(
        "proj_qkv_dx_norm0_bwd", (dqkv, w_ai, "nt", aw, dict(n_out=dm)), x0, [dhn0_z], dx1,
        nw[0], scale[0], shift[0] + zero_row(l0_token))

    small_g = [_pack_small([dsh0, dsc0, dgate0, dsh1, dsc1, dgate1, g_nw0, g_nw1, g_fnw], g_dtb, [dalog_f, dalog_b], g_d, loss_part)]
    sm_handles, sm_token = _exchange_start("small_grads_start", small_g, _landing_zones("small_grads_place", small_g, "gather"),
                                           "gather", dx0)

    whole = (slice(None), slice(None))
    r_so, r_si, r_small = _exchange_wait("l1_grads_wait", l1_handles, "scatter", sm_token)
    si_out = [o.T for o in _adamw("adamw_ssd_w_in", ssd_w_in[0].T, r_si, m_ssd_w_in[0].T, v_ssd_w_in[0].T, n_ssd_in // NDEV, 256)]
    so_out = _adamw("adamw_ssd_w_out", ssd_w_out[0], r_so, m_ssd_w_out[0], v_ssd_w_out[0], 256)
    cw_cols = ssd_conv_w.shape[2]
    cw_out, cb_out, snw_out = _adamw_windows(
        "adamw_ssd_small", r_small,
        [(ssd_conv_w, m_ssd_conv_w, v_ssd_conv_w), (ssd_conv_b, m_ssd_conv_b, v_ssd_conv_b),
         (ssd_norm_w, m_ssd_norm_w, v_ssd_norm_w)],
        [(0, slice(0, CONV_WIDTH), slice(0, cw_cols), (0, slice(None), slice(None))),
         (1, slice(5, 6), slice(0, cw_cols), whole), (2, slice(6, 7), slice(0, si // NDEV), whole)])
    r_ai, r_ao = _exchange_wait("l0_grads_wait", l0_handles, "chips", so_out[0])
    ai_out = _adamw("adamw_attn_w_in", attn_w_in[0], r_ai, m_attn_w_in[0], v_attn_w_in[0], 256)
    ao_out = _adamw("adamw_attn_w_out", attn_w_out[0], r_ao, m_attn_w_out[0], v_attn_w_out[0], 192)

    (small_all,) = _exchange_wait("small_grads_wait", sm_handles, "gather", ai_out[0])
    full = slice(0, PACK_COLS)
    nhd = SSD_HEADS
    windows = [(0, slice(3 * i + k, 3 * i + k + 1), full, (slice(i, i + 1), slice(k * dm, (k + 1) * dm)))
               for i in range(2) for k in range(3)]
    windows += [(1, slice(6 + i, 7 + i), full, (slice(i, i + 1), slice(None))) for i in range(2)]
    windows += [(2, slice(8, 9), full, whole)]
    windows += [(3 + q, slice(9, 10), slice(2 * nhd * q + nhd * j, 2 * nhd * q + nhd * (j + 1)), (0, slice(j, j + 1), slice(None)))
                for q in range(2) for j in range(2)]
    windows += [(5, slice(9, 10), slice(4 * nhd, 5 * nhd), whole)]
    as_row = lambda a: a.reshape(1, dm)
    mb_out, nw_out, fnw_out, dtb_out, alog_out, d_out, loss = _adamw_windows(
        "adamw_small", small_all,
        [(mod_b, m_mod_b, v_mod_b), (norm_w, m_norm_w, v_norm_w), (fnw, as_row(m_final_norm_w), as_row(v_final_norm_w)),
         (ssd_dt_bias, m_ssd_dt_bias, v_ssd_dt_bias), (ssd_a_log, m_ssd_a_log, v_ssd_a_log), (ssd_d, m_ssd_d, v_ssd_d)],
        windows, extra=(slice(9, 10), slice(256, 257)))
    loss = loss.reshape(())

    ncol = mod_w.shape[2]
    dmod_all = small_all[:, 0:6, :].reshape(NDEV, 2, 3 * dm)
    dmod_sh = lax.dynamic_slice_in_dim(dmod_all, me * ncol, ncol, axis=2).transpose(1, 0, 2)
    g_modw = _mod_grad(c_all, dmod_sh).reshape(1, 2 * dm, ncol)
    modw_out = _adamw("adamw_mod_w", mod_w.reshape(2 * dm, ncol), g_modw, m_mod_w.reshape(2 * dm, ncol),
                      v_mod_w.reshape(2 * dm, ncol), 256)

    per_kind = []
    for k in range(4):
        per_kind.append([
            nw_out[k], modw_out[k].reshape(mod_w.shape), mb_out[k], ai_out[k][None], ao_out[k][None], si_out[k][None],
            cw_out[k], cb_out[k], dtb_out[k], alog_out[k], d_out[k], snw_out[k], so_out[k][None], fnw_out[k].reshape(dm)])
    return (loss, dx0.reshape(x.shape), *per_kind[0], *per_kind[1], *per_kind[2], *per_kind[3])


def _pack_ssd_small_blocks(g_cw, g_cb, g_nw):
    nper = g_cw.shape[1] // NDEV
    nwper = g_nw.shape[1] // NDEV

    def body(cw_ref, cb_ref, nw_ref, o_ref):
        o_ref[...] = jnp.zeros_like(o_ref)
        for d in range(NDEV):
            o_ref[d, 0:5, :] = cw_ref[:, d * nper:(d + 1) * nper]
            o_ref[d, 5:6, :] = cb_ref[:, d * nper:(d + 1) * nper]
            o_ref[d, 6:7, 0:nwper] = nw_ref[:, d * nwper:(d + 1) * nwper]

    return pl.pallas_call(body, name="pack_ssd_small_grads", out_shape=jax.ShapeDtypeStruct((NDEV, 8, nper), F32))(g_cw, g_cb, g_nw)
```

```python
import functools
import math

import jax
import jax.numpy as jnp
from jax import lax
from jax.experimental import pallas as pl
from jax.experimental.pallas import tpu as pltpu

F32 = jnp.float32
BF16 = jnp.bfloat16
HI = lax.Precision.HIGHEST
MESH = pl.DeviceIdType.MESH
NDEV = 8

NORM_EPS = 1e-6
ROPE_THETA = 500000.0
ROT_DIM = 16
HEAD_DIM = 64
DILATIONS = (1, 4, 16)
BAND = 64
NEG_BIG = -1e30
CHUNK = 128
SSD_HEADS = 32
SSD_GROUPS = 8
CONV_WIDTH = 5

ADAM_LR = 0.001
ADAM_B1 = 0.9
ADAM_B2 = 0.999
ADAM_EPS = 1e-08
ADAM_WD = 0.01
ADAM_STEP = 10

VMEM_BIG = 56 * 1024 * 1024
MM_T = 1024


def _params(sem=None, vmem=None):
    kw = {}
    if sem is not None:
        kw["dimension_semantics"] = sem
    if vmem is not None:
        kw["vmem_limit_bytes"] = vmem
    return pltpu.CompilerParams(**kw)


def _dg(a, b, ca, cb, prec=None):
    return lax.dot_general(a, b, (((ca,), (cb,)), ((), ())), preferred_element_type=F32, precision=prec)


def _nn(a, b):
    return _dg(a.astype(BF16), b.astype(BF16), 1, 0)


def _nt(a, b):
    return _dg(a.astype(BF16), b.astype(BF16), 1, 1)


def _tn(a, b):
    return _dg(a.astype(BF16), b.astype(BF16), 0, 0)


def _hnn(a, b):
    return _dg(a, b, 1, 0, HI)


@jax.custom_vjp
def _bnn(a, b):
    return _nn(a, b)


_bnn.defvjp(lambda a, b: (_nn(a, b), (a, b)), lambda r, g: (_nt(g, r[1]), _tn(r[0], g)))


@jax.custom_vjp
def _bnt(a, b):
    return _nt(a, b)


_bnt.defvjp(lambda a, b: (_nt(a, b), (a, b)), lambda r, g: (_nn(g, r[1]), _tn(g, r[0])))


@jax.custom_vjp
def _btn(a, b):
    return _tn(a, b)


_btn.defvjp(lambda a, b: (_tn(a, b), (a, b)), lambda r, g: (_nt(r[1], g), _nn(r[0], g)))


def _silu(x):
    return x * jax.nn.sigmoid(x)


def _b_spec(b, mode, tn, tk, no, ko, jk):
    if mode == "nt":
        return pl.BlockSpec((tn, tk), lambda *g: (jk(*g)[0] + no, jk(*g)[1] + ko))
    return pl.BlockSpec((tk, tn), lambda *g: (jk(*g)[1] + ko, jk(*g)[0] + no))


def _matmul(name, a, b, mode, out_dtype, tm, tn, tk, *, epilogue=None, tiled=(), mrows=(), ncols=(),
            b_noff=0, b_koff=0, n_out=None, out_blocks=None, dest=None):
    if mode == "tn":
        K, M = a.shape
    else:
        M, K = a.shape
    N = n_out if n_out is not None else (b.shape[0] if mode == "nt" else b.shape[1])
    tm, tn, tk = min(tm, M), min(tn, N), min(tk, K)
    assert M % tm == 0 and N % tn == 0 and K % tk == 0, (name, M, N, K, tm, tn, tk)
    assert b_noff % tn == 0 and b_koff % tk == 0
    no, ko = b_noff // tn, b_koff // tk
    nk = K // tk
    if mode == "tn":
        a_spec = pl.BlockSpec((tk, tm), lambda i, j, k: (k, i))
    else:
        a_spec = pl.BlockSpec((tm, tk), lambda i, j, k: (i, k))
    specs = [a_spec, _b_spec(b, mode, tn, tk, no, ko, lambda i, j, k: (j, k))]
    specs += [pl.BlockSpec((tm, tn), lambda i, j, k: (i, j)) for _ in tiled]
    specs += [pl.BlockSpec((tm, r.shape[1]), lambda i, j, k: (i, 0)) for r in mrows]
    specs += [pl.BlockSpec((1, tn), lambda i, j, k: (0, j)) for _ in ncols]
    total, off, earlier = dest if dest is not None else (None, 0, None)
    if out_blocks is None:
        assert off % tm == 0
        mo = off // tm
        out_shape = jax.ShapeDtypeStruct((M if total is None else total, N), out_dtype)
        out_spec = pl.BlockSpec((tm, tn), lambda i, j, k: (i + mo, j))
    else:
        nper = N // out_blocks
        assert nper % tn == 0
        jb = nper // tn
        out_shape = jax.ShapeDtypeStruct((out_blocks if total is None else total, M, nper), out_dtype)
        out_spec = pl.BlockSpec((None, tm, tn), lambda i, j, k: (j // jb + off, i, j % jb))
    if earlier is not None:
        assert earlier.shape == out_shape.shape and earlier.dtype == out_shape.dtype
    ne = len(tiled) + len(mrows) + len(ncols)
    dot = {"nn": _nn, "nt": _nt, "tn": _tn}[mode]

    def body(a_ref, b_ref, *rest):
        extras, o_ref = rest[:ne], rest[ne]

        def finish(acc):
            if epilogue is not None:
                acc = epilogue(acc, *[e[...] for e in extras])
            o_ref[...] = acc.astype(o_ref.dtype)

        if nk == 1:
            finish(dot(a_ref[...], b_ref[...]))
        else:
            acc_ref = rest[ne + 1]
            k = pl.program_id(2)

            @pl.when(k == 0)
            def _():
                acc_ref[...] = jnp.zeros_like(acc_ref)

            acc_ref[...] += dot(a_ref[...], b_ref[...])

            @pl.when(k == nk - 1)
            def _():
                finish(acc_ref[...])

    args = [a, b, *tiled, *mrows, *ncols]
    aliases = {}
    if earlier is not None:
        specs.append(pl.BlockSpec(memory_space=pl.ANY))
        aliases = {len(args): 0}
        args.append(earlier)

    def body_with_dest(*refs):
        body(*refs[:2 + ne], *refs[2 + ne + (earlier is not None):])

    return pl.pallas_call(
        body_with_dest, name=name, out_shape=out_shape, grid=(M // tm, N // tn, nk),
        in_specs=specs, out_specs=out_spec, input_output_aliases=aliases,
        scratch_shapes=[] if nk == 1 else [pltpu.VMEM((tm, tn), F32)],
        compiler_params=_params(("parallel", "parallel", "arbitrary"), VMEM_BIG),
    )(*args)


def _matmul_rows(name, a, b, mode, tm, tk, fn, rows, consts, outs, accs, *, n_out=None, b_noff=0, b_koff=0):
    rl = [(t, t.shape[1], 0) if not isinstance(t, tuple) else t for t in rows]
    make_a = a if callable(a) else None
    M, K = (rl[0][0].shape[0], b.shape[1 if mode == "nt" else 0]) if make_a else a.shape
    N = n_out if n_out is not None else (b.shape[0] if mode == "nt" else b.shape[1])
    tm, tk = min(tm, M), min(tk, K)
    assert M % tm == 0 and K % tk == 0 and b_koff % tk == 0 and b_noff % N == 0, (name, M, N, K)
    no, ko, nk = b_noff // N, b_koff // tk, K // tk
    assert make_a is None or nk == 1
    nr, nc, no_, na = len(rl), len(consts), len(outs), len(accs)
    dot = _nt if mode == "nt" else _nn

    def body(*refs):
        a_ref, b_ref, rest = (None, refs[0], refs[1:]) if make_a else (refs[0], refs[1], refs[2:])
        r_refs, c_refs = rest[:nr], rest[nr:nr + nc]
        o_refs, acc_refs = rest[nr + nc:nr + nc + no_], rest[nr + nc + no_:nr + nc + no_ + na]
        i, k = pl.program_id(0), pl.program_id(1)

        def finish(prod, *made):
            res_o, res_a = fn(prod, *made, *[r[...] for r in r_refs], *[c[...] for c in c_refs])
            for r, v in zip(o_refs, res_o, strict=True):
                r[...] = v.astype(r.dtype)
            if acc_refs:
                @pl.when(i == 0)
                def _():
                    for r in acc_refs:
                        r[...] = jnp.zeros_like(r)

                for r, v in zip(acc_refs, res_a, strict=True):
                    r[...] += v

        if make_a:
            left = make_a(*[r[...] for r in r_refs], *[c[...] for c in c_refs])
            finish(dot(left, b_ref[...]), left)
        elif nk == 1:
            finish(dot(a_ref[...], b_ref[...]))
        else:
            prod_ref = rest[-1]

            @pl.when(k == 0)
            def _():
                prod_ref[...] = jnp.zeros_like(prod_ref)

            prod_ref[...] += dot(a_ref[...], b_ref[...])

            @pl.when(k == nk - 1)
            def _():
                finish(prod_ref[...])

    b_spec = _b_spec(b, mode, N, tk, no, ko, lambda i, k: (0, k))
    in_specs = ([] if make_a else [pl.BlockSpec((tm, tk), lambda i, k: (i, k))]) + [b_spec]
    in_specs += [pl.BlockSpec((tm, w), functools.partial(lambda i, k, cb: (i, cb), cb=cb)) for (_, w, cb) in rl]
    in_specs += [pl.BlockSpec(c.shape, lambda i, k: (0, 0)) for c in consts]
    out_specs = [pl.BlockSpec((tm, c), lambda i, k: (i, 0)) for (c, _) in outs]
    out_specs += [pl.BlockSpec(shp, lambda i, k: (0, 0)) for shp in accs]
    out_shape = [jax.ShapeDtypeStruct((M, c), dt) for (c, dt) in outs] + [jax.ShapeDtypeStruct(shp, F32) for shp in accs]
    res = pl.pallas_call(
        body, name=name, out_shape=out_shape, grid=(M // tm, nk), in_specs=in_specs, out_specs=out_specs,
        scratch_shapes=[] if nk == 1 else [pltpu.VMEM((tm, N), F32)],
        compiler_params=_params(("arbitrary" if accs else "parallel", "arbitrary"), VMEM_BIG),
    )(*([] if make_a else [a]), b, *[t[0] for t in rl], *consts)
    return res[:no_], res[no_:]


def _rowwise(name, fn, tiled, consts, outs, accs, ts):
    tl = [(t, t.shape[1], 0) if not isinstance(t, tuple) else t for t in tiled]
    s_len = tl[0][0].shape[0]
    assert s_len % ts == 0
    nt_, nc_, no_ = len(tl), len(consts), len(outs)

    def body(*refs):
        t_refs, c_refs = refs[:nt_], refs[nt_:nt_ + nc_]
        o_refs, a_refs = refs[nt_ + nc_:nt_ + nc_ + no_], refs[nt_ + nc_ + no_:]
        res_o, res_a = fn(*[r[...] for r in t_refs], *[r[...] for r in c_refs])
        for r, v in zip(o_refs, res_o, strict=True):
            r[...] = v.astype(r.dtype)
        if a_refs:
            @pl.when(pl.program_id(0) == 0)
            def _():
                for r in a_refs:
                    r[...] = jnp.zeros_like(r)

            for r, v in zip(a_refs, res_a, strict=True):
                r[...] += v

    in_specs = [pl.BlockSpec((ts, w), functools.partial(lambda i, cb: (i, cb), cb=cb)) for (_, w, cb) in tl]
    in_specs += [pl.BlockSpec(c.shape, lambda i: (0, 0)) for c in consts]
    out_specs = [pl.BlockSpec((ts, c), lambda i: (i, 0)) for (c, _) in outs]
    out_specs += [pl.BlockSpec(shp, lambda i: (0, 0)) for shp in accs]
    out_shape = [jax.ShapeDtypeStruct((s_len, c), dt) for (c, dt) in outs]
    out_shape += [jax.ShapeDtypeStruct(shp, F32) for shp in accs]
    res = pl.pallas_call(
        body, name=name, out_shape=out_shape, grid=(s_len // ts,), in_specs=in_specs, out_specs=out_specs,
        compiler_params=_params(("arbitrary",) if accs else ("parallel",), VMEM_BIG),
    )(*[t[0] for t in tl], *consts)
    return res[:no_], res[no_:]


def _norm_mod_fn(x, nw, sc, sh):
    r = lax.rsqrt(jnp.mean(x * x, axis=-1, keepdims=True) + NORM_EPS)
    return (x * r * nw) * (1.0 + sc) + sh


def _norm_mod_fwd(name, x, nw, sc, sh):
    (hn,), _ = _rowwise(name, lambda x, nw, sc, sh: ([_norm_mod_fn(x, nw, sc, sh)], []),
                        [x], [nw, sc, sh], [(x.shape[1], BF16)], [], 512)
    return hn


def _norm_mod_bwd(name, last, x, dhn_parts, dres, nw, sc, sh, prev=None):
    n = len(dhn_parts)
    d = x.shape[1]
    a, b, mode, tk, kw = last

    def fn(dhn, x, *rest):
        for p in rest[:n]:
            dhn = dhn + p
        dres, rest = rest[n], rest[n + 1:]
        y_prev, (nw, sc, sh), gate = (rest[0], rest[1:4], rest[4]) if prev is not None else (None, rest[0:3], None)
        r = lax.rsqrt(jnp.mean(x * x, axis=-1, keepdims=True) + NORM_EPS)
        xh = x * r
        dxh = dhn * (nw * (1.0 + sc))
        dx = r * (dxh - xh * jnp.mean(dxh * xh, axis=-1, keepdims=True)) + dres
        along = jnp.sum(dhn * xh, axis=0, keepdims=True)
        dnw, dsc, dsh = along * (1.0 + sc), along * nw, jnp.sum(dhn, axis=0, keepdims=True)
        if prev is None:
            return [dx], [dnw, dsc, dsh]
        return [dx, gate * dx], [dnw, dsc, dsh, jnp.sum(dx * y_prev, axis=0, keepdims=True)]

    rows = [x, *dhn_parts, dres] + ([prev[0]] if prev is not None else [])
    consts = [nw, sc, sh] + ([prev[1]] if prev is not None else [])
    outs = [(d, F32)] + ([(d, BF16)] if prev is not None else [])
    res_o, res_a = _matmul_rows(name, a, b, mode, 512, tk, fn, rows, consts, outs, [(1, d)] * (3 + (prev is not None)), **kw)
    return (*res_o, *res_a)


def _rope_tables(pos_col, inv_row):
    def fn(pos, inv):
        ang = pos.astype(F32) * inv
        e = lax.broadcasted_iota(jnp.int32, (1, 128), 1) % HEAD_DIM
        cos, sin = jnp.cos(ang), jnp.sin(ang)
        half = ROT_DIM // 2
        return [jnp.where(e < ROT_DIM, cos, 1.0), jnp.where(e < half, -sin, 0.0),
                jnp.where((e >= half) & (e < ROT_DIM), sin, 0.0)], []

    (c, sa, sb), _ = _rowwise("rope_tables", fn, [pos_col], [inv_row], [(128, F32)] * 3, [], 512)
    return c, sa, sb


def _rot_fwd(t, c, sa, sb):
    n = t.shape[1]
    rep = n // 128
    c, sa, sb = (jnp.tile(u, (1, rep)) for u in (c, sa, sb))
    return t * c + pltpu.roll(t, n - ROT_DIM // 2, 1) * sa + pltpu.roll(t, ROT_DIM // 2, 1) * sb


def _rot_bwd(g, c, sa, sb):
    n = g.shape[1]
    rep = n // 128
    c, sa, sb = (jnp.tile(u, (1, rep)) for u in (c, sa, sb))
    return g * c + pltpu.roll(g * sa, ROT_DIM // 2, 1) + pltpu.roll(g * sb, n - ROT_DIM // 2, 1)


ATT_TQ = 128


def _attn_tiles(l):
    tk = ATT_TQ + 2 * BAND
    return (l, l) if l <= tk else (ATT_TQ, tk)


def _attn_specs(g, s_len):
    def blk(off):
        return pl.BlockSpec((s_len, 128), functools.partial(lambda hp, off: (0, off + hp), off=off))

    return blk(4 * g), blk(12 + 4 * g), blk(4 * g), blk(0)


def _attn_tile_geometry(t, d, l):
    tq, tk = _attn_tiles(l)
    nts = l // tq
    r = t // nts
    ts = t % nts
    q0 = ts * tq
    ws = jnp.clip(q0 - BAND, 0, l - tk)
    kind = jnp.where(ts == 0, 0, jnp.where(ts == nts - 1, 2, 1))
    if d == 1:
        return pl.ds(pl.multiple_of(q0, tq), tq), pl.ds(pl.multiple_of(ws, BAND), tk), kind
    return pl.ds(r + d * q0, tq, stride=d), pl.ds(r + d * ws, tk, stride=d), kind


def _attn_fill_bias(bias_ref):
    _, tq2, tk = bias_ref.shape
    iq = lax.broadcasted_iota(jnp.int32, (tq2, 1), 0) % (tq2 // 2)
    ik = lax.broadcasted_iota(jnp.int32, (1, tk), 1)
    for i, off in enumerate((0, -BAND, -2 * BAND)):
        bias_ref[i] = jnp.where(jnp.abs(ik + off - iq) <= BAND, 0.0, NEG_BIG)


def _split_heads(t, in_h):
    zero = jnp.zeros_like(t)
    return jnp.concatenate([jnp.where(in_h[0], t, zero), jnp.where(in_h[1], t, zero)], axis=0)


def _attn_fwd(g, qk, v):
    s_len = qk.shape[0]
    d = DILATIONS[g]
    l = s_len // d
    tq, tk = _attn_tiles(l)
    assert l % tq == 0 and l >= tk
    q_spec, k_spec, v_spec, o_spec = _attn_specs(g, s_len)
    scale = 1.0 / math.sqrt(HEAD_DIM)

    def body(q_ref, k_ref, v_ref, o_ref, lse_ref, bias_ref):
        lane = lax.broadcasted_iota(jnp.int32, (1, 128), 1)
        in_h = [lane < HEAD_DIM, lane >= HEAD_DIM]
        _attn_fill_bias(bias_ref)

        def tile(t, carry):
            rows, win, kind = _attn_tile_geometry(t, d, l)
            q = (q_ref[rows, :] * scale).astype(BF16)
            k = k_ref[win, :].astype(BF16)
            vv = v_ref[win, :].astype(BF16)
            s = _nt(_split_heads(q, in_h), k) + bias_ref[kind]
            m = jnp.max(s, axis=1, keepdims=True)
            p = jnp.exp(s - m)
            den = jnp.sum(p, axis=1, keepdims=True)
            out = _nn(p, vv) / den
            lse = m + jnp.log(den)
            o_ref[rows, :] = jnp.where(in_h[0], out[:tq], out[tq:])
            lse_ref[rows, :] = jnp.where(in_h[0], lse[:tq], lse[tq:])
            return carry

        lax.fori_loop(0, s_len // tq, tile, 0, unroll=8 * ATT_TQ // tq)

    return pl.pallas_call(
        body, name=f"attn_fwd_g{g}", grid=(4,),
        out_shape=[jax.ShapeDtypeStruct((s_len, 512), F32)] * 2,
        in_specs=[q_spec, k_spec, v_spec], out_specs=[o_spec, o_spec],
        scratch_shapes=[pltpu.VMEM((3, 2 * tq, tk), F32)],
        compiler_params=_params(("parallel",), VMEM_BIG),
    )(qk, qk, v)


def _attn_bwd(g, qk, v, o, lse, do, dlse):
    s_len = qk.shape[0]
    d = DILATIONS[g]
    l = s_len // d
    tq, tk = _attn_tiles(l)
    q_spec, k_spec, v_spec, o_spec = _attn_specs(g, s_len)
    scale = 1.0 / math.sqrt(HEAD_DIM)

    def body(q_ref, k_ref, v_ref, o_ref, lse_ref, do_ref, dlse_ref, dq_ref, dk_ref, dv_ref, bias_ref):
        lane = lax.broadcasted_iota(jnp.int32, (1, 128), 1)
        in_h = [lane < HEAD_DIM, lane >= HEAD_DIM]
        dk_ref[...] = jnp.zeros_like(dk_ref)
        dv_ref[...] = jnp.zeros_like(dv_ref)
        _attn_fill_bias(bias_ref)

        def tile(t, carry):
            rows, win, kind = _attn_tile_geometry(t, d, l)
            k, vv = k_ref[win, :].astype(BF16), v_ref[win, :].astype(BF16)
            dout, lse_t, dlse_t = do_ref[rows, :], lse_ref[rows, :], dlse_ref[rows, :]
            od = dout * o_ref[rows, :]
            q2 = _split_heads((q_ref[rows, :] * scale).astype(BF16), in_h)
            do2 = _split_heads(dout.astype(BF16), in_h)
            head_col = lambda a: jnp.concatenate([a[:, 0:1], a[:, HEAD_DIM:HEAD_DIM + 1]], axis=0)
            delta = jnp.concatenate([jnp.sum(jnp.where(m, od, 0.0), axis=1, keepdims=True) for m in in_h], axis=0)
            p = jnp.exp(_nt(q2, k) + bias_ref[kind] - head_col(lse_t))
            ds = (p * (_nt(do2, vv) - delta + head_col(dlse_t))).astype(BF16)
            dq2 = _nn(ds, k) * scale
            dq_ref[rows, :] = jnp.where(in_h[0], dq2[:tq], dq2[tq:])
            dk_ref[win, :] += _tn(ds, q2)
            dv_ref[win, :] += _tn(p, do2)
            return carry

        lax.fori_loop(0, s_len // tq, tile, 0, unroll=8 * ATT_TQ // tq)

    return pl.pallas_call(
        body, name=f"attn_bwd_g{g}", grid=(4,),
        out_shape=[jax.ShapeDtypeStruct((s_len, 512), F32)] * 3,
        in_specs=[q_spec, k_spec, v_spec, o_spec, o_spec, o_spec, o_spec], out_specs=[o_spec] * 3,
        scratch_shapes=[pltpu.VMEM((3, 2 * tq, tk), F32)],
        compiler_params=_params(("parallel",), VMEM_BIG),
    )(qk, qk, v, o, lse, do, dlse)


def _mix_weights(ls):
    mx = jnp.maximum(jnp.maximum(ls[0], ls[1]), ls[2])
    es = [jnp.exp(x - mx) for x in ls]
    tot = es[0] + es[1] + es[2]
    return [e / tot for e in es]


def _attn_out(os_, lses, z, x, gate, w_out):
    s_len, dm = x.shape
    tm = 256
    wdt = 512
    z, z_block = z

    def body(o0, o1, o2, l0, l1, l2, z_ref, x_ref, g_ref, w_ref, a_ref, y_ref, x1_ref):
        alphas = _mix_weights([l0[...], l1[...], l2[...]])
        y = jnp.zeros((tm, dm), F32)
        for g, o_ref in enumerate((o0, o1, o2)):
            a_g = (o_ref[...] * alphas[g] * _silu(z_ref[:, g * wdt:(g + 1) * wdt])).astype(BF16)
            a_ref[:, g * wdt:(g + 1) * wdt] = a_g
            y = y + _nn(a_g, w_ref[g * wdt:(g + 1) * wdt, :])
        y_ref[...] = y
        x1_ref[...] = x_ref[...] + g_ref[...] * y

    row = lambda c: pl.BlockSpec((tm, c), lambda i: (i, 0))
    return pl.pallas_call(
        body, name="attn_out", grid=(s_len // tm,),
        out_shape=[jax.ShapeDtypeStruct((s_len, 3 * wdt), BF16), jax.ShapeDtypeStruct((s_len, dm), F32),
                   jax.ShapeDtypeStruct((s_len, dm), F32)],
        in_specs=[row(wdt)] * 6 + [pl.BlockSpec((tm, 3 * wdt), lambda i: (i, z_block)), row(dm),
                                   pl.BlockSpec((1, dm), lambda i: (0, 0)), pl.BlockSpec(w_out.shape, lambda i: (0, 0))],
        out_specs=[row(3 * wdt), row(dm), row(dm)],
        compiler_params=_params(("parallel",), VMEM_BIG),
    )(*os_, *lses, z, x, gate, w_out)


def _mix_bwd(dy, w_out, os_, lses, z):
    wdt = 512

    def fn(da, o0, o1, o2, l0, l1, l2, z):
        os_t, ls = [o0, o1, o2], [l0, l1, l2]
        alphas = _mix_weights(ls)
        hi = lax.broadcasted_iota(jnp.int32, (2 * wdt, wdt), 0) % wdt // HEAD_DIM
        hj = lax.broadcasted_iota(jnp.int32, (2 * wdt, wdt), 1) // HEAD_DIM
        seg = (hi == hj).astype(BF16)
        head_sum = lambda t: _dg(jnp.concatenate(_bf16_parts(t, 2), axis=1), seg, 1, 0)
        dos, dal, dzs = [], [], []
        for g in range(3):
            zg = z[:, g * wdt:(g + 1) * wdt]
            sig = jax.nn.sigmoid(zg)
            dag = da[:, g * wdt:(g + 1) * wdt]
            dmix = dag * zg * sig
            dzs.append(dag * os_t[g] * alphas[g] * (sig * (1.0 + zg * (1.0 - sig))))
            dos.append(dmix * alphas[g])
            dal.append(head_sum(dmix * os_t[g]))
        mean = alphas[0] * dal[0] + alphas[1] * dal[1] + alphas[2] * dal[2]
        dls = [alphas[g] * (dal[g] - mean) for g in range(3)]
        return dos + dls + [jnp.concatenate(dzs, axis=1)], []

    outs, _ = _matmul_rows("attn_out_dx_mix_bwd", dy, w_out, "nt", 256, dy.shape[1], fn, [*os_, *lses, (z[0], 3 * wdt, z[1])], [],
                           [(wdt, F32)] * 6 + [(3 * wdt, BF16)], [])
    return outs[:3], outs[3:6], outs[6]


def _rot_pack_bwd(dqs, dks, dvs, tabs):
    wdt = 512

    def fn(*args):
        grads, (c, sa, sb) = args[:9], args[9:]
        cols = [_rot_bwd(gq, c, sa, sb) for gq in grads[:6]] + list(grads[6:])
        return [jnp.concatenate(cols, axis=1)], []

    (out,), _ = _rowwise("rot_pack_bwd", fn, [*dqs, *dks, *dvs, *tabs], [], [(9 * wdt, BF16)], [], 512)
    return out


CONV_CB = 128
CONV_R = 256
CONV_PAD = 8


def _conv_taps(buf, base, off, sign):
    return [buf[pl.ds(base + off + sign * j, CONV_R), :] for j in range(CONV_WIDTH)]


def _conv_tap_sum(taps, w):
    acc = None
    for j, t in enumerate(taps):
        term = t * w[j:j + 1, :]
        acc = term if acc is None else acc + term
    return acc


def _conv_fwd(xpre, cw, cb):
    s_len, ch = xpre.shape
    nchunk = s_len // CONV_R

    def body(x_ref, w_ref, b_ref, o_ref, xp):
        zero = jnp.zeros((CONV_PAD, CONV_CB), F32)
        xp[0:CONV_PAD, :] = zero
        xp[s_len + CONV_PAD:s_len + 2 * CONV_PAD, :] = zero

        def fill(ci, carry):
            base = pl.multiple_of(ci * CONV_R, CONV_R)
            xp[pl.ds(base + CONV_PAD, CONV_R), :] = x_ref[pl.ds(base, CONV_R), :]
            return carry

        lax.fori_loop(0, nchunk, fill, 0)
        w = w_ref[...]
        b = b_ref[...]

        def chunk(ci, carry):
            base = pl.multiple_of(ci * CONV_R, CONV_R)
            u = _conv_tap_sum(_conv_taps(xp, base, CONV_PAD - CONV_WIDTH // 2, 1), w) + b
            o_ref[pl.ds(base, CONV_R), :] = _silu(u)
            return carry

        lax.fori_loop(0, nchunk, chunk, 0, unroll=2)

    col = lambda r: pl.BlockSpec((r, CONV_CB), lambda j: (0, j))
    return pl.pallas_call(
        body, name="conv_fwd", grid=(ch // CONV_CB,), out_shape=jax.ShapeDtypeStruct((s_len, ch), F32),
        in_specs=[col(s_len), col(CONV_WIDTH), col(1)], out_specs=col(s_len),
        scratch_shapes=[pltpu.VMEM((s_len + 2 * CONV_PAD, CONV_CB), F32)],
        compiler_params=_params(("parallel",), VMEM_BIG),
    )(xpre, cw, cb)


def _conv_bwd(xpre, da, cw, cb):
    s_len, ch = xpre.shape
    nchunk = s_len // CONV_R
    half = CONV_WIDTH // 2

    def body(x_ref, da_ref, w_ref, b_ref, dx_ref, gw_ref, gb_ref, xp, dcp):
        zero = jnp.zeros((CONV_PAD, CONV_CB), F32)
        for buf in (xp, dcp):
            buf[0:CONV_PAD, :] = zero
            buf[s_len + CONV_PAD:s_len + 2 * CONV_PAD, :] = zero

        def fill(ci, carry):
            base = pl.multiple_of(ci * CONV_R, CONV_R)
            xp[pl.ds(base + CONV_PAD, CONV_R), :] = x_ref[pl.ds(base, CONV_R), :]
            return carry

        lax.fori_loop(0, nchunk, fill, 0)
        w = w_ref[...]
        b = b_ref[...]

        def first(ci, carry):
            base = pl.multiple_of(ci * CONV_R, CONV_R)
            taps = _conv_taps(xp, base, CONV_PAD - half, 1)
            u = _conv_tap_sum(taps, w) + b
            sig = jax.nn.sigmoid(u)
            dc = da_ref[pl.ds(base, CONV_R), :] * (sig * (1.0 + u * (1.0 - sig)))
            dcp[pl.ds(base + CONV_PAD, CONV_R), :] = dc
            gb = carry[0] + jnp.sum(dc, axis=0, keepdims=True)
            gws = [carry[1 + j] + jnp.sum(dc * taps[j], axis=0, keepdims=True) for j in range(CONV_WIDTH)]
            return (gb, *gws)

        z1 = jnp.zeros((1, CONV_CB), F32)
        sums = lax.fori_loop(0, nchunk, first, (z1,) * (1 + CONV_WIDTH), unroll=2)
        gb_ref[...] = sums[0]
        for j in range(CONV_WIDTH):
            gw_ref[j:j + 1, :] = sums[1 + j]

        def second(ci, carry):
            base = pl.multiple_of(ci * CONV_R, CONV_R)
            dx_ref[pl.ds(base, CONV_R), :] = _conv_tap_sum(_conv_taps(dcp, base, CONV_PAD + half, -1), w).astype(dx_ref.dtype)
            return carry

        lax.fori_loop(0, nchunk, second, 0, unroll=2)

    col = lambda r: pl.BlockSpec((r, CONV_CB), lambda j: (0, j))
    return pl.pallas_call(
        body, name="conv_bwd", grid=(ch // CONV_CB,),
        out_shape=[jax.ShapeDtypeStruct((s_len, ch), BF16), jax.ShapeDtypeStruct((CONV_WIDTH, ch), F32),
                   jax.ShapeDtypeStruct((1, ch), F32)],
        in_specs=[col(s_len), col(s_len), col(CONV_WIDTH), col(1)],
        out_specs=[col(s_len), col(CONV_WIDTH), col(1)],
        scratch_shapes=[pltpu.VMEM((s_len + 2 * CONV_PAD, CONV_CB), F32)] * 2,
        compiler_params=_params(("parallel",), VMEM_BIG),
    )(xpre, da, cw, cb)


SSD_GW = 256
SSD_N = 128
SSD_DTW = 128


def _bf16_parts(x, n):
    parts, rest = [], x
    for _ in range(n):
        p = rest.astype(BF16)
        parts.append(p)
        rest = rest - p.astype(F32)
    return parts


@jax.custom_vjp
def _expand(x, e):
    eb = e.astype(BF16)
    return _dg(jnp.concatenate(_bf16_parts(x, 2), axis=1), jnp.concatenate([eb, eb], axis=0), 1, 0)


def _expand_fwd(x, e):
    return _expand(x, e), e


def _expand_bwd(e, g):
    return _dg(g.astype(BF16), e.astype(BF16), 1, 1), jnp.zeros_like(e)


_expand.defvjp(_expand_fwd, _expand_bwd)


@jax.custom_vjp
def _running_sum(tri, x):
    tb = tri.astype(BF16)
    return sum(_dg(tb, p, 1, 0) for p in _bf16_parts(x, 3))


def _running_sum_fwd(tri, x):
    return _running_sum(tri, x), tri


def _running_sum_bwd(tri, g):
    tb = tri.astype(BF16)
    return jnp.zeros_like(tri), sum(_dg(tb, p, 0, 0) for p in _bf16_parts(g, 3))


_running_sum.defvjp(_running_sum_fwd, _running_sum_bwd)


def _pick_col(a, h):
    @jax.custom_vjp
    def pick(a):
        return a[:, h:h + 1]

    pick.defvjp(lambda a: (a[:, h:h + 1], None),
                lambda _, g: (g * (lax.broadcasted_iota(jnp.int32, (1, a.shape[1]), 1) == h).astype(F32),))
    return pick(a)


def _pick_row(a, h):
    @jax.custom_vjp
    def pick(a):
        return a[h:h + 1, :]

    pick.defvjp(lambda a: (a[h:h + 1, :], None),
                lambda _, g: (g * (lax.broadcasted_iota(jnp.int32, (a.shape[0], 1), 0) == h).astype(F32),))
    return pick(a)


def _ssd_mask(dirn):
    ri = lax.broadcasted_iota(jnp.int32, (CHUNK, CHUNK), 0)
    cj = lax.broadcasted_iota(jnp.int32, (CHUNK, CHUNK), 1)
    return (cj <= ri) if dirn == 0 else (cj >= ri)


def _ssd_rowsel(dirn):
    last = CHUNK - 1 if dirn == 0 else 0
    return (lax.broadcasted_iota(jnp.int32, (CHUNK, 1), 0) == last).astype(F32)


def _ssd_chunk_pre(dirn):
    nh = SSD_DTW

    def f(dt, alog):
        da = dt * (-jnp.exp(alog))
        cum = _running_sum(_ssd_mask(dirn).astype(F32), da)
        tot = jnp.sum(cum * _ssd_rowsel(dirn), axis=0, keepdims=True)
        hh = lax.broadcasted_iota(jnp.int32, (nh, SSD_HEADS * HEAD_DIM), 0)
        jj = lax.broadcasted_iota(jnp.int32, (nh, SSD_HEADS * HEAD_DIM), 1)
        expand = (hh == dirn * SSD_HEADS + jj // HEAD_DIM).astype(F32)
        return cum, cum.T, _expand(dt, expand), _expand(jnp.exp(tot - cum), expand), _expand(jnp.exp(cum), expand)

    return f


def _ssd_group_fn(g, dirn, stacked):
    def f(xs, bm, cm, st, cum, cum_t, dt_e, w_e, ce_e):
        mask = _ssd_mask(dirn)
        xdt = xs * dt_e
        cd_e = jnp.sum(ce_e * _ssd_rowsel(dirn), axis=0, keepdims=True)
        cb = _bnt(cm, bm)
        lane_head = lax.broadcasted_iota(jnp.int32, (1, SSD_GW), 1) // HEAD_DIM
        y = _bnn(cm, st) * ce_e
        decayed, inputs = [], []
        for j in range(4):
            hidx = dirn * SSD_HEADS + 4 * g + j
            col, row = _pick_col(cum, hidx), _pick_row(cum_t, hidx)
            dec = cb * jnp.exp(jnp.where(mask, col - row, NEG_BIG))
            head = (lane_head == j).astype(F32)
            if stacked:
                decayed.append(dec)
                inputs.append(xdt * head)
            else:
                y = y + _bnn(dec, xdt) * head
        if stacked:
            y = y + _bnn(jnp.concatenate(decayed, axis=1), jnp.concatenate(inputs, axis=0))
        st_out = st * cd_e + _btn(bm, xdt * w_e)
        return y, st_out

    return f


def _ssd_in_specs(kk):
    ln = CHUNK
    return [pl.BlockSpec((ln, 2048), lambda i: (kk(i), 0)),
            pl.BlockSpec((ln, 1024), lambda i: (kk(i), 2)),
            pl.BlockSpec((ln, 1024), lambda i: (kk(i), 3)),
            pl.BlockSpec((ln, SSD_DTW), lambda i: (kk(i), 0)),
            pl.BlockSpec((1, SSD_DTW), lambda i: (0, 0))]


def _ssd_fwd(xbc, dt, alog, dirn, prior=None):
    s_len = xbc.shape[0]
    nc = s_len // CHUNK
    kk = (lambda i: i) if dirn == 0 else (lambda i: nc - 1 - i)

    def body(x_ref, b_ref, c_ref, dt_ref, al_ref, *rest):
        prior_ref = rest[0] if prior is not None else None
        y_ref, sts_ref, st = rest[prior is not None:]

        @pl.when(pl.program_id(0) == 0)
        def _():
            st[...] = jnp.zeros_like(st)

        sts_ref[0] = st[...]
        cum, cum_t, dt_e, w_e, ce_e = _ssd_chunk_pre(dirn)(dt_ref[...], al_ref[...])
        for g in range(SSD_GROUPS):
            xc = slice(g * SSD_GW, (g + 1) * SSD_GW)
            gc = slice(g * SSD_N, (g + 1) * SSD_N)
            y, st_new = _ssd_group_fn(g, dirn, True)(x_ref[:, xc], b_ref[:, gc], c_ref[:, gc], st[:, xc], cum, cum_t,
                                               dt_e[:, xc], w_e[:, xc], ce_e[:, xc])
            y_ref[:, xc] = y if prior is None else y + prior_ref[:, xc]
            st[:, xc] = st_new

    return pl.pallas_call(
        body, name=f"ssd_fwd_d{dirn}", grid=(nc,),
        out_shape=[jax.ShapeDtypeStruct((s_len, 2048), F32), jax.ShapeDtypeStruct((nc, SSD_N, 2048), F32)],
        in_specs=_ssd_in_specs(kk) + ([pl.BlockSpec((CHUNK, 2048), lambda i: (kk(i), 0))] if prior is not None else []),
        out_specs=[pl.BlockSpec((CHUNK, 2048), lambda i: (kk(i), 0)),
                   pl.BlockSpec((1, SSD_N, 2048), lambda i: (kk(i), 0, 0))],
        scratch_shapes=[pltpu.VMEM((SSD_N, 2048), F32)],
        compiler_params=_params(("arbitrary",), VMEM_BIG),
    )(xbc, xbc, xbc, dt, alog, *([prior] if prior is not None else []))


def _ssd_bwd(xbc, dt, alog, states, dy, d_e, dirn, prior=None):
    s_len = xbc.shape[0]
    nc = s_len // CHUNK
    kk = (lambda i: nc - 1 - i) if dirn == 0 else (lambda i: i)

    def body(x_ref, b_ref, c_ref, dt_ref, al_ref, sts_ref, dy_ref, de_ref, *rest):
        prior_ref = rest[0] if prior is not None else None
        dx_ref, ddt_ref, dal_ref, dst = rest[prior is not None:]
        plus_prior = (lambda v, cols: v + prior_ref[:, cols]) if prior is not None else (lambda v, cols: v)

        @pl.when(pl.program_id(0) == 0)
        def _():
            dst[...] = jnp.zeros_like(dst)
            dal_ref[...] = jnp.zeros_like(dal_ref)

        (cum, cum_t, dt_e, w_e, ce_e), pre_vjp = jax.vjp(_ssd_chunk_pre(dirn), dt_ref[...], al_ref[...])
        dcum = jnp.zeros_like(cum)
        dcum_t = jnp.zeros_like(cum_t)
        d_dt_e, d_w_e, d_ce_e = [], [], []
        for g in range(SSD_GROUPS):
            xc = slice(g * SSD_GW, (g + 1) * SSD_GW)
            gc = slice(g * SSD_N, (g + 1) * SSD_N)
            _, vjp = jax.vjp(_ssd_group_fn(g, dirn, False), x_ref[:, xc], b_ref[:, gc], c_ref[:, gc], sts_ref[0, :, xc], cum, cum_t,
                             dt_e[:, xc], w_e[:, xc], ce_e[:, xc])
            dyg = dy_ref[:, xc]
            dxs, dbm, dcm, dst_g, dcum_g, dcum_t_g, ddte_g, dwe_g, dcee_g = vjp((dyg, dst[:, xc]))
            if dirn == 0:
                dxs = dxs + dyg * de_ref[:, xc]
            bc, cc = slice(2048 + g * SSD_N, 2048 + (g + 1) * SSD_N), slice(3072 + g * SSD_N, 3072 + (g + 1) * SSD_N)
            dx_ref[:, xc] = plus_prior(dxs, xc)
            dx_ref[:, bc] = plus_prior(dbm, bc)
            dx_ref[:, cc] = plus_prior(dcm, cc)
            dst[:, xc] = dst_g
            dcum = dcum + dcum_g
            dcum_t = dcum_t + dcum_t_g
            d_dt_e.append(ddte_g)
            d_w_e.append(dwe_g)
            d_ce_e.append(dcee_g)
        ddt, dal = pre_vjp((dcum, dcum_t, jnp.concatenate(d_dt_e, axis=1), jnp.concatenate(d_w_e, axis=1),
                            jnp.concatenate(d_ce_e, axis=1)))
        ddt_ref[...] = ddt
        dal_ref[...] += dal

    return pl.pallas_call(
        body, name=f"ssd_bwd_d{dirn}", grid=(nc,),
        out_shape=[jax.ShapeDtypeStruct((s_len, 4096), F32), jax.ShapeDtypeStruct((s_len, SSD_DTW), F32),
                   jax.ShapeDtypeStruct((1, SSD_DTW), F32)],
        in_specs=_ssd_in_specs(kk) + [pl.BlockSpec((1, SSD_N, 2048), lambda i: (kk(i), 0, 0)),
                                      pl.BlockSpec((CHUNK, 2048), lambda i: (kk(i), 0)),
                                      pl.BlockSpec((1, 2048), lambda i: (0, 0))]
        + ([pl.BlockSpec((CHUNK, 4096), lambda i: (kk(i), 0))] if prior is not None else []),
        out_specs=[pl.BlockSpec((CHUNK, 4096), lambda i: (kk(i), 0)),
                   pl.BlockSpec((CHUNK, SSD_DTW), lambda i: (kk(i), 0)),
                   pl.BlockSpec((1, SSD_DTW), lambda i: (0, 0))],
        scratch_shapes=[pltpu.VMEM((SSD_N, 2048), F32)],
        compiler_params=_params(("arbitrary",), VMEM_BIG),
    )(xbc, xbc, xbc, dt, alog, states, dy, d_e, *([prior] if prior is not None else []))


def _gate_norm_fn(y, xs, z, d_e, nw):
    yg = (y + xs * d_e) * _silu(z)
    return yg * lax.rsqrt(jnp.mean(yg * yg, axis=-1, keepdims=True) + NORM_EPS) * nw


def _gate_norm_bwd(dy, w_out, y, xbc, z, d_e, nw):
    def fn(du, y, xs, z, d_e, nw):
        sig = jax.nn.sigmoid(z)
        gate = z * sig
        ysum = y + xs * d_e
        yg = ysum * gate
        r = lax.rsqrt(jnp.mean(yg * yg, axis=-1, keepdims=True) + NORM_EPS)
        t = du * nw
        dyg = t * r - yg * (jnp.mean(t * yg, axis=-1, keepdims=True) * (r * r * r))
        dys = dyg * gate
        dz = dyg * ysum * (sig * (1.0 + z * (1.0 - sig)))
        dnw = jnp.sum(du * yg * r, axis=0, keepdims=True)
        dde = jnp.sum(dys * xs, axis=0, keepdims=True)
        hh = lax.broadcasted_iota(jnp.int32, (2048, SSD_HEADS), 0) // HEAD_DIM
        jj = lax.broadcasted_iota(jnp.int32, (2048, SSD_HEADS), 1)
        return [dys, dz], [dnw, _hnn(jnp.broadcast_to(dde, (8, 2048)), (hh == jj).astype(F32))[0:1]]

    (dys, dz), (g_nw, g_d) = _matmul_rows("ssd_out_dx_gate_norm_bwd", dy, w_out, "nt", 256, dy.shape[1], fn,
                                          [y, (xbc, 2048, 0), z], [d_e, nw], [(2048, F32), (2048, BF16)],
                                          [(1, 2048), (1, SSD_HEADS)])
    return dys, dz, g_nw, g_d


def _ssd_tail_loss(y, xbc, z, d_e, snw, w_out, x1, tgt, gate, fnw):
    dm = x1.shape[1]
    si = y.shape[1]

    def make_u(y, xs, z, x1, tgt, d_e, snw, gate, fnw):
        return _gate_norm_fn(y, xs, z, d_e, snw).astype(BF16)

    def fn(y1, u, y, xs, z, x1, tgt, d_e, snw, gate, fnw):
        x2 = x1 + gate * y1
        r = lax.rsqrt(jnp.mean(x2 * x2, axis=-1, keepdims=True) + NORM_EPS)
        xh = x2 * r
        err = xh * fnw - tgt
        loss = 0.5 * jnp.sum(jnp.mean(err * err, axis=-1, keepdims=True), axis=0, keepdims=True)
        dy = err * (1.0 / dm)
        dxh = dy * fnw
        dx2 = r * (dxh - xh * jnp.mean(dxh * xh, axis=-1, keepdims=True))
        dfnw = jnp.sum(dy * xh, axis=0, keepdims=True)
        return [u, dx2, gate * dx2], [dfnw, jnp.sum(dx2 * y1, axis=0, keepdims=True), jnp.broadcast_to(loss, (1, 128))]

    (u, dx2, dy1), (g_fnw, dgate, loss) = _matmul_rows(
        "ssd_out_loss", make_u, w_out, "nn", 256, si, fn, [y, (xbc, si, 0), z, x1, tgt], [d_e, snw, gate, fnw],
        [(si, BF16), (dm, F32), (dm, BF16)], [(1, dm), (1, dm), (1, 128)])
    return u, dx2, dy1, g_fnw, dgate, loss


def _softplus_fwd(dt_raw, bias):
    (dt,), _ = _rowwise("dt_softplus", lambda r, b: ([jax.nn.softplus(r + b)], []), [dt_raw], [bias],
                        [(dt_raw.shape[1], F32)], [], 512)
    return dt


def _softplus_bwd(ddt_f, ddt_b, dt_raw, bias):
    def fn(df, db, r, b):
        g = (df + db) * jax.nn.sigmoid(r + b)
        return [g], [jnp.sum(g, axis=0, keepdims=True)]

    w = dt_raw.shape[1]
    (g,), (gb,) = _rowwise("dt_softplus_bwd", fn, [ddt_f, ddt_b, dt_raw], [bias], [(w, BF16)], [(1, w)], 512)
    return g, gb


def _mod_part(c_all, mod_w):
    nl, _, ncol = mod_w.shape
    nb = c_all.shape[0]

    def body(c_ref, w_ref, o_ref):
        cond = _silu(c_ref[...])
        for i in range(nl):
            o_ref[i * nb:(i + 1) * nb, :] = _nn(cond, w_ref[i])

    return pl.pallas_call(body, name="mod_part", out_shape=jax.ShapeDtypeStruct((nl * nb, ncol), F32),
                          compiler_params=_params(None, VMEM_BIG))(c_all, mod_w)


def _mod_finish(mod_nb, mod_b, norm_w, tokens):
    nl, dm = norm_w.shape

    def body(a_ref, b_ref, nw_ref, *rest):
        tok_refs, o_refs = rest[:len(tokens)], rest[len(tokens):]
        tok = sum(t[0:1, 0:1] for t in tok_refs)
        for i in range(nl):
            for k in range(3):
                cols = slice(k * dm, (k + 1) * dm)
                o_refs[4 * i + k][...] = a_ref[i:i + 1, cols] + b_ref[i:i + 1, cols]
            o_refs[4 * i + 3][...] = nw_ref[i:i + 1, :] + tok

    rows = pl.pallas_call(body, name="mod_finish", out_shape=[jax.ShapeDtypeStruct((1, dm), F32)] * (4 * nl))(
        mod_nb, mod_b, norm_w, *tokens)
    return [rows[4 * i:4 * i + 4] for i in range(nl)]


def _mod_grad(c_all, dmod_sh):
    nl, nb, ncol = dmod_sh.shape
    dm = c_all.shape[1]

    def body(c_ref, d_ref, o_ref):
        cond = _silu(c_ref[...])
        for i in range(nl):
            o_ref[i] = _tn(cond, d_ref[i])

    return pl.pallas_call(body, name="mod_grad", out_shape=jax.ShapeDtypeStruct((nl, dm, ncol), F32),
                          compiler_params=_params(None, VMEM_BIG))(c_all, dmod_sh)


PACK_ROWS = 16
PACK_COLS = 1024


def _pack_small(rows, b64, a64s, d32, extra):
    nr, na = len(rows), len(a64s)

    def body(*refs):
        o_ref = refs[-1]
        o_ref[...] = jnp.zeros_like(o_ref)
        for i in range(nr):
            o_ref[i:i + 1, :] = refs[i][...]
        b_ref, a_refs, d_ref, e_ref = refs[nr], refs[nr + 1:nr + 1 + na], refs[nr + 1 + na], refs[nr + 2 + na]
        o_ref[nr:nr + 1, 0:64] = b_ref[:, 0:64]
        o_ref[nr:nr + 1, 64:128] = sum(a[:, 0:64] for a in a_refs)
        o_ref[nr:nr + 1, 128:160] = d_ref[...]
        o_ref[nr:nr + 1, 256:384] = e_ref[...]

    return pl.pallas_call(body, name="pack_small", out_shape=jax.ShapeDtypeStruct((PACK_ROWS, PACK_COLS), F32))(
        *rows, b64, *a64s, d32, extra)


def _pack_ssd_small(cw, cb, nw):
    def body(cw_ref, cb_ref, nw_ref, o_ref):
        o_ref[...] = jnp.zeros_like(o_ref)
        o_ref[0:5, :] = cw_ref[...]
        o_ref[5:6, :] = cb_ref[...]
        o_ref[6:7, 0:256] = nw_ref[...]

    return pl.pallas_call(body, name="pack_ssd_small", out_shape=jax.ShapeDtypeStruct((8, 512), F32))(cw, cb, nw)


def _sum_parts(p_ref):
    g = p_ref[0].astype(F32)
    for s in range(1, p_ref.shape[0]):
        g = g + p_ref[s].astype(F32)
    return g


def _adam_update(w, g, m, v):
    m2 = ADAM_B1 * m + (1.0 - ADAM_B1) * g
    v2 = ADAM_B2 * v + (1.0 - ADAM_B2) * (g * g)
    m_hat = m2 / (1.0 - ADAM_B1 ** ADAM_STEP)
    v_hat = v2 / (1.0 - ADAM_B2 ** ADAM_STEP)
    return -ADAM_LR * (m_hat / (jnp.sqrt(v_hat) + ADAM_EPS) + ADAM_WD * w), m2, v2


def _adamw_windows(name, parts, params, windows, extra=None):
    n = len(params)

    def body(p_ref, *rest):
        ins, outs = rest[:3 * n], rest[3 * n:]
        g = _sum_parts(p_ref)
        for pi, rows, cols, idx in windows:
            w_ref, m_ref, v_ref = ins[3 * pi:3 * pi + 3]
            gw = g[rows, cols]
            dw, m2, v2 = _adam_update(w_ref[idx], gw, m_ref[idx], v_ref[idx])
            for o_ref, val in zip(outs[4 * pi:4 * pi + 4], (gw, dw, m2, v2), strict=True):
                o_ref[idx] = val
        if extra is not None:
            outs[4 * n][...] = g[extra[0], extra[1]]

    out_shape = [jax.ShapeDtypeStruct(w.shape, F32) for (w, _, _) in params for _ in range(4)]
    if extra is not None:
        out_shape.append(jax.ShapeDtypeStruct((extra[0].stop - extra[0].start, extra[1].stop - extra[1].start), F32))
    res = pl.pallas_call(body, name=name, out_shape=out_shape)(parts, *[a for p in params for a in p])
    return [res[4 * i:4 * i + 4] for i in range(n)] + ([res[4 * n]] if extra is not None else [])


def _adamw(name, w, parts, m, v, tr, tc=None):
    r_, c_ = w.shape
    p_ = parts.shape[0]
    tr = min(tr, r_)
    tc = c_ if tc is None else tc
    assert r_ % tr == 0 and c_ % tc == 0

    def body(w_ref, p_ref, m_ref, v_ref, g_ref, d_ref, m2_ref, v2_ref):
        g = _sum_parts(p_ref)
        g_ref[...] = g
        d_ref[...], m2_ref[...], v2_ref[...] = _adam_update(w_ref[...], g, m_ref[...], v_ref[...])

    blk = pl.BlockSpec((tr, tc), lambda i, j: (i, j))
    return pl.pallas_call(
        body, name=name, grid=(r_ // tr, c_ // tc), out_shape=[jax.ShapeDtypeStruct((r_, c_), F32)] * 4,
        in_specs=[blk, pl.BlockSpec((p_, tr, tc), lambda i, j: (0, i, j)), blk, blk], out_specs=[blk] * 4,
        compiler_params=_params(("parallel", "parallel"), VMEM_BIG),
    )(w, parts, m, v)


def _dev_index(p):
    return 4 * p[0] + 2 * p[1] + p[2]


def _all_gather(name, xs):
    n = len(xs)
    hbm = pl.BlockSpec(memory_space=pl.ANY)

    def body(*refs):
        x_refs, o_refs = refs[:n], refs[n:2 * n]
        send_sems, recv_sems, local_sems = refs[2 * n:]
        x, y, c = lax.axis_index("x"), lax.axis_index("y"), lax.axis_index("c")
        me, sibling = (x, y, c), (x, y, 1 - c)
        chips = [(1 - x, y), (x, 1 - y), (1 - x, 1 - y)]

        def place(a, block):
            return o_refs[a].at[_dev_index(block)]

        def copy(a, k, block, to, src=None):
            dst = place(a, block)
            return pltpu.make_async_remote_copy(
                src_ref=dst if src is None else src, dst_ref=dst, send_sem=send_sems.at[a, k],
                recv_sem=recv_sems.at[a, k], device_id=to, device_id_type=MESH)

        mine = [pltpu.make_async_copy(x_refs[a], place(a, me), local_sems.at[a]) for a in range(n)]
        for cp in mine:
            cp.start()
        first = []
        for a in range(n):
            first.append(copy(a, 0, me, sibling, src=x_refs[a]))
            first += [copy(a, 1 + j, me, (*chip, c), src=x_refs[a]) for j, chip in enumerate(chips)]
        for cp in first:
            cp.start()
        passed = []
        for j, chip in enumerate(chips):
            for a in range(n):
                copy(a, 1 + j, (*chip, c), me).wait_recv()
                cp = copy(a, 4 + j, (*chip, c), sibling)
                cp.start()
                passed.append(cp)
        for a in range(n):
            copy(a, 0, sibling, me).wait_recv()
            for j, chip in enumerate(chips):
                copy(a, 4 + j, (*chip, 1 - c), me).wait_recv()
        for cp in first + passed:
            cp.wait_send()
        for cp in mine:
            cp.wait()

    return pl.pallas_call(
        body, name=name, out_shape=[jax.ShapeDtypeStruct((NDEV, *x.shape), x.dtype) for x in xs],
        in_specs=[hbm] * n, out_specs=[hbm] * n,
        scratch_shapes=[pltpu.SemaphoreType.DMA((n, 7)), pltpu.SemaphoreType.DMA((n, 7)), pltpu.SemaphoreType.DMA((n,))],
    )(*xs)


_HBM = pl.BlockSpec(memory_space=pltpu.HBM)
_SEM = pl.BlockSpec(memory_space=pltpu.SEMAPHORE)
_EFFECT = pltpu.SideEffectType.DATAFLOW_SIDE_EFFECTING


def _mesh_position():
    return lax.axis_index("x"), lax.axis_index("y"), lax.axis_index("c")


def _peers(me):
    return [(k, tuple(1 - v if (k >> b) & 1 else v for v, b in zip(me, (2, 1, 0)))) for k in range(1, NDEV)]


def _column_window(ref, block, width):
    return ref.at[:, pl.ds(pl.multiple_of(_dev_index(block) * width, 128), width)]


RELAY_COPIES = 3


def _relay_copies(o_ref, send_sems, recv_sems, with_arrivals):
    width = o_ref.shape[1] // NDEV
    x, y, c = me = _mesh_position()
    sibling = (x, y, 1 - c)
    x_side, y_side, diagonal = (1 - x, y), (x, 1 - y), (1 - x, 1 - y)
    first = c == 0
    via = (jnp.where(first, 1 - x, x), jnp.where(first, y, 1 - y))
    to = (jnp.where(first, x, 1 - x), jnp.where(first, 1 - y, y))

    def copy(k, block, device):
        window = _column_window(o_ref, block, width)
        return pltpu.make_async_remote_copy(src_ref=window, dst_ref=window, send_sem=send_sems.at[k],
                                            recv_sem=recv_sems.at[k], device_id=device, device_id_type=MESH)

    sent = [copy(0, (*via, c), (*to, c)), copy(1, (*x_side, c), sibling), copy(2, (*y_side, c), sibling)]
    if not with_arrivals:
        return sent
    arrivals =[copy(0, (*diagonal, c), me), copy(1, (*x_side, 1 - c), me), copy(2, (*y_side, 1 - c), me)]
    return sent, arrivals


def _relay_start(name, gathered, dep):
    def body(g_ref, dep_ref, send_sems, recv_sems, o_ref, token):
        for cp in _relay_copies(g_ref, send_sems, recv_sems, with_arrivals=False):
            cp.start()
        token[...] = jnp.zeros_like(token)

    sems = pltpu.SemaphoreType.DMA((RELAY_COPIES,))
    res = pl.pallas_call(
        body, name=name,
        out_shape=(sems, sems, pltpu.HBM(gathered.shape, gathered.dtype), jax.ShapeDtypeStruct((8, 128), F32)),
        in_specs=[_HBM, pl.BlockSpec(memory_space=pl.ANY)],
        out_specs=(_SEM, _SEM, _HBM, pl.BlockSpec(memory_space=pltpu.VMEM)),
        input_output_aliases={0: 2},
        compiler_params=pltpu.CompilerParams(has_side_effects=_EFFECT),
    )(gathered, dep)
    return res[:-1], res[-1]


def _relay_wait(name, handles, after):
    send_sems, recv_sems, gathered = handles

    def body(g_ref, s_sems, r_sems, after_ref, o_ref):
        sent, arrivals = _relay_copies(g_ref, s_sems, r_sems, with_arrivals=True)
        for cp, arrival in zip(sent, arrivals):
            cp.wait_send()
            arrival.wait_recv()

    return pl.pallas_call(
        body, name=name, out_shape=pltpu.HBM(gathered.shape, gathered.dtype),
        in_specs=[_HBM, _SEM, _SEM, pl.BlockSpec(memory_space=pl.ANY)], out_specs=_HBM, input_output_aliases={0: 0},
        compiler_params=pltpu.CompilerParams(has_side_effects=_EFFECT),
    )(gathered, send_sems, recv_sems, after)


def _columns_last(name, gathered):
    width = gathered.shape[1] // NDEV
    hbm = pl.BlockSpec(memory_space=pl.ANY)

    def body(g_ref, o_ref, send_sem, recv_sem):
        x, y, c = _mesh_position()

        def copy(core, device):
            window = _column_window(o_ref, (1 - x, 1 - y, core), width)
            return pltpu.make_async_remote_copy(src_ref=window, dst_ref=window, send_sem=send_sem, recv_sem=recv_sem,
                                                device_id=device, device_id_type=MESH)

        onward = copy(c, (x, y, 1 - c))
        onward.start()
        copy(1 - c, (x, y, c)).wait_recv()
        onward.wait_send()

    return pl.pallas_call(
        body, name=name, out_shape=jax.ShapeDtypeStruct(gathered.shape, gathered.dtype), in_specs=[hbm], out_specs=hbm,
        input_output_aliases={0: 0}, scratch_shapes=[pltpu.SemaphoreType.DMA, pltpu.SemaphoreType.DMA],
    )(gathered)


NCHIP = NDEV // 2
EXCHANGE_COPIES = {"columns": NCHIP - 1, "gather": NDEV - 1, "scatter": NDEV - 1, "pair": NCHIP, "chips": NCHIP - 1}


def _landing_zones(name, xs, mode):
    x_, y_, c_ = _mesh_position()
    mine = (2 * x_ + y_ if mode == "chips" else _dev_index((x_, y_, c_))).astype(jnp.int32).reshape(1)
    lands = []
    for a, x in enumerate(xs):
        rows, cols = x.shape[-2:]
        if mode == "pair":
            lands.append(lax.empty((NCHIP, rows, cols), x.dtype))
            continue
        tr = 256 if rows % 256 == 0 else rows

        def body(me_ref, x_ref, o_ref):
            o_ref[...] = x_ref[...]

        if mode in ("gather", "columns"):
            in_spec = pl.BlockSpec((tr, cols), lambda i, me_ref: (i, 0))
        else:
            in_spec = pl.BlockSpec((None, tr, cols), lambda i, me_ref: (me_ref[0], i, 0))
        if mode == "columns":
            out_shape, out_spec = (rows, NDEV * cols), pl.BlockSpec((tr, cols), lambda i, me_ref: (i, me_ref[0]))
        else:
            out_shape = (NCHIP if mode == "chips" else NDEV, rows, cols)
            out_spec = pl.BlockSpec((None, tr, cols), lambda i, me_ref: (me_ref[0], i, 0))
        lands.append(pl.pallas_call(
            body, name=f"{name}_{a}", out_shape=jax.ShapeDtypeStruct(out_shape, x.dtype),
            grid_spec=pltpu.PrefetchScalarGridSpec(num_scalar_prefetch=1, grid=(rows // tr,), in_specs=[in_spec],
                                                   out_specs=out_spec),
            compiler_params=_params(("arbitrary",)),
        )(mine, x))
    return lands


def _exchange_copies(x_refs, land_refs, send_sems, recv_sems, mode):
    x_, y_, c_ = me = _mesh_position()
    per_array = EXCHANGE_COPIES[mode]
    out = []

    def add(a, k, src, dst, peer):
        sem = a * per_array + k
        out.append(pltpu.make_async_remote_copy(src_ref=src, dst_ref=dst, send_sem=send_sems.at[sem], recv_sem=recv_sems.at[sem],
                                                device_id=peer, device_id_type=MESH))

    for a, (x_ref, land_ref) in enumerate(zip(x_refs, land_refs)):
        if mode == "columns":
            for k, peer in enumerate([(x_, y_, 1 - c_), (1 - x_, y_, c_), (x_, 1 - y_, c_)]):
                add(a, k, x_ref, _column_window(land_ref, me, x_ref.shape[1]), peer)
        elif mode in ("gather", "scatter"):
            for k, peer in _peers(me):
                add(a, k - 1, x_ref.at[_dev_index(peer)] if mode == "scatter" else x_ref, land_ref.at[_dev_index(me)], peer)
        elif mode == "pair":
            for chip in range(NCHIP):
                add(a, chip, x_ref.at[2 * chip + 1 - c_], land_ref.at[chip], (x_, y_, 1 - c_))
        else:
            for k in range(1, NCHIP):
                px, py = (1 - x_ if k & 2 else x_), (1 - y_ if k & 1 else y_)
                add(a, k - 1, x_ref.at[2 * px + py], land_ref.at[2 * x_ + y_], (px, py, c_))
    return out


def _exchange_start(name, xs, lands, mode, dep, carry=False):
    n = len(xs)

    def body(*refs):
        x_refs, land_refs = refs[:n], refs[n:2 * n]
        send_sems, recv_sems = refs[2 * n + 1], refs[2 * n + 2]
        for cp in _exchange_copies(x_refs, land_refs, send_sems, recv_sems, mode):
            cp.start()
        if not carry:
            refs[-1][...] = jnp.zeros_like(refs[-1])

    sems = pltpu.SemaphoreType.DMA((n * EXCHANGE_COPIES[mode],))
    moved = [pltpu.with_memory_space_constraint(a, pltpu.HBM) for a in (*xs, *lands, *([dep] if carry else []))]
    res = pl.pallas_call(
        body, name=name,
        out_shape=(sems, sems, *[pltpu.HBM(a.shape, a.dtype) for a in moved],
                   *([] if carry else [jax.ShapeDtypeStruct((8, 128), F32)])),
        in_specs=[_HBM] * len(moved) + ([] if carry else [pl.BlockSpec(memory_space=pl.ANY)]),
        out_specs=(_SEM, _SEM, *[_HBM] * len(moved), *([] if carry else [pl.BlockSpec(memory_space=pltpu.VMEM)])),
        input_output_aliases={i: 2 + i for i in range(len(moved))},
        compiler_params=pltpu.CompilerParams(has_side_effects=_EFFECT),
    )(*moved, *([] if carry else [dep]))
    return res[:-1], res[-1]


def _exchange_wait(name, handles, mode, after, with_sources=False):
    send_sems, recv_sems = handles[0], handles[1]
    bufs = handles[2:]
    n = len(bufs) // 2
    afters = list(after) if isinstance(after, (list, tuple)) else [after]

    def body(*refs):
        x_refs, land_refs = refs[:n], refs[n:2 * n]
        s_sems, r_sems = refs[2 * n], refs[2 * n + 1]
        for cp in _exchange_copies(x_refs, land_refs, s_sems, r_sems, mode):
            cp.wait_send()
            cp.wait_recv()

    res = pl.pallas_call(
        body, name=name, out_shape=tuple(pltpu.HBM(a.shape, a.dtype) for a in bufs),
        in_specs=[_HBM] * (2 * n) + [_SEM, _SEM] + [pl.BlockSpec(memory_space=pl.ANY)] * len(afters),
        out_specs=tuple([_HBM] * (2 * n)), input_output_aliases={i: i for i in range(2 * n)},
        compiler_params=pltpu.CompilerParams(has_side_effects=_EFFECT),
    )(*bufs, send_sems, recv_sems, *afters)
    return (res[n:], res[:n]) if with_sources else res[n:]


def _pair_sum(name, x, from_sibling):
    _, rows, cols = x.shape
    tr = 256 if rows % 256 == 0 else rows
    core = lax.axis_index("c").astype(jnp.int32).reshape(1)

    def body(c_ref, x_ref, s_ref, o_ref):
        o_ref[...] = (x_ref[...].astype(F32) + s_ref[...].astype(F32)).astype(o_ref.dtype)

    return pl.pallas_call(
        body, name=name, out_shape=jax.ShapeDtypeStruct((NCHIP, rows, cols), x.dtype),
        grid_spec=pltpu.PrefetchScalarGridSpec(
            num_scalar_prefetch=1, grid=(NCHIP, rows // tr),
            in_specs=[pl.BlockSpec((None, tr, cols), lambda j, i, c_ref: (2 * j + c_ref[0], i, 0)),
                      pl.BlockSpec((None, tr, cols), lambda j, i, c_ref: (j, i, 0))],
            out_specs=pl.BlockSpec((None, tr, cols), lambda j, i, c_ref: (j, i, 0))),
        compiler_params=_params(("parallel", "parallel")),
    )(core, x, from_sibling)


def kernel(x, c, positions, norm_w, mod_w, mod_b, attn_w_in, attn_w_out, ssd_w_in, ssd_conv_w, ssd_conv_b, ssd_dt_bias, ssd_a_log, ssd_d, ssd_norm_w, ssd_w_out, final_norm_w, loss_target, m_norm_w, m_mod_w, m_mod_b, m_attn_w_in, m_attn_w_out, m_ssd_w_in, m_ssd_conv_w, m_ssd_conv_b, m_ssd_dt_bias, m_ssd_a_log, m_ssd_d, m_ssd_norm_w, m_ssd_w_out, m_final_norm_w, v_norm_w, v_mod_w, v_mod_b, v_attn_w_in, v_attn_w_out, v_ssd_w_in, v_ssd_conv_w, v_ssd_conv_b, v_ssd_dt_bias, v_ssd_a_log, v_ssd_d, v_ssd_norm_w, v_ssd_w_out, v_final_norm_w):
    s_len, dm = x.shape[1], x.shape[2]
    me = 4 * lax.axis_index("x") + 2 * lax.axis_index("y") + lax.axis_index("c")
    x0 = x.reshape(s_len, dm)
    tgt = loss_target.reshape(s_len, dm)
    aw = 3 * 512
    si = 2 * dm
    sxbc = 2 * si
    n_ssd_in = ssd_w_in.shape[2] * NDEV

    (c_all,) = _all_gather("gather_c", [c])
    c_all = c_all.reshape(NDEV, dm)
    part = _mod_part(c_all, mod_w)
    (part_all,) = _all_gather("gather_mod", [part])
    mod_nb = jnp.stack([lax.dynamic_index_in_dim(part_all, i * NDEV + me, axis=1, keepdims=False).reshape(3 * dm)
                        for i in range(2)])

    wcol = attn_w_in.shape[2]
    ai_shard = [attn_w_in[0].astype(BF16)]
    ai_handles, ai_token = _exchange_start("attn_w_in_start", ai_shard, _landing_zones("attn_w_in_place", ai_shard, "columns"),
                                           "columns", part_all)
    inv_freq = ROPE_THETA ** (-jnp.arange(0, ROT_DIM, 2, dtype=F32) / ROT_DIM)
    per_head = jnp.concatenate([inv_freq, inv_freq, jnp.zeros(HEAD_DIM - ROT_DIM, F32)])
    inv_row = jnp.tile(per_head, 128 // HEAD_DIM).reshape(1, 128) + ai_token[0:1]
    tabs = _rope_tables(positions.reshape(s_len, 1), inv_row)
    ssd_small = _pack_ssd_small(ssd_conv_w[0], ssd_conv_b, ssd_norm_w)
    ao_shard = [attn_w_out[0].astype(BF16)]
    late_shards = [ssd_w_in[0].T.astype(BF16), ssd_w_out[0].astype(BF16), ssd_small]
    ao_lands = _landing_zones("w_out_place", ao_shard, "gather")
    late_lands = _landing_zones("weights_place", late_shards, "gather")
    (w_ai,) = _exchange_wait("attn_w_in_wait", ai_handles, "columns", [*tabs, *ao_lands, *late_lands])
    relay_handles, relay_token = _relay_start("attn_w_in_relay_start", w_ai, tabs[0])
    (shift0, scale0, gate0, nw0), (shift1, scale1, gate1, nw1) = _mod_finish(mod_nb, mod_b, norm_w, [relay_token])
    shift, scale, gate, nw = [shift0, shift1], [scale0, scale1], [gate0, gate1], [nw0, nw1]
    hn0 = _norm_mod_fwd("norm0", x0, nw[0], scale[0], shift[0])
    w_ai = _columns_last("gather_attn_w_in_last", _relay_wait("attn_w_in_relay_wait", relay_handles, hn0))

    ao_handles, w_ai = _exchange_start("w_out_start", ao_shard, ao_lands, "gather", w_ai, carry=True)
    w_handles, w_ai = _exchange_start("weights_start", late_shards, late_lands, "gather", w_ai, carry=True)

    qk = _matmul("proj_qk", hn0, w_ai, "nn", F32, MM_T, MM_T, dm, epilogue=_rot_fwd, mrows=tabs, n_out=2 * aw)
    v = _matmul("proj_vz", hn0, w_ai, "nn", F32, MM_T, MM_T, dm, b_noff=2 * aw, n_out=2 * aw)
    z0 = (v, 1)
    att = [_attn_fwd(g, qk, v) for g in range(3)]
    os_, lses = [a[0] for a in att], [a[1] for a in att]
    (g_ao,) = _exchange_wait("w_out_wait", ao_handles, "gather", lses[2])
    a0, y0, x1 = _attn_out(os_, lses, z0, x0, gate[0], g_ao.reshape(aw, dm))

    hn1 = _norm_mod_fwd("norm1", x1, nw[1], scale[1], shift[1])
    g_si, g_so, g_small = _exchange_wait("weights_wait", w_handles, "gather", hn1)
    w_ao = g_ao.reshape(aw, dm)
    w_si_t = g_si.reshape(n_ssd_in, dm)
    w_so = g_so.reshape(si, dm)
    conv_w = g_small[:, 0:CONV_WIDTH, :].transpose(1, 0, 2).reshape(CONV_WIDTH, sxbc)
    conv_b = g_small[:, 5, :].reshape(1, sxbc)
    snw = g_small[:, 6, 0:si // NDEV].reshape(1, si)
    ndt = 2 * SSD_HEADS
    z1 = _matmul("ssd_proj_z", hn1, w_si_t, "nt", F32, MM_T, MM_T, dm, n_out=si)
    xpre = _matmul("ssd_proj_xbc", hn1, w_si_t, "nt", F32, MM_T, MM_T, dm, b_noff=si, n_out=sxbc)
    dt_raw = _matmul("ssd_proj_dt", hn1, w_si_t, "nt", F32, MM_T, ndt, dm, b_noff=si + sxbc, n_out=ndt)
    xbc = _conv_fwd(xpre, conv_w, conv_b)
    widen = lambda a: jnp.pad(a, ((0, 0), (0, SSD_DTW - ndt)))
    dt_raw = widen(dt_raw)
    dt_bias = widen(ssd_dt_bias.reshape(1, ndt))
    alog = widen(ssd_a_log.reshape(1, ndt))
    dt = _softplus_fwd(dt_raw, dt_bias)
    y_f, st_f = _ssd_fwd(xbc, dt, alog, 0)
    y_fb, st_b = _ssd_fwd(xbc, dt, alog, 1, prior=y_f)
    d_e = jnp.repeat(ssd_d.reshape(SSD_HEADS), HEAD_DIM).reshape(1, si)

    fnw = final_norm_w.reshape(1, dm)
    u, dx2, dy1, g_fnw, dgate1, loss_part = _ssd_tail_loss(y_fb, xbc, z1, d_e, snw, w_so, x1, tgt, gate[1], fnw)
    gw_so = _matmul("ssd_out_dw", u, dy1, "tn", BF16, MM_T, MM_T, MM_T)
    dys, dz1, g_snw, g_d = _gate_norm_bwd(dy1, w_so, y_fb, xbc, z1, d_e, snw)
    dxbc_f, ddt_f, dalog_f = _ssd_bwd(xbc, dt, alog, st_f, dys, d_e, 0)
    dxbc, ddt_b, dalog_b = _ssd_bwd(xbc, dt, alog, st_b, dys, d_e, 1, prior=dxbc_f)
    dpre, g_cw, g_cb = _conv_bwd(xpre, dxbc, conv_w, conv_b)
    ddt_raw, g_dtb = _softplus_bwd(ddt_f, ddt_b, dt_raw, dt_bias)
    ddt_raw = ddt_raw[:, :ndt]
    dhn1 = [_matmul("ssd_proj_z_dx", dz1, w_si_t, "nn", F32, MM_T, MM_T, MM_T),
            _matmul("ssd_proj_xbc_dx", dpre, w_si_t, "nn", F32, MM_T, MM_T, MM_T, b_koff=si)]
    gw_si_t = _matmul("ssd_proj_z_dw", dz1, hn1, "tn", BF16, MM_T, MM_T, MM_T, dest=(n_ssd_in, 0, None))
    gw_si_t = _matmul("ssd_proj_xbc_dw", dpre, hn1, "tn", BF16, MM_T, MM_T, MM_T, dest=(n_ssd_in, si, gw_si_t))
    gw_si_t = _matmul("ssd_proj_dt_dw", ddt_raw, hn1, "tn", BF16, ndt, MM_T, MM_T, dest=(n_ssd_in, si + sxbc, gw_si_t))

    l1_grads = [gw_so.reshape(NDEV, si // NDEV, dm), gw_si_t.reshape(NDEV, n_ssd_in // NDEV, dm),
                _pack_ssd_small_blocks(g_cw, g_cb, g_snw)]
    l1_handles, l1_token = _exchange_start("l1_grads_start", l1_grads, _landing_zones("l1_grads_place", l1_grads, "scatter"),
                                           "scatter", dhn1[1])
    dx1, dy0, g_nw1, dsc1, dsh1, dgate0 = _norm_mod_bwd(
        "ssd_proj_dt_dx_norm1_bwd", (ddt_raw, w_si_t, "nn", ndt, dict(b_koff=si + sxbc)), x1, dhn1, dx2,
        nw[1], scale[1], shift[1], prev=(y0, gate[0] + l1_token[0:1, 0:1]))

    gw_ao = _matmul("attn_out_dw", a0, dy0, "tn", BF16, aw // 2, MM_T, MM_T)
    dos, dls, dz0 = _mix_bwd(dy0, w_ao, os_, lses, z0)
    datt = [_attn_bwd(g, qk, v, os_[g], lses[g], dos[g], dls[g]) for g in range(3)]
    dqkv = _rot_pack_bwd([t[0] for t in datt], [t[1] for t in datt], [t[2] for t in datt], tabs)
    gw_ai = _matmul("proj_qkv_dw", hn0, dqkv, "tn", BF16, MM_T, wcol, MM_T, out_blocks=3 * aw // wcol, dest=(NDEV, 0, None))
    gw_ai = _matmul("proj_z_dw", hn0, dz0, "tn", BF16, MM_T, wcol, MM_T, out_blocks=aw // wcol,
                    dest=(NDEV, 3 * aw // wcol, gw_ai))
    after_start = lambda acc, t: acc + t
    zero_row = lambda token: jnp.tile(token[0:1], (1, dm // 128))
    l0_grads = [gw_ai, gw_ao.reshape(NDEV, aw // NDEV, dm)]
    pair_handles, pair_token = _exchange_start("l0_pair_start", l0_grads, _landing_zones("l0_pair_place", l0_grads, "pair"),
                                               "pair", dqkv)
    dhn0_z = _matmul("proj_z_dx", dz0, w_ai, "nt", F32, MM_T, MM_T, aw, b_koff=3 * aw, n_out=dm, epilogue=after_start,
                     ncols=(zero_row(pair_token),))
    from_sibling, l0_grads = _exchange_wait("l0_pair_wait", pair_handles, "pair", dhn0_z, with_sources=True)
    chip_sums = [_pair_sum(f"l0_pair_sum_{a}", g, s) for a, (g, s) in enumerate(zip(l0_grads, from_sibling))]
    l0_handles, l0_token = _exchange_start("l0_grads_start", chip_sums, _landing_zones("l0_grads_place", chip_sums, "chips"),
                                           "chips", dhn0_z)
    dx0, g_nw0, dsc0, dsh0 = _norm_mod_bwd(
        "proj_qkv_dx_norm0_bwd", (dqkv, w_ai, "nt", aw, dict(n_out=dm)), x0, [dhn0_z], dx1,
        nw[0], scale[0], shift[0] + zero_row(l0_token))

    small_g = [_pack_small([dsh0, dsc0, dgate0, dsh1, dsc1, dgate1, g_nw0, g_nw1, g_fnw], g_dtb, [dalog_f, dalog_b], g_d, loss_part)]
    sm_handles, sm_token = _exchange_start("small_grads_start", small_g, _landing_zones("small_grads_place", small_g, "gather"),
                                           "gather", dx0)

    whole = (slice(None), slice(None))
    r_so, r_si, r_small = _exchange_wait("l1_grads_wait", l1_handles, "scatter", sm_token)
    si_out = [o.T for o in _adamw("adamw_ssd_w_in", ssd_w_in[0].T, r_si, m_ssd_w_in[0].T, v_ssd_w_in[0].T, n_ssd_in // NDEV, 128)]
    so_out = _adamw("adamw_ssd_w_out", ssd_w_out[0], r_so, m_ssd_w_out[0], v_ssd_w_out[0], 256)
    cw_cols = ssd_conv_w.shape[2]
    cw_out, cb_out, snw_out = _adamw_windows(
        "adamw_ssd_small", r_small,
        [(ssd_conv_w, m_ssd_conv_w, v_ssd_conv_w), (ssd_conv_b, m_ssd_conv_b, v_ssd_conv_b),
         (ssd_norm_w, m_ssd_norm_w, v_ssd_norm_w)],
        [(0, slice(0, CONV_WIDTH), slice(0, cw_cols), (0, slice(None), slice(None))),
         (1, slice(5, 6), slice(0, cw_cols), whole), (2, slice(6, 7), slice(0, si // NDEV), whole)])
    r_ai, r_ao = _exchange_wait("l0_grads_wait", l0_handles, "chips", so_out[0])
    ai_out = _adamw("adamw_attn_w_in", attn_w_in[0], r_ai, m_attn_w_in[0], v_attn_w_in[0], 128)
    ao_out = _adamw("adamw_attn_w_out", attn_w_out[0], r_ao, m_attn_w_out[0], v_attn_w_out[0], 192)

    (small_all,) = _exchange_wait("small_grads_wait", sm_handles, "gather", ai_out[0])
    full = slice(0, PACK_COLS)
    nhd = SSD_HEADS
    windows = [(0, slice(3 * i + k, 3 * i + k + 1), full, (slice(i, i + 1), slice(k * dm, (k + 1) * dm)))
               for i in range(2) for k in range(3)]
    windows += [(1, slice(6 + i, 7 + i), full, (slice(i, i + 1), slice(None))) for i in range(2)]
    windows += [(2, slice(8, 9), full, whole)]
    windows += [(3 + q, slice(9, 10), slice(2 * nhd * q + nhd * j, 2 * nhd * q + nhd * (j + 1)), (0, slice(j, j + 1), slice(None)))
                for q in range(2) for j in range(2)]
    windows += [(5, slice(9, 10), slice(4 * nhd, 5 * nhd), whole)]
    as_row = lambda a: a.reshape(1, dm)
    mb_out, nw_out, fnw_out, dtb_out, alog_out, d_out, loss = _adamw_windows(
        "adamw_small", small_all,
        [(mod_b, m_mod_b, v_mod_b), (norm_w, m_norm_w, v_norm_w), (fnw, as_row(m_final_norm_w), as_row(v_final_norm_w)),
         (ssd_dt_bias, m_ssd_dt_bias, v_ssd_dt_bias), (ssd_a_log, m_ssd_a_log, v_ssd_a_log), (ssd_d, m_ssd_d, v_ssd_d)],
        windows, extra=(slice(9, 10), slice(256, 257)))
    loss = loss.reshape(())

    ncol = mod_w.shape[2]
    dmod_all = small_all[:, 0:6, :].reshape(NDEV, 2, 3 * dm)
    dmod_sh = lax.dynamic_slice_in_dim(dmod_all, me * ncol, ncol, axis=2).transpose(1, 0, 2)
    g_modw = _mod_grad(c_all, dmod_sh).reshape(1, 2 * dm, ncol)
    modw_out = _adamw("adamw_mod_w", mod_w.reshape(2 * dm, ncol), g_modw, m_mod_w.reshape(2 * dm, ncol),
                      v_mod_w.reshape(2 * dm, ncol), 256)

    per_kind = []
    for k in range(4):
        per_kind.append([
            nw_out[k], modw_out[k].reshape(mod_w.shape), mb_out[k], ai_out[k][None], ao_out[k][None], si_out[k][None],
            cw_out[k], cb_out[k], dtb_out[k], alog_out[k], d_out[k], snw_out[k], so_out[k][None], fnw_out[k].reshape(dm)])
    return (loss, dx0.reshape(x.shape), *per_kind[0], *per_kind[1], *per_kind[2], *per_kind[3])


def _pack_ssd_small_blocks(g_cw, g_cb, g_nw):
    nper = g_cw.shape[1] // NDEV
    nwper = g_nw.shape[1] // NDEV

    def body(cw_ref, cb_ref, nw_ref, o_ref):
        o_ref[...] = jnp.zeros_like(o_ref)
        for d in range(NDEV):
            o_ref[d, 0:5, :] = cw_ref[:, d * nper:(d + 1) * nper]
            o_ref[d, 5:6, :] = cb_ref[:, d * nper:(d + 1) * nper]
            o_ref[d, 6:7, 0:nwper] = nw_ref[:, d * nwper:(d + 1) * nwper]

    return pl.pallas_call(body, name="pack_ssd_small_grads", out_shape=jax.ShapeDtypeStruct((NDEV, 8, nper), F32))(g_cw, g_cb, g_nw)
```

```python
import functools
import math

import jax
import jax.numpy as jnp
from jax import lax
from jax.experimental import pallas as pl
from jax.experimental.pallas import tpu as pltpu

F32 = jnp.float32
BF16 = jnp.bfloat16
HI = lax.Precision.HIGHEST
MESH = pl.DeviceIdType.MESH
NDEV = 8

NORM_EPS = 1e-6
ROPE_THETA = 500000.0
ROT_DIM = 16
HEAD_DIM = 64
DILATIONS = (1, 4, 16)
BAND = 64
NEG_BIG = -1e30
CHUNK = 128
SSD_HEADS = 32
SSD_GROUPS = 8
CONV_WIDTH = 5

ADAM_LR = 0.001
ADAM_B1 = 0.9
ADAM_B2 = 0.999
ADAM_EPS = 1e-08
ADAM_WD = 0.01
ADAM_STEP = 10

VMEM_BIG = 56 * 1024 * 1024
MM_T = 1024


def _params(sem=None, vmem=None):
    kw = {}
    if sem is not None:
        kw["dimension_semantics"] = sem
    if vmem is not None:
        kw["vmem_limit_bytes"] = vmem
    return pltpu.CompilerParams(**kw)


def _dg(a, b, ca, cb, prec=None):
    return lax.dot_general(a, b, (((ca,), (cb,)), ((), ())), preferred_element_type=F32, precision=prec)


def _nn(a, b):
    return _dg(a.astype(BF16), b.astype(BF16), 1, 0)


def _nt(a, b):
    return _dg(a.astype(BF16), b.astype(BF16), 1, 1)


def _tn(a, b):
    return _dg(a.astype(BF16), b.astype(BF16), 0, 0)


def _hnn(a, b):
    return _dg(a, b, 1, 0, HI)


@jax.custom_vjp
def _bnn(a, b):
    return _nn(a, b)


_bnn.defvjp(lambda a, b: (_nn(a, b), (a, b)), lambda r, g: (_nt(g, r[1]), _tn(r[0], g)))


@jax.custom_vjp
def _bnt(a, b):
    return _nt(a, b)


_bnt.defvjp(lambda a, b: (_nt(a, b), (a, b)), lambda r, g: (_nn(g, r[1]), _tn(g, r[0])))


@jax.custom_vjp
def _btn(a, b):
    return _tn(a, b)


_btn.defvjp(lambda a, b: (_tn(a, b), (a, b)), lambda r, g: (_nt(r[1], g), _nn(r[0], g)))


def _silu(x):
    return x * jax.nn.sigmoid(x)


def _b_spec(b, mode, tn, tk, no, ko, jk):
    if mode == "nt":
        return pl.BlockSpec((tn, tk), lambda *g: (jk(*g)[0] + no, jk(*g)[1] + ko))
    return pl.BlockSpec((tk, tn), lambda *g: (jk(*g)[1] + ko, jk(*g)[0] + no))


def _matmul(name, a, b, mode, out_dtype, tm, tn, tk, *, epilogue=None, tiled=(), mrows=(), ncols=(),
            b_noff=0, b_koff=0, n_out=None, out_blocks=None, dest=None):
    if mode == "tn":
        K, M = a.shape
    else:
        M, K = a.shape
    N = n_out if n_out is not None else (b.shape[0] if mode == "nt" else b.shape[1])
    tm, tn, tk = min(tm, M), min(tn, N), min(tk, K)
    assert M % tm == 0 and N % tn == 0 and K % tk == 0, (name, M, N, K, tm, tn, tk)
    assert b_noff % tn == 0 and b_koff % tk == 0
    no, ko = b_noff // tn, b_koff // tk
    nk = K // tk
    if mode == "tn":
        a_spec = pl.BlockSpec((tk, tm), lambda i, j, k: (k, i))
    else:
        a_spec = pl.BlockSpec((tm, tk), lambda i, j, k: (i, k))
    specs = [a_spec, _b_spec(b, mode, tn, tk, no, ko, lambda i, j, k: (j, k))]
    specs += [pl.BlockSpec((tm, tn), lambda i, j, k: (i, j)) for _ in tiled]
    specs += [pl.BlockSpec((tm, r.shape[1]), lambda i, j, k: (i, 0)) for r in mrows]
    specs += [pl.BlockSpec((1, tn), lambda i, j, k: (0, j)) for _ in ncols]
    total, off, earlier = dest if dest is not None else (None, 0, None)
    if out_blocks is None:
        assert off % tm == 0
        mo = off // tm
        out_shape = jax.ShapeDtypeStruct((M if total is None else total, N), out_dtype)
        out_spec = pl.BlockSpec((tm, tn), lambda i, j, k: (i + mo, j))
    else:
        nper = N // out_blocks
        assert nper % tn == 0
        jb = nper // tn
        out_shape = jax.ShapeDtypeStruct((out_blocks if total is None else total, M, nper), out_dtype)
        out_spec = pl.BlockSpec((None, tm, tn), lambda i, j, k: (j // jb + off, i, j % jb))
    if earlier is not None:
        assert earlier.shape == out_shape.shape and earlier.dtype == out_shape.dtype
    ne = len(tiled) + len(mrows) + len(ncols)
    dot = {"nn": _nn, "nt": _nt, "tn": _tn}[mode]

    def body(a_ref, b_ref, *rest):
        extras, o_ref = rest[:ne], rest[ne]

        def finish(acc):
            if epilogue is not None:
                acc = epilogue(acc, *[e[...] for e in extras])
            o_ref[...] = acc.astype(o_ref.dtype)

        if nk == 1:
            finish(dot(a_ref[...], b_ref[...]))
        else:
            acc_ref = rest[ne + 1]
            k = pl.program_id(2)

            @pl.when(k == 0)
            def _():
                acc_ref[...] = jnp.zeros_like(acc_ref)

            acc_ref[...] += dot(a_ref[...], b_ref[...])

            @pl.when(k == nk - 1)
            def _():
                finish(acc_ref[...])

    args = [a, b, *tiled, *mrows, *ncols]
    aliases = {}
    if earlier is not None:
        specs.append(pl.BlockSpec(memory_space=pl.ANY))
        aliases = {len(args): 0}
        args.append(earlier)

    def body_with_dest(*refs):
        body(*refs[:2 + ne], *refs[2 + ne + (earlier is not None):])

    return pl.pallas_call(
        body_with_dest, name=name, out_shape=out_shape, grid=(M // tm, N // tn, nk),
        in_specs=specs, out_specs=out_spec, input_output_aliases=aliases,
        scratch_shapes=[] if nk == 1 else [pltpu.VMEM((tm, tn), F32)],
        compiler_params=_params(("parallel", "parallel", "arbitrary"), VMEM_BIG),
    )(*args)


def _matmul_rows(name, a, b, mode, tm, tk, fn, rows, consts, outs, accs, *, n_out=None, b_noff=0, b_koff=0):
    rl = [(t, t.shape[1], 0) if not isinstance(t, tuple) else t for t in rows]
    make_a = a if callable(a) else None
    M, K = (rl[0][0].shape[0], b.shape[1 if mode == "nt" else 0]) if make_a else a.shape
    N = n_out if n_out is not None else (b.shape[0] if mode == "nt" else b.shape[1])
    tm, tk = min(tm, M), min(tk, K)
    assert M % tm == 0 and K % tk == 0 and b_koff % tk == 0 and b_noff % N == 0, (name, M, N, K)
    no, ko, nk = b_noff // N, b_koff // tk, K // tk
    assert make_a is None or nk == 1
    nr, nc, no_, na = len(rl), len(consts), len(outs), len(accs)
    dot = _nt if mode == "nt" else _nn

    def body(*refs):
        a_ref, b_ref, rest = (None, refs[0], refs[1:]) if make_a else (refs[0], refs[1], refs[2:])
        r_refs, c_refs = rest[:nr], rest[nr:nr + nc]
        o_refs, acc_refs = rest[nr + nc:nr + nc + no_], rest[nr + nc + no_:nr + nc + no_ + na]
        i, k = pl.program_id(0), pl.program_id(1)

        def finish(prod, *made):
            res_o, res_a = fn(prod, *made, *[r[...] for r in r_refs], *[c[...] for c in c_refs])
            for r, v in zip(o_refs, res_o, strict=True):
                r[...] = v.astype(r.dtype)
            if acc_refs:
                @pl.when(i == 0)
                def _():
                    for r in acc_refs:
                        r[...] = jnp.zeros_like(r)

                for r, v in zip(acc_refs, res_a, strict=True):
                    r[...] += v

        if make_a:
            left = make_a(*[r[...] for r in r_refs], *[c[...] for c in c_refs])
            finish(dot(left, b_ref[...]), left)
        elif nk == 1:
            finish(dot(a_ref[...], b_ref[...]))
        else:
            prod_ref = rest[-1]

            @pl.when(k == 0)
            def _():
                prod_ref[...] = jnp.zeros_like(prod_ref)

            prod_ref[...] += dot(a_ref[...], b_ref[...])

            @pl.when(k == nk - 1)
            def _():
                finish(prod_ref[...])

    b_spec = _b_spec(b, mode, N, tk, no, ko, lambda i, k: (0, k))
    in_specs = ([] if make_a else [pl.BlockSpec((tm, tk), lambda i, k: (i, k))]) + [b_spec]
    in_specs += [pl.BlockSpec((tm, w), functools.partial(lambda i, k, cb: (i, cb), cb=cb)) for (_, w, cb) in rl]
    in_specs += [pl.BlockSpec(c.shape, lambda i, k: (0, 0)) for c in consts]
    out_specs = [pl.BlockSpec((tm, c), lambda i, k: (i, 0)) for (c, _) in outs]
    out_specs += [pl.BlockSpec(shp, lambda i, k: (0, 0)) for shp in accs]
    out_shape = [jax.ShapeDtypeStruct((M, c), dt) for (c, dt) in outs] + [jax.ShapeDtypeStruct(shp, F32) for shp in accs]
    res = pl.pallas_call(
        body, name=name, out_shape=out_shape, grid=(M // tm, nk), in_specs=in_specs, out_specs=out_specs,
        scratch_shapes=[] if nk == 1 else [pltpu.VMEM((tm, N), F32)],
        compiler_params=_params(("arbitrary" if accs else "parallel", "arbitrary"), VMEM_BIG),
    )(*([] if make_a else [a]), b, *[t[0] for t in rl], *consts)
    return res[:no_], res[no_:]


def _rowwise(name, fn, tiled, consts, outs, accs, ts):
    tl = [(t, t.shape[1], 0) if not isinstance(t, tuple) else t for t in tiled]
    s_len = tl[0][0].shape[0]
    assert s_len % ts == 0
    nt_, nc_, no_ = len(tl), len(consts), len(outs)

    def body(*refs):
        t_refs, c_refs = refs[:nt_], refs[nt_:nt_ + nc_]
        o_refs, a_refs = refs[nt_ + nc_:nt_ + nc_ + no_], refs[nt_ + nc_ + no_:]
        res_o, res_a = fn(*[r[...] for r in t_refs], *[r[...] for r in c_refs])
        for r, v in zip(o_refs, res_o, strict=True):
            r[...] = v.astype(r.dtype)
        if a_refs:
            @pl.when(pl.program_id(0) == 0)
            def _():
                for r in a_refs:
                    r[...] = jnp.zeros_like(r)

            for r, v in zip(a_refs, res_a, strict=True):
                r[...] += v

    in_specs = [pl.BlockSpec((ts, w), functools.partial(lambda i, cb: (i, cb), cb=cb)) for (_, w, cb) in tl]
    in_specs += [pl.BlockSpec(c.shape, lambda i: (0, 0)) for c in consts]
    out_specs = [pl.BlockSpec((ts, c), lambda i: (i, 0)) for (c, _) in outs]
    out_specs += [pl.BlockSpec(shp, lambda i: (0, 0)) for shp in accs]
    out_shape = [jax.ShapeDtypeStruct((s_len, c), dt) for (c, dt) in outs]
    out_shape += [jax.ShapeDtypeStruct(shp, F32) for shp in accs]
    res = pl.pallas_call(
        body, name=name, out_shape=out_shape, grid=(s_len // ts,), in_specs=in_specs, out_specs=out_specs,
        compiler_params=_params(("arbitrary",) if accs else ("parallel",), VMEM_BIG),
    )(*[t[0] for t in tl], *consts)
    return res[:no_], res[no_:]


def _norm_mod_fn(x, nw, sc, sh):
    r = lax.rsqrt(jnp.mean(x * x, axis=-1, keepdims=True) + NORM_EPS)
    return (x * r * nw) * (1.0 + sc) + sh


def _norm_mod_fwd(name, x, nw, sc, sh):
    (hn,), _ = _rowwise(name, lambda x, nw, sc, sh: ([_norm_mod_fn(x, nw, sc, sh)], []),
                        [x], [nw, sc, sh], [(x.shape[1], BF16)], [], 512)
    return hn


def _norm_mod_bwd(name, last, x, dhn_parts, dres, nw, sc, sh, prev=None):
    n = len(dhn_parts)
    d = x.shape[1]
    a, b, mode, tk, kw = last

    def fn(dhn, x, *rest):
        for p in rest[:n]:
            dhn = dhn + p
        dres, rest = rest[n], rest[n + 1:]
        y_prev, (nw, sc, sh), gate = (rest[0], rest[1:4], rest[4]) if prev is not None else (None, rest[0:3], None)
        r = lax.rsqrt(jnp.mean(x * x, axis=-1, keepdims=True) + NORM_EPS)
        xh = x * r
        dxh = dhn * (nw * (1.0 + sc))
        dx = r * (dxh - xh * jnp.mean(dxh * xh, axis=-1, keepdims=True)) + dres
        along = jnp.sum(dhn * xh, axis=0, keepdims=True)
        dnw, dsc, dsh = along * (1.0 + sc), along * nw, jnp.sum(dhn, axis=0, keepdims=True)
        if prev is None:
            return [dx], [dnw, dsc, dsh]
        return [dx, gate * dx], [dnw, dsc, dsh, jnp.sum(dx * y_prev, axis=0, keepdims=True)]

    rows = [x, *dhn_parts, dres] + ([prev[0]] if prev is not None else [])
    consts = [nw, sc, sh] + ([prev[1]] if prev is not None else [])
    outs = [(d, F32)] + ([(d, BF16)] if prev is not None else [])
    res_o, res_a = _matmul_rows(name, a, b, mode, 512, tk, fn, rows, consts, outs, [(1, d)] * (3 + (prev is not None)), **kw)
    return (*res_o, *res_a)


def _rope_tables(pos_col, inv_row):
    def fn(pos, inv):
        ang = pos.astype(F32) * inv
        e = lax.broadcasted_iota(jnp.int32, (1, 128), 1) % HEAD_DIM
        cos, sin = jnp.cos(ang), jnp.sin(ang)
        half = ROT_DIM // 2
        return [jnp.where(e < ROT_DIM, cos, 1.0), jnp.where(e < half, -sin, 0.0),
                jnp.where((e >= half) & (e < ROT_DIM), sin, 0.0)], []

    (c, sa, sb), _ = _rowwise("rope_tables", fn, [pos_col], [inv_row], [(128, F32)] * 3, [], 512)
    return c, sa, sb


def _rot_fwd(t, c, sa, sb):
    n = t.shape[1]
    rep = n // 128
    c, sa, sb = (jnp.tile(u, (1, rep)) for u in (c, sa, sb))
    return t * c + pltpu.roll(t, n - ROT_DIM // 2, 1) * sa + pltpu.roll(t, ROT_DIM // 2, 1) * sb


def _rot_bwd(g, c, sa, sb):
    n = g.shape[1]
    rep = n // 128
    c, sa, sb = (jnp.tile(u, (1, rep)) for u in (c, sa, sb))
    return g * c + pltpu.roll(g * sa, ROT_DIM // 2, 1) + pltpu.roll(g * sb, n - ROT_DIM // 2, 1)


ATT_TQ = 128


def _attn_tiles(l):
    tk = ATT_TQ + 2 * BAND
    return (l, l) if l <= tk else (ATT_TQ, tk)


def _attn_specs(g, s_len):
    def blk(off):
        return pl.BlockSpec((s_len, 128), functools.partial(lambda hp, off: (0, off + hp), off=off))

    return blk(4 * g), blk(12 + 4 * g), blk(4 * g), blk(0)


def _attn_tile_geometry(t, d, l):
    tq, tk = _attn_tiles(l)
    nts = l // tq
    r = t // nts
    ts = t % nts
    q0 = ts * tq
    ws = jnp.clip(q0 - BAND, 0, l - tk)
    kind = jnp.where(ts == 0, 0, jnp.where(ts == nts - 1, 2, 1))
    if d == 1:
        return pl.ds(pl.multiple_of(q0, tq), tq), pl.ds(pl.multiple_of(ws, BAND), tk), kind
    return pl.ds(r + d * q0, tq, stride=d), pl.ds(r + d * ws, tk, stride=d), kind


def _attn_fill_bias(bias_ref):
    _, tq2, tk = bias_ref.shape
    iq = lax.broadcasted_iota(jnp.int32, (tq2, 1), 0) % (tq2 // 2)
    ik = lax.broadcasted_iota(jnp.int32, (1, tk), 1)
    for i, off in enumerate((0, -BAND, -2 * BAND)):
        bias_ref[i] = jnp.where(jnp.abs(ik + off - iq) <= BAND, 0.0, NEG_BIG)


def _split_heads(t, in_h):
    zero = jnp.zeros_like(t)
    return jnp.concatenate([jnp.where(in_h[0], t, zero), jnp.where(in_h[1], t, zero)], axis=0)


def _attn_fwd(g, qk, v):
    s_len = qk.shape[0]
    d = DILATIONS[g]
    l = s_len // d
    tq, tk = _attn_tiles(l)
    assert l % tq == 0 and l >= tk
    q_spec, k_spec, v_spec, o_spec = _attn_specs(g, s_len)
    scale = 1.0 / math.sqrt(HEAD_DIM)

    def body(q_ref, k_ref, v_ref, o_ref, lse_ref, bias_ref):
        lane = lax.broadcasted_iota(jnp.int32, (1, 128), 1)
        in_h = [lane < HEAD_DIM, lane >= HEAD_DIM]
        _attn_fill_bias(bias_ref)

        def tile(t, carry):
            rows, win, kind = _attn_tile_geometry(t, d, l)
            q = (q_ref[rows, :] * scale).astype(BF16)
            k = k_ref[win, :].astype(BF16)
            vv = v_ref[win, :].astype(BF16)
            s = _nt(_split_heads(q, in_h), k) + bias_ref[kind]
            m = jnp.max(s, axis=1, keepdims=True)
            p = jnp.exp(s - m)
            den = jnp.sum(p, axis=1, keepdims=True)
            out = _nn(p, vv) / den
            lse = m + jnp.log(den)
            o_ref[rows, :] = jnp.where(in_h[0], out[:tq], out[tq:])
            lse_ref[rows, :] = jnp.where(in_h[0], lse[:tq], lse[tq:])
            return carry

        lax.fori_loop(0, s_len // tq, tile, 0, unroll=8 * ATT_TQ // tq)

    return pl.pallas_call(
        body, name=f"attn_fwd_g{g}", grid=(4,),
        out_shape=[jax.ShapeDtypeStruct((s_len, 512), F32)] * 2,
        in_specs=[q_spec, k_spec, v_spec], out_specs=[o_spec, o_spec],
        scratch_shapes=[pltpu.VMEM((3, 2 * tq, tk), F32)],
        compiler_params=_params(("parallel",), VMEM_BIG),
    )(qk, qk, v)


def _attn_bwd(g, qk, v, o, lse, do, dlse):
    s_len = qk.shape[0]
    d = DILATIONS[g]
    l = s_len // d
    tq, tk = _attn_tiles(l)
    q_spec, k_spec, v_spec, o_spec = _attn_specs(g, s_len)
    scale = 1.0 / math.sqrt(HEAD_DIM)

    def body(q_ref, k_ref, v_ref, o_ref, lse_ref, do_ref, dlse_ref, dq_ref, dk_ref, dv_ref, bias_ref):
        lane = lax.broadcasted_iota(jnp.int32, (1, 128), 1)
        in_h = [lane < HEAD_DIM, lane >= HEAD_DIM]
        dk_ref[...] = jnp.zeros_like(dk_ref)
        dv_ref[...] = jnp.zeros_like(dv_ref)
        _attn_fill_bias(bias_ref)

        def tile(t, carry):
            rows, win, kind = _attn_tile_geometry(t, d, l)
            k, vv = k_ref[win, :].astype(BF16), v_ref[win, :].astype(BF16)
            dout, lse_t, dlse_t = do_ref[rows, :], lse_ref[rows, :], dlse_ref[rows, :]
            od = dout * o_ref[rows, :]
            q2 = _split_heads((q_ref[rows, :] * scale).astype(BF16), in_h)
            do2 = _split_heads(dout.astype(BF16), in_h)
            head_col = lambda a: jnp.concatenate([a[:, 0:1], a[:, HEAD_DIM:HEAD_DIM + 1]], axis=0)
            delta = jnp.concatenate([jnp.sum(jnp.where(m, od, 0.0), axis=1, keepdims=True) for m in in_h], axis=0)
            p = jnp.exp(_nt(q2, k) + bias_ref[kind] - head_col(lse_t))
            ds = (p * (_nt(do2, vv) - delta + head_col(dlse_t))).astype(BF16)
            dq2 = _nn(ds, k) * scale
            dq_ref[rows, :] = jnp.where(in_h[0], dq2[:tq], dq2[tq:])
            dk_ref[win, :] += _tn(ds, q2)
            dv_ref[win, :] += _tn(p, do2)
            return carry

        lax.fori_loop(0, s_len // tq, tile, 0, unroll=8 * ATT_TQ // tq)

    return pl.pallas_call(
        body, name=f"attn_bwd_g{g}", grid=(4,),
        out_shape=[jax.ShapeDtypeStruct((s_len, 512), F32)] * 3,
        in_specs=[q_spec, k_spec, v_spec, o_spec, o_spec, o_spec, o_spec], out_specs=[o_spec] * 3,
        scratch_shapes=[pltpu.VMEM((3, 2 * tq, tk), F32)],
        compiler_params=_params(("parallel",), VMEM_BIG),
    )(qk, qk, v, o, lse, do, dlse)


def _mix_weights(ls):
    mx = jnp.maximum(jnp.maximum(ls[0], ls[1]), ls[2])
    es = [jnp.exp(x - mx) for x in ls]
    tot = es[0] + es[1] + es[2]
    return [e / tot for e in es]


def _attn_out(os_, lses, z, x, gate, w_out):
    s_len, dm = x.shape
    tm = 256
    wdt = 512
    z, z_block = z

    def body(o0, o1, o2, l0, l1, l2, z_ref, x_ref, g_ref, w_ref, a_ref, y_ref, x1_ref):
        alphas = _mix_weights([l0[...], l1[...], l2[...]])
        y = jnp.zeros((tm, dm), F32)
        for g, o_ref in enumerate((o0, o1, o2)):
            a_g = (o_ref[...] * alphas[g] * _silu(z_ref[:, g * wdt:(g + 1) * wdt])).astype(BF16)
            a_ref[:, g * wdt:(g + 1) * wdt] = a_g
            y = y + _nn(a_g, w_ref[g * wdt:(g + 1) * wdt, :])
        y_ref[...] = y
        x1_ref[...] = x_ref[...] + g_ref[...] * y

    row = lambda c: pl.BlockSpec((tm, c), lambda i: (i, 0))
    return pl.pallas_call(
        body, name="attn_out", grid=(s_len // tm,),
        out_shape=[jax.ShapeDtypeStruct((s_len, 3 * wdt), BF16), jax.ShapeDtypeStruct((s_len, dm), F32),
                   jax.ShapeDtypeStruct((s_len, dm), F32)],
        in_specs=[row(wdt)] * 6 + [pl.BlockSpec((tm, 3 * wdt), lambda i: (i, z_block)), row(dm),
                                   pl.BlockSpec((1, dm), lambda i: (0, 0)), pl.BlockSpec(w_out.shape, lambda i: (0, 0))],
        out_specs=[row(3 * wdt), row(dm), row(dm)],
        compiler_params=_params(("parallel",), VMEM_BIG),
    )(*os_, *lses, z, x, gate, w_out)


def _mix_bwd(dy, w_out, os_, lses, z):
    wdt = 512

    def fn(da, o0, o1, o2, l0, l1, l2, z):
        os_t, ls = [o0, o1, o2], [l0, l1, l2]
        alphas = _mix_weights(ls)
        hi = lax.broadcasted_iota(jnp.int32, (2 * wdt, wdt), 0) % wdt // HEAD_DIM
        hj = lax.broadcasted_iota(jnp.int32, (2 * wdt, wdt), 1) // HEAD_DIM
        seg = (hi == hj).astype(BF16)
        head_sum = lambda t: _dg(jnp.concatenate(_bf16_parts(t, 2), axis=1), seg, 1, 0)
        dos, dal, dzs = [], [], []
        for g in range(3):
            zg = z[:, g * wdt:(g + 1) * wdt]
            sig = jax.nn.sigmoid(zg)
            dag = da[:, g * wdt:(g + 1) * wdt]
            dmix = dag * zg * sig
            dzs.append(dag * os_t[g] * alphas[g] * (sig * (1.0 + zg * (1.0 - sig))))
            dos.append(dmix * alphas[g])
            dal.append(head_sum(dmix * os_t[g]))
        mean = alphas[0] * dal[0] + alphas[1] * dal[1] + alphas[2] * dal[2]
        dls = [alphas[g] * (dal[g] - mean) for g in range(3)]
        return dos + dls + [jnp.concatenate(dzs, axis=1)], []

    outs, _ = _matmul_rows("attn_out_dx_mix_bwd", dy, w_out, "nt", 256, dy.shape[1], fn, [*os_, *lses, (z[0], 3 * wdt, z[1])], [],
                           [(wdt, F32)] * 6 + [(3 * wdt, BF16)], [])
    return outs[:3], outs[3:6], outs[6]


def _rot_pack_bwd(dqs, dks, dvs, tabs):
    wdt = 512

    def fn(*args):
        grads, (c, sa, sb) = args[:9], args[9:]
        cols = [_rot_bwd(gq, c, sa, sb) for gq in grads[:6]] + list(grads[6:])
        return [jnp.concatenate(cols, axis=1)], []

    (out,), _ = _rowwise("rot_pack_bwd", fn, [*dqs, *dks, *dvs, *tabs], [], [(9 * wdt, BF16)], [], 512)
    return out


CONV_CB = 128
CONV_R = 256
CONV_PAD = 8


def _conv_taps(buf, base, off, sign):
    return [buf[pl.ds(base + off + sign * j, CONV_R), :] for j in range(CONV_WIDTH)]


def _conv_tap_sum(taps, w):
    acc = None
    for j, t in enumerate(taps):
        term = t * w[j:j + 1, :]
        acc = term if acc is None else acc + term
    return acc


def _conv_fwd(xpre, cw, cb):
    s_len, ch = xpre.shape
    nchunk = s_len // CONV_R

    def body(x_ref, w_ref, b_ref, o_ref, xp):
        zero = jnp.zeros((CONV_PAD, CONV_CB), F32)
        xp[0:CONV_PAD, :] = zero
        xp[s_len + CONV_PAD:s_len + 2 * CONV_PAD, :] = zero

        def fill(ci, carry):
            base = pl.multiple_of(ci * CONV_R, CONV_R)
            xp[pl.ds(base + CONV_PAD, CONV_R), :] = x_ref[pl.ds(base, CONV_R), :]
            return carry

        lax.fori_loop(0, nchunk, fill, 0)
        w = w_ref[...]
        b = b_ref[...]

        def chunk(ci, carry):
            base = pl.multiple_of(ci * CONV_R, CONV_R)
            u = _conv_tap_sum(_conv_taps(xp, base, CONV_PAD - CONV_WIDTH // 2, 1), w) + b
            o_ref[pl.ds(base, CONV_R), :] = _silu(u)
            return carry

        lax.fori_loop(0, nchunk, chunk, 0, unroll=2)

    col = lambda r: pl.BlockSpec((r, CONV_CB), lambda j: (0, j))
    return pl.pallas_call(
        body, name="conv_fwd", grid=(ch // CONV_CB,), out_shape=jax.ShapeDtypeStruct((s_len, ch), F32),
        in_specs=[col(s_len), col(CONV_WIDTH), col(1)], out_specs=col(s_len),
        scratch_shapes=[pltpu.VMEM((s_len + 2 * CONV_PAD, CONV_CB), F32)],
        compiler_params=_params(("parallel",), VMEM_BIG),
    )(xpre, cw, cb)


def _conv_bwd(xpre, da, cw, cb):
    s_len, ch = xpre.shape
    nchunk = s_len // CONV_R
    half = CONV_WIDTH // 2

    def body(x_ref, da_ref, w_ref, b_ref, dx_ref, gw_ref, gb_ref, xp, dcp):
        zero = jnp.zeros((CONV_PAD, CONV_CB), F32)
        for buf in (xp, dcp):
            buf[0:CONV_PAD, :] = zero
            buf[s_len + CONV_PAD:s_len + 2 * CONV_PAD, :] = zero

        def fill(ci, carry):
            base = pl.multiple_of(ci * CONV_R, CONV_R)
            xp[pl.ds(base + CONV_PAD, CONV_R), :] = x_ref[pl.ds(base, CONV_R), :]
            return carry

        lax.fori_loop(0, nchunk, fill, 0)
        w = w_ref[...]
        b = b_ref[...]

        def first(ci, carry):
            base = pl.multiple_of(ci * CONV_R, CONV_R)
            taps = _conv_taps(xp, base, CONV_PAD - half, 1)
            u = _conv_tap_sum(taps, w) + b
            sig = jax.nn.sigmoid(u)
            dc = da_ref[pl.ds(base, CONV_R), :] * (sig * (1.0 + u * (1.0 - sig)))
            dcp[pl.ds(base + CONV_PAD, CONV_R), :] = dc
            gb = carry[0] + jnp.sum(dc, axis=0, keepdims=True)
            gws = [carry[1 + j] + jnp.sum(dc * taps[j], axis=0, keepdims=True) for j in range(CONV_WIDTH)]
            return (gb, *gws)

        z1 = jnp.zeros((1, CONV_CB), F32)
        sums = lax.fori_loop(0, nchunk, first, (z1,) * (1 + CONV_WIDTH), unroll=2)
        gb_ref[...] = sums[0]
        for j in range(CONV_WIDTH):
            gw_ref[j:j + 1, :] = sums[1 + j]

        def second(ci, carry):
            base = pl.multiple_of(ci * CONV_R, CONV_R)
            dx_ref[pl.ds(base, CONV_R), :] = _conv_tap_sum(_conv_taps(dcp, base, CONV_PAD + half, -1), w).astype(dx_ref.dtype)
            return carry

        lax.fori_loop(0, nchunk, second, 0, unroll=2)

    col = lambda r: pl.BlockSpec((r, CONV_CB), lambda j: (0, j))
    return pl.pallas_call(
        body, name="conv_bwd", grid=(ch // CONV_CB,),
        out_shape=[jax.ShapeDtypeStruct((s_len, ch), BF16), jax.ShapeDtypeStruct((CONV_WIDTH, ch), F32),
                   jax.ShapeDtypeStruct((1, ch), F32)],
        in_specs=[col(s_len), col(s_len), col(CONV_WIDTH), col(1)],
        out_specs=[col(s_len), col(CONV_WIDTH), col(1)],
        scratch_shapes=[pltpu.VMEM((s_len + 2 * CONV_PAD, CONV_CB), F32)] * 2,
        compiler_params=_params(("parallel",), VMEM_BIG),
    )(xpre, da, cw, cb)


SSD_GW = 256
SSD_N = 128
SSD_DTW = 128


def _bf16_parts(x, n):
    parts, rest = [], x
    for _ in range(n):
        p = rest.astype(BF16)
        parts.append(p)
        rest = rest - p.astype(F32)
    return parts


@jax.custom_vjp
def _expand(x, e):
    eb = e.astype(BF16)
    return _dg(jnp.concatenate(_bf16_parts(x, 2), axis=1), jnp.concatenate([eb, eb], axis=0), 1, 0)


def _expand_fwd(x, e):
    return _expand(x, e), e


def _expand_bwd(e, g):
    return _dg(g.astype(BF16), e.astype(BF16), 1, 1), jnp.zeros_like(e)


_expand.defvjp(_expand_fwd, _expand_bwd)


@jax.custom_vjp
def _running_sum(tri, x):
    tb = tri.astype(BF16)
    return sum(_dg(tb, p, 1, 0) for p in _bf16_parts(x, 3))


def _running_sum_fwd(tri, x):
    return _running_sum(tri, x), tri


def _running_sum_bwd(tri, g):
    tb = tri.astype(BF16)
    return jnp.zeros_like(tri), sum(_dg(tb, p, 0, 0) for p in _bf16_parts(g, 3))


_running_sum.defvjp(_running_sum_fwd, _running_sum_bwd)


def _pick_col(a, h):
    @jax.custom_vjp
    def pick(a):
        return a[:, h:h + 1]

    pick.defvjp(lambda a: (a[:, h:h + 1], None),
                lambda _, g: (g * (lax.broadcasted_iota(jnp.int32, (1, a.shape[1]), 1) == h).astype(F32),))
    return pick(a)


def _pick_row(a, h):
    @jax.custom_vjp
    def pick(a):
        return a[h:h + 1, :]

    pick.defvjp(lambda a: (a[h:h + 1, :], None),
                lambda _, g: (g * (lax.broadcasted_iota(jnp.int32, (a.shape[0], 1), 0) == h).astype(F32),))
    return pick(a)


def _ssd_mask(dirn):
    ri = lax.broadcasted_iota(jnp.int32, (CHUNK, CHUNK), 0)
    cj = lax.broadcasted_iota(jnp.int32, (CHUNK, CHUNK), 1)
    return (cj <= ri) if dirn == 0 else (cj >= ri)


def _ssd_rowsel(dirn):
    last = CHUNK - 1 if dirn == 0 else 0
    return (lax.broadcasted_iota(jnp.int32, (CHUNK, 1), 0) == last).astype(F32)


def _ssd_chunk_pre(dirn):
    nh = SSD_DTW

    def f(dt, alog):
        da = dt * (-jnp.exp(alog))
        cum = _running_sum(_ssd_mask(dirn).astype(F32), da)
        tot = jnp.sum(cum * _ssd_rowsel(dirn), axis=0, keepdims=True)
        hh = lax.broadcasted_iota(jnp.int32, (nh, SSD_HEADS * HEAD_DIM), 0)
        jj = lax.broadcasted_iota(jnp.int32, (nh, SSD_HEADS * HEAD_DIM), 1)
        expand = (hh == dirn * SSD_HEADS + jj // HEAD_DIM).astype(F32)
        return cum, cum.T, _expand(dt, expand), _expand(jnp.exp(tot - cum), expand), _expand(jnp.exp(cum), expand)

    return f


def _ssd_group_fn(g, dirn, stacked):
    def f(xs, bm, cm, st, cum, cum_t, dt_e, w_e, ce_e):
        mask = _ssd_mask(dirn)
        xdt = xs * dt_e
        cd_e = jnp.sum(ce_e * _ssd_rowsel(dirn), axis=0, keepdims=True)
        cb = _bnt(cm, bm)
        lane_head = lax.broadcasted_iota(jnp.int32, (1, SSD_GW), 1) // HEAD_DIM
        y = _bnn(cm, st) * ce_e
        decayed, inputs = [], []
        for j in range(4):
            hidx = dirn * SSD_HEADS + 4 * g + j
            col, row = _pick_col(cum, hidx), _pick_row(cum_t, hidx)
            dec = cb * jnp.exp(jnp.where(mask, col - row, NEG_BIG))
            head = (lane_head == j).astype(F32)
            if stacked:
                decayed.append(dec)
                inputs.append(xdt * head)
            else:
                y = y + _bnn(dec, xdt) * head
        if stacked:
            y = y + _bnn(jnp.concatenate(decayed, axis=1), jnp.concatenate(inputs, axis=0))
        st_out = st * cd_e + _btn(bm, xdt * w_e)
        return y, st_out

    return f


def _ssd_in_specs(kk):
    ln = CHUNK
    return [pl.BlockSpec((ln, 2048), lambda i: (kk(i), 0)),
            pl.BlockSpec((ln, 1024), lambda i: (kk(i), 2)),
            pl.BlockSpec((ln, 1024), lambda i: (kk(i), 3)),
            pl.BlockSpec((ln, SSD_DTW), lambda i: (kk(i), 0)),
            pl.BlockSpec((1, SSD_DTW), lambda i: (0, 0))]


def _ssd_fwd(xbc, dt, alog, dirn, prior=None):
    s_len = xbc.shape[0]
    nc = s_len // CHUNK
    kk = (lambda i: i) if dirn == 0 else (lambda i: nc - 1 - i)

    def body(x_ref, b_ref, c_ref, dt_ref, al_ref, *rest):
        prior_ref = rest[0] if prior is not None else None
        y_ref, sts_ref, st = rest[prior is not None:]

        @pl.when(pl.program_id(0) == 0)
        def _():
            st[...] = jnp.zeros_like(st)

        sts_ref[0] = st[...]
        cum, cum_t, dt_e, w_e, ce_e = _ssd_chunk_pre(dirn)(dt_ref[...], al_ref[...])
        for g in range(SSD_GROUPS):
            xc = slice(g * SSD_GW, (g + 1) * SSD_GW)
            gc = slice(g * SSD_N, (g + 1) * SSD_N)
            y, st_new = _ssd_group_fn(g, dirn, True)(x_ref[:, xc], b_ref[:, gc], c_ref[:, gc], st[:, xc], cum, cum_t,
                                               dt_e[:, xc], w_e[:, xc], ce_e[:, xc])
            y_ref[:, xc] = y if prior is None else y + prior_ref[:, xc]
            st[:, xc] = st_new

    return pl.pallas_call(
        body, name=f"ssd_fwd_d{dirn}", grid=(nc,),
        out_shape=[jax.ShapeDtypeStruct((s_len, 2048), F32), jax.ShapeDtypeStruct((nc, SSD_N, 2048), F32)],
        in_specs=_ssd_in_specs(kk) + ([pl.BlockSpec((CHUNK, 2048), lambda i: (kk(i), 0))] if prior is not None else []),
        out_specs=[pl.BlockSpec((CHUNK, 2048), lambda i: (kk(i), 0)),
                   pl.BlockSpec((1, SSD_N, 2048), lambda i: (kk(i), 0, 0))],
        scratch_shapes=[pltpu.VMEM((SSD_N, 2048), F32)],
        compiler_params=_params(("arbitrary",), VMEM_BIG),
    )(xbc, xbc, xbc, dt, alog, *([prior] if prior is not None else []))


def _ssd_bwd(xbc, dt, alog, states, dy, d_e, dirn, prior=None):
    s_len = xbc.shape[0]
    nc = s_len // CHUNK
    kk = (lambda i: nc - 1 - i) if dirn == 0 else (lambda i: i)

    def body(x_ref, b_ref, c_ref, dt_ref, al_ref, sts_ref, dy_ref, de_ref, *rest):
        prior_ref = rest[0] if prior is not None else None
        dx_ref, ddt_ref, dal_ref, dst = rest[prior is not None:]
        plus_prior = (lambda v, cols: v + prior_ref[:, cols]) if prior is not None else (lambda v, cols: v)

        @pl.when(pl.program_id(0) == 0)
        def _():
            dst[...] = jnp.zeros_like(dst)
            dal_ref[...] = jnp.zeros_like(dal_ref)

        (cum, cum_t, dt_e, w_e, ce_e), pre_vjp = jax.vjp(_ssd_chunk_pre(dirn), dt_ref[...], al_ref[...])
        dcum = jnp.zeros_like(cum)
        dcum_t = jnp.zeros_like(cum_t)
        d_dt_e, d_w_e, d_ce_e = [], [], []
        for g in range(SSD_GROUPS):
            xc = slice(g * SSD_GW, (g + 1) * SSD_GW)
            gc = slice(g * SSD_N, (g + 1) * SSD_N)
            _, vjp = jax.vjp(_ssd_group_fn(g, dirn, False), x_ref[:, xc], b_ref[:, gc], c_ref[:, gc], sts_ref[0, :, xc], cum, cum_t,
                             dt_e[:, xc], w_e[:, xc], ce_e[:, xc])
            dyg = dy_ref[:, xc]
            dxs, dbm, dcm, dst_g, dcum_g, dcum_t_g, ddte_g, dwe_g, dcee_g = vjp((dyg, dst[:, xc]))
            if dirn == 0:
                dxs = dxs + dyg * de_ref[:, xc]
            bc, cc = slice(2048 + g * SSD_N, 2048 + (g + 1) * SSD_N), slice(3072 + g * SSD_N, 3072 + (g + 1) * SSD_N)
            dx_ref[:, xc] = plus_prior(dxs, xc)
            dx_ref[:, bc] = plus_prior(dbm, bc)
            dx_ref[:, cc] = plus_prior(dcm, cc)
            dst[:, xc] = dst_g
            dcum = dcum + dcum_g
            dcum_t = dcum_t + dcum_t_g
            d_dt_e.append(ddte_g)
            d_w_e.append(dwe_g)
            d_ce_e.append(dcee_g)
        ddt, dal = pre_vjp((dcum, dcum_t, jnp.concatenate(d_dt_e, axis=1), jnp.concatenate(d_w_e, axis=1),
                            jnp.concatenate(d_ce_e, axis=1)))
        ddt_ref[...] = ddt
        dal_ref[...] += dal

    return pl.pallas_call(
        body, name=f"ssd_bwd_d{dirn}", grid=(nc,),
        out_shape=[jax.ShapeDtypeStruct((s_len, 4096), F32), jax.ShapeDtypeStruct((s_len, SSD_DTW), F32),
                   jax.ShapeDtypeStruct((1, SSD_DTW), F32)],
        in_specs=_ssd_in_specs(kk) + [pl.BlockSpec((1, SSD_N, 2048), lambda i: (kk(i), 0, 0)),
                                      pl.BlockSpec((CHUNK, 2048), lambda i: (kk(i), 0)),
                                      pl.BlockSpec((1, 2048), lambda i: (0, 0))]
        + ([pl.BlockSpec((CHUNK, 4096), lambda i: (kk(i), 0))] if prior is not None else []),
        out_specs=[pl.BlockSpec((CHUNK, 4096), lambda i: (kk(i), 0)),
                   pl.BlockSpec((CHUNK, SSD_DTW), lambda i: (kk(i), 0)),
                   pl.BlockSpec((1, SSD_DTW), lambda i: (0, 0))],
        scratch_shapes=[pltpu.VMEM((SSD_N, 2048), F32)],
        compiler_params=_params(("arbitrary",), VMEM_BIG),
    )(xbc, xbc, xbc, dt, alog, states, dy, d_e, *([prior] if prior is not None else []))


def _gate_norm_fn(y, xs, z, d_e, nw):
    yg = (y + xs * d_e) * _silu(z)
    return yg * lax.rsqrt(jnp.mean(yg * yg, axis=-1, keepdims=True) + NORM_EPS) * nw


def _gate_norm_bwd(dy, w_out, y, xbc, z, d_e, nw):
    def fn(du, y, xs, z, d_e, nw):
        sig = jax.nn.sigmoid(z)
        gate = z * sig
        ysum = y + xs * d_e
        yg = ysum * gate
        r = lax.rsqrt(jnp.mean(yg * yg, axis=-1, keepdims=True) + NORM_EPS)
        t = du * nw
        dyg = t * r - yg * (jnp.mean(t * yg, axis=-1, keepdims=True) * (r * r * r))
        dys = dyg * gate
        dz = dyg * ysum * (sig * (1.0 + z * (1.0 - sig)))
        dnw = jnp.sum(du * yg * r, axis=0, keepdims=True)
        dde = jnp.sum(dys * xs, axis=0, keepdims=True)
        hh = lax.broadcasted_iota(jnp.int32, (2048, SSD_HEADS), 0) // HEAD_DIM
        jj = lax.broadcasted_iota(jnp.int32, (2048, SSD_HEADS), 1)
        return [dys, dz], [dnw, _hnn(jnp.broadcast_to(dde, (8, 2048)), (hh == jj).astype(F32))[0:1]]

    (dys, dz), (g_nw, g_d) = _matmul_rows("ssd_out_dx_gate_norm_bwd", dy, w_out, "nt", 256, dy.shape[1], fn,
                                          [y, (xbc, 2048, 0), z], [d_e, nw], [(2048, F32), (2048, BF16)],
                                          [(1, 2048), (1, SSD_HEADS)])
    return dys, dz, g_nw, g_d


def _ssd_tail_loss(y, xbc, z, d_e, snw, w_out, x1, tgt, gate, fnw):
    dm = x1.shape[1]
    si = y.shape[1]

    def make_u(y, xs, z, x1, tgt, d_e, snw, gate, fnw):
        return _gate_norm_fn(y, xs, z, d_e, snw).astype(BF16)

    def fn(y1, u, y, xs, z, x1, tgt, d_e, snw, gate, fnw):
        x2 = x1 + gate * y1
        r = lax.rsqrt(jnp.mean(x2 * x2, axis=-1, keepdims=True) + NORM_EPS)
        xh = x2 * r
        err = xh * fnw - tgt
        loss = 0.5 * jnp.sum(jnp.mean(err * err, axis=-1, keepdims=True), axis=0, keepdims=True)
        dy = err * (1.0 / dm)
        dxh = dy * fnw
        dx2 = r * (dxh - xh * jnp.mean(dxh * xh, axis=-1, keepdims=True))
        dfnw = jnp.sum(dy * xh, axis=0, keepdims=True)
        return [u, dx2, gate * dx2], [dfnw, jnp.sum(dx2 * y1, axis=0, keepdims=True), jnp.broadcast_to(loss, (1, 128))]

    (u, dx2, dy1), (g_fnw, dgate, loss) = _matmul_rows(
        "ssd_out_loss", make_u, w_out, "nn", 256, si, fn, [y, (xbc, si, 0), z, x1, tgt], [d_e, snw, gate, fnw],
        [(si, BF16), (dm, F32), (dm, BF16)], [(1, dm), (1, dm), (1, 128)])
    return u, dx2, dy1, g_fnw, dgate, loss


def _softplus_fwd(dt_raw, bias):
    (dt,), _ = _rowwise("dt_softplus", lambda r, b: ([jax.nn.softplus(r + b)], []), [dt_raw], [bias],
                        [(dt_raw.shape[1], F32)], [], 512)
    return dt


def _softplus_bwd(ddt_f, ddt_b, dt_raw, bias):
    def fn(df, db, r, b):
        g = (df + db) * jax.nn.sigmoid(r + b)
        return [g], [jnp.sum(g, axis=0, keepdims=True)]

    w = dt_raw.shape[1]
    (g,), (gb,) = _rowwise("dt_softplus_bwd", fn, [ddt_f, ddt_b, dt_raw], [bias], [(w, BF16)], [(1, w)], 512)
    return g, gb


def _mod_part(c_all, mod_w):
    nl, _, ncol = mod_w.shape
    nb = c_all.shape[0]

    def body(c_ref, w_ref, o_ref):
        cond = _silu(c_ref[...])
        for i in range(nl):
            o_ref[i * nb:(i + 1) * nb, :] = _nn(cond, w_ref[i])

    return pl.pallas_call(body, name="mod_part", out_shape=jax.ShapeDtypeStruct((nl * nb, ncol), F32),
                          compiler_params=_params(None, VMEM_BIG))(c_all, mod_w)


def _mod_finish(mod_nb, mod_b, norm_w, tokens):
    nl, dm = norm_w.shape

    def body(a_ref, b_ref, nw_ref, *rest):
        tok_refs, o_refs = rest[:len(tokens)], rest[len(tokens):]
        tok = sum(t[0:1, 0:1] for t in tok_refs)
        for i in range(nl):
            for k in range(3):
                cols = slice(k * dm, (k + 1) * dm)
                o_refs[4 * i + k][...] = a_ref[i:i + 1, cols] + b_ref[i:i + 1, cols]
            o_refs[4 * i + 3][...] = nw_ref[i:i + 1, :] + tok

    rows = pl.pallas_call(body, name="mod_finish", out_shape=[jax.ShapeDtypeStruct((1, dm), F32)] * (4 * nl))(
        mod_nb, mod_b, norm_w, *tokens)
    return [rows[4 * i:4 * i + 4] for i in range(nl)]


def _mod_grad(c_all, dmod_sh):
    nl, nb, ncol = dmod_sh.shape
    dm = c_all.shape[1]

    def body(c_ref, d_ref, o_ref):
        cond = _silu(c_ref[...])
        for i in range(nl):
            o_ref[i] = _tn(cond, d_ref[i])

    return pl.pallas_call(body, name="mod_grad", out_shape=jax.ShapeDtypeStruct((nl, dm, ncol), F32),
                          compiler_params=_params(None, VMEM_BIG))(c_all, dmod_sh)


PACK_ROWS = 16
PACK_COLS = 1024


def _pack_small(rows, b64, a64s, d32, extra):
    nr, na = len(rows), len(a64s)

    def body(*refs):
        o_ref = refs[-1]
        o_ref[...] = jnp.zeros_like(o_ref)
        for i in range(nr):
            o_ref[i:i + 1, :] = refs[i][...]
        b_ref, a_refs, d_ref, e_ref = refs[nr], refs[nr + 1:nr + 1 + na], refs[nr + 1 + na], refs[nr + 2 + na]
        o_ref[nr:nr + 1, 0:64] = b_ref[:, 0:64]
        o_ref[nr:nr + 1, 64:128] = sum(a[:, 0:64] for a in a_refs)
        o_ref[nr:nr + 1, 128:160] = d_ref[...]
        o_ref[nr:nr + 1, 256:384] = e_ref[...]

    return pl.pallas_call(body, name="pack_small", out_shape=jax.ShapeDtypeStruct((PACK_ROWS, PACK_COLS), F32))(
        *rows, b64, *a64s, d32, extra)


def _pack_ssd_small(cw, cb, nw):
    def body(cw_ref, cb_ref, nw_ref, o_ref):
        o_ref[...] = jnp.zeros_like(o_ref)
        o_ref[0:5, :] = cw_ref[...]
        o_ref[5:6, :] = cb_ref[...]
        o_ref[6:7, 0:256] = nw_ref[...]

    return pl.pallas_call(body, name="pack_ssd_small", out_shape=jax.ShapeDtypeStruct((8, 512), F32))(cw, cb, nw)


def _sum_parts(p_ref):
    g = p_ref[0].astype(F32)
    for s in range(1, p_ref.shape[0]):
        g = g + p_ref[s].astype(F32)
    return g


def _adam_update(w, g, m, v):
    m2 = ADAM_B1 * m + (1.0 - ADAM_B1) * g
    v2 = ADAM_B2 * v + (1.0 - ADAM_B2) * (g * g)
    m_hat = m2 / (1.0 - ADAM_B1 ** ADAM_STEP)
    v_hat = v2 / (1.0 - ADAM_B2 ** ADAM_STEP)
    return -ADAM_LR * (m_hat / (jnp.sqrt(v_hat) + ADAM_EPS) + ADAM_WD * w), m2, v2


def _adamw_windows(name, parts, params, windows, extra=None):
    n = len(params)

    def body(p_ref, *rest):
        ins, outs = rest[:3 * n], rest[3 * n:]
        g = _sum_parts(p_ref)
        for pi, rows, cols, idx in windows:
            w_ref, m_ref, v_ref = ins[3 * pi:3 * pi + 3]
            gw = g[rows, cols]
            dw, m2, v2 = _adam_update(w_ref[idx], gw, m_ref[idx], v_ref[idx])
            for o_ref, val in zip(outs[4 * pi:4 * pi + 4], (gw, dw, m2, v2), strict=True):
                o_ref[idx] = val
        if extra is not None:
            outs[4 * n][...] = g[extra[0], extra[1]]

    out_shape = [jax.ShapeDtypeStruct(w.shape, F32) for (w, _, _) in params for _ in range(4)]
    if extra is not None:
        out_shape.append(jax.ShapeDtypeStruct((extra[0].stop - extra[0].start, extra[1].stop - extra[1].start), F32))
    res = pl.pallas_call(body, name=name, out_shape=out_shape)(parts, *[a for p in params for a in p])
    return [res[4 * i:4 * i + 4] for i in range(n)] + ([res[4 * n]] if extra is not None else [])


def _adamw(name, w, parts, m, v, tr, tc=None):
    r_, c_ = w.shape
    p_ = parts.shape[0]
    tr = min(tr, r_)
    tc = c_ if tc is None else tc
    assert r_ % tr == 0 and c_ % tc == 0

    def body(w_ref, p_ref, m_ref, v_ref, g_ref, d_ref, m2_ref, v2_ref):
        g = _sum_parts(p_ref)
        g_ref[...] = g
        d_ref[...], m2_ref[...], v2_ref[...] = _adam_update(w_ref[...], g, m_ref[...], v_ref[...])

    blk = pl.BlockSpec((tr, tc), lambda i, j: (i, j))
    return pl.pallas_call(
        body, name=name, grid=(r_ // tr, c_ // tc), out_shape=[jax.ShapeDtypeStruct((r_, c_), F32)] * 4,
        in_specs=[blk, pl.BlockSpec((p_, tr, tc), lambda i, j: (0, i, j)), blk, blk], out_specs=[blk] * 4,
        compiler_params=_params(("parallel", "parallel"), VMEM_BIG),
    )(w, parts, m, v)


def _dev_index(p):
    return 4 * p[0] + 2 * p[1] + p[2]


def _all_gather(name, xs):
    n = len(xs)
    hbm = pl.BlockSpec(memory_space=pl.ANY)

    def body(*refs):
        x_refs, o_refs = refs[:n], refs[n:2 * n]
        send_sems, recv_sems, local_sems = refs[2 * n:]
        x, y, c = lax.axis_index("x"), lax.axis_index("y"), lax.axis_index("c")
        me, sibling = (x, y, c), (x, y, 1 - c)
        chips = [(1 - x, y), (x, 1 - y), (1 - x, 1 - y)]

        def place(a, block):
            return o_refs[a].at[_dev_index(block)]

        def copy(a, k, block, to, src=None):
            dst = place(a, block)
            return pltpu.make_async_remote_copy(
                src_ref=dst if src is None else src, dst_ref=dst, send_sem=send_sems.at[a, k],
                recv_sem=recv_sems.at[a, k], device_id=to, device_id_type=MESH)

        mine = [pltpu.make_async_copy(x_refs[a], place(a, me), local_sems.at[a]) for a in range(n)]
        for cp in mine:
            cp.start()
        first = []
        for a in range(n):
            first.append(copy(a, 0, me, sibling, src=x_refs[a]))
            first += [copy(a, 1 + j, me, (*chip, c), src=x_refs[a]) for j, chip in enumerate(chips)]
        for cp in first:
            cp.start()
        passed = []
        for j, chip in enumerate(chips):
            for a in range(n):
                copy(a, 1 + j, (*chip, c), me).wait_recv()
                cp = copy(a, 4 + j, (*chip, c), sibling)
                cp.start()
                passed.append(cp)
        for a in range(n):
            copy(a, 0, sibling, me).wait_recv()
            for j, chip in enumerate(chips):
                copy(a, 4 + j, (*chip, 1 - c), me).wait_recv()
        for cp in first + passed:
            cp.wait_send()
        for cp in mine:
            cp.wait()

    return pl.pallas_call(
        body, name=name, out_shape=[jax.ShapeDtypeStruct((NDEV, *x.shape), x.dtype) for x in xs],
        in_specs=[hbm] * n, out_specs=[hbm] * n,
        scratch_shapes=[pltpu.SemaphoreType.DMA((n, 7)), pltpu.SemaphoreType.DMA((n, 7)), pltpu.SemaphoreType.DMA((n,))],
    )(*xs)


_HBM = pl.BlockSpec(memory_space=pltpu.HBM)
_SEM = pl.BlockSpec(memory_space=pltpu.SEMAPHORE)
_EFFECT = pltpu.SideEffectType.DATAFLOW_SIDE_EFFECTING


def _mesh_position():
    return lax.axis_index("x"), lax.axis_index("y"), lax.axis_index("c")


def _peers(me):
    return [(k, tuple(1 - v if (k >> b) & 1 else v for v, b in zip(me, (2, 1, 0)))) for k in range(1, NDEV)]


def _column_window(ref, block, width):
    return ref.at[:, pl.ds(pl.multiple_of(_dev_index(block) * width, 128), width)]


RELAY_COPIES = 3


def _relay_copies(o_ref, send_sems, recv_sems, with_arrivals):
    width = o_ref.shape[1] // NDEV
    x, y, c = me = _mesh_position()
    sibling = (x, y, 1 - c)
    x_side, y_side, diagonal = (1 - x, y), (x, 1 - y), (1 - x, 1 - y)
    first = c == 0
    via = (jnp.where(first, 1 - x, x), jnp.where(first, y, 1 - y))
    to = (jnp.where(first, x, 1 - x), jnp.where(first, 1 - y, y))

    def copy(k, block, device):
        window = _column_window(o_ref, block, width)
        return pltpu.make_async_remote_copy(src_ref=window, dst_ref=window, send_sem=send_sems.at[k],
                                            recv_sem=recv_sems.at[k], device_id=device, device_id_type=MESH)

    sent = [copy(0, (*via, c), (*to, c)), copy(1, (*x_side, c), sibling), copy(2, (*y_side, c), sibling)]
    if not with_arrivals:
        return sent
    arrivals =[copy(0, (*diagonal, c), me), copy(1, (*x_side, 1 - c), me), copy(2, (*y_side, 1 - c), me)]
    return sent, arrivals


def _relay_start(name, gathered, dep):
    def body(g_ref, dep_ref, send_sems, recv_sems, o_ref, token):
        for cp in _relay_copies(g_ref, send_sems, recv_sems, with_arrivals=False):
            cp.start()
        token[...] = jnp.zeros_like(token)

    sems = pltpu.SemaphoreType.DMA((RELAY_COPIES,))
    res = pl.pallas_call(
        body, name=name,
        out_shape=(sems, sems, pltpu.HBM(gathered.shape, gathered.dtype), jax.ShapeDtypeStruct((8, 128), F32)),
        in_specs=[_HBM, pl.BlockSpec(memory_space=pl.ANY)],
        out_specs=(_SEM, _SEM, _HBM, pl.BlockSpec(memory_space=pltpu.VMEM)),
        input_output_aliases={0: 2},
        compiler_params=pltpu.CompilerParams(has_side_effects=_EFFECT),
    )(gathered, dep)
    return res[:-1], res[-1]


def _relay_wait(name, handles, after):
    send_sems, recv_sems, gathered = handles

    def body(g_ref, s_sems, r_sems, after_ref, o_ref):
        sent, arrivals = _relay_copies(g_ref, s_sems, r_sems, with_arrivals=True)
        for cp, arrival in zip(sent, arrivals):
            cp.wait_send()
            arrival.wait_recv()

    return pl.pallas_call(
        body, name=name, out_shape=pltpu.HBM(gathered.shape, gathered.dtype),
        in_specs=[_HBM, _SEM, _SEM, pl.BlockSpec(memory_space=pl.ANY)], out_specs=_HBM, input_output_aliases={0: 0},
        compiler_params=pltpu.CompilerParams(has_side_effects=_EFFECT),
    )(gathered, send_sems, recv_sems, after)


def _columns_last(name, gathered):
    width = gathered.shape[1] // NDEV
    hbm = pl.BlockSpec(memory_space=pl.ANY)

    def body(g_ref, o_ref, send_sem, recv_sem):
        x, y, c = _mesh_position()

        def copy(core, device):
            window = _column_window(o_ref, (1 - x, 1 - y, core), width)
            return pltpu.make_async_remote_copy(src_ref=window, dst_ref=window, send_sem=send_sem, recv_sem=recv_sem,
                                                device_id=device, device_id_type=MESH)

        onward = copy(c, (x, y, 1 - c))
        onward.start()
        copy(1 - c, (x, y, c)).wait_recv()
        onward.wait_send()

    return pl.pallas_call(
        body, name=name, out_shape=jax.ShapeDtypeStruct(gathered.shape, gathered.dtype), in_specs=[hbm], out_specs=hbm,
        input_output_aliases={0: 0}, scratch_shapes=[pltpu.SemaphoreType.DMA, pltpu.SemaphoreType.DMA],
    )(gathered)


NCHIP = NDEV // 2
EXCHANGE_COPIES = {"columns": NCHIP - 1, "gather": NDEV - 1, "scatter": NDEV - 1, "pair": NCHIP, "chips": NCHIP - 1}


def _landing_zones(name, xs, mode):
    x_, y_, c_ = _mesh_position()
    mine = (2 * x_ + y_ if mode == "chips" else _dev_index((x_, y_, c_))).astype(jnp.int32).reshape(1)
    lands = []
    for a, x in enumerate(xs):
        rows, cols = x.shape[-2:]
        if mode == "pair":
            lands.append(lax.empty((NCHIP, rows, cols), x.dtype))
            continue
        tr = 256 if rows % 256 == 0 else rows

        def body(me_ref, x_ref, o_ref):
            o_ref[...] = x_ref[...]

        if mode in ("gather", "columns"):
            in_spec = pl.BlockSpec((tr, cols), lambda i, me_ref: (i, 0))
        else:
            in_spec = pl.BlockSpec((None, tr, cols), lambda i, me_ref: (me_ref[0], i, 0))
        if mode == "columns":
            out_shape, out_spec = (rows, NDEV * cols), pl.BlockSpec((tr, cols), lambda i, me_ref: (i, me_ref[0]))
        else:
            out_shape = (NCHIP if mode == "chips" else NDEV, rows, cols)
            out_spec = pl.BlockSpec((None, tr, cols), lambda i, me_ref: (me_ref[0], i, 0))
        lands.append(pl.pallas_call(
            body, name=f"{name}_{a}", out_shape=jax.ShapeDtypeStruct(out_shape, x.dtype),
            grid_spec=pltpu.PrefetchScalarGridSpec(num_scalar_prefetch=1, grid=(rows // tr,), in_specs=[in_spec],
                                                   out_specs=out_spec),
            compiler_params=_params(("arbitrary",)),
        )(mine, x))
    return lands


def _exchange_copies(x_refs, land_refs, send_sems, recv_sems, mode):
    x_, y_, c_ = me = _mesh_position()
    per_array = EXCHANGE_COPIES[mode]
    out = []

    def add(a, k, src, dst, peer):
        sem = a * per_array + k
        out.append(pltpu.make_async_remote_copy(src_ref=src, dst_ref=dst, send_sem=send_sems.at[sem], recv_sem=recv_sems.at[sem],
                                                device_id=peer, device_id_type=MESH))

    for a, (x_ref, land_ref) in enumerate(zip(x_refs, land_refs)):
        if mode == "columns":
            for k, peer in enumerate([(x_, y_, 1 - c_), (1 - x_, y_, c_), (x_, 1 - y_, c_)]):
                add(a, k, x_ref, _column_window(land_ref, me, x_ref.shape[1]), peer)
        elif mode in ("gather", "scatter"):
            for k, peer in _peers(me):
                add(a, k - 1, x_ref.at[_dev_index(peer)] if mode == "scatter" else x_ref, land_ref.at[_dev_index(me)], peer)
        elif mode == "pair":
            for chip in range(NCHIP):
                add(a, chip, x_ref.at[2 * chip + 1 - c_], land_ref.at[chip], (x_, y_, 1 - c_))
        else:
            for k in range(1, NCHIP):
                px, py = (1 - x_ if k & 2 else x_), (1 - y_ if k & 1 else y_)
                add(a, k - 1, x_ref.at[2 * px + py], land_ref.at[2 * x_ + y_], (px, py, c_))
    return out


def _exchange_start(name, xs, lands, mode, dep, carry=False):
    n = len(xs)

    def body(*refs):
        x_refs, land_refs = refs[:n], refs[n:2 * n]
        send_sems, recv_sems = refs[2 * n + 1], refs[2 * n + 2]
        for cp in _exchange_copies(x_refs, land_refs, send_sems, recv_sems, mode):
            cp.start()
        if not carry:
            refs[-1][...] = jnp.zeros_like(refs[-1])

    sems = pltpu.SemaphoreType.DMA((n * EXCHANGE_COPIES[mode],))
    moved = [pltpu.with_memory_space_constraint(a, pltpu.HBM) for a in (*xs, *lands, *([dep] if carry else []))]
    res = pl.pallas_call(
        body, name=name,
        out_shape=(sems, sems, *[pltpu.HBM(a.shape, a.dtype) for a in moved],
                   *([] if carry else [jax.ShapeDtypeStruct((8, 128), F32)])),
        in_specs=[_HBM] * len(moved) + ([] if carry else [pl.BlockSpec(memory_space=pl.ANY)]),
        out_specs=(_SEM, _SEM, *[_HBM] * len(moved), *([] if carry else [pl.BlockSpec(memory_space=pltpu.VMEM)])),
        input_output_aliases={i: 2 + i for i in range(len(moved))},
        compiler_params=pltpu.CompilerParams(has_side_effects=_EFFECT),
    )(*moved, *([] if carry else [dep]))
    return res[:-1], res[-1]


def _exchange_wait(name, handles, mode, after, with_sources=False):
    send_sems, recv_sems = handles[0], handles[1]
    bufs = handles[2:]
    n = len(bufs) // 2
    afters = list(after) if isinstance(after, (list, tuple)) else [after]

    def body(*refs):
        x_refs, land_refs = refs[:n], refs[n:2 * n]
        s_sems, r_sems = refs[2 * n], refs[2 * n + 1]
        for cp in _exchange_copies(x_refs, land_refs, s_sems, r_sems, mode):
            cp.wait_send()
            cp.wait_recv()

    res = pl.pallas_call(
        body, name=name, out_shape=tuple(pltpu.HBM(a.shape, a.dtype) for a in bufs),
        in_specs=[_HBM] * (2 * n) + [_SEM, _SEM] + [pl.BlockSpec(memory_space=pl.ANY)] * len(afters),
        out_specs=tuple([_HBM] * (2 * n)), input_output_aliases={i: i for i in range(2 * n)},
        compiler_params=pltpu.CompilerParams(has_side_effects=_EFFECT),
    )(*bufs, send_sems, recv_sems, *afters)
    return (res[n:], res[:n]) if with_sources else res[n:]


def _pair_sum(name, x, from_sibling):
    _, rows, cols = x.shape
    tr = rows
    core = lax.axis_index("c").astype(jnp.int32).reshape(1)

    def body(c_ref, x_ref, s_ref, o_ref):
        o_ref[...] = (x_ref[...].astype(F32) + s_ref[...].astype(F32)).astype(o_ref.dtype)

    return pl.pallas_call(
        body, name=name, out_shape=jax.ShapeDtypeStruct((NCHIP, rows, cols), x.dtype),
        grid_spec=pltpu.PrefetchScalarGridSpec(
            num_scalar_prefetch=1, grid=(NCHIP, rows // tr),
            in_specs=[pl.BlockSpec((None, tr, cols), lambda j, i, c_ref: (2 * j + c_ref[0], i, 0)),
                      pl.BlockSpec((None, tr, cols), lambda j, i, c_ref: (j, i, 0))],
            out_specs=pl.BlockSpec((None, tr, cols), lambda j, i, c_ref: (j, i, 0))),
        compiler_params=_params(("parallel", "parallel")),
    )(core, x, from_sibling)


def kernel(x, c, positions, norm_w, mod_w, mod_b, attn_w_in, attn_w_out, ssd_w_in, ssd_conv_w, ssd_conv_b, ssd_dt_bias, ssd_a_log, ssd_d, ssd_norm_w, ssd_w_out, final_norm_w, loss_target, m_norm_w, m_mod_w, m_mod_b, m_attn_w_in, m_attn_w_out, m_ssd_w_in, m_ssd_conv_w, m_ssd_conv_b, m_ssd_dt_bias, m_ssd_a_log, m_ssd_d, m_ssd_norm_w, m_ssd_w_out, m_final_norm_w, v_norm_w, v_mod_w, v_mod_b, v_attn_w_in, v_attn_w_out, v_ssd_w_in, v_ssd_conv_w, v_ssd_conv_b, v_ssd_dt_bias, v_ssd_a_log, v_ssd_d, v_ssd_norm_w, v_ssd_w_out, v_final_norm_w):
    s_len, dm = x.shape[1], x.shape[2]
    me = 4 * lax.axis_index("x") + 2 * lax.axis_index("y") + lax.axis_index("c")
    x0 = x.reshape(s_len, dm)
    tgt = loss_target.reshape(s_len, dm)
    aw = 3 * 512
    si = 2 * dm
    sxbc = 2 * si
    n_ssd_in = ssd_w_in.shape[2] * NDEV

    (c_all,) = _all_gather("gather_c", [c])
    c_all = c_all.reshape(NDEV, dm)
    part = _mod_part(c_all, mod_w)
    (part_all,) = _all_gather("gather_mod", [part])
    mod_nb = jnp.stack([lax.dynamic_index_in_dim(part_all, i * NDEV + me, axis=1, keepdims=False).reshape(3 * dm)
                        for i in range(2)])

    wcol = attn_w_in.shape[2]
    ai_shard = [attn_w_in[0].astype(BF16)]
    ai_handles, ai_token = _exchange_start("attn_w_in_start", ai_shard, _landing_zones("attn_w_in_place", ai_shard, "columns"),
                                           "columns", part_all)
    inv_freq = ROPE_THETA ** (-jnp.arange(0, ROT_DIM, 2, dtype=F32) / ROT_DIM)
    per_head = jnp.concatenate([inv_freq, inv_freq, jnp.zeros(HEAD_DIM - ROT_DIM, F32)])
    inv_row = jnp.tile(per_head, 128 // HEAD_DIM).reshape(1, 128) + ai_token[0:1]
    tabs = _rope_tables(positions.reshape(s_len, 1), inv_row)
    ssd_small = _pack_ssd_small(ssd_conv_w[0], ssd_conv_b, ssd_norm_w)
    ao_shard = [attn_w_out[0].astype(BF16)]
    late_shards = [ssd_w_in[0].T.astype(BF16), ssd_w_out[0].astype(BF16), ssd_small]
    ao_lands = _landing_zones("w_out_place", ao_shard, "gather")
    late_lands = _landing_zones("weights_place", late_shards, "gather")
    (w_ai,) = _exchange_wait("attn_w_in_wait", ai_handles, "columns", [*tabs, *ao_lands, *late_lands])
    relay_handles, relay_token = _relay_start("attn_w_in_relay_start", w_ai, tabs[0])
    (shift0, scale0, gate0, nw0), (shift1, scale1, gate1, nw1) = _mod_finish(mod_nb, mod_b, norm_w, [relay_token])
    shift, scale, gate, nw = [shift0, shift1], [scale0, scale1], [gate0, gate1], [nw0, nw1]
    hn0 = _norm_mod_fwd("norm0", x0, nw[0], scale[0], shift[0])
    w_ai = _columns_last("gather_attn_w_in_last", _relay_wait("attn_w_in_relay_wait", relay_handles, hn0))

    ao_handles, w_ai = _exchange_start("w_out_start", ao_shard, ao_lands, "gather", w_ai, carry=True)
    w_handles, w_ai = _exchange_start("weights_start", late_shards, late_lands, "gather", w_ai, carry=True)

    qk = _matmul("proj_qk", hn0, w_ai, "nn", F32, MM_T, MM_T, dm, epilogue=_rot_fwd, mrows=tabs, n_out=2 * aw)
    v = _matmul("proj_vz", hn0, w_ai, "nn", F32, MM_T, MM_T, dm, b_noff=2 * aw, n_out=2 * aw)
    z0 = (v, 1)
    att = [_attn_fwd(g, qk, v) for g in range(3)]
    os_, lses = [a[0] for a in att], [a[1] for a in att]
    (g_ao,) = _exchange_wait("w_out_wait", ao_handles, "gather", lses[2])
    a0, y0, x1 = _attn_out(os_, lses, z0, x0, gate[0], g_ao.reshape(aw, dm))

    hn1 = _norm_mod_fwd("norm1", x1, nw[1], scale[1], shift[1])
    g_si, g_so, g_small = _exchange_wait("weights_wait", w_handles, "gather", hn1)
    w_ao = g_ao.reshape(aw, dm)
    w_si_t = g_si.reshape(n_ssd_in, dm)
    w_so = g_so.reshape(si, dm)
    conv_w = g_small[:, 0:CONV_WIDTH, :].transpose(1, 0, 2).reshape(CONV_WIDTH, sxbc)
    conv_b = g_small[:, 5, :].reshape(1, sxbc)
    snw = g_small[:, 6, 0:si // NDEV].reshape(1, si)
    ndt = 2 * SSD_HEADS
    z1 = _matmul("ssd_proj_z", hn1, w_si_t, "nt", F32, MM_T, MM_T, dm, n_out=si)
    xpre = _matmul("ssd_proj_xbc", hn1, w_si_t, "nt", F32, MM_T, MM_T, dm, b_noff=si, n_out=sxbc)
    dt_raw = _matmul("ssd_proj_dt", hn1, w_si_t, "nt", F32, MM_T, ndt, dm, b_noff=si + sxbc, n_out=ndt)
    xbc = _conv_fwd(xpre, conv_w, conv_b)
    widen = lambda a: jnp.pad(a, ((0, 0), (0, SSD_DTW - ndt)))
    dt_raw = widen(dt_raw)
    dt_bias = widen(ssd_dt_bias.reshape(1, ndt))
    alog = widen(ssd_a_log.reshape(1, ndt))
    dt = _softplus_fwd(dt_raw, dt_bias)
    y_f, st_f = _ssd_fwd(xbc, dt, alog, 0)
    y_fb, st_b = _ssd_fwd(xbc, dt, alog, 1, prior=y_f)
    d_e = jnp.repeat(ssd_d.reshape(SSD_HEADS), HEAD_DIM).reshape(1, si)

    fnw = final_norm_w.reshape(1, dm)
    u, dx2, dy1, g_fnw, dgate1, loss_part = _ssd_tail_loss(y_fb, xbc, z1, d_e, snw, w_so, x1, tgt, gate[1], fnw)
    gw_so = _matmul("ssd_out_dw", u, dy1, "tn", BF16, MM_T, MM_T, MM_T)
    dys, dz1, g_snw, g_d = _gate_norm_bwd(dy1, w_so, y_fb, xbc, z1, d_e, snw)
    dxbc_f, ddt_f, dalog_f = _ssd_bwd(xbc, dt, alog, st_f, dys, d_e, 0)
    dxbc, ddt_b, dalog_b = _ssd_bwd(xbc, dt, alog, st_b, dys, d_e, 1, prior=dxbc_f)
    dpre, g_cw, g_cb = _conv_bwd(xpre, dxbc, conv_w, conv_b)
    ddt_raw, g_dtb = _softplus_bwd(ddt_f, ddt_b, dt_raw, dt_bias)
    ddt_raw = ddt_raw[:, :ndt]
    dhn1 = [_matmul("ssd_proj_z_dx", dz1, w_si_t, "nn", F32, MM_T, MM_T, MM_T),
            _matmul("ssd_proj_xbc_dx", dpre, w_si_t, "nn", F32, MM_T, MM_T, MM_T, b_koff=si)]
    gw_si_t = _matmul("ssd_proj_z_dw", dz1, hn1, "tn", BF16, MM_T, MM_T, MM_T, dest=(n_ssd_in, 0, None))
    gw_si_t = _matmul("ssd_proj_xbc_dw", dpre, hn1, "tn", BF16, MM_T, MM_T, MM_T, dest=(n_ssd_in, si, gw_si_t))
    gw_si_t = _matmul("ssd_proj_dt_dw", ddt_raw, hn1, "tn", BF16, ndt, MM_T, MM_T, dest=(n_ssd_in, si + sxbc, gw_si_t))

    l1_grads = [gw_so.reshape(NDEV, si // NDEV, dm), gw_si_t.reshape(NDEV, n_ssd_in // NDEV, dm),
                _pack_ssd_small_blocks(g_cw, g_cb, g_snw)]
    l1_handles, l1_token = _exchange_start("l1_grads_start", l1_grads, _landing_zones("l1_grads_place", l1_grads, "scatter"),
                                           "scatter", dhn1[1])
    dx1, dy0, g_nw1, dsc1, dsh1, dgate0 = _norm_mod_bwd(
        "ssd_proj_dt_dx_norm1_bwd", (ddt_raw, w_si_t, "nn", ndt, dict(b_koff=si + sxbc)), x1, dhn1, dx2,
        nw[1], scale[1], shift[1], prev=(y0, gate[0] + l1_token[0:1, 0:1]))

    gw_ao = _matmul("attn_out_dw", a0, dy0, "tn", BF16, aw // 2, MM_T, MM_T)
    dos, dls, dz0 = _mix_bwd(dy0, w_ao, os_, lses, z0)
    datt = [_attn_bwd(g, qk, v, os_[g], lses[g], dos[g], dls[g]) for g in range(3)]
    dqkv = _rot_pack_bwd([t[0] for t in datt], [t[1] for t in datt], [t[2] for t in datt], tabs)
    gw_ai = _matmul("proj_qkv_dw", hn0, dqkv, "tn", BF16, MM_T, wcol, MM_T, out_blocks=3 * aw // wcol, dest=(NDEV, 0, None))
    gw_ai = _matmul("proj_z_dw", hn0, dz0, "tn", BF16, MM_T, wcol, MM_T, out_blocks=aw // wcol,
                    dest=(NDEV, 3 * aw // wcol, gw_ai))
    after_start = lambda acc, t: acc + t
    zero_row = lambda token: jnp.tile(token[0:1], (1, dm // 128))
    l0_grads = [gw_ai, gw_ao.reshape(NDEV, aw // NDEV, dm)]
    pair_handles, pair_token = _exchange_start("l0_pair_start", l0_grads, _landing_zones("l0_pair_place", l0_grads, "pair"),
                                               "pair", dqkv)
    dhn0_z = _matmul("proj_z_dx", dz0, w_ai, "nt", F32, MM_T, MM_T, aw, b_koff=3 * aw, n_out=dm, epilogue=after_start,
                     ncols=(zero_row(pair_token),))
    from_sibling, l0_grads = _exchange_wait("l0_pair_wait", pair_handles, "pair", dhn0_z, with_sources=True)
    chip_sums = [_pair_sum(f"l0_pair_sum_{a}", g, s) for a, (g, s) in enumerate(zip(l0_grads, from_sibling))]
    l0_handles, l0_token = _exchange_start("l0_grads_start", chip_sums, _landing_zones("l0_grads_place", chip_sums, "chips"),
                                           "chips", dhn0_z)
    dx0, g_nw0, dsc0, dsh0 = _norm_mod_bwd(
        "proj_qkv_dx_norm0_bwd", (dqkv, w_ai, "nt", aw, dict(n_out=dm)), x0, [dhn0_z], dx1,
        nw[0], scale[0], shift[0] + zero_row(l0_token))

    small_g = [_pack_small([dsh0, dsc0, dgate0, dsh1, dsc1, dgate1, g_nw0, g_nw1, g_fnw], g_dtb, [dalog_f, dalog_b], g_d, loss_part)]
    sm_handles, sm_token = _exchange_start("small_grads_start", small_g, _landing_zones("small_grads_place", small_g, "gather"),
                                           "gather", dx0)

    whole = (slice(None), slice(None))
    r_so, r_si, r_small = _exchange_wait("l1_grads_wait", l1_handles, "scatter", sm_token)
    si_out = [o.T for o in _adamw("adamw_ssd_w_in", ssd_w_in[0].T, r_si, m_ssd_w_in[0].T, v_ssd_w_in[0].T, n_ssd_in // NDEV, 512)]
    so_out = _adamw("adamw_ssd_w_out", ssd_w_out[0], r_so, m_ssd_w_out[0], v_ssd_w_out[0], 256)
    cw_cols = ssd_conv_w.shape[2]
    cw_out, cb_out, snw_out = _adamw_windows(
        "adamw_ssd_small", r_small,
        [(ssd_conv_w, m_ssd_conv_w, v_ssd_conv_w), (ssd_conv_b, m_ssd_conv_b, v_ssd_conv_b),
         (ssd_norm_w, m_ssd_norm_w, v_ssd_norm_w)],
        [(0, slice(0, CONV_WIDTH), slice(0, cw_cols), (0, slice(None), slice(None))),
         (1, slice(5, 6), slice(0, cw_cols), whole), (2, slice(6, 7), slice(0, si // NDEV), whole)])
    r_ai, r_ao = _exchange_wait("l0_grads_wait", l0_handles, "chips", so_out[0])
    ai_out = _adamw("adamw_attn_w_in", attn_w_in[0], r_ai, m_attn_w_in[0], v_attn_w_in[0], 512)
    ao_out = _adamw("adamw_attn_w_out", attn_w_out[0], r_ao, m_attn_w_out[0], v_attn_w_out[0], 192)

    (small_all,) = _exchange_wait("small_grads_wait", sm_handles, "gather", ai_out[0])
    full = slice(0, PACK_COLS)
    nhd = SSD_HEADS
    windows = [(0, slice(3 * i + k, 3 * i + k + 1), full, (slice(i, i + 1), slice(k * dm, (k + 1) * dm)))
               for i in range(2) for k in range(3)]
    windows += [(1, slice(6 + i, 7 + i), full, (slice(i, i + 1), slice(None))) for i in range(2)]
    windows += [(2, slice(8, 9), full, whole)]
    windows += [(3 + q, slice(9, 10), slice(2 * nhd * q + nhd * j, 2 * nhd * q + nhd * (j + 1)), (0, slice(j, j + 1), slice(None)))
                for q in range(2) for j in range(2)]
    windows += [(5, slice(9, 10), slice(4 * nhd, 5 * nhd), whole)]
    as_row = lambda a: a.reshape(1, dm)
    mb_out, nw_out, fnw_out, dtb_out, alog_out, d_out, loss = _adamw_windows(
        "adamw_small", small_all,
        [(mod_b, m_mod_b, v_mod_b), (norm_w, m_norm_w, v_norm_w), (fnw, as_row(m_final_norm_w), as_row(v_final_norm_w)),
         (ssd_dt_bias, m_ssd_dt_bias, v_ssd_dt_bias), (ssd_a_log, m_ssd_a_log, v_ssd_a_log), (ssd_d, m_ssd_d, v_ssd_d)],
        windows, extra=(slice(9, 10), slice(256, 257)))
    loss = loss.reshape(())

    ncol = mod_w.shape[2]
    dmod_all = small_all[:, 0:6, :].reshape(NDEV, 2, 3 * dm)
    dmod_sh = lax.dynamic_slice_in_dim(dmod_all, me * ncol, ncol, axis=2).transpose(1, 0, 2)
    g_modw = _mod_grad(c_all, dmod_sh).reshape(1, 2 * dm, ncol)
    modw_out = _adamw("adamw_mod_w", mod_w.reshape(2 * dm, ncol), g_modw, m_mod_w.reshape(2 * dm, ncol),
                      v_mod_w.reshape(2 * dm, ncol), 512)

    per_kind = []
    for k in range(4):
        per_kind.append([
            nw_out[k], modw_out[k].reshape(mod_w.shape), mb_out[k], ai_out[k][None], ao_out[k][None], si_out[k][None],
            cw_out[k], cb_out[k], dtb_out[k], alog_out[k], d_out[k], snw_out[k], so_out[k][None], fnw_out[k].reshape(dm)])
    return (loss, dx0.reshape(x.shape), *per_kind[0], *per_kind[1], *per_kind[2], *per_kind[3])


def _pack_ssd_small_blocks(g_cw, g_cb, g_nw):
    nper = g_cw.shape[1] // NDEV
    nwper = g_nw.shape[1] // NDEV

    def body(cw_ref, cb_ref, nw_ref, o_ref):
        o_ref[...] = jnp.zeros_like(o_ref)
        for d in range(NDEV):
            o_ref[d, 0:5, :] = cw_ref[:, d * nper:(d + 1) * nper]
            o_ref[d, 5:6, :] = cb_ref[:, d * nper:(d + 1) * nper]
            o_ref[d, 6:7, 0:nwper] = nw_ref[:, d * nwper:(d + 1) * nwper]

    return pl.pallas_call(body, name="pack_ssd_small_grads", out_shape=jax.ShapeDtypeStruct((NDEV, 8, nper), F32))(g_cw, g_cb, g_nw)
```

```python
import functools
import math

import jax
import jax.numpy as jnp
from jax import lax
from jax.experimental import pallas as pl
from jax.experimental.pallas import tpu as pltpu

F32 = jnp.float32
BF16 = jnp.bfloat16
HI = lax.Precision.HIGHEST
MESH = pl.DeviceIdType.MESH
NDEV = 8

NORM_EPS = 1e-6
ROPE_THETA = 500000.0
ROT_DIM = 16
HEAD_DIM = 64
DILATIONS = (1, 4, 16)
BAND = 64
NEG_BIG = -1e30
CHUNK = 128
SSD_HEADS = 32
SSD_GROUPS = 8
CONV_WIDTH = 5

ADAM_LR = 0.001
ADAM_B1 = 0.9
ADAM_B2 = 0.999
ADAM_EPS = 1e-08
ADAM_WD = 0.01
ADAM_STEP = 10

VMEM_BIG = 56 * 1024 * 1024
MM_T = 1024


def _params(sem=None, vmem=None):
    kw = {}
    if sem is not None:
        kw["dimension_semantics"] = sem
    if vmem is not None:
        kw["vmem_limit_bytes"] = vmem
    return pltpu.CompilerParams(**kw)


def _dg(a, b, ca, cb, prec=None):
    return lax.dot_general(a, b, (((ca,), (cb,)), ((), ())), preferred_element_type=F32, precision=prec)


def _nn(a, b):
    return _dg(a.astype(BF16), b.astype(BF16), 1, 0)


def _nt(a, b):
    return _dg(a.astype(BF16), b.astype(BF16), 1, 1)


def _tn(a, b):
    return _dg(a.astype(BF16), b.astype(BF16), 0, 0)


def _hnn(a, b):
    return _dg(a, b, 1, 0, HI)


@jax.custom_vjp
def _bnn(a, b):
    return _nn(a, b)


_bnn.defvjp(lambda a, b: (_nn(a, b), (a, b)), lambda r, g: (_nt(g, r[1]), _tn(r[0], g)))


@jax.custom_vjp
def _bnt(a, b):
    return _nt(a, b)


_bnt.defvjp(lambda a, b: (_nt(a, b), (a, b)), lambda r, g: (_nn(g, r[1]), _tn(g, r[0])))


@jax.custom_vjp
def _btn(a, b):
    return _tn(a, b)


_btn.defvjp(lambda a, b: (_tn(a, b), (a, b)), lambda r, g: (_nt(r[1], g), _nn(r[0], g)))


def _silu(x):
    return x * jax.nn.sigmoid(x)


def _b_spec(b, mode, tn, tk, no, ko, jk):
    if mode == "nt":
        return pl.BlockSpec((tn, tk), lambda *g: (jk(*g)[0] + no, jk(*g)[1] + ko))
    return pl.BlockSpec((tk, tn), lambda *g: (jk(*g)[1] + ko, jk(*g)[0] + no))


def _matmul(name, a, b, mode, out_dtype, tm, tn, tk, *, epilogue=None, tiled=(), mrows=(), ncols=(),
            b_noff=0, b_koff=0, n_out=None, out_blocks=None, dest=None):
    if mode == "tn":
        K, M = a.shape
    else:
        M, K = a.shape
    N = n_out if n_out is not None else (b.shape[0] if mode == "nt" else b.shape[1])
    tm, tn, tk = min(tm, M), min(tn, N), min(tk, K)
    assert M % tm == 0 and N % tn == 0 and K % tk == 0, (name, M, N, K, tm, tn, tk)
    assert b_noff % tn == 0 and b_koff % tk == 0
    no, ko = b_noff // tn, b_koff // tk
    nk = K // tk
    if mode == "tn":
        a_spec = pl.BlockSpec((tk, tm), lambda i, j, k: (k, i))
    else:
        a_spec = pl.BlockSpec((tm, tk), lambda i, j, k: (i, k))
    specs = [a_spec, _b_spec(b, mode, tn, tk, no, ko, lambda i, j, k: (j, k))]
    specs += [pl.BlockSpec((tm, tn), lambda i, j, k: (i, j)) for _ in tiled]
    specs += [pl.BlockSpec((tm, r.shape[1]), lambda i, j, k: (i, 0)) for r in mrows]
    specs += [pl.BlockSpec((1, tn), lambda i, j, k: (0, j)) for _ in ncols]
    total, off, earlier = dest if dest is not None else (None, 0, None)
    if out_blocks is None:
        assert off % tm == 0
        mo = off // tm
        out_shape = jax.ShapeDtypeStruct((M if total is None else total, N), out_dtype)
        out_spec = pl.BlockSpec((tm, tn), lambda i, j, k: (i + mo, j))
    else:
        nper = N // out_blocks
        assert nper % tn == 0
        jb = nper // tn
        out_shape = jax.ShapeDtypeStruct((out_blocks if total is None else total, M, nper), out_dtype)
        out_spec = pl.BlockSpec((None, tm, tn), lambda i, j, k: (j // jb + off, i, j % jb))
    if earlier is not None:
        assert earlier.shape == out_shape.shape and earlier.dtype == out_shape.dtype
    ne = len(tiled) + len(mrows) + len(ncols)
    dot = {"nn": _nn, "nt": _nt, "tn": _tn}[mode]

    def body(a_ref, b_ref, *rest):
        extras, o_ref = rest[:ne], rest[ne]

        def finish(acc):
            if epilogue is not None:
                acc = epilogue(acc, *[e[...] for e in extras])
            o_ref[...] = acc.astype(o_ref.dtype)

        if nk == 1:
            finish(dot(a_ref[...], b_ref[...]))
        else:
            acc_ref = rest[ne + 1]
            k = pl.program_id(2)

            @pl.when(k == 0)
            def _():
                acc_ref[...] = jnp.zeros_like(acc_ref)

            acc_ref[...] += dot(a_ref[...], b_ref[...])

            @pl.when(k == nk - 1)
            def _():
                finish(acc_ref[...])

    args = [a, b, *tiled, *mrows, *ncols]
    aliases = {}
    if earlier is not None:
        specs.append(pl.BlockSpec(memory_space=pl.ANY))
        aliases = {len(args): 0}
        args.append(earlier)

    def body_with_dest(*refs):
        body(*refs[:2 + ne], *refs[2 + ne + (earlier is not None):])

    return pl.pallas_call(
        body_with_dest, name=name, out_shape=out_shape, grid=(M // tm, N // tn, nk),
        in_specs=specs, out_specs=out_spec, input_output_aliases=aliases,
        scratch_shapes=[] if nk == 1 else [pltpu.VMEM((tm, tn), F32)],
        compiler_params=_params(("parallel", "parallel", "arbitrary"), VMEM_BIG),
    )(*args)


def _matmul_rows(name, a, b, mode, tm, tk, fn, rows, consts, outs, accs, *, n_out=None, b_noff=0, b_koff=0):
    rl = [(t, t.shape[1], 0) if not isinstance(t, tuple) else t for t in rows]
    make_a = a if callable(a) else None
    M, K = (rl[0][0].shape[0], b.shape[1 if mode == "nt" else 0]) if make_a else a.shape
    N = n_out if n_out is not None else (b.shape[0] if mode == "nt" else b.shape[1])
    tm, tk = min(tm, M), min(tk, K)
    assert M % tm == 0 and K % tk == 0 and b_koff % tk == 0 and b_noff % N == 0, (name, M, N, K)
    no, ko, nk = b_noff // N, b_koff // tk, K // tk
    assert make_a is None or nk == 1
    nr, nc, no_, na = len(rl), len(consts), len(outs), len(accs)
    dot = _nt if mode == "nt" else _nn

    def body(*refs):
        a_ref, b_ref, rest = (None, refs[0], refs[1:]) if make_a else (refs[0], refs[1], refs[2:])
        r_refs, c_refs = rest[:nr], rest[nr:nr + nc]
        o_refs, acc_refs = rest[nr + nc:nr + nc + no_], rest[nr + nc + no_:nr + nc + no_ + na]
        i, k = pl.program_id(0), pl.program_id(1)

        def finish(prod, *made):
            res_o, res_a = fn(prod, *made, *[r[...] for r in r_refs], *[c[...] for c in c_refs])
            for r, v in zip(o_refs, res_o, strict=True):
                r[...] = v.astype(r.dtype)
            if acc_refs:
                @pl.when(i == 0)
                def _():
                    for r in acc_refs:
                        r[...] = jnp.zeros_like(r)

                for r, v in zip(acc_refs, res_a, strict=True):
                    r[...] += v

        if make_a:
            left = make_a(*[r[...] for r in r_refs], *[c[...] for c in c_refs])
            finish(dot(left, b_ref[...]), left)
        elif nk == 1:
            finish(dot(a_ref[...], b_ref[...]))
        else:
            prod_ref = rest[-1]

            @pl.when(k == 0)
            def _():
                prod_ref[...] = jnp.zeros_like(prod_ref)

            prod_ref[...] += dot(a_ref[...], b_ref[...])

            @pl.when(k == nk - 1)
            def _():
                finish(prod_ref[...])

    b_spec = _b_spec(b, mode, N, tk, no, ko, lambda i, k: (0, k))
    in_specs = ([] if make_a else [pl.BlockSpec((tm, tk), lambda i, k: (i, k))]) + [b_spec]
    in_specs += [pl.BlockSpec((tm, w), functools.partial(lambda i, k, cb: (i, cb), cb=cb)) for (_, w, cb) in rl]
    in_specs += [pl.BlockSpec(c.shape, lambda i, k: (0, 0)) for c in consts]
    out_specs = [pl.BlockSpec((tm, c), lambda i, k: (i, 0)) for (c, _) in outs]
    out_specs += [pl.BlockSpec(shp, lambda i, k: (0, 0)) for shp in accs]
    out_shape = [jax.ShapeDtypeStruct((M, c), dt) for (c, dt) in outs] + [jax.ShapeDtypeStruct(shp, F32) for shp in accs]
    res = pl.pallas_call(
        body, name=name, out_shape=out_shape, grid=(M // tm, nk), in_specs=in_specs, out_specs=out_specs,
        scratch_shapes=[] if nk == 1 else [pltpu.VMEM((tm, N), F32)],
        compiler_params=_params(("arbitrary" if accs else "parallel", "arbitrary"), VMEM_BIG),
    )(*([] if make_a else [a]), b, *[t[0] for t in rl], *consts)
    return res[:no_], res[no_:]


def _rowwise(name, fn, tiled, consts, outs, accs, ts):
    tl = [(t, t.shape[1], 0) if not isinstance(t, tuple) else t for t in tiled]
    s_len = tl[0][0].shape[0]
    assert s_len % ts == 0
    nt_, nc_, no_ = len(tl), len(consts), len(outs)

    def body(*refs):
        t_refs, c_refs = refs[:nt_], refs[nt_:nt_ + nc_]
        o_refs, a_refs = refs[nt_ + nc_:nt_ + nc_ + no_], refs[nt_ + nc_ + no_:]
        res_o, res_a = fn(*[r[...] for r in t_refs], *[r[...] for r in c_refs])
        for r, v in zip(o_refs, res_o, strict=True):
            r[...] = v.astype(r.dtype)
        if a_refs:
            @pl.when(pl.program_id(0) == 0)
            def _():
                for r in a_refs:
                    r[...] = jnp.zeros_like(r)

            for r, v in zip(a_refs, res_a, strict=True):
                r[...] += v

    in_specs = [pl.BlockSpec((ts, w), functools.partial(lambda i, cb: (i, cb), cb=cb)) for (_, w, cb) in tl]
    in_specs += [pl.BlockSpec(c.shape, lambda i: (0, 0)) for c in consts]
    out_specs = [pl.BlockSpec((ts, c), lambda i: (i, 0)) for (c, _) in outs]
    out_specs += [pl.BlockSpec(shp, lambda i: (0, 0)) for shp in accs]
    out_shape = [jax.ShapeDtypeStruct((s_len, c), dt) for (c, dt) in outs]
    out_shape += [jax.ShapeDtypeStruct(shp, F32) for shp in accs]
    res = pl.pallas_call(
        body, name=name, out_shape=out_shape, grid=(s_len // ts,), in_specs=in_specs, out_specs=out_specs,
        compiler_params=_params(("arbitrary",) if accs else ("parallel",), VMEM_BIG),
    )(*[t[0] for t in tl], *consts)
    return res[:no_], res[no_:]


def _norm_mod_fn(x, nw, sc, sh):
    r = lax.rsqrt(jnp.mean(x * x, axis=-1, keepdims=True) + NORM_EPS)
    return (x * r * nw) * (1.0 + sc) + sh


def _norm_mod_fwd(name, x, nw, sc, sh):
    (hn,), _ = _rowwise(name, lambda x, nw, sc, sh: ([_norm_mod_fn(x, nw, sc, sh)], []),
                        [x], [nw, sc, sh], [(x.shape[1], BF16)], [], 1024)
    return hn


def _norm_mod_bwd(name, last, x, dhn_parts, dres, nw, sc, sh, prev=None):
    n = len(dhn_parts)
    d = x.shape[1]
    a, b, mode, tk, kw = last

    def fn(dhn, x, *rest):
        for p in rest[:n]:
            dhn = dhn + p
        dres, rest = rest[n], rest[n + 1:]
        y_prev, (nw, sc, sh), gate = (rest[0], rest[1:4], rest[4]) if prev is not None else (None, rest[0:3], None)
        r = lax.rsqrt(jnp.mean(x * x, axis=-1, keepdims=True) + NORM_EPS)
        xh = x * r
        dxh = dhn * (nw * (1.0 + sc))
        dx = r * (dxh - xh * jnp.mean(dxh * xh, axis=-1, keepdims=True)) + dres
        along = jnp.sum(dhn * xh, axis=0, keepdims=True)
        dnw, dsc, dsh = along * (1.0 + sc), along * nw, jnp.sum(dhn, axis=0, keepdims=True)
        if prev is None:
            return [dx], [dnw, dsc, dsh]
        return [dx, gate * dx], [dnw, dsc, dsh, jnp.sum(dx * y_prev, axis=0, keepdims=True)]

    rows = [x, *dhn_parts, dres] + ([prev[0]] if prev is not None else [])
    consts = [nw, sc, sh] + ([prev[1]] if prev is not None else [])
    outs = [(d, F32)] + ([(d, BF16)] if prev is not None else [])
    res_o, res_a = _matmul_rows(name, a, b, mode, 512, tk, fn, rows, consts, outs, [(1, d)] * (3 + (prev is not None)), **kw)
    return (*res_o, *res_a)


def _rope_tables(pos_col, inv_row):
    def fn(pos, inv):
        ang = pos.astype(F32) * inv
        e = lax.broadcasted_iota(jnp.int32, (1, 128), 1) % HEAD_DIM
        cos, sin = jnp.cos(ang), jnp.sin(ang)
        half = ROT_DIM // 2
        return [jnp.where(e < ROT_DIM, cos, 1.0), jnp.where(e < half, -sin, 0.0),
                jnp.where((e >= half) & (e < ROT_DIM), sin, 0.0)], []

    (c, sa, sb), _ = _rowwise("rope_tables", fn, [pos_col], [inv_row], [(128, F32)] * 3, [], 512)
    return c, sa, sb


def _rot_fwd(t, c, sa, sb):
    n = t.shape[1]
    rep = n // 128
    c, sa, sb = (jnp.tile(u, (1, rep)) for u in (c, sa, sb))
    return t * c + pltpu.roll(t, n - ROT_DIM // 2, 1) * sa + pltpu.roll(t, ROT_DIM // 2, 1) * sb


def _rot_bwd(g, c, sa, sb):
    n = g.shape[1]
    rep = n // 128
    c, sa, sb = (jnp.tile(u, (1, rep)) for u in (c, sa, sb))
    return g * c + pltpu.roll(g * sa, ROT_DIM // 2, 1) + pltpu.roll(g * sb, n - ROT_DIM // 2, 1)


ATT_TQ = 128


def _attn_tiles(l):
    tk = ATT_TQ + 2 * BAND
    return (l, l) if l <= tk else (ATT_TQ, tk)


def _attn_specs(g, s_len):
    def blk(off):
        return pl.BlockSpec((s_len, 128), functools.partial(lambda hp, off: (0, off + hp), off=off))

    return blk(4 * g), blk(12 + 4 * g), blk(4 * g), blk(0)


def _attn_tile_geometry(t, d, l):
    tq, tk = _attn_tiles(l)
    nts = l // tq
    r = t // nts
    ts = t % nts
    q0 = ts * tq
    ws = jnp.clip(q0 - BAND, 0, l - tk)
    kind = jnp.where(ts == 0, 0, jnp.where(ts == nts - 1, 2, 1))
    if d == 1:
        return pl.ds(pl.multiple_of(q0, tq), tq), pl.ds(pl.multiple_of(ws, BAND), tk), kind
    return pl.ds(r + d * q0, tq, stride=d), pl.ds(r + d * ws, tk, stride=d), kind


def _attn_fill_bias(bias_ref):
    _, tq2, tk = bias_ref.shape
    iq = lax.broadcasted_iota(jnp.int32, (tq2, 1), 0) % (tq2 // 2)
    ik = lax.broadcasted_iota(jnp.int32, (1, tk), 1)
    for i, off in enumerate((0, -BAND, -2 * BAND)):
        bias_ref[i] = jnp.where(jnp.abs(ik + off - iq) <= BAND, 0.0, NEG_BIG)


def _split_heads(t, in_h):
    zero = jnp.zeros_like(t)
    return jnp.concatenate([jnp.where(in_h[0], t, zero), jnp.where(in_h[1], t, zero)], axis=0)


def _attn_fwd(g, qk, v):
    s_len = qk.shape[0]
    d = DILATIONS[g]
    l = s_len // d
    tq, tk = _attn_tiles(l)
    assert l % tq == 0 and l >= tk
    q_spec, k_spec, v_spec, o_spec = _attn_specs(g, s_len)
    scale = 1.0 / math.sqrt(HEAD_DIM)

    def body(q_ref, k_ref, v_ref, o_ref, lse_ref, bias_ref):
        lane = lax.broadcasted_iota(jnp.int32, (1, 128), 1)
        in_h = [lane < HEAD_DIM, lane >= HEAD_DIM]
        _attn_fill_bias(bias_ref)

        def tile(t, carry):
            rows, win, kind = _attn_tile_geometry(t, d, l)
            q = (q_ref[rows, :] * scale).astype(BF16)
            k = k_ref[win, :].astype(BF16)
            vv = v_ref[win, :].astype(BF16)
            s = _nt(_split_heads(q, in_h), k) + bias_ref[kind]
            m = jnp.max(s, axis=1, keepdims=True)
            p = jnp.exp(s - m)
            den = jnp.sum(p, axis=1, keepdims=True)
            out = _nn(p, vv) / den
            lse = m + jnp.log(den)
            o_ref[rows, :] = jnp.where(in_h[0], out[:tq], out[tq:])
            lse_ref[rows, :] = jnp.where(in_h[0], lse[:tq], lse[tq:])
            return carry

        lax.fori_loop(0, s_len // tq, tile, 0, unroll=8 * ATT_TQ // tq)

    return pl.pallas_call(
        body, name=f"attn_fwd_g{g}", grid=(4,),
        out_shape=[jax.ShapeDtypeStruct((s_len, 512), F32)] * 2,
        in_specs=[q_spec, k_spec, v_spec], out_specs=[o_spec, o_spec],
        scratch_shapes=[pltpu.VMEM((3, 2 * tq, tk), F32)],
        compiler_params=_params(("parallel",), VMEM_BIG),
    )(qk, qk, v)


def _attn_bwd(g, qk, v, o, lse, do, dlse):
    s_len = qk.shape[0]
    d = DILATIONS[g]
    l = s_len // d
    tq, tk = _attn_tiles(l)
    q_spec, k_spec, v_spec, o_spec = _attn_specs(g, s_len)
    scale = 1.0 / math.sqrt(HEAD_DIM)

    def body(q_ref, k_ref, v_ref, o_ref, lse_ref, do_ref, dlse_ref, dq_ref, dk_ref, dv_ref, bias_ref):
        lane = lax.broadcasted_iota(jnp.int32, (1, 128), 1)
        in_h = [lane < HEAD_DIM, lane >= HEAD_DIM]
        dk_ref[...] = jnp.zeros_like(dk_ref)
        dv_ref[...] = jnp.zeros_like(dv_ref)
        _attn_fill_bias(bias_ref)

        def tile(t, carry):
            rows, win, kind = _attn_tile_geometry(t, d, l)
            k, vv = k_ref[win, :].astype(BF16), v_ref[win, :].astype(BF16)
            dout, lse_t, dlse_t = do_ref[rows, :], lse_ref[rows, :], dlse_ref[rows, :]
            od = dout * o_ref[rows, :]
            q2 = _split_heads((q_ref[rows, :] * scale).astype(BF16), in_h)
            do2 = _split_heads(dout.astype(BF16), in_h)
            head_col = lambda a: jnp.concatenate([a[:, 0:1], a[:, HEAD_DIM:HEAD_DIM + 1]], axis=0)
            delta = jnp.concatenate([jnp.sum(jnp.where(m, od, 0.0), axis=1, keepdims=True) for m in in_h], axis=0)
            p = jnp.exp(_nt(q2, k) + bias_ref[kind] - head_col(lse_t))
            ds = (p * (_nt(do2, vv) - delta + head_col(dlse_t))).astype(BF16)
            dq2 = _nn(ds, k) * scale
            dq_ref[rows, :] = jnp.where(in_h[0], dq2[:tq], dq2[tq:])
            dk_ref[win, :] += _tn(ds, q2)
            dv_ref[win, :] += _tn(p, do2)
            return carry

        lax.fori_loop(0, s_len // tq, tile, 0, unroll=8 * ATT_TQ // tq)

    return pl.pallas_call(
        body, name=f"attn_bwd_g{g}", grid=(4,),
        out_shape=[jax.ShapeDtypeStruct((s_len, 512), F32)] * 3,
        in_specs=[q_spec, k_spec, v_spec, o_spec, o_spec, o_spec, o_spec], out_specs=[o_spec] * 3,
        scratch_shapes=[pltpu.VMEM((3, 2 * tq, tk), F32)],
        compiler_params=_params(("parallel",), VMEM_BIG),
    )(qk, qk, v, o, lse, do, dlse)


def _mix_weights(ls):
    mx = jnp.maximum(jnp.maximum(ls[0], ls[1]), ls[2])
    es = [jnp.exp(x - mx) for x in ls]
    tot = es[0] + es[1] + es[2]
    return [e / tot for e in es]


def _attn_out(os_, lses, z, x, gate, w_out):
    s_len, dm = x.shape
    tm = 256
    wdt = 512
    z, z_block = z

    def body(o0, o1, o2, l0, l1, l2, z_ref, x_ref, g_ref, w_ref, a_ref, y_ref, x1_ref):
        alphas = _mix_weights([l0[...], l1[...], l2[...]])
        y = jnp.zeros((tm, dm), F32)
        for g, o_ref in enumerate((o0, o1, o2)):
            a_g = (o_ref[...] * alphas[g] * _silu(z_ref[:, g * wdt:(g + 1) * wdt])).astype(BF16)
            a_ref[:, g * wdt:(g + 1) * wdt] = a_g
            y = y + _nn(a_g, w_ref[g * wdt:(g + 1) * wdt, :])
        y_ref[...] = y
        x1_ref[...] = x_ref[...] + g_ref[...] * y

    row = lambda c: pl.BlockSpec((tm, c), lambda i: (i, 0))
    return pl.pallas_call(
        body, name="attn_out", grid=(s_len // tm,),
        out_shape=[jax.ShapeDtypeStruct((s_len, 3 * wdt), BF16), jax.ShapeDtypeStruct((s_len, dm), F32),
                   jax.ShapeDtypeStruct((s_len, dm), F32)],
        in_specs=[row(wdt)] * 6 + [pl.BlockSpec((tm, 3 * wdt), lambda i: (i, z_block)), row(dm),
                                   pl.BlockSpec((1, dm), lambda i: (0, 0)), pl.BlockSpec(w_out.shape, lambda i: (0, 0))],
        out_specs=[row(3 * wdt), row(dm), row(dm)],
        compiler_params=_params(("parallel",), VMEM_BIG),
    )(*os_, *lses, z, x, gate, w_out)


def _mix_bwd(dy, w_out, os_, lses, z):
    wdt = 512

    def fn(da, o0, o1, o2, l0, l1, l2, z):
        os_t, ls = [o0, o1, o2], [l0, l1, l2]
        alphas = _mix_weights(ls)
        hi = lax.broadcasted_iota(jnp.int32, (2 * wdt, wdt), 0) % wdt // HEAD_DIM
        hj = lax.broadcasted_iota(jnp.int32, (2 * wdt, wdt), 1) // HEAD_DIM
        seg = (hi == hj).astype(BF16)
        head_sum = lambda t: _dg(jnp.concatenate(_bf16_parts(t, 2), axis=1), seg, 1, 0)
        dos, dal, dzs = [], [], []
        for g in range(3):
            zg = z[:, g * wdt:(g + 1) * wdt]
            sig = jax.nn.sigmoid(zg)
            dag = da[:, g * wdt:(g + 1) * wdt]
            dmix = dag * zg * sig
            dzs.append(dag * os_t[g] * alphas[g] * (sig * (1.0 + zg * (1.0 - sig))))
            dos.append(dmix * alphas[g])
            dal.append(head_sum(dmix * os_t[g]))
        mean = alphas[0] * dal[0] + alphas[1] * dal[1] + alphas[2] * dal[2]
        dls = [alphas[g] * (dal[g] - mean) for g in range(3)]
        return dos + dls + [jnp.concatenate(dzs, axis=1)], []

    outs, _ = _matmul_rows("attn_out_dx_mix_bwd", dy, w_out, "nt", 256, dy.shape[1], fn, [*os_, *lses, (z[0], 3 * wdt, z[1])], [],
                           [(wdt, F32)] * 6 + [(3 * wdt, BF16)], [])
    return outs[:3], outs[3:6], outs[6]


def _rot_pack_bwd(dqs, dks, dvs, tabs):
    wdt = 512

    def fn(*args):
        grads, (c, sa, sb) = args[:9], args[9:]
        cols = [_rot_bwd(gq, c, sa, sb) for gq in grads[:6]] + list(grads[6:])
        return [jnp.concatenate(cols, axis=1)], []

    (out,), _ = _rowwise("rot_pack_bwd", fn, [*dqs, *dks, *dvs, *tabs], [], [(9 * wdt, BF16)], [], 512)
    return out


CONV_CB = 128
CONV_R = 256
CONV_PAD = 8


def _conv_taps(buf, base, off, sign):
    return [buf[pl.ds(base + off + sign * j, CONV_R), :] for j in range(CONV_WIDTH)]


def _conv_tap_sum(taps, w):
    acc = None
    for j, t in enumerate(taps):
        term = t * w[j:j + 1, :]
        acc = term if acc is None else acc + term
    return acc


def _conv_fwd(xpre, cw, cb):
    s_len, ch = xpre.shape
    nchunk = s_len // CONV_R

    def body(x_ref, w_ref, b_ref, o_ref, xp):
        zero = jnp.zeros((CONV_PAD, CONV_CB), F32)
        xp[0:CONV_PAD, :] = zero
        xp[s_len + CONV_PAD:s_len + 2 * CONV_PAD, :] = zero

        def fill(ci, carry):
            base = pl.multiple_of(ci * CONV_R, CONV_R)
            xp[pl.ds(base + CONV_PAD, CONV_R), :] = x_ref[pl.ds(base, CONV_R), :]
            return carry

        lax.fori_loop(0, nchunk, fill, 0)
        w = w_ref[...]
        b = b_ref[...]

        def chunk(ci, carry):
            base = pl.multiple_of(ci * CONV_R, CONV_R)
            u = _conv_tap_sum(_conv_taps(xp, base, CONV_PAD - CONV_WIDTH // 2, 1), w) + b
            o_ref[pl.ds(base, CONV_R), :] = _silu(u)
            return carry

        lax.fori_loop(0, nchunk, chunk, 0, unroll=2)

    col = lambda r: pl.BlockSpec((r, CONV_CB), lambda j: (0, j))
    return pl.pallas_call(
        body, name="conv_fwd", grid=(ch // CONV_CB,), out_shape=jax.ShapeDtypeStruct((s_len, ch), F32),
        in_specs=[col(s_len), col(CONV_WIDTH), col(1)], out_specs=col(s_len),
        scratch_shapes=[pltpu.VMEM((s_len + 2 * CONV_PAD, CONV_CB), F32)],
        compiler_params=_params(("parallel",), VMEM_BIG),
    )(xpre, cw, cb)


def _conv_bwd(xpre, da, cw, cb):
    s_len, ch = xpre.shape
    nchunk = s_len // CONV_R
    half = CONV_WIDTH // 2

    def body(x_ref, da_ref, w_ref, b_ref, dx_ref, gw_ref, gb_ref, xp, dcp):
        zero = jnp.zeros((CONV_PAD, CONV_CB), F32)
        for buf in (xp, dcp):
            buf[0:CONV_PAD, :] = zero
            buf[s_len + CONV_PAD:s_len + 2 * CONV_PAD, :] = zero

        def fill(ci, carry):
            base = pl.multiple_of(ci * CONV_R, CONV_R)
            xp[pl.ds(base + CONV_PAD, CONV_R), :] = x_ref[pl.ds(base, CONV_R), :]
            return carry

        lax.fori_loop(0, nchunk, fill, 0)
        w = w_ref[...]
        b = b_ref[...]

        def first(ci, carry):
            base = pl.multiple_of(ci * CONV_R, CONV_R)
            taps = _conv_taps(xp, base, CONV_PAD - half, 1)
            u = _conv_tap_sum(taps, w) + b
            sig = jax.nn.sigmoid(u)
            dc = da_ref[pl.ds(base, CONV_R), :] * (sig * (1.0 + u * (1.0 - sig)))
            dcp[pl.ds(base + CONV_PAD, CONV_R), :] = dc
            gb = carry[0] + jnp.sum(dc, axis=0, keepdims=True)
            gws = [carry[1 + j] + jnp.sum(dc * taps[j], axis=0, keepdims=True) for j in range(CONV_WIDTH)]
            return (gb, *gws)

        z1 = jnp.zeros((1, CONV_CB), F32)
        sums = lax.fori_loop(0, nchunk, first, (z1,) * (1 + CONV_WIDTH), unroll=2)
        gb_ref[...] = sums[0]
        for j in range(CONV_WIDTH):
            gw_ref[j:j + 1, :] = sums[1 + j]

        def second(ci, carry):
            base = pl.multiple_of(ci * CONV_R, CONV_R)
            dx_ref[pl.ds(base, CONV_R), :] = _conv_tap_sum(_conv_taps(dcp, base, CONV_PAD + half, -1), w).astype(dx_ref.dtype)
            return carry

        lax.fori_loop(0, nchunk, second, 0, unroll=2)

    col = lambda r: pl.BlockSpec((r, CONV_CB), lambda j: (0, j))
    return pl.pallas_call(
        body, name="conv_bwd", grid=(ch // CONV_CB,),
        out_shape=[jax.ShapeDtypeStruct((s_len, ch), BF16), jax.ShapeDtypeStruct((CONV_WIDTH, ch), F32),
                   jax.ShapeDtypeStruct((1, ch), F32)],
        in_specs=[col(s_len), col(s_len), col(CONV_WIDTH), col(1)],
        out_specs=[col(s_len), col(CONV_WIDTH), col(1)],
        scratch_shapes=[pltpu.VMEM((s_len + 2 * CONV_PAD, CONV_CB), F32)] * 2,
        compiler_params=_params(("parallel",), VMEM_BIG),
    )(xpre, da, cw, cb)


SSD_GW = 256
SSD_N = 128
SSD_DTW = 128


def _bf16_parts(x, n):
    parts, rest = [], x
    for _ in range(n):
        p = rest.astype(BF16)
        parts.append(p)
        rest = rest - p.astype(F32)
    return parts


@jax.custom_vjp
def _expand(x, e):
    eb = e.astype(BF16)
    return _dg(jnp.concatenate(_bf16_parts(x, 2), axis=1), jnp.concatenate([eb, eb], axis=0), 1, 0)


def _expand_fwd(x, e):
    return _expand(x, e), e


def _expand_bwd(e, g):
    return _dg(g.astype(BF16), e.astype(BF16), 1, 1), jnp.zeros_like(e)


_expand.defvjp(_expand_fwd, _expand_bwd)


@jax.custom_vjp
def _running_sum(tri, x):
    tb = tri.astype(BF16)
    return sum(_dg(tb, p, 1, 0) for p in _bf16_parts(x, 3))


def _running_sum_fwd(tri, x):
    return _running_sum(tri, x), tri


def _running_sum_bwd(tri, g):
    tb = tri.astype(BF16)
    return jnp.zeros_like(tri), sum(_dg(tb, p, 0, 0) for p in _bf16_parts(g, 3))


_running_sum.defvjp(_running_sum_fwd, _running_sum_bwd)


def _pick_col(a, h):
    @jax.custom_vjp
    def pick(a):
        return a[:, h:h + 1]

    pick.defvjp(lambda a: (a[:, h:h + 1], None),
                lambda _, g: (g * (lax.broadcasted_iota(jnp.int32, (1, a.shape[1]), 1) == h).astype(F32),))
    return pick(a)


def _pick_row(a, h):
    @jax.custom_vjp
    def pick(a):
        return a[h:h + 1, :]

    pick.defvjp(lambda a: (a[h:h + 1, :], None),
                lambda _, g: (g * (lax.broadcasted_iota(jnp.int32, (a.shape[0], 1), 0) == h).astype(F32),))
    return pick(a)


def _ssd_mask(dirn):
    ri = lax.broadcasted_iota(jnp.int32, (CHUNK, CHUNK), 0)
    cj = lax.broadcasted_iota(jnp.int32, (CHUNK, CHUNK), 1)
    return (cj <= ri) if dirn == 0 else (cj >= ri)


def _ssd_rowsel(dirn):
    last = CHUNK - 1 if dirn == 0 else 0
    return (lax.broadcasted_iota(jnp.int32, (CHUNK, 1), 0) == last).astype(F32)


def _ssd_chunk_pre(dirn):
    nh = SSD_DTW

    def f(dt, alog):
        da = dt * (-jnp.exp(alog))
        cum = _running_sum(_ssd_mask(dirn).astype(F32), da)
        tot = jnp.sum(cum * _ssd_rowsel(dirn), axis=0, keepdims=True)
        hh = lax.broadcasted_iota(jnp.int32, (nh, SSD_HEADS * HEAD_DIM), 0)
        jj = lax.broadcasted_iota(jnp.int32, (nh, SSD_HEADS * HEAD_DIM), 1)
        expand = (hh == dirn * SSD_HEADS + jj // HEAD_DIM).astype(F32)
        return cum, cum.T, _expand(dt, expand), _expand(jnp.exp(tot - cum), expand), _expand(jnp.exp(cum), expand)

    return f


def _ssd_group_fn(g, dirn, stacked):
    def f(xs, bm, cm, st, cum, cum_t, dt_e, w_e, ce_e):
        mask = _ssd_mask(dirn)
        xdt = xs * dt_e
        cd_e = jnp.sum(ce_e * _ssd_rowsel(dirn), axis=0, keepdims=True)
        cb = _bnt(cm, bm)
        lane_head = lax.broadcasted_iota(jnp.int32, (1, SSD_GW), 1) // HEAD_DIM
        y = _bnn(cm, st) * ce_e
        decayed, inputs = [], []
        for j in range(4):
            hidx = dirn * SSD_HEADS + 4 * g + j
            col, row = _pick_col(cum, hidx), _pick_row(cum_t, hidx)
            dec = cb * jnp.exp(jnp.where(mask, col - row, NEG_BIG))
            head = (lane_head == j).astype(F32)
            if stacked:
                decayed.append(dec)
                inputs.append(xdt * head)
            else:
                y = y + _bnn(dec, xdt) * head
        if stacked:
            y = y + _bnn(jnp.concatenate(decayed, axis=1), jnp.concatenate(inputs, axis=0))
        st_out = st * cd_e + _btn(bm, xdt * w_e)
        return y, st_out

    return f


def _ssd_in_specs(kk):
    ln = CHUNK
    return [pl.BlockSpec((ln, 2048), lambda i: (kk(i), 0)),
            pl.BlockSpec((ln, 1024), lambda i: (kk(i), 2)),
            pl.BlockSpec((ln, 1024), lambda i: (kk(i), 3)),
            pl.BlockSpec((ln, SSD_DTW), lambda i: (kk(i), 0)),
            pl.BlockSpec((1, SSD_DTW), lambda i: (0, 0))]


def _ssd_fwd(xbc, dt, alog, dirn, prior=None):
    s_len = xbc.shape[0]
    nc = s_len // CHUNK
    kk = (lambda i: i) if dirn == 0 else (lambda i: nc - 1 - i)

    def body(x_ref, b_ref, c_ref, dt_ref, al_ref, *rest):
        prior_ref = rest[0] if prior is not None else None
        y_ref, sts_ref, st = rest[prior is not None:]

        @pl.when(pl.program_id(0) == 0)
        def _():
            st[...] = jnp.zeros_like(st)

        sts_ref[0] = st[...]
        cum, cum_t, dt_e, w_e, ce_e = _ssd_chunk_pre(dirn)(dt_ref[...], al_ref[...])
        for g in range(SSD_GROUPS):
            xc = slice(g * SSD_GW, (g + 1) * SSD_GW)
            gc = slice(g * SSD_N, (g + 1) * SSD_N)
            y, st_new = _ssd_group_fn(g, dirn, True)(x_ref[:, xc], b_ref[:, gc], c_ref[:, gc], st[:, xc], cum, cum_t,
                                               dt_e[:, xc], w_e[:, xc], ce_e[:, xc])
            y_ref[:, xc] = y if prior is None else y + prior_ref[:, xc]
            st[:, xc] = st_new

    return pl.pallas_call(
        body, name=f"ssd_fwd_d{dirn}", grid=(nc,),
        out_shape=[jax.ShapeDtypeStruct((s_len, 2048), F32), jax.ShapeDtypeStruct((nc, SSD_N, 2048), F32)],
        in_specs=_ssd_in_specs(kk) + ([pl.BlockSpec((CHUNK, 2048), lambda i: (kk(i), 0))] if prior is not None else []),
        out_specs=[pl.BlockSpec((CHUNK, 2048), lambda i: (kk(i), 0)),
                   pl.BlockSpec((1, SSD_N, 2048), lambda i: (kk(i), 0, 0))],
        scratch_shapes=[pltpu.VMEM((SSD_N, 2048), F32)],
        compiler_params=_params(("arbitrary",), VMEM_BIG),
    )(xbc, xbc, xbc, dt, alog, *([prior] if prior is not None else []))


def _ssd_bwd(xbc, dt, alog, states, dy, d_e, dirn, prior=None):
    s_len = xbc.shape[0]
    nc = s_len // CHUNK
    kk = (lambda i: nc - 1 - i) if dirn == 0 else (lambda i: i)

    def body(x_ref, b_ref, c_ref, dt_ref, al_ref, sts_ref, dy_ref, de_ref, *rest):
        prior_ref = rest[0] if prior is not None else None
        dx_ref, ddt_ref, dal_ref, dst = rest[prior is not None:]
        plus_prior = (lambda v, cols: v + prior_ref[:, cols]) if prior is not None else (lambda v, cols: v)

        @pl.when(pl.program_id(0) == 0)
        def _():
            dst[...] = jnp.zeros_like(dst)
            dal_ref[...] = jnp.zeros_like(dal_ref)

        (cum, cum_t, dt_e, w_e, ce_e), pre_vjp = jax.vjp(_ssd_chunk_pre(dirn), dt_ref[...], al_ref[...])
        dcum = jnp.zeros_like(cum)
        dcum_t = jnp.zeros_like(cum_t)
        d_dt_e, d_w_e, d_ce_e = [], [], []
        for g in range(SSD_GROUPS):
            xc = slice(g * SSD_GW, (g + 1) * SSD_GW)
            gc = slice(g * SSD_N, (g + 1) * SSD_N)
            _, vjp = jax.vjp(_ssd_group_fn(g, dirn, False), x_ref[:, xc], b_ref[:, gc], c_ref[:, gc], sts_ref[0, :, xc], cum, cum_t,
                             dt_e[:, xc], w_e[:, xc], ce_e[:, xc])
            dyg = dy_ref[:, xc]
            dxs, dbm, dcm, dst_g, dcum_g, dcum_t_g, ddte_g, dwe_g, dcee_g = vjp((dyg, dst[:, xc]))
            if dirn == 0:
                dxs = dxs + dyg * de_ref[:, xc]
            bc, cc = slice(2048 + g * SSD_N, 2048 + (g + 1) * SSD_N), slice(3072 + g * SSD_N, 3072 + (g + 1) * SSD_N)
            dx_ref[:, xc] = plus_prior(dxs, xc)
            dx_ref[:, bc] = plus_prior(dbm, bc)
            dx_ref[:, cc] = plus_prior(dcm, cc)
            dst[:, xc] = dst_g
            dcum = dcum + dcum_g
            dcum_t = dcum_t + dcum_t_g
            d_dt_e.append(ddte_g)
            d_w_e.append(dwe_g)
            d_ce_e.append(dcee_g)
        ddt, dal = pre_vjp((dcum, dcum_t, jnp.concatenate(d_dt_e, axis=1), jnp.concatenate(d_w_e, axis=1),
                            jnp.concatenate(d_ce_e, axis=1)))
        ddt_ref[...] = ddt
        dal_ref[...] += dal

    return pl.pallas_call(
        body, name=f"ssd_bwd_d{dirn}", grid=(nc,),
        out_shape=[jax.ShapeDtypeStruct((s_len, 4096), F32), jax.ShapeDtypeStruct((s_len, SSD_DTW), F32),
                   jax.ShapeDtypeStruct((1, SSD_DTW), F32)],
        in_specs=_ssd_in_specs(kk) + [pl.BlockSpec((1, SSD_N, 2048), lambda i: (kk(i), 0, 0)),
                                      pl.BlockSpec((CHUNK, 2048), lambda i: (kk(i), 0)),
                                      pl.BlockSpec((1, 2048), lambda i: (0, 0))]
        + ([pl.BlockSpec((CHUNK, 4096), lambda i: (kk(i), 0))] if prior is not None else []),
        out_specs=[pl.BlockSpec((CHUNK, 4096), lambda i: (kk(i), 0)),
                   pl.BlockSpec((CHUNK, SSD_DTW), lambda i: (kk(i), 0)),
                   pl.BlockSpec((1, SSD_DTW), lambda i: (0, 0))],
        scratch_shapes=[pltpu.VMEM((SSD_N, 2048), F32)],
        compiler_params=_params(("arbitrary",), VMEM_BIG),
    )(xbc, xbc, xbc, dt, alog, states, dy, d_e, *([prior] if prior is not None else []))


def _gate_norm_fn(y, xs, z, d_e, nw):
    yg = (y + xs * d_e) * _silu(z)
    return yg * lax.rsqrt(jnp.mean(yg * yg, axis=-1, keepdims=True) + NORM_EPS) * nw


def _gate_norm_bwd(dy, w_out, y, xbc, z, d_e, nw):
    def fn(du, y, xs, z, d_e, nw):
        sig = jax.nn.sigmoid(z)
        gate = z * sig
        ysum = y + xs * d_e
        yg = ysum * gate
        r = lax.rsqrt(jnp.mean(yg * yg, axis=-1, keepdims=True) + NORM_EPS)
        t = du * nw
        dyg = t * r - yg * (jnp.mean(t * yg, axis=-1, keepdims=True) * (r * r * r))
        dys = dyg * gate
        dz = dyg * ysum * (sig * (1.0 + z * (1.0 - sig)))
        dnw = jnp.sum(du * yg * r, axis=0, keepdims=True)
        dde = jnp.sum(dys * xs, axis=0, keepdims=True)
        hh = lax.broadcasted_iota(jnp.int32, (2048, SSD_HEADS), 0) // HEAD_DIM
        jj = lax.broadcasted_iota(jnp.int32, (2048, SSD_HEADS), 1)
        return [dys, dz], [dnw, _hnn(jnp.broadcast_to(dde, (8, 2048)), (hh == jj).astype(F32))[0:1]]

    (dys, dz), (g_nw, g_d) = _matmul_rows("ssd_out_dx_gate_norm_bwd", dy, w_out, "nt", 256, dy.shape[1], fn,
                                          [y, (xbc, 2048, 0), z], [d_e, nw], [(2048, F32), (2048, BF16)],
                                          [(1, 2048), (1, SSD_HEADS)])
    return dys, dz, g_nw, g_d


def _ssd_tail_loss(y, xbc, z, d_e, snw, w_out, x1, tgt, gate, fnw):
    dm = x1.shape[1]
    si = y.shape[1]

    def make_u(y, xs, z, x1, tgt, d_e, snw, gate, fnw):
        return _gate_norm_fn(y, xs, z, d_e, snw).astype(BF16)

    def fn(y1, u, y, xs, z, x1, tgt, d_e, snw, gate, fnw):
        x2 = x1 + gate * y1
        r = lax.rsqrt(jnp.mean(x2 * x2, axis=-1, keepdims=True) + NORM_EPS)
        xh = x2 * r
        err = xh * fnw - tgt
        loss = 0.5 * jnp.sum(jnp.mean(err * err, axis=-1, keepdims=True), axis=0, keepdims=True)
        dy = err * (1.0 / dm)
        dxh = dy * fnw
        dx2 = r * (dxh - xh * jnp.mean(dxh * xh, axis=-1, keepdims=True))
        dfnw = jnp.sum(dy * xh, axis=0, keepdims=True)
        return [u, dx2, gate * dx2], [dfnw, jnp.sum(dx2 * y1, axis=0, keepdims=True), jnp.broadcast_to(loss, (1, 128))]

    (u, dx2, dy1), (g_fnw, dgate, loss) = _matmul_rows(
        "ssd_out_loss", make_u, w_out, "nn", 256, si, fn, [y, (xbc, si, 0), z, x1, tgt], [d_e, snw, gate, fnw],
        [(si, BF16), (dm, F32), (dm, BF16)], [(1, dm), (1, dm), (1, 128)])
    return u, dx2, dy1, g_fnw, dgate, loss


def _softplus_fwd(dt_raw, bias):
    (dt,), _ = _rowwise("dt_softplus", lambda r, b: ([jax.nn.softplus(r + b)], []), [dt_raw], [bias],
                        [(dt_raw.shape[1], F32)], [], 512)
    return dt


def _softplus_bwd(ddt_f, ddt_b, dt_raw, bias):
    def fn(df, db, r, b):
        g = (df + db) * jax.nn.sigmoid(r + b)
        return [g], [jnp.sum(g, axis=0, keepdims=True)]

    w = dt_raw.shape[1]
    (g,), (gb,) = _rowwise("dt_softplus_bwd", fn, [ddt_f, ddt_b, dt_raw], [bias], [(w, BF16)], [(1, w)], 512)
    return g, gb


def _mod_part(c_all, mod_w):
    nl, _, ncol = mod_w.shape
    nb = c_all.shape[0]

    def body(c_ref, w_ref, o_ref):
        cond = _silu(c_ref[...])
        for i in range(nl):
            o_ref[i * nb:(i + 1) * nb, :] = _nn(cond, w_ref[i])

    return pl.pallas_call(body, name="mod_part", out_shape=jax.ShapeDtypeStruct((nl * nb, ncol), F32),
                          compiler_params=_params(None, VMEM_BIG))(c_all, mod_w)


def _mod_finish(mod_nb, mod_b, norm_w, tokens):
    nl, dm = norm_w.shape

    def body(a_ref, b_ref, nw_ref, *rest):
        tok_refs, o_refs = rest[:len(tokens)], rest[len(tokens):]
        tok = sum(t[0:1, 0:1] for t in tok_refs)
        for i in range(nl):
            for k in range(3):
                cols = slice(k * dm, (k + 1) * dm)
                o_refs[4 * i + k][...] = a_ref[i:i + 1, cols] + b_ref[i:i + 1, cols]
            o_refs[4 * i + 3][...] = nw_ref[i:i + 1, :] + tok

    rows = pl.pallas_call(body, name="mod_finish", out_shape=[jax.ShapeDtypeStruct((1, dm), F32)] * (4 * nl))(
        mod_nb, mod_b, norm_w, *tokens)
    return [rows[4 * i:4 * i + 4] for i in range(nl)]


def _mod_grad(c_all, dmod_sh):
    nl, nb, ncol = dmod_sh.shape
    dm = c_all.shape[1]

    def body(c_ref, d_ref, o_ref):
        cond = _silu(c_ref[...])
        for i in range(nl):
            o_ref[i] = _tn(cond, d_ref[i])

    return pl.pallas_call(body, name="mod_grad", out_shape=jax.ShapeDtypeStruct((nl, dm, ncol), F32),
                          compiler_params=_params(None, VMEM_BIG))(c_all, dmod_sh)


PACK_ROWS = 16
PACK_COLS = 1024


def _pack_small(rows, b64, a64s, d32, extra):
    nr, na = len(rows), len(a64s)

    def body(*refs):
        o_ref = refs[-1]
        o_ref[...] = jnp.zeros_like(o_ref)
        for i in range(nr):
            o_ref[i:i + 1, :] = refs[i][...]
        b_ref, a_refs, d_ref, e_ref = refs[nr], refs[nr + 1:nr + 1 + na], refs[nr + 1 + na], refs[nr + 2 + na]
        o_ref[nr:nr + 1, 0:64] = b_ref[:, 0:64]
        o_ref[nr:nr + 1, 64:128] = sum(a[:, 0:64] for a in a_refs)
        o_ref[nr:nr + 1, 128:160] = d_ref[...]
        o_ref[nr:nr + 1, 256:384] = e_ref[...]

    return pl.pallas_call(body, name="pack_small", out_shape=jax.ShapeDtypeStruct((PACK_ROWS, PACK_COLS), F32))(
        *rows, b64, *a64s, d32, extra)


def _pack_ssd_small(cw, cb, nw):
    def body(cw_ref, cb_ref, nw_ref, o_ref):
        o_ref[...] = jnp.zeros_like(o_ref)
        o_ref[0:5, :] = cw_ref[...]
        o_ref[5:6, :] = cb_ref[...]
        o_ref[6:7, 0:256] = nw_ref[...]

    return pl.pallas_call(body, name="pack_ssd_small", out_shape=jax.ShapeDtypeStruct((8, 512), F32))(cw, cb, nw)


def _sum_parts(p_ref):
    g = p_ref[0].astype(F32)
    for s in range(1, p_ref.shape[0]):
        g = g + p_ref[s].astype(F32)
    return g


def _adam_update(w, g, m, v):
    m2 = ADAM_B1 * m + (1.0 - ADAM_B1) * g
    v2 = ADAM_B2 * v + (1.0 - ADAM_B2) * (g * g)
    m_hat = m2 / (1.0 - ADAM_B1 ** ADAM_STEP)
    v_hat = v2 / (1.0 - ADAM_B2 ** ADAM_STEP)
    return -ADAM_LR * (m_hat / (jnp.sqrt(v_hat) + ADAM_EPS) + ADAM_WD * w), m2, v2


def _adamw_windows(name, parts, params, windows, extra=None):
    n = len(params)

    def body(p_ref, *rest):
        ins, outs = rest[:3 * n], rest[3 * n:]
        g = _sum_parts(p_ref)
        for pi, rows, cols, idx in windows:
            w_ref, m_ref, v_ref = ins[3 * pi:3 * pi + 3]
            gw = g[rows, cols]
            dw, m2, v2 = _adam_update(w_ref[idx], gw, m_ref[idx], v_ref[idx])
            for o_ref, val in zip(outs[4 * pi:4 * pi + 4], (gw, dw, m2, v2), strict=True):
                o_ref[idx] = val
        if extra is not None:
            outs[4 * n][...] = g[extra[0], extra[1]]

    out_shape = [jax.ShapeDtypeStruct(w.shape, F32) for (w, _, _) in params for _ in range(4)]
    if extra is not None:
        out_shape.append(jax.ShapeDtypeStruct((extra[0].stop - extra[0].start, extra[1].stop - extra[1].start), F32))
    res = pl.pallas_call(body, name=name, out_shape=out_shape)(parts, *[a for p in params for a in p])
    return [res[4 * i:4 * i + 4] for i in range(n)] + ([res[4 * n]] if extra is not None else [])


def _adamw(name, w, parts, m, v, tr, tc=None):
    r_, c_ = w.shape
    p_ = parts.shape[0]
    tr = min(tr, r_)
    tc = c_ if tc is None else tc
    assert r_ % tr == 0 and c_ % tc == 0

    def body(w_ref, p_ref, m_ref, v_ref, g_ref, d_ref, m2_ref, v2_ref):
        g = _sum_parts(p_ref)
        g_ref[...] = g
        d_ref[...], m2_ref[...], v2_ref[...] = _adam_update(w_ref[...], g, m_ref[...], v_ref[...])

    blk = pl.BlockSpec((tr, tc), lambda i, j: (i, j))
    return pl.pallas_call(
        body, name=name, grid=(r_ // tr, c_ // tc), out_shape=[jax.ShapeDtypeStruct((r_, c_), F32)] * 4,
        in_specs=[blk, pl.BlockSpec((p_, tr, tc), lambda i, j: (0, i, j)), blk, blk], out_specs=[blk] * 4,
        compiler_params=_params(("parallel", "parallel"), VMEM_BIG),
    )(w, parts, m, v)


def _dev_index(p):
    return 4 * p[0] + 2 * p[1] + p[2]


def _all_gather(name, xs):
    n = len(xs)
    hbm = pl.BlockSpec(memory_space=pl.ANY)

    def body(*refs):
        x_refs, o_refs = refs[:n], refs[n:2 * n]
        send_sems, recv_sems, local_sems = refs[2 * n:]
        x, y, c = lax.axis_index("x"), lax.axis_index("y"), lax.axis_index("c")
        me, sibling = (x, y, c), (x, y, 1 - c)
        chips = [(1 - x, y), (x, 1 - y), (1 - x, 1 - y)]

        def place(a, block):
            return o_refs[a].at[_dev_index(block)]

        def copy(a, k, block, to, src=None):
            dst = place(a, block)
            return pltpu.make_async_remote_copy(
                src_ref=dst if src is None else src, dst_ref=dst, send_sem=send_sems.at[a, k],
                recv_sem=recv_sems.at[a, k], device_id=to, device_id_type=MESH)

        mine = [pltpu.make_async_copy(x_refs[a], place(a, me), local_sems.at[a]) for a in range(n)]
        for cp in mine:
            cp.start()
        first = []
        for a in range(n):
            first.append(copy(a, 0, me, sibling, src=x_refs[a]))
            first += [copy(a, 1 + j, me, (*chip, c), src=x_refs[a]) for j, chip in enumerate(chips)]
        for cp in first:
            cp.start()
        passed = []
        for j, chip in enumerate(chips):
            for a in range(n):
                copy(a, 1 + j, (*chip, c), me).wait_recv()
                cp = copy(a, 4 + j, (*chip, c), sibling)
                cp.start()
                passed.append(cp)
        for a in range(n):
            copy(a, 0, sibling, me).wait_recv()
            for j, chip in enumerate(chips):
                copy(a, 4 + j, (*chip, 1 - c), me).wait_recv()
        for cp in first + passed:
            cp.wait_send()
        for cp in mine:
            cp.wait()

    return pl.pallas_call(
        body, name=name, out_shape=[jax.ShapeDtypeStruct((NDEV, *x.shape), x.dtype) for x in xs],
        in_specs=[hbm] * n, out_specs=[hbm] * n,
        scratch_shapes=[pltpu.SemaphoreType.DMA((n, 7)), pltpu.SemaphoreType.DMA((n, 7)), pltpu.SemaphoreType.DMA((n,))],
    )(*xs)


_HBM = pl.BlockSpec(memory_space=pltpu.HBM)
_SEM = pl.BlockSpec(memory_space=pltpu.SEMAPHORE)
_EFFECT = pltpu.SideEffectType.DATAFLOW_SIDE_EFFECTING


def _mesh_position():
    return lax.axis_index("x"), lax.axis_index("y"), lax.axis_index("c")


def _peers(me):
    return [(k, tuple(1 - v if (k >> b) & 1 else v for v, b in zip(me, (2, 1, 0)))) for k in range(1, NDEV)]


def _column_window(ref, block, width):
    return ref.at[:, pl.ds(pl.multiple_of(_dev_index(block) * width, 128), width)]


RELAY_COPIES = 3


def _relay_copies(o_ref, send_sems, recv_sems, with_arrivals):
    width = o_ref.shape[1] // NDEV
    x, y, c = me = _mesh_position()
    sibling = (x, y, 1 - c)
    x_side, y_side, diagonal = (1 - x, y), (x, 1 - y), (1 - x, 1 - y)
    first = c == 0
    via = (jnp.where(first, 1 - x, x), jnp.where(first, y, 1 - y))
    to = (jnp.where(first, x, 1 - x), jnp.where(first, 1 - y, y))

    def copy(k, block, device):
        window = _column_window(o_ref, block, width)
        return pltpu.make_async_remote_copy(src_ref=window, dst_ref=window, send_sem=send_sems.at[k],
                                            recv_sem=recv_sems.at[k], device_id=device, device_id_type=MESH)

    sent = [copy(0, (*via, c), (*to, c)), copy(1, (*x_side, c), sibling), copy(2, (*y_side, c), sibling)]
    if not with_arrivals:
        return sent
    arrivals =[copy(0, (*diagonal, c), me), copy(1, (*x_side, 1 - c), me), copy(2, (*y_side, 1 - c), me)]
    return sent, arrivals


def _relay_start(name, gathered, dep):
    def body(g_ref, dep_ref, send_sems, recv_sems, o_ref, token):
        for cp in _relay_copies(g_ref, send_sems, recv_sems, with_arrivals=False):
            cp.start()
        token[...] = jnp.zeros_like(token)

    sems = pltpu.SemaphoreType.DMA((RELAY_COPIES,))
    res = pl.pallas_call(
        body, name=name,
        out_shape=(sems, sems, pltpu.HBM(gathered.shape, gathered.dtype), jax.ShapeDtypeStruct((8, 128), F32)),
        in_specs=[_HBM, pl.BlockSpec(memory_space=pl.ANY)],
        out_specs=(_SEM, _SEM, _HBM, pl.BlockSpec(memory_space=pltpu.VMEM)),
        input_output_aliases={0: 2},
        compiler_params=pltpu.CompilerParams(has_side_effects=_EFFECT),
    )(gathered, dep)
    return res[:-1], res[-1]


def _relay_wait(name, handles, after):
    send_sems, recv_sems, gathered = handles

    def body(g_ref, s_sems, r_sems, after_ref, o_ref):
        sent, arrivals = _relay_copies(g_ref, s_sems, r_sems, with_arrivals=True)
        for cp, arrival in zip(sent, arrivals):
            cp.wait_send()
            arrival.wait_recv()

    return pl.pallas_call(
        body, name=name, out_shape=pltpu.HBM(gathered.shape, gathered.dtype),
        in_specs=[_HBM, _SEM, _SEM, pl.BlockSpec(memory_space=pl.ANY)], out_specs=_HBM, input_output_aliases={0: 0},
        compiler_params=pltpu.CompilerParams(has_side_effects=_EFFECT),
    )(gathered, send_sems, recv_sems, after)


def _columns_last(name, gathered):
    width = gathered.shape[1] // NDEV
    hbm = pl.BlockSpec(memory_space=pl.ANY)

    def body(g_ref, o_ref, send_sem, recv_sem):
        x, y, c = _mesh_position()

        def copy(core, device):
            window = _column_window(o_ref, (1 - x, 1 - y, core), width)
            return pltpu.make_async_remote_copy(src_ref=window, dst_ref=window, send_sem=send_sem, recv_sem=recv_sem,
                                                device_id=device, device_id_type=MESH)

        onward = copy(c, (x, y, 1 - c))
        onward.start()
        copy(1 - c, (x, y, c)).wait_recv()
        onward.wait_send()

    return pl.pallas_call(
        body, name=name, out_shape=jax.ShapeDtypeStruct(gathered.shape, gathered.dtype), in_specs=[hbm], out_specs=hbm,
        input_output_aliases={0: 0}, scratch_shapes=[pltpu.SemaphoreType.DMA, pltpu.SemaphoreType.DMA],
    )(gathered)


NCHIP = NDEV // 2
EXCHANGE_COPIES = {"columns": NCHIP - 1, "gather": NDEV - 1, "scatter": NDEV - 1, "pair": NCHIP, "chips": NCHIP - 1}


def _landing_zones(name, xs, mode):
    x_, y_, c_ = _mesh_position()
    mine = (2 * x_ + y_ if mode == "chips" else _dev_index((x_, y_, c_))).astype(jnp.int32).reshape(1)
    lands = []
    for a, x in enumerate(xs):
        rows, cols = x.shape[-2:]
        if mode == "pair":
            lands.append(lax.empty((NCHIP, rows, cols), x.dtype))
            continue
        tr = 512 if rows % 512 == 0 else rows

        def body(me_ref, x_ref, o_ref):
            o_ref[...] = x_ref[...]

        if mode in ("gather", "columns"):
            in_spec = pl.BlockSpec((tr, cols), lambda i, me_ref: (i, 0))
        else:
            in_spec = pl.BlockSpec((None, tr, cols), lambda i, me_ref: (me_ref[0], i, 0))
        if mode == "columns":
            out_shape, out_spec = (rows, NDEV * cols), pl.BlockSpec((tr, cols), lambda i, me_ref: (i, me_ref[0]))
        else:
            out_shape = (NCHIP if mode == "chips" else NDEV, rows, cols)
            out_spec = pl.BlockSpec((None, tr, cols), lambda i, me_ref: (me_ref[0], i, 0))
        lands.append(pl.pallas_call(
            body, name=f"{name}_{a}", out_shape=jax.ShapeDtypeStruct(out_shape, x.dtype),
            grid_spec=pltpu.PrefetchScalarGridSpec(num_scalar_prefetch=1, grid=(rows // tr,), in_specs=[in_spec],
                                                   out_specs=out_spec),
            compiler_params=_params(("arbitrary",)),
        )(mine, x))
    return lands


def _exchange_copies(x_refs, land_refs, send_sems, recv_sems, mode):
    x_, y_, c_ = me = _mesh_position()
    per_array = EXCHANGE_COPIES[mode]
    out = []

    def add(a, k, src, dst, peer):
        sem = a * per_array + k
        out.append(pltpu.make_async_remote_copy(src_ref=src, dst_ref=dst, send_sem=send_sems.at[sem], recv_sem=recv_sems.at[sem],
                                                device_id=peer, device_id_type=MESH))

    for a, (x_ref, land_ref) in enumerate(zip(x_refs, land_refs)):
        if mode == "columns":
            for k, peer in enumerate([(x_, y_, 1 - c_), (1 - x_, y_, c_), (x_, 1 - y_, c_)]):
                add(a, k, x_ref, _column_window(land_ref, me, x_ref.shape[1]), peer)
        elif mode in ("gather", "scatter"):
            for k, peer in _peers(me):
                add(a, k - 1, x_ref.at[_dev_index(peer)] if mode == "scatter" else x_ref, land_ref.at[_dev_index(me)], peer)
        elif mode == "pair":
            for chip in range(NCHIP):
                add(a, chip, x_ref.at[2 * chip + 1 - c_], land_ref.at[chip], (x_, y_, 1 - c_))
        else:
            for k in range(1, NCHIP):
                px, py = (1 - x_ if k & 2 else x_), (1 - y_ if k & 1 else y_)
                add(a, k - 1, x_ref.at[2 * px + py], land_ref.at[2 * x_ + y_], (px, py, c_))
    return out


def _exchange_start(name, xs, lands, mode, dep, carry=False):
    n = len(xs)

    def body(*refs):
        x_refs, land_refs = refs[:n], refs[n:2 * n]
        send_sems, recv_sems = refs[2 * n + 1], refs[2 * n + 2]
        for cp in _exchange_copies(x_refs, land_refs, send_sems, recv_sems, mode):
            cp.start()
        if not carry:
            refs[-1][...] = jnp.zeros_like(refs[-1])

    sems = pltpu.SemaphoreType.DMA((n * EXCHANGE_COPIES[mode],))
    moved = [pltpu.with_memory_space_constraint(a, pltpu.HBM) for a in (*xs, *lands, *([dep] if carry else []))]
    res = pl.pallas_call(
        body, name=name,
        out_shape=(sems, sems, *[pltpu.HBM(a.shape, a.dtype) for a in moved],
                   *([] if carry else [jax.ShapeDtypeStruct((8, 128), F32)])),
        in_specs=[_HBM] * len(moved) + ([] if carry else [pl.BlockSpec(memory_space=pl.ANY)]),
        out_specs=(_SEM, _SEM, *[_HBM] * len(moved), *([] if carry else [pl.BlockSpec(memory_space=pltpu.VMEM)])),
        input_output_aliases={i: 2 + i for i in range(len(moved))},
        compiler_params=pltpu.CompilerParams(has_side_effects=_EFFECT),
    )(*moved, *([] if carry else [dep]))
    return res[:-1], res[-1]


def _exchange_wait(name, handles, mode, after, with_sources=False):
    send_sems, recv_sems = handles[0], handles[1]
    bufs = handles[2:]
    n = len(bufs) // 2
    afters = list(after) if isinstance(after, (list, tuple)) else [after]

    def body(*refs):
        x_refs, land_refs = refs[:n], refs[n:2 * n]
        s_sems, r_sems = refs[2 * n], refs[2 * n + 1]
        for cp in _exchange_copies(x_refs, land_refs, s_sems, r_sems, mode):
            cp.wait_send()
            cp.wait_recv()

    res = pl.pallas_call(
        body, name=name, out_shape=tuple(pltpu.HBM(a.shape, a.dtype) for a in bufs),
        in_specs=[_HBM] * (2 * n) + [_SEM, _SEM] + [pl.BlockSpec(memory_space=pl.ANY)] * len(afters),
        out_specs=tuple([_HBM] * (2 * n)), input_output_aliases={i: i for i in range(2 * n)},
        compiler_params=pltpu.CompilerParams(has_side_effects=_EFFECT),
    )(*bufs, send_sems, recv_sems, *afters)
    return (res[n:], res[:n]) if with_sources else res[n:]


def _pair_sum(name, x, from_sibling):
    _, rows, cols = x.shape
    tr = rows
    core = lax.axis_index("c").astype(jnp.int32).reshape(1)

    def body(c_ref, x_ref, s_ref, o_ref):
        o_ref[...] = (x_ref[...].astype(F32) + s_ref[...].astype(F32)).astype(o_ref.dtype)

    return pl.pallas_call(
        body, name=name, out_shape=jax.ShapeDtypeStruct((NCHIP, rows, cols), x.dtype),
        grid_spec=pltpu.PrefetchScalarGridSpec(
            num_scalar_prefetch=1, grid=(NCHIP, rows // tr),
            in_specs=[pl.BlockSpec((None, tr, cols), lambda j, i, c_ref: (2 * j + c_ref[0], i, 0)),
                      pl.BlockSpec((None, tr, cols), lambda j, i, c_ref: (j, i, 0))],
            out_specs=pl.BlockSpec((None, tr, cols), lambda j, i, c_ref: (j, i, 0))),
        compiler_params=_params(("parallel", "parallel")),
    )(core, x, from_sibling)


def kernel(x, c, positions, norm_w, mod_w, mod_b, attn_w_in, attn_w_out, ssd_w_in, ssd_conv_w, ssd_conv_b, ssd_dt_bias, ssd_a_log, ssd_d, ssd_norm_w, ssd_w_out, final_norm_w, loss_target, m_norm_w, m_mod_w, m_mod_b, m_attn_w_in, m_attn_w_out, m_ssd_w_in, m_ssd_conv_w, m_ssd_conv_b, m_ssd_dt_bias, m_ssd_a_log, m_ssd_d, m_ssd_norm_w, m_ssd_w_out, m_final_norm_w, v_norm_w, v_mod_w, v_mod_b, v_attn_w_in, v_attn_w_out, v_ssd_w_in, v_ssd_conv_w, v_ssd_conv_b, v_ssd_dt_bias, v_ssd_a_log, v_ssd_d, v_ssd_norm_w, v_ssd_w_out, v_final_norm_w):
    s_len, dm = x.shape[1], x.shape[2]
    me = 4 * lax.axis_index("x") + 2 * lax.axis_index("y") + lax.axis_index("c")
    x0 = x.reshape(s_len, dm)
    tgt = loss_target.reshape(s_len, dm)
    aw = 3 * 512
    si = 2 * dm
    sxbc = 2 * si
    n_ssd_in = ssd_w_in.shape[2] * NDEV

    (c_all,) = _all_gather("gather_c", [c])
    c_all = c_all.reshape(NDEV, dm)
    part = _mod_part(c_all, mod_w)
    (part_all,) = _all_gather("gather_mod", [part])
    mod_nb = jnp.stack([lax.dynamic_index_in_dim(part_all, i * NDEV + me, axis=1, keepdims=False).reshape(3 * dm)
                        for i in range(2)])

    wcol = attn_w_in.shape[2]
    ai_shard = [attn_w_in[0].astype(BF16)]
    ai_handles, ai_token = _exchange_start("attn_w_in_start", ai_shard, _landing_zones("attn_w_in_place", ai_shard, "columns"),
                                           "columns", part_all)
    inv_freq = ROPE_THETA ** (-jnp.arange(0, ROT_DIM, 2, dtype=F32) / ROT_DIM)
    per_head = jnp.concatenate([inv_freq, inv_freq, jnp.zeros(HEAD_DIM - ROT_DIM, F32)])
    inv_row = jnp.tile(per_head, 128 // HEAD_DIM).reshape(1, 128) + ai_token[0:1]
    tabs = _rope_tables(positions.reshape(s_len, 1), inv_row)
    ssd_small = _pack_ssd_small(ssd_conv_w[0], ssd_conv_b, ssd_norm_w)
    ao_shard = [attn_w_out[0].astype(BF16)]
    late_shards = [ssd_w_in[0].T.astype(BF16), ssd_w_out[0].astype(BF16), ssd_small]
    ao_lands = _landing_zones("w_out_place", ao_shard, "gather")
    late_lands = _landing_zones("weights_place", late_shards, "gather")
    (w_ai,) = _exchange_wait("attn_w_in_wait", ai_handles, "columns", [*tabs, *ao_lands, *late_lands])
    relay_handles, relay_token = _relay_start("attn_w_in_relay_start", w_ai, tabs[0])
    (shift0, scale0, gate0, nw0), (shift1, scale1, gate1, nw1) = _mod_finish(mod_nb, mod_b, norm_w, [relay_token])
    shift, scale, gate, nw = [shift0, shift1], [scale0, scale1], [gate0, gate1], [nw0, nw1]
    hn0 = _norm_mod_fwd("norm0", x0, nw[0], scale[0], shift[0])
    w_ai = _columns_last("gather_attn_w_in_last", _relay_wait("attn_w_in_relay_wait", relay_handles, hn0))

    ao_handles, w_ai = _exchange_start("w_out_start", ao_shard, ao_lands, "gather", w_ai, carry=True)
    w_handles, w_ai = _exchange_start("weights_start", late_shards, late_lands, "gather", w_ai, carry=True)

    qk = _matmul("proj_qk", hn0, w_ai, "nn", F32, MM_T, MM_T, dm, epilogue=_rot_fwd, mrows=tabs, n_out=2 * aw)
    v = _matmul("proj_vz", hn0, w_ai, "nn", F32, MM_T, MM_T, dm, b_noff=2 * aw, n_out=2 * aw)
    z0 = (v, 1)
    att = [_attn_fwd(g, qk, v) for g in range(3)]
    os_, lses = [a[0] for a in att], [a[1] for a in att]
    (g_ao,) = _exchange_wait("w_out_wait", ao_handles, "gather", lses[2])
    a0, y0, x1 = _attn_out(os_, lses, z0, x0, gate[0], g_ao.reshape(aw, dm))

    hn1 = _norm_mod_fwd("norm1", x1, nw[1], scale[1], shift[1])
    g_si, g_so, g_small = _exchange_wait("weights_wait", w_handles, "gather", hn1)
    w_ao = g_ao.reshape(aw, dm)
    w_si_t = g_si.reshape(n_ssd_in, dm)
    w_so = g_so.reshape(si, dm)
    conv_w = g_small[:, 0:CONV_WIDTH, :].transpose(1, 0, 2).reshape(CONV_WIDTH, sxbc)
    conv_b = g_small[:, 5, :].reshape(1, sxbc)
    snw = g_small[:, 6, 0:si // NDEV].reshape(1, si)
    ndt = 2 * SSD_HEADS
    z1 = _matmul("ssd_proj_z", hn1, w_si_t, "nt", F32, MM_T, MM_T, dm, n_out=si)
    xpre = _matmul("ssd_proj_xbc", hn1, w_si_t, "nt", F32, MM_T, MM_T, dm, b_noff=si, n_out=sxbc)
    dt_raw = _matmul("ssd_proj_dt", hn1, w_si_t, "nt", F32, MM_T, ndt, dm, b_noff=si + sxbc, n_out=ndt)
    xbc = _conv_fwd(xpre, conv_w, conv_b)
    widen = lambda a: jnp.pad(a, ((0, 0), (0, SSD_DTW - ndt)))
    dt_raw = widen(dt_raw)
    dt_bias = widen(ssd_dt_bias.reshape(1, ndt))
    alog = widen(ssd_a_log.reshape(1, ndt))
    dt = _softplus_fwd(dt_raw, dt_bias)
    y_f, st_f = _ssd_fwd(xbc, dt, alog, 0)
    y_fb, st_b = _ssd_fwd(xbc, dt, alog, 1, prior=y_f)
    d_e = jnp.repeat(ssd_d.reshape(SSD_HEADS), HEAD_DIM).reshape(1, si)

    fnw = final_norm_w.reshape(1, dm)
    u, dx2, dy1, g_fnw, dgate1, loss_part = _ssd_tail_loss(y_fb, xbc, z1, d_e, snw, w_so, x1, tgt, gate[1], fnw)
    gw_so = _matmul("ssd_out_dw", u, dy1, "tn", BF16, MM_T, MM_T, MM_T)
    dys, dz1, g_snw, g_d = _gate_norm_bwd(dy1, w_so, y_fb, xbc, z1, d_e, snw)
    dxbc_f, ddt_f, dalog_f = _ssd_bwd(xbc, dt, alog, st_f, dys, d_e, 0)
    dxbc, ddt_b, dalog_b = _ssd_bwd(xbc, dt, alog, st_b, dys, d_e, 1, prior=dxbc_f)
    dpre, g_cw, g_cb = _conv_bwd(xpre, dxbc, conv_w, conv_b)
    ddt_raw, g_dtb = _softplus_bwd(ddt_f, ddt_b, dt_raw, dt_bias)
    ddt_raw = ddt_raw[:, :ndt]
    dhn1 = [_matmul("ssd_proj_z_dx", dz1, w_si_t, "nn", F32, MM_T, MM_T, MM_T),
            _matmul("ssd_proj_xbc_dx", dpre, w_si_t, "nn", F32, MM_T, MM_T, MM_T, b_koff=si)]
    gw_si_t = _matmul("ssd_proj_z_dw", dz1, hn1, "tn", BF16, MM_T, MM_T, MM_T, dest=(n_ssd_in, 0, None))
    gw_si_t = _matmul("ssd_proj_xbc_dw", dpre, hn1, "tn", BF16, MM_T, MM_T, MM_T, dest=(n_ssd_in, si, gw_si_t))
    gw_si_t = _matmul("ssd_proj_dt_dw", ddt_raw, hn1, "tn", BF16, ndt, MM_T, MM_T, dest=(n_ssd_in, si + sxbc, gw_si_t))

    l1_grads = [gw_so.reshape(NDEV, si // NDEV, dm), gw_si_t.reshape(NDEV, n_ssd_in // NDEV, dm),
                _pack_ssd_small_blocks(g_cw, g_cb, g_snw)]
    l1_handles, l1_token = _exchange_start("l1_grads_start", l1_grads, _landing_zones("l1_grads_place", l1_grads, "scatter"),
                                           "scatter", dhn1[1])
    dx1, dy0, g_nw1, dsc1, dsh1, dgate0 = _norm_mod_bwd(
        "ssd_proj_dt_dx_norm1_bwd", (ddt_raw, w_si_t, "nn", ndt, dict(b_koff=si + sxbc)), x1, dhn1, dx2,
        nw[1], scale[1], shift[1], prev=(y0, gate[0] + l1_token[0:1, 0:1]))

    gw_ao = _matmul("attn_out_dw", a0, dy0, "tn", BF16, aw // 2, MM_T, MM_T)
    dos, dls, dz0 = _mix_bwd(dy0, w_ao, os_, lses, z0)
    datt = [_attn_bwd(g, qk, v, os_[g], lses[g], dos[g], dls[g]) for g in range(3)]
    dqkv = _rot_pack_bwd([t[0] for t in datt], [t[1] for t in datt], [t[2] for t in datt], tabs)
    gw_ai = _matmul("proj_qkv_dw", hn0, dqkv, "tn", BF16, MM_T, wcol, MM_T, out_blocks=3 * aw // wcol, dest=(NDEV, 0, None))
    gw_ai = _matmul("proj_z_dw", hn0, dz0, "tn", BF16, MM_T, wcol, MM_T, out_blocks=aw // wcol,
                    dest=(NDEV, 3 * aw // wcol, gw_ai))
    after_start = lambda acc, t: acc + t
    zero_row = lambda token: jnp.tile(token[0:1], (1, dm // 128))
    l0_grads = [gw_ai, gw_ao.reshape(NDEV, aw // NDEV, dm)]
    pair_handles, pair_token = _exchange_start("l0_pair_start", l0_grads, _landing_zones("l0_pair_place", l0_grads, "pair"),
                                               "pair", dqkv)
    dhn0_z = _matmul("proj_z_dx", dz0, w_ai, "nt", F32, MM_T, MM_T, aw, b_koff=3 * aw, n_out=dm, epilogue=after_start,
                     ncols=(zero_row(pair_token),))
    from_sibling, l0_grads = _exchange_wait("l0_pair_wait", pair_handles, "pair", dhn0_z, with_sources=True)
    chip_sums = [_pair_sum(f"l0_pair_sum_{a}", g, s) for a, (g, s) in enumerate(zip(l0_grads, from_sibling))]
    l0_handles, l0_token = _exchange_start("l0_grads_start", chip_sums, _landing_zones("l0_grads_place", chip_sums, "chips"),
                                           "chips", dhn0_z)
    dx0, g_nw0, dsc0, dsh0 = _norm_mod_bwd(
        "proj_qkv_dx_norm0_bwd", (dqkv, w_ai, "nt", aw, dict(n_out=dm)), x0, [dhn0_z], dx1,
        nw[0], scale[0], shift[0] + zero_row(l0_token))

    small_g = [_pack_small([dsh0, dsc0, dgate0, dsh1, dsc1, dgate1, g_nw0, g_nw1, g_fnw], g_dtb, [dalog_f, dalog_b], g_d, loss_part)]
    sm_handles, sm_token = _exchange_start("small_grads_start", small_g, _landing_zones("small_grads_place", small_g, "gather"),
                                           "gather", dx0)

    whole = (slice(None), slice(None))
    r_so, r_si, r_small = _exchange_wait("l1_grads_wait", l1_handles, "scatter", sm_token)
    si_out = [o.T for o in _adamw("adamw_ssd_w_in", ssd_w_in[0].T, r_si, m_ssd_w_in[0].T, v_ssd_w_in[0].T, n_ssd_in // NDEV, 512)]
    so_out = _adamw("adamw_ssd_w_out", ssd_w_out[0], r_so, m_ssd_w_out[0], v_ssd_w_out[0], 256)
    cw_cols = ssd_conv_w.shape[2]
    cw_out, cb_out, snw_out = _adamw_windows(
        "adamw_ssd_small", r_small,
        [(ssd_conv_w, m_ssd_conv_w, v_ssd_conv_w), (ssd_conv_b, m_ssd_conv_b, v_ssd_conv_b),
         (ssd_norm_w, m_ssd_norm_w, v_ssd_norm_w)],
        [(0, slice(0, CONV_WIDTH), slice(0, cw_cols), (0, slice(None), slice(None))),
         (1, slice(5, 6), slice(0, cw_cols), whole), (2, slice(6, 7), slice(0, si // NDEV), whole)])
    r_ai, r_ao = _exchange_wait("l0_grads_wait", l0_handles, "chips", so_out[0])
    ai_out = _adamw("adamw_attn_w_in", attn_w_in[0], r_ai, m_attn_w_in[0], v_attn_w_in[0], 512)
    ao_out = _adamw("adamw_attn_w_out", attn_w_out[0], r_ao, m_attn_w_out[0], v_attn_w_out[0], 192)

    (small_all,) = _exchange_wait("small_grads_wait", sm_handles, "gather", ai_out[0])
    full = slice(0, PACK_COLS)
    nhd = SSD_HEADS
    windows = [(0, slice(3 * i + k, 3 * i + k + 1), full, (slice(i, i + 1), slice(k * dm, (k + 1) * dm)))
               for i in range(2) for k in range(3)]
    windows += [(1, slice(6 + i, 7 + i), full, (slice(i, i + 1), slice(None))) for i in range(2)]
    windows += [(2, slice(8, 9), full, whole)]
    windows += [(3 + q, slice(9, 10), slice(2 * nhd * q + nhd * j, 2 * nhd * q + nhd * (j + 1)), (0, slice(j, j + 1), slice(None)))
                for q in range(2) for j in range(2)]
    windows += [(5, slice(9, 10), slice(4 * nhd, 5 * nhd), whole)]
    as_row = lambda a: a.reshape(1, dm)
    mb_out, nw_out, fnw_out, dtb_out, alog_out, d_out, loss = _adamw_windows(
        "adamw_small", small_all,
        [(mod_b, m_mod_b, v_mod_b), (norm_w, m_norm_w, v_norm_w), (fnw, as_row(m_final_norm_w), as_row(v_final_norm_w)),
         (ssd_dt_bias, m_ssd_dt_bias, v_ssd_dt_bias), (ssd_a_log, m_ssd_a_log, v_ssd_a_log), (ssd_d, m_ssd_d, v_ssd_d)],
        windows, extra=(slice(9, 10), slice(256, 257)))
    loss = loss.reshape(())

    ncol = mod_w.shape[2]
    dmod_all = small_all[:, 0:6, :].reshape(NDEV, 2, 3 * dm)
    dmod_sh = lax.dynamic_slice_in_dim(dmod_all, me * ncol, ncol, axis=2).transpose(1, 0, 2)
    g_modw = _mod_grad(c_all, dmod_sh).reshape(1, 2 * dm, ncol)
    modw_out = _adamw("adamw_mod_w", mod_w.reshape(2 * dm, ncol), g_modw, m_mod_w.reshape(2 * dm, ncol),
                      v_mod_w.reshape(2 * dm, ncol), 512)

    per_kind = []
    for k in range(4):
        per_kind.append([
            nw_out[k], modw_out[k].reshape(mod_w.shape), mb_out[k], ai_out[k][None], ao_out[k][None], si_out[k][None],
            cw_out[k], cb_out[k], dtb_out[k], alog_out[k], d_out[k], snw_out[k], so_out[k][None], fnw_out[k].reshape(dm)])
    return (loss, dx0.reshape(x.shape), *per_kind[0], *per_kind[1], *per_kind[2], *per_kind[3])


def _pack_ssd_small_blocks(g_cw, g_cb, g_nw):
    nper = g_cw.shape[1] // NDEV
    nwper = g_nw.shape[1] // NDEV

    def body(cw_ref, cb_ref, nw_ref, o_ref):
        o_ref[...] = jnp.zeros_like(o_ref)
        for d in range(NDEV):
            o_ref[d, 0:5, :] = cw_ref[:, d * nper:(d + 1) * nper]
            o_ref[d, 5:6, :] = cb_ref[:, d * nper:(d + 1) * nper]
            o_ref[d, 6:7, 0:nwper] = nw_ref[:, d * nwper:(d + 1) * nwper]

    return pl.pallas_call(body, name="pack_ssd_small_grads", out_shape=jax.ShapeDtypeStruct((NDEV, 8, nper), F32))(g_cw, g_cb, g_nw)
```

```python
import functools
import math

import jax
import jax.numpy as jnp
from jax import lax
from jax.experimental import pallas as pl
from jax.experimental.pallas import tpu as pltpu

F32 = jnp.float32
BF16 = jnp.bfloat16
HI = lax.Precision.HIGHEST
MESH = pl.DeviceIdType.MESH
NDEV = 8

NORM_EPS = 1e-6
ROPE_THETA = 500000.0
ROT_DIM = 16
HEAD_DIM = 64
DILATIONS = (1, 4, 16)
BAND = 64
NEG_BIG = -1e30
CHUNK = 128
SSD_HEADS = 32
SSD_GROUPS = 8
CONV_WIDTH = 5

ADAM_LR = 0.001
ADAM_B1 = 0.9
ADAM_B2 = 0.999
ADAM_EPS = 1e-08
ADAM_WD = 0.01
ADAM_STEP = 10

VMEM_BIG = 56 * 1024 * 1024
MM_T = 1024


def _params(sem=None, vmem=None):
    kw = {}
    if sem is not None:
        kw["dimension_semantics"] = sem
    if vmem is not None:
        kw["vmem_limit_bytes"] = vmem
    return pltpu.CompilerParams(**kw)


def _dg(a, b, ca, cb, prec=None):
    return lax.dot_general(a, b, (((ca,), (cb,)), ((), ())), preferred_element_type=F32, precision=prec)


def _nn(a, b):
    return _dg(a.astype(BF16), b.astype(BF16), 1, 0)


def _nt(a, b):
    return _dg(a.astype(BF16), b.astype(BF16), 1, 1)


def _tn(a, b):
    return _dg(a.astype(BF16), b.astype(BF16), 0, 0)


def _hnn(a, b):
    return _dg(a, b, 1, 0, HI)


@jax.custom_vjp
def _bnn(a, b):
    return _nn(a, b)


_bnn.defvjp(lambda a, b: (_nn(a, b), (a, b)), lambda r, g: (_nt(g, r[1]), _tn(r[0], g)))


@jax.custom_vjp
def _bnt(a, b):
    return _nt(a, b)


_bnt.defvjp(lambda a, b: (_nt(a, b), (a, b)), lambda r, g: (_nn(g, r[1]), _tn(g, r[0])))


@jax.custom_vjp
def _btn(a, b):
    return _tn(a, b)


_btn.defvjp(lambda a, b: (_tn(a, b), (a, b)), lambda r, g: (_nt(r[1], g), _nn(r[0], g)))


def _silu(x):
    return x * jax.nn.sigmoid(x)


def _b_spec(b, mode, tn, tk, no, ko, jk):
    if mode == "nt":
        return pl.BlockSpec((tn, tk), lambda *g: (jk(*g)[0] + no, jk(*g)[1] + ko))
    return pl.BlockSpec((tk, tn), lambda *g: (jk(*g)[1] + ko, jk(*g)[0] + no))


def _matmul(name, a, b, mode, out_dtype, tm, tn, tk, *, epilogue=None, tiled=(), mrows=(), ncols=(),
            b_noff=0, b_koff=0, n_out=None, out_blocks=None, dest=None):
    if mode == "tn":
        K, M = a.shape
    else:
        M, K = a.shape
    N = n_out if n_out is not None else (b.shape[0] if mode == "nt" else b.shape[1])
    tm, tn, tk = min(tm, M), min(tn, N), min(tk, K)
    assert M % tm == 0 and N % tn == 0 and K % tk == 0, (name, M, N, K, tm, tn, tk)
    assert b_noff % tn == 0 and b_koff % tk == 0
    no, ko = b_noff // tn, b_koff // tk
    nk = K // tk
    if mode == "tn":
        a_spec = pl.BlockSpec((tk, tm), lambda i, j, k: (k, i))
    else:
        a_spec = pl.BlockSpec((tm, tk), lambda i, j, k: (i, k))
    specs = [a_spec, _b_spec(b, mode, tn, tk, no, ko, lambda i, j, k: (j, k))]
    specs += [pl.BlockSpec((tm, tn), lambda i, j, k: (i, j)) for _ in tiled]
    specs += [pl.BlockSpec((tm, r.shape[1]), lambda i, j, k: (i, 0)) for r in mrows]
    specs += [pl.BlockSpec((1, tn), lambda i, j, k: (0, j)) for _ in ncols]
    total, off, earlier = dest if dest is not None else (None, 0, None)
    if out_blocks is None:
        assert off % tm == 0
        mo = off // tm
        out_shape = jax.ShapeDtypeStruct((M if total is None else total, N), out_dtype)
        out_spec = pl.BlockSpec((tm, tn), lambda i, j, k: (i + mo, j))
    else:
        nper = N // out_blocks
        assert nper % tn == 0
        jb = nper // tn
        out_shape = jax.ShapeDtypeStruct((out_blocks if total is None else total, M, nper), out_dtype)
        out_spec = pl.BlockSpec((None, tm, tn), lambda i, j, k: (j // jb + off, i, j % jb))
    if earlier is not None:
        assert earlier.shape == out_shape.shape and earlier.dtype == out_shape.dtype
    ne = len(tiled) + len(mrows) + len(ncols)
    dot = {"nn": _nn, "nt": _nt, "tn": _tn}[mode]

    def body(a_ref, b_ref, *rest):
        extras, o_ref = rest[:ne], rest[ne]

        def finish(acc):
            if epilogue is not None:
                acc = epilogue(acc, *[e[...] for e in extras])
            o_ref[...] = acc.astype(o_ref.dtype)

        if nk == 1:
            finish(dot(a_ref[...], b_ref[...]))
        else:
            acc_ref = rest[ne + 1]
            k = pl.program_id(2)

            @pl.when(k == 0)
            def _():
                acc_ref[...] = jnp.zeros_like(acc_ref)

            acc_ref[...] += dot(a_ref[...], b_ref[...])

            @pl.when(k == nk - 1)
            def _():
                finish(acc_ref[...])

    args = [a, b, *tiled, *mrows, *ncols]
    aliases = {}
    if earlier is not None:
        specs.append(pl.BlockSpec(memory_space=pl.ANY))
        aliases = {len(args): 0}
        args.append(earlier)

    def body_with_dest(*refs):
        body(*refs[:2 + ne], *refs[2 + ne + (earlier is not None):])

    return pl.pallas_call(
        body_with_dest, name=name, out_shape=out_shape, grid=(M // tm, N // tn, nk),
        in_specs=specs, out_specs=out_spec, input_output_aliases=aliases,
        scratch_shapes=[] if nk == 1 else [pltpu.VMEM((tm, tn), F32)],
        compiler_params=_params(("parallel", "parallel", "arbitrary"), VMEM_BIG),
    )(*args)


def _matmul_rows(name, a, b, mode, tm, tk, fn, rows, consts, outs, accs, *, n_out=None, b_noff=0, b_koff=0):
    rl = [(t, t.shape[1], 0) if not isinstance(t, tuple) else t for t in rows]
    make_a = a if callable(a) else None
    M, K = (rl[0][0].shape[0], b.shape[1 if mode == "nt" else 0]) if make_a else a.shape
    N = n_out if n_out is not None else (b.shape[0] if mode == "nt" else b.shape[1])
    tm, tk = min(tm, M), min(tk, K)
    assert M % tm == 0 and K % tk == 0 and b_koff % tk == 0 and b_noff % N == 0, (name, M, N, K)
    no, ko, nk = b_noff // N, b_koff // tk, K // tk
    assert make_a is None or nk == 1
    nr, nc, no_, na = len(rl), len(consts), len(outs), len(accs)
    dot = _nt if mode == "nt" else _nn

    def body(*refs):
        a_ref, b_ref, rest = (None, refs[0], refs[1:]) if make_a else (refs[0], refs[1], refs[2:])
        r_refs, c_refs = rest[:nr], rest[nr:nr + nc]
        o_refs, acc_refs = rest[nr + nc:nr + nc + no_], rest[nr + nc + no_:nr + nc + no_ + na]
        i, k = pl.program_id(0), pl.program_id(1)

        def finish(prod, *made):
            res_o, res_a = fn(prod, *made, *[r[...] for r in r_refs], *[c[...] for c in c_refs])
            for r, v in zip(o_refs, res_o, strict=True):
                r[...] = v.astype(r.dtype)
            if acc_refs:
                @pl.when(i == 0)
                def _():
                    for r in acc_refs:
                        r[...] = jnp.zeros_like(r)

                for r, v in zip(acc_refs, res_a, strict=True):
                    r[...] += v

        if make_a:
            left = make_a(*[r[...] for r in r_refs], *[c[...] for c in c_refs])
            finish(dot(left, b_ref[...]), left)
        elif nk == 1:
            finish(dot(a_ref[...], b_ref[...]))
        else:
            prod_ref = rest[-1]

            @pl.when(k == 0)
            def _():
                prod_ref[...] = jnp.zeros_like(prod_ref)

            prod_ref[...] += dot(a_ref[...], b_ref[...])

            @pl.when(k == nk - 1)
            def _():
                finish(prod_ref[...])

    b_spec = _b_spec(b, mode, N, tk, no, ko, lambda i, k: (0, k))
    in_specs = ([] if make_a else [pl.BlockSpec((tm, tk), lambda i, k: (i, k))]) + [b_spec]
    in_specs += [pl.BlockSpec((tm, w), functools.partial(lambda i, k, cb: (i, cb), cb=cb)) for (_, w, cb) in rl]
    in_specs += [pl.BlockSpec(c.shape, lambda i, k: (0, 0)) for c in consts]
    out_specs = [pl.BlockSpec((tm, c), lambda i, k: (i, 0)) for (c, _) in outs]
    out_specs += [pl.BlockSpec(shp, lambda i, k: (0, 0)) for shp in accs]
    out_shape = [jax.ShapeDtypeStruct((M, c), dt) for (c, dt) in outs] + [jax.ShapeDtypeStruct(shp, F32) for shp in accs]
    res = pl.pallas_call(
        body, name=name, out_shape=out_shape, grid=(M // tm, nk), in_specs=in_specs, out_specs=out_specs,
        scratch_shapes=[] if nk == 1 else [pltpu.VMEM((tm, N), F32)],
        compiler_params=_params(("arbitrary" if accs else "parallel", "arbitrary"), VMEM_BIG),
    )(*([] if make_a else [a]), b, *[t[0] for t in rl], *consts)
    return res[:no_], res[no_:]


def _rowwise(name, fn, tiled, consts, outs, accs, ts):
    tl = [(t, t.shape[1], 0) if not isinstance(t, tuple) else t for t in tiled]
    s_len = tl[0][0].shape[0]
    assert s_len % ts == 0
    nt_, nc_, no_ = len(tl), len(consts), len(outs)

    def body(*refs):
        t_refs, c_refs = refs[:nt_], refs[nt_:nt_ + nc_]
        o_refs, a_refs = refs[nt_ + nc_:nt_ + nc_ + no_], refs[nt_ + nc_ + no_:]
        res_o, res_a = fn(*[r[...] for r in t_refs], *[r[...] for r in c_refs])
        for r, v in zip(o_refs, res_o, strict=True):
            r[...] = v.astype(r.dtype)
        if a_refs:
            @pl.when(pl.program_id(0) == 0)
            def _():
                for r in a_refs:
                    r[...] = jnp.zeros_like(r)

            for r, v in zip(a_refs, res_a, strict=True):
                r[...] += v

    in_specs = [pl.BlockSpec((ts, w), functools.partial(lambda i, cb: (i, cb), cb=cb)) for (_, w, cb) in tl]
    in_specs += [pl.BlockSpec(c.shape, lambda i: (0, 0)) for c in consts]
    out_specs = [pl.BlockSpec((ts, c), lambda i: (i, 0)) for (c, _) in outs]
    out_specs += [pl.BlockSpec(shp, lambda i: (0, 0)) for shp in accs]
    out_shape = [jax.ShapeDtypeStruct((s_len, c), dt) for (c, dt) in outs]
    out_shape += [jax.ShapeDtypeStruct(shp, F32) for shp in accs]
    res = pl.pallas_call(
        body, name=name, out_shape=out_shape, grid=(s_len // ts,), in_specs=in_specs, out_specs=out_specs,
        compiler_params=_params(("arbitrary",) if accs else ("parallel",), VMEM_BIG),
    )(*[t[0] for t in tl], *consts)
    return res[:no_], res[no_:]


def _norm_mod_fn(x, nw, sc, sh):
    r = lax.rsqrt(jnp.mean(x * x, axis=-1, keepdims=True) + NORM_EPS)
    return (x * r * nw) * (1.0 + sc) + sh


def _norm_mod_fwd(name, x, nw, sc, sh):
    (hn,), _ = _rowwise(name, lambda x, nw, sc, sh: ([_norm_mod_fn(x, nw, sc, sh)], []),
                        [x], [nw, sc, sh], [(x.shape[1], BF16)], [], 1024)
    return hn


def _norm_mod_bwd(name, last, x, dhn_parts, dres, nw, sc, sh, prev=None):
    n = len(dhn_parts)
    d = x.shape[1]
    a, b, mode, tk, kw = last

    def fn(dhn, x, *rest):
        for p in rest[:n]:
            dhn = dhn + p
        dres, rest = rest[n], rest[n + 1:]
        y_prev, (nw, sc, sh), gate = (rest[0], rest[1:4], rest[4]) if prev is not None else (None, rest[0:3], None)
        r = lax.rsqrt(jnp.mean(x * x, axis=-1, keepdims=True) + NORM_EPS)
        xh = x * r
        dxh = dhn * (nw * (1.0 + sc))
        dx = r * (dxh - xh * jnp.mean(dxh * xh, axis=-1, keepdims=True)) + dres
        along = jnp.sum(dhn * xh, axis=0, keepdims=True)
        dnw, dsc, dsh = along * (1.0 + sc), along * nw, jnp.sum(dhn, axis=0, keepdims=True)
        if prev is None:
            return [dx], [dnw, dsc, dsh]
        return [dx, gate * dx], [dnw, dsc, dsh, jnp.sum(dx * y_prev, axis=0, keepdims=True)]

    rows = [x, *dhn_parts, dres] + ([prev[0]] if prev is not None else [])
    consts = [nw, sc, sh] + ([prev[1]] if prev is not None else [])
    outs = [(d, F32)] + ([(d, BF16)] if prev is not None else [])
    res_o, res_a = _matmul_rows(name, a, b, mode, 512, tk, fn, rows, consts, outs, [(1, d)] * (3 + (prev is not None)), **kw)
    return (*res_o, *res_a)


def _rope_tables(pos_col, inv_row):
    def fn(pos, inv):
        ang = pos.astype(F32) * inv
        e = lax.broadcasted_iota(jnp.int32, (1, 128), 1) % HEAD_DIM
        cos, sin = jnp.cos(ang), jnp.sin(ang)
        half = ROT_DIM // 2
        return [jnp.where(e < ROT_DIM, cos, 1.0), jnp.where(e < half, -sin, 0.0),
                jnp.where((e >= half) & (e < ROT_DIM), sin, 0.0)], []

    (c, sa, sb), _ = _rowwise("rope_tables", fn, [pos_col], [inv_row], [(128, F32)] * 3, [], 512)
    return c, sa, sb


def _rot_fwd(t, c, sa, sb):
    n = t.shape[1]
    rep = n // 128
    c, sa, sb = (jnp.tile(u, (1, rep)) for u in (c, sa, sb))
    return t * c + pltpu.roll(t, n - ROT_DIM // 2, 1) * sa + pltpu.roll(t, ROT_DIM // 2, 1) * sb


def _rot_bwd(g, c, sa, sb):
    n = g.shape[1]
    rep = n // 128
    c, sa, sb = (jnp.tile(u, (1, rep)) for u in (c, sa, sb))
    return g * c + pltpu.roll(g * sa, ROT_DIM // 2, 1) + pltpu.roll(g * sb, n - ROT_DIM // 2, 1)


ATT_TQ = 128


def _attn_tiles(l):
    tk = ATT_TQ + 2 * BAND
    return (l, l) if l <= tk else (ATT_TQ, tk)


def _attn_specs(g, s_len):
    def blk(off):
        return pl.BlockSpec((s_len, 128), functools.partial(lambda hp, off: (0, off + hp), off=off))

    return blk(4 * g), blk(12 + 4 * g), blk(4 * g), blk(0)


def _attn_tile_geometry(t, d, l):
    tq, tk = _attn_tiles(l)
    nts = l // tq
    r = t // nts
    ts = t % nts
    q0 = ts * tq
    ws = jnp.clip(q0 - BAND, 0, l - tk)
    kind = jnp.where(ts == 0, 0, jnp.where(ts == nts - 1, 2, 1))
    if d == 1:
        return pl.ds(pl.multiple_of(q0, tq), tq), pl.ds(pl.multiple_of(ws, BAND), tk), kind
    return pl.ds(r + d * q0, tq, stride=d), pl.ds(r + d * ws, tk, stride=d), kind


def _attn_fill_bias(bias_ref):
    _, tq2, tk = bias_ref.shape
    iq = lax.broadcasted_iota(jnp.int32, (tq2, 1), 0) % (tq2 // 2)
    ik = lax.broadcasted_iota(jnp.int32, (1, tk), 1)
    for i, off in enumerate((0, -BAND, -2 * BAND)):
        bias_ref[i] = jnp.where(jnp.abs(ik + off - iq) <= BAND, 0.0, NEG_BIG)


def _split_heads(t, in_h):
    zero = jnp.zeros_like(t)
    return jnp.concatenate([jnp.where(in_h[0], t, zero), jnp.where(in_h[1], t, zero)], axis=0)


def _attn_fwd(g, qk, v):
    s_len = qk.shape[0]
    d = DILATIONS[g]
    l = s_len // d
    tq, tk = _attn_tiles(l)
    assert l % tq == 0 and l >= tk
    q_spec, k_spec, v_spec, o_spec = _attn_specs(g, s_len)
    scale = 1.0 / math.sqrt(HEAD_DIM)

    def body(q_ref, k_ref, v_ref, o_ref, lse_ref, bias_ref):
        lane = lax.broadcasted_iota(jnp.int32, (1, 128), 1)
        in_h = [lane < HEAD_DIM, lane >= HEAD_DIM]
        _attn_fill_bias(bias_ref)

        def tile(t, carry):
            rows, win, kind = _attn_tile_geometry(t, d, l)
            q = (q_ref[rows, :] * scale).astype(BF16)
            k = k_ref[win, :].astype(BF16)
            vv = v_ref[win, :].astype(BF16)
            s = _nt(_split_heads(q, in_h), k) + bias_ref[kind]
            m = jnp.max(s, axis=1, keepdims=True)
            p = jnp.exp(s - m)
            den = jnp.sum(p, axis=1, keepdims=True)
            out = _nn(p, vv) / den
            lse = m + jnp.log(den)
            o_ref[rows, :] = jnp.where(in_h[0], out[:tq], out[tq:])
            lse_ref[rows, :] = jnp.where(in_h[0], lse[:tq], lse[tq:])
            return carry

        lax.fori_loop(0, s_len // tq, tile, 0, unroll=8 * ATT_TQ // tq)

    return pl.pallas_call(
        body, name=f"attn_fwd_g{g}", grid=(4,),
        out_shape=[jax.ShapeDtypeStruct((s_len, 512), F32)] * 2,
        in_specs=[q_spec, k_spec, v_spec], out_specs=[o_spec, o_spec],
        scratch_shapes=[pltpu.VMEM((3, 2 * tq, tk), F32)],
        compiler_params=_params(("parallel",), VMEM_BIG),
    )(qk, qk, v)


def _attn_bwd(g, qk, v, o, lse, do, dlse):
    s_len = qk.shape[0]
    d = DILATIONS[g]
    l = s_len // d
    tq, tk = _attn_tiles(l)
    q_spec, k_spec, v_spec, o_spec = _attn_specs(g, s_len)
    scale = 1.0 / math.sqrt(HEAD_DIM)

    def body(q_ref, k_ref, v_ref, o_ref, lse_ref, do_ref, dlse_ref, dq_ref, dk_ref, dv_ref, bias_ref):
        lane = lax.broadcasted_iota(jnp.int32, (1, 128), 1)
        in_h = [lane < HEAD_DIM, lane >= HEAD_DIM]
        dk_ref[...] = jnp.zeros_like(dk_ref)
        dv_ref[...] = jnp.zeros_like(dv_ref)
        _attn_fill_bias(bias_ref)

        def tile(t, carry):
            rows, win, kind = _attn_tile_geometry(t, d, l)
            k, vv = k_ref[win, :].astype(BF16), v_ref[win, :].astype(BF16)
            dout, lse_t, dlse_t = do_ref[rows, :], lse_ref[rows, :], dlse_ref[rows, :]
            od = dout * o_ref[rows, :]
            q2 = _split_heads((q_ref[rows, :] * scale).astype(BF16), in_h)
            do2 = _split_heads(dout.astype(BF16), in_h)
            head_col = lambda a: jnp.concatenate([a[:, 0:1], a[:, HEAD_DIM:HEAD_DIM + 1]], axis=0)
            delta = jnp.concatenate([jnp.sum(jnp.where(m, od, 0.0), axis=1, keepdims=True) for m in in_h], axis=0)
            p = jnp.exp(_nt(q2, k) + bias_ref[kind] - head_col(lse_t))
            ds = (p * (_nt(do2, vv) - delta + head_col(dlse_t))).astype(BF16)
            dq2 = _nn(ds, k) * scale
            dq_ref[rows, :] = jnp.where(in_h[0], dq2[:tq], dq2[tq:])
            dk_ref[win, :] += _tn(ds, q2)
            dv_ref[win, :] += _tn(p, do2)
            return carry

        lax.fori_loop(0, s_len // tq, tile, 0, unroll=8 * ATT_TQ // tq)

    return pl.pallas_call(
        body, name=f"attn_bwd_g{g}", grid=(4,),
        out_shape=[jax.ShapeDtypeStruct((s_len, 512), F32)] * 3,
        in_specs=[q_spec, k_spec, v_spec, o_spec, o_spec, o_spec, o_spec], out_specs=[o_spec] * 3,
        scratch_shapes=[pltpu.VMEM((3, 2 * tq, tk), F32)],
        compiler_params=_params(("parallel",), VMEM_BIG),
    )(qk, qk, v, o, lse, do, dlse)


def _mix_weights(ls):
    mx = jnp.maximum(jnp.maximum(ls[0], ls[1]), ls[2])
    es = [jnp.exp(x - mx) for x in ls]
    tot = es[0] + es[1] + es[2]
    return [e / tot for e in es]


def _attn_out(os_, lses, z, x, gate, w_out):
    s_len, dm = x.shape
    tm = 256
    wdt = 512
    z, z_block = z

    def body(o0, o1, o2, l0, l1, l2, z_ref, x_ref, g_ref, w_ref, a_ref, y_ref, x1_ref):
        alphas = _mix_weights([l0[...], l1[...], l2[...]])
        y = jnp.zeros((tm, dm), F32)
        for g, o_ref in enumerate((o0, o1, o2)):
            a_g = (o_ref[...] * alphas[g] * _silu(z_ref[:, g * wdt:(g + 1) * wdt])).astype(BF16)
            a_ref[:, g * wdt:(g + 1) * wdt] = a_g
            y = y + _nn(a_g, w_ref[g * wdt:(g + 1) * wdt, :])
        y_ref[...] = y
        x1_ref[...] = x_ref[...] + g_ref[...] * y

    row = lambda c: pl.BlockSpec((tm, c), lambda i: (i, 0))
    return pl.pallas_call(
        body, name="attn_out", grid=(s_len // tm,),
        out_shape=[jax.ShapeDtypeStruct((s_len, 3 * wdt), BF16), jax.ShapeDtypeStruct((s_len, dm), F32),
                   jax.ShapeDtypeStruct((s_len, dm), F32)],
        in_specs=[row(wdt)] * 6 + [pl.BlockSpec((tm, 3 * wdt), lambda i: (i, z_block)), row(dm),
                                   pl.BlockSpec((1, dm), lambda i: (0, 0)), pl.BlockSpec(w_out.shape, lambda i: (0, 0))],
        out_specs=[row(3 * wdt), row(dm), row(dm)],
        compiler_params=_params(("parallel",), VMEM_BIG),
    )(*os_, *lses, z, x, gate, w_out)


def _mix_bwd(dy, w_out, os_, lses, z):
    wdt = 512

    def fn(da, o0, o1, o2, l0, l1, l2, z):
        os_t, ls = [o0, o1, o2], [l0, l1, l2]
        alphas = _mix_weights(ls)
        hi = lax.broadcasted_iota(jnp.int32, (2 * wdt, wdt), 0) % wdt // HEAD_DIM
        hj = lax.broadcasted_iota(jnp.int32, (2 * wdt, wdt), 1) // HEAD_DIM
        seg = (hi == hj).astype(BF16)
        head_sum = lambda t: _dg(jnp.concatenate(_bf16_parts(t, 2), axis=1), seg, 1, 0)
        dos, dal, dzs = [], [], []
        for g in range(3):
            zg = z[:, g * wdt:(g + 1) * wdt]
            sig = jax.nn.sigmoid(zg)
            dag = da[:, g * wdt:(g + 1) * wdt]
            dmix = dag * zg * sig
            dzs.append(dag * os_t[g] * alphas[g] * (sig * (1.0 + zg * (1.0 - sig))))
            dos.append(dmix * alphas[g])
            dal.append(head_sum(dmix * os_t[g]))
        mean = alphas[0] * dal[0] + alphas[1] * dal[1] + alphas[2] * dal[2]
        dls = [alphas[g] * (dal[g] - mean) for g in range(3)]
        return dos + dls + [jnp.concatenate(dzs, axis=1)], []

    outs, _ = _matmul_rows("attn_out_dx_mix_bwd", dy, w_out, "nt", 256, dy.shape[1], fn, [*os_, *lses, (z[0], 3 * wdt, z[1])], [],
                           [(wdt, F32)] * 6 + [(3 * wdt, BF16)], [])
    return outs[:3], outs[3:6], outs[6]


def _rot_pack_bwd(dqs, dks, dvs, tabs):
    wdt = 512

    def fn(*args):
        grads, (c, sa, sb) = args[:9], args[9:]
        cols = [_rot_bwd(gq, c, sa, sb) for gq in grads[:6]] + list(grads[6:])
        return [jnp.concatenate(cols, axis=1)], []

    (out,), _ = _rowwise("rot_pack_bwd", fn, [*dqs, *dks, *dvs, *tabs], [], [(9 * wdt, BF16)], [], 512)
    return out


CONV_CB = 128
CONV_R = 256
CONV_PAD = 8


def _conv_taps(buf, base, off, sign):
    return [buf[pl.ds(base + off + sign * j, CONV_R), :] for j in range(CONV_WIDTH)]


def _conv_tap_sum(taps, w):
    acc = None
    for j, t in enumerate(taps):
        term = t * w[j:j + 1, :]
        acc = term if acc is None else acc + term
    return acc


def _conv_fwd(xpre, cw, cb):
    s_len, ch = xpre.shape
    nchunk = s_len // CONV_R

    def body(x_ref, w_ref, b_ref, o_ref, xp):
        zero = jnp.zeros((CONV_PAD, CONV_CB), F32)
        xp[0:CONV_PAD, :] = zero
        xp[s_len + CONV_PAD:s_len + 2 * CONV_PAD, :] = zero

        def fill(ci, carry):
            base = pl.multiple_of(ci * CONV_R, CONV_R)
            xp[pl.ds(base + CONV_PAD, CONV_R), :] = x_ref[pl.ds(base, CONV_R), :]
            return carry

        lax.fori_loop(0, nchunk, fill, 0)
        w = w_ref[...]
        b = b_ref[...]

        def chunk(ci, carry):
            base = pl.multiple_of(ci * CONV_R, CONV_R)
            u = _conv_tap_sum(_conv_taps(xp, base, CONV_PAD - CONV_WIDTH // 2, 1), w) + b
            o_ref[pl.ds(base, CONV_R), :] = _silu(u)
            return carry

        lax.fori_loop(0, nchunk, chunk, 0, unroll=2)

    col = lambda r: pl.BlockSpec((r, CONV_CB), lambda j: (0, j))
    return pl.pallas_call(
        body, name="conv_fwd", grid=(ch // CONV_CB,), out_shape=jax.ShapeDtypeStruct((s_len, ch), F32),
        in_specs=[col(s_len), col(CONV_WIDTH), col(1)], out_specs=col(s_len),
        scratch_shapes=[pltpu.VMEM((s_len + 2 * CONV_PAD, CONV_CB), F32)],
        compiler_params=_params(("parallel",), VMEM_BIG),
    )(xpre, cw, cb)


def _conv_bwd(xpre, da, cw, cb):
    s_len, ch = xpre.shape
    nchunk = s_len // CONV_R
    half = CONV_WIDTH // 2

    def body(x_ref, da_ref, w_ref, b_ref, dx_ref, gw_ref, gb_ref, xp, dcp):
        zero = jnp.zeros((CONV_PAD, CONV_CB), F32)
        for buf in (xp, dcp):
            buf[0:CONV_PAD, :] = zero
            buf[s_len + CONV_PAD:s_len + 2 * CONV_PAD, :] = zero

        def fill(ci, carry):
            base = pl.multiple_of(ci * CONV_R, CONV_R)
            xp[pl.ds(base + CONV_PAD, CONV_R), :] = x_ref[pl.ds(base, CONV_R), :]
            return carry

        lax.fori_loop(0, nchunk, fill, 0)
        w = w_ref[...]
        b = b_ref[...]

        def first(ci, carry):
            base = pl.multiple_of(ci * CONV_R, CONV_R)
            taps = _conv_taps(xp, base, CONV_PAD - half, 1)
            u = _conv_tap_sum(taps, w) + b
            sig = jax.nn.sigmoid(u)
            dc = da_ref[pl.ds(base, CONV_R), :] * (sig * (1.0 + u * (1.0 - sig)))
            dcp[pl.ds(base + CONV_PAD, CONV_R), :] = dc
            gb = carry[0] + jnp.sum(dc, axis=0, keepdims=True)
            gws = [carry[1 + j] + jnp.sum(dc * taps[j], axis=0, keepdims=True) for j in range(CONV_WIDTH)]
            return (gb, *gws)

        z1 = jnp.zeros((1, CONV_CB), F32)
        sums = lax.fori_loop(0, nchunk, first, (z1,) * (1 + CONV_WIDTH), unroll=2)
        gb_ref[...] = sums[0]
        for j in range(CONV_WIDTH):
            gw_ref[j:j + 1, :] = sums[1 + j]

        def second(ci, carry):
            base = pl.multiple_of(ci * CONV_R, CONV_R)
            dx_ref[pl.ds(base, CONV_R), :] = _conv_tap_sum(_conv_taps(dcp, base, CONV_PAD + half, -1), w).astype(dx_ref.dtype)
            return carry

        lax.fori_loop(0, nchunk, second, 0, unroll=2)

    col = lambda r: pl.BlockSpec((r, CONV_CB), lambda j: (0, j))
    return pl.pallas_call(
        body, name="conv_bwd", grid=(ch // CONV_CB,),
        out_shape=[jax.ShapeDtypeStruct((s_len, ch), BF16), jax.ShapeDtypeStruct((CONV_WIDTH, ch), F32),
                   jax.ShapeDtypeStruct((1, ch), F32)],
        in_specs=[col(s_len), col(s_len), col(CONV_WIDTH), col(1)],
        out_specs=[col(s_len), col(CONV_WIDTH), col(1)],
        scratch_shapes=[pltpu.VMEM((s_len + 2 * CONV_PAD, CONV_CB), F32)] * 2,
        compiler_params=_params(("parallel",), VMEM_BIG),
    )(xpre, da, cw, cb)


SSD_GW = 256
SSD_N = 128
SSD_DTW = 128


def _bf16_parts(x, n):
    parts, rest = [], x
    for _ in range(n):
        p = rest.astype(BF16)
        parts.append(p)
        rest = rest - p.astype(F32)
    return parts


@jax.custom_vjp
def _expand(x, e):
    eb = e.astype(BF16)
    return _dg(jnp.concatenate(_bf16_parts(x, 2), axis=1), jnp.concatenate([eb, eb], axis=0), 1, 0)


def _expand_fwd(x, e):
    return _expand(x, e), e


def _expand_bwd(e, g):
    return _dg(g.astype(BF16), e.astype(BF16), 1, 1), jnp.zeros_like(e)


_expand.defvjp(_expand_fwd, _expand_bwd)


@jax.custom_vjp
def _running_sum(tri, x):
    tb = tri.astype(BF16)
    return sum(_dg(tb, p, 1, 0) for p in _bf16_parts(x, 3))


def _running_sum_fwd(tri, x):
    return _running_sum(tri, x), tri


def _running_sum_bwd(tri, g):
    tb = tri.astype(BF16)
    return jnp.zeros_like(tri), sum(_dg(tb, p, 0, 0) for p in _bf16_parts(g, 3))


_running_sum.defvjp(_running_sum_fwd, _running_sum_bwd)


def _pick_col(a, h):
    @jax.custom_vjp
    def pick(a):
        return a[:, h:h + 1]

    pick.defvjp(lambda a: (a[:, h:h + 1], None),
                lambda _, g: (g * (lax.broadcasted_iota(jnp.int32, (1, a.shape[1]), 1) == h).astype(F32),))
    return pick(a)


def _pick_row(a, h):
    @jax.custom_vjp
    def pick(a):
        return a[h:h + 1, :]

    pick.defvjp(lambda a: (a[h:h + 1, :], None),
                lambda _, g: (g * (lax.broadcasted_iota(jnp.int32, (a.shape[0], 1), 0) == h).astype(F32),))
    return pick(a)


def _ssd_mask(dirn):
    ri = lax.broadcasted_iota(jnp.int32, (CHUNK, CHUNK), 0)
    cj = lax.broadcasted_iota(jnp.int32, (CHUNK, CHUNK), 1)
    return (cj <= ri) if dirn == 0 else (cj >= ri)


def _ssd_rowsel(dirn):
    last = CHUNK - 1 if dirn == 0 else 0
    return (lax.broadcasted_iota(jnp.int32, (CHUNK, 1), 0) == last).astype(F32)


def _ssd_chunk_pre(dirn):
    nh = SSD_DTW

    def f(dt, alog):
        da = dt * (-jnp.exp(alog))
        cum = _running_sum(_ssd_mask(dirn).astype(F32), da)
        tot = jnp.sum(cum * _ssd_rowsel(dirn), axis=0, keepdims=True)
        hh = lax.broadcasted_iota(jnp.int32, (nh, SSD_HEADS * HEAD_DIM), 0)
        jj = lax.broadcasted_iota(jnp.int32, (nh, SSD_HEADS * HEAD_DIM), 1)
        expand = (hh == dirn * SSD_HEADS + jj // HEAD_DIM).astype(F32)
        return cum, cum.T, _expand(dt, expand), _expand(jnp.exp(tot - cum), expand), _expand(jnp.exp(cum), expand)

    return f


def _ssd_group_fn(g, dirn, stacked):
    def f(xs, bm, cm, st, cum, cum_t, dt_e, w_e, ce_e):
        mask = _ssd_mask(dirn)
        xdt = xs * dt_e
        cd_e = jnp.sum(ce_e * _ssd_rowsel(dirn), axis=0, keepdims=True)
        cb = _bnt(cm, bm)
        lane_head = lax.broadcasted_iota(jnp.int32, (1, SSD_GW), 1) // HEAD_DIM
        y = _bnn(cm, st) * ce_e
        decayed, inputs = [], []
        for j in range(4):
            hidx = dirn * SSD_HEADS + 4 * g + j
            col, row = _pick_col(cum, hidx), _pick_row(cum_t, hidx)
            dec = cb * jnp.exp(jnp.where(mask, col - row, NEG_BIG))
            head = (lane_head == j).astype(F32)
            if stacked:
                decayed.append(dec)
                inputs.append(xdt * head)
            else:
                y = y + _bnn(dec, xdt) * head
        if stacked:
            y = y + _bnn(jnp.concatenate(decayed, axis=1), jnp.concatenate(inputs, axis=0))
        st_out = st * cd_e + _btn(bm, xdt * w_e)
        return y, st_out

    return f


def _ssd_in_specs(kk):
    ln = CHUNK
    return [pl.BlockSpec((ln, 2048), lambda i: (kk(i), 0)),
            pl.BlockSpec((ln, 1024), lambda i: (kk(i), 2)),
            pl.BlockSpec((ln, 1024), lambda i: (kk(i), 3)),
            pl.BlockSpec((ln, SSD_DTW), lambda i: (kk(i), 0)),
            pl.BlockSpec((1, SSD_DTW), lambda i: (0, 0))]


def _ssd_fwd(xbc, dt, alog, dirn, prior=None):
    s_len = xbc.shape[0]
    nc = s_len // CHUNK
    kk = (lambda i: i) if dirn == 0 else (lambda i: nc - 1 - i)

    def body(x_ref, b_ref, c_ref, dt_ref, al_ref, *rest):
        prior_ref = rest[0] if prior is not None else None
        y_ref, sts_ref, st = rest[prior is not None:]

        @pl.when(pl.program_id(0) == 0)
        def _():
            st[...] = jnp.zeros_like(st)

        sts_ref[0] = st[...]
        cum, cum_t, dt_e, w_e, ce_e = _ssd_chunk_pre(dirn)(dt_ref[...], al_ref[...])
        for g in range(SSD_GROUPS):
            xc = slice(g * SSD_GW, (g + 1) * SSD_GW)
            gc = slice(g * SSD_N, (g + 1) * SSD_N)
            y, st_new = _ssd_group_fn(g, dirn, True)(x_ref[:, xc], b_ref[:, gc], c_ref[:, gc], st[:, xc], cum, cum_t,
                                               dt_e[:, xc], w_e[:, xc], ce_e[:, xc])
            y_ref[:, xc] = y if prior is None else y + prior_ref[:, xc]
            st[:, xc] = st_new

    return pl.pallas_call(
        body, name=f"ssd_fwd_d{dirn}", grid=(nc,),
        out_shape=[jax.ShapeDtypeStruct((s_len, 2048), F32), jax.ShapeDtypeStruct((nc, SSD_N, 2048), F32)],
        in_specs=_ssd_in_specs(kk) + ([pl.BlockSpec((CHUNK, 2048), lambda i: (kk(i), 0))] if prior is not None else []),
        out_specs=[pl.BlockSpec((CHUNK, 2048), lambda i: (kk(i), 0)),
                   pl.BlockSpec((1, SSD_N, 2048), lambda i: (kk(i), 0, 0))],
        scratch_shapes=[pltpu.VMEM((SSD_N, 2048), F32)],
        compiler_params=_params(("arbitrary",), VMEM_BIG),
    )(xbc, xbc, xbc, dt, alog, *([prior] if prior is not None else []))


def _ssd_bwd(xbc, dt, alog, states, dy, d_e, dirn, prior=None):
    s_len = xbc.shape[0]
    nc = s_len // CHUNK
    kk = (lambda i: nc - 1 - i) if dirn == 0 else (lambda i: i)

    def body(x_ref, b_ref, c_ref, dt_ref, al_ref, sts_ref, dy_ref, de_ref, *rest):
        prior_ref = rest[0] if prior is not None else None
        dx_ref, ddt_ref, dal_ref, dst = rest[prior is not None:]
        plus_prior = (lambda v, cols: v + prior_ref[:, cols]) if prior is not None else (lambda v, cols: v)

        @pl.when(pl.program_id(0) == 0)
        def _():
            dst[...] = jnp.zeros_like(dst)
            dal_ref[...] = jnp.zeros_like(dal_ref)

        (cum, cum_t, dt_e, w_e, ce_e), pre_vjp = jax.vjp(_ssd_chunk_pre(dirn), dt_ref[...], al_ref[...])
        dcum = jnp.zeros_like(cum)
        dcum_t = jnp.zeros_like(cum_t)
        d_dt_e, d_w_e, d_ce_e = [], [], []
        for g in range(SSD_GROUPS):
            xc = slice(g * SSD_GW, (g + 1) * SSD_GW)
            gc = slice(g * SSD_N, (g + 1) * SSD_N)
            _, vjp = jax.vjp(_ssd_group_fn(g, dirn, False), x_ref[:, xc], b_ref[:, gc], c_ref[:, gc], sts_ref[0, :, xc], cum, cum_t,
                             dt_e[:, xc], w_e[:, xc], ce_e[:, xc])
            dyg = dy_ref[:, xc]
            dxs, dbm, dcm, dst_g, dcum_g, dcum_t_g, ddte_g, dwe_g, dcee_g = vjp((dyg, dst[:, xc]))
            if dirn == 0:
                dxs = dxs + dyg * de_ref[:, xc]
            bc, cc = slice(2048 + g * SSD_N, 2048 + (g + 1) * SSD_N), slice(3072 + g * SSD_N, 3072 + (g + 1) * SSD_N)
            dx_ref[:, xc] = plus_prior(dxs, xc)
            dx_ref[:, bc] = plus_prior(dbm, bc)
            dx_ref[:, cc] = plus_prior(dcm, cc)
            dst[:, xc] = dst_g
            dcum = dcum + dcum_g
            dcum_t = dcum_t + dcum_t_g
            d_dt_e.append(ddte_g)
            d_w_e.append(dwe_g)
            d_ce_e.append(dcee_g)
        ddt, dal = pre_vjp((dcum, dcum_t, jnp.concatenate(d_dt_e, axis=1), jnp.concatenate(d_w_e, axis=1),
                            jnp.concatenate(d_ce_e, axis=1)))
        ddt_ref[...] = ddt
        dal_ref[...] += dal

    return pl.pallas_call(
        body, name=f"ssd_bwd_d{dirn}", grid=(nc,),
        out_shape=[jax.ShapeDtypeStruct((s_len, 4096), F32), jax.ShapeDtypeStruct((s_len, SSD_DTW), F32),
                   jax.ShapeDtypeStruct((1, SSD_DTW), F32)],
        in_specs=_ssd_in_specs(kk) + [pl.BlockSpec((1, SSD_N, 2048), lambda i: (kk(i), 0, 0)),
                                      pl.BlockSpec((CHUNK, 2048), lambda i: (kk(i), 0)),
                                      pl.BlockSpec((1, 2048), lambda i: (0, 0))]
        + ([pl.BlockSpec((CHUNK, 4096), lambda i: (kk(i), 0))] if prior is not None else []),
        out_specs=[pl.BlockSpec((CHUNK, 4096), lambda i: (kk(i), 0)),
                   pl.BlockSpec((CHUNK, SSD_DTW), lambda i: (kk(i), 0)),
                   pl.BlockSpec((1, SSD_DTW), lambda i: (0, 0))],
        scratch_shapes=[pltpu.VMEM((SSD_N, 2048), F32)],
        compiler_params=_params(("arbitrary",), VMEM_BIG),
    )(xbc, xbc, xbc, dt, alog, states, dy, d_e, *([prior] if prior is not None else []))


def _gate_norm_fn(y, xs, z, d_e, nw):
    yg = (y + xs * d_e) * _silu(z)
    return yg * lax.rsqrt(jnp.mean(yg * yg, axis=-1, keepdims=True) + NORM_EPS) * nw


def _gate_norm_bwd(dy, w_out, y, xbc, z, d_e, nw):
    def fn(du, y, xs, z, d_e, nw):
        sig = jax.nn.sigmoid(z)
        gate = z * sig
        ysum = y + xs * d_e
        yg = ysum * gate
        r = lax.rsqrt(jnp.mean(yg * yg, axis=-1, keepdims=True) + NORM_EPS)
        t = du * nw
        dyg = t * r - yg * (jnp.mean(t * yg, axis=-1, keepdims=True) * (r * r * r))
        dys = dyg * gate
        dz = dyg * ysum * (sig * (1.0 + z * (1.0 - sig)))
        dnw = jnp.sum(du * yg * r, axis=0, keepdims=True)
        dde = jnp.sum(dys * xs, axis=0, keepdims=True)
        return [dys, dz], [dnw, dde]

    (dys, dz), (g_nw, g_d) = _matmul_rows("ssd_out_dx_gate_norm_bwd", dy, w_out, "nt", 256, dy.shape[1], fn,
                                          [y, (xbc, 2048, 0), z], [d_e, nw], [(2048, F32), (2048, BF16)],
                                          [(1, 2048), (1, 2048)])
    return dys, dz, g_nw, g_d


def _ssd_tail_loss(y, xbc, z, d_e, snw, w_out, x1, tgt, gate, fnw):
    dm = x1.shape[1]
    si = y.shape[1]

    def make_u(y, xs, z, x1, tgt, d_e, snw, gate, fnw):
        return _gate_norm_fn(y, xs, z, d_e, snw).astype(BF16)

    def fn(y1, u, y, xs, z, x1, tgt, d_e, snw, gate, fnw):
        x2 = x1 + gate * y1
        r = lax.rsqrt(jnp.mean(x2 * x2, axis=-1, keepdims=True) + NORM_EPS)
        xh = x2 * r
        err = xh * fnw - tgt
        loss = 0.5 * jnp.sum(jnp.mean(err * err, axis=-1, keepdims=True), axis=0, keepdims=True)
        dy = err * (1.0 / dm)
        dxh = dy * fnw
        dx2 = r * (dxh - xh * jnp.mean(dxh * xh, axis=-1, keepdims=True))
        dfnw = jnp.sum(dy * xh, axis=0, keepdims=True)
        return [u, dx2, gate * dx2], [dfnw, jnp.sum(dx2 * y1, axis=0, keepdims=True), jnp.broadcast_to(loss, (1, 128))]

    (u, dx2, dy1), (g_fnw, dgate, loss) = _matmul_rows(
        "ssd_out_loss", make_u, w_out, "nn", 256, si, fn, [y, (xbc, si, 0), z, x1, tgt], [d_e, snw, gate, fnw],
        [(si, BF16), (dm, F32), (dm, BF16)], [(1, dm), (1, dm), (1, 128)])
    return u, dx2, dy1, g_fnw, dgate, loss


def _softplus_fwd(dt_raw, bias):
    (dt,), _ = _rowwise("dt_softplus", lambda r, b: ([jax.nn.softplus(r + b)], []), [dt_raw], [bias],
                        [(dt_raw.shape[1], F32)], [], 512)
    return dt


def _softplus_bwd(ddt_f, ddt_b, dt_raw, bias):
    def fn(df, db, r, b):
        g = (df + db) * jax.nn.sigmoid(r + b)
        return [g], [jnp.sum(g, axis=0, keepdims=True)]

    w = dt_raw.shape[1]
    (g,), (gb,) = _rowwise("dt_softplus_bwd", fn, [ddt_f, ddt_b, dt_raw], [bias], [(w, BF16)], [(1, w)], 512)
    return g, gb


def _mod_part(c_all, mod_w):
    nl, _, ncol = mod_w.shape
    nb = c_all.shape[0]

    def body(c_ref, w_ref, o_ref):
        cond = _silu(c_ref[...])
        for i in range(nl):
            o_ref[i * nb:(i + 1) * nb, :] = _nn(cond, w_ref[i])

    return pl.pallas_call(body, name="mod_part", out_shape=jax.ShapeDtypeStruct((nl * nb, ncol), F32),
                          compiler_params=_params(None, VMEM_BIG))(c_all, mod_w)


def _mod_finish(mod_nb, mod_b, norm_w, tokens):
    nl, dm = norm_w.shape

    def body(a_ref, b_ref, nw_ref, *rest):
        tok_refs, o_refs = rest[:len(tokens)], rest[len(tokens):]
        tok = sum(t[0:1, 0:1] for t in tok_refs)
        for i in range(nl):
            for k in range(3):
                cols = slice(k * dm, (k + 1) * dm)
                o_refs[4 * i + k][...] = a_ref[i:i + 1, cols] + b_ref[i:i + 1, cols]
            o_refs[4 * i + 3][...] = nw_ref[i:i + 1, :] + tok

    rows = pl.pallas_call(body, name="mod_finish", out_shape=[jax.ShapeDtypeStruct((1, dm), F32)] * (4 * nl))(
        mod_nb, mod_b, norm_w, *tokens)
    return [rows[4 * i:4 * i + 4] for i in range(nl)]


def _mod_grad(c_all, dmod_sh):
    nl, nb, ncol = dmod_sh.shape
    dm = c_all.shape[1]

    def body(c_ref, d_ref, o_ref):
        cond = _silu(c_ref[...])
        for i in range(nl):
            o_ref[i] = _tn(cond, d_ref[i])

    return pl.pallas_call(body, name="mod_grad", out_shape=jax.ShapeDtypeStruct((nl, dm, ncol), F32),
                          compiler_params=_params(None, VMEM_BIG))(c_all, dmod_sh)


PACK_ROWS = 16
PACK_COLS = 1024


def _pack_small(rows, b64, a64s, d32, extra):
    nr, na = len(rows), len(a64s)

    def body(*refs):
        o_ref = refs[-1]
        o_ref[...] = jnp.zeros_like(o_ref)
        for i in range(nr):
            o_ref[i:i + 1, :] = refs[i][...]
        b_ref, a_refs, d_ref, e_ref = refs[nr], refs[nr + 1:nr + 1 + na], refs[nr + 1 + na], refs[nr + 2 + na]
        o_ref[nr:nr + 1, 0:64] = b_ref[:, 0:64]
        o_ref[nr:nr + 1, 64:128] = sum(a[:, 0:64] for a in a_refs)
        nch = d_ref.shape[1]
        head = lax.broadcasted_iota(jnp.int32, (nch, SSD_HEADS), 0) // HEAD_DIM
        col = lax.broadcasted_iota(jnp.int32, (nch, SSD_HEADS), 1)
        o_ref[nr:nr + 1, 128:160] = _hnn(jnp.broadcast_to(d_ref[...], (8, nch)), (head == col).astype(F32))[0:1]
        o_ref[nr:nr + 1, 256:384] = e_ref[...]

    return pl.pallas_call(body, name="pack_small", out_shape=jax.ShapeDtypeStruct((PACK_ROWS, PACK_COLS), F32))(
        *rows, b64, *a64s, d32, extra)


def _pack_ssd_small(cw, cb, nw):
    def body(cw_ref, cb_ref, nw_ref, o_ref):
        o_ref[...] = jnp.zeros_like(o_ref)
        o_ref[0:5, :] = cw_ref[...]
        o_ref[5:6, :] = cb_ref[...]
        o_ref[6:7, 0:256] = nw_ref[...]

    return pl.pallas_call(body, name="pack_ssd_small", out_shape=jax.ShapeDtypeStruct((8, 512), F32))(cw, cb, nw)


def _sum_parts(p_ref):
    g = p_ref[0].astype(F32)
    for s in range(1, p_ref.shape[0]):
        g = g + p_ref[s].astype(F32)
    return g


def _adam_update(w, g, m, v):
    m2 = ADAM_B1 * m + (1.0 - ADAM_B1) * g
    v2 = ADAM_B2 * v + (1.0 - ADAM_B2) * (g * g)
    m_hat = m2 / (1.0 - ADAM_B1 ** ADAM_STEP)
    v_hat = v2 / (1.0 - ADAM_B2 ** ADAM_STEP)
    return -ADAM_LR * (m_hat / (jnp.sqrt(v_hat) + ADAM_EPS) + ADAM_WD * w), m2, v2


def _adamw_windows(name, parts, params, windows, extra=None):
    n = len(params)

    def body(p_ref, *rest):
        ins, outs = rest[:3 * n], rest[3 * n:]
        g = _sum_parts(p_ref)
        for pi, rows, cols, idx in windows:
            w_ref, m_ref, v_ref = ins[3 * pi:3 * pi + 3]
            gw = g[rows, cols]
            dw, m2, v2 = _adam_update(w_ref[idx], gw, m_ref[idx], v_ref[idx])
            for o_ref, val in zip(outs[4 * pi:4 * pi + 4], (gw, dw, m2, v2), strict=True):
                o_ref[idx] = val
        if extra is not None:
            outs[4 * n][...] = g[extra[0], extra[1]]

    out_shape = [jax.ShapeDtypeStruct(w.shape, F32) for (w, _, _) in params for _ in range(4)]
    if extra is not None:
        out_shape.append(jax.ShapeDtypeStruct((extra[0].stop - extra[0].start, extra[1].stop - extra[1].start), F32))
    res = pl.pallas_call(body, name=name, out_shape=out_shape)(parts, *[a for p in params for a in p])
    return [res[4 * i:4 * i + 4] for i in range(n)] + ([res[4 * n]] if extra is not None else [])


def _adamw(name, w, parts, m, v, tr, tc=None):
    r_, c_ = w.shape
    p_ = parts.shape[0]
    tr = min(tr, r_)
    tc = c_ if tc is None else tc
    assert r_ % tr == 0 and c_ % tc == 0

    def body(w_ref, p_ref, m_ref, v_ref, g_ref, d_ref, m2_ref, v2_ref):
        g = _sum_parts(p_ref)
        g_ref[...] = g
        d_ref[...], m2_ref[...], v2_ref[...] = _adam_update(w_ref[...], g, m_ref[...], v_ref[...])

    blk = pl.BlockSpec((tr, tc), lambda i, j: (i, j))
    return pl.pallas_call(
        body, name=name, grid=(r_ // tr, c_ // tc), out_shape=[jax.ShapeDtypeStruct((r_, c_), F32)] * 4,
        in_specs=[blk, pl.BlockSpec((p_, tr, tc), lambda i, j: (0, i, j)), blk, blk], out_specs=[blk] * 4,
        compiler_params=_params(("parallel", "parallel"), VMEM_BIG),
    )(w, parts, m, v)


def _dev_index(p):
    return 4 * p[0] + 2 * p[1] + p[2]


def _all_gather(name, xs):
    n = len(xs)
    hbm = pl.BlockSpec(memory_space=pl.ANY)

    def body(*refs):
        x_refs, o_refs = refs[:n], refs[n:2 * n]
        send_sems, recv_sems, local_sems = refs[2 * n:]
        x, y, c = lax.axis_index("x"), lax.axis_index("y"), lax.axis_index("c")
        me, sibling = (x, y, c), (x, y, 1 - c)
        chips = [(1 - x, y), (x, 1 - y), (1 - x, 1 - y)]

        def place(a, block):
            return o_refs[a].at[_dev_index(block)]

        def copy(a, k, block, to, src=None):
            dst = place(a, block)
            return pltpu.make_async_remote_copy(
                src_ref=dst if src is None else src, dst_ref=dst, send_sem=send_sems.at[a, k],
                recv_sem=recv_sems.at[a, k], device_id=to, device_id_type=MESH)

        mine = [pltpu.make_async_copy(x_refs[a], place(a, me), local_sems.at[a]) for a in range(n)]
        for cp in mine:
            cp.start()
        first = []
        for a in range(n):
            first.append(copy(a, 0, me, sibling, src=x_refs[a]))
            first += [copy(a, 1 + j, me, (*chip, c), src=x_refs[a]) for j, chip in enumerate(chips)]
        for cp in first:
            cp.start()
        passed = []
        for j, chip in enumerate(chips):
            for a in range(n):
                copy(a, 1 + j, (*chip, c), me).wait_recv()
                cp = copy(a, 4 + j, (*chip, c), sibling)
                cp.start()
                passed.append(cp)
        for a in range(n):
            copy(a, 0, sibling, me).wait_recv()
            for j, chip in enumerate(chips):
                copy(a, 4 + j, (*chip, 1 - c), me).wait_recv()
        for cp in first + passed:
            cp.wait_send()
        for cp in mine:
            cp.wait()

    return pl.pallas_call(
        body, name=name, out_shape=[jax.ShapeDtypeStruct((NDEV, *x.shape), x.dtype) for x in xs],
        in_specs=[hbm] * n, out_specs=[hbm] * n,
        scratch_shapes=[pltpu.SemaphoreType.DMA((n, 7)), pltpu.SemaphoreType.DMA((n, 7)), pltpu.SemaphoreType.DMA((n,))],
    )(*xs)


_HBM = pl.BlockSpec(memory_space=pltpu.HBM)
_SEM = pl.BlockSpec(memory_space=pltpu.SEMAPHORE)
_EFFECT = pltpu.SideEffectType.DATAFLOW_SIDE_EFFECTING


def _mesh_position():
    return lax.axis_index("x"), lax.axis_index("y"), lax.axis_index("c")


def _peers(me):
    return [(k, tuple(1 - v if (k >> b) & 1 else v for v, b in zip(me, (2, 1, 0)))) for k in range(1, NDEV)]


def _column_window(ref, block, width):
    return ref.at[:, pl.ds(pl.multiple_of(_dev_index(block) * width, 128), width)]


RELAY_COPIES = 3


def _relay_copies(o_ref, send_sems, recv_sems, with_arrivals):
    width = o_ref.shape[1] // NDEV
    x, y, c = me = _mesh_position()
    sibling = (x, y, 1 - c)
    x_side, y_side, diagonal = (1 - x, y), (x, 1 - y), (1 - x, 1 - y)
    first = c == 0
    via = (jnp.where(first, 1 - x, x), jnp.where(first, y, 1 - y))
    to = (jnp.where(first, x, 1 - x), jnp.where(first, 1 - y, y))

    def copy(k, block, device):
        window = _column_window(o_ref, block, width)
        return pltpu.make_async_remote_copy(src_ref=window, dst_ref=window, send_sem=send_sems.at[k],
                                            recv_sem=recv_sems.at[k], device_id=device, device_id_type=MESH)

    sent = [copy(0, (*via, c), (*to, c)), copy(1, (*x_side, c), sibling), copy(2, (*y_side, c), sibling)]
    if not with_arrivals:
        return sent
    arrivals =[copy(0, (*diagonal, c), me), copy(1, (*x_side, 1 - c), me), copy(2, (*y_side, 1 - c), me)]
    return sent, arrivals


def _relay_start(name, gathered, dep):
    def body(g_ref, dep_ref, send_sems, recv_sems, o_ref, token):
        for cp in _relay_copies(g_ref, send_sems, recv_sems, with_arrivals=False):
            cp.start()
        token[...] = jnp.zeros_like(token)

    sems = pltpu.SemaphoreType.DMA((RELAY_COPIES,))
    res = pl.pallas_call(
        body, name=name,
        out_shape=(sems, sems, pltpu.HBM(gathered.shape, gathered.dtype), jax.ShapeDtypeStruct((8, 128), F32)),
        in_specs=[_HBM, pl.BlockSpec(memory_space=pl.ANY)],
        out_specs=(_SEM, _SEM, _HBM, pl.BlockSpec(memory_space=pltpu.VMEM)),
        input_output_aliases={0: 2},
        compiler_params=pltpu.CompilerParams(has_side_effects=_EFFECT),
    )(gathered, dep)
    return res[:-1], res[-1]


def _relay_wait(name, handles, after):
    send_sems, recv_sems, gathered = handles

    def body(g_ref, s_sems, r_sems, after_ref, o_ref):
        sent, arrivals = _relay_copies(g_ref, s_sems, r_sems, with_arrivals=True)
        for cp, arrival in zip(sent, arrivals):
            cp.wait_send()
            arrival.wait_recv()

    return pl.pallas_call(
        body, name=name, out_shape=pltpu.HBM(gathered.shape, gathered.dtype),
        in_specs=[_HBM, _SEM, _SEM, pl.BlockSpec(memory_space=pl.ANY)], out_specs=_HBM, input_output_aliases={0: 0},
        compiler_params=pltpu.CompilerParams(has_side_effects=_EFFECT),
    )(gathered, send_sems, recv_sems, after)


def _columns_last(name, gathered):
    width = gathered.shape[1] // NDEV
    hbm = pl.BlockSpec(memory_space=pl.ANY)

    def body(g_ref, o_ref, send_sem, recv_sem):
        x, y, c = _mesh_position()

        def copy(core, device):
            window = _column_window(o_ref, (1 - x, 1 - y, core), width)
            return pltpu.make_async_remote_copy(src_ref=window, dst_ref=window, send_sem=send_sem, recv_sem=recv_sem,
                                                device_id=device, device_id_type=MESH)

        onward = copy(c, (x, y, 1 - c))
        onward.start()
        copy(1 - c, (x, y, c)).wait_recv()
        onward.wait_send()

    return pl.pallas_call(
        body, name=name, out_shape=jax.ShapeDtypeStruct(gathered.shape, gathered.dtype), in_specs=[hbm], out_specs=hbm,
        input_output_aliases={0: 0}, scratch_shapes=[pltpu.SemaphoreType.DMA, pltpu.SemaphoreType.DMA],
    )(gathered)


NCHIP = NDEV // 2
EXCHANGE_COPIES = {"columns": NCHIP - 1, "gather": NDEV - 1, "scatter": NDEV - 1, "pair": NCHIP, "chips": NCHIP - 1}


def _landing_zones(name, xs, mode):
    x_, y_, c_ = _mesh_position()
    mine = (2 * x_ + y_ if mode == "chips" else _dev_index((x_, y_, c_))).astype(jnp.int32).reshape(1)
    lands = []
    for a, x in enumerate(xs):
        rows, cols = x.shape[-2:]
        if mode == "pair":
            lands.append(lax.empty((NCHIP, rows, cols), x.dtype))
            continue
        tr = 512 if rows % 512 == 0 else rows

        def body(me_ref, x_ref, o_ref):
            o_ref[...] = x_ref[...]

        if mode in ("gather", "columns"):
            in_spec = pl.BlockSpec((tr, cols), lambda i, me_ref: (i, 0))
        else:
            in_spec = pl.BlockSpec((None, tr, cols), lambda i, me_ref: (me_ref[0], i, 0))
        if mode == "columns":
            out_shape, out_spec = (rows, NDEV * cols), pl.BlockSpec((tr, cols), lambda i, me_ref: (i, me_ref[0]))
        else:
            out_shape = (NCHIP if mode == "chips" else NDEV, rows, cols)
            out_spec = pl.BlockSpec((None, tr, cols), lambda i, me_ref: (me_ref[0], i, 0))
        lands.append(pl.pallas_call(
            body, name=f"{name}_{a}", out_shape=jax.ShapeDtypeStruct(out_shape, x.dtype),
            grid_spec=pltpu.PrefetchScalarGridSpec(num_scalar_prefetch=1, grid=(rows // tr,), in_specs=[in_spec],
                                                   out_specs=out_spec),
            compiler_params=_params(("arbitrary",)),
        )(mine, x))
    return lands


def _exchange_copies(x_refs, land_refs, send_sems, recv_sems, mode):
    x_, y_, c_ = me = _mesh_position()
    per_array = EXCHANGE_COPIES[mode]
    out = []

    def add(a, k, src, dst, peer):
        sem = a * per_array + k
        out.append(pltpu.make_async_remote_copy(src_ref=src, dst_ref=dst, send_sem=send_sems.at[sem], recv_sem=recv_sems.at[sem],
                                                device_id=peer, device_id_type=MESH))

    for a, (x_ref, land_ref) in enumerate(zip(x_refs, land_refs)):
        if mode == "columns":
            for k, peer in enumerate([(x_, y_, 1 - c_), (1 - x_, y_, c_), (x_, 1 - y_, c_)]):
                add(a, k, x_ref, _column_window(land_ref, me, x_ref.shape[1]), peer)
        elif mode in ("gather", "scatter"):
            for k, peer in _peers(me):
                add(a, k - 1, x_ref.at[_dev_index(peer)] if mode == "scatter" else x_ref, land_ref.at[_dev_index(me)], peer)
        elif mode == "pair":
            for chip in range(NCHIP):
                add(a, chip, x_ref.at[2 * chip + 1 - c_], land_ref.at[chip], (x_, y_, 1 - c_))
        else:
            for k in range(1, NCHIP):
                px, py = (1 - x_ if k & 2 else x_), (1 - y_ if k & 1 else y_)
                add(a, k - 1, x_ref.at[2 * px + py], land_ref.at[2 * x_ + y_], (px, py, c_))
    return out


def _exchange_start(name, xs, lands, mode, dep, carry=False):
    n = len(xs)

    def body(*refs):
        x_refs, land_refs = refs[:n], refs[n:2 * n]
        send_sems, recv_sems = refs[2 * n + 1], refs[2 * n + 2]
        for cp in _exchange_copies(x_refs, land_refs, send_sems, recv_sems, mode):
            cp.start()
        if not carry:
            refs[-1][...] = jnp.zeros_like(refs[-1])

    sems = pltpu.SemaphoreType.DMA((n * EXCHANGE_COPIES[mode],))
    moved = [pltpu.with_memory_space_constraint(a, pltpu.HBM) for a in (*xs, *lands, *([dep] if carry else []))]
    res = pl.pallas_call(
        body, name=name,
        out_shape=(sems, sems, *[pltpu.HBM(a.shape, a.dtype) for a in moved],
                   *([] if carry else [jax.ShapeDtypeStruct((8, 128), F32)])),
        in_specs=[_HBM] * len(moved) + ([] if carry else [pl.BlockSpec(memory_space=pl.ANY)]),
        out_specs=(_SEM, _SEM, *[_HBM] * len(moved), *([] if carry else [pl.BlockSpec(memory_space=pltpu.VMEM)])),
        input_output_aliases={i: 2 + i for i in range(len(moved))},
        compiler_params=pltpu.CompilerParams(has_side_effects=_EFFECT),
    )(*moved, *([] if carry else [dep]))
    return res[:-1], res[-1]


def _exchange_wait(name, handles, mode, after, with_sources=False):
    send_sems, recv_sems = handles[0], handles[1]
    bufs = handles[2:]
    n = len(bufs) // 2
    afters = list(after) if isinstance(after, (list, tuple)) else [after]

    def body(*refs):
        x_refs, land_refs = refs[:n], refs[n:2 * n]
        s_sems, r_sems = refs[2 * n], refs[2 * n + 1]
        for cp in _exchange_copies(x_refs, land_refs, s_sems, r_sems, mode):
            cp.wait_send()
            cp.wait_recv()

    res = pl.pallas_call(
        body, name=name, out_shape=tuple(pltpu.HBM(a.shape, a.dtype) for a in bufs),
        in_specs=[_HBM] * (2 * n) + [_SEM, _SEM] + [pl.BlockSpec(memory_space=pl.ANY)] * len(afters),
        out_specs=tuple([_HBM] * (2 * n)), input_output_aliases={i: i for i in range(2 * n)},
        compiler_params=pltpu.CompilerParams(has_side_effects=_EFFECT),
    )(*bufs, send_sems, recv_sems, *afters)
    return (res[n:], res[:n]) if with_sources else res[n:]


def _pair_sum(name, x, from_sibling):
    _, rows, cols = x.shape
    tr = rows
    core = lax.axis_index("c").astype(jnp.int32).reshape(1)

    def body(c_ref, x_ref, s_ref, o_ref):
        o_ref[...] = (x_ref[...].astype(F32) + s_ref[...].astype(F32)).astype(o_ref.dtype)

    return pl.pallas_call(
        body, name=name, out_shape=jax.ShapeDtypeStruct((NCHIP, rows, cols), x.dtype),
        grid_spec=pltpu.PrefetchScalarGridSpec(
            num_scalar_prefetch=1, grid=(NCHIP, rows // tr),
            in_specs=[pl.BlockSpec((None, tr, cols), lambda j, i, c_ref: (2 * j + c_ref[0], i, 0)),
                      pl.BlockSpec((None, tr, cols), lambda j, i, c_ref: (j, i, 0))],
            out_specs=pl.BlockSpec((None, tr, cols), lambda j, i, c_ref: (j, i, 0))),
        compiler_params=_params(("parallel", "parallel")),
    )(core, x, from_sibling)


def kernel(x, c, positions, norm_w, mod_w, mod_b, attn_w_in, attn_w_out, ssd_w_in, ssd_conv_w, ssd_conv_b, ssd_dt_bias, ssd_a_log, ssd_d, ssd_norm_w, ssd_w_out, final_norm_w, loss_target, m_norm_w, m_mod_w, m_mod_b, m_attn_w_in, m_attn_w_out, m_ssd_w_in, m_ssd_conv_w, m_ssd_conv_b, m_ssd_dt_bias, m_ssd_a_log, m_ssd_d, m_ssd_norm_w, m_ssd_w_out, m_final_norm_w, v_norm_w, v_mod_w, v_mod_b, v_attn_w_in, v_attn_w_out, v_ssd_w_in, v_ssd_conv_w, v_ssd_conv_b, v_ssd_dt_bias, v_ssd_a_log, v_ssd_d, v_ssd_norm_w, v_ssd_w_out, v_final_norm_w):
    s_len, dm = x.shape[1], x.shape[2]
    me = 4 * lax.axis_index("x") + 2 * lax.axis_index("y") + lax.axis_index("c")
    x0 = x.reshape(s_len, dm)
    tgt = loss_target.reshape(s_len, dm)
    aw = 3 * 512
    si = 2 * dm
    sxbc = 2 * si
    n_ssd_in = ssd_w_in.shape[2] * NDEV

    (c_all,) = _all_gather("gather_c", [c])
    c_all = c_all.reshape(NDEV, dm)
    part = _mod_part(c_all, mod_w)
    (part_all,) = _all_gather("gather_mod", [part])
    mod_nb = jnp.stack([lax.dynamic_index_in_dim(part_all, i * NDEV + me, axis=1, keepdims=False).reshape(3 * dm)
                        for i in range(2)])

    wcol = attn_w_in.shape[2]
    ai_shard = [attn_w_in[0].astype(BF16)]
    ai_handles, ai_token = _exchange_start("attn_w_in_start", ai_shard, _landing_zones("attn_w_in_place", ai_shard, "columns"),
                                           "columns", part_all)
    inv_freq = ROPE_THETA ** (-jnp.arange(0, ROT_DIM, 2, dtype=F32) / ROT_DIM)
    per_head = jnp.concatenate([inv_freq, inv_freq, jnp.zeros(HEAD_DIM - ROT_DIM, F32)])
    inv_row = jnp.tile(per_head, 128 // HEAD_DIM).reshape(1, 128) + ai_token[0:1]
    tabs = _rope_tables(positions.reshape(s_len, 1), inv_row)
    ssd_small = _pack_ssd_small(ssd_conv_w[0], ssd_conv_b, ssd_norm_w)
    ao_shard = [attn_w_out[0].astype(BF16)]
    late_shards = [ssd_w_in[0].T.astype(BF16), ssd_w_out[0].astype(BF16), ssd_small]
    ao_lands = _landing_zones("w_out_place", ao_shard, "gather")
    late_lands = _landing_zones("weights_place", late_shards, "gather")
    (w_ai,) = _exchange_wait("attn_w_in_wait", ai_handles, "columns", [*tabs, *ao_lands, *late_lands])
    relay_handles, relay_token = _relay_start("attn_w_in_relay_start", w_ai, tabs[0])
    (shift0, scale0, gate0, nw0), (shift1, scale1, gate1, nw1) = _mod_finish(mod_nb, mod_b, norm_w, [relay_token])
    shift, scale, gate, nw = [shift0, shift1], [scale0, scale1], [gate0, gate1], [nw0, nw1]
    hn0 = _norm_mod_fwd("norm0", x0, nw[0], scale[0], shift[0])
    w_ai = _columns_last("gather_attn_w_in_last", _relay_wait("attn_w_in_relay_wait", relay_handles, hn0))

    ao_handles, w_ai = _exchange_start("w_out_start", ao_shard, ao_lands, "gather", w_ai, carry=True)
    w_handles, w_ai = _exchange_start("weights_start", late_shards, late_lands, "gather", w_ai, carry=True)

    qk = _matmul("proj_qk", hn0, w_ai, "nn", F32, MM_T, MM_T, dm, epilogue=_rot_fwd, mrows=tabs, n_out=2 * aw)
    v = _matmul("proj_vz", hn0, w_ai, "nn", F32, MM_T, MM_T, dm, b_noff=2 * aw, n_out=2 * aw)
    z0 = (v, 1)
    att = [_attn_fwd(g, qk, v) for g in range(3)]
    os_, lses = [a[0] for a in att], [a[1] for a in att]
    (g_ao,) = _exchange_wait("w_out_wait", ao_handles, "gather", lses[2])
    a0, y0, x1 = _attn_out(os_, lses, z0, x0, gate[0], g_ao.reshape(aw, dm))

    hn1 = _norm_mod_fwd("norm1", x1, nw[1], scale[1], shift[1])
    g_si, g_so, g_small = _exchange_wait("weights_wait", w_handles, "gather", hn1)
    w_ao = g_ao.reshape(aw, dm)
    w_si_t = g_si.reshape(n_ssd_in, dm)
    w_so = g_so.reshape(si, dm)
    conv_w = g_small[:, 0:CONV_WIDTH, :].transpose(1, 0, 2).reshape(CONV_WIDTH, sxbc)
    conv_b = g_small[:, 5, :].reshape(1, sxbc)
    snw = g_small[:, 6, 0:si // NDEV].reshape(1, si)
    ndt = 2 * SSD_HEADS
    z1 = _matmul("ssd_proj_z", hn1, w_si_t, "nt", F32, MM_T, MM_T, dm, n_out=si)
    xpre = _matmul("ssd_proj_xbc", hn1, w_si_t, "nt", F32, MM_T, MM_T, dm, b_noff=si, n_out=sxbc)
    dt_raw = _matmul("ssd_proj_dt", hn1, w_si_t, "nt", F32, MM_T, ndt, dm, b_noff=si + sxbc, n_out=ndt)
    xbc = _conv_fwd(xpre, conv_w, conv_b)
    widen = lambda a: jnp.pad(a, ((0, 0), (0, SSD_DTW - ndt)))
    dt_raw = widen(dt_raw)
    dt_bias = widen(ssd_dt_bias.reshape(1, ndt))
    alog = widen(ssd_a_log.reshape(1, ndt))
    dt = _softplus_fwd(dt_raw, dt_bias)
    y_f, st_f = _ssd_fwd(xbc, dt, alog, 0)
    y_fb, st_b = _ssd_fwd(xbc, dt, alog, 1, prior=y_f)
    d_e = jnp.repeat(ssd_d.reshape(SSD_HEADS), HEAD_DIM).reshape(1, si)

    fnw = final_norm_w.reshape(1, dm)
    u, dx2, dy1, g_fnw, dgate1, loss_part = _ssd_tail_loss(y_fb, xbc, z1, d_e, snw, w_so, x1, tgt, gate[1], fnw)
    gw_so = _matmul("ssd_out_dw", u, dy1, "tn", BF16, MM_T, MM_T, MM_T)
    dys, dz1, g_snw, g_d = _gate_norm_bwd(dy1, w_so, y_fb, xbc, z1, d_e, snw)
    dxbc_f, ddt_f, dalog_f = _ssd_bwd(xbc, dt, alog, st_f, dys, d_e, 0)
    dxbc, ddt_b, dalog_b = _ssd_bwd(xbc, dt, alog, st_b, dys, d_e, 1, prior=dxbc_f)
    dpre, g_cw, g_cb = _conv_bwd(xpre, dxbc, conv_w, conv_b)
    ddt_raw, g_dtb = _softplus_bwd(ddt_f, ddt_b, dt_raw, dt_bias)
    ddt_raw = ddt_raw[:, :ndt]
    dhn1 = [_matmul("ssd_proj_z_dx", dz1, w_si_t, "nn", F32, MM_T, MM_T, MM_T),
            _matmul("ssd_proj_xbc_dx", dpre, w_si_t, "nn", F32, MM_T, MM_T, MM_T, b_koff=si)]
    gw_si_t = _matmul("ssd_proj_z_dw", dz1, hn1, "tn", BF16, MM_T, MM_T, MM_T, dest=(n_ssd_in, 0, None))
    gw_si_t = _matmul("ssd_proj_xbc_dw", dpre, hn1, "tn", BF16, MM_T, MM_T, MM_T, dest=(n_ssd_in, si, gw_si_t))
    gw_si_t = _matmul("ssd_proj_dt_dw", ddt_raw, hn1, "tn", BF16, ndt, MM_T, MM_T, dest=(n_ssd_in, si + sxbc, gw_si_t))

    l1_grads = [gw_so.reshape(NDEV, si // NDEV, dm), gw_si_t.reshape(NDEV, n_ssd_in // NDEV, dm),
                _pack_ssd_small_blocks(g_cw, g_cb, g_snw)]
    l1_handles, l1_token = _exchange_start("l1_grads_start", l1_grads, _landing_zones("l1_grads_place", l1_grads, "scatter"),
                                           "scatter", dhn1[1])
    dx1, dy0, g_nw1, dsc1, dsh1, dgate0 = _norm_mod_bwd(
        "ssd_proj_dt_dx_norm1_bwd", (ddt_raw, w_si_t, "nn", ndt, dict(b_koff=si + sxbc)), x1, dhn1, dx2,
        nw[1], scale[1], shift[1], prev=(y0, gate[0] + l1_token[0:1, 0:1]))

    gw_ao = _matmul("attn_out_dw", a0, dy0, "tn", BF16, aw // 2, MM_T, MM_T)
    dos, dls, dz0 = _mix_bwd(dy0, w_ao, os_, lses, z0)
    datt = [_attn_bwd(g, qk, v, os_[g], lses[g], dos[g], dls[g]) for g in range(3)]
    dqkv = _rot_pack_bwd([t[0] for t in datt], [t[1] for t in datt], [t[2] for t in datt], tabs)
    gw_ai = _matmul("proj_qkv_dw", hn0, dqkv, "tn", BF16, MM_T, wcol, MM_T, out_blocks=3 * aw // wcol, dest=(NDEV, 0, None))
    gw_ai = _matmul("proj_z_dw", hn0, dz0, "tn", BF16, MM_T, wcol, MM_T, out_blocks=aw // wcol,
                    dest=(NDEV, 3 * aw // wcol, gw_ai))
    after_start = lambda acc, t: acc + t
    zero_row = lambda token: jnp.tile(token[0:1], (1, dm // 128))
    l0_grads = [gw_ai, gw_ao.reshape(NDEV, aw // NDEV, dm)]
    pair_handles, pair_token = _exchange_start("l0_pair_start", l0_grads, _landing_zones("l0_pair_place", l0_grads, "pair"),
                                               "pair", dqkv)
    dhn0_z = _matmul("proj_z_dx", dz0, w_ai, "nt", F32, MM_T, MM_T, aw, b_koff=3 * aw, n_out=dm, epilogue=after_start,
                     ncols=(zero_row(pair_token),))
    from_sibling, l0_grads = _exchange_wait("l0_pair_wait", pair_handles, "pair", dhn0_z, with_sources=True)
    chip_sums = [_pair_sum(f"l0_pair_sum_{a}", g, s) for a, (g, s) in enumerate(zip(l0_grads, from_sibling))]
    l0_handles, l0_token = _exchange_start("l0_grads_start", chip_sums, _landing_zones("l0_grads_place", chip_sums, "chips"),
                                           "chips", dhn0_z)
    dx0, g_nw0, dsc0, dsh0 = _norm_mod_bwd(
        "proj_qkv_dx_norm0_bwd", (dqkv, w_ai, "nt", aw, dict(n_out=dm)), x0, [dhn0_z], dx1,
        nw[0], scale[0], shift[0] + zero_row(l0_token))

    small_g = [_pack_small([dsh0, dsc0, dgate0, dsh1, dsc1, dgate1, g_nw0, g_nw1, g_fnw], g_dtb, [dalog_f, dalog_b], g_d, loss_part)]
    sm_handles, sm_token = _exchange_start("small_grads_start", small_g, _landing_zones("small_grads_place", small_g, "gather"),
                                           "gather", dx0)

    whole = (slice(None), slice(None))
    r_so, r_si, r_small = _exchange_wait("l1_grads_wait", l1_handles, "scatter", sm_token)
    si_out = [o.T for o in _adamw("adamw_ssd_w_in", ssd_w_in[0].T, r_si, m_ssd_w_in[0].T, v_ssd_w_in[0].T, n_ssd_in // NDEV, 512)]
    so_out = _adamw("adamw_ssd_w_out", ssd_w_out[0], r_so, m_ssd_w_out[0], v_ssd_w_out[0], 256)
    cw_cols = ssd_conv_w.shape[2]
    cw_out, cb_out, snw_out = _adamw_windows(
        "adamw_ssd_small", r_small,
        [(ssd_conv_w, m_ssd_conv_w, v_ssd_conv_w), (ssd_conv_b, m_ssd_conv_b, v_ssd_conv_b),
         (ssd_norm_w, m_ssd_norm_w, v_ssd_norm_w)],
        [(0, slice(0, CONV_WIDTH), slice(0, cw_cols), (0, slice(None), slice(None))),
         (1, slice(5, 6), slice(0, cw_cols), whole), (2, slice(6, 7), slice(0, si // NDEV), whole)])
    r_ai, r_ao = _exchange_wait("l0_grads_wait", l0_handles, "chips", so_out[0])
    ai_out = _adamw("adamw_attn_w_in", attn_w_in[0], r_ai, m_attn_w_in[0], v_attn_w_in[0], 512)
    ao_out = _adamw("adamw_attn_w_out", attn_w_out[0], r_ao, m_attn_w_out[0], v_attn_w_out[0], 192)

    (small_all,) = _exchange_wait("small_grads_wait", sm_handles, "gather", ai_out[0])
    full = slice(0, PACK_COLS)
    nhd = SSD_HEADS
    windows = [(0, slice(3 * i + k, 3 * i + k + 1), full, (slice(i, i + 1), slice(k * dm, (k + 1) * dm)))
               for i in range(2) for k in range(3)]
    windows += [(1, slice(6 + i, 7 + i), full, (slice(i, i + 1), slice(None))) for i in range(2)]
    windows += [(2, slice(8, 9), full, whole)]
    windows += [(3 + q, slice(9, 10), slice(2 * nhd * q + nhd * j, 2 * nhd * q + nhd * (j + 1)), (0, slice(j, j + 1), slice(None)))
                for q in range(2) for j in range(2)]
    windows += [(5, slice(9, 10), slice(4 * nhd, 5 * nhd), whole)]
    as_row = lambda a: a.reshape(1, dm)
    mb_out, nw_out, fnw_out, dtb_out, alog_out, d_out, loss = _adamw_windows(
        "adamw_small", small_all,
        [(mod_b, m_mod_b, v_mod_b), (norm_w, m_norm_w, v_norm_w), (fnw, as_row(m_final_norm_w), as_row(v_final_norm_w)),
         (ssd_dt_bias, m_ssd_dt_bias, v_ssd_dt_bias), (ssd_a_log, m_ssd_a_log, v_ssd_a_log), (ssd_d, m_ssd_d, v_ssd_d)],
        windows, extra=(slice(9, 10), slice(256, 257)))
    loss = loss.reshape(())

    ncol = mod_w.shape[2]
    dmod_all = small_all[:, 0:6, :].reshape(NDEV, 2, 3 * dm)
    dmod_sh = lax.dynamic_slice_in_dim(dmod_all, me * ncol, ncol, axis=2).transpose(1, 0, 2)
    g_modw = _mod_grad(c_all, dmod_sh).reshape(1, 2 * dm, ncol)
    modw_out = _adamw("adamw_mod_w", mod_w.reshape(2 * dm, ncol), g_modw, m_mod_w.reshape(2 * dm, ncol),
                      v_mod_w.reshape(2 * dm, ncol), 512)

    per_kind = []
    for k in range(4):
        per_kind.append([
            nw_out[k], modw_out[k].reshape(mod_w.shape), mb_out[k], ai_out[k][None], ao_out[k][None], si_out[k][None],
            cw_out[k], cb_out[k], dtb_out[k], alog_out[k], d_out[k], snw_out[k], so_out[k][None], fnw_out[k].reshape(dm)])
    return (loss, dx0.reshape(x.shape), *per_kind[0], *per_kind[1], *per_kind[2], *per_kind[3])


def _pack_ssd_small_blocks(g_cw, g_cb, g_nw):
    nper = g_cw.shape[1] // NDEV
    nwper = g_nw.shape[1] // NDEV

    def body(cw_ref, cb_ref, nw_ref, o_ref):
        o_ref[...] = jnp.zeros_like(o_ref)
        for d in range(NDEV):
            o_ref[d, 0:5, :] = cw_ref[:, d * nper:(d + 1) * nper]
            o_ref[d, 5:6, :] = cb_ref[:, d * nper:(d + 1) * nper]
            o_ref[d, 6:7, 0:nwper] = nw_ref[:, d * nwper:(d + 1) * nwper]

    return pl.pallas_call(body, name="pack_ssd_small_grads", out_shape=jax.ShapeDtypeStruct((NDEV, 8, nper), F32))(g_cw, g_cb, g_nw)
```

```python
import functools
import math

import jax
import jax.numpy as jnp
from jax import lax
from jax.experimental import pallas as pl
from jax.experimental.pallas import tpu as pltpu

F32 = jnp.float32
BF16 = jnp.bfloat16
HI = lax.Precision.HIGHEST
MESH = pl.DeviceIdType.MESH
NDEV = 8

NORM_EPS = 1e-6
ROPE_THETA = 500000.0
ROT_DIM = 16
HEAD_DIM = 64
DILATIONS = (1, 4, 16)
BAND = 64
NEG_BIG = -1e30
CHUNK = 128
SSD_HEADS = 32
SSD_GROUPS = 8
CONV_WIDTH = 5

ADAM_LR = 0.001
ADAM_B1 = 0.9
ADAM_B2 = 0.999
ADAM_EPS = 1e-08
ADAM_WD = 0.01
ADAM_STEP = 10

VMEM_BIG = 56 * 1024 * 1024
MM_T = 1024


def _params(sem=None, vmem=None):
    kw = {}
    if sem is not None:
        kw["dimension_semantics"] = sem
    if vmem is not None:
        kw["vmem_limit_bytes"] = vmem
    return pltpu.CompilerParams(**kw)


def _dg(a, b, ca, cb, prec=None):
    return lax.dot_general(a, b, (((ca,), (cb,)), ((), ())), preferred_element_type=F32, precision=prec)


def _nn(a, b):
    return _dg(a.astype(BF16), b.astype(BF16), 1, 0)


def _nt(a, b):
    return _dg(a.astype(BF16), b.astype(BF16), 1, 1)


def _tn(a, b):
    return _dg(a.astype(BF16), b.astype(BF16), 0, 0)


def _hnn(a, b):
    return _dg(a, b, 1, 0, HI)


@jax.custom_vjp
def _bnn(a, b):
    return _nn(a, b)


_bnn.defvjp(lambda a, b: (_nn(a, b), (a, b)), lambda r, g: (_nt(g, r[1]), _tn(r[0], g)))


@jax.custom_vjp
def _bnt(a, b):
    return _nt(a, b)


_bnt.defvjp(lambda a, b: (_nt(a, b), (a, b)), lambda r, g: (_nn(g, r[1]), _tn(g, r[0])))


@jax.custom_vjp
def _btn(a, b):
    return _tn(a, b)


_btn.defvjp(lambda a, b: (_tn(a, b), (a, b)), lambda r, g: (_nt(r[1], g), _nn(r[0], g)))


def _silu(x):
    return x * jax.nn.sigmoid(x)


def _b_spec(b, mode, tn, tk, no, ko, jk):
    if mode == "nt":
        return pl.BlockSpec((tn, tk), lambda *g: (jk(*g)[0] + no, jk(*g)[1] + ko))
    return pl.BlockSpec((tk, tn), lambda *g: (jk(*g)[1] + ko, jk(*g)[0] + no))


def _matmul(name, a, b, mode, out_dtype, tm, tn, tk, *, epilogue=None, tiled=(), mrows=(), ncols=(),
            b_noff=0, b_koff=0, n_out=None, out_blocks=None, dest=None):
    if mode == "tn":
        K, M = a.shape
    else:
        M, K = a.shape
    N = n_out if n_out is not None else (b.shape[0] if mode == "nt" else b.shape[1])
    tm, tn, tk = min(tm, M), min(tn, N), min(tk, K)
    assert M % tm == 0 and N % tn == 0 and K % tk == 0, (name, M, N, K, tm, tn, tk)
    assert b_noff % tn == 0 and b_koff % tk == 0
    no, ko = b_noff // tn, b_koff // tk
    nk = K // tk
    if mode == "tn":
        a_spec = pl.BlockSpec((tk, tm), lambda i, j, k: (k, i))
    else:
        a_spec = pl.BlockSpec((tm, tk), lambda i, j, k: (i, k))
    specs = [a_spec, _b_spec(b, mode, tn, tk, no, ko, lambda i, j, k: (j, k))]
    specs += [pl.BlockSpec((tm, tn), lambda i, j, k: (i, j)) for _ in tiled]
    specs += [pl.BlockSpec((tm, r.shape[1]), lambda i, j, k: (i, 0)) for r in mrows]
    specs += [pl.BlockSpec((1, tn), lambda i, j, k: (0, j)) for _ in ncols]
    total, off, earlier = dest if dest is not None else (None, 0, None)
    if out_blocks is None:
        assert off % tm == 0
        mo = off // tm
        out_shape = jax.ShapeDtypeStruct((M if total is None else total, N), out_dtype)
        out_spec = pl.BlockSpec((tm, tn), lambda i, j, k: (i + mo, j))
    else:
        nper = N // out_blocks
        assert nper % tn == 0
        jb = nper // tn
        out_shape = jax.ShapeDtypeStruct((out_blocks if total is None else total, M, nper), out_dtype)
        out_spec = pl.BlockSpec((None, tm, tn), lambda i, j, k: (j // jb + off, i, j % jb))
    if earlier is not None:
        assert earlier.shape == out_shape.shape and earlier.dtype == out_shape.dtype
    ne = len(tiled) + len(mrows) + len(ncols)
    dot = {"nn": _nn, "nt": _nt, "tn": _tn}[mode]

    def body(a_ref, b_ref, *rest):
        extras, o_ref = rest[:ne], rest[ne]

        def finish(acc):
            if epilogue is not None:
                acc = epilogue(acc, *[e[...] for e in extras])
            o_ref[...] = acc.astype(o_ref.dtype)

        if nk == 1:
            finish(dot(a_ref[...], b_ref[...]))
        else:
            acc_ref = rest[ne + 1]
            k = pl.program_id(2)

            @pl.when(k == 0)
            def _():
                acc_ref[...] = jnp.zeros_like(acc_ref)

            acc_ref[...] += dot(a_ref[...], b_ref[...])

            @pl.when(k == nk - 1)
            def _():
                finish(acc_ref[...])

    args = [a, b, *tiled, *mrows, *ncols]
    aliases = {}
    if earlier is not None:
        specs.append(pl.BlockSpec(memory_space=pl.ANY))
        aliases = {len(args): 0}
        args.append(earlier)

    def body_with_dest(*refs):
        body(*refs[:2 + ne], *refs[2 + ne + (earlier is not None):])

    return pl.pallas_call(
        body_with_dest, name=name, out_shape=out_shape, grid=(M // tm, N // tn, nk),
        in_specs=specs, out_specs=out_spec, input_output_aliases=aliases,
        scratch_shapes=[] if nk == 1 else [pltpu.VMEM((tm, tn), F32)],
        compiler_params=_params(("parallel", "parallel", "arbitrary"), VMEM_BIG),
    )(*args)


def _matmul_rows(name, a, b, mode, tm, tk, fn, rows, consts, outs, accs, *, n_out=None, b_noff=0, b_koff=0):
    rl = [(t, t.shape[1], 0) if not isinstance(t, tuple) else t for t in rows]
    make_a = a if callable(a) else None
    M, K = (rl[0][0].shape[0], b.shape[1 if mode == "nt" else 0]) if make_a else a.shape
    N = n_out if n_out is not None else (b.shape[0] if mode == "nt" else b.shape[1])
    tm, tk = min(tm, M), min(tk, K)
    assert M % tm == 0 and K % tk == 0 and b_koff % tk == 0 and b_noff % N == 0, (name, M, N, K)
    no, ko, nk = b_noff // N, b_koff // tk, K // tk
    assert make_a is None or nk == 1
    nr, nc, no_, na = len(rl), len(consts), len(outs), len(accs)
    dot = _nt if mode == "nt" else _nn

    def body(*refs):
        a_ref, b_ref, rest = (None, refs[0], refs[1:]) if make_a else (refs[0], refs[1], refs[2:])
        r_refs, c_refs = rest[:nr], rest[nr:nr + nc]
        o_refs, acc_refs = rest[nr + nc:nr + nc + no_], rest[nr + nc + no_:nr + nc + no_ + na]
        i, k = pl.program_id(0), pl.program_id(1)

        def finish(prod, *made):
            res_o, res_a = fn(prod, *made, *[r[...] for r in r_refs], *[c[...] for c in c_refs])
            for r, v in zip(o_refs, res_o, strict=True):
                r[...] = v.astype(r.dtype)
            if acc_refs:
                @pl.when(i == 0)
                def _():
                    for r in acc_refs:
                        r[...] = jnp.zeros_like(r)

                for r, v in zip(acc_refs, res_a, strict=True):
                    r[...] += v

        if make_a:
            left = make_a(*[r[...] for r in r_refs], *[c[...] for c in c_refs])
            finish(dot(left, b_ref[...]), left)
        elif nk == 1:
            finish(dot(a_ref[...], b_ref[...]))
        else:
            prod_ref = rest[-1]

            @pl.when(k == 0)
            def _():
                prod_ref[...] = jnp.zeros_like(prod_ref)

            prod_ref[...] += dot(a_ref[...], b_ref[...])

            @pl.when(k == nk - 1)
            def _():
                finish(prod_ref[...])

    b_spec = _b_spec(b, mode, N, tk, no, ko, lambda i, k: (0, k))
    in_specs = ([] if make_a else [pl.BlockSpec((tm, tk), lambda i, k: (i, k))]) + [b_spec]
    in_specs += [pl.BlockSpec((tm, w), functools.partial(lambda i, k, cb: (i, cb), cb=cb)) for (_, w, cb) in rl]
    in_specs += [pl.BlockSpec(c.shape, lambda i, k: (0, 0)) for c in consts]
    out_specs = [pl.BlockSpec((tm, c), lambda i, k: (i, 0)) for (c, _) in outs]
    out_specs += [pl.BlockSpec(shp, lambda i, k: (0, 0)) for shp in accs]
    out_shape = [jax.ShapeDtypeStruct((M, c), dt) for (c, dt) in outs] + [jax.ShapeDtypeStruct(shp, F32) for shp in accs]
    res = pl.pallas_call(
        body, name=name, out_shape=out_shape, grid=(M // tm, nk), in_specs=in_specs, out_specs=out_specs,
        scratch_shapes=[] if nk == 1 else [pltpu.VMEM((tm, N), F32)],
        compiler_params=_params(("arbitrary" if accs else "parallel", "arbitrary"), VMEM_BIG),
    )(*([] if make_a else [a]), b, *[t[0] for t in rl], *consts)
    return res[:no_], res[no_:]


def _rowwise(name, fn, tiled, consts, outs, accs, ts):
    tl = [(t, t.shape[1], 0) if not isinstance(t, tuple) else t for t in tiled]
    s_len = tl[0][0].shape[0]
    assert s_len % ts == 0
    nt_, nc_, no_ = len(tl), len(consts), len(outs)

    def body(*refs):
        t_refs, c_refs = refs[:nt_], refs[nt_:nt_ + nc_]
        o_refs, a_refs = refs[nt_ + nc_:nt_ + nc_ + no_], refs[nt_ + nc_ + no_:]
        res_o, res_a = fn(*[r[...] for r in t_refs], *[r[...] for r in c_refs])
        for r, v in zip(o_refs, res_o, strict=True):
            r[...] = v.astype(r.dtype)
        if a_refs:
            @pl.when(pl.program_id(0) == 0)
            def _():
                for r in a_refs:
                    r[...] = jnp.zeros_like(r)

            for r, v in zip(a_refs, res_a, strict=True):
                r[...] += v

    in_specs = [pl.BlockSpec((ts, w), functools.partial(lambda i, cb: (i, cb), cb=cb)) for (_, w, cb) in tl]
    in_specs += [pl.BlockSpec(c.shape, lambda i: (0, 0)) for c in consts]
    out_specs = [pl.BlockSpec((ts, c), lambda i: (i, 0)) for (c, _) in outs]
    out_specs += [pl.BlockSpec(shp, lambda i: (0, 0)) for shp in accs]
    out_shape = [jax.ShapeDtypeStruct((s_len, c), dt) for (c, dt) in outs]
    out_shape += [jax.ShapeDtypeStruct(shp, F32) for shp in accs]
    res = pl.pallas_call(
        body, name=name, out_shape=out_shape, grid=(s_len // ts,), in_specs=in_specs, out_specs=out_specs,
        compiler_params=_params(("arbitrary",) if accs else ("parallel",), VMEM_BIG),
    )(*[t[0] for t in tl], *consts)
    return res[:no_], res[no_:]


def _norm_mod_fn(x, nw, sc, sh):
    r = lax.rsqrt(jnp.mean(x * x, axis=-1, keepdims=True) + NORM_EPS)
    return (x * r * nw) * (1.0 + sc) + sh


def _norm_mod_fwd(name, x, nw, sc, sh):
    (hn,), _ = _rowwise(name, lambda x, nw, sc, sh: ([_norm_mod_fn(x, nw, sc, sh)], []),
                        [x], [nw, sc, sh], [(x.shape[1], BF16)], [], 1024)
    return hn


def _norm_mod_bwd(name, last, x, dhn_parts, dres, nw, sc, sh, prev=None):
    n = len(dhn_parts)
    d = x.shape[1]
    a, b, mode, tk, kw = last

    def fn(dhn, x, *rest):
        for p in rest[:n]:
            dhn = dhn + p
        dres, rest = rest[n], rest[n + 1:]
        y_prev, (nw, sc, sh), gate = (rest[0], rest[1:4], rest[4]) if prev is not None else (None, rest[0:3], None)
        r = lax.rsqrt(jnp.mean(x * x, axis=-1, keepdims=True) + NORM_EPS)
        xh = x * r
        dxh = dhn * (nw * (1.0 + sc))
        dx = r * (dxh - xh * jnp.mean(dxh * xh, axis=-1, keepdims=True)) + dres
        along = jnp.sum(dhn * xh, axis=0, keepdims=True)
        dnw, dsc, dsh = along * (1.0 + sc), along * nw, jnp.sum(dhn, axis=0, keepdims=True)
        if prev is None:
            return [dx], [dnw, dsc, dsh]
        return [dx, gate * dx], [dnw, dsc, dsh, jnp.sum(dx * y_prev, axis=0, keepdims=True)]

    rows = [x, *dhn_parts, dres] + ([prev[0]] if prev is not None else [])
    consts = [nw, sc, sh] + ([prev[1]] if prev is not None else [])
    outs = [(d, F32)] + ([(d, BF16)] if prev is not None else [])
    res_o, res_a = _matmul_rows(name, a, b, mode, 512, tk, fn, rows, consts, outs, [(1, d)] * (3 + (prev is not None)), **kw)
    return (*res_o, *res_a)


def _rope_tables(pos_col, inv_row):
    def fn(pos, inv):
        ang = pos.astype(F32) * inv
        e = lax.broadcasted_iota(jnp.int32, (1, 128), 1) % HEAD_DIM
        cos, sin = jnp.cos(ang), jnp.sin(ang)
        half = ROT_DIM // 2
        return [jnp.where(e < ROT_DIM, cos, 1.0), jnp.where(e < half, -sin, 0.0),
                jnp.where((e >= half) & (e < ROT_DIM), sin, 0.0)], []

    (c, sa, sb), _ = _rowwise("rope_tables", fn, [pos_col], [inv_row], [(128, F32)] * 3, [], 512)
    return c, sa, sb


def _rot_fwd(t, c, sa, sb):
    n = t.shape[1]
    rep = n // 128
    c, sa, sb = (jnp.tile(u, (1, rep)) for u in (c, sa, sb))
    return t * c + pltpu.roll(t, n - ROT_DIM // 2, 1) * sa + pltpu.roll(t, ROT_DIM // 2, 1) * sb


def _rot_bwd(g, c, sa, sb):
    n = g.shape[1]
    rep = n // 128
    c, sa, sb = (jnp.tile(u, (1, rep)) for u in (c, sa, sb))
    return g * c + pltpu.roll(g * sa, ROT_DIM // 2, 1) + pltpu.roll(g * sb, n - ROT_DIM // 2, 1)


ATT_TQ = 128


def _attn_tiles(l):
    tk = ATT_TQ + 2 * BAND
    return (l, l) if l <= tk else (ATT_TQ, tk)


def _attn_specs(g, s_len):
    def blk(off):
        return pl.BlockSpec((s_len, 128), functools.partial(lambda hp, off: (0, off + hp), off=off))

    return blk(4 * g), blk(12 + 4 * g), blk(4 * g), blk(0)


def _attn_tile_geometry(t, d, l):
    tq, tk = _attn_tiles(l)
    nts = l // tq
    r = t // nts
    ts = t % nts
    q0 = ts * tq
    ws = jnp.clip(q0 - BAND, 0, l - tk)
    kind = jnp.where(ts == 0, 0, jnp.where(ts == nts - 1, 2, 1))
    if d == 1:
        return pl.ds(pl.multiple_of(q0, tq), tq), pl.ds(pl.multiple_of(ws, BAND), tk), kind
    return pl.ds(r + d * q0, tq, stride=d), pl.ds(r + d * ws, tk, stride=d), kind


def _attn_fill_bias(bias_ref):
    _, tq2, tk = bias_ref.shape
    iq = lax.broadcasted_iota(jnp.int32, (tq2, 1), 0) % (tq2 // 2)
    ik = lax.broadcasted_iota(jnp.int32, (1, tk), 1)
    for i, off in enumerate((0, -BAND, -2 * BAND)):
        bias_ref[i] = jnp.where(jnp.abs(ik + off - iq) <= BAND, 0.0, NEG_BIG)


def _split_heads(t, in_h):
    zero = jnp.zeros_like(t)
    return jnp.concatenate([jnp.where(in_h[0], t, zero), jnp.where(in_h[1], t, zero)], axis=0)


def _attn_fwd(g, qk, v):
    s_len = qk.shape[0]
    d = DILATIONS[g]
    l = s_len // d
    tq, tk = _attn_tiles(l)
    assert l % tq == 0 and l >= tk
    q_spec, k_spec, v_spec, o_spec = _attn_specs(g, s_len)
    scale = 1.0 / math.sqrt(HEAD_DIM)

    def body(q_ref, k_ref, v_ref, o_ref, lse_ref, bias_ref):
        lane = lax.broadcasted_iota(jnp.int32, (1, 128), 1)
        in_h = [lane < HEAD_DIM, lane >= HEAD_DIM]
        _attn_fill_bias(bias_ref)

        def tile(t, carry):
            rows, win, kind = _attn_tile_geometry(t, d, l)
            q = (q_ref[rows, :] * scale).astype(BF16)
            k = k_ref[win, :].astype(BF16)
            vv = v_ref[win, :].astype(BF16)
            s = _nt(_split_heads(q, in_h), k) + bias_ref[kind]
            m = jnp.max(s, axis=1, keepdims=True)
            p = jnp.exp(s - m)
            den = jnp.sum(p, axis=1, keepdims=True)
            out = _nn(p, vv) / den
            lse = m + jnp.log(den)
            o_ref[rows, :] = jnp.where(in_h[0], out[:tq], out[tq:])
            lse_ref[rows, :] = jnp.where(in_h[0], lse[:tq], lse[tq:])
            return carry

        lax.fori_loop(0, s_len // tq, tile, 0, unroll=8 * ATT_TQ // tq)

    return pl.pallas_call(
        body, name=f"attn_fwd_g{g}", grid=(4,),
        out_shape=[jax.ShapeDtypeStruct((s_len, 512), F32)] * 2,
        in_specs=[q_spec, k_spec, v_spec], out_specs=[o_spec, o_spec],
        scratch_shapes=[pltpu.VMEM((3, 2 * tq, tk), F32)],
        compiler_params=_params(("parallel",), VMEM_BIG),
    )(qk, qk, v)


def _attn_bwd(g, qk, v, o, lse, do, dlse):
    s_len = qk.shape[0]
    d = DILATIONS[g]
    l = s_len // d
    tq, tk = _attn_tiles(l)
    q_spec, k_spec, v_spec, o_spec = _attn_specs(g, s_len)
    scale = 1.0 / math.sqrt(HEAD_DIM)

    def body(q_ref, k_ref, v_ref, o_ref, lse_ref, do_ref, dlse_ref, dq_ref, dk_ref, dv_ref, bias_ref):
        lane = lax.broadcasted_iota(jnp.int32, (1, 128), 1)
        in_h = [lane < HEAD_DIM, lane >= HEAD_DIM]
        dk_ref[...] = jnp.zeros_like(dk_ref)
        dv_ref[...] = jnp.zeros_like(dv_ref)
        _attn_fill_bias(bias_ref)

        def tile(t, carry):
            rows, win, kind = _attn_tile_geometry(t, d, l)
            k, vv = k_ref[win, :].astype(BF16), v_ref[win, :].astype(BF16)
            dout, lse_t, dlse_t = do_ref[rows, :], lse_ref[rows, :], dlse_ref[rows, :]
            od = dout * o_ref[rows, :]
            q2 = _split_heads((q_ref[rows, :] * scale).astype(BF16), in_h)
            do2 = _split_heads(dout.astype(BF16), in_h)
            head_col = lambda a: jnp.concatenate([a[:, 0:1], a[:, HEAD_DIM:HEAD_DIM + 1]], axis=0)
            delta = jnp.concatenate([jnp.sum(jnp.where(m, od, 0.0), axis=1, keepdims=True) for m in in_h], axis=0)
            p = jnp.exp(_nt(q2, k) + bias_ref[kind] - head_col(lse_t))
            ds = (p * (_nt(do2, vv) - delta + head_col(dlse_t))).astype(BF16)
            dq2 = _nn(ds, k) * scale
            dq_ref[rows, :] = jnp.where(in_h[0], dq2[:tq], dq2[tq:])
            dk_ref[win, :] += _tn(ds, q2)
            dv_ref[win, :] += _tn(p, do2)
            return carry

        lax.fori_loop(0, s_len // tq, tile, 0, unroll=8 * ATT_TQ // tq)

    return pl.pallas_call(
        body, name=f"attn_bwd_g{g}", grid=(4,),
        out_shape=[jax.ShapeDtypeStruct((s_len, 512), F32)] * 3,
        in_specs=[q_spec, k_spec, v_spec, o_spec, o_spec, o_spec, o_spec], out_specs=[o_spec] * 3,
        scratch_shapes=[pltpu.VMEM((3, 2 * tq, tk), F32)],
        compiler_params=_params(("parallel",), VMEM_BIG),
    )(qk, qk, v, o, lse, do, dlse)


def _mix_weights(ls):
    mx = jnp.maximum(jnp.maximum(ls[0], ls[1]), ls[2])
    es = [jnp.exp(x - mx) for x in ls]
    tot = es[0] + es[1] + es[2]
    return [e / tot for e in es]


def _attn_out(os_, lses, z, x, gate, w_out):
    s_len, dm = x.shape
    tm = 256
    wdt = 512
    z, z_block = z

    def body(o0, o1, o2, l0, l1, l2, z_ref, x_ref, g_ref, w_ref, a_ref, y_ref, x1_ref):
        alphas = _mix_weights([l0[...], l1[...], l2[...]])
        y = jnp.zeros((tm, dm), F32)
        for g, o_ref in enumerate((o0, o1, o2)):
            a_g = (o_ref[...] * alphas[g] * _silu(z_ref[:, g * wdt:(g + 1) * wdt])).astype(BF16)
            a_ref[:, g * wdt:(g + 1) * wdt] = a_g
            y = y + _nn(a_g, w_ref[g * wdt:(g + 1) * wdt, :])
        y_ref[...] = y
        x1_ref[...] = x_ref[...] + g_ref[...] * y

    row = lambda c: pl.BlockSpec((tm, c), lambda i: (i, 0))
    return pl.pallas_call(
        body, name="attn_out", grid=(s_len // tm,),
        out_shape=[jax.ShapeDtypeStruct((s_len, 3 * wdt), BF16), jax.ShapeDtypeStruct((s_len, dm), F32),
                   jax.ShapeDtypeStruct((s_len, dm), F32)],
        in_specs=[row(wdt)] * 6 + [pl.BlockSpec((tm, 3 * wdt), lambda i: (i, z_block)), row(dm),
                                   pl.BlockSpec((1, dm), lambda i: (0, 0)), pl.BlockSpec(w_out.shape, lambda i: (0, 0))],
        out_specs=[row(3 * wdt), row(dm), row(dm)],
        compiler_params=_params(("parallel",), VMEM_BIG),
    )(*os_, *lses, z, x, gate, w_out)


def _mix_bwd(dy, w_out, os_, lses, z):
    wdt = 512

    hi = lax.broadcasted_iota(jnp.int32, (2 * wdt, wdt), 0) % wdt // HEAD_DIM
    hj = lax.broadcasted_iota(jnp.int32, (2 * wdt, wdt), 1) // HEAD_DIM
    seg_all = (hi == hj).astype(BF16)

    def fn(da, o0, o1, o2, l0, l1, l2, z, seg):
        os_t, ls = [o0, o1, o2], [l0, l1, l2]
        alphas = _mix_weights(ls)
        head_sum = lambda t: _dg(jnp.concatenate(_bf16_parts(t, 2), axis=1), seg, 1, 0)
        dos, dal, dzs = [], [], []
        for g in range(3):
            zg = z[:, g * wdt:(g + 1) * wdt]
            sig = jax.nn.sigmoid(zg)
            dag = da[:, g * wdt:(g + 1) * wdt]
            dmix = dag * zg * sig
            dzs.append(dag * os_t[g] * alphas[g] * (sig * (1.0 + zg * (1.0 - sig))))
            dos.append(dmix * alphas[g])
            dal.append(head_sum(dmix * os_t[g]))
        mean = alphas[0] * dal[0] + alphas[1] * dal[1] + alphas[2] * dal[2]
        dls = [alphas[g] * (dal[g] - mean) for g in range(3)]
        return dos + dls + [jnp.concatenate(dzs, axis=1)], []

    outs, _ = _matmul_rows("attn_out_dx_mix_bwd", dy, w_out, "nt", 256, dy.shape[1], fn, [*os_, *lses, (z[0], 3 * wdt, z[1])], [seg_all],
                           [(wdt, F32)] * 6 + [(3 * wdt, BF16)], [])
    return outs[:3], outs[3:6], outs[6]


def _rot_pack_bwd(dqs, dks, dvs, tabs):
    wdt = 512

    def fn(*args):
        grads, (c, sa, sb) = args[:9], args[9:]
        cols = [_rot_bwd(gq, c, sa, sb) for gq in grads[:6]] + list(grads[6:])
        return [jnp.concatenate(cols, axis=1)], []

    (out,), _ = _rowwise("rot_pack_bwd", fn, [*dqs, *dks, *dvs, *tabs], [], [(9 * wdt, BF16)], [], 512)
    return out


CONV_CB = 128
CONV_R = 256
CONV_PAD = 8


def _conv_taps(buf, base, off, sign):
    return [buf[pl.ds(base + off + sign * j, CONV_R), :] for j in range(CONV_WIDTH)]


def _conv_tap_sum(taps, w):
    acc = None
    for j, t in enumerate(taps):
        term = t * w[j:j + 1, :]
        acc = term if acc is None else acc + term
    return acc


def _conv_fwd(xpre, cw, cb):
    s_len, ch = xpre.shape
    nchunk = s_len // CONV_R

    def body(x_ref, w_ref, b_ref, o_ref, xp):
        zero = jnp.zeros((CONV_PAD, CONV_CB), F32)
        xp[0:CONV_PAD, :] = zero
        xp[s_len + CONV_PAD:s_len + 2 * CONV_PAD, :] = zero

        def fill(ci, carry):
            base = pl.multiple_of(ci * CONV_R, CONV_R)
            xp[pl.ds(base + CONV_PAD, CONV_R), :] = x_ref[pl.ds(base, CONV_R), :]
            return carry

        lax.fori_loop(0, nchunk, fill, 0)
        w = w_ref[...]
        b = b_ref[...]

        def chunk(ci, carry):
            base = pl.multiple_of(ci * CONV_R, CONV_R)
            u = _conv_tap_sum(_conv_taps(xp, base, CONV_PAD - CONV_WIDTH // 2, 1), w) + b
            o_ref[pl.ds(base, CONV_R), :] = _silu(u)
            return carry

        lax.fori_loop(0, nchunk, chunk, 0, unroll=2)

    col = lambda r: pl.BlockSpec((r, CONV_CB), lambda j: (0, j))
    return pl.pallas_call(
        body, name="conv_fwd", grid=(ch // CONV_CB,), out_shape=jax.ShapeDtypeStruct((s_len, ch), F32),
        in_specs=[col(s_len), col(CONV_WIDTH), col(1)], out_specs=col(s_len),
        scratch_shapes=[pltpu.VMEM((s_len + 2 * CONV_PAD, CONV_CB), F32)],
        compiler_params=_params(("parallel",), VMEM_BIG),
    )(xpre, cw, cb)


def _conv_bwd(xpre, da, cw, cb):
    s_len, ch = xpre.shape
    nchunk = s_len // CONV_R
    half = CONV_WIDTH // 2

    def body(x_ref, da_ref, w_ref, b_ref, dx_ref, gw_ref, gb_ref, xp, dcp):
        zero = jnp.zeros((CONV_PAD, CONV_CB), F32)
        for buf in (xp, dcp):
            buf[0:CONV_PAD, :] = zero
            buf[s_len + CONV_PAD:s_len + 2 * CONV_PAD, :] = zero

        def fill(ci, carry):
            base = pl.multiple_of(ci * CONV_R, CONV_R)
            xp[pl.ds(base + CONV_PAD, CONV_R), :] = x_ref[pl.ds(base, CONV_R), :]
            return carry

        lax.fori_loop(0, nchunk, fill, 0)
        w = w_ref[...]
        b = b_ref[...]

        def first(ci, carry):
            base = pl.multiple_of(ci * CONV_R, CONV_R)
            taps = _conv_taps(xp, base, CONV_PAD - half, 1)
            u = _conv_tap_sum(taps, w) + b
            sig = jax.nn.sigmoid(u)
            dc = da_ref[pl.ds(base, CONV_R), :] * (sig * (1.0 + u * (1.0 - sig)))
            dcp[pl.ds(base + CONV_PAD, CONV_R), :] = dc
            gb = carry[0] + jnp.sum(dc, axis=0, keepdims=True)
            gws = [carry[1 + j] + jnp.sum(dc * taps[j], axis=0, keepdims=True) for j in range(CONV_WIDTH)]
            return (gb, *gws)

        z1 = jnp.zeros((1, CONV_CB), F32)
        sums = lax.fori_loop(0, nchunk, first, (z1,) * (1 + CONV_WIDTH), unroll=2)
        gb_ref[...] = sums[0]
        for j in range(CONV_WIDTH):
            gw_ref[j:j + 1, :] = sums[1 + j]

        def second(ci, carry):
            base = pl.multiple_of(ci * CONV_R, CONV_R)
            dx_ref[pl.ds(base, CONV_R), :] = _conv_tap_sum(_conv_taps(dcp, base, CONV_PAD + half, -1), w).astype(dx_ref.dtype)
            return carry

        lax.fori_loop(0, nchunk, second, 0, unroll=2)

    col = lambda r: pl.BlockSpec((r, CONV_CB), lambda j: (0, j))
    return pl.pallas_call(
        body, name="conv_bwd", grid=(ch // CONV_CB,),
        out_shape=[jax.ShapeDtypeStruct((s_len, ch), BF16), jax.ShapeDtypeStruct((CONV_WIDTH, ch), F32),
                   jax.ShapeDtypeStruct((1, ch), F32)],
        in_specs=[col(s_len), col(s_len), col(CONV_WIDTH), col(1)],
        out_specs=[col(s_len), col(CONV_WIDTH), col(1)],
        scratch_shapes=[pltpu.VMEM((s_len + 2 * CONV_PAD, CONV_CB), F32)] * 2,
        compiler_params=_params(("parallel",), VMEM_BIG),
    )(xpre, da, cw, cb)


SSD_GW = 256
SSD_N = 128
SSD_DTW = 128


def _bf16_parts(x, n):
    parts, rest = [], x
    for _ in range(n):
        p = rest.astype(BF16)
        parts.append(p)
        rest = rest - p.astype(F32)
    return parts


@jax.custom_vjp
def _expand(x, e):
    eb = e.astype(BF16)
    return _dg(jnp.concatenate(_bf16_parts(x, 2), axis=1), jnp.concatenate([eb, eb], axis=0), 1, 0)


def _expand_fwd(x, e):
    return _expand(x, e), e


def _expand_bwd(e, g):
    return _dg(g.astype(BF16), e.astype(BF16), 1, 1), jnp.zeros_like(e)


_expand.defvjp(_expand_fwd, _expand_bwd)


@jax.custom_vjp
def _running_sum(tri, x):
    tb = tri.astype(BF16)
    return sum(_dg(tb, p, 1, 0) for p in _bf16_parts(x, 3))


def _running_sum_fwd(tri, x):
    return _running_sum(tri, x), tri


def _running_sum_bwd(tri, g):
    tb = tri.astype(BF16)
    return jnp.zeros_like(tri), sum(_dg(tb, p, 0, 0) for p in _bf16_parts(g, 3))


_running_sum.defvjp(_running_sum_fwd, _running_sum_bwd)


def _pick_col(a, h):
    @jax.custom_vjp
    def pick(a):
        return a[:, h:h + 1]

    pick.defvjp(lambda a: (a[:, h:h + 1], None),
                lambda _, g: (g * (lax.broadcasted_iota(jnp.int32, (1, a.shape[1]), 1) == h).astype(F32),))
    return pick(a)


def _pick_row(a, h):
    @jax.custom_vjp
    def pick(a):
        return a[h:h + 1, :]

    pick.defvjp(lambda a: (a[h:h + 1, :], None),
                lambda _, g: (g * (lax.broadcasted_iota(jnp.int32, (a.shape[0], 1), 0) == h).astype(F32),))
    return pick(a)


def _ssd_mask(dirn):
    ri = lax.broadcasted_iota(jnp.int32, (CHUNK, CHUNK), 0)
    cj = lax.broadcasted_iota(jnp.int32, (CHUNK, CHUNK), 1)
    return (cj <= ri) if dirn == 0 else (cj >= ri)


def _ssd_rowsel(dirn):
    last = CHUNK - 1 if dirn == 0 else 0
    return (lax.broadcasted_iota(jnp.int32, (CHUNK, 1), 0) == last).astype(F32)


def _ssd_chunk_pre(dirn):
    nh = SSD_DTW

    def f(dt, alog):
        da = dt * (-jnp.exp(alog))
        cum = _running_sum(_ssd_mask(dirn).astype(F32), da)
        tot = jnp.sum(cum * _ssd_rowsel(dirn), axis=0, keepdims=True)
        hh = lax.broadcasted_iota(jnp.int32, (nh, SSD_HEADS * HEAD_DIM), 0)
        jj = lax.broadcasted_iota(jnp.int32, (nh, SSD_HEADS * HEAD_DIM), 1)
        expand = (hh == dirn * SSD_HEADS + jj // HEAD_DIM).astype(F32)
        return cum, cum.T, _expand(dt, expand), _expand(jnp.exp(tot - cum), expand), _expand(jnp.exp(cum), expand)

    return f


def _ssd_group_fn(g, dirn, stacked):
    def f(xs, bm, cm, st, cum, cum_t, dt_e, w_e, ce_e):
        mask = _ssd_mask(dirn)
        xdt = xs * dt_e
        cd_e = jnp.sum(ce_e * _ssd_rowsel(dirn), axis=0, keepdims=True)
        cb = _bnt(cm, bm)
        lane_head = lax.broadcasted_iota(jnp.int32, (1, SSD_GW), 1) // HEAD_DIM
        y = _bnn(cm, st) * ce_e
        decayed, inputs = [], []
        for j in range(4):
            hidx = dirn * SSD_HEADS + 4 * g + j
            col, row = _pick_col(cum, hidx), _pick_row(cum_t, hidx)
            dec = cb * jnp.exp(jnp.where(mask, col - row, NEG_BIG))
            head = (lane_head == j).astype(F32)
            if stacked:
                decayed.append(dec)
                inputs.append(xdt * head)
            else:
                y = y + _bnn(dec, xdt) * head
        if stacked:
            y = y + _bnn(jnp.concatenate(decayed, axis=1), jnp.concatenate(inputs, axis=0))
        st_out = st * cd_e + _btn(bm, xdt * w_e)
        return y, st_out

    return f


def _ssd_in_specs(kk):
    ln = CHUNK
    return [pl.BlockSpec((ln, 2048), lambda i: (kk(i), 0)),
            pl.BlockSpec((ln, 1024), lambda i: (kk(i), 2)),
            pl.BlockSpec((ln, 1024), lambda i: (kk(i), 3)),
            pl.BlockSpec((ln, SSD_DTW), lambda i: (kk(i), 0)),
            pl.BlockSpec((1, SSD_DTW), lambda i: (0, 0))]


def _ssd_fwd(xbc, dt, alog, dirn, prior=None):
    s_len = xbc.shape[0]
    nc = s_len // CHUNK
    kk = (lambda i: i) if dirn == 0 else (lambda i: nc - 1 - i)

    def body(x_ref, b_ref, c_ref, dt_ref, al_ref, *rest):
        prior_ref = rest[0] if prior is not None else None
        y_ref, sts_ref, st = rest[prior is not None:]

        @pl.when(pl.program_id(0) == 0)
        def _():
            st[...] = jnp.zeros_like(st)

        sts_ref[0] = st[...]
        cum, cum_t, dt_e, w_e, ce_e = _ssd_chunk_pre(dirn)(dt_ref[...], al_ref[...])
        for g in range(SSD_GROUPS):
            xc = slice(g * SSD_GW, (g + 1) * SSD_GW)
            gc = slice(g * SSD_N, (g + 1) * SSD_N)
            y, st_new = _ssd_group_fn(g, dirn, True)(x_ref[:, xc], b_ref[:, gc], c_ref[:, gc], st[:, xc], cum, cum_t,
                                               dt_e[:, xc], w_e[:, xc], ce_e[:, xc])
            y_ref[:, xc] = y if prior is None else y + prior_ref[:, xc]
            st[:, xc] = st_new

    return pl.pallas_call(
        body, name=f"ssd_fwd_d{dirn}", grid=(nc,),
        out_shape=[jax.ShapeDtypeStruct((s_len, 2048), F32), jax.ShapeDtypeStruct((nc, SSD_N, 2048), F32)],
        in_specs=_ssd_in_specs(kk) + ([pl.BlockSpec((CHUNK, 2048), lambda i: (kk(i), 0))] if prior is not None else []),
        out_specs=[pl.BlockSpec((CHUNK, 2048), lambda i: (kk(i), 0)),
                   pl.BlockSpec((1, SSD_N, 2048), lambda i: (kk(i), 0, 0))],
        scratch_shapes=[pltpu.VMEM((SSD_N, 2048), F32)],
        compiler_params=_params(("arbitrary",), VMEM_BIG),
    )(xbc, xbc, xbc, dt, alog, *([prior] if prior is not None else []))


def _ssd_bwd(xbc, dt, alog, states, dy, d_e, dirn, prior=None):
    s_len = xbc.shape[0]
    nc = s_len // CHUNK
    kk = (lambda i: nc - 1 - i) if dirn == 0 else (lambda i: i)

    def body(x_ref, b_ref, c_ref, dt_ref, al_ref, sts_ref, dy_ref, de_ref, *rest):
        prior_ref = rest[0] if prior is not None else None
        dx_ref, ddt_ref, dal_ref, dst = rest[prior is not None:]
        plus_prior = (lambda v, cols: v + prior_ref[:, cols]) if prior is not None else (lambda v, cols: v)

        @pl.when(pl.program_id(0) == 0)
        def _():
            dst[...] = jnp.zeros_like(dst)
            dal_ref[...] = jnp.zeros_like(dal_ref)

        (cum, cum_t, dt_e, w_e, ce_e), pre_vjp = jax.vjp(_ssd_chunk_pre(dirn), dt_ref[...], al_ref[...])
        dcum = jnp.zeros_like(cum)
        dcum_t = jnp.zeros_like(cum_t)
        d_dt_e, d_w_e, d_ce_e = [], [], []
        for g in range(SSD_GROUPS):
            xc = slice(g * SSD_GW, (g + 1) * SSD_GW)
            gc = slice(g * SSD_N, (g + 1) * SSD_N)
            _, vjp = jax.vjp(_ssd_group_fn(g, dirn, False), x_ref[:, xc], b_ref[:, gc], c_ref[:, gc], sts_ref[0, :, xc], cum, cum_t,
                             dt_e[:, xc], w_e[:, xc], ce_e[:, xc])
            dyg = dy_ref[:, xc]
            dxs, dbm, dcm, dst_g, dcum_g, dcum_t_g, ddte_g, dwe_g, dcee_g = vjp((dyg, dst[:, xc]))
            if dirn == 0:
                dxs = dxs + dyg * de_ref[:, xc]
            bc, cc = slice(2048 + g * SSD_N, 2048 + (g + 1) * SSD_N), slice(3072 + g * SSD_N, 3072 + (g + 1) * SSD_N)
            dx_ref[:, xc] = plus_prior(dxs, xc)
            dx_ref[:, bc] = plus_prior(dbm, bc)
            dx_ref[:, cc] = plus_prior(dcm, cc)
            dst[:, xc] = dst_g
            dcum = dcum + dcum_g
            dcum_t = dcum_t + dcum_t_g
            d_dt_e.append(ddte_g)
            d_w_e.append(dwe_g)
            d_ce_e.append(dcee_g)
        ddt, dal = pre_vjp((dcum, dcum_t, jnp.concatenate(d_dt_e, axis=1), jnp.concatenate(d_w_e, axis=1),
                            jnp.concatenate(d_ce_e, axis=1)))
        ddt_ref[...] = ddt
        dal_ref[...] += dal

    return pl.pallas_call(
        body, name=f"ssd_bwd_d{dirn}", grid=(nc,),
        out_shape=[jax.ShapeDtypeStruct((s_len, 4096), F32), jax.ShapeDtypeStruct((s_len, SSD_DTW), F32),
                   jax.ShapeDtypeStruct((1, SSD_DTW), F32)],
        in_specs=_ssd_in_specs(kk) + [pl.BlockSpec((1, SSD_N, 2048), lambda i: (kk(i), 0, 0)),
                                      pl.BlockSpec((CHUNK, 2048), lambda i: (kk(i), 0)),
                                      pl.BlockSpec((1, 2048), lambda i: (0, 0))]
        + ([pl.BlockSpec((CHUNK, 4096), lambda i: (kk(i), 0))] if prior is not None else []),
        out_specs=[pl.BlockSpec((CHUNK, 4096), lambda i: (kk(i), 0)),
                   pl.BlockSpec((CHUNK, SSD_DTW), lambda i: (kk(i), 0)),
                   pl.BlockSpec((1, SSD_DTW), lambda i: (0, 0))],
        scratch_shapes=[pltpu.VMEM((SSD_N, 2048), F32)],
        compiler_params=_params(("arbitrary",), VMEM_BIG),
    )(xbc, xbc, xbc, dt, alog, states, dy, d_e, *([prior] if prior is not None else []))


def _gate_norm_fn(y, xs, z, d_e, nw):
    yg = (y + xs * d_e) * _silu(z)
    return yg * lax.rsqrt(jnp.mean(yg * yg, axis=-1, keepdims=True) + NORM_EPS) * nw


def _gate_norm_bwd(dy, w_out, y, xbc, z, d_e, nw):
    def fn(du, y, xs, z, d_e, nw):
        sig = jax.nn.sigmoid(z)
        gate = z * sig
        ysum = y + xs * d_e
        yg = ysum * gate
        r = lax.rsqrt(jnp.mean(yg * yg, axis=-1, keepdims=True) + NORM_EPS)
        t = du * nw
        dyg = t * r - yg * (jnp.mean(t * yg, axis=-1, keepdims=True) * (r * r * r))
        dys = dyg * gate
        dz = dyg * ysum * (sig * (1.0 + z * (1.0 - sig)))
        dnw = jnp.sum(du * yg * r, axis=0, keepdims=True)
        dde = jnp.sum(dys * xs, axis=0, keepdims=True)
        return [dys, dz], [dnw, dde]

    (dys, dz), (g_nw, g_d) = _matmul_rows("ssd_out_dx_gate_norm_bwd", dy, w_out, "nt", 256, dy.shape[1], fn,
                                          [y, (xbc, 2048, 0), z], [d_e, nw], [(2048, F32), (2048, BF16)],
                                          [(1, 2048), (1, 2048)])
    return dys, dz, g_nw, g_d


def _ssd_tail_loss(y, xbc, z, d_e, snw, w_out, x1, tgt, gate, fnw):
    dm = x1.shape[1]
    si = y.shape[1]

    def make_u(y, xs, z, x1, tgt, d_e, snw, gate, fnw):
        return _gate_norm_fn(y, xs, z, d_e, snw).astype(BF16)

    def fn(y1, u, y, xs, z, x1, tgt, d_e, snw, gate, fnw):
        x2 = x1 + gate * y1
        r = lax.rsqrt(jnp.mean(x2 * x2, axis=-1, keepdims=True) + NORM_EPS)
        xh = x2 * r
        err = xh * fnw - tgt
        loss = 0.5 * jnp.sum(jnp.mean(err * err, axis=-1, keepdims=True), axis=0, keepdims=True)
        dy = err * (1.0 / dm)
        dxh = dy * fnw
        dx2 = r * (dxh - xh * jnp.mean(dxh * xh, axis=-1, keepdims=True))
        dfnw = jnp.sum(dy * xh, axis=0, keepdims=True)
        return [u, dx2, gate * dx2], [dfnw, jnp.sum(dx2 * y1, axis=0, keepdims=True), jnp.broadcast_to(loss, (1, 128))]

    (u, dx2, dy1), (g_fnw, dgate, loss) = _matmul_rows(
        "ssd_out_loss", make_u, w_out, "nn", 256, si, fn, [y, (xbc, si, 0), z, x1, tgt], [d_e, snw, gate, fnw],
        [(si, BF16), (dm, F32), (dm, BF16)], [(1, dm), (1, dm), (1, 128)])
    return u, dx2, dy1, g_fnw, dgate, loss


def _softplus_fwd(dt_raw, bias):
    (dt,), _ = _rowwise("dt_softplus", lambda r, b: ([jax.nn.softplus(r + b)], []), [dt_raw], [bias],
                        [(dt_raw.shape[1], F32)], [], 512)
    return dt


def _softplus_bwd(ddt_f, ddt_b, dt_raw, bias):
    def fn(df, db, r, b):
        g = (df + db) * jax.nn.sigmoid(r + b)
        return [g], [jnp.sum(g, axis=0, keepdims=True)]

    w = dt_raw.shape[1]
    (g,), (gb,) = _rowwise("dt_softplus_bwd", fn, [ddt_f, ddt_b, dt_raw], [bias], [(w, BF16)], [(1, w)], 512)
    return g, gb


def _mod_part(c_all, mod_w):
    nl, _, ncol = mod_w.shape
    nb = c_all.shape[0]

    def body(c_ref, w_ref, o_ref):
        cond = _silu(c_ref[...])
        for i in range(nl):
            o_ref[i * nb:(i + 1) * nb, :] = _nn(cond, w_ref[i])

    return pl.pallas_call(body, name="mod_part", out_shape=jax.ShapeDtypeStruct((nl * nb, ncol), F32),
                          compiler_params=_params(None, VMEM_BIG))(c_all, mod_w)


def _mod_finish(mod_nb, mod_b, norm_w, tokens):
    nl, dm = norm_w.shape

    def body(a_ref, b_ref, nw_ref, *rest):
        tok_refs, o_refs = rest[:len(tokens)], rest[len(tokens):]
        tok = sum(t[0:1, 0:1] for t in tok_refs)
        for i in range(nl):
            for k in range(3):
                cols = slice(k * dm, (k + 1) * dm)
                o_refs[4 * i + k][...] = a_ref[i:i + 1, cols] + b_ref[i:i + 1, cols]
            o_refs[4 * i + 3][...] = nw_ref[i:i + 1, :] + tok

    rows = pl.pallas_call(body, name="mod_finish", out_shape=[jax.ShapeDtypeStruct((1, dm), F32)] * (4 * nl))(
        mod_nb, mod_b, norm_w, *tokens)
    return [rows[4 * i:4 * i + 4] for i in range(nl)]


def _mod_grad(c_all, dmod_sh):
    nl, nb, ncol = dmod_sh.shape
    dm = c_all.shape[1]

    def body(c_ref, d_ref, o_ref):
        cond = _silu(c_ref[...])
        for i in range(nl):
            o_ref[i] = _tn(cond, d_ref[i])

    return pl.pallas_call(body, name="mod_grad", out_shape=jax.ShapeDtypeStruct((nl, dm, ncol), F32),
                          compiler_params=_params(None, VMEM_BIG))(c_all, dmod_sh)


PACK_ROWS = 16
PACK_COLS = 1024


def _pack_small(rows, b64, a64s, d32, extra):
    nr, na = len(rows), len(a64s)

    def body(*refs):
        o_ref = refs[-1]
        o_ref[...] = jnp.zeros_like(o_ref)
        for i in range(nr):
            o_ref[i:i + 1, :] = refs[i][...]
        b_ref, a_refs, d_ref, e_ref = refs[nr], refs[nr + 1:nr + 1 + na], refs[nr + 1 + na], refs[nr + 2 + na]
        o_ref[nr:nr + 1, 0:64] = b_ref[:, 0:64]
        o_ref[nr:nr + 1, 64:128] = sum(a[:, 0:64] for a in a_refs)
        nch = d_ref.shape[1]
        head = lax.broadcasted_iota(jnp.int32, (nch, SSD_HEADS), 0) // HEAD_DIM
        col = lax.broadcasted_iota(jnp.int32, (nch, SSD_HEADS), 1)
        o_ref[nr:nr + 1, 128:160] = _hnn(jnp.broadcast_to(d_ref[...], (8, nch)), (head == col).astype(F32))[0:1]
        o_ref[nr:nr + 1, 256:384] = e_ref[...]

    return pl.pallas_call(body, name="pack_small", out_shape=jax.ShapeDtypeStruct((PACK_ROWS, PACK_COLS), F32))(
        *rows, b64, *a64s, d32, extra)


def _pack_ssd_small(cw, cb, nw):
    def body(cw_ref, cb_ref, nw_ref, o_ref):
        o_ref[...] = jnp.zeros_like(o_ref)
        o_ref[0:5, :] = cw_ref[...]
        o_ref[5:6, :] = cb_ref[...]
        o_ref[6:7, 0:256] = nw_ref[...]

    return pl.pallas_call(body, name="pack_ssd_small", out_shape=jax.ShapeDtypeStruct((8, 512), F32))(cw, cb, nw)


def _sum_parts(p_ref):
    g = p_ref[0].astype(F32)
    for s in range(1, p_ref.shape[0]):
        g = g + p_ref[s].astype(F32)
    return g


def _adam_update(w, g, m, v):
    m2 = ADAM_B1 * m + (1.0 - ADAM_B1) * g
    v2 = ADAM_B2 * v + (1.0 - ADAM_B2) * (g * g)
    m_hat = m2 / (1.0 - ADAM_B1 ** ADAM_STEP)
    v_hat = v2 / (1.0 - ADAM_B2 ** ADAM_STEP)
    return -ADAM_LR * (m_hat / (jnp.sqrt(v_hat) + ADAM_EPS) + ADAM_WD * w), m2, v2


def _adamw_windows(name, parts, params, windows, extra=None):
    n = len(params)

    def body(p_ref, *rest):
        ins, outs = rest[:3 * n], rest[3 * n:]
        g = _sum_parts(p_ref)
        for pi, rows, cols, idx in windows:
            w_ref, m_ref, v_ref = ins[3 * pi:3 * pi + 3]
            gw = g[rows, cols]
            dw, m2, v2 = _adam_update(w_ref[idx], gw, m_ref[idx], v_ref[idx])
            for o_ref, val in zip(outs[4 * pi:4 * pi + 4], (gw, dw, m2, v2), strict=True):
                o_ref[idx] = val
        if extra is not None:
            outs[4 * n][...] = g[extra[0], extra[1]]

    out_shape = [jax.ShapeDtypeStruct(w.shape, F32) for (w, _, _) in params for _ in range(4)]
    if extra is not None:
        out_shape.append(jax.ShapeDtypeStruct((extra[0].stop - extra[0].start, extra[1].stop - extra[1].start), F32))
    res = pl.pallas_call(body, name=name, out_shape=out_shape)(parts, *[a for p in params for a in p])
    return [res[4 * i:4 * i + 4] for i in range(n)] + ([res[4 * n]] if extra is not None else [])


def _adamw(name, w, parts, m, v, tr, tc=None):
    r_, c_ = w.shape
    p_ = parts.shape[0]
    tr = min(tr, r_)
    tc = c_ if tc is None else tc
    assert r_ % tr == 0 and c_ % tc == 0

    def body(w_ref, p_ref, m_ref, v_ref, g_ref, d_ref, m2_ref, v2_ref):
        g = _sum_parts(p_ref)
        g_ref[...] = g
        d_ref[...], m2_ref[...], v2_ref[...] = _adam_update(w_ref[...], g, m_ref[...], v_ref[...])

    blk = pl.BlockSpec((tr, tc), lambda i, j: (i, j))
    return pl.pallas_call(
        body, name=name, grid=(r_ // tr, c_ // tc), out_shape=[jax.ShapeDtypeStruct((r_, c_), F32)] * 4,
        in_specs=[blk, pl.BlockSpec((p_, tr, tc), lambda i, j: (0, i, j)), blk, blk], out_specs=[blk] * 4,
        compiler_params=_params(("parallel", "parallel"), VMEM_BIG),
    )(w, parts, m, v)


def _dev_index(p):
    return 4 * p[0] + 2 * p[1] + p[2]


def _all_gather(name, xs):
    n = len(xs)
    hbm = pl.BlockSpec(memory_space=pl.ANY)

    def body(*refs):
        x_refs, o_refs = refs[:n], refs[n:2 * n]
        send_sems, recv_sems, local_sems = refs[2 * n:]
        x, y, c = lax.axis_index("x"), lax.axis_index("y"), lax.axis_index("c")
        me, sibling = (x, y, c), (x, y, 1 - c)
        chips = [(1 - x, y), (x, 1 - y), (1 - x, 1 - y)]

        def place(a, block):
            return o_refs[a].at[_dev_index(block)]

        def copy(a, k, block, to, src=None):
            dst = place(a, block)
            return pltpu.make_async_remote_copy(
                src_ref=dst if src is None else src, dst_ref=dst, send_sem=send_sems.at[a, k],
                recv_sem=recv_sems.at[a, k], device_id=to, device_id_type=MESH)

        mine = [pltpu.make_async_copy(x_refs[a], place(a, me), local_sems.at[a]) for a in range(n)]
        for cp in mine:
            cp.start()
        first = []
        for a in range(n):
            first.append(copy(a, 0, me, sibling, src=x_refs[a]))
            first += [copy(a, 1 + j, me, (*chip, c), src=x_refs[a]) for j, chip in enumerate(chips)]
        for cp in first:
            cp.start()
        passed = []
        for j, chip in enumerate(chips):
            for a in range(n):
                copy(a, 1 + j, (*chip, c), me).wait_recv()
                cp = copy(a, 4 + j, (*chip, c), sibling)
                cp.start()
                passed.append(cp)
        for a in range(n):
            copy(a, 0, sibling, me).wait_recv()
            for j, chip in enumerate(chips):
                copy(a, 4 + j, (*chip, 1 - c), me).wait_recv()
        for cp in first + passed:
            cp.wait_send()
        for cp in mine:
            cp.wait()

    return pl.pallas_call(
        body, name=name, out_shape=[jax.ShapeDtypeStruct((NDEV, *x.shape), x.dtype) for x in xs],
        in_specs=[hbm] * n, out_specs=[hbm] * n,
        scratch_shapes=[pltpu.SemaphoreType.DMA((n, 7)), pltpu.SemaphoreType.DMA((n, 7)), pltpu.SemaphoreType.DMA((n,))],
    )(*xs)


_HBM = pl.BlockSpec(memory_space=pltpu.HBM)
_SEM = pl.BlockSpec(memory_space=pltpu.SEMAPHORE)
_EFFECT = pltpu.SideEffectType.DATAFLOW_SIDE_EFFECTING


def _mesh_position():
    return lax.axis_index("x"), lax.axis_index("y"), lax.axis_index("c")


def _peers(me):
    return [(k, tuple(1 - v if (k >> b) & 1 else v for v, b in zip(me, (2, 1, 0)))) for k in range(1, NDEV)]


def _column_window(ref, block, width):
    return ref.at[:, pl.ds(pl.multiple_of(_dev_index(block) * width, 128), width)]


RELAY_COPIES = 3


def _relay_copies(o_ref, send_sems, recv_sems, with_arrivals):
    width = o_ref.shape[1] // NDEV
    x, y, c = me = _mesh_position()
    sibling = (x, y, 1 - c)
    x_side, y_side, diagonal = (1 - x, y), (x, 1 - y), (1 - x, 1 - y)
    first = c == 0
    via = (jnp.where(first, 1 - x, x), jnp.where(first, y, 1 - y))
    to = (jnp.where(first, x, 1 - x), jnp.where(first, 1 - y, y))

    def copy(k, block, device):
        window = _column_window(o_ref, block, width)
        return pltpu.make_async_remote_copy(src_ref=window, dst_ref=window, send_sem=send_sems.at[k],
                                            recv_sem=recv_sems.at[k], device_id=device, device_id_type=MESH)

    sent = [copy(0, (*via, c), (*to, c)), copy(1, (*x_side, c), sibling), copy(2, (*y_side, c), sibling)]
    if not with_arrivals:
        return sent
    arrivals =[copy(0, (*diagonal, c), me), copy(1, (*x_side, 1 - c), me), copy(2, (*y_side, 1 - c), me)]
    return sent, arrivals


def _relay_start(name, gathered, dep):
    def body(g_ref, dep_ref, send_sems, recv_sems, o_ref, token):
        for cp in _relay_copies(g_ref, send_sems, recv_sems, with_arrivals=False):
            cp.start()
        token[...] = jnp.zeros_like(token)

    sems = pltpu.SemaphoreType.DMA((RELAY_COPIES,))
    res = pl.pallas_call(
        body, name=name,
        out_shape=(sems, sems, pltpu.HBM(gathered.shape, gathered.dtype), jax.ShapeDtypeStruct((8, 128), F32)),
        in_specs=[_HBM, pl.BlockSpec(memory_space=pl.ANY)],
        out_specs=(_SEM, _SEM, _HBM, pl.BlockSpec(memory_space=pltpu.VMEM)),
        input_output_aliases={0: 2},
        compiler_params=pltpu.CompilerParams(has_side_effects=_EFFECT),
    )(gathered, dep)
    return res[:-1], res[-1]


def _relay_wait(name, handles, after):
    send_sems, recv_sems, gathered = handles

    def body(g_ref, s_sems, r_sems, after_ref, o_ref):
        sent, arrivals = _relay_copies(g_ref, s_sems, r_sems, with_arrivals=True)
        for cp, arrival in zip(sent, arrivals):
            cp.wait_send()
            arrival.wait_recv()

    return pl.pallas_call(
        body, name=name, out_shape=pltpu.HBM(gathered.shape, gathered.dtype),
        in_specs=[_HBM, _SEM, _SEM, pl.BlockSpec(memory_space=pl.ANY)], out_specs=_HBM, input_output_aliases={0: 0},
        compiler_params=pltpu.CompilerParams(has_side_effects=_EFFECT),
    )(gathered, send_sems, recv_sems, after)


def _columns_last(name, gathered):
    width = gathered.shape[1] // NDEV
    hbm = pl.BlockSpec(memory_space=pl.ANY)

    def body(g_ref, o_ref, send_sem, recv_sem):
        x, y, c = _mesh_position()

        def copy(core, device):
            window = _column_window(o_ref, (1 - x, 1 - y, core), width)
            return pltpu.make_async_remote_copy(src_ref=window, dst_ref=window, send_sem=send_sem, recv_sem=recv_sem,
                                                device_id=device, device_id_type=MESH)

        onward = copy(c, (x, y, 1 - c))
        onward.start()
        copy(1 - c, (x, y, c)).wait_recv()
        onward.wait_send()

    return pl.pallas_call(
        body, name=name, out_shape=jax.ShapeDtypeStruct(gathered.shape, gathered.dtype), in_specs=[hbm], out_specs=hbm,
        input_output_aliases={0: 0}, scratch_shapes=[pltpu.SemaphoreType.DMA, pltpu.SemaphoreType.DMA],
    )(gathered)


NCHIP = NDEV // 2
EXCHANGE_COPIES = {"columns": NCHIP - 1, "gather": NDEV - 1, "scatter": NDEV - 1, "pair": NCHIP, "chips": NCHIP - 1}


def _landing_zones(name, xs, mode):
    x_, y_, c_ = _mesh_position()
    mine = (2 * x_ + y_ if mode == "chips" else _dev_index((x_, y_, c_))).astype(jnp.int32).reshape(1)
    lands = []
    for a, x in enumerate(xs):
        rows, cols = x.shape[-2:]
        if mode == "pair":
            lands.append(lax.empty((NCHIP, rows, cols), x.dtype))
            continue
        tr = 512 if rows % 512 == 0 else rows

        def body(me_ref, x_ref, o_ref):
            o_ref[...] = x_ref[...]

        if mode in ("gather", "columns"):
            in_spec = pl.BlockSpec((tr, cols), lambda i, me_ref: (i, 0))
        else:
            in_spec = pl.BlockSpec((None, tr, cols), lambda i, me_ref: (me_ref[0], i, 0))
        if mode == "columns":
            out_shape, out_spec = (rows, NDEV * cols), pl.BlockSpec((tr, cols), lambda i, me_ref: (i, me_ref[0]))
        else:
            out_shape = (NCHIP if mode == "chips" else NDEV, rows, cols)
            out_spec = pl.BlockSpec((None, tr, cols), lambda i, me_ref: (me_ref[0], i, 0))
        lands.append(pl.pallas_call(
            body, name=f"{name}_{a}", out_shape=jax.ShapeDtypeStruct(out_shape, x.dtype),
            grid_spec=pltpu.PrefetchScalarGridSpec(num_scalar_prefetch=1, grid=(rows // tr,), in_specs=[in_spec],
                                                   out_specs=out_spec),
            compiler_params=_params(("arbitrary",)),
        )(mine, x))
    return lands


def _exchange_copies(x_refs, land_refs, send_sems, recv_sems, mode):
    x_, y_, c_ = me = _mesh_position()
    per_array = EXCHANGE_COPIES[mode]
    out = []

    def add(a, k, src, dst, peer):
        sem = a * per_array + k
        out.append(pltpu.make_async_remote_copy(src_ref=src, dst_ref=dst, send_sem=send_sems.at[sem], recv_sem=recv_sems.at[sem],
                                                device_id=peer, device_id_type=MESH))

    for a, (x_ref, land_ref) in enumerate(zip(x_refs, land_refs)):
        if mode == "columns":
            for k, peer in enumerate([(x_, y_, 1 - c_), (1 - x_, y_, c_), (x_, 1 - y_, c_)]):
                add(a, k, x_ref, _column_window(land_ref, me, x_ref.shape[1]), peer)
        elif mode in ("gather", "scatter"):
            for k, peer in _peers(me):
                add(a, k - 1, x_ref.at[_dev_index(peer)] if mode == "scatter" else x_ref, land_ref.at[_dev_index(me)], peer)
        elif mode == "pair":
            for chip in range(NCHIP):
                add(a, chip, x_ref.at[2 * chip + 1 - c_], land_ref.at[chip], (x_, y_, 1 - c_))
        else:
            for k in range(1, NCHIP):
                px, py = (1 - x_ if k & 2 else x_), (1 - y_ if k & 1 else y_)
                add(a, k - 1, x_ref.at[2 * px + py], land_ref.at[2 * x_ + y_], (px, py, c_))
    return out


def _exchange_start(name, xs, lands, mode, dep, carry=False):
    n = len(xs)

    def body(*refs):
        x_refs, land_refs = refs[:n], refs[n:2 * n]
        send_sems, recv_sems = refs[2 * n + 1], refs[2 * n + 2]
        for cp in _exchange_copies(x_refs, land_refs, send_sems, recv_sems, mode):
            cp.start()
        if not carry:
            refs[-1][...] = jnp.zeros_like(refs[-1])

    sems = pltpu.SemaphoreType.DMA((n * EXCHANGE_COPIES[mode],))
    moved = [pltpu.with_memory_space_constraint(a, pltpu.HBM) for a in (*xs, *lands, *([dep] if carry else []))]
    res = pl.pallas_call(
        body, name=name,
        out_shape=(sems, sems, *[pltpu.HBM(a.shape, a.dtype) for a in moved],
                   *([] if carry else [jax.ShapeDtypeStruct((8, 128), F32)])),
        in_specs=[_HBM] * len(moved) + ([] if carry else [pl.BlockSpec(memory_space=pl.ANY)]),
        out_specs=(_SEM, _SEM, *[_HBM] * len(moved), *([] if carry else [pl.BlockSpec(memory_space=pltpu.VMEM)])),
        input_output_aliases={i: 2 + i for i in range(len(moved))},
        compiler_params=pltpu.CompilerParams(has_side_effects=_EFFECT),
    )(*moved, *([] if carry else [dep]))
    return res[:-1], res[-1]


def _exchange_wait(name, handles, mode, after, with_sources=False):
    send_sems, recv_sems = handles[0], handles[1]
    bufs = handles[2:]
    n = len(bufs) // 2
    afters = list(after) if isinstance(after, (list, tuple)) else [after]

    def body(*refs):
        x_refs, land_refs = refs[:n], refs[n:2 * n]
        s_sems, r_sems = refs[2 * n], refs[2 * n + 1]
        for cp in _exchange_copies(x_refs, land_refs, s_sems, r_sems, mode):
            cp.wait_send()
            cp.wait_recv()

    res = pl.pallas_call(
        body, name=name, out_shape=tuple(pltpu.HBM(a.shape, a.dtype) for a in bufs),
        in_specs=[_HBM] * (2 * n) + [_SEM, _SEM] + [pl.BlockSpec(memory_space=pl.ANY)] * len(afters),
        out_specs=tuple([_HBM] * (2 * n)), input_output_aliases={i: i for i in range(2 * n)},
        compiler_params=pltpu.CompilerParams(has_side_effects=_EFFECT),
    )(*bufs, send_sems, recv_sems, *afters)
    return (res[n:], res[:n]) if with_sources else res[n:]


def _pair_sum(name, x, from_sibling):
    _, rows, cols = x.shape
    tr = rows
    core = lax.axis_index("c").astype(jnp.int32).reshape(1)

    def body(c_ref, x_ref, s_ref, o_ref):
        o_ref[...] = (x_ref[...].astype(F32) + s_ref[...].astype(F32)).astype(o_ref.dtype)

    return pl.pallas_call(
        body, name=name, out_shape=jax.ShapeDtypeStruct((NCHIP, rows, cols), x.dtype),
        grid_spec=pltpu.PrefetchScalarGridSpec(
            num_scalar_prefetch=1, grid=(NCHIP, rows // tr),
            in_specs=[pl.BlockSpec((None, tr, cols), lambda j, i, c_ref: (2 * j + c_ref[0], i, 0)),
                      pl.BlockSpec((None, tr, cols), lambda j, i, c_ref: (j, i, 0))],
            out_specs=pl.BlockSpec((None, tr, cols), lambda j, i, c_ref: (j, i, 0))),
        compiler_params=_params(("parallel", "parallel")),
    )(core, x, from_sibling)


def kernel(x, c, positions, norm_w, mod_w, mod_b, attn_w_in, attn_w_out, ssd_w_in, ssd_conv_w, ssd_conv_b, ssd_dt_bias, ssd_a_log, ssd_d, ssd_norm_w, ssd_w_out, final_norm_w, loss_target, m_norm_w, m_mod_w, m_mod_b, m_attn_w_in, m_attn_w_out, m_ssd_w_in, m_ssd_conv_w, m_ssd_conv_b, m_ssd_dt_bias, m_ssd_a_log, m_ssd_d, m_ssd_norm_w, m_ssd_w_out, m_final_norm_w, v_norm_w, v_mod_w, v_mod_b, v_attn_w_in, v_attn_w_out, v_ssd_w_in, v_ssd_conv_w, v_ssd_conv_b, v_ssd_dt_bias, v_ssd_a_log, v_ssd_d, v_ssd_norm_w, v_ssd_w_out, v_final_norm_w):
    s_len, dm = x.shape[1], x.shape[2]
    me = 4 * lax.axis_index("x") + 2 * lax.axis_index("y") + lax.axis_index("c")
    x0 = x.reshape(s_len, dm)
    tgt = loss_target.reshape(s_len, dm)
    aw = 3 * 512
    si = 2 * dm
    sxbc = 2 * si
    n_ssd_in = ssd_w_in.shape[2] * NDEV

    (c_all,) = _all_gather("gather_c", [c])
    c_all = c_all.reshape(NDEV, dm)
    part = _mod_part(c_all, mod_w)
    (part_all,) = _all_gather("gather_mod", [part])
    mod_nb = jnp.stack([lax.dynamic_index_in_dim(part_all, i * NDEV + me, axis=1, keepdims=False).reshape(3 * dm)
                        for i in range(2)])

    wcol = attn_w_in.shape[2]
    ai_shard = [attn_w_in[0].astype(BF16)]
    ai_handles, ai_token = _exchange_start("attn_w_in_start", ai_shard, _landing_zones("attn_w_in_place", ai_shard, "columns"),
                                           "columns", part_all)
    inv_freq = ROPE_THETA ** (-jnp.arange(0, ROT_DIM, 2, dtype=F32) / ROT_DIM)
    per_head = jnp.concatenate([inv_freq, inv_freq, jnp.zeros(HEAD_DIM - ROT_DIM, F32)])
    inv_row = jnp.tile(per_head, 128 // HEAD_DIM).reshape(1, 128) + ai_token[0:1]
    tabs = _rope_tables(positions.reshape(s_len, 1), inv_row)
    ssd_small = _pack_ssd_small(ssd_conv_w[0], ssd_conv_b, ssd_norm_w)
    ao_shard = [attn_w_out[0].astype(BF16)]
    late_shards = [ssd_w_in[0].T.astype(BF16), ssd_w_out[0].astype(BF16), ssd_small]
    ao_lands = _landing_zones("w_out_place", ao_shard, "gather")
    late_lands = _landing_zones("weights_place", late_shards, "gather")
    (w_ai,) = _exchange_wait("attn_w_in_wait", ai_handles, "columns", [*tabs, *ao_lands, *late_lands])
    relay_handles, relay_token = _relay_start("attn_w_in_relay_start", w_ai, tabs[0])
    (shift0, scale0, gate0, nw0), (shift1, scale1, gate1, nw1) = _mod_finish(mod_nb, mod_b, norm_w, [relay_token])
    shift, scale, gate, nw = [shift0, shift1], [scale0, scale1], [gate0, gate1], [nw0, nw1]
    hn0 = _norm_mod_fwd("norm0", x0, nw[0], scale[0], shift[0])
    w_ai = _columns_last("gather_attn_w_in_last", _relay_wait("attn_w_in_relay_wait", relay_handles, hn0))

    ao_handles, w_ai = _exchange_start("w_out_start", ao_shard, ao_lands, "gather", w_ai, carry=True)
    w_handles, w_ai = _exchange_start("weights_start", late_shards, late_lands, "gather", w_ai, carry=True)

    qk = _matmul("proj_qk", hn0, w_ai, "nn", F32, MM_T, MM_T, dm, epilogue=_rot_fwd, mrows=tabs, n_out=2 * aw)
    v = _matmul("proj_vz", hn0, w_ai, "nn", F32, MM_T, MM_T, dm, b_noff=2 * aw, n_out=2 * aw)
    z0 = (v, 1)
    att = [_attn_fwd(g, qk, v) for g in range(3)]
    os_, lses = [a[0] for a in att], [a[1] for a in att]
    (g_ao,) = _exchange_wait("w_out_wait", ao_handles, "gather", lses[2])
    a0, y0, x1 = _attn_out(os_, lses, z0, x0, gate[0], g_ao.reshape(aw, dm))

    hn1 = _norm_mod_fwd("norm1", x1, nw[1], scale[1], shift[1])
    g_si, g_so, g_small = _exchange_wait("weights_wait", w_handles, "gather", hn1)
    w_ao = g_ao.reshape(aw, dm)
    w_si_t = g_si.reshape(n_ssd_in, dm)
    w_so = g_so.reshape(si, dm)
    conv_w = g_small[:, 0:CONV_WIDTH, :].transpose(1, 0, 2).reshape(CONV_WIDTH, sxbc)
    conv_b = g_small[:, 5, :].reshape(1, sxbc)
    snw = g_small[:, 6, 0:si // NDEV].reshape(1, si)
    ndt = 2 * SSD_HEADS
    z1 = _matmul("ssd_proj_z", hn1, w_si_t, "nt", F32, MM_T, MM_T, dm, n_out=si)
    xpre = _matmul("ssd_proj_xbc", hn1, w_si_t, "nt", F32, MM_T, MM_T, dm, b_noff=si, n_out=sxbc)
    dt_raw = _matmul("ssd_proj_dt", hn1, w_si_t, "nt", F32, MM_T, ndt, dm, b_noff=si + sxbc, n_out=ndt)
    xbc = _conv_fwd(xpre, conv_w, conv_b)
    widen = lambda a: jnp.pad(a, ((0, 0), (0, SSD_DTW - ndt)))
    dt_raw = widen(dt_raw)
    dt_bias = widen(ssd_dt_bias.reshape(1, ndt))
    alog = widen(ssd_a_log.reshape(1, ndt))
    dt = _softplus_fwd(dt_raw, dt_bias)
    y_f, st_f = _ssd_fwd(xbc, dt, alog, 0)
    y_fb, st_b = _ssd_fwd(xbc, dt, alog, 1, prior=y_f)
    d_e = jnp.repeat(ssd_d.reshape(SSD_HEADS), HEAD_DIM).reshape(1, si)

    fnw = final_norm_w.reshape(1, dm)
    u, dx2, dy1, g_fnw, dgate1, loss_part = _ssd_tail_loss(y_fb, xbc, z1, d_e, snw, w_so, x1, tgt, gate[1], fnw)
    gw_so = _matmul("ssd_out_dw", u, dy1, "tn", BF16, MM_T, MM_T, MM_T)
    dys, dz1, g_snw, g_d = _gate_norm_bwd(dy1, w_so, y_fb, xbc, z1, d_e, snw)
    dxbc_f, ddt_f, dalog_f = _ssd_bwd(xbc, dt, alog, st_f, dys, d_e, 0)
    dxbc, ddt_b, dalog_b = _ssd_bwd(xbc, dt, alog, st_b, dys, d_e, 1, prior=dxbc_f)
    dpre, g_cw, g_cb = _conv_bwd(xpre, dxbc, conv_w, conv_b)
    ddt_raw, g_dtb = _softplus_bwd(ddt_f, ddt_b, dt_raw, dt_bias)
    ddt_raw = ddt_raw[:, :ndt]
    dhn1 = [_matmul("ssd_proj_z_dx", dz1, w_si_t, "nn", F32, MM_T, MM_T, MM_T),
            _matmul("ssd_proj_xbc_dx", dpre, w_si_t, "nn", F32, MM_T, MM_T, MM_T, b_koff=si)]
    gw_si_t = _matmul("ssd_proj_z_dw", dz1, hn1, "tn", BF16, MM_T, MM_T, MM_T, dest=(n_ssd_in, 0, None))
    gw_si_t = _matmul("ssd_proj_xbc_dw", dpre, hn1, "tn", BF16, MM_T, MM_T, MM_T, dest=(n_ssd_in, si, gw_si_t))
    gw_si_t = _matmul("ssd_proj_dt_dw", ddt_raw, hn1, "tn", BF16, ndt, MM_T, MM_T, dest=(n_ssd_in, si + sxbc, gw_si_t))

    l1_grads = [gw_so.reshape(NDEV, si // NDEV, dm), gw_si_t.reshape(NDEV, n_ssd_in // NDEV, dm),
                _pack_ssd_small_blocks(g_cw, g_cb, g_snw)]
    l1_handles, l1_token = _exchange_start("l1_grads_start", l1_grads, _landing_zones("l1_grads_place", l1_grads, "scatter"),
                                           "scatter", dhn1[1])
    dx1, dy0, g_nw1, dsc1, dsh1, dgate0 = _norm_mod_bwd(
        "ssd_proj_dt_dx_norm1_bwd", (ddt_raw, w_si_t, "nn", ndt, dict(b_koff=si + sxbc)), x1, dhn1, dx2,
        nw[1], scale[1], shift[1], prev=(y0, gate[0] + l1_token[0:1, 0:1]))

    gw_ao = _matmul("attn_out_dw", a0, dy0, "tn", BF16, aw // 2, MM_T, MM_T)
    dos, dls, dz0 = _mix_bwd(dy0, w_ao, os_, lses, z0)
    datt = [_attn_bwd(g, qk, v, os_[g], lses[g], dos[g], dls[g]) for g in range(3)]
    dqkv = _rot_pack_bwd([t[0] for t in datt], [t[1] for t in datt], [t[2] for t in datt], tabs)
    gw_ai = _matmul("proj_qkv_dw", hn0, dqkv, "tn", BF16, MM_T, wcol, MM_T, out_blocks=3 * aw // wcol, dest=(NDEV, 0, None))
    gw_ai = _matmul("proj_z_dw", hn0, dz0, "tn", BF16, MM_T, wcol, MM_T, out_blocks=aw // wcol,
                    dest=(NDEV, 3 * aw // wcol, gw_ai))
    after_start = lambda acc, t: acc + t
    zero_row = lambda token: jnp.tile(token[0:1], (1, dm // 128))
    l0_grads = [gw_ai, gw_ao.reshape(NDEV, aw // NDEV, dm)]
    pair_handles, pair_token = _exchange_start("l0_pair_start", l0_grads, _landing_zones("l0_pair_place", l0_grads, "pair"),
                                               "pair", dqkv)
    dhn0_z = _matmul("proj_z_dx", dz0, w_ai, "nt", F32, MM_T, MM_T, aw, b_koff=3 * aw, n_out=dm, epilogue=after_start,
                     ncols=(zero_row(pair_token),))
    from_sibling, l0_grads = _exchange_wait("l0_pair_wait", pair_handles, "pair", dhn0_z, with_sources=True)
    chip_sums = [_pair_sum(f"l0_pair_sum_{a}", g, s) for a, (g, s) in enumerate(zip(l0_grads, from_sibling))]
    l0_handles, l0_token = _exchange_start("l0_grads_start", chip_sums, _landing_zones("l0_grads_place", chip_sums, "chips"),
                                           "chips", dhn0_z)
    dx0, g_nw0, dsc0, dsh0 = _norm_mod_bwd(
        "proj_qkv_dx_norm0_bwd", (dqkv, w_ai, "nt", aw, dict(n_out=dm)), x0, [dhn0_z], dx1,
        nw[0], scale[0], shift[0] + zero_row(l0_token))

    small_g = [_pack_small([dsh0, dsc0, dgate0, dsh1, dsc1, dgate1, g_nw0, g_nw1, g_fnw], g_dtb, [dalog_f, dalog_b], g_d, loss_part)]
    sm_handles, sm_token = _exchange_start("small_grads_start", small_g, _landing_zones("small_grads_place", small_g, "gather"),
                                           "gather", dx0)

    whole = (slice(None), slice(None))
    r_so, r_si, r_small = _exchange_wait("l1_grads_wait", l1_handles, "scatter", sm_token)
    si_out = [o.T for o in _adamw("adamw_ssd_w_in", ssd_w_in[0].T, r_si, m_ssd_w_in[0].T, v_ssd_w_in[0].T, n_ssd_in // NDEV, 512)]
    so_out = _adamw("adamw_ssd_w_out", ssd_w_out[0], r_so, m_ssd_w_out[0], v_ssd_w_out[0], 256)
    cw_cols = ssd_conv_w.shape[2]
    cw_out, cb_out, snw_out = _adamw_windows(
        "adamw_ssd_small", r_small,
        [(ssd_conv_w, m_ssd_conv_w, v_ssd_conv_w), (ssd_conv_b, m_ssd_conv_b, v_ssd_conv_b),
         (ssd_norm_w, m_ssd_norm_w, v_ssd_norm_w)],
        [(0, slice(0, CONV_WIDTH), slice(0, cw_cols), (0, slice(None), slice(None))),
         (1, slice(5, 6), slice(0, cw_cols), whole), (2, slice(6, 7), slice(0, si // NDEV), whole)])
    r_ai, r_ao = _exchange_wait("l0_grads_wait", l0_handles, "chips", so_out[0])
    ai_out = _adamw("adamw_attn_w_in", attn_w_in[0], r_ai, m_attn_w_in[0], v_attn_w_in[0], 512)
    ao_out = _adamw("adamw_attn_w_out", attn_w_out[0], r_ao, m_attn_w_out[0], v_attn_w_out[0], 192)

    (small_all,) = _exchange_wait("small_grads_wait", sm_handles, "gather", ai_out[0])
    full = slice(0, PACK_COLS)
    nhd = SSD_HEADS
    windows = [(0, slice(3 * i + k, 3 * i + k + 1), full, (slice(i, i + 1), slice(k * dm, (k + 1) * dm)))
               for i in range(2) for k in range(3)]
    windows += [(1, slice(6 + i, 7 + i), full, (slice(i, i + 1), slice(None))) for i in range(2)]
    windows += [(2, slice(8, 9), full, whole)]
    windows += [(3 + q, slice(9, 10), slice(2 * nhd * q + nhd * j, 2 * nhd * q + nhd * (j + 1)), (0, slice(j, j + 1), slice(None)))
                for q in range(2) for j in range(2)]
    windows += [(5, slice(9, 10), slice(4 * nhd, 5 * nhd), whole)]
    as_row = lambda a: a.reshape(1, dm)
    mb_out, nw_out, fnw_out, dtb_out, alog_out, d_out, loss = _adamw_windows(
        "adamw_small", small_all,
        [(mod_b, m_mod_b, v_mod_b), (norm_w, m_norm_w, v_norm_w), (fnw, as_row(m_final_norm_w), as_row(v_final_norm_w)),
         (ssd_dt_bias, m_ssd_dt_bias, v_ssd_dt_bias), (ssd_a_log, m_ssd_a_log, v_ssd_a_log), (ssd_d, m_ssd_d, v_ssd_d)],
        windows, extra=(slice(9, 10), slice(256, 257)))
    loss = loss.reshape(())

    ncol = mod_w.shape[2]
    dmod_all = small_all[:, 0:6, :].reshape(NDEV, 2, 3 * dm)
    dmod_sh = lax.dynamic_slice_in_dim(dmod_all, me * ncol, ncol, axis=2).transpose(1, 0, 2)
    g_modw = _mod_grad(c_all, dmod_sh).reshape(1, 2 * dm, ncol)
    modw_out = _adamw("adamw_mod_w", mod_w.reshape(2 * dm, ncol), g_modw, m_mod_w.reshape(2 * dm, ncol),
                      v_mod_w.reshape(2 * dm, ncol), 512)

    per_kind = []
    for k in range(4):
        per_kind.append([
            nw_out[k], modw_out[k].reshape(mod_w.shape), mb_out[k], ai_out[k][None], ao_out[k][None], si_out[k][None],
            cw_out[k], cb_out[k], dtb_out[k], alog_out[k], d_out[k], snw_out[k], so_out[k][None], fnw_out[k].reshape(dm)])
    return (loss, dx0.reshape(x.shape), *per_kind[0], *per_kind[1], *per_kind[2], *per_kind[3])


def _pack_ssd_small_blocks(g_cw, g_cb, g_nw):
    nper = g_cw.shape[1] // NDEV
    nwper = g_nw.shape[1] // NDEV

    def body(cw_ref, cb_ref, nw_ref, o_ref):
        o_ref[...] = jnp.zeros_like(o_ref)
        for d in range(NDEV):
            o_ref[d, 0:5, :] = cw_ref[:, d * nper:(d + 1) * nper]
            o_ref[d, 5:6, :] = cb_ref[:, d * nper:(d + 1) * nper]
            o_ref[d, 6:7, 0:nwper] = nw_ref[:, d * nwper:(d + 1) * nwper]

    return pl.pallas_call(body, name="pack_ssd_small_grads", out_shape=jax.ShapeDtypeStruct((NDEV, 8, nper), F32))(g_cw, g_cb, g_nw)
```
